```python
import jax, jax.numpy as jnp
from jax import lax
import numpy as np

D_MODEL = 1024
BATCH = 16
SEQ = 2048
DEPTH = 1

POOL_WIDTH = D_MODEL // 2
POOL_WINDOWS = (2, 4, 8, 16)
N_POOL_GROUPS = len(POOL_WINDOWS)
POOL_GROUP = POOL_WIDTH // N_POOL_GROUPS
SSD_HEAD_DIM = 64
SSD_INNER = D_MODEL
SSD_HEADS = SSD_INNER // SSD_HEAD_DIM
SSD_GROUPS = 2
SSD_HPG = SSD_HEADS // SSD_GROUPS
SSD_STATE = 128
CONV_WIDTH = 4
CHUNK = 128
CONV_CH = SSD_INNER + 2 * SSD_GROUPS * SSD_STATE
MIX_WIDTH = POOL_WIDTH + SSD_INNER
OFF_POOL = 0
OFF_Z = OFF_POOL + POOL_WIDTH
OFF_XBC = OFF_Z + SSD_INNER
OFF_DT = OFF_XBC + CONV_CH
IN_WIDTH = OFF_DT + SSD_HEADS
D_FF = 4 * D_MODEL
N_MOD = 6
EPS = 1e-5

kernel_name = "hybrid_pool_ssd_adaln_block"


def rms_norm(x, g):
    x32 = x.astype(jnp.float32)
    y = x32 * lax.rsqrt(jnp.mean(x32 * x32, axis=-1, keepdims=True) + EPS)
    return (y * g.astype(jnp.float32)).astype(x.dtype)


def pool_mixer(u, w_pool, pool_scale):
    Bsz, S, _ = u.shape
    u32 = u.astype(jnp.float32)
    cs = jnp.cumsum(u32, axis=1)
    t = jnp.arange(1, S + 1, dtype=jnp.float32)[None, :, None]
    outs = []
    for gi, w in enumerate(POOL_WINDOWS):
        sl = slice(gi * POOL_GROUP, (gi + 1) * POOL_GROUP)
        cs_g = cs[..., sl]
        prev = jnp.pad(cs_g, ((0, 0), (w, 0), (0, 0)))[:, :S]
        mean = (cs_g - prev) / jnp.minimum(t, float(w))
        outs.append(mean - u32[..., sl])
    p = jnp.stack(outs, axis=2).astype(u.dtype)
    y = jnp.einsum('bsgc,gcd->bsgd', p, w_pool).reshape(Bsz, S, POOL_WIDTH)
    return y * pool_scale


def causal_depthwise_conv(u, w, b):
    K = w.shape[0]
    S = u.shape[1]
    up = jnp.pad(u, ((0, 0), (K - 1, 0), (0, 0)))
    out = b + up[:, 0:S] * w[0]
    for k in range(1, K):
        out = out + up[:, k:k + S] * w[k]
    return out


def ssd_chunked(xs, dt, a, bm, cm):
    Bsz, S, G, R, P = xs.shape
    N = bm.shape[-1]
    nc = S // CHUNK
    xdt = (xs.astype(jnp.float32) * dt[..., None]).reshape(Bsz, nc, CHUNK, G, R, P)
    da = (dt * a).reshape(Bsz, nc, CHUNK, G, R)
    bc = bm.astype(jnp.float32).reshape(Bsz, nc, CHUNK, G, N)
    cc = cm.astype(jnp.float32).reshape(Bsz, nc, CHUNK, G, N)
    a_cum = jnp.cumsum(da, axis=2)
    causal = jnp.tril(jnp.ones((CHUNK, CHUNK), dtype=bool))[None, None, :, :, None, None]
    seg = a_cum[:, :, :, None] - a_cum[:, :, None, :]
    decay_in = jnp.exp(jnp.where(causal, seg, -jnp.inf))
    scores = jnp.einsum('bclgn,bcsgn->bclsg', cc, bc)
    y_diag = jnp.einsum('bclsgr,bcsgrp->bclgrp', scores[..., None] * decay_in, xdt)
    decay_out = jnp.exp(a_cum[:, :, -1:] - a_cum)
    states = jnp.einsum('bclgn,bclgr,bclgrp->bcgrpn', bc, decay_out, xdt)
    chunk_decay = jnp.exp(a_cum[:, :, -1])

    def step(h, inp):
        st, dec = inp
        return h * dec[..., None, None] + st, h

    h0 = jnp.zeros((Bsz, G, R, P, N), jnp.float32)
    _, prev = lax.scan(step, h0, (jnp.moveaxis(states, 1, 0), jnp.moveaxis(chunk_decay, 1, 0)))
    prev = jnp.moveaxis(prev, 0, 1)
    y_off = jnp.einsum('bclgn,bcgrpn,bclgr->bclgrp', cc, prev, jnp.exp(a_cum))
    return (y_diag + y_off).reshape(Bsz, S, G, R, P)


def ssd_mixer(z, u_xbc, u_dt, conv_w, conv_b, dt_bias, a_log, d_skip, g_ssd):
    Bsz, S, _ = z.shape
    xbc = jax.nn.silu(causal_depthwise_conv(u_xbc, conv_w, conv_b))
    GN = SSD_GROUPS * SSD_STATE
    xs = xbc[..., :SSD_INNER].reshape(Bsz, S, SSD_GROUPS, SSD_HPG, SSD_HEAD_DIM)
    bm = xbc[..., SSD_INNER:SSD_INNER + GN].reshape(Bsz, S, SSD_GROUPS, SSD_STATE)
    cm = xbc[..., SSD_INNER + GN:].reshape(Bsz, S, SSD_GROUPS, SSD_STATE)
    dt = jax.nn.softplus(u_dt.astype(jnp.float32) + dt_bias.astype(jnp.float32))
    dt = dt.reshape(Bsz, S, SSD_GROUPS, SSD_HPG)
    a = -jnp.exp(a_log.astype(jnp.float32)).reshape(SSD_GROUPS, SSD_HPG)
    y = ssd_chunked(xs, dt, a, bm, cm)
    y = y + d_skip.astype(jnp.float32).reshape(SSD_GROUPS, SSD_HPG)[:, :, None] * xs.astype(jnp.float32)
    y = y.reshape(Bsz, S, SSD_INNER) * jax.nn.silu(z.astype(jnp.float32))
    yg = y.reshape(Bsz, S, SSD_GROUPS, SSD_INNER // SSD_GROUPS)
    yg = yg * lax.rsqrt(jnp.mean(yg * yg, axis=-1, keepdims=True) + EPS)
    y = yg.reshape(Bsz, S, SSD_INNER) * g_ssd.astype(jnp.float32)
    return y.astype(z.dtype)


def _fwd_setup_inputs(seed: int = 0) -> dict:
    key = jax.random.key(seed)
    ks = jax.random.split(key, 20)
    f32 = jnp.float32
    L = DEPTH
    x = jax.random.normal(ks[0], (BATCH, SEQ, D_MODEL), f32)
    c = jax.random.normal(ks[1], (BATCH, D_MODEL), f32)
    w_ada = jax.random.normal(ks[2], (L, D_MODEL, N_MOD * D_MODEL), f32) * D_MODEL ** -0.5
    b_ada = jax.random.normal(ks[3], (L, N_MOD * D_MODEL), f32) * 0.02
    g_mix = 1.0 + 0.02 * jax.random.normal(ks[4], (L, D_MODEL), f32)
    w_in = jax.random.normal(ks[5], (L, D_MODEL, IN_WIDTH), f32) * D_MODEL ** -0.5
    conv_w = jax.random.normal(ks[6], (L, CONV_WIDTH, CONV_CH), f32) * CONV_WIDTH ** -0.5
    conv_b = jax.random.normal(ks[7], (L, CONV_CH), f32) * 0.02
    dt0 = jnp.exp(jax.random.uniform(ks[8], (L, SSD_HEADS), f32, np.log(1e-3), np.log(1e-1)))
    dt_bias = dt0 + jnp.log(-jnp.expm1(-dt0))
    a_log = jnp.log(jax.random.uniform(ks[9], (L, SSD_HEADS), f32, 1.0, 16.0))
    d_skip = 1.0 + 0.1 * jax.random.normal(ks[10], (L, SSD_HEADS), f32)
    g_ssd = 1.0 + 0.02 * jax.random.normal(ks[11], (L, SSD_INNER), f32)
    w_pool = jax.random.normal(ks[12], (L, N_POOL_GROUPS, POOL_GROUP, POOL_GROUP), f32) * POOL_GROUP ** -0.5
    pool_scale = 1.0 + 0.1 * jax.random.normal(ks[13], (L, POOL_WIDTH), f32)
    w_out = jax.random.normal(ks[14], (L, MIX_WIDTH, D_MODEL), f32) * MIX_WIDTH ** -0.5
    g_mlp = 1.0 + 0.02 * jax.random.normal(ks[15], (L, D_MODEL), f32)
    w_up = jax.random.normal(ks[16], (L, D_MODEL, D_FF), f32) * D_MODEL ** -0.5
    w_down = jax.random.normal(ks[17], (L, D_FF, D_MODEL), f32) * D_FF ** -0.5
    g_final = 1.0 + 0.02 * jax.random.normal(ks[18], (D_MODEL,), f32)
    return {"x": x, "c": c, "w_ada": w_ada, "b_ada": b_ada, "g_mix": g_mix, "w_in": w_in,
            "conv_w": conv_w, "conv_b": conv_b, "dt_bias": dt_bias, "a_log": a_log,
            "d_skip": d_skip, "g_ssd": g_ssd, "w_pool": w_pool, "pool_scale": pool_scale,
            "w_out": w_out, "g_mlp": g_mlp, "w_up": w_up, "w_down": w_down, "g_final": g_final}


def _fwd_reference(x, c, w_ada, b_ada, g_mix, w_in, conv_w, conv_b, dt_bias, a_log, d_skip, g_ssd,
              w_pool, pool_scale, w_out, g_mlp, w_up, w_down, g_final):
    h = x
    c_act = jax.nn.silu(c)
    for layer in range(DEPTH):
        mod = jnp.einsum('bd,de->be', c_act, w_ada[layer]) + b_ada[layer]
        shift_m, scale_m, gate_m, shift_f, scale_f, gate_f = jnp.split(mod[:, None, :], N_MOD, axis=-1)

        u = rms_norm(h, g_mix[layer]) * (1.0 + scale_m) + shift_m
        proj = jnp.einsum('bsd,de->bse', u, w_in[layer])
        u_pool = proj[..., OFF_POOL:OFF_Z]
        z = proj[..., OFF_Z:OFF_XBC]
        u_xbc = proj[..., OFF_XBC:OFF_DT]
        u_dt = proj[..., OFF_DT:]
        y_pool = pool_mixer(u_pool, w_pool[layer], pool_scale[layer])
        y_ssd = ssd_mixer(z, u_xbc, u_dt, conv_w[layer], conv_b[layer], dt_bias[layer],
                          a_log[layer], d_skip[layer], g_ssd[layer])
        y_mix = jnp.concatenate([y_pool.astype(h.dtype), y_ssd.astype(h.dtype)], axis=-1)
        h = h + gate_m * jnp.einsum('bse,ed->bsd', y_mix, w_out[layer])

        u = rms_norm(h, g_mlp[layer]) * (1.0 + scale_f) + shift_f
        f = jnp.square(jax.nn.relu(jnp.einsum('bsd,df->bsf', u, w_up[layer])))
        h = h + gate_f * jnp.einsum('bsf,fd->bsd', f, w_down[layer])
    return rms_norm(h, g_final)


import jax as _jax
import jax.numpy as _jnp

TWIN_FORMAT = 'train_step'
FWD_PARAMS = ['x', 'c', 'w_ada', 'b_ada', 'g_mix', 'w_in', 'conv_w', 'conv_b', 'dt_bias', 'a_log', 'd_skip', 'g_ssd', 'w_pool', 'pool_scale', 'w_out', 'g_mlp', 'w_up', 'w_down', 'g_final']
TWIN_WEIGHTS = ['w_ada', 'b_ada', 'g_mix', 'w_in', 'conv_w', 'conv_b', 'dt_bias', 'a_log', 'd_skip', 'g_ssd', 'w_pool', 'pool_scale', 'w_out', 'g_mlp', 'w_up', 'w_down', 'g_final']
TWIN_DIFF_INPUT = 'x'
TWIN_INPUTS = ['x', 'c', 'w_ada', 'b_ada', 'g_mix', 'w_in', 'conv_w', 'conv_b', 'dt_bias', 'a_log', 'd_skip', 'g_ssd', 'w_pool', 'pool_scale', 'w_out', 'g_mlp', 'w_up', 'w_down', 'g_final', 'loss_target', 'm_w_ada', 'm_b_ada', 'm_g_mix', 'm_w_in', 'm_conv_w', 'm_conv_b', 'm_dt_bias', 'm_a_log', 'm_d_skip', 'm_g_ssd', 'm_w_pool', 'm_pool_scale', 'm_w_out', 'm_g_mlp', 'm_w_up', 'm_w_down', 'm_g_final', 'v_w_ada', 'v_b_ada', 'v_g_mix', 'v_w_in', 'v_conv_w', 'v_conv_b', 'v_dt_bias', 'v_a_log', 'v_d_skip', 'v_g_ssd', 'v_w_pool', 'v_pool_scale', 'v_w_out', 'v_g_mlp', 'v_w_up', 'v_w_down', 'v_g_final']
TWIN_OUTPUTS = ['loss', 'grad_x', 'grad_w_ada', 'grad_b_ada', 'grad_g_mix', 'grad_w_in', 'grad_conv_w', 'grad_conv_b', 'grad_dt_bias', 'grad_a_log', 'grad_d_skip', 'grad_g_ssd', 'grad_w_pool', 'grad_pool_scale', 'grad_w_out', 'grad_g_mlp', 'grad_w_up', 'grad_w_down', 'grad_g_final', 'delta_w_ada', 'delta_b_ada', 'delta_g_mix', 'delta_w_in', 'delta_conv_w', 'delta_conv_b', 'delta_dt_bias', 'delta_a_log', 'delta_d_skip', 'delta_g_ssd', 'delta_w_pool', 'delta_pool_scale', 'delta_w_out', 'delta_g_mlp', 'delta_w_up', 'delta_w_down', 'delta_g_final', 'new_m_w_ada', 'new_m_b_ada', 'new_m_g_mix', 'new_m_w_in', 'new_m_conv_w', 'new_m_conv_b', 'new_m_dt_bias', 'new_m_a_log', 'new_m_d_skip', 'new_m_g_ssd', 'new_m_w_pool', 'new_m_pool_scale', 'new_m_w_out', 'new_m_g_mlp', 'new_m_w_up', 'new_m_w_down', 'new_m_g_final', 'new_v_w_ada', 'new_v_b_ada', 'new_v_g_mix', 'new_v_w_in', 'new_v_conv_w', 'new_v_conv_b', 'new_v_dt_bias', 'new_v_a_log', 'new_v_d_skip', 'new_v_g_ssd', 'new_v_w_pool', 'new_v_pool_scale', 'new_v_w_out', 'new_v_g_mlp', 'new_v_w_up', 'new_v_w_down', 'new_v_g_final']
TWIN_LEAF_KINDS = {'loss': 'loss', 'grad_x': 'grad_x', 'grad_w_ada': 'grad_w', 'grad_b_ada': 'grad_w', 'grad_g_mix': 'grad_w', 'grad_w_in': 'grad_w', 'grad_conv_w': 'grad_w', 'grad_conv_b': 'grad_w', 'grad_dt_bias': 'grad_w', 'grad_a_log': 'grad_w', 'grad_d_skip': 'grad_w', 'grad_g_ssd': 'grad_w', 'grad_w_pool': 'grad_w', 'grad_pool_scale': 'grad_w', 'grad_w_out': 'grad_w', 'grad_g_mlp': 'grad_w', 'grad_w_up': 'grad_w', 'grad_w_down': 'grad_w', 'grad_g_final': 'grad_w', 'delta_w_ada': 'delta_w', 'delta_b_ada': 'delta_w', 'delta_g_mix': 'delta_w', 'delta_w_in': 'delta_w', 'delta_conv_w': 'delta_w', 'delta_conv_b': 'delta_w', 'delta_dt_bias': 'delta_w', 'delta_a_log': 'delta_w', 'delta_d_skip': 'delta_w', 'delta_g_ssd': 'delta_w', 'delta_w_pool': 'delta_w', 'delta_pool_scale': 'delta_w', 'delta_w_out': 'delta_w', 'delta_g_mlp': 'delta_w', 'delta_w_up': 'delta_w', 'delta_w_down': 'delta_w', 'delta_g_final': 'delta_w', 'new_m_w_ada': 'new_m', 'new_m_b_ada': 'new_m', 'new_m_g_mix': 'new_m', 'new_m_w_in': 'new_m', 'new_m_conv_w': 'new_m', 'new_m_conv_b': 'new_m', 'new_m_dt_bias': 'new_m', 'new_m_a_log': 'new_m', 'new_m_d_skip': 'new_m', 'new_m_g_ssd': 'new_m', 'new_m_w_pool': 'new_m', 'new_m_pool_scale': 'new_m', 'new_m_w_out': 'new_m', 'new_m_g_mlp': 'new_m', 'new_m_w_up': 'new_m', 'new_m_w_down': 'new_m', 'new_m_g_final': 'new_m', 'new_v_w_ada': 'new_v', 'new_v_b_ada': 'new_v', 'new_v_g_mix': 'new_v', 'new_v_w_in': 'new_v', 'new_v_conv_w': 'new_v', 'new_v_conv_b': 'new_v', 'new_v_dt_bias': 'new_v', 'new_v_a_log': 'new_v', 'new_v_d_skip': 'new_v', 'new_v_g_ssd': 'new_v', 'new_v_w_pool': 'new_v', 'new_v_pool_scale': 'new_v', 'new_v_w_out': 'new_v', 'new_v_g_mlp': 'new_v', 'new_v_w_up': 'new_v', 'new_v_w_down': 'new_v', 'new_v_g_final': 'new_v'}


def _forward(args):
    return _fwd_reference(*[args[k] for k in FWD_PARAMS])


def _output_shape():
    out = _jax.eval_shape(lambda: _forward(_fwd_setup_inputs(0)))
    return out.shape, out.dtype

N_MICROBATCH = 1
ADAM_LR = 0.001
ADAM_B1 = 0.9
ADAM_B2 = 0.999
ADAM_EPS = 1e-08
ADAM_WD = 0.01
ADAM_STEP = 10
PER_EXAMPLE_BATCH_AXIS = {'x': 0, 'c': 0, 'loss_target': 0}
SHARED_INPUTS = []
_WEIGHT_DTYPES = {'w_ada': _jnp.float32, 'b_ada': _jnp.float32, 'g_mix': _jnp.float32, 'w_in': _jnp.float32, 'conv_w': _jnp.float32, 'conv_b': _jnp.float32, 'dt_bias': _jnp.float32, 'a_log': _jnp.float32, 'd_skip': _jnp.float32, 'g_ssd': _jnp.float32, 'w_pool': _jnp.float32, 'pool_scale': _jnp.float32, 'w_out': _jnp.float32, 'g_mlp': _jnp.float32, 'w_up': _jnp.float32, 'w_down': _jnp.float32, 'g_final': _jnp.float32}
MOMENT_SCALE = {'w_ada': 1.308310e-01, 'b_ada': 2.350970e-01, 'g_mix': 1.196952e-01, 'w_in': 7.678317e-02, 'conv_w': 7.070313e-02, 'conv_b': 7.355286e-02, 'dt_bias': 5.343848e-01, 'a_log': 5.046265e-01, 'd_skip': 2.414982e-01, 'g_ssd': 7.914968e-02, 'w_pool': 8.050383e-02, 'pool_scale': 8.460023e-02, 'w_out': 9.907042e-02, 'g_mlp': 1.359657e-01, 'w_up': 8.498272e-02, 'w_down': 1.807494e-01, 'g_final': 3.529796e+01}


def _to_microbatches(a, axis):
    t = _jnp.moveaxis(a, axis, 0)
    t = t.reshape((N_MICROBATCH, t.shape[0] // N_MICROBATCH) + t.shape[1:])
    return _jnp.moveaxis(t, 1, axis + 1)


def setup_inputs(seed: int = 0) -> dict:
    inp = _fwd_setup_inputs(seed)
    key = _jax.random.fold_in(_jax.random.key(seed), 7919)
    shape, _ = _output_shape()
    out = dict(inp)
    out["loss_target"] = _jax.random.normal(_jax.random.fold_in(key, 0), shape, _jnp.float32)
    for i, name in enumerate(TWIN_WEIGHTS):
        w = inp[name].astype(_jnp.float32)
        if MOMENT_SCALE is None:
            s = _jnp.sqrt(_jnp.mean(_jnp.square(w)) + 1e-30)
        else:
            s = MOMENT_SCALE[name]
        km, kv = _jax.random.split(_jax.random.fold_in(key, i + 1))
        out[name] = w
        out["m_" + name] = s * _jax.random.normal(km, w.shape, _jnp.float32)
        out["v_" + name] = (s * s) * _jax.random.uniform(kv, w.shape, _jnp.float32, 0.5, 1.5)
    if N_MICROBATCH > 1:
        for name, axis in PER_EXAMPLE_BATCH_AXIS.items():
            out[name] = _to_microbatches(out[name], axis)
    return {'x': out['x'], 'c': out['c'], 'w_ada': out['w_ada'], 'b_ada': out['b_ada'], 'g_mix': out['g_mix'], 'w_in': out['w_in'], 'conv_w': out['conv_w'], 'conv_b': out['conv_b'], 'dt_bias': out['dt_bias'], 'a_log': out['a_log'], 'd_skip': out['d_skip'], 'g_ssd': out['g_ssd'], 'w_pool': out['w_pool'], 'pool_scale': out['pool_scale'], 'w_out': out['w_out'], 'g_mlp': out['g_mlp'], 'w_up': out['w_up'], 'w_down': out['w_down'], 'g_final': out['g_final'], 'loss_target': out['loss_target'], 'm_w_ada': out['m_w_ada'], 'm_b_ada': out['m_b_ada'], 'm_g_mix': out['m_g_mix'], 'm_w_in': out['m_w_in'], 'm_conv_w': out['m_conv_w'], 'm_conv_b': out['m_conv_b'], 'm_dt_bias': out['m_dt_bias'], 'm_a_log': out['m_a_log'], 'm_d_skip': out['m_d_skip'], 'm_g_ssd': out['m_g_ssd'], 'm_w_pool': out['m_w_pool'], 'm_pool_scale': out['m_pool_scale'], 'm_w_out': out['m_w_out'], 'm_g_mlp': out['m_g_mlp'], 'm_w_up': out['m_w_up'], 'm_w_down': out['m_w_down'], 'm_g_final': out['m_g_final'], 'v_w_ada': out['v_w_ada'], 'v_b_ada': out['v_b_ada'], 'v_g_mix': out['v_g_mix'], 'v_w_in': out['v_w_in'], 'v_conv_w': out['v_conv_w'], 'v_conv_b': out['v_conv_b'], 'v_dt_bias': out['v_dt_bias'], 'v_a_log': out['v_a_log'], 'v_d_skip': out['v_d_skip'], 'v_g_ssd': out['v_g_ssd'], 'v_w_pool': out['v_w_pool'], 'v_pool_scale': out['v_pool_scale'], 'v_w_out': out['v_w_out'], 'v_g_mlp': out['v_g_mlp'], 'v_w_up': out['v_w_up'], 'v_w_down': out['v_w_down'], 'v_g_final': out['v_g_final']}


def _loss(weights, diff, rest, loss_target):
    with _jax.named_scope("forward"):
        args = {**rest, TWIN_DIFF_INPUT: diff, **{k: w.astype(_WEIGHT_DTYPES[k]) for k, w in weights.items()}}
        y = _forward(args)
    with _jax.named_scope("loss_head"):
        err = _jnp.square(y.astype(_jnp.float32) - loss_target)
        return 0.5 * _jnp.sum(_jnp.mean(err, axis=-1)) if err.ndim else 0.5 * err


def _adamw(w, g, m, v):
    m = ADAM_B1 * m + (1.0 - ADAM_B1) * g
    v = ADAM_B2 * v + (1.0 - ADAM_B2) * _jnp.square(g)
    m_hat = m / (1.0 - ADAM_B1 ** ADAM_STEP)
    v_hat = v / (1.0 - ADAM_B2 ** ADAM_STEP)
    delta = -ADAM_LR * (m_hat / (_jnp.sqrt(v_hat) + ADAM_EPS) + ADAM_WD * w)
    return delta, m, v


def reference(x, c, w_ada, b_ada, g_mix, w_in, conv_w, conv_b, dt_bias, a_log, d_skip, g_ssd, w_pool, pool_scale, w_out, g_mlp, w_up, w_down, g_final, loss_target, m_w_ada, m_b_ada, m_g_mix, m_w_in, m_conv_w, m_conv_b, m_dt_bias, m_a_log, m_d_skip, m_g_ssd, m_w_pool, m_pool_scale, m_w_out, m_g_mlp, m_w_up, m_w_down, m_g_final, v_w_ada, v_b_ada, v_g_mix, v_w_in, v_conv_w, v_conv_b, v_dt_bias, v_a_log, v_d_skip, v_g_ssd, v_w_pool, v_pool_scale, v_w_out, v_g_mlp, v_w_up, v_w_down, v_g_final):
    given = dict(x=x, c=c, w_ada=w_ada, b_ada=b_ada, g_mix=g_mix, w_in=w_in, conv_w=conv_w, conv_b=conv_b, dt_bias=dt_bias, a_log=a_log, d_skip=d_skip, g_ssd=g_ssd, w_pool=w_pool, pool_scale=pool_scale, w_out=w_out, g_mlp=g_mlp, w_up=w_up, w_down=w_down, g_final=g_final, loss_target=loss_target, m_w_ada=m_w_ada, m_b_ada=m_b_ada, m_g_mix=m_g_mix, m_w_in=m_w_in, m_conv_w=m_conv_w, m_conv_b=m_conv_b, m_dt_bias=m_dt_bias, m_a_log=m_a_log, m_d_skip=m_d_skip, m_g_ssd=m_g_ssd, m_w_pool=m_w_pool, m_pool_scale=m_pool_scale, m_w_out=m_w_out, m_g_mlp=m_g_mlp, m_w_up=m_w_up, m_w_down=m_w_down, m_g_final=m_g_final, v_w_ada=v_w_ada, v_b_ada=v_b_ada, v_g_mix=v_g_mix, v_w_in=v_w_in, v_conv_w=v_conv_w, v_conv_b=v_conv_b, v_dt_bias=v_dt_bias, v_a_log=v_a_log, v_d_skip=v_d_skip, v_g_ssd=v_g_ssd, v_w_pool=v_w_pool, v_pool_scale=v_pool_scale, v_w_out=v_w_out, v_g_mlp=v_g_mlp, v_w_up=v_w_up, v_w_down=v_w_down, v_g_final=v_g_final)
    weights = {n: given[n] for n in TWIN_WEIGHTS}
    shared = {n: given[n] for n in SHARED_INPUTS}
    per_example = {n: given[n] for n in ['x', 'c']}
    grad_fn = _jax.value_and_grad(_loss, argnums=(0, 1))

    def one_microbatch(ex, loss_target):
        ex = dict(ex)
        diff = ex.pop(TWIN_DIFF_INPUT)
        return grad_fn(weights, diff, {**shared, **ex}, loss_target)

    if N_MICROBATCH == 1:
        loss, (grad_w, grad_x) = one_microbatch(per_example, given["loss_target"])
    else:
        def body(carry, xs):
            loss_sum, grad_sum = carry
            l_k, (gw_k, gx_k) = one_microbatch(xs[0], xs[1])
            with _jax.named_scope("update"):
                return (loss_sum + l_k, _jax.tree.map(_jnp.add, grad_sum, gw_k)), gx_k

        init = (_jnp.zeros((), _jnp.float32), _jax.tree.map(_jnp.zeros_like, weights))
        (loss, grad_w), grad_x = _jax.lax.scan(body, init, (per_example, given["loss_target"]))
    with _jax.named_scope("update"):
        delta_w, new_m, new_v = {}, {}, {}
        for n in TWIN_WEIGHTS:
            delta_w[n], new_m[n], new_v[n] = _adamw(weights[n], grad_w[n], given["m_" + n], given["v_" + n])
    return (loss, grad_x, *[grad_w[n] for n in TWIN_WEIGHTS], *[delta_w[n] for n in TWIN_WEIGHTS],
            *[new_m[n] for n in TWIN_WEIGHTS], *[new_v[n] for n in TWIN_WEIGHTS])
```

```python
import functools

import jax
import jax.numpy as jnp
from jax import lax
from jax.experimental import pallas as pl
from jax.experimental.pallas import tpu as pltpu

F32, BF16 = jnp.float32, jnp.bfloat16
MESH = pl.DeviceIdType.MESH
N_DEV = 8
D = 1024
LANES = 128
CHUNK = 128
POOL_W = 512
WINDOWS = (2, 4, 8, 16)
N_HEADS = 16
HEAD_DIM = 64
N_GROUPS = 2
GROUP_W = 512
N_STATE = 128
CONV_CH = 1536
OFF_Z, OFF_XBC, OFF_DT, IN_W = 512, 1536, 3072, 3088
MIX_W = 1536
D_FF = 4096
FF_BLK = 512
EPS = 1e-5
LR, B1, B2, AEPS, WD, STEP = 0.001, 0.9, 0.999, 1e-08, 0.01, 10
POOL_HALO = 16
CONV_HALO = 8
VMEM_LIMIT = 56 << 20
ADAM_BLOCK_BYTES = 1600 << 10


def _cparams(**kw):
    return pltpu.CompilerParams(vmem_limit_bytes=VMEM_LIMIT, **kw)


def _mm(a, b):
    return jnp.dot(a.astype(BF16), b.astype(BF16), preferred_element_type=F32)


def _mm_nt(a, b):
    return lax.dot_general(a.astype(BF16), b.astype(BF16), (((1,), (1,)), ((), ())), preferred_element_type=F32)


def _mm_tn(a, b):
    return lax.dot_general(a.astype(BF16), b.astype(BF16), (((0,), (0,)), ((), ())), preferred_element_type=F32)


def _mm_exact(a, b):
    return jnp.dot(a, b, preferred_element_type=F32, precision=lax.Precision.HIGHEST)


def _sigmoid(v):
    return 1.0 / (1.0 + jnp.exp(-v))


def _expand_mat():
    r = lax.broadcasted_iota(jnp.int32, (LANES, D), 0)
    c = lax.broadcasted_iota(jnp.int32, (LANES, D), 1)
    return (r == c // HEAD_DIM).astype(F32)


def _reduce_mat():
    r = lax.broadcasted_iota(jnp.int32, (D, LANES), 0)
    c = lax.broadcasted_iota(jnp.int32, (D, LANES), 1)
    return (c == r // HEAD_DIM).astype(F32)


def _all_gather(xs, name):
    n = len(xs)

    def body(*refs):
        x_refs, o_refs = refs[:n], refs[n:2 * n]
        send, recv, loc = refs[2 * n:]
        x, y, c = lax.axis_index("x"), lax.axis_index("y"), lax.axis_index("c")
        me, sib = (x, y, c), (x, y, 1 - c)
        chips = [(1 - x, y), (x, 1 - y), (1 - x, 1 - y)]

        def idx(p):
            return 4 * p[0] + 2 * p[1] + p[2]

        def cp(a, k, block, to, src=None):
            dst = o_refs[a].at[idx(block)]
            return pltpu.make_async_remote_copy(
                src_ref=dst if src is None else src, dst_ref=dst,
                send_sem=send.at[a * 7 + k], recv_sem=recv.at[a * 7 + k],
                device_id=to, device_id_type=MESH)

        mine = [pltpu.make_async_copy(x_refs[a], o_refs[a].at[idx(me)], loc.at[a]) for a in range(n)]
        for m in mine:
            m.start()
        first = []
        for a in range(n):
            first.append(cp(a, 0, me, sib, src=x_refs[a]))
            first += [cp(a, 1 + j, me, (*chip, c), src=x_refs[a]) for j, chip in enumerate(chips)]
        for f in first:
            f.start()
        passed = []
        for j, chip in enumerate(chips):
            for a in range(n):
                cp(a, 1 + j, (*chip, c), me).wait_recv()
                p = cp(a, 4 + j, (*chip, c), sib)
                p.start()
                passed.append(p)
        for a in range(n):
            cp(a, 0, sib, me).wait_recv()
            for j, chip in enumerate(chips):
                cp(a, 4 + j, (*chip, 1 - c), me).wait_recv()
        for f in first + passed:
            f.wait_send()
        for m in mine:
            m.wait()

    any_spec = pl.BlockSpec(memory_space=pl.ANY)
    return pl.pallas_call(
        body, name=name,
        out_shape=[jax.ShapeDtypeStruct((N_DEV,) + v.shape, v.dtype) for v in xs],
        in_specs=[any_spec] * n, out_specs=[any_spec] * n,
        scratch_shapes=[pltpu.SemaphoreType.DMA((7 * n,)), pltpu.SemaphoreType.DMA((7 * n,)),
                        pltpu.SemaphoreType.DMA((n,))],
    )(*xs)


def _all_to_all(xs, name):
    n = len(xs)

    def body(*refs):
        x_refs, o_refs = refs[:n], refs[n:2 * n]
        send, recv, loc = refs[2 * n:]
        x, y, c = lax.axis_index("x"), lax.axis_index("y"), lax.axis_index("c")
        me_i = 4 * x + 2 * y + c
        mine = [pltpu.make_async_copy(x_refs[a].at[me_i], o_refs[a].at[me_i], loc.at[a]) for a in range(n)]
        for m in mine:
            m.start()
        peers = []
        for k in range(1, N_DEV):
            fx, fy, fc = (k >> 2) & 1, (k >> 1) & 1, k & 1
            px = 1 - x if fx else x
            py = 1 - y if fy else y
            pc = 1 - c if fc else c
            peers.append(((px, py, pc), 4 * px + 2 * py + pc))
        sends = []
        for a in range(n):
            for k, (peer, peer_i) in enumerate(peers):
                sends.append(pltpu.make_async_remote_copy(
                    src_ref=x_refs[a].at[peer_i], dst_ref=o_refs[a].at[me_i],
                    send_sem=send.at[a * 7 + k], recv_sem=recv.at[a * 7 + k],
                    device_id=peer, device_id_type=MESH))
        for s in sends:
            s.start()
        for a in range(n):
            for k, (peer, peer_i) in enumerate(peers):
                pltpu.make_async_remote_copy(
                    src_ref=x_refs[a].at[peer_i], dst_ref=o_refs[a].at[peer_i],
                    send_sem=send.at[a * 7 + k], recv_sem=recv.at[a * 7 + k],
                    device_id=peer, device_id_type=MESH).wait_recv()
        for s in sends:
            s.wait_send()
        for m in mine:
            m.wait()

    any_spec = pl.BlockSpec(memory_space=pl.ANY)
    return pl.pallas_call(
        body, name=name,
        out_shape=[jax.ShapeDtypeStruct(v.shape, v.dtype) for v in xs],
        in_specs=[any_spec] * n, out_specs=[any_spec] * n,
        scratch_shapes=[pltpu.SemaphoreType.DMA((7 * n,)), pltpu.SemaphoreType.DMA((7 * n,)),
                        pltpu.SemaphoreType.DMA((n,))],
    )(*xs)


def _ada_fwd(c_all, w_ada, b_slice):
    def body(c_ref, w_ref, b_ref, o_ref):
        cv = c_ref[...]
        act = cv * _sigmoid(cv)
        o_ref[...] = _mm(act, w_ref[...]) + b_ref[...]

    nb, nc = c_all.shape[0], w_ada.shape[1]
    return pl.pallas_call(body, name="ada_fwd", out_shape=jax.ShapeDtypeStruct((nb, nc), F32),
                          compiler_params=_cparams())(c_all, w_ada, b_slice)


def _adam_math(w, g, m, v):
    m = B1 * m + (1.0 - B1) * g
    v = B2 * v + (1.0 - B2) * jnp.square(g)
    m_hat = m / (1.0 - B1 ** STEP)
    v_hat = v / (1.0 - B2 ** STEP)
    delta = -LR * (m_hat / (jnp.sqrt(v_hat) + AEPS) + WD * w)
    return delta, m, v


def _ada_bwd_adam(c_all, dmod_slice, w, m, v):
    rows, cols = w.shape
    br = 256

    def body(c_ref, d_ref, w_ref, m_ref, v_ref, g_out, dl_out, m_out, v_out):
        cv = c_ref[...]
        act = cv * _sigmoid(cv)
        g = _mm_tn(act, d_ref[...])
        g_out[...] = g
        dl, mn, vn = _adam_math(w_ref[...], g, m_ref[...], v_ref[...])
        dl_out[...] = dl
        m_out[...] = mn
        v_out[...] = vn

    nb = c_all.shape[0]
    wspec = pl.BlockSpec((br, cols), lambda i: (i, 0))
    return pl.pallas_call(
        body, name="ada_bwd_adam", grid=(rows // br,),
        in_specs=[pl.BlockSpec((nb, br), lambda i: (0, i)), pl.BlockSpec((nb, cols), lambda i: (0, 0)),
                  wspec, wspec, wspec],
        out_specs=[wspec] * 4, out_shape=[jax.ShapeDtypeStruct((rows, cols), F32)] * 4,
        compiler_params=_cparams(),
    )(c_all, dmod_slice, w, m, v)


def _mix_in(x2, mod, g_mix, w_main, w_dt, seq):
    T = x2.shape[0]
    tm = min(512, seq)
    tps = seq // tm

    def body(x_ref, mod_ref, g_ref, w_ref, wdt_ref, u_ref, pm_ref, dt_ref):
        x = x_ref[...]
        r = lax.rsqrt(jnp.mean(x * x, axis=-1, keepdims=True) + EPS)
        md = mod_ref[0]
        u = (x * r * g_ref[...]) * (1.0 + md[1:2]) + md[0:1]
        ub = u.astype(BF16)
        u_ref[...] = ub
        pm_ref[...] = jnp.dot(ub, w_ref[...], preferred_element_type=F32)
        dt_ref[...] = jnp.dot(ub, wdt_ref[...], preferred_element_type=F32)

    whole = pl.BlockSpec(memory_space=pltpu.VMEM)
    return pl.pallas_call(
        body, name="mix_in", grid=(T // tm,),
        in_specs=[pl.BlockSpec((tm, D), lambda i: (i, 0)), pl.BlockSpec((1, 8, D), lambda i: (i // tps, 0, 0)),
                  pl.BlockSpec((1, D), lambda i: (0, 0)), whole, whole],
        out_specs=[pl.BlockSpec((tm, D), lambda i: (i, 0)), pl.BlockSpec((tm, OFF_DT), lambda i: (i, 0)),
                   pl.BlockSpec((tm, LANES), lambda i: (i, 0))],
        out_shape=[jax.ShapeDtypeStruct((T, D), BF16), jax.ShapeDtypeStruct((T, OFF_DT), F32),
                   jax.ShapeDtypeStruct((T, LANES), F32)],
        compiler_params=_cparams(),
    )(x2, mod, g_mix, w_main, w_dt)


def _chunk_forward(up, z, ux, dtin, halo_p, halo_x, hprev, cw, cb, hp, gssd, wpool, pscale, t0, y_scr):
    L = CHUNK
    out = {}
    row = lax.broadcasted_iota(jnp.int32, (L, 1), 0)
    t = (t0 + row + 1).astype(F32)
    e = jnp.concatenate([halo_p, up], axis=0)
    s2 = e + pltpu.roll(e, 1, 0)
    s4 = s2 + pltpu.roll(s2, 2, 0)
    s8 = s4 + pltpu.roll(s4, 4, 0)
    s16 = s8 + pltpu.roll(s8, 8, 0)
    sums = (s2, s4, s8, s16)
    p, inv, yp = [], [], []
    for gi, w in enumerate(WINDOWS):
        sl = slice(gi * LANES, (gi + 1) * LANES)
        ic = 1.0 / jnp.minimum(t, float(w))
        pg = sums[gi][POOL_HALO:, sl] * ic - up[:, sl]
        p.append(pg)
        inv.append(ic)
        yp.append(_mm(pg, wpool[gi]))
    out["p"], out["inv"], out["yp"] = p, inv, yp
    out["y_pool"] = jnp.concatenate(yp, axis=1) * pscale
    ex = jnp.concatenate([halo_x, ux], axis=0)
    taps = [pltpu.roll(ex, 3, 0)[CONV_HALO:], pltpu.roll(ex, 2, 0)[CONV_HALO:], pltpu.roll(ex, 1, 0)[CONV_HALO:], ux]
    cv = cb + taps[0] * cw[0:1] + taps[1] * cw[1:2] + taps[2] * cw[2:3] + taps[3] * cw[3:4]
    sg = _sigmoid(cv)
    xbc = cv * sg
    out["taps"], out["cv"], out["sg"] = taps, cv, sg
    X = xbc[:, :D]
    Bm = xbc[:, D:D + N_GROUPS * N_STATE]
    Cm = xbc[:, D + N_GROUPS * N_STATE:]
    pre = dtin + hp[0:1]
    dt = jnp.maximum(pre, 0.0) + jnp.log(1.0 + jnp.exp(-jnp.abs(pre)))
    a_row = -jnp.exp(hp[1:2])
    da = dt * a_row
    ri = lax.broadcasted_iota(jnp.int32, (L, L), 0)
    ci = lax.broadcasted_iota(jnp.int32, (L, L), 1)
    causal = ri >= ci
    cum = _mm_exact(causal.astype(F32), da)
    cum_t = cum.T
    cum_last = cum[L - 1:L]
    eo = jnp.exp(cum)
    dec = jnp.exp(cum_last - cum)
    cd = jnp.exp(cum_last)
    exm = _expand_mat()
    dt_rep = _mm_exact(dt, exm)
    eo_rep = _mm_exact(eo, exm)
    dec_rep = _mm_exact(dec, exm)
    rows8 = jnp.concatenate([cd, hp[2:3], jnp.zeros((6, LANES), F32)], axis=0)
    rows8_rep = _mm_exact(rows8, exm)
    cd_rep, dskip_rep = rows8_rep[0:1], rows8_rep[1:2]
    xdt = X * dt_rep
    out.update(X=X, Bm=Bm, Cm=Cm, pre=pre, dt=dt, a_row=a_row, cum=cum, cum_t=cum_t, eo=eo, dec=dec, cd=cd,
               dt_rep=dt_rep, eo_rep=eo_rep, dec_rep=dec_rep, cd_rep=cd_rep, dskip_rep=dskip_rep, xdt=xdt,
               causal=causal, exm=exm)
    G, lms, yoff, hnew, xdec = [], [], [], [], []
    for g in range(N_GROUPS):
        gs = slice(g * GROUP_W, (g + 1) * GROUP_W)
        Bg = Bm[:, g * N_STATE:(g + 1) * N_STATE]
        Cg = Cm[:, g * N_STATE:(g + 1) * N_STATE]
        Gg = _mm_nt(Cg, Bg)
        G.append(Gg)
        for hh in range(N_HEADS // N_GROUPS):
            h = g * (N_HEADS // N_GROUPS) + hh
            seg = cum[:, h:h + 1] - cum_t[h:h + 1, :]
            lm = jnp.where(causal, jnp.exp(jnp.minimum(seg, 0.0)), 0.0)
            lms.append(lm)
            hs = slice(h * HEAD_DIM, (h + 1) * HEAD_DIM)
            y_scr[:, hs] = _mm(Gg * lm, xdt[:, hs])
        xd = xdt[:, gs] * dec_rep[:, gs]
        xdec.append(xd)
        sgm = _mm_tn(Bg, xd)
        yoff.append(_mm(Cg, hprev[g]) * eo_rep[:, gs])
        hnew.append(hprev[g] * cd_rep[:, gs] + sgm)
    out.update(G=G, lms=lms, yoff=yoff, hnew=hnew, xdec=xdec)
    y = y_scr[...] + jnp.concatenate(yoff, axis=1) + dskip_rep * X
    sz = _sigmoid(z)
    silz = z * sz
    yz = y * silz
    rg, yn = [], []
    for g in range(N_GROUPS):
        gs = slice(g * GROUP_W, (g + 1) * GROUP_W)
        r = lax.rsqrt(jnp.mean(yz[:, gs] * yz[:, gs], axis=-1, keepdims=True) + EPS)
        rg.append(r)
        yn.append(yz[:, gs] * r)
    yn = jnp.concatenate(yn, axis=1)
    out.update(y=y, sz=sz, silz=silz, rg=rg, yn=yn)
    out["y_ssd"] = yn * gssd
    return out


def _mixer_fwd(pm, dtp, cw, cb, hp, gssd, wpool, pscale, nb, seq):
    nc = seq // CHUNK

    def body(pm_ref, dt_ref, cw_ref, cb_ref, hp_ref, gs_ref, wp_ref, ps_ref, ym_ref, hs_ref,
             halo_p, halo_x, state, y_scr):
        c = pl.program_id(1)

        @pl.when(c == 0)
        def _():
            halo_p[...] = jnp.zeros_like(halo_p)
            halo_x[...] = jnp.zeros_like(halo_x)
            state[...] = jnp.zeros_like(state)

        up = pm_ref[:, 0:POOL_W]
        z = pm_ref[:, OFF_Z:OFF_XBC]
        ux = pm_ref[:, OFF_XBC:OFF_DT]
        hprev = [state[0], state[1]]
        hs_ref[0, 0, 0] = hprev[0]
        hs_ref[0, 0, 1] = hprev[1]
        o = _chunk_forward(up, z, ux, dt_ref[...], halo_p[...], halo_x[...], hprev, cw_ref[...], cb_ref[...],
                           hp_ref[...], gs_ref[...], wp_ref[...], ps_ref[...], c * CHUNK, y_scr)
        ym_ref[:, 0:POOL_W] = o["y_pool"].astype(BF16)
        ym_ref[:, POOL_W:] = o["y_ssd"].astype(BF16)
        state[0] = o["hnew"][0]
        state[1] = o["hnew"][1]
        halo_p[...] = up[CHUNK - POOL_HALO:]
        halo_x[...] = ux[CHUNK - CONV_HALO:]

    def full(shape):
        return pl.BlockSpec(shape, lambda b, c: (0,) * len(shape))

    T = nb * seq
    return pl.pallas_call(
        body, name="mixer_fwd", grid=(nb, nc),
        in_specs=[pl.BlockSpec((CHUNK, OFF_DT), lambda b, c: (b * nc + c, 0)),
                  pl.BlockSpec((CHUNK, LANES), lambda b, c: (b * nc + c, 0)),
                  full((4, CONV_CH)), full((1, CONV_CH)), full((8, LANES)), full((1, D)),
                  full((4, LANES, LANES)), full((1, POOL_W))],
        out_specs=[pl.BlockSpec((CHUNK, MIX_W), lambda b, c: (b * nc + c, 0)),
                   pl.BlockSpec((1, 1, N_GROUPS, N_STATE, GROUP_W), lambda b, c: (b, c, 0, 0, 0))],
        out_shape=[jax.ShapeDtypeStruct((T, MIX_W), BF16),
                   jax.ShapeDtypeStruct((nb, nc, N_GROUPS, N_STATE, GROUP_W), F32)],
        scratch_shapes=[pltpu.VMEM((POOL_HALO, POOL_W), F32), pltpu.VMEM((CONV_HALO, CONV_CH), F32),
                        pltpu.VMEM((N_GROUPS, N_STATE, GROUP_W), F32), pltpu.VMEM((CHUNK, D), F32)],
        compiler_params=_cparams(),
    )(pm, dtp, cw, cb, hp, gssd, wpool, pscale)


def _mixer_bwd(pm, dtp, dym, hstates, cw, cb, hp, gssd, wpool, pscale, nb, seq):
    nc = seq // CHUNK
    hpg = N_HEADS // N_GROUPS

    def body(pm_ref, hpool_ref, hxbc_ref, dt_ref, dy_ref, hs_ref, cw_ref, cb_ref, hp_ref, gs_ref, wp_ref, ps_ref,
             dpm_ref, ddt_ref, dcw_ref, dcb_ref, dhp_ref, dgs_ref, dwp_ref, dps_ref,
             nxt_q, nxt_cv, rstate, y_scr, dx_scr):
        b = pl.program_id(0)
        ci = pl.program_id(1)
        c = nc - 1 - ci

        @pl.when((b == 0) & (ci == 0))
        def _():
            for r in (dcw_ref, dcb_ref, dhp_ref, dgs_ref, dwp_ref, dps_ref):
                r[...] = jnp.zeros_like(r)

        @pl.when(ci == 0)
        def _():
            nxt_q[...] = jnp.zeros_like(nxt_q)
            nxt_cv[...] = jnp.zeros_like(nxt_cv)
            rstate[...] = jnp.zeros_like(rstate)

        first = (c > 0).astype(F32)
        up = pm_ref[:, 0:POOL_W]
        z = pm_ref[:, OFF_Z:OFF_XBC]
        ux = pm_ref[:, OFF_XBC:OFF_DT]
        halo_p = hpool_ref[...] * first
        halo_x = hxbc_ref[...] * first
        hprev = [hs_ref[0, 0, 0], hs_ref[0, 0, 1]]
        cw, cb, hp, gssd, wpool, pscale = cw_ref[...], cb_ref[...], hp_ref[...], gs_ref[...], wp_ref[...], ps_ref[...]
        o = _chunk_forward(up, z, ux, dt_ref[...], halo_p, halo_x, hprev, cw, cb, hp, gssd, wpool, pscale,
                           c * CHUNK, y_scr)
        L = CHUNK
        dy_pool = dy_ref[:, 0:POOL_W].astype(F32)
        dy_ssd = dy_ref[:, POOL_W:].astype(F32)

        dps_ref[...] += jnp.sum(dy_pool * jnp.concatenate(o["yp"], axis=1), axis=0, keepdims=True)
        dyp = dy_pool * pscale
        qs = []
        dps = []
        for gi in range(len(WINDOWS)):
            sl = slice(gi * LANES, (gi + 1) * LANES)
            dwp_ref[gi] += _mm_tn(o["p"][gi], dyp[:, sl])
            dpg = _mm_nt(dyp[:, sl], wpool[gi])
            dps.append(dpg)
            qs.append(dpg * o["inv"][gi])
        q = jnp.concatenate(qs, axis=1)
        e = jnp.concatenate([q, nxt_q[...]], axis=0)
        n = L + POOL_HALO
        s2 = e + pltpu.roll(e, n - 1, 0)
        s4 = s2 + pltpu.roll(s2, n - 2, 0)
        s8 = s4 + pltpu.roll(s4, n - 4, 0)
        s16 = s8 + pltpu.roll(s8, n - 8, 0)
        sums = (s2, s4, s8, s16)
        for gi in range(len(WINDOWS)):
            sl = slice(gi * LANES, (gi + 1) * LANES)
            dpm_ref[:, sl] = sums[gi][:L, sl] - dps[gi]
        nxt_q[...] = q[:POOL_HALO]

        yn, y, silz, sz = o["yn"], o["y"], o["silz"], o["sz"]
        dgs_ref[...] += jnp.sum(dy_ssd * yn, axis=0, keepdims=True)
        dyn = dy_ssd * gssd
        dyz = []
        for g in range(N_GROUPS):
            gs = slice(g * GROUP_W, (g + 1) * GROUP_W)
            mean = jnp.mean(dyn[:, gs] * yn[:, gs], axis=-1, keepdims=True)
            dyz.append(o["rg"][g] * (dyn[:, gs] - yn[:, gs] * mean))
        dyz = jnp.concatenate(dyz, axis=1)
        dyv = dyz * silz
        dpm_ref[:, OFF_Z:OFF_XBC] = dyz * y * (sz * (1.0 + z * (1.0 - sz)))

        X, Bm, Cm, xdt = o["X"], o["Bm"], o["Cm"], o["xdt"]
        exm = o["exm"]
        rdm = _reduce_mat()
        lane = lax.broadcasted_iota(jnp.int32, (1, LANES), 1)
        sub = lax.broadcasted_iota(jnp.int32, (LANES, 1), 0)
        d_dskip = _mm_exact(jnp.sum(dyv * X, axis=0, keepdims=True) * jnp.ones((8, 1), F32), rdm)[0:1]
        dX = o["dskip_rep"] * dyv
        yoff_full = jnp.concatenate(o["yoff"], axis=1)
        dcum = _mm_exact(dyv * yoff_full, rdm)
        rs = jnp.zeros((L, LANES), F32)
        cs_t = jnp.zeros((LANES, L), F32)
        dBs, dCs = [], []
        dcd_row = jnp.zeros((1, LANES), F32)
        ddec = []
        for g in range(N_GROUPS):
            gs = slice(g * GROUP_W, (g + 1) * GROUP_W)
            Bg = Bm[:, g * N_STATE:(g + 1) * N_STATE]
            Cg = Cm[:, g * N_STATE:(g + 1) * N_STATE]
            Gg = o["G"][g]
            R = rstate[g]
            dwm = dyv[:, gs] * o["eo_rep"][:, gs]
            dC = _mm_nt(dwm, hprev[g])
            dH = _mm_tn(Cg, dwm)
            dG = jnp.zeros((L, L), F32)
            for hh in range(hpg):
                h = g * hpg + hh
                hs = slice(h * HEAD_DIM, (h + 1) * HEAD_DIM)
                lm = o["lms"][h]
                m_h = Gg * lm
                dM = _mm_nt(dyv[:, hs], xdt[:, hs])
                dx_scr[:, hs] = _mm_tn(m_h, dyv[:, hs])
                qm = dM * m_h
                rs = rs + jnp.sum(qm, axis=1, keepdims=True) * (lane == h).astype(F32)
                cs_t = cs_t + (sub == h).astype(F32) * jnp.sum(qm, axis=0, keepdims=True)
                dG = dG + dM * lm
            dC = dC + _mm(dG, Bg)
            dB = _mm_tn(dG, Cg)
            zx = _mm(Bg, R)
            dxdt_state = zx * o["dec_rep"][:, gs]
            ddec.append(zx * xdt[:, gs])
            dB = dB + _mm_nt(o["xdec"][g], R)
            dcd_row = dcd_row + _mm_exact(jnp.sum(R * hprev[g], axis=0, keepdims=True) * jnp.ones((8, 1), F32),
                                          rdm[gs, :])[0:1]
            rstate[g] = dH + o["cd_rep"][:, gs] * R
            dx_scr[:, gs] = dx_scr[:, gs] + dxdt_state
            dBs.append(dB)
            dCs.append(dC)
        dxdt = dx_scr[...]
        ddec_h = _mm_exact(jnp.concatenate(ddec, axis=1), rdm) * o["dec"]
        dcum_last = jnp.sum(ddec_h, axis=0, keepdims=True) + dcd_row * o["cd"]
        dcum = dcum + rs - cs_t.T - ddec_h + (sub == L - 1).astype(F32) * dcum_last
        dda = _mm_exact(o["causal"].astype(F32).T, dcum)
        ddt_v = dda * o["a_row"] + _mm_exact(dxdt * X, rdm)
        dX = dX + dxdt * o["dt_rep"]
        head_mask = (lane < N_HEADS).astype(F32)
        d_alog = jnp.sum(dda * o["dt"], axis=0, keepdims=True) * o["a_row"] * head_mask
        dpre = ddt_v * _sigmoid(o["pre"]) * head_mask
        ddt_ref[...] = dpre
        d_dtb = jnp.sum(dpre, axis=0, keepdims=True)
        dhp_ref[...] += jnp.concatenate([d_dtb, d_alog, d_dskip * head_mask, jnp.zeros((5, LANES), F32)], axis=0)

        dxbc = jnp.concatenate([dX] + dBs + dCs, axis=1)
        sg, cv = o["sg"], o["cv"]
        dcv = dxbc * (sg * (1.0 + cv * (1.0 - sg)))
        dcb_ref[...] += jnp.sum(dcv, axis=0, keepdims=True)
        dcw_ref[...] += jnp.concatenate(
            [jnp.sum(dcv * o["taps"][k], axis=0, keepdims=True) for k in range(4)], axis=0)
        e2 = jnp.concatenate([dcv, nxt_cv[...]], axis=0)
        n2 = L + CONV_HALO
        dux = (dcv * cw[3:4] + pltpu.roll(e2, n2 - 1, 0)[:L] * cw[2:3] + pltpu.roll(e2, n2 - 2, 0)[:L] * cw[1:2]
               + pltpu.roll(e2, n2 - 3, 0)[:L] * cw[0:1])
        dpm_ref[:, OFF_XBC:OFF_DT] = dux
        nxt_cv[...] = dcv[:CONV_HALO]

    def full(shape):
        return pl.BlockSpec(shape, lambda b, c: (0,) * len(shape))

    def rowblk(b, c):
        return b * nc + (nc - 1 - c)

    hp_blocks = CHUNK // POOL_HALO
    hx_blocks = CHUNK // CONV_HALO
    T = nb * seq
    return pl.pallas_call(
        body, name="mixer_bwd", grid=(nb, nc),
        in_specs=[pl.BlockSpec((CHUNK, OFF_DT), lambda b, c: (rowblk(b, c), 0)),
                  pl.BlockSpec((POOL_HALO, POOL_W), lambda b, c: (jnp.maximum(rowblk(b, c) * hp_blocks - 1, 0), 0)),
                  pl.BlockSpec((CONV_HALO, CONV_CH), lambda b, c: (jnp.maximum(rowblk(b, c) * hx_blocks - 1, 0), 1)),
                  pl.BlockSpec((CHUNK, LANES), lambda b, c: (rowblk(b, c), 0)),
                  pl.BlockSpec((CHUNK, MIX_W), lambda b, c: (rowblk(b, c), 0)),
                  pl.BlockSpec((1, 1, N_GROUPS, N_STATE, GROUP_W), lambda b, c: (b, nc - 1 - c, 0, 0, 0)),
                  full((4, CONV_CH)), full((1, CONV_CH)), full((8, LANES)), full((1, D)),
                  full((4, LANES, LANES)), full((1, POOL_W))],
        out_specs=[pl.BlockSpec((CHUNK, OFF_DT), lambda b, c: (rowblk(b, c), 0)),
                   pl.BlockSpec((CHUNK, LANES), lambda b, c: (rowblk(b, c), 0)),
                   full((4, CONV_CH)), full((1, CONV_CH)), full((8, LANES)), full((1, D)),
                   full((4, LANES, LANES)), full((1, POOL_W))],
        out_shape=[jax.ShapeDtypeStruct((T, OFF_DT), F32), jax.ShapeDtypeStruct((T, LANES), F32),
                   jax.ShapeDtypeStruct((4, CONV_CH), F32), jax.ShapeDtypeStruct((1, CONV_CH), F32),
                   jax.ShapeDtypeStruct((8, LANES), F32), jax.ShapeDtypeStruct((1, D), F32),
                   jax.ShapeDtypeStruct((4, LANES, LANES), F32), jax.ShapeDtypeStruct((1, POOL_W), F32)],
        scratch_shapes=[pltpu.VMEM((POOL_HALO, POOL_W), F32), pltpu.VMEM((CONV_HALO, CONV_CH), F32),
                        pltpu.VMEM((N_GROUPS, N_STATE, GROUP_W), F32), pltpu.VMEM((CHUNK, D), F32),
                        pltpu.VMEM((CHUNK, D), F32)],
        compiler_params=_cparams(),
    )(pm, pm, pm, dtp, dym, hstates, cw, cb, hp, gssd, wpool, pscale)


def _mlp_fused(x2, ymix, target, mod, g_mlp, g_final, w_out, w_up, w_down, seq):
    T = x2.shape[0]
    tm = min(256, seq)
    tps = seq // tm
    nblk = D_FF // FF_BLK

    def body(x_ref, ym_ref, tg_ref, mod_ref, gm_ref, gf_ref, wo_ref, wu_ref, wd_ref,
             da_ref, dym_ref, dh1_ref, u2_ref, f_ref, dup_ref, ddn_ref, dmod_ref, acc_ref, relu_scr):
        i = pl.program_id(0)

        @pl.when(i == 0)
        def _():
            acc_ref[...] = jnp.zeros_like(acc_ref)

        @pl.when(i % tps == 0)
        def _():
            dmod_ref[...] = jnp.zeros_like(dmod_ref)

        md = mod_ref[0]
        gate_m, shift_f, scale_f, gate_f = md[2:3], md[3:4], md[4:5], md[5:6]
        g_mlp, g_fin = gm_ref[...], gf_ref[...]
        a = jnp.dot(ym_ref[...], wo_ref[...], preferred_element_type=F32)
        h1 = x_ref[...] + gate_m * a
        r2 = lax.rsqrt(jnp.mean(h1 * h1, axis=-1, keepdims=True) + EPS)
        n2 = h1 * r2
        u2 = (n2 * g_mlp) * (1.0 + scale_f) + shift_f
        u2b = u2.astype(BF16)
        u2_ref[...] = u2b
        dn = jnp.zeros((tm, D), F32)
        for j in range(nblk):
            js = slice(j * FF_BLK, (j + 1) * FF_BLK)
            upj = jnp.maximum(jnp.dot(u2b, wu_ref[j], preferred_element_type=F32), 0.0)
            relu_scr[:, js] = upj
            fj = (upj * upj).astype(BF16)
            f_ref[:, js] = fj
            dn = dn + jnp.dot(fj, wd_ref[j], preferred_element_type=F32)
        h2 = h1 + gate_f * dn
        r3 = lax.rsqrt(jnp.mean(h2 * h2, axis=-1, keepdims=True) + EPS)
        n3 = h2 * r3
        err = n3 * g_fin - tg_ref[...]
        loss = 0.5 * jnp.sum(jnp.mean(err * err, axis=-1, keepdims=True), axis=0, keepdims=True)
        dout = err * (1.0 / D)
        d_gfin = jnp.sum(dout * n3, axis=0, keepdims=True)
        dn3 = dout * g_fin
        dh2 = r3 * (dn3 - n3 * jnp.mean(dn3 * n3, axis=-1, keepdims=True))
        d_gate_f = jnp.sum(dh2 * dn, axis=0, keepdims=True)
        ddn = (gate_f * dh2).astype(BF16)
        ddn_ref[...] = ddn
        du2 = jnp.zeros((tm, D), F32)
        for j in range(nblk):
            js = slice(j * FF_BLK, (j + 1) * FF_BLK)
            dfj = lax.dot_general(ddn, wd_ref[j], (((1,), (1,)), ((), ())), preferred_element_type=F32)
            dupj = (dfj * (2.0 * relu_scr[:, js])).astype(BF16)
            dup_ref[:, js] = dupj
            du2 = du2 + lax.dot_general(dupj, wu_ref[j], (((1,), (1,)), ((), ())), preferred_element_type=F32)
        d_scale_f = jnp.sum(du2 * (n2 * g_mlp), axis=0, keepdims=True)
        d_shift_f = jnp.sum(du2, axis=0, keepdims=True)
        d_gmlp = jnp.sum(du2 * (1.0 + scale_f) * n2, axis=0, keepdims=True)
        dn2 = du2 * (g_mlp * (1.0 + scale_f))
        dh1 = dh2 + r2 * (dn2 - n2 * jnp.mean(dn2 * n2, axis=-1, keepdims=True))
        dh1_ref[...] = dh1
        d_gate_m = jnp.sum(dh1 * a, axis=0, keepdims=True)
        da = (gate_m * dh1).astype(BF16)
        da_ref[...] = da
        dym_ref[...] = lax.dot_general(da, wo_ref[...], (((1,), (1,)), ((), ())),
                                       preferred_element_type=F32).astype(BF16)
        dmod_ref[0] += jnp.concatenate([d_gate_m, d_shift_f, d_scale_f, d_gate_f, jnp.zeros((4, D), F32)], axis=0)
        acc_ref[...] += jnp.concatenate([d_gmlp, d_gfin, loss * jnp.ones((1, D), F32), jnp.zeros((5, D), F32)], axis=0)

    whole = pl.BlockSpec(memory_space=pltpu.VMEM)

    def tok(w):
        return pl.BlockSpec((tm, w), lambda i: (i, 0))

    def vec():
        return pl.BlockSpec((1, D), lambda i: (0, 0))

    nb = T // seq
    return pl.pallas_call(
        body, name="mlp_fused", grid=(T // tm,),
        in_specs=[tok(D), tok(MIX_W), tok(D), pl.BlockSpec((1, 8, D), lambda i: (i // tps, 0, 0)), vec(), vec(),
                  whole, whole, whole],
        out_specs=[tok(D), tok(MIX_W), tok(D), tok(D), tok(D_FF), tok(D_FF), tok(D),
                   pl.BlockSpec((1, 8, D), lambda i: (i // tps, 0, 0)), pl.BlockSpec((8, D), lambda i: (0, 0))],
        out_shape=[jax.ShapeDtypeStruct((T, D), BF16), jax.ShapeDtypeStruct((T, MIX_W), BF16),
                   jax.ShapeDtypeStruct((T, D), F32), jax.ShapeDtypeStruct((T, D), BF16),
                   jax.ShapeDtypeStruct((T, D_FF), BF16), jax.ShapeDtypeStruct((T, D_FF), BF16),
                   jax.ShapeDtypeStruct((T, D), BF16), jax.ShapeDtypeStruct((nb, 8, D), F32),
                   jax.ShapeDtypeStruct((8, D), F32)],
        scratch_shapes=[pltpu.VMEM((tm, D_FF), F32)],
        compiler_params=_cparams(),
    )(x2, ymix, target, mod, g_mlp, g_final, w_out, w_up, w_down)


def _in_bwd(x2, dh1, dpm, ddt, mod, g_mix, w_main, w_dt, seq):
    T = x2.shape[0]
    tm = min(512, seq)
    tps = seq // tm

    def body(x_ref, dh_ref, dpm_ref, ddt_ref, mod_ref, g_ref, w_ref, wdt_ref, dx_ref, dpb_ref, dmod_ref, acc_ref):
        i = pl.program_id(0)

        @pl.when(i == 0)
        def _():
            acc_ref[...] = jnp.zeros_like(acc_ref)

        @pl.when(i % tps == 0)
        def _():
            dmod_ref[...] = jnp.zeros_like(dmod_ref)

        dpb = dpm_ref[...].astype(BF16)
        ddb = ddt_ref[...].astype(BF16)
        dpb_ref[:, 0:OFF_DT] = dpb
        dpb_ref[:, OFF_DT:] = ddb
        du = lax.dot_general(dpb, w_ref[...], (((1,), (1,)), ((), ())), preferred_element_type=F32)
        du = du + lax.dot_general(ddb, wdt_ref[...], (((1,), (1,)), ((), ())), preferred_element_type=F32)
        x = x_ref[...]
        md = mod_ref[0]
        g = g_ref[...]
        r = lax.rsqrt(jnp.mean(x * x, axis=-1, keepdims=True) + EPS)
        n1 = x * r
        d_scale = jnp.sum(du * (n1 * g), axis=0, keepdims=True)
        d_shift = jnp.sum(du, axis=0, keepdims=True)
        d_g = jnp.sum(du * (1.0 + md[1:2]) * n1, axis=0, keepdims=True)
        dn1 = du * (g * (1.0 + md[1:2]))
        dx_ref[...] = dh_ref[...] + r * (dn1 - n1 * jnp.mean(dn1 * n1, axis=-1, keepdims=True))
        dmod_ref[0] += jnp.concatenate([d_shift, d_scale, jnp.zeros((6, D), F32)], axis=0)
        acc_ref[...] += jnp.concatenate([d_g, jnp.zeros((7, D), F32)], axis=0)

    whole = pl.BlockSpec(memory_space=pltpu.VMEM)
    nb = T // seq
    pw = OFF_DT + LANES
    return pl.pallas_call(
        body, name="in_bwd", grid=(T // tm,),
        in_specs=[pl.BlockSpec((tm, D), lambda i: (i, 0)), pl.BlockSpec((tm, D), lambda i: (i, 0)),
                  pl.BlockSpec((tm, OFF_DT), lambda i: (i, 0)), pl.BlockSpec((tm, LANES), lambda i: (i, 0)),
                  pl.BlockSpec((1, 8, D), lambda i: (i // tps, 0, 0)), pl.BlockSpec((1, D), lambda i: (0, 0)),
                  whole, whole],
        out_specs=[pl.BlockSpec((tm, D), lambda i: (i, 0)), pl.BlockSpec((tm, pw), lambda i: (i, 0)),
                   pl.BlockSpec((1, 8, D), lambda i: (i // tps, 0, 0)), pl.BlockSpec((8, D), lambda i: (0, 0))],
        out_shape=[jax.ShapeDtypeStruct((T, D), F32), jax.ShapeDtypeStruct((T, pw), BF16),
                   jax.ShapeDtypeStruct((nb, 8, D), F32), jax.ShapeDtypeStruct((8, D), F32)],
        compiler_params=_cparams(),
    )(x2, dh1, dpm, ddt, mod, g_mix, w_main, w_dt)


def _dw(a, b, name, bm, bn):
    T, M = a.shape
    N = b.shape[1]
    bk = min(512, T)
    nk = T // bk

    def body(a_ref, b_ref, o_ref):
        k = pl.program_id(2)

        @pl.when(k == 0)
        def _():
            o_ref[...] = jnp.zeros_like(o_ref)

        o_ref[...] += lax.dot_general(a_ref[...], b_ref[...], (((0,), (0,)), ((), ())), preferred_element_type=F32)

    return pl.pallas_call(
        body, name=name, grid=(M // bm, N // bn, nk),
        in_specs=[pl.BlockSpec((bk, bm), lambda i, j, k: (k, i)), pl.BlockSpec((bk, bn), lambda i, j, k: (k, j))],
        out_specs=pl.BlockSpec((bm, bn), lambda i, j, k: (i, j)),
        out_shape=jax.ShapeDtypeStruct((M, N), F32),
        compiler_params=_cparams(),
    )(a, b)


def _adam_parts(parts, w, m, v, name):
    rows, cols = w.shape
    br = rows
    for cand in range(rows, 15, -16):
        if rows % cand == 0 and cand * cols * 4 <= ADAM_BLOCK_BYTES:
            br = cand
            break

    def body(p_ref, w_ref, m_ref, v_ref, g_out, dl_out, m_out, v_out):
        g = p_ref[0].astype(F32)
        for k in range(1, N_DEV):
            g = g + p_ref[k].astype(F32)
        g_out[...] = g
        dl, mn, vn = _adam_math(w_ref[...], g, m_ref[...], v_ref[...])
        dl_out[...] = dl
        m_out[...] = mn
        v_out[...] = vn

    wspec = pl.BlockSpec((br, cols), lambda i: (i, 0))
    return pl.pallas_call(
        body, name=name, grid=(rows // br,),
        in_specs=[pl.BlockSpec((N_DEV, br, cols), lambda i: (0, i, 0)), wspec, wspec, wspec],
        out_specs=[wspec] * 4, out_shape=[jax.ShapeDtypeStruct((rows, cols), F32)] * 4,
        compiler_params=_cparams(),
    )(parts, w, m, v)


def _adam_plain(g, w, m, v, name):
    def body(g_ref, w_ref, m_ref, v_ref, dl_out, m_out, v_out):
        dl, mn, vn = _adam_math(w_ref[...], g_ref[...], m_ref[...], v_ref[...])
        dl_out[...] = dl
        m_out[...] = mn
        v_out[...] = vn

    return pl.pallas_call(body, name=name, out_shape=[jax.ShapeDtypeStruct(w.shape, F32)] * 3,
                          compiler_params=_cparams())(g, w, m, v)


_SMALL = [("b_ada", 48), ("_dmod1", 48), ("g_mix", 8), ("conv_w", 48), ("conv_b", 12), ("heads", 8), ("g_ssd", 8),
          ("w_pool", 512), ("pool_scale", 4), ("g_mlp", 8), ("g_final", 8), ("loss", 8)]
_SMALL_OFF = {}
_off = 0
for _n, _r in _SMALL:
    _SMALL_OFF[_n] = (_off, _r)
    _off += _r
SMALL_ROWS = _off


def _small_sum_adam(gathered, w, m, v):
    nd = _SMALL_OFF["_dmod1"][0]

    def body(p_ref, w_ref, m_ref, v_ref, g_out, dl_out, m_out, v_out):
        tot = p_ref[0]
        for k in range(1, N_DEV):
            tot = tot + p_ref[k]
        g = jnp.concatenate([tot[0:nd] + tot[nd:2 * nd], tot[nd:]], axis=0)
        g_out[...] = g
        dl, mn, vn = _adam_math(w_ref[...], g, m_ref[...], v_ref[...])
        dl_out[...] = dl
        m_out[...] = mn
        v_out[...] = vn

    return pl.pallas_call(body, name="small_sum_adam",
                          out_shape=[jax.ShapeDtypeStruct((SMALL_ROWS, LANES), F32)] * 4,
                          compiler_params=_cparams())(gathered, w, m, v)


def _rows(a, n):
    flat = a.reshape(-1)
    return jnp.pad(flat, (0, n * LANES - flat.shape[0])).reshape(n, LANES)


def _pack_small(named):
    parts = []
    for name, r in _SMALL:
        parts.append(_rows(named[name], r) if name in named else jnp.zeros((r, LANES), F32))
    return jnp.concatenate(parts, axis=0)


def _unpack(pack, name, shape):
    off, r = _SMALL_OFF[name]
    size = 1
    for s in shape:
        size *= s
    return pack[off:off + r].reshape(-1)[:size].reshape(shape)


def kernel(x, c, w_ada, b_ada, g_mix, w_in, conv_w, conv_b, dt_bias, a_log, d_skip, g_ssd, w_pool, pool_scale, w_out, g_mlp, w_up, w_down, g_final, loss_target, m_w_ada, m_b_ada, m_g_mix, m_w_in, m_conv_w, m_conv_b, m_dt_bias, m_a_log, m_d_skip, m_g_ssd, m_w_pool, m_pool_scale, m_w_out, m_g_mlp, m_w_up, m_w_down, m_g_final, v_w_ada, v_b_ada, v_g_mix, v_w_in, v_conv_w, v_conv_b, v_dt_bias, v_a_log, v_d_skip, v_g_ssd, v_w_pool, v_pool_scale, v_w_out, v_g_mlp, v_w_up, v_w_down, v_g_final):
    nb, seq, _ = x.shape
    T = nb * seq
    me = 4 * lax.axis_index("x") + 2 * lax.axis_index("y") + lax.axis_index("c")
    in_cols = w_in.shape[2]
    ada_cols = w_ada.shape[2]
    cw_cols = conv_w.shape[2]
    in_rows = D * in_cols // LANES

    small_f = jnp.concatenate([_rows(c, 16), _rows(conv_w[0], 8)], axis=0)
    (small_g,) = _all_gather([small_f], "ag_small_fwd")
    c_all = small_g[:, :16].reshape(N_DEV * nb, D)
    cw_full = small_g[:, 16:24].reshape(N_DEV, -1)[:, :4 * cw_cols].reshape(N_DEV, 4, cw_cols)
    cw_full = cw_full.transpose(1, 0, 2).reshape(4, CONV_CH)

    win_s = w_in[0].astype(BF16).reshape(in_rows, LANES)
    wout_s = w_out[0].astype(BF16)
    wup_s = w_up[0].astype(BF16)
    wdn_s = w_down[0].astype(BF16)
    win_g, wout_g, wup_g, wdn_g = _all_gather([win_s, wout_s, wup_s, wdn_s], "ag_weights")
    win_full = win_g.reshape(N_DEV, D, in_cols).transpose(1, 0, 2).reshape(D, IN_W)
    w_main = win_full[:, :OFF_DT]
    w_dt = jnp.pad(win_full[:, OFF_DT:], ((0, 0), (0, LANES - (IN_W - OFF_DT))))
    wout_full = wout_g.reshape(MIX_W, D)

    b_slice = lax.dynamic_slice(b_ada, (0, me * ada_cols), (1, ada_cols))
    mod_cols = _ada_fwd(c_all, w_ada[0], b_slice)
    (mod_g,) = _all_gather([mod_cols], "ag_mod")
    mod_all = mod_g.transpose(1, 0, 2).reshape(N_DEV * nb, 6, D)
    mod_mine = lax.dynamic_slice(mod_all, (me * nb, 0, 0), (nb, 6, D))
    mod = jnp.pad(mod_mine, ((0, 0), (0, 2), (0, 0)))

    x2 = x.reshape(T, D)
    tg2 = loss_target.reshape(T, D)
    heads = jnp.pad(jnp.concatenate([dt_bias, a_log, d_skip], axis=0), ((0, 5), (0, LANES - N_HEADS)))
    wpool_b = w_pool[0]
    u_b, pm, dtp = _mix_in(x2, mod, g_mix, w_main, w_dt, seq)
    ymix, hstates = _mixer_fwd(pm, dtp, cw_full, conv_b, heads, g_ssd, wpool_b, pool_scale, nb, seq)
    da_b, dym, dh1, u2_b, f_b, dup_b, ddn_b, dmod_a, acc_a = _mlp_fused(
        x2, ymix, tg2, mod, g_mlp, g_final.reshape(1, D), wout_full, wup_g, wdn_g, seq)

    dpm, ddt, d_cw, d_cb, d_heads, d_gssd, d_wpool, d_pscale = _mixer_bwd(
        pm, dtp, dym, hstates, cw_full, conv_b, heads, g_ssd, wpool_b, pool_scale, nb, seq)
    grad_x2, dproj_b, dmod_b, acc_b = _in_bwd(x2, dh1, dpm, ddt, mod, g_mix, w_main, w_dt, seq)

    gw_out = _dw(ymix, da_b, "dw_out", MIX_W // 2, D)
    gw_up = _dw(u2_b, dup_b, "dw_up", D, 1024)
    gw_dn = _dw(f_b, ddn_b, "dw_down", 1024, D)
    gw_in = _dw(u_b, dproj_b, "dw_in", D // 2, 640)

    gin_p = gw_in[:, :IN_W].reshape(D, N_DEV, in_cols).transpose(1, 0, 2).reshape(N_DEV, in_rows, LANES).astype(BF16)
    gout_p = gw_out.reshape(N_DEV, MIX_W // N_DEV, D).astype(BF16)
    gup_p = gw_up.reshape(D, N_DEV, D_FF // N_DEV).transpose(1, 0, 2).astype(BF16)
    gdn_p = gw_dn.reshape(N_DEV, D_FF // N_DEV, D).astype(BF16)
    gin_r, gout_r, gup_r, gdn_r = _all_to_all([gin_p, gout_p, gup_p, gdn_p], "a2a_grads")

    def flat_in(a):
        return a[0].reshape(in_rows, LANES)

    g_in, d_in, nm_in, nv_in = _adam_parts(gin_r, flat_in(w_in), flat_in(m_w_in), flat_in(v_w_in), "adam_w_in")
    g_out, d_out, nm_out, nv_out = _adam_parts(gout_r, w_out[0], m_w_out[0], v_w_out[0], "adam_w_out")
    g_up, d_up, nm_up, nv_up = _adam_parts(gup_r, w_up[0], m_w_up[0], v_w_up[0], "adam_w_up")
    g_dn, d_dn, nm_dn, nv_dn = _adam_parts(gdn_r, w_down[0], m_w_down[0], v_w_down[0], "adam_w_down")

    dmod = jnp.concatenate([dmod_b[:, 0:2], dmod_a[:, 0:4]], axis=1)
    small_named = {
        "b_ada": dmod[0], "_dmod1": dmod[1], "g_mix": acc_b[0], "conv_w": d_cw, "conv_b": d_cb, "heads": d_heads,
        "g_ssd": d_gssd, "w_pool": d_wpool, "pool_scale": d_pscale, "g_mlp": acc_a[0], "g_final": acc_a[1],
        "loss": acc_a[2, 0:1],
    }
    (small_all,) = _all_gather([_pack_small(small_named)], "ag_small_bwd")
    zeros16 = jnp.zeros((1, N_HEADS), F32)

    def pack_params(b_, gmix_, cb_, dtb_, al_, ds_, gs_, wp_, ps_, gm_, gf_):
        hd = jnp.pad(jnp.concatenate([dtb_, al_, ds_], axis=0), ((0, 5), (0, LANES - N_HEADS)))
        return _pack_small({"b_ada": b_, "g_mix": gmix_, "conv_b": cb_, "heads": hd, "g_ssd": gs_, "w_pool": wp_,
                            "pool_scale": ps_, "g_mlp": gm_, "g_final": gf_})

    w_pack = pack_params(b_ada, g_mix, conv_b, dt_bias, a_log, d_skip, g_ssd, w_pool, pool_scale, g_mlp, g_final)
    m_pack = pack_params(m_b_ada, m_g_mix, m_conv_b, m_dt_bias, m_a_log, m_d_skip, m_g_ssd, m_w_pool, m_pool_scale,
                         m_g_mlp, m_g_final)
    v_pack = pack_params(v_b_ada, v_g_mix, v_conv_b, v_dt_bias, v_a_log, v_d_skip, v_g_ssd, v_w_pool, v_pool_scale,
                         v_g_mlp, v_g_final)
    g_pack, d_pack, nm_pack, nv_pack = _small_sum_adam(small_all, w_pack, m_pack, v_pack)
    loss = _unpack(g_pack, "loss", (1,))[0]

    g_cw_full = _unpack(g_pack, "conv_w", (4, CONV_CH))
    g_cw = lax.dynamic_slice(g_cw_full, (0, me * cw_cols), (4, cw_cols))
    d_cwp, nm_cwp, nv_cwp = _adam_plain(g_cw, conv_w[0], m_conv_w[0], v_conv_w[0], "adam_conv_w")

    dmod_all = small_all[:, 0:96].reshape(N_DEV * nb, 6 * D)
    dmod_slice = lax.dynamic_slice(dmod_all, (0, me * ada_cols), (N_DEV * nb, ada_cols))
    g_ada, d_ada, nm_ada, nv_ada = _ada_bwd_adam(c_all, dmod_slice, w_ada[0], m_w_ada[0], v_w_ada[0])

    def small_outs(pack):
        hd = _unpack(pack, "heads", (8, LANES))
        return {
            "b_ada": _unpack(pack, "b_ada", (1, 6 * D)), "g_mix": _unpack(pack, "g_mix", (1, D)),
            "conv_b": _unpack(pack, "conv_b", (1, CONV_CH)), "dt_bias": hd[0:1, :N_HEADS], "a_log": hd[1:2, :N_HEADS],
            "d_skip": hd[2:3, :N_HEADS], "g_ssd": _unpack(pack, "g_ssd", (1, D)),
            "w_pool": _unpack(pack, "w_pool", (1, 4, LANES, LANES)), "pool_scale": _unpack(pack, "pool_scale", (1, POOL_W)),
            "g_mlp": _unpack(pack, "g_mlp", (1, D)), "g_final": _unpack(pack, "g_final", (D,)),
        }

    def big_outs(ada, win, cwp, wout, wup, wdn):
        return {"w_ada": ada[None], "w_in": win.reshape(1, D, in_cols), "conv_w": cwp[None], "w_out": wout[None],
                "w_up": wup[None], "w_down": wdn[None]}

    order = ["w_ada", "b_ada", "g_mix", "w_in", "conv_w", "conv_b", "dt_bias", "a_log", "d_skip", "g_ssd", "w_pool",
             "pool_scale", "w_out", "g_mlp", "w_up", "w_down", "g_final"]
    groups = [
        {**small_outs(g_pack), **big_outs(g_ada, g_in, g_cw, g_out, g_up, g_dn)},
        {**small_outs(d_pack), **big_outs(d_ada, d_in, d_cwp, d_out, d_up, d_dn)},
        {**small_outs(nm_pack), **big_outs(nm_ada, nm_in, nm_cwp, nm_out, nm_up, nm_dn)},
        {**small_outs(nv_pack), **big_outs(nv_ada, nv_in, nv_cwp, nv_out, nv_up, nv_dn)},
    ]
    outs = [loss, grad_x2.reshape(nb, seq, D)]
    for grp in groups:
        outs += [grp[n] for n in order]
    return tuple(outs)
```

```python
import functools

import jax
import jax.numpy as jnp
from jax import lax
from jax.experimental import pallas as pl
from jax.experimental.pallas import tpu as pltpu

F32, BF16 = jnp.float32, jnp.bfloat16
MESH = pl.DeviceIdType.MESH
N_DEV = 8
D = 1024
LANES = 128
CHUNK = 128
POOL_W = 512
WINDOWS = (2, 4, 8, 16)
N_HEADS = 16
HEAD_DIM = 64
N_GROUPS = 2
GROUP_W = 512
N_STATE = 128
CONV_CH = 1536
OFF_Z, OFF_XBC, OFF_DT, IN_W = 512, 1536, 3072, 3088
PROJ_W = OFF_DT + LANES
MIX_W = 1536
D_FF = 4096
FF_BLK = 512
EPS = 1e-5
LR, B1, B2, AEPS, WD, STEP = 0.001, 0.9, 0.999, 1e-08, 0.01, 10
POOL_HALO = 16
CONV_HALO = 8
VMEM_LIMIT = 56 << 20
ADAM_BLOCK_BYTES = 1600 << 10


def _cparams(**kw):
    return pltpu.CompilerParams(vmem_limit_bytes=VMEM_LIMIT, **kw)


def _mm(a, b):
    return jnp.dot(a.astype(BF16), b.astype(BF16), preferred_element_type=F32)


def _mm_nt(a, b):
    return lax.dot_general(a.astype(BF16), b.astype(BF16), (((1,), (1,)), ((), ())), preferred_element_type=F32)


def _mm_tn(a, b):
    return lax.dot_general(a.astype(BF16), b.astype(BF16), (((0,), (0,)), ((), ())), preferred_element_type=F32)


def _mm_exact(a, b):
    return jnp.dot(a, b, preferred_element_type=F32, precision=lax.Precision.HIGHEST)


def _sigmoid(v):
    return 1.0 / (1.0 + jnp.exp(-v))


def _expand_mat():
    r = lax.broadcasted_iota(jnp.int32, (LANES, D), 0)
    c = lax.broadcasted_iota(jnp.int32, (LANES, D), 1)
    return (r == c // HEAD_DIM).astype(F32)


def _reduce_mat():
    r = lax.broadcasted_iota(jnp.int32, (D, LANES), 0)
    c = lax.broadcasted_iota(jnp.int32, (D, LANES), 1)
    return (c == r // HEAD_DIM).astype(F32)


def _pos():
    return lax.axis_index("x"), lax.axis_index("y"), lax.axis_index("c")


class _Gather:
    def __init__(self, x_refs, o_refs, send, recv, loc):
        self.x_refs, self.o_refs, self.send, self.recv, self.loc = x_refs, o_refs, send, recv, loc
        self.n = len(x_refs)
        x, y, c = _pos()
        self.c = c
        self.me, self.sib = (x, y, c), (x, y, 1 - c)
        self.chips = [(1 - x, y), (x, 1 - y), (1 - x, 1 - y)]

    def _cp(self, a, k, block, to, src=None):
        dst = self.o_refs[a].at[4 * block[0] + 2 * block[1] + block[2]]
        return pltpu.make_async_remote_copy(
            src_ref=dst if src is None else src, dst_ref=dst,
            send_sem=self.send.at[a * 7 + k], recv_sem=self.recv.at[a * 7 + k],
            device_id=to, device_id_type=MESH)

    def _mine(self, a):
        me = self.me
        return pltpu.make_async_copy(self.x_refs[a], self.o_refs[a].at[4 * me[0] + 2 * me[1] + me[2]], self.loc.at[a])

    def _first(self, a):
        cps = [self._cp(a, 0, self.me, self.sib, src=self.x_refs[a])]
        return cps + [self._cp(a, 1 + j, self.me, (*chip, self.c), src=self.x_refs[a])
                      for j, chip in enumerate(self.chips)]

    def _passed(self, a, j):
        return self._cp(a, 4 + j, (*self.chips[j], self.c), self.sib)

    def start(self):
        for a in range(self.n):
            self._mine(a).start()
            for cp in self._first(a):
                cp.start()

    def forward(self):
        for j, chip in enumerate(self.chips):
            for a in range(self.n):
                self._cp(a, 1 + j, (*chip, self.c), self.me).wait_recv()
                self._passed(a, j).start()

    def finish(self):
        for a in range(self.n):
            self._cp(a, 0, self.sib, self.me).wait_recv()
            for j, chip in enumerate(self.chips):
                self._cp(a, 4 + j, (*chip, 1 - self.c), self.me).wait_recv()
        for a in range(self.n):
            for cp in self._first(a):
                cp.wait_send()
            for j in range(3):
                self._passed(a, j).wait_send()
            self._mine(a).wait()


class _Exchange:
    def __init__(self, x_refs, o_refs, send, recv, loc):
        self.x_refs, self.o_refs, self.send, self.recv, self.loc = x_refs, o_refs, send, recv, loc
        self.n = len(x_refs)
        x, y, c = _pos()
        self.me_i = 4 * x + 2 * y + c
        self.peers = []
        for k in range(1, N_DEV):
            px = 1 - x if (k >> 2) & 1 else x
            py = 1 - y if (k >> 1) & 1 else y
            pc = 1 - c if k & 1 else c
            self.peers.append(((px, py, pc), 4 * px + 2 * py + pc))

    def _mine(self, a):
        return pltpu.make_async_copy(self.x_refs[a].at[self.me_i], self.o_refs[a].at[self.me_i], self.loc.at[a])

    def _cp(self, a, k, landing):
        peer, peer_i = self.peers[k]
        return pltpu.make_async_remote_copy(
            src_ref=self.x_refs[a].at[peer_i], dst_ref=self.o_refs[a].at[landing],
            send_sem=self.send.at[a * 7 + k], recv_sem=self.recv.at[a * 7 + k],
            device_id=peer, device_id_type=MESH)

    def start(self):
        for a in range(self.n):
            self._mine(a).start()
            for k in range(N_DEV - 1):
                self._cp(a, k, self.me_i).start()

    def finish(self):
        for a in range(self.n):
            for k in range(N_DEV - 1):
                self._cp(a, k, self.peers[k][1]).wait_recv()
        for a in range(self.n):
            for k in range(N_DEV - 1):
                self._cp(a, k, self.me_i).wait_send()
            self._mine(a).wait()


def _comm_scratch(n):
    return [pltpu.SemaphoreType.DMA((7 * n,)), pltpu.SemaphoreType.DMA((7 * n,)), pltpu.SemaphoreType.DMA((n,))]


ANY_SPEC = pl.BlockSpec(memory_space=pl.ANY)


def _all_gather(xs, name):
    n = len(xs)

    def body(*refs):
        g = _Gather(refs[:n], refs[n:2 * n], *refs[2 * n:])
        g.start()
        g.forward()
        g.finish()

    return pl.pallas_call(
        body, name=name,
        out_shape=[jax.ShapeDtypeStruct((N_DEV,) + v.shape, v.dtype) for v in xs],
        in_specs=[ANY_SPEC] * n, out_specs=[ANY_SPEC] * n, scratch_shapes=_comm_scratch(n),
    )(*xs)


def _ada_fwd(c_all, w_ada, b_slice):
    def body(c_ref, w_ref, b_ref, o_ref):
        cv = c_ref[...]
        act = cv * _sigmoid(cv)
        o_ref[...] = _mm(act, w_ref[...]) + b_ref[...]

    nb, nc = c_all.shape[0], w_ada.shape[1]
    return pl.pallas_call(body, name="ada_fwd", out_shape=jax.ShapeDtypeStruct((nb, nc), F32),
                          compiler_params=_cparams())(c_all, w_ada, b_slice)


def _adam_math(w, g, m, v):
    m = B1 * m + (1.0 - B1) * g
    v = B2 * v + (1.0 - B2) * jnp.square(g)
    m_hat = m / (1.0 - B1 ** STEP)
    v_hat = v / (1.0 - B2 ** STEP)
    delta = -LR * (m_hat / (jnp.sqrt(v_hat) + AEPS) + WD * w)
    return delta, m, v


def _ada_bwd_adam(c_all, dmod_slice, w, m, v):
    rows, cols = w.shape
    br = 256

    def body(c_ref, d_ref, w_ref, m_ref, v_ref, g_out, dl_out, m_out, v_out):
        cv = c_ref[...]
        act = cv * _sigmoid(cv)
        g = _mm_tn(act, d_ref[...])
        g_out[...] = g
        dl, mn, vn = _adam_math(w_ref[...], g, m_ref[...], v_ref[...])
        dl_out[...] = dl
        m_out[...] = mn
        v_out[...] = vn

    nb = c_all.shape[0]
    wspec = pl.BlockSpec((br, cols), lambda i: (i, 0))
    return pl.pallas_call(
        body, name="ada_bwd_adam", grid=(rows // br,),
        in_specs=[pl.BlockSpec((nb, br), lambda i: (0, i)), pl.BlockSpec((nb, cols), lambda i: (0, 0)),
                  wspec, wspec, wspec],
        out_specs=[wspec] * 4, out_shape=[jax.ShapeDtypeStruct((rows, cols), F32)] * 4,
        compiler_params=_cparams(),
    )(c_all, dmod_slice, w, m, v)


def _mix_in(x2, mod, g_mix, w_cat, seq):
    T = x2.shape[0]
    tm = min(512, seq)
    tps = seq // tm

    def body(x_ref, mod_ref, g_ref, w_ref, u_ref, pm_ref):
        x = x_ref[...]
        r = lax.rsqrt(jnp.mean(x * x, axis=-1, keepdims=True) + EPS)
        md = mod_ref[0]
        u = (x * r * g_ref[...]) * (1.0 + md[1:2]) + md[0:1]
        ub = u.astype(BF16)
        u_ref[...] = ub
        pm_ref[...] = jnp.dot(ub, w_ref[...], preferred_element_type=F32)

    whole = pl.BlockSpec(memory_space=pltpu.VMEM)
    return pl.pallas_call(
        body, name="mix_in", grid=(T // tm,),
        in_specs=[pl.BlockSpec((tm, D), lambda i: (i, 0)), pl.BlockSpec((1, 8, D), lambda i: (i // tps, 0, 0)),
                  pl.BlockSpec((1, D), lambda i: (0, 0)), whole],
        out_specs=[pl.BlockSpec((tm, D), lambda i: (i, 0)), pl.BlockSpec((tm, PROJ_W), lambda i: (i, 0))],
        out_shape=[jax.ShapeDtypeStruct((T, D), BF16), jax.ShapeDtypeStruct((T, PROJ_W), F32)],
        compiler_params=_cparams(),
    )(x2, mod, g_mix, w_cat)


def _chunk_forward(up, z, ux, dtin, halo_p, halo_x, hprev, cw, cb, hp, gssd, wpool, pscale, t0, y_scr):
    L = CHUNK
    out = {}
    row = lax.broadcasted_iota(jnp.int32, (L, 1), 0)
    t = (t0 + row + 1).astype(F32)
    e = jnp.concatenate([halo_p, up], axis=0)
    s2 = e + pltpu.roll(e, 1, 0)
    s4 = s2 + pltpu.roll(s2, 2, 0)
    s8 = s4 + pltpu.roll(s4, 4, 0)
    s16 = s8 + pltpu.roll(s8, 8, 0)
    sums = (s2, s4, s8, s16)
    p, inv, yp = [], [], []
    for gi, w in enumerate(WINDOWS):
        sl = slice(gi * LANES, (gi + 1) * LANES)
        ic = 1.0 / jnp.minimum(t, float(w))
        pg = sums[gi][POOL_HALO:, sl] * ic - up[:, sl]
        p.append(pg)
        inv.append(ic)
        yp.append(_mm(pg, wpool[gi]))
    out["p"], out["inv"], out["yp"] = p, inv, yp
    out["y_pool"] = jnp.concatenate(yp, axis=1) * pscale
    ex = jnp.concatenate([halo_x, ux], axis=0)
    taps = [pltpu.roll(ex, 3, 0)[CONV_HALO:], pltpu.roll(ex, 2, 0)[CONV_HALO:], pltpu.roll(ex, 1, 0)[CONV_HALO:], ux]
    cv = cb + taps[0] * cw[0:1] + taps[1] * cw[1:2] + taps[2] * cw[2:3] + taps[3] * cw[3:4]
    sg = _sigmoid(cv)
    xbc = cv * sg
    out["taps"], out["cv"], out["sg"] = taps, cv, sg
    X = xbc[:, :D]
    Bm = xbc[:, D:D + N_GROUPS * N_STATE]
    Cm = xbc[:, D + N_GROUPS * N_STATE:]
    pre = dtin + hp[0:1]
    dt = jnp.maximum(pre, 0.0) + jnp.log(1.0 + jnp.exp(-jnp.abs(pre)))
    a_row = -jnp.exp(hp[1:2])
    da = dt * a_row
    ri = lax.broadcasted_iota(jnp.int32, (L, L), 0)
    ci = lax.broadcasted_iota(jnp.int32, (L, L), 1)
    causal = ri >= ci
    cum = _mm_exact(causal.astype(F32), da)
    cum_t = cum.T
    cum_last = cum[L - 1:L]
    eo = jnp.exp(cum)
    dec = jnp.exp(cum_last - cum)
    cd = jnp.exp(cum_last)
    exm = _expand_mat()
    dt_rep = _mm_exact(dt, exm)
    eo_rep = _mm_exact(eo, exm)
    dec_rep = _mm_exact(dec, exm)
    rows8 = jnp.concatenate([cd, hp[2:3], jnp.zeros((6, LANES), F32)], axis=0)
    rows8_rep = _mm_exact(rows8, exm)
    cd_rep, dskip_rep = rows8_rep[0:1], rows8_rep[1:2]
    xdt = X * dt_rep
    out.update(X=X, Bm=Bm, Cm=Cm, pre=pre, dt=dt, a_row=a_row, cum=cum, cum_t=cum_t, eo=eo, dec=dec, cd=cd,
               dt_rep=dt_rep, eo_rep=eo_rep, dec_rep=dec_rep, cd_rep=cd_rep, dskip_rep=dskip_rep, xdt=xdt,
               causal=causal, exm=exm)
    G, lms, yoff, hnew, xdec = [], [], [], [], []
    for g in range(N_GROUPS):
        gs = slice(g * GROUP_W, (g + 1) * GROUP_W)
        Bg = Bm[:, g * N_STATE:(g + 1) * N_STATE]
        Cg = Cm[:, g * N_STATE:(g + 1) * N_STATE]
        Gg = _mm_nt(Cg, Bg)
        G.append(Gg)
        for hh in range(N_HEADS // N_GROUPS):
            h = g * (N_HEADS // N_GROUPS) + hh
            seg = cum[:, h:h + 1] - cum_t[h:h + 1, :]
            lm = jnp.where(causal, jnp.exp(jnp.minimum(seg, 0.0)), 0.0)
            lms.append(lm)
            hs = slice(h * HEAD_DIM, (h + 1) * HEAD_DIM)
            y_scr[:, hs] = _mm(Gg * lm, xdt[:, hs])
        xd = xdt[:, gs] * dec_rep[:, gs]
        xdec.append(xd)
        sgm = _mm_tn(Bg, xd)
        yoff.append(_mm(Cg, hprev[g]) * eo_rep[:, gs])
        hnew.append(hprev[g] * cd_rep[:, gs] + sgm)
    out.update(G=G, lms=lms, yoff=yoff, hnew=hnew, xdec=xdec)
    y = y_scr[...] + jnp.concatenate(yoff, axis=1) + dskip_rep * X
    sz = _sigmoid(z)
    silz = z * sz
    yz = y * silz
    rg, yn = [], []
    for g in range(N_GROUPS):
        gs = slice(g * GROUP_W, (g + 1) * GROUP_W)
        r = lax.rsqrt(jnp.mean(yz[:, gs] * yz[:, gs], axis=-1, keepdims=True) + EPS)
        rg.append(r)
        yn.append(yz[:, gs] * r)
    yn = jnp.concatenate(yn, axis=1)
    out.update(y=y, sz=sz, silz=silz, rg=rg, yn=yn)
    out["y_ssd"] = yn * gssd
    return out


def _mixer_fwd(pm, cw, cb, hp, gssd, wpool, pscale, nb, seq, shards):
    nc = seq // CHUNK
    ns = len(shards)
    steps = nb * nc
    fwd_step = (3 * steps) // 4

    def body(*refs):
        pm_ref, cw_ref, cb_ref, hp_ref, gs_ref, wp_ref, ps_ref = refs[:7]
        sh_refs = refs[7:7 + ns]
        ym_ref, hs_ref = refs[7 + ns:9 + ns]
        ga_refs = refs[9 + ns:9 + 2 * ns]
        halo_p, halo_x, state, y_scr, send, recv, loc = refs[9 + 2 * ns:]
        c = pl.program_id(1)
        step = pl.program_id(0) * nc + c
        gather = _Gather(sh_refs, ga_refs, send, recv, loc)

        @pl.when(step == 0)
        def _():
            gather.start()

        @pl.when(step == fwd_step)
        def _():
            gather.forward()

        @pl.when(c == 0)
        def _():
            halo_p[...] = jnp.zeros_like(halo_p)
            halo_x[...] = jnp.zeros_like(halo_x)
            state[...] = jnp.zeros_like(state)

        up = pm_ref[:, 0:POOL_W]
        z = pm_ref[:, OFF_Z:OFF_XBC]
        ux = pm_ref[:, OFF_XBC:OFF_DT]
        hprev = [state[0], state[1]]
        hs_ref[0, 0, 0] = hprev[0]
        hs_ref[0, 0, 1] = hprev[1]
        o = _chunk_forward(up, z, ux, pm_ref[:, OFF_DT:], halo_p[...], halo_x[...], hprev, cw_ref[...], cb_ref[...],
                           hp_ref[...], gs_ref[...], wp_ref[...], ps_ref[...], c * CHUNK, y_scr)
        ym_ref[:, 0:POOL_W] = o["y_pool"].astype(BF16)
        ym_ref[:, POOL_W:] = o["y_ssd"].astype(BF16)
        state[0] = o["hnew"][0]
        state[1] = o["hnew"][1]
        halo_p[...] = up[CHUNK - POOL_HALO:]
        halo_x[...] = ux[CHUNK - CONV_HALO:]

        @pl.when(step == steps - 1)
        def _():
            gather.finish()

    def full(shape):
        return pl.BlockSpec(shape, lambda b, c: (0,) * len(shape))

    T = nb * seq
    return pl.pallas_call(
        body, name="mixer_fwd", grid=(nb, nc),
        in_specs=[pl.BlockSpec((CHUNK, PROJ_W), lambda b, c: (b * nc + c, 0)),
                  full((4, CONV_CH)), full((1, CONV_CH)), full((8, LANES)), full((1, D)),
                  full((4, LANES, LANES)), full((1, POOL_W))] + [ANY_SPEC] * ns,
        out_specs=[pl.BlockSpec((CHUNK, MIX_W), lambda b, c: (b * nc + c, 0)),
                   pl.BlockSpec((1, 1, N_GROUPS, N_STATE, GROUP_W), lambda b, c: (b, c, 0, 0, 0))] + [ANY_SPEC] * ns,
        out_shape=[jax.ShapeDtypeStruct((T, MIX_W), BF16),
                   jax.ShapeDtypeStruct((nb, nc, N_GROUPS, N_STATE, GROUP_W), F32)]
        + [jax.ShapeDtypeStruct((N_DEV,) + v.shape, v.dtype) for v in shards],
        scratch_shapes=[pltpu.VMEM((POOL_HALO, POOL_W), F32), pltpu.VMEM((CONV_HALO, CONV_CH), F32),
                        pltpu.VMEM((N_GROUPS, N_STATE, GROUP_W), F32), pltpu.VMEM((CHUNK, D), F32)] + _comm_scratch(ns),
        compiler_params=_cparams(),
    )(pm, cw, cb, hp, gssd, wpool, pscale, *shards)


def _mixer_bwd(pm, dym, hstates, cw, cb, hp, gssd, wpool, pscale, nb, seq, blocks):
    nc = seq // CHUNK
    hpg = N_HEADS // N_GROUPS
    ns = len(blocks)
    steps = nb * nc

    def body(*refs):
        (pm_ref, hpool_ref, hxbc_ref, dy_ref, hs_ref, cw_ref, cb_ref, hp_ref, gs_ref, wp_ref, ps_ref) = refs[:11]
        bl_refs = refs[11:11 + ns]
        dpm_ref, dcw_ref, dcb_ref, dhp_ref, dgs_ref, dwp_ref, dps_ref = refs[11 + ns:18 + ns]
        ex_refs = refs[18 + ns:18 + 2 * ns]
        nxt_q, nxt_cv, rstate, y_scr, dx_scr, send, recv, loc = refs[18 + 2 * ns:]
        b = pl.program_id(0)
        ci = pl.program_id(1)
        c = nc - 1 - ci
        exchange = _Exchange(bl_refs, ex_refs, send, recv, loc)

        @pl.when((b == 0) & (ci == 0))
        def _():
            exchange.start()

        @pl.when((b == 0) & (ci == 0))
        def _():
            for r in (dcw_ref, dcb_ref, dhp_ref, dgs_ref, dwp_ref, dps_ref):
                r[...] = jnp.zeros_like(r)

        @pl.when(ci == 0)
        def _():
            nxt_q[...] = jnp.zeros_like(nxt_q)
            nxt_cv[...] = jnp.zeros_like(nxt_cv)
            rstate[...] = jnp.zeros_like(rstate)

        first = (c > 0).astype(F32)
        up = pm_ref[:, 0:POOL_W]
        z = pm_ref[:, OFF_Z:OFF_XBC]
        ux = pm_ref[:, OFF_XBC:OFF_DT]
        halo_p = hpool_ref[...] * first
        halo_x = hxbc_ref[...] * first
        hprev = [hs_ref[0, 0, 0], hs_ref[0, 0, 1]]
        cw, cb, hp, gssd, wpool, pscale = cw_ref[...], cb_ref[...], hp_ref[...], gs_ref[...], wp_ref[...], ps_ref[...]
        o = _chunk_forward(up, z, ux, pm_ref[:, OFF_DT:], halo_p, halo_x, hprev, cw, cb, hp, gssd, wpool, pscale,
                           c * CHUNK, y_scr)
        L = CHUNK
        dy_pool = dy_ref[:, 0:POOL_W].astype(F32)
        dy_ssd = dy_ref[:, POOL_W:].astype(F32)

        dps_ref[...] += jnp.sum(dy_pool * jnp.concatenate(o["yp"], axis=1), axis=0, keepdims=True)
        dyp = dy_pool * pscale
        qs = []
        dps = []
        for gi in range(len(WINDOWS)):
            sl = slice(gi * LANES, (gi + 1) * LANES)
            dwp_ref[gi] += _mm_tn(o["p"][gi], dyp[:, sl])
            dpg = _mm_nt(dyp[:, sl], wpool[gi])
            dps.append(dpg)
            qs.append(dpg * o["inv"][gi])
        q = jnp.concatenate(qs, axis=1)
        e = jnp.concatenate([q, nxt_q[...]], axis=0)
        n = L + POOL_HALO
        s2 = e + pltpu.roll(e, n - 1, 0)
        s4 = s2 + pltpu.roll(s2, n - 2, 0)
        s8 = s4 + pltpu.roll(s4, n - 4, 0)
        s16 = s8 + pltpu.roll(s8, n - 8, 0)
        sums = (s2, s4, s8, s16)
        for gi in range(len(WINDOWS)):
            sl = slice(gi * LANES, (gi + 1) * LANES)
            dpm_ref[:, sl] = (sums[gi][:L, sl] - dps[gi]).astype(BF16)
        nxt_q[...] = q[:POOL_HALO]

        yn, y, silz, sz = o["yn"], o["y"], o["silz"], o["sz"]
        dgs_ref[...] += jnp.sum(dy_ssd * yn, axis=0, keepdims=True)
        dyn = dy_ssd * gssd
        dyz = []
        for g in range(N_GROUPS):
            gs = slice(g * GROUP_W, (g + 1) * GROUP_W)
            mean = jnp.mean(dyn[:, gs] * yn[:, gs], axis=-1, keepdims=True)
            dyz.append(o["rg"][g] * (dyn[:, gs] - yn[:, gs] * mean))
        dyz = jnp.concatenate(dyz, axis=1)
        dyv = dyz * silz
        dpm_ref[:, OFF_Z:OFF_XBC] = (dyz * y * (sz * (1.0 + z * (1.0 - sz)))).astype(BF16)

        X, Bm, Cm, xdt = o["X"], o["Bm"], o["Cm"], o["xdt"]
        exm = o["exm"]
        rdm = _reduce_mat()
        lane = lax.broadcasted_iota(jnp.int32, (1, LANES), 1)
        sub = lax.broadcasted_iota(jnp.int32, (LANES, 1), 0)
        d_dskip = _mm_exact(jnp.sum(dyv * X, axis=0, keepdims=True) * jnp.ones((8, 1), F32), rdm)[0:1]
        dX = o["dskip_rep"] * dyv
        yoff_full = jnp.concatenate(o["yoff"], axis=1)
        dcum = _mm_exact(dyv * yoff_full, rdm)
        rs = jnp.zeros((L, LANES), F32)
        cs_t = jnp.zeros((LANES, L), F32)
        dBs, dCs = [], []
        dcd_row = jnp.zeros((1, LANES), F32)
        ddec = []
        for g in range(N_GROUPS):
            gs = slice(g * GROUP_W, (g + 1) * GROUP_W)
            Bg = Bm[:, g * N_STATE:(g + 1) * N_STATE]
            Cg = Cm[:, g * N_STATE:(g + 1) * N_STATE]
            Gg = o["G"][g]
            R = rstate[g]
            dwm = dyv[:, gs] * o["eo_rep"][:, gs]
            dC = _mm_nt(dwm, hprev[g])
            dH = _mm_tn(Cg, dwm)
            dG = jnp.zeros((L, L), F32)
            for hh in range(hpg):
                h = g * hpg + hh
                hs = slice(h * HEAD_DIM, (h + 1) * HEAD_DIM)
                lm = o["lms"][h]
                m_h = Gg * lm
                dM = _mm_nt(dyv[:, hs], xdt[:, hs])
                dx_scr[:, hs] = _mm_tn(m_h, dyv[:, hs])
                qm = dM * m_h
                rs = rs + jnp.sum(qm, axis=1, keepdims=True) * (lane == h).astype(F32)
                cs_t = cs_t + (sub == h).astype(F32) * jnp.sum(qm, axis=0, keepdims=True)
                dG = dG + dM * lm
            dC = dC + _mm(dG, Bg)
            dB = _mm_tn(dG, Cg)
            zx = _mm(Bg, R)
            dxdt_state = zx * o["dec_rep"][:, gs]
            ddec.append(zx * xdt[:, gs])
            dB = dB + _mm_nt(o["xdec"][g], R)
            dcd_row = dcd_row + _mm_exact(jnp.sum(R * hprev[g], axis=0, keepdims=True) * jnp.ones((8, 1), F32),
                                          rdm[gs, :])[0:1]
            rstate[g] = dH + o["cd_rep"][:, gs] * R
            dx_scr[:, gs] = dx_scr[:, gs] + dxdt_state
            dBs.append(dB)
            dCs.append(dC)
        dxdt = dx_scr[...]
        ddec_h = _mm_exact(jnp.concatenate(ddec, axis=1), rdm) * o["dec"]
        dcum_last = jnp.sum(ddec_h, axis=0, keepdims=True) + dcd_row * o["cd"]
        dcum = dcum + rs - cs_t.T - ddec_h + (sub == L - 1).astype(F32) * dcum_last
        dda = _mm_exact(o["causal"].astype(F32).T, dcum)
        ddt_v = dda * o["a_row"] + _mm_exact(dxdt * X, rdm)
        dX = dX + dxdt * o["dt_rep"]
        head_mask = (lane < N_HEADS).astype(F32)
        d_alog = jnp.sum(dda * o["dt"], axis=0, keepdims=True) * o["a_row"] * head_mask
        dpre = ddt_v * _sigmoid(o["pre"]) * head_mask
        dpm_ref[:, OFF_DT:] = dpre.astype(BF16)
        d_dtb = jnp.sum(dpre, axis=0, keepdims=True)
        dhp_ref[...] += jnp.concatenate([d_dtb, d_alog, d_dskip * head_mask, jnp.zeros((5, LANES), F32)], axis=0)

        dxbc = jnp.concatenate([dX] + dBs + dCs, axis=1)
        sg, cv = o["sg"], o["cv"]
        dcv = dxbc * (sg * (1.0 + cv * (1.0 - sg)))
        dcb_ref[...] += jnp.sum(dcv, axis=0, keepdims=True)
        dcw_ref[...] += jnp.concatenate(
            [jnp.sum(dcv * o["taps"][k], axis=0, keepdims=True) for k in range(4)], axis=0)
        e2 = jnp.concatenate([dcv, nxt_cv[...]], axis=0)
        n2 = L + CONV_HALO
        dux = (dcv * cw[3:4] + pltpu.roll(e2, n2 - 1, 0)[:L] * cw[2:3] + pltpu.roll(e2, n2 - 2, 0)[:L] * cw[1:2]
               + pltpu.roll(e2, n2 - 3, 0)[:L] * cw[0:1])
        dpm_ref[:, OFF_XBC:OFF_DT] = dux.astype(BF16)
        nxt_cv[...] = dcv[:CONV_HALO]

        @pl.when((b == nb - 1) & (ci == nc - 1))
        def _():
            exchange.finish()

    def full(shape):
        return pl.BlockSpec(shape, lambda b, c: (0,) * len(shape))

    def rowblk(b, c):
        return b * nc + (nc - 1 - c)

    hp_blocks = CHUNK // POOL_HALO
    hx_blocks = CHUNK // CONV_HALO
    T = nb * seq
    return pl.pallas_call(
        body, name="mixer_bwd", grid=(nb, nc),
        in_specs=[pl.BlockSpec((CHUNK, PROJ_W), lambda b, c: (rowblk(b, c), 0)),
                  pl.BlockSpec((POOL_HALO, POOL_W), lambda b, c: (jnp.maximum(rowblk(b, c) * hp_blocks - 1, 0), 0)),
                  pl.BlockSpec((CONV_HALO, CONV_CH), lambda b, c: (jnp.maximum(rowblk(b, c) * hx_blocks - 1, 0), 1)),
                  pl.BlockSpec((CHUNK, MIX_W), lambda b, c: (rowblk(b, c), 0)),
                  pl.BlockSpec((1, 1, N_GROUPS, N_STATE, GROUP_W), lambda b, c: (b, nc - 1 - c, 0, 0, 0)),
                  full((4, CONV_CH)), full((1, CONV_CH)), full((8, LANES)), full((1, D)),
                  full((4, LANES, LANES)), full((1, POOL_W))] + [ANY_SPEC] * ns,
        out_specs=[pl.BlockSpec((CHUNK, PROJ_W), lambda b, c: (rowblk(b, c), 0)),
                   full((4, CONV_CH)), full((1, CONV_CH)), full((8, LANES)), full((1, D)),
                   full((4, LANES, LANES)), full((1, POOL_W))] + [ANY_SPEC] * ns,
        out_shape=[jax.ShapeDtypeStruct((T, PROJ_W), BF16),
                   jax.ShapeDtypeStruct((4, CONV_CH), F32), jax.ShapeDtypeStruct((1, CONV_CH), F32),
                   jax.ShapeDtypeStruct((8, LANES), F32), jax.ShapeDtypeStruct((1, D), F32),
                   jax.ShapeDtypeStruct((4, LANES, LANES), F32), jax.ShapeDtypeStruct((1, POOL_W), F32)]
        + [jax.ShapeDtypeStruct(v.shape, v.dtype) for v in blocks],
        scratch_shapes=[pltpu.VMEM((POOL_HALO, POOL_W), F32), pltpu.VMEM((CONV_HALO, CONV_CH), F32),
                        pltpu.VMEM((N_GROUPS, N_STATE, GROUP_W), F32), pltpu.VMEM((CHUNK, D), F32),
                        pltpu.VMEM((CHUNK, D), F32)] + _comm_scratch(ns),
        compiler_params=_cparams(),
    )(pm, pm, pm, dym, hstates, cw, cb, hp, gssd, wpool, pscale, *blocks)


def _mlp_fused(x2, ymix, target, mod, g_mlp, g_final, w_out, w_up, w_down, seq):
    T = x2.shape[0]
    tm = min(256, seq)
    tps = seq // tm
    nblk = D_FF // FF_BLK

    def body(x_ref, ym_ref, tg_ref, mod_ref, gm_ref, gf_ref, wo_ref, wu_ref, wd_ref,
             da_ref, dym_ref, dh1_ref, u2_ref, f_ref, dup_ref, ddn_ref, dmod_ref, acc_ref, relu_scr):
        i = pl.program_id(0)

        @pl.when(i == 0)
        def _():
            acc_ref[...] = jnp.zeros_like(acc_ref)

        @pl.when(i % tps == 0)
        def _():
            dmod_ref[...] = jnp.zeros_like(dmod_ref)

        md = mod_ref[0]
        gate_m, shift_f, scale_f, gate_f = md[2:3], md[3:4], md[4:5], md[5:6]
        g_mlp, g_fin = gm_ref[...], gf_ref[...]
        a = jnp.dot(ym_ref[...], wo_ref[...], preferred_element_type=F32)
        h1 = x_ref[...] + gate_m * a
        r2 = lax.rsqrt(jnp.mean(h1 * h1, axis=-1, keepdims=True) + EPS)
        n2 = h1 * r2
        u2 = (n2 * g_mlp) * (1.0 + scale_f) + shift_f
        u2b = u2.astype(BF16)
        u2_ref[...] = u2b
        dn = jnp.zeros((tm, D), F32)
        for j in range(nblk):
            js = slice(j * FF_BLK, (j + 1) * FF_BLK)
            upj = jnp.maximum(jnp.dot(u2b, wu_ref[j], preferred_element_type=F32), 0.0)
            relu_scr[:, js] = upj
            fj = (upj * upj).astype(BF16)
            f_ref[:, js] = fj
            dn = dn + jnp.dot(fj, wd_ref[j], preferred_element_type=F32)
        h2 = h1 + gate_f * dn
        r3 = lax.rsqrt(jnp.mean(h2 * h2, axis=-1, keepdims=True) + EPS)
        n3 = h2 * r3
        err = n3 * g_fin - tg_ref[...]
        loss = 0.5 * jnp.sum(jnp.mean(err * err, axis=-1, keepdims=True), axis=0, keepdims=True)
        dout = err * (1.0 / D)
        d_gfin = jnp.sum(dout * n3, axis=0, keepdims=True)
        dn3 = dout * g_fin
        dh2 = r3 * (dn3 - n3 * jnp.mean(dn3 * n3, axis=-1, keepdims=True))
        d_gate_f = jnp.sum(dh2 * dn, axis=0, keepdims=True)
        ddn = (gate_f * dh2).astype(BF16)
        ddn_ref[...] = ddn
        du2 = jnp.zeros((tm, D), F32)
        for j in range(nblk):
            js = slice(j * FF_BLK, (j + 1) * FF_BLK)
            dfj = lax.dot_general(ddn, wd_ref[j], (((1,), (1,)), ((), ())), preferred_element_type=F32)
            dupj = (dfj * (2.0 * relu_scr[:, js])).astype(BF16)
            dup_ref[:, js] = dupj
            du2 = du2 + lax.dot_general(dupj, wu_ref[j], (((1,), (1,)), ((), ())), preferred_element_type=F32)
        d_scale_f = jnp.sum(du2 * (n2 * g_mlp), axis=0, keepdims=True)
        d_shift_f = jnp.sum(du2, axis=0, keepdims=True)
        d_gmlp = jnp.sum(du2 * (1.0 + scale_f) * n2, axis=0, keepdims=True)
        dn2 = du2 * (g_mlp * (1.0 + scale_f))
        dh1 = dh2 + r2 * (dn2 - n2 * jnp.mean(dn2 * n2, axis=-1, keepdims=True))
        dh1_ref[...] = dh1
        d_gate_m = jnp.sum(dh1 * a, axis=0, keepdims=True)
        da = (gate_m * dh1).astype(BF16)
        da_ref[...] = da
        dym_ref[...] = lax.dot_general(da, wo_ref[...], (((1,), (1,)), ((), ())),
                                       preferred_element_type=F32).astype(BF16)
        dmod_ref[0] += jnp.concatenate([d_gate_m, d_shift_f, d_scale_f, d_gate_f, jnp.zeros((4, D), F32)], axis=0)
        acc_ref[...] += jnp.concatenate([d_gmlp, d_gfin, loss * jnp.ones((1, D), F32), jnp.zeros((5, D), F32)], axis=0)

    whole = pl.BlockSpec(memory_space=pltpu.VMEM)

    def tok(w):
        return pl.BlockSpec((tm, w), lambda i: (i, 0))

    def vec():
        return pl.BlockSpec((1, D), lambda i: (0, 0))

    nb = T // seq
    return pl.pallas_call(
        body, name="mlp_fused", grid=(T // tm,),
        in_specs=[tok(D), tok(MIX_W), tok(D), pl.BlockSpec((1, 8, D), lambda i: (i // tps, 0, 0)), vec(), vec(),
                  whole, whole, whole],
        out_specs=[tok(D), tok(MIX_W), tok(D), tok(D), tok(D_FF), tok(D_FF), tok(D),
                   pl.BlockSpec((1, 8, D), lambda i: (i // tps, 0, 0)), pl.BlockSpec((8, D), lambda i: (0, 0))],
        out_shape=[jax.ShapeDtypeStruct((T, D), BF16), jax.ShapeDtypeStruct((T, MIX_W), BF16),
                   jax.ShapeDtypeStruct((T, D), F32), jax.ShapeDtypeStruct((T, D), BF16),
                   jax.ShapeDtypeStruct((T, D_FF), BF16), jax.ShapeDtypeStruct((T, D_FF), BF16),
                   jax.ShapeDtypeStruct((T, D), BF16), jax.ShapeDtypeStruct((nb, 8, D), F32),
                   jax.ShapeDtypeStruct((8, D), F32)],
        scratch_shapes=[pltpu.VMEM((tm, D_FF), F32)],
        compiler_params=_cparams(),
    )(x2, ymix, target, mod, g_mlp, g_final, w_out, w_up, w_down)


def _in_bwd(x2, dh1, dpb, mod, g_mix, w_cat, seq, blocks):
    T = x2.shape[0]
    tm = min(512, seq)
    tps = seq // tm
    ns = len(blocks)
    steps = T // tm

    def body(*refs):
        x_ref, dh_ref, dpb_ref, mod_ref, g_ref, w_ref = refs[:6]
        bl_refs = refs[6:6 + ns]
        dx_ref, dmod_ref, acc_ref = refs[6 + ns:9 + ns]
        ex_refs = refs[9 + ns:9 + 2 * ns]
        send, recv, loc = refs[9 + 2 * ns:]
        i = pl.program_id(0)
        exchange = _Exchange(bl_refs, ex_refs, send, recv, loc)

        @pl.when(i == 0)
        def _():
            exchange.start()

        @pl.when(i == 0)
        def _():
            acc_ref[...] = jnp.zeros_like(acc_ref)

        @pl.when(i % tps == 0)
        def _():
            dmod_ref[...] = jnp.zeros_like(dmod_ref)

        du = lax.dot_general(dpb_ref[...], w_ref[...], (((1,), (1,)), ((), ())), preferred_element_type=F32)
        x = x_ref[...]
        md = mod_ref[0]
        g = g_ref[...]
        r = lax.rsqrt(jnp.mean(x * x, axis=-1, keepdims=True) + EPS)
        n1 = x * r
        d_scale = jnp.sum(du * (n1 * g), axis=0, keepdims=True)
        d_shift = jnp.sum(du, axis=0, keepdims=True)
        d_g = jnp.sum(du * (1.0 + md[1:2]) * n1, axis=0, keepdims=True)
        dn1 = du * (g * (1.0 + md[1:2]))
        dx_ref[...] = dh_ref[...] + r * (dn1 - n1 * jnp.mean(dn1 * n1, axis=-1, keepdims=True))
        dmod_ref[0] += jnp.concatenate([d_shift, d_scale, jnp.zeros((6, D), F32)], axis=0)
        acc_ref[...] += jnp.concatenate([d_g, jnp.zeros((7, D), F32)], axis=0)

        @pl.when(i == steps - 1)
        def _():
            exchange.finish()

    whole = pl.BlockSpec(memory_space=pltpu.VMEM)
    nb = T // seq
    return pl.pallas_call(
        body, name="in_bwd", grid=(steps,),
        in_specs=[pl.BlockSpec((tm, D), lambda i: (i, 0)), pl.BlockSpec((tm, D), lambda i: (i, 0)),
                  pl.BlockSpec((tm, PROJ_W), lambda i: (i, 0)),
                  pl.BlockSpec((1, 8, D), lambda i: (i // tps, 0, 0)), pl.BlockSpec((1, D), lambda i: (0, 0)),
                  whole] + [ANY_SPEC] * ns,
        out_specs=[pl.BlockSpec((tm, D), lambda i: (i, 0)),
                   pl.BlockSpec((1, 8, D), lambda i: (i // tps, 0, 0)), pl.BlockSpec((8, D), lambda i: (0, 0))]
        + [ANY_SPEC] * ns,
        out_shape=[jax.ShapeDtypeStruct((T, D), F32),
                   jax.ShapeDtypeStruct((nb, 8, D), F32), jax.ShapeDtypeStruct((8, D), F32)]
        + [jax.ShapeDtypeStruct(v.shape, v.dtype) for v in blocks],
        scratch_shapes=_comm_scratch(ns),
        compiler_params=_cparams(),
    )(x2, dh1, dpb, mod, g_mix, w_cat, *blocks)


def _dw(a, b, name, bm, bn):
    T, M = a.shape
    N = b.shape[1]
    bk = min(512, T)
    nk = T // bk

    def body(a_ref, b_ref, o_ref):
        k = pl.program_id(2)

        @pl.when(k == 0)
        def _():
            o_ref[...] = jnp.zeros_like(o_ref)

        o_ref[...] += lax.dot_general(a_ref[...], b_ref[...], (((0,), (0,)), ((), ())), preferred_element_type=F32)

    return pl.pallas_call(
        body, name=name, grid=(M // bm, N // bn, nk),
        in_specs=[pl.BlockSpec((bk, bm), lambda i, j, k: (k, i)), pl.BlockSpec((bk, bn), lambda i, j, k: (k, j))],
        out_specs=pl.BlockSpec((bm, bn), lambda i, j, k: (i, j)),
        out_shape=jax.ShapeDtypeStruct((M, N), F32),
        compiler_params=_cparams(),
    )(a, b)


def _dw_blocks(a, b, name, by_rows, per_step=1):
    T, M = a.shape
    N = b.shape[1]
    bk = min(512, T)
    nk = T // bk
    whole = pl.BlockSpec(memory_space=pltpu.VMEM)
    if by_rows:
        rows = M // N_DEV
        am = rows * per_step
        nblk = N_DEV // per_step
        a_spec, b_spec = pl.BlockSpec((bk, am), lambda i, k: (k, i)), whole
        out_blk, acc_shape = (per_step, rows, N), (am, N)
    else:
        cols = N // N_DEV
        nblk = N_DEV
        a_spec, b_spec = whole, pl.BlockSpec((bk, cols), lambda i, k: (k, i))
        out_blk, acc_shape = (1, M, cols), (M, cols)

    def body(a_ref, b_ref, o_ref, acc):
        k = pl.program_id(1)

        @pl.when(k == 0)
        def _():
            acc[...] = jnp.zeros_like(acc)

        tok = pl.ds(pl.multiple_of(k * bk, bk), bk)
        a_blk = a_ref[...] if by_rows else a_ref[tok, :]
        b_blk = b_ref[tok, :] if by_rows else b_ref[...]
        acc[...] += lax.dot_general(a_blk, b_blk, (((0,), (0,)), ((), ())), preferred_element_type=F32)

        @pl.when(k == nk - 1)
        def _():
            o_ref[...] = acc[...].reshape(out_blk).astype(BF16)

    return pl.pallas_call(
        body, name=name, grid=(nblk, nk), in_specs=[a_spec, b_spec],
        out_specs=pl.BlockSpec(out_blk, lambda i, k: (i, 0, 0)),
        out_shape=jax.ShapeDtypeStruct((N_DEV,) + out_blk[1:], BF16),
        scratch_shapes=[pltpu.VMEM(acc_shape, F32)],
        compiler_params=_cparams(),
    )(a, b)


def _adam_parts(parts, w, m, v, name):
    rows, cols = w.shape
    br = rows
    for cand in range(rows, 15, -16):
        if rows % cand == 0 and cand * cols * 4 <= ADAM_BLOCK_BYTES:
            br = cand
            break

    def body(p_ref, w_ref, m_ref, v_ref, g_out, dl_out, m_out, v_out):
        g = p_ref[0].astype(F32)
        for k in range(1, N_DEV):
            g = g + p_ref[k].astype(F32)
        g_out[...] = g
        dl, mn, vn = _adam_math(w_ref[...], g, m_ref[...], v_ref[...])
        dl_out[...] = dl
        m_out[...] = mn
        v_out[...] = vn

    wspec = pl.BlockSpec((br, cols), lambda i: (i, 0))
    return pl.pallas_call(
        body, name=name, grid=(rows // br,),
        in_specs=[pl.BlockSpec((N_DEV, br, cols), lambda i: (0, i, 0)), wspec, wspec, wspec],
        out_specs=[wspec] * 4, out_shape=[jax.ShapeDtypeStruct((rows, cols), F32)] * 4,
        compiler_params=_cparams(),
    )(parts, w, m, v)


def _adam_plain(g, w, m, v, name):
    def body(g_ref, w_ref, m_ref, v_ref, dl_out, m_out, v_out):
        dl, mn, vn = _adam_math(w_ref[...], g_ref[...], m_ref[...], v_ref[...])
        dl_out[...] = dl
        m_out[...] = mn
        v_out[...] = vn

    return pl.pallas_call(body, name=name, out_shape=[jax.ShapeDtypeStruct(w.shape, F32)] * 3,
                          compiler_params=_cparams())(g, w, m, v)


_SMALL = [("b_ada", 48), ("_dmod1", 48), ("g_mix", 8), ("conv_w", 48), ("conv_b", 12), ("heads", 8), ("g_ssd", 8),
          ("w_pool", 512), ("pool_scale", 4), ("g_mlp", 8), ("g_final", 8), ("loss", 8)]
_SMALL_OFF = {}
_off = 0
for _n, _r in _SMALL:
    _SMALL_OFF[_n] = (_off, _r)
    _off += _r
SMALL_ROWS = _off


def _small_sum_adam(gathered, w, m, v):
    nd = _SMALL_OFF["_dmod1"][0]

    def body(p_ref, w_ref, m_ref, v_ref, g_out, dl_out, m_out, v_out):
        tot = p_ref[0]
        for k in range(1, N_DEV):
            tot = tot + p_ref[k]
        g = jnp.concatenate([tot[0:nd] + tot[nd:2 * nd], tot[nd:]], axis=0)
        g_out[...] = g
        dl, mn, vn = _adam_math(w_ref[...], g, m_ref[...], v_ref[...])
        dl_out[...] = dl
        m_out[...] = mn
        v_out[...] = vn

    return pl.pallas_call(body, name="small_sum_adam",
                          out_shape=[jax.ShapeDtypeStruct((SMALL_ROWS, LANES), F32)] * 4,
                          compiler_params=_cparams())(gathered, w, m, v)


def _rows(a, n):
    flat = a.reshape(-1)
    return jnp.pad(flat, (0, n * LANES - flat.shape[0])).reshape(n, LANES)


def _pack_small(named):
    parts = []
    for name, r in _SMALL:
        parts.append(_rows(named[name], r) if name in named else jnp.zeros((r, LANES), F32))
    return jnp.concatenate(parts, axis=0)


def _unpack(pack, name, shape):
    off, r = _SMALL_OFF[name]
    size = 1
    for s in shape:
        size *= s
    return pack[off:off + r].reshape(-1)[:size].reshape(shape)


def kernel(x, c, w_ada, b_ada, g_mix, w_in, conv_w, conv_b, dt_bias, a_log, d_skip, g_ssd, w_pool, pool_scale, w_out, g_mlp, w_up, w_down, g_final, loss_target, m_w_ada, m_b_ada, m_g_mix, m_w_in, m_conv_w, m_conv_b, m_dt_bias, m_a_log, m_d_skip, m_g_ssd, m_w_pool, m_pool_scale, m_w_out, m_g_mlp, m_w_up, m_w_down, m_g_final, v_w_ada, v_b_ada, v_g_mix, v_w_in, v_conv_w, v_conv_b, v_dt_bias, v_a_log, v_d_skip, v_g_ssd, v_w_pool, v_pool_scale, v_w_out, v_g_mlp, v_w_up, v_w_down, v_g_final):
    nb, seq, _ = x.shape
    T = nb * seq
    me = 4 * lax.axis_index("x") + 2 * lax.axis_index("y") + lax.axis_index("c")
    in_cols = w_in.shape[2]
    ada_cols = w_ada.shape[2]
    cw_cols = conv_w.shape[2]
    in_rows = D * in_cols // LANES

    small_f = jnp.concatenate([_rows(c, 16), _rows(conv_w[0], 8)], axis=0)
    win_s = w_in[0].astype(BF16).reshape(in_rows, LANES)
    small_g, win_g = _all_gather([small_f, win_s], "ag_first")
    c_all = small_g[:, :16].reshape(N_DEV * nb, D)
    cw_full = small_g[:, 16:24].reshape(N_DEV, -1)[:, :4 * cw_cols].reshape(N_DEV, 4, cw_cols)
    cw_full = cw_full.transpose(1, 0, 2).reshape(4, CONV_CH)
    win_full = win_g.reshape(N_DEV, D, in_cols).transpose(1, 0, 2).reshape(D, IN_W)
    w_cat = jnp.pad(win_full, ((0, 0), (0, PROJ_W - IN_W)))

    b_slice = lax.dynamic_slice(b_ada, (0, me * ada_cols), (1, ada_cols))
    mod_cols = _ada_fwd(c_all, w_ada[0], b_slice)
    (mod_g,) = _all_gather([mod_cols], "ag_mod")
    mod_all = mod_g.transpose(1, 0, 2).reshape(N_DEV * nb, 6, D)
    mod_mine = lax.dynamic_slice(mod_all, (me * nb, 0, 0), (nb, 6, D))
    mod = jnp.pad(mod_mine, ((0, 0), (0, 2), (0, 0)))

    x2 = x.reshape(T, D)
    tg2 = loss_target.reshape(T, D)
    heads = jnp.pad(jnp.concatenate([dt_bias, a_log, d_skip], axis=0), ((0, 5), (0, LANES - N_HEADS)))
    wpool_b = w_pool[0]
    u_b, pm = _mix_in(x2, mod, g_mix, w_cat, seq)
    ymix, hstates, wout_g, wup_g, wdn_g = _mixer_fwd(
        pm, cw_full, conv_b, heads, g_ssd, wpool_b, pool_scale, nb, seq,
        [w_out[0].astype(BF16), w_up[0].astype(BF16), w_down[0].astype(BF16)])
    da_b, dym, dh1, u2_b, f_b, dup_b, ddn_b, dmod_a, acc_a = _mlp_fused(
        x2, ymix, tg2, mod, g_mlp, g_final.reshape(1, D), wout_g.reshape(MIX_W, D), wup_g, wdn_g, seq)

    gout_p = _dw_blocks(ymix, da_b, "dw_out", True, per_step=4)
    gup_p = _dw_blocks(u2_b, dup_b, "dw_up", False)
    gdn_p = _dw_blocks(f_b, ddn_b, "dw_down", True)
    dpb, d_cw, d_cb, d_heads, d_gssd, d_wpool, d_pscale, gout_r, gup_r, gdn_r = _mixer_bwd(
        pm, dym, hstates, cw_full, conv_b, heads, g_ssd, wpool_b, pool_scale, nb, seq, [gout_p, gup_p, gdn_p])
    gw_in = _dw(u_b, dpb, "dw_in", D // 2, 640)
    gin_p = gw_in[:, :IN_W].reshape(D, N_DEV, in_cols).transpose(1, 0, 2).reshape(N_DEV, in_rows, LANES).astype(BF16)
    grad_x2, dmod_b, acc_b, gin_r = _in_bwd(x2, dh1, dpb, mod, g_mix, w_cat, seq, [gin_p])

    def flat_in(a):
        return a[0].reshape(in_rows, LANES)

    g_in, d_in, nm_in, nv_in = _adam_parts(gin_r, flat_in(w_in), flat_in(m_w_in), flat_in(v_w_in), "adam_w_in")
    g_out, d_out, nm_out, nv_out = _adam_parts(gout_r, w_out[0], m_w_out[0], v_w_out[0], "adam_w_out")
    g_up, d_up, nm_up, nv_up = _adam_parts(gup_r, w_up[0], m_w_up[0], v_w_up[0], "adam_w_up")
    g_dn, d_dn, nm_dn, nv_dn = _adam_parts(gdn_r, w_down[0], m_w_down[0], v_w_down[0], "adam_w_down")

    dmod = jnp.concatenate([dmod_b[:, 0:2], dmod_a[:, 0:4]], axis=1)
    small_named = {
        "b_ada": dmod[0], "_dmod1": dmod[1], "g_mix": acc_b[0], "conv_w": d_cw, "conv_b": d_cb, "heads": d_heads,
        "g_ssd": d_gssd, "w_pool": d_wpool, "pool_scale": d_pscale, "g_mlp": acc_a[0], "g_final": acc_a[1],
        "loss": acc_a[2, 0:1],
    }
    (small_all,) = _all_gather([_pack_small(small_named)], "ag_small_bwd")
    zeros16 = jnp.zeros((1, N_HEADS), F32)

    def pack_params(b_, gmix_, cb_, dtb_, al_, ds_, gs_, wp_, ps_, gm_, gf_):
        hd = jnp.pad(jnp.concatenate([dtb_, al_, ds_], axis=0), ((0, 5), (0, LANES - N_HEADS)))
        return _pack_small({"b_ada": b_, "g_mix": gmix_, "conv_b": cb_, "heads": hd, "g_ssd": gs_, "w_pool": wp_,
                            "pool_scale": ps_, "g_mlp": gm_, "g_final": gf_})

    w_pack = pack_params(b_ada, g_mix, conv_b, dt_bias, a_log, d_skip, g_ssd, w_pool, pool_scale, g_mlp, g_final)
    m_pack = pack_params(m_b_ada, m_g_mix, m_conv_b, m_dt_bias, m_a_log, m_d_skip, m_g_ssd, m_w_pool, m_pool_scale,
                         m_g_mlp, m_g_final)
    v_pack = pack_params(v_b_ada, v_g_mix, v_conv_b, v_dt_bias, v_a_log, v_d_skip, v_g_ssd, v_w_pool, v_pool_scale,
                         v_g_mlp, v_g_final)
    g_pack, d_pack, nm_pack, nv_pack = _small_sum_adam(small_all, w_pack, m_pack, v_pack)
    loss = _unpack(g_pack, "loss", (1,))[0]

    g_cw_full = _unpack(g_pack, "conv_w", (4, CONV_CH))
    g_cw = lax.dynamic_slice(g_cw_full, (0, me * cw_cols), (4, cw_cols))
    d_cwp, nm_cwp, nv_cwp = _adam_plain(g_cw, conv_w[0], m_conv_w[0], v_conv_w[0], "adam_conv_w")

    dmod_all = small_all[:, 0:96].reshape(N_DEV * nb, 6 * D)
    dmod_slice = lax.dynamic_slice(dmod_all, (0, me * ada_cols), (N_DEV * nb, ada_cols))
    g_ada, d_ada, nm_ada, nv_ada = _ada_bwd_adam(c_all, dmod_slice, w_ada[0], m_w_ada[0], v_w_ada[0])

    def small_outs(pack):
        hd = _unpack(pack, "heads", (8, LANES))
        return {
            "b_ada": _unpack(pack, "b_ada", (1, 6 * D)), "g_mix": _unpack(pack, "g_mix", (1, D)),
            "conv_b": _unpack(pack, "conv_b", (1, CONV_CH)), "dt_bias": hd[0:1, :N_HEADS], "a_log": hd[1:2, :N_HEADS],
            "d_skip": hd[2:3, :N_HEADS], "g_ssd": _unpack(pack, "g_ssd", (1, D)),
            "w_pool": _unpack(pack, "w_pool", (1, 4, LANES, LANES)), "pool_scale": _unpack(pack, "pool_scale", (1, POOL_W)),
            "g_mlp": _unpack(pack, "g_mlp", (1, D)), "g_final": _unpack(pack, "g_final", (D,)),
        }

    def big_outs(ada, win, cwp, wout, wup, wdn):
        return {"w_ada": ada[None], "w_in": win.reshape(1, D, in_cols), "conv_w": cwp[None], "w_out": wout[None],
                "w_up": wup[None], "w_down": wdn[None]}

    order = ["w_ada", "b_ada", "g_mix", "w_in", "conv_w", "conv_b", "dt_bias", "a_log", "d_skip", "g_ssd", "w_pool",
             "pool_scale", "w_out", "g_mlp", "w_up", "w_down", "g_final"]
    groups = [
        {**small_outs(g_pack), **big_outs(g_ada, g_in, g_cw, g_out, g_up, g_dn)},
        {**small_outs(d_pack), **big_outs(d_ada, d_in, d_cwp, d_out, d_up, d_dn)},
        {**small_outs(nm_pack), **big_outs(nm_ada, nm_in, nm_cwp, nm_out, nm_up, nm_dn)},
        {**small_outs(nv_pack), **big_outs(nv_ada, nv_in, nv_cwp, nv_out, nv_up, nv_dn)},
    ]
    outs = [loss, grad_x2.reshape(nb, seq, D)]
    for grp in groups:
        outs += [grp[n] for n in order]
    return tuple(outs)
```

```python
import functools

import jax
import jax.numpy as jnp
from jax import lax
from jax.experimental import pallas as pl
from jax.experimental.pallas import tpu as pltpu

F32, BF16 = jnp.float32, jnp.bfloat16
MESH = pl.DeviceIdType.MESH
N_DEV = 8
D = 1024
LANES = 128
CHUNK = 128
POOL_W = 512
WINDOWS = (2, 4, 8, 16)
N_HEADS = 16
HEAD_DIM = 64
N_GROUPS = 2
GROUP_W = 512
N_STATE = 128
CONV_CH = 1536
OFF_Z, OFF_XBC, OFF_DT, IN_W = 512, 1536, 3072, 3088
PROJ_W = OFF_DT + LANES
MIX_W = 1536
D_FF = 4096
FF_BLK = 512
EPS = 1e-5
LR, B1, B2, AEPS, WD, STEP = 0.001, 0.9, 0.999, 1e-08, 0.01, 10
POOL_HALO = 16
CONV_HALO = 8
VMEM_LIMIT = 56 << 20
ADAM_BLOCK_BYTES = 1 << 20
DW_IN_WIN = 512


def _cparams(**kw):
    return pltpu.CompilerParams(vmem_limit_bytes=VMEM_LIMIT, **kw)


def _mm(a, b):
    return jnp.dot(a.astype(BF16), b.astype(BF16), preferred_element_type=F32)


def _mm_nt(a, b):
    return lax.dot_general(a.astype(BF16), b.astype(BF16), (((1,), (1,)), ((), ())), preferred_element_type=F32)


def _mm_tn(a, b):
    return lax.dot_general(a.astype(BF16), b.astype(BF16), (((0,), (0,)), ((), ())), preferred_element_type=F32)


def _mm_exact(a, b):
    return jnp.dot(a, b, preferred_element_type=F32, precision=lax.Precision.HIGHEST)


def _sigmoid(v):
    return 1.0 / (1.0 + jnp.exp(-v))


def _expand_mat():
    r = lax.broadcasted_iota(jnp.int32, (LANES, D), 0)
    c = lax.broadcasted_iota(jnp.int32, (LANES, D), 1)
    return (r == c // HEAD_DIM).astype(F32)


def _reduce_mat():
    r = lax.broadcasted_iota(jnp.int32, (D, LANES), 0)
    c = lax.broadcasted_iota(jnp.int32, (D, LANES), 1)
    return (c == r // HEAD_DIM).astype(F32)


def _pos():
    return lax.axis_index("x"), lax.axis_index("y"), lax.axis_index("c")


class _Gather:
    def __init__(self, x_refs, o_refs, send, recv, loc):
        self.x_refs, self.o_refs, self.send, self.recv, self.loc = x_refs, o_refs, send, recv, loc
        self.n = len(x_refs)
        x, y, c = _pos()
        self.c = c
        self.me, self.sib = (x, y, c), (x, y, 1 - c)
        self.chips = [(1 - x, y), (x, 1 - y), (1 - x, 1 - y)]

    def _cp(self, a, k, block, to, src=None):
        dst = self.o_refs[a].at[4 * block[0] + 2 * block[1] + block[2]]
        return pltpu.make_async_remote_copy(
            src_ref=dst if src is None else src, dst_ref=dst,
            send_sem=self.send.at[a * 7 + k], recv_sem=self.recv.at[a * 7 + k],
            device_id=to, device_id_type=MESH)

    def _mine(self, a):
        me = self.me
        return pltpu.make_async_copy(self.x_refs[a], self.o_refs[a].at[4 * me[0] + 2 * me[1] + me[2]], self.loc.at[a])

    def _first(self, a):
        cps = [self._cp(a, 0, self.me, self.sib, src=self.x_refs[a])]
        return cps + [self._cp(a, 1 + j, self.me, (*chip, self.c), src=self.x_refs[a])
                      for j, chip in enumerate(self.chips)]

    def _passed(self, a, j):
        return self._cp(a, 4 + j, (*self.chips[j], self.c), self.sib)

    def start(self):
        for a in range(self.n):
            self._mine(a).start()
            for cp in self._first(a):
                cp.start()

    def forward(self):
        for j, chip in enumerate(self.chips):
            for a in range(self.n):
                self._cp(a, 1 + j, (*chip, self.c), self.me).wait_recv()
                self._passed(a, j).start()

    def finish(self):
        for a in range(self.n):
            self._cp(a, 0, self.sib, self.me).wait_recv()
            for j, chip in enumerate(self.chips):
                self._cp(a, 4 + j, (*chip, 1 - self.c), self.me).wait_recv()
        for a in range(self.n):
            for cp in self._first(a):
                cp.wait_send()
            for j in range(3):
                self._passed(a, j).wait_send()
            self._mine(a).wait()


class _Exchange:
    def __init__(self, x_refs, o_refs, send, recv, loc):
        self.x_refs, self.o_refs, self.send, self.recv, self.loc = x_refs, o_refs, send, recv, loc
        self.n = len(x_refs)
        x, y, c = _pos()
        self.me_i = 4 * x + 2 * y + c
        self.peers = []
        for k in range(1, N_DEV):
            px = 1 - x if (k >> 2) & 1 else x
            py = 1 - y if (k >> 1) & 1 else y
            pc = 1 - c if k & 1 else c
            self.peers.append(((px, py, pc), 4 * px + 2 * py + pc))

    def _mine(self, a):
        return pltpu.make_async_copy(self.x_refs[a].at[self.me_i], self.o_refs[a].at[self.me_i], self.loc.at[a])

    def _cp(self, a, k, landing):
        peer, peer_i = self.peers[k]
        return pltpu.make_async_remote_copy(
            src_ref=self.x_refs[a].at[peer_i], dst_ref=self.o_refs[a].at[landing],
            send_sem=self.send.at[a * 7 + k], recv_sem=self.recv.at[a * 7 + k],
            device_id=peer, device_id_type=MESH)

    def start(self):
        for a in range(self.n):
            self._mine(a).start()
            for k in range(N_DEV - 1):
                self._cp(a, k, self.me_i).start()

    def finish(self):
        for a in range(self.n):
            for k in range(N_DEV - 1):
                self._cp(a, k, self.peers[k][1]).wait_recv()
        for a in range(self.n):
            for k in range(N_DEV - 1):
                self._cp(a, k, self.me_i).wait_send()
            self._mine(a).wait()


def _comm_scratch(n):
    return [pltpu.SemaphoreType.DMA((7 * n,)), pltpu.SemaphoreType.DMA((7 * n,)), pltpu.SemaphoreType.DMA((n,))]


ANY_SPEC = pl.BlockSpec(memory_space=pl.ANY)


def _all_gather(xs, name):
    n = len(xs)

    def body(*refs):
        g = _Gather(refs[:n], refs[n:2 * n], *refs[2 * n:])
        g.start()
        g.forward()
        g.finish()

    return pl.pallas_call(
        body, name=name,
        out_shape=[jax.ShapeDtypeStruct((N_DEV,) + v.shape, v.dtype) for v in xs],
        in_specs=[ANY_SPEC] * n, out_specs=[ANY_SPEC] * n, scratch_shapes=_comm_scratch(n),
    )(*xs)


def _ada_fwd(c_all, w_ada, b_slice):
    def body(c_ref, w_ref, b_ref, o_ref):
        cv = c_ref[...]
        act = cv * _sigmoid(cv)
        o_ref[...] = _mm(act, w_ref[...]) + b_ref[...]

    nb, nc = c_all.shape[0], w_ada.shape[1]
    return pl.pallas_call(body, name="ada_fwd", out_shape=jax.ShapeDtypeStruct((nb, nc), F32),
                          compiler_params=_cparams())(c_all, w_ada, b_slice)


def _adam_math(w, g, m, v):
    m = B1 * m + (1.0 - B1) * g
    v = B2 * v + (1.0 - B2) * jnp.square(g)
    m_hat = m / (1.0 - B1 ** STEP)
    v_hat = v / (1.0 - B2 ** STEP)
    delta = -LR * (m_hat / (jnp.sqrt(v_hat) + AEPS) + WD * w)
    return delta, m, v


def _ada_bwd_adam(c_all, dmod_slice, w, m, v):
    rows, cols = w.shape
    br = 256

    def body(c_ref, d_ref, w_ref, m_ref, v_ref, g_out, dl_out, m_out, v_out):
        cv = c_ref[...]
        act = cv * _sigmoid(cv)
        g = _mm_tn(act, d_ref[...])
        g_out[...] = g
        dl, mn, vn = _adam_math(w_ref[...], g, m_ref[...], v_ref[...])
        dl_out[...] = dl
        m_out[...] = mn
        v_out[...] = vn

    nb = c_all.shape[0]
    wspec = pl.BlockSpec((br, cols), lambda i: (i, 0))
    return pl.pallas_call(
        body, name="ada_bwd_adam", grid=(rows // br,),
        in_specs=[pl.BlockSpec((nb, br), lambda i: (0, i)), pl.BlockSpec((nb, cols), lambda i: (0, 0)),
                  wspec, wspec, wspec],
        out_specs=[wspec] * 4, out_shape=[jax.ShapeDtypeStruct((rows, cols), F32)] * 4,
        compiler_params=_cparams(),
    )(c_all, dmod_slice, w, m, v)


def _mix_in(x2, mod, g_mix, win_g, seq):
    T = x2.shape[0]
    tm = min(512, seq)
    tps = seq // tm
    in_cols = win_g.shape[2]

    def body(x_ref, mod_ref, g_ref, wb_ref, u_ref, pm_ref, wc_ref, w_ref):
        @pl.when(pl.program_id(0) == 0)
        def _():
            w_ref[:, OFF_DT:] = jnp.zeros((D, PROJ_W - OFF_DT), BF16)
            for j in range(N_DEV):
                w_ref[:, in_cols * j:in_cols * (j + 1)] = wb_ref[j]
            wc_ref[...] = w_ref[...]

        x = x_ref[...]
        r = lax.rsqrt(jnp.mean(x * x, axis=-1, keepdims=True) + EPS)
        md = mod_ref[0]
        u = (x * r * g_ref[...]) * (1.0 + md[1:2]) + md[0:1]
        ub = u.astype(BF16)
        u_ref[...] = ub
        pm_ref[...] = jnp.dot(ub, w_ref[...], preferred_element_type=F32)

    whole = pl.BlockSpec(memory_space=pltpu.VMEM)
    return pl.pallas_call(
        body, name="mix_in", grid=(T // tm,),
        in_specs=[pl.BlockSpec((tm, D), lambda i: (i, 0)), pl.BlockSpec((1, 8, D), lambda i: (i // tps, 0, 0)),
                  pl.BlockSpec((1, D), lambda i: (0, 0)), whole],
        out_specs=[pl.BlockSpec((tm, D), lambda i: (i, 0)), pl.BlockSpec((tm, PROJ_W), lambda i: (i, 0)),
                   pl.BlockSpec((D, PROJ_W), lambda i: (0, 0))],
        out_shape=[jax.ShapeDtypeStruct((T, D), BF16), jax.ShapeDtypeStruct((T, PROJ_W), F32),
                   jax.ShapeDtypeStruct((D, PROJ_W), BF16)],
        scratch_shapes=[pltpu.VMEM((D, PROJ_W), BF16)],
        compiler_params=_cparams(),
    )(x2, mod, g_mix, win_g)


def _chunk_forward(up, z, ux, dtin, halo_p, halo_x, hprev, cw, cb, hp, gssd, wpool, pscale, t0, y_scr):
    L = CHUNK
    out = {}
    row = lax.broadcasted_iota(jnp.int32, (L, 1), 0)
    t = (t0 + row + 1).astype(F32)
    e = jnp.concatenate([halo_p, up], axis=0)
    s2 = e + pltpu.roll(e, 1, 0)
    s4 = s2 + pltpu.roll(s2, 2, 0)
    s8 = s4 + pltpu.roll(s4, 4, 0)
    s16 = s8 + pltpu.roll(s8, 8, 0)
    sums = (s2, s4, s8, s16)
    p, inv, yp = [], [], []
    for gi, w in enumerate(WINDOWS):
        sl = slice(gi * LANES, (gi + 1) * LANES)
        ic = 1.0 / jnp.minimum(t, float(w))
        pg = sums[gi][POOL_HALO:, sl] * ic - up[:, sl]
        p.append(pg)
        inv.append(ic)
        yp.append(_mm(pg, wpool[gi]))
    out["p"], out["inv"], out["yp"] = p, inv, yp
    out["y_pool"] = jnp.concatenate(yp, axis=1) * pscale
    ex = jnp.concatenate([halo_x, ux], axis=0)
    taps = [pltpu.roll(ex, 3, 0)[CONV_HALO:], pltpu.roll(ex, 2, 0)[CONV_HALO:], pltpu.roll(ex, 1, 0)[CONV_HALO:], ux]
    cv = cb + taps[0] * cw[0:1] + taps[1] * cw[1:2] + taps[2] * cw[2:3] + taps[3] * cw[3:4]
    sg = _sigmoid(cv)
    xbc = cv * sg
    out["taps"], out["cv"], out["sg"] = taps, cv, sg
    X = xbc[:, :D]
    Bm = xbc[:, D:D + N_GROUPS * N_STATE]
    Cm = xbc[:, D + N_GROUPS * N_STATE:]
    pre = dtin + hp[0:1]
    dt = jnp.maximum(pre, 0.0) + jnp.log(1.0 + jnp.exp(-jnp.abs(pre)))
    a_row = -jnp.exp(hp[1:2])
    da = dt * a_row
    ri = lax.broadcasted_iota(jnp.int32, (L, L), 0)
    ci = lax.broadcasted_iota(jnp.int32, (L, L), 1)
    causal = ri >= ci
    cum = _mm_exact(causal.astype(F32), da)
    cum_t = cum.T
    cum_last = cum[L - 1:L]
    eo = jnp.exp(cum)
    dec = jnp.exp(cum_last - cum)
    cd = jnp.exp(cum_last)
    exm = _expand_mat()
    dt_rep = _mm_exact(dt, exm)
    eo_rep = _mm_exact(eo, exm)
    dec_rep = _mm_exact(dec, exm)
    rows8 = jnp.concatenate([cd, hp[2:3], jnp.zeros((6, LANES), F32)], axis=0)
    rows8_rep = _mm_exact(rows8, exm)
    cd_rep, dskip_rep = rows8_rep[0:1], rows8_rep[1:2]
    xdt = X * dt_rep
    out.update(X=X, Bm=Bm, Cm=Cm, pre=pre, dt=dt, a_row=a_row, cum=cum, cum_t=cum_t, eo=eo, dec=dec, cd=cd,
               dt_rep=dt_rep, eo_rep=eo_rep, dec_rep=dec_rep, cd_rep=cd_rep, dskip_rep=dskip_rep, xdt=xdt,
               causal=causal, exm=exm)
    G, lms, yoff, hnew, xdec = [], [], [], [], []
    for g in range(N_GROUPS):
        gs = slice(g * GROUP_W, (g + 1) * GROUP_W)
        Bg = Bm[:, g * N_STATE:(g + 1) * N_STATE]
        Cg = Cm[:, g * N_STATE:(g + 1) * N_STATE]
        Gg = _mm_nt(Cg, Bg)
        G.append(Gg)
        for hh in range(N_HEADS // N_GROUPS):
            h = g * (N_HEADS // N_GROUPS) + hh
            seg = cum[:, h:h + 1] - cum_t[h:h + 1, :]
            lm = jnp.where(causal, jnp.exp(jnp.minimum(seg, 0.0)), 0.0)
            lms.append(lm)
            hs = slice(h * HEAD_DIM, (h + 1) * HEAD_DIM)
            y_scr[:, hs] = _mm(Gg * lm, xdt[:, hs])
        xd = xdt[:, gs] * dec_rep[:, gs]
        xdec.append(xd)
        sgm = _mm_tn(Bg, xd)
        yoff.append(_mm(Cg, hprev[g]) * eo_rep[:, gs])
        hnew.append(hprev[g] * cd_rep[:, gs] + sgm)
    out.update(G=G, lms=lms, yoff=yoff, hnew=hnew, xdec=xdec)
    y = y_scr[...] + jnp.concatenate(yoff, axis=1) + dskip_rep * X
    sz = _sigmoid(z)
    silz = z * sz
    yz = y * silz
    rg, yn = [], []
    for g in range(N_GROUPS):
        gs = slice(g * GROUP_W, (g + 1) * GROUP_W)
        r = lax.rsqrt(jnp.mean(yz[:, gs] * yz[:, gs], axis=-1, keepdims=True) + EPS)
        rg.append(r)
        yn.append(yz[:, gs] * r)
    yn = jnp.concatenate(yn, axis=1)
    out.update(y=y, sz=sz, silz=silz, rg=rg, yn=yn)
    out["y_ssd"] = yn * gssd
    return out


def _mixer_fwd(pm, cw, cb, hp, gssd, wpool, pscale, nb, seq, shards):
    nc = seq // CHUNK
    ns = len(shards)
    steps = nb * nc
    fwd_step = (3 * steps) // 4

    def body(*refs):
        pm_ref, cw_ref, cb_ref, hp_ref, gs_ref, wp_ref, ps_ref = refs[:7]
        sh_refs = refs[7:7 + ns]
        ym_ref, hs_ref = refs[7 + ns:9 + ns]
        ga_refs = refs[9 + ns:9 + 2 * ns]
        halo_p, halo_x, state, y_scr, send, recv, loc = refs[9 + 2 * ns:]
        c = pl.program_id(1)
        step = pl.program_id(0) * nc + c
        gather = _Gather(sh_refs, ga_refs, send, recv, loc)

        @pl.when(step == 0)
        def _():
            gather.start()

        @pl.when(step == fwd_step)
        def _():
            gather.forward()

        @pl.when(c == 0)
        def _():
            halo_p[...] = jnp.zeros_like(halo_p)
            halo_x[...] = jnp.zeros_like(halo_x)
            state[...] = jnp.zeros_like(state)

        up = pm_ref[:, 0:POOL_W]
        z = pm_ref[:, OFF_Z:OFF_XBC]
        ux = pm_ref[:, OFF_XBC:OFF_DT]
        hprev = [state[0], state[1]]
        hs_ref[0, 0, 0] = hprev[0]
        hs_ref[0, 0, 1] = hprev[1]
        o = _chunk_forward(up, z, ux, pm_ref[:, OFF_DT:], halo_p[...], halo_x[...], hprev, cw_ref[...], cb_ref[...],
                           hp_ref[...], gs_ref[...], wp_ref[...], ps_ref[...], c * CHUNK, y_scr)
        ym_ref[:, 0:POOL_W] = o["y_pool"].astype(BF16)
        ym_ref[:, POOL_W:] = o["y_ssd"].astype(BF16)
        state[0] = o["hnew"][0]
        state[1] = o["hnew"][1]
        halo_p[...] = up[CHUNK - POOL_HALO:]
        halo_x[...] = ux[CHUNK - CONV_HALO:]

        @pl.when(step == steps - 1)
        def _():
            gather.finish()

    def full(shape):
        return pl.BlockSpec(shape, lambda b, c: (0,) * len(shape))

    T = nb * seq
    return pl.pallas_call(
        body, name="mixer_fwd", grid=(nb, nc),
        in_specs=[pl.BlockSpec((CHUNK, PROJ_W), lambda b, c: (b * nc + c, 0)),
                  full((4, CONV_CH)), full((1, CONV_CH)), full((8, LANES)), full((1, D)),
                  full((4, LANES, LANES)), full((1, POOL_W))] + [ANY_SPEC] * ns,
        out_specs=[pl.BlockSpec((CHUNK, MIX_W), lambda b, c: (b * nc + c, 0)),
                   pl.BlockSpec((1, 1, N_GROUPS, N_STATE, GROUP_W), lambda b, c: (b, c, 0, 0, 0))] + [ANY_SPEC] * ns,
        out_shape=[jax.ShapeDtypeStruct((T, MIX_W), BF16),
                   jax.ShapeDtypeStruct((nb, nc, N_GROUPS, N_STATE, GROUP_W), F32)]
        + [jax.ShapeDtypeStruct((N_DEV,) + v.shape, v.dtype) for v in shards],
        scratch_shapes=[pltpu.VMEM((POOL_HALO, POOL_W), F32), pltpu.VMEM((CONV_HALO, CONV_CH), F32),
                        pltpu.VMEM((N_GROUPS, N_STATE, GROUP_W), F32), pltpu.VMEM((CHUNK, D), F32)] + _comm_scratch(ns),
        compiler_params=_cparams(),
    )(pm, cw, cb, hp, gssd, wpool, pscale, *shards)


def _mixer_bwd(pm, dym, hstates, cw, cb, hp, gssd, wpool, pscale, nb, seq, blocks):
    nc = seq // CHUNK
    hpg = N_HEADS // N_GROUPS
    ns = len(blocks)
    steps = nb * nc

    def body(*refs):
        (pm_ref, hpool_ref, hxbc_ref, dy_ref, hs_ref, cw_ref, cb_ref, hp_ref, gs_ref, wp_ref, ps_ref) = refs[:11]
        bl_refs = refs[11:11 + ns]
        dpm_ref, dcw_ref, dcb_ref, dhp_ref, dgs_ref, dwp_ref, dps_ref = refs[11 + ns:18 + ns]
        ex_refs = refs[18 + ns:18 + 2 * ns]
        nxt_q, nxt_cv, rstate, y_scr, dx_scr, send, recv, loc = refs[18 + 2 * ns:]
        b = pl.program_id(0)
        ci = pl.program_id(1)
        c = nc - 1 - ci
        exchange = _Exchange(bl_refs, ex_refs, send, recv, loc)

        @pl.when((b == 0) & (ci == 0))
        def _():
            exchange.start()

        @pl.when((b == 0) & (ci == 0))
        def _():
            for r in (dcw_ref, dcb_ref, dhp_ref, dgs_ref, dwp_ref, dps_ref):
                r[...] = jnp.zeros_like(r)

        @pl.when(ci == 0)
        def _():
            nxt_q[...] = jnp.zeros_like(nxt_q)
            nxt_cv[...] = jnp.zeros_like(nxt_cv)
            rstate[...] = jnp.zeros_like(rstate)

        first = (c > 0).astype(F32)
        up = pm_ref[:, 0:POOL_W]
        z = pm_ref[:, OFF_Z:OFF_XBC]
        ux = pm_ref[:, OFF_XBC:OFF_DT]
        halo_p = hpool_ref[...] * first
        halo_x = hxbc_ref[...] * first
        hprev = [hs_ref[0, 0, 0], hs_ref[0, 0, 1]]
        cw, cb, hp, gssd, wpool, pscale = cw_ref[...], cb_ref[...], hp_ref[...], gs_ref[...], wp_ref[...], ps_ref[...]
        o = _chunk_forward(up, z, ux, pm_ref[:, OFF_DT:], halo_p, halo_x, hprev, cw, cb, hp, gssd, wpool, pscale,
                           c * CHUNK, y_scr)
        L = CHUNK
        dy_pool = dy_ref[:, 0:POOL_W].astype(F32)
        dy_ssd = dy_ref[:, POOL_W:].astype(F32)

        dps_ref[...] += jnp.sum(dy_pool * jnp.concatenate(o["yp"], axis=1), axis=0, keepdims=True)
        dyp = dy_pool * pscale
        qs = []
        dps = []
        for gi in range(len(WINDOWS)):
            sl = slice(gi * LANES, (gi + 1) * LANES)
            dwp_ref[gi] += _mm_tn(o["p"][gi], dyp[:, sl])
            dpg = _mm_nt(dyp[:, sl], wpool[gi])
            dps.append(dpg)
            qs.append(dpg * o["inv"][gi])
        q = jnp.concatenate(qs, axis=1)
        e = jnp.concatenate([q, nxt_q[...]], axis=0)
        n = L + POOL_HALO
        s2 = e + pltpu.roll(e, n - 1, 0)
        s4 = s2 + pltpu.roll(s2, n - 2, 0)
        s8 = s4 + pltpu.roll(s4, n - 4, 0)
        s16 = s8 + pltpu.roll(s8, n - 8, 0)
        sums = (s2, s4, s8, s16)
        for gi in range(len(WINDOWS)):
            sl = slice(gi * LANES, (gi + 1) * LANES)
            dpm_ref[:, sl] = (sums[gi][:L, sl] - dps[gi]).astype(BF16)
        nxt_q[...] = q[:POOL_HALO]

        yn, y, silz, sz = o["yn"], o["y"], o["silz"], o["sz"]
        dgs_ref[...] += jnp.sum(dy_ssd * yn, axis=0, keepdims=True)
        dyn = dy_ssd * gssd
        dyz = []
        for g in range(N_GROUPS):
            gs = slice(g * GROUP_W, (g + 1) * GROUP_W)
            mean = jnp.mean(dyn[:, gs] * yn[:, gs], axis=-1, keepdims=True)
            dyz.append(o["rg"][g] * (dyn[:, gs] - yn[:, gs] * mean))
        dyz = jnp.concatenate(dyz, axis=1)
        dyv = dyz * silz
        dpm_ref[:, OFF_Z:OFF_XBC] = (dyz * y * (sz * (1.0 + z * (1.0 - sz)))).astype(BF16)

        X, Bm, Cm, xdt = o["X"], o["Bm"], o["Cm"], o["xdt"]
        exm = o["exm"]
        rdm = _reduce_mat()
        lane = lax.broadcasted_iota(jnp.int32, (1, LANES), 1)
        sub = lax.broadcasted_iota(jnp.int32, (LANES, 1), 0)
        d_dskip = _mm_exact(jnp.sum(dyv * X, axis=0, keepdims=True) * jnp.ones((8, 1), F32), rdm)[0:1]
        dX = o["dskip_rep"] * dyv
        yoff_full = jnp.concatenate(o["yoff"], axis=1)
        dcum = _mm_exact(dyv * yoff_full, rdm)
        rs = jnp.zeros((L, LANES), F32)
        cs_t = jnp.zeros((LANES, L), F32)
        dBs, dCs = [], []
        dcd_row = jnp.zeros((1, LANES), F32)
        ddec = []
        for g in range(N_GROUPS):
            gs = slice(g * GROUP_W, (g + 1) * GROUP_W)
            Bg = Bm[:, g * N_STATE:(g + 1) * N_STATE]
            Cg = Cm[:, g * N_STATE:(g + 1) * N_STATE]
            Gg = o["G"][g]
            R = rstate[g]
            dwm = dyv[:, gs] * o["eo_rep"][:, gs]
            dC = _mm_nt(dwm, hprev[g])
            dH = _mm_tn(Cg, dwm)
            dG = jnp.zeros((L, L), F32)
            for hh in range(hpg):
                h = g * hpg + hh
                hs = slice(h * HEAD_DIM, (h + 1) * HEAD_DIM)
                lm = o["lms"][h]
                m_h = Gg * lm
                dM = _mm_nt(dyv[:, hs], xdt[:, hs])
                dx_scr[:, hs] = _mm_tn(m_h, dyv[:, hs])
                qm = dM * m_h
                rs = rs + jnp.sum(qm, axis=1, keepdims=True) * (lane == h).astype(F32)
                cs_t = cs_t + (sub == h).astype(F32) * jnp.sum(qm, axis=0, keepdims=True)
                dG = dG + dM * lm
            dC = dC + _mm(dG, Bg)
            dB = _mm_tn(dG, Cg)
            zx = _mm(Bg, R)
            dxdt_state = zx * o["dec_rep"][:, gs]
            ddec.append(zx * xdt[:, gs])
            dB = dB + _mm_nt(o["xdec"][g], R)
            dcd_row = dcd_row + _mm_exact(jnp.sum(R * hprev[g], axis=0, keepdims=True) * jnp.ones((8, 1), F32),
                                          rdm[gs, :])[0:1]
            rstate[g] = dH + o["cd_rep"][:, gs] * R
            dx_scr[:, gs] = dx_scr[:, gs] + dxdt_state
            dBs.append(dB)
            dCs.append(dC)
        dxdt = dx_scr[...]
        ddec_h = _mm_exact(jnp.concatenate(ddec, axis=1), rdm) * o["dec"]
        dcum_last = jnp.sum(ddec_h, axis=0, keepdims=True) + dcd_row * o["cd"]
        dcum = dcum + rs - cs_t.T - ddec_h + (sub == L - 1).astype(F32) * dcum_last
        dda = _mm_exact(o["causal"].astype(F32).T, dcum)
        ddt_v = dda * o["a_row"] + _mm_exact(dxdt * X, rdm)
        dX = dX + dxdt * o["dt_rep"]
        head_mask = (lane < N_HEADS).astype(F32)
        d_alog = jnp.sum(dda * o["dt"], axis=0, keepdims=True) * o["a_row"] * head_mask
        dpre = ddt_v * _sigmoid(o["pre"]) * head_mask
        dpm_ref[:, OFF_DT:] = dpre.astype(BF16)
        d_dtb = jnp.sum(dpre, axis=0, keepdims=True)
        dhp_ref[...] += jnp.concatenate([d_dtb, d_alog, d_dskip * head_mask, jnp.zeros((5, LANES), F32)], axis=0)

        dxbc = jnp.concatenate([dX] + dBs + dCs, axis=1)
        sg, cv = o["sg"], o["cv"]
        dcv = dxbc * (sg * (1.0 + cv * (1.0 - sg)))
        dcb_ref[...] += jnp.sum(dcv, axis=0, keepdims=True)
        dcw_ref[...] += jnp.concatenate(
            [jnp.sum(dcv * o["taps"][k], axis=0, keepdims=True) for k in range(4)], axis=0)
        e2 = jnp.concatenate([dcv, nxt_cv[...]], axis=0)
        n2 = L + CONV_HALO
        dux = (dcv * cw[3:4] + pltpu.roll(e2, n2 - 1, 0)[:L] * cw[2:3] + pltpu.roll(e2, n2 - 2, 0)[:L] * cw[1:2]
               + pltpu.roll(e2, n2 - 3, 0)[:L] * cw[0:1])
        dpm_ref[:, OFF_XBC:OFF_DT] = dux.astype(BF16)
        nxt_cv[...] = dcv[:CONV_HALO]

        @pl.when((b == nb - 1) & (ci == nc - 1))
        def _():
            exchange.finish()

    def full(shape):
        return pl.BlockSpec(shape, lambda b, c: (0,) * len(shape))

    def rowblk(b, c):
        return b * nc + (nc - 1 - c)

    hp_blocks = CHUNK // POOL_HALO
    hx_blocks = CHUNK // CONV_HALO
    T = nb * seq
    return pl.pallas_call(
        body, name="mixer_bwd", grid=(nb, nc),
        in_specs=[pl.BlockSpec((CHUNK, PROJ_W), lambda b, c: (rowblk(b, c), 0)),
                  pl.BlockSpec((POOL_HALO, POOL_W), lambda b, c: (jnp.maximum(rowblk(b, c) * hp_blocks - 1, 0), 0)),
                  pl.BlockSpec((CONV_HALO, CONV_CH), lambda b, c: (jnp.maximum(rowblk(b, c) * hx_blocks - 1, 0), 1)),
                  pl.BlockSpec((CHUNK, MIX_W), lambda b, c: (rowblk(b, c), 0)),
                  pl.BlockSpec((1, 1, N_GROUPS, N_STATE, GROUP_W), lambda b, c: (b, nc - 1 - c, 0, 0, 0)),
                  full((4, CONV_CH)), full((1, CONV_CH)), full((8, LANES)), full((1, D)),
                  full((4, LANES, LANES)), full((1, POOL_W))] + [ANY_SPEC] * ns,
        out_specs=[pl.BlockSpec((CHUNK, PROJ_W), lambda b, c: (rowblk(b, c), 0)),
                   full((4, CONV_CH)), full((1, CONV_CH)), full((8, LANES)), full((1, D)),
                   full((4, LANES, LANES)), full((1, POOL_W))] + [ANY_SPEC] * ns,
        out_shape=[jax.ShapeDtypeStruct((T, PROJ_W), BF16),
                   jax.ShapeDtypeStruct((4, CONV_CH), F32), jax.ShapeDtypeStruct((1, CONV_CH), F32),
                   jax.ShapeDtypeStruct((8, LANES), F32), jax.ShapeDtypeStruct((1, D), F32),
                   jax.ShapeDtypeStruct((4, LANES, LANES), F32), jax.ShapeDtypeStruct((1, POOL_W), F32)]
        + [jax.ShapeDtypeStruct(v.shape, v.dtype) for v in blocks],
        scratch_shapes=[pltpu.VMEM((POOL_HALO, POOL_W), F32), pltpu.VMEM((CONV_HALO, CONV_CH), F32),
                        pltpu.VMEM((N_GROUPS, N_STATE, GROUP_W), F32), pltpu.VMEM((CHUNK, D), F32),
                        pltpu.VMEM((CHUNK, D), F32)] + _comm_scratch(ns),
        compiler_params=_cparams(),
    )(pm, pm, pm, dym, hstates, cw, cb, hp, gssd, wpool, pscale, *blocks)


def _mlp_fused(x2, ymix, target, mod, g_mlp, g_final, w_out, w_up, w_down, seq):
    T = x2.shape[0]
    tm = min(256, seq)
    tps = seq // tm
    nblk = D_FF // FF_BLK

    def body(x_ref, ym_ref, tg_ref, mod_ref, gm_ref, gf_ref, wo_ref, wu_ref, wd_ref,
             da_ref, dym_ref, dh1_ref, u2_ref, f_ref, dup_ref, ddn_ref, dmod_ref, acc_ref, relu_scr):
        i = pl.program_id(0)

        @pl.when(i == 0)
        def _():
            acc_ref[...] = jnp.zeros_like(acc_ref)

        @pl.when(i % tps == 0)
        def _():
            dmod_ref[...] = jnp.zeros_like(dmod_ref)

        md = mod_ref[0]
        gate_m, shift_f, scale_f, gate_f = md[2:3], md[3:4], md[4:5], md[5:6]
        g_mlp, g_fin = gm_ref[...], gf_ref[...]
        a = jnp.dot(ym_ref[...], wo_ref[...], preferred_element_type=F32)
        h1 = x_ref[...] + gate_m * a
        r2 = lax.rsqrt(jnp.mean(h1 * h1, axis=-1, keepdims=True) + EPS)
        n2 = h1 * r2
        u2 = (n2 * g_mlp) * (1.0 + scale_f) + shift_f
        u2b = u2.astype(BF16)
        u2_ref[...] = u2b
        dn = jnp.zeros((tm, D), F32)
        for j in range(nblk):
            js = slice(j * FF_BLK, (j + 1) * FF_BLK)
            upj = jnp.maximum(jnp.dot(u2b, wu_ref[j], preferred_element_type=F32), 0.0)
            relu_scr[:, js] = upj
            fj = (upj * upj).astype(BF16)
            f_ref[:, js] = fj
            dn = dn + jnp.dot(fj, wd_ref[j], preferred_element_type=F32)
        h2 = h1 + gate_f * dn
        r3 = lax.rsqrt(jnp.mean(h2 * h2, axis=-1, keepdims=True) + EPS)
        n3 = h2 * r3
        err = n3 * g_fin - tg_ref[...]
        loss = 0.5 * jnp.sum(jnp.mean(err * err, axis=-1, keepdims=True), axis=0, keepdims=True)
        dout = err * (1.0 / D)
        d_gfin = jnp.sum(dout * n3, axis=0, keepdims=True)
        dn3 = dout * g_fin
        dh2 = r3 * (dn3 - n3 * jnp.mean(dn3 * n3, axis=-1, keepdims=True))
        d_gate_f = jnp.sum(dh2 * dn, axis=0, keepdims=True)
        ddn = (gate_f * dh2).astype(BF16)
        ddn_ref[...] = ddn
        du2 = jnp.zeros((tm, D), F32)
        for j in range(nblk):
            js = slice(j * FF_BLK, (j + 1) * FF_BLK)
            dfj = lax.dot_general(ddn, wd_ref[j], (((1,), (1,)), ((), ())), preferred_element_type=F32)
            dupj = (dfj * (2.0 * relu_scr[:, js])).astype(BF16)
            dup_ref[:, js] = dupj
            du2 = du2 + lax.dot_general(dupj, wu_ref[j], (((1,), (1,)), ((), ())), preferred_element_type=F32)
        d_scale_f = jnp.sum(du2 * (n2 * g_mlp), axis=0, keepdims=True)
        d_shift_f = jnp.sum(du2, axis=0, keepdims=True)
        d_gmlp = jnp.sum(du2 * (1.0 + scale_f) * n2, axis=0, keepdims=True)
        dn2 = du2 * (g_mlp * (1.0 + scale_f))
        dh1 = dh2 + r2 * (dn2 - n2 * jnp.mean(dn2 * n2, axis=-1, keepdims=True))
        dh1_ref[...] = dh1
        d_gate_m = jnp.sum(dh1 * a, axis=0, keepdims=True)
        da = (gate_m * dh1).astype(BF16)
        da_ref[...] = da
        dym_ref[...] = lax.dot_general(da, wo_ref[...], (((1,), (1,)), ((), ())),
                                       preferred_element_type=F32).astype(BF16)
        dmod_ref[0] += jnp.concatenate([d_gate_m, d_shift_f, d_scale_f, d_gate_f, jnp.zeros((4, D), F32)], axis=0)
        acc_ref[...] += jnp.concatenate([d_gmlp, d_gfin, loss * jnp.ones((1, D), F32), jnp.zeros((5, D), F32)], axis=0)

    whole = pl.BlockSpec(memory_space=pltpu.VMEM)

    def tok(w):
        return pl.BlockSpec((tm, w), lambda i: (i, 0))

    def vec():
        return pl.BlockSpec((1, D), lambda i: (0, 0))

    nb = T // seq
    return pl.pallas_call(
        body, name="mlp_fused", grid=(T // tm,),
        in_specs=[tok(D), tok(MIX_W), tok(D), pl.BlockSpec((1, 8, D), lambda i: (i // tps, 0, 0)), vec(), vec(),
                  whole, whole, whole],
        out_specs=[tok(D), tok(MIX_W), tok(D), tok(D), tok(D_FF), tok(D_FF), tok(D),
                   pl.BlockSpec((1, 8, D), lambda i: (i // tps, 0, 0)), pl.BlockSpec((8, D), lambda i: (0, 0))],
        out_shape=[jax.ShapeDtypeStruct((T, D), BF16), jax.ShapeDtypeStruct((T, MIX_W), BF16),
                   jax.ShapeDtypeStruct((T, D), F32), jax.ShapeDtypeStruct((T, D), BF16),
                   jax.ShapeDtypeStruct((T, D_FF), BF16), jax.ShapeDtypeStruct((T, D_FF), BF16),
                   jax.ShapeDtypeStruct((T, D), BF16), jax.ShapeDtypeStruct((nb, 8, D), F32),
                   jax.ShapeDtypeStruct((8, D), F32)],
        scratch_shapes=[pltpu.VMEM((tm, D_FF), F32)],
        compiler_params=_cparams(),
    )(x2, ymix, target, mod, g_mlp, g_final, w_out, w_up, w_down)


def _in_bwd(x2, dh1, dpb, mod, g_mix, w_cat, seq, blocks):
    T = x2.shape[0]
    tm = min(512, seq)
    tps = seq // tm
    ns = len(blocks)
    steps = T // tm

    def body(*refs):
        x_ref, dh_ref, dpb_ref, mod_ref, g_ref, w_ref = refs[:6]
        bl_refs = refs[6:6 + ns]
        dx_ref, dmod_ref, acc_ref = refs[6 + ns:9 + ns]
        ex_refs = refs[9 + ns:9 + 2 * ns]
        send, recv, loc = refs[9 + 2 * ns:]
        i = pl.program_id(0)
        exchange = _Exchange(bl_refs, ex_refs, send, recv, loc)

        @pl.when(i == 0)
        def _():
            exchange.start()

        @pl.when(i == 0)
        def _():
            acc_ref[...] = jnp.zeros_like(acc_ref)

        @pl.when(i % tps == 0)
        def _():
            dmod_ref[...] = jnp.zeros_like(dmod_ref)

        du = lax.dot_general(dpb_ref[...], w_ref[...], (((1,), (1,)), ((), ())), preferred_element_type=F32)
        x = x_ref[...]
        md = mod_ref[0]
        g = g_ref[...]
        r = lax.rsqrt(jnp.mean(x * x, axis=-1, keepdims=True) + EPS)
        n1 = x * r
        d_scale = jnp.sum(du * (n1 * g), axis=0, keepdims=True)
        d_shift = jnp.sum(du, axis=0, keepdims=True)
        d_g = jnp.sum(du * (1.0 + md[1:2]) * n1, axis=0, keepdims=True)
        dn1 = du * (g * (1.0 + md[1:2]))
        dx_ref[...] = dh_ref[...] + r * (dn1 - n1 * jnp.mean(dn1 * n1, axis=-1, keepdims=True))
        dmod_ref[0] += jnp.concatenate([d_shift, d_scale, jnp.zeros((6, D), F32)], axis=0)
        acc_ref[...] += jnp.concatenate([d_g, jnp.zeros((7, D), F32)], axis=0)

        @pl.when(i == steps - 1)
        def _():
            exchange.finish()

    whole = pl.BlockSpec(memory_space=pltpu.VMEM)
    nb = T // seq
    return pl.pallas_call(
        body, name="in_bwd", grid=(steps,),
        in_specs=[pl.BlockSpec((tm, D), lambda i: (i, 0)), pl.BlockSpec((tm, D), lambda i: (i, 0)),
                  pl.BlockSpec((tm, PROJ_W), lambda i: (i, 0)),
                  pl.BlockSpec((1, 8, D), lambda i: (i // tps, 0, 0)), pl.BlockSpec((1, D), lambda i: (0, 0)),
                  whole] + [ANY_SPEC] * ns,
        out_specs=[pl.BlockSpec((tm, D), lambda i: (i, 0)),
                   pl.BlockSpec((1, 8, D), lambda i: (i // tps, 0, 0)), pl.BlockSpec((8, D), lambda i: (0, 0))]
        + [ANY_SPEC] * ns,
        out_shape=[jax.ShapeDtypeStruct((T, D), F32),
                   jax.ShapeDtypeStruct((nb, 8, D), F32), jax.ShapeDtypeStruct((8, D), F32)]
        + [jax.ShapeDtypeStruct(v.shape, v.dtype) for v in blocks],
        scratch_shapes=_comm_scratch(ns),
        compiler_params=_cparams(),
    )(x2, dh1, dpb, mod, g_mix, w_cat, *blocks)


def _dw_in(u_b, dpb, in_cols):
    T = u_b.shape[0]
    bk = min(512, T)
    nk = T // bk
    starts = [(in_cols * j // LANES) * LANES for j in range(N_DEV)]
    assert all(s + DW_IN_WIN <= PROJ_W and in_cols * (j + 1) <= s + DW_IN_WIN for j, s in enumerate(starts))

    def body(u_ref, d_ref, o_ref, acc):
        k = pl.program_id(0)

        @pl.when(k == 0)
        def _():
            acc[...] = jnp.zeros_like(acc)

        ut = u_ref[...].T
        for j in range(N_DEV):
            acc[j] += jnp.dot(ut, d_ref[:, starts[j]:starts[j] + DW_IN_WIN], preferred_element_type=F32)

        @pl.when(k == nk - 1)
        def _():
            for j in range(N_DEV):
                off = in_cols * j - starts[j]
                o_ref[j] = acc[j][:, off:off + in_cols].astype(BF16)

    return pl.pallas_call(
        body, name="dw_in", grid=(nk,),
        in_specs=[pl.BlockSpec((bk, D), lambda k: (k, 0)), pl.BlockSpec((bk, PROJ_W), lambda k: (k, 0))],
        out_specs=pl.BlockSpec((N_DEV, D, in_cols), lambda k: (0, 0, 0)),
        out_shape=jax.ShapeDtypeStruct((N_DEV, D, in_cols), BF16),
        scratch_shapes=[pltpu.VMEM((N_DEV, D, DW_IN_WIN), F32)],
        compiler_params=_cparams(),
    )(u_b, dpb)


def _dw_blocks(a, b, name, by_rows, per_step=1):
    T, M = a.shape
    N = b.shape[1]
    bk = min(1024, T)
    nk = T // bk
    whole = pl.BlockSpec(memory_space=pltpu.VMEM)
    if by_rows:
        rows = M // N_DEV
        am = rows * per_step
        nblk = N_DEV // per_step
        a_spec, b_spec = pl.BlockSpec((bk, am), lambda i, k: (k, i)), whole
        out_blk, acc_shape = (per_step, rows, N), (am, N)
    else:
        cols = N // N_DEV
        nblk = N_DEV
        a_spec, b_spec = whole, pl.BlockSpec((bk, cols), lambda i, k: (k, i))
        out_blk, acc_shape = (1, M, cols), (M, cols)

    def body(a_ref, b_ref, o_ref, acc):
        k = pl.program_id(1)

        @pl.when(k == 0)
        def _():
            acc[...] = jnp.zeros_like(acc)

        tok = pl.ds(pl.multiple_of(k * bk, bk), bk)
        a_blk = a_ref[...] if by_rows else a_ref[tok, :]
        b_blk = b_ref[tok, :] if by_rows else b_ref[...]
        acc[...] += lax.dot_general(a_blk, b_blk, (((0,), (0,)), ((), ())), preferred_element_type=F32)

        @pl.when(k == nk - 1)
        def _():
            o_ref[...] = acc[...].reshape(out_blk).astype(BF16)

    return pl.pallas_call(
        body, name=name, grid=(nblk, nk), in_specs=[a_spec, b_spec],
        out_specs=pl.BlockSpec(out_blk, lambda i, k: (i, 0, 0)),
        out_shape=jax.ShapeDtypeStruct((N_DEV,) + out_blk[1:], BF16),
        scratch_shapes=[pltpu.VMEM(acc_shape, F32)],
        compiler_params=_cparams(),
    )(a, b)


def _adam_parts(parts, w, m, v, name):
    rows, cols = w.shape
    br = rows
    for cand in range(rows, 15, -16):
        if rows % cand == 0 and cand * cols * 4 <= ADAM_BLOCK_BYTES:
            br = cand
            break

    def body(p_ref, w_ref, m_ref, v_ref, g_out, dl_out, m_out, v_out):
        g = p_ref[0].astype(F32)
        for k in range(1, N_DEV):
            g = g + p_ref[k].astype(F32)
        g_out[...] = g
        dl, mn, vn = _adam_math(w_ref[...], g, m_ref[...], v_ref[...])
        dl_out[...] = dl
        m_out[...] = mn
        v_out[...] = vn

    wspec = pl.BlockSpec((br, cols), lambda i: (i, 0))
    return pl.pallas_call(
        body, name=name, grid=(rows // br,),
        in_specs=[pl.BlockSpec((N_DEV, br, cols), lambda i: (0, i, 0)), wspec, wspec, wspec],
        out_specs=[wspec] * 4, out_shape=[jax.ShapeDtypeStruct((rows, cols), F32)] * 4,
        compiler_params=_cparams(),
    )(parts, w, m, v)


def _adam_plain(g, w, m, v, name):
    def body(g_ref, w_ref, m_ref, v_ref, dl_out, m_out, v_out):
        dl, mn, vn = _adam_math(w_ref[...], g_ref[...], m_ref[...], v_ref[...])
        dl_out[...] = dl
        m_out[...] = mn
        v_out[...] = vn

    return pl.pallas_call(body, name=name, out_shape=[jax.ShapeDtypeStruct(w.shape, F32)] * 3,
                          compiler_params=_cparams())(g, w, m, v)


_SMALL = [("b_ada", 48), ("_dmod1", 48), ("g_mix", 8), ("conv_w", 48), ("conv_b", 12), ("heads", 8), ("g_ssd", 8),
          ("pool_scale", 4), ("g_mlp", 8), ("g_final", 8), ("loss", 8)]
_SMALL_OFF = {}
_off = 0
for _n, _r in _SMALL:
    _SMALL_OFF[_n] = (_off, _r)
    _off += _r
SMALL_ROWS = _off


def _small_sum_adam(gathered, w, m, v):
    nd = _SMALL_OFF["_dmod1"][0]

    def body(p_ref, w_ref, m_ref, v_ref, g_out, dl_out, m_out, v_out):
        tot = p_ref[0]
        for k in range(1, N_DEV):
            tot = tot + p_ref[k]
        g = jnp.concatenate([tot[0:nd] + tot[nd:2 * nd], tot[nd:]], axis=0)
        g_out[...] = g
        dl, mn, vn = _adam_math(w_ref[...], g, m_ref[...], v_ref[...])
        dl_out[...] = dl
        m_out[...] = mn
        v_out[...] = vn

    return pl.pallas_call(body, name="small_sum_adam",
                          out_shape=[jax.ShapeDtypeStruct((SMALL_ROWS, LANES), F32)] * 4,
                          compiler_params=_cparams())(gathered, w, m, v)


def _rows(a, n):
    flat = a.reshape(-1)
    return jnp.pad(flat, (0, n * LANES - flat.shape[0])).reshape(n, LANES)


def _pack_small(named):
    parts = []
    for name, r in _SMALL:
        parts.append(_rows(named[name], r) if name in named else jnp.zeros((r, LANES), F32))
    return jnp.concatenate(parts, axis=0)


def _unpack(pack, name, shape):
    off, r = _SMALL_OFF[name]
    size = 1
    for s in shape:
        size *= s
    return pack[off:off + r].reshape(-1)[:size].reshape(shape)


def kernel(x, c, w_ada, b_ada, g_mix, w_in, conv_w, conv_b, dt_bias, a_log, d_skip, g_ssd, w_pool, pool_scale, w_out, g_mlp, w_up, w_down, g_final, loss_target, m_w_ada, m_b_ada, m_g_mix, m_w_in, m_conv_w, m_conv_b, m_dt_bias, m_a_log, m_d_skip, m_g_ssd, m_w_pool, m_pool_scale, m_w_out, m_g_mlp, m_w_up, m_w_down, m_g_final, v_w_ada, v_b_ada, v_g_mix, v_w_in, v_conv_w, v_conv_b, v_dt_bias, v_a_log, v_d_skip, v_g_ssd, v_w_pool, v_pool_scale, v_w_out, v_g_mlp, v_w_up, v_w_down, v_g_final):
    nb, seq, _ = x.shape
    T = nb * seq
    me = 4 * lax.axis_index("x") + 2 * lax.axis_index("y") + lax.axis_index("c")
    in_cols = w_in.shape[2]
    ada_cols = w_ada.shape[2]
    cw_cols = conv_w.shape[2]

    small_f = jnp.concatenate([_rows(c, 16), _rows(conv_w[0], 8)], axis=0)
    small_g, win_g = _all_gather([small_f, w_in[0].astype(BF16)], "ag_first")
    c_all = small_g[:, :16].reshape(N_DEV * nb, D)
    cw_full = small_g[:, 16:24].reshape(N_DEV, -1)[:, :4 * cw_cols].reshape(N_DEV, 4, cw_cols)
    cw_full = cw_full.transpose(1, 0, 2).reshape(4, CONV_CH)

    b_slice = lax.dynamic_slice(b_ada, (0, me * ada_cols), (1, ada_cols))
    mod_cols = _ada_fwd(c_all, w_ada[0], b_slice)
    (mod_g,) = _all_gather([mod_cols], "ag_mod")
    mod_all = mod_g.transpose(1, 0, 2).reshape(N_DEV * nb, 6, D)
    mod_mine = lax.dynamic_slice(mod_all, (me * nb, 0, 0), (nb, 6, D))
    mod = jnp.pad(mod_mine, ((0, 0), (0, 2), (0, 0)))

    x2 = x.reshape(T, D)
    tg2 = loss_target.reshape(T, D)
    heads = jnp.pad(jnp.concatenate([dt_bias, a_log, d_skip], axis=0), ((0, 5), (0, LANES - N_HEADS)))
    wpool_b = w_pool[0]
    u_b, pm, w_cat = _mix_in(x2, mod, g_mix, win_g, seq)
    ymix, hstates, wout_g, wup_g, wdn_g = _mixer_fwd(
        pm, cw_full, conv_b, heads, g_ssd, wpool_b, pool_scale, nb, seq,
        [w_out[0].astype(BF16), w_up[0].astype(BF16), w_down[0].astype(BF16)])
    da_b, dym, dh1, u2_b, f_b, dup_b, ddn_b, dmod_a, acc_a = _mlp_fused(
        x2, ymix, tg2, mod, g_mlp, g_final.reshape(1, D), wout_g.reshape(MIX_W, D), wup_g, wdn_g, seq)

    gout_p = _dw_blocks(ymix, da_b, "dw_out", True, per_step=4)
    gup_p = _dw_blocks(u2_b, dup_b, "dw_up", False)
    gdn_p = _dw_blocks(f_b, ddn_b, "dw_down", True)
    dpb, d_cw, d_cb, d_heads, d_gssd, d_wpool, d_pscale, gout_r, gup_r, gdn_r = _mixer_bwd(
        pm, dym, hstates, cw_full, conv_b, heads, g_ssd, wpool_b, pool_scale, nb, seq, [gout_p, gup_p, gdn_p])
    gin_p = _dw_in(u_b, dpb, in_cols)
    grad_x2, dmod_b, acc_b, gin_r = _in_bwd(x2, dh1, dpb, mod, g_mix, w_cat, seq, [gin_p])

    g_in, d_in, nm_in, nv_in = _adam_parts(gin_r, w_in[0], m_w_in[0], v_w_in[0], "adam_w_in")
    g_out, d_out, nm_out, nv_out = _adam_parts(gout_r, w_out[0], m_w_out[0], v_w_out[0], "adam_w_out")
    g_up, d_up, nm_up, nv_up = _adam_parts(gup_r, w_up[0], m_w_up[0], v_w_up[0], "adam_w_up")
    g_dn, d_dn, nm_dn, nv_dn = _adam_parts(gdn_r, w_down[0], m_w_down[0], v_w_down[0], "adam_w_down")

    dmod = jnp.concatenate([dmod_b[:, 0:2], dmod_a[:, 0:4]], axis=1)
    small_named = {
        "b_ada": dmod[0], "_dmod1": dmod[1], "g_mix": acc_b[0], "conv_w": d_cw, "conv_b": d_cb, "heads": d_heads,
        "g_ssd": d_gssd, "pool_scale": d_pscale, "g_mlp": acc_a[0], "g_final": acc_a[1],
        "loss": acc_a[2, 0:1],
    }
    small_all, wpool_parts = _all_gather([_pack_small(small_named), d_wpool.reshape(4 * LANES, LANES)], "ag_small_bwd")
    zeros16 = jnp.zeros((1, N_HEADS), F32)

    def pack_params(b_, gmix_, cb_, dtb_, al_, ds_, gs_, ps_, gm_, gf_):
        hd = jnp.pad(jnp.concatenate([dtb_, al_, ds_], axis=0), ((0, 5), (0, LANES - N_HEADS)))
        return _pack_small({"b_ada": b_, "g_mix": gmix_, "conv_b": cb_, "heads": hd, "g_ssd": gs_,
                            "pool_scale": ps_, "g_mlp": gm_, "g_final": gf_})

    w_pack = pack_params(b_ada, g_mix, conv_b, dt_bias, a_log, d_skip, g_ssd, pool_scale, g_mlp, g_final)
    m_pack = pack_params(m_b_ada, m_g_mix, m_conv_b, m_dt_bias, m_a_log, m_d_skip, m_g_ssd, m_pool_scale,
                         m_g_mlp, m_g_final)
    v_pack = pack_params(v_b_ada, v_g_mix, v_conv_b, v_dt_bias, v_a_log, v_d_skip, v_g_ssd, v_pool_scale,
                         v_g_mlp, v_g_final)
    pool2 = (4 * LANES, LANES)
    wpool_outs = _adam_parts(wpool_parts, w_pool.reshape(pool2), m_w_pool.reshape(pool2), v_w_pool.reshape(pool2),
                             "adam_w_pool")
    g_pack, d_pack, nm_pack, nv_pack = _small_sum_adam(small_all, w_pack, m_pack, v_pack)
    loss = _unpack(g_pack, "loss", (1,))[0]

    g_cw_full = _unpack(g_pack, "conv_w", (4, CONV_CH))
    g_cw = lax.dynamic_slice(g_cw_full, (0, me * cw_cols), (4, cw_cols))
    d_cwp, nm_cwp, nv_cwp = _adam_plain(g_cw, conv_w[0], m_conv_w[0], v_conv_w[0], "adam_conv_w")

    dmod_all = small_all[:, 0:96].reshape(N_DEV * nb, 6 * D)
    dmod_slice = lax.dynamic_slice(dmod_all, (0, me * ada_cols), (N_DEV * nb, ada_cols))
    g_ada, d_ada, nm_ada, nv_ada = _ada_bwd_adam(c_all, dmod_slice, w_ada[0], m_w_ada[0], v_w_ada[0])

    def small_outs(pack, wpool):
        hd = _unpack(pack, "heads", (8, LANES))
        return {
            "b_ada": _unpack(pack, "b_ada", (1, 6 * D)), "g_mix": _unpack(pack, "g_mix", (1, D)),
            "conv_b": _unpack(pack, "conv_b", (1, CONV_CH)), "dt_bias": hd[0:1, :N_HEADS], "a_log": hd[1:2, :N_HEADS],
            "d_skip": hd[2:3, :N_HEADS], "g_ssd": _unpack(pack, "g_ssd", (1, D)),
            "w_pool": wpool.reshape(1, 4, LANES, LANES), "pool_scale": _unpack(pack, "pool_scale", (1, POOL_W)),
            "g_mlp": _unpack(pack, "g_mlp", (1, D)), "g_final": _unpack(pack, "g_final", (D,)),
        }

    def big_outs(ada, win, cwp, wout, wup, wdn):
        return {"w_ada": ada[None], "w_in": win.reshape(1, D, in_cols), "conv_w": cwp[None], "w_out": wout[None],
                "w_up": wup[None], "w_down": wdn[None]}

    order = ["w_ada", "b_ada", "g_mix", "w_in", "conv_w", "conv_b", "dt_bias", "a_log", "d_skip", "g_ssd", "w_pool",
             "pool_scale", "w_out", "g_mlp", "w_up", "w_down", "g_final"]
    groups = [
        {**small_outs(g_pack, wpool_outs[0]), **big_outs(g_ada, g_in, g_cw, g_out, g_up, g_dn)},
        {**small_outs(d_pack, wpool_outs[1]), **big_outs(d_ada, d_in, d_cwp, d_out, d_up, d_dn)},
        {**small_outs(nm_pack, wpool_outs[2]), **big_outs(nm_ada, nm_in, nm_cwp, nm_out, nm_up, nm_dn)},
        {**small_outs(nv_pack, wpool_outs[3]), **big_outs(nv_ada, nv_in, nv_cwp, nv_out, nv_up, nv_dn)},
    ]
    outs = [loss, grad_x2.reshape(nb, seq, D)]
    for grp in groups:
        outs += [grp[n] for n in order]
    return tuple(outs)
```

```python
import functools

import jax
import jax.numpy as jnp
from jax import lax
from jax.experimental import pallas as pl
from jax.experimental.pallas import tpu as pltpu

F32, BF16 = jnp.float32, jnp.bfloat16
MESH = pl.DeviceIdType.MESH
N_DEV = 8
D = 1024
LANES = 128
CHUNK = 128
POOL_W = 512
WINDOWS = (2, 4, 8, 16)
N_HEADS = 16
HEAD_DIM = 64
N_GROUPS = 2
GROUP_W = 512
N_STATE = 128
CONV_CH = 1536
OFF_Z, OFF_XBC, OFF_DT, IN_W = 512, 1536, 3072, 3088
PROJ_W = OFF_DT + LANES
MIX_W = 1536
D_FF = 4096
FF_BLK = 512
EPS = 1e-5
LR, B1, B2, AEPS, WD, STEP = 0.001, 0.9, 0.999, 1e-08, 0.01, 10
POOL_HALO = 16
CONV_HALO = 8
VMEM_LIMIT = 56 << 20
ADAM_BLOCK_BYTES = 1 << 20
DW_IN_WIN = 512


def _cparams(**kw):
    return pltpu.CompilerParams(vmem_limit_bytes=VMEM_LIMIT, **kw)


def _mm(a, b):
    return jnp.dot(a.astype(BF16), b.astype(BF16), preferred_element_type=F32)


def _mm_nt(a, b):
    return lax.dot_general(a.astype(BF16), b.astype(BF16), (((1,), (1,)), ((), ())), preferred_element_type=F32)


def _mm_tn(a, b):
    return lax.dot_general(a.astype(BF16), b.astype(BF16), (((0,), (0,)), ((), ())), preferred_element_type=F32)


def _split_bf16(v, terms):
    parts, rest = [], v
    for t in range(terms):
        p = rest.astype(BF16)
        parts.append(p)
        if t + 1 < terms:
            rest = rest - p.astype(F32)
    return parts


def _dot01(a, b, terms, split_lhs=True):
    if split_lhs:
        bb = b.astype(BF16)
        prods = [jnp.dot(p, bb, preferred_element_type=F32) for p in _split_bf16(a, terms)]
    else:
        ab = a.astype(BF16)
        prods = [jnp.dot(ab, p, preferred_element_type=F32) for p in _split_bf16(b, terms)]
    out = prods[0]
    for q in prods[1:]:
        out = out + q
    return out


def _sigmoid(v):
    return 1.0 / (1.0 + jnp.exp(-v))


def _expand_mat():
    r = lax.broadcasted_iota(jnp.int32, (LANES, D), 0)
    c = lax.broadcasted_iota(jnp.int32, (LANES, D), 1)
    return (r == c // HEAD_DIM).astype(F32)


def _reduce_mat():
    r = lax.broadcasted_iota(jnp.int32, (D, LANES), 0)
    c = lax.broadcasted_iota(jnp.int32, (D, LANES), 1)
    return (c == r // HEAD_DIM).astype(F32)


def _pos():
    return lax.axis_index("x"), lax.axis_index("y"), lax.axis_index("c")


class _Gather:
    def __init__(self, x_refs, o_refs, send, recv, loc):
        self.x_refs, self.o_refs, self.send, self.recv, self.loc = x_refs, o_refs, send, recv, loc
        self.n = len(x_refs)
        x, y, c = _pos()
        self.c = c
        self.me, self.sib = (x, y, c), (x, y, 1 - c)
        self.chips = [(1 - x, y), (x, 1 - y), (1 - x, 1 - y)]

    def _cp(self, a, k, block, to, src=None):
        dst = self.o_refs[a].at[4 * block[0] + 2 * block[1] + block[2]]
        return pltpu.make_async_remote_copy(
            src_ref=dst if src is None else src, dst_ref=dst,
            send_sem=self.send.at[a * 7 + k], recv_sem=self.recv.at[a * 7 + k],
            device_id=to, device_id_type=MESH)

    def _mine(self, a):
        me = self.me
        return pltpu.make_async_copy(self.x_refs[a], self.o_refs[a].at[4 * me[0] + 2 * me[1] + me[2]], self.loc.at[a])

    def _first(self, a):
        cps = [self._cp(a, 0, self.me, self.sib, src=self.x_refs[a])]
        return cps + [self._cp(a, 1 + j, self.me, (*chip, self.c), src=self.x_refs[a])
                      for j, chip in enumerate(self.chips)]

    def _passed(self, a, j):
        return self._cp(a, 4 + j, (*self.chips[j], self.c), self.sib)

    def start(self):
        for a in range(self.n):
            self._mine(a).start()
            for cp in self._first(a):
                cp.start()

    def forward(self):
        for j, chip in enumerate(self.chips):
            for a in range(self.n):
                self._cp(a, 1 + j, (*chip, self.c), self.me).wait_recv()
                self._passed(a, j).start()

    def finish(self):
        for a in range(self.n):
            self._cp(a, 0, self.sib, self.me).wait_recv()
            for j, chip in enumerate(self.chips):
                self._cp(a, 4 + j, (*chip, 1 - self.c), self.me).wait_recv()
        for a in range(self.n):
            for cp in self._first(a):
                cp.wait_send()
            for j in range(3):
                self._passed(a, j).wait_send()
            self._mine(a).wait()


class _Exchange:
    def __init__(self, x_refs, o_refs, send, recv, loc):
        self.x_refs, self.o_refs, self.send, self.recv, self.loc = x_refs, o_refs, send, recv, loc
        self.n = len(x_refs)
        x, y, c = _pos()
        self.me_i = 4 * x + 2 * y + c
        self.peers = []
        for k in range(1, N_DEV):
            px = 1 - x if (k >> 2) & 1 else x
            py = 1 - y if (k >> 1) & 1 else y
            pc = 1 - c if k & 1 else c
            self.peers.append(((px, py, pc), 4 * px + 2 * py + pc))

    def _mine(self, a):
        return pltpu.make_async_copy(self.x_refs[a].at[self.me_i], self.o_refs[a].at[self.me_i], self.loc.at[a])

    def _cp(self, a, k, landing):
        peer, peer_i = self.peers[k]
        return pltpu.make_async_remote_copy(
            src_ref=self.x_refs[a].at[peer_i], dst_ref=self.o_refs[a].at[landing],
            send_sem=self.send.at[a * 7 + k], recv_sem=self.recv.at[a * 7 + k],
            device_id=peer, device_id_type=MESH)

    def start(self):
        for a in range(self.n):
            self._mine(a).start()
            for k in range(N_DEV - 1):
                self._cp(a, k, self.me_i).start()

    def finish(self):
        for a in range(self.n):
            for k in range(N_DEV - 1):
                self._cp(a, k, self.peers[k][1]).wait_recv()
        for a in range(self.n):
            for k in range(N_DEV - 1):
                self._cp(a, k, self.me_i).wait_send()
            self._mine(a).wait()


def _comm_scratch(n):
    return [pltpu.SemaphoreType.DMA((7 * n,)), pltpu.SemaphoreType.DMA((7 * n,)), pltpu.SemaphoreType.DMA((n,))]


ANY_SPEC = pl.BlockSpec(memory_space=pl.ANY)


def _all_gather(xs, name):
    n = len(xs)

    def body(*refs):
        g = _Gather(refs[:n], refs[n:2 * n], *refs[2 * n:])
        g.start()
        g.forward()
        g.finish()

    return pl.pallas_call(
        body, name=name,
        out_shape=[jax.ShapeDtypeStruct((N_DEV,) + v.shape, v.dtype) for v in xs],
        in_specs=[ANY_SPEC] * n, out_specs=[ANY_SPEC] * n, scratch_shapes=_comm_scratch(n),
    )(*xs)


def _ada_fwd(c_all, w_ada, b_slice):
    def body(c_ref, w_ref, b_ref, o_ref):
        cv = c_ref[...]
        act = cv * _sigmoid(cv)
        o_ref[...] = _mm(act, w_ref[...]) + b_ref[...]

    nb, nc = c_all.shape[0], w_ada.shape[1]
    return pl.pallas_call(body, name="ada_fwd", out_shape=jax.ShapeDtypeStruct((nb, nc), F32),
                          compiler_params=_cparams())(c_all, w_ada, b_slice)


def _adam_math(w, g, m, v):
    m = B1 * m + (1.0 - B1) * g
    v = B2 * v + (1.0 - B2) * jnp.square(g)
    m_hat = m / (1.0 - B1 ** STEP)
    v_hat = v / (1.0 - B2 ** STEP)
    delta = -LR * (m_hat / (jnp.sqrt(v_hat) + AEPS) + WD * w)
    return delta, m, v


def _ada_bwd_adam(c_all, dmod_slice, w, m, v):
    rows, cols = w.shape
    br = 256

    def body(c_ref, d_ref, w_ref, m_ref, v_ref, g_out, dl_out, m_out, v_out):
        cv = c_ref[...]
        act = cv * _sigmoid(cv)
        g = _mm_tn(act, d_ref[...])
        g_out[...] = g
        dl, mn, vn = _adam_math(w_ref[...], g, m_ref[...], v_ref[...])
        dl_out[...] = dl
        m_out[...] = mn
        v_out[...] = vn

    nb = c_all.shape[0]
    wspec = pl.BlockSpec((br, cols), lambda i: (i, 0))
    return pl.pallas_call(
        body, name="ada_bwd_adam", grid=(rows // br,),
        in_specs=[pl.BlockSpec((nb, br), lambda i: (0, i)), pl.BlockSpec((nb, cols), lambda i: (0, 0)),
                  wspec, wspec, wspec],
        out_specs=[wspec] * 4, out_shape=[jax.ShapeDtypeStruct((rows, cols), F32)] * 4,
        compiler_params=_cparams(),
    )(c_all, dmod_slice, w, m, v)


def _mix_in(x2, mod, g_mix, win_g, seq):
    T = x2.shape[0]
    tm = min(512, seq)
    tps = seq // tm
    in_cols = win_g.shape[2]

    def body(x_ref, mod_ref, g_ref, wb_ref, u_ref, pm_ref, wc_ref, w_ref):
        @pl.when(pl.program_id(0) == 0)
        def _():
            w_ref[:, OFF_DT:] = jnp.zeros((D, PROJ_W - OFF_DT), BF16)
            for j in range(N_DEV):
                w_ref[:, in_cols * j:in_cols * (j + 1)] = wb_ref[j]
            wc_ref[...] = w_ref[...]

        x = x_ref[...]
        r = lax.rsqrt(jnp.mean(x * x, axis=-1, keepdims=True) + EPS)
        md = mod_ref[0]
        u = (x * r * g_ref[...]) * (1.0 + md[1:2]) + md[0:1]
        ub = u.astype(BF16)
        u_ref[...] = ub
        pm_ref[...] = jnp.dot(ub, w_ref[...], preferred_element_type=F32)

    whole = pl.BlockSpec(memory_space=pltpu.VMEM)
    return pl.pallas_call(
        body, name="mix_in", grid=(T // tm,),
        in_specs=[pl.BlockSpec((tm, D), lambda i: (i, 0)), pl.BlockSpec((1, 8, D), lambda i: (i // tps, 0, 0)),
                  pl.BlockSpec((1, D), lambda i: (0, 0)), whole],
        out_specs=[pl.BlockSpec((tm, D), lambda i: (i, 0)), pl.BlockSpec((tm, PROJ_W), lambda i: (i, 0)),
                   pl.BlockSpec((D, PROJ_W), lambda i: (0, 0))],
        out_shape=[jax.ShapeDtypeStruct((T, D), BF16), jax.ShapeDtypeStruct((T, PROJ_W), F32),
                   jax.ShapeDtypeStruct((D, PROJ_W), BF16)],
        scratch_shapes=[pltpu.VMEM((D, PROJ_W), BF16)],
        compiler_params=_cparams(),
    )(x2, mod, g_mix, win_g)


def _chunk_forward(up, z, ux, dtin, halo_p, halo_x, hprev, cw, cb, hp, gssd, wpool, pscale, t0, y_scr):
    L = CHUNK
    out = {}
    row = lax.broadcasted_iota(jnp.int32, (L, 1), 0)
    t = (t0 + row + 1).astype(F32)
    e = jnp.concatenate([halo_p, up], axis=0)
    s2 = e + pltpu.roll(e, 1, 0)
    s4 = s2 + pltpu.roll(s2, 2, 0)
    s8 = s4 + pltpu.roll(s4, 4, 0)
    s16 = s8 + pltpu.roll(s8, 8, 0)
    sums = (s2, s4, s8, s16)
    p, inv, yp = [], [], []
    for gi, w in enumerate(WINDOWS):
        sl = slice(gi * LANES, (gi + 1) * LANES)
        ic = 1.0 / jnp.minimum(t, float(w))
        pg = sums[gi][POOL_HALO:, sl] * ic - up[:, sl]
        p.append(pg)
        inv.append(ic)
        yp.append(_mm(pg, wpool[gi]))
    out["p"], out["inv"], out["yp"] = p, inv, yp
    out["y_pool"] = jnp.concatenate(yp, axis=1) * pscale
    ex = jnp.concatenate([halo_x, ux], axis=0)
    taps = [pltpu.roll(ex, 3, 0)[CONV_HALO:], pltpu.roll(ex, 2, 0)[CONV_HALO:], pltpu.roll(ex, 1, 0)[CONV_HALO:], ux]
    cv = cb + taps[0] * cw[0:1] + taps[1] * cw[1:2] + taps[2] * cw[2:3] + taps[3] * cw[3:4]
    sg = _sigmoid(cv)
    xbc = cv * sg
    out["taps"], out["cv"], out["sg"] = taps, cv, sg
    X = xbc[:, :D]
    Bm = xbc[:, D:D + N_GROUPS * N_STATE]
    Cm = xbc[:, D + N_GROUPS * N_STATE:]
    pre = dtin + hp[0:1]
    dt = jnp.maximum(pre, 0.0) + jnp.log(1.0 + jnp.exp(-jnp.abs(pre)))
    a_row = -jnp.exp(hp[1:2])
    da = dt * a_row
    ri = lax.broadcasted_iota(jnp.int32, (L, L), 0)
    ci = lax.broadcasted_iota(jnp.int32, (L, L), 1)
    causal = ri >= ci
    cum = _dot01(causal.astype(F32), da, 3, split_lhs=False)
    cum_t = cum.T
    cum_last = cum[L - 1:L]
    eo = jnp.exp(cum)
    dec = jnp.exp(cum_last - cum)
    cd = jnp.exp(cum_last)
    exm = _expand_mat()
    rows8 = jnp.concatenate([cd, hp[2:3], jnp.zeros((6, LANES), F32)], axis=0)
    rep = _dot01(jnp.concatenate([dt, eo, dec, rows8], axis=0), exm, 2)
    dt_rep, eo_rep, dec_rep = rep[0:L], rep[L:2 * L], rep[2 * L:3 * L]
    cd_rep, dskip_rep = rep[3 * L:3 * L + 1], rep[3 * L + 1:3 * L + 2]
    xdt = X * dt_rep
    out.update(X=X, Bm=Bm, Cm=Cm, pre=pre, dt=dt, a_row=a_row, cum=cum, cum_t=cum_t, eo=eo, dec=dec, cd=cd,
               dt_rep=dt_rep, eo_rep=eo_rep, dec_rep=dec_rep, cd_rep=cd_rep, dskip_rep=dskip_rep, xdt=xdt,
               causal=causal, anti=(ri <= ci).astype(F32), exm=exm)
    G, lms, yoff, hnew, xdec = [], [], [], [], []
    for g in range(N_GROUPS):
        gs = slice(g * GROUP_W, (g + 1) * GROUP_W)
        Bg = Bm[:, g * N_STATE:(g + 1) * N_STATE]
        Cg = Cm[:, g * N_STATE:(g + 1) * N_STATE]
        Gg = _mm_nt(Cg, Bg)
        G.append(Gg)
        for hh in range(N_HEADS // N_GROUPS):
            h = g * (N_HEADS // N_GROUPS) + hh
            seg = cum[:, h:h + 1] - cum_t[h:h + 1, :]
            lm = jnp.where(causal, jnp.exp(jnp.minimum(seg, 0.0)), 0.0)
            lms.append(lm)
            hs = slice(h * HEAD_DIM, (h + 1) * HEAD_DIM)
            y_scr[:, hs] = _mm(Gg * lm, xdt[:, hs])
        xd = xdt[:, gs] * dec_rep[:, gs]
        xdec.append(xd)
        sgm = _mm_tn(Bg, xd)
        yoff.append(_mm(Cg, hprev[g]) * eo_rep[:, gs])
        hnew.append(hprev[g] * cd_rep[:, gs] + sgm)
    out.update(G=G, lms=lms, yoff=yoff, hnew=hnew, xdec=xdec)
    y = y_scr[...] + jnp.concatenate(yoff, axis=1) + dskip_rep * X
    sz = _sigmoid(z)
    silz = z * sz
    yz = y * silz
    rg, yn = [], []
    for g in range(N_GROUPS):
        gs = slice(g * GROUP_W, (g + 1) * GROUP_W)
        r = lax.rsqrt(jnp.mean(yz[:, gs] * yz[:, gs], axis=-1, keepdims=True) + EPS)
        rg.append(r)
        yn.append(yz[:, gs] * r)
    yn = jnp.concatenate(yn, axis=1)
    out.update(y=y, sz=sz, silz=silz, rg=rg, yn=yn)
    out["y_ssd"] = yn * gssd
    return out


def _mixer_fwd(pm, cw, cb, hp, gssd, wpool, pscale, nb, seq, shards):
    nc = seq // CHUNK
    ns = len(shards)
    steps = nb * nc
    fwd_step = (3 * steps) // 4

    def body(*refs):
        pm_ref, cw_ref, cb_ref, hp_ref, gs_ref, wp_ref, ps_ref = refs[:7]
        sh_refs = refs[7:7 + ns]
        ym_ref, hs_ref = refs[7 + ns:9 + ns]
        ga_refs = refs[9 + ns:9 + 2 * ns]
        halo_p, halo_x, state, y_scr, send, recv, loc = refs[9 + 2 * ns:]
        c = pl.program_id(1)
        step = pl.program_id(0) * nc + c
        gather = _Gather(sh_refs, ga_refs, send, recv, loc)

        @pl.when(step == 0)
        def _():
            gather.start()

        @pl.when(step == fwd_step)
        def _():
            gather.forward()

        @pl.when(c == 0)
        def _():
            halo_p[...] = jnp.zeros_like(halo_p)
            halo_x[...] = jnp.zeros_like(halo_x)
            state[...] = jnp.zeros_like(state)

        up = pm_ref[:, 0:POOL_W]
        z = pm_ref[:, OFF_Z:OFF_XBC]
        ux = pm_ref[:, OFF_XBC:OFF_DT]
        hprev = [state[0], state[1]]
        hs_ref[0, 0, 0] = hprev[0]
        hs_ref[0, 0, 1] = hprev[1]
        o = _chunk_forward(up, z, ux, pm_ref[:, OFF_DT:], halo_p[...], halo_x[...], hprev, cw_ref[...], cb_ref[...],
                           hp_ref[...], gs_ref[...], wp_ref[...], ps_ref[...], c * CHUNK, y_scr)
        ym_ref[:, 0:POOL_W] = o["y_pool"].astype(BF16)
        ym_ref[:, POOL_W:] = o["y_ssd"].astype(BF16)
        state[0] = o["hnew"][0]
        state[1] = o["hnew"][1]
        halo_p[...] = up[CHUNK - POOL_HALO:]
        halo_x[...] = ux[CHUNK - CONV_HALO:]

        @pl.when(step == steps - 1)
        def _():
            gather.finish()

    def full(shape):
        return pl.BlockSpec(shape, lambda b, c: (0,) * len(shape))

    T = nb * seq
    return pl.pallas_call(
        body, name="mixer_fwd", grid=(nb, nc),
        in_specs=[pl.BlockSpec((CHUNK, PROJ_W), lambda b, c: (b * nc + c, 0)),
                  full((4, CONV_CH)), full((1, CONV_CH)), full((8, LANES)), full((1, D)),
                  full((4, LANES, LANES)), full((1, POOL_W))] + [ANY_SPEC] * ns,
        out_specs=[pl.BlockSpec((CHUNK, MIX_W), lambda b, c: (b * nc + c, 0)),
                   pl.BlockSpec((1, 1, N_GROUPS, N_STATE, GROUP_W), lambda b, c: (b, c, 0, 0, 0))] + [ANY_SPEC] * ns,
        out_shape=[jax.ShapeDtypeStruct((T, MIX_W), BF16),
                   jax.ShapeDtypeStruct((nb, nc, N_GROUPS, N_STATE, GROUP_W), F32)]
        + [jax.ShapeDtypeStruct((N_DEV,) + v.shape, v.dtype) for v in shards],
        scratch_shapes=[pltpu.VMEM((POOL_HALO, POOL_W), F32), pltpu.VMEM((CONV_HALO, CONV_CH), F32),
                        pltpu.VMEM((N_GROUPS, N_STATE, GROUP_W), F32), pltpu.VMEM((CHUNK, D), F32)] + _comm_scratch(ns),
        compiler_params=_cparams(),
    )(pm, cw, cb, hp, gssd, wpool, pscale, *shards)


def _mixer_bwd(pm, dym, hstates, cw, cb, hp, gssd, wpool, pscale, nb, seq, blocks):
    nc = seq // CHUNK
    hpg = N_HEADS // N_GROUPS
    ns = len(blocks)
    steps = nb * nc

    def body(*refs):
        (pm_ref, hpool_ref, hxbc_ref, dy_ref, hs_ref, cw_ref, cb_ref, hp_ref, gs_ref, wp_ref, ps_ref) = refs[:11]
        bl_refs = refs[11:11 + ns]
        dpm_ref, dconv_ref, dhp_ref, dvec_ref, dwp_ref = refs[11 + ns:16 + ns]
        ex_refs = refs[16 + ns:16 + 2 * ns]
        nxt_q, nxt_cv, rstate, y_scr, dx_scr, send, recv, loc = refs[16 + 2 * ns:]
        b = pl.program_id(0)
        ci = pl.program_id(1)
        c = nc - 1 - ci
        exchange = _Exchange(bl_refs, ex_refs, send, recv, loc)

        @pl.when((b == 0) & (ci == 0))
        def _():
            exchange.start()

        @pl.when((b == 0) & (ci == 0))
        def _():
            for r in (dconv_ref, dhp_ref, dvec_ref, dwp_ref):
                r[...] = jnp.zeros_like(r)

        @pl.when(ci == 0)
        def _():
            nxt_q[...] = jnp.zeros_like(nxt_q)
            nxt_cv[...] = jnp.zeros_like(nxt_cv)
            rstate[...] = jnp.zeros_like(rstate)

        first = (c > 0).astype(F32)
        up = pm_ref[:, 0:POOL_W]
        z = pm_ref[:, OFF_Z:OFF_XBC]
        ux = pm_ref[:, OFF_XBC:OFF_DT]
        halo_p = hpool_ref[...] * first
        halo_x = hxbc_ref[...] * first
        hprev = [hs_ref[0, 0, 0], hs_ref[0, 0, 1]]
        cw, cb, hp, gssd, wpool, pscale = cw_ref[...], cb_ref[...], hp_ref[...], gs_ref[...], wp_ref[...], ps_ref[...]
        o = _chunk_forward(up, z, ux, pm_ref[:, OFF_DT:], halo_p, halo_x, hprev, cw, cb, hp, gssd, wpool, pscale,
                           c * CHUNK, y_scr)
        L = CHUNK
        dy_pool = dy_ref[:, 0:POOL_W].astype(F32)
        dy_ssd = dy_ref[:, POOL_W:].astype(F32)

        dvec_ref[1:2, 0:POOL_W] += jnp.sum(dy_pool * jnp.concatenate(o["yp"], axis=1), axis=0, keepdims=True)
        dyp = dy_pool * pscale
        qs = []
        dps = []
        for gi in range(len(WINDOWS)):
            sl = slice(gi * LANES, (gi + 1) * LANES)
            dwp_ref[gi] += _mm_tn(o["p"][gi], dyp[:, sl])
            dpg = _mm_nt(dyp[:, sl], wpool[gi])
            dps.append(dpg)
            qs.append(dpg * o["inv"][gi])
        q = jnp.concatenate(qs, axis=1)
        e = jnp.concatenate([q, nxt_q[...]], axis=0)
        n = L + POOL_HALO
        s2 = e + pltpu.roll(e, n - 1, 0)
        s4 = s2 + pltpu.roll(s2, n - 2, 0)
        s8 = s4 + pltpu.roll(s4, n - 4, 0)
        s16 = s8 + pltpu.roll(s8, n - 8, 0)
        sums = (s2, s4, s8, s16)
        for gi in range(len(WINDOWS)):
            sl = slice(gi * LANES, (gi + 1) * LANES)
            dpm_ref[:, sl] = (sums[gi][:L, sl] - dps[gi]).astype(BF16)
        nxt_q[...] = q[:POOL_HALO]

        yn, y, silz, sz = o["yn"], o["y"], o["silz"], o["sz"]
        dvec_ref[0:1] += jnp.sum(dy_ssd * yn, axis=0, keepdims=True)
        dyn = dy_ssd * gssd
        dyz = []
        for g in range(N_GROUPS):
            gs = slice(g * GROUP_W, (g + 1) * GROUP_W)
            mean = jnp.mean(dyn[:, gs] * yn[:, gs], axis=-1, keepdims=True)
            dyz.append(o["rg"][g] * (dyn[:, gs] - yn[:, gs] * mean))
        dyz = jnp.concatenate(dyz, axis=1)
        dyv = dyz * silz
        dpm_ref[:, OFF_Z:OFF_XBC] = (dyz * y * (sz * (1.0 + z * (1.0 - sz)))).astype(BF16)

        X, Bm, Cm, xdt = o["X"], o["Bm"], o["Cm"], o["xdt"]
        exm = o["exm"]
        rdm = _reduce_mat()
        lane = lax.broadcasted_iota(jnp.int32, (1, LANES), 1)
        sub = lax.broadcasted_iota(jnp.int32, (LANES, 1), 0)
        dX = o["dskip_rep"] * dyv
        yoff_full = jnp.concatenate(o["yoff"], axis=1)
        rs = jnp.zeros((L, LANES), F32)
        cs_t = jnp.zeros((LANES, L), F32)
        dBs, dCs = [], []
        rh_sums = []
        ddec = []
        for g in range(N_GROUPS):
            gs = slice(g * GROUP_W, (g + 1) * GROUP_W)
            Bg = Bm[:, g * N_STATE:(g + 1) * N_STATE]
            Cg = Cm[:, g * N_STATE:(g + 1) * N_STATE]
            Gg = o["G"][g]
            R = rstate[g]
            dwm = dyv[:, gs] * o["eo_rep"][:, gs]
            dC = _mm_nt(dwm, hprev[g])
            dH = _mm_tn(Cg, dwm)
            dG = jnp.zeros((L, L), F32)
            for hh in range(hpg):
                h = g * hpg + hh
                hs = slice(h * HEAD_DIM, (h + 1) * HEAD_DIM)
                lm = o["lms"][h]
                m_h = Gg * lm
                dM = _mm_nt(dyv[:, hs], xdt[:, hs])
                dx_scr[:, hs] = _mm_tn(m_h, dyv[:, hs])
                qm = dM * m_h
                rs = rs + jnp.sum(qm, axis=1, keepdims=True) * (lane == h).astype(F32)
                cs_t = cs_t + (sub == h).astype(F32) * jnp.sum(qm, axis=0, keepdims=True)
                dG = dG + dM * lm
            dC = dC + _mm(dG, Bg)
            dB = _mm_tn(dG, Cg)
            zx = _mm(Bg, R)
            dxdt_state = zx * o["dec_rep"][:, gs]
            ddec.append(zx * xdt[:, gs])
            dB = dB + _mm_nt(o["xdec"][g], R)
            rh_sums.append(jnp.sum(R * hprev[g], axis=0, keepdims=True))
            rstate[g] = dH + o["cd_rep"][:, gs] * R
            dx_scr[:, gs] = dx_scr[:, gs] + dxdt_state
            dBs.append(dB)
            dCs.append(dC)
        dxdt = dx_scr[...]
        tail = jnp.concatenate([jnp.sum(dyv * X, axis=0, keepdims=True), jnp.concatenate(rh_sums, axis=1),
                                jnp.zeros((6, D), F32)], axis=0)
        red = _dot01(jnp.concatenate([dyv * yoff_full, jnp.concatenate(ddec, axis=1), dxdt * X, tail], axis=0), rdm, 2)
        d_dskip, dcd_row = red[3 * L:3 * L + 1], red[3 * L + 1:3 * L + 2]
        ddec_h = red[L:2 * L] * o["dec"]
        dcum_last = jnp.sum(ddec_h, axis=0, keepdims=True) + dcd_row * o["cd"]
        dcum = red[0:L] + rs - cs_t.T - ddec_h + (sub == L - 1).astype(F32) * dcum_last
        dda = _dot01(o["anti"], dcum, 3, split_lhs=False)
        ddt_v = dda * o["a_row"] + red[2 * L:3 * L]
        dX = dX + dxdt * o["dt_rep"]
        head_mask = (lane < N_HEADS).astype(F32)
        d_alog = jnp.sum(dda * o["dt"], axis=0, keepdims=True) * o["a_row"] * head_mask
        dpre = ddt_v * _sigmoid(o["pre"]) * head_mask
        dpm_ref[:, OFF_DT:] = dpre.astype(BF16)
        d_dtb = jnp.sum(dpre, axis=0, keepdims=True)
        dhp_ref[...] += jnp.concatenate([d_dtb, d_alog, d_dskip * head_mask, jnp.zeros((5, LANES), F32)], axis=0)

        dxbc = jnp.concatenate([dX] + dBs + dCs, axis=1)
        sg, cv = o["sg"], o["cv"]
        dcv = dxbc * (sg * (1.0 + cv * (1.0 - sg)))
        dconv_ref[0:5] += jnp.concatenate(
            [jnp.sum(dcv * o["taps"][k], axis=0, keepdims=True) for k in range(4)]
            + [jnp.sum(dcv, axis=0, keepdims=True)], axis=0)
        e2 = jnp.concatenate([dcv, nxt_cv[...]], axis=0)
        n2 = L + CONV_HALO
        dux = (dcv * cw[3:4] + pltpu.roll(e2, n2 - 1, 0)[:L] * cw[2:3] + pltpu.roll(e2, n2 - 2, 0)[:L] * cw[1:2]
               + pltpu.roll(e2, n2 - 3, 0)[:L] * cw[0:1])
        dpm_ref[:, OFF_XBC:OFF_DT] = dux.astype(BF16)
        nxt_cv[...] = dcv[:CONV_HALO]

        @pl.when((b == nb - 1) & (ci == nc - 1))
        def _():
            exchange.finish()

    def full(shape):
        return pl.BlockSpec(shape, lambda b, c: (0,) * len(shape))

    def rowblk(b, c):
        return b * nc + (nc - 1 - c)

    hp_blocks = CHUNK // POOL_HALO
    hx_blocks = CHUNK // CONV_HALO
    T = nb * seq
    return pl.pallas_call(
        body, name="mixer_bwd", grid=(nb, nc),
        in_specs=[pl.BlockSpec((CHUNK, PROJ_W), lambda b, c: (rowblk(b, c), 0)),
                  pl.BlockSpec((POOL_HALO, POOL_W), lambda b, c: (jnp.maximum(rowblk(b, c) * hp_blocks - 1, 0), 0)),
                  pl.BlockSpec((CONV_HALO, CONV_CH), lambda b, c: (jnp.maximum(rowblk(b, c) * hx_blocks - 1, 0), 1)),
                  pl.BlockSpec((CHUNK, MIX_W), lambda b, c: (rowblk(b, c), 0)),
                  pl.BlockSpec((1, 1, N_GROUPS, N_STATE, GROUP_W), lambda b, c: (b, nc - 1 - c, 0, 0, 0)),
                  full((4, CONV_CH)), full((1, CONV_CH)), full((8, LANES)), full((1, D)),
                  full((4, LANES, LANES)), full((1, POOL_W))] + [ANY_SPEC] * ns,
        out_specs=[pl.BlockSpec((CHUNK, PROJ_W), lambda b, c: (rowblk(b, c), 0)),
                   full((8, CONV_CH)), full((8, LANES)), full((8, D)), full((4, LANES, LANES))] + [ANY_SPEC] * ns,
        out_shape=[jax.ShapeDtypeStruct((T, PROJ_W), BF16),
                   jax.ShapeDtypeStruct((8, CONV_CH), F32), jax.ShapeDtypeStruct((8, LANES), F32),
                   jax.ShapeDtypeStruct((8, D), F32), jax.ShapeDtypeStruct((4, LANES, LANES), F32)]
        + [jax.ShapeDtypeStruct(v.shape, v.dtype) for v in blocks],
        scratch_shapes=[pltpu.VMEM((POOL_HALO, POOL_W), F32), pltpu.VMEM((CONV_HALO, CONV_CH), F32),
                        pltpu.VMEM((N_GROUPS, N_STATE, GROUP_W), F32), pltpu.VMEM((CHUNK, D), F32),
                        pltpu.VMEM((CHUNK, D), F32)] + _comm_scratch(ns),
        compiler_params=_cparams(),
    )(pm, pm, pm, dym, hstates, cw, cb, hp, gssd, wpool, pscale, *blocks)


def _mlp_fused(x2, ymix, target, mod, g_mlp, g_final, w_out, w_up, w_down, seq):
    T = x2.shape[0]
    tm = min(256, seq)
    tps = seq // tm
    nblk = D_FF // FF_BLK

    def body(x_ref, ym_ref, tg_ref, mod_ref, gm_ref, gf_ref, wo_ref, wu_ref, wd_ref,
             da_ref, dym_ref, dh1_ref, u2_ref, f_ref, dup_ref, ddn_ref, dmod_ref, acc_ref, relu_scr):
        i = pl.program_id(0)

        @pl.when(i == 0)
        def _():
            acc_ref[...] = jnp.zeros_like(acc_ref)

        @pl.when(i % tps == 0)
        def _():
            dmod_ref[...] = jnp.zeros_like(dmod_ref)

        md = mod_ref[0]
        gate_m, shift_f, scale_f, gate_f = md[2:3], md[3:4], md[4:5], md[5:6]
        g_mlp, g_fin = gm_ref[...], gf_ref[...]
        a = jnp.dot(ym_ref[...], wo_ref[...], preferred_element_type=F32)
        h1 = x_ref[...] + gate_m * a
        r2 = lax.rsqrt(jnp.mean(h1 * h1, axis=-1, keepdims=True) + EPS)
        n2 = h1 * r2
        u2 = (n2 * g_mlp) * (1.0 + scale_f) + shift_f
        u2b = u2.astype(BF16)
        u2_ref[...] = u2b
        dn = jnp.zeros((tm, D), F32)
        for j in range(nblk):
            js = slice(j * FF_BLK, (j + 1) * FF_BLK)
            upj = jnp.maximum(jnp.dot(u2b, wu_ref[j], preferred_element_type=F32), 0.0)
            relu_scr[:, js] = upj
            fj = (upj * upj).astype(BF16)
            f_ref[:, js] = fj
            dn = dn + jnp.dot(fj, wd_ref[j], preferred_element_type=F32)
        h2 = h1 + gate_f * dn
        r3 = lax.rsqrt(jnp.mean(h2 * h2, axis=-1, keepdims=True) + EPS)
        n3 = h2 * r3
        err = n3 * g_fin - tg_ref[...]
        loss = 0.5 * jnp.sum(jnp.mean(err * err, axis=-1, keepdims=True), axis=0, keepdims=True)
        dout = err * (1.0 / D)
        d_gfin = jnp.sum(dout * n3, axis=0, keepdims=True)
        dn3 = dout * g_fin
        dh2 = r3 * (dn3 - n3 * jnp.mean(dn3 * n3, axis=-1, keepdims=True))
        d_gate_f = jnp.sum(dh2 * dn, axis=0, keepdims=True)
        ddn = (gate_f * dh2).astype(BF16)
        ddn_ref[...] = ddn
        du2 = jnp.zeros((tm, D), F32)
        for j in range(nblk):
            js = slice(j * FF_BLK, (j + 1) * FF_BLK)
            dfj = lax.dot_general(ddn, wd_ref[j], (((1,), (1,)), ((), ())), preferred_element_type=F32)
            dupj = (dfj * (2.0 * relu_scr[:, js])).astype(BF16)
            dup_ref[:, js] = dupj
            du2 = du2 + lax.dot_general(dupj, wu_ref[j], (((1,), (1,)), ((), ())), preferred_element_type=F32)
        d_scale_f = jnp.sum(du2 * (n2 * g_mlp), axis=0, keepdims=True)
        d_shift_f = jnp.sum(du2, axis=0, keepdims=True)
        d_gmlp = jnp.sum(du2 * (1.0 + scale_f) * n2, axis=0, keepdims=True)
        dn2 = du2 * (g_mlp * (1.0 + scale_f))
        dh1 = dh2 + r2 * (dn2 - n2 * jnp.mean(dn2 * n2, axis=-1, keepdims=True))
        dh1_ref[...] = dh1
        d_gate_m = jnp.sum(dh1 * a, axis=0, keepdims=True)
        da = (gate_m * dh1).astype(BF16)
        da_ref[...] = da
        dym_ref[...] = lax.dot_general(da, wo_ref[...], (((1,), (1,)), ((), ())),
                                       preferred_element_type=F32).astype(BF16)
        dmod_ref[0] += jnp.concatenate([jnp.zeros((2, D), F32), d_gate_m, d_shift_f, d_scale_f, d_gate_f,
                                        jnp.zeros((2, D), F32)], axis=0)
        acc_ref[...] += jnp.concatenate([d_gmlp, d_gfin, loss * jnp.ones((1, D), F32), jnp.zeros((5, D), F32)], axis=0)

    whole = pl.BlockSpec(memory_space=pltpu.VMEM)

    def tok(w):
        return pl.BlockSpec((tm, w), lambda i: (i, 0))

    def vec():
        return pl.BlockSpec((1, D), lambda i: (0, 0))

    nb = T // seq
    return pl.pallas_call(
        body, name="mlp_fused", grid=(T // tm,),
        in_specs=[tok(D), tok(MIX_W), tok(D), pl.BlockSpec((1, 8, D), lambda i: (i // tps, 0, 0)), vec(), vec(),
                  whole, whole, whole],
        out_specs=[tok(D), tok(MIX_W), tok(D), tok(D), tok(D_FF), tok(D_FF), tok(D),
                   pl.BlockSpec((1, 8, D), lambda i: (i // tps, 0, 0)), pl.BlockSpec((8, D), lambda i: (0, 0))],
        out_shape=[jax.ShapeDtypeStruct((T, D), BF16), jax.ShapeDtypeStruct((T, MIX_W), BF16),
                   jax.ShapeDtypeStruct((T, D), F32), jax.ShapeDtypeStruct((T, D), BF16),
                   jax.ShapeDtypeStruct((T, D_FF), BF16), jax.ShapeDtypeStruct((T, D_FF), BF16),
                   jax.ShapeDtypeStruct((T, D), BF16), jax.ShapeDtypeStruct((nb, 8, D), F32),
                   jax.ShapeDtypeStruct((8, D), F32)],
        scratch_shapes=[pltpu.VMEM((tm, D_FF), F32)],
        compiler_params=_cparams(),
    )(x2, ymix, target, mod, g_mlp, g_final, w_out, w_up, w_down)


def _in_bwd(x2, dh1, dpb, mod, g_mix, w_cat, dmod_a, acc_a, seq, blocks):
    T = x2.shape[0]
    tm = min(512, seq)
    tps = seq // tm
    ns = len(blocks)
    steps = T // tm

    def body(*refs):
        x_ref, dh_ref, dpb_ref, mod_ref, g_ref, w_ref, dma_ref, acca_ref = refs[:8]
        bl_refs = refs[8:8 + ns]
        dx_ref, dmod_ref, acc_ref = refs[8 + ns:11 + ns]
        ex_refs = refs[11 + ns:11 + 2 * ns]
        send, recv, loc = refs[11 + 2 * ns:]
        i = pl.program_id(0)
        exchange = _Exchange(bl_refs, ex_refs, send, recv, loc)

        @pl.when(i == 0)
        def _():
            exchange.start()

        @pl.when(i == 0)
        def _():
            acc_ref[...] = acca_ref[...]

        @pl.when(i % tps == 0)
        def _():
            dmod_ref[...] = dma_ref[...]

        du = lax.dot_general(dpb_ref[...], w_ref[...], (((1,), (1,)), ((), ())), preferred_element_type=F32)
        x = x_ref[...]
        md = mod_ref[0]
        g = g_ref[...]
        r = lax.rsqrt(jnp.mean(x * x, axis=-1, keepdims=True) + EPS)
        n1 = x * r
        d_scale = jnp.sum(du * (n1 * g), axis=0, keepdims=True)
        d_shift = jnp.sum(du, axis=0, keepdims=True)
        d_g = jnp.sum(du * (1.0 + md[1:2]) * n1, axis=0, keepdims=True)
        dn1 = du * (g * (1.0 + md[1:2]))
        dx_ref[...] = dh_ref[...] + r * (dn1 - n1 * jnp.mean(dn1 * n1, axis=-1, keepdims=True))
        dmod_ref[0] += jnp.concatenate([d_shift, d_scale, jnp.zeros((6, D), F32)], axis=0)
        acc_ref[...] += jnp.concatenate([jnp.zeros((3, D), F32), d_g, jnp.zeros((4, D), F32)], axis=0)

        @pl.when(i == steps - 1)
        def _():
            exchange.finish()

    whole = pl.BlockSpec(memory_space=pltpu.VMEM)
    nb = T // seq
    return pl.pallas_call(
        body, name="in_bwd", grid=(steps,),
        in_specs=[pl.BlockSpec((tm, D), lambda i: (i, 0)), pl.BlockSpec((tm, D), lambda i: (i, 0)),
                  pl.BlockSpec((tm, PROJ_W), lambda i: (i, 0)),
                  pl.BlockSpec((1, 8, D), lambda i: (i // tps, 0, 0)), pl.BlockSpec((1, D), lambda i: (0, 0)),
                  whole, pl.BlockSpec((1, 8, D), lambda i: (i // tps, 0, 0)), pl.BlockSpec((8, D), lambda i: (0, 0))]
        + [ANY_SPEC] * ns,
        out_specs=[pl.BlockSpec((tm, D), lambda i: (i, 0)),
                   pl.BlockSpec((1, 8, D), lambda i: (i // tps, 0, 0)), pl.BlockSpec((8, D), lambda i: (0, 0))]
        + [ANY_SPEC] * ns,
        out_shape=[jax.ShapeDtypeStruct((T, D), F32),
                   jax.ShapeDtypeStruct((nb, 8, D), F32), jax.ShapeDtypeStruct((8, D), F32)]
        + [jax.ShapeDtypeStruct(v.shape, v.dtype) for v in blocks],
        scratch_shapes=_comm_scratch(ns),
        compiler_params=_cparams(),
    )(x2, dh1, dpb, mod, g_mix, w_cat, dmod_a, acc_a, *blocks)


def _dw_in(u_b, dpb, in_cols):
    T = u_b.shape[0]
    bk = min(512, T)
    nk = T // bk
    starts = [(in_cols * j // LANES) * LANES for j in range(N_DEV)]
    assert all(s + DW_IN_WIN <= PROJ_W and in_cols * (j + 1) <= s + DW_IN_WIN for j, s in enumerate(starts))

    def body(u_ref, d_ref, o_ref, acc):
        k = pl.program_id(0)

        @pl.when(k == 0)
        def _():
            acc[...] = jnp.zeros_like(acc)

        ut = u_ref[...].T
        for j in range(N_DEV):
            acc[j] += jnp.dot(ut, d_ref[:, starts[j]:starts[j] + DW_IN_WIN], preferred_element_type=F32)

        @pl.when(k == nk - 1)
        def _():
            for j in range(N_DEV):
                off = in_cols * j - starts[j]
                o_ref[j] = acc[j][:, off:off + in_cols].astype(BF16)

    return pl.pallas_call(
        body, name="dw_in", grid=(nk,),
        in_specs=[pl.BlockSpec((bk, D), lambda k: (k, 0)), pl.BlockSpec((bk, PROJ_W), lambda k: (k, 0))],
        out_specs=pl.BlockSpec((N_DEV, D, in_cols), lambda k: (0, 0, 0)),
        out_shape=jax.ShapeDtypeStruct((N_DEV, D, in_cols), BF16),
        scratch_shapes=[pltpu.VMEM((N_DEV, D, DW_IN_WIN), F32)],
        compiler_params=_cparams(),
    )(u_b, dpb)


def _dw_blocks(a, b, name, by_rows, per_step=1):
    T, M = a.shape
    N = b.shape[1]
    bk = min(1024, T)
    nk = T // bk
    whole = pl.BlockSpec(memory_space=pltpu.VMEM)
    if by_rows:
        rows = M // N_DEV
        am = rows * per_step
        nblk = N_DEV // per_step
        a_spec, b_spec = pl.BlockSpec((bk, am), lambda i, k: (k, i)), whole
        out_blk, acc_shape = (per_step, rows, N), (am, N)
    else:
        cols = N // N_DEV
        nblk = N_DEV
        a_spec, b_spec = whole, pl.BlockSpec((bk, cols), lambda i, k: (k, i))
        out_blk, acc_shape = (1, M, cols), (M, cols)

    def body(a_ref, b_ref, o_ref, acc):
        k = pl.program_id(1)

        @pl.when(k == 0)
        def _():
            acc[...] = jnp.zeros_like(acc)

        tok = pl.ds(pl.multiple_of(k * bk, bk), bk)
        a_blk = a_ref[...] if by_rows else a_ref[tok, :]
        b_blk = b_ref[tok, :] if by_rows else b_ref[...]
        acc[...] += lax.dot_general(a_blk, b_blk, (((0,), (0,)), ((), ())), preferred_element_type=F32)

        @pl.when(k == nk - 1)
        def _():
            o_ref[...] = acc[...].reshape(out_blk).astype(BF16)

    return pl.pallas_call(
        body, name=name, grid=(nblk, nk), in_specs=[a_spec, b_spec],
        out_specs=pl.BlockSpec(out_blk, lambda i, k: (i, 0, 0)),
        out_shape=jax.ShapeDtypeStruct((N_DEV,) + out_blk[1:], BF16),
        scratch_shapes=[pltpu.VMEM(acc_shape, F32)],
        compiler_params=_cparams(),
    )(a, b)


def _adam_parts(parts, w, m, v, name):
    rows, cols = w.shape
    br = rows
    for cand in range(rows, 15, -16):
        if rows % cand == 0 and cand * cols * 4 <= ADAM_BLOCK_BYTES:
            br = cand
            break

    def body(p_ref, w_ref, m_ref, v_ref, g_out, dl_out, m_out, v_out):
        g = p_ref[0].astype(F32)
        for k in range(1, N_DEV):
            g = g + p_ref[k].astype(F32)
        g_out[...] = g
        dl, mn, vn = _adam_math(w_ref[...], g, m_ref[...], v_ref[...])
        dl_out[...] = dl
        m_out[...] = mn
        v_out[...] = vn

    wspec = pl.BlockSpec((br, cols), lambda i: (i, 0))
    return pl.pallas_call(
        body, name=name, grid=(rows // br,),
        in_specs=[pl.BlockSpec((N_DEV, br, cols), lambda i: (0, i, 0)), wspec, wspec, wspec],
        out_specs=[wspec] * 4, out_shape=[jax.ShapeDtypeStruct((rows, cols), F32)] * 4,
        compiler_params=_cparams(),
    )(parts, w, m, v)


def _adam_plain(g, w, m, v, name):
    def body(g_ref, w_ref, m_ref, v_ref, dl_out, m_out, v_out):
        dl, mn, vn = _adam_math(w_ref[...], g_ref[...], m_ref[...], v_ref[...])
        dl_out[...] = dl
        m_out[...] = mn
        v_out[...] = vn

    return pl.pallas_call(body, name=name, out_shape=[jax.ShapeDtypeStruct(w.shape, F32)] * 3,
                          compiler_params=_cparams())(g, w, m, v)


SMALL_PARAMS = ("b_ada", "g_mix", "conv_b", "dt_bias", "a_log", "d_skip", "g_ssd", "pool_scale", "g_mlp", "g_final")


def _small_adam(gathered, params):
    n_par = len(SMALL_PARAMS)
    nb = gathered[0].shape[1]

    def body(*refs):
        dmod_ref, acc_ref, conv_ref, vec_ref, hd_ref = refs[:5]
        par_refs = refs[5:5 + 3 * n_par]
        out_refs = refs[5 + 3 * n_par:5 + 7 * n_par]
        cw_out, acc_out = refs[5 + 7 * n_par:]

        def total(ref):
            t = ref[0]
            for k in range(1, N_DEV):
                t = t + ref[k]
            return t

        dm = total(dmod_ref)
        dmb = dm[0]
        for b in range(1, nb):
            dmb = dmb + dm[b]
        ac, cv, vc, hd = total(acc_ref), total(conv_ref), total(vec_ref), total(hd_ref)
        cw_out[...] = cv[0:4]
        acc_out[...] = ac
        grads = {
            "b_ada": jnp.concatenate([dmb[r:r + 1] for r in range(6)], axis=1), "g_mix": ac[3:4], "conv_b": cv[4:5],
            "dt_bias": hd[0:1, 0:N_HEADS], "a_log": hd[1:2, 0:N_HEADS], "d_skip": hd[2:3, 0:N_HEADS],
            "g_ssd": vc[0:1], "pool_scale": vc[1:2, 0:POOL_W], "g_mlp": ac[0:1], "g_final": ac[1:2],
        }
        for i, name in enumerate(SMALL_PARAMS):
            w_ref, m_ref, v_ref = par_refs[3 * i:3 * i + 3]
            g = grads[name]
            dl, mn, vn = _adam_math(w_ref[...], g, m_ref[...], v_ref[...])
            g_o, d_o, m_o, v_o = out_refs[4 * i:4 * i + 4]
            g_o[...] = g
            d_o[...] = dl
            m_o[...] = mn
            v_o[...] = vn

    flat = [a for name in SMALL_PARAMS for a in params[name]]
    out_shape = [jax.ShapeDtypeStruct(params[name][0].shape, F32) for name in SMALL_PARAMS for _ in range(4)]
    out_shape += [jax.ShapeDtypeStruct((4, CONV_CH), F32), jax.ShapeDtypeStruct((8, D), F32)]
    return pl.pallas_call(body, name="small_adam", out_shape=out_shape, compiler_params=_cparams())(*gathered, *flat)


def kernel(x, c, w_ada, b_ada, g_mix, w_in, conv_w, conv_b, dt_bias, a_log, d_skip, g_ssd, w_pool, pool_scale, w_out, g_mlp, w_up, w_down, g_final, loss_target, m_w_ada, m_b_ada, m_g_mix, m_w_in, m_conv_w, m_conv_b, m_dt_bias, m_a_log, m_d_skip, m_g_ssd, m_w_pool, m_pool_scale, m_w_out, m_g_mlp, m_w_up, m_w_down, m_g_final, v_w_ada, v_b_ada, v_g_mix, v_w_in, v_conv_w, v_conv_b, v_dt_bias, v_a_log, v_d_skip, v_g_ssd, v_w_pool, v_pool_scale, v_w_out, v_g_mlp, v_w_up, v_w_down, v_g_final):
    nb, seq, _ = x.shape
    T = nb * seq
    me = 4 * lax.axis_index("x") + 2 * lax.axis_index("y") + lax.axis_index("c")
    in_cols = w_in.shape[2]
    ada_cols = w_ada.shape[2]
    cw_cols = conv_w.shape[2]

    c_g, cw_g, win_g = _all_gather([c, conv_w[0], w_in[0].astype(BF16)], "ag_first")
    c_all = c_g.reshape(N_DEV * nb, D)
    cw_full = cw_g.transpose(1, 0, 2).reshape(4, CONV_CH)

    b_slice = lax.dynamic_slice(b_ada, (0, me * ada_cols), (1, ada_cols))
    mod_cols = _ada_fwd(c_all, w_ada[0], b_slice)
    (mod_g,) = _all_gather([mod_cols], "ag_mod")
    mod_all = mod_g.transpose(1, 0, 2).reshape(N_DEV * nb, 6, D)
    mod_mine = lax.dynamic_slice(mod_all, (me * nb, 0, 0), (nb, 6, D))
    mod = jnp.pad(mod_mine, ((0, 0), (0, 2), (0, 0)))

    x2 = x.reshape(T, D)
    tg2 = loss_target.reshape(T, D)
    heads = jnp.pad(jnp.concatenate([dt_bias, a_log, d_skip], axis=0), ((0, 5), (0, LANES - N_HEADS)))
    wpool_b = w_pool[0]
    u_b, pm, w_cat = _mix_in(x2, mod, g_mix, win_g, seq)
    ymix, hstates, wout_g, wup_g, wdn_g = _mixer_fwd(
        pm, cw_full, conv_b, heads, g_ssd, wpool_b, pool_scale, nb, seq,
        [w_out[0].astype(BF16), w_up[0].astype(BF16), w_down[0].astype(BF16)])
    da_b, dym, dh1, u2_b, f_b, dup_b, ddn_b, dmod_a, acc_a = _mlp_fused(
        x2, ymix, tg2, mod, g_mlp, g_final.reshape(1, D), wout_g.reshape(MIX_W, D), wup_g, wdn_g, seq)

    gout_p = _dw_blocks(ymix, da_b, "dw_out", True, per_step=4)
    gup_p = _dw_blocks(u2_b, dup_b, "dw_up", False)
    gdn_p = _dw_blocks(f_b, ddn_b, "dw_down", True)
    dpb, d_conv, d_heads, d_vec, d_wpool, gout_r, gup_r, gdn_r = _mixer_bwd(
        pm, dym, hstates, cw_full, conv_b, heads, g_ssd, wpool_b, pool_scale, nb, seq, [gout_p, gup_p, gdn_p])
    gin_p = _dw_in(u_b, dpb, in_cols)
    grad_x2, dmod, acc, gin_r = _in_bwd(x2, dh1, dpb, mod, g_mix, w_cat, dmod_a, acc_a, seq, [gin_p])

    g_in, d_in, nm_in, nv_in = _adam_parts(gin_r, w_in[0], m_w_in[0], v_w_in[0], "adam_w_in")
    g_out, d_out, nm_out, nv_out = _adam_parts(gout_r, w_out[0], m_w_out[0], v_w_out[0], "adam_w_out")
    g_up, d_up, nm_up, nv_up = _adam_parts(gup_r, w_up[0], m_w_up[0], v_w_up[0], "adam_w_up")
    g_dn, d_dn, nm_dn, nv_dn = _adam_parts(gdn_r, w_down[0], m_w_down[0], v_w_down[0], "adam_w_down")

    dmod_g, acc_g, conv_g, vec_g, heads_g, wpool_parts = _all_gather(
        [dmod, acc, d_conv, d_vec, d_heads, d_wpool.reshape(4 * LANES, LANES)], "ag_small_bwd")
    pool2 = (4 * LANES, LANES)
    wpool_outs = _adam_parts(wpool_parts, w_pool.reshape(pool2), m_w_pool.reshape(pool2), v_w_pool.reshape(pool2),
                             "adam_w_pool")
    small_params = {
        "b_ada": (b_ada, m_b_ada, v_b_ada), "g_mix": (g_mix, m_g_mix, v_g_mix), "conv_b": (conv_b, m_conv_b, v_conv_b),
        "dt_bias": (dt_bias, m_dt_bias, v_dt_bias), "a_log": (a_log, m_a_log, v_a_log),
        "d_skip": (d_skip, m_d_skip, v_d_skip), "g_ssd": (g_ssd, m_g_ssd, v_g_ssd),
        "pool_scale": (pool_scale, m_pool_scale, v_pool_scale), "g_mlp": (g_mlp, m_g_mlp, v_g_mlp),
        "g_final": tuple(a.reshape(1, D) for a in (g_final, m_g_final, v_g_final)),
    }
    small_res = _small_adam([dmod_g, acc_g, conv_g, vec_g, heads_g], small_params)
    g_cw_full, acc_sum = small_res[-2:]
    loss = acc_sum[2, 0]

    g_cw = lax.dynamic_slice(g_cw_full, (0, me * cw_cols), (4, cw_cols))
    d_cwp, nm_cwp, nv_cwp = _adam_plain(g_cw, conv_w[0], m_conv_w[0], v_conv_w[0], "adam_conv_w")

    dmod_all = dmod_g[:, :, 0:6].reshape(N_DEV * nb, 6 * D)
    dmod_slice = lax.dynamic_slice(dmod_all, (0, me * ada_cols), (N_DEV * nb, ada_cols))
    g_ada, d_ada, nm_ada, nv_ada = _ada_bwd_adam(c_all, dmod_slice, w_ada[0], m_w_ada[0], v_w_ada[0])

    def small_outs(kind, wpool):
        res = {name: small_res[4 * i + kind] for i, name in enumerate(SMALL_PARAMS)}
        res["g_final"] = res["g_final"].reshape(D)
        res["w_pool"] = wpool.reshape(1, 4, LANES, LANES)
        return res

    def big_outs(ada, win, cwp, wout, wup, wdn):
        return {"w_ada": ada[None], "w_in": win.reshape(1, D, in_cols), "conv_w": cwp[None], "w_out": wout[None],
                "w_up": wup[None], "w_down": wdn[None]}

    order = ["w_ada", "b_ada", "g_mix", "w_in", "conv_w", "conv_b", "dt_bias", "a_log", "d_skip", "g_ssd", "w_pool",
             "pool_scale", "w_out", "g_mlp", "w_up", "w_down", "g_final"]
    groups = [
        {**small_outs(0, wpool_outs[0]), **big_outs(g_ada, g_in, g_cw, g_out, g_up, g_dn)},
        {**small_outs(1, wpool_outs[1]), **big_outs(d_ada, d_in, d_cwp, d_out, d_up, d_dn)},
        {**small_outs(2, wpool_outs[2]), **big_outs(nm_ada, nm_in, nm_cwp, nm_out, nm_up, nm_dn)},
        {**small_outs(3, wpool_outs[3]), **big_outs(nv_ada, nv_in, nv_cwp, nv_out, nv_up, nv_dn)},
    ]
    outs = [loss, grad_x2.reshape(nb, seq, D)]
    for grp in groups:
        outs += [grp[n] for n in order]
    return tuple(outs)
```

```python
import functools

import jax
import jax.numpy as jnp
from jax import lax
from jax.experimental import pallas as pl
from jax.experimental.pallas import tpu as pltpu

F32, BF16 = jnp.float32, jnp.bfloat16
MESH = pl.DeviceIdType.MESH
N_DEV = 8
D = 1024
LANES = 128
CHUNK = 128
POOL_W = 512
WINDOWS = (2, 4, 8, 16)
N_HEADS = 16
HEAD_DIM = 64
N_GROUPS = 2
GROUP_W = 512
N_STATE = 128
CONV_CH = 1536
OFF_Z, OFF_XBC, OFF_DT, IN_W = 512, 1536, 3072, 3088
PROJ_W = OFF_DT + LANES
MIX_W = 1536
D_FF = 4096
FF_BLK = 512
EPS = 1e-5
LR, B1, B2, AEPS, WD, STEP = 0.001, 0.9, 0.999, 1e-08, 0.01, 10
POOL_HALO = 16
CONV_HALO = 8
VMEM_LIMIT = 56 << 20
ADAM_BLOCK_BYTES = 1 << 20
DW_IN_WIN = 512


def _cparams(**kw):
    return pltpu.CompilerParams(vmem_limit_bytes=VMEM_LIMIT, **kw)


def _mm(a, b):
    return jnp.dot(a.astype(BF16), b.astype(BF16), preferred_element_type=F32)


def _mm_nt(a, b):
    return lax.dot_general(a.astype(BF16), b.astype(BF16), (((1,), (1,)), ((), ())), preferred_element_type=F32)


def _mm_tn(a, b):
    return lax.dot_general(a.astype(BF16), b.astype(BF16), (((0,), (0,)), ((), ())), preferred_element_type=F32)


def _split_bf16(v, terms):
    parts, rest = [], v
    for t in range(terms):
        p = rest.astype(BF16)
        parts.append(p)
        if t + 1 < terms:
            rest = rest - p.astype(F32)
    return parts


def _dot01(a, b, terms, split_lhs=True):
    if split_lhs:
        bb = b.astype(BF16)
        prods = [jnp.dot(p, bb, preferred_element_type=F32) for p in _split_bf16(a, terms)]
    else:
        ab = a.astype(BF16)
        prods = [jnp.dot(ab, p, preferred_element_type=F32) for p in _split_bf16(b, terms)]
    out = prods[0]
    for q in prods[1:]:
        out = out + q
    return out


def _sigmoid(v):
    return 1.0 / (1.0 + jnp.exp(-v))


def _expand_mat():
    r = lax.broadcasted_iota(jnp.int32, (LANES, D), 0)
    c = lax.broadcasted_iota(jnp.int32, (LANES, D), 1)
    return (r == c // HEAD_DIM).astype(F32)


def _reduce_mat():
    r = lax.broadcasted_iota(jnp.int32, (D, LANES), 0)
    c = lax.broadcasted_iota(jnp.int32, (D, LANES), 1)
    return (c == r // HEAD_DIM).astype(F32)


def _pos():
    return lax.axis_index("x"), lax.axis_index("y"), lax.axis_index("c")


class _Gather:
    def __init__(self, x_refs, o_refs, send, recv, loc):
        self.x_refs, self.o_refs, self.send, self.recv, self.loc = x_refs, o_refs, send, recv, loc
        self.n = len(x_refs)
        x, y, c = _pos()
        self.c = c
        self.me, self.sib = (x, y, c), (x, y, 1 - c)
        self.chips = [(1 - x, y), (x, 1 - y), (1 - x, 1 - y)]

    def _cp(self, a, k, block, to, src=None):
        dst = self.o_refs[a].at[4 * block[0] + 2 * block[1] + block[2]]
        return pltpu.make_async_remote_copy(
            src_ref=dst if src is None else src, dst_ref=dst,
            send_sem=self.send.at[a * 7 + k], recv_sem=self.recv.at[a * 7 + k],
            device_id=to, device_id_type=MESH)

    def _mine(self, a):
        me = self.me
        return pltpu.make_async_copy(self.x_refs[a], self.o_refs[a].at[4 * me[0] + 2 * me[1] + me[2]], self.loc.at[a])

    def _first(self, a):
        cps = [self._cp(a, 0, self.me, self.sib, src=self.x_refs[a])]
        return cps + [self._cp(a, 1 + j, self.me, (*chip, self.c), src=self.x_refs[a])
                      for j, chip in enumerate(self.chips)]

    def _passed(self, a, j):
        return self._cp(a, 4 + j, (*self.chips[j], self.c), self.sib)

    def start(self):
        for a in range(self.n):
            self._mine(a).start()
            for cp in self._first(a):
                cp.start()

    def forward(self):
        for j, chip in enumerate(self.chips):
            for a in range(self.n):
                self._cp(a, 1 + j, (*chip, self.c), self.me).wait_recv()
                self._passed(a, j).start()

    def finish(self):
        for a in range(self.n):
            self._cp(a, 0, self.sib, self.me).wait_recv()
            for j, chip in enumerate(self.chips):
                self._cp(a, 4 + j, (*chip, 1 - self.c), self.me).wait_recv()
        for a in range(self.n):
            for cp in self._first(a):
                cp.wait_send()
            for j in range(3):
                self._passed(a, j).wait_send()
            self._mine(a).wait()


class _Exchange:
    def __init__(self, x_refs, o_refs, send, recv, loc):
        self.x_refs, self.o_refs, self.send, self.recv, self.loc = x_refs, o_refs, send, recv, loc
        self.n = len(x_refs)
        x, y, c = _pos()
        self.me_i = 4 * x + 2 * y + c
        self.peers = []
        for k in range(1, N_DEV):
            px = 1 - x if (k >> 2) & 1 else x
            py = 1 - y if (k >> 1) & 1 else y
            pc = 1 - c if k & 1 else c
            self.peers.append(((px, py, pc), 4 * px + 2 * py + pc))

    def _mine(self, a):
        return pltpu.make_async_copy(self.x_refs[a].at[self.me_i], self.o_refs[a].at[self.me_i], self.loc.at[a])

    def _cp(self, a, k, landing):
        peer, peer_i = self.peers[k]
        return pltpu.make_async_remote_copy(
            src_ref=self.x_refs[a].at[peer_i], dst_ref=self.o_refs[a].at[landing],
            send_sem=self.send.at[a * 7 + k], recv_sem=self.recv.at[a * 7 + k],
            device_id=peer, device_id_type=MESH)

    def start(self):
        for a in range(self.n):
            self._mine(a).start()
            for k in range(N_DEV - 1):
                self._cp(a, k, self.me_i).start()

    def finish(self):
        for a in range(self.n):
            for k in range(N_DEV - 1):
                self._cp(a, k, self.peers[k][1]).wait_recv()
        for a in range(self.n):
            for k in range(N_DEV - 1):
                self._cp(a, k, self.me_i).wait_send()
            self._mine(a).wait()


def _comm_scratch(n):
    return [pltpu.SemaphoreType.DMA((7 * n,)), pltpu.SemaphoreType.DMA((7 * n,)), pltpu.SemaphoreType.DMA((n,))]


ANY_SPEC = pl.BlockSpec(memory_space=pl.ANY)


def _all_gather(xs, name):
    n = len(xs)

    def body(*refs):
        g = _Gather(refs[:n], refs[n:2 * n], *refs[2 * n:])
        g.start()
        g.forward()
        g.finish()

    return pl.pallas_call(
        body, name=name,
        out_shape=[jax.ShapeDtypeStruct((N_DEV,) + v.shape, v.dtype) for v in xs],
        in_specs=[ANY_SPEC] * n, out_specs=[ANY_SPEC] * n, scratch_shapes=_comm_scratch(n),
    )(*xs)


HBM_SPEC = pl.BlockSpec(memory_space=pltpu.HBM)
SEM_SPEC = pl.BlockSpec(memory_space=pltpu.SEMAPHORE)
VMEM_SPEC = pl.BlockSpec(memory_space=pltpu.VMEM)
SPLIT_EFFECT = pltpu.SideEffectType.DATAFLOW_SIDE_EFFECTING


def _in_hbm(v):
    return pltpu.with_memory_space_constraint(v, pltpu.HBM)


def _exchange_start(blocks, name):
    def body(x_ref, land_ref, send, recv, x_thru, land_thru, token):
        ex = _Exchange([x_ref], [land_ref], send, recv, None)
        for k in range(N_DEV - 1):
            ex._cp(0, k, ex.me_i).start()
        token[...] = jnp.zeros_like(token)

    hbm = pltpu.HBM(blocks.shape, blocks.dtype)
    return pl.pallas_call(
        body, name=name,
        out_shape=(pltpu.SemaphoreType.DMA((N_DEV - 1,)), pltpu.SemaphoreType.DMA((N_DEV - 1,)), hbm, hbm,
                   jax.ShapeDtypeStruct((8, LANES), F32)),
        in_specs=(HBM_SPEC, HBM_SPEC), out_specs=(SEM_SPEC, SEM_SPEC, HBM_SPEC, HBM_SPEC, VMEM_SPEC),
        input_output_aliases={0: 2, 1: 3},
        compiler_params=pltpu.CompilerParams(has_side_effects=SPLIT_EFFECT),
    )(_in_hbm(blocks), _in_hbm(lax.empty(blocks.shape, blocks.dtype)))


def _exchange_wait(send, recv, x_thru, land_thru, after, name):
    def body(x_ref, land_ref, send_ref, recv_ref, after_ref, x_dead, got_ref):
        ex = _Exchange([x_ref], [land_ref], send_ref, recv_ref, None)
        for k in range(N_DEV - 1):
            ex._cp(0, k, ex.me_i).wait_send()
            ex._cp(0, k, ex.peers[k][1]).wait_recv()

    hbm = pltpu.HBM(x_thru.shape, x_thru.dtype)
    return pl.pallas_call(
        body, name=name, out_shape=(hbm, hbm),
        in_specs=(HBM_SPEC, HBM_SPEC, SEM_SPEC, SEM_SPEC, ANY_SPEC), out_specs=(HBM_SPEC, HBM_SPEC),
        input_output_aliases={0: 0, 1: 1},
        compiler_params=pltpu.CompilerParams(has_side_effects=SPLIT_EFFECT),
    )(x_thru, land_thru, send, recv, after)[1]


def _ada_fwd(c_all, w_ada, b_slice):
    def body(c_ref, w_ref, b_ref, o_ref):
        cv = c_ref[...]
        act = cv * _sigmoid(cv)
        o_ref[...] = _mm(act, w_ref[...]) + b_ref[...]

    nb, nc = c_all.shape[0], w_ada.shape[1]
    return pl.pallas_call(body, name="ada_fwd", out_shape=jax.ShapeDtypeStruct((nb, nc), F32),
                          compiler_params=_cparams())(c_all, w_ada, b_slice)


def _adam_math(w, g, m, v):
    m = B1 * m + (1.0 - B1) * g
    v = B2 * v + (1.0 - B2) * jnp.square(g)
    m_hat = m / (1.0 - B1 ** STEP)
    v_hat = v / (1.0 - B2 ** STEP)
    delta = -LR * (m_hat / (jnp.sqrt(v_hat) + AEPS) + WD * w)
    return delta, m, v


def _ada_bwd_adam(c_all, dmod_slice, w, m, v):
    rows, cols = w.shape
    br = 256

    def body(c_ref, d_ref, w_ref, m_ref, v_ref, g_out, dl_out, m_out, v_out):
        cv = c_ref[...]
        act = cv * _sigmoid(cv)
        g = _mm_tn(act, d_ref[...])
        g_out[...] = g
        dl, mn, vn = _adam_math(w_ref[...], g, m_ref[...], v_ref[...])
        dl_out[...] = dl
        m_out[...] = mn
        v_out[...] = vn

    nb = c_all.shape[0]
    wspec = pl.BlockSpec((br, cols), lambda i: (i, 0))
    return pl.pallas_call(
        body, name="ada_bwd_adam", grid=(rows // br,),
        in_specs=[pl.BlockSpec((nb, br), lambda i: (0, i)), pl.BlockSpec((nb, cols), lambda i: (0, 0)),
                  wspec, wspec, wspec],
        out_specs=[wspec] * 4, out_shape=[jax.ShapeDtypeStruct((rows, cols), F32)] * 4,
        compiler_params=_cparams(),
    )(c_all, dmod_slice, w, m, v)


def _mix_in(x2, mod, g_mix, win_g, seq):
    T = x2.shape[0]
    tm = min(512, seq)
    tps = seq // tm
    in_cols = win_g.shape[2]

    def body(x_ref, mod_ref, g_ref, wb_ref, u_ref, pm_ref, wc_ref, w_ref):
        @pl.when(pl.program_id(0) == 0)
        def _():
            w_ref[:, OFF_DT:] = jnp.zeros((D, PROJ_W - OFF_DT), BF16)
            for j in range(N_DEV):
                w_ref[:, in_cols * j:in_cols * (j + 1)] = wb_ref[j]
            wc_ref[...] = w_ref[...]

        x = x_ref[...]
        r = lax.rsqrt(jnp.mean(x * x, axis=-1, keepdims=True) + EPS)
        md = mod_ref[0]
        u = (x * r * g_ref[...]) * (1.0 + md[1:2]) + md[0:1]
        ub = u.astype(BF16)
        u_ref[...] = ub
        pm_ref[...] = jnp.dot(ub, w_ref[...], preferred_element_type=F32)

    whole = pl.BlockSpec(memory_space=pltpu.VMEM)
    return pl.pallas_call(
        body, name="mix_in", grid=(T // tm,),
        in_specs=[pl.BlockSpec((tm, D), lambda i: (i, 0)), pl.BlockSpec((1, 8, D), lambda i: (i // tps, 0, 0)),
                  pl.BlockSpec((1, D), lambda i: (0, 0)), whole],
        out_specs=[pl.BlockSpec((tm, D), lambda i: (i, 0)), pl.BlockSpec((tm, PROJ_W), lambda i: (i, 0)),
                   pl.BlockSpec((D, PROJ_W), lambda i: (0, 0))],
        out_shape=[jax.ShapeDtypeStruct((T, D), BF16), jax.ShapeDtypeStruct((T, PROJ_W), F32),
                   jax.ShapeDtypeStruct((D, PROJ_W), BF16)],
        scratch_shapes=[pltpu.VMEM((D, PROJ_W), BF16)],
        compiler_params=_cparams(),
    )(x2, mod, g_mix, win_g)


def _chunk_forward(up, z, ux, dtin, halo_p, halo_x, hprev, cw, cb, hp, gssd, wpool, pscale, t0, y_scr):
    L = CHUNK
    out = {}
    row = lax.broadcasted_iota(jnp.int32, (L, 1), 0)
    t = (t0 + row + 1).astype(F32)
    e = jnp.concatenate([halo_p, up], axis=0)
    s2 = e + pltpu.roll(e, 1, 0)
    s4 = s2 + pltpu.roll(s2, 2, 0)
    s8 = s4 + pltpu.roll(s4, 4, 0)
    s16 = s8 + pltpu.roll(s8, 8, 0)
    sums = (s2, s4, s8, s16)
    p, inv, yp = [], [], []
    for gi, w in enumerate(WINDOWS):
        sl = slice(gi * LANES, (gi + 1) * LANES)
        ic = 1.0 / jnp.minimum(t, float(w))
        pg = sums[gi][POOL_HALO:, sl] * ic - up[:, sl]
        p.append(pg)
        inv.append(ic)
        yp.append(_mm(pg, wpool[gi]))
    out["p"], out["inv"], out["yp"] = p, inv, yp
    out["y_pool"] = jnp.concatenate(yp, axis=1) * pscale
    ex = jnp.concatenate([halo_x, ux], axis=0)
    taps = [pltpu.roll(ex, 3, 0)[CONV_HALO:], pltpu.roll(ex, 2, 0)[CONV_HALO:], pltpu.roll(ex, 1, 0)[CONV_HALO:], ux]
    cv = cb + taps[0] * cw[0:1] + taps[1] * cw[1:2] + taps[2] * cw[2:3] + taps[3] * cw[3:4]
    sg = _sigmoid(cv)
    xbc = cv * sg
    out["taps"], out["cv"], out["sg"] = taps, cv, sg
    X = xbc[:, :D]
    Bm = xbc[:, D:D + N_GROUPS * N_STATE]
    Cm = xbc[:, D + N_GROUPS * N_STATE:]
    pre = dtin + hp[0:1]
    dt = jnp.maximum(pre, 0.0) + jnp.log(1.0 + jnp.exp(-jnp.abs(pre)))
    a_row = -jnp.exp(hp[1:2])
    da = dt * a_row
    ri = lax.broadcasted_iota(jnp.int32, (L, L), 0)
    ci = lax.broadcasted_iota(jnp.int32, (L, L), 1)
    causal = ri >= ci
    cum = _dot01(causal.astype(F32), da, 3, split_lhs=False)
    cum_t = cum.T
    cum_last = cum[L - 1:L]
    eo = jnp.exp(cum)
    dec = jnp.exp(cum_last - cum)
    cd = jnp.exp(cum_last)
    exm = _expand_mat()
    rows8 = jnp.concatenate([cd, hp[2:3], jnp.zeros((6, LANES), F32)], axis=0)
    rep = _dot01(jnp.concatenate([dt, eo, dec, rows8], axis=0), exm, 2)
    dt_rep, eo_rep, dec_rep = rep[0:L], rep[L:2 * L], rep[2 * L:3 * L]
    cd_rep, dskip_rep = rep[3 * L:3 * L + 1], rep[3 * L + 1:3 * L + 2]
    xdt = X * dt_rep
    out.update(X=X, Bm=Bm, Cm=Cm, pre=pre, dt=dt, a_row=a_row, cum=cum, cum_t=cum_t, eo=eo, dec=dec, cd=cd,
               dt_rep=dt_rep, eo_rep=eo_rep, dec_rep=dec_rep, cd_rep=cd_rep, dskip_rep=dskip_rep, xdt=xdt,
               causal=causal, anti=(ri <= ci).astype(F32), exm=exm)
    G, lms, yoff, hnew, xdec = [], [], [], [], []
    for g in range(N_GROUPS):
        gs = slice(g * GROUP_W, (g + 1) * GROUP_W)
        Bg = Bm[:, g * N_STATE:(g + 1) * N_STATE]
        Cg = Cm[:, g * N_STATE:(g + 1) * N_STATE]
        Gg = _mm_nt(Cg, Bg)
        G.append(Gg)
        for hh in range(N_HEADS // N_GROUPS):
            h = g * (N_HEADS // N_GROUPS) + hh
            seg = cum[:, h:h + 1] - cum_t[h:h + 1, :]
            lm = jnp.where(causal, jnp.exp(jnp.minimum(seg, 0.0)), 0.0)
            lms.append(lm)
            hs = slice(h * HEAD_DIM, (h + 1) * HEAD_DIM)
            y_scr[:, hs] = _mm(Gg * lm, xdt[:, hs])
        xd = xdt[:, gs] * dec_rep[:, gs]
        xdec.append(xd)
        sgm = _mm_tn(Bg, xd)
        yoff.append(_mm(Cg, hprev[g]) * eo_rep[:, gs])
        hnew.append(hprev[g] * cd_rep[:, gs] + sgm)
    out.update(G=G, lms=lms, yoff=yoff, hnew=hnew, xdec=xdec)
    y = y_scr[...] + jnp.concatenate(yoff, axis=1) + dskip_rep * X
    sz = _sigmoid(z)
    silz = z * sz
    yz = y * silz
    rg, yn = [], []
    for g in range(N_GROUPS):
        gs = slice(g * GROUP_W, (g + 1) * GROUP_W)
        r = lax.rsqrt(jnp.mean(yz[:, gs] * yz[:, gs], axis=-1, keepdims=True) + EPS)
        rg.append(r)
        yn.append(yz[:, gs] * r)
    yn = jnp.concatenate(yn, axis=1)
    out.update(y=y, sz=sz, silz=silz, rg=rg, yn=yn)
    out["y_ssd"] = yn * gssd
    return out


def _mixer_fwd(pm, cw, cb, hp, gssd, wpool, pscale, nb, seq, shards):
    nc = seq // CHUNK
    ns = len(shards)
    steps = nb * nc
    fwd_step = (3 * steps) // 4

    def body(*refs):
        pm_ref, cw_ref, cb_ref, hp_ref, gs_ref, wp_ref, ps_ref = refs[:7]
        sh_refs = refs[7:7 + ns]
        ym_ref, hs_ref = refs[7 + ns:9 + ns]
        ga_refs = refs[9 + ns:9 + 2 * ns]
        halo_p, halo_x, state, y_scr, send, recv, loc = refs[9 + 2 * ns:]
        c = pl.program_id(1)
        step = pl.program_id(0) * nc + c
        gather = _Gather(sh_refs, ga_refs, send, recv, loc)

        @pl.when(step == 0)
        def _():
            gather.start()

        @pl.when(step == fwd_step)
        def _():
            gather.forward()

        @pl.when(c == 0)
        def _():
            halo_p[...] = jnp.zeros_like(halo_p)
            halo_x[...] = jnp.zeros_like(halo_x)
            state[...] = jnp.zeros_like(state)

        up = pm_ref[:, 0:POOL_W]
        z = pm_ref[:, OFF_Z:OFF_XBC]
        ux = pm_ref[:, OFF_XBC:OFF_DT]
        hprev = [state[0], state[1]]
        hs_ref[0, 0, 0] = hprev[0]
        hs_ref[0, 0, 1] = hprev[1]
        o = _chunk_forward(up, z, ux, pm_ref[:, OFF_DT:], halo_p[...], halo_x[...], hprev, cw_ref[...], cb_ref[...],
                           hp_ref[...], gs_ref[...], wp_ref[...], ps_ref[...], c * CHUNK, y_scr)
        ym_ref[:, 0:POOL_W] = o["y_pool"].astype(BF16)
        ym_ref[:, POOL_W:] = o["y_ssd"].astype(BF16)
        state[0] = o["hnew"][0]
        state[1] = o["hnew"][1]
        halo_p[...] = up[CHUNK - POOL_HALO:]
        halo_x[...] = ux[CHUNK - CONV_HALO:]

        @pl.when(step == steps - 1)
        def _():
            gather.finish()

    def full(shape):
        return pl.BlockSpec(shape, lambda b, c: (0,) * len(shape))

    T = nb * seq
    return pl.pallas_call(
        body, name="mixer_fwd", grid=(nb, nc),
        in_specs=[pl.BlockSpec((CHUNK, PROJ_W), lambda b, c: (b * nc + c, 0)),
                  full((4, CONV_CH)), full((1, CONV_CH)), full((8, LANES)), full((1, D)),
                  full((4, LANES, LANES)), full((1, POOL_W))] + [ANY_SPEC] * ns,
        out_specs=[pl.BlockSpec((CHUNK, MIX_W), lambda b, c: (b * nc + c, 0)),
                   pl.BlockSpec((1, 1, N_GROUPS, N_STATE, GROUP_W), lambda b, c: (b, c, 0, 0, 0))] + [ANY_SPEC] * ns,
        out_shape=[jax.ShapeDtypeStruct((T, MIX_W), BF16),
                   jax.ShapeDtypeStruct((nb, nc, N_GROUPS, N_STATE, GROUP_W), F32)]
        + [jax.ShapeDtypeStruct((N_DEV,) + v.shape, v.dtype) for v in shards],
        scratch_shapes=[pltpu.VMEM((POOL_HALO, POOL_W), F32), pltpu.VMEM((CONV_HALO, CONV_CH), F32),
                        pltpu.VMEM((N_GROUPS, N_STATE, GROUP_W), F32), pltpu.VMEM((CHUNK, D), F32)] + _comm_scratch(ns),
        compiler_params=_cparams(),
    )(pm, cw, cb, hp, gssd, wpool, pscale, *shards)


def _mixer_bwd(pm, dym, hstates, cw, cb, hp, gssd, wpool, pscale, nb, seq, blocks):
    nc = seq // CHUNK
    hpg = N_HEADS // N_GROUPS
    ns = len(blocks)
    steps = nb * nc

    def body(*refs):
        (pm_ref, hpool_ref, hxbc_ref, dy_ref, hs_ref, cw_ref, cb_ref, hp_ref, gs_ref, wp_ref, ps_ref) = refs[:11]
        bl_refs = refs[11:11 + ns]
        dpm_ref, dconv_ref, dhp_ref, dvec_ref, dwp_ref = refs[11 + ns:16 + ns]
        ex_refs = refs[16 + ns:16 + 2 * ns]
        nxt_q, nxt_cv, rstate, y_scr, dx_scr, send, recv, loc = refs[16 + 2 * ns:]
        b = pl.program_id(0)
        ci = pl.program_id(1)
        c = nc - 1 - ci
        exchange = _Exchange(bl_refs, ex_refs, send, recv, loc)

        @pl.when((b == 0) & (ci == 0))
        def _():
            exchange.start()

        @pl.when((b == 0) & (ci == 0))
        def _():
            for r in (dconv_ref, dhp_ref, dvec_ref, dwp_ref):
                r[...] = jnp.zeros_like(r)

        @pl.when(ci == 0)
        def _():
            nxt_q[...] = jnp.zeros_like(nxt_q)
            nxt_cv[...] = jnp.zeros_like(nxt_cv)
            rstate[...] = jnp.zeros_like(rstate)

        first = (c > 0).astype(F32)
        up = pm_ref[:, 0:POOL_W]
        z = pm_ref[:, OFF_Z:OFF_XBC]
        ux = pm_ref[:, OFF_XBC:OFF_DT]
        halo_p = hpool_ref[...] * first
        halo_x = hxbc_ref[...] * first
        hprev = [hs_ref[0, 0, 0], hs_ref[0, 0, 1]]
        cw, cb, hp, gssd, wpool, pscale = cw_ref[...], cb_ref[...], hp_ref[...], gs_ref[...], wp_ref[...], ps_ref[...]
        o = _chunk_forward(up, z, ux, pm_ref[:, OFF_DT:], halo_p, halo_x, hprev, cw, cb, hp, gssd, wpool, pscale,
                           c * CHUNK, y_scr)
        L = CHUNK
        dy_pool = dy_ref[:, 0:POOL_W].astype(F32)
        dy_ssd = dy_ref[:, POOL_W:].astype(F32)

        dvec_ref[1:2, 0:POOL_W] += jnp.sum(dy_pool * jnp.concatenate(o["yp"], axis=1), axis=0, keepdims=True)
        dyp = dy_pool * pscale
        qs = []
        dps = []
        for gi in range(len(WINDOWS)):
            sl = slice(gi * LANES, (gi + 1) * LANES)
            dwp_ref[gi] += _mm_tn(o["p"][gi], dyp[:, sl])
            dpg = _mm_nt(dyp[:, sl], wpool[gi])
            dps.append(dpg)
            qs.append(dpg * o["inv"][gi])
        q = jnp.concatenate(qs, axis=1)
        e = jnp.concatenate([q, nxt_q[...]], axis=0)
        n = L + POOL_HALO
        s2 = e + pltpu.roll(e, n - 1, 0)
        s4 = s2 + pltpu.roll(s2, n - 2, 0)
        s8 = s4 + pltpu.roll(s4, n - 4, 0)
        s16 = s8 + pltpu.roll(s8, n - 8, 0)
        sums = (s2, s4, s8, s16)
        for gi in range(len(WINDOWS)):
            sl = slice(gi * LANES, (gi + 1) * LANES)
            dpm_ref[:, sl] = (sums[gi][:L, sl] - dps[gi]).astype(BF16)
        nxt_q[...] = q[:POOL_HALO]

        yn, y, silz, sz = o["yn"], o["y"], o["silz"], o["sz"]
        dvec_ref[0:1] += jnp.sum(dy_ssd * yn, axis=0, keepdims=True)
        dyn = dy_ssd * gssd
        dyz = []
        for g in range(N_GROUPS):
            gs = slice(g * GROUP_W, (g + 1) * GROUP_W)
            mean = jnp.mean(dyn[:, gs] * yn[:, gs], axis=-1, keepdims=True)
            dyz.append(o["rg"][g] * (dyn[:, gs] - yn[:, gs] * mean))
        dyz = jnp.concatenate(dyz, axis=1)
        dyv = dyz * silz
        dpm_ref[:, OFF_Z:OFF_XBC] = (dyz * y * (sz * (1.0 + z * (1.0 - sz)))).astype(BF16)

        X, Bm, Cm, xdt = o["X"], o["Bm"], o["Cm"], o["xdt"]
        exm = o["exm"]
        rdm = _reduce_mat()
        lane = lax.broadcasted_iota(jnp.int32, (1, LANES), 1)
        sub = lax.broadcasted_iota(jnp.int32, (LANES, 1), 0)
        dX = o["dskip_rep"] * dyv
        yoff_full = jnp.concatenate(o["yoff"], axis=1)
        rs = jnp.zeros((L, LANES), F32)
        cs_t = jnp.zeros((LANES, L), F32)
        dBs, dCs = [], []
        rh_sums = []
        ddec = []
        for g in range(N_GROUPS):
            gs = slice(g * GROUP_W, (g + 1) * GROUP_W)
            Bg = Bm[:, g * N_STATE:(g + 1) * N_STATE]
            Cg = Cm[:, g * N_STATE:(g + 1) * N_STATE]
            Gg = o["G"][g]
            R = rstate[g]
            dwm = dyv[:, gs] * o["eo_rep"][:, gs]
            dC = _mm_nt(dwm, hprev[g])
            dH = _mm_tn(Cg, dwm)
            dG = jnp.zeros((L, L), F32)
            for hh in range(hpg):
                h = g * hpg + hh
                hs = slice(h * HEAD_DIM, (h + 1) * HEAD_DIM)
                lm = o["lms"][h]
                m_h = Gg * lm
                dM = _mm_nt(dyv[:, hs], xdt[:, hs])
                dx_scr[:, hs] = _mm_tn(m_h, dyv[:, hs])
                qm = dM * m_h
                rs = rs + jnp.sum(qm, axis=1, keepdims=True) * (lane == h).astype(F32)
                cs_t = cs_t + (sub == h).astype(F32) * jnp.sum(qm, axis=0, keepdims=True)
                dG = dG + dM * lm
            dC = dC + _mm(dG, Bg)
            dB = _mm_tn(dG, Cg)
            zx = _mm(Bg, R)
            dxdt_state = zx * o["dec_rep"][:, gs]
            ddec.append(zx * xdt[:, gs])
            dB = dB + _mm_nt(o["xdec"][g], R)
            rh_sums.append(jnp.sum(R * hprev[g], axis=0, keepdims=True))
            rstate[g] = dH + o["cd_rep"][:, gs] * R
            dx_scr[:, gs] = dx_scr[:, gs] + dxdt_state
            dBs.append(dB)
            dCs.append(dC)
        dxdt = dx_scr[...]
        tail = jnp.concatenate([jnp.sum(dyv * X, axis=0, keepdims=True), jnp.concatenate(rh_sums, axis=1),
                                jnp.zeros((6, D), F32)], axis=0)
        red = _dot01(jnp.concatenate([dyv * yoff_full, jnp.concatenate(ddec, axis=1), dxdt * X, tail], axis=0), rdm, 2)
        d_dskip, dcd_row = red[3 * L:3 * L + 1], red[3 * L + 1:3 * L + 2]
        ddec_h = red[L:2 * L] * o["dec"]
        dcum_last = jnp.sum(ddec_h, axis=0, keepdims=True) + dcd_row * o["cd"]
        dcum = red[0:L] + rs - cs_t.T - ddec_h + (sub == L - 1).astype(F32) * dcum_last
        dda = _dot01(o["anti"], dcum, 3, split_lhs=False)
        ddt_v = dda * o["a_row"] + red[2 * L:3 * L]
        dX = dX + dxdt * o["dt_rep"]
        head_mask = (lane < N_HEADS).astype(F32)
        d_alog = jnp.sum(dda * o["dt"], axis=0, keepdims=True) * o["a_row"] * head_mask
        dpre = ddt_v * _sigmoid(o["pre"]) * head_mask
        dpm_ref[:, OFF_DT:] = dpre.astype(BF16)
        d_dtb = jnp.sum(dpre, axis=0, keepdims=True)
        dhp_ref[...] += jnp.concatenate([d_dtb, d_alog, d_dskip * head_mask, jnp.zeros((5, LANES), F32)], axis=0)

        dxbc = jnp.concatenate([dX] + dBs + dCs, axis=1)
        sg, cv = o["sg"], o["cv"]
        dcv = dxbc * (sg * (1.0 + cv * (1.0 - sg)))
        dconv_ref[0:5] += jnp.concatenate(
            [jnp.sum(dcv * o["taps"][k], axis=0, keepdims=True) for k in range(4)]
            + [jnp.sum(dcv, axis=0, keepdims=True)], axis=0)
        e2 = jnp.concatenate([dcv, nxt_cv[...]], axis=0)
        n2 = L + CONV_HALO
        dux = (dcv * cw[3:4] + pltpu.roll(e2, n2 - 1, 0)[:L] * cw[2:3] + pltpu.roll(e2, n2 - 2, 0)[:L] * cw[1:2]
               + pltpu.roll(e2, n2 - 3, 0)[:L] * cw[0:1])
        dpm_ref[:, OFF_XBC:OFF_DT] = dux.astype(BF16)
        nxt_cv[...] = dcv[:CONV_HALO]

        @pl.when((b == nb - 1) & (ci == nc - 1))
        def _():
            exchange.finish()

    def full(shape):
        return pl.BlockSpec(shape, lambda b, c: (0,) * len(shape))

    def rowblk(b, c):
        return b * nc + (nc - 1 - c)

    hp_blocks = CHUNK // POOL_HALO
    hx_blocks = CHUNK // CONV_HALO
    T = nb * seq
    return pl.pallas_call(
        body, name="mixer_bwd", grid=(nb, nc),
        in_specs=[pl.BlockSpec((CHUNK, PROJ_W), lambda b, c: (rowblk(b, c), 0)),
                  pl.BlockSpec((POOL_HALO, POOL_W), lambda b, c: (jnp.maximum(rowblk(b, c) * hp_blocks - 1, 0), 0)),
                  pl.BlockSpec((CONV_HALO, CONV_CH), lambda b, c: (jnp.maximum(rowblk(b, c) * hx_blocks - 1, 0), 1)),
                  pl.BlockSpec((CHUNK, MIX_W), lambda b, c: (rowblk(b, c), 0)),
                  pl.BlockSpec((1, 1, N_GROUPS, N_STATE, GROUP_W), lambda b, c: (b, nc - 1 - c, 0, 0, 0)),
                  full((4, CONV_CH)), full((1, CONV_CH)), full((8, LANES)), full((1, D)),
                  full((4, LANES, LANES)), full((1, POOL_W))] + [ANY_SPEC] * ns,
        out_specs=[pl.BlockSpec((CHUNK, PROJ_W), lambda b, c: (rowblk(b, c), 0)),
                   full((8, CONV_CH)), full((8, LANES)), full((8, D)), full((4, LANES, LANES))] + [ANY_SPEC] * ns,
        out_shape=[jax.ShapeDtypeStruct((T, PROJ_W), BF16),
                   jax.ShapeDtypeStruct((8, CONV_CH), F32), jax.ShapeDtypeStruct((8, LANES), F32),
                   jax.ShapeDtypeStruct((8, D), F32), jax.ShapeDtypeStruct((4, LANES, LANES), F32)]
        + [jax.ShapeDtypeStruct(v.shape, v.dtype) for v in blocks],
        scratch_shapes=[pltpu.VMEM((POOL_HALO, POOL_W), F32), pltpu.VMEM((CONV_HALO, CONV_CH), F32),
                        pltpu.VMEM((N_GROUPS, N_STATE, GROUP_W), F32), pltpu.VMEM((CHUNK, D), F32),
                        pltpu.VMEM((CHUNK, D), F32)] + _comm_scratch(ns),
        compiler_params=_cparams(),
    )(pm, pm, pm, dym, hstates, cw, cb, hp, gssd, wpool, pscale, *blocks)


def _mlp_fused(x2, ymix, target, mod, g_mlp, g_final, w_out, w_up, w_down, seq):
    T = x2.shape[0]
    tm = min(256, seq)
    tps = seq // tm
    nblk = D_FF // FF_BLK

    def body(x_ref, ym_ref, tg_ref, mod_ref, gm_ref, gf_ref, wo_ref, wu_ref, wd_ref,
             da_ref, dym_ref, dh1_ref, u2_ref, f_ref, dup_ref, ddn_ref, dmod_ref, acc_ref, relu_scr):
        i = pl.program_id(0)

        @pl.when(i == 0)
        def _():
            acc_ref[...] = jnp.zeros_like(acc_ref)

        @pl.when(i % tps == 0)
        def _():
            dmod_ref[...] = jnp.zeros_like(dmod_ref)

        md = mod_ref[0]
        gate_m, shift_f, scale_f, gate_f = md[2:3], md[3:4], md[4:5], md[5:6]
        g_mlp, g_fin = gm_ref[...], gf_ref[...]
        a = jnp.dot(ym_ref[...], wo_ref[...], preferred_element_type=F32)
        h1 = x_ref[...] + gate_m * a
        r2 = lax.rsqrt(jnp.mean(h1 * h1, axis=-1, keepdims=True) + EPS)
        n2 = h1 * r2
        u2 = (n2 * g_mlp) * (1.0 + scale_f) + shift_f
        u2b = u2.astype(BF16)
        u2_ref[...] = u2b
        dn = jnp.zeros((tm, D), F32)
        for j in range(nblk):
            js = slice(j * FF_BLK, (j + 1) * FF_BLK)
            upj = jnp.maximum(jnp.dot(u2b, wu_ref[j], preferred_element_type=F32), 0.0)
            relu_scr[:, js] = upj
            fj = (upj * upj).astype(BF16)
            f_ref[:, js] = fj
            dn = dn + jnp.dot(fj, wd_ref[j], preferred_element_type=F32)
        h2 = h1 + gate_f * dn
        r3 = lax.rsqrt(jnp.mean(h2 * h2, axis=-1, keepdims=True) + EPS)
        n3 = h2 * r3
        err = n3 * g_fin - tg_ref[...]
        loss = 0.5 * jnp.sum(jnp.mean(err * err, axis=-1, keepdims=True), axis=0, keepdims=True)
        dout = err * (1.0 / D)
        d_gfin = jnp.sum(dout * n3, axis=0, keepdims=True)
        dn3 = dout * g_fin
        dh2 = r3 * (dn3 - n3 * jnp.mean(dn3 * n3, axis=-1, keepdims=True))
        d_gate_f = jnp.sum(dh2 * dn, axis=0, keepdims=True)
        ddn = (gate_f * dh2).astype(BF16)
        ddn_ref[...] = ddn
        du2 = jnp.zeros((tm, D), F32)
        for j in range(nblk):
            js = slice(j * FF_BLK, (j + 1) * FF_BLK)
            dfj = lax.dot_general(ddn, wd_ref[j], (((1,), (1,)), ((), ())), preferred_element_type=F32)
            dupj = (dfj * (2.0 * relu_scr[:, js])).astype(BF16)
            dup_ref[:, js] = dupj
            du2 = du2 + lax.dot_general(dupj, wu_ref[j], (((1,), (1,)), ((), ())), preferred_element_type=F32)
        d_scale_f = jnp.sum(du2 * (n2 * g_mlp), axis=0, keepdims=True)
        d_shift_f = jnp.sum(du2, axis=0, keepdims=True)
        d_gmlp = jnp.sum(du2 * (1.0 + scale_f) * n2, axis=0, keepdims=True)
        dn2 = du2 * (g_mlp * (1.0 + scale_f))
        dh1 = dh2 + r2 * (dn2 - n2 * jnp.mean(dn2 * n2, axis=-1, keepdims=True))
        dh1_ref[...] = dh1
        d_gate_m = jnp.sum(dh1 * a, axis=0, keepdims=True)
        da = (gate_m * dh1).astype(BF16)
        da_ref[...] = da
        dym_ref[...] = lax.dot_general(da, wo_ref[...], (((1,), (1,)), ((), ())),
                                       preferred_element_type=F32).astype(BF16)
        dmod_ref[0] += jnp.concatenate([jnp.zeros((2, D), F32), d_gate_m, d_shift_f, d_scale_f, d_gate_f,
                                        jnp.zeros((2, D), F32)], axis=0)
        acc_ref[...] += jnp.concatenate([d_gmlp, d_gfin, loss * jnp.ones((1, D), F32), jnp.zeros((5, D), F32)], axis=0)

    whole = pl.BlockSpec(memory_space=pltpu.VMEM)

    def tok(w):
        return pl.BlockSpec((tm, w), lambda i: (i, 0))

    def vec():
        return pl.BlockSpec((1, D), lambda i: (0, 0))

    nb = T // seq
    return pl.pallas_call(
        body, name="mlp_fused", grid=(T // tm,),
        in_specs=[tok(D), tok(MIX_W), tok(D), pl.BlockSpec((1, 8, D), lambda i: (i // tps, 0, 0)), vec(), vec(),
                  whole, whole, whole],
        out_specs=[tok(D), tok(MIX_W), tok(D), tok(D), tok(D_FF), tok(D_FF), tok(D),
                   pl.BlockSpec((1, 8, D), lambda i: (i // tps, 0, 0)), pl.BlockSpec((8, D), lambda i: (0, 0))],
        out_shape=[jax.ShapeDtypeStruct((T, D), BF16), jax.ShapeDtypeStruct((T, MIX_W), BF16),
                   jax.ShapeDtypeStruct((T, D), F32), jax.ShapeDtypeStruct((T, D), BF16),
                   jax.ShapeDtypeStruct((T, D_FF), BF16), jax.ShapeDtypeStruct((T, D_FF), BF16),
                   jax.ShapeDtypeStruct((T, D), BF16), jax.ShapeDtypeStruct((nb, 8, D), F32),
                   jax.ShapeDtypeStruct((8, D), F32)],
        scratch_shapes=[pltpu.VMEM((tm, D_FF), F32)],
        compiler_params=_cparams(),
    )(x2, ymix, target, mod, g_mlp, g_final, w_out, w_up, w_down)


def _in_bwd(x2, dh1, dpb, mod, g_mix, w_cat, dmod_a, acc_a, seq):
    T = x2.shape[0]
    tm = min(512, seq)
    tps = seq // tm
    steps = T // tm

    def body(x_ref, dh_ref, dpb_ref, mod_ref, g_ref, w_ref, dma_ref, acca_ref, dx_ref, dmod_ref, acc_ref):
        i = pl.program_id(0)

        @pl.when(i == 0)
        def _():
            acc_ref[...] = acca_ref[...]

        @pl.when(i % tps == 0)
        def _():
            dmod_ref[...] = dma_ref[...]

        du = lax.dot_general(dpb_ref[...], w_ref[...], (((1,), (1,)), ((), ())), preferred_element_type=F32)
        x = x_ref[...]
        md = mod_ref[0]
        g = g_ref[...]
        r = lax.rsqrt(jnp.mean(x * x, axis=-1, keepdims=True) + EPS)
        n1 = x * r
        d_scale = jnp.sum(du * (n1 * g), axis=0, keepdims=True)
        d_shift = jnp.sum(du, axis=0, keepdims=True)
        d_g = jnp.sum(du * (1.0 + md[1:2]) * n1, axis=0, keepdims=True)
        dn1 = du * (g * (1.0 + md[1:2]))
        dx_ref[...] = dh_ref[...] + r * (dn1 - n1 * jnp.mean(dn1 * n1, axis=-1, keepdims=True))
        dmod_ref[0] += jnp.concatenate([d_shift, d_scale, jnp.zeros((6, D), F32)], axis=0)
        acc_ref[...] += jnp.concatenate([jnp.zeros((3, D), F32), d_g, jnp.zeros((4, D), F32)], axis=0)

    whole = pl.BlockSpec(memory_space=pltpu.VMEM)
    nb = T // seq
    return pl.pallas_call(
        body, name="in_bwd", grid=(steps,),
        in_specs=[pl.BlockSpec((tm, D), lambda i: (i, 0)), pl.BlockSpec((tm, D), lambda i: (i, 0)),
                  pl.BlockSpec((tm, PROJ_W), lambda i: (i, 0)),
                  pl.BlockSpec((1, 8, D), lambda i: (i // tps, 0, 0)), pl.BlockSpec((1, D), lambda i: (0, 0)),
                  whole, pl.BlockSpec((1, 8, D), lambda i: (i // tps, 0, 0)), pl.BlockSpec((8, D), lambda i: (0, 0))],
        out_specs=[pl.BlockSpec((tm, D), lambda i: (i, 0)),
                   pl.BlockSpec((1, 8, D), lambda i: (i // tps, 0, 0)), pl.BlockSpec((8, D), lambda i: (0, 0))],
        out_shape=[jax.ShapeDtypeStruct((T, D), F32),
                   jax.ShapeDtypeStruct((nb, 8, D), F32), jax.ShapeDtypeStruct((8, D), F32)],
        compiler_params=_cparams(),
    )(x2, dh1, dpb, mod, g_mix, w_cat, dmod_a, acc_a)


def _dw_in(u_b, dpb, in_cols):
    T = u_b.shape[0]
    bk = min(512, T)
    nk = T // bk
    starts = [(in_cols * j // LANES) * LANES for j in range(N_DEV)]
    assert all(s + DW_IN_WIN <= PROJ_W and in_cols * (j + 1) <= s + DW_IN_WIN for j, s in enumerate(starts))

    def body(u_ref, d_ref, o_ref, acc):
        k = pl.program_id(0)

        @pl.when(k == 0)
        def _():
            acc[...] = jnp.zeros_like(acc)

        ut = u_ref[...].T
        for j in range(N_DEV):
            acc[j] += jnp.dot(ut, d_ref[:, starts[j]:starts[j] + DW_IN_WIN], preferred_element_type=F32)

        @pl.when(k == nk - 1)
        def _():
            for j in range(N_DEV):
                off = in_cols * j - starts[j]
                o_ref[j] = acc[j][:, off:off + in_cols].astype(BF16)

    return pl.pallas_call(
        body, name="dw_in", grid=(nk,),
        in_specs=[pl.BlockSpec((bk, D), lambda k: (k, 0)), pl.BlockSpec((bk, PROJ_W), lambda k: (k, 0))],
        out_specs=pl.BlockSpec((N_DEV, D, in_cols), lambda k: (0, 0, 0)),
        out_shape=jax.ShapeDtypeStruct((N_DEV, D, in_cols), BF16),
        scratch_shapes=[pltpu.VMEM((N_DEV, D, DW_IN_WIN), F32)],
        compiler_params=_cparams(),
    )(u_b, dpb)


def _dw_blocks(a, b, name, by_rows, per_step=1):
    T, M = a.shape
    N = b.shape[1]
    bk = min(1024, T)
    nk = T // bk
    whole = pl.BlockSpec(memory_space=pltpu.VMEM)
    if by_rows:
        rows = M // N_DEV
        am = rows * per_step
        nblk = N_DEV // per_step
        a_spec, b_spec = pl.BlockSpec((bk, am), lambda i, k: (k, i)), whole
        out_blk, acc_shape = (per_step, rows, N), (am, N)
    else:
        cols = N // N_DEV
        nblk = N_DEV
        a_spec, b_spec = whole, pl.BlockSpec((bk, cols), lambda i, k: (k, i))
        out_blk, acc_shape = (1, M, cols), (M, cols)

    def body(a_ref, b_ref, o_ref, acc):
        k = pl.program_id(1)

        @pl.when(k == 0)
        def _():
            acc[...] = jnp.zeros_like(acc)

        tok = pl.ds(pl.multiple_of(k * bk, bk), bk)
        a_blk = a_ref[...] if by_rows else a_ref[tok, :]
        b_blk = b_ref[tok, :] if by_rows else b_ref[...]
        acc[...] += lax.dot_general(a_blk, b_blk, (((0,), (0,)), ((), ())), preferred_element_type=F32)

        @pl.when(k == nk - 1)
        def _():
            o_ref[...] = acc[...].reshape(out_blk).astype(BF16)

    return pl.pallas_call(
        body, name=name, grid=(nblk, nk), in_specs=[a_spec, b_spec],
        out_specs=pl.BlockSpec(out_blk, lambda i, k: (i, 0, 0)),
        out_shape=jax.ShapeDtypeStruct((N_DEV,) + out_blk[1:], BF16),
        scratch_shapes=[pltpu.VMEM(acc_shape, F32)],
        compiler_params=_cparams(),
    )(a, b)


def _adam_parts(parts, w, m, v, name):
    rows, cols = w.shape
    br = rows
    for cand in range(rows, 15, -16):
        if rows % cand == 0 and cand * cols * 4 <= ADAM_BLOCK_BYTES:
            br = cand
            break

    def body(p_ref, w_ref, m_ref, v_ref, g_out, dl_out, m_out, v_out):
        g = p_ref[0].astype(F32)
        for k in range(1, N_DEV):
            g = g + p_ref[k].astype(F32)
        g_out[...] = g
        dl, mn, vn = _adam_math(w_ref[...], g, m_ref[...], v_ref[...])
        dl_out[...] = dl
        m_out[...] = mn
        v_out[...] = vn

    wspec = pl.BlockSpec((br, cols), lambda i: (i, 0))
    return pl.pallas_call(
        body, name=name, grid=(rows // br,),
        in_specs=[pl.BlockSpec((N_DEV, br, cols), lambda i: (0, i, 0)), wspec, wspec, wspec],
        out_specs=[wspec] * 4, out_shape=[jax.ShapeDtypeStruct((rows, cols), F32)] * 4,
        compiler_params=_cparams(),
    )(parts, w, m, v)


def _adam_plain(g, w, m, v, name):
    def body(g_ref, w_ref, m_ref, v_ref, dl_out, m_out, v_out):
        dl, mn, vn = _adam_math(w_ref[...], g_ref[...], m_ref[...], v_ref[...])
        dl_out[...] = dl
        m_out[...] = mn
        v_out[...] = vn

    return pl.pallas_call(body, name=name, out_shape=[jax.ShapeDtypeStruct(w.shape, F32)] * 3,
                          compiler_params=_cparams())(g, w, m, v)


SMALL_PARAMS = ("b_ada", "g_mix", "conv_b", "dt_bias", "a_log", "d_skip", "g_ssd", "pool_scale", "g_mlp", "g_final")


def _small_adam(gathered, params):
    n_par = len(SMALL_PARAMS)
    nb = gathered[0].shape[1]

    def body(*refs):
        dmod_ref, acc_ref, conv_ref, vec_ref, hd_ref = refs[:5]
        par_refs = refs[5:5 + 3 * n_par]
        out_refs = refs[5 + 3 * n_par:5 + 7 * n_par]
        cw_out, acc_out = refs[5 + 7 * n_par:]

        def total(ref):
            t = ref[0]
            for k in range(1, N_DEV):
                t = t + ref[k]
            return t

        dm = total(dmod_ref)
        dmb = dm[0]
        for b in range(1, nb):
            dmb = dmb + dm[b]
        ac, cv, vc, hd = total(acc_ref), total(conv_ref), total(vec_ref), total(hd_ref)
        cw_out[...] = cv[0:4]
        acc_out[...] = ac
        grads = {
            "b_ada": jnp.concatenate([dmb[r:r + 1] for r in range(6)], axis=1), "g_mix": ac[3:4], "conv_b": cv[4:5],
            "dt_bias": hd[0:1, 0:N_HEADS], "a_log": hd[1:2, 0:N_HEADS], "d_skip": hd[2:3, 0:N_HEADS],
            "g_ssd": vc[0:1], "pool_scale": vc[1:2, 0:POOL_W], "g_mlp": ac[0:1], "g_final": ac[1:2],
        }
        for i, name in enumerate(SMALL_PARAMS):
            w_ref, m_ref, v_ref = par_refs[3 * i:3 * i + 3]
            g = grads[name]
            dl, mn, vn = _adam_math(w_ref[...], g, m_ref[...], v_ref[...])
            g_o, d_o, m_o, v_o = out_refs[4 * i:4 * i + 4]
            g_o[...] = g
            d_o[...] = dl
            m_o[...] = mn
            v_o[...] = vn

    flat = [a for name in SMALL_PARAMS for a in params[name]]
    out_shape = [jax.ShapeDtypeStruct(params[name][0].shape, F32) for name in SMALL_PARAMS for _ in range(4)]
    out_shape += [jax.ShapeDtypeStruct((4, CONV_CH), F32), jax.ShapeDtypeStruct((8, D), F32)]
    return pl.pallas_call(body, name="small_adam", out_shape=out_shape, compiler_params=_cparams())(*gathered, *flat)


def kernel(x, c, w_ada, b_ada, g_mix, w_in, conv_w, conv_b, dt_bias, a_log, d_skip, g_ssd, w_pool, pool_scale, w_out, g_mlp, w_up, w_down, g_final, loss_target, m_w_ada, m_b_ada, m_g_mix, m_w_in, m_conv_w, m_conv_b, m_dt_bias, m_a_log, m_d_skip, m_g_ssd, m_w_pool, m_pool_scale, m_w_out, m_g_mlp, m_w_up, m_w_down, m_g_final, v_w_ada, v_b_ada, v_g_mix, v_w_in, v_conv_w, v_conv_b, v_dt_bias, v_a_log, v_d_skip, v_g_ssd, v_w_pool, v_pool_scale, v_w_out, v_g_mlp, v_w_up, v_w_down, v_g_final):
    nb, seq, _ = x.shape
    T = nb * seq
    me = 4 * lax.axis_index("x") + 2 * lax.axis_index("y") + lax.axis_index("c")
    in_cols = w_in.shape[2]
    ada_cols = w_ada.shape[2]
    cw_cols = conv_w.shape[2]

    c_g, cw_g, win_g = _all_gather([c, conv_w[0], w_in[0].astype(BF16)], "ag_first")
    c_all = c_g.reshape(N_DEV * nb, D)
    cw_full = cw_g.transpose(1, 0, 2).reshape(4, CONV_CH)

    b_slice = lax.dynamic_slice(b_ada, (0, me * ada_cols), (1, ada_cols))
    mod_cols = _ada_fwd(c_all, w_ada[0], b_slice)
    (mod_g,) = _all_gather([mod_cols], "ag_mod")
    mod_all = mod_g.transpose(1, 0, 2).reshape(N_DEV * nb, 6, D)
    mod_mine = lax.dynamic_slice(mod_all, (me * nb, 0, 0), (nb, 6, D))
    mod = jnp.pad(mod_mine, ((0, 0), (0, 2), (0, 0)))

    x2 = x.reshape(T, D)
    tg2 = loss_target.reshape(T, D)
    heads = jnp.pad(jnp.concatenate([dt_bias, a_log, d_skip], axis=0), ((0, 5), (0, LANES - N_HEADS)))
    wpool_b = w_pool[0]
    u_b, pm, w_cat = _mix_in(x2, mod, g_mix, win_g, seq)
    ymix, hstates, wout_g, wup_g, wdn_g = _mixer_fwd(
        pm, cw_full, conv_b, heads, g_ssd, wpool_b, pool_scale, nb, seq,
        [w_out[0].astype(BF16), w_up[0].astype(BF16), w_down[0].astype(BF16)])
    da_b, dym, dh1, u2_b, f_b, dup_b, ddn_b, dmod_a, acc_a = _mlp_fused(
        x2, ymix, tg2, mod, g_mlp, g_final.reshape(1, D), wout_g.reshape(MIX_W, D), wup_g, wdn_g, seq)

    gout_p = _dw_blocks(ymix, da_b, "dw_out", True, per_step=4)
    gup_p = _dw_blocks(u2_b, dup_b, "dw_up", False)
    gdn_p = _dw_blocks(f_b, ddn_b, "dw_down", True)
    dpb, d_conv, d_heads, d_vec, d_wpool, gout_r, gup_r, gdn_r = _mixer_bwd(
        pm, dym, hstates, cw_full, conv_b, heads, g_ssd, wpool_b, pool_scale, nb, seq, [gout_p, gup_p, gdn_p])
    gin_p = _dw_in(u_b, dpb, in_cols)
    ex_send, ex_recv, gin_thru, gin_land, ex_token = _exchange_start(gin_p, "gin_start")
    grad_x2, dmod, acc = _in_bwd(x2, dh1, dpb, mod, g_mix + ex_token[0:1, 0:1], w_cat, dmod_a, acc_a, seq)

    g_out, d_out, nm_out, nv_out = _adam_parts(gout_r, w_out[0], m_w_out[0], v_w_out[0], "adam_w_out")
    g_up, d_up, nm_up, nv_up = _adam_parts(gup_r, w_up[0], m_w_up[0], v_w_up[0], "adam_w_up")
    g_dn, d_dn, nm_dn, nv_dn = _adam_parts(gdn_r, w_down[0], m_w_down[0], v_w_down[0], "adam_w_down")

    dmod_g, acc_g, conv_g, vec_g, heads_g, wpool_parts = _all_gather(
        [dmod, acc, d_conv, d_vec, d_heads, d_wpool.reshape(4 * LANES, LANES)], "ag_small_bwd")
    pool2 = (4 * LANES, LANES)
    wpool_outs = _adam_parts(wpool_parts, w_pool.reshape(pool2), m_w_pool.reshape(pool2), v_w_pool.reshape(pool2),
                             "adam_w_pool")
    small_params = {
        "b_ada": (b_ada, m_b_ada, v_b_ada), "g_mix": (g_mix, m_g_mix, v_g_mix), "conv_b": (conv_b, m_conv_b, v_conv_b),
        "dt_bias": (dt_bias, m_dt_bias, v_dt_bias), "a_log": (a_log, m_a_log, v_a_log),
        "d_skip": (d_skip, m_d_skip, v_d_skip), "g_ssd": (g_ssd, m_g_ssd, v_g_ssd),
        "pool_scale": (pool_scale, m_pool_scale, v_pool_scale), "g_mlp": (g_mlp, m_g_mlp, v_g_mlp),
        "g_final": tuple(a.reshape(1, D) for a in (g_final, m_g_final, v_g_final)),
    }
    small_res = _small_adam([dmod_g, acc_g, conv_g, vec_g, heads_g], small_params)
    g_cw_full, acc_sum = small_res[-2:]
    loss = acc_sum[2, 0]

    g_cw = lax.dynamic_slice(g_cw_full, (0, me * cw_cols), (4, cw_cols))
    d_cwp, nm_cwp, nv_cwp = _adam_plain(g_cw, conv_w[0], m_conv_w[0], v_conv_w[0], "adam_conv_w")

    dmod_all = dmod_g[:, :, 0:6].reshape(N_DEV * nb, 6 * D)
    dmod_slice = lax.dynamic_slice(dmod_all, (0, me * ada_cols), (N_DEV * nb, ada_cols))
    g_ada, d_ada, nm_ada, nv_ada = _ada_bwd_adam(c_all, dmod_slice, w_ada[0], m_w_ada[0], v_w_ada[0])

    ex_after = nm_dn[0:8, 0:LANES] + nm_up[0:8, 0:LANES] + nm_out[0:8, 0:LANES] + nm_ada[0:8, 0:LANES] + acc_sum[:, 0:LANES]
    gin_landed = _exchange_wait(ex_send, ex_recv, gin_thru, gin_land, ex_after, "gin_wait")
    gin_r = lax.dynamic_update_slice(gin_landed, lax.dynamic_slice(gin_p, (me, 0, 0), (1, D, in_cols)), (me, 0, 0))
    g_in, d_in, nm_in, nv_in = _adam_parts(gin_r, w_in[0], m_w_in[0], v_w_in[0], "adam_w_in")

    def small_outs(kind, wpool):
        res = {name: small_res[4 * i + kind] for i, name in enumerate(SMALL_PARAMS)}
        res["g_final"] = res["g_final"].reshape(D)
        res["w_pool"] = wpool.reshape(1, 4, LANES, LANES)
        return res

    def big_outs(ada, win, cwp, wout, wup, wdn):
        return {"w_ada": ada[None], "w_in": win.reshape(1, D, in_cols), "conv_w": cwp[None], "w_out": wout[None],
                "w_up": wup[None], "w_down": wdn[None]}

    order = ["w_ada", "b_ada", "g_mix", "w_in", "conv_w", "conv_b", "dt_bias", "a_log", "d_skip", "g_ssd", "w_pool",
             "pool_scale", "w_out", "g_mlp", "w_up", "w_down", "g_final"]
    groups = [
        {**small_outs(0, wpool_outs[0]), **big_outs(g_ada, g_in, g_cw, g_out, g_up, g_dn)},
        {**small_outs(1, wpool_outs[1]), **big_outs(d_ada, d_in, d_cwp, d_out, d_up, d_dn)},
        {**small_outs(2, wpool_outs[2]), **big_outs(nm_ada, nm_in, nm_cwp, nm_out, nm_up, nm_dn)},
        {**small_outs(3, wpool_outs[3]), **big_outs(nv_ada, nv_in, nv_cwp, nv_out, nv_up, nv_dn)},
    ]
    outs = [loss, grad_x2.reshape(nb, seq, D)]
    for grp in groups:
        outs += [grp[n] for n in order]
    return tuple(outs)
```

```python
import functools

import jax
import jax.numpy as jnp
from jax import lax
from jax.experimental import pallas as pl
from jax.experimental.pallas import tpu as pltpu

F32, BF16 = jnp.float32, jnp.bfloat16
MESH = pl.DeviceIdType.MESH
N_DEV = 8
D = 1024
LANES = 128
CHUNK = 128
POOL_W = 512
WINDOWS = (2, 4, 8, 16)
N_HEADS = 16
HEAD_DIM = 64
N_GROUPS = 2
GROUP_W = 512
N_STATE = 128
CONV_CH = 1536
OFF_Z, OFF_XBC, OFF_DT, IN_W = 512, 1536, 3072, 3088
PROJ_W = OFF_DT + LANES
MIX_W = 1536
D_FF = 4096
FF_BLK = 512
EPS = 1e-5
LR, B1, B2, AEPS, WD, STEP = 0.001, 0.9, 0.999, 1e-08, 0.01, 10
POOL_HALO = 16
CONV_HALO = 8
VMEM_LIMIT = 56 << 20
ADAM_BLOCK_BYTES = 1 << 20
DW_IN_WIN = 512


def _cparams(**kw):
    return pltpu.CompilerParams(vmem_limit_bytes=VMEM_LIMIT, **kw)


def _mm(a, b):
    return jnp.dot(a.astype(BF16), b.astype(BF16), preferred_element_type=F32)


def _mm_nt(a, b):
    return lax.dot_general(a.astype(BF16), b.astype(BF16), (((1,), (1,)), ((), ())), preferred_element_type=F32)


def _mm_tn(a, b):
    return lax.dot_general(a.astype(BF16), b.astype(BF16), (((0,), (0,)), ((), ())), preferred_element_type=F32)


def _split_bf16(v, terms):
    parts, rest = [], v
    for t in range(terms):
        p = rest.astype(BF16)
        parts.append(p)
        if t + 1 < terms:
            rest = rest - p.astype(F32)
    return parts


def _dot01(a, b, terms, split_lhs=True):
    if split_lhs:
        bb = b.astype(BF16)
        prods = [jnp.dot(p, bb, preferred_element_type=F32) for p in _split_bf16(a, terms)]
    else:
        ab = a.astype(BF16)
        prods = [jnp.dot(ab, p, preferred_element_type=F32) for p in _split_bf16(b, terms)]
    out = prods[0]
    for q in prods[1:]:
        out = out + q
    return out


def _sigmoid(v):
    return 1.0 / (1.0 + jnp.exp(-v))


def _expand_mat():
    r = lax.broadcasted_iota(jnp.int32, (LANES, D), 0)
    c = lax.broadcasted_iota(jnp.int32, (LANES, D), 1)
    return (r == c // HEAD_DIM).astype(F32)


def _reduce_mat():
    r = lax.broadcasted_iota(jnp.int32, (D, LANES), 0)
    c = lax.broadcasted_iota(jnp.int32, (D, LANES), 1)
    return (c == r // HEAD_DIM).astype(F32)


def _pos():
    return lax.axis_index("x"), lax.axis_index("y"), lax.axis_index("c")


class _Gather:
    def __init__(self, x_refs, o_refs, send, recv, loc):
        self.x_refs, self.o_refs, self.send, self.recv, self.loc = x_refs, o_refs, send, recv, loc
        self.n = len(x_refs)
        x, y, c = _pos()
        self.c = c
        self.me, self.sib = (x, y, c), (x, y, 1 - c)
        self.chips = [(1 - x, y), (x, 1 - y), (1 - x, 1 - y)]

    def _cp(self, a, k, block, to, src=None):
        dst = self.o_refs[a].at[4 * block[0] + 2 * block[1] + block[2]]
        return pltpu.make_async_remote_copy(
            src_ref=dst if src is None else src, dst_ref=dst,
            send_sem=self.send.at[a * 7 + k], recv_sem=self.recv.at[a * 7 + k],
            device_id=to, device_id_type=MESH)

    def _mine(self, a):
        me = self.me
        return pltpu.make_async_copy(self.x_refs[a], self.o_refs[a].at[4 * me[0] + 2 * me[1] + me[2]], self.loc.at[a])

    def _first(self, a):
        cps = [self._cp(a, 0, self.me, self.sib, src=self.x_refs[a])]
        return cps + [self._cp(a, 1 + j, self.me, (*chip, self.c), src=self.x_refs[a])
                      for j, chip in enumerate(self.chips)]

    def _passed(self, a, j):
        return self._cp(a, 4 + j, (*self.chips[j], self.c), self.sib)

    def start(self):
        for a in range(self.n):
            self._mine(a).start()
            for cp in self._first(a):
                cp.start()

    def forward(self):
        for j, chip in enumerate(self.chips):
            for a in range(self.n):
                self._cp(a, 1 + j, (*chip, self.c), self.me).wait_recv()
                self._passed(a, j).start()

    def finish(self):
        for a in range(self.n):
            self._cp(a, 0, self.sib, self.me).wait_recv()
            for j, chip in enumerate(self.chips):
                self._cp(a, 4 + j, (*chip, 1 - self.c), self.me).wait_recv()
        for a in range(self.n):
            for cp in self._first(a):
                cp.wait_send()
            for j in range(3):
                self._passed(a, j).wait_send()
            self._mine(a).wait()


class _Exchange:
    def __init__(self, x_refs, o_refs, send, recv, loc):
        self.x_refs, self.o_refs, self.send, self.recv, self.loc = x_refs, o_refs, send, recv, loc
        self.n = len(x_refs)
        x, y, c = _pos()
        self.me_i = 4 * x + 2 * y + c
        self.peers = []
        for k in range(1, N_DEV):
            px = 1 - x if (k >> 2) & 1 else x
            py = 1 - y if (k >> 1) & 1 else y
            pc = 1 - c if k & 1 else c
            self.peers.append(((px, py, pc), 4 * px + 2 * py + pc))

    def _mine(self, a):
        return pltpu.make_async_copy(self.x_refs[a].at[self.me_i], self.o_refs[a].at[self.me_i], self.loc.at[a])

    def _cp(self, a, k, landing):
        peer, peer_i = self.peers[k]
        return pltpu.make_async_remote_copy(
            src_ref=self.x_refs[a].at[peer_i], dst_ref=self.o_refs[a].at[landing],
            send_sem=self.send.at[a * 7 + k], recv_sem=self.recv.at[a * 7 + k],
            device_id=peer, device_id_type=MESH)

    def start(self):
        for a in range(self.n):
            self._mine(a).start()
            for k in range(N_DEV - 1):
                self._cp(a, k, self.me_i).start()

    def finish(self):
        for a in range(self.n):
            for k in range(N_DEV - 1):
                self._cp(a, k, self.peers[k][1]).wait_recv()
        for a in range(self.n):
            for k in range(N_DEV - 1):
                self._cp(a, k, self.me_i).wait_send()
            self._mine(a).wait()


def _comm_scratch(n):
    return [pltpu.SemaphoreType.DMA((7 * n,)), pltpu.SemaphoreType.DMA((7 * n,)), pltpu.SemaphoreType.DMA((n,))]


ANY_SPEC = pl.BlockSpec(memory_space=pl.ANY)


def _all_gather(xs, name, after=()):
    n, na = len(xs), len(after)

    def body(*refs):
        g = _Gather(refs[:n], refs[n + na:2 * n + na], *refs[2 * n + na:])
        g.start()
        g.forward()
        g.finish()

    return pl.pallas_call(
        body, name=name,
        out_shape=[jax.ShapeDtypeStruct((N_DEV,) + v.shape, v.dtype) for v in xs],
        in_specs=[ANY_SPEC] * (n + na), out_specs=[ANY_SPEC] * n, scratch_shapes=_comm_scratch(n),
    )(*xs, *after)


HBM_SPEC = pl.BlockSpec(memory_space=pltpu.HBM)
SEM_SPEC = pl.BlockSpec(memory_space=pltpu.SEMAPHORE)
VMEM_SPEC = pl.BlockSpec(memory_space=pltpu.VMEM)
SPLIT_EFFECT = pltpu.SideEffectType.DATAFLOW_SIDE_EFFECTING


def _in_hbm(v):
    return pltpu.with_memory_space_constraint(v, pltpu.HBM)


def _gather_start(shards, name):
    n = len(shards)

    def body(*refs):
        x_refs, land_refs = refs[:n], refs[n:2 * n]
        send, recv = refs[2 * n:2 * n + 2]
        token = refs[-1]
        g = _Gather(x_refs, land_refs, send, recv, None)
        for a in range(n):
            for cp in g._first(a):
                cp.start()
        token[...] = jnp.zeros_like(token)

    lands = [lax.empty((N_DEV,) + v.shape, v.dtype) for v in shards]
    hbm = [pltpu.HBM(v.shape, v.dtype) for v in list(shards) + lands]
    return pl.pallas_call(
        body, name=name,
        out_shape=(pltpu.SemaphoreType.DMA((7 * n,)), pltpu.SemaphoreType.DMA((7 * n,)), *hbm,
                   jax.ShapeDtypeStruct((8, LANES), F32)),
        in_specs=(HBM_SPEC,) * (2 * n), out_specs=(SEM_SPEC, SEM_SPEC) + (HBM_SPEC,) * (2 * n) + (VMEM_SPEC,),
        input_output_aliases={i: i + 2 for i in range(2 * n)},
        compiler_params=pltpu.CompilerParams(has_side_effects=SPLIT_EFFECT),
    )(*[_in_hbm(v) for v in list(shards) + lands])


def _gather_wait(send, recv, thru, after, name):
    n = len(thru) // 2

    def body(*refs):
        x_refs, land_refs = refs[:n], refs[n:2 * n]
        send_ref, recv_ref = refs[2 * n:2 * n + 2]
        g = _Gather(x_refs, land_refs, send_ref, recv_ref, None)
        for a in range(n):
            for cp in g._first(a):
                cp.wait_send()
            g._cp(a, 0, g.sib, g.me).wait_recv()
            for j, chip in enumerate(g.chips):
                g._cp(a, 1 + j, (*chip, g.c), g.me).wait_recv()

    hbm = tuple(pltpu.HBM(v.shape, v.dtype) for v in thru)
    return pl.pallas_call(
        body, name=name, out_shape=hbm,
        in_specs=(HBM_SPEC,) * (2 * n) + (SEM_SPEC, SEM_SPEC, ANY_SPEC), out_specs=(HBM_SPEC,) * (2 * n),
        input_output_aliases={i: i for i in range(2 * n)},
        compiler_params=pltpu.CompilerParams(has_side_effects=SPLIT_EFFECT),
    )(*thru, send, recv, after)


def _gather_forward(shards, lands, name):
    n = len(shards)

    def body(*refs):
        x_refs = refs[:n]
        land_refs = refs[2 * n:3 * n]
        send, recv, loc = refs[3 * n:]
        g = _Gather(x_refs, land_refs, send, recv, loc)
        for a in range(n):
            g._mine(a).start()
            for j in range(3):
                g._passed(a, j).start()
        for a in range(n):
            for j, chip in enumerate(g.chips):
                g._cp(a, 4 + j, (*chip, 1 - g.c), g.me).wait_recv()
        for a in range(n):
            for j in range(3):
                g._passed(a, j).wait_send()
            g._mine(a).wait()

    return pl.pallas_call(
        body, name=name, out_shape=[jax.ShapeDtypeStruct(v.shape, v.dtype) for v in lands],
        in_specs=[ANY_SPEC] * (2 * n), out_specs=[ANY_SPEC] * n, scratch_shapes=_comm_scratch(n),
        input_output_aliases={n + i: i for i in range(n)},
    )(*shards, *lands)


def _exchange_start(blocks, name):
    def body(x_ref, land_ref, send, recv, x_thru, land_thru, token):
        ex = _Exchange([x_ref], [land_ref], send, recv, None)
        for k in range(N_DEV - 1):
            ex._cp(0, k, ex.me_i).start()
        token[...] = jnp.zeros_like(token)

    hbm = pltpu.HBM(blocks.shape, blocks.dtype)
    return pl.pallas_call(
        body, name=name,
        out_shape=(pltpu.SemaphoreType.DMA((N_DEV - 1,)), pltpu.SemaphoreType.DMA((N_DEV - 1,)), hbm, hbm,
                   jax.ShapeDtypeStruct((8, LANES), F32)),
        in_specs=(HBM_SPEC, HBM_SPEC), out_specs=(SEM_SPEC, SEM_SPEC, HBM_SPEC, HBM_SPEC, VMEM_SPEC),
        input_output_aliases={0: 2, 1: 3},
        compiler_params=pltpu.CompilerParams(has_side_effects=SPLIT_EFFECT),
    )(_in_hbm(blocks), _in_hbm(lax.empty(blocks.shape, blocks.dtype)))


def _exchange_wait(send, recv, x_thru, land_thru, after, name):
    def body(x_ref, land_ref, send_ref, recv_ref, after_ref, x_dead, got_ref):
        ex = _Exchange([x_ref], [land_ref], send_ref, recv_ref, None)
        for k in range(N_DEV - 1):
            ex._cp(0, k, ex.me_i).wait_send()
            ex._cp(0, k, ex.peers[k][1]).wait_recv()

    hbm = pltpu.HBM(x_thru.shape, x_thru.dtype)
    return pl.pallas_call(
        body, name=name, out_shape=(hbm, hbm),
        in_specs=(HBM_SPEC, HBM_SPEC, SEM_SPEC, SEM_SPEC, ANY_SPEC), out_specs=(HBM_SPEC, HBM_SPEC),
        input_output_aliases={0: 0, 1: 1},
        compiler_params=pltpu.CompilerParams(has_side_effects=SPLIT_EFFECT),
    )(x_thru, land_thru, send, recv, after)


def _ada_fwd(c_all, w_ada, b_slice):
    def body(c_ref, w_ref, b_ref, o_ref):
        cv = c_ref[...]
        act = cv * _sigmoid(cv)
        o_ref[...] = _mm(act, w_ref[...]) + b_ref[...]

    nb, nc = c_all.shape[0], w_ada.shape[1]
    return pl.pallas_call(body, name="ada_fwd", out_shape=jax.ShapeDtypeStruct((nb, nc), F32),
                          compiler_params=_cparams())(c_all, w_ada, b_slice)


def _adam_math(w, g, m, v):
    m = B1 * m + (1.0 - B1) * g
    v = B2 * v + (1.0 - B2) * jnp.square(g)
    m_hat = m / (1.0 - B1 ** STEP)
    v_hat = v / (1.0 - B2 ** STEP)
    delta = -LR * (m_hat / (jnp.sqrt(v_hat) + AEPS) + WD * w)
    return delta, m, v


def _ada_bwd_adam(c_all, dmod_slice, w, m, v):
    rows, cols = w.shape
    br = 256

    def body(c_ref, d_ref, w_ref, m_ref, v_ref, g_out, dl_out, m_out, v_out):
        cv = c_ref[...]
        act = cv * _sigmoid(cv)
        g = _mm_tn(act, d_ref[...])
        g_out[...] = g
        dl, mn, vn = _adam_math(w_ref[...], g, m_ref[...], v_ref[...])
        dl_out[...] = dl
        m_out[...] = mn
        v_out[...] = vn

    nb = c_all.shape[0]
    wspec = pl.BlockSpec((br, cols), lambda i: (i, 0))
    return pl.pallas_call(
        body, name="ada_bwd_adam", grid=(rows // br,),
        in_specs=[pl.BlockSpec((nb, br), lambda i: (0, i)), pl.BlockSpec((nb, cols), lambda i: (0, 0)),
                  wspec, wspec, wspec],
        out_specs=[wspec] * 4, out_shape=[jax.ShapeDtypeStruct((rows, cols), F32)] * 4,
        compiler_params=_cparams(),
    )(c_all, dmod_slice, w, m, v)


def _mix_in(x2, mod, g_mix, win_g, seq):
    T = x2.shape[0]
    tm = min(512, seq)
    tps = seq // tm
    in_cols = win_g.shape[2]

    def body(x_ref, mod_ref, g_ref, wb_ref, u_ref, pm_ref, wc_ref, w_ref):
        @pl.when(pl.program_id(0) == 0)
        def _():
            w_ref[:, OFF_DT:] = jnp.zeros((D, PROJ_W - OFF_DT), BF16)
            for j in range(N_DEV):
                w_ref[:, in_cols * j:in_cols * (j + 1)] = wb_ref[j]
            wc_ref[...] = w_ref[...]

        x = x_ref[...]
        r = lax.rsqrt(jnp.mean(x * x, axis=-1, keepdims=True) + EPS)
        md = mod_ref[0]
        u = (x * r * g_ref[...]) * (1.0 + md[1:2]) + md[0:1]
        ub = u.astype(BF16)
        u_ref[...] = ub
        pm_ref[...] = jnp.dot(ub, w_ref[...], preferred_element_type=F32)

    whole = pl.BlockSpec(memory_space=pltpu.VMEM)
    return pl.pallas_call(
        body, name="mix_in", grid=(T // tm,),
        in_specs=[pl.BlockSpec((tm, D), lambda i: (i, 0)), pl.BlockSpec((1, 8, D), lambda i: (i // tps, 0, 0)),
                  pl.BlockSpec((1, D), lambda i: (0, 0)), whole],
        out_specs=[pl.BlockSpec((tm, D), lambda i: (i, 0)), pl.BlockSpec((tm, PROJ_W), lambda i: (i, 0)),
                   pl.BlockSpec((D, PROJ_W), lambda i: (0, 0))],
        out_shape=[jax.ShapeDtypeStruct((T, D), BF16), jax.ShapeDtypeStruct((T, PROJ_W), F32),
                   jax.ShapeDtypeStruct((D, PROJ_W), BF16)],
        scratch_shapes=[pltpu.VMEM((D, PROJ_W), BF16)],
        compiler_params=_cparams(),
    )(x2, mod, g_mix, win_g)


def _chunk_forward(up, z, ux, dtin, halo_p, halo_x, hprev, cw, cb, hp, gssd, wpool, pscale, t0, y_scr):
    L = CHUNK
    out = {}
    row = lax.broadcasted_iota(jnp.int32, (L, 1), 0)
    t = (t0 + row + 1).astype(F32)
    e = jnp.concatenate([halo_p, up], axis=0)
    s2 = e + pltpu.roll(e, 1, 0)
    s4 = s2 + pltpu.roll(s2, 2, 0)
    s8 = s4 + pltpu.roll(s4, 4, 0)
    s16 = s8 + pltpu.roll(s8, 8, 0)
    sums = (s2, s4, s8, s16)
    p, inv, yp = [], [], []
    for gi, w in enumerate(WINDOWS):
        sl = slice(gi * LANES, (gi + 1) * LANES)
        ic = 1.0 / jnp.minimum(t, float(w))
        pg = sums[gi][POOL_HALO:, sl] * ic - up[:, sl]
        p.append(pg)
        inv.append(ic)
        yp.append(_mm(pg, wpool[gi]))
    out["p"], out["inv"], out["yp"] = p, inv, yp
    out["y_pool"] = jnp.concatenate(yp, axis=1) * pscale
    ex = jnp.concatenate([halo_x, ux], axis=0)
    taps = [pltpu.roll(ex, 3, 0)[CONV_HALO:], pltpu.roll(ex, 2, 0)[CONV_HALO:], pltpu.roll(ex, 1, 0)[CONV_HALO:], ux]
    cv = cb + taps[0] * cw[0:1] + taps[1] * cw[1:2] + taps[2] * cw[2:3] + taps[3] * cw[3:4]
    sg = _sigmoid(cv)
    xbc = cv * sg
    out["taps"], out["cv"], out["sg"] = taps, cv, sg
    X = xbc[:, :D]
    Bm = xbc[:, D:D + N_GROUPS * N_STATE]
    Cm = xbc[:, D + N_GROUPS * N_STATE:]
    pre = dtin + hp[0:1]
    dt = jnp.maximum(pre, 0.0) + jnp.log(1.0 + jnp.exp(-jnp.abs(pre)))
    a_row = -jnp.exp(hp[1:2])
    da = dt * a_row
    ri = lax.broadcasted_iota(jnp.int32, (L, L), 0)
    ci = lax.broadcasted_iota(jnp.int32, (L, L), 1)
    causal = ri >= ci
    cum = _dot01(causal.astype(F32), da, 3, split_lhs=False)
    cum_t = cum.T
    cum_last = cum[L - 1:L]
    eo = jnp.exp(cum)
    dec = jnp.exp(cum_last - cum)
    cd = jnp.exp(cum_last)
    exm = _expand_mat()
    rows8 = jnp.concatenate([cd, hp[2:3], jnp.zeros((6, LANES), F32)], axis=0)
    rep = _dot01(jnp.concatenate([dt, eo, dec, rows8], axis=0), exm, 2)
    dt_rep, eo_rep, dec_rep = rep[0:L], rep[L:2 * L], rep[2 * L:3 * L]
    cd_rep, dskip_rep = rep[3 * L:3 * L + 1], rep[3 * L + 1:3 * L + 2]
    xdt = X * dt_rep
    out.update(X=X, Bm=Bm, Cm=Cm, pre=pre, dt=dt, a_row=a_row, cum=cum, cum_t=cum_t, eo=eo, dec=dec, cd=cd,
               dt_rep=dt_rep, eo_rep=eo_rep, dec_rep=dec_rep, cd_rep=cd_rep, dskip_rep=dskip_rep, xdt=xdt,
               causal=causal, anti=(ri <= ci).astype(F32), exm=exm)
    G, lms, yoff, hnew, xdec = [], [], [], [], []
    for g in range(N_GROUPS):
        gs = slice(g * GROUP_W, (g + 1) * GROUP_W)
        Bg = Bm[:, g * N_STATE:(g + 1) * N_STATE]
        Cg = Cm[:, g * N_STATE:(g + 1) * N_STATE]
        Gg = _mm_nt(Cg, Bg)
        G.append(Gg)
        for hh in range(N_HEADS // N_GROUPS):
            h = g * (N_HEADS // N_GROUPS) + hh
            seg = cum[:, h:h + 1] - cum_t[h:h + 1, :]
            lm = jnp.where(causal, jnp.exp(jnp.minimum(seg, 0.0)), 0.0)
            lms.append(lm)
            hs = slice(h * HEAD_DIM, (h + 1) * HEAD_DIM)
            y_scr[:, hs] = _mm(Gg * lm, xdt[:, hs])
        xd = xdt[:, gs] * dec_rep[:, gs]
        xdec.append(xd)
        sgm = _mm_tn(Bg, xd)
        yoff.append(_mm(Cg, hprev[g]) * eo_rep[:, gs])
        hnew.append(hprev[g] * cd_rep[:, gs] + sgm)
    out.update(G=G, lms=lms, yoff=yoff, hnew=hnew, xdec=xdec)
    y = y_scr[...] + jnp.concatenate(yoff, axis=1) + dskip_rep * X
    sz = _sigmoid(z)
    silz = z * sz
    yz = y * silz
    rg, yn = [], []
    for g in range(N_GROUPS):
        gs = slice(g * GROUP_W, (g + 1) * GROUP_W)
        r = lax.rsqrt(jnp.mean(yz[:, gs] * yz[:, gs], axis=-1, keepdims=True) + EPS)
        rg.append(r)
        yn.append(yz[:, gs] * r)
    yn = jnp.concatenate(yn, axis=1)
    out.update(y=y, sz=sz, silz=silz, rg=rg, yn=yn)
    out["y_ssd"] = yn * gssd
    return out


def _mixer_fwd(pm, cw, cb, hp, gssd, wpool, pscale, nb, seq):
    nc = seq // CHUNK

    def body(pm_ref, cw_ref, cb_ref, hp_ref, gs_ref, wp_ref, ps_ref, ym_ref, hs_ref, halo_p, halo_x, state, y_scr):
        c = pl.program_id(1)

        @pl.when(c == 0)
        def _():
            halo_p[...] = jnp.zeros_like(halo_p)
            halo_x[...] = jnp.zeros_like(halo_x)
            state[...] = jnp.zeros_like(state)

        up = pm_ref[:, 0:POOL_W]
        z = pm_ref[:, OFF_Z:OFF_XBC]
        ux = pm_ref[:, OFF_XBC:OFF_DT]
        hprev = [state[0], state[1]]
        hs_ref[0, 0, 0] = hprev[0]
        hs_ref[0, 0, 1] = hprev[1]
        o = _chunk_forward(up, z, ux, pm_ref[:, OFF_DT:], halo_p[...], halo_x[...], hprev, cw_ref[...], cb_ref[...],
                           hp_ref[...], gs_ref[...], wp_ref[...], ps_ref[...], c * CHUNK, y_scr)
        ym_ref[:, 0:POOL_W] = o["y_pool"].astype(BF16)
        ym_ref[:, POOL_W:] = o["y_ssd"].astype(BF16)
        state[0] = o["hnew"][0]
        state[1] = o["hnew"][1]
        halo_p[...] = up[CHUNK - POOL_HALO:]
        halo_x[...] = ux[CHUNK - CONV_HALO:]

    def full(shape):
        return pl.BlockSpec(shape, lambda b, c: (0,) * len(shape))

    T = nb * seq
    return pl.pallas_call(
        body, name="mixer_fwd", grid=(nb, nc),
        in_specs=[pl.BlockSpec((CHUNK, PROJ_W), lambda b, c: (b * nc + c, 0)),
                  full((4, CONV_CH)), full((1, CONV_CH)), full((8, LANES)), full((1, D)),
                  full((4, LANES, LANES)), full((1, POOL_W))],
        out_specs=[pl.BlockSpec((CHUNK, MIX_W), lambda b, c: (b * nc + c, 0)),
                   pl.BlockSpec((1, 1, N_GROUPS, N_STATE, GROUP_W), lambda b, c: (b, c, 0, 0, 0))],
        out_shape=[jax.ShapeDtypeStruct((T, MIX_W), BF16),
                   jax.ShapeDtypeStruct((nb, nc, N_GROUPS, N_STATE, GROUP_W), F32)],
        scratch_shapes=[pltpu.VMEM((POOL_HALO, POOL_W), F32), pltpu.VMEM((CONV_HALO, CONV_CH), F32),
                        pltpu.VMEM((N_GROUPS, N_STATE, GROUP_W), F32), pltpu.VMEM((CHUNK, D), F32)],
        compiler_params=_cparams(),
    )(pm, cw, cb, hp, gssd, wpool, pscale)


def _mixer_bwd(pm, dym, hstates, cw, cb, hp, gssd, wpool, pscale, nb, seq, blocks):
    nc = seq // CHUNK
    hpg = N_HEADS // N_GROUPS
    ns = len(blocks)
    steps = nb * nc

    def body(*refs):
        (pm_ref, hpool_ref, hxbc_ref, dy_ref, hs_ref, cw_ref, cb_ref, hp_ref, gs_ref, wp_ref, ps_ref) = refs[:11]
        bl_refs = refs[11:11 + ns]
        dpm_ref, dconv_ref, dhp_ref, dvec_ref, dwp_ref = refs[11 + ns:16 + ns]
        ex_refs = refs[16 + ns:16 + 2 * ns]
        nxt_q, nxt_cv, rstate, y_scr, dx_scr, send, recv, loc = refs[16 + 2 * ns:]
        b = pl.program_id(0)
        ci = pl.program_id(1)
        c = nc - 1 - ci
        exchange = _Exchange(bl_refs, ex_refs, send, recv, loc)

        @pl.when((b == 0) & (ci == 0))
        def _():
            exchange.start()

        @pl.when((b == 0) & (ci == 0))
        def _():
            for r in (dconv_ref, dhp_ref, dvec_ref, dwp_ref):
                r[...] = jnp.zeros_like(r)

        @pl.when(ci == 0)
        def _():
            nxt_q[...] = jnp.zeros_like(nxt_q)
            nxt_cv[...] = jnp.zeros_like(nxt_cv)
            rstate[...] = jnp.zeros_like(rstate)

        first = (c > 0).astype(F32)
        up = pm_ref[:, 0:POOL_W]
        z = pm_ref[:, OFF_Z:OFF_XBC]
        ux = pm_ref[:, OFF_XBC:OFF_DT]
        halo_p = hpool_ref[...] * first
        halo_x = hxbc_ref[...] * first
        hprev = [hs_ref[0, 0, 0], hs_ref[0, 0, 1]]
        cw, cb, hp, gssd, wpool, pscale = cw_ref[...], cb_ref[...], hp_ref[...], gs_ref[...], wp_ref[...], ps_ref[...]
        o = _chunk_forward(up, z, ux, pm_ref[:, OFF_DT:], halo_p, halo_x, hprev, cw, cb, hp, gssd, wpool, pscale,
                           c * CHUNK, y_scr)
        L = CHUNK
        dy_pool = dy_ref[:, 0:POOL_W].astype(F32)
        dy_ssd = dy_ref[:, POOL_W:].astype(F32)

        dvec_ref[1:2, 0:POOL_W] += jnp.sum(dy_pool * jnp.concatenate(o["yp"], axis=1), axis=0, keepdims=True)
        dyp = dy_pool * pscale
        qs = []
        dps = []
        for gi in range(len(WINDOWS)):
            sl = slice(gi * LANES, (gi + 1) * LANES)
            dwp_ref[gi] += _mm_tn(o["p"][gi], dyp[:, sl])
            dpg = _mm_nt(dyp[:, sl], wpool[gi])
            dps.append(dpg)
            qs.append(dpg * o["inv"][gi])
        q = jnp.concatenate(qs, axis=1)
        e = jnp.concatenate([q, nxt_q[...]], axis=0)
        n = L + POOL_HALO
        s2 = e + pltpu.roll(e, n - 1, 0)
        s4 = s2 + pltpu.roll(s2, n - 2, 0)
        s8 = s4 + pltpu.roll(s4, n - 4, 0)
        s16 = s8 + pltpu.roll(s8, n - 8, 0)
        sums = (s2, s4, s8, s16)
        for gi in range(len(WINDOWS)):
            sl = slice(gi * LANES, (gi + 1) * LANES)
            dpm_ref[:, sl] = (sums[gi][:L, sl] - dps[gi]).astype(BF16)
        nxt_q[...] = q[:POOL_HALO]

        yn, y, silz, sz = o["yn"], o["y"], o["silz"], o["sz"]
        dvec_ref[0:1] += jnp.sum(dy_ssd * yn, axis=0, keepdims=True)
        dyn = dy_ssd * gssd
        dyz = []
        for g in range(N_GROUPS):
            gs = slice(g * GROUP_W, (g + 1) * GROUP_W)
            mean = jnp.mean(dyn[:, gs] * yn[:, gs], axis=-1, keepdims=True)
            dyz.append(o["rg"][g] * (dyn[:, gs] - yn[:, gs] * mean))
        dyz = jnp.concatenate(dyz, axis=1)
        dyv = dyz * silz
        dpm_ref[:, OFF_Z:OFF_XBC] = (dyz * y * (sz * (1.0 + z * (1.0 - sz)))).astype(BF16)

        X, Bm, Cm, xdt = o["X"], o["Bm"], o["Cm"], o["xdt"]
        exm = o["exm"]
        rdm = _reduce_mat()
        lane = lax.broadcasted_iota(jnp.int32, (1, LANES), 1)
        sub = lax.broadcasted_iota(jnp.int32, (LANES, 1), 0)
        dX = o["dskip_rep"] * dyv
        yoff_full = jnp.concatenate(o["yoff"], axis=1)
        rs = jnp.zeros((L, LANES), F32)
        cs_t = jnp.zeros((LANES, L), F32)
        dBs, dCs = [], []
        rh_sums = []
        ddec = []
        for g in range(N_GROUPS):
            gs = slice(g * GROUP_W, (g + 1) * GROUP_W)
            Bg = Bm[:, g * N_STATE:(g + 1) * N_STATE]
            Cg = Cm[:, g * N_STATE:(g + 1) * N_STATE]
            Gg = o["G"][g]
            R = rstate[g]
            dwm = dyv[:, gs] * o["eo_rep"][:, gs]
            dC = _mm_nt(dwm, hprev[g])
            dH = _mm_tn(Cg, dwm)
            dG = jnp.zeros((L, L), F32)
            for hh in range(hpg):
                h = g * hpg + hh
                hs = slice(h * HEAD_DIM, (h + 1) * HEAD_DIM)
                lm = o["lms"][h]
                m_h = Gg * lm
                dM = _mm_nt(dyv[:, hs], xdt[:, hs])
                dx_scr[:, hs] = _mm_tn(m_h, dyv[:, hs])
                qm = dM * m_h
                rs = rs + jnp.sum(qm, axis=1, keepdims=True) * (lane == h).astype(F32)
                cs_t = cs_t + (sub == h).astype(F32) * jnp.sum(qm, axis=0, keepdims=True)
                dG = dG + dM * lm
            dC = dC + _mm(dG, Bg)
            dB = _mm_tn(dG, Cg)
            zx = _mm(Bg, R)
            dxdt_state = zx * o["dec_rep"][:, gs]
            ddec.append(zx * xdt[:, gs])
            dB = dB + _mm_nt(o["xdec"][g], R)
            rh_sums.append(jnp.sum(R * hprev[g], axis=0, keepdims=True))
            rstate[g] = dH + o["cd_rep"][:, gs] * R
            dx_scr[:, gs] = dx_scr[:, gs] + dxdt_state
            dBs.append(dB)
            dCs.append(dC)
        dxdt = dx_scr[...]
        tail = jnp.concatenate([jnp.sum(dyv * X, axis=0, keepdims=True), jnp.concatenate(rh_sums, axis=1),
                                jnp.zeros((6, D), F32)], axis=0)
        red = _dot01(jnp.concatenate([dyv * yoff_full, jnp.concatenate(ddec, axis=1), dxdt * X, tail], axis=0), rdm, 2)
        d_dskip, dcd_row = red[3 * L:3 * L + 1], red[3 * L + 1:3 * L + 2]
        ddec_h = red[L:2 * L] * o["dec"]
        dcum_last = jnp.sum(ddec_h, axis=0, keepdims=True) + dcd_row * o["cd"]
        dcum = red[0:L] + rs - cs_t.T - ddec_h + (sub == L - 1).astype(F32) * dcum_last
        dda = _dot01(o["anti"], dcum, 3, split_lhs=False)
        ddt_v = dda * o["a_row"] + red[2 * L:3 * L]
        dX = dX + dxdt * o["dt_rep"]
        head_mask = (lane < N_HEADS).astype(F32)
        d_alog = jnp.sum(dda * o["dt"], axis=0, keepdims=True) * o["a_row"] * head_mask
        dpre = ddt_v * _sigmoid(o["pre"]) * head_mask
        dpm_ref[:, OFF_DT:] = dpre.astype(BF16)
        d_dtb = jnp.sum(dpre, axis=0, keepdims=True)
        dhp_ref[...] += jnp.concatenate([d_dtb, d_alog, d_dskip * head_mask, jnp.zeros((5, LANES), F32)], axis=0)

        dxbc = jnp.concatenate([dX] + dBs + dCs, axis=1)
        sg, cv = o["sg"], o["cv"]
        dcv = dxbc * (sg * (1.0 + cv * (1.0 - sg)))
        dconv_ref[0:5] += jnp.concatenate(
            [jnp.sum(dcv * o["taps"][k], axis=0, keepdims=True) for k in range(4)]
            + [jnp.sum(dcv, axis=0, keepdims=True)], axis=0)
        e2 = jnp.concatenate([dcv, nxt_cv[...]], axis=0)
        n2 = L + CONV_HALO
        dux = (dcv * cw[3:4] + pltpu.roll(e2, n2 - 1, 0)[:L] * cw[2:3] + pltpu.roll(e2, n2 - 2, 0)[:L] * cw[1:2]
               + pltpu.roll(e2, n2 - 3, 0)[:L] * cw[0:1])
        dpm_ref[:, OFF_XBC:OFF_DT] = dux.astype(BF16)
        nxt_cv[...] = dcv[:CONV_HALO]

        @pl.when((b == nb - 1) & (ci == nc - 1))
        def _():
            exchange.finish()

    def full(shape):
        return pl.BlockSpec(shape, lambda b, c: (0,) * len(shape))

    def rowblk(b, c):
        return b * nc + (nc - 1 - c)

    hp_blocks = CHUNK // POOL_HALO
    hx_blocks = CHUNK // CONV_HALO
    T = nb * seq
    return pl.pallas_call(
        body, name="mixer_bwd", grid=(nb, nc),
        in_specs=[pl.BlockSpec((CHUNK, PROJ_W), lambda b, c: (rowblk(b, c), 0)),
                  pl.BlockSpec((POOL_HALO, POOL_W), lambda b, c: (jnp.maximum(rowblk(b, c) * hp_blocks - 1, 0), 0)),
                  pl.BlockSpec((CONV_HALO, CONV_CH), lambda b, c: (jnp.maximum(rowblk(b, c) * hx_blocks - 1, 0), 1)),
                  pl.BlockSpec((CHUNK, MIX_W), lambda b, c: (rowblk(b, c), 0)),
                  pl.BlockSpec((1, 1, N_GROUPS, N_STATE, GROUP_W), lambda b, c: (b, nc - 1 - c, 0, 0, 0)),
                  full((4, CONV_CH)), full((1, CONV_CH)), full((8, LANES)), full((1, D)),
                  full((4, LANES, LANES)), full((1, POOL_W))] + [ANY_SPEC] * ns,
        out_specs=[pl.BlockSpec((CHUNK, PROJ_W), lambda b, c: (rowblk(b, c), 0)),
                   full((8, CONV_CH)), full((8, LANES)), full((8, D)), full((4, LANES, LANES))] + [ANY_SPEC] * ns,
        out_shape=[jax.ShapeDtypeStruct((T, PROJ_W), BF16),
                   jax.ShapeDtypeStruct((8, CONV_CH), F32), jax.ShapeDtypeStruct((8, LANES), F32),
                   jax.ShapeDtypeStruct((8, D), F32), jax.ShapeDtypeStruct((4, LANES, LANES), F32)]
        + [jax.ShapeDtypeStruct(v.shape, v.dtype) for v in blocks],
        scratch_shapes=[pltpu.VMEM((POOL_HALO, POOL_W), F32), pltpu.VMEM((CONV_HALO, CONV_CH), F32),
                        pltpu.VMEM((N_GROUPS, N_STATE, GROUP_W), F32), pltpu.VMEM((CHUNK, D), F32),
                        pltpu.VMEM((CHUNK, D), F32)] + _comm_scratch(ns),
        compiler_params=_cparams(),
    )(pm, pm, pm, dym, hstates, cw, cb, hp, gssd, wpool, pscale, *blocks)


def _mlp_fused(x2, ymix, target, mod, g_mlp, g_final, w_out, w_up, w_down, seq):
    T = x2.shape[0]
    tm = min(256, seq)
    tps = seq // tm
    nblk = D_FF // FF_BLK

    def body(x_ref, ym_ref, tg_ref, mod_ref, gm_ref, gf_ref, wo_ref, wu_ref, wd_ref,
             da_ref, dym_ref, dh1_ref, u2_ref, f_ref, dup_ref, ddn_ref, dmod_ref, acc_ref, relu_scr):
        i = pl.program_id(0)

        @pl.when(i == 0)
        def _():
            acc_ref[...] = jnp.zeros_like(acc_ref)

        @pl.when(i % tps == 0)
        def _():
            dmod_ref[...] = jnp.zeros_like(dmod_ref)

        md = mod_ref[0]
        gate_m, shift_f, scale_f, gate_f = md[2:3], md[3:4], md[4:5], md[5:6]
        g_mlp, g_fin = gm_ref[...], gf_ref[...]
        a = jnp.dot(ym_ref[...], wo_ref[...], preferred_element_type=F32)
        h1 = x_ref[...] + gate_m * a
        r2 = lax.rsqrt(jnp.mean(h1 * h1, axis=-1, keepdims=True) + EPS)
        n2 = h1 * r2
        u2 = (n2 * g_mlp) * (1.0 + scale_f) + shift_f
        u2b = u2.astype(BF16)
        u2_ref[...] = u2b
        dn = jnp.zeros((tm, D), F32)
        for j in range(nblk):
            js = slice(j * FF_BLK, (j + 1) * FF_BLK)
            upj = jnp.maximum(jnp.dot(u2b, wu_ref[j], preferred_element_type=F32), 0.0)
            relu_scr[:, js] = upj
            fj = (upj * upj).astype(BF16)
            f_ref[:, js] = fj
            dn = dn + jnp.dot(fj, wd_ref[j], preferred_element_type=F32)
        h2 = h1 + gate_f * dn
        r3 = lax.rsqrt(jnp.mean(h2 * h2, axis=-1, keepdims=True) + EPS)
        n3 = h2 * r3
        err = n3 * g_fin - tg_ref[...]
        loss = 0.5 * jnp.sum(jnp.mean(err * err, axis=-1, keepdims=True), axis=0, keepdims=True)
        dout = err * (1.0 / D)
        d_gfin = jnp.sum(dout * n3, axis=0, keepdims=True)
        dn3 = dout * g_fin
        dh2 = r3 * (dn3 - n3 * jnp.mean(dn3 * n3, axis=-1, keepdims=True))
        d_gate_f = jnp.sum(dh2 * dn, axis=0, keepdims=True)
        ddn = (gate_f * dh2).astype(BF16)
        ddn_ref[...] = ddn
        du2 = jnp.zeros((tm, D), F32)
        for j in range(nblk):
            js = slice(j * FF_BLK, (j + 1) * FF_BLK)
            dfj = lax.dot_general(ddn, wd_ref[j], (((1,), (1,)), ((), ())), preferred_element_type=F32)
            dupj = (dfj * (2.0 * relu_scr[:, js])).astype(BF16)
            dup_ref[:, js] = dupj
            du2 = du2 + lax.dot_general(dupj, wu_ref[j], (((1,), (1,)), ((), ())), preferred_element_type=F32)
        d_scale_f = jnp.sum(du2 * (n2 * g_mlp), axis=0, keepdims=True)
        d_shift_f = jnp.sum(du2, axis=0, keepdims=True)
        d_gmlp = jnp.sum(du2 * (1.0 + scale_f) * n2, axis=0, keepdims=True)
        dn2 = du2 * (g_mlp * (1.0 + scale_f))
        dh1 = dh2 + r2 * (dn2 - n2 * jnp.mean(dn2 * n2, axis=-1, keepdims=True))
        dh1_ref[...] = dh1
        d_gate_m = jnp.sum(dh1 * a, axis=0, keepdims=True)
        da = (gate_m * dh1).astype(BF16)
        da_ref[...] = da
        dym_ref[...] = lax.dot_general(da, wo_ref[...], (((1,), (1,)), ((), ())),
                                       preferred_element_type=F32).astype(BF16)
        dmod_ref[0] += jnp.concatenate([jnp.zeros((2, D), F32), d_gate_m, d_shift_f, d_scale_f, d_gate_f,
                                        jnp.zeros((2, D), F32)], axis=0)
        acc_ref[...] += jnp.concatenate([d_gmlp, d_gfin, loss * jnp.ones((1, D), F32), jnp.zeros((5, D), F32)], axis=0)

    whole = pl.BlockSpec(memory_space=pltpu.VMEM)

    def tok(w):
        return pl.BlockSpec((tm, w), lambda i: (i, 0))

    def vec():
        return pl.BlockSpec((1, D), lambda i: (0, 0))

    nb = T // seq
    return pl.pallas_call(
        body, name="mlp_fused", grid=(T // tm,),
        in_specs=[tok(D), tok(MIX_W), tok(D), pl.BlockSpec((1, 8, D), lambda i: (i // tps, 0, 0)), vec(), vec(),
                  whole, whole, whole],
        out_specs=[tok(D), tok(MIX_W), tok(D), tok(D), tok(D_FF), tok(D_FF), tok(D),
                   pl.BlockSpec((1, 8, D), lambda i: (i // tps, 0, 0)), pl.BlockSpec((8, D), lambda i: (0, 0))],
        out_shape=[jax.ShapeDtypeStruct((T, D), BF16), jax.ShapeDtypeStruct((T, MIX_W), BF16),
                   jax.ShapeDtypeStruct((T, D), F32), jax.ShapeDtypeStruct((T, D), BF16),
                   jax.ShapeDtypeStruct((T, D_FF), BF16), jax.ShapeDtypeStruct((T, D_FF), BF16),
                   jax.ShapeDtypeStruct((T, D), BF16), jax.ShapeDtypeStruct((nb, 8, D), F32),
                   jax.ShapeDtypeStruct((8, D), F32)],
        scratch_shapes=[pltpu.VMEM((tm, D_FF), F32)],
        compiler_params=_cparams(),
    )(x2, ymix, target, mod, g_mlp, g_final, w_out, w_up, w_down)


def _in_bwd(x2, dh1, dpb, mod, g_mix, w_cat, dmod_a, acc_a, seq):
    T = x2.shape[0]
    tm = min(512, seq)
    tps = seq // tm
    steps = T // tm

    def body(x_ref, dh_ref, dpb_ref, mod_ref, g_ref, w_ref, dma_ref, acca_ref, dx_ref, dmod_ref, acc_ref):
        i = pl.program_id(0)

        @pl.when(i == 0)
        def _():
            acc_ref[...] = acca_ref[...]

        @pl.when(i % tps == 0)
        def _():
            dmod_ref[...] = dma_ref[...]

        du = lax.dot_general(dpb_ref[...], w_ref[...], (((1,), (1,)), ((), ())), preferred_element_type=F32)
        x = x_ref[...]
        md = mod_ref[0]
        g = g_ref[...]
        r = lax.rsqrt(jnp.mean(x * x, axis=-1, keepdims=True) + EPS)
        n1 = x * r
        d_scale = jnp.sum(du * (n1 * g), axis=0, keepdims=True)
        d_shift = jnp.sum(du, axis=0, keepdims=True)
        d_g = jnp.sum(du * (1.0 + md[1:2]) * n1, axis=0, keepdims=True)
        dn1 = du * (g * (1.0 + md[1:2]))
        dx_ref[...] = dh_ref[...] + r * (dn1 - n1 * jnp.mean(dn1 * n1, axis=-1, keepdims=True))
        dmod_ref[0] += jnp.concatenate([d_shift, d_scale, jnp.zeros((6, D), F32)], axis=0)
        acc_ref[...] += jnp.concatenate([jnp.zeros((3, D), F32), d_g, jnp.zeros((4, D), F32)], axis=0)

    whole = pl.BlockSpec(memory_space=pltpu.VMEM)
    nb = T // seq
    return pl.pallas_call(
        body, name="in_bwd", grid=(steps,),
        in_specs=[pl.BlockSpec((tm, D), lambda i: (i, 0)), pl.BlockSpec((tm, D), lambda i: (i, 0)),
                  pl.BlockSpec((tm, PROJ_W), lambda i: (i, 0)),
                  pl.BlockSpec((1, 8, D), lambda i: (i // tps, 0, 0)), pl.BlockSpec((1, D), lambda i: (0, 0)),
                  whole, pl.BlockSpec((1, 8, D), lambda i: (i // tps, 0, 0)), pl.BlockSpec((8, D), lambda i: (0, 0))],
        out_specs=[pl.BlockSpec((tm, D), lambda i: (i, 0)),
                   pl.BlockSpec((1, 8, D), lambda i: (i // tps, 0, 0)), pl.BlockSpec((8, D), lambda i: (0, 0))],
        out_shape=[jax.ShapeDtypeStruct((T, D), F32),
                   jax.ShapeDtypeStruct((nb, 8, D), F32), jax.ShapeDtypeStruct((8, D), F32)],
        compiler_params=_cparams(),
    )(x2, dh1, dpb, mod, g_mix, w_cat, dmod_a, acc_a)


def _dw_in(u_b, dpb, in_cols):
    T = u_b.shape[0]
    bk = min(512, T)
    nk = T // bk
    starts = [(in_cols * j // LANES) * LANES for j in range(N_DEV)]
    assert all(s + DW_IN_WIN <= PROJ_W and in_cols * (j + 1) <= s + DW_IN_WIN for j, s in enumerate(starts))

    def body(u_ref, d_ref, o_ref, acc):
        k = pl.program_id(0)

        @pl.when(k == 0)
        def _():
            acc[...] = jnp.zeros_like(acc)

        ut = u_ref[...].T
        for j in range(N_DEV):
            acc[j] += jnp.dot(ut, d_ref[:, starts[j]:starts[j] + DW_IN_WIN], preferred_element_type=F32)

        @pl.when(k == nk - 1)
        def _():
            for j in range(N_DEV):
                off = in_cols * j - starts[j]
                o_ref[j] = acc[j][:, off:off + in_cols].astype(BF16)

    return pl.pallas_call(
        body, name="dw_in", grid=(nk,),
        in_specs=[pl.BlockSpec((bk, D), lambda k: (k, 0)), pl.BlockSpec((bk, PROJ_W), lambda k: (k, 0))],
        out_specs=pl.BlockSpec((N_DEV, D, in_cols), lambda k: (0, 0, 0)),
        out_shape=jax.ShapeDtypeStruct((N_DEV, D, in_cols), BF16),
        scratch_shapes=[pltpu.VMEM((N_DEV, D, DW_IN_WIN), F32)],
        compiler_params=_cparams(),
    )(u_b, dpb)


def _dw_blocks(a, b, name, by_rows, per_step=1):
    T, M = a.shape
    N = b.shape[1]
    bk = min(1024, T)
    nk = T // bk
    whole = pl.BlockSpec(memory_space=pltpu.VMEM)
    if by_rows:
        rows = M // N_DEV
        am = rows * per_step
        nblk = N_DEV // per_step
        a_spec, b_spec = pl.BlockSpec((bk, am), lambda i, k: (k, i)), whole
        out_blk, acc_shape = (per_step, rows, N), (am, N)
    else:
        cols = N // N_DEV
        nblk = N_DEV
        a_spec, b_spec = whole, pl.BlockSpec((bk, cols), lambda i, k: (k, i))
        out_blk, acc_shape = (1, M, cols), (M, cols)

    def body(a_ref, b_ref, o_ref, acc):
        k = pl.program_id(1)

        @pl.when(k == 0)
        def _():
            acc[...] = jnp.zeros_like(acc)

        tok = pl.ds(pl.multiple_of(k * bk, bk), bk)
        a_blk = a_ref[...] if by_rows else a_ref[tok, :]
        b_blk = b_ref[tok, :] if by_rows else b_ref[...]
        acc[...] += lax.dot_general(a_blk, b_blk, (((0,), (0,)), ((), ())), preferred_element_type=F32)

        @pl.when(k == nk - 1)
        def _():
            o_ref[...] = acc[...].reshape(out_blk).astype(BF16)

    return pl.pallas_call(
        body, name=name, grid=(nblk, nk), in_specs=[a_spec, b_spec],
        out_specs=pl.BlockSpec(out_blk, lambda i, k: (i, 0, 0)),
        out_shape=jax.ShapeDtypeStruct((N_DEV,) + out_blk[1:], BF16),
        scratch_shapes=[pltpu.VMEM(acc_shape, F32)],
        compiler_params=_cparams(),
    )(a, b)


def _adam_parts(parts, w, m, v, name):
    rows, cols = w.shape
    br = rows
    for cand in range(rows, 15, -16):
        if rows % cand == 0 and cand * cols * 4 <= ADAM_BLOCK_BYTES:
            br = cand
            break

    def body(p_ref, w_ref, m_ref, v_ref, g_out, dl_out, m_out, v_out):
        g = p_ref[0].astype(F32)
        for k in range(1, N_DEV):
            g = g + p_ref[k].astype(F32)
        g_out[...] = g
        dl, mn, vn = _adam_math(w_ref[...], g, m_ref[...], v_ref[...])
        dl_out[...] = dl
        m_out[...] = mn
        v_out[...] = vn

    wspec = pl.BlockSpec((br, cols), lambda i: (i, 0))
    return pl.pallas_call(
        body, name=name, grid=(rows // br,),
        in_specs=[pl.BlockSpec((N_DEV, br, cols), lambda i: (0, i, 0)), wspec, wspec, wspec],
        out_specs=[wspec] * 4, out_shape=[jax.ShapeDtypeStruct((rows, cols), F32)] * 4,
        compiler_params=_cparams(),
    )(parts, w, m, v)


def _adam_plain(g, w, m, v, name):
    def body(g_ref, w_ref, m_ref, v_ref, dl_out, m_out, v_out):
        dl, mn, vn = _adam_math(w_ref[...], g_ref[...], m_ref[...], v_ref[...])
        dl_out[...] = dl
        m_out[...] = mn
        v_out[...] = vn

    return pl.pallas_call(body, name=name, out_shape=[jax.ShapeDtypeStruct(w.shape, F32)] * 3,
                          compiler_params=_cparams())(g, w, m, v)


SMALL_PARAMS = ("b_ada", "g_mix", "conv_b", "dt_bias", "a_log", "d_skip", "g_ssd", "pool_scale", "g_mlp", "g_final")


def _small_adam(gathered, params):
    n_par = len(SMALL_PARAMS)
    nb = gathered[0].shape[1]

    def body(*refs):
        dmod_ref, acc_ref, conv_ref, vec_ref, hd_ref = refs[:5]
        par_refs = refs[5:5 + 3 * n_par]
        out_refs = refs[5 + 3 * n_par:5 + 7 * n_par]
        cw_out, acc_out = refs[5 + 7 * n_par:]

        def total(ref):
            t = ref[0]
            for k in range(1, N_DEV):
                t = t + ref[k]
            return t

        dm = total(dmod_ref)
        dmb = dm[0]
        for b in range(1, nb):
            dmb = dmb + dm[b]
        ac, cv, vc, hd = total(acc_ref), total(conv_ref), total(vec_ref), total(hd_ref)
        cw_out[...] = cv[0:4]
        acc_out[...] = ac
        grads = {
            "b_ada": jnp.concatenate([dmb[r:r + 1] for r in range(6)], axis=1), "g_mix": ac[3:4], "conv_b": cv[4:5],
            "dt_bias": hd[0:1, 0:N_HEADS], "a_log": hd[1:2, 0:N_HEADS], "d_skip": hd[2:3, 0:N_HEADS],
            "g_ssd": vc[0:1], "pool_scale": vc[1:2, 0:POOL_W], "g_mlp": ac[0:1], "g_final": ac[1:2],
        }
        for i, name in enumerate(SMALL_PARAMS):
            w_ref, m_ref, v_ref = par_refs[3 * i:3 * i + 3]
            g = grads[name]
            dl, mn, vn = _adam_math(w_ref[...], g, m_ref[...], v_ref[...])
            g_o, d_o, m_o, v_o = out_refs[4 * i:4 * i + 4]
            g_o[...] = g
            d_o[...] = dl
            m_o[...] = mn
            v_o[...] = vn

    flat = [a for name in SMALL_PARAMS for a in params[name]]
    out_shape = [jax.ShapeDtypeStruct(params[name][0].shape, F32) for name in SMALL_PARAMS for _ in range(4)]
    out_shape += [jax.ShapeDtypeStruct((4, CONV_CH), F32), jax.ShapeDtypeStruct((8, D), F32)]
    return pl.pallas_call(body, name="small_adam", out_shape=out_shape, compiler_params=_cparams())(*gathered, *flat)


def kernel(x, c, w_ada, b_ada, g_mix, w_in, conv_w, conv_b, dt_bias, a_log, d_skip, g_ssd, w_pool, pool_scale, w_out, g_mlp, w_up, w_down, g_final, loss_target, m_w_ada, m_b_ada, m_g_mix, m_w_in, m_conv_w, m_conv_b, m_dt_bias, m_a_log, m_d_skip, m_g_ssd, m_w_pool, m_pool_scale, m_w_out, m_g_mlp, m_w_up, m_w_down, m_g_final, v_w_ada, v_b_ada, v_g_mix, v_w_in, v_conv_w, v_conv_b, v_dt_bias, v_a_log, v_d_skip, v_g_ssd, v_w_pool, v_pool_scale, v_w_out, v_g_mlp, v_w_up, v_w_down, v_g_final):
    nb, seq, _ = x.shape
    T = nb * seq
    me = 4 * lax.axis_index("x") + 2 * lax.axis_index("y") + lax.axis_index("c")
    in_cols = w_in.shape[2]
    ada_cols = w_ada.shape[2]
    cw_cols = conv_w.shape[2]

    shards = [w_out[0].astype(BF16), w_up[0].astype(BF16), w_down[0].astype(BF16)]
    ga = _gather_start(shards, "wg_start")
    ga_send, ga_recv, ga_thru, ga_token = ga[0], ga[1], ga[2:-1], ga[-1]
    c_g, cw_g, win_g = _all_gather([c, conv_w[0], w_in[0].astype(BF16)], "ag_first", after=[ga_token])
    c_all = c_g.reshape(N_DEV * nb, D)
    cw_full = cw_g.transpose(1, 0, 2).reshape(4, CONV_CH)

    b_slice = lax.dynamic_slice(b_ada, (0, me * ada_cols), (1, ada_cols))
    mod_cols = _ada_fwd(c_all, w_ada[0], b_slice)
    (mod_g,) = _all_gather([mod_cols], "ag_mod")
    mod_all = mod_g.transpose(1, 0, 2).reshape(N_DEV * nb, 6, D)
    mod_mine = lax.dynamic_slice(mod_all, (me * nb, 0, 0), (nb, 6, D))
    mod = jnp.pad(mod_mine, ((0, 0), (0, 2), (0, 0)))

    x2 = x.reshape(T, D)
    tg2 = loss_target.reshape(T, D)
    heads = jnp.pad(jnp.concatenate([dt_bias, a_log, d_skip], axis=0), ((0, 5), (0, LANES - N_HEADS)))
    wpool_b = w_pool[0]
    u_b, pm, w_cat = _mix_in(x2, mod, g_mix, win_g, seq)
    ymix, hstates = _mixer_fwd(pm, cw_full, conv_b, heads, g_ssd, wpool_b, pool_scale, nb, seq)
    ga_done = _gather_wait(ga_send, ga_recv, ga_thru, ymix, "wg_wait")
    wout_g, wup_g, wdn_g = _gather_forward(ga_done[:3], ga_done[3:], "wg_forward")
    da_b, dym, dh1, u2_b, f_b, dup_b, ddn_b, dmod_a, acc_a = _mlp_fused(
        x2, ymix, tg2, mod, g_mlp, g_final.reshape(1, D), wout_g.reshape(MIX_W, D), wup_g, wdn_g, seq)

    gout_p = _dw_blocks(ymix, da_b, "dw_out", True, per_step=4)
    gup_p = _dw_blocks(u2_b, dup_b, "dw_up", False)
    gdn_p = _dw_blocks(f_b, ddn_b, "dw_down", True)
    dpb, d_conv, d_heads, d_vec, d_wpool, gout_r, gup_r, gdn_r = _mixer_bwd(
        pm, dym, hstates, cw_full, conv_b, heads, g_ssd, wpool_b, pool_scale, nb, seq, [gout_p, gup_p, gdn_p])
    gin_p = _dw_in(u_b, dpb, in_cols)
    ex_send, ex_recv, gin_thru, gin_land, ex_token = _exchange_start(gin_p, "gin_start")
    grad_x2, dmod, acc = _in_bwd(x2, dh1, dpb, mod, g_mix + ex_token[0:1, 0:1], w_cat, dmod_a, acc_a, seq)

    g_out, d_out, nm_out, nv_out = _adam_parts(gout_r, w_out[0], m_w_out[0], v_w_out[0], "adam_w_out")
    g_up, d_up, nm_up, nv_up = _adam_parts(gup_r, w_up[0], m_w_up[0], v_w_up[0], "adam_w_up")
    g_dn, d_dn, nm_dn, nv_dn = _adam_parts(gdn_r, w_down[0], m_w_down[0], v_w_down[0], "adam_w_down")

    dmod_g, acc_g, conv_g, vec_g, heads_g, wpool_parts = _all_gather(
        [dmod, acc, d_conv, d_vec, d_heads, d_wpool.reshape(4 * LANES, LANES)], "ag_small_bwd",
        after=[nm_out, nm_up, nm_dn])
    pool2 = (4 * LANES, LANES)
    wpool_outs = _adam_parts(wpool_parts, w_pool.reshape(pool2), m_w_pool.reshape(pool2), v_w_pool.reshape(pool2),
                             "adam_w_pool")
    small_params = {
        "b_ada": (b_ada, m_b_ada, v_b_ada), "g_mix": (g_mix, m_g_mix, v_g_mix), "conv_b": (conv_b, m_conv_b, v_conv_b),
        "dt_bias": (dt_bias, m_dt_bias, v_dt_bias), "a_log": (a_log, m_a_log, v_a_log),
        "d_skip": (d_skip, m_d_skip, v_d_skip), "g_ssd": (g_ssd, m_g_ssd, v_g_ssd),
        "pool_scale": (pool_scale, m_pool_scale, v_pool_scale), "g_mlp": (g_mlp, m_g_mlp, v_g_mlp),
        "g_final": tuple(a.reshape(1, D) for a in (g_final, m_g_final, v_g_final)),
    }
    small_res = _small_adam([dmod_g, acc_g, conv_g, vec_g, heads_g], small_params)
    g_cw_full, acc_sum = small_res[-2:]
    loss = acc_sum[2, 0]

    g_cw = lax.dynamic_slice(g_cw_full, (0, me * cw_cols), (4, cw_cols))
    d_cwp, nm_cwp, nv_cwp = _adam_plain(g_cw, conv_w[0], m_conv_w[0], v_conv_w[0], "adam_conv_w")

    dmod_all = dmod_g[:, :, 0:6].reshape(N_DEV * nb, 6 * D)
    dmod_slice = lax.dynamic_slice(dmod_all, (0, me * ada_cols), (N_DEV * nb, ada_cols))
    g_ada, d_ada, nm_ada, nv_ada = _ada_bwd_adam(c_all, dmod_slice, w_ada[0], m_w_ada[0], v_w_ada[0])

    ex_after = nm_ada[0:8, 0:LANES] + acc_sum[:, 0:LANES]
    gin_own, gin_landed = _exchange_wait(ex_send, ex_recv, gin_thru, gin_land, ex_after, "gin_wait")
    gin_r = lax.dynamic_update_slice(gin_landed, lax.dynamic_slice(gin_own, (me, 0, 0), (1, D, in_cols)), (me, 0, 0))
    g_in, d_in, nm_in, nv_in = _adam_parts(gin_r, w_in[0], m_w_in[0], v_w_in[0], "adam_w_in")

    def small_outs(kind, wpool):
        res = {name: small_res[4 * i + kind] for i, name in enumerate(SMALL_PARAMS)}
        res["g_final"] = res["g_final"].reshape(D)
        res["w_pool"] = wpool.reshape(1, 4, LANES, LANES)
        return res

    def big_outs(ada, win, cwp, wout, wup, wdn):
        return {"w_ada": ada[None], "w_in": win.reshape(1, D, in_cols), "conv_w": cwp[None], "w_out": wout[None],
                "w_up": wup[None], "w_down": wdn[None]}

    order = ["w_ada", "b_ada", "g_mix", "w_in", "conv_w", "conv_b", "dt_bias", "a_log", "d_skip", "g_ssd", "w_pool",
             "pool_scale", "w_out", "g_mlp", "w_up", "w_down", "g_final"]
    groups = [
        {**small_outs(0, wpool_outs[0]), **big_outs(g_ada, g_in, g_cw, g_out, g_up, g_dn)},
        {**small_outs(1, wpool_outs[1]), **big_outs(d_ada, d_in, d_cwp, d_out, d_up, d_dn)},
        {**small_outs(2, wpool_outs[2]), **big_outs(nm_ada, nm_in, nm_cwp, nm_out, nm_up, nm_dn)},
        {**small_outs(3, wpool_outs[3]), **big_outs(nv_ada, nv_in, nv_cwp, nv_out, nv_up, nv_dn)},
    ]
    outs = [loss, grad_x2.reshape(nb, seq, D)]
    for grp in groups:
        outs += [grp[n] for n in order]
    return tuple(outs)
```

```python
import functools

import jax
import jax.numpy as jnp
from jax import lax
from jax.experimental import pallas as pl
from jax.experimental.pallas import tpu as pltpu

F32, BF16 = jnp.float32, jnp.bfloat16
MESH = pl.DeviceIdType.MESH
N_DEV = 8
D = 1024
LANES = 128
CHUNK = 128
POOL_W = 512
WINDOWS = (2, 4, 8, 16)
N_HEADS = 16
HEAD_DIM = 64
N_GROUPS = 2
GROUP_W = 512
N_STATE = 128
CONV_CH = 1536
OFF_Z, OFF_XBC, OFF_DT, IN_W = 512, 1536, 3072, 3088
PROJ_W = OFF_DT + LANES
MIX_W = 1536
D_FF = 4096
FF_BLK = 512
EPS = 1e-5
LR, B1, B2, AEPS, WD, STEP = 0.001, 0.9, 0.999, 1e-08, 0.01, 10
POOL_HALO = 16
CONV_HALO = 8
VMEM_LIMIT = 56 << 20
ADAM_BLOCK_BYTES = 1 << 20
DW_IN_WIN = 512


def _cparams(**kw):
    return pltpu.CompilerParams(vmem_limit_bytes=VMEM_LIMIT, **kw)


def _mm(a, b):
    return jnp.dot(a.astype(BF16), b.astype(BF16), preferred_element_type=F32)


def _mm_nt(a, b):
    return lax.dot_general(a.astype(BF16), b.astype(BF16), (((1,), (1,)), ((), ())), preferred_element_type=F32)


def _mm_tn(a, b):
    return lax.dot_general(a.astype(BF16), b.astype(BF16), (((0,), (0,)), ((), ())), preferred_element_type=F32)


def _split_bf16(v, terms):
    parts, rest = [], v
    for t in range(terms):
        p = rest.astype(BF16)
        parts.append(p)
        if t + 1 < terms:
            rest = rest - p.astype(F32)
    return parts


def _dot01(a, b, terms, split_lhs=True):
    if split_lhs:
        bb = b.astype(BF16)
        prods = [jnp.dot(p, bb, preferred_element_type=F32) for p in _split_bf16(a, terms)]
    else:
        ab = a.astype(BF16)
        prods = [jnp.dot(ab, p, preferred_element_type=F32) for p in _split_bf16(b, terms)]
    out = prods[0]
    for q in prods[1:]:
        out = out + q
    return out


def _sigmoid(v):
    return 1.0 / (1.0 + jnp.exp(-v))


def _expand_mat():
    r = lax.broadcasted_iota(jnp.int32, (LANES, D), 0)
    c = lax.broadcasted_iota(jnp.int32, (LANES, D), 1)
    return (r == c // HEAD_DIM).astype(F32)


def _reduce_mat():
    r = lax.broadcasted_iota(jnp.int32, (D, LANES), 0)
    c = lax.broadcasted_iota(jnp.int32, (D, LANES), 1)
    return (c == r // HEAD_DIM).astype(F32)


def _pos():
    return lax.axis_index("x"), lax.axis_index("y"), lax.axis_index("c")


class _Gather:
    def __init__(self, x_refs, o_refs, send, recv, loc):
        self.x_refs, self.o_refs, self.send, self.recv, self.loc = x_refs, o_refs, send, recv, loc
        self.n = len(x_refs)
        x, y, c = _pos()
        self.c = c
        self.me, self.sib = (x, y, c), (x, y, 1 - c)
        self.chips = [(1 - x, y), (x, 1 - y), (1 - x, 1 - y)]

    def _cp(self, a, k, block, to, src=None):
        dst = self.o_refs[a].at[4 * block[0] + 2 * block[1] + block[2]]
        return pltpu.make_async_remote_copy(
            src_ref=dst if src is None else src, dst_ref=dst,
            send_sem=self.send.at[a * 7 + k], recv_sem=self.recv.at[a * 7 + k],
            device_id=to, device_id_type=MESH)

    def _mine(self, a):
        me = self.me
        return pltpu.make_async_copy(self.x_refs[a], self.o_refs[a].at[4 * me[0] + 2 * me[1] + me[2]], self.loc.at[a])

    def _first(self, a):
        cps = [self._cp(a, 0, self.me, self.sib, src=self.x_refs[a])]
        return cps + [self._cp(a, 1 + j, self.me, (*chip, self.c), src=self.x_refs[a])
                      for j, chip in enumerate(self.chips)]

    def _passed(self, a, j):
        return self._cp(a, 4 + j, (*self.chips[j], self.c), self.sib)

    def start(self):
        for a in range(self.n):
            self._mine(a).start()
            for cp in self._first(a):
                cp.start()

    def forward(self):
        for j, chip in enumerate(self.chips):
            for a in range(self.n):
                self._cp(a, 1 + j, (*chip, self.c), self.me).wait_recv()
                self._passed(a, j).start()

    def finish(self):
        for a in range(self.n):
            self._cp(a, 0, self.sib, self.me).wait_recv()
            for j, chip in enumerate(self.chips):
                self._cp(a, 4 + j, (*chip, 1 - self.c), self.me).wait_recv()
        for a in range(self.n):
            for cp in self._first(a):
                cp.wait_send()
            for j in range(3):
                self._passed(a, j).wait_send()
            self._mine(a).wait()


class _Exchange:
    def __init__(self, x_refs, o_refs, send, recv, loc):
        self.x_refs, self.o_refs, self.send, self.recv, self.loc = x_refs, o_refs, send, recv, loc
        self.n = len(x_refs)
        x, y, c = _pos()
        self.me_i = 4 * x + 2 * y + c
        self.peers = []
        for k in range(1, N_DEV):
            px = 1 - x if (k >> 2) & 1 else x
            py = 1 - y if (k >> 1) & 1 else y
            pc = 1 - c if k & 1 else c
            self.peers.append(((px, py, pc), 4 * px + 2 * py + pc))

    def _mine(self, a):
        return pltpu.make_async_copy(self.x_refs[a].at[self.me_i], self.o_refs[a].at[self.me_i], self.loc.at[a])

    def _cp(self, a, k, landing):
        peer, peer_i = self.peers[k]
        return pltpu.make_async_remote_copy(
            src_ref=self.x_refs[a].at[peer_i], dst_ref=self.o_refs[a].at[landing],
            send_sem=self.send.at[a * 7 + k], recv_sem=self.recv.at[a * 7 + k],
            device_id=peer, device_id_type=MESH)

    def start(self):
        for a in range(self.n):
            self._mine(a).start()
            for k in range(N_DEV - 1):
                self._cp(a, k, self.me_i).start()

    def finish(self):
        for a in range(self.n):
            for k in range(N_DEV - 1):
                self._cp(a, k, self.peers[k][1]).wait_recv()
        for a in range(self.n):
            for k in range(N_DEV - 1):
                self._cp(a, k, self.me_i).wait_send()
            self._mine(a).wait()


def _comm_scratch(n):
    return [pltpu.SemaphoreType.DMA((7 * n,)), pltpu.SemaphoreType.DMA((7 * n,)), pltpu.SemaphoreType.DMA((n,))]


ANY_SPEC = pl.BlockSpec(memory_space=pl.ANY)


def _all_gather(xs, name, after=()):
    n, na = len(xs), len(after)

    def body(*refs):
        g = _Gather(refs[:n], refs[n + na:2 * n + na], *refs[2 * n + na:])
        g.start()
        g.forward()
        g.finish()

    return pl.pallas_call(
        body, name=name,
        out_shape=[jax.ShapeDtypeStruct((N_DEV,) + v.shape, v.dtype) for v in xs],
        in_specs=[ANY_SPEC] * (n + na), out_specs=[ANY_SPEC] * n, scratch_shapes=_comm_scratch(n),
    )(*xs, *after)


HBM_SPEC = pl.BlockSpec(memory_space=pltpu.HBM)
SEM_SPEC = pl.BlockSpec(memory_space=pltpu.SEMAPHORE)
VMEM_SPEC = pl.BlockSpec(memory_space=pltpu.VMEM)
SPLIT_EFFECT = pltpu.SideEffectType.DATAFLOW_SIDE_EFFECTING


def _in_hbm(v):
    return pltpu.with_memory_space_constraint(v, pltpu.HBM)


def _exchange_start(blocks, name):
    def body(x_ref, land_ref, send, recv, x_thru, land_thru, token):
        ex = _Exchange([x_ref], [land_ref], send, recv, None)
        for k in range(N_DEV - 1):
            ex._cp(0, k, ex.me_i).start()
        token[...] = jnp.zeros_like(token)

    hbm = pltpu.HBM(blocks.shape, blocks.dtype)
    return pl.pallas_call(
        body, name=name,
        out_shape=(pltpu.SemaphoreType.DMA((N_DEV - 1,)), pltpu.SemaphoreType.DMA((N_DEV - 1,)), hbm, hbm,
                   jax.ShapeDtypeStruct((8, LANES), F32)),
        in_specs=(HBM_SPEC, HBM_SPEC), out_specs=(SEM_SPEC, SEM_SPEC, HBM_SPEC, HBM_SPEC, VMEM_SPEC),
        input_output_aliases={0: 2, 1: 3},
        compiler_params=pltpu.CompilerParams(has_side_effects=SPLIT_EFFECT),
    )(_in_hbm(blocks), _in_hbm(lax.empty(blocks.shape, blocks.dtype)))


def _exchange_wait(send, recv, x_thru, land_thru, after, name):
    def body(x_ref, land_ref, send_ref, recv_ref, after_ref, x_dead, got_ref):
        ex = _Exchange([x_ref], [land_ref], send_ref, recv_ref, None)
        for k in range(N_DEV - 1):
            ex._cp(0, k, ex.me_i).wait_send()
            ex._cp(0, k, ex.peers[k][1]).wait_recv()

    hbm = pltpu.HBM(x_thru.shape, x_thru.dtype)
    return pl.pallas_call(
        body, name=name, out_shape=(hbm, hbm),
        in_specs=(HBM_SPEC, HBM_SPEC, SEM_SPEC, SEM_SPEC, ANY_SPEC), out_specs=(HBM_SPEC, HBM_SPEC),
        input_output_aliases={0: 0, 1: 1},
        compiler_params=pltpu.CompilerParams(has_side_effects=SPLIT_EFFECT),
    )(x_thru, land_thru, send, recv, after)


def _ada_fwd(c_all, w_ada, b_slice):
    def body(c_ref, w_ref, b_ref, o_ref):
        cv = c_ref[...]
        act = cv * _sigmoid(cv)
        o_ref[...] = _mm(act, w_ref[...]) + b_ref[...]

    nb, nc = c_all.shape[0], w_ada.shape[1]
    return pl.pallas_call(body, name="ada_fwd", out_shape=jax.ShapeDtypeStruct((nb, nc), F32),
                          compiler_params=_cparams())(c_all, w_ada, b_slice)


def _adam_math(w, g, m, v):
    m = B1 * m + (1.0 - B1) * g
    v = B2 * v + (1.0 - B2) * jnp.square(g)
    m_hat = m / (1.0 - B1 ** STEP)
    v_hat = v / (1.0 - B2 ** STEP)
    delta = -LR * (m_hat / (jnp.sqrt(v_hat) + AEPS) + WD * w)
    return delta, m, v


def _ada_bwd_adam(c_all, dmod_slice, w, m, v):
    rows, cols = w.shape
    br = 256

    def body(c_ref, d_ref, w_ref, m_ref, v_ref, g_out, dl_out, m_out, v_out):
        cv = c_ref[...]
        act = cv * _sigmoid(cv)
        g = _mm_tn(act, d_ref[...])
        g_out[...] = g
        dl, mn, vn = _adam_math(w_ref[...], g, m_ref[...], v_ref[...])
        dl_out[...] = dl
        m_out[...] = mn
        v_out[...] = vn

    nb = c_all.shape[0]
    wspec = pl.BlockSpec((br, cols), lambda i: (i, 0))
    return pl.pallas_call(
        body, name="ada_bwd_adam", grid=(rows // br,),
        in_specs=[pl.BlockSpec((nb, br), lambda i: (0, i)), pl.BlockSpec((nb, cols), lambda i: (0, 0)),
                  wspec, wspec, wspec],
        out_specs=[wspec] * 4, out_shape=[jax.ShapeDtypeStruct((rows, cols), F32)] * 4,
        compiler_params=_cparams(),
    )(c_all, dmod_slice, w, m, v)


def _mix_in(x2, mod, g_mix, win_g, seq, shards):
    T = x2.shape[0]
    tm = min(512, seq)
    tps = seq // tm
    in_cols = win_g.shape[2]
    ns = len(shards)
    steps = T // tm
    fwd_step = (3 * steps) // 4

    def body(*refs):
        x_ref, mod_ref, g_ref, wb_ref = refs[:4]
        sh_refs = refs[4:4 + ns]
        u_ref, pm_ref, wc_ref = refs[4 + ns:7 + ns]
        ga_refs = refs[7 + ns:7 + 2 * ns]
        w_ref, send, recv, loc = refs[7 + 2 * ns:]
        step = pl.program_id(0)
        gather = _Gather(sh_refs, ga_refs, send, recv, loc)

        @pl.when(step == 0)
        def _():
            gather.start()

        @pl.when(step == fwd_step)
        def _():
            gather.forward()

        @pl.when(step == steps - 1)
        def _():
            gather.finish()

        @pl.when(step == 0)
        def _():
            w_ref[:, OFF_DT:] = jnp.zeros((D, PROJ_W - OFF_DT), BF16)
            for j in range(N_DEV):
                w_ref[:, in_cols * j:in_cols * (j + 1)] = wb_ref[j]
            wc_ref[...] = w_ref[...]

        x = x_ref[...]
        r = lax.rsqrt(jnp.mean(x * x, axis=-1, keepdims=True) + EPS)
        md = mod_ref[0]
        u = (x * r * g_ref[...]) * (1.0 + md[1:2]) + md[0:1]
        ub = u.astype(BF16)
        u_ref[...] = ub
        pm_ref[...] = jnp.dot(ub, w_ref[...], preferred_element_type=F32)

    whole = pl.BlockSpec(memory_space=pltpu.VMEM)
    return pl.pallas_call(
        body, name="mix_in", grid=(T // tm,),
        in_specs=[pl.BlockSpec((tm, D), lambda i: (i, 0)), pl.BlockSpec((1, 8, D), lambda i: (i // tps, 0, 0)),
                  pl.BlockSpec((1, D), lambda i: (0, 0)), whole] + [ANY_SPEC] * ns,
        out_specs=[pl.BlockSpec((tm, D), lambda i: (i, 0)), pl.BlockSpec((tm, PROJ_W), lambda i: (i, 0)),
                   pl.BlockSpec((D, PROJ_W), lambda i: (0, 0))] + [ANY_SPEC] * ns,
        out_shape=[jax.ShapeDtypeStruct((T, D), BF16), jax.ShapeDtypeStruct((T, PROJ_W), F32),
                   jax.ShapeDtypeStruct((D, PROJ_W), BF16)]
        + [jax.ShapeDtypeStruct((N_DEV,) + v.shape, v.dtype) for v in shards],
        scratch_shapes=[pltpu.VMEM((D, PROJ_W), BF16)] + _comm_scratch(ns),
        compiler_params=_cparams(),
    )(x2, mod, g_mix, win_g, *shards)


def _chunk_forward(up, z, ux, dtin, halo_p, halo_x, hprev, cw, cb, hp, gssd, wpool, pscale, t0, y_scr):
    L = CHUNK
    out = {}
    row = lax.broadcasted_iota(jnp.int32, (L, 1), 0)
    t = (t0 + row + 1).astype(F32)
    e = jnp.concatenate([halo_p, up], axis=0)
    s2 = e + pltpu.roll(e, 1, 0)
    s4 = s2 + pltpu.roll(s2, 2, 0)
    s8 = s4 + pltpu.roll(s4, 4, 0)
    s16 = s8 + pltpu.roll(s8, 8, 0)
    sums = (s2, s4, s8, s16)
    p, inv, yp = [], [], []
    for gi, w in enumerate(WINDOWS):
        sl = slice(gi * LANES, (gi + 1) * LANES)
        ic = 1.0 / jnp.minimum(t, float(w))
        pg = sums[gi][POOL_HALO:, sl] * ic - up[:, sl]
        p.append(pg)
        inv.append(ic)
        yp.append(_mm(pg, wpool[gi]))
    out["p"], out["inv"], out["yp"] = p, inv, yp
    out["y_pool"] = jnp.concatenate(yp, axis=1) * pscale
    ex = jnp.concatenate([halo_x, ux], axis=0)
    taps = [pltpu.roll(ex, 3, 0)[CONV_HALO:], pltpu.roll(ex, 2, 0)[CONV_HALO:], pltpu.roll(ex, 1, 0)[CONV_HALO:], ux]
    cv = cb + taps[0] * cw[0:1] + taps[1] * cw[1:2] + taps[2] * cw[2:3] + taps[3] * cw[3:4]
    sg = _sigmoid(cv)
    xbc = cv * sg
    out["taps"], out["cv"], out["sg"] = taps, cv, sg
    X = xbc[:, :D]
    Bm = xbc[:, D:D + N_GROUPS * N_STATE]
    Cm = xbc[:, D + N_GROUPS * N_STATE:]
    pre = dtin + hp[0:1]
    dt = jnp.maximum(pre, 0.0) + jnp.log(1.0 + jnp.exp(-jnp.abs(pre)))
    a_row = -jnp.exp(hp[1:2])
    da = dt * a_row
    ri = lax.broadcasted_iota(jnp.int32, (L, L), 0)
    ci = lax.broadcasted_iota(jnp.int32, (L, L), 1)
    causal = ri >= ci
    cum = _dot01(causal.astype(F32), da, 3, split_lhs=False)
    cum_t = cum.T
    cum_last = cum[L - 1:L]
    eo = jnp.exp(cum)
    dec = jnp.exp(cum_last - cum)
    cd = jnp.exp(cum_last)
    exm = _expand_mat()
    rows8 = jnp.concatenate([cd, hp[2:3], jnp.zeros((6, LANES), F32)], axis=0)
    rep = _dot01(jnp.concatenate([dt, eo, dec, rows8], axis=0), exm, 2)
    dt_rep, eo_rep, dec_rep = rep[0:L], rep[L:2 * L], rep[2 * L:3 * L]
    cd_rep, dskip_rep = rep[3 * L:3 * L + 1], rep[3 * L + 1:3 * L + 2]
    xdt = X * dt_rep
    out.update(X=X, Bm=Bm, Cm=Cm, pre=pre, dt=dt, a_row=a_row, cum=cum, cum_t=cum_t, eo=eo, dec=dec, cd=cd,
               dt_rep=dt_rep, eo_rep=eo_rep, dec_rep=dec_rep, cd_rep=cd_rep, dskip_rep=dskip_rep, xdt=xdt,
               causal=causal, anti=(ri <= ci).astype(F32), exm=exm)
    G, lms, yoff, hnew, xdec = [], [], [], [], []
    for g in range(N_GROUPS):
        gs = slice(g * GROUP_W, (g + 1) * GROUP_W)
        Bg = Bm[:, g * N_STATE:(g + 1) * N_STATE]
        Cg = Cm[:, g * N_STATE:(g + 1) * N_STATE]
        Gg = _mm_nt(Cg, Bg)
        G.append(Gg)
        for hh in range(N_HEADS // N_GROUPS):
            h = g * (N_HEADS // N_GROUPS) + hh
            seg = cum[:, h:h + 1] - cum_t[h:h + 1, :]
            lm = jnp.where(causal, jnp.exp(jnp.minimum(seg, 0.0)), 0.0)
            lms.append(lm)
            hs = slice(h * HEAD_DIM, (h + 1) * HEAD_DIM)
            y_scr[:, hs] = _mm(Gg * lm, xdt[:, hs])
        xd = xdt[:, gs] * dec_rep[:, gs]
        xdec.append(xd)
        sgm = _mm_tn(Bg, xd)
        yoff.append(_mm(Cg, hprev[g]) * eo_rep[:, gs])
        hnew.append(hprev[g] * cd_rep[:, gs] + sgm)
    out.update(G=G, lms=lms, yoff=yoff, hnew=hnew, xdec=xdec)
    y = y_scr[...] + jnp.concatenate(yoff, axis=1) + dskip_rep * X
    sz = _sigmoid(z)
    silz = z * sz
    yz = y * silz
    rg, yn = [], []
    for g in range(N_GROUPS):
        gs = slice(g * GROUP_W, (g + 1) * GROUP_W)
        r = lax.rsqrt(jnp.mean(yz[:, gs] * yz[:, gs], axis=-1, keepdims=True) + EPS)
        rg.append(r)
        yn.append(yz[:, gs] * r)
    yn = jnp.concatenate(yn, axis=1)
    out.update(y=y, sz=sz, silz=silz, rg=rg, yn=yn)
    out["y_ssd"] = yn * gssd
    return out


def _mixer_fwd(pm, cw, cb, hp, gssd, wpool, pscale, nb, seq, shards):
    nc = seq // CHUNK
    ns = len(shards)
    steps = nb * nc
    fwd_step = (3 * steps) // 4

    def body(*refs):
        pm_ref, cw_ref, cb_ref, hp_ref, gs_ref, wp_ref, ps_ref = refs[:7]
        sh_refs = refs[7:7 + ns]
        ym_ref, hs_ref = refs[7 + ns:9 + ns]
        ga_refs = refs[9 + ns:9 + 2 * ns]
        halo_p, halo_x, state, y_scr, send, recv, loc = refs[9 + 2 * ns:]
        c = pl.program_id(1)
        step = pl.program_id(0) * nc + c
        gather = _Gather(sh_refs, ga_refs, send, recv, loc)

        @pl.when(step == 0)
        def _():
            gather.start()

        @pl.when(step == fwd_step)
        def _():
            gather.forward()

        @pl.when(c == 0)
        def _():
            halo_p[...] = jnp.zeros_like(halo_p)
            halo_x[...] = jnp.zeros_like(halo_x)
            state[...] = jnp.zeros_like(state)

        up = pm_ref[:, 0:POOL_W]
        z = pm_ref[:, OFF_Z:OFF_XBC]
        ux = pm_ref[:, OFF_XBC:OFF_DT]
        hprev = [state[0], state[1]]
        hs_ref[0, 0, 0] = hprev[0]
        hs_ref[0, 0, 1] = hprev[1]
        o = _chunk_forward(up, z, ux, pm_ref[:, OFF_DT:], halo_p[...], halo_x[...], hprev, cw_ref[...], cb_ref[...],
                           hp_ref[...], gs_ref[...], wp_ref[...], ps_ref[...], c * CHUNK, y_scr)
        ym_ref[:, 0:POOL_W] = o["y_pool"].astype(BF16)
        ym_ref[:, POOL_W:] = o["y_ssd"].astype(BF16)
        state[0] = o["hnew"][0]
        state[1] = o["hnew"][1]
        halo_p[...] = up[CHUNK - POOL_HALO:]
        halo_x[...] = ux[CHUNK - CONV_HALO:]

        @pl.when(step == steps - 1)
        def _():
            gather.finish()

    def full(shape):
        return pl.BlockSpec(shape, lambda b, c: (0,) * len(shape))

    T = nb * seq
    return pl.pallas_call(
        body, name="mixer_fwd", grid=(nb, nc),
        in_specs=[pl.BlockSpec((CHUNK, PROJ_W), lambda b, c: (b * nc + c, 0)),
                  full((4, CONV_CH)), full((1, CONV_CH)), full((8, LANES)), full((1, D)),
                  full((4, LANES, LANES)), full((1, POOL_W))] + [ANY_SPEC] * ns,
        out_specs=[pl.BlockSpec((CHUNK, MIX_W), lambda b, c: (b * nc + c, 0)),
                   pl.BlockSpec((1, 1, N_GROUPS, N_STATE, GROUP_W), lambda b, c: (b, c, 0, 0, 0))] + [ANY_SPEC] * ns,
        out_shape=[jax.ShapeDtypeStruct((T, MIX_W), BF16),
                   jax.ShapeDtypeStruct((nb, nc, N_GROUPS, N_STATE, GROUP_W), F32)]
        + [jax.ShapeDtypeStruct((N_DEV,) + v.shape, v.dtype) for v in shards],
        scratch_shapes=[pltpu.VMEM((POOL_HALO, POOL_W), F32), pltpu.VMEM((CONV_HALO, CONV_CH), F32),
                        pltpu.VMEM((N_GROUPS, N_STATE, GROUP_W), F32), pltpu.VMEM((CHUNK, D), F32)] + _comm_scratch(ns),
        compiler_params=_cparams(),
    )(pm, cw, cb, hp, gssd, wpool, pscale, *shards)


def _mixer_bwd(pm, dym, hstates, cw, cb, hp, gssd, wpool, pscale, nb, seq, blocks):
    nc = seq // CHUNK
    hpg = N_HEADS // N_GROUPS
    ns = len(blocks)
    steps = nb * nc

    def body(*refs):
        (pm_ref, hpool_ref, hxbc_ref, dy_ref, hs_ref, cw_ref, cb_ref, hp_ref, gs_ref, wp_ref, ps_ref) = refs[:11]
        bl_refs = refs[11:11 + ns]
        dpm_ref, dconv_ref, dhp_ref, dvec_ref, dwp_ref = refs[11 + ns:16 + ns]
        ex_refs = refs[16 + ns:16 + 2 * ns]
        nxt_q, nxt_cv, rstate, y_scr, dx_scr, send, recv, loc = refs[16 + 2 * ns:]
        b = pl.program_id(0)
        ci = pl.program_id(1)
        c = nc - 1 - ci
        exchange = _Exchange(bl_refs, ex_refs, send, recv, loc)

        @pl.when((b == 0) & (ci == 0))
        def _():
            exchange.start()

        @pl.when((b == 0) & (ci == 0))
        def _():
            for r in (dconv_ref, dhp_ref, dvec_ref, dwp_ref):
                r[...] = jnp.zeros_like(r)

        @pl.when(ci == 0)
        def _():
            nxt_q[...] = jnp.zeros_like(nxt_q)
            nxt_cv[...] = jnp.zeros_like(nxt_cv)
            rstate[...] = jnp.zeros_like(rstate)

        first = (c > 0).astype(F32)
        up = pm_ref[:, 0:POOL_W]
        z = pm_ref[:, OFF_Z:OFF_XBC]
        ux = pm_ref[:, OFF_XBC:OFF_DT]
        halo_p = hpool_ref[...] * first
        halo_x = hxbc_ref[...] * first
        hprev = [hs_ref[0, 0, 0], hs_ref[0, 0, 1]]
        cw, cb, hp, gssd, wpool, pscale = cw_ref[...], cb_ref[...], hp_ref[...], gs_ref[...], wp_ref[...], ps_ref[...]
        o = _chunk_forward(up, z, ux, pm_ref[:, OFF_DT:], halo_p, halo_x, hprev, cw, cb, hp, gssd, wpool, pscale,
                           c * CHUNK, y_scr)
        L = CHUNK
        dy_pool = dy_ref[:, 0:POOL_W].astype(F32)
        dy_ssd = dy_ref[:, POOL_W:].astype(F32)

        dvec_ref[1:2, 0:POOL_W] += jnp.sum(dy_pool * jnp.concatenate(o["yp"], axis=1), axis=0, keepdims=True)
        dyp = dy_pool * pscale
        qs = []
        dps = []
        for gi in range(len(WINDOWS)):
            sl = slice(gi * LANES, (gi + 1) * LANES)
            dwp_ref[gi] += _mm_tn(o["p"][gi], dyp[:, sl])
            dpg = _mm_nt(dyp[:, sl], wpool[gi])
            dps.append(dpg)
            qs.append(dpg * o["inv"][gi])
        q = jnp.concatenate(qs, axis=1)
        e = jnp.concatenate([q, nxt_q[...]], axis=0)
        n = L + POOL_HALO
        s2 = e + pltpu.roll(e, n - 1, 0)
        s4 = s2 + pltpu.roll(s2, n - 2, 0)
        s8 = s4 + pltpu.roll(s4, n - 4, 0)
        s16 = s8 + pltpu.roll(s8, n - 8, 0)
        sums = (s2, s4, s8, s16)
        for gi in range(len(WINDOWS)):
            sl = slice(gi * LANES, (gi + 1) * LANES)
            dpm_ref[:, sl] = (sums[gi][:L, sl] - dps[gi]).astype(BF16)
        nxt_q[...] = q[:POOL_HALO]

        yn, y, silz, sz = o["yn"], o["y"], o["silz"], o["sz"]
        dvec_ref[0:1] += jnp.sum(dy_ssd * yn, axis=0, keepdims=True)
        dyn = dy_ssd * gssd
        dyz = []
        for g in range(N_GROUPS):
            gs = slice(g * GROUP_W, (g + 1) * GROUP_W)
            mean = jnp.mean(dyn[:, gs] * yn[:, gs], axis=-1, keepdims=True)
            dyz.append(o["rg"][g] * (dyn[:, gs] - yn[:, gs] * mean))
        dyz = jnp.concatenate(dyz, axis=1)
        dyv = dyz * silz
        dpm_ref[:, OFF_Z:OFF_XBC] = (dyz * y * (sz * (1.0 + z * (1.0 - sz)))).astype(BF16)

        X, Bm, Cm, xdt = o["X"], o["Bm"], o["Cm"], o["xdt"]
        exm = o["exm"]
        rdm = _reduce_mat()
        lane = lax.broadcasted_iota(jnp.int32, (1, LANES), 1)
        sub = lax.broadcasted_iota(jnp.int32, (LANES, 1), 0)
        dX = o["dskip_rep"] * dyv
        yoff_full = jnp.concatenate(o["yoff"], axis=1)
        rs = jnp.zeros((L, LANES), F32)
        cs_t = jnp.zeros((LANES, L), F32)
        dBs, dCs = [], []
        rh_sums = []
        ddec = []
        for g in range(N_GROUPS):
            gs = slice(g * GROUP_W, (g + 1) * GROUP_W)
            Bg = Bm[:, g * N_STATE:(g + 1) * N_STATE]
            Cg = Cm[:, g * N_STATE:(g + 1) * N_STATE]
            Gg = o["G"][g]
            R = rstate[g]
            dwm = dyv[:, gs] * o["eo_rep"][:, gs]
            dC = _mm_nt(dwm, hprev[g])
            dH = _mm_tn(Cg, dwm)
            dG = jnp.zeros((L, L), F32)
            for hh in range(hpg):
                h = g * hpg + hh
                hs = slice(h * HEAD_DIM, (h + 1) * HEAD_DIM)
                lm = o["lms"][h]
                m_h = Gg * lm
                dM = _mm_nt(dyv[:, hs], xdt[:, hs])
                dx_scr[:, hs] = _mm_tn(m_h, dyv[:, hs])
                qm = dM * m_h
                rs = rs + jnp.sum(qm, axis=1, keepdims=True) * (lane == h).astype(F32)
                cs_t = cs_t + (sub == h).astype(F32) * jnp.sum(qm, axis=0, keepdims=True)
                dG = dG + dM * lm
            dC = dC + _mm(dG, Bg)
            dB = _mm_tn(dG, Cg)
            zx = _mm(Bg, R)
            dxdt_state = zx * o["dec_rep"][:, gs]
            ddec.append(zx * xdt[:, gs])
            dB = dB + _mm_nt(o["xdec"][g], R)
            rh_sums.append(jnp.sum(R * hprev[g], axis=0, keepdims=True))
            rstate[g] = dH + o["cd_rep"][:, gs] * R
            dx_scr[:, gs] = dx_scr[:, gs] + dxdt_state
            dBs.append(dB)
            dCs.append(dC)
        dxdt = dx_scr[...]
        tail = jnp.concatenate([jnp.sum(dyv * X, axis=0, keepdims=True), jnp.concatenate(rh_sums, axis=1),
                                jnp.zeros((6, D), F32)], axis=0)
        red = _dot01(jnp.concatenate([dyv * yoff_full, jnp.concatenate(ddec, axis=1), dxdt * X, tail], axis=0), rdm, 2)
        d_dskip, dcd_row = red[3 * L:3 * L + 1], red[3 * L + 1:3 * L + 2]
        ddec_h = red[L:2 * L] * o["dec"]
        dcum_last = jnp.sum(ddec_h, axis=0, keepdims=True) + dcd_row * o["cd"]
        dcum = red[0:L] + rs - cs_t.T - ddec_h + (sub == L - 1).astype(F32) * dcum_last
        dda = _dot01(o["anti"], dcum, 3, split_lhs=False)
        ddt_v = dda * o["a_row"] + red[2 * L:3 * L]
        dX = dX + dxdt * o["dt_rep"]
        head_mask = (lane < N_HEADS).astype(F32)
        d_alog = jnp.sum(dda * o["dt"], axis=0, keepdims=True) * o["a_row"] * head_mask
        dpre = ddt_v * _sigmoid(o["pre"]) * head_mask
        dpm_ref[:, OFF_DT:] = dpre.astype(BF16)
        d_dtb = jnp.sum(dpre, axis=0, keepdims=True)
        dhp_ref[...] += jnp.concatenate([d_dtb, d_alog, d_dskip * head_mask, jnp.zeros((5, LANES), F32)], axis=0)

        dxbc = jnp.concatenate([dX] + dBs + dCs, axis=1)
        sg, cv = o["sg"], o["cv"]
        dcv = dxbc * (sg * (1.0 + cv * (1.0 - sg)))
        dconv_ref[0:5] += jnp.concatenate(
            [jnp.sum(dcv * o["taps"][k], axis=0, keepdims=True) for k in range(4)]
            + [jnp.sum(dcv, axis=0, keepdims=True)], axis=0)
        e2 = jnp.concatenate([dcv, nxt_cv[...]], axis=0)
        n2 = L + CONV_HALO
        dux = (dcv * cw[3:4] + pltpu.roll(e2, n2 - 1, 0)[:L] * cw[2:3] + pltpu.roll(e2, n2 - 2, 0)[:L] * cw[1:2]
               + pltpu.roll(e2, n2 - 3, 0)[:L] * cw[0:1])
        dpm_ref[:, OFF_XBC:OFF_DT] = dux.astype(BF16)
        nxt_cv[...] = dcv[:CONV_HALO]

        @pl.when((b == nb - 1) & (ci == nc - 1))
        def _():
            exchange.finish()

    def full(shape):
        return pl.BlockSpec(shape, lambda b, c: (0,) * len(shape))

    def rowblk(b, c):
        return b * nc + (nc - 1 - c)

    hp_blocks = CHUNK // POOL_HALO
    hx_blocks = CHUNK // CONV_HALO
    T = nb * seq
    return pl.pallas_call(
        body, name="mixer_bwd", grid=(nb, nc),
        in_specs=[pl.BlockSpec((CHUNK, PROJ_W), lambda b, c: (rowblk(b, c), 0)),
                  pl.BlockSpec((POOL_HALO, POOL_W), lambda b, c: (jnp.maximum(rowblk(b, c) * hp_blocks - 1, 0), 0)),
                  pl.BlockSpec((CONV_HALO, CONV_CH), lambda b, c: (jnp.maximum(rowblk(b, c) * hx_blocks - 1, 0), 1)),
                  pl.BlockSpec((CHUNK, MIX_W), lambda b, c: (rowblk(b, c), 0)),
                  pl.BlockSpec((1, 1, N_GROUPS, N_STATE, GROUP_W), lambda b, c: (b, nc - 1 - c, 0, 0, 0)),
                  full((4, CONV_CH)), full((1, CONV_CH)), full((8, LANES)), full((1, D)),
                  full((4, LANES, LANES)), full((1, POOL_W))] + [ANY_SPEC] * ns,
        out_specs=[pl.BlockSpec((CHUNK, PROJ_W), lambda b, c: (rowblk(b, c), 0)),
                   full((8, CONV_CH)), full((8, LANES)), full((8, D)), full((4, LANES, LANES))] + [ANY_SPEC] * ns,
        out_shape=[jax.ShapeDtypeStruct((T, PROJ_W), BF16),
                   jax.ShapeDtypeStruct((8, CONV_CH), F32), jax.ShapeDtypeStruct((8, LANES), F32),
                   jax.ShapeDtypeStruct((8, D), F32), jax.ShapeDtypeStruct((4, LANES, LANES), F32)]
        + [jax.ShapeDtypeStruct(v.shape, v.dtype) for v in blocks],
        scratch_shapes=[pltpu.VMEM((POOL_HALO, POOL_W), F32), pltpu.VMEM((CONV_HALO, CONV_CH), F32),
                        pltpu.VMEM((N_GROUPS, N_STATE, GROUP_W), F32), pltpu.VMEM((CHUNK, D), F32),
                        pltpu.VMEM((CHUNK, D), F32)] + _comm_scratch(ns),
        compiler_params=_cparams(),
    )(pm, pm, pm, dym, hstates, cw, cb, hp, gssd, wpool, pscale, *blocks)


def _mlp_fused(x2, ymix, target, mod, g_mlp, g_final, w_out, w_up, w_down, seq):
    T = x2.shape[0]
    tm = min(256, seq)
    tps = seq // tm
    nblk = D_FF // FF_BLK

    def body(x_ref, ym_ref, tg_ref, mod_ref, gm_ref, gf_ref, wo_ref, wu_ref, wd_ref,
             da_ref, dym_ref, dh1_ref, u2_ref, f_ref, dup_ref, ddn_ref, dmod_ref, acc_ref, relu_scr):
        i = pl.program_id(0)

        @pl.when(i == 0)
        def _():
            acc_ref[...] = jnp.zeros_like(acc_ref)

        @pl.when(i % tps == 0)
        def _():
            dmod_ref[...] = jnp.zeros_like(dmod_ref)

        md = mod_ref[0]
        gate_m, shift_f, scale_f, gate_f = md[2:3], md[3:4], md[4:5], md[5:6]
        g_mlp, g_fin = gm_ref[...], gf_ref[...]
        a = jnp.dot(ym_ref[...], wo_ref[...], preferred_element_type=F32)
        h1 = x_ref[...] + gate_m * a
        r2 = lax.rsqrt(jnp.mean(h1 * h1, axis=-1, keepdims=True) + EPS)
        n2 = h1 * r2
        u2 = (n2 * g_mlp) * (1.0 + scale_f) + shift_f
        u2b = u2.astype(BF16)
        u2_ref[...] = u2b
        dn = jnp.zeros((tm, D), F32)
        for j in range(nblk):
            js = slice(j * FF_BLK, (j + 1) * FF_BLK)
            upj = jnp.maximum(jnp.dot(u2b, wu_ref[j], preferred_element_type=F32), 0.0)
            relu_scr[:, js] = upj
            fj = (upj * upj).astype(BF16)
            f_ref[:, js] = fj
            dn = dn + jnp.dot(fj, wd_ref[j], preferred_element_type=F32)
        h2 = h1 + gate_f * dn
        r3 = lax.rsqrt(jnp.mean(h2 * h2, axis=-1, keepdims=True) + EPS)
        n3 = h2 * r3
        err = n3 * g_fin - tg_ref[...]
        loss = 0.5 * jnp.sum(jnp.mean(err * err, axis=-1, keepdims=True), axis=0, keepdims=True)
        dout = err * (1.0 / D)
        d_gfin = jnp.sum(dout * n3, axis=0, keepdims=True)
        dn3 = dout * g_fin
        dh2 = r3 * (dn3 - n3 * jnp.mean(dn3 * n3, axis=-1, keepdims=True))
        d_gate_f = jnp.sum(dh2 * dn, axis=0, keepdims=True)
        ddn = (gate_f * dh2).astype(BF16)
        ddn_ref[...] = ddn
        du2 = jnp.zeros((tm, D), F32)
        for j in range(nblk):
            js = slice(j * FF_BLK, (j + 1) * FF_BLK)
            dfj = lax.dot_general(ddn, wd_ref[j], (((1,), (1,)), ((), ())), preferred_element_type=F32)
            dupj = (dfj * (2.0 * relu_scr[:, js])).astype(BF16)
            dup_ref[:, js] = dupj
            du2 = du2 + lax.dot_general(dupj, wu_ref[j], (((1,), (1,)), ((), ())), preferred_element_type=F32)
        d_scale_f = jnp.sum(du2 * (n2 * g_mlp), axis=0, keepdims=True)
        d_shift_f = jnp.sum(du2, axis=0, keepdims=True)
        d_gmlp = jnp.sum(du2 * (1.0 + scale_f) * n2, axis=0, keepdims=True)
        dn2 = du2 * (g_mlp * (1.0 + scale_f))
        dh1 = dh2 + r2 * (dn2 - n2 * jnp.mean(dn2 * n2, axis=-1, keepdims=True))
        dh1_ref[...] = dh1
        d_gate_m = jnp.sum(dh1 * a, axis=0, keepdims=True)
        da = (gate_m * dh1).astype(BF16)
        da_ref[...] = da
        dym_ref[...] = lax.dot_general(da, wo_ref[...], (((1,), (1,)), ((), ())),
                                       preferred_element_type=F32).astype(BF16)
        dmod_ref[0] += jnp.concatenate([jnp.zeros((2, D), F32), d_gate_m, d_shift_f, d_scale_f, d_gate_f,
                                        jnp.zeros((2, D), F32)], axis=0)
        acc_ref[...] += jnp.concatenate([d_gmlp, d_gfin, loss * jnp.ones((1, D), F32), jnp.zeros((5, D), F32)], axis=0)

    whole = pl.BlockSpec(memory_space=pltpu.VMEM)

    def tok(w):
        return pl.BlockSpec((tm, w), lambda i: (i, 0))

    def vec():
        return pl.BlockSpec((1, D), lambda i: (0, 0))

    nb = T // seq
    return pl.pallas_call(
        body, name="mlp_fused", grid=(T // tm,),
        in_specs=[tok(D), tok(MIX_W), tok(D), pl.BlockSpec((1, 8, D), lambda i: (i // tps, 0, 0)), vec(), vec(),
                  whole, whole, whole],
        out_specs=[tok(D), tok(MIX_W), tok(D), tok(D), tok(D_FF), tok(D_FF), tok(D),
                   pl.BlockSpec((1, 8, D), lambda i: (i // tps, 0, 0)), pl.BlockSpec((8, D), lambda i: (0, 0))],
        out_shape=[jax.ShapeDtypeStruct((T, D), BF16), jax.ShapeDtypeStruct((T, MIX_W), BF16),
                   jax.ShapeDtypeStruct((T, D), F32), jax.ShapeDtypeStruct((T, D), BF16),
                   jax.ShapeDtypeStruct((T, D_FF), BF16), jax.ShapeDtypeStruct((T, D_FF), BF16),
                   jax.ShapeDtypeStruct((T, D), BF16), jax.ShapeDtypeStruct((nb, 8, D), F32),
                   jax.ShapeDtypeStruct((8, D), F32)],
        scratch_shapes=[pltpu.VMEM((tm, D_FF), F32)],
        compiler_params=_cparams(),
    )(x2, ymix, target, mod, g_mlp, g_final, w_out, w_up, w_down)


def _in_bwd(x2, dh1, dpb, mod, g_mix, w_cat, dmod_a, acc_a, seq):
    T = x2.shape[0]
    tm = min(512, seq)
    tps = seq // tm
    steps = T // tm

    def body(x_ref, dh_ref, dpb_ref, mod_ref, g_ref, w_ref, dma_ref, acca_ref, dx_ref, dmod_ref, acc_ref):
        i = pl.program_id(0)

        @pl.when(i == 0)
        def _():
            acc_ref[...] = acca_ref[...]

        @pl.when(i % tps == 0)
        def _():
            dmod_ref[...] = dma_ref[...]

        du = lax.dot_general(dpb_ref[...], w_ref[...], (((1,), (1,)), ((), ())), preferred_element_type=F32)
        x = x_ref[...]
        md = mod_ref[0]
        g = g_ref[...]
        r = lax.rsqrt(jnp.mean(x * x, axis=-1, keepdims=True) + EPS)
        n1 = x * r
        d_scale = jnp.sum(du * (n1 * g), axis=0, keepdims=True)
        d_shift = jnp.sum(du, axis=0, keepdims=True)
        d_g = jnp.sum(du * (1.0 + md[1:2]) * n1, axis=0, keepdims=True)
        dn1 = du * (g * (1.0 + md[1:2]))
        dx_ref[...] = dh_ref[...] + r * (dn1 - n1 * jnp.mean(dn1 * n1, axis=-1, keepdims=True))
        dmod_ref[0] += jnp.concatenate([d_shift, d_scale, jnp.zeros((6, D), F32)], axis=0)
        acc_ref[...] += jnp.concatenate([jnp.zeros((3, D), F32), d_g, jnp.zeros((4, D), F32)], axis=0)

    whole = pl.BlockSpec(memory_space=pltpu.VMEM)
    nb = T // seq
    return pl.pallas_call(
        body, name="in_bwd", grid=(steps,),
        in_specs=[pl.BlockSpec((tm, D), lambda i: (i, 0)), pl.BlockSpec((tm, D), lambda i: (i, 0)),
                  pl.BlockSpec((tm, PROJ_W), lambda i: (i, 0)),
                  pl.BlockSpec((1, 8, D), lambda i: (i // tps, 0, 0)), pl.BlockSpec((1, D), lambda i: (0, 0)),
                  whole, pl.BlockSpec((1, 8, D), lambda i: (i // tps, 0, 0)), pl.BlockSpec((8, D), lambda i: (0, 0))],
        out_specs=[pl.BlockSpec((tm, D), lambda i: (i, 0)),
                   pl.BlockSpec((1, 8, D), lambda i: (i // tps, 0, 0)), pl.BlockSpec((8, D), lambda i: (0, 0))],
        out_shape=[jax.ShapeDtypeStruct((T, D), F32),
                   jax.ShapeDtypeStruct((nb, 8, D), F32), jax.ShapeDtypeStruct((8, D), F32)],
        compiler_params=_cparams(),
    )(x2, dh1, dpb, mod, g_mix, w_cat, dmod_a, acc_a)


def _dw_in(u_b, dpb, in_cols):
    T = u_b.shape[0]
    bk = min(512, T)
    nk = T // bk
    starts = [(in_cols * j // LANES) * LANES for j in range(N_DEV)]
    assert all(s + DW_IN_WIN <= PROJ_W and in_cols * (j + 1) <= s + DW_IN_WIN for j, s in enumerate(starts))

    def body(u_ref, d_ref, o_ref, acc):
        k = pl.program_id(0)

        @pl.when(k == 0)
        def _():
            acc[...] = jnp.zeros_like(acc)

        ut = u_ref[...].T
        for j in range(N_DEV):
            acc[j] += jnp.dot(ut, d_ref[:, starts[j]:starts[j] + DW_IN_WIN], preferred_element_type=F32)

        @pl.when(k == nk - 1)
        def _():
            for j in range(N_DEV):
                off = in_cols * j - starts[j]
                o_ref[j] = acc[j][:, off:off + in_cols].astype(BF16)

    return pl.pallas_call(
        body, name="dw_in", grid=(nk,),
        in_specs=[pl.BlockSpec((bk, D), lambda k: (k, 0)), pl.BlockSpec((bk, PROJ_W), lambda k: (k, 0))],
        out_specs=pl.BlockSpec((N_DEV, D, in_cols), lambda k: (0, 0, 0)),
        out_shape=jax.ShapeDtypeStruct((N_DEV, D, in_cols), BF16),
        scratch_shapes=[pltpu.VMEM((N_DEV, D, DW_IN_WIN), F32)],
        compiler_params=_cparams(),
    )(u_b, dpb)


def _dw_blocks(a, b, name, by_rows, per_step=1):
    T, M = a.shape
    N = b.shape[1]
    bk = min(1024, T)
    nk = T // bk
    whole = pl.BlockSpec(memory_space=pltpu.VMEM)
    if by_rows:
        rows = M // N_DEV
        am = rows * per_step
        nblk = N_DEV // per_step
        a_spec, b_spec = pl.BlockSpec((bk, am), lambda i, k: (k, i)), whole
        out_blk, acc_shape = (per_step, rows, N), (am, N)
    else:
        cols = N // N_DEV
        nblk = N_DEV
        a_spec, b_spec = whole, pl.BlockSpec((bk, cols), lambda i, k: (k, i))
        out_blk, acc_shape = (1, M, cols), (M, cols)

    def body(a_ref, b_ref, o_ref, acc):
        k = pl.program_id(1)

        @pl.when(k == 0)
        def _():
            acc[...] = jnp.zeros_like(acc)

        tok = pl.ds(pl.multiple_of(k * bk, bk), bk)
        a_blk = a_ref[...] if by_rows else a_ref[tok, :]
        b_blk = b_ref[tok, :] if by_rows else b_ref[...]
        acc[...] += lax.dot_general(a_blk, b_blk, (((0,), (0,)), ((), ())), preferred_element_type=F32)

        @pl.when(k == nk - 1)
        def _():
            o_ref[...] = acc[...].reshape(out_blk).astype(BF16)

    return pl.pallas_call(
        body, name=name, grid=(nblk, nk), in_specs=[a_spec, b_spec],
        out_specs=pl.BlockSpec(out_blk, lambda i, k: (i, 0, 0)),
        out_shape=jax.ShapeDtypeStruct((N_DEV,) + out_blk[1:], BF16),
        scratch_shapes=[pltpu.VMEM(acc_shape, F32)],
        compiler_params=_cparams(),
    )(a, b)


def _adam_parts(parts, w, m, v, name):
    rows, cols = w.shape
    br = rows
    for cand in range(rows, 15, -16):
        if rows % cand == 0 and cand * cols * 4 <= ADAM_BLOCK_BYTES:
            br = cand
            break

    def body(p_ref, w_ref, m_ref, v_ref, g_out, dl_out, m_out, v_out):
        g = p_ref[0].astype(F32)
        for k in range(1, N_DEV):
            g = g + p_ref[k].astype(F32)
        g_out[...] = g
        dl, mn, vn = _adam_math(w_ref[...], g, m_ref[...], v_ref[...])
        dl_out[...] = dl
        m_out[...] = mn
        v_out[...] = vn

    wspec = pl.BlockSpec((br, cols), lambda i: (i, 0))
    return pl.pallas_call(
        body, name=name, grid=(rows // br,),
        in_specs=[pl.BlockSpec((N_DEV, br, cols), lambda i: (0, i, 0)), wspec, wspec, wspec],
        out_specs=[wspec] * 4, out_shape=[jax.ShapeDtypeStruct((rows, cols), F32)] * 4,
        compiler_params=_cparams(),
    )(parts, w, m, v)


def _adam_plain(g, w, m, v, name):
    def body(g_ref, w_ref, m_ref, v_ref, dl_out, m_out, v_out):
        dl, mn, vn = _adam_math(w_ref[...], g_ref[...], m_ref[...], v_ref[...])
        dl_out[...] = dl
        m_out[...] = mn
        v_out[...] = vn

    return pl.pallas_call(body, name=name, out_shape=[jax.ShapeDtypeStruct(w.shape, F32)] * 3,
                          compiler_params=_cparams())(g, w, m, v)


SMALL_PARAMS = ("b_ada", "g_mix", "conv_b", "dt_bias", "a_log", "d_skip", "g_ssd", "pool_scale", "g_mlp", "g_final")


def _small_adam(gathered, params):
    n_par = len(SMALL_PARAMS)
    nb = gathered[0].shape[1]

    def body(*refs):
        dmod_ref, acc_ref, conv_ref, vec_ref, hd_ref = refs[:5]
        par_refs = refs[5:5 + 3 * n_par]
        out_refs = refs[5 + 3 * n_par:5 + 7 * n_par]
        cw_out, acc_out = refs[5 + 7 * n_par:]

        def total(ref):
            t = ref[0]
            for k in range(1, N_DEV):
                t = t + ref[k]
            return t

        dm = total(dmod_ref)
        dmb = dm[0]
        for b in range(1, nb):
            dmb = dmb + dm[b]
        ac, cv, vc, hd = total(acc_ref), total(conv_ref), total(vec_ref), total(hd_ref)
        cw_out[...] = cv[0:4]
        acc_out[...] = ac
        grads = {
            "b_ada": jnp.concatenate([dmb[r:r + 1] for r in range(6)], axis=1), "g_mix": ac[3:4], "conv_b": cv[4:5],
            "dt_bias": hd[0:1, 0:N_HEADS], "a_log": hd[1:2, 0:N_HEADS], "d_skip": hd[2:3, 0:N_HEADS],
            "g_ssd": vc[0:1], "pool_scale": vc[1:2, 0:POOL_W], "g_mlp": ac[0:1], "g_final": ac[1:2],
        }
        for i, name in enumerate(SMALL_PARAMS):
            w_ref, m_ref, v_ref = par_refs[3 * i:3 * i + 3]
            g = grads[name]
            dl, mn, vn = _adam_math(w_ref[...], g, m_ref[...], v_ref[...])
            g_o, d_o, m_o, v_o = out_refs[4 * i:4 * i + 4]
            g_o[...] = g
            d_o[...] = dl
            m_o[...] = mn
            v_o[...] = vn

    flat = [a for name in SMALL_PARAMS for a in params[name]]
    out_shape = [jax.ShapeDtypeStruct(params[name][0].shape, F32) for name in SMALL_PARAMS for _ in range(4)]
    out_shape += [jax.ShapeDtypeStruct((4, CONV_CH), F32), jax.ShapeDtypeStruct((8, D), F32)]
    return pl.pallas_call(body, name="small_adam", out_shape=out_shape, compiler_params=_cparams())(*gathered, *flat)


def kernel(x, c, w_ada, b_ada, g_mix, w_in, conv_w, conv_b, dt_bias, a_log, d_skip, g_ssd, w_pool, pool_scale, w_out, g_mlp, w_up, w_down, g_final, loss_target, m_w_ada, m_b_ada, m_g_mix, m_w_in, m_conv_w, m_conv_b, m_dt_bias, m_a_log, m_d_skip, m_g_ssd, m_w_pool, m_pool_scale, m_w_out, m_g_mlp, m_w_up, m_w_down, m_g_final, v_w_ada, v_b_ada, v_g_mix, v_w_in, v_conv_w, v_conv_b, v_dt_bias, v_a_log, v_d_skip, v_g_ssd, v_w_pool, v_pool_scale, v_w_out, v_g_mlp, v_w_up, v_w_down, v_g_final):
    nb, seq, _ = x.shape
    T = nb * seq
    me = 4 * lax.axis_index("x") + 2 * lax.axis_index("y") + lax.axis_index("c")
    in_cols = w_in.shape[2]
    ada_cols = w_ada.shape[2]
    cw_cols = conv_w.shape[2]

    c_g, cw_g, win_g = _all_gather([c, conv_w[0], w_in[0].astype(BF16)], "ag_first")
    c_all = c_g.reshape(N_DEV * nb, D)
    cw_full = cw_g.transpose(1, 0, 2).reshape(4, CONV_CH)

    b_slice = lax.dynamic_slice(b_ada, (0, me * ada_cols), (1, ada_cols))
    mod_cols = _ada_fwd(c_all, w_ada[0], b_slice)
    (mod_g,) = _all_gather([mod_cols], "ag_mod")
    mod_all = mod_g.transpose(1, 0, 2).reshape(N_DEV * nb, 6, D)
    mod_mine = lax.dynamic_slice(mod_all, (me * nb, 0, 0), (nb, 6, D))
    mod = jnp.pad(mod_mine, ((0, 0), (0, 2), (0, 0)))

    x2 = x.reshape(T, D)
    tg2 = loss_target.reshape(T, D)
    heads = jnp.pad(jnp.concatenate([dt_bias, a_log, d_skip], axis=0), ((0, 5), (0, LANES - N_HEADS)))
    wpool_b = w_pool[0]
    u_b, pm, w_cat, wup_g = _mix_in(x2, mod, g_mix, win_g, seq, [w_up[0].astype(BF16)])
    ymix, hstates, wout_g, wdn_g = _mixer_fwd(
        pm, cw_full, conv_b, heads, g_ssd, wpool_b, pool_scale, nb, seq,
        [w_out[0].astype(BF16), w_down[0].astype(BF16)])
    da_b, dym, dh1, u2_b, f_b, dup_b, ddn_b, dmod_a, acc_a = _mlp_fused(
        x2, ymix, tg2, mod, g_mlp, g_final.reshape(1, D), wout_g.reshape(MIX_W, D), wup_g, wdn_g, seq)

    gout_p = _dw_blocks(ymix, da_b, "dw_out", True, per_step=4)
    gup_p = _dw_blocks(u2_b, dup_b, "dw_up", False)
    gdn_p = _dw_blocks(f_b, ddn_b, "dw_down", True)
    dpb, d_conv, d_heads, d_vec, d_wpool, gout_r, gup_r, gdn_r = _mixer_bwd(
        pm, dym, hstates, cw_full, conv_b, heads, g_ssd, wpool_b, pool_scale, nb, seq, [gout_p, gup_p, gdn_p])
    gin_p = _dw_in(u_b, dpb, in_cols)
    ex_send, ex_recv, gin_thru, gin_land, ex_token = _exchange_start(gin_p, "gin_start")
    grad_x2, dmod, acc = _in_bwd(x2, dh1, dpb, mod, g_mix + ex_token[0:1, 0:1], w_cat, dmod_a, acc_a, seq)

    g_out, d_out, nm_out, nv_out = _adam_parts(gout_r, w_out[0], m_w_out[0], v_w_out[0], "adam_w_out")
    g_up, d_up, nm_up, nv_up = _adam_parts(gup_r, w_up[0], m_w_up[0], v_w_up[0], "adam_w_up")
    g_dn, d_dn, nm_dn, nv_dn = _adam_parts(gdn_r, w_down[0], m_w_down[0], v_w_down[0], "adam_w_down")

    dmod_g, acc_g, conv_g, vec_g, heads_g, wpool_parts = _all_gather(
        [dmod, acc, d_conv, d_vec, d_heads, d_wpool.reshape(4 * LANES, LANES)], "ag_small_bwd",
        after=[nm_out, nm_up, nm_dn])
    pool2 = (4 * LANES, LANES)
    wpool_outs = _adam_parts(wpool_parts, w_pool.reshape(pool2), m_w_pool.reshape(pool2), v_w_pool.reshape(pool2),
                             "adam_w_pool")
    small_params = {
        "b_ada": (b_ada, m_b_ada, v_b_ada), "g_mix": (g_mix, m_g_mix, v_g_mix), "conv_b": (conv_b, m_conv_b, v_conv_b),
        "dt_bias": (dt_bias, m_dt_bias, v_dt_bias), "a_log": (a_log, m_a_log, v_a_log),
        "d_skip": (d_skip, m_d_skip, v_d_skip), "g_ssd": (g_ssd, m_g_ssd, v_g_ssd),
        "pool_scale": (pool_scale, m_pool_scale, v_pool_scale), "g_mlp": (g_mlp, m_g_mlp, v_g_mlp),
        "g_final": tuple(a.reshape(1, D) for a in (g_final, m_g_final, v_g_final)),
    }
    small_res = _small_adam([dmod_g, acc_g, conv_g, vec_g, heads_g], small_params)
    g_cw_full, acc_sum = small_res[-2:]
    loss = acc_sum[2, 0]

    g_cw = lax.dynamic_slice(g_cw_full, (0, me * cw_cols), (4, cw_cols))
    d_cwp, nm_cwp, nv_cwp = _adam_plain(g_cw, conv_w[0], m_conv_w[0], v_conv_w[0], "adam_conv_w")

    dmod_all = dmod_g[:, :, 0:6].reshape(N_DEV * nb, 6 * D)
    dmod_slice = lax.dynamic_slice(dmod_all, (0, me * ada_cols), (N_DEV * nb, ada_cols))
    g_ada, d_ada, nm_ada, nv_ada = _ada_bwd_adam(c_all, dmod_slice, w_ada[0], m_w_ada[0], v_w_ada[0])

    ex_after = nm_ada[0:8, 0:LANES] + acc_sum[:, 0:LANES]
    gin_own, gin_landed = _exchange_wait(ex_send, ex_recv, gin_thru, gin_land, ex_after, "gin_wait")
    gin_r = lax.dynamic_update_slice(gin_landed, lax.dynamic_slice(gin_own, (me, 0, 0), (1, D, in_cols)), (me, 0, 0))
    g_in, d_in, nm_in, nv_in = _adam_parts(gin_r, w_in[0], m_w_in[0], v_w_in[0], "adam_w_in")

    def small_outs(kind, wpool):
        res = {name: small_res[4 * i + kind] for i, name in enumerate(SMALL_PARAMS)}
        res["g_final"] = res["g_final"].reshape(D)
        res["w_pool"] = wpool.reshape(1, 4, LANES, LANES)
        return res

    def big_outs(ada, win, cwp, wout, wup, wdn):
        return {"w_ada": ada[None], "w_in": win.reshape(1, D, in_cols), "conv_w": cwp[None], "w_out": wout[None],
                "w_up": wup[None], "w_down": wdn[None]}

    order = ["w_ada", "b_ada", "g_mix", "w_in", "conv_w", "conv_b", "dt_bias", "a_log", "d_skip", "g_ssd", "w_pool",
             "pool_scale", "w_out", "g_mlp", "w_up", "w_down", "g_final"]
    groups = [
        {**small_outs(0, wpool_outs[0]), **big_outs(g_ada, g_in, g_cw, g_out, g_up, g_dn)},
        {**small_outs(1, wpool_outs[1]), **big_outs(d_ada, d_in, d_cwp, d_out, d_up, d_dn)},
        {**small_outs(2, wpool_outs[2]), **big_outs(nm_ada, nm_in, nm_cwp, nm_out, nm_up, nm_dn)},
        {**small_outs(3, wpool_outs[3]), **big_outs(nv_ada, nv_in, nv_cwp, nv_out, nv_up, nv_dn)},
    ]
    outs = [loss, grad_x2.reshape(nb, seq, D)]
    for grp in groups:
        outs += [grp[n] for n in order]
    return tuple(outs)
```

```python
import functools

import jax
import jax.numpy as jnp
from jax import lax
from jax.experimental import pallas as pl
from jax.experimental.pallas import tpu as pltpu

F32, BF16 = jnp.float32, jnp.bfloat16
MESH = pl.DeviceIdType.MESH
N_DEV = 8
D = 1024
LANES = 128
CHUNK = 128
POOL_W = 512
WINDOWS = (2, 4, 8, 16)
N_HEADS = 16
HEAD_DIM = 64
N_GROUPS = 2
GROUP_W = 512
N_STATE = 128
CONV_CH = 1536
OFF_Z, OFF_XBC, OFF_DT, IN_W = 512, 1536, 3072, 3088
PROJ_W = OFF_DT + LANES
MIX_W = 1536
D_FF = 4096
FF_BLK = 512
EPS = 1e-5
LR, B1, B2, AEPS, WD, STEP = 0.001, 0.9, 0.999, 1e-08, 0.01, 10
POOL_HALO = 16
CONV_HALO = 8
VMEM_LIMIT = 56 << 20
ADAM_BLOCK_BYTES = 1 << 20
DW_IN_WIN = 512


def _cparams(**kw):
    return pltpu.CompilerParams(vmem_limit_bytes=VMEM_LIMIT, **kw)


def _mm(a, b):
    return jnp.dot(a.astype(BF16), b.astype(BF16), preferred_element_type=F32)


def _mm_nt(a, b):
    return lax.dot_general(a.astype(BF16), b.astype(BF16), (((1,), (1,)), ((), ())), preferred_element_type=F32)


def _mm_tn(a, b):
    return lax.dot_general(a.astype(BF16), b.astype(BF16), (((0,), (0,)), ((), ())), preferred_element_type=F32)


def _split_bf16(v, terms):
    parts, rest = [], v
    for t in range(terms):
        p = rest.astype(BF16)
        parts.append(p)
        if t + 1 < terms:
            rest = rest - p.astype(F32)
    return parts


def _dot01(a, b, terms, split_lhs=True):
    if split_lhs:
        bb = b.astype(BF16)
        prods = [jnp.dot(p, bb, preferred_element_type=F32) for p in _split_bf16(a, terms)]
    else:
        ab = a.astype(BF16)
        prods = [jnp.dot(ab, p, preferred_element_type=F32) for p in _split_bf16(b, terms)]
    out = prods[0]
    for q in prods[1:]:
        out = out + q
    return out


def _sigmoid(v):
    return 1.0 / (1.0 + jnp.exp(-v))


def _expand_mat():
    r = lax.broadcasted_iota(jnp.int32, (LANES, D), 0)
    c = lax.broadcasted_iota(jnp.int32, (LANES, D), 1)
    return (r == c // HEAD_DIM).astype(F32)


def _reduce_mat():
    r = lax.broadcasted_iota(jnp.int32, (D, LANES), 0)
    c = lax.broadcasted_iota(jnp.int32, (D, LANES), 1)
    return (c == r // HEAD_DIM).astype(F32)


def _pos():
    return lax.axis_index("x"), lax.axis_index("y"), lax.axis_index("c")


class _Gather:
    def __init__(self, x_refs, o_refs, send, recv, loc):
        self.x_refs, self.o_refs, self.send, self.recv, self.loc = x_refs, o_refs, send, recv, loc
        self.n = len(x_refs)
        x, y, c = _pos()
        self.c = c
        self.me, self.sib = (x, y, c), (x, y, 1 - c)
        self.chips = [(1 - x, y), (x, 1 - y), (1 - x, 1 - y)]

    def _cp(self, a, k, block, to, src=None):
        dst = self.o_refs[a].at[4 * block[0] + 2 * block[1] + block[2]]
        return pltpu.make_async_remote_copy(
            src_ref=dst if src is None else src, dst_ref=dst,
            send_sem=self.send.at[a * 7 + k], recv_sem=self.recv.at[a * 7 + k],
            device_id=to, device_id_type=MESH)

    def _mine(self, a):
        me = self.me
        return pltpu.make_async_copy(self.x_refs[a], self.o_refs[a].at[4 * me[0] + 2 * me[1] + me[2]], self.loc.at[a])

    def _first(self, a):
        cps = [self._cp(a, 0, self.me, self.sib, src=self.x_refs[a])]
        return cps + [self._cp(a, 1 + j, self.me, (*chip, self.c), src=self.x_refs[a])
                      for j, chip in enumerate(self.chips)]

    def _passed(self, a, j):
        return self._cp(a, 4 + j, (*self.chips[j], self.c), self.sib)

    def start(self):
        for a in range(self.n):
            self._mine(a).start()
            for cp in self._first(a):
                cp.start()

    def forward(self):
        for j, chip in enumerate(self.chips):
            for a in range(self.n):
                self._cp(a, 1 + j, (*chip, self.c), self.me).wait_recv()
                self._passed(a, j).start()

    def finish(self):
        for a in range(self.n):
            self._cp(a, 0, self.sib, self.me).wait_recv()
            for j, chip in enumerate(self.chips):
                self._cp(a, 4 + j, (*chip, 1 - self.c), self.me).wait_recv()
        for a in range(self.n):
            for cp in self._first(a):
                cp.wait_send()
            for j in range(3):
                self._passed(a, j).wait_send()
            self._mine(a).wait()


class _Exchange:
    def __init__(self, x_refs, o_refs, send, recv, loc):
        self.x_refs, self.o_refs, self.send, self.recv, self.loc = x_refs, o_refs, send, recv, loc
        self.n = len(x_refs)
        x, y, c = _pos()
        self.me_i = 4 * x + 2 * y + c
        self.peers = []
        for k in range(1, N_DEV):
            px = 1 - x if (k >> 2) & 1 else x
            py = 1 - y if (k >> 1) & 1 else y
            pc = 1 - c if k & 1 else c
            self.peers.append(((px, py, pc), 4 * px + 2 * py + pc))

    def _mine(self, a):
        return pltpu.make_async_copy(self.x_refs[a].at[self.me_i], self.o_refs[a].at[self.me_i], self.loc.at[a])

    def _cp(self, a, k, landing):
        peer, peer_i = self.peers[k]
        return pltpu.make_async_remote_copy(
            src_ref=self.x_refs[a].at[peer_i], dst_ref=self.o_refs[a].at[landing],
            send_sem=self.send.at[a * 7 + k], recv_sem=self.recv.at[a * 7 + k],
            device_id=peer, device_id_type=MESH)

    def start(self):
        for a in range(self.n):
            self._mine(a).start()
            for k in range(N_DEV - 1):
                self._cp(a, k, self.me_i).start()

    def finish(self):
        for a in range(self.n):
            for k in range(N_DEV - 1):
                self._cp(a, k, self.peers[k][1]).wait_recv()
        for a in range(self.n):
            for k in range(N_DEV - 1):
                self._cp(a, k, self.me_i).wait_send()
            self._mine(a).wait()


def _comm_scratch(n):
    return [pltpu.SemaphoreType.DMA((7 * n,)), pltpu.SemaphoreType.DMA((7 * n,)), pltpu.SemaphoreType.DMA((n,))]


ANY_SPEC = pl.BlockSpec(memory_space=pl.ANY)


def _all_gather(xs, name, after=()):
    n, na = len(xs), len(after)

    def body(*refs):
        g = _Gather(refs[:n], refs[n + na:2 * n + na], *refs[2 * n + na:])
        g.start()
        g.forward()
        g.finish()

    return pl.pallas_call(
        body, name=name,
        out_shape=[jax.ShapeDtypeStruct((N_DEV,) + v.shape, v.dtype) for v in xs],
        in_specs=[ANY_SPEC] * (n + na), out_specs=[ANY_SPEC] * n, scratch_shapes=_comm_scratch(n),
    )(*xs, *after)


HBM_SPEC = pl.BlockSpec(memory_space=pltpu.HBM)
SEM_SPEC = pl.BlockSpec(memory_space=pltpu.SEMAPHORE)
VMEM_SPEC = pl.BlockSpec(memory_space=pltpu.VMEM)
SPLIT_EFFECT = pltpu.SideEffectType.DATAFLOW_SIDE_EFFECTING


def _in_hbm(v):
    return pltpu.with_memory_space_constraint(v, pltpu.HBM)


def _exchange_start(blocks, name):
    def body(x_ref, land_ref, send, recv, x_thru, land_thru, token):
        ex = _Exchange([x_ref], [land_ref], send, recv, None)
        for k in range(N_DEV - 1):
            ex._cp(0, k, ex.me_i).start()
        token[...] = jnp.zeros_like(token)

    hbm = pltpu.HBM(blocks.shape, blocks.dtype)
    return pl.pallas_call(
        body, name=name,
        out_shape=(pltpu.SemaphoreType.DMA((N_DEV - 1,)), pltpu.SemaphoreType.DMA((N_DEV - 1,)), hbm, hbm,
                   jax.ShapeDtypeStruct((8, LANES), F32)),
        in_specs=(HBM_SPEC, HBM_SPEC), out_specs=(SEM_SPEC, SEM_SPEC, HBM_SPEC, HBM_SPEC, VMEM_SPEC),
        input_output_aliases={0: 2, 1: 3},
        compiler_params=pltpu.CompilerParams(has_side_effects=SPLIT_EFFECT),
    )(_in_hbm(blocks), _in_hbm(lax.empty(blocks.shape, blocks.dtype)))


def _exchange_wait(send, recv, x_thru, land_thru, after, name):
    def body(x_ref, land_ref, send_ref, recv_ref, after_ref, x_dead, got_ref):
        ex = _Exchange([x_ref], [land_ref], send_ref, recv_ref, None)
        for k in range(N_DEV - 1):
            ex._cp(0, k, ex.me_i).wait_send()
            ex._cp(0, k, ex.peers[k][1]).wait_recv()

    hbm = pltpu.HBM(x_thru.shape, x_thru.dtype)
    return pl.pallas_call(
        body, name=name, out_shape=(hbm, hbm),
        in_specs=(HBM_SPEC, HBM_SPEC, SEM_SPEC, SEM_SPEC, ANY_SPEC), out_specs=(HBM_SPEC, HBM_SPEC),
        input_output_aliases={0: 0, 1: 1},
        compiler_params=pltpu.CompilerParams(has_side_effects=SPLIT_EFFECT),
    )(x_thru, land_thru, send, recv, after)


def _ada_fwd(c_all, w_ada, b_slice):
    def body(c_ref, w_ref, b_ref, o_ref):
        cv = c_ref[...]
        act = cv * _sigmoid(cv)
        o_ref[...] = _mm(act, w_ref[...]) + b_ref[...]

    nb, nc = c_all.shape[0], w_ada.shape[1]
    return pl.pallas_call(body, name="ada_fwd", out_shape=jax.ShapeDtypeStruct((nb, nc), F32),
                          compiler_params=_cparams())(c_all, w_ada, b_slice)


def _adam_math(w, g, m, v):
    m = B1 * m + (1.0 - B1) * g
    v = B2 * v + (1.0 - B2) * jnp.square(g)
    m_hat = m / (1.0 - B1 ** STEP)
    v_hat = v / (1.0 - B2 ** STEP)
    delta = -LR * (m_hat / (jnp.sqrt(v_hat) + AEPS) + WD * w)
    return delta, m, v


def _ada_bwd_adam(c_all, dmod_slice, w, m, v):
    rows, cols = w.shape
    br = 256

    def body(c_ref, d_ref, w_ref, m_ref, v_ref, g_out, dl_out, m_out, v_out):
        cv = c_ref[...]
        act = cv * _sigmoid(cv)
        g = _mm_tn(act, d_ref[...])
        g_out[...] = g
        dl, mn, vn = _adam_math(w_ref[...], g, m_ref[...], v_ref[...])
        dl_out[...] = dl
        m_out[...] = mn
        v_out[...] = vn

    nb = c_all.shape[0]
    wspec = pl.BlockSpec((br, cols), lambda i: (i, 0))
    return pl.pallas_call(
        body, name="ada_bwd_adam", grid=(rows // br,),
        in_specs=[pl.BlockSpec((nb, br), lambda i: (0, i)), pl.BlockSpec((nb, cols), lambda i: (0, 0)),
                  wspec, wspec, wspec],
        out_specs=[wspec] * 4, out_shape=[jax.ShapeDtypeStruct((rows, cols), F32)] * 4,
        compiler_params=_cparams(),
    )(c_all, dmod_slice, w, m, v)


def _mix_in(x2, mod, g_mix, win_g, seq, shards):
    T = x2.shape[0]
    tm = min(512, seq)
    tps = seq // tm
    in_cols = win_g.shape[2]
    ns = len(shards)
    steps = T // tm
    fwd_step = (3 * steps) // 4

    def body(*refs):
        x_ref, mod_ref, g_ref, wb_ref = refs[:4]
        sh_refs = refs[4:4 + ns]
        u_ref, pm_ref, wc_ref = refs[4 + ns:7 + ns]
        ga_refs = refs[7 + ns:7 + 2 * ns]
        w_ref, send, recv, loc = refs[7 + 2 * ns:]
        step = pl.program_id(0)
        gather = _Gather(sh_refs, ga_refs, send, recv, loc)

        @pl.when(step == 0)
        def _():
            gather.start()

        @pl.when(step == fwd_step)
        def _():
            gather.forward()

        @pl.when(step == steps - 1)
        def _():
            gather.finish()

        @pl.when(step == 0)
        def _():
            w_ref[:, OFF_DT:] = jnp.zeros((D, PROJ_W - OFF_DT), BF16)
            for j in range(N_DEV):
                w_ref[:, in_cols * j:in_cols * (j + 1)] = wb_ref[j]
            wc_ref[...] = w_ref[...]

        x = x_ref[...]
        r = lax.rsqrt(jnp.mean(x * x, axis=-1, keepdims=True) + EPS)
        md = mod_ref[0]
        u = (x * r * g_ref[...]) * (1.0 + md[1:2]) + md[0:1]
        ub = u.astype(BF16)
        u_ref[...] = ub
        pm_ref[...] = jnp.dot(ub, w_ref[...], preferred_element_type=F32)

    whole = pl.BlockSpec(memory_space=pltpu.VMEM)
    return pl.pallas_call(
        body, name="mix_in", grid=(T // tm,),
        in_specs=[pl.BlockSpec((tm, D), lambda i: (i, 0)), pl.BlockSpec((1, 8, D), lambda i: (i // tps, 0, 0)),
                  pl.BlockSpec((1, D), lambda i: (0, 0)), whole] + [ANY_SPEC] * ns,
        out_specs=[pl.BlockSpec((tm, D), lambda i: (i, 0)), pl.BlockSpec((tm, PROJ_W), lambda i: (i, 0)),
                   pl.BlockSpec((D, PROJ_W), lambda i: (0, 0))] + [ANY_SPEC] * ns,
        out_shape=[jax.ShapeDtypeStruct((T, D), BF16), jax.ShapeDtypeStruct((T, PROJ_W), F32),
                   jax.ShapeDtypeStruct((D, PROJ_W), BF16)]
        + [jax.ShapeDtypeStruct((N_DEV,) + v.shape, v.dtype) for v in shards],
        scratch_shapes=[pltpu.VMEM((D, PROJ_W), BF16)] + _comm_scratch(ns),
        compiler_params=_cparams(),
    )(x2, mod, g_mix, win_g, *shards)


def _chunk_forward(up, z, ux, dtin, halo_p, halo_x, hprev, cw, cb, hp, gssd, wpool, pscale, t0, y_scr, cv=None):
    L = CHUNK
    out = {}
    row = lax.broadcasted_iota(jnp.int32, (L, 1), 0)
    t = (t0 + row + 1).astype(F32)
    e = jnp.concatenate([halo_p, up], axis=0)
    s2 = e + pltpu.roll(e, 1, 0)
    s4 = s2 + pltpu.roll(s2, 2, 0)
    s8 = s4 + pltpu.roll(s4, 4, 0)
    s16 = s8 + pltpu.roll(s8, 8, 0)
    sums = (s2, s4, s8, s16)
    p, inv, yp = [], [], []
    for gi, w in enumerate(WINDOWS):
        sl = slice(gi * LANES, (gi + 1) * LANES)
        ic = 1.0 / jnp.minimum(t, float(w))
        pg = sums[gi][POOL_HALO:, sl] * ic - up[:, sl]
        p.append(pg)
        inv.append(ic)
        yp.append(_mm(pg, wpool[gi]))
    out["p"], out["inv"], out["yp"] = p, inv, yp
    out["y_pool"] = jnp.concatenate(yp, axis=1) * pscale
    if cv is None:
        ex = jnp.concatenate([halo_x, ux], axis=0)
        taps = [pltpu.roll(ex, 3, 0)[CONV_HALO:], pltpu.roll(ex, 2, 0)[CONV_HALO:], pltpu.roll(ex, 1, 0)[CONV_HALO:], ux]
        cv = cb + taps[0] * cw[0:1] + taps[1] * cw[1:2] + taps[2] * cw[2:3] + taps[3] * cw[3:4]
    sg = _sigmoid(cv)
    xbc = cv * sg
    out["cv"], out["sg"] = cv, sg
    X = xbc[:, :D]
    Bm = xbc[:, D:D + N_GROUPS * N_STATE]
    Cm = xbc[:, D + N_GROUPS * N_STATE:]
    pre = dtin + hp[0:1]
    dt = jnp.maximum(pre, 0.0) + jnp.log(1.0 + jnp.exp(-jnp.abs(pre)))
    a_row = -jnp.exp(hp[1:2])
    da = dt * a_row
    ri = lax.broadcasted_iota(jnp.int32, (L, L), 0)
    ci = lax.broadcasted_iota(jnp.int32, (L, L), 1)
    causal = ri >= ci
    cum = _dot01(causal.astype(F32), da, 3, split_lhs=False)
    cum_t = cum.T
    cum_last = cum[L - 1:L]
    eo = jnp.exp(cum)
    dec = jnp.exp(cum_last - cum)
    cd = jnp.exp(cum_last)
    exm = _expand_mat()
    rows8 = jnp.concatenate([cd, hp[2:3], jnp.zeros((6, LANES), F32)], axis=0)
    rep = _dot01(jnp.concatenate([dt, eo, dec, rows8], axis=0), exm, 2)
    dt_rep, eo_rep, dec_rep = rep[0:L], rep[L:2 * L], rep[2 * L:3 * L]
    cd_rep, dskip_rep = rep[3 * L:3 * L + 1], rep[3 * L + 1:3 * L + 2]
    xdt = X * dt_rep
    out.update(X=X, Bm=Bm, Cm=Cm, pre=pre, dt=dt, a_row=a_row, cum=cum, cum_t=cum_t, eo=eo, dec=dec, cd=cd,
               dt_rep=dt_rep, eo_rep=eo_rep, dec_rep=dec_rep, cd_rep=cd_rep, dskip_rep=dskip_rep, xdt=xdt,
               causal=causal, anti=(ri <= ci).astype(F32), exm=exm)
    G, lms, yoff, hnew, xdec = [], [], [], [], []
    for g in range(N_GROUPS):
        gs = slice(g * GROUP_W, (g + 1) * GROUP_W)
        Bg = Bm[:, g * N_STATE:(g + 1) * N_STATE]
        Cg = Cm[:, g * N_STATE:(g + 1) * N_STATE]
        Gg = _mm_nt(Cg, Bg)
        G.append(Gg)
        for hh in range(N_HEADS // N_GROUPS):
            h = g * (N_HEADS // N_GROUPS) + hh
            seg = cum[:, h:h + 1] - cum_t[h:h + 1, :]
            lm = jnp.where(causal, jnp.exp(jnp.minimum(seg, 0.0)), 0.0)
            lms.append(lm)
            hs = slice(h * HEAD_DIM, (h + 1) * HEAD_DIM)
            y_scr[:, hs] = _mm(Gg * lm, xdt[:, hs])
        xd = xdt[:, gs] * dec_rep[:, gs]
        xdec.append(xd)
        sgm = _mm_tn(Bg, xd)
        yoff.append(_mm(Cg, hprev[g]) * eo_rep[:, gs])
        hnew.append(hprev[g] * cd_rep[:, gs] + sgm)
    out.update(G=G, lms=lms, yoff=yoff, hnew=hnew, xdec=xdec)
    y = y_scr[...] + jnp.concatenate(yoff, axis=1) + dskip_rep * X
    sz = _sigmoid(z)
    silz = z * sz
    yz = y * silz
    rg, yn = [], []
    for g in range(N_GROUPS):
        gs = slice(g * GROUP_W, (g + 1) * GROUP_W)
        r = lax.rsqrt(jnp.mean(yz[:, gs] * yz[:, gs], axis=-1, keepdims=True) + EPS)
        rg.append(r)
        yn.append(yz[:, gs] * r)
    yn = jnp.concatenate(yn, axis=1)
    out.update(y=y, sz=sz, silz=silz, rg=rg, yn=yn)
    out["y_ssd"] = yn * gssd
    return out


def _mixer_fwd(pm, cw, cb, hp, gssd, wpool, pscale, nb, seq, shards):
    nc = seq // CHUNK
    ns = len(shards)
    steps = nb * nc
    fwd_step = (3 * steps) // 4

    def body(*refs):
        pm_ref, cw_ref, cb_ref, hp_ref, gs_ref, wp_ref, ps_ref = refs[:7]
        sh_refs = refs[7:7 + ns]
        ym_ref, hs_ref, cv_ref = refs[7 + ns:10 + ns]
        ga_refs = refs[10 + ns:10 + 2 * ns]
        halo_p, halo_x, state, y_scr, send, recv, loc = refs[10 + 2 * ns:]
        c = pl.program_id(1)
        step = pl.program_id(0) * nc + c
        gather = _Gather(sh_refs, ga_refs, send, recv, loc)

        @pl.when(step == 0)
        def _():
            gather.start()

        @pl.when(step == fwd_step)
        def _():
            gather.forward()

        @pl.when(c == 0)
        def _():
            halo_p[...] = jnp.zeros_like(halo_p)
            halo_x[...] = jnp.zeros_like(halo_x)
            state[...] = jnp.zeros_like(state)

        up = pm_ref[:, 0:POOL_W]
        z = pm_ref[:, OFF_Z:OFF_XBC]
        ux = pm_ref[:, OFF_XBC:OFF_DT]
        hprev = [state[0], state[1]]
        hs_ref[0, 0, 0] = hprev[0]
        hs_ref[0, 0, 1] = hprev[1]
        o = _chunk_forward(up, z, ux, pm_ref[:, OFF_DT:], halo_p[...], halo_x[...], hprev, cw_ref[...], cb_ref[...],
                           hp_ref[...], gs_ref[...], wp_ref[...], ps_ref[...], c * CHUNK, y_scr)
        ym_ref[:, 0:POOL_W] = o["y_pool"].astype(BF16)
        ym_ref[:, POOL_W:] = o["y_ssd"].astype(BF16)
        cv_ref[...] = o["cv"]
        state[0] = o["hnew"][0]
        state[1] = o["hnew"][1]
        halo_p[...] = up[CHUNK - POOL_HALO:]
        halo_x[...] = ux[CHUNK - CONV_HALO:]

        @pl.when(step == steps - 1)
        def _():
            gather.finish()

    def full(shape):
        return pl.BlockSpec(shape, lambda b, c: (0,) * len(shape))

    T = nb * seq
    return pl.pallas_call(
        body, name="mixer_fwd", grid=(nb, nc),
        in_specs=[pl.BlockSpec((CHUNK, PROJ_W), lambda b, c: (b * nc + c, 0)),
                  full((4, CONV_CH)), full((1, CONV_CH)), full((8, LANES)), full((1, D)),
                  full((4, LANES, LANES)), full((1, POOL_W))] + [ANY_SPEC] * ns,
        out_specs=[pl.BlockSpec((CHUNK, MIX_W), lambda b, c: (b * nc + c, 0)),
                   pl.BlockSpec((1, 1, N_GROUPS, N_STATE, GROUP_W), lambda b, c: (b, c, 0, 0, 0)),
                   pl.BlockSpec((CHUNK, CONV_CH), lambda b, c: (b * nc + c, 0))] + [ANY_SPEC] * ns,
        out_shape=[jax.ShapeDtypeStruct((T, MIX_W), BF16),
                   jax.ShapeDtypeStruct((nb, nc, N_GROUPS, N_STATE, GROUP_W), F32),
                   jax.ShapeDtypeStruct((T, CONV_CH), F32)]
        + [jax.ShapeDtypeStruct((N_DEV,) + v.shape, v.dtype) for v in shards],
        scratch_shapes=[pltpu.VMEM((POOL_HALO, POOL_W), F32), pltpu.VMEM((CONV_HALO, CONV_CH), F32),
                        pltpu.VMEM((N_GROUPS, N_STATE, GROUP_W), F32), pltpu.VMEM((CHUNK, D), F32)] + _comm_scratch(ns),
        compiler_params=_cparams(),
    )(pm, cw, cb, hp, gssd, wpool, pscale, *shards)


def _mixer_bwd(pm, cvs, dym, hstates, cw, cb, hp, gssd, wpool, pscale, nb, seq, after=()):
    nc = seq // CHUNK
    hpg = N_HEADS // N_GROUPS
    na = len(after)

    def body(*refs):
        (pm_ref, hpool_ref, cv_ref, dy_ref, hs_ref, cw_ref, cb_ref, hp_ref, gs_ref, wp_ref, ps_ref) = refs[:11]
        dpm_ref, dconv_ref, dhp_ref, dvec_ref, dwp_ref = refs[11 + na:16 + na]
        nxt_q, nxt_cv, rstate, y_scr, dx_scr = refs[16 + na:]
        b = pl.program_id(0)
        ci = pl.program_id(1)
        c = nc - 1 - ci

        @pl.when((b == 0) & (ci == 0))
        def _():
            for r in (dconv_ref, dhp_ref, dvec_ref, dwp_ref):
                r[...] = jnp.zeros_like(r)

        @pl.when(ci == 0)
        def _():
            nxt_q[...] = jnp.zeros_like(nxt_q)
            nxt_cv[...] = jnp.zeros_like(nxt_cv)
            rstate[...] = jnp.zeros_like(rstate)

        first = (c > 0).astype(F32)
        up = pm_ref[:, 0:POOL_W]
        z = pm_ref[:, OFF_Z:OFF_XBC]
        ux = pm_ref[:, OFF_XBC:OFF_DT]
        halo_p = hpool_ref[...] * first
        hprev = [hs_ref[0, 0, 0], hs_ref[0, 0, 1]]
        cw, cb, hp, gssd, wpool, pscale = cw_ref[...], cb_ref[...], hp_ref[...], gs_ref[...], wp_ref[...], ps_ref[...]
        o = _chunk_forward(up, z, ux, pm_ref[:, OFF_DT:], halo_p, None, hprev, cw, cb, hp, gssd, wpool, pscale,
                           c * CHUNK, y_scr, cv=cv_ref[...])
        L = CHUNK
        dy_pool = dy_ref[:, 0:POOL_W].astype(F32)
        dy_ssd = dy_ref[:, POOL_W:].astype(F32)

        dvec_ref[1:2, 0:POOL_W] += jnp.sum(dy_pool * jnp.concatenate(o["yp"], axis=1), axis=0, keepdims=True)
        dyp = dy_pool * pscale
        qs = []
        dps = []
        for gi in range(len(WINDOWS)):
            sl = slice(gi * LANES, (gi + 1) * LANES)
            dwp_ref[gi] += _mm_tn(o["p"][gi], dyp[:, sl])
            dpg = _mm_nt(dyp[:, sl], wpool[gi])
            dps.append(dpg)
            qs.append(dpg * o["inv"][gi])
        q = jnp.concatenate(qs, axis=1)
        e = jnp.concatenate([q, nxt_q[...]], axis=0)
        n = L + POOL_HALO
        s2 = e + pltpu.roll(e, n - 1, 0)
        s4 = s2 + pltpu.roll(s2, n - 2, 0)
        s8 = s4 + pltpu.roll(s4, n - 4, 0)
        s16 = s8 + pltpu.roll(s8, n - 8, 0)
        sums = (s2, s4, s8, s16)
        for gi in range(len(WINDOWS)):
            sl = slice(gi * LANES, (gi + 1) * LANES)
            dpm_ref[:, sl] = (sums[gi][:L, sl] - dps[gi]).astype(BF16)
        nxt_q[...] = q[:POOL_HALO]

        yn, y, silz, sz = o["yn"], o["y"], o["silz"], o["sz"]
        dvec_ref[0:1] += jnp.sum(dy_ssd * yn, axis=0, keepdims=True)
        dyn = dy_ssd * gssd
        dyz = []
        for g in range(N_GROUPS):
            gs = slice(g * GROUP_W, (g + 1) * GROUP_W)
            mean = jnp.mean(dyn[:, gs] * yn[:, gs], axis=-1, keepdims=True)
            dyz.append(o["rg"][g] * (dyn[:, gs] - yn[:, gs] * mean))
        dyz = jnp.concatenate(dyz, axis=1)
        dyv = dyz * silz
        dpm_ref[:, OFF_Z:OFF_XBC] = (dyz * y * (sz * (1.0 + z * (1.0 - sz)))).astype(BF16)

        X, Bm, Cm, xdt = o["X"], o["Bm"], o["Cm"], o["xdt"]
        exm = o["exm"]
        rdm = _reduce_mat()
        lane = lax.broadcasted_iota(jnp.int32, (1, LANES), 1)
        sub = lax.broadcasted_iota(jnp.int32, (LANES, 1), 0)
        dX = o["dskip_rep"] * dyv
        yoff_full = jnp.concatenate(o["yoff"], axis=1)
        rs = jnp.zeros((L, LANES), F32)
        cs_t = jnp.zeros((LANES, L), F32)
        dBs, dCs = [], []
        rh_sums = []
        ddec = []
        for g in range(N_GROUPS):
            gs = slice(g * GROUP_W, (g + 1) * GROUP_W)
            Bg = Bm[:, g * N_STATE:(g + 1) * N_STATE]
            Cg = Cm[:, g * N_STATE:(g + 1) * N_STATE]
            Gg = o["G"][g]
            R = rstate[g]
            dwm = dyv[:, gs] * o["eo_rep"][:, gs]
            dC = _mm_nt(dwm, hprev[g])
            dH = _mm_tn(Cg, dwm)
            dG = jnp.zeros((L, L), F32)
            for hh in range(hpg):
                h = g * hpg + hh
                hs = slice(h * HEAD_DIM, (h + 1) * HEAD_DIM)
                lm = o["lms"][h]
                m_h = Gg * lm
                dM = _mm_nt(dyv[:, hs], xdt[:, hs])
                dx_scr[:, hs] = _mm_tn(m_h, dyv[:, hs])
                qm = dM * m_h
                rs = rs + jnp.sum(qm, axis=1, keepdims=True) * (lane == h).astype(F32)
                cs_t = cs_t + (sub == h).astype(F32) * jnp.sum(qm, axis=0, keepdims=True)
                dG = dG + dM * lm
            dC = dC + _mm(dG, Bg)
            dB = _mm_tn(dG, Cg)
            zx = _mm(Bg, R)
            dxdt_state = zx * o["dec_rep"][:, gs]
            ddec.append(zx * xdt[:, gs])
            dB = dB + _mm_nt(o["xdec"][g], R)
            rh_sums.append(jnp.sum(R * hprev[g], axis=0, keepdims=True))
            rstate[g] = dH + o["cd_rep"][:, gs] * R
            dx_scr[:, gs] = dx_scr[:, gs] + dxdt_state
            dBs.append(dB)
            dCs.append(dC)
        dxdt = dx_scr[...]
        tail = jnp.concatenate([jnp.sum(dyv * X, axis=0, keepdims=True), jnp.concatenate(rh_sums, axis=1),
                                jnp.zeros((6, D), F32)], axis=0)
        red = _dot01(jnp.concatenate([dyv * yoff_full, jnp.concatenate(ddec, axis=1), dxdt * X, tail], axis=0), rdm, 2)
        d_dskip, dcd_row = red[3 * L:3 * L + 1], red[3 * L + 1:3 * L + 2]
        ddec_h = red[L:2 * L] * o["dec"]
        dcum_last = jnp.sum(ddec_h, axis=0, keepdims=True) + dcd_row * o["cd"]
        dcum = red[0:L] + rs - cs_t.T - ddec_h + (sub == L - 1).astype(F32) * dcum_last
        dda = _dot01(o["anti"], dcum, 3, split_lhs=False)
        ddt_v = dda * o["a_row"] + red[2 * L:3 * L]
        dX = dX + dxdt * o["dt_rep"]
        head_mask = (lane < N_HEADS).astype(F32)
        d_alog = jnp.sum(dda * o["dt"], axis=0, keepdims=True) * o["a_row"] * head_mask
        dpre = ddt_v * _sigmoid(o["pre"]) * head_mask
        dpm_ref[:, OFF_DT:] = dpre.astype(BF16)
        d_dtb = jnp.sum(dpre, axis=0, keepdims=True)
        dhp_ref[...] += jnp.concatenate([d_dtb, d_alog, d_dskip * head_mask, jnp.zeros((5, LANES), F32)], axis=0)

        dxbc = jnp.concatenate([dX] + dBs + dCs, axis=1)
        sg, cv = o["sg"], o["cv"]
        dcv = dxbc * (sg * (1.0 + cv * (1.0 - sg)))
        e2 = jnp.concatenate([dcv, nxt_cv[...]], axis=0)
        n2 = L + CONV_HALO
        ahead = [dcv, pltpu.roll(e2, n2 - 1, 0)[:L], pltpu.roll(e2, n2 - 2, 0)[:L], pltpu.roll(e2, n2 - 3, 0)[:L]]
        dconv_ref[0:5] += jnp.concatenate(
            [jnp.sum(ux * ahead[3 - k], axis=0, keepdims=True) for k in range(4)]
            + [jnp.sum(dcv, axis=0, keepdims=True)], axis=0)
        dux = ahead[0] * cw[3:4] + ahead[1] * cw[2:3] + ahead[2] * cw[1:2] + ahead[3] * cw[0:1]
        dpm_ref[:, OFF_XBC:OFF_DT] = dux.astype(BF16)
        nxt_cv[...] = dcv[:CONV_HALO]

    def full(shape):
        return pl.BlockSpec(shape, lambda b, c: (0,) * len(shape))

    def rowblk(b, c):
        return b * nc + (nc - 1 - c)

    hp_blocks = CHUNK // POOL_HALO
    T = nb * seq
    return pl.pallas_call(
        body, name="mixer_bwd", grid=(nb, nc),
        in_specs=[pl.BlockSpec((CHUNK, PROJ_W), lambda b, c: (rowblk(b, c), 0)),
                  pl.BlockSpec((POOL_HALO, POOL_W), lambda b, c: (jnp.maximum(rowblk(b, c) * hp_blocks - 1, 0), 0)),
                  pl.BlockSpec((CHUNK, CONV_CH), lambda b, c: (rowblk(b, c), 0)),
                  pl.BlockSpec((CHUNK, MIX_W), lambda b, c: (rowblk(b, c), 0)),
                  pl.BlockSpec((1, 1, N_GROUPS, N_STATE, GROUP_W), lambda b, c: (b, nc - 1 - c, 0, 0, 0)),
                  full((4, CONV_CH)), full((1, CONV_CH)), full((8, LANES)), full((1, D)),
                  full((4, LANES, LANES)), full((1, POOL_W))] + [ANY_SPEC] * na,
        out_specs=[pl.BlockSpec((CHUNK, PROJ_W), lambda b, c: (rowblk(b, c), 0)),
                   full((8, CONV_CH)), full((8, LANES)), full((8, D)), full((4, LANES, LANES))],
        out_shape=[jax.ShapeDtypeStruct((T, PROJ_W), BF16),
                   jax.ShapeDtypeStruct((8, CONV_CH), F32), jax.ShapeDtypeStruct((8, LANES), F32),
                   jax.ShapeDtypeStruct((8, D), F32), jax.ShapeDtypeStruct((4, LANES, LANES), F32)],
        scratch_shapes=[pltpu.VMEM((POOL_HALO, POOL_W), F32), pltpu.VMEM((CONV_HALO, CONV_CH), F32),
                        pltpu.VMEM((N_GROUPS, N_STATE, GROUP_W), F32), pltpu.VMEM((CHUNK, D), F32),
                        pltpu.VMEM((CHUNK, D), F32)],
        compiler_params=_cparams(),
    )(pm, pm, cvs, dym, hstates, cw, cb, hp, gssd, wpool, pscale, *after)


def _mlp_fused(x2, ymix, target, mod, g_mlp, g_final, w_out, w_up, w_down, seq):
    T = x2.shape[0]
    tm = min(256, seq)
    tps = seq // tm
    nblk = D_FF // FF_BLK

    def body(x_ref, ym_ref, tg_ref, mod_ref, gm_ref, gf_ref, wo_ref, wu_ref, wd_ref,
             da_ref, dym_ref, dh1_ref, u2_ref, f_ref, dup_ref, ddn_ref, dmod_ref, acc_ref, relu_scr):
        i = pl.program_id(0)

        @pl.when(i == 0)
        def _():
            acc_ref[...] = jnp.zeros_like(acc_ref)

        @pl.when(i % tps == 0)
        def _():
            dmod_ref[...] = jnp.zeros_like(dmod_ref)

        md = mod_ref[0]
        gate_m, shift_f, scale_f, gate_f = md[2:3], md[3:4], md[4:5], md[5:6]
        g_mlp, g_fin = gm_ref[...], gf_ref[...]
        a = jnp.dot(ym_ref[...], wo_ref[...], preferred_element_type=F32)
        h1 = x_ref[...] + gate_m * a
        r2 = lax.rsqrt(jnp.mean(h1 * h1, axis=-1, keepdims=True) + EPS)
        n2 = h1 * r2
        u2 = (n2 * g_mlp) * (1.0 + scale_f) + shift_f
        u2b = u2.astype(BF16)
        u2_ref[...] = u2b
        dn = jnp.zeros((tm, D), F32)
        for j in range(nblk):
            js = slice(j * FF_BLK, (j + 1) * FF_BLK)
            upj = jnp.maximum(jnp.dot(u2b, wu_ref[j], preferred_element_type=F32), 0.0)
            relu_scr[:, js] = upj
            fj = (upj * upj).astype(BF16)
            f_ref[:, js] = fj
            dn = dn + jnp.dot(fj, wd_ref[j], preferred_element_type=F32)
        h2 = h1 + gate_f * dn
        r3 = lax.rsqrt(jnp.mean(h2 * h2, axis=-1, keepdims=True) + EPS)
        n3 = h2 * r3
        err = n3 * g_fin - tg_ref[...]
        loss = 0.5 * jnp.sum(jnp.mean(err * err, axis=-1, keepdims=True), axis=0, keepdims=True)
        dout = err * (1.0 / D)
        d_gfin = jnp.sum(dout * n3, axis=0, keepdims=True)
        dn3 = dout * g_fin
        dh2 = r3 * (dn3 - n3 * jnp.mean(dn3 * n3, axis=-1, keepdims=True))
        d_gate_f = jnp.sum(dh2 * dn, axis=0, keepdims=True)
        ddn = (gate_f * dh2).astype(BF16)
        ddn_ref[...] = ddn
        du2 = jnp.zeros((tm, D), F32)
        for j in range(nblk):
            js = slice(j * FF_BLK, (j + 1) * FF_BLK)
            dfj = lax.dot_general(ddn, wd_ref[j], (((1,), (1,)), ((), ())), preferred_element_type=F32)
            dupj = (dfj * (2.0 * relu_scr[:, js])).astype(BF16)
            dup_ref[:, js] = dupj
            du2 = du2 + lax.dot_general(dupj, wu_ref[j], (((1,), (1,)), ((), ())), preferred_element_type=F32)
        d_scale_f = jnp.sum(du2 * (n2 * g_mlp), axis=0, keepdims=True)
        d_shift_f = jnp.sum(du2, axis=0, keepdims=True)
        d_gmlp = jnp.sum(du2 * (1.0 + scale_f) * n2, axis=0, keepdims=True)
        dn2 = du2 * (g_mlp * (1.0 + scale_f))
        dh1 = dh2 + r2 * (dn2 - n2 * jnp.mean(dn2 * n2, axis=-1, keepdims=True))
        dh1_ref[...] = dh1
        d_gate_m = jnp.sum(dh1 * a, axis=0, keepdims=True)
        da = (gate_m * dh1).astype(BF16)
        da_ref[...] = da
        dym_ref[...] = lax.dot_general(da, wo_ref[...], (((1,), (1,)), ((), ())),
                                       preferred_element_type=F32).astype(BF16)
        dmod_ref[0] += jnp.concatenate([jnp.zeros((2, D), F32), d_gate_m, d_shift_f, d_scale_f, d_gate_f,
                                        jnp.zeros((2, D), F32)], axis=0)
        acc_ref[...] += jnp.concatenate([d_gmlp, d_gfin, loss * jnp.ones((1, D), F32), jnp.zeros((5, D), F32)], axis=0)

    whole = pl.BlockSpec(memory_space=pltpu.VMEM)

    def tok(w):
        return pl.BlockSpec((tm, w), lambda i: (i, 0))

    def vec():
        return pl.BlockSpec((1, D), lambda i: (0, 0))

    nb = T // seq
    return pl.pallas_call(
        body, name="mlp_fused", grid=(T // tm,),
        in_specs=[tok(D), tok(MIX_W), tok(D), pl.BlockSpec((1, 8, D), lambda i: (i // tps, 0, 0)), vec(), vec(),
                  whole, whole, whole],
        out_specs=[tok(D), tok(MIX_W), tok(D), tok(D), tok(D_FF), tok(D_FF), tok(D),
                   pl.BlockSpec((1, 8, D), lambda i: (i // tps, 0, 0)), pl.BlockSpec((8, D), lambda i: (0, 0))],
        out_shape=[jax.ShapeDtypeStruct((T, D), BF16), jax.ShapeDtypeStruct((T, MIX_W), BF16),
                   jax.ShapeDtypeStruct((T, D), F32), jax.ShapeDtypeStruct((T, D), BF16),
                   jax.ShapeDtypeStruct((T, D_FF), BF16), jax.ShapeDtypeStruct((T, D_FF), BF16),
                   jax.ShapeDtypeStruct((T, D), BF16), jax.ShapeDtypeStruct((nb, 8, D), F32),
                   jax.ShapeDtypeStruct((8, D), F32)],
        scratch_shapes=[pltpu.VMEM((tm, D_FF), F32)],
        compiler_params=_cparams(),
    )(x2, ymix, target, mod, g_mlp, g_final, w_out, w_up, w_down)


def _in_bwd(x2, dh1, dpb, mod, g_mix, w_cat, dmod_a, acc_a, seq):
    T = x2.shape[0]
    tm = min(512, seq)
    tps = seq // tm
    steps = T // tm

    def body(x_ref, dh_ref, dpb_ref, mod_ref, g_ref, w_ref, dma_ref, acca_ref, dx_ref, dmod_ref, acc_ref):
        i = pl.program_id(0)

        @pl.when(i == 0)
        def _():
            acc_ref[...] = acca_ref[...]

        @pl.when(i % tps == 0)
        def _():
            dmod_ref[...] = dma_ref[...]

        du = lax.dot_general(dpb_ref[...], w_ref[...], (((1,), (1,)), ((), ())), preferred_element_type=F32)
        x = x_ref[...]
        md = mod_ref[0]
        g = g_ref[...]
        r = lax.rsqrt(jnp.mean(x * x, axis=-1, keepdims=True) + EPS)
        n1 = x * r
        d_scale = jnp.sum(du * (n1 * g), axis=0, keepdims=True)
        d_shift = jnp.sum(du, axis=0, keepdims=True)
        d_g = jnp.sum(du * (1.0 + md[1:2]) * n1, axis=0, keepdims=True)
        dn1 = du * (g * (1.0 + md[1:2]))
        dx_ref[...] = dh_ref[...] + r * (dn1 - n1 * jnp.mean(dn1 * n1, axis=-1, keepdims=True))
        dmod_ref[0] += jnp.concatenate([d_shift, d_scale, jnp.zeros((6, D), F32)], axis=0)
        acc_ref[...] += jnp.concatenate([jnp.zeros((3, D), F32), d_g, jnp.zeros((4, D), F32)], axis=0)

    whole = pl.BlockSpec(memory_space=pltpu.VMEM)
    nb = T // seq
    return pl.pallas_call(
        body, name="in_bwd", grid=(steps,),
        in_specs=[pl.BlockSpec((tm, D), lambda i: (i, 0)), pl.BlockSpec((tm, D), lambda i: (i, 0)),
                  pl.BlockSpec((tm, PROJ_W), lambda i: (i, 0)),
                  pl.BlockSpec((1, 8, D), lambda i: (i // tps, 0, 0)), pl.BlockSpec((1, D), lambda i: (0, 0)),
                  whole, pl.BlockSpec((1, 8, D), lambda i: (i // tps, 0, 0)), pl.BlockSpec((8, D), lambda i: (0, 0))],
        out_specs=[pl.BlockSpec((tm, D), lambda i: (i, 0)),
                   pl.BlockSpec((1, 8, D), lambda i: (i // tps, 0, 0)), pl.BlockSpec((8, D), lambda i: (0, 0))],
        out_shape=[jax.ShapeDtypeStruct((T, D), F32),
                   jax.ShapeDtypeStruct((nb, 8, D), F32), jax.ShapeDtypeStruct((8, D), F32)],
        compiler_params=_cparams(),
    )(x2, dh1, dpb, mod, g_mix, w_cat, dmod_a, acc_a)


def _dw_in(u_b, dpb, in_cols):
    T = u_b.shape[0]
    bk = min(512, T)
    nk = T // bk
    starts = [(in_cols * j // LANES) * LANES for j in range(N_DEV)]
    assert all(s + DW_IN_WIN <= PROJ_W and in_cols * (j + 1) <= s + DW_IN_WIN for j, s in enumerate(starts))

    def body(u_ref, d_ref, o_ref, acc):
        k = pl.program_id(0)

        @pl.when(k == 0)
        def _():
            acc[...] = jnp.zeros_like(acc)

        ut = u_ref[...].T
        for j in range(N_DEV):
            acc[j] += jnp.dot(ut, d_ref[:, starts[j]:starts[j] + DW_IN_WIN], preferred_element_type=F32)

        @pl.when(k == nk - 1)
        def _():
            for j in range(N_DEV):
                off = in_cols * j - starts[j]
                o_ref[j] = acc[j][:, off:off + in_cols].astype(BF16)

    return pl.pallas_call(
        body, name="dw_in", grid=(nk,),
        in_specs=[pl.BlockSpec((bk, D), lambda k: (k, 0)), pl.BlockSpec((bk, PROJ_W), lambda k: (k, 0))],
        out_specs=pl.BlockSpec((N_DEV, D, in_cols), lambda k: (0, 0, 0)),
        out_shape=jax.ShapeDtypeStruct((N_DEV, D, in_cols), BF16),
        scratch_shapes=[pltpu.VMEM((N_DEV, D, DW_IN_WIN), F32)],
        compiler_params=_cparams(),
    )(u_b, dpb)


def _dw_blocks(a, b, name, by_rows, per_step=1, after=()):
    T, M = a.shape
    N = b.shape[1]
    bk = min(1024, T)
    nk = T // bk
    whole = pl.BlockSpec(memory_space=pltpu.VMEM)
    if by_rows:
        rows = M // N_DEV
        am = rows * per_step
        nblk = N_DEV // per_step
        a_spec, b_spec = pl.BlockSpec((bk, am), lambda i, k: (k, i)), whole
        out_blk, acc_shape = (per_step, rows, N), (am, N)
    else:
        cols = N // N_DEV
        nblk = N_DEV
        a_spec, b_spec = whole, pl.BlockSpec((bk, cols), lambda i, k: (k, i))
        out_blk, acc_shape = (1, M, cols), (M, cols)

    def body(a_ref, b_ref, *rest):
        o_ref, acc = rest[len(after):]
        k = pl.program_id(1)

        @pl.when(k == 0)
        def _():
            acc[...] = jnp.zeros_like(acc)

        tok = pl.ds(pl.multiple_of(k * bk, bk), bk)
        a_blk = a_ref[...] if by_rows else a_ref[tok, :]
        b_blk = b_ref[tok, :] if by_rows else b_ref[...]
        acc[...] += lax.dot_general(a_blk, b_blk, (((0,), (0,)), ((), ())), preferred_element_type=F32)

        @pl.when(k == nk - 1)
        def _():
            o_ref[...] = acc[...].reshape(out_blk).astype(BF16)

    return pl.pallas_call(
        body, name=name, grid=(nblk, nk), in_specs=[a_spec, b_spec] + [ANY_SPEC] * len(after),
        out_specs=pl.BlockSpec(out_blk, lambda i, k: (i, 0, 0)),
        out_shape=jax.ShapeDtypeStruct((N_DEV,) + out_blk[1:], BF16),
        scratch_shapes=[pltpu.VMEM(acc_shape, F32)],
        compiler_params=_cparams(),
    )(a, b, *after)


def _adam_parts(parts, w, m, v, name):
    rows, cols = w.shape
    br = rows
    for cand in range(rows, 15, -16):
        if rows % cand == 0 and cand * cols * 4 <= ADAM_BLOCK_BYTES:
            br = cand
            break

    def body(p_ref, w_ref, m_ref, v_ref, g_out, dl_out, m_out, v_out):
        g = p_ref[0].astype(F32)
        for k in range(1, N_DEV):
            g = g + p_ref[k].astype(F32)
        g_out[...] = g
        dl, mn, vn = _adam_math(w_ref[...], g, m_ref[...], v_ref[...])
        dl_out[...] = dl
        m_out[...] = mn
        v_out[...] = vn

    wspec = pl.BlockSpec((br, cols), lambda i: (i, 0))
    return pl.pallas_call(
        body, name=name, grid=(rows // br,),
        in_specs=[pl.BlockSpec((N_DEV, br, cols), lambda i: (0, i, 0)), wspec, wspec, wspec],
        out_specs=[wspec] * 4, out_shape=[jax.ShapeDtypeStruct((rows, cols), F32)] * 4,
        compiler_params=_cparams(),
    )(parts, w, m, v)


def _adam_plain(g, w, m, v, name):
    def body(g_ref, w_ref, m_ref, v_ref, dl_out, m_out, v_out):
        dl, mn, vn = _adam_math(w_ref[...], g_ref[...], m_ref[...], v_ref[...])
        dl_out[...] = dl
        m_out[...] = mn
        v_out[...] = vn

    return pl.pallas_call(body, name=name, out_shape=[jax.ShapeDtypeStruct(w.shape, F32)] * 3,
                          compiler_params=_cparams())(g, w, m, v)


SMALL_PARAMS = ("b_ada", "g_mix", "conv_b", "dt_bias", "a_log", "d_skip", "g_ssd", "pool_scale", "g_mlp", "g_final")


def _small_adam(gathered, params):
    n_par = len(SMALL_PARAMS)
    nb = gathered[0].shape[1]

    def body(*refs):
        dmod_ref, acc_ref, conv_ref, vec_ref, hd_ref = refs[:5]
        par_refs = refs[5:5 + 3 * n_par]
        out_refs = refs[5 + 3 * n_par:5 + 7 * n_par]
        cw_out, acc_out = refs[5 + 7 * n_par:]

        def total(ref):
            t = ref[0]
            for k in range(1, N_DEV):
                t = t + ref[k]
            return t

        dm = total(dmod_ref)
        dmb = dm[0]
        for b in range(1, nb):
            dmb = dmb + dm[b]
        ac, cv, vc, hd = total(acc_ref), total(conv_ref), total(vec_ref), total(hd_ref)
        cw_out[...] = cv[0:4]
        acc_out[...] = ac
        grads = {
            "b_ada": jnp.concatenate([dmb[r:r + 1] for r in range(6)], axis=1), "g_mix": ac[3:4], "conv_b": cv[4:5],
            "dt_bias": hd[0:1, 0:N_HEADS], "a_log": hd[1:2, 0:N_HEADS], "d_skip": hd[2:3, 0:N_HEADS],
            "g_ssd": vc[0:1], "pool_scale": vc[1:2, 0:POOL_W], "g_mlp": ac[0:1], "g_final": ac[1:2],
        }
        for i, name in enumerate(SMALL_PARAMS):
            w_ref, m_ref, v_ref = par_refs[3 * i:3 * i + 3]
            g = grads[name]
            dl, mn, vn = _adam_math(w_ref[...], g, m_ref[...], v_ref[...])
            g_o, d_o, m_o, v_o = out_refs[4 * i:4 * i + 4]
            g_o[...] = g
            d_o[...] = dl
            m_o[...] = mn
            v_o[...] = vn

    flat = [a for name in SMALL_PARAMS for a in params[name]]
    out_shape = [jax.ShapeDtypeStruct(params[name][0].shape, F32) for name in SMALL_PARAMS for _ in range(4)]
    out_shape += [jax.ShapeDtypeStruct((4, CONV_CH), F32), jax.ShapeDtypeStruct((8, D), F32)]
    return pl.pallas_call(body, name="small_adam", out_shape=out_shape, compiler_params=_cparams())(*gathered, *flat)


def kernel(x, c, w_ada, b_ada, g_mix, w_in, conv_w, conv_b, dt_bias, a_log, d_skip, g_ssd, w_pool, pool_scale, w_out, g_mlp, w_up, w_down, g_final, loss_target, m_w_ada, m_b_ada, m_g_mix, m_w_in, m_conv_w, m_conv_b, m_dt_bias, m_a_log, m_d_skip, m_g_ssd, m_w_pool, m_pool_scale, m_w_out, m_g_mlp, m_w_up, m_w_down, m_g_final, v_w_ada, v_b_ada, v_g_mix, v_w_in, v_conv_w, v_conv_b, v_dt_bias, v_a_log, v_d_skip, v_g_ssd, v_w_pool, v_pool_scale, v_w_out, v_g_mlp, v_w_up, v_w_down, v_g_final):
    nb, seq, _ = x.shape
    T = nb * seq
    me = 4 * lax.axis_index("x") + 2 * lax.axis_index("y") + lax.axis_index("c")
    in_cols = w_in.shape[2]
    ada_cols = w_ada.shape[2]
    cw_cols = conv_w.shape[2]

    c_g, cw_g, win_g = _all_gather([c, conv_w[0], w_in[0].astype(BF16)], "ag_first")
    c_all = c_g.reshape(N_DEV * nb, D)
    cw_full = cw_g.transpose(1, 0, 2).reshape(4, CONV_CH)

    b_slice = lax.dynamic_slice(b_ada, (0, me * ada_cols), (1, ada_cols))
    mod_cols = _ada_fwd(c_all, w_ada[0], b_slice)
    (mod_g,) = _all_gather([mod_cols], "ag_mod")
    mod_all = mod_g.transpose(1, 0, 2).reshape(N_DEV * nb, 6, D)
    mod_mine = lax.dynamic_slice(mod_all, (me * nb, 0, 0), (nb, 6, D))
    mod = jnp.pad(mod_mine, ((0, 0), (0, 2), (0, 0)))

    x2 = x.reshape(T, D)
    tg2 = loss_target.reshape(T, D)
    heads = jnp.pad(jnp.concatenate([dt_bias, a_log, d_skip], axis=0), ((0, 5), (0, LANES - N_HEADS)))
    wpool_b = w_pool[0]
    u_b, pm, w_cat, wup_g = _mix_in(x2, mod, g_mix, win_g, seq, [w_up[0].astype(BF16)])
    ymix, hstates, cvs, wout_g, wdn_g = _mixer_fwd(
        pm, cw_full, conv_b, heads, g_ssd, wpool_b, pool_scale, nb, seq,
        [w_out[0].astype(BF16), w_down[0].astype(BF16)])
    da_b, dym, dh1, u2_b, f_b, dup_b, ddn_b, dmod_a, acc_a = _mlp_fused(
        x2, ymix, tg2, mod, g_mlp, g_final.reshape(1, D), wout_g.reshape(MIX_W, D), wup_g, wdn_g, seq)

    gout_p = _dw_blocks(ymix, da_b, "dw_out", True, per_step=4)
    ex_out = _exchange_start(gout_p, "gout_start")
    gup_p = _dw_blocks(u2_b, dup_b, "dw_up", False, after=[ex_out[4]])
    ex_up = _exchange_start(gup_p, "gup_start")
    gdn_p = _dw_blocks(f_b, ddn_b, "dw_down", True, after=[ex_up[4]])
    ex_dn = _exchange_start(gdn_p, "gdn_start")
    dpb, d_conv, d_heads, d_vec, d_wpool = _mixer_bwd(
        pm, cvs, dym, hstates, cw_full, conv_b, heads, g_ssd, wpool_b, pool_scale, nb, seq, after=[ex_dn[4]])
    gin_p = _dw_in(u_b, dpb, in_cols)
    ex_in = _exchange_start(gin_p, "gin_start")
    grad_x2, dmod, acc = _in_bwd(x2, dh1, dpb, mod, g_mix + ex_in[4][0:1, 0:1], w_cat, dmod_a, acc_a, seq)

    def landed(ex, after, name):
        own, land = _exchange_wait(ex[0], ex[1], ex[2], ex[3], after, name)
        mine = lax.dynamic_slice(own, (me, 0, 0), (1,) + own.shape[1:])
        return lax.dynamic_update_slice(land, mine, (me, 0, 0))

    gout_r, gup_r, gdn_r = landed(ex_out, dmod, "gout_wait"), landed(ex_up, dmod, "gup_wait"), landed(ex_dn, dmod, "gdn_wait")
    g_out, d_out, nm_out, nv_out = _adam_parts(gout_r, w_out[0], m_w_out[0], v_w_out[0], "adam_w_out")
    g_up, d_up, nm_up, nv_up = _adam_parts(gup_r, w_up[0], m_w_up[0], v_w_up[0], "adam_w_up")
    g_dn, d_dn, nm_dn, nv_dn = _adam_parts(gdn_r, w_down[0], m_w_down[0], v_w_down[0], "adam_w_down")

    dmod_g, acc_g, conv_g, vec_g, heads_g, wpool_parts = _all_gather(
        [dmod, acc, d_conv, d_vec, d_heads, d_wpool.reshape(4 * LANES, LANES)], "ag_small_bwd",
        after=[nm_out, nm_up, nm_dn])
    pool2 = (4 * LANES, LANES)
    wpool_outs = _adam_parts(wpool_parts, w_pool.reshape(pool2), m_w_pool.reshape(pool2), v_w_pool.reshape(pool2),
                             "adam_w_pool")
    small_params = {
        "b_ada": (b_ada, m_b_ada, v_b_ada), "g_mix": (g_mix, m_g_mix, v_g_mix), "conv_b": (conv_b, m_conv_b, v_conv_b),
        "dt_bias": (dt_bias, m_dt_bias, v_dt_bias), "a_log": (a_log, m_a_log, v_a_log),
        "d_skip": (d_skip, m_d_skip, v_d_skip), "g_ssd": (g_ssd, m_g_ssd, v_g_ssd),
        "pool_scale": (pool_scale, m_pool_scale, v_pool_scale), "g_mlp": (g_mlp, m_g_mlp, v_g_mlp),
        "g_final": tuple(a.reshape(1, D) for a in (g_final, m_g_final, v_g_final)),
    }
    small_res = _small_adam([dmod_g, acc_g, conv_g, vec_g, heads_g], small_params)
    g_cw_full, acc_sum = small_res[-2:]
    loss = acc_sum[2, 0]

    g_cw = lax.dynamic_slice(g_cw_full, (0, me * cw_cols), (4, cw_cols))
    d_cwp, nm_cwp, nv_cwp = _adam_plain(g_cw, conv_w[0], m_conv_w[0], v_conv_w[0], "adam_conv_w")

    dmod_all = dmod_g[:, :, 0:6].reshape(N_DEV * nb, 6 * D)
    dmod_slice = lax.dynamic_slice(dmod_all, (0, me * ada_cols), (N_DEV * nb, ada_cols))
    g_ada, d_ada, nm_ada, nv_ada = _ada_bwd_adam(c_all, dmod_slice, w_ada[0], m_w_ada[0], v_w_ada[0])

    ex_after = nm_ada[0:8, 0:LANES] + acc_sum[:, 0:LANES]
    gin_r = landed(ex_in, ex_after, "gin_wait")
    g_in, d_in, nm_in, nv_in = _adam_parts(gin_r, w_in[0], m_w_in[0], v_w_in[0], "adam_w_in")

    def small_outs(kind, wpool):
        res = {name: small_res[4 * i + kind] for i, name in enumerate(SMALL_PARAMS)}
        res["g_final"] = res["g_final"].reshape(D)
        res["w_pool"] = wpool.reshape(1, 4, LANES, LANES)
        return res

    def big_outs(ada, win, cwp, wout, wup, wdn):
        return {"w_ada": ada[None], "w_in": win.reshape(1, D, in_cols), "conv_w": cwp[None], "w_out": wout[None],
                "w_up": wup[None], "w_down": wdn[None]}

    order = ["w_ada", "b_ada", "g_mix", "w_in", "conv_w", "conv_b", "dt_bias", "a_log", "d_skip", "g_ssd", "w_pool",
             "pool_scale", "w_out", "g_mlp", "w_up", "w_down", "g_final"]
    groups = [
        {**small_outs(0, wpool_outs[0]), **big_outs(g_ada, g_in, g_cw, g_out, g_up, g_dn)},
        {**small_outs(1, wpool_outs[1]), **big_outs(d_ada, d_in, d_cwp, d_out, d_up, d_dn)},
        {**small_outs(2, wpool_outs[2]), **big_outs(nm_ada, nm_in, nm_cwp, nm_out, nm_up, nm_dn)},
        {**small_outs(3, wpool_outs[3]), **big_outs(nv_ada, nv_in, nv_cwp, nv_out, nv_up, nv_dn)},
    ]
    outs = [loss, grad_x2.reshape(nb, seq, D)]
    for grp in groups:
        outs += [grp[n] for n in order]
    return tuple(outs)
```

```python
import functools

import jax
import jax.numpy as jnp
from jax import lax
from jax.experimental import pallas as pl
from jax.experimental.pallas import tpu as pltpu

F32, BF16 = jnp.float32, jnp.bfloat16
MESH = pl.DeviceIdType.MESH
N_DEV = 8
D = 1024
LANES = 128
CHUNK = 128
POOL_W = 512
WINDOWS = (2, 4, 8, 16)
N_HEADS = 16
HEAD_DIM = 64
N_GROUPS = 2
GROUP_W = 512
N_STATE = 128
CONV_CH = 1536
OFF_Z, OFF_XBC, OFF_DT, IN_W = 512, 1536, 3072, 3088
PROJ_W = OFF_DT + LANES
MIX_W = 1536
D_FF = 4096
FF_BLK = 512
EPS = 1e-5
LR, B1, B2, AEPS, WD, STEP = 0.001, 0.9, 0.999, 1e-08, 0.01, 10
POOL_HALO = 16
CONV_HALO = 8
VMEM_LIMIT = 56 << 20
ADAM_BLOCK_BYTES = 1 << 20
DW_IN_WIN = 512


def _cparams(**kw):
    return pltpu.CompilerParams(vmem_limit_bytes=VMEM_LIMIT, **kw)


def _mm(a, b):
    return jnp.dot(a.astype(BF16), b.astype(BF16), preferred_element_type=F32)


def _mm_nt(a, b):
    return lax.dot_general(a.astype(BF16), b.astype(BF16), (((1,), (1,)), ((), ())), preferred_element_type=F32)


def _mm_tn(a, b):
    return lax.dot_general(a.astype(BF16), b.astype(BF16), (((0,), (0,)), ((), ())), preferred_element_type=F32)


def _split_bf16(v, terms):
    parts, rest = [], v
    for t in range(terms):
        p = rest.astype(BF16)
        parts.append(p)
        if t + 1 < terms:
            rest = rest - p.astype(F32)
    return parts


def _dot01(a, b, terms, split_lhs=True):
    if split_lhs:
        bb = b.astype(BF16)
        prods = [jnp.dot(p, bb, preferred_element_type=F32) for p in _split_bf16(a, terms)]
    else:
        ab = a.astype(BF16)
        prods = [jnp.dot(ab, p, preferred_element_type=F32) for p in _split_bf16(b, terms)]
    out = prods[0]
    for q in prods[1:]:
        out = out + q
    return out


def _sigmoid(v):
    return 1.0 / (1.0 + jnp.exp(-v))


def _expand_mat():
    r = lax.broadcasted_iota(jnp.int32, (LANES, D), 0)
    c = lax.broadcasted_iota(jnp.int32, (LANES, D), 1)
    return (r == c // HEAD_DIM).astype(F32)


def _reduce_mat():
    r = lax.broadcasted_iota(jnp.int32, (D, LANES), 0)
    c = lax.broadcasted_iota(jnp.int32, (D, LANES), 1)
    return (c == r // HEAD_DIM).astype(F32)


def _pos():
    return lax.axis_index("x"), lax.axis_index("y"), lax.axis_index("c")


class _Gather:
    def __init__(self, x_refs, o_refs, send, recv, loc):
        self.x_refs, self.o_refs, self.send, self.recv, self.loc = x_refs, o_refs, send, recv, loc
        self.n = len(x_refs)
        x, y, c = _pos()
        self.c = c
        self.me, self.sib = (x, y, c), (x, y, 1 - c)
        self.chips = [(1 - x, y), (x, 1 - y), (1 - x, 1 - y)]

    def _cp(self, a, k, block, to, src=None):
        dst = self.o_refs[a].at[4 * block[0] + 2 * block[1] + block[2]]
        return pltpu.make_async_remote_copy(
            src_ref=dst if src is None else src, dst_ref=dst,
            send_sem=self.send.at[a * 7 + k], recv_sem=self.recv.at[a * 7 + k],
            device_id=to, device_id_type=MESH)

    def _mine(self, a):
        me = self.me
        return pltpu.make_async_copy(self.x_refs[a], self.o_refs[a].at[4 * me[0] + 2 * me[1] + me[2]], self.loc.at[a])

    def _first(self, a):
        cps = [self._cp(a, 0, self.me, self.sib, src=self.x_refs[a])]
        return cps + [self._cp(a, 1 + j, self.me, (*chip, self.c), src=self.x_refs[a])
                      for j, chip in enumerate(self.chips)]

    def _passed(self, a, j):
        return self._cp(a, 4 + j, (*self.chips[j], self.c), self.sib)

    def start(self):
        for a in range(self.n):
            self._mine(a).start()
            for cp in self._first(a):
                cp.start()

    def forward(self):
        for j, chip in enumerate(self.chips):
            for a in range(self.n):
                self._cp(a, 1 + j, (*chip, self.c), self.me).wait_recv()
                self._passed(a, j).start()

    def finish(self):
        for a in range(self.n):
            self._cp(a, 0, self.sib, self.me).wait_recv()
            for j, chip in enumerate(self.chips):
                self._cp(a, 4 + j, (*chip, 1 - self.c), self.me).wait_recv()
        for a in range(self.n):
            for cp in self._first(a):
                cp.wait_send()
            for j in range(3):
                self._passed(a, j).wait_send()
            self._mine(a).wait()


class _Exchange:
    def __init__(self, x_refs, o_refs, send, recv, loc):
        self.x_refs, self.o_refs, self.send, self.recv, self.loc = x_refs, o_refs, send, recv, loc
        self.n = len(x_refs)
        x, y, c = _pos()
        self.me_i = 4 * x + 2 * y + c
        self.peers = []
        for k in range(1, N_DEV):
            px = 1 - x if (k >> 2) & 1 else x
            py = 1 - y if (k >> 1) & 1 else y
            pc = 1 - c if k & 1 else c
            self.peers.append(((px, py, pc), 4 * px + 2 * py + pc))

    def _mine(self, a):
        return pltpu.make_async_copy(self.x_refs[a].at[self.me_i], self.o_refs[a].at[self.me_i], self.loc.at[a])

    def _cp(self, a, k, landing):
        peer, peer_i = self.peers[k]
        return pltpu.make_async_remote_copy(
            src_ref=self.x_refs[a].at[peer_i], dst_ref=self.o_refs[a].at[landing],
            send_sem=self.send.at[a * 7 + k], recv_sem=self.recv.at[a * 7 + k],
            device_id=peer, device_id_type=MESH)

    def start(self):
        for a in range(self.n):
            self._mine(a).start()
            for k in range(N_DEV - 1):
                self._cp(a, k, self.me_i).start()

    def finish(self):
        for a in range(self.n):
            for k in range(N_DEV - 1):
                self._cp(a, k, self.peers[k][1]).wait_recv()
        for a in range(self.n):
            for k in range(N_DEV - 1):
                self._cp(a, k, self.me_i).wait_send()
            self._mine(a).wait()


def _comm_scratch(n):
    return [pltpu.SemaphoreType.DMA((7 * n,)), pltpu.SemaphoreType.DMA((7 * n,)), pltpu.SemaphoreType.DMA((n,))]


ANY_SPEC = pl.BlockSpec(memory_space=pl.ANY)


def _all_gather(xs, name, after=()):
    n, na = len(xs), len(after)

    def body(*refs):
        g = _Gather(refs[:n], refs[n + na:2 * n + na], *refs[2 * n + na:])
        g.start()
        g.forward()
        g.finish()

    return pl.pallas_call(
        body, name=name,
        out_shape=[jax.ShapeDtypeStruct((N_DEV,) + v.shape, v.dtype) for v in xs],
        in_specs=[ANY_SPEC] * (n + na), out_specs=[ANY_SPEC] * n, scratch_shapes=_comm_scratch(n),
    )(*xs, *after)


HBM_SPEC = pl.BlockSpec(memory_space=pltpu.HBM)
SEM_SPEC = pl.BlockSpec(memory_space=pltpu.SEMAPHORE)
VMEM_SPEC = pl.BlockSpec(memory_space=pltpu.VMEM)
SPLIT_EFFECT = pltpu.SideEffectType.DATAFLOW_SIDE_EFFECTING


def _in_hbm(v):
    return pltpu.with_memory_space_constraint(v, pltpu.HBM)


def _exchange_start(blocks, name):
    def body(x_ref, land_ref, send, recv, x_thru, land_thru, token):
        ex = _Exchange([x_ref], [land_ref], send, recv, None)
        for k in range(N_DEV - 1):
            ex._cp(0, k, ex.me_i).start()
        token[...] = jnp.zeros_like(token)

    hbm = pltpu.HBM(blocks.shape, blocks.dtype)
    return pl.pallas_call(
        body, name=name,
        out_shape=(pltpu.SemaphoreType.DMA((N_DEV - 1,)), pltpu.SemaphoreType.DMA((N_DEV - 1,)), hbm, hbm,
                   jax.ShapeDtypeStruct((8, LANES), F32)),
        in_specs=(HBM_SPEC, HBM_SPEC), out_specs=(SEM_SPEC, SEM_SPEC, HBM_SPEC, HBM_SPEC, VMEM_SPEC),
        input_output_aliases={0: 2, 1: 3},
        compiler_params=pltpu.CompilerParams(has_side_effects=SPLIT_EFFECT),
    )(_in_hbm(blocks), _in_hbm(lax.empty(blocks.shape, blocks.dtype)))


def _exchange_wait(send, recv, x_thru, land_thru, after, name):
    def body(x_ref, land_ref, send_ref, recv_ref, after_ref, x_dead, got_ref):
        ex = _Exchange([x_ref], [land_ref], send_ref, recv_ref, None)
        for k in range(N_DEV - 1):
            ex._cp(0, k, ex.me_i).wait_send()
            ex._cp(0, k, ex.peers[k][1]).wait_recv()

    hbm = pltpu.HBM(x_thru.shape, x_thru.dtype)
    return pl.pallas_call(
        body, name=name, out_shape=(hbm, hbm),
        in_specs=(HBM_SPEC, HBM_SPEC, SEM_SPEC, SEM_SPEC, ANY_SPEC), out_specs=(HBM_SPEC, HBM_SPEC),
        input_output_aliases={0: 0, 1: 1},
        compiler_params=pltpu.CompilerParams(has_side_effects=SPLIT_EFFECT),
    )(x_thru, land_thru, send, recv, after)


def _ada_fwd(c_all, w_ada, b_slice):
    def body(c_ref, w_ref, b_ref, o_ref):
        cv = c_ref[...]
        act = cv * _sigmoid(cv)
        o_ref[...] = _mm(act, w_ref[...]) + b_ref[...]

    nb, nc = c_all.shape[0], w_ada.shape[1]
    return pl.pallas_call(body, name="ada_fwd", out_shape=jax.ShapeDtypeStruct((nb, nc), F32),
                          compiler_params=_cparams())(c_all, w_ada, b_slice)


def _adam_math(w, g, m, v):
    m = B1 * m + (1.0 - B1) * g
    v = B2 * v + (1.0 - B2) * jnp.square(g)
    m_hat = m / (1.0 - B1 ** STEP)
    v_hat = v / (1.0 - B2 ** STEP)
    delta = -LR * (m_hat / (jnp.sqrt(v_hat) + AEPS) + WD * w)
    return delta, m, v


def _ada_bwd_adam(c_all, dmod_slice, w, m, v):
    rows, cols = w.shape
    br = 256

    def body(c_ref, d_ref, w_ref, m_ref, v_ref, g_out, dl_out, m_out, v_out):
        cv = c_ref[...]
        act = cv * _sigmoid(cv)
        g = _mm_tn(act, d_ref[...])
        g_out[...] = g
        dl, mn, vn = _adam_math(w_ref[...], g, m_ref[...], v_ref[...])
        dl_out[...] = dl
        m_out[...] = mn
        v_out[...] = vn

    nb = c_all.shape[0]
    wspec = pl.BlockSpec((br, cols), lambda i: (i, 0))
    return pl.pallas_call(
        body, name="ada_bwd_adam", grid=(rows // br,),
        in_specs=[pl.BlockSpec((nb, br), lambda i: (0, i)), pl.BlockSpec((nb, cols), lambda i: (0, 0)),
                  wspec, wspec, wspec],
        out_specs=[wspec] * 4, out_shape=[jax.ShapeDtypeStruct((rows, cols), F32)] * 4,
        compiler_params=_cparams(),
    )(c_all, dmod_slice, w, m, v)


def _mix_in(x2, mod, g_mix, win_g, seq, shards):
    T = x2.shape[0]
    tm = min(512, seq)
    tps = seq // tm
    in_cols = win_g.shape[2]
    ns = len(shards)
    steps = T // tm
    fwd_step = (3 * steps) // 4

    def body(*refs):
        x_ref, mod_ref, g_ref, wb_ref = refs[:4]
        sh_refs = refs[4:4 + ns]
        u_ref, pm_ref, wc_ref = refs[4 + ns:7 + ns]
        ga_refs = refs[7 + ns:7 + 2 * ns]
        w_ref, send, recv, loc = refs[7 + 2 * ns:]
        step = pl.program_id(0)
        gather = _Gather(sh_refs, ga_refs, send, recv, loc)

        @pl.when(step == 0)
        def _():
            gather.start()

        @pl.when(step == fwd_step)
        def _():
            gather.forward()

        @pl.when(step == steps - 1)
        def _():
            gather.finish()

        @pl.when(step == 0)
        def _():
            w_ref[:, OFF_DT:] = jnp.zeros((D, PROJ_W - OFF_DT), BF16)
            for j in range(N_DEV):
                w_ref[:, in_cols * j:in_cols * (j + 1)] = wb_ref[j]
            wc_ref[...] = w_ref[...]

        x = x_ref[...]
        r = lax.rsqrt(jnp.mean(x * x, axis=-1, keepdims=True) + EPS)
        md = mod_ref[0]
        u = (x * r * g_ref[...]) * (1.0 + md[1:2]) + md[0:1]
        ub = u.astype(BF16)
        u_ref[...] = ub
        pm_ref[...] = jnp.dot(ub, w_ref[...], preferred_element_type=F32)

    whole = pl.BlockSpec(memory_space=pltpu.VMEM)
    return pl.pallas_call(
        body, name="mix_in", grid=(T // tm,),
        in_specs=[pl.BlockSpec((tm, D), lambda i: (i, 0)), pl.BlockSpec((1, 8, D), lambda i: (i // tps, 0, 0)),
                  pl.BlockSpec((1, D), lambda i: (0, 0)), whole] + [ANY_SPEC] * ns,
        out_specs=[pl.BlockSpec((tm, D), lambda i: (i, 0)), pl.BlockSpec((tm, PROJ_W), lambda i: (i, 0)),
                   pl.BlockSpec((D, PROJ_W), lambda i: (0, 0))] + [ANY_SPEC] * ns,
        out_shape=[jax.ShapeDtypeStruct((T, D), BF16), jax.ShapeDtypeStruct((T, PROJ_W), F32),
                   jax.ShapeDtypeStruct((D, PROJ_W), BF16)]
        + [jax.ShapeDtypeStruct((N_DEV,) + v.shape, v.dtype) for v in shards],
        scratch_shapes=[pltpu.VMEM((D, PROJ_W), BF16)] + _comm_scratch(ns),
        compiler_params=_cparams(),
    )(x2, mod, g_mix, win_g, *shards)


def _chunk_forward(up, z, ux, dtin, halo_p, halo_x, hprev, cw, cb, hp, gssd, wpool, pscale, t0, y_scr, cv=None):
    L = CHUNK
    out = {}
    row = lax.broadcasted_iota(jnp.int32, (L, 1), 0)
    t = (t0 + row + 1).astype(F32)
    e = jnp.concatenate([halo_p, up], axis=0)
    s2 = e + pltpu.roll(e, 1, 0)
    s4 = s2 + pltpu.roll(s2, 2, 0)
    s8 = s4 + pltpu.roll(s4, 4, 0)
    s16 = s8 + pltpu.roll(s8, 8, 0)
    sums = (s2, s4, s8, s16)
    p, inv, yp = [], [], []
    for gi, w in enumerate(WINDOWS):
        sl = slice(gi * LANES, (gi + 1) * LANES)
        ic = 1.0 / jnp.minimum(t, float(w))
        pg = sums[gi][POOL_HALO:, sl] * ic - up[:, sl]
        p.append(pg)
        inv.append(ic)
        yp.append(_mm(pg, wpool[gi]))
    out["p"], out["inv"], out["yp"] = p, inv, yp
    out["y_pool"] = jnp.concatenate(yp, axis=1) * pscale
    if cv is None:
        ex = jnp.concatenate([halo_x, ux], axis=0)
        taps = [pltpu.roll(ex, 3, 0)[CONV_HALO:], pltpu.roll(ex, 2, 0)[CONV_HALO:], pltpu.roll(ex, 1, 0)[CONV_HALO:], ux]
        cv = cb + taps[0] * cw[0:1] + taps[1] * cw[1:2] + taps[2] * cw[2:3] + taps[3] * cw[3:4]
    sg = _sigmoid(cv)
    xbc = cv * sg
    out["cv"], out["sg"] = cv, sg
    X = xbc[:, :D]
    Bm = xbc[:, D:D + N_GROUPS * N_STATE]
    Cm = xbc[:, D + N_GROUPS * N_STATE:]
    pre = dtin + hp[0:1]
    dt = jnp.maximum(pre, 0.0) + jnp.log(1.0 + jnp.exp(-jnp.abs(pre)))
    a_row = -jnp.exp(hp[1:2])
    da = dt * a_row
    ri = lax.broadcasted_iota(jnp.int32, (L, L), 0)
    ci = lax.broadcasted_iota(jnp.int32, (L, L), 1)
    causal = ri >= ci
    cum = _dot01(causal.astype(F32), da, 3, split_lhs=False)
    cum_t = cum.T
    cum_last = cum[L - 1:L]
    eo = jnp.exp(cum)
    dec = jnp.exp(cum_last - cum)
    cd = jnp.exp(cum_last)
    exm = _expand_mat()
    rows8 = jnp.concatenate([cd, hp[2:3], jnp.zeros((6, LANES), F32)], axis=0)
    rep = _dot01(jnp.concatenate([dt, eo, dec, rows8], axis=0), exm, 2)
    dt_rep, eo_rep, dec_rep = rep[0:L], rep[L:2 * L], rep[2 * L:3 * L]
    cd_rep, dskip_rep = rep[3 * L:3 * L + 1], rep[3 * L + 1:3 * L + 2]
    xdt = X * dt_rep
    out.update(X=X, Bm=Bm, Cm=Cm, pre=pre, dt=dt, a_row=a_row, cum=cum, cum_t=cum_t, eo=eo, dec=dec, cd=cd,
               dt_rep=dt_rep, eo_rep=eo_rep, dec_rep=dec_rep, cd_rep=cd_rep, dskip_rep=dskip_rep, xdt=xdt,
               causal=causal, anti=(ri <= ci).astype(F32), exm=exm)
    G, lms, yoff, hnew, xdec = [], [], [], [], []
    for g in range(N_GROUPS):
        gs = slice(g * GROUP_W, (g + 1) * GROUP_W)
        Bg = Bm[:, g * N_STATE:(g + 1) * N_STATE]
        Cg = Cm[:, g * N_STATE:(g + 1) * N_STATE]
        Gg = _mm_nt(Cg, Bg)
        G.append(Gg)
        for hh in range(N_HEADS // N_GROUPS):
            h = g * (N_HEADS // N_GROUPS) + hh
            seg = cum[:, h:h + 1] - cum_t[h:h + 1, :]
            lm = jnp.where(causal, jnp.exp(jnp.minimum(seg, 0.0)), 0.0)
            lms.append(lm)
            hs = slice(h * HEAD_DIM, (h + 1) * HEAD_DIM)
            y_scr[:, hs] = _mm(Gg * lm, xdt[:, hs])
        xd = xdt[:, gs] * dec_rep[:, gs]
        xdec.append(xd)
        sgm = _mm_tn(Bg, xd)
        yoff.append(_mm(Cg, hprev[g]) * eo_rep[:, gs])
        hnew.append(hprev[g] * cd_rep[:, gs] + sgm)
    out.update(G=G, lms=lms, yoff=yoff, hnew=hnew, xdec=xdec)
    y = y_scr[...] + jnp.concatenate(yoff, axis=1) + dskip_rep * X
    sz = _sigmoid(z)
    silz = z * sz
    yz = y * silz
    rg, yn = [], []
    for g in range(N_GROUPS):
        gs = slice(g * GROUP_W, (g + 1) * GROUP_W)
        r = lax.rsqrt(jnp.mean(yz[:, gs] * yz[:, gs], axis=-1, keepdims=True) + EPS)
        rg.append(r)
        yn.append(yz[:, gs] * r)
    yn = jnp.concatenate(yn, axis=1)
    out.update(y=y, sz=sz, silz=silz, rg=rg, yn=yn)
    out["y_ssd"] = yn * gssd
    return out


def _mixer_fwd(pm, cw, cb, hp, gssd, wpool, pscale, nb, seq, shards):
    nc = seq // CHUNK
    ns = len(shards)
    steps = nb * nc
    fwd_step = (3 * steps) // 4

    def body(*refs):
        pm_ref, cw_ref, cb_ref, hp_ref, gs_ref, wp_ref, ps_ref = refs[:7]
        sh_refs = refs[7:7 + ns]
        ym_ref, hs_ref, cv_ref = refs[7 + ns:10 + ns]
        ga_refs = refs[10 + ns:10 + 2 * ns]
        halo_p, halo_x, state, y_scr, send, recv, loc = refs[10 + 2 * ns:]
        c = pl.program_id(1)
        step = pl.program_id(0) * nc + c
        gather = _Gather(sh_refs, ga_refs, send, recv, loc)

        @pl.when(step == 0)
        def _():
            gather.start()

        @pl.when(step == fwd_step)
        def _():
            gather.forward()

        @pl.when(c == 0)
        def _():
            halo_p[...] = jnp.zeros_like(halo_p)
            halo_x[...] = jnp.zeros_like(halo_x)
            state[...] = jnp.zeros_like(state)

        up = pm_ref[:, 0:POOL_W]
        z = pm_ref[:, OFF_Z:OFF_XBC]
        ux = pm_ref[:, OFF_XBC:OFF_DT]
        hprev = [state[0], state[1]]
        hs_ref[0, 0, 0] = hprev[0]
        hs_ref[0, 0, 1] = hprev[1]
        o = _chunk_forward(up, z, ux, pm_ref[:, OFF_DT:], halo_p[...], halo_x[...], hprev, cw_ref[...], cb_ref[...],
                           hp_ref[...], gs_ref[...], wp_ref[...], ps_ref[...], c * CHUNK, y_scr)
        ym_ref[:, 0:POOL_W] = o["y_pool"].astype(BF16)
        ym_ref[:, POOL_W:] = o["y_ssd"].astype(BF16)
        cv_ref[...] = o["cv"]
        state[0] = o["hnew"][0]
        state[1] = o["hnew"][1]
        halo_p[...] = up[CHUNK - POOL_HALO:]
        halo_x[...] = ux[CHUNK - CONV_HALO:]

        @pl.when(step == steps - 1)
        def _():
            gather.finish()

    def full(shape):
        return pl.BlockSpec(shape, lambda b, c: (0,) * len(shape))

    T = nb * seq
    return pl.pallas_call(
        body, name="mixer_fwd", grid=(nb, nc),
        in_specs=[pl.BlockSpec((CHUNK, PROJ_W), lambda b, c: (b * nc + c, 0)),
                  full((4, CONV_CH)), full((1, CONV_CH)), full((8, LANES)), full((1, D)),
                  full((4, LANES, LANES)), full((1, POOL_W))] + [ANY_SPEC] * ns,
        out_specs=[pl.BlockSpec((CHUNK, MIX_W), lambda b, c: (b * nc + c, 0)),
                   pl.BlockSpec((1, 1, N_GROUPS, N_STATE, GROUP_W), lambda b, c: (b, c, 0, 0, 0)),
                   pl.BlockSpec((CHUNK, CONV_CH), lambda b, c: (b * nc + c, 0))] + [ANY_SPEC] * ns,
        out_shape=[jax.ShapeDtypeStruct((T, MIX_W), BF16),
                   jax.ShapeDtypeStruct((nb, nc, N_GROUPS, N_STATE, GROUP_W), F32),
                   jax.ShapeDtypeStruct((T, CONV_CH), F32)]
        + [jax.ShapeDtypeStruct((N_DEV,) + v.shape, v.dtype) for v in shards],
        scratch_shapes=[pltpu.VMEM((POOL_HALO, POOL_W), F32), pltpu.VMEM((CONV_HALO, CONV_CH), F32),
                        pltpu.VMEM((N_GROUPS, N_STATE, GROUP_W), F32), pltpu.VMEM((CHUNK, D), F32)] + _comm_scratch(ns),
        compiler_params=_cparams(),
    )(pm, cw, cb, hp, gssd, wpool, pscale, *shards)


def _mixer_bwd(pm, cvs, dym, hstates, cw, cb, hp, gssd, wpool, pscale, nb, seq, after=()):
    nc = seq // CHUNK
    hpg = N_HEADS // N_GROUPS
    na = len(after)

    def body(*refs):
        (pm_ref, hpool_ref, cv_ref, dy_ref, hs_ref, cw_ref, cb_ref, hp_ref, gs_ref, wp_ref, ps_ref) = refs[:11]
        dpm_ref, dconv_ref, dhp_ref, dvec_ref, dwp_ref = refs[11 + na:16 + na]
        nxt_q, nxt_cv, rstate, y_scr, dx_scr = refs[16 + na:]
        b = pl.program_id(0)
        ci = pl.program_id(1)
        c = nc - 1 - ci

        @pl.when((b == 0) & (ci == 0))
        def _():
            for r in (dconv_ref, dhp_ref, dvec_ref, dwp_ref):
                r[...] = jnp.zeros_like(r)

        @pl.when(ci == 0)
        def _():
            nxt_q[...] = jnp.zeros_like(nxt_q)
            nxt_cv[...] = jnp.zeros_like(nxt_cv)
            rstate[...] = jnp.zeros_like(rstate)

        first = (c > 0).astype(F32)
        up = pm_ref[:, 0:POOL_W]
        z = pm_ref[:, OFF_Z:OFF_XBC]
        ux = pm_ref[:, OFF_XBC:OFF_DT]
        halo_p = hpool_ref[...] * first
        hprev = [hs_ref[0, 0, 0], hs_ref[0, 0, 1]]
        cw, cb, hp, gssd, wpool, pscale = cw_ref[...], cb_ref[...], hp_ref[...], gs_ref[...], wp_ref[...], ps_ref[...]
        o = _chunk_forward(up, z, ux, pm_ref[:, OFF_DT:], halo_p, None, hprev, cw, cb, hp, gssd, wpool, pscale,
                           c * CHUNK, y_scr, cv=cv_ref[...])
        L = CHUNK
        dy_pool = dy_ref[:, 0:POOL_W].astype(F32)
        dy_ssd = dy_ref[:, POOL_W:].astype(F32)

        dvec_ref[1:2, 0:POOL_W] += jnp.sum(dy_pool * jnp.concatenate(o["yp"], axis=1), axis=0, keepdims=True)
        dyp = dy_pool * pscale
        qs = []
        dps = []
        for gi in range(len(WINDOWS)):
            sl = slice(gi * LANES, (gi + 1) * LANES)
            dwp_ref[gi] += _mm_tn(o["p"][gi], dyp[:, sl])
            dpg = _mm_nt(dyp[:, sl], wpool[gi])
            dps.append(dpg)
            qs.append(dpg * o["inv"][gi])
        q = jnp.concatenate(qs, axis=1)
        e = jnp.concatenate([q, nxt_q[...]], axis=0)
        n = L + POOL_HALO
        s2 = e + pltpu.roll(e, n - 1, 0)
        s4 = s2 + pltpu.roll(s2, n - 2, 0)
        s8 = s4 + pltpu.roll(s4, n - 4, 0)
        s16 = s8 + pltpu.roll(s8, n - 8, 0)
        sums = (s2, s4, s8, s16)
        for gi in range(len(WINDOWS)):
            sl = slice(gi * LANES, (gi + 1) * LANES)
            dpm_ref[:, sl] = (sums[gi][:L, sl] - dps[gi]).astype(BF16)
        nxt_q[...] = q[:POOL_HALO]

        yn, y, silz, sz = o["yn"], o["y"], o["silz"], o["sz"]
        dvec_ref[0:1] += jnp.sum(dy_ssd * yn, axis=0, keepdims=True)
        dyn = dy_ssd * gssd
        dyz = []
        for g in range(N_GROUPS):
            gs = slice(g * GROUP_W, (g + 1) * GROUP_W)
            mean = jnp.mean(dyn[:, gs] * yn[:, gs], axis=-1, keepdims=True)
            dyz.append(o["rg"][g] * (dyn[:, gs] - yn[:, gs] * mean))
        dyz = jnp.concatenate(dyz, axis=1)
        dyv = dyz * silz
        dpm_ref[:, OFF_Z:OFF_XBC] = (dyz * y * (sz * (1.0 + z * (1.0 - sz)))).astype(BF16)

        X, Bm, Cm, xdt = o["X"], o["Bm"], o["Cm"], o["xdt"]
        exm = o["exm"]
        rdm = _reduce_mat()
        lane = lax.broadcasted_iota(jnp.int32, (1, LANES), 1)
        sub = lax.broadcasted_iota(jnp.int32, (LANES, 1), 0)
        dX = o["dskip_rep"] * dyv
        yoff_full = jnp.concatenate(o["yoff"], axis=1)
        rs = jnp.zeros((L, LANES), F32)
        cs_t = jnp.zeros((LANES, L), F32)
        dBs, dCs = [], []
        rh_sums = []
        ddec = []
        for g in range(N_GROUPS):
            gs = slice(g * GROUP_W, (g + 1) * GROUP_W)
            Bg = Bm[:, g * N_STATE:(g + 1) * N_STATE]
            Cg = Cm[:, g * N_STATE:(g + 1) * N_STATE]
            Gg = o["G"][g]
            R = rstate[g]
            dwm = dyv[:, gs] * o["eo_rep"][:, gs]
            dC = _mm_nt(dwm, hprev[g])
            dH = _mm_tn(Cg, dwm)
            dG = jnp.zeros((L, L), F32)
            for hh in range(hpg):
                h = g * hpg + hh
                hs = slice(h * HEAD_DIM, (h + 1) * HEAD_DIM)
                lm = o["lms"][h]
                m_h = Gg * lm
                dM = _mm_nt(dyv[:, hs], xdt[:, hs])
                dx_scr[:, hs] = _mm_tn(m_h, dyv[:, hs])
                qm = dM * m_h
                rs = rs + jnp.sum(qm, axis=1, keepdims=True) * (lane == h).astype(F32)
                cs_t = cs_t + (sub == h).astype(F32) * jnp.sum(qm, axis=0, keepdims=True)
                dG = dG + dM * lm
            dC = dC + _mm(dG, Bg)
            dB = _mm_tn(dG, Cg)
            zx = _mm(Bg, R)
            dxdt_state = zx * o["dec_rep"][:, gs]
            ddec.append(zx * xdt[:, gs])
            dB = dB + _mm_nt(o["xdec"][g], R)
            rh_sums.append(jnp.sum(R * hprev[g], axis=0, keepdims=True))
            rstate[g] = dH + o["cd_rep"][:, gs] * R
            dx_scr[:, gs] = dx_scr[:, gs] + dxdt_state
            dBs.append(dB)
            dCs.append(dC)
        dxdt = dx_scr[...]
        tail = jnp.concatenate([jnp.sum(dyv * X, axis=0, keepdims=True), jnp.concatenate(rh_sums, axis=1),
                                jnp.zeros((6, D), F32)], axis=0)
        red = _dot01(jnp.concatenate([dyv * yoff_full, jnp.concatenate(ddec, axis=1), dxdt * X, tail], axis=0), rdm, 2)
        d_dskip, dcd_row = red[3 * L:3 * L + 1], red[3 * L + 1:3 * L + 2]
        ddec_h = red[L:2 * L] * o["dec"]
        dcum_last = jnp.sum(ddec_h, axis=0, keepdims=True) + dcd_row * o["cd"]
        dcum = red[0:L] + rs - cs_t.T - ddec_h + (sub == L - 1).astype(F32) * dcum_last
        dda = _dot01(o["anti"], dcum, 3, split_lhs=False)
        ddt_v = dda * o["a_row"] + red[2 * L:3 * L]
        dX = dX + dxdt * o["dt_rep"]
        head_mask = (lane < N_HEADS).astype(F32)
        d_alog = jnp.sum(dda * o["dt"], axis=0, keepdims=True) * o["a_row"] * head_mask
        dpre = ddt_v * _sigmoid(o["pre"]) * head_mask
        dpm_ref[:, OFF_DT:] = dpre.astype(BF16)
        d_dtb = jnp.sum(dpre, axis=0, keepdims=True)
        dhp_ref[...] += jnp.concatenate([d_dtb, d_alog, d_dskip * head_mask, jnp.zeros((5, LANES), F32)], axis=0)

        dxbc = jnp.concatenate([dX] + dBs + dCs, axis=1)
        sg, cv = o["sg"], o["cv"]
        dcv = dxbc * (sg * (1.0 + cv * (1.0 - sg)))
        e2 = jnp.concatenate([dcv, nxt_cv[...]], axis=0)
        n2 = L + CONV_HALO
        ahead = [dcv, pltpu.roll(e2, n2 - 1, 0)[:L], pltpu.roll(e2, n2 - 2, 0)[:L], pltpu.roll(e2, n2 - 3, 0)[:L]]
        dconv_ref[0:5] += jnp.concatenate(
            [jnp.sum(ux * ahead[3 - k], axis=0, keepdims=True) for k in range(4)]
            + [jnp.sum(dcv, axis=0, keepdims=True)], axis=0)
        dux = ahead[0] * cw[3:4] + ahead[1] * cw[2:3] + ahead[2] * cw[1:2] + ahead[3] * cw[0:1]
        dpm_ref[:, OFF_XBC:OFF_DT] = dux.astype(BF16)
        nxt_cv[...] = dcv[:CONV_HALO]

    def full(shape):
        return pl.BlockSpec(shape, lambda b, c: (0,) * len(shape))

    def rowblk(b, c):
        return b * nc + (nc - 1 - c)

    hp_blocks = CHUNK // POOL_HALO
    T = nb * seq
    return pl.pallas_call(
        body, name="mixer_bwd", grid=(nb, nc),
        in_specs=[pl.BlockSpec((CHUNK, PROJ_W), lambda b, c: (rowblk(b, c), 0)),
                  pl.BlockSpec((POOL_HALO, POOL_W), lambda b, c: (jnp.maximum(rowblk(b, c) * hp_blocks - 1, 0), 0)),
                  pl.BlockSpec((CHUNK, CONV_CH), lambda b, c: (rowblk(b, c), 0)),
                  pl.BlockSpec((CHUNK, MIX_W), lambda b, c: (rowblk(b, c), 0)),
                  pl.BlockSpec((1, 1, N_GROUPS, N_STATE, GROUP_W), lambda b, c: (b, nc - 1 - c, 0, 0, 0)),
                  full((4, CONV_CH)), full((1, CONV_CH)), full((8, LANES)), full((1, D)),
                  full((4, LANES, LANES)), full((1, POOL_W))] + [ANY_SPEC] * na,
        out_specs=[pl.BlockSpec((CHUNK, PROJ_W), lambda b, c: (rowblk(b, c), 0)),
                   full((8, CONV_CH)), full((8, LANES)), full((8, D)), full((4, LANES, LANES))],
        out_shape=[jax.ShapeDtypeStruct((T, PROJ_W), BF16),
                   jax.ShapeDtypeStruct((8, CONV_CH), F32), jax.ShapeDtypeStruct((8, LANES), F32),
                   jax.ShapeDtypeStruct((8, D), F32), jax.ShapeDtypeStruct((4, LANES, LANES), F32)],
        scratch_shapes=[pltpu.VMEM((POOL_HALO, POOL_W), F32), pltpu.VMEM((CONV_HALO, CONV_CH), F32),
                        pltpu.VMEM((N_GROUPS, N_STATE, GROUP_W), F32), pltpu.VMEM((CHUNK, D), F32),
                        pltpu.VMEM((CHUNK, D), F32)],
        compiler_params=_cparams(),
    )(pm, pm, cvs, dym, hstates, cw, cb, hp, gssd, wpool, pscale, *after)


def _mlp_fused(x2, ymix, target, mod, g_mlp, g_final, w_out, w_up, w_down, seq):
    T = x2.shape[0]
    tm = min(256, seq)
    tps = seq // tm
    nblk = D_FF // FF_BLK

    def body(x_ref, ym_ref, tg_ref, mod_ref, gm_ref, gf_ref, wo_ref, wu_ref, wd_ref,
             da_ref, dym_ref, dh1_ref, u2_ref, f_ref, dup_ref, ddn_ref, dmod_ref, acc_ref, relu_scr):
        i = pl.program_id(0)

        @pl.when(i == 0)
        def _():
            acc_ref[...] = jnp.zeros_like(acc_ref)

        @pl.when(i % tps == 0)
        def _():
            dmod_ref[...] = jnp.zeros_like(dmod_ref)

        md = mod_ref[0]
        gate_m, shift_f, scale_f, gate_f = md[2:3], md[3:4], md[4:5], md[5:6]
        g_mlp, g_fin = gm_ref[...], gf_ref[...]
        a = jnp.dot(ym_ref[...], wo_ref[...], preferred_element_type=F32)
        h1 = x_ref[...] + gate_m * a
        r2 = lax.rsqrt(jnp.mean(h1 * h1, axis=-1, keepdims=True) + EPS)
        n2 = h1 * r2
        u2 = (n2 * g_mlp) * (1.0 + scale_f) + shift_f
        u2b = u2.astype(BF16)
        u2_ref[...] = u2b
        dn = jnp.zeros((tm, D), F32)
        for j in range(nblk):
            js = slice(j * FF_BLK, (j + 1) * FF_BLK)
            upj = jnp.maximum(jnp.dot(u2b, wu_ref[j], preferred_element_type=F32), 0.0)
            relu_scr[:, js] = upj
            fj = (upj * upj).astype(BF16)
            f_ref[:, js] = fj
            dn = dn + jnp.dot(fj, wd_ref[j], preferred_element_type=F32)
        h2 = h1 + gate_f * dn
        r3 = lax.rsqrt(jnp.mean(h2 * h2, axis=-1, keepdims=True) + EPS)
        n3 = h2 * r3
        err = n3 * g_fin - tg_ref[...]
        loss = 0.5 * jnp.sum(jnp.mean(err * err, axis=-1, keepdims=True), axis=0, keepdims=True)
        dout = err * (1.0 / D)
        d_gfin = jnp.sum(dout * n3, axis=0, keepdims=True)
        dn3 = dout * g_fin
        dh2 = r3 * (dn3 - n3 * jnp.mean(dn3 * n3, axis=-1, keepdims=True))
        d_gate_f = jnp.sum(dh2 * dn, axis=0, keepdims=True)
        ddn = (gate_f * dh2).astype(BF16)
        ddn_ref[...] = ddn
        du2 = jnp.zeros((tm, D), F32)
        for j in range(nblk):
            js = slice(j * FF_BLK, (j + 1) * FF_BLK)
            dfj = lax.dot_general(ddn, wd_ref[j], (((1,), (1,)), ((), ())), preferred_element_type=F32)
            dupj = (dfj * (2.0 * relu_scr[:, js])).astype(BF16)
            dup_ref[:, js] = dupj
            du2 = du2 + lax.dot_general(dupj, wu_ref[j], (((1,), (1,)), ((), ())), preferred_element_type=F32)
        d_scale_f = jnp.sum(du2 * (n2 * g_mlp), axis=0, keepdims=True)
        d_shift_f = jnp.sum(du2, axis=0, keepdims=True)
        d_gmlp = jnp.sum(du2 * (1.0 + scale_f) * n2, axis=0, keepdims=True)
        dn2 = du2 * (g_mlp * (1.0 + scale_f))
        dh1 = dh2 + r2 * (dn2 - n2 * jnp.mean(dn2 * n2, axis=-1, keepdims=True))
        dh1_ref[...] = dh1
        d_gate_m = jnp.sum(dh1 * a, axis=0, keepdims=True)
        da = (gate_m * dh1).astype(BF16)
        da_ref[...] = da
        dym_ref[...] = lax.dot_general(da, wo_ref[...], (((1,), (1,)), ((), ())),
                                       preferred_element_type=F32).astype(BF16)
        dmod_ref[0] += jnp.concatenate([jnp.zeros((2, D), F32), d_gate_m, d_shift_f, d_scale_f, d_gate_f,
                                        jnp.zeros((2, D), F32)], axis=0)
        acc_ref[...] += jnp.concatenate([d_gmlp, d_gfin, loss * jnp.ones((1, D), F32), jnp.zeros((5, D), F32)], axis=0)

    whole = pl.BlockSpec(memory_space=pltpu.VMEM)

    def tok(w):
        return pl.BlockSpec((tm, w), lambda i: (i, 0))

    def vec():
        return pl.BlockSpec((1, D), lambda i: (0, 0))

    nb = T // seq
    return pl.pallas_call(
        body, name="mlp_fused", grid=(T // tm,),
        in_specs=[tok(D), tok(MIX_W), tok(D), pl.BlockSpec((1, 8, D), lambda i: (i // tps, 0, 0)), vec(), vec(),
                  whole, whole, whole],
        out_specs=[tok(D), tok(MIX_W), tok(D), tok(D), tok(D_FF), tok(D_FF), tok(D),
                   pl.BlockSpec((1, 8, D), lambda i: (i // tps, 0, 0)), pl.BlockSpec((8, D), lambda i: (0, 0))],
        out_shape=[jax.ShapeDtypeStruct((T, D), BF16), jax.ShapeDtypeStruct((T, MIX_W), BF16),
                   jax.ShapeDtypeStruct((T, D), F32), jax.ShapeDtypeStruct((T, D), BF16),
                   jax.ShapeDtypeStruct((T, D_FF), BF16), jax.ShapeDtypeStruct((T, D_FF), BF16),
                   jax.ShapeDtypeStruct((T, D), BF16), jax.ShapeDtypeStruct((nb, 8, D), F32),
                   jax.ShapeDtypeStruct((8, D), F32)],
        scratch_shapes=[pltpu.VMEM((tm, D_FF), F32)],
        compiler_params=_cparams(),
    )(x2, ymix, target, mod, g_mlp, g_final, w_out, w_up, w_down)


def _in_bwd(x2, dh1, dpb, mod, g_mix, w_cat, dmod_a, acc_a, seq):
    T = x2.shape[0]
    tm = min(512, seq)
    tps = seq // tm
    steps = T // tm

    def body(x_ref, dh_ref, dpb_ref, mod_ref, g_ref, w_ref, dma_ref, acca_ref, dx_ref, dmod_ref, acc_ref):
        i = pl.program_id(0)

        @pl.when(i == 0)
        def _():
            acc_ref[...] = acca_ref[...]

        @pl.when(i % tps == 0)
        def _():
            dmod_ref[...] = dma_ref[...]

        du = lax.dot_general(dpb_ref[...], w_ref[...], (((1,), (1,)), ((), ())), preferred_element_type=F32)
        x = x_ref[...]
        md = mod_ref[0]
        g = g_ref[...]
        r = lax.rsqrt(jnp.mean(x * x, axis=-1, keepdims=True) + EPS)
        n1 = x * r
        d_scale = jnp.sum(du * (n1 * g), axis=0, keepdims=True)
        d_shift = jnp.sum(du, axis=0, keepdims=True)
        d_g = jnp.sum(du * (1.0 + md[1:2]) * n1, axis=0, keepdims=True)
        dn1 = du * (g * (1.0 + md[1:2]))
        dx_ref[...] = dh_ref[...] + r * (dn1 - n1 * jnp.mean(dn1 * n1, axis=-1, keepdims=True))
        dmod_ref[0] += jnp.concatenate([d_shift, d_scale, jnp.zeros((6, D), F32)], axis=0)
        acc_ref[...] += jnp.concatenate([jnp.zeros((3, D), F32), d_g, jnp.zeros((4, D), F32)], axis=0)

    whole = pl.BlockSpec(memory_space=pltpu.VMEM)
    nb = T // seq
    return pl.pallas_call(
        body, name="in_bwd", grid=(steps,),
        in_specs=[pl.BlockSpec((tm, D), lambda i: (i, 0)), pl.BlockSpec((tm, D), lambda i: (i, 0)),
                  pl.BlockSpec((tm, PROJ_W), lambda i: (i, 0)),
                  pl.BlockSpec((1, 8, D), lambda i: (i // tps, 0, 0)), pl.BlockSpec((1, D), lambda i: (0, 0)),
                  whole, pl.BlockSpec((1, 8, D), lambda i: (i // tps, 0, 0)), pl.BlockSpec((8, D), lambda i: (0, 0))],
        out_specs=[pl.BlockSpec((tm, D), lambda i: (i, 0)),
                   pl.BlockSpec((1, 8, D), lambda i: (i // tps, 0, 0)), pl.BlockSpec((8, D), lambda i: (0, 0))],
        out_shape=[jax.ShapeDtypeStruct((T, D), F32),
                   jax.ShapeDtypeStruct((nb, 8, D), F32), jax.ShapeDtypeStruct((8, D), F32)],
        compiler_params=_cparams(),
    )(x2, dh1, dpb, mod, g_mix, w_cat, dmod_a, acc_a)


def _dw_in(u_b, dpb, in_cols, shards):
    T = u_b.shape[0]
    bk = min(512, T)
    nk = T // bk
    ns = len(shards)
    fwd_step = (3 * nk) // 4
    starts = [(in_cols * j // LANES) * LANES for j in range(N_DEV)]
    assert all(s + DW_IN_WIN <= PROJ_W and in_cols * (j + 1) <= s + DW_IN_WIN for j, s in enumerate(starts))

    def body(*refs):
        u_ref, d_ref = refs[:2]
        sh_refs = refs[2:2 + ns]
        o_ref = refs[2 + ns]
        ga_refs = refs[3 + ns:3 + 2 * ns]
        acc, send, recv, loc = refs[3 + 2 * ns:]
        k = pl.program_id(0)
        gather = _Gather(sh_refs, ga_refs, send, recv, loc)

        @pl.when(k == 0)
        def _():
            gather.start()

        @pl.when(k == fwd_step)
        def _():
            gather.forward()

        @pl.when(k == nk - 1)
        def _():
            gather.finish()

        @pl.when(k == 0)
        def _():
            acc[...] = jnp.zeros_like(acc)

        ut = u_ref[...].T
        for j in range(N_DEV):
            acc[j] += jnp.dot(ut, d_ref[:, starts[j]:starts[j] + DW_IN_WIN], preferred_element_type=F32)

        @pl.when(k == nk - 1)
        def _():
            for j in range(N_DEV):
                off = in_cols * j - starts[j]
                o_ref[j] = acc[j][:, off:off + in_cols].astype(BF16)

    return pl.pallas_call(
        body, name="dw_in", grid=(nk,),
        in_specs=[pl.BlockSpec((bk, D), lambda k: (k, 0)), pl.BlockSpec((bk, PROJ_W), lambda k: (k, 0))]
        + [ANY_SPEC] * ns,
        out_specs=[pl.BlockSpec((N_DEV, D, in_cols), lambda k: (0, 0, 0))] + [ANY_SPEC] * ns,
        out_shape=[jax.ShapeDtypeStruct((N_DEV, D, in_cols), BF16)]
        + [jax.ShapeDtypeStruct((N_DEV,) + v.shape, v.dtype) for v in shards],
        scratch_shapes=[pltpu.VMEM((N_DEV, D, DW_IN_WIN), F32)] + _comm_scratch(ns),
        compiler_params=_cparams(),
    )(u_b, dpb, *shards)


def _dw_blocks(a, b, name, by_rows, per_step=1, after=()):
    T, M = a.shape
    N = b.shape[1]
    bk = min(2048, T)
    nk = T // bk
    whole = pl.BlockSpec(memory_space=pltpu.VMEM)
    if by_rows:
        rows = M // N_DEV
        am = rows * per_step
        nblk = N_DEV // per_step
        a_spec, b_spec = pl.BlockSpec((bk, am), lambda i, k: (k, i)), whole
        out_blk, acc_shape = (per_step, rows, N), (am, N)
    else:
        cols = N // N_DEV
        nblk = N_DEV
        a_spec, b_spec = whole, pl.BlockSpec((bk, cols), lambda i, k: (k, i))
        out_blk, acc_shape = (1, M, cols), (M, cols)

    def body(a_ref, b_ref, *rest):
        o_ref, acc = rest[len(after):]
        k = pl.program_id(1)

        @pl.when(k == 0)
        def _():
            acc[...] = jnp.zeros_like(acc)

        tok = pl.ds(pl.multiple_of(k * bk, bk), bk)
        a_blk = a_ref[...] if by_rows else a_ref[tok, :]
        b_blk = b_ref[tok, :] if by_rows else b_ref[...]
        acc[...] += lax.dot_general(a_blk, b_blk, (((0,), (0,)), ((), ())), preferred_element_type=F32)

        @pl.when(k == nk - 1)
        def _():
            o_ref[...] = acc[...].reshape(out_blk).astype(BF16)

    return pl.pallas_call(
        body, name=name, grid=(nblk, nk), in_specs=[a_spec, b_spec] + [ANY_SPEC] * len(after),
        out_specs=pl.BlockSpec(out_blk, lambda i, k: (i, 0, 0)),
        out_shape=jax.ShapeDtypeStruct((N_DEV,) + out_blk[1:], BF16),
        scratch_shapes=[pltpu.VMEM(acc_shape, F32)],
        compiler_params=_cparams(),
    )(a, b, *after)


def _adam_parts(parts, w, m, v, name):
    rows, cols = w.shape
    br = rows
    for cand in range(rows, 15, -16):
        if rows % cand == 0 and cand * cols * 4 <= ADAM_BLOCK_BYTES:
            br = cand
            break

    def body(p_ref, w_ref, m_ref, v_ref, g_out, dl_out, m_out, v_out):
        g = p_ref[0].astype(F32)
        for k in range(1, N_DEV):
            g = g + p_ref[k].astype(F32)
        g_out[...] = g
        dl, mn, vn = _adam_math(w_ref[...], g, m_ref[...], v_ref[...])
        dl_out[...] = dl
        m_out[...] = mn
        v_out[...] = vn

    wspec = pl.BlockSpec((br, cols), lambda i: (i, 0))
    return pl.pallas_call(
        body, name=name, grid=(rows // br,),
        in_specs=[pl.BlockSpec((N_DEV, br, cols), lambda i: (0, i, 0)), wspec, wspec, wspec],
        out_specs=[wspec] * 4, out_shape=[jax.ShapeDtypeStruct((rows, cols), F32)] * 4,
        compiler_params=_cparams(),
    )(parts, w, m, v)


def _adam_plain(g, w, m, v, name):
    def body(g_ref, w_ref, m_ref, v_ref, dl_out, m_out, v_out):
        dl, mn, vn = _adam_math(w_ref[...], g_ref[...], m_ref[...], v_ref[...])
        dl_out[...] = dl
        m_out[...] = mn
        v_out[...] = vn

    return pl.pallas_call(body, name=name, out_shape=[jax.ShapeDtypeStruct(w.shape, F32)] * 3,
                          compiler_params=_cparams())(g, w, m, v)


SMALL_PARAMS = ("b_ada", "g_mix", "conv_b", "dt_bias", "a_log", "d_skip", "g_ssd", "pool_scale", "g_mlp", "g_final")


def _small_adam(gathered, params):
    n_par = len(SMALL_PARAMS)
    nb = gathered[0].shape[1]

    def body(*refs):
        dmod_ref, acc_ref, conv_ref, vec_ref, hd_ref = refs[:5]
        par_refs = refs[5:5 + 3 * n_par]
        out_refs = refs[5 + 3 * n_par:5 + 7 * n_par]
        cw_out, acc_out = refs[5 + 7 * n_par:]

        def total(ref):
            t = ref[0]
            for k in range(1, N_DEV):
                t = t + ref[k]
            return t

        dm = total(dmod_ref)
        dmb = dm[0]
        for b in range(1, nb):
            dmb = dmb + dm[b]
        ac, cv, vc, hd = total(acc_ref), total(conv_ref), total(vec_ref), total(hd_ref)
        cw_out[...] = cv[0:4]
        acc_out[...] = ac
        grads = {
            "b_ada": jnp.concatenate([dmb[r:r + 1] for r in range(6)], axis=1), "g_mix": ac[3:4], "conv_b": cv[4:5],
            "dt_bias": hd[0:1, 0:N_HEADS], "a_log": hd[1:2, 0:N_HEADS], "d_skip": hd[2:3, 0:N_HEADS],
            "g_ssd": vc[0:1], "pool_scale": vc[1:2, 0:POOL_W], "g_mlp": ac[0:1], "g_final": ac[1:2],
        }
        for i, name in enumerate(SMALL_PARAMS):
            w_ref, m_ref, v_ref = par_refs[3 * i:3 * i + 3]
            g = grads[name]
            dl, mn, vn = _adam_math(w_ref[...], g, m_ref[...], v_ref[...])
            g_o, d_o, m_o, v_o = out_refs[4 * i:4 * i + 4]
            g_o[...] = g
            d_o[...] = dl
            m_o[...] = mn
            v_o[...] = vn

    flat = [a for name in SMALL_PARAMS for a in params[name]]
    out_shape = [jax.ShapeDtypeStruct(params[name][0].shape, F32) for name in SMALL_PARAMS for _ in range(4)]
    out_shape += [jax.ShapeDtypeStruct((4, CONV_CH), F32), jax.ShapeDtypeStruct((8, D), F32)]
    return pl.pallas_call(body, name="small_adam", out_shape=out_shape, compiler_params=_cparams())(*gathered, *flat)


def kernel(x, c, w_ada, b_ada, g_mix, w_in, conv_w, conv_b, dt_bias, a_log, d_skip, g_ssd, w_pool, pool_scale, w_out, g_mlp, w_up, w_down, g_final, loss_target, m_w_ada, m_b_ada, m_g_mix, m_w_in, m_conv_w, m_conv_b, m_dt_bias, m_a_log, m_d_skip, m_g_ssd, m_w_pool, m_pool_scale, m_w_out, m_g_mlp, m_w_up, m_w_down, m_g_final, v_w_ada, v_b_ada, v_g_mix, v_w_in, v_conv_w, v_conv_b, v_dt_bias, v_a_log, v_d_skip, v_g_ssd, v_w_pool, v_pool_scale, v_w_out, v_g_mlp, v_w_up, v_w_down, v_g_final):
    nb, seq, _ = x.shape
    T = nb * seq
    me = 4 * lax.axis_index("x") + 2 * lax.axis_index("y") + lax.axis_index("c")
    in_cols = w_in.shape[2]
    ada_cols = w_ada.shape[2]
    cw_cols = conv_w.shape[2]

    c_g, cw_g, win_g = _all_gather([c, conv_w[0], w_in[0].astype(BF16)], "ag_first")
    c_all = c_g.reshape(N_DEV * nb, D)
    cw_full = cw_g.transpose(1, 0, 2).reshape(4, CONV_CH)

    b_slice = lax.dynamic_slice(b_ada, (0, me * ada_cols), (1, ada_cols))
    mod_cols = _ada_fwd(c_all, w_ada[0], b_slice)
    (mod_g,) = _all_gather([mod_cols], "ag_mod")
    mod_all = mod_g.transpose(1, 0, 2).reshape(N_DEV * nb, 6, D)
    mod_mine = lax.dynamic_slice(mod_all, (me * nb, 0, 0), (nb, 6, D))
    mod = jnp.pad(mod_mine, ((0, 0), (0, 2), (0, 0)))

    x2 = x.reshape(T, D)
    tg2 = loss_target.reshape(T, D)
    heads = jnp.pad(jnp.concatenate([dt_bias, a_log, d_skip], axis=0), ((0, 5), (0, LANES - N_HEADS)))
    wpool_b = w_pool[0]
    u_b, pm, w_cat, wup_g = _mix_in(x2, mod, g_mix, win_g, seq, [w_up[0].astype(BF16)])
    ymix, hstates, cvs, wout_g, wdn_g = _mixer_fwd(
        pm, cw_full, conv_b, heads, g_ssd, wpool_b, pool_scale, nb, seq,
        [w_out[0].astype(BF16), w_down[0].astype(BF16)])
    da_b, dym, dh1, u2_b, f_b, dup_b, ddn_b, dmod_a, acc_a = _mlp_fused(
        x2, ymix, tg2, mod, g_mlp, g_final.reshape(1, D), wout_g.reshape(MIX_W, D), wup_g, wdn_g, seq)

    gout_p = _dw_blocks(ymix, da_b, "dw_out", True, per_step=4)
    ex_out = _exchange_start(gout_p, "gout_start")
    gup_p = _dw_blocks(u2_b, dup_b, "dw_up", False, after=[ex_out[4]])
    ex_up = _exchange_start(gup_p, "gup_start")
    gdn_p = _dw_blocks(f_b, ddn_b, "dw_down", True, after=[ex_up[4]])
    ex_dn = _exchange_start(gdn_p, "gdn_start")
    dpb, d_conv, d_heads, d_vec, d_wpool = _mixer_bwd(
        pm, cvs, dym, hstates, cw_full, conv_b, heads, g_ssd, wpool_b, pool_scale, nb, seq, after=[ex_dn[4]])
    gin_p, conv_g, vec_g, heads_g, wpool_parts = _dw_in(
        u_b, dpb, in_cols, [d_conv, d_vec, d_heads, d_wpool.reshape(4 * LANES, LANES)])
    ex_in = _exchange_start(gin_p, "gin_start")
    grad_x2, dmod, acc = _in_bwd(x2, dh1, dpb, mod, g_mix + ex_in[4][0:1, 0:1], w_cat, dmod_a, acc_a, seq)

    def landed(ex, after, name):
        own, land = _exchange_wait(ex[0], ex[1], ex[2], ex[3], after, name)
        mine = lax.dynamic_slice(own, (me, 0, 0), (1,) + own.shape[1:])
        return lax.dynamic_update_slice(land, mine, (me, 0, 0))

    gout_r, gup_r, gdn_r = landed(ex_out, dmod, "gout_wait"), landed(ex_up, dmod, "gup_wait"), landed(ex_dn, dmod, "gdn_wait")
    g_out, d_out, nm_out, nv_out = _adam_parts(gout_r, w_out[0], m_w_out[0], v_w_out[0], "adam_w_out")
    g_up, d_up, nm_up, nv_up = _adam_parts(gup_r, w_up[0], m_w_up[0], v_w_up[0], "adam_w_up")
    g_dn, d_dn, nm_dn, nv_dn = _adam_parts(gdn_r, w_down[0], m_w_down[0], v_w_down[0], "adam_w_down")

    dmod_g, acc_g = _all_gather([dmod, acc], "ag_small_bwd", after=[nm_out, nm_up, nm_dn])
    pool2 = (4 * LANES, LANES)
    wpool_outs = _adam_parts(wpool_parts, w_pool.reshape(pool2), m_w_pool.reshape(pool2), v_w_pool.reshape(pool2),
                             "adam_w_pool")
    small_params = {
        "b_ada": (b_ada, m_b_ada, v_b_ada), "g_mix": (g_mix, m_g_mix, v_g_mix), "conv_b": (conv_b, m_conv_b, v_conv_b),
        "dt_bias": (dt_bias, m_dt_bias, v_dt_bias), "a_log": (a_log, m_a_log, v_a_log),
        "d_skip": (d_skip, m_d_skip, v_d_skip), "g_ssd": (g_ssd, m_g_ssd, v_g_ssd),
        "pool_scale": (pool_scale, m_pool_scale, v_pool_scale), "g_mlp": (g_mlp, m_g_mlp, v_g_mlp),
        "g_final": tuple(a.reshape(1, D) for a in (g_final, m_g_final, v_g_final)),
    }
    small_res = _small_adam([dmod_g, acc_g, conv_g, vec_g, heads_g], small_params)
    g_cw_full, acc_sum = small_res[-2:]
    loss = acc_sum[2, 0]

    g_cw = lax.dynamic_slice(g_cw_full, (0, me * cw_cols), (4, cw_cols))
    d_cwp, nm_cwp, nv_cwp = _adam_plain(g_cw, conv_w[0], m_conv_w[0], v_conv_w[0], "adam_conv_w")

    dmod_all = dmod_g[:, :, 0:6].reshape(N_DEV * nb, 6 * D)
    dmod_slice = lax.dynamic_slice(dmod_all, (0, me * ada_cols), (N_DEV * nb, ada_cols))
    g_ada, d_ada, nm_ada, nv_ada = _ada_bwd_adam(c_all, dmod_slice, w_ada[0], m_w_ada[0], v_w_ada[0])

    ex_after = nm_ada[0:8, 0:LANES] + acc_sum[:, 0:LANES]
    gin_r = landed(ex_in, ex_after, "gin_wait")
    g_in, d_in, nm_in, nv_in = _adam_parts(gin_r, w_in[0], m_w_in[0], v_w_in[0], "adam_w_in")

    def small_outs(kind, wpool):
        res = {name: small_res[4 * i + kind] for i, name in enumerate(SMALL_PARAMS)}
        res["g_final"] = res["g_final"].reshape(D)
        res["w_pool"] = wpool.reshape(1, 4, LANES, LANES)
        return res

    def big_outs(ada, win, cwp, wout, wup, wdn):
        return {"w_ada": ada[None], "w_in": win.reshape(1, D, in_cols), "conv_w": cwp[None], "w_out": wout[None],
                "w_up": wup[None], "w_down": wdn[None]}

    order = ["w_ada", "b_ada", "g_mix", "w_in", "conv_w", "conv_b", "dt_bias", "a_log", "d_skip", "g_ssd", "w_pool",
             "pool_scale", "w_out", "g_mlp", "w_up", "w_down", "g_final"]
    groups = [
        {**small_outs(0, wpool_outs[0]), **big_outs(g_ada, g_in, g_cw, g_out, g_up, g_dn)},
        {**small_outs(1, wpool_outs[1]), **big_outs(d_ada, d_in, d_cwp, d_out, d_up, d_dn)},
        {**small_outs(2, wpool_outs[2]), **big_outs(nm_ada, nm_in, nm_cwp, nm_out, nm_up, nm_dn)},
        {**small_outs(3, wpool_outs[3]), **big_outs(nv_ada, nv_in, nv_cwp, nv_out, nv_up, nv_dn)},
    ]
    outs = [loss, grad_x2.reshape(nb, seq, D)]
    for grp in groups:
        outs += [grp[n] for n in order]
    return tuple(outs)
```

```python
import functools

import jax
import jax.numpy as jnp
from jax import lax
from jax.experimental import pallas as pl
from jax.experimental.pallas import tpu as pltpu

F32, BF16 = jnp.float32, jnp.bfloat16
MESH = pl.DeviceIdType.MESH
N_DEV = 8
D = 1024
LANES = 128
CHUNK = 128
POOL_W = 512
WINDOWS = (2, 4, 8, 16)
N_HEADS = 16
HEAD_DIM = 64
N_GROUPS = 2
GROUP_W = 512
N_STATE = 128
CONV_CH = 1536
OFF_Z, OFF_XBC, OFF_DT, IN_W = 512, 1536, 3072, 3088
PROJ_W = OFF_DT + LANES
MIX_W = 1536
D_FF = 4096
FF_BLK = 512
EPS = 1e-5
LR, B1, B2, AEPS, WD, STEP = 0.001, 0.9, 0.999, 1e-08, 0.01, 10
POOL_HALO = 16
CONV_HALO = 8
VMEM_LIMIT = 56 << 20
ADAM_BLOCK_BYTES = 1 << 20
DW_IN_WIN = 512


def _cparams(**kw):
    return pltpu.CompilerParams(vmem_limit_bytes=VMEM_LIMIT, **kw)


def _mm(a, b):
    return jnp.dot(a.astype(BF16), b.astype(BF16), preferred_element_type=F32)


def _mm_nt(a, b):
    return lax.dot_general(a.astype(BF16), b.astype(BF16), (((1,), (1,)), ((), ())), preferred_element_type=F32)


def _mm_tn(a, b):
    return lax.dot_general(a.astype(BF16), b.astype(BF16), (((0,), (0,)), ((), ())), preferred_element_type=F32)


def _split_bf16(v, terms):
    parts, rest = [], v
    for t in range(terms):
        p = rest.astype(BF16)
        parts.append(p)
        if t + 1 < terms:
            rest = rest - p.astype(F32)
    return parts


def _dot01(a, b, terms, split_lhs=True):
    if split_lhs:
        bb = b.astype(BF16)
        prods = [jnp.dot(p, bb, preferred_element_type=F32) for p in _split_bf16(a, terms)]
    else:
        ab = a.astype(BF16)
        prods = [jnp.dot(ab, p, preferred_element_type=F32) for p in _split_bf16(b, terms)]
    out = prods[0]
    for q in prods[1:]:
        out = out + q
    return out


def _sigmoid(v):
    return 1.0 / (1.0 + jnp.exp(-v))


def _expand_mat():
    r = lax.broadcasted_iota(jnp.int32, (LANES, D), 0)
    c = lax.broadcasted_iota(jnp.int32, (LANES, D), 1)
    return (r == c // HEAD_DIM).astype(F32)


def _reduce_mat():
    r = lax.broadcasted_iota(jnp.int32, (D, LANES), 0)
    c = lax.broadcasted_iota(jnp.int32, (D, LANES), 1)
    return (c == r // HEAD_DIM).astype(F32)


def _pos():
    return lax.axis_index("x"), lax.axis_index("y"), lax.axis_index("c")


GATHER_PIECES = 4
GATHER_PIECE_BYTES = 64 << 10


def _pieces(shape, dtype):
    rows = shape[0]
    size = jnp.dtype(dtype).itemsize
    for d in shape:
        size *= d
    whole_tiles = rows % (GATHER_PIECES * 16) == 0
    return GATHER_PIECES if whole_tiles and size // GATHER_PIECES >= GATHER_PIECE_BYTES else 1


class _Gather:
    def __init__(self, x_refs, o_refs, send, recv, loc):
        self.x_refs, self.o_refs, self.send, self.recv, self.loc = x_refs, o_refs, send, recv, loc
        self.n = len(x_refs)
        self.pieces = [_pieces(r.shape, r.dtype) for r in x_refs]
        self.base = [7 * sum(self.pieces[:a]) for a in range(self.n)]
        x, y, c = _pos()
        self.c = c
        self.me, self.sib = (x, y, c), (x, y, 1 - c)
        self.chips = [(1 - x, y), (x, 1 - y), (1 - x, 1 - y)]

    def _rows(self, a, p):
        rows = self.x_refs[a].shape[0] // self.pieces[a]
        return pl.ds(p * rows, rows)

    def _cp(self, a, p, k, block, to, own=False):
        dst = self.o_refs[a].at[4 * block[0] + 2 * block[1] + block[2], self._rows(a, p)]
        sem = self.base[a] + 7 * p + k
        return pltpu.make_async_remote_copy(
            src_ref=self.x_refs[a].at[self._rows(a, p)] if own else dst, dst_ref=dst,
            send_sem=self.send.at[sem], recv_sem=self.recv.at[sem], device_id=to, device_id_type=MESH)

    def _mine(self, a):
        me = self.me
        return pltpu.make_async_copy(self.x_refs[a], self.o_refs[a].at[4 * me[0] + 2 * me[1] + me[2]], self.loc.at[a])

    def _first(self, a, p):
        cps = [self._cp(a, p, 0, self.me, self.sib, own=True)]
        return cps + [self._cp(a, p, 1 + j, self.me, (*chip, self.c), own=True) for j, chip in enumerate(self.chips)]

    def _passed(self, a, p, j):
        return self._cp(a, p, 4 + j, (*self.chips[j], self.c), self.sib)

    def start(self):
        for a in range(self.n):
            self._mine(a).start()
        for p in range(max(self.pieces)):
            for a in range(self.n):
                if p < self.pieces[a]:
                    for cp in self._first(a, p):
                        cp.start()

    def forward(self, p):
        for j, chip in enumerate(self.chips):
            for a in range(self.n):
                if p < self.pieces[a]:
                    self._cp(a, p, 1 + j, (*chip, self.c), self.me).wait_recv()
                    self._passed(a, p, j).start()

    def finish(self):
        for a in range(self.n):
            for p in range(self.pieces[a]):
                self._cp(a, p, 0, self.sib, self.me).wait_recv()
                for j, chip in enumerate(self.chips):
                    self._cp(a, p, 4 + j, (*chip, 1 - self.c), self.me).wait_recv()
        for a in range(self.n):
            for p in range(self.pieces[a]):
                for cp in self._first(a, p):
                    cp.wait_send()
                for j in range(3):
                    self._passed(a, p, j).wait_send()
            self._mine(a).wait()

    def begin_hosted(self, step, steps):
        @pl.when(step == 0)
        def _():
            self.start()

        n_p = max(self.pieces)
        for p in range(n_p):
            @pl.when(step == min(((p + 1) * 7 * steps) // (8 * n_p), steps - 1))
            def _():
                self.forward(p)

    def end_hosted(self, step, steps):
        @pl.when(step == steps - 1)
        def _():
            self.finish()


class _Exchange:
    def __init__(self, x_refs, o_refs, send, recv, loc):
        self.x_refs, self.o_refs, self.send, self.recv, self.loc = x_refs, o_refs, send, recv, loc
        self.n = len(x_refs)
        x, y, c = _pos()
        self.me_i = 4 * x + 2 * y + c
        self.peers = []
        for k in range(1, N_DEV):
            px = 1 - x if (k >> 2) & 1 else x
            py = 1 - y if (k >> 1) & 1 else y
            pc = 1 - c if k & 1 else c
            self.peers.append(((px, py, pc), 4 * px + 2 * py + pc))

    def _mine(self, a):
        return pltpu.make_async_copy(self.x_refs[a].at[self.me_i], self.o_refs[a].at[self.me_i], self.loc.at[a])

    def _cp(self, a, k, landing):
        peer, peer_i = self.peers[k]
        return pltpu.make_async_remote_copy(
            src_ref=self.x_refs[a].at[peer_i], dst_ref=self.o_refs[a].at[landing],
            send_sem=self.send.at[a * 7 + k], recv_sem=self.recv.at[a * 7 + k],
            device_id=peer, device_id_type=MESH)

    def start(self):
        for a in range(self.n):
            self._mine(a).start()
            for k in range(N_DEV - 1):
                self._cp(a, k, self.me_i).start()

    def finish(self):
        for a in range(self.n):
            for k in range(N_DEV - 1):
                self._cp(a, k, self.peers[k][1]).wait_recv()
        for a in range(self.n):
            for k in range(N_DEV - 1):
                self._cp(a, k, self.me_i).wait_send()
            self._mine(a).wait()


def _gather_scratch(xs):
    n_sem = 7 * sum(_pieces(v.shape, v.dtype) for v in xs)
    return [pltpu.SemaphoreType.DMA((n_sem,)), pltpu.SemaphoreType.DMA((n_sem,)), pltpu.SemaphoreType.DMA((len(xs),))]


ANY_SPEC = pl.BlockSpec(memory_space=pl.ANY)


def _all_gather(xs, name, after=()):
    n, na = len(xs), len(after)

    def body(*refs):
        g = _Gather(refs[:n], refs[n + na:2 * n + na], *refs[2 * n + na:])
        g.start()
        for p in range(max(g.pieces)):
            g.forward(p)
        g.finish()

    return pl.pallas_call(
        body, name=name,
        out_shape=[jax.ShapeDtypeStruct((N_DEV,) + v.shape, v.dtype) for v in xs],
        in_specs=[ANY_SPEC] * (n + na), out_specs=[ANY_SPEC] * n, scratch_shapes=_gather_scratch(xs),
    )(*xs, *after)


HBM_SPEC = pl.BlockSpec(memory_space=pltpu.HBM)
SEM_SPEC = pl.BlockSpec(memory_space=pltpu.SEMAPHORE)
VMEM_SPEC = pl.BlockSpec(memory_space=pltpu.VMEM)
SPLIT_EFFECT = pltpu.SideEffectType.DATAFLOW_SIDE_EFFECTING


def _in_hbm(v):
    return pltpu.with_memory_space_constraint(v, pltpu.HBM)


def _exchange_start(blocks, name):
    def body(x_ref, land_ref, send, recv, x_thru, land_thru, token):
        ex = _Exchange([x_ref], [land_ref], send, recv, None)
        for k in range(N_DEV - 1):
            ex._cp(0, k, ex.me_i).start()
        token[...] = jnp.zeros_like(token)

    hbm = pltpu.HBM(blocks.shape, blocks.dtype)
    return pl.pallas_call(
        body, name=name,
        out_shape=(pltpu.SemaphoreType.DMA((N_DEV - 1,)), pltpu.SemaphoreType.DMA((N_DEV - 1,)), hbm, hbm,
                   jax.ShapeDtypeStruct((8, LANES), F32)),
        in_specs=(HBM_SPEC, HBM_SPEC), out_specs=(SEM_SPEC, SEM_SPEC, HBM_SPEC, HBM_SPEC, VMEM_SPEC),
        input_output_aliases={0: 2, 1: 3},
        compiler_params=pltpu.CompilerParams(has_side_effects=SPLIT_EFFECT),
    )(_in_hbm(blocks), _in_hbm(lax.empty(blocks.shape, blocks.dtype)))


def _exchange_wait(send, recv, x_thru, land_thru, after, name):
    def body(x_ref, land_ref, send_ref, recv_ref, after_ref, x_dead, got_ref):
        ex = _Exchange([x_ref], [land_ref], send_ref, recv_ref, None)
        for k in range(N_DEV - 1):
            ex._cp(0, k, ex.me_i).wait_send()
            ex._cp(0, k, ex.peers[k][1]).wait_recv()

    hbm = pltpu.HBM(x_thru.shape, x_thru.dtype)
    return pl.pallas_call(
        body, name=name, out_shape=(hbm, hbm),
        in_specs=(HBM_SPEC, HBM_SPEC, SEM_SPEC, SEM_SPEC, ANY_SPEC), out_specs=(HBM_SPEC, HBM_SPEC),
        input_output_aliases={0: 0, 1: 1},
        compiler_params=pltpu.CompilerParams(has_side_effects=SPLIT_EFFECT),
    )(x_thru, land_thru, send, recv, after)


def _ada_fwd(c_all, w_ada, b_slice):
    def body(c_ref, w_ref, b_ref, o_ref):
        cv = c_ref[...]
        act = cv * _sigmoid(cv)
        o_ref[...] = _mm(act, w_ref[...]) + b_ref[...]

    nb, nc = c_all.shape[0], w_ada.shape[1]
    return pl.pallas_call(body, name="ada_fwd", out_shape=jax.ShapeDtypeStruct((nb, nc), F32),
                          compiler_params=_cparams())(c_all, w_ada, b_slice)


def _adam_math(w, g, m, v):
    m = B1 * m + (1.0 - B1) * g
    v = B2 * v + (1.0 - B2) * jnp.square(g)
    m_hat = m / (1.0 - B1 ** STEP)
    v_hat = v / (1.0 - B2 ** STEP)
    delta = -LR * (m_hat / (jnp.sqrt(v_hat) + AEPS) + WD * w)
    return delta, m, v


def _ada_bwd_adam(c_all, dmod_slice, w, m, v):
    rows, cols = w.shape
    br = 256

    def body(c_ref, d_ref, w_ref, m_ref, v_ref, g_out, dl_out, m_out, v_out):
        cv = c_ref[...]
        act = cv * _sigmoid(cv)
        g = _mm_tn(act, d_ref[...])
        g_out[...] = g
        dl, mn, vn = _adam_math(w_ref[...], g, m_ref[...], v_ref[...])
        dl_out[...] = dl
        m_out[...] = mn
        v_out[...] = vn

    nb = c_all.shape[0]
    wspec = pl.BlockSpec((br, cols), lambda i: (i, 0))
    return pl.pallas_call(
        body, name="ada_bwd_adam", grid=(rows // br,),
        in_specs=[pl.BlockSpec((nb, br), lambda i: (0, i)), pl.BlockSpec((nb, cols), lambda i: (0, 0)),
                  wspec, wspec, wspec],
        out_specs=[wspec] * 4, out_shape=[jax.ShapeDtypeStruct((rows, cols), F32)] * 4,
        compiler_params=_cparams(),
    )(c_all, dmod_slice, w, m, v)


def _mix_in(x2, mod, g_mix, win_g, seq, shards):
    T = x2.shape[0]
    tm = min(512, seq)
    tps = seq // tm
    in_cols = win_g.shape[2]
    ns = len(shards)
    steps = T // tm

    def body(*refs):
        x_ref, mod_ref, g_ref, wb_ref = refs[:4]
        sh_refs = refs[4:4 + ns]
        u_ref, pm_ref, wc_ref = refs[4 + ns:7 + ns]
        ga_refs = refs[7 + ns:7 + 2 * ns]
        w_ref, send, recv, loc = refs[7 + 2 * ns:]
        step = pl.program_id(0)
        gather = _Gather(sh_refs, ga_refs, send, recv, loc)
        gather.begin_hosted(step, steps)

        @pl.when(step == 0)
        def _():
            w_ref[:, OFF_DT:] = jnp.zeros((D, PROJ_W - OFF_DT), BF16)
            for j in range(N_DEV):
                w_ref[:, in_cols * j:in_cols * (j + 1)] = wb_ref[j]
            wc_ref[...] = w_ref[...]

        x = x_ref[...]
        r = lax.rsqrt(jnp.mean(x * x, axis=-1, keepdims=True) + EPS)
        md = mod_ref[0]
        u = (x * r * g_ref[...]) * (1.0 + md[1:2]) + md[0:1]
        ub = u.astype(BF16)
        u_ref[...] = ub
        pm_ref[...] = jnp.dot(ub, w_ref[...], preferred_element_type=F32)
        gather.end_hosted(step, steps)

    whole = pl.BlockSpec(memory_space=pltpu.VMEM)
    return pl.pallas_call(
        body, name="mix_in", grid=(T // tm,),
        in_specs=[pl.BlockSpec((tm, D), lambda i: (i, 0)), pl.BlockSpec((1, 8, D), lambda i: (i // tps, 0, 0)),
                  pl.BlockSpec((1, D), lambda i: (0, 0)), whole] + [ANY_SPEC] * ns,
        out_specs=[pl.BlockSpec((tm, D), lambda i: (i, 0)), pl.BlockSpec((tm, PROJ_W), lambda i: (i, 0)),
                   pl.BlockSpec((D, PROJ_W), lambda i: (0, 0))] + [ANY_SPEC] * ns,
        out_shape=[jax.ShapeDtypeStruct((T, D), BF16), jax.ShapeDtypeStruct((T, PROJ_W), F32),
                   jax.ShapeDtypeStruct((D, PROJ_W), BF16)]
        + [jax.ShapeDtypeStruct((N_DEV,) + v.shape, v.dtype) for v in shards],
        scratch_shapes=[pltpu.VMEM((D, PROJ_W), BF16)] + _gather_scratch(shards),
        compiler_params=_cparams(),
    )(x2, mod, g_mix, win_g, *shards)


def _chunk_forward(up, z, ux, dtin, halo_p, halo_x, hprev, cw, cb, hp, gssd, wpool, pscale, t0, y_scr, cv=None):
    L = CHUNK
    out = {}
    row = lax.broadcasted_iota(jnp.int32, (L, 1), 0)
    t = (t0 + row + 1).astype(F32)
    e = jnp.concatenate([halo_p, up], axis=0)
    s2 = e + pltpu.roll(e, 1, 0)
    s4 = s2 + pltpu.roll(s2, 2, 0)
    s8 = s4 + pltpu.roll(s4, 4, 0)
    s16 = s8 + pltpu.roll(s8, 8, 0)
    sums = (s2, s4, s8, s16)
    p, inv, yp = [], [], []
    for gi, w in enumerate(WINDOWS):
        sl = slice(gi * LANES, (gi + 1) * LANES)
        ic = 1.0 / jnp.minimum(t, float(w))
        pg = sums[gi][POOL_HALO:, sl] * ic - up[:, sl]
        p.append(pg)
        inv.append(ic)
        yp.append(_mm(pg, wpool[gi]))
    out["p"], out["inv"], out["yp"] = p, inv, yp
    out["y_pool"] = jnp.concatenate(yp, axis=1) * pscale
    if cv is None:
        ex = jnp.concatenate([halo_x, ux], axis=0)
        taps = [pltpu.roll(ex, 3, 0)[CONV_HALO:], pltpu.roll(ex, 2, 0)[CONV_HALO:], pltpu.roll(ex, 1, 0)[CONV_HALO:], ux]
        cv = cb + taps[0] * cw[0:1] + taps[1] * cw[1:2] + taps[2] * cw[2:3] + taps[3] * cw[3:4]
    sg = _sigmoid(cv)
    xbc = cv * sg
    out["cv"], out["sg"] = cv, sg
    X = xbc[:, :D]
    Bm = xbc[:, D:D + N_GROUPS * N_STATE]
    Cm = xbc[:, D + N_GROUPS * N_STATE:]
    pre = dtin + hp[0:1]
    dt = jnp.maximum(pre, 0.0) + jnp.log(1.0 + jnp.exp(-jnp.abs(pre)))
    a_row = -jnp.exp(hp[1:2])
    da = dt * a_row
    ri = lax.broadcasted_iota(jnp.int32, (L, L), 0)
    ci = lax.broadcasted_iota(jnp.int32, (L, L), 1)
    causal = ri >= ci
    cum = _dot01(causal.astype(F32), da, 3, split_lhs=False)
    cum_t = cum.T
    cum_last = cum[L - 1:L]
    eo = jnp.exp(cum)
    dec = jnp.exp(cum_last - cum)
    cd = jnp.exp(cum_last)
    exm = _expand_mat()
    rows8 = jnp.concatenate([cd, hp[2:3], jnp.zeros((6, LANES), F32)], axis=0)
    rep = _dot01(jnp.concatenate([dt, eo, dec, rows8], axis=0), exm, 2)
    dt_rep, eo_rep, dec_rep = rep[0:L], rep[L:2 * L], rep[2 * L:3 * L]
    cd_rep, dskip_rep = rep[3 * L:3 * L + 1], rep[3 * L + 1:3 * L + 2]
    xdt = X * dt_rep
    out.update(X=X, Bm=Bm, Cm=Cm, pre=pre, dt=dt, a_row=a_row, cum=cum, cum_t=cum_t, eo=eo, dec=dec, cd=cd,
               dt_rep=dt_rep, eo_rep=eo_rep, dec_rep=dec_rep, cd_rep=cd_rep, dskip_rep=dskip_rep, xdt=xdt,
               causal=causal, anti=(ri <= ci).astype(F32), exm=exm)
    G, lms, yoff, hnew, xdec = [], [], [], [], []
    for g in range(N_GROUPS):
        gs = slice(g * GROUP_W, (g + 1) * GROUP_W)
        Bg = Bm[:, g * N_STATE:(g + 1) * N_STATE]
        Cg = Cm[:, g * N_STATE:(g + 1) * N_STATE]
        Gg = _mm_nt(Cg, Bg)
        G.append(Gg)
        for hh in range(N_HEADS // N_GROUPS):
            h = g * (N_HEADS // N_GROUPS) + hh
            seg = cum[:, h:h + 1] - cum_t[h:h + 1, :]
            lm = jnp.where(causal, jnp.exp(jnp.minimum(seg, 0.0)), 0.0)
            lms.append(lm)
            hs = slice(h * HEAD_DIM, (h + 1) * HEAD_DIM)
            y_scr[:, hs] = _mm(Gg * lm, xdt[:, hs])
        xd = xdt[:, gs] * dec_rep[:, gs]
        xdec.append(xd)
        sgm = _mm_tn(Bg, xd)
        yoff.append(_mm(Cg, hprev[g]) * eo_rep[:, gs])
        hnew.append(hprev[g] * cd_rep[:, gs] + sgm)
    out.update(G=G, lms=lms, yoff=yoff, hnew=hnew, xdec=xdec)
    y = y_scr[...] + jnp.concatenate(yoff, axis=1) + dskip_rep * X
    sz = _sigmoid(z)
    silz = z * sz
    yz = y * silz
    rg, yn = [], []
    for g in range(N_GROUPS):
        gs = slice(g * GROUP_W, (g + 1) * GROUP_W)
        r = lax.rsqrt(jnp.mean(yz[:, gs] * yz[:, gs], axis=-1, keepdims=True) + EPS)
        rg.append(r)
        yn.append(yz[:, gs] * r)
    yn = jnp.concatenate(yn, axis=1)
    out.update(y=y, sz=sz, silz=silz, rg=rg, yn=yn)
    out["y_ssd"] = yn * gssd
    return out


def _mixer_fwd(pm, cw, cb, hp, gssd, wpool, pscale, nb, seq, shards):
    nc = seq // CHUNK
    ns = len(shards)
    steps = nb * nc

    def body(*refs):
        pm_ref, cw_ref, cb_ref, hp_ref, gs_ref, wp_ref, ps_ref = refs[:7]
        sh_refs = refs[7:7 + ns]
        ym_ref, hs_ref, cv_ref = refs[7 + ns:10 + ns]
        ga_refs = refs[10 + ns:10 + 2 * ns]
        halo_p, halo_x, state, y_scr, send, recv, loc = refs[10 + 2 * ns:]
        c = pl.program_id(1)
        step = pl.program_id(0) * nc + c
        gather = _Gather(sh_refs, ga_refs, send, recv, loc)
        gather.begin_hosted(step, steps)

        @pl.when(c == 0)
        def _():
            halo_p[...] = jnp.zeros_like(halo_p)
            halo_x[...] = jnp.zeros_like(halo_x)
            state[...] = jnp.zeros_like(state)

        up = pm_ref[:, 0:POOL_W]
        z = pm_ref[:, OFF_Z:OFF_XBC]
        ux = pm_ref[:, OFF_XBC:OFF_DT]
        hprev = [state[0], state[1]]
        hs_ref[0, 0, 0] = hprev[0]
        hs_ref[0, 0, 1] = hprev[1]
        o = _chunk_forward(up, z, ux, pm_ref[:, OFF_DT:], halo_p[...], halo_x[...], hprev, cw_ref[...], cb_ref[...],
                           hp_ref[...], gs_ref[...], wp_ref[...], ps_ref[...], c * CHUNK, y_scr)
        ym_ref[:, 0:POOL_W] = o["y_pool"].astype(BF16)
        ym_ref[:, POOL_W:] = o["y_ssd"].astype(BF16)
        cv_ref[...] = o["cv"]
        state[0] = o["hnew"][0]
        state[1] = o["hnew"][1]
        halo_p[...] = up[CHUNK - POOL_HALO:]
        halo_x[...] = ux[CHUNK - CONV_HALO:]
        gather.end_hosted(step, steps)

    def full(shape):
        return pl.BlockSpec(shape, lambda b, c: (0,) * len(shape))

    T = nb * seq
    return pl.pallas_call(
        body, name="mixer_fwd", grid=(nb, nc),
        in_specs=[pl.BlockSpec((CHUNK, PROJ_W), lambda b, c: (b * nc + c, 0)),
                  full((4, CONV_CH)), full((1, CONV_CH)), full((8, LANES)), full((1, D)),
                  full((4, LANES, LANES)), full((1, POOL_W))] + [ANY_SPEC] * ns,
        out_specs=[pl.BlockSpec((CHUNK, MIX_W), lambda b, c: (b * nc + c, 0)),
                   pl.BlockSpec((1, 1, N_GROUPS, N_STATE, GROUP_W), lambda b, c: (b, c, 0, 0, 0)),
                   pl.BlockSpec((CHUNK, CONV_CH), lambda b, c: (b * nc + c, 0))] + [ANY_SPEC] * ns,
        out_shape=[jax.ShapeDtypeStruct((T, MIX_W), BF16),
                   jax.ShapeDtypeStruct((nb, nc, N_GROUPS, N_STATE, GROUP_W), F32),
                   jax.ShapeDtypeStruct((T, CONV_CH), F32)]
        + [jax.ShapeDtypeStruct((N_DEV,) + v.shape, v.dtype) for v in shards],
        scratch_shapes=[pltpu.VMEM((POOL_HALO, POOL_W), F32), pltpu.VMEM((CONV_HALO, CONV_CH), F32),
                        pltpu.VMEM((N_GROUPS, N_STATE, GROUP_W), F32), pltpu.VMEM((CHUNK, D), F32)] + _gather_scratch(shards),
        compiler_params=_cparams(),
    )(pm, cw, cb, hp, gssd, wpool, pscale, *shards)


def _mixer_bwd(pm, cvs, dym, hstates, cw, cb, hp, gssd, wpool, pscale, nb, seq, after=()):
    nc = seq // CHUNK
    hpg = N_HEADS // N_GROUPS
    na = len(after)

    def body(*refs):
        (pm_ref, hpool_ref, cv_ref, dy_ref, hs_ref, cw_ref, cb_ref, hp_ref, gs_ref, wp_ref, ps_ref) = refs[:11]
        dpm_ref, dconv_ref, dhp_ref, dvec_ref, dwp_ref = refs[11 + na:16 + na]
        nxt_q, nxt_cv, rstate, y_scr, dx_scr = refs[16 + na:]
        b = pl.program_id(0)
        ci = pl.program_id(1)
        c = nc - 1 - ci

        @pl.when((b == 0) & (ci == 0))
        def _():
            for r in (dconv_ref, dhp_ref, dvec_ref, dwp_ref):
                r[...] = jnp.zeros_like(r)

        @pl.when(ci == 0)
        def _():
            nxt_q[...] = jnp.zeros_like(nxt_q)
            nxt_cv[...] = jnp.zeros_like(nxt_cv)
            rstate[...] = jnp.zeros_like(rstate)

        first = (c > 0).astype(F32)
        up = pm_ref[:, 0:POOL_W]
        z = pm_ref[:, OFF_Z:OFF_XBC]
        ux = pm_ref[:, OFF_XBC:OFF_DT]
        halo_p = hpool_ref[...] * first
        hprev = [hs_ref[0, 0, 0], hs_ref[0, 0, 1]]
        cw, cb, hp, gssd, wpool, pscale = cw_ref[...], cb_ref[...], hp_ref[...], gs_ref[...], wp_ref[...], ps_ref[...]
        o = _chunk_forward(up, z, ux, pm_ref[:, OFF_DT:], halo_p, None, hprev, cw, cb, hp, gssd, wpool, pscale,
                           c * CHUNK, y_scr, cv=cv_ref[...])
        L = CHUNK
        dy_pool = dy_ref[:, 0:POOL_W].astype(F32)
        dy_ssd = dy_ref[:, POOL_W:].astype(F32)

        dvec_ref[1:2, 0:POOL_W] += jnp.sum(dy_pool * jnp.concatenate(o["yp"], axis=1), axis=0, keepdims=True)
        dyp = dy_pool * pscale
        qs = []
        dps = []
        for gi in range(len(WINDOWS)):
            sl = slice(gi * LANES, (gi + 1) * LANES)
            dwp_ref[gi] += _mm_tn(o["p"][gi], dyp[:, sl])
            dpg = _mm_nt(dyp[:, sl], wpool[gi])
            dps.append(dpg)
            qs.append(dpg * o["inv"][gi])
        q = jnp.concatenate(qs, axis=1)
        e = jnp.concatenate([q, nxt_q[...]], axis=0)
        n = L + POOL_HALO
        s2 = e + pltpu.roll(e, n - 1, 0)
        s4 = s2 + pltpu.roll(s2, n - 2, 0)
        s8 = s4 + pltpu.roll(s4, n - 4, 0)
        s16 = s8 + pltpu.roll(s8, n - 8, 0)
        sums = (s2, s4, s8, s16)
        for gi in range(len(WINDOWS)):
            sl = slice(gi * LANES, (gi + 1) * LANES)
            dpm_ref[:, sl] = (sums[gi][:L, sl] - dps[gi]).astype(BF16)
        nxt_q[...] = q[:POOL_HALO]

        yn, y, silz, sz = o["yn"], o["y"], o["silz"], o["sz"]
        dvec_ref[0:1] += jnp.sum(dy_ssd * yn, axis=0, keepdims=True)
        dyn = dy_ssd * gssd
        dyz = []
        for g in range(N_GROUPS):
            gs = slice(g * GROUP_W, (g + 1) * GROUP_W)
            mean = jnp.mean(dyn[:, gs] * yn[:, gs], axis=-1, keepdims=True)
            dyz.append(o["rg"][g] * (dyn[:, gs] - yn[:, gs] * mean))
        dyz = jnp.concatenate(dyz, axis=1)
        dyv = dyz * silz
        dpm_ref[:, OFF_Z:OFF_XBC] = (dyz * y * (sz * (1.0 + z * (1.0 - sz)))).astype(BF16)

        X, Bm, Cm, xdt = o["X"], o["Bm"], o["Cm"], o["xdt"]
        exm = o["exm"]
        rdm = _reduce_mat()
        lane = lax.broadcasted_iota(jnp.int32, (1, LANES), 1)
        sub = lax.broadcasted_iota(jnp.int32, (LANES, 1), 0)
        dX = o["dskip_rep"] * dyv
        yoff_full = jnp.concatenate(o["yoff"], axis=1)
        rs = jnp.zeros((L, LANES), F32)
        cs_t = jnp.zeros((LANES, L), F32)
        dBs, dCs = [], []
        rh_sums = []
        ddec = []
        for g in range(N_GROUPS):
            gs = slice(g * GROUP_W, (g + 1) * GROUP_W)
            Bg = Bm[:, g * N_STATE:(g + 1) * N_STATE]
            Cg = Cm[:, g * N_STATE:(g + 1) * N_STATE]
            Gg = o["G"][g]
            R = rstate[g]
            dwm = dyv[:, gs] * o["eo_rep"][:, gs]
            dC = _mm_nt(dwm, hprev[g])
            dH = _mm_tn(Cg, dwm)
            dG = jnp.zeros((L, L), F32)
            for hh in range(hpg):
                h = g * hpg + hh
                hs = slice(h * HEAD_DIM, (h + 1) * HEAD_DIM)
                lm = o["lms"][h]
                m_h = Gg * lm
                dM = _mm_nt(dyv[:, hs], xdt[:, hs])
                dx_scr[:, hs] = _mm_tn(m_h, dyv[:, hs])
                qm = dM * m_h
                rs = rs + jnp.sum(qm, axis=1, keepdims=True) * (lane == h).astype(F32)
                cs_t = cs_t + (sub == h).astype(F32) * jnp.sum(qm, axis=0, keepdims=True)
                dG = dG + dM * lm
            dC = dC + _mm(dG, Bg)
            dB = _mm_tn(dG, Cg)
            zx = _mm(Bg, R)
            dxdt_state = zx * o["dec_rep"][:, gs]
            ddec.append(zx * xdt[:, gs])
            dB = dB + _mm_nt(o["xdec"][g], R)
            rh_sums.append(jnp.sum(R * hprev[g], axis=0, keepdims=True))
            rstate[g] = dH + o["cd_rep"][:, gs] * R
            dx_scr[:, gs] = dx_scr[:, gs] + dxdt_state
            dBs.append(dB)
            dCs.append(dC)
        dxdt = dx_scr[...]
        tail = jnp.concatenate([jnp.sum(dyv * X, axis=0, keepdims=True), jnp.concatenate(rh_sums, axis=1),
                                jnp.zeros((6, D), F32)], axis=0)
        red = _dot01(jnp.concatenate([dyv * yoff_full, jnp.concatenate(ddec, axis=1), dxdt * X, tail], axis=0), rdm, 2)
        d_dskip, dcd_row = red[3 * L:3 * L + 1], red[3 * L + 1:3 * L + 2]
        ddec_h = red[L:2 * L] * o["dec"]
        dcum_last = jnp.sum(ddec_h, axis=0, keepdims=True) + dcd_row * o["cd"]
        dcum = red[0:L] + rs - cs_t.T - ddec_h + (sub == L - 1).astype(F32) * dcum_last
        dda = _dot01(o["anti"], dcum, 3, split_lhs=False)
        ddt_v = dda * o["a_row"] + red[2 * L:3 * L]
        dX = dX + dxdt * o["dt_rep"]
        head_mask = (lane < N_HEADS).astype(F32)
        d_alog = jnp.sum(dda * o["dt"], axis=0, keepdims=True) * o["a_row"] * head_mask
        dpre = ddt_v * _sigmoid(o["pre"]) * head_mask
        dpm_ref[:, OFF_DT:] = dpre.astype(BF16)
        d_dtb = jnp.sum(dpre, axis=0, keepdims=True)
        dhp_ref[...] += jnp.concatenate([d_dtb, d_alog, d_dskip * head_mask, jnp.zeros((5, LANES), F32)], axis=0)

        dxbc = jnp.concatenate([dX] + dBs + dCs, axis=1)
        sg, cv = o["sg"], o["cv"]
        dcv = dxbc * (sg * (1.0 + cv * (1.0 - sg)))
        e2 = jnp.concatenate([dcv, nxt_cv[...]], axis=0)
        n2 = L + CONV_HALO
        ahead = [dcv, pltpu.roll(e2, n2 - 1, 0)[:L], pltpu.roll(e2, n2 - 2, 0)[:L], pltpu.roll(e2, n2 - 3, 0)[:L]]
        dconv_ref[0:5] += jnp.concatenate(
            [jnp.sum(ux * ahead[3 - k], axis=0, keepdims=True) for k in range(4)]
            + [jnp.sum(dcv, axis=0, keepdims=True)], axis=0)
        dux = ahead[0] * cw[3:4] + ahead[1] * cw[2:3] + ahead[2] * cw[1:2] + ahead[3] * cw[0:1]
        dpm_ref[:, OFF_XBC:OFF_DT] = dux.astype(BF16)
        nxt_cv[...] = dcv[:CONV_HALO]

    def full(shape):
        return pl.BlockSpec(shape, lambda b, c: (0,) * len(shape))

    def rowblk(b, c):
        return b * nc + (nc - 1 - c)

    hp_blocks = CHUNK // POOL_HALO
    T = nb * seq
    return pl.pallas_call(
        body, name="mixer_bwd", grid=(nb, nc),
        in_specs=[pl.BlockSpec((CHUNK, PROJ_W), lambda b, c: (rowblk(b, c), 0)),
                  pl.BlockSpec((POOL_HALO, POOL_W), lambda b, c: (jnp.maximum(rowblk(b, c) * hp_blocks - 1, 0), 0)),
                  pl.BlockSpec((CHUNK, CONV_CH), lambda b, c: (rowblk(b, c), 0)),
                  pl.BlockSpec((CHUNK, MIX_W), lambda b, c: (rowblk(b, c), 0)),
                  pl.BlockSpec((1, 1, N_GROUPS, N_STATE, GROUP_W), lambda b, c: (b, nc - 1 - c, 0, 0, 0)),
                  full((4, CONV_CH)), full((1, CONV_CH)), full((8, LANES)), full((1, D)),
                  full((4, LANES, LANES)), full((1, POOL_W))] + [ANY_SPEC] * na,
        out_specs=[pl.BlockSpec((CHUNK, PROJ_W), lambda b, c: (rowblk(b, c), 0)),
                   full((8, CONV_CH)), full((8, LANES)), full((8, D)), full((4, LANES, LANES))],
        out_shape=[jax.ShapeDtypeStruct((T, PROJ_W), BF16),
                   jax.ShapeDtypeStruct((8, CONV_CH), F32), jax.ShapeDtypeStruct((8, LANES), F32),
                   jax.ShapeDtypeStruct((8, D), F32), jax.ShapeDtypeStruct((4, LANES, LANES), F32)],
        scratch_shapes=[pltpu.VMEM((POOL_HALO, POOL_W), F32), pltpu.VMEM((CONV_HALO, CONV_CH), F32),
                        pltpu.VMEM((N_GROUPS, N_STATE, GROUP_W), F32), pltpu.VMEM((CHUNK, D), F32),
                        pltpu.VMEM((CHUNK, D), F32)],
        compiler_params=_cparams(),
    )(pm, pm, cvs, dym, hstates, cw, cb, hp, gssd, wpool, pscale, *after)


def _mlp_fused(x2, ymix, target, mod, g_mlp, g_final, w_out, w_up, w_down, seq):
    T = x2.shape[0]
    tm = min(256, seq)
    tps = seq // tm
    nblk = D_FF // FF_BLK

    def body(x_ref, ym_ref, tg_ref, mod_ref, gm_ref, gf_ref, wo_ref, wu_ref, wd_ref,
             da_ref, dym_ref, dh1_ref, u2_ref, f_ref, dup_ref, ddn_ref, dmod_ref, acc_ref, relu_scr):
        i = pl.program_id(0)

        @pl.when(i == 0)
        def _():
            acc_ref[...] = jnp.zeros_like(acc_ref)

        @pl.when(i % tps == 0)
        def _():
            dmod_ref[...] = jnp.zeros_like(dmod_ref)

        md = mod_ref[0]
        gate_m, shift_f, scale_f, gate_f = md[2:3], md[3:4], md[4:5], md[5:6]
        g_mlp, g_fin = gm_ref[...], gf_ref[...]
        a = jnp.dot(ym_ref[...], wo_ref[...], preferred_element_type=F32)
        h1 = x_ref[...] + gate_m * a
        r2 = lax.rsqrt(jnp.mean(h1 * h1, axis=-1, keepdims=True) + EPS)
        n2 = h1 * r2
        u2 = (n2 * g_mlp) * (1.0 + scale_f) + shift_f
        u2b = u2.astype(BF16)
        u2_ref[...] = u2b
        dn = jnp.zeros((tm, D), F32)
        for j in range(nblk):
            js = slice(j * FF_BLK, (j + 1) * FF_BLK)
            upj = jnp.maximum(jnp.dot(u2b, wu_ref[j], preferred_element_type=F32), 0.0)
            relu_scr[:, js] = upj
            fj = (upj * upj).astype(BF16)
            f_ref[:, js] = fj
            dn = dn + jnp.dot(fj, wd_ref[j], preferred_element_type=F32)
        h2 = h1 + gate_f * dn
        r3 = lax.rsqrt(jnp.mean(h2 * h2, axis=-1, keepdims=True) + EPS)
        n3 = h2 * r3
        err = n3 * g_fin - tg_ref[...]
        loss = 0.5 * jnp.sum(jnp.mean(err * err, axis=-1, keepdims=True), axis=0, keepdims=True)
        dout = err * (1.0 / D)
        d_gfin = jnp.sum(dout * n3, axis=0, keepdims=True)
        dn3 = dout * g_fin
        dh2 = r3 * (dn3 - n3 * jnp.mean(dn3 * n3, axis=-1, keepdims=True))
        d_gate_f = jnp.sum(dh2 * dn, axis=0, keepdims=True)
        ddn = (gate_f * dh2).astype(BF16)
        ddn_ref[...] = ddn
        du2 = jnp.zeros((tm, D), F32)
        for j in range(nblk):
            js = slice(j * FF_BLK, (j + 1) * FF_BLK)
            dfj = lax.dot_general(ddn, wd_ref[j], (((1,), (1,)), ((), ())), preferred_element_type=F32)
            dupj = (dfj * (2.0 * relu_scr[:, js])).astype(BF16)
            dup_ref[:, js] = dupj
            du2 = du2 + lax.dot_general(dupj, wu_ref[j], (((1,), (1,)), ((), ())), preferred_element_type=F32)
        d_scale_f = jnp.sum(du2 * (n2 * g_mlp), axis=0, keepdims=True)
        d_shift_f = jnp.sum(du2, axis=0, keepdims=True)
        d_gmlp = jnp.sum(du2 * (1.0 + scale_f) * n2, axis=0, keepdims=True)
        dn2 = du2 * (g_mlp * (1.0 + scale_f))
        dh1 = dh2 + r2 * (dn2 - n2 * jnp.mean(dn2 * n2, axis=-1, keepdims=True))
        dh1_ref[...] = dh1
        d_gate_m = jnp.sum(dh1 * a, axis=0, keepdims=True)
        da = (gate_m * dh1).astype(BF16)
        da_ref[...] = da
        dym_ref[...] = lax.dot_general(da, wo_ref[...], (((1,), (1,)), ((), ())),
                                       preferred_element_type=F32).astype(BF16)
        dmod_ref[0] += jnp.concatenate([jnp.zeros((2, D), F32), d_gate_m, d_shift_f, d_scale_f, d_gate_f,
                                        jnp.zeros((2, D), F32)], axis=0)
        acc_ref[...] += jnp.concatenate([d_gmlp, d_gfin, loss * jnp.ones((1, D), F32), jnp.zeros((5, D), F32)], axis=0)

    whole = pl.BlockSpec(memory_space=pltpu.VMEM)

    def tok(w):
        return pl.BlockSpec((tm, w), lambda i: (i, 0))

    def vec():
        return pl.BlockSpec((1, D), lambda i: (0, 0))

    nb = T // seq
    return pl.pallas_call(
        body, name="mlp_fused", grid=(T // tm,),
        in_specs=[tok(D), tok(MIX_W), tok(D), pl.BlockSpec((1, 8, D), lambda i: (i // tps, 0, 0)), vec(), vec(),
                  whole, whole, whole],
        out_specs=[tok(D), tok(MIX_W), tok(D), tok(D), tok(D_FF), tok(D_FF), tok(D),
                   pl.BlockSpec((1, 8, D), lambda i: (i // tps, 0, 0)), pl.BlockSpec((8, D), lambda i: (0, 0))],
        out_shape=[jax.ShapeDtypeStruct((T, D), BF16), jax.ShapeDtypeStruct((T, MIX_W), BF16),
                   jax.ShapeDtypeStruct((T, D), F32), jax.ShapeDtypeStruct((T, D), BF16),
                   jax.ShapeDtypeStruct((T, D_FF), BF16), jax.ShapeDtypeStruct((T, D_FF), BF16),
                   jax.ShapeDtypeStruct((T, D), BF16), jax.ShapeDtypeStruct((nb, 8, D), F32),
                   jax.ShapeDtypeStruct((8, D), F32)],
        scratch_shapes=[pltpu.VMEM((tm, D_FF), F32)],
        compiler_params=_cparams(),
    )(x2, ymix, target, mod, g_mlp, g_final, w_out, w_up, w_down)


def _in_bwd(x2, dh1, dpb, mod, g_mix, w_cat, dmod_a, acc_a, seq):
    T = x2.shape[0]
    tm = min(512, seq)
    tps = seq // tm
    steps = T // tm

    def body(x_ref, dh_ref, dpb_ref, mod_ref, g_ref, w_ref, dma_ref, acca_ref, dx_ref, dmod_ref, acc_ref):
        i = pl.program_id(0)

        @pl.when(i == 0)
        def _():
            acc_ref[...] = acca_ref[...]

        @pl.when(i % tps == 0)
        def _():
            dmod_ref[...] = dma_ref[...]

        du = lax.dot_general(dpb_ref[...], w_ref[...], (((1,), (1,)), ((), ())), preferred_element_type=F32)
        x = x_ref[...]
        md = mod_ref[0]
        g = g_ref[...]
        r = lax.rsqrt(jnp.mean(x * x, axis=-1, keepdims=True) + EPS)
        n1 = x * r
        d_scale = jnp.sum(du * (n1 * g), axis=0, keepdims=True)
        d_shift = jnp.sum(du, axis=0, keepdims=True)
        d_g = jnp.sum(du * (1.0 + md[1:2]) * n1, axis=0, keepdims=True)
        dn1 = du * (g * (1.0 + md[1:2]))
        dx_ref[...] = dh_ref[...] + r * (dn1 - n1 * jnp.mean(dn1 * n1, axis=-1, keepdims=True))
        dmod_ref[0] += jnp.concatenate([d_shift, d_scale, jnp.zeros((6, D), F32)], axis=0)
        acc_ref[...] += jnp.concatenate([jnp.zeros((3, D), F32), d_g, jnp.zeros((4, D), F32)], axis=0)

    whole = pl.BlockSpec(memory_space=pltpu.VMEM)
    nb = T // seq
    return pl.pallas_call(
        body, name="in_bwd", grid=(steps,),
        in_specs=[pl.BlockSpec((tm, D), lambda i: (i, 0)), pl.BlockSpec((tm, D), lambda i: (i, 0)),
                  pl.BlockSpec((tm, PROJ_W), lambda i: (i, 0)),
                  pl.BlockSpec((1, 8, D), lambda i: (i // tps, 0, 0)), pl.BlockSpec((1, D), lambda i: (0, 0)),
                  whole, pl.BlockSpec((1, 8, D), lambda i: (i // tps, 0, 0)), pl.BlockSpec((8, D), lambda i: (0, 0))],
        out_specs=[pl.BlockSpec((tm, D), lambda i: (i, 0)),
                   pl.BlockSpec((1, 8, D), lambda i: (i // tps, 0, 0)), pl.BlockSpec((8, D), lambda i: (0, 0))],
        out_shape=[jax.ShapeDtypeStruct((T, D), F32),
                   jax.ShapeDtypeStruct((nb, 8, D), F32), jax.ShapeDtypeStruct((8, D), F32)],
        compiler_params=_cparams(),
    )(x2, dh1, dpb, mod, g_mix, w_cat, dmod_a, acc_a)


def _dw_in(u_b, dpb, in_cols, shards):
    T = u_b.shape[0]
    bk = min(512, T)
    nk = T // bk
    ns = len(shards)
    starts = [(in_cols * j // LANES) * LANES for j in range(N_DEV)]
    assert all(s + DW_IN_WIN <= PROJ_W and in_cols * (j + 1) <= s + DW_IN_WIN for j, s in enumerate(starts))

    def body(*refs):
        u_ref, d_ref = refs[:2]
        sh_refs = refs[2:2 + ns]
        o_ref = refs[2 + ns]
        ga_refs = refs[3 + ns:3 + 2 * ns]
        acc, send, recv, loc = refs[3 + 2 * ns:]
        k = pl.program_id(0)
        gather = _Gather(sh_refs, ga_refs, send, recv, loc)
        gather.begin_hosted(k, nk)

        @pl.when(k == 0)
        def _():
            acc[...] = jnp.zeros_like(acc)

        ut = u_ref[...].T
        for j in range(N_DEV):
            acc[j] += jnp.dot(ut, d_ref[:, starts[j]:starts[j] + DW_IN_WIN], preferred_element_type=F32)

        @pl.when(k == nk - 1)
        def _():
            for j in range(N_DEV):
                off = in_cols * j - starts[j]
                o_ref[j] = acc[j][:, off:off + in_cols].astype(BF16)

        gather.end_hosted(k, nk)

    return pl.pallas_call(
        body, name="dw_in", grid=(nk,),
        in_specs=[pl.BlockSpec((bk, D), lambda k: (k, 0)), pl.BlockSpec((bk, PROJ_W), lambda k: (k, 0))]
        + [ANY_SPEC] * ns,
        out_specs=[pl.BlockSpec((N_DEV, D, in_cols), lambda k: (0, 0, 0))] + [ANY_SPEC] * ns,
        out_shape=[jax.ShapeDtypeStruct((N_DEV, D, in_cols), BF16)]
        + [jax.ShapeDtypeStruct((N_DEV,) + v.shape, v.dtype) for v in shards],
        scratch_shapes=[pltpu.VMEM((N_DEV, D, DW_IN_WIN), F32)] + _gather_scratch(shards),
        compiler_params=_cparams(),
    )(u_b, dpb, *shards)


def _dw_blocks(a, b, name, by_rows, per_step=1, after=()):
    T, M = a.shape
    N = b.shape[1]
    bk = min(2048, T)
    nk = T // bk
    whole = pl.BlockSpec(memory_space=pltpu.VMEM)
    if by_rows:
        rows = M // N_DEV
        am = rows * per_step
        nblk = N_DEV // per_step
        a_spec, b_spec = pl.BlockSpec((bk, am), lambda i, k: (k, i)), whole
        out_blk, acc_shape = (per_step, rows, N), (am, N)
    else:
        cols = N // N_DEV
        nblk = N_DEV
        a_spec, b_spec = whole, pl.BlockSpec((bk, cols), lambda i, k: (k, i))
        out_blk, acc_shape = (1, M, cols), (M, cols)

    def body(a_ref, b_ref, *rest):
        o_ref, acc = rest[len(after):]
        k = pl.program_id(1)

        @pl.when(k == 0)
        def _():
            acc[...] = jnp.zeros_like(acc)

        tok = pl.ds(pl.multiple_of(k * bk, bk), bk)
        a_blk = a_ref[...] if by_rows else a_ref[tok, :]
        b_blk = b_ref[tok, :] if by_rows else b_ref[...]
        acc[...] += lax.dot_general(a_blk, b_blk, (((0,), (0,)), ((), ())), preferred_element_type=F32)

        @pl.when(k == nk - 1)
        def _():
            o_ref[...] = acc[...].reshape(out_blk).astype(BF16)

    return pl.pallas_call(
        body, name=name, grid=(nblk, nk), in_specs=[a_spec, b_spec] + [ANY_SPEC] * len(after),
        out_specs=pl.BlockSpec(out_blk, lambda i, k: (i, 0, 0)),
        out_shape=jax.ShapeDtypeStruct((N_DEV,) + out_blk[1:], BF16),
        scratch_shapes=[pltpu.VMEM(acc_shape, F32)],
        compiler_params=_cparams(),
    )(a, b, *after)


def _adam_parts(parts, w, m, v, name):
    rows, cols = w.shape
    br = rows
    for cand in range(rows, 15, -16):
        if rows % cand == 0 and cand * cols * 4 <= ADAM_BLOCK_BYTES:
            br = cand
            break

    def body(p_ref, w_ref, m_ref, v_ref, g_out, dl_out, m_out, v_out):
        g = p_ref[0].astype(F32)
        for k in range(1, N_DEV):
            g = g + p_ref[k].astype(F32)
        g_out[...] = g
        dl, mn, vn = _adam_math(w_ref[...], g, m_ref[...], v_ref[...])
        dl_out[...] = dl
        m_out[...] = mn
        v_out[...] = vn

    wspec = pl.BlockSpec((br, cols), lambda i: (i, 0))
    return pl.pallas_call(
        body, name=name, grid=(rows // br,),
        in_specs=[pl.BlockSpec((N_DEV, br, cols), lambda i: (0, i, 0)), wspec, wspec, wspec],
        out_specs=[wspec] * 4, out_shape=[jax.ShapeDtypeStruct((rows, cols), F32)] * 4,
        compiler_params=_cparams(),
    )(parts, w, m, v)


def _adam_plain(g, w, m, v, name):
    def body(g_ref, w_ref, m_ref, v_ref, dl_out, m_out, v_out):
        dl, mn, vn = _adam_math(w_ref[...], g_ref[...], m_ref[...], v_ref[...])
        dl_out[...] = dl
        m_out[...] = mn
        v_out[...] = vn

    return pl.pallas_call(body, name=name, out_shape=[jax.ShapeDtypeStruct(w.shape, F32)] * 3,
                          compiler_params=_cparams())(g, w, m, v)


SMALL_PARAMS = ("b_ada", "g_mix", "conv_b", "dt_bias", "a_log", "d_skip", "g_ssd", "pool_scale", "g_mlp", "g_final")


def _small_adam(gathered, params):
    n_par = len(SMALL_PARAMS)
    nb = gathered[0].shape[1]

    def body(*refs):
        dmod_ref, acc_ref, conv_ref, vec_ref, hd_ref = refs[:5]
        par_refs = refs[5:5 + 3 * n_par]
        out_refs = refs[5 + 3 * n_par:5 + 7 * n_par]
        cw_out, acc_out = refs[5 + 7 * n_par:]

        def total(ref):
            t = ref[0]
            for k in range(1, N_DEV):
                t = t + ref[k]
            return t

        dm = total(dmod_ref)
        dmb = dm[0]
        for b in range(1, nb):
            dmb = dmb + dm[b]
        ac, cv, vc, hd = total(acc_ref), total(conv_ref), total(vec_ref), total(hd_ref)
        cw_out[...] = cv[0:4]
        acc_out[...] = ac
        grads = {
            "b_ada": jnp.concatenate([dmb[r:r + 1] for r in range(6)], axis=1), "g_mix": ac[3:4], "conv_b": cv[4:5],
            "dt_bias": hd[0:1, 0:N_HEADS], "a_log": hd[1:2, 0:N_HEADS], "d_skip": hd[2:3, 0:N_HEADS],
            "g_ssd": vc[0:1], "pool_scale": vc[1:2, 0:POOL_W], "g_mlp": ac[0:1], "g_final": ac[1:2],
        }
        for i, name in enumerate(SMALL_PARAMS):
            w_ref, m_ref, v_ref = par_refs[3 * i:3 * i + 3]
            g = grads[name]
            dl, mn, vn = _adam_math(w_ref[...], g, m_ref[...], v_ref[...])
            g_o, d_o, m_o, v_o = out_refs[4 * i:4 * i + 4]
            g_o[...] = g
            d_o[...] = dl
            m_o[...] = mn
            v_o[...] = vn

    flat = [a for name in SMALL_PARAMS for a in params[name]]
    out_shape = [jax.ShapeDtypeStruct(params[name][0].shape, F32) for name in SMALL_PARAMS for _ in range(4)]
    out_shape += [jax.ShapeDtypeStruct((4, CONV_CH), F32), jax.ShapeDtypeStruct((8, D), F32)]
    return pl.pallas_call(body, name="small_adam", out_shape=out_shape, compiler_params=_cparams())(*gathered, *flat)


def kernel(x, c, w_ada, b_ada, g_mix, w_in, conv_w, conv_b, dt_bias, a_log, d_skip, g_ssd, w_pool, pool_scale, w_out, g_mlp, w_up, w_down, g_final, loss_target, m_w_ada, m_b_ada, m_g_mix, m_w_in, m_conv_w, m_conv_b, m_dt_bias, m_a_log, m_d_skip, m_g_ssd, m_w_pool, m_pool_scale, m_w_out, m_g_mlp, m_w_up, m_w_down, m_g_final, v_w_ada, v_b_ada, v_g_mix, v_w_in, v_conv_w, v_conv_b, v_dt_bias, v_a_log, v_d_skip, v_g_ssd, v_w_pool, v_pool_scale, v_w_out, v_g_mlp, v_w_up, v_w_down, v_g_final):
    nb, seq, _ = x.shape
    T = nb * seq
    me = 4 * lax.axis_index("x") + 2 * lax.axis_index("y") + lax.axis_index("c")
    in_cols = w_in.shape[2]
    ada_cols = w_ada.shape[2]
    cw_cols = conv_w.shape[2]

    c_g, cw_g, win_g = _all_gather([c, conv_w[0], w_in[0].astype(BF16)], "ag_first")
    c_all = c_g.reshape(N_DEV * nb, D)
    cw_full = cw_g.transpose(1, 0, 2).reshape(4, CONV_CH)

    b_slice = lax.dynamic_slice(b_ada, (0, me * ada_cols), (1, ada_cols))
    mod_cols = _ada_fwd(c_all, w_ada[0], b_slice)
    (mod_g,) = _all_gather([mod_cols], "ag_mod")
    mod_all = mod_g.transpose(1, 0, 2).reshape(N_DEV * nb, 6, D)
    mod_mine = lax.dynamic_slice(mod_all, (me * nb, 0, 0), (nb, 6, D))
    mod = jnp.pad(mod_mine, ((0, 0), (0, 2), (0, 0)))

    x2 = x.reshape(T, D)
    tg2 = loss_target.reshape(T, D)
    heads = jnp.pad(jnp.concatenate([dt_bias, a_log, d_skip], axis=0), ((0, 5), (0, LANES - N_HEADS)))
    wpool_b = w_pool[0]
    u_b, pm, w_cat, wup_g = _mix_in(x2, mod, g_mix, win_g, seq, [w_up[0].astype(BF16)])
    ymix, hstates, cvs, wout_g, wdn_g = _mixer_fwd(
        pm, cw_full, conv_b, heads, g_ssd, wpool_b, pool_scale, nb, seq,
        [w_out[0].astype(BF16), w_down[0].astype(BF16)])
    da_b, dym, dh1, u2_b, f_b, dup_b, ddn_b, dmod_a, acc_a = _mlp_fused(
        x2, ymix, tg2, mod, g_mlp, g_final.reshape(1, D), wout_g.reshape(MIX_W, D), wup_g, wdn_g, seq)

    gout_p = _dw_blocks(ymix, da_b, "dw_out", True, per_step=4)
    ex_out = _exchange_start(gout_p, "gout_start")
    gup_p = _dw_blocks(u2_b, dup_b, "dw_up", False, after=[ex_out[4]])
    ex_up = _exchange_start(gup_p, "gup_start")
    gdn_p = _dw_blocks(f_b, ddn_b, "dw_down", True, after=[ex_up[4]])
    ex_dn = _exchange_start(gdn_p, "gdn_start")
    dpb, d_conv, d_heads, d_vec, d_wpool = _mixer_bwd(
        pm, cvs, dym, hstates, cw_full, conv_b, heads, g_ssd, wpool_b, pool_scale, nb, seq, after=[ex_dn[4]])
    gin_p, conv_g, vec_g, heads_g, wpool_parts = _dw_in(
        u_b, dpb, in_cols, [d_conv, d_vec, d_heads, d_wpool.reshape(4 * LANES, LANES)])
    ex_in = _exchange_start(gin_p, "gin_start")
    grad_x2, dmod, acc = _in_bwd(x2, dh1, dpb, mod, g_mix + ex_in[4][0:1, 0:1], w_cat, dmod_a, acc_a, seq)

    def landed(ex, after, name):
        own, land = _exchange_wait(ex[0], ex[1], ex[2], ex[3], after, name)
        mine = lax.dynamic_slice(own, (me, 0, 0), (1,) + own.shape[1:])
        return lax.dynamic_update_slice(land, mine, (me, 0, 0))

    gout_r, gup_r, gdn_r = landed(ex_out, dmod, "gout_wait"), landed(ex_up, dmod, "gup_wait"), landed(ex_dn, dmod, "gdn_wait")
    g_out, d_out, nm_out, nv_out = _adam_parts(gout_r, w_out[0], m_w_out[0], v_w_out[0], "adam_w_out")
    g_up, d_up, nm_up, nv_up = _adam_parts(gup_r, w_up[0], m_w_up[0], v_w_up[0], "adam_w_up")
    g_dn, d_dn, nm_dn, nv_dn = _adam_parts(gdn_r, w_down[0], m_w_down[0], v_w_down[0], "adam_w_down")

    dmod_g, acc_g = _all_gather([dmod, acc], "ag_small_bwd", after=[nm_out, nm_up, nm_dn])
    pool2 = (4 * LANES, LANES)
    wpool_outs = _adam_parts(wpool_parts, w_pool.reshape(pool2), m_w_pool.reshape(pool2), v_w_pool.reshape(pool2),
                             "adam_w_pool")
    small_params = {
        "b_ada": (b_ada, m_b_ada, v_b_ada), "g_mix": (g_mix, m_g_mix, v_g_mix), "conv_b": (conv_b, m_conv_b, v_conv_b),
        "dt_bias": (dt_bias, m_dt_bias, v_dt_bias), "a_log": (a_log, m_a_log, v_a_log),
        "d_skip": (d_skip, m_d_skip, v_d_skip), "g_ssd": (g_ssd, m_g_ssd, v_g_ssd),
        "pool_scale": (pool_scale, m_pool_scale, v_pool_scale), "g_mlp": (g_mlp, m_g_mlp, v_g_mlp),
        "g_final": tuple(a.reshape(1, D) for a in (g_final, m_g_final, v_g_final)),
    }
    small_res = _small_adam([dmod_g, acc_g, conv_g, vec_g, heads_g], small_params)
    g_cw_full, acc_sum = small_res[-2:]
    loss = acc_sum[2, 0]

    g_cw = lax.dynamic_slice(g_cw_full, (0, me * cw_cols), (4, cw_cols))
    d_cwp, nm_cwp, nv_cwp = _adam_plain(g_cw, conv_w[0], m_conv_w[0], v_conv_w[0], "adam_conv_w")

    dmod_all = dmod_g[:, :, 0:6].reshape(N_DEV * nb, 6 * D)
    dmod_slice = lax.dynamic_slice(dmod_all, (0, me * ada_cols), (N_DEV * nb, ada_cols))
    g_ada, d_ada, nm_ada, nv_ada = _ada_bwd_adam(c_all, dmod_slice, w_ada[0], m_w_ada[0], v_w_ada[0])

    ex_after = nm_ada[0:8, 0:LANES] + acc_sum[:, 0:LANES]
    gin_r = landed(ex_in, ex_after, "gin_wait")
    g_in, d_in, nm_in, nv_in = _adam_parts(gin_r, w_in[0], m_w_in[0], v_w_in[0], "adam_w_in")

    def small_outs(kind, wpool):
        res = {name: small_res[4 * i + kind] for i, name in enumerate(SMALL_PARAMS)}
        res["g_final"] = res["g_final"].reshape(D)
        res["w_pool"] = wpool.reshape(1, 4, LANES, LANES)
        return res

    def big_outs(ada, win, cwp, wout, wup, wdn):
        return {"w_ada": ada[None], "w_in": win.reshape(1, D, in_cols), "conv_w": cwp[None], "w_out": wout[None],
                "w_up": wup[None], "w_down": wdn[None]}

    order = ["w_ada", "b_ada", "g_mix", "w_in", "conv_w", "conv_b", "dt_bias", "a_log", "d_skip", "g_ssd", "w_pool",
             "pool_scale", "w_out", "g_mlp", "w_up", "w_down", "g_final"]
    groups = [
        {**small_outs(0, wpool_outs[0]), **big_outs(g_ada, g_in, g_cw, g_out, g_up, g_dn)},
        {**small_outs(1, wpool_outs[1]), **big_outs(d_ada, d_in, d_cwp, d_out, d_up, d_dn)},
        {**small_outs(2, wpool_outs[2]), **big_outs(nm_ada, nm_in, nm_cwp, nm_out, nm_up, nm_dn)},
        {**small_outs(3, wpool_outs[3]), **big_outs(nv_ada, nv_in, nv_cwp, nv_out, nv_up, nv_dn)},
    ]
    outs = [loss, grad_x2.reshape(nb, seq, D)]
    for grp in groups:
        outs += [grp[n] for n in order]
    return tuple(outs)
```

```python
import functools

import jax
import jax.numpy as jnp
from jax import lax
from jax.experimental import pallas as pl
from jax.experimental.pallas import tpu as pltpu

F32, BF16 = jnp.float32, jnp.bfloat16
MESH = pl.DeviceIdType.MESH
N_DEV = 8
D = 1024
LANES = 128
CHUNK = 128
POOL_W = 512
WINDOWS = (2, 4, 8, 16)
N_HEADS = 16
HEAD_DIM = 64
N_GROUPS = 2
GROUP_W = 512
N_STATE = 128
CONV_CH = 1536
OFF_Z, OFF_XBC, OFF_DT, IN_W = 512, 1536, 3072, 3088
PROJ_W = OFF_DT + LANES
MIX_W = 1536
D_FF = 4096
FF_BLK = 512
EPS = 1e-5
LR, B1, B2, AEPS, WD, STEP = 0.001, 0.9, 0.999, 1e-08, 0.01, 10
POOL_HALO = 16
CONV_HALO = 8
VMEM_LIMIT = 56 << 20
ADAM_BLOCK_BYTES = 1 << 20
DW_IN_WIN = 512


def _cparams(**kw):
    return pltpu.CompilerParams(vmem_limit_bytes=VMEM_LIMIT, **kw)


def _mm(a, b):
    return jnp.dot(a.astype(BF16), b.astype(BF16), preferred_element_type=F32)


def _mm_nt(a, b):
    return lax.dot_general(a.astype(BF16), b.astype(BF16), (((1,), (1,)), ((), ())), preferred_element_type=F32)


def _mm_tn(a, b):
    return lax.dot_general(a.astype(BF16), b.astype(BF16), (((0,), (0,)), ((), ())), preferred_element_type=F32)


def _split_bf16(v, terms):
    parts, rest = [], v
    for t in range(terms):
        p = rest.astype(BF16)
        parts.append(p)
        if t + 1 < terms:
            rest = rest - p.astype(F32)
    return parts


def _dot01(a, b, terms, split_lhs=True):
    if split_lhs:
        bb = b.astype(BF16)
        prods = [jnp.dot(p, bb, preferred_element_type=F32) for p in _split_bf16(a, terms)]
    else:
        ab = a.astype(BF16)
        prods = [jnp.dot(ab, p, preferred_element_type=F32) for p in _split_bf16(b, terms)]
    out = prods[0]
    for q in prods[1:]:
        out = out + q
    return out


def _sigmoid(v):
    return 1.0 / (1.0 + jnp.exp(-v))


def _expand_mat():
    r = lax.broadcasted_iota(jnp.int32, (LANES, D), 0)
    c = lax.broadcasted_iota(jnp.int32, (LANES, D), 1)
    return (r == c // HEAD_DIM).astype(F32)


def _reduce_mat():
    r = lax.broadcasted_iota(jnp.int32, (D, LANES), 0)
    c = lax.broadcasted_iota(jnp.int32, (D, LANES), 1)
    return (c == r // HEAD_DIM).astype(F32)


def _pos():
    return lax.axis_index("x"), lax.axis_index("y"), lax.axis_index("c")


GATHER_PIECES = 4
GATHER_PIECE_BYTES = 128 << 10


def _pieces(shape, dtype):
    rows = shape[0]
    size = jnp.dtype(dtype).itemsize
    for d in shape:
        size *= d
    whole_tiles = rows % (GATHER_PIECES * 16) == 0
    return GATHER_PIECES if whole_tiles and size // GATHER_PIECES >= GATHER_PIECE_BYTES else 1


class _Gather:
    def __init__(self, x_refs, o_refs, send, recv, loc):
        self.x_refs, self.o_refs, self.send, self.recv, self.loc = x_refs, o_refs, send, recv, loc
        self.n = len(x_refs)
        self.pieces = [_pieces(r.shape, r.dtype) for r in x_refs]
        self.base = [7 * sum(self.pieces[:a]) for a in range(self.n)]
        x, y, c = _pos()
        self.c = c
        self.me, self.sib = (x, y, c), (x, y, 1 - c)
        self.chips = [(1 - x, y), (x, 1 - y), (1 - x, 1 - y)]

    def _rows(self, a, p):
        rows = self.x_refs[a].shape[0] // self.pieces[a]
        return pl.ds(p * rows, rows)

    def _cp(self, a, p, k, block, to, own=False):
        dst = self.o_refs[a].at[4 * block[0] + 2 * block[1] + block[2], self._rows(a, p)]
        sem = self.base[a] + 7 * p + k
        return pltpu.make_async_remote_copy(
            src_ref=self.x_refs[a].at[self._rows(a, p)] if own else dst, dst_ref=dst,
            send_sem=self.send.at[sem], recv_sem=self.recv.at[sem], device_id=to, device_id_type=MESH)

    def _mine(self, a):
        me = self.me
        return pltpu.make_async_copy(self.x_refs[a], self.o_refs[a].at[4 * me[0] + 2 * me[1] + me[2]], self.loc.at[a])

    def _nbr(self, j):
        x, y, _ = self.me
        return (x + (1 - j) * (1 - 2 * x), y + j * (1 - 2 * y))

    def _first(self, a, p):
        cps = [self._cp(a, p, 0, self.me, self.sib, own=True)]
        return cps + [self._cp(a, p, 1 + j, self.me, (*self.chips[j], self.c), own=True) for j in range(2)]

    def _relay(self, a, p):
        src_chip, dst_chip = self._nbr(1 - self.c), self._nbr(self.c)
        return self._cp(a, p, 3, (*src_chip, self.c), (*dst_chip, self.c))

    def _passed(self, a, p, j):
        return self._cp(a, p, 4 + j, (*self.chips[j], self.c), self.sib)

    def start(self):
        for a in range(self.n):
            self._mine(a).start()
        for p in range(max(self.pieces)):
            for a in range(self.n):
                if p < self.pieces[a]:
                    for cp in self._first(a, p):
                        cp.start()

    def forward(self, p):
        c = self.c
        arrays = [a for a in range(self.n) if p < self.pieces[a]]
        for relayed, j in ((True, 1 - c), (False, c)):
            chip = self._nbr(j)
            for a in arrays:
                self._cp(a, p, 1 + j, (*chip, c), self.me).wait_recv()
                if relayed:
                    self._relay(a, p).start()
                self._cp(a, p, 4 + j, (*chip, c), self.sib).start()
        for a in arrays:
            self._cp(a, p, 3, (*self.chips[2], c), self.me).wait_recv()
            self._passed(a, p, 2).start()

    def finish(self):
        for a in range(self.n):
            for p in range(self.pieces[a]):
                self._cp(a, p, 0, self.sib, self.me).wait_recv()
                for j, chip in enumerate(self.chips):
                    self._cp(a, p, 4 + j, (*chip, 1 - self.c), self.me).wait_recv()
        for a in range(self.n):
            for p in range(self.pieces[a]):
                for cp in self._first(a, p):
                    cp.wait_send()
                self._relay(a, p).wait_send()
                for j in range(3):
                    self._passed(a, p, j).wait_send()
            self._mine(a).wait()

    def begin_hosted(self, step, steps):
        @pl.when(step == 0)
        def _():
            self.start()

        n_p = max(self.pieces)
        for p in range(n_p):
            @pl.when(step == min(((p + 1) * 7 * steps) // (8 * n_p), steps - 1))
            def _():
                self.forward(p)

    def end_hosted(self, step, steps):
        @pl.when(step == steps - 1)
        def _():
            self.finish()


class _Exchange:
    def __init__(self, x_refs, o_refs, send, recv, loc):
        self.x_refs, self.o_refs, self.send, self.recv, self.loc = x_refs, o_refs, send, recv, loc
        self.n = len(x_refs)
        x, y, c = _pos()
        self.me_i = 4 * x + 2 * y + c
        self.peers = []
        for k in range(1, N_DEV):
            px = 1 - x if (k >> 2) & 1 else x
            py = 1 - y if (k >> 1) & 1 else y
            pc = 1 - c if k & 1 else c
            self.peers.append(((px, py, pc), 4 * px + 2 * py + pc))

    def _mine(self, a):
        return pltpu.make_async_copy(self.x_refs[a].at[self.me_i], self.o_refs[a].at[self.me_i], self.loc.at[a])

    def _cp(self, a, k, landing):
        peer, peer_i = self.peers[k]
        return pltpu.make_async_remote_copy(
            src_ref=self.x_refs[a].at[peer_i], dst_ref=self.o_refs[a].at[landing],
            send_sem=self.send.at[a * 7 + k], recv_sem=self.recv.at[a * 7 + k],
            device_id=peer, device_id_type=MESH)

    def start(self):
        for a in range(self.n):
            self._mine(a).start()
            for k in range(N_DEV - 1):
                self._cp(a, k, self.me_i).start()

    def finish(self):
        for a in range(self.n):
            for k in range(N_DEV - 1):
                self._cp(a, k, self.peers[k][1]).wait_recv()
        for a in range(self.n):
            for k in range(N_DEV - 1):
                self._cp(a, k, self.me_i).wait_send()
            self._mine(a).wait()


def _gather_scratch(xs):
    n_sem = 7 * sum(_pieces(v.shape, v.dtype) for v in xs)
    return [pltpu.SemaphoreType.DMA((n_sem,)), pltpu.SemaphoreType.DMA((n_sem,)), pltpu.SemaphoreType.DMA((len(xs),))]


ANY_SPEC = pl.BlockSpec(memory_space=pl.ANY)


def _all_gather(xs, name, after=()):
    n, na = len(xs), len(after)

    def body(*refs):
        g = _Gather(refs[:n], refs[n + na:2 * n + na], *refs[2 * n + na:])
        g.start()
        for p in range(max(g.pieces)):
            g.forward(p)
        g.finish()

    return pl.pallas_call(
        body, name=name,
        out_shape=[jax.ShapeDtypeStruct((N_DEV,) + v.shape, v.dtype) for v in xs],
        in_specs=[ANY_SPEC] * (n + na), out_specs=[ANY_SPEC] * n, scratch_shapes=_gather_scratch(xs),
    )(*xs, *after)


HBM_SPEC = pl.BlockSpec(memory_space=pltpu.HBM)
SEM_SPEC = pl.BlockSpec(memory_space=pltpu.SEMAPHORE)
VMEM_SPEC = pl.BlockSpec(memory_space=pltpu.VMEM)
SPLIT_EFFECT = pltpu.SideEffectType.DATAFLOW_SIDE_EFFECTING


def _in_hbm(v):
    return pltpu.with_memory_space_constraint(v, pltpu.HBM)


def _exchange_start(blocks, name):
    def body(x_ref, land_ref, send, recv, x_thru, land_thru, token):
        ex = _Exchange([x_ref], [land_ref], send, recv, None)
        for k in range(N_DEV - 1):
            ex._cp(0, k, ex.me_i).start()
        token[...] = jnp.zeros_like(token)

    hbm = pltpu.HBM(blocks.shape, blocks.dtype)
    return pl.pallas_call(
        body, name=name,
        out_shape=(pltpu.SemaphoreType.DMA((N_DEV - 1,)), pltpu.SemaphoreType.DMA((N_DEV - 1,)), hbm, hbm,
                   jax.ShapeDtypeStruct((8, LANES), F32)),
        in_specs=(HBM_SPEC, HBM_SPEC), out_specs=(SEM_SPEC, SEM_SPEC, HBM_SPEC, HBM_SPEC, VMEM_SPEC),
        input_output_aliases={0: 2, 1: 3},
        compiler_params=pltpu.CompilerParams(has_side_effects=SPLIT_EFFECT),
    )(_in_hbm(blocks), _in_hbm(lax.empty(blocks.shape, blocks.dtype)))


def _exchange_wait(send, recv, x_thru, land_thru, after, name):
    def body(x_ref, land_ref, send_ref, recv_ref, after_ref, x_dead, got_ref):
        ex = _Exchange([x_ref], [land_ref], send_ref, recv_ref, None)
        for k in range(N_DEV - 1):
            ex._cp(0, k, ex.me_i).wait_send()
            ex._cp(0, k, ex.peers[k][1]).wait_recv()

    hbm = pltpu.HBM(x_thru.shape, x_thru.dtype)
    return pl.pallas_call(
        body, name=name, out_shape=(hbm, hbm),
        in_specs=(HBM_SPEC, HBM_SPEC, SEM_SPEC, SEM_SPEC, ANY_SPEC), out_specs=(HBM_SPEC, HBM_SPEC),
        input_output_aliases={0: 0, 1: 1},
        compiler_params=pltpu.CompilerParams(has_side_effects=SPLIT_EFFECT),
    )(x_thru, land_thru, send, recv, after)


def _ada_fwd(c_all, w_ada, b_slice):
    def body(c_ref, w_ref, b_ref, o_ref):
        cv = c_ref[...]
        act = cv * _sigmoid(cv)
        o_ref[...] = _mm(act, w_ref[...]) + b_ref[...]

    nb, nc = c_all.shape[0], w_ada.shape[1]
    return pl.pallas_call(body, name="ada_fwd", out_shape=jax.ShapeDtypeStruct((nb, nc), F32),
                          compiler_params=_cparams())(c_all, w_ada, b_slice)


def _adam_math(w, g, m, v):
    m = B1 * m + (1.0 - B1) * g
    v = B2 * v + (1.0 - B2) * jnp.square(g)
    m_hat = m / (1.0 - B1 ** STEP)
    v_hat = v / (1.0 - B2 ** STEP)
    delta = -LR * (m_hat / (jnp.sqrt(v_hat) + AEPS) + WD * w)
    return delta, m, v


def _ada_bwd_adam(c_all, dmod_slice, w, m, v):
    rows, cols = w.shape
    br = 256

    def body(c_ref, d_ref, w_ref, m_ref, v_ref, g_out, dl_out, m_out, v_out):
        cv = c_ref[...]
        act = cv * _sigmoid(cv)
        g = _mm_tn(act, d_ref[...])
        g_out[...] = g
        dl, mn, vn = _adam_math(w_ref[...], g, m_ref[...], v_ref[...])
        dl_out[...] = dl
        m_out[...] = mn
        v_out[...] = vn

    nb = c_all.shape[0]
    wspec = pl.BlockSpec((br, cols), lambda i: (i, 0))
    return pl.pallas_call(
        body, name="ada_bwd_adam", grid=(rows // br,),
        in_specs=[pl.BlockSpec((nb, br), lambda i: (0, i)), pl.BlockSpec((nb, cols), lambda i: (0, 0)),
                  wspec, wspec, wspec],
        out_specs=[wspec] * 4, out_shape=[jax.ShapeDtypeStruct((rows, cols), F32)] * 4,
        compiler_params=_cparams(),
    )(c_all, dmod_slice, w, m, v)


def _mix_in(x2, mod, g_mix, win_g, seq, shards):
    T = x2.shape[0]
    tm = min(512, seq)
    tps = seq // tm
    in_cols = win_g.shape[2]
    ns = len(shards)
    steps = T // tm

    def body(*refs):
        x_ref, mod_ref, g_ref, wb_ref = refs[:4]
        sh_refs = refs[4:4 + ns]
        u_ref, pm_ref, wc_ref = refs[4 + ns:7 + ns]
        ga_refs = refs[7 + ns:7 + 2 * ns]
        w_ref, send, recv, loc = refs[7 + 2 * ns:]
        step = pl.program_id(0)
        gather = _Gather(sh_refs, ga_refs, send, recv, loc)
        gather.begin_hosted(step, steps)

        @pl.when(step == 0)
        def _():
            w_ref[:, OFF_DT:] = jnp.zeros((D, PROJ_W - OFF_DT), BF16)
            for j in range(N_DEV):
                w_ref[:, in_cols * j:in_cols * (j + 1)] = wb_ref[j]
            wc_ref[...] = w_ref[...]

        x = x_ref[...]
        r = lax.rsqrt(jnp.mean(x * x, axis=-1, keepdims=True) + EPS)
        md = mod_ref[0]
        u = (x * r * g_ref[...]) * (1.0 + md[1:2]) + md[0:1]
        ub = u.astype(BF16)
        u_ref[...] = ub
        pm_ref[...] = jnp.dot(ub, w_ref[...], preferred_element_type=F32)
        gather.end_hosted(step, steps)

    whole = pl.BlockSpec(memory_space=pltpu.VMEM)
    return pl.pallas_call(
        body, name="mix_in", grid=(T // tm,),
        in_specs=[pl.BlockSpec((tm, D), lambda i: (i, 0)), pl.BlockSpec((1, 8, D), lambda i: (i // tps, 0, 0)),
                  pl.BlockSpec((1, D), lambda i: (0, 0)), whole] + [ANY_SPEC] * ns,
        out_specs=[pl.BlockSpec((tm, D), lambda i: (i, 0)), pl.BlockSpec((tm, PROJ_W), lambda i: (i, 0)),
                   pl.BlockSpec((D, PROJ_W), lambda i: (0, 0))] + [ANY_SPEC] * ns,
        out_shape=[jax.ShapeDtypeStruct((T, D), BF16), jax.ShapeDtypeStruct((T, PROJ_W), F32),
                   jax.ShapeDtypeStruct((D, PROJ_W), BF16)]
        + [jax.ShapeDtypeStruct((N_DEV,) + v.shape, v.dtype) for v in shards],
        scratch_shapes=[pltpu.VMEM((D, PROJ_W), BF16)] + _gather_scratch(shards),
        compiler_params=_cparams(),
    )(x2, mod, g_mix, win_g, *shards)


def _chunk_forward(up, z, ux, dtin, halo_p, halo_x, hprev, cw, cb, hp, gssd, wpool, pscale, t0, y_scr, cv=None):
    L = CHUNK
    out = {}
    row = lax.broadcasted_iota(jnp.int32, (L, 1), 0)
    t = (t0 + row + 1).astype(F32)
    e = jnp.concatenate([halo_p, up], axis=0)
    s2 = e + pltpu.roll(e, 1, 0)
    s4 = s2 + pltpu.roll(s2, 2, 0)
    s8 = s4 + pltpu.roll(s4, 4, 0)
    s16 = s8 + pltpu.roll(s8, 8, 0)
    sums = (s2, s4, s8, s16)
    p, inv, yp = [], [], []
    for gi, w in enumerate(WINDOWS):
        sl = slice(gi * LANES, (gi + 1) * LANES)
        ic = 1.0 / jnp.minimum(t, float(w))
        pg = sums[gi][POOL_HALO:, sl] * ic - up[:, sl]
        p.append(pg)
        inv.append(ic)
        yp.append(_mm(pg, wpool[gi]))
    out["p"], out["inv"], out["yp"] = p, inv, yp
    out["y_pool"] = jnp.concatenate(yp, axis=1) * pscale
    if cv is None:
        ex = jnp.concatenate([halo_x, ux], axis=0)
        taps = [pltpu.roll(ex, 3, 0)[CONV_HALO:], pltpu.roll(ex, 2, 0)[CONV_HALO:], pltpu.roll(ex, 1, 0)[CONV_HALO:], ux]
        cv = cb + taps[0] * cw[0:1] + taps[1] * cw[1:2] + taps[2] * cw[2:3] + taps[3] * cw[3:4]
    sg = _sigmoid(cv)
    xbc = cv * sg
    out["cv"], out["sg"] = cv, sg
    X = xbc[:, :D]
    Bm = xbc[:, D:D + N_GROUPS * N_STATE]
    Cm = xbc[:, D + N_GROUPS * N_STATE:]
    pre = dtin + hp[0:1]
    dt = jnp.maximum(pre, 0.0) + jnp.log(1.0 + jnp.exp(-jnp.abs(pre)))
    a_row = -jnp.exp(hp[1:2])
    da = dt * a_row
    ri = lax.broadcasted_iota(jnp.int32, (L, L), 0)
    ci = lax.broadcasted_iota(jnp.int32, (L, L), 1)
    causal = ri >= ci
    cum = _dot01(causal.astype(F32), da, 3, split_lhs=False)
    cum_t = cum.T
    cum_last = cum[L - 1:L]
    eo = jnp.exp(cum)
    dec = jnp.exp(cum_last - cum)
    cd = jnp.exp(cum_last)
    exm = _expand_mat()
    rows8 = jnp.concatenate([cd, hp[2:3], jnp.zeros((6, LANES), F32)], axis=0)
    rep = _dot01(jnp.concatenate([dt, eo, dec, rows8], axis=0), exm, 2)
    dt_rep, eo_rep, dec_rep = rep[0:L], rep[L:2 * L], rep[2 * L:3 * L]
    cd_rep, dskip_rep = rep[3 * L:3 * L + 1], rep[3 * L + 1:3 * L + 2]
    xdt = X * dt_rep
    out.update(X=X, Bm=Bm, Cm=Cm, pre=pre, dt=dt, a_row=a_row, cum=cum, cum_t=cum_t, eo=eo, dec=dec, cd=cd,
               dt_rep=dt_rep, eo_rep=eo_rep, dec_rep=dec_rep, cd_rep=cd_rep, dskip_rep=dskip_rep, xdt=xdt,
               causal=causal, anti=(ri <= ci).astype(F32), exm=exm)
    G, lms, yoff, hnew, xdec = [], [], [], [], []
    for g in range(N_GROUPS):
        gs = slice(g * GROUP_W, (g + 1) * GROUP_W)
        Bg = Bm[:, g * N_STATE:(g + 1) * N_STATE]
        Cg = Cm[:, g * N_STATE:(g + 1) * N_STATE]
        Gg = _mm_nt(Cg, Bg)
        G.append(Gg)
        for hh in range(N_HEADS // N_GROUPS):
            h = g * (N_HEADS // N_GROUPS) + hh
            seg = cum[:, h:h + 1] - cum_t[h:h + 1, :]
            lm = jnp.where(causal, jnp.exp(jnp.minimum(seg, 0.0)), 0.0)
            lms.append(lm)
            hs = slice(h * HEAD_DIM, (h + 1) * HEAD_DIM)
            y_scr[:, hs] = _mm(Gg * lm, xdt[:, hs])
        xd = xdt[:, gs] * dec_rep[:, gs]
        xdec.append(xd)
        sgm = _mm_tn(Bg, xd)
        yoff.append(_mm(Cg, hprev[g]) * eo_rep[:, gs])
        hnew.append(hprev[g] * cd_rep[:, gs] + sgm)
    out.update(G=G, lms=lms, yoff=yoff, hnew=hnew, xdec=xdec)
    y = y_scr[...] + jnp.concatenate(yoff, axis=1) + dskip_rep * X
    sz = _sigmoid(z)
    silz = z * sz
    yz = y * silz
    rg, yn = [], []
    for g in range(N_GROUPS):
        gs = slice(g * GROUP_W, (g + 1) * GROUP_W)
        r = lax.rsqrt(jnp.mean(yz[:, gs] * yz[:, gs], axis=-1, keepdims=True) + EPS)
        rg.append(r)
        yn.append(yz[:, gs] * r)
    yn = jnp.concatenate(yn, axis=1)
    out.update(y=y, sz=sz, silz=silz, rg=rg, yn=yn)
    out["y_ssd"] = yn * gssd
    return out


def _mixer_fwd(pm, cw, cb, hp, gssd, wpool, pscale, nb, seq, shards):
    nc = seq // CHUNK
    ns = len(shards)
    steps = nb * nc

    def body(*refs):
        pm_ref, cw_ref, cb_ref, hp_ref, gs_ref, wp_ref, ps_ref = refs[:7]
        sh_refs = refs[7:7 + ns]
        ym_ref, hs_ref, cv_ref = refs[7 + ns:10 + ns]
        ga_refs = refs[10 + ns:10 + 2 * ns]
        halo_p, halo_x, state, y_scr, send, recv, loc = refs[10 + 2 * ns:]
        c = pl.program_id(1)
        step = pl.program_id(0) * nc + c
        gather = _Gather(sh_refs, ga_refs, send, recv, loc)
        gather.begin_hosted(step, steps)

        @pl.when(c == 0)
        def _():
            halo_p[...] = jnp.zeros_like(halo_p)
            halo_x[...] = jnp.zeros_like(halo_x)
            state[...] = jnp.zeros_like(state)

        up = pm_ref[:, 0:POOL_W]
        z = pm_ref[:, OFF_Z:OFF_XBC]
        ux = pm_ref[:, OFF_XBC:OFF_DT]
        hprev = [state[0], state[1]]
        hs_ref[0, 0, 0] = hprev[0]
        hs_ref[0, 0, 1] = hprev[1]
        o = _chunk_forward(up, z, ux, pm_ref[:, OFF_DT:], halo_p[...], halo_x[...], hprev, cw_ref[...], cb_ref[...],
                           hp_ref[...], gs_ref[...], wp_ref[...], ps_ref[...], c * CHUNK, y_scr)
        ym_ref[:, 0:POOL_W] = o["y_pool"].astype(BF16)
        ym_ref[:, POOL_W:] = o["y_ssd"].astype(BF16)
        cv_ref[...] = o["cv"]
        state[0] = o["hnew"][0]
        state[1] = o["hnew"][1]
        halo_p[...] = up[CHUNK - POOL_HALO:]
        halo_x[...] = ux[CHUNK - CONV_HALO:]
        gather.end_hosted(step, steps)

    def full(shape):
        return pl.BlockSpec(shape, lambda b, c: (0,) * len(shape))

    T = nb * seq
    return pl.pallas_call(
        body, name="mixer_fwd", grid=(nb, nc),
        in_specs=[pl.BlockSpec((CHUNK, PROJ_W), lambda b, c: (b * nc + c, 0)),
                  full((4, CONV_CH)), full((1, CONV_CH)), full((8, LANES)), full((1, D)),
                  full((4, LANES, LANES)), full((1, POOL_W))] + [ANY_SPEC] * ns,
        out_specs=[pl.BlockSpec((CHUNK, MIX_W), lambda b, c: (b * nc + c, 0)),
                   pl.BlockSpec((1, 1, N_GROUPS, N_STATE, GROUP_W), lambda b, c: (b, c, 0, 0, 0)),
                   pl.BlockSpec((CHUNK, CONV_CH), lambda b, c: (b * nc + c, 0))] + [ANY_SPEC] * ns,
        out_shape=[jax.ShapeDtypeStruct((T, MIX_W), BF16),
                   jax.ShapeDtypeStruct((nb, nc, N_GROUPS, N_STATE, GROUP_W), F32),
                   jax.ShapeDtypeStruct((T, CONV_CH), F32)]
        + [jax.ShapeDtypeStruct((N_DEV,) + v.shape, v.dtype) for v in shards],
        scratch_shapes=[pltpu.VMEM((POOL_HALO, POOL_W), F32), pltpu.VMEM((CONV_HALO, CONV_CH), F32),
                        pltpu.VMEM((N_GROUPS, N_STATE, GROUP_W), F32), pltpu.VMEM((CHUNK, D), F32)] + _gather_scratch(shards),
        compiler_params=_cparams(),
    )(pm, cw, cb, hp, gssd, wpool, pscale, *shards)


def _mixer_bwd(pm, cvs, dym, hstates, cw, cb, hp, gssd, wpool, pscale, nb, seq, after=()):
    nc = seq // CHUNK
    hpg = N_HEADS // N_GROUPS
    na = len(after)

    def body(*refs):
        (pm_ref, hpool_ref, cv_ref, dy_ref, hs_ref, cw_ref, cb_ref, hp_ref, gs_ref, wp_ref, ps_ref) = refs[:11]
        dpm_ref, dconv_ref, dhp_ref, dvec_ref, dwp_ref = refs[11 + na:16 + na]
        nxt_q, nxt_cv, rstate, y_scr, dx_scr = refs[16 + na:]
        b = pl.program_id(0)
        ci = pl.program_id(1)
        c = nc - 1 - ci

        @pl.when((b == 0) & (ci == 0))
        def _():
            for r in (dconv_ref, dhp_ref, dvec_ref, dwp_ref):
                r[...] = jnp.zeros_like(r)

        @pl.when(ci == 0)
        def _():
            nxt_q[...] = jnp.zeros_like(nxt_q)
            nxt_cv[...] = jnp.zeros_like(nxt_cv)
            rstate[...] = jnp.zeros_like(rstate)

        first = (c > 0).astype(F32)
        up = pm_ref[:, 0:POOL_W]
        z = pm_ref[:, OFF_Z:OFF_XBC]
        ux = pm_ref[:, OFF_XBC:OFF_DT]
        halo_p = hpool_ref[...] * first
        hprev = [hs_ref[0, 0, 0], hs_ref[0, 0, 1]]
        cw, cb, hp, gssd, wpool, pscale = cw_ref[...], cb_ref[...], hp_ref[...], gs_ref[...], wp_ref[...], ps_ref[...]
        o = _chunk_forward(up, z, ux, pm_ref[:, OFF_DT:], halo_p, None, hprev, cw, cb, hp, gssd, wpool, pscale,
                           c * CHUNK, y_scr, cv=cv_ref[...])
        L = CHUNK
        dy_pool = dy_ref[:, 0:POOL_W].astype(F32)
        dy_ssd = dy_ref[:, POOL_W:].astype(F32)

        dvec_ref[1:2, 0:POOL_W] += jnp.sum(dy_pool * jnp.concatenate(o["yp"], axis=1), axis=0, keepdims=True)
        dyp = dy_pool * pscale
        qs = []
        dps = []
        for gi in range(len(WINDOWS)):
            sl = slice(gi * LANES, (gi + 1) * LANES)
            dwp_ref[gi] += _mm_tn(o["p"][gi], dyp[:, sl])
            dpg = _mm_nt(dyp[:, sl], wpool[gi])
            dps.append(dpg)
            qs.append(dpg * o["inv"][gi])
        q = jnp.concatenate(qs, axis=1)
        e = jnp.concatenate([q, nxt_q[...]], axis=0)
        n = L + POOL_HALO
        s2 = e + pltpu.roll(e, n - 1, 0)
        s4 = s2 + pltpu.roll(s2, n - 2, 0)
        s8 = s4 + pltpu.roll(s4, n - 4, 0)
        s16 = s8 + pltpu.roll(s8, n - 8, 0)
        sums = (s2, s4, s8, s16)
        for gi in range(len(WINDOWS)):
            sl = slice(gi * LANES, (gi + 1) * LANES)
            dpm_ref[:, sl] = (sums[gi][:L, sl] - dps[gi]).astype(BF16)
        nxt_q[...] = q[:POOL_HALO]

        yn, y, silz, sz = o["yn"], o["y"], o["silz"], o["sz"]
        dvec_ref[0:1] += jnp.sum(dy_ssd * yn, axis=0, keepdims=True)
        dyn = dy_ssd * gssd
        dyz = []
        for g in range(N_GROUPS):
            gs = slice(g * GROUP_W, (g + 1) * GROUP_W)
            mean = jnp.mean(dyn[:, gs] * yn[:, gs], axis=-1, keepdims=True)
            dyz.append(o["rg"][g] * (dyn[:, gs] - yn[:, gs] * mean))
        dyz = jnp.concatenate(dyz, axis=1)
        dyv = dyz * silz
        dpm_ref[:, OFF_Z:OFF_XBC] = (dyz * y * (sz * (1.0 + z * (1.0 - sz)))).astype(BF16)

        X, Bm, Cm, xdt = o["X"], o["Bm"], o["Cm"], o["xdt"]
        exm = o["exm"]
        rdm = _reduce_mat()
        lane = lax.broadcasted_iota(jnp.int32, (1, LANES), 1)
        sub = lax.broadcasted_iota(jnp.int32, (LANES, 1), 0)
        dX = o["dskip_rep"] * dyv
        yoff_full = jnp.concatenate(o["yoff"], axis=1)
        rs = jnp.zeros((L, LANES), F32)
        cs_t = jnp.zeros((LANES, L), F32)
        dBs, dCs = [], []
        rh_sums = []
        ddec = []
        for g in range(N_GROUPS):
            gs = slice(g * GROUP_W, (g + 1) * GROUP_W)
            Bg = Bm[:, g * N_STATE:(g + 1) * N_STATE]
            Cg = Cm[:, g * N_STATE:(g + 1) * N_STATE]
            Gg = o["G"][g]
            R = rstate[g]
            dwm = dyv[:, gs] * o["eo_rep"][:, gs]
            dC = _mm_nt(dwm, hprev[g])
            dH = _mm_tn(Cg, dwm)
            dG = jnp.zeros((L, L), F32)
            for hh in range(hpg):
                h = g * hpg + hh
                hs = slice(h * HEAD_DIM, (h + 1) * HEAD_DIM)
                lm = o["lms"][h]
                m_h = Gg * lm
                dM = _mm_nt(dyv[:, hs], xdt[:, hs])
                dx_scr[:, hs] = _mm_tn(m_h, dyv[:, hs])
                qm = dM * m_h
                rs = rs + jnp.sum(qm, axis=1, keepdims=True) * (lane == h).astype(F32)
                cs_t = cs_t + (sub == h).astype(F32) * jnp.sum(qm, axis=0, keepdims=True)
                dG = dG + dM * lm
            dC = dC + _mm(dG, Bg)
            dB = _mm_tn(dG, Cg)
            zx = _mm(Bg, R)
            dxdt_state = zx * o["dec_rep"][:, gs]
            ddec.append(zx * xdt[:, gs])
            dB = dB + _mm_nt(o["xdec"][g], R)
            rh_sums.append(jnp.sum(R * hprev[g], axis=0, keepdims=True))
            rstate[g] = dH + o["cd_rep"][:, gs] * R
            dx_scr[:, gs] = dx_scr[:, gs] + dxdt_state
            dBs.append(dB)
            dCs.append(dC)
        dxdt = dx_scr[...]
        tail = jnp.concatenate([jnp.sum(dyv * X, axis=0, keepdims=True), jnp.concatenate(rh_sums, axis=1),
                                jnp.zeros((6, D), F32)], axis=0)
        red = _dot01(jnp.concatenate([dyv * yoff_full, jnp.concatenate(ddec, axis=1), dxdt * X, tail], axis=0), rdm, 2)
        d_dskip, dcd_row = red[3 * L:3 * L + 1], red[3 * L + 1:3 * L + 2]
        ddec_h = red[L:2 * L] * o["dec"]
        dcum_last = jnp.sum(ddec_h, axis=0, keepdims=True) + dcd_row * o["cd"]
        dcum = red[0:L] + rs - cs_t.T - ddec_h + (sub == L - 1).astype(F32) * dcum_last
        dda = _dot01(o["anti"], dcum, 3, split_lhs=False)
        ddt_v = dda * o["a_row"] + red[2 * L:3 * L]
        dX = dX + dxdt * o["dt_rep"]
        head_mask = (lane < N_HEADS).astype(F32)
        d_alog = jnp.sum(dda * o["dt"], axis=0, keepdims=True) * o["a_row"] * head_mask
        dpre = ddt_v * _sigmoid(o["pre"]) * head_mask
        dpm_ref[:, OFF_DT:] = dpre.astype(BF16)
        d_dtb = jnp.sum(dpre, axis=0, keepdims=True)
        dhp_ref[...] += jnp.concatenate([d_dtb, d_alog, d_dskip * head_mask, jnp.zeros((5, LANES), F32)], axis=0)

        dxbc = jnp.concatenate([dX] + dBs + dCs, axis=1)
        sg, cv = o["sg"], o["cv"]
        dcv = dxbc * (sg * (1.0 + cv * (1.0 - sg)))
        e2 = jnp.concatenate([dcv, nxt_cv[...]], axis=0)
        n2 = L + CONV_HALO
        ahead = [dcv, pltpu.roll(e2, n2 - 1, 0)[:L], pltpu.roll(e2, n2 - 2, 0)[:L], pltpu.roll(e2, n2 - 3, 0)[:L]]
        dconv_ref[0:5] += jnp.concatenate(
            [jnp.sum(ux * ahead[3 - k], axis=0, keepdims=True) for k in range(4)]
            + [jnp.sum(dcv, axis=0, keepdims=True)], axis=0)
        dux = ahead[0] * cw[3:4] + ahead[1] * cw[2:3] + ahead[2] * cw[1:2] + ahead[3] * cw[0:1]
        dpm_ref[:, OFF_XBC:OFF_DT] = dux.astype(BF16)
        nxt_cv[...] = dcv[:CONV_HALO]

    def full(shape):
        return pl.BlockSpec(shape, lambda b, c: (0,) * len(shape))

    def rowblk(b, c):
        return b * nc + (nc - 1 - c)

    hp_blocks = CHUNK // POOL_HALO
    T = nb * seq
    return pl.pallas_call(
        body, name="mixer_bwd", grid=(nb, nc),
        in_specs=[pl.BlockSpec((CHUNK, PROJ_W), lambda b, c: (rowblk(b, c), 0)),
                  pl.BlockSpec((POOL_HALO, POOL_W), lambda b, c: (jnp.maximum(rowblk(b, c) * hp_blocks - 1, 0), 0)),
                  pl.BlockSpec((CHUNK, CONV_CH), lambda b, c: (rowblk(b, c), 0)),
                  pl.BlockSpec((CHUNK, MIX_W), lambda b, c: (rowblk(b, c), 0)),
                  pl.BlockSpec((1, 1, N_GROUPS, N_STATE, GROUP_W), lambda b, c: (b, nc - 1 - c, 0, 0, 0)),
                  full((4, CONV_CH)), full((1, CONV_CH)), full((8, LANES)), full((1, D)),
                  full((4, LANES, LANES)), full((1, POOL_W))] + [ANY_SPEC] * na,
        out_specs=[pl.BlockSpec((CHUNK, PROJ_W), lambda b, c: (rowblk(b, c), 0)),
                   full((8, CONV_CH)), full((8, LANES)), full((8, D)), full((4, LANES, LANES))],
        out_shape=[jax.ShapeDtypeStruct((T, PROJ_W), BF16),
                   jax.ShapeDtypeStruct((8, CONV_CH), F32), jax.ShapeDtypeStruct((8, LANES), F32),
                   jax.ShapeDtypeStruct((8, D), F32), jax.ShapeDtypeStruct((4, LANES, LANES), F32)],
        scratch_shapes=[pltpu.VMEM((POOL_HALO, POOL_W), F32), pltpu.VMEM((CONV_HALO, CONV_CH), F32),
                        pltpu.VMEM((N_GROUPS, N_STATE, GROUP_W), F32), pltpu.VMEM((CHUNK, D), F32),
                        pltpu.VMEM((CHUNK, D), F32)],
        compiler_params=_cparams(),
    )(pm, pm, cvs, dym, hstates, cw, cb, hp, gssd, wpool, pscale, *after)


def _mlp_fused(x2, ymix, target, mod, g_mlp, g_final, w_out, w_up, w_down, seq):
    T = x2.shape[0]
    tm = min(256, seq)
    tps = seq // tm
    nblk = D_FF // FF_BLK

    def body(x_ref, ym_ref, tg_ref, mod_ref, gm_ref, gf_ref, wo_ref, wu_ref, wd_ref,
             da_ref, dym_ref, dh1_ref, u2_ref, f_ref, dup_ref, ddn_ref, dmod_ref, acc_ref, relu_scr):
        i = pl.program_id(0)

        @pl.when(i == 0)
        def _():
            acc_ref[...] = jnp.zeros_like(acc_ref)

        @pl.when(i % tps == 0)
        def _():
            dmod_ref[...] = jnp.zeros_like(dmod_ref)

        md = mod_ref[0]
        gate_m, shift_f, scale_f, gate_f = md[2:3], md[3:4], md[4:5], md[5:6]
        g_mlp, g_fin = gm_ref[...], gf_ref[...]
        a = jnp.dot(ym_ref[...], wo_ref[...], preferred_element_type=F32)
        h1 = x_ref[...] + gate_m * a
        r2 = lax.rsqrt(jnp.mean(h1 * h1, axis=-1, keepdims=True) + EPS)
        n2 = h1 * r2
        u2 = (n2 * g_mlp) * (1.0 + scale_f) + shift_f
        u2b = u2.astype(BF16)
        u2_ref[...] = u2b
        dn = jnp.zeros((tm, D), F32)
        for j in range(nblk):
            js = slice(j * FF_BLK, (j + 1) * FF_BLK)
            upj = jnp.maximum(jnp.dot(u2b, wu_ref[j], preferred_element_type=F32), 0.0)
            relu_scr[:, js] = upj
            fj = (upj * upj).astype(BF16)
            f_ref[:, js] = fj
            dn = dn + jnp.dot(fj, wd_ref[j], preferred_element_type=F32)
        h2 = h1 + gate_f * dn
        r3 = lax.rsqrt(jnp.mean(h2 * h2, axis=-1, keepdims=True) + EPS)
        n3 = h2 * r3
        err = n3 * g_fin - tg_ref[...]
        loss = 0.5 * jnp.sum(jnp.mean(err * err, axis=-1, keepdims=True), axis=0, keepdims=True)
        dout = err * (1.0 / D)
        d_gfin = jnp.sum(dout * n3, axis=0, keepdims=True)
        dn3 = dout * g_fin
        dh2 = r3 * (dn3 - n3 * jnp.mean(dn3 * n3, axis=-1, keepdims=True))
        d_gate_f = jnp.sum(dh2 * dn, axis=0, keepdims=True)
        ddn = (gate_f * dh2).astype(BF16)
        ddn_ref[...] = ddn
        du2 = jnp.zeros((tm, D), F32)
        for j in range(nblk):
            js = slice(j * FF_BLK, (j + 1) * FF_BLK)
            dfj = lax.dot_general(ddn, wd_ref[j], (((1,), (1,)), ((), ())), preferred_element_type=F32)
            dupj = (dfj * (2.0 * relu_scr[:, js])).astype(BF16)
            dup_ref[:, js] = dupj
            du2 = du2 + lax.dot_general(dupj, wu_ref[j], (((1,), (1,)), ((), ())), preferred_element_type=F32)
        d_scale_f = jnp.sum(du2 * (n2 * g_mlp), axis=0, keepdims=True)
        d_shift_f = jnp.sum(du2, axis=0, keepdims=True)
        d_gmlp = jnp.sum(du2 * (1.0 + scale_f) * n2, axis=0, keepdims=True)
        dn2 = du2 * (g_mlp * (1.0 + scale_f))
        dh1 = dh2 + r2 * (dn2 - n2 * jnp.mean(dn2 * n2, axis=-1, keepdims=True))
        dh1_ref[...] = dh1
        d_gate_m = jnp.sum(dh1 * a, axis=0, keepdims=True)
        da = (gate_m * dh1).astype(BF16)
        da_ref[...] = da
        dym_ref[...] = lax.dot_general(da, wo_ref[...], (((1,), (1,)), ((), ())),
                                       preferred_element_type=F32).astype(BF16)
        dmod_ref[0] += jnp.concatenate([jnp.zeros((2, D), F32), d_gate_m, d_shift_f, d_scale_f, d_gate_f,
                                        jnp.zeros((2, D), F32)], axis=0)
        acc_ref[...] += jnp.concatenate([d_gmlp, d_gfin, loss * jnp.ones((1, D), F32), jnp.zeros((5, D), F32)], axis=0)

    whole = pl.BlockSpec(memory_space=pltpu.VMEM)

    def tok(w):
        return pl.BlockSpec((tm, w), lambda i: (i, 0))

    def vec():
        return pl.BlockSpec((1, D), lambda i: (0, 0))

    nb = T // seq
    return pl.pallas_call(
        body, name="mlp_fused", grid=(T // tm,),
        in_specs=[tok(D), tok(MIX_W), tok(D), pl.BlockSpec((1, 8, D), lambda i: (i // tps, 0, 0)), vec(), vec(),
                  whole, whole, whole],
        out_specs=[tok(D), tok(MIX_W), tok(D), tok(D), tok(D_FF), tok(D_FF), tok(D),
                   pl.BlockSpec((1, 8, D), lambda i: (i // tps, 0, 0)), pl.BlockSpec((8, D), lambda i: (0, 0))],
        out_shape=[jax.ShapeDtypeStruct((T, D), BF16), jax.ShapeDtypeStruct((T, MIX_W), BF16),
                   jax.ShapeDtypeStruct((T, D), F32), jax.ShapeDtypeStruct((T, D), BF16),
                   jax.ShapeDtypeStruct((T, D_FF), BF16), jax.ShapeDtypeStruct((T, D_FF), BF16),
                   jax.ShapeDtypeStruct((T, D), BF16), jax.ShapeDtypeStruct((nb, 8, D), F32),
                   jax.ShapeDtypeStruct((8, D), F32)],
        scratch_shapes=[pltpu.VMEM((tm, D_FF), F32)],
        compiler_params=_cparams(),
    )(x2, ymix, target, mod, g_mlp, g_final, w_out, w_up, w_down)


def _in_bwd(x2, dh1, dpb, mod, g_mix, w_cat, dmod_a, acc_a, seq):
    T = x2.shape[0]
    tm = min(512, seq)
    tps = seq // tm
    steps = T // tm

    def body(x_ref, dh_ref, dpb_ref, mod_ref, g_ref, w_ref, dma_ref, acca_ref, dx_ref, dmod_ref, acc_ref):
        i = pl.program_id(0)

        @pl.when(i == 0)
        def _():
            acc_ref[...] = acca_ref[...]

        @pl.when(i % tps == 0)
        def _():
            dmod_ref[...] = dma_ref[...]

        du = lax.dot_general(dpb_ref[...], w_ref[...], (((1,), (1,)), ((), ())), preferred_element_type=F32)
        x = x_ref[...]
        md = mod_ref[0]
        g = g_ref[...]
        r = lax.rsqrt(jnp.mean(x * x, axis=-1, keepdims=True) + EPS)
        n1 = x * r
        d_scale = jnp.sum(du * (n1 * g), axis=0, keepdims=True)
        d_shift = jnp.sum(du, axis=0, keepdims=True)
        d_g = jnp.sum(du * (1.0 + md[1:2]) * n1, axis=0, keepdims=True)
        dn1 = du * (g * (1.0 + md[1:2]))
        dx_ref[...] = dh_ref[...] + r * (dn1 - n1 * jnp.mean(dn1 * n1, axis=-1, keepdims=True))
        dmod_ref[0] += jnp.concatenate([d_shift, d_scale, jnp.zeros((6, D), F32)], axis=0)
        acc_ref[...] += jnp.concatenate([jnp.zeros((3, D), F32), d_g, jnp.zeros((4, D), F32)], axis=0)

    whole = pl.BlockSpec(memory_space=pltpu.VMEM)
    nb = T // seq
    return pl.pallas_call(
        body, name="in_bwd", grid=(steps,),
        in_specs=[pl.BlockSpec((tm, D), lambda i: (i, 0)), pl.BlockSpec((tm, D), lambda i: (i, 0)),
                  pl.BlockSpec((tm, PROJ_W), lambda i: (i, 0)),
                  pl.BlockSpec((1, 8, D), lambda i: (i // tps, 0, 0)), pl.BlockSpec((1, D), lambda i: (0, 0)),
                  whole, pl.BlockSpec((1, 8, D), lambda i: (i // tps, 0, 0)), pl.BlockSpec((8, D), lambda i: (0, 0))],
        out_specs=[pl.BlockSpec((tm, D), lambda i: (i, 0)),
                   pl.BlockSpec((1, 8, D), lambda i: (i // tps, 0, 0)), pl.BlockSpec((8, D), lambda i: (0, 0))],
        out_shape=[jax.ShapeDtypeStruct((T, D), F32),
                   jax.ShapeDtypeStruct((nb, 8, D), F32), jax.ShapeDtypeStruct((8, D), F32)],
        compiler_params=_cparams(),
    )(x2, dh1, dpb, mod, g_mix, w_cat, dmod_a, acc_a)


def _dw_in(u_b, dpb, in_cols, shards):
    T = u_b.shape[0]
    bk = min(512, T)
    nk = T // bk
    ns = len(shards)
    starts = [(in_cols * j // LANES) * LANES for j in range(N_DEV)]
    assert all(s + DW_IN_WIN <= PROJ_W and in_cols * (j + 1) <= s + DW_IN_WIN for j, s in enumerate(starts))

    def body(*refs):
        u_ref, d_ref = refs[:2]
        sh_refs = refs[2:2 + ns]
        o_ref = refs[2 + ns]
        ga_refs = refs[3 + ns:3 + 2 * ns]
        acc, send, recv, loc = refs[3 + 2 * ns:]
        k = pl.program_id(0)
        gather = _Gather(sh_refs, ga_refs, send, recv, loc)
        gather.begin_hosted(k, nk)

        @pl.when(k == 0)
        def _():
            acc[...] = jnp.zeros_like(acc)

        ut = u_ref[...].T
        for j in range(N_DEV):
            acc[j] += jnp.dot(ut, d_ref[:, starts[j]:starts[j] + DW_IN_WIN], preferred_element_type=F32)

        @pl.when(k == nk - 1)
        def _():
            for j in range(N_DEV):
                off = in_cols * j - starts[j]
                o_ref[j] = acc[j][:, off:off + in_cols].astype(BF16)

        gather.end_hosted(k, nk)

    return pl.pallas_call(
        body, name="dw_in", grid=(nk,),
        in_specs=[pl.BlockSpec((bk, D), lambda k: (k, 0)), pl.BlockSpec((bk, PROJ_W), lambda k: (k, 0))]
        + [ANY_SPEC] * ns,
        out_specs=[pl.BlockSpec((N_DEV, D, in_cols), lambda k: (0, 0, 0))] + [ANY_SPEC] * ns,
        out_shape=[jax.ShapeDtypeStruct((N_DEV, D, in_cols), BF16)]
        + [jax.ShapeDtypeStruct((N_DEV,) + v.shape, v.dtype) for v in shards],
        scratch_shapes=[pltpu.VMEM((N_DEV, D, DW_IN_WIN), F32)] + _gather_scratch(shards),
        compiler_params=_cparams(),
    )(u_b, dpb, *shards)


def _dw_blocks(a, b, name, by_rows, per_step=1, after=()):
    T, M = a.shape
    N = b.shape[1]
    bk = min(2048, T)
    nk = T // bk
    whole = pl.BlockSpec(memory_space=pltpu.VMEM)
    if by_rows:
        rows = M // N_DEV
        am = rows * per_step
        nblk = N_DEV // per_step
        a_spec, b_spec = pl.BlockSpec((bk, am), lambda i, k: (k, i)), whole
        out_blk, acc_shape = (per_step, rows, N), (am, N)
    else:
        cols = N // N_DEV
        nblk = N_DEV
        a_spec, b_spec = whole, pl.BlockSpec((bk, cols), lambda i, k: (k, i))
        out_blk, acc_shape = (1, M, cols), (M, cols)

    def body(a_ref, b_ref, *rest):
        o_ref, acc = rest[len(after):]
        k = pl.program_id(1)

        @pl.when(k == 0)
        def _():
            acc[...] = jnp.zeros_like(acc)

        tok = pl.ds(pl.multiple_of(k * bk, bk), bk)
        a_blk = a_ref[...] if by_rows else a_ref[tok, :]
        b_blk = b_ref[tok, :] if by_rows else b_ref[...]
        acc[...] += lax.dot_general(a_blk, b_blk, (((0,), (0,)), ((), ())), preferred_element_type=F32)

        @pl.when(k == nk - 1)
        def _():
            o_ref[...] = acc[...].reshape(out_blk).astype(BF16)

    return pl.pallas_call(
        body, name=name, grid=(nblk, nk), in_specs=[a_spec, b_spec] + [ANY_SPEC] * len(after),
        out_specs=pl.BlockSpec(out_blk, lambda i, k: (i, 0, 0)),
        out_shape=jax.ShapeDtypeStruct((N_DEV,) + out_blk[1:], BF16),
        scratch_shapes=[pltpu.VMEM(acc_shape, F32)],
        compiler_params=_cparams(),
    )(a, b, *after)


def _adam_parts(parts, w, m, v, name):
    rows, cols = w.shape
    br = rows
    for cand in range(rows, 15, -16):
        if rows % cand == 0 and cand * cols * 4 <= ADAM_BLOCK_BYTES:
            br = cand
            break

    def body(p_ref, w_ref, m_ref, v_ref, g_out, dl_out, m_out, v_out):
        g = p_ref[0].astype(F32)
        for k in range(1, N_DEV):
            g = g + p_ref[k].astype(F32)
        g_out[...] = g
        dl, mn, vn = _adam_math(w_ref[...], g, m_ref[...], v_ref[...])
        dl_out[...] = dl
        m_out[...] = mn
        v_out[...] = vn

    wspec = pl.BlockSpec((br, cols), lambda i: (i, 0))
    return pl.pallas_call(
        body, name=name, grid=(rows // br,),
        in_specs=[pl.BlockSpec((N_DEV, br, cols), lambda i: (0, i, 0)), wspec, wspec, wspec],
        out_specs=[wspec] * 4, out_shape=[jax.ShapeDtypeStruct((rows, cols), F32)] * 4,
        compiler_params=_cparams(),
    )(parts, w, m, v)


def _adam_plain(g, w, m, v, name):
    def body(g_ref, w_ref, m_ref, v_ref, dl_out, m_out, v_out):
        dl, mn, vn = _adam_math(w_ref[...], g_ref[...], m_ref[...], v_ref[...])
        dl_out[...] = dl
        m_out[...] = mn
        v_out[...] = vn

    return pl.pallas_call(body, name=name, out_shape=[jax.ShapeDtypeStruct(w.shape, F32)] * 3,
                          compiler_params=_cparams())(g, w, m, v)


SMALL_PARAMS = ("b_ada", "g_mix", "conv_b", "dt_bias", "a_log", "d_skip", "g_ssd", "pool_scale", "g_mlp", "g_final")


def _small_adam(gathered, params):
    n_par = len(SMALL_PARAMS)
    nb = gathered[0].shape[1]

    def body(*refs):
        dmod_ref, acc_ref, conv_ref, vec_ref, hd_ref = refs[:5]
        par_refs = refs[5:5 + 3 * n_par]
        out_refs = refs[5 + 3 * n_par:5 + 7 * n_par]
        cw_out, acc_out = refs[5 + 7 * n_par:]

        def total(ref):
            t = ref[0]
            for k in range(1, N_DEV):
                t = t + ref[k]
            return t

        dm = total(dmod_ref)
        dmb = dm[0]
        for b in range(1, nb):
            dmb = dmb + dm[b]
        ac, cv, vc, hd = total(acc_ref), total(conv_ref), total(vec_ref), total(hd_ref)
        cw_out[...] = cv[0:4]
        acc_out[...] = ac
        grads = {
            "b_ada": jnp.concatenate([dmb[r:r + 1] for r in range(6)], axis=1), "g_mix": ac[3:4], "conv_b": cv[4:5],
            "dt_bias": hd[0:1, 0:N_HEADS], "a_log": hd[1:2, 0:N_HEADS], "d_skip": hd[2:3, 0:N_HEADS],
            "g_ssd": vc[0:1], "pool_scale": vc[1:2, 0:POOL_W], "g_mlp": ac[0:1], "g_final": ac[1:2],
        }
        for i, name in enumerate(SMALL_PARAMS):
            w_ref, m_ref, v_ref = par_refs[3 * i:3 * i + 3]
            g = grads[name]
            dl, mn, vn = _adam_math(w_ref[...], g, m_ref[...], v_ref[...])
            g_o, d_o, m_o, v_o = out_refs[4 * i:4 * i + 4]
            g_o[...] = g
            d_o[...] = dl
            m_o[...] = mn
            v_o[...] = vn

    flat = [a for name in SMALL_PARAMS for a in params[name]]
    out_shape = [jax.ShapeDtypeStruct(params[name][0].shape, F32) for name in SMALL_PARAMS for _ in range(4)]
    out_shape += [jax.ShapeDtypeStruct((4, CONV_CH), F32), jax.ShapeDtypeStruct((8, D), F32)]
    return pl.pallas_call(body, name="small_adam", out_shape=out_shape, compiler_params=_cparams())(*gathered, *flat)


def kernel(x, c, w_ada, b_ada, g_mix, w_in, conv_w, conv_b, dt_bias, a_log, d_skip, g_ssd, w_pool, pool_scale, w_out, g_mlp, w_up, w_down, g_final, loss_target, m_w_ada, m_b_ada, m_g_mix, m_w_in, m_conv_w, m_conv_b, m_dt_bias, m_a_log, m_d_skip, m_g_ssd, m_w_pool, m_pool_scale, m_w_out, m_g_mlp, m_w_up, m_w_down, m_g_final, v_w_ada, v_b_ada, v_g_mix, v_w_in, v_conv_w, v_conv_b, v_dt_bias, v_a_log, v_d_skip, v_g_ssd, v_w_pool, v_pool_scale, v_w_out, v_g_mlp, v_w_up, v_w_down, v_g_final):
    nb, seq, _ = x.shape
    T = nb * seq
    me = 4 * lax.axis_index("x") + 2 * lax.axis_index("y") + lax.axis_index("c")
    in_cols = w_in.shape[2]
    ada_cols = w_ada.shape[2]
    cw_cols = conv_w.shape[2]

    c_g, cw_g, win_g = _all_gather([c, conv_w[0], w_in[0].astype(BF16)], "ag_first")
    c_all = c_g.reshape(N_DEV * nb, D)
    cw_full = cw_g.transpose(1, 0, 2).reshape(4, CONV_CH)

    b_slice = lax.dynamic_slice(b_ada, (0, me * ada_cols), (1, ada_cols))
    mod_cols = _ada_fwd(c_all, w_ada[0], b_slice)
    (mod_g,) = _all_gather([mod_cols], "ag_mod")
    mod_all = mod_g.transpose(1, 0, 2).reshape(N_DEV * nb, 6, D)
    mod_mine = lax.dynamic_slice(mod_all, (me * nb, 0, 0), (nb, 6, D))
    mod = jnp.pad(mod_mine, ((0, 0), (0, 2), (0, 0)))

    x2 = x.reshape(T, D)
    tg2 = loss_target.reshape(T, D)
    heads = jnp.pad(jnp.concatenate([dt_bias, a_log, d_skip], axis=0), ((0, 5), (0, LANES - N_HEADS)))
    wpool_b = w_pool[0]
    u_b, pm, w_cat, wup_g = _mix_in(x2, mod, g_mix, win_g, seq, [w_up[0].astype(BF16)])
    ymix, hstates, cvs, wout_g, wdn_g = _mixer_fwd(
        pm, cw_full, conv_b, heads, g_ssd, wpool_b, pool_scale, nb, seq,
        [w_out[0].astype(BF16), w_down[0].astype(BF16)])
    da_b, dym, dh1, u2_b, f_b, dup_b, ddn_b, dmod_a, acc_a = _mlp_fused(
        x2, ymix, tg2, mod, g_mlp, g_final.reshape(1, D), wout_g.reshape(MIX_W, D), wup_g, wdn_g, seq)

    gout_p = _dw_blocks(ymix, da_b, "dw_out", True, per_step=4)
    ex_out = _exchange_start(gout_p, "gout_start")
    gup_p = _dw_blocks(u2_b, dup_b, "dw_up", False, after=[ex_out[4]])
    ex_up = _exchange_start(gup_p, "gup_start")
    gdn_p = _dw_blocks(f_b, ddn_b, "dw_down", True, after=[ex_up[4]])
    ex_dn = _exchange_start(gdn_p, "gdn_start")
    dpb, d_conv, d_heads, d_vec, d_wpool = _mixer_bwd(
        pm, cvs, dym, hstates, cw_full, conv_b, heads, g_ssd, wpool_b, pool_scale, nb, seq, after=[ex_dn[4]])
    gin_p, conv_g, vec_g, heads_g, wpool_parts = _dw_in(
        u_b, dpb, in_cols, [d_conv, d_vec, d_heads, d_wpool.reshape(4 * LANES, LANES)])
    ex_in = _exchange_start(gin_p, "gin_start")
    grad_x2, dmod, acc = _in_bwd(x2, dh1, dpb, mod, g_mix + ex_in[4][0:1, 0:1], w_cat, dmod_a, acc_a, seq)

    def landed(ex, after, name):
        own, land = _exchange_wait(ex[0], ex[1], ex[2], ex[3], after, name)
        mine = lax.dynamic_slice(own, (me, 0, 0), (1,) + own.shape[1:])
        return lax.dynamic_update_slice(land, mine, (me, 0, 0))

    gout_r, gup_r, gdn_r = landed(ex_out, dmod, "gout_wait"), landed(ex_up, dmod, "gup_wait"), landed(ex_dn, dmod, "gdn_wait")
    g_out, d_out, nm_out, nv_out = _adam_parts(gout_r, w_out[0], m_w_out[0], v_w_out[0], "adam_w_out")
    g_up, d_up, nm_up, nv_up = _adam_parts(gup_r, w_up[0], m_w_up[0], v_w_up[0], "adam_w_up")
    g_dn, d_dn, nm_dn, nv_dn = _adam_parts(gdn_r, w_down[0], m_w_down[0], v_w_down[0], "adam_w_down")

    dmod_g, acc_g = _all_gather([dmod, acc], "ag_small_bwd", after=[nm_out, nm_up, nm_dn])
    pool2 = (4 * LANES, LANES)
    wpool_outs = _adam_parts(wpool_parts, w_pool.reshape(pool2), m_w_pool.reshape(pool2), v_w_pool.reshape(pool2),
                             "adam_w_pool")
    small_params = {
        "b_ada": (b_ada, m_b_ada, v_b_ada), "g_mix": (g_mix, m_g_mix, v_g_mix), "conv_b": (conv_b, m_conv_b, v_conv_b),
        "dt_bias": (dt_bias, m_dt_bias, v_dt_bias), "a_log": (a_log, m_a_log, v_a_log),
        "d_skip": (d_skip, m_d_skip, v_d_skip), "g_ssd": (g_ssd, m_g_ssd, v_g_ssd),
        "pool_scale": (pool_scale, m_pool_scale, v_pool_scale), "g_mlp": (g_mlp, m_g_mlp, v_g_mlp),
        "g_final": tuple(a.reshape(1, D) for a in (g_final, m_g_final, v_g_final)),
    }
    small_res = _small_adam([dmod_g, acc_g, conv_g, vec_g, heads_g], small_params)
    g_cw_full, acc_sum = small_res[-2:]
    loss = acc_sum[2, 0]

    g_cw = lax.dynamic_slice(g_cw_full, (0, me * cw_cols), (4, cw_cols))
    d_cwp, nm_cwp, nv_cwp = _adam_plain(g_cw, conv_w[0], m_conv_w[0], v_conv_w[0], "adam_conv_w")

    dmod_all = dmod_g[:, :, 0:6].reshape(N_DEV * nb, 6 * D)
    dmod_slice = lax.dynamic_slice(dmod_all, (0, me * ada_cols), (N_DEV * nb, ada_cols))
    g_ada, d_ada, nm_ada, nv_ada = _ada_bwd_adam(c_all, dmod_slice, w_ada[0], m_w_ada[0], v_w_ada[0])

    ex_after = nm_ada[0:8, 0:LANES] + acc_sum[:, 0:LANES]
    gin_r = landed(ex_in, ex_after, "gin_wait")
    g_in, d_in, nm_in, nv_in = _adam_parts(gin_r, w_in[0], m_w_in[0], v_w_in[0], "adam_w_in")

    def small_outs(kind, wpool):
        res = {name: small_res[4 * i + kind] for i, name in enumerate(SMALL_PARAMS)}
        res["g_final"] = res["g_final"].reshape(D)
        res["w_pool"] = wpool.reshape(1, 4, LANES, LANES)
        return res

    def big_outs(ada, win, cwp, wout, wup, wdn):
        return {"w_ada": ada[None], "w_in": win.reshape(1, D, in_cols), "conv_w": cwp[None], "w_out": wout[None],
                "w_up": wup[None], "w_down": wdn[None]}

    order = ["w_ada", "b_ada", "g_mix", "w_in", "conv_w", "conv_b", "dt_bias", "a_log", "d_skip", "g_ssd", "w_pool",
             "pool_scale", "w_out", "g_mlp", "w_up", "w_down", "g_final"]
    groups = [
        {**small_outs(0, wpool_outs[0]), **big_outs(g_ada, g_in, g_cw, g_out, g_up, g_dn)},
        {**small_outs(1, wpool_outs[1]), **big_outs(d_ada, d_in, d_cwp, d_out, d_up, d_dn)},
        {**small_outs(2, wpool_outs[2]), **big_outs(nm_ada, nm_in, nm_cwp, nm_out, nm_up, nm_dn)},
        {**small_outs(3, wpool_outs[3]), **big_outs(nv_ada, nv_in, nv_cwp, nv_out, nv_up, nv_dn)},
    ]
    outs = [loss, grad_x2.reshape(nb, seq, D)]
    for grp in groups:
        outs += [grp[n] for n in order]
    return tuple(outs)
```

```python
import functools

import jax
import jax.numpy as jnp
from jax import lax
from jax.experimental import pallas as pl
from jax.experimental.pallas import tpu as pltpu

F32, BF16 = jnp.float32, jnp.bfloat16
MESH = pl.DeviceIdType.MESH
N_DEV = 8
D = 1024
LANES = 128
CHUNK = 128
POOL_W = 512
WINDOWS = (2, 4, 8, 16)
N_HEADS = 16
HEAD_DIM = 64
N_GROUPS = 2
GROUP_W = 512
N_STATE = 128
CONV_CH = 1536
OFF_Z, OFF_XBC, OFF_DT, IN_W = 512, 1536, 3072, 3088
PROJ_W = OFF_DT + LANES
MIX_W = 1536
D_FF = 4096
FF_BLK = 512
EPS = 1e-5
LR, B1, B2, AEPS, WD, STEP = 0.001, 0.9, 0.999, 1e-08, 0.01, 10
POOL_HALO = 16
CONV_HALO = 8
VMEM_LIMIT = 56 << 20
ADAM_BLOCK_BYTES = 1 << 20
DW_IN_WIN = 512


def _cparams(**kw):
    return pltpu.CompilerParams(vmem_limit_bytes=VMEM_LIMIT, **kw)


def _mm(a, b):
    return jnp.dot(a.astype(BF16), b.astype(BF16), preferred_element_type=F32)


def _mm_nt(a, b):
    return lax.dot_general(a.astype(BF16), b.astype(BF16), (((1,), (1,)), ((), ())), preferred_element_type=F32)


def _mm_tn(a, b):
    return lax.dot_general(a.astype(BF16), b.astype(BF16), (((0,), (0,)), ((), ())), preferred_element_type=F32)


def _split_bf16(v, terms):
    parts, rest = [], v
    for t in range(terms):
        p = rest.astype(BF16)
        parts.append(p)
        if t + 1 < terms:
            rest = rest - p.astype(F32)
    return parts


def _dot01(a, b, terms, split_lhs=True):
    if split_lhs:
        bb = b.astype(BF16)
        prods = [jnp.dot(p, bb, preferred_element_type=F32) for p in _split_bf16(a, terms)]
    else:
        ab = a.astype(BF16)
        prods = [jnp.dot(ab, p, preferred_element_type=F32) for p in _split_bf16(b, terms)]
    out = prods[0]
    for q in prods[1:]:
        out = out + q
    return out


def _sigmoid(v):
    return 0.5 * jnp.tanh(0.5 * v) + 0.5


def _expand_mat():
    r = lax.broadcasted_iota(jnp.int32, (LANES, D), 0)
    c = lax.broadcasted_iota(jnp.int32, (LANES, D), 1)
    return (r == c // HEAD_DIM).astype(F32)


def _reduce_mat():
    r = lax.broadcasted_iota(jnp.int32, (D, LANES), 0)
    c = lax.broadcasted_iota(jnp.int32, (D, LANES), 1)
    return (c == r // HEAD_DIM).astype(F32)


def _pos():
    return lax.axis_index("x"), lax.axis_index("y"), lax.axis_index("c")


GATHER_PIECES = 4
GATHER_PIECE_BYTES = 128 << 10


def _pieces(shape, dtype):
    rows = shape[0]
    size = jnp.dtype(dtype).itemsize
    for d in shape:
        size *= d
    whole_tiles = rows % (GATHER_PIECES * 16) == 0
    return GATHER_PIECES if whole_tiles and size // GATHER_PIECES >= GATHER_PIECE_BYTES else 1


class _Gather:
    def __init__(self, x_refs, o_refs, send, recv, loc):
        self.x_refs, self.o_refs, self.send, self.recv, self.loc = x_refs, o_refs, send, recv, loc
        self.n = len(x_refs)
        self.pieces = [_pieces(r.shape, r.dtype) for r in x_refs]
        self.base = [7 * sum(self.pieces[:a]) for a in range(self.n)]
        x, y, c = _pos()
        self.c = c
        self.me, self.sib = (x, y, c), (x, y, 1 - c)
        self.chips = [(1 - x, y), (x, 1 - y), (1 - x, 1 - y)]

    def _rows(self, a, p):
        rows = self.x_refs[a].shape[0] // self.pieces[a]
        return pl.ds(p * rows, rows)

    def _cp(self, a, p, k, block, to, own=False):
        dst = self.o_refs[a].at[4 * block[0] + 2 * block[1] + block[2], self._rows(a, p)]
        sem = self.base[a] + 7 * p + k
        return pltpu.make_async_remote_copy(
            src_ref=self.x_refs[a].at[self._rows(a, p)] if own else dst, dst_ref=dst,
            send_sem=self.send.at[sem], recv_sem=self.recv.at[sem], device_id=to, device_id_type=MESH)

    def _mine(self, a):
        me = self.me
        return pltpu.make_async_copy(self.x_refs[a], self.o_refs[a].at[4 * me[0] + 2 * me[1] + me[2]], self.loc.at[a])

    def _first(self, a, p):
        cps = [self._cp(a, p, 0, self.me, self.sib, own=True)]
        return cps + [self._cp(a, p, 1 + j, self.me, (*chip, self.c), own=True) for j, chip in enumerate(self.chips)]

    def _passed(self, a, p, j):
        return self._cp(a, p, 4 + j, (*self.chips[j], self.c), self.sib)

    def start(self):
        for a in range(self.n):
            self._mine(a).start()
        for p in range(max(self.pieces)):
            for a in range(self.n):
                if p < self.pieces[a]:
                    for cp in self._first(a, p):
                        cp.start()

    def forward(self, p):
        for j, chip in enumerate(self.chips):
            for a in range(self.n):
                if p < self.pieces[a]:
                    self._cp(a, p, 1 + j, (*chip, self.c), self.me).wait_recv()
                    self._passed(a, p, j).start()

    def finish(self):
        for a in range(self.n):
            for p in range(self.pieces[a]):
                self._cp(a, p, 0, self.sib, self.me).wait_recv()
                for j, chip in enumerate(self.chips):
                    self._cp(a, p, 4 + j, (*chip, 1 - self.c), self.me).wait_recv()
        for a in range(self.n):
            for p in range(self.pieces[a]):
                for cp in self._first(a, p):
                    cp.wait_send()
                for j in range(3):
                    self._passed(a, p, j).wait_send()
            self._mine(a).wait()

    def begin_hosted(self, step, steps):
        @pl.when(step == 0)
        def _():
            self.start()

        n_p = max(self.pieces)
        for p in range(n_p):
            @pl.when(step == min(((p + 1) * 7 * steps) // (8 * n_p), steps - 1))
            def _():
                self.forward(p)

    def end_hosted(self, step, steps):
        @pl.when(step == steps - 1)
        def _():
            self.finish()


class _Exchange:
    def __init__(self, x_refs, o_refs, send, recv, loc):
        self.x_refs, self.o_refs, self.send, self.recv, self.loc = x_refs, o_refs, send, recv, loc
        self.n = len(x_refs)
        x, y, c = _pos()
        self.me_i = 4 * x + 2 * y + c
        self.peers = []
        for k in range(1, N_DEV):
            px = 1 - x if (k >> 2) & 1 else x
            py = 1 - y if (k >> 1) & 1 else y
            pc = 1 - c if k & 1 else c
            self.peers.append(((px, py, pc), 4 * px + 2 * py + pc))

    def _mine(self, a):
        return pltpu.make_async_copy(self.x_refs[a].at[self.me_i], self.o_refs[a].at[self.me_i], self.loc.at[a])

    def _cp(self, a, k, landing):
        peer, peer_i = self.peers[k]
        return pltpu.make_async_remote_copy(
            src_ref=self.x_refs[a].at[peer_i], dst_ref=self.o_refs[a].at[landing],
            send_sem=self.send.at[a * 7 + k], recv_sem=self.recv.at[a * 7 + k],
            device_id=peer, device_id_type=MESH)

    def start(self):
        for a in range(self.n):
            self._mine(a).start()
            for k in range(N_DEV - 1):
                self._cp(a, k, self.me_i).start()

    def finish(self):
        for a in range(self.n):
            for k in range(N_DEV - 1):
                self._cp(a, k, self.peers[k][1]).wait_recv()
        for a in range(self.n):
            for k in range(N_DEV - 1):
                self._cp(a, k, self.me_i).wait_send()
            self._mine(a).wait()


def _gather_scratch(xs):
    n_sem = 7 * sum(_pieces(v.shape, v.dtype) for v in xs)
    return [pltpu.SemaphoreType.DMA((n_sem,)), pltpu.SemaphoreType.DMA((n_sem,)), pltpu.SemaphoreType.DMA((len(xs),))]


ANY_SPEC = pl.BlockSpec(memory_space=pl.ANY)


def _all_gather(xs, name, after=()):
    n, na = len(xs), len(after)

    def body(*refs):
        g = _Gather(refs[:n], refs[n + na:2 * n + na], *refs[2 * n + na:])
        g.start()
        for p in range(max(g.pieces)):
            g.forward(p)
        g.finish()

    return pl.pallas_call(
        body, name=name,
        out_shape=[jax.ShapeDtypeStruct((N_DEV,) + v.shape, v.dtype) for v in xs],
        in_specs=[ANY_SPEC] * (n + na), out_specs=[ANY_SPEC] * n, scratch_shapes=_gather_scratch(xs),
    )(*xs, *after)


HBM_SPEC = pl.BlockSpec(memory_space=pltpu.HBM)
SEM_SPEC = pl.BlockSpec(memory_space=pltpu.SEMAPHORE)
VMEM_SPEC = pl.BlockSpec(memory_space=pltpu.VMEM)
SPLIT_EFFECT = pltpu.SideEffectType.DATAFLOW_SIDE_EFFECTING


def _in_hbm(v):
    return pltpu.with_memory_space_constraint(v, pltpu.HBM)


def _exchange_start(blocks, name):
    def body(x_ref, land_ref, send, recv, x_thru, land_thru, token):
        ex = _Exchange([x_ref], [land_ref], send, recv, None)
        for k in range(N_DEV - 1):
            ex._cp(0, k, ex.me_i).start()
        token[...] = jnp.zeros_like(token)

    hbm = pltpu.HBM(blocks.shape, blocks.dtype)
    return pl.pallas_call(
        body, name=name,
        out_shape=(pltpu.SemaphoreType.DMA((N_DEV - 1,)), pltpu.SemaphoreType.DMA((N_DEV - 1,)), hbm, hbm,
                   jax.ShapeDtypeStruct((8, LANES), F32)),
        in_specs=(HBM_SPEC, HBM_SPEC), out_specs=(SEM_SPEC, SEM_SPEC, HBM_SPEC, HBM_SPEC, VMEM_SPEC),
        input_output_aliases={0: 2, 1: 3},
        compiler_params=pltpu.CompilerParams(has_side_effects=SPLIT_EFFECT),
    )(_in_hbm(blocks), _in_hbm(lax.empty(blocks.shape, blocks.dtype)))


def _exchange_wait(send, recv, x_thru, land_thru, after, name):
    def body(x_ref, land_ref, send_ref, recv_ref, after_ref, x_dead, got_ref):
        ex = _Exchange([x_ref], [land_ref], send_ref, recv_ref, None)
        for k in range(N_DEV - 1):
            ex._cp(0, k, ex.me_i).wait_send()
            ex._cp(0, k, ex.peers[k][1]).wait_recv()

    hbm = pltpu.HBM(x_thru.shape, x_thru.dtype)
    return pl.pallas_call(
        body, name=name, out_shape=(hbm, hbm),
        in_specs=(HBM_SPEC, HBM_SPEC, SEM_SPEC, SEM_SPEC, ANY_SPEC), out_specs=(HBM_SPEC, HBM_SPEC),
        input_output_aliases={0: 0, 1: 1},
        compiler_params=pltpu.CompilerParams(has_side_effects=SPLIT_EFFECT),
    )(x_thru, land_thru, send, recv, after)


def _ada_fwd(c_all, w_ada, b_slice):
    def body(c_ref, w_ref, b_ref, o_ref):
        cv = c_ref[...]
        act = cv * _sigmoid(cv)
        o_ref[...] = _mm(act, w_ref[...]) + b_ref[...]

    nb, nc = c_all.shape[0], w_ada.shape[1]
    return pl.pallas_call(body, name="ada_fwd", out_shape=jax.ShapeDtypeStruct((nb, nc), F32),
                          compiler_params=_cparams())(c_all, w_ada, b_slice)


def _adam_math(w, g, m, v):
    m = B1 * m + (1.0 - B1) * g
    v = B2 * v + (1.0 - B2) * jnp.square(g)
    m_hat = m / (1.0 - B1 ** STEP)
    v_hat = v / (1.0 - B2 ** STEP)
    delta = -LR * (m_hat / (jnp.sqrt(v_hat) + AEPS) + WD * w)
    return delta, m, v


def _ada_bwd_adam(c_all, dmod_slice, w, m, v):
    rows, cols = w.shape
    br = 256

    def body(c_ref, d_ref, w_ref, m_ref, v_ref, g_out, dl_out, m_out, v_out):
        cv = c_ref[...]
        act = cv * _sigmoid(cv)
        g = _mm_tn(act, d_ref[...])
        g_out[...] = g
        dl, mn, vn = _adam_math(w_ref[...], g, m_ref[...], v_ref[...])
        dl_out[...] = dl
        m_out[...] = mn
        v_out[...] = vn

    nb = c_all.shape[0]
    wspec = pl.BlockSpec((br, cols), lambda i: (i, 0))
    return pl.pallas_call(
        body, name="ada_bwd_adam", grid=(rows // br,),
        in_specs=[pl.BlockSpec((nb, br), lambda i: (0, i)), pl.BlockSpec((nb, cols), lambda i: (0, 0)),
                  wspec, wspec, wspec],
        out_specs=[wspec] * 4, out_shape=[jax.ShapeDtypeStruct((rows, cols), F32)] * 4,
        compiler_params=_cparams(),
    )(c_all, dmod_slice, w, m, v)


def _mix_in(x2, mod, g_mix, win_g, seq, shards):
    T = x2.shape[0]
    tm = min(512, seq)
    tps = seq // tm
    in_cols = win_g.shape[2]
    ns = len(shards)
    steps = T // tm

    def body(*refs):
        x_ref, mod_ref, g_ref, wb_ref = refs[:4]
        sh_refs = refs[4:4 + ns]
        u_ref, pm_ref, wc_ref = refs[4 + ns:7 + ns]
        ga_refs = refs[7 + ns:7 + 2 * ns]
        w_ref, send, recv, loc = refs[7 + 2 * ns:]
        step = pl.program_id(0)
        gather = _Gather(sh_refs, ga_refs, send, recv, loc)
        gather.begin_hosted(step, steps)

        @pl.when(step == 0)
        def _():
            w_ref[:, OFF_DT:] = jnp.zeros((D, PROJ_W - OFF_DT), BF16)
            for j in range(N_DEV):
                w_ref[:, in_cols * j:in_cols * (j + 1)] = wb_ref[j]
            wc_ref[...] = w_ref[...]

        x = x_ref[...]
        r = lax.rsqrt(jnp.mean(x * x, axis=-1, keepdims=True) + EPS)
        md = mod_ref[0]
        u = (x * r * g_ref[...]) * (1.0 + md[1:2]) + md[0:1]
        ub = u.astype(BF16)
        u_ref[...] = ub
        pm_ref[...] = jnp.dot(ub, w_ref[...], preferred_element_type=F32)
        gather.end_hosted(step, steps)

    whole = pl.BlockSpec(memory_space=pltpu.VMEM)
    return pl.pallas_call(
        body, name="mix_in", grid=(T // tm,),
        in_specs=[pl.BlockSpec((tm, D), lambda i: (i, 0)), pl.BlockSpec((1, 8, D), lambda i: (i // tps, 0, 0)),
                  pl.BlockSpec((1, D), lambda i: (0, 0)), whole] + [ANY_SPEC] * ns,
        out_specs=[pl.BlockSpec((tm, D), lambda i: (i, 0)), pl.BlockSpec((tm, PROJ_W), lambda i: (i, 0)),
                   pl.BlockSpec((D, PROJ_W), lambda i: (0, 0))] + [ANY_SPEC] * ns,
        out_shape=[jax.ShapeDtypeStruct((T, D), BF16), jax.ShapeDtypeStruct((T, PROJ_W), F32),
                   jax.ShapeDtypeStruct((D, PROJ_W), BF16)]
        + [jax.ShapeDtypeStruct((N_DEV,) + v.shape, v.dtype) for v in shards],
        scratch_shapes=[pltpu.VMEM((D, PROJ_W), BF16)] + _gather_scratch(shards),
        compiler_params=_cparams(),
    )(x2, mod, g_mix, win_g, *shards)


def _chunk_forward(up, z, ux, dtin, halo_p, halo_x, hprev, cw, cb, hp, gssd, wpool, pscale, t0, y_scr, cv=None):
    L = CHUNK
    out = {}
    row = lax.broadcasted_iota(jnp.int32, (L, 1), 0)
    t = (t0 + row + 1).astype(F32)
    e = jnp.concatenate([halo_p, up], axis=0)
    s2 = e + pltpu.roll(e, 1, 0)
    s4 = s2 + pltpu.roll(s2, 2, 0)
    s8 = s4 + pltpu.roll(s4, 4, 0)
    s16 = s8 + pltpu.roll(s8, 8, 0)
    sums = (s2, s4, s8, s16)
    p, inv, yp = [], [], []
    for gi, w in enumerate(WINDOWS):
        sl = slice(gi * LANES, (gi + 1) * LANES)
        ic = 1.0 / jnp.minimum(t, float(w))
        pg = sums[gi][POOL_HALO:, sl] * ic - up[:, sl]
        p.append(pg)
        inv.append(ic)
        yp.append(_mm(pg, wpool[gi]))
    out["p"], out["inv"], out["yp"] = p, inv, yp
    out["y_pool"] = jnp.concatenate(yp, axis=1) * pscale
    if cv is None:
        ex = jnp.concatenate([halo_x, ux], axis=0)
        taps = [pltpu.roll(ex, 3, 0)[CONV_HALO:], pltpu.roll(ex, 2, 0)[CONV_HALO:], pltpu.roll(ex, 1, 0)[CONV_HALO:], ux]
        cv = cb + taps[0] * cw[0:1] + taps[1] * cw[1:2] + taps[2] * cw[2:3] + taps[3] * cw[3:4]
    sg = _sigmoid(cv)
    xbc = cv * sg
    out["cv"], out["sg"] = cv, sg
    X = xbc[:, :D]
    Bm = xbc[:, D:D + N_GROUPS * N_STATE]
    Cm = xbc[:, D + N_GROUPS * N_STATE:]
    pre = dtin + hp[0:1]
    dt = jnp.maximum(pre, 0.0) + jnp.log(1.0 + jnp.exp(-jnp.abs(pre)))
    a_row = -jnp.exp(hp[1:2])
    da = dt * a_row
    ri = lax.broadcasted_iota(jnp.int32, (L, L), 0)
    ci = lax.broadcasted_iota(jnp.int32, (L, L), 1)
    causal = ri >= ci
    cum = _dot01(causal.astype(F32), da, 3, split_lhs=False)
    cum_t = cum.T
    cum_last = cum[L - 1:L]
    eo = jnp.exp(cum)
    dec = jnp.exp(cum_last - cum)
    cd = jnp.exp(cum_last)
    exm = _expand_mat()
    rows8 = jnp.concatenate([cd, hp[2:3], jnp.zeros((6, LANES), F32)], axis=0)
    rep = _dot01(jnp.concatenate([dt, eo, dec, rows8], axis=0), exm, 2)
    dt_rep, eo_rep, dec_rep = rep[0:L], rep[L:2 * L], rep[2 * L:3 * L]
    cd_rep, dskip_rep = rep[3 * L:3 * L + 1], rep[3 * L + 1:3 * L + 2]
    xdt = X * dt_rep
    out.update(X=X, Bm=Bm, Cm=Cm, pre=pre, dt=dt, a_row=a_row, cum=cum, cum_t=cum_t, eo=eo, dec=dec, cd=cd,
               dt_rep=dt_rep, eo_rep=eo_rep, dec_rep=dec_rep, cd_rep=cd_rep, dskip_rep=dskip_rep, xdt=xdt,
               causal=causal, anti=(ri <= ci).astype(F32), exm=exm)
    G, lms, yoff, hnew, xdec = [], [], [], [], []
    for g in range(N_GROUPS):
        gs = slice(g * GROUP_W, (g + 1) * GROUP_W)
        Bg = Bm[:, g * N_STATE:(g + 1) * N_STATE]
        Cg = Cm[:, g * N_STATE:(g + 1) * N_STATE]
        Gg = _mm_nt(Cg, Bg)
        G.append(Gg)
        for hh in range(N_HEADS // N_GROUPS):
            h = g * (N_HEADS // N_GROUPS) + hh
            seg = cum[:, h:h + 1] - cum_t[h:h + 1, :]
            lm = jnp.where(causal, jnp.exp(jnp.minimum(seg, 0.0)), 0.0)
            lms.append(lm)
            hs = slice(h * HEAD_DIM, (h + 1) * HEAD_DIM)
            y_scr[:, hs] = _mm(Gg * lm, xdt[:, hs])
        xd = xdt[:, gs] * dec_rep[:, gs]
        xdec.append(xd)
        sgm = _mm_tn(Bg, xd)
        yoff.append(_mm(Cg, hprev[g]) * eo_rep[:, gs])
        hnew.append(hprev[g] * cd_rep[:, gs] + sgm)
    out.update(G=G, lms=lms, yoff=yoff, hnew=hnew, xdec=xdec)
    y = y_scr[...] + jnp.concatenate(yoff, axis=1) + dskip_rep * X
    sz = _sigmoid(z)
    silz = z * sz
    yz = y * silz
    rg, yn = [], []
    for g in range(N_GROUPS):
        gs = slice(g * GROUP_W, (g + 1) * GROUP_W)
        r = lax.rsqrt(jnp.mean(yz[:, gs] * yz[:, gs], axis=-1, keepdims=True) + EPS)
        rg.append(r)
        yn.append(yz[:, gs] * r)
    yn = jnp.concatenate(yn, axis=1)
    out.update(y=y, sz=sz, silz=silz, rg=rg, yn=yn)
    out["y_ssd"] = yn * gssd
    return out


def _mixer_fwd(pm, cw, cb, hp, gssd, wpool, pscale, nb, seq, shards):
    nc = seq // CHUNK
    ns = len(shards)
    steps = nb * nc

    def body(*refs):
        pm_ref, cw_ref, cb_ref, hp_ref, gs_ref, wp_ref, ps_ref = refs[:7]
        sh_refs = refs[7:7 + ns]
        ym_ref, hs_ref, cv_ref = refs[7 + ns:10 + ns]
        ga_refs = refs[10 + ns:10 + 2 * ns]
        halo_p, halo_x, state, y_scr, send, recv, loc = refs[10 + 2 * ns:]
        c = pl.program_id(1)
        step = pl.program_id(0) * nc + c
        gather = _Gather(sh_refs, ga_refs, send, recv, loc)
        gather.begin_hosted(step, steps)

        @pl.when(c == 0)
        def _():
            halo_p[...] = jnp.zeros_like(halo_p)
            halo_x[...] = jnp.zeros_like(halo_x)
            state[...] = jnp.zeros_like(state)

        up = pm_ref[:, 0:POOL_W]
        z = pm_ref[:, OFF_Z:OFF_XBC]
        ux = pm_ref[:, OFF_XBC:OFF_DT]
        hprev = [state[0], state[1]]
        hs_ref[0, 0, 0] = hprev[0]
        hs_ref[0, 0, 1] = hprev[1]
        o = _chunk_forward(up, z, ux, pm_ref[:, OFF_DT:], halo_p[...], halo_x[...], hprev, cw_ref[...], cb_ref[...],
                           hp_ref[...], gs_ref[...], wp_ref[...], ps_ref[...], c * CHUNK, y_scr)
        ym_ref[:, 0:POOL_W] = o["y_pool"].astype(BF16)
        ym_ref[:, POOL_W:] = o["y_ssd"].astype(BF16)
        cv_ref[...] = o["cv"]
        state[0] = o["hnew"][0]
        state[1] = o["hnew"][1]
        halo_p[...] = up[CHUNK - POOL_HALO:]
        halo_x[...] = ux[CHUNK - CONV_HALO:]
        gather.end_hosted(step, steps)

    def full(shape):
        return pl.BlockSpec(shape, lambda b, c: (0,) * len(shape))

    T = nb * seq
    return pl.pallas_call(
        body, name="mixer_fwd", grid=(nb, nc),
        in_specs=[pl.BlockSpec((CHUNK, PROJ_W), lambda b, c: (b * nc + c, 0)),
                  full((4, CONV_CH)), full((1, CONV_CH)), full((8, LANES)), full((1, D)),
                  full((4, LANES, LANES)), full((1, POOL_W))] + [ANY_SPEC] * ns,
        out_specs=[pl.BlockSpec((CHUNK, MIX_W), lambda b, c: (b * nc + c, 0)),
                   pl.BlockSpec((1, 1, N_GROUPS, N_STATE, GROUP_W), lambda b, c: (b, c, 0, 0, 0)),
                   pl.BlockSpec((CHUNK, CONV_CH), lambda b, c: (b * nc + c, 0))] + [ANY_SPEC] * ns,
        out_shape=[jax.ShapeDtypeStruct((T, MIX_W), BF16),
                   jax.ShapeDtypeStruct((nb, nc, N_GROUPS, N_STATE, GROUP_W), F32),
                   jax.ShapeDtypeStruct((T, CONV_CH), F32)]
        + [jax.ShapeDtypeStruct((N_DEV,) + v.shape, v.dtype) for v in shards],
        scratch_shapes=[pltpu.VMEM((POOL_HALO, POOL_W), F32), pltpu.VMEM((CONV_HALO, CONV_CH), F32),
                        pltpu.VMEM((N_GROUPS, N_STATE, GROUP_W), F32), pltpu.VMEM((CHUNK, D), F32)] + _gather_scratch(shards),
        compiler_params=_cparams(),
    )(pm, cw, cb, hp, gssd, wpool, pscale, *shards)


def _mixer_bwd(pm, cvs, dym, hstates, cw, cb, hp, gssd, wpool, pscale, nb, seq, after=()):
    nc = seq // CHUNK
    hpg = N_HEADS // N_GROUPS
    na = len(after)

    def body(*refs):
        (pm_ref, hpool_ref, cv_ref, dy_ref, hs_ref, cw_ref, cb_ref, hp_ref, gs_ref, wp_ref, ps_ref) = refs[:11]
        dpm_ref, dconv_ref, dhp_ref, dvec_ref, dwp_ref = refs[11 + na:16 + na]
        nxt_q, nxt_cv, rstate, y_scr, dx_scr = refs[16 + na:]
        b = pl.program_id(0)
        ci = pl.program_id(1)
        c = nc - 1 - ci

        @pl.when((b == 0) & (ci == 0))
        def _():
            for r in (dconv_ref, dhp_ref, dvec_ref, dwp_ref):
                r[...] = jnp.zeros_like(r)

        @pl.when(ci == 0)
        def _():
            nxt_q[...] = jnp.zeros_like(nxt_q)
            nxt_cv[...] = jnp.zeros_like(nxt_cv)
            rstate[...] = jnp.zeros_like(rstate)

        first = (c > 0).astype(F32)
        up = pm_ref[:, 0:POOL_W]
        z = pm_ref[:, OFF_Z:OFF_XBC]
        ux = pm_ref[:, OFF_XBC:OFF_DT]
        halo_p = hpool_ref[...] * first
        hprev = [hs_ref[0, 0, 0], hs_ref[0, 0, 1]]
        cw, cb, hp, gssd, wpool, pscale = cw_ref[...], cb_ref[...], hp_ref[...], gs_ref[...], wp_ref[...], ps_ref[...]
        o = _chunk_forward(up, z, ux, pm_ref[:, OFF_DT:], halo_p, None, hprev, cw, cb, hp, gssd, wpool, pscale,
                           c * CHUNK, y_scr, cv=cv_ref[...])
        L = CHUNK
        dy_pool = dy_ref[:, 0:POOL_W].astype(F32)
        dy_ssd = dy_ref[:, POOL_W:].astype(F32)

        dvec_ref[1:2, 0:POOL_W] += jnp.sum(dy_pool * jnp.concatenate(o["yp"], axis=1), axis=0, keepdims=True)
        dyp = dy_pool * pscale
        qs = []
        dps = []
        for gi in range(len(WINDOWS)):
            sl = slice(gi * LANES, (gi + 1) * LANES)
            dwp_ref[gi] += _mm_tn(o["p"][gi], dyp[:, sl])
            dpg = _mm_nt(dyp[:, sl], wpool[gi])
            dps.append(dpg)
            qs.append(dpg * o["inv"][gi])
        q = jnp.concatenate(qs, axis=1)
        e = jnp.concatenate([q, nxt_q[...]], axis=0)
        n = L + POOL_HALO
        s2 = e + pltpu.roll(e, n - 1, 0)
        s4 = s2 + pltpu.roll(s2, n - 2, 0)
        s8 = s4 + pltpu.roll(s4, n - 4, 0)
        s16 = s8 + pltpu.roll(s8, n - 8, 0)
        sums = (s2, s4, s8, s16)
        for gi in range(len(WINDOWS)):
            sl = slice(gi * LANES, (gi + 1) * LANES)
            dpm_ref[:, sl] = (sums[gi][:L, sl] - dps[gi]).astype(BF16)
        nxt_q[...] = q[:POOL_HALO]

        yn, y, silz, sz = o["yn"], o["y"], o["silz"], o["sz"]
        dvec_ref[0:1] += jnp.sum(dy_ssd * yn, axis=0, keepdims=True)
        dyn = dy_ssd * gssd
        dyz = []
        for g in range(N_GROUPS):
            gs = slice(g * GROUP_W, (g + 1) * GROUP_W)
            mean = jnp.mean(dyn[:, gs] * yn[:, gs], axis=-1, keepdims=True)
            dyz.append(o["rg"][g] * (dyn[:, gs] - yn[:, gs] * mean))
        dyz = jnp.concatenate(dyz, axis=1)
        dyv = dyz * silz
        dpm_ref[:, OFF_Z:OFF_XBC] = (dyz * y * (sz * (1.0 + z * (1.0 - sz)))).astype(BF16)

        X, Bm, Cm, xdt = o["X"], o["Bm"], o["Cm"], o["xdt"]
        exm = o["exm"]
        rdm = _reduce_mat()
        lane = lax.broadcasted_iota(jnp.int32, (1, LANES), 1)
        sub = lax.broadcasted_iota(jnp.int32, (LANES, 1), 0)
        dX = o["dskip_rep"] * dyv
        yoff_full = jnp.concatenate(o["yoff"], axis=1)
        rs = jnp.zeros((L, LANES), F32)
        cs_t = jnp.zeros((LANES, L), F32)
        dBs, dCs = [], []
        rh_sums = []
        ddec = []
        for g in range(N_GROUPS):
            gs = slice(g * GROUP_W, (g + 1) * GROUP_W)
            Bg = Bm[:, g * N_STATE:(g + 1) * N_STATE]
            Cg = Cm[:, g * N_STATE:(g + 1) * N_STATE]
            Gg = o["G"][g]
            R = rstate[g]
            dwm = dyv[:, gs] * o["eo_rep"][:, gs]
            dC = _mm_nt(dwm, hprev[g])
            dH = _mm_tn(Cg, dwm)
            dG = jnp.zeros((L, L), F32)
            for hh in range(hpg):
                h = g * hpg + hh
                hs = slice(h * HEAD_DIM, (h + 1) * HEAD_DIM)
                lm = o["lms"][h]
                m_h = Gg * lm
                dM = _mm_nt(dyv[:, hs], xdt[:, hs])
                dx_scr[:, hs] = _mm_tn(m_h, dyv[:, hs])
                qm = dM * m_h
                rs = rs + jnp.sum(qm, axis=1, keepdims=True) * (lane == h).astype(F32)
                cs_t = cs_t + (sub == h).astype(F32) * jnp.sum(qm, axis=0, keepdims=True)
                dG = dG + dM * lm
            dC = dC + _mm(dG, Bg)
            dB = _mm_tn(dG, Cg)
            zx = _mm(Bg, R)
            dxdt_state = zx * o["dec_rep"][:, gs]
            ddec.append(zx * xdt[:, gs])
            dB = dB + _mm_nt(o["xdec"][g], R)
            rh_sums.append(jnp.sum(R * hprev[g], axis=0, keepdims=True))
            rstate[g] = dH + o["cd_rep"][:, gs] * R
            dx_scr[:, gs] = dx_scr[:, gs] + dxdt_state
            dBs.append(dB)
            dCs.append(dC)
        dxdt = dx_scr[...]
        tail = jnp.concatenate([jnp.sum(dyv * X, axis=0, keepdims=True), jnp.concatenate(rh_sums, axis=1),
                                jnp.zeros((6, D), F32)], axis=0)
        red = _dot01(jnp.concatenate([dyv * yoff_full, jnp.concatenate(ddec, axis=1), dxdt * X, tail], axis=0), rdm, 2)
        d_dskip, dcd_row = red[3 * L:3 * L + 1], red[3 * L + 1:3 * L + 2]
        ddec_h = red[L:2 * L] * o["dec"]
        dcum_last = jnp.sum(ddec_h, axis=0, keepdims=True) + dcd_row * o["cd"]
        dcum = red[0:L] + rs - cs_t.T - ddec_h + (sub == L - 1).astype(F32) * dcum_last
        dda = _dot01(o["anti"], dcum, 3, split_lhs=False)
        ddt_v = dda * o["a_row"] + red[2 * L:3 * L]
        dX = dX + dxdt * o["dt_rep"]
        head_mask = (lane < N_HEADS).astype(F32)
        d_alog = jnp.sum(dda * o["dt"], axis=0, keepdims=True) * o["a_row"] * head_mask
        dpre = ddt_v * _sigmoid(o["pre"]) * head_mask
        dpm_ref[:, OFF_DT:] = dpre.astype(BF16)
        d_dtb = jnp.sum(dpre, axis=0, keepdims=True)
        dhp_ref[...] += jnp.concatenate([d_dtb, d_alog, d_dskip * head_mask, jnp.zeros((5, LANES), F32)], axis=0)

        dxbc = jnp.concatenate([dX] + dBs + dCs, axis=1)
        sg, cv = o["sg"], o["cv"]
        dcv = dxbc * (sg * (1.0 + cv * (1.0 - sg)))
        e2 = jnp.concatenate([dcv, nxt_cv[...]], axis=0)
        n2 = L + CONV_HALO
        ahead = [dcv, pltpu.roll(e2, n2 - 1, 0)[:L], pltpu.roll(e2, n2 - 2, 0)[:L], pltpu.roll(e2, n2 - 3, 0)[:L]]
        dconv_ref[0:5] += jnp.concatenate(
            [jnp.sum(ux * ahead[3 - k], axis=0, keepdims=True) for k in range(4)]
            + [jnp.sum(dcv, axis=0, keepdims=True)], axis=0)
        dux = ahead[0] * cw[3:4] + ahead[1] * cw[2:3] + ahead[2] * cw[1:2] + ahead[3] * cw[0:1]
        dpm_ref[:, OFF_XBC:OFF_DT] = dux.astype(BF16)
        nxt_cv[...] = dcv[:CONV_HALO]

    def full(shape):
        return pl.BlockSpec(shape, lambda b, c: (0,) * len(shape))

    def rowblk(b, c):
        return b * nc + (nc - 1 - c)

    hp_blocks = CHUNK // POOL_HALO
    T = nb * seq
    return pl.pallas_call(
        body, name="mixer_bwd", grid=(nb, nc),
        in_specs=[pl.BlockSpec((CHUNK, PROJ_W), lambda b, c: (rowblk(b, c), 0)),
                  pl.BlockSpec((POOL_HALO, POOL_W), lambda b, c: (jnp.maximum(rowblk(b, c) * hp_blocks - 1, 0), 0)),
                  pl.BlockSpec((CHUNK, CONV_CH), lambda b, c: (rowblk(b, c), 0)),
                  pl.BlockSpec((CHUNK, MIX_W), lambda b, c: (rowblk(b, c), 0)),
                  pl.BlockSpec((1, 1, N_GROUPS, N_STATE, GROUP_W), lambda b, c: (b, nc - 1 - c, 0, 0, 0)),
                  full((4, CONV_CH)), full((1, CONV_CH)), full((8, LANES)), full((1, D)),
                  full((4, LANES, LANES)), full((1, POOL_W))] + [ANY_SPEC] * na,
        out_specs=[pl.BlockSpec((CHUNK, PROJ_W), lambda b, c: (rowblk(b, c), 0)),
                   full((8, CONV_CH)), full((8, LANES)), full((8, D)), full((4, LANES, LANES))],
        out_shape=[jax.ShapeDtypeStruct((T, PROJ_W), BF16),
                   jax.ShapeDtypeStruct((8, CONV_CH), F32), jax.ShapeDtypeStruct((8, LANES), F32),
                   jax.ShapeDtypeStruct((8, D), F32), jax.ShapeDtypeStruct((4, LANES, LANES), F32)],
        scratch_shapes=[pltpu.VMEM((POOL_HALO, POOL_W), F32), pltpu.VMEM((CONV_HALO, CONV_CH), F32),
                        pltpu.VMEM((N_GROUPS, N_STATE, GROUP_W), F32), pltpu.VMEM((CHUNK, D), F32),
                        pltpu.VMEM((CHUNK, D), F32)],
        compiler_params=_cparams(),
    )(pm, pm, cvs, dym, hstates, cw, cb, hp, gssd, wpool, pscale, *after)


def _mlp_fused(x2, ymix, target, mod, g_mlp, g_final, w_out, w_up, w_down, seq):
    T = x2.shape[0]
    tm = min(256, seq)
    tps = seq // tm
    nblk = D_FF // FF_BLK

    def body(x_ref, ym_ref, tg_ref, mod_ref, gm_ref, gf_ref, wo_ref, wu_ref, wd_ref,
             da_ref, dym_ref, dh1_ref, u2_ref, f_ref, dup_ref, ddn_ref, dmod_ref, acc_ref, relu_scr):
        i = pl.program_id(0)

        @pl.when(i == 0)
        def _():
            acc_ref[...] = jnp.zeros_like(acc_ref)

        @pl.when(i % tps == 0)
        def _():
            dmod_ref[...] = jnp.zeros_like(dmod_ref)

        md = mod_ref[0]
        gate_m, shift_f, scale_f, gate_f = md[2:3], md[3:4], md[4:5], md[5:6]
        g_mlp, g_fin = gm_ref[...], gf_ref[...]
        a = jnp.dot(ym_ref[...], wo_ref[...], preferred_element_type=F32)
        h1 = x_ref[...] + gate_m * a
        r2 = lax.rsqrt(jnp.mean(h1 * h1, axis=-1, keepdims=True) + EPS)
        n2 = h1 * r2
        u2 = (n2 * g_mlp) * (1.0 + scale_f) + shift_f
        u2b = u2.astype(BF16)
        u2_ref[...] = u2b
        dn = jnp.zeros((tm, D), F32)
        for j in range(nblk):
            js = slice(j * FF_BLK, (j + 1) * FF_BLK)
            upj = jnp.maximum(jnp.dot(u2b, wu_ref[j], preferred_element_type=F32), 0.0)
            relu_scr[:, js] = upj
            fj = (upj * upj).astype(BF16)
            f_ref[:, js] = fj
            dn = dn + jnp.dot(fj, wd_ref[j], preferred_element_type=F32)
        h2 = h1 + gate_f * dn
        r3 = lax.rsqrt(jnp.mean(h2 * h2, axis=-1, keepdims=True) + EPS)
        n3 = h2 * r3
        err = n3 * g_fin - tg_ref[...]
        loss = 0.5 * jnp.sum(jnp.mean(err * err, axis=-1, keepdims=True), axis=0, keepdims=True)
        dout = err * (1.0 / D)
        d_gfin = jnp.sum(dout * n3, axis=0, keepdims=True)
        dn3 = dout * g_fin
        dh2 = r3 * (dn3 - n3 * jnp.mean(dn3 * n3, axis=-1, keepdims=True))
        d_gate_f = jnp.sum(dh2 * dn, axis=0, keepdims=True)
        ddn = (gate_f * dh2).astype(BF16)
        ddn_ref[...] = ddn
        du2 = jnp.zeros((tm, D), F32)
        for j in range(nblk):
            js = slice(j * FF_BLK, (j + 1) * FF_BLK)
            dfj = lax.dot_general(ddn, wd_ref[j], (((1,), (1,)), ((), ())), preferred_element_type=F32)
            dupj = (dfj * (2.0 * relu_scr[:, js])).astype(BF16)
            dup_ref[:, js] = dupj
            du2 = du2 + lax.dot_general(dupj, wu_ref[j], (((1,), (1,)), ((), ())), preferred_element_type=F32)
        d_scale_f = jnp.sum(du2 * (n2 * g_mlp), axis=0, keepdims=True)
        d_shift_f = jnp.sum(du2, axis=0, keepdims=True)
        d_gmlp = jnp.sum(du2 * (1.0 + scale_f) * n2, axis=0, keepdims=True)
        dn2 = du2 * (g_mlp * (1.0 + scale_f))
        dh1 = dh2 + r2 * (dn2 - n2 * jnp.mean(dn2 * n2, axis=-1, keepdims=True))
        dh1_ref[...] = dh1
        d_gate_m = jnp.sum(dh1 * a, axis=0, keepdims=True)
        da = (gate_m * dh1).astype(BF16)
        da_ref[...] = da
        dym_ref[...] = lax.dot_general(da, wo_ref[...], (((1,), (1,)), ((), ())),
                                       preferred_element_type=F32).astype(BF16)
        dmod_ref[0] += jnp.concatenate([jnp.zeros((2, D), F32), d_gate_m, d_shift_f, d_scale_f, d_gate_f,
                                        jnp.zeros((2, D), F32)], axis=0)
        acc_ref[...] += jnp.concatenate([d_gmlp, d_gfin, loss * jnp.ones((1, D), F32), jnp.zeros((5, D), F32)], axis=0)

    whole = pl.BlockSpec(memory_space=pltpu.VMEM)

    def tok(w):
        return pl.BlockSpec((tm, w), lambda i: (i, 0))

    def vec():
        return pl.BlockSpec((1, D), lambda i: (0, 0))

    nb = T // seq
    return pl.pallas_call(
        body, name="mlp_fused", grid=(T // tm,),
        in_specs=[tok(D), tok(MIX_W), tok(D), pl.BlockSpec((1, 8, D), lambda i: (i // tps, 0, 0)), vec(), vec(),
                  whole, whole, whole],
        out_specs=[tok(D), tok(MIX_W), tok(D), tok(D), tok(D_FF), tok(D_FF), tok(D),
                   pl.BlockSpec((1, 8, D), lambda i: (i // tps, 0, 0)), pl.BlockSpec((8, D), lambda i: (0, 0))],
        out_shape=[jax.ShapeDtypeStruct((T, D), BF16), jax.ShapeDtypeStruct((T, MIX_W), BF16),
                   jax.ShapeDtypeStruct((T, D), F32), jax.ShapeDtypeStruct((T, D), BF16),
                   jax.ShapeDtypeStruct((T, D_FF), BF16), jax.ShapeDtypeStruct((T, D_FF), BF16),
                   jax.ShapeDtypeStruct((T, D), BF16), jax.ShapeDtypeStruct((nb, 8, D), F32),
                   jax.ShapeDtypeStruct((8, D), F32)],
        scratch_shapes=[pltpu.VMEM((tm, D_FF), F32)],
        compiler_params=_cparams(),
    )(x2, ymix, target, mod, g_mlp, g_final, w_out, w_up, w_down)


def _in_bwd(x2, dh1, dpb, mod, g_mix, w_cat, dmod_a, acc_a, seq):
    T = x2.shape[0]
    tm = min(512, seq)
    tps = seq // tm
    steps = T // tm

    def body(x_ref, dh_ref, dpb_ref, mod_ref, g_ref, w_ref, dma_ref, acca_ref, dx_ref, dmod_ref, acc_ref):
        i = pl.program_id(0)

        @pl.when(i == 0)
        def _():
            acc_ref[...] = acca_ref[...]

        @pl.when(i % tps == 0)
        def _():
            dmod_ref[...] = dma_ref[...]

        du = lax.dot_general(dpb_ref[...], w_ref[...], (((1,), (1,)), ((), ())), preferred_element_type=F32)
        x = x_ref[...]
        md = mod_ref[0]
        g = g_ref[...]
        r = lax.rsqrt(jnp.mean(x * x, axis=-1, keepdims=True) + EPS)
        n1 = x * r
        d_scale = jnp.sum(du * (n1 * g), axis=0, keepdims=True)
        d_shift = jnp.sum(du, axis=0, keepdims=True)
        d_g = jnp.sum(du * (1.0 + md[1:2]) * n1, axis=0, keepdims=True)
        dn1 = du * (g * (1.0 + md[1:2]))
        dx_ref[...] = dh_ref[...] + r * (dn1 - n1 * jnp.mean(dn1 * n1, axis=-1, keepdims=True))
        dmod_ref[0] += jnp.concatenate([d_shift, d_scale, jnp.zeros((6, D), F32)], axis=0)
        acc_ref[...] += jnp.concatenate([jnp.zeros((3, D), F32), d_g, jnp.zeros((4, D), F32)], axis=0)

    whole = pl.BlockSpec(memory_space=pltpu.VMEM)
    nb = T // seq
    return pl.pallas_call(
        body, name="in_bwd", grid=(steps,),
        in_specs=[pl.BlockSpec((tm, D), lambda i: (i, 0)), pl.BlockSpec((tm, D), lambda i: (i, 0)),
                  pl.BlockSpec((tm, PROJ_W), lambda i: (i, 0)),
                  pl.BlockSpec((1, 8, D), lambda i: (i // tps, 0, 0)), pl.BlockSpec((1, D), lambda i: (0, 0)),
                  whole, pl.BlockSpec((1, 8, D), lambda i: (i // tps, 0, 0)), pl.BlockSpec((8, D), lambda i: (0, 0))],
        out_specs=[pl.BlockSpec((tm, D), lambda i: (i, 0)),
                   pl.BlockSpec((1, 8, D), lambda i: (i // tps, 0, 0)), pl.BlockSpec((8, D), lambda i: (0, 0))],
        out_shape=[jax.ShapeDtypeStruct((T, D), F32),
                   jax.ShapeDtypeStruct((nb, 8, D), F32), jax.ShapeDtypeStruct((8, D), F32)],
        compiler_params=_cparams(),
    )(x2, dh1, dpb, mod, g_mix, w_cat, dmod_a, acc_a)


def _dw_in(u_b, dpb, in_cols, shards):
    T = u_b.shape[0]
    bk = min(512, T)
    nk = T // bk
    ns = len(shards)
    starts = [(in_cols * j // LANES) * LANES for j in range(N_DEV)]
    assert all(s + DW_IN_WIN <= PROJ_W and in_cols * (j + 1) <= s + DW_IN_WIN for j, s in enumerate(starts))

    def body(*refs):
        u_ref, d_ref = refs[:2]
        sh_refs = refs[2:2 + ns]
        o_ref = refs[2 + ns]
        ga_refs = refs[3 + ns:3 + 2 * ns]
        acc, send, recv, loc = refs[3 + 2 * ns:]
        k = pl.program_id(0)
        gather = _Gather(sh_refs, ga_refs, send, recv, loc)
        gather.begin_hosted(k, nk)

        @pl.when(k == 0)
        def _():
            acc[...] = jnp.zeros_like(acc)

        ut = u_ref[...].T
        for j in range(N_DEV):
            acc[j] += jnp.dot(ut, d_ref[:, starts[j]:starts[j] + DW_IN_WIN], preferred_element_type=F32)

        @pl.when(k == nk - 1)
        def _():
            for j in range(N_DEV):
                off = in_cols * j - starts[j]
                o_ref[j] = acc[j][:, off:off + in_cols].astype(BF16)

        gather.end_hosted(k, nk)

    return pl.pallas_call(
        body, name="dw_in", grid=(nk,),
        in_specs=[pl.BlockSpec((bk, D), lambda k: (k, 0)), pl.BlockSpec((bk, PROJ_W), lambda k: (k, 0))]
        + [ANY_SPEC] * ns,
        out_specs=[pl.BlockSpec((N_DEV, D, in_cols), lambda k: (0, 0, 0))] + [ANY_SPEC] * ns,
        out_shape=[jax.ShapeDtypeStruct((N_DEV, D, in_cols), BF16)]
        + [jax.ShapeDtypeStruct((N_DEV,) + v.shape, v.dtype) for v in shards],
        scratch_shapes=[pltpu.VMEM((N_DEV, D, DW_IN_WIN), F32)] + _gather_scratch(shards),
        compiler_params=_cparams(),
    )(u_b, dpb, *shards)


def _dw_blocks(a, b, name, by_rows, per_step=1, after=()):
    T, M = a.shape
    N = b.shape[1]
    bk = min(2048, T)
    nk = T // bk
    whole = pl.BlockSpec(memory_space=pltpu.VMEM)
    if by_rows:
        rows = M // N_DEV
        am = rows * per_step
        nblk = N_DEV // per_step
        a_spec, b_spec = pl.BlockSpec((bk, am), lambda i, k: (k, i)), whole
        out_blk, acc_shape = (per_step, rows, N), (am, N)
    else:
        cols = N // N_DEV
        nblk = N_DEV
        a_spec, b_spec = whole, pl.BlockSpec((bk, cols), lambda i, k: (k, i))
        out_blk, acc_shape = (1, M, cols), (M, cols)

    def body(a_ref, b_ref, *rest):
        o_ref, acc = rest[len(after):]
        k = pl.program_id(1)

        @pl.when(k == 0)
        def _():
            acc[...] = jnp.zeros_like(acc)

        tok = pl.ds(pl.multiple_of(k * bk, bk), bk)
        a_blk = a_ref[...] if by_rows else a_ref[tok, :]
        b_blk = b_ref[tok, :] if by_rows else b_ref[...]
        acc[...] += lax.dot_general(a_blk, b_blk, (((0,), (0,)), ((), ())), preferred_element_type=F32)

        @pl.when(k == nk - 1)
        def _():
            o_ref[...] = acc[...].reshape(out_blk).astype(BF16)

    return pl.pallas_call(
        body, name=name, grid=(nblk, nk), in_specs=[a_spec, b_spec] + [ANY_SPEC] * len(after),
        out_specs=pl.BlockSpec(out_blk, lambda i, k: (i, 0, 0)),
        out_shape=jax.ShapeDtypeStruct((N_DEV,) + out_blk[1:], BF16),
        scratch_shapes=[pltpu.VMEM(acc_shape, F32)],
        compiler_params=_cparams(),
    )(a, b, *after)


def _adam_parts(parts, w, m, v, name):
    rows, cols = w.shape
    br = rows
    for cand in range(rows, 15, -16):
        if rows % cand == 0 and cand * cols * 4 <= ADAM_BLOCK_BYTES:
            br = cand
            break

    def body(p_ref, w_ref, m_ref, v_ref, g_out, dl_out, m_out, v_out):
        g = p_ref[0].astype(F32)
        for k in range(1, N_DEV):
            g = g + p_ref[k].astype(F32)
        g_out[...] = g
        dl, mn, vn = _adam_math(w_ref[...], g, m_ref[...], v_ref[...])
        dl_out[...] = dl
        m_out[...] = mn
        v_out[...] = vn

    wspec = pl.BlockSpec((br, cols), lambda i: (i, 0))
    return pl.pallas_call(
        body, name=name, grid=(rows // br,),
        in_specs=[pl.BlockSpec((N_DEV, br, cols), lambda i: (0, i, 0)), wspec, wspec, wspec],
        out_specs=[wspec] * 4, out_shape=[jax.ShapeDtypeStruct((rows, cols), F32)] * 4,
        compiler_params=_cparams(),
    )(parts, w, m, v)


def _adam_plain(g, w, m, v, name):
    def body(g_ref, w_ref, m_ref, v_ref, dl_out, m_out, v_out):
        dl, mn, vn = _adam_math(w_ref[...], g_ref[...], m_ref[...], v_ref[...])
        dl_out[...] = dl
        m_out[...] = mn
        v_out[...] = vn

    return pl.pallas_call(body, name=name, out_shape=[jax.ShapeDtypeStruct(w.shape, F32)] * 3,
                          compiler_params=_cparams())(g, w, m, v)


SMALL_PARAMS = ("b_ada", "g_mix", "conv_b", "dt_bias", "a_log", "d_skip", "g_ssd", "pool_scale", "g_mlp", "g_final")


def _small_adam(gathered, params):
    n_par = len(SMALL_PARAMS)
    nb = gathered[0].shape[1]

    def body(*refs):
        dmod_ref, acc_ref, conv_ref, vec_ref, hd_ref = refs[:5]
        par_refs = refs[5:5 + 3 * n_par]
        out_refs = refs[5 + 3 * n_par:5 + 7 * n_par]
        cw_out, acc_out = refs[5 + 7 * n_par:]

        def total(ref):
            t = ref[0]
            for k in range(1, N_DEV):
                t = t + ref[k]
            return t

        dm = total(dmod_ref)
        dmb = dm[0]
        for b in range(1, nb):
            dmb = dmb + dm[b]
        ac, cv, vc, hd = total(acc_ref), total(conv_ref), total(vec_ref), total(hd_ref)
        cw_out[...] = cv[0:4]
        acc_out[...] = ac
        grads = {
            "b_ada": jnp.concatenate([dmb[r:r + 1] for r in range(6)], axis=1), "g_mix": ac[3:4], "conv_b": cv[4:5],
            "dt_bias": hd[0:1, 0:N_HEADS], "a_log": hd[1:2, 0:N_HEADS], "d_skip": hd[2:3, 0:N_HEADS],
            "g_ssd": vc[0:1], "pool_scale": vc[1:2, 0:POOL_W], "g_mlp": ac[0:1], "g_final": ac[1:2],
        }
        for i, name in enumerate(SMALL_PARAMS):
            w_ref, m_ref, v_ref = par_refs[3 * i:3 * i + 3]
            g = grads[name]
            dl, mn, vn = _adam_math(w_ref[...], g, m_ref[...], v_ref[...])
            g_o, d_o, m_o, v_o = out_refs[4 * i:4 * i + 4]
            g_o[...] = g
            d_o[...] = dl
            m_o[...] = mn
            v_o[...] = vn

    flat = [a for name in SMALL_PARAMS for a in params[name]]
    out_shape = [jax.ShapeDtypeStruct(params[name][0].shape, F32) for name in SMALL_PARAMS for _ in range(4)]
    out_shape += [jax.ShapeDtypeStruct((4, CONV_CH), F32), jax.ShapeDtypeStruct((8, D), F32)]
    return pl.pallas_call(body, name="small_adam", out_shape=out_shape, compiler_params=_cparams())(*gathered, *flat)


def kernel(x, c, w_ada, b_ada, g_mix, w_in, conv_w, conv_b, dt_bias, a_log, d_skip, g_ssd, w_pool, pool_scale, w_out, g_mlp, w_up, w_down, g_final, loss_target, m_w_ada, m_b_ada, m_g_mix, m_w_in, m_conv_w, m_conv_b, m_dt_bias, m_a_log, m_d_skip, m_g_ssd, m_w_pool, m_pool_scale, m_w_out, m_g_mlp, m_w_up, m_w_down, m_g_final, v_w_ada, v_b_ada, v_g_mix, v_w_in, v_conv_w, v_conv_b, v_dt_bias, v_a_log, v_d_skip, v_g_ssd, v_w_pool, v_pool_scale, v_w_out, v_g_mlp, v_w_up, v_w_down, v_g_final):
    nb, seq, _ = x.shape
    T = nb * seq
    me = 4 * lax.axis_index("x") + 2 * lax.axis_index("y") + lax.axis_index("c")
    in_cols = w_in.shape[2]
    ada_cols = w_ada.shape[2]
    cw_cols = conv_w.shape[2]

    c_g, cw_g, win_g = _all_gather([c, conv_w[0], w_in[0].astype(BF16)], "ag_first")
    c_all = c_g.reshape(N_DEV * nb, D)
    cw_full = cw_g.transpose(1, 0, 2).reshape(4, CONV_CH)

    b_slice = lax.dynamic_slice(b_ada, (0, me * ada_cols), (1, ada_cols))
    mod_cols = _ada_fwd(c_all, w_ada[0], b_slice)
    (mod_g,) = _all_gather([mod_cols], "ag_mod")
    mod_all = mod_g.transpose(1, 0, 2).reshape(N_DEV * nb, 6, D)
    mod_mine = lax.dynamic_slice(mod_all, (me * nb, 0, 0), (nb, 6, D))
    mod = jnp.pad(mod_mine, ((0, 0), (0, 2), (0, 0)))

    x2 = x.reshape(T, D)
    tg2 = loss_target.reshape(T, D)
    heads = jnp.pad(jnp.concatenate([dt_bias, a_log, d_skip], axis=0), ((0, 5), (0, LANES - N_HEADS)))
    wpool_b = w_pool[0]
    u_b, pm, w_cat, wup_g = _mix_in(x2, mod, g_mix, win_g, seq, [w_up[0].astype(BF16)])
    ymix, hstates, cvs, wout_g, wdn_g = _mixer_fwd(
        pm, cw_full, conv_b, heads, g_ssd, wpool_b, pool_scale, nb, seq,
        [w_out[0].astype(BF16), w_down[0].astype(BF16)])
    da_b, dym, dh1, u2_b, f_b, dup_b, ddn_b, dmod_a, acc_a = _mlp_fused(
        x2, ymix, tg2, mod, g_mlp, g_final.reshape(1, D), wout_g.reshape(MIX_W, D), wup_g, wdn_g, seq)

    gout_p = _dw_blocks(ymix, da_b, "dw_out", True, per_step=4)
    ex_out = _exchange_start(gout_p, "gout_start")
    gup_p = _dw_blocks(u2_b, dup_b, "dw_up", False, after=[ex_out[4]])
    ex_up = _exchange_start(gup_p, "gup_start")
    gdn_p = _dw_blocks(f_b, ddn_b, "dw_down", True, after=[ex_up[4]])
    ex_dn = _exchange_start(gdn_p, "gdn_start")
    dpb, d_conv, d_heads, d_vec, d_wpool = _mixer_bwd(
        pm, cvs, dym, hstates, cw_full, conv_b, heads, g_ssd, wpool_b, pool_scale, nb, seq, after=[ex_dn[4]])
    gin_p, conv_g, vec_g, heads_g, wpool_parts = _dw_in(
        u_b, dpb, in_cols, [d_conv, d_vec, d_heads, d_wpool.reshape(4 * LANES, LANES)])
    ex_in = _exchange_start(gin_p, "gin_start")
    grad_x2, dmod, acc = _in_bwd(x2, dh1, dpb, mod, g_mix + ex_in[4][0:1, 0:1], w_cat, dmod_a, acc_a, seq)

    def landed(ex, after, name):
        own, land = _exchange_wait(ex[0], ex[1], ex[2], ex[3], after, name)
        mine = lax.dynamic_slice(own, (me, 0, 0), (1,) + own.shape[1:])
        return lax.dynamic_update_slice(land, mine, (me, 0, 0))

    gout_r, gup_r, gdn_r = landed(ex_out, dmod, "gout_wait"), landed(ex_up, dmod, "gup_wait"), landed(ex_dn, dmod, "gdn_wait")
    g_out, d_out, nm_out, nv_out = _adam_parts(gout_r, w_out[0], m_w_out[0], v_w_out[0], "adam_w_out")
    g_up, d_up, nm_up, nv_up = _adam_parts(gup_r, w_up[0], m_w_up[0], v_w_up[0], "adam_w_up")
    g_dn, d_dn, nm_dn, nv_dn = _adam_parts(gdn_r, w_down[0], m_w_down[0], v_w_down[0], "adam_w_down")

    dmod_g, acc_g = _all_gather([dmod, acc], "ag_small_bwd", after=[nm_out, nm_up, nm_dn])
    pool2 = (4 * LANES, LANES)
    wpool_outs = _adam_parts(wpool_parts, w_pool.reshape(pool2), m_w_pool.reshape(pool2), v_w_pool.reshape(pool2),
                             "adam_w_pool")
    small_params = {
        "b_ada": (b_ada, m_b_ada, v_b_ada), "g_mix": (g_mix, m_g_mix, v_g_mix), "conv_b": (conv_b, m_conv_b, v_conv_b),
        "dt_bias": (dt_bias, m_dt_bias, v_dt_bias), "a_log": (a_log, m_a_log, v_a_log),
        "d_skip": (d_skip, m_d_skip, v_d_skip), "g_ssd": (g_ssd, m_g_ssd, v_g_ssd),
        "pool_scale": (pool_scale, m_pool_scale, v_pool_scale), "g_mlp": (g_mlp, m_g_mlp, v_g_mlp),
        "g_final": tuple(a.reshape(1, D) for a in (g_final, m_g_final, v_g_final)),
    }
    small_res = _small_adam([dmod_g, acc_g, conv_g, vec_g, heads_g], small_params)
    g_cw_full, acc_sum = small_res[-2:]
    loss = acc_sum[2, 0]

    g_cw = lax.dynamic_slice(g_cw_full, (0, me * cw_cols), (4, cw_cols))
    d_cwp, nm_cwp, nv_cwp = _adam_plain(g_cw, conv_w[0], m_conv_w[0], v_conv_w[0], "adam_conv_w")

    dmod_all = dmod_g[:, :, 0:6].reshape(N_DEV * nb, 6 * D)
    dmod_slice = lax.dynamic_slice(dmod_all, (0, me * ada_cols), (N_DEV * nb, ada_cols))
    g_ada, d_ada, nm_ada, nv_ada = _ada_bwd_adam(c_all, dmod_slice, w_ada[0], m_w_ada[0], v_w_ada[0])

    ex_after = nm_ada[0:8, 0:LANES] + acc_sum[:, 0:LANES]
    gin_r = landed(ex_in, ex_after, "gin_wait")
    g_in, d_in, nm_in, nv_in = _adam_parts(gin_r, w_in[0], m_w_in[0], v_w_in[0], "adam_w_in")

    def small_outs(kind, wpool):
        res = {name: small_res[4 * i + kind] for i, name in enumerate(SMALL_PARAMS)}
        res["g_final"] = res["g_final"].reshape(D)
        res["w_pool"] = wpool.reshape(1, 4, LANES, LANES)
        return res

    def big_outs(ada, win, cwp, wout, wup, wdn):
        return {"w_ada": ada[None], "w_in": win.reshape(1, D, in_cols), "conv_w": cwp[None], "w_out": wout[None],
                "w_up": wup[None], "w_down": wdn[None]}

    order = ["w_ada", "b_ada", "g_mix", "w_in", "conv_w", "conv_b", "dt_bias", "a_log", "d_skip", "g_ssd", "w_pool",
             "pool_scale", "w_out", "g_mlp", "w_up", "w_down", "g_final"]
    groups = [
        {**small_outs(0, wpool_outs[0]), **big_outs(g_ada, g_in, g_cw, g_out, g_up, g_dn)},
        {**small_outs(1, wpool_outs[1]), **big_outs(d_ada, d_in, d_cwp, d_out, d_up, d_dn)},
        {**small_outs(2, wpool_outs[2]), **big_outs(nm_ada, nm_in, nm_cwp, nm_out, nm_up, nm_dn)},
        {**small_outs(3, wpool_outs[3]), **big_outs(nv_ada, nv_in, nv_cwp, nv_out, nv_up, nv_dn)},
    ]
    outs = [loss, grad_x2.reshape(nb, seq, D)]
    for grp in groups:
        outs += [grp[n] for n in order]
    return tuple(outs)
```

```python
import functools

import jax
import jax.numpy as jnp
from jax import lax
from jax.experimental import pallas as pl
from jax.experimental.pallas import tpu as pltpu

F32, BF16 = jnp.float32, jnp.bfloat16
MESH = pl.DeviceIdType.MESH
N_DEV = 8
D = 1024
LANES = 128
CHUNK = 128
POOL_W = 512
WINDOWS = (2, 4, 8, 16)
N_HEADS = 16
HEAD_DIM = 64
N_GROUPS = 2
GROUP_W = 512
N_STATE = 128
CONV_CH = 1536
OFF_Z, OFF_XBC, OFF_DT, IN_W = 512, 1536, 3072, 3088
PROJ_W = OFF_DT + LANES
MIX_W = 1536
D_FF = 4096
FF_BLK = 512
EPS = 1e-5
LR, B1, B2, AEPS, WD, STEP = 0.001, 0.9, 0.999, 1e-08, 0.01, 10
POOL_HALO = 16
CONV_HALO = 8
VMEM_LIMIT = 56 << 20
ADAM_BLOCK_BYTES = 1 << 20
DW_IN_WIN = 512


def _cparams(**kw):
    return pltpu.CompilerParams(vmem_limit_bytes=VMEM_LIMIT, **kw)


def _mm(a, b):
    return jnp.dot(a.astype(BF16), b.astype(BF16), preferred_element_type=F32)


def _mm_nt(a, b):
    return lax.dot_general(a.astype(BF16), b.astype(BF16), (((1,), (1,)), ((), ())), preferred_element_type=F32)


def _mm_tn(a, b):
    return lax.dot_general(a.astype(BF16), b.astype(BF16), (((0,), (0,)), ((), ())), preferred_element_type=F32)


def _split_bf16(v, terms):
    parts, rest = [], v
    for t in range(terms):
        p = rest.astype(BF16)
        parts.append(p)
        if t + 1 < terms:
            rest = rest - p.astype(F32)
    return parts


def _dot01(a, b, terms, split_lhs=True):
    if split_lhs:
        bb = b.astype(BF16)
        prods = [jnp.dot(p, bb, preferred_element_type=F32) for p in _split_bf16(a, terms)]
    else:
        ab = a.astype(BF16)
        prods = [jnp.dot(ab, p, preferred_element_type=F32) for p in _split_bf16(b, terms)]
    out = prods[0]
    for q in prods[1:]:
        out = out + q
    return out


def _sigmoid(v):
    return 1.0 / (1.0 + jnp.exp(-v))


def _expand_mat():
    r = lax.broadcasted_iota(jnp.int32, (LANES, D), 0)
    c = lax.broadcasted_iota(jnp.int32, (LANES, D), 1)
    return (r == c // HEAD_DIM).astype(F32)


def _reduce_mat():
    r = lax.broadcasted_iota(jnp.int32, (D, LANES), 0)
    c = lax.broadcasted_iota(jnp.int32, (D, LANES), 1)
    return (c == r // HEAD_DIM).astype(F32)


def _pos():
    return lax.axis_index("x"), lax.axis_index("y"), lax.axis_index("c")


GATHER_PIECES = 4
GATHER_PIECE_BYTES = 96 << 10


def _pieces(shape, dtype):
    rows = shape[0]
    size = jnp.dtype(dtype).itemsize
    for d in shape:
        size *= d
    whole_tiles = rows % (GATHER_PIECES * 16) == 0
    return GATHER_PIECES if whole_tiles and size // GATHER_PIECES >= GATHER_PIECE_BYTES else 1


class _Gather:
    def __init__(self, x_refs, o_refs, send, recv, loc):
        self.x_refs, self.o_refs, self.send, self.recv, self.loc = x_refs, o_refs, send, recv, loc
        self.n = len(x_refs)
        self.pieces = [_pieces(r.shape, r.dtype) for r in x_refs]
        self.base = [7 * sum(self.pieces[:a]) for a in range(self.n)]
        x, y, c = _pos()
        self.c = c
        self.me, self.sib = (x, y, c), (x, y, 1 - c)
        self.chips = [(1 - x, y), (x, 1 - y), (1 - x, 1 - y)]

    def _rows(self, a, p):
        rows = self.x_refs[a].shape[0] // self.pieces[a]
        return pl.ds(p * rows, rows)

    def _cp(self, a, p, k, block, to, own=False):
        dst = self.o_refs[a].at[4 * block[0] + 2 * block[1] + block[2], self._rows(a, p)]
        sem = self.base[a] + 7 * p + k
        return pltpu.make_async_remote_copy(
            src_ref=self.x_refs[a].at[self._rows(a, p)] if own else dst, dst_ref=dst,
            send_sem=self.send.at[sem], recv_sem=self.recv.at[sem], device_id=to, device_id_type=MESH)

    def _mine(self, a):
        me = self.me
        return pltpu.make_async_copy(self.x_refs[a], self.o_refs[a].at[4 * me[0] + 2 * me[1] + me[2]], self.loc.at[a])

    def _first(self, a, p):
        cps = [self._cp(a, p, 0, self.me, self.sib, own=True)]
        return cps + [self._cp(a, p, 1 + j, self.me, (*chip, self.c), own=True) for j, chip in enumerate(self.chips)]

    def _passed(self, a, p, j):
        return self._cp(a, p, 4 + j, (*self.chips[j], self.c), self.sib)

    def start(self):
        for a in range(self.n):
            self._mine(a).start()
        for p in range(max(self.pieces)):
            for a in range(self.n):
                if p < self.pieces[a]:
                    for cp in self._first(a, p):
                        cp.start()

    def forward(self, p):
        for j, chip in enumerate(self.chips):
            for a in range(self.n):
                if p < self.pieces[a]:
                    self._cp(a, p, 1 + j, (*chip, self.c), self.me).wait_recv()
                    self._passed(a, p, j).start()

    def finish(self):
        for a in range(self.n):
            for p in range(self.pieces[a]):
                self._cp(a, p, 0, self.sib, self.me).wait_recv()
                for j, chip in enumerate(self.chips):
                    self._cp(a, p, 4 + j, (*chip, 1 - self.c), self.me).wait_recv()
        for a in range(self.n):
            for p in range(self.pieces[a]):
                for cp in self._first(a, p):
                    cp.wait_send()
                for j in range(3):
                    self._passed(a, p, j).wait_send()
            self._mine(a).wait()

    def begin_hosted(self, step, steps):
        @pl.when(step == 0)
        def _():
            self.start()

        n_p = max(self.pieces)
        for p in range(n_p):
            @pl.when(step == min(((p + 1) * 7 * steps) // (8 * n_p), steps - 1))
            def _():
                self.forward(p)

    def end_hosted(self, step, steps):
        @pl.when(step == steps - 1)
        def _():
            self.finish()


class _Exchange:
    def __init__(self, x_refs, o_refs, send, recv, loc):
        self.x_refs, self.o_refs, self.send, self.recv, self.loc = x_refs, o_refs, send, recv, loc
        self.n = len(x_refs)
        x, y, c = _pos()
        self.me_i = 4 * x + 2 * y + c
        self.peers = []
        for k in range(1, N_DEV):
            px = 1 - x if (k >> 2) & 1 else x
            py = 1 - y if (k >> 1) & 1 else y
            pc = 1 - c if k & 1 else c
            self.peers.append(((px, py, pc), 4 * px + 2 * py + pc))

    def _mine(self, a):
        return pltpu.make_async_copy(self.x_refs[a].at[self.me_i], self.o_refs[a].at[self.me_i], self.loc.at[a])

    def _cp(self, a, k, landing):
        peer, peer_i = self.peers[k]
        return pltpu.make_async_remote_copy(
            src_ref=self.x_refs[a].at[peer_i], dst_ref=self.o_refs[a].at[landing],
            send_sem=self.send.at[a * 7 + k], recv_sem=self.recv.at[a * 7 + k],
            device_id=peer, device_id_type=MESH)

    def start(self):
        for a in range(self.n):
            self._mine(a).start()
            for k in range(N_DEV - 1):
                self._cp(a, k, self.me_i).start()

    def finish(self):
        for a in range(self.n):
            for k in range(N_DEV - 1):
                self._cp(a, k, self.peers[k][1]).wait_recv()
        for a in range(self.n):
            for k in range(N_DEV - 1):
                self._cp(a, k, self.me_i).wait_send()
            self._mine(a).wait()


def _gather_scratch(xs):
    n_sem = 7 * sum(_pieces(v.shape, v.dtype) for v in xs)
    return [pltpu.SemaphoreType.DMA((n_sem,)), pltpu.SemaphoreType.DMA((n_sem,)), pltpu.SemaphoreType.DMA((len(xs),))]


ANY_SPEC = pl.BlockSpec(memory_space=pl.ANY)


def _all_gather(xs, name, after=()):
    n, na = len(xs), len(after)

    def body(*refs):
        g = _Gather(refs[:n], refs[n + na:2 * n + na], *refs[2 * n + na:])
        g.start()
        for p in range(max(g.pieces)):
            g.forward(p)
        g.finish()

    return pl.pallas_call(
        body, name=name,
        out_shape=[jax.ShapeDtypeStruct((N_DEV,) + v.shape, v.dtype) for v in xs],
        in_specs=[ANY_SPEC] * (n + na), out_specs=[ANY_SPEC] * n, scratch_shapes=_gather_scratch(xs),
    )(*xs, *after)


HBM_SPEC = pl.BlockSpec(memory_space=pltpu.HBM)
SEM_SPEC = pl.BlockSpec(memory_space=pltpu.SEMAPHORE)
VMEM_SPEC = pl.BlockSpec(memory_space=pltpu.VMEM)
SPLIT_EFFECT = pltpu.SideEffectType.DATAFLOW_SIDE_EFFECTING


def _in_hbm(v):
    return pltpu.with_memory_space_constraint(v, pltpu.HBM)


def _exchange_start(blocks, name):
    def body(x_ref, land_ref, send, recv, x_thru, land_thru, token):
        ex = _Exchange([x_ref], [land_ref], send, recv, None)
        for k in range(N_DEV - 1):
            ex._cp(0, k, ex.me_i).start()
        token[...] = jnp.zeros_like(token)

    hbm = pltpu.HBM(blocks.shape, blocks.dtype)
    return pl.pallas_call(
        body, name=name,
        out_shape=(pltpu.SemaphoreType.DMA((N_DEV - 1,)), pltpu.SemaphoreType.DMA((N_DEV - 1,)), hbm, hbm,
                   jax.ShapeDtypeStruct((8, LANES), F32)),
        in_specs=(HBM_SPEC, HBM_SPEC), out_specs=(SEM_SPEC, SEM_SPEC, HBM_SPEC, HBM_SPEC, VMEM_SPEC),
        input_output_aliases={0: 2, 1: 3},
        compiler_params=pltpu.CompilerParams(has_side_effects=SPLIT_EFFECT),
    )(_in_hbm(blocks), _in_hbm(lax.empty(blocks.shape, blocks.dtype)))


def _exchange_wait(send, recv, x_thru, land_thru, after, name):
    def body(x_ref, land_ref, send_ref, recv_ref, after_ref, x_dead, got_ref):
        ex = _Exchange([x_ref], [land_ref], send_ref, recv_ref, None)
        for k in range(N_DEV - 1):
            ex._cp(0, k, ex.me_i).wait_send()
            ex._cp(0, k, ex.peers[k][1]).wait_recv()

    hbm = pltpu.HBM(x_thru.shape, x_thru.dtype)
    return pl.pallas_call(
        body, name=name, out_shape=(hbm, hbm),
        in_specs=(HBM_SPEC, HBM_SPEC, SEM_SPEC, SEM_SPEC, ANY_SPEC), out_specs=(HBM_SPEC, HBM_SPEC),
        input_output_aliases={0: 0, 1: 1},
        compiler_params=pltpu.CompilerParams(has_side_effects=SPLIT_EFFECT),
    )(x_thru, land_thru, send, recv, after)


def _ada_fwd(c_all, w_ada, b_slice):
    def body(c_ref, w_ref, b_ref, o_ref):
        cv = c_ref[...]
        act = cv * _sigmoid(cv)
        o_ref[...] = _mm(act, w_ref[...]) + b_ref[...]

    nb, nc = c_all.shape[0], w_ada.shape[1]
    return pl.pallas_call(body, name="ada_fwd", out_shape=jax.ShapeDtypeStruct((nb, nc), F32),
                          compiler_params=_cparams())(c_all, w_ada, b_slice)


def _adam_math(w, g, m, v):
    m = B1 * m + (1.0 - B1) * g
    v = B2 * v + (1.0 - B2) * jnp.square(g)
    m_hat = m / (1.0 - B1 ** STEP)
    v_hat = v / (1.0 - B2 ** STEP)
    delta = -LR * (m_hat / (jnp.sqrt(v_hat) + AEPS) + WD * w)
    return delta, m, v


def _ada_bwd_adam(c_all, dmod_slice, w, m, v):
    rows, cols = w.shape
    br = 256

    def body(c_ref, d_ref, w_ref, m_ref, v_ref, g_out, dl_out, m_out, v_out):
        cv = c_ref[...]
        act = cv * _sigmoid(cv)
        g = _mm_tn(act, d_ref[...])
        g_out[...] = g
        dl, mn, vn = _adam_math(w_ref[...], g, m_ref[...], v_ref[...])
        dl_out[...] = dl
        m_out[...] = mn
        v_out[...] = vn

    nb = c_all.shape[0]
    wspec = pl.BlockSpec((br, cols), lambda i: (i, 0))
    return pl.pallas_call(
        body, name="ada_bwd_adam", grid=(rows // br,),
        in_specs=[pl.BlockSpec((nb, br), lambda i: (0, i)), pl.BlockSpec((nb, cols), lambda i: (0, 0)),
                  wspec, wspec, wspec],
        out_specs=[wspec] * 4, out_shape=[jax.ShapeDtypeStruct((rows, cols), F32)] * 4,
        compiler_params=_cparams(),
    )(c_all, dmod_slice, w, m, v)


def _mix_in(x2, mod, g_mix, win_g, seq, shards):
    T = x2.shape[0]
    tm = min(512, seq)
    tps = seq // tm
    in_cols = win_g.shape[2]
    ns = len(shards)
    steps = T // tm

    def body(*refs):
        x_ref, mod_ref, g_ref, wb_ref = refs[:4]
        sh_refs = refs[4:4 + ns]
        u_ref, pm_ref, wc_ref = refs[4 + ns:7 + ns]
        ga_refs = refs[7 + ns:7 + 2 * ns]
        w_ref, send, recv, loc = refs[7 + 2 * ns:]
        step = pl.program_id(0)
        gather = _Gather(sh_refs, ga_refs, send, recv, loc)
        gather.begin_hosted(step, steps)

        @pl.when(step == 0)
        def _():
            w_ref[:, OFF_DT:] = jnp.zeros((D, PROJ_W - OFF_DT), BF16)
            for j in range(N_DEV):
                w_ref[:, in_cols * j:in_cols * (j + 1)] = wb_ref[j]
            wc_ref[...] = w_ref[...]

        x = x_ref[...]
        r = lax.rsqrt(jnp.mean(x * x, axis=-1, keepdims=True) + EPS)
        md = mod_ref[0]
        u = (x * r * g_ref[...]) * (1.0 + md[1:2]) + md[0:1]
        ub = u.astype(BF16)
        u_ref[...] = ub
        pm_ref[...] = jnp.dot(ub, w_ref[...], preferred_element_type=F32)
        gather.end_hosted(step, steps)

    whole = pl.BlockSpec(memory_space=pltpu.VMEM)
    return pl.pallas_call(
        body, name="mix_in", grid=(T // tm,),
        in_specs=[pl.BlockSpec((tm, D), lambda i: (i, 0)), pl.BlockSpec((1, 8, D), lambda i: (i // tps, 0, 0)),
                  pl.BlockSpec((1, D), lambda i: (0, 0)), whole] + [ANY_SPEC] * ns,
        out_specs=[pl.BlockSpec((tm, D), lambda i: (i, 0)), pl.BlockSpec((tm, PROJ_W), lambda i: (i, 0)),
                   pl.BlockSpec((D, PROJ_W), lambda i: (0, 0))] + [ANY_SPEC] * ns,
        out_shape=[jax.ShapeDtypeStruct((T, D), BF16), jax.ShapeDtypeStruct((T, PROJ_W), F32),
                   jax.ShapeDtypeStruct((D, PROJ_W), BF16)]
        + [jax.ShapeDtypeStruct((N_DEV,) + v.shape, v.dtype) for v in shards],
        scratch_shapes=[pltpu.VMEM((D, PROJ_W), BF16)] + _gather_scratch(shards),
        compiler_params=_cparams(),
    )(x2, mod, g_mix, win_g, *shards)


def _chunk_forward(up, z, ux, dtin, halo_p, halo_x, hprev, cw, cb, hp, gssd, wpool, pscale, t0, y_scr, cv=None):
    L = CHUNK
    out = {}
    row = lax.broadcasted_iota(jnp.int32, (L, 1), 0)
    t = (t0 + row + 1).astype(F32)
    e = jnp.concatenate([halo_p, up], axis=0)
    s2 = e + pltpu.roll(e, 1, 0)
    s4 = s2 + pltpu.roll(s2, 2, 0)
    s8 = s4 + pltpu.roll(s4, 4, 0)
    s16 = s8 + pltpu.roll(s8, 8, 0)
    sums = (s2, s4, s8, s16)
    p, inv, yp = [], [], []
    for gi, w in enumerate(WINDOWS):
        sl = slice(gi * LANES, (gi + 1) * LANES)
        ic = 1.0 / jnp.minimum(t, float(w))
        pg = sums[gi][POOL_HALO:, sl] * ic - up[:, sl]
        p.append(pg)
        inv.append(ic)
        yp.append(_mm(pg, wpool[gi]))
    out["p"], out["inv"], out["yp"] = p, inv, yp
    out["y_pool"] = jnp.concatenate(yp, axis=1) * pscale
    if cv is None:
        ex = jnp.concatenate([halo_x, ux], axis=0)
        taps = [pltpu.roll(ex, 3, 0)[CONV_HALO:], pltpu.roll(ex, 2, 0)[CONV_HALO:], pltpu.roll(ex, 1, 0)[CONV_HALO:], ux]
        cv = cb + taps[0] * cw[0:1] + taps[1] * cw[1:2] + taps[2] * cw[2:3] + taps[3] * cw[3:4]
    sg = _sigmoid(cv)
    xbc = cv * sg
    out["cv"], out["sg"] = cv, sg
    X = xbc[:, :D]
    Bm = xbc[:, D:D + N_GROUPS * N_STATE]
    Cm = xbc[:, D + N_GROUPS * N_STATE:]
    pre = dtin + hp[0:1]
    dt = jnp.maximum(pre, 0.0) + jnp.log(1.0 + jnp.exp(-jnp.abs(pre)))
    a_row = -jnp.exp(hp[1:2])
    da = dt * a_row
    ri = lax.broadcasted_iota(jnp.int32, (L, L), 0)
    ci = lax.broadcasted_iota(jnp.int32, (L, L), 1)
    causal = ri >= ci
    cum = _dot01(causal.astype(F32), da, 3, split_lhs=False)
    cum_t = cum.T
    cum_last = cum[L - 1:L]
    eo = jnp.exp(cum)
    dec = jnp.exp(cum_last - cum)
    cd = jnp.exp(cum_last)
    exm = _expand_mat()
    rows8 = jnp.concatenate([cd, hp[2:3], jnp.zeros((6, LANES), F32)], axis=0)
    rep = _dot01(jnp.concatenate([dt, eo, dec, rows8], axis=0), exm, 2)
    dt_rep, eo_rep, dec_rep = rep[0:L], rep[L:2 * L], rep[2 * L:3 * L]
    cd_rep, dskip_rep = rep[3 * L:3 * L + 1], rep[3 * L + 1:3 * L + 2]
    xdt = X * dt_rep
    out.update(X=X, Bm=Bm, Cm=Cm, pre=pre, dt=dt, a_row=a_row, cum=cum, cum_t=cum_t, eo=eo, dec=dec, cd=cd,
               dt_rep=dt_rep, eo_rep=eo_rep, dec_rep=dec_rep, cd_rep=cd_rep, dskip_rep=dskip_rep, xdt=xdt,
               causal=causal, anti=(ri <= ci).astype(F32), exm=exm)
    G, lms, yoff, hnew, xdec = [], [], [], [], []
    for g in range(N_GROUPS):
        gs = slice(g * GROUP_W, (g + 1) * GROUP_W)
        Bg = Bm[:, g * N_STATE:(g + 1) * N_STATE]
        Cg = Cm[:, g * N_STATE:(g + 1) * N_STATE]
        Gg = _mm_nt(Cg, Bg)
        G.append(Gg)
        for hh in range(N_HEADS // N_GROUPS):
            h = g * (N_HEADS // N_GROUPS) + hh
            seg = cum[:, h:h + 1] - cum_t[h:h + 1, :]
            lm = jnp.where(causal, jnp.exp(jnp.minimum(seg, 0.0)), 0.0)
            lms.append(lm)
            hs = slice(h * HEAD_DIM, (h + 1) * HEAD_DIM)
            y_scr[:, hs] = _mm(Gg * lm, xdt[:, hs])
        xd = xdt[:, gs] * dec_rep[:, gs]
        xdec.append(xd)
        sgm = _mm_tn(Bg, xd)
        yoff.append(_mm(Cg, hprev[g]) * eo_rep[:, gs])
        hnew.append(hprev[g] * cd_rep[:, gs] + sgm)
    out.update(G=G, lms=lms, yoff=yoff, hnew=hnew, xdec=xdec)
    y = y_scr[...] + jnp.concatenate(yoff, axis=1) + dskip_rep * X
    sz = _sigmoid(z)
    silz = z * sz
    yz = y * silz
    rg, yn = [], []
    for g in range(N_GROUPS):
        gs = slice(g * GROUP_W, (g + 1) * GROUP_W)
        r = lax.rsqrt(jnp.mean(yz[:, gs] * yz[:, gs], axis=-1, keepdims=True) + EPS)
        rg.append(r)
        yn.append(yz[:, gs] * r)
    yn = jnp.concatenate(yn, axis=1)
    out.update(y=y, sz=sz, silz=silz, rg=rg, yn=yn)
    out["y_ssd"] = yn * gssd
    return out


def _mixer_fwd(pm, cw, cb, hp, gssd, wpool, pscale, nb, seq, shards):
    nc = seq // CHUNK
    ns = len(shards)
    steps = nb * nc

    def body(*refs):
        pm_ref, cw_ref, cb_ref, hp_ref, gs_ref, wp_ref, ps_ref = refs[:7]
        sh_refs = refs[7:7 + ns]
        ym_ref, hs_ref, cv_ref = refs[7 + ns:10 + ns]
        ga_refs = refs[10 + ns:10 + 2 * ns]
        halo_p, halo_x, state, y_scr, send, recv, loc = refs[10 + 2 * ns:]
        c = pl.program_id(1)
        step = pl.program_id(0) * nc + c
        gather = _Gather(sh_refs, ga_refs, send, recv, loc)
        gather.begin_hosted(step, steps)

        @pl.when(c == 0)
        def _():
            halo_p[...] = jnp.zeros_like(halo_p)
            halo_x[...] = jnp.zeros_like(halo_x)
            state[...] = jnp.zeros_like(state)

        up = pm_ref[:, 0:POOL_W]
        z = pm_ref[:, OFF_Z:OFF_XBC]
        ux = pm_ref[:, OFF_XBC:OFF_DT]
        hprev = [state[0], state[1]]
        hs_ref[0, 0, 0] = hprev[0]
        hs_ref[0, 0, 1] = hprev[1]
        o = _chunk_forward(up, z, ux, pm_ref[:, OFF_DT:], halo_p[...], halo_x[...], hprev, cw_ref[...], cb_ref[...],
                           hp_ref[...], gs_ref[...], wp_ref[...], ps_ref[...], c * CHUNK, y_scr)
        ym_ref[:, 0:POOL_W] = o["y_pool"].astype(BF16)
        ym_ref[:, POOL_W:] = o["y_ssd"].astype(BF16)
        cv_ref[...] = o["cv"]
        state[0] = o["hnew"][0]
        state[1] = o["hnew"][1]
        halo_p[...] = up[CHUNK - POOL_HALO:]
        halo_x[...] = ux[CHUNK - CONV_HALO:]
        gather.end_hosted(step, steps)

    def full(shape):
        return pl.BlockSpec(shape, lambda b, c: (0,) * len(shape))

    T = nb * seq
    return pl.pallas_call(
        body, name="mixer_fwd", grid=(nb, nc),
        in_specs=[pl.BlockSpec((CHUNK, PROJ_W), lambda b, c: (b * nc + c, 0)),
                  full((4, CONV_CH)), full((1, CONV_CH)), full((8, LANES)), full((1, D)),
                  full((4, LANES, LANES)), full((1, POOL_W))] + [ANY_SPEC] * ns,
        out_specs=[pl.BlockSpec((CHUNK, MIX_W), lambda b, c: (b * nc + c, 0)),
                   pl.BlockSpec((1, 1, N_GROUPS, N_STATE, GROUP_W), lambda b, c: (b, c, 0, 0, 0)),
                   pl.BlockSpec((CHUNK, CONV_CH), lambda b, c: (b * nc + c, 0))] + [ANY_SPEC] * ns,
        out_shape=[jax.ShapeDtypeStruct((T, MIX_W), BF16),
                   jax.ShapeDtypeStruct((nb, nc, N_GROUPS, N_STATE, GROUP_W), F32),
                   jax.ShapeDtypeStruct((T, CONV_CH), F32)]
        + [jax.ShapeDtypeStruct((N_DEV,) + v.shape, v.dtype) for v in shards],
        scratch_shapes=[pltpu.VMEM((POOL_HALO, POOL_W), F32), pltpu.VMEM((CONV_HALO, CONV_CH), F32),
                        pltpu.VMEM((N_GROUPS, N_STATE, GROUP_W), F32), pltpu.VMEM((CHUNK, D), F32)] + _gather_scratch(shards),
        compiler_params=_cparams(),
    )(pm, cw, cb, hp, gssd, wpool, pscale, *shards)


def _mixer_bwd(pm, cvs, dym, hstates, cw, cb, hp, gssd, wpool, pscale, nb, seq, after=()):
    nc = seq // CHUNK
    hpg = N_HEADS // N_GROUPS
    na = len(after)

    def body(*refs):
        (pm_ref, hpool_ref, cv_ref, dy_ref, hs_ref, cw_ref, cb_ref, hp_ref, gs_ref, wp_ref, ps_ref) = refs[:11]
        dpm_ref, dconv_ref, dhp_ref, dvec_ref, dwp_ref = refs[11 + na:16 + na]
        nxt_q, nxt_cv, rstate, y_scr, dx_scr = refs[16 + na:]
        b = pl.program_id(0)
        ci = pl.program_id(1)
        c = nc - 1 - ci

        @pl.when((b == 0) & (ci == 0))
        def _():
            for r in (dconv_ref, dhp_ref, dvec_ref, dwp_ref):
                r[...] = jnp.zeros_like(r)

        @pl.when(ci == 0)
        def _():
            nxt_q[...] = jnp.zeros_like(nxt_q)
            nxt_cv[...] = jnp.zeros_like(nxt_cv)
            rstate[...] = jnp.zeros_like(rstate)

        first = (c > 0).astype(F32)
        up = pm_ref[:, 0:POOL_W]
        z = pm_ref[:, OFF_Z:OFF_XBC]
        ux = pm_ref[:, OFF_XBC:OFF_DT]
        halo_p = hpool_ref[...] * first
        hprev = [hs_ref[0, 0, 0], hs_ref[0, 0, 1]]
        cw, cb, hp, gssd, wpool, pscale = cw_ref[...], cb_ref[...], hp_ref[...], gs_ref[...], wp_ref[...], ps_ref[...]
        o = _chunk_forward(up, z, ux, pm_ref[:, OFF_DT:], halo_p, None, hprev, cw, cb, hp, gssd, wpool, pscale,
                           c * CHUNK, y_scr, cv=cv_ref[...])
        L = CHUNK
        dy_pool = dy_ref[:, 0:POOL_W].astype(F32)
        dy_ssd = dy_ref[:, POOL_W:].astype(F32)

        dvec_ref[1:2, 0:POOL_W] += jnp.sum(dy_pool * jnp.concatenate(o["yp"], axis=1), axis=0, keepdims=True)
        dyp = dy_pool * pscale
        qs = []
        dps = []
        for gi in range(len(WINDOWS)):
            sl = slice(gi * LANES, (gi + 1) * LANES)
            dwp_ref[gi] += _mm_tn(o["p"][gi], dyp[:, sl])
            dpg = _mm_nt(dyp[:, sl], wpool[gi])
            dps.append(dpg)
            qs.append(dpg * o["inv"][gi])
        q = jnp.concatenate(qs, axis=1)
        e = jnp.concatenate([q, nxt_q[...]], axis=0)
        n = L + POOL_HALO
        s2 = e + pltpu.roll(e, n - 1, 0)
        s4 = s2 + pltpu.roll(s2, n - 2, 0)
        s8 = s4 + pltpu.roll(s4, n - 4, 0)
        s16 = s8 + pltpu.roll(s8, n - 8, 0)
        sums = (s2, s4, s8, s16)
        for gi in range(len(WINDOWS)):
            sl = slice(gi * LANES, (gi + 1) * LANES)
            dpm_ref[:, sl] = (sums[gi][:L, sl] - dps[gi]).astype(BF16)
        nxt_q[...] = q[:POOL_HALO]

        yn, y, silz, sz = o["yn"], o["y"], o["silz"], o["sz"]
        dvec_ref[0:1] += jnp.sum(dy_ssd * yn, axis=0, keepdims=True)
        dyn = dy_ssd * gssd
        dyz = []
        for g in range(N_GROUPS):
            gs = slice(g * GROUP_W, (g + 1) * GROUP_W)
            mean = jnp.mean(dyn[:, gs] * yn[:, gs], axis=-1, keepdims=True)
            dyz.append(o["rg"][g] * (dyn[:, gs] - yn[:, gs] * mean))
        dyz = jnp.concatenate(dyz, axis=1)
        dyv = dyz * silz
        dpm_ref[:, OFF_Z:OFF_XBC] = (dyz * y * (sz * (1.0 + z * (1.0 - sz)))).astype(BF16)

        X, Bm, Cm, xdt = o["X"], o["Bm"], o["Cm"], o["xdt"]
        exm = o["exm"]
        rdm = _reduce_mat()
        lane = lax.broadcasted_iota(jnp.int32, (1, LANES), 1)
        sub = lax.broadcasted_iota(jnp.int32, (LANES, 1), 0)
        dX = o["dskip_rep"] * dyv
        yoff_full = jnp.concatenate(o["yoff"], axis=1)
        rs = jnp.zeros((L, LANES), F32)
        cs_t = jnp.zeros((LANES, L), F32)
        dBs, dCs = [], []
        rh_sums = []
        ddec = []
        for g in range(N_GROUPS):
            gs = slice(g * GROUP_W, (g + 1) * GROUP_W)
            Bg = Bm[:, g * N_STATE:(g + 1) * N_STATE]
            Cg = Cm[:, g * N_STATE:(g + 1) * N_STATE]
            Gg = o["G"][g]
            R = rstate[g]
            dwm = dyv[:, gs] * o["eo_rep"][:, gs]
            dC = _mm_nt(dwm, hprev[g])
            dH = _mm_tn(Cg, dwm)
            dG = jnp.zeros((L, L), F32)
            for hh in range(hpg):
                h = g * hpg + hh
                hs = slice(h * HEAD_DIM, (h + 1) * HEAD_DIM)
                lm = o["lms"][h]
                m_h = Gg * lm
                dM = _mm_nt(dyv[:, hs], xdt[:, hs])
                dx_scr[:, hs] = _mm_tn(m_h, dyv[:, hs])
                qm = dM * m_h
                rs = rs + jnp.sum(qm, axis=1, keepdims=True) * (lane == h).astype(F32)
                cs_t = cs_t + (sub == h).astype(F32) * jnp.sum(qm, axis=0, keepdims=True)
                dG = dG + dM * lm
            dC = dC + _mm(dG, Bg)
            dB = _mm_tn(dG, Cg)
            zx = _mm(Bg, R)
            dxdt_state = zx * o["dec_rep"][:, gs]
            ddec.append(zx * xdt[:, gs])
            dB = dB + _mm_nt(o["xdec"][g], R)
            rh_sums.append(jnp.sum(R * hprev[g], axis=0, keepdims=True))
            rstate[g] = dH + o["cd_rep"][:, gs] * R
            dx_scr[:, gs] = dx_scr[:, gs] + dxdt_state
            dBs.append(dB)
            dCs.append(dC)
        dxdt = dx_scr[...]
        tail = jnp.concatenate([jnp.sum(dyv * X, axis=0, keepdims=True), jnp.concatenate(rh_sums, axis=1),
                                jnp.zeros((6, D), F32)], axis=0)
        red = _dot01(jnp.concatenate([dyv * yoff_full, jnp.concatenate(ddec, axis=1), dxdt * X, tail], axis=0), rdm, 2)
        d_dskip, dcd_row = red[3 * L:3 * L + 1], red[3 * L + 1:3 * L + 2]
        ddec_h = red[L:2 * L] * o["dec"]
        dcum_last = jnp.sum(ddec_h, axis=0, keepdims=True) + dcd_row * o["cd"]
        dcum = red[0:L] + rs - cs_t.T - ddec_h + (sub == L - 1).astype(F32) * dcum_last
        dda = _dot01(o["anti"], dcum, 3, split_lhs=False)
        ddt_v = dda * o["a_row"] + red[2 * L:3 * L]
        dX = dX + dxdt * o["dt_rep"]
        head_mask = (lane < N_HEADS).astype(F32)
        d_alog = jnp.sum(dda * o["dt"], axis=0, keepdims=True) * o["a_row"] * head_mask
        dpre = ddt_v * _sigmoid(o["pre"]) * head_mask
        dpm_ref[:, OFF_DT:] = dpre.astype(BF16)
        d_dtb = jnp.sum(dpre, axis=0, keepdims=True)
        dhp_ref[...] += jnp.concatenate([d_dtb, d_alog, d_dskip * head_mask, jnp.zeros((5, LANES), F32)], axis=0)

        dxbc = jnp.concatenate([dX] + dBs + dCs, axis=1)
        sg, cv = o["sg"], o["cv"]
        dcv = dxbc * (sg * (1.0 + cv * (1.0 - sg)))
        e2 = jnp.concatenate([dcv, nxt_cv[...]], axis=0)
        n2 = L + CONV_HALO
        ahead = [dcv, pltpu.roll(e2, n2 - 1, 0)[:L], pltpu.roll(e2, n2 - 2, 0)[:L], pltpu.roll(e2, n2 - 3, 0)[:L]]
        dconv_ref[0:5] += jnp.concatenate(
            [jnp.sum(ux * ahead[3 - k], axis=0, keepdims=True) for k in range(4)]
            + [jnp.sum(dcv, axis=0, keepdims=True)], axis=0)
        dux = ahead[0] * cw[3:4] + ahead[1] * cw[2:3] + ahead[2] * cw[1:2] + ahead[3] * cw[0:1]
        dpm_ref[:, OFF_XBC:OFF_DT] = dux.astype(BF16)
        nxt_cv[...] = dcv[:CONV_HALO]

    def full(shape):
        return pl.BlockSpec(shape, lambda b, c: (0,) * len(shape))

    def rowblk(b, c):
        return b * nc + (nc - 1 - c)

    hp_blocks = CHUNK // POOL_HALO
    T = nb * seq
    return pl.pallas_call(
        body, name="mixer_bwd", grid=(nb, nc),
        in_specs=[pl.BlockSpec((CHUNK, PROJ_W), lambda b, c: (rowblk(b, c), 0)),
                  pl.BlockSpec((POOL_HALO, POOL_W), lambda b, c: (jnp.maximum(rowblk(b, c) * hp_blocks - 1, 0), 0)),
                  pl.BlockSpec((CHUNK, CONV_CH), lambda b, c: (rowblk(b, c), 0)),
                  pl.BlockSpec((CHUNK, MIX_W), lambda b, c: (rowblk(b, c), 0)),
                  pl.BlockSpec((1, 1, N_GROUPS, N_STATE, GROUP_W), lambda b, c: (b, nc - 1 - c, 0, 0, 0)),
                  full((4, CONV_CH)), full((1, CONV_CH)), full((8, LANES)), full((1, D)),
                  full((4, LANES, LANES)), full((1, POOL_W))] + [ANY_SPEC] * na,
        out_specs=[pl.BlockSpec((CHUNK, PROJ_W), lambda b, c: (rowblk(b, c), 0)),
                   full((8, CONV_CH)), full((8, LANES)), full((8, D)), full((4, LANES, LANES))],
        out_shape=[jax.ShapeDtypeStruct((T, PROJ_W), BF16),
                   jax.ShapeDtypeStruct((8, CONV_CH), F32), jax.ShapeDtypeStruct((8, LANES), F32),
                   jax.ShapeDtypeStruct((8, D), F32), jax.ShapeDtypeStruct((4, LANES, LANES), F32)],
        scratch_shapes=[pltpu.VMEM((POOL_HALO, POOL_W), F32), pltpu.VMEM((CONV_HALO, CONV_CH), F32),
                        pltpu.VMEM((N_GROUPS, N_STATE, GROUP_W), F32), pltpu.VMEM((CHUNK, D), F32),
                        pltpu.VMEM((CHUNK, D), F32)],
        compiler_params=_cparams(),
    )(pm, pm, cvs, dym, hstates, cw, cb, hp, gssd, wpool, pscale, *after)


def _mlp_fused(x2, ymix, target, mod, g_mlp, g_final, w_out, w_up, w_down, seq):
    T = x2.shape[0]
    tm = min(256, seq)
    tps = seq // tm
    nblk = D_FF // FF_BLK

    def body(x_ref, ym_ref, tg_ref, mod_ref, gm_ref, gf_ref, wo_ref, wu_ref, wd_ref,
             da_ref, dym_ref, dh1_ref, u2_ref, f_ref, dup_ref, ddn_ref, dmod_ref, acc_ref, relu_scr):
        i = pl.program_id(0)

        @pl.when(i == 0)
        def _():
            acc_ref[...] = jnp.zeros_like(acc_ref)

        @pl.when(i % tps == 0)
        def _():
            dmod_ref[...] = jnp.zeros_like(dmod_ref)

        md = mod_ref[0]
        gate_m, shift_f, scale_f, gate_f = md[2:3], md[3:4], md[4:5], md[5:6]
        g_mlp, g_fin = gm_ref[...], gf_ref[...]
        a = jnp.dot(ym_ref[...], wo_ref[...], preferred_element_type=F32)
        h1 = x_ref[...] + gate_m * a
        r2 = lax.rsqrt(jnp.mean(h1 * h1, axis=-1, keepdims=True) + EPS)
        n2 = h1 * r2
        u2 = (n2 * g_mlp) * (1.0 + scale_f) + shift_f
        u2b = u2.astype(BF16)
        u2_ref[...] = u2b
        dn = jnp.zeros((tm, D), F32)
        for j in range(nblk):
            js = slice(j * FF_BLK, (j + 1) * FF_BLK)
            upj = jnp.maximum(jnp.dot(u2b, wu_ref[j], preferred_element_type=F32), 0.0)
            relu_scr[:, js] = upj
            fj = (upj * upj).astype(BF16)
            f_ref[:, js] = fj
            dn = dn + jnp.dot(fj, wd_ref[j], preferred_element_type=F32)
        h2 = h1 + gate_f * dn
        r3 = lax.rsqrt(jnp.mean(h2 * h2, axis=-1, keepdims=True) + EPS)
        n3 = h2 * r3
        err = n3 * g_fin - tg_ref[...]
        loss = 0.5 * jnp.sum(jnp.mean(err * err, axis=-1, keepdims=True), axis=0, keepdims=True)
        dout = err * (1.0 / D)
        d_gfin = jnp.sum(dout * n3, axis=0, keepdims=True)
        dn3 = dout * g_fin
        dh2 = r3 * (dn3 - n3 * jnp.mean(dn3 * n3, axis=-1, keepdims=True))
        d_gate_f = jnp.sum(dh2 * dn, axis=0, keepdims=True)
        ddn = (gate_f * dh2).astype(BF16)
        ddn_ref[...] = ddn
        du2 = jnp.zeros((tm, D), F32)
        for j in range(nblk):
            js = slice(j * FF_BLK, (j + 1) * FF_BLK)
            dfj = lax.dot_general(ddn, wd_ref[j], (((1,), (1,)), ((), ())), preferred_element_type=F32)
            dupj = (dfj * (2.0 * relu_scr[:, js])).astype(BF16)
            dup_ref[:, js] = dupj
            du2 = du2 + lax.dot_general(dupj, wu_ref[j], (((1,), (1,)), ((), ())), preferred_element_type=F32)
        d_scale_f = jnp.sum(du2 * (n2 * g_mlp), axis=0, keepdims=True)
        d_shift_f = jnp.sum(du2, axis=0, keepdims=True)
        d_gmlp = jnp.sum(du2 * (1.0 + scale_f) * n2, axis=0, keepdims=True)
        dn2 = du2 * (g_mlp * (1.0 + scale_f))
        dh1 = dh2 + r2 * (dn2 - n2 * jnp.mean(dn2 * n2, axis=-1, keepdims=True))
        dh1_ref[...] = dh1
        d_gate_m = jnp.sum(dh1 * a, axis=0, keepdims=True)
        da = (gate_m * dh1).astype(BF16)
        da_ref[...] = da
        dym_ref[...] = lax.dot_general(da, wo_ref[...], (((1,), (1,)), ((), ())),
                                       preferred_element_type=F32).astype(BF16)
        dmod_ref[0] += jnp.concatenate([jnp.zeros((2, D), F32), d_gate_m, d_shift_f, d_scale_f, d_gate_f,
                                        jnp.zeros((2, D), F32)], axis=0)
        acc_ref[...] += jnp.concatenate([d_gmlp, d_gfin, loss * jnp.ones((1, D), F32), jnp.zeros((5, D), F32)], axis=0)

    whole = pl.BlockSpec(memory_space=pltpu.VMEM)

    def tok(w):
        return pl.BlockSpec((tm, w), lambda i: (i, 0))

    def vec():
        return pl.BlockSpec((1, D), lambda i: (0, 0))

    nb = T // seq
    return pl.pallas_call(
        body, name="mlp_fused", grid=(T // tm,),
        in_specs=[tok(D), tok(MIX_W), tok(D), pl.BlockSpec((1, 8, D), lambda i: (i // tps, 0, 0)), vec(), vec(),
                  whole, whole, whole],
        out_specs=[tok(D), tok(MIX_W), tok(D), tok(D), tok(D_FF), tok(D_FF), tok(D),
                   pl.BlockSpec((1, 8, D), lambda i: (i // tps, 0, 0)), pl.BlockSpec((8, D), lambda i: (0, 0))],
        out_shape=[jax.ShapeDtypeStruct((T, D), BF16), jax.ShapeDtypeStruct((T, MIX_W), BF16),
                   jax.ShapeDtypeStruct((T, D), F32), jax.ShapeDtypeStruct((T, D), BF16),
                   jax.ShapeDtypeStruct((T, D_FF), BF16), jax.ShapeDtypeStruct((T, D_FF), BF16),
                   jax.ShapeDtypeStruct((T, D), BF16), jax.ShapeDtypeStruct((nb, 8, D), F32),
                   jax.ShapeDtypeStruct((8, D), F32)],
        scratch_shapes=[pltpu.VMEM((tm, D_FF), F32)],
        compiler_params=_cparams(),
    )(x2, ymix, target, mod, g_mlp, g_final, w_out, w_up, w_down)


def _in_bwd(x2, dh1, dpb, mod, g_mix, w_cat, dmod_a, acc_a, seq):
    T = x2.shape[0]
    tm = min(1024, seq)
    tps = seq // tm
    steps = T // tm

    def body(x_ref, dh_ref, dpb_ref, mod_ref, g_ref, w_ref, dma_ref, acca_ref, dx_ref, dmod_ref, acc_ref):
        i = pl.program_id(0)

        @pl.when(i == 0)
        def _():
            acc_ref[...] = acca_ref[...]

        @pl.when(i % tps == 0)
        def _():
            dmod_ref[...] = dma_ref[...]

        du = lax.dot_general(dpb_ref[...], w_ref[...], (((1,), (1,)), ((), ())), preferred_element_type=F32)
        x = x_ref[...]
        md = mod_ref[0]
        g = g_ref[...]
        r = lax.rsqrt(jnp.mean(x * x, axis=-1, keepdims=True) + EPS)
        n1 = x * r
        d_scale = jnp.sum(du * (n1 * g), axis=0, keepdims=True)
        d_shift = jnp.sum(du, axis=0, keepdims=True)
        d_g = jnp.sum(du * (1.0 + md[1:2]) * n1, axis=0, keepdims=True)
        dn1 = du * (g * (1.0 + md[1:2]))
        dx_ref[...] = dh_ref[...] + r * (dn1 - n1 * jnp.mean(dn1 * n1, axis=-1, keepdims=True))
        dmod_ref[0] += jnp.concatenate([d_shift, d_scale, jnp.zeros((6, D), F32)], axis=0)
        acc_ref[...] += jnp.concatenate([jnp.zeros((3, D), F32), d_g, jnp.zeros((4, D), F32)], axis=0)

    whole = pl.BlockSpec(memory_space=pltpu.VMEM)
    nb = T // seq
    return pl.pallas_call(
        body, name="in_bwd", grid=(steps,),
        in_specs=[pl.BlockSpec((tm, D), lambda i: (i, 0)), pl.BlockSpec((tm, D), lambda i: (i, 0)),
                  pl.BlockSpec((tm, PROJ_W), lambda i: (i, 0)),
                  pl.BlockSpec((1, 8, D), lambda i: (i // tps, 0, 0)), pl.BlockSpec((1, D), lambda i: (0, 0)),
                  whole, pl.BlockSpec((1, 8, D), lambda i: (i // tps, 0, 0)), pl.BlockSpec((8, D), lambda i: (0, 0))],
        out_specs=[pl.BlockSpec((tm, D), lambda i: (i, 0)),
                   pl.BlockSpec((1, 8, D), lambda i: (i // tps, 0, 0)), pl.BlockSpec((8, D), lambda i: (0, 0))],
        out_shape=[jax.ShapeDtypeStruct((T, D), F32),
                   jax.ShapeDtypeStruct((nb, 8, D), F32), jax.ShapeDtypeStruct((8, D), F32)],
        compiler_params=_cparams(),
    )(x2, dh1, dpb, mod, g_mix, w_cat, dmod_a, acc_a)


def _dw_in(u_b, dpb, in_cols, shards):
    T = u_b.shape[0]
    bk = min(512, T)
    nk = T // bk
    ns = len(shards)
    starts = [(in_cols * j // LANES) * LANES for j in range(N_DEV)]
    assert all(s + DW_IN_WIN <= PROJ_W and in_cols * (j + 1) <= s + DW_IN_WIN for j, s in enumerate(starts))

    def body(*refs):
        u_ref, d_ref = refs[:2]
        sh_refs = refs[2:2 + ns]
        o_ref = refs[2 + ns]
        ga_refs = refs[3 + ns:3 + 2 * ns]
        acc, send, recv, loc = refs[3 + 2 * ns:]
        k = pl.program_id(0)
        gather = _Gather(sh_refs, ga_refs, send, recv, loc)
        gather.begin_hosted(k, nk)

        @pl.when(k == 0)
        def _():
            acc[...] = jnp.zeros_like(acc)

        ut = u_ref[...].T
        for j in range(N_DEV):
            acc[j] += jnp.dot(ut, d_ref[:, starts[j]:starts[j] + DW_IN_WIN], preferred_element_type=F32)

        @pl.when(k == nk - 1)
        def _():
            for j in range(N_DEV):
                off = in_cols * j - starts[j]
                o_ref[j] = acc[j][:, off:off + in_cols].astype(BF16)

        gather.end_hosted(k, nk)

    return pl.pallas_call(
        body, name="dw_in", grid=(nk,),
        in_specs=[pl.BlockSpec((bk, D), lambda k: (k, 0)), pl.BlockSpec((bk, PROJ_W), lambda k: (k, 0))]
        + [ANY_SPEC] * ns,
        out_specs=[pl.BlockSpec((N_DEV, D, in_cols), lambda k: (0, 0, 0))] + [ANY_SPEC] * ns,
        out_shape=[jax.ShapeDtypeStruct((N_DEV, D, in_cols), BF16)]
        + [jax.ShapeDtypeStruct((N_DEV,) + v.shape, v.dtype) for v in shards],
        scratch_shapes=[pltpu.VMEM((N_DEV, D, DW_IN_WIN), F32)] + _gather_scratch(shards),
        compiler_params=_cparams(),
    )(u_b, dpb, *shards)


def _dw_blocks(a, b, name, by_rows, per_step=1, after=()):
    T, M = a.shape
    N = b.shape[1]
    bk = min(2048, T)
    nk = T // bk
    whole = pl.BlockSpec(memory_space=pltpu.VMEM)
    if by_rows:
        rows = M // N_DEV
        am = rows * per_step
        nblk = N_DEV // per_step
        a_spec, b_spec = pl.BlockSpec((bk, am), lambda i, k: (k, i)), whole
        out_blk, acc_shape = (per_step, rows, N), (am, N)
    else:
        cols = N // N_DEV
        nblk = N_DEV
        a_spec, b_spec = whole, pl.BlockSpec((bk, cols), lambda i, k: (k, i))
        out_blk, acc_shape = (1, M, cols), (M, cols)

    def body(a_ref, b_ref, *rest):
        o_ref, acc = rest[len(after):]
        k = pl.program_id(1)

        @pl.when(k == 0)
        def _():
            acc[...] = jnp.zeros_like(acc)

        tok = pl.ds(pl.multiple_of(k * bk, bk), bk)
        a_blk = a_ref[...] if by_rows else a_ref[tok, :]
        b_blk = b_ref[tok, :] if by_rows else b_ref[...]
        acc[...] += lax.dot_general(a_blk, b_blk, (((0,), (0,)), ((), ())), preferred_element_type=F32)

        @pl.when(k == nk - 1)
        def _():
            o_ref[...] = acc[...].reshape(out_blk).astype(BF16)

    return pl.pallas_call(
        body, name=name, grid=(nblk, nk), in_specs=[a_spec, b_spec] + [ANY_SPEC] * len(after),
        out_specs=pl.BlockSpec(out_blk, lambda i, k: (i, 0, 0)),
        out_shape=jax.ShapeDtypeStruct((N_DEV,) + out_blk[1:], BF16),
        scratch_shapes=[pltpu.VMEM(acc_shape, F32)],
        compiler_params=_cparams(),
    )(a, b, *after)


def _adam_parts(parts, w, m, v, name):
    rows, cols = w.shape
    br = rows
    for cand in range(rows, 15, -16):
        if rows % cand == 0 and cand * cols * 4 <= ADAM_BLOCK_BYTES:
            br = cand
            break

    def body(p_ref, w_ref, m_ref, v_ref, g_out, dl_out, m_out, v_out):
        g = p_ref[0].astype(F32)
        for k in range(1, N_DEV):
            g = g + p_ref[k].astype(F32)
        g_out[...] = g
        dl, mn, vn = _adam_math(w_ref[...], g, m_ref[...], v_ref[...])
        dl_out[...] = dl
        m_out[...] = mn
        v_out[...] = vn

    wspec = pl.BlockSpec((br, cols), lambda i: (i, 0))
    return pl.pallas_call(
        body, name=name, grid=(rows // br,),
        in_specs=[pl.BlockSpec((N_DEV, br, cols), lambda i: (0, i, 0)), wspec, wspec, wspec],
        out_specs=[wspec] * 4, out_shape=[jax.ShapeDtypeStruct((rows, cols), F32)] * 4,
        compiler_params=_cparams(),
    )(parts, w, m, v)


def _adam_plain(g, w, m, v, name):
    def body(g_ref, w_ref, m_ref, v_ref, dl_out, m_out, v_out):
        dl, mn, vn = _adam_math(w_ref[...], g_ref[...], m_ref[...], v_ref[...])
        dl_out[...] = dl
        m_out[...] = mn
        v_out[...] = vn

    return pl.pallas_call(body, name=name, out_shape=[jax.ShapeDtypeStruct(w.shape, F32)] * 3,
                          compiler_params=_cparams())(g, w, m, v)


SMALL_PARAMS = ("b_ada", "g_mix", "conv_b", "dt_bias", "a_log", "d_skip", "g_ssd", "pool_scale", "g_mlp", "g_final")


def _small_adam(gathered, params):
    n_par = len(SMALL_PARAMS)
    nb = gathered[0].shape[1]

    def body(*refs):
        dmod_ref, acc_ref, conv_ref, vec_ref, hd_ref = refs[:5]
        par_refs = refs[5:5 + 3 * n_par]
        out_refs = refs[5 + 3 * n_par:5 + 7 * n_par]
        cw_out, acc_out = refs[5 + 7 * n_par:]

        def total(ref):
            t = ref[0]
            for k in range(1, N_DEV):
                t = t + ref[k]
            return t

        dm = total(dmod_ref)
        dmb = dm[0]
        for b in range(1, nb):
            dmb = dmb + dm[b]
        ac, cv, vc, hd = total(acc_ref), total(conv_ref), total(vec_ref), total(hd_ref)
        cw_out[...] = cv[0:4]
        acc_out[...] = ac
        grads = {
            "b_ada": jnp.concatenate([dmb[r:r + 1] for r in range(6)], axis=1), "g_mix": ac[3:4], "conv_b": cv[4:5],
            "dt_bias": hd[0:1, 0:N_HEADS], "a_log": hd[1:2, 0:N_HEADS], "d_skip": hd[2:3, 0:N_HEADS],
            "g_ssd": vc[0:1], "pool_scale": vc[1:2, 0:POOL_W], "g_mlp": ac[0:1], "g_final": ac[1:2],
        }
        for i, name in enumerate(SMALL_PARAMS):
            w_ref, m_ref, v_ref = par_refs[3 * i:3 * i + 3]
            g = grads[name]
            dl, mn, vn = _adam_math(w_ref[...], g, m_ref[...], v_ref[...])
            g_o, d_o, m_o, v_o = out_refs[4 * i:4 * i + 4]
            g_o[...] = g
            d_o[...] = dl
            m_o[...] = mn
            v_o[...] = vn

    flat = [a for name in SMALL_PARAMS for a in params[name]]
    out_shape = [jax.ShapeDtypeStruct(params[name][0].shape, F32) for name in SMALL_PARAMS for _ in range(4)]
    out_shape += [jax.ShapeDtypeStruct((4, CONV_CH), F32), jax.ShapeDtypeStruct((8, D), F32)]
    return pl.pallas_call(body, name="small_adam", out_shape=out_shape, compiler_params=_cparams())(*gathered, *flat)


def kernel(x, c, w_ada, b_ada, g_mix, w_in, conv_w, conv_b, dt_bias, a_log, d_skip, g_ssd, w_pool, pool_scale, w_out, g_mlp, w_up, w_down, g_final, loss_target, m_w_ada, m_b_ada, m_g_mix, m_w_in, m_conv_w, m_conv_b, m_dt_bias, m_a_log, m_d_skip, m_g_ssd, m_w_pool, m_pool_scale, m_w_out, m_g_mlp, m_w_up, m_w_down, m_g_final, v_w_ada, v_b_ada, v_g_mix, v_w_in, v_conv_w, v_conv_b, v_dt_bias, v_a_log, v_d_skip, v_g_ssd, v_w_pool, v_pool_scale, v_w_out, v_g_mlp, v_w_up, v_w_down, v_g_final):
    nb, seq, _ = x.shape
    T = nb * seq
    me = 4 * lax.axis_index("x") + 2 * lax.axis_index("y") + lax.axis_index("c")
    in_cols = w_in.shape[2]
    ada_cols = w_ada.shape[2]
    cw_cols = conv_w.shape[2]

    c_g, cw_g, win_g = _all_gather([c, conv_w[0], w_in[0].astype(BF16)], "ag_first")
    c_all = c_g.reshape(N_DEV * nb, D)
    cw_full = cw_g.transpose(1, 0, 2).reshape(4, CONV_CH)

    b_slice = lax.dynamic_slice(b_ada, (0, me * ada_cols), (1, ada_cols))
    mod_cols = _ada_fwd(c_all, w_ada[0], b_slice)
    (mod_g,) = _all_gather([mod_cols], "ag_mod")
    mod_all = mod_g.transpose(1, 0, 2).reshape(N_DEV * nb, 6, D)
    mod_mine = lax.dynamic_slice(mod_all, (me * nb, 0, 0), (nb, 6, D))
    mod = jnp.pad(mod_mine, ((0, 0), (0, 2), (0, 0)))

    x2 = x.reshape(T, D)
    tg2 = loss_target.reshape(T, D)
    heads = jnp.pad(jnp.concatenate([dt_bias, a_log, d_skip], axis=0), ((0, 5), (0, LANES - N_HEADS)))
    wpool_b = w_pool[0]
    u_b, pm, w_cat, wup_g = _mix_in(x2, mod, g_mix, win_g, seq, [w_up[0].astype(BF16)])
    ymix, hstates, cvs, wout_g, wdn_g = _mixer_fwd(
        pm, cw_full, conv_b, heads, g_ssd, wpool_b, pool_scale, nb, seq,
        [w_out[0].astype(BF16), w_down[0].astype(BF16)])
    da_b, dym, dh1, u2_b, f_b, dup_b, ddn_b, dmod_a, acc_a = _mlp_fused(
        x2, ymix, tg2, mod, g_mlp, g_final.reshape(1, D), wout_g.reshape(MIX_W, D), wup_g, wdn_g, seq)

    gout_p = _dw_blocks(ymix, da_b, "dw_out", True, per_step=4)
    ex_out = _exchange_start(gout_p, "gout_start")
    gup_p = _dw_blocks(u2_b, dup_b, "dw_up", False, after=[ex_out[4]])
    ex_up = _exchange_start(gup_p, "gup_start")
    gdn_p = _dw_blocks(f_b, ddn_b, "dw_down", True, after=[ex_up[4]])
    ex_dn = _exchange_start(gdn_p, "gdn_start")
    dpb, d_conv, d_heads, d_vec, d_wpool = _mixer_bwd(
        pm, cvs, dym, hstates, cw_full, conv_b, heads, g_ssd, wpool_b, pool_scale, nb, seq, after=[ex_dn[4]])
    gin_p, conv_g, vec_g, heads_g, wpool_parts = _dw_in(
        u_b, dpb, in_cols, [d_conv, d_vec, d_heads, d_wpool.reshape(4 * LANES, LANES)])
    ex_in = _exchange_start(gin_p, "gin_start")
    grad_x2, dmod, acc = _in_bwd(x2, dh1, dpb, mod, g_mix + ex_in[4][0:1, 0:1], w_cat, dmod_a, acc_a, seq)

    def landed(ex, after, name):
        own, land = _exchange_wait(ex[0], ex[1], ex[2], ex[3], after, name)
        mine = lax.dynamic_slice(own, (me, 0, 0), (1,) + own.shape[1:])
        return lax.dynamic_update_slice(land, mine, (me, 0, 0))

    gout_r, gup_r, gdn_r = landed(ex_out, dmod, "gout_wait"), landed(ex_up, dmod, "gup_wait"), landed(ex_dn, dmod, "gdn_wait")
    g_out, d_out, nm_out, nv_out = _adam_parts(gout_r, w_out[0], m_w_out[0], v_w_out[0], "adam_w_out")
    g_up, d_up, nm_up, nv_up = _adam_parts(gup_r, w_up[0], m_w_up[0], v_w_up[0], "adam_w_up")
    g_dn, d_dn, nm_dn, nv_dn = _adam_parts(gdn_r, w_down[0], m_w_down[0], v_w_down[0], "adam_w_down")

    dmod_g, acc_g = _all_gather([dmod, acc], "ag_small_bwd", after=[nm_out, nm_up, nm_dn])
    pool2 = (4 * LANES, LANES)
    wpool_outs = _adam_parts(wpool_parts, w_pool.reshape(pool2), m_w_pool.reshape(pool2), v_w_pool.reshape(pool2),
                             "adam_w_pool")
    small_params = {
        "b_ada": (b_ada, m_b_ada, v_b_ada), "g_mix": (g_mix, m_g_mix, v_g_mix), "conv_b": (conv_b, m_conv_b, v_conv_b),
        "dt_bias": (dt_bias, m_dt_bias, v_dt_bias), "a_log": (a_log, m_a_log, v_a_log),
        "d_skip": (d_skip, m_d_skip, v_d_skip), "g_ssd": (g_ssd, m_g_ssd, v_g_ssd),
        "pool_scale": (pool_scale, m_pool_scale, v_pool_scale), "g_mlp": (g_mlp, m_g_mlp, v_g_mlp),
        "g_final": tuple(a.reshape(1, D) for a in (g_final, m_g_final, v_g_final)),
    }
    small_res = _small_adam([dmod_g, acc_g, conv_g, vec_g, heads_g], small_params)
    g_cw_full, acc_sum = small_res[-2:]
    loss = acc_sum[2, 0]

    g_cw = lax.dynamic_slice(g_cw_full, (0, me * cw_cols), (4, cw_cols))
    d_cwp, nm_cwp, nv_cwp = _adam_plain(g_cw, conv_w[0], m_conv_w[0], v_conv_w[0], "adam_conv_w")

    dmod_all = dmod_g[:, :, 0:6].reshape(N_DEV * nb, 6 * D)
    dmod_slice = lax.dynamic_slice(dmod_all, (0, me * ada_cols), (N_DEV * nb, ada_cols))
    g_ada, d_ada, nm_ada, nv_ada = _ada_bwd_adam(c_all, dmod_slice, w_ada[0], m_w_ada[0], v_w_ada[0])

    ex_after = nm_ada[0:8, 0:LANES] + acc_sum[:, 0:LANES]
    gin_r = landed(ex_in, ex_after, "gin_wait")
    g_in, d_in, nm_in, nv_in = _adam_parts(gin_r, w_in[0], m_w_in[0], v_w_in[0], "adam_w_in")

    def small_outs(kind, wpool):
        res = {name: small_res[4 * i + kind] for i, name in enumerate(SMALL_PARAMS)}
        res["g_final"] = res["g_final"].reshape(D)
        res["w_pool"] = wpool.reshape(1, 4, LANES, LANES)
        return res

    def big_outs(ada, win, cwp, wout, wup, wdn):
        return {"w_ada": ada[None], "w_in": win.reshape(1, D, in_cols), "conv_w": cwp[None], "w_out": wout[None],
                "w_up": wup[None], "w_down": wdn[None]}

    order = ["w_ada", "b_ada", "g_mix", "w_in", "conv_w", "conv_b", "dt_bias", "a_log", "d_skip", "g_ssd", "w_pool",
             "pool_scale", "w_out", "g_mlp", "w_up", "w_down", "g_final"]
    groups = [
        {**small_outs(0, wpool_outs[0]), **big_outs(g_ada, g_in, g_cw, g_out, g_up, g_dn)},
        {**small_outs(1, wpool_outs[1]), **big_outs(d_ada, d_in, d_cwp, d_out, d_up, d_dn)},
        {**small_outs(2, wpool_outs[2]), **big_outs(nm_ada, nm_in, nm_cwp, nm_out, nm_up, nm_dn)},
        {**small_outs(3, wpool_outs[3]), **big_outs(nv_ada, nv_in, nv_cwp, nv_out, nv_up, nv_dn)},
    ]
    outs = [loss, grad_x2.reshape(nb, seq, D)]
    for grp in groups:
        outs += [grp[n] for n in order]
    return tuple(outs)
```

```python
import functools

import jax
import jax.numpy as jnp
from jax import lax
from jax.experimental import pallas as pl
from jax.experimental.pallas import tpu as pltpu

F32, BF16 = jnp.float32, jnp.bfloat16
MESH = pl.DeviceIdType.MESH
N_DEV = 8
D = 1024
LANES = 128
CHUNK = 128
POOL_W = 512
WINDOWS = (2, 4, 8, 16)
N_HEADS = 16
HEAD_DIM = 64
N_GROUPS = 2
GROUP_W = 512
N_STATE = 128
CONV_CH = 1536
OFF_Z, OFF_XBC, OFF_DT, IN_W = 512, 1536, 3072, 3088
PROJ_W = OFF_DT + LANES
MIX_W = 1536
D_FF = 4096
FF_BLK = 512
EPS = 1e-5
LR, B1, B2, AEPS, WD, STEP = 0.001, 0.9, 0.999, 1e-08, 0.01, 10
POOL_HALO = 16
CONV_HALO = 8
VMEM_LIMIT = 56 << 20
ADAM_BLOCK_BYTES = 1 << 20
DW_IN_WIN = 512


def _cparams(**kw):
    return pltpu.CompilerParams(vmem_limit_bytes=VMEM_LIMIT, **kw)


def _mm(a, b):
    return jnp.dot(a.astype(BF16), b.astype(BF16), preferred_element_type=F32)


def _mm_nt(a, b):
    return lax.dot_general(a.astype(BF16), b.astype(BF16), (((1,), (1,)), ((), ())), preferred_element_type=F32)


def _mm_tn(a, b):
    return lax.dot_general(a.astype(BF16), b.astype(BF16), (((0,), (0,)), ((), ())), preferred_element_type=F32)


def _split_bf16(v, terms):
    parts, rest = [], v
    for t in range(terms):
        p = rest.astype(BF16)
        parts.append(p)
        if t + 1 < terms:
            rest = rest - p.astype(F32)
    return parts


def _dot01(a, b, terms, split_lhs=True):
    if split_lhs:
        bb = b.astype(BF16)
        prods = [jnp.dot(p, bb, preferred_element_type=F32) for p in _split_bf16(a, terms)]
    else:
        ab = a.astype(BF16)
        prods = [jnp.dot(ab, p, preferred_element_type=F32) for p in _split_bf16(b, terms)]
    out = prods[0]
    for q in prods[1:]:
        out = out + q
    return out


def _sigmoid(v):
    return 1.0 / (1.0 + jnp.exp(-v))


def _expand_mat():
    r = lax.broadcasted_iota(jnp.int32, (LANES, D), 0)
    c = lax.broadcasted_iota(jnp.int32, (LANES, D), 1)
    return (r == c // HEAD_DIM).astype(F32)


def _reduce_mat():
    r = lax.broadcasted_iota(jnp.int32, (D, LANES), 0)
    c = lax.broadcasted_iota(jnp.int32, (D, LANES), 1)
    return (c == r // HEAD_DIM).astype(F32)


def _pos():
    return lax.axis_index("x"), lax.axis_index("y"), lax.axis_index("c")


GATHER_PIECES = 4
GATHER_PIECE_BYTES = 96 << 10


def _pieces(shape, dtype):
    rows = shape[0]
    size = jnp.dtype(dtype).itemsize
    for d in shape:
        size *= d
    whole_tiles = rows % (GATHER_PIECES * 16) == 0
    return GATHER_PIECES if whole_tiles and size // GATHER_PIECES >= GATHER_PIECE_BYTES else 1


class _Gather:
    def __init__(self, x_refs, o_refs, send, recv, loc):
        self.x_refs, self.o_refs, self.send, self.recv, self.loc = x_refs, o_refs, send, recv, loc
        self.n = len(x_refs)
        self.pieces = [_pieces(r.shape, r.dtype) for r in x_refs]
        self.base = [7 * sum(self.pieces[:a]) for a in range(self.n)]
        x, y, c = _pos()
        self.c = c
        self.me, self.sib = (x, y, c), (x, y, 1 - c)
        self.chips = [(1 - x, y), (x, 1 - y), (1 - x, 1 - y)]

    def _rows(self, a, p):
        rows = self.x_refs[a].shape[0] // self.pieces[a]
        return pl.ds(p * rows, rows)

    def _cp(self, a, p, k, block, to, own=False):
        dst = self.o_refs[a].at[4 * block[0] + 2 * block[1] + block[2], self._rows(a, p)]
        sem = self.base[a] + 7 * p + k
        return pltpu.make_async_remote_copy(
            src_ref=self.x_refs[a].at[self._rows(a, p)] if own else dst, dst_ref=dst,
            send_sem=self.send.at[sem], recv_sem=self.recv.at[sem], device_id=to, device_id_type=MESH)

    def _mine(self, a):
        me = self.me
        return pltpu.make_async_copy(self.x_refs[a], self.o_refs[a].at[4 * me[0] + 2 * me[1] + me[2]], self.loc.at[a])

    def _first(self, a, p):
        cps = [self._cp(a, p, 0, self.me, self.sib, own=True)]
        return cps + [self._cp(a, p, 1 + j, self.me, (*chip, self.c), own=True) for j, chip in enumerate(self.chips)]

    def _passed(self, a, p, j):
        return self._cp(a, p, 4 + j, (*self.chips[j], self.c), self.sib)

    def start(self):
        for a in range(self.n):
            self._mine(a).start()
        for p in range(max(self.pieces)):
            for a in range(self.n):
                if p < self.pieces[a]:
                    for cp in self._first(a, p):
                        cp.start()

    def forward(self, p):
        for j, chip in enumerate(self.chips):
            for a in range(self.n):
                if p < self.pieces[a]:
                    self._cp(a, p, 1 + j, (*chip, self.c), self.me).wait_recv()
                    self._passed(a, p, j).start()

    def finish(self):
        for a in range(self.n):
            for p in range(self.pieces[a]):
                self._cp(a, p, 0, self.sib, self.me).wait_recv()
                for j, chip in enumerate(self.chips):
                    self._cp(a, p, 4 + j, (*chip, 1 - self.c), self.me).wait_recv()
        for a in range(self.n):
            for p in range(self.pieces[a]):
                for cp in self._first(a, p):
                    cp.wait_send()
                for j in range(3):
                    self._passed(a, p, j).wait_send()
            self._mine(a).wait()

    def begin_hosted(self, step, steps):
        @pl.when(step == 0)
        def _():
            self.start()

        n_p = max(self.pieces)
        for p in range(n_p):
            @pl.when(step == min(((p + 1) * 7 * steps) // (8 * n_p), steps - 1))
            def _():
                self.forward(p)

    def end_hosted(self, step, steps):
        @pl.when(step == steps - 1)
        def _():
            self.finish()


class _Exchange:
    def __init__(self, x_refs, o_refs, send, recv, loc):
        self.x_refs, self.o_refs, self.send, self.recv, self.loc = x_refs, o_refs, send, recv, loc
        self.n = len(x_refs)
        x, y, c = _pos()
        self.me_i = 4 * x + 2 * y + c
        self.peers = []
        for k in range(1, N_DEV):
            px = 1 - x if (k >> 2) & 1 else x
            py = 1 - y if (k >> 1) & 1 else y
            pc = 1 - c if k & 1 else c
            self.peers.append(((px, py, pc), 4 * px + 2 * py + pc))

    def _mine(self, a):
        return pltpu.make_async_copy(self.x_refs[a].at[self.me_i], self.o_refs[a].at[self.me_i], self.loc.at[a])

    def _cp(self, a, k, landing):
        peer, peer_i = self.peers[k]
        return pltpu.make_async_remote_copy(
            src_ref=self.x_refs[a].at[peer_i], dst_ref=self.o_refs[a].at[landing],
            send_sem=self.send.at[a * 7 + k], recv_sem=self.recv.at[a * 7 + k],
            device_id=peer, device_id_type=MESH)

    def start(self):
        for a in range(self.n):
            self._mine(a).start()
            for k in range(N_DEV - 1):
                self._cp(a, k, self.me_i).start()

    def finish(self):
        for a in range(self.n):
            for k in range(N_DEV - 1):
                self._cp(a, k, self.peers[k][1]).wait_recv()
        for a in range(self.n):
            for k in range(N_DEV - 1):
                self._cp(a, k, self.me_i).wait_send()
            self._mine(a).wait()


def _gather_scratch(xs):
    n_sem = 7 * sum(_pieces(v.shape, v.dtype) for v in xs)
    return [pltpu.SemaphoreType.DMA((n_sem,)), pltpu.SemaphoreType.DMA((n_sem,)), pltpu.SemaphoreType.DMA((len(xs),))]


ANY_SPEC = pl.BlockSpec(memory_space=pl.ANY)


def _all_gather(xs, name, after=()):
    n, na = len(xs), len(after)

    def body(*refs):
        g = _Gather(refs[:n], refs[n + na:2 * n + na], *refs[2 * n + na:])
        g.start()
        for p in range(max(g.pieces)):
            g.forward(p)
        g.finish()

    return pl.pallas_call(
        body, name=name,
        out_shape=[jax.ShapeDtypeStruct((N_DEV,) + v.shape, v.dtype) for v in xs],
        in_specs=[ANY_SPEC] * (n + na), out_specs=[ANY_SPEC] * n, scratch_shapes=_gather_scratch(xs),
    )(*xs, *after)


HBM_SPEC = pl.BlockSpec(memory_space=pltpu.HBM)
SEM_SPEC = pl.BlockSpec(memory_space=pltpu.SEMAPHORE)
VMEM_SPEC = pl.BlockSpec(memory_space=pltpu.VMEM)
SPLIT_EFFECT = pltpu.SideEffectType.DATAFLOW_SIDE_EFFECTING


def _in_hbm(v):
    return pltpu.with_memory_space_constraint(v, pltpu.HBM)


def _exchange_start(blocks, name):
    n = len(blocks)

    def body(*refs):
        x_refs, land_refs = refs[:n], refs[n:2 * n]
        send, recv = refs[2 * n:2 * n + 2]
        token, loc = refs[-2:]
        ex = _Exchange(x_refs, land_refs, send, recv, loc)
        ex.start()
        for a in range(n):
            ex._mine(a).wait()
        token[...] = jnp.zeros_like(token)

    lands = [lax.empty(v.shape, v.dtype) for v in blocks]
    hbm = tuple(pltpu.HBM(v.shape, v.dtype) for v in list(blocks) + lands)
    n_sem = (N_DEV - 1) * n
    out = pl.pallas_call(
        body, name=name,
        out_shape=(pltpu.SemaphoreType.DMA((n_sem,)), pltpu.SemaphoreType.DMA((n_sem,))) + hbm
        + (jax.ShapeDtypeStruct((8, LANES), F32),),
        in_specs=(HBM_SPEC,) * (2 * n), out_specs=(SEM_SPEC, SEM_SPEC) + (HBM_SPEC,) * (2 * n) + (VMEM_SPEC,),
        input_output_aliases={i: i + 2 for i in range(2 * n)},
        scratch_shapes=[pltpu.SemaphoreType.DMA((n,))],
        compiler_params=pltpu.CompilerParams(has_side_effects=SPLIT_EFFECT),
    )(*[_in_hbm(v) for v in list(blocks) + lands])
    return out[0], out[1], list(out[2:2 + n]), list(out[2 + n:2 + 2 * n]), out[-1]


def _exchange_wait(ex, after, name):
    send, recv, thru, lands, _ = ex
    n = len(thru)

    def body(*refs):
        x_refs, land_refs = refs[:n], refs[n:2 * n]
        send_ref, recv_ref = refs[2 * n:2 * n + 2]
        e = _Exchange(x_refs, land_refs, send_ref, recv_ref, None)
        for a in range(n):
            for k in range(N_DEV - 1):
                e._cp(a, k, e.me_i).wait_send()
                e._cp(a, k, e.peers[k][1]).wait_recv()

    hbm = tuple(pltpu.HBM(v.shape, v.dtype) for v in list(thru) + list(lands))
    out = pl.pallas_call(
        body, name=name, out_shape=hbm,
        in_specs=(HBM_SPEC,) * (2 * n) + (SEM_SPEC, SEM_SPEC, ANY_SPEC), out_specs=(HBM_SPEC,) * (2 * n),
        input_output_aliases={i: i for i in range(2 * n)},
        compiler_params=pltpu.CompilerParams(has_side_effects=SPLIT_EFFECT),
    )(*thru, *lands, send, recv, after)
    return list(out[n:])


def _ada_fwd(c_all, w_ada, b_slice):
    def body(c_ref, w_ref, b_ref, o_ref):
        cv = c_ref[...]
        act = cv * _sigmoid(cv)
        o_ref[...] = _mm(act, w_ref[...]) + b_ref[...]

    nb, nc = c_all.shape[0], w_ada.shape[1]
    return pl.pallas_call(body, name="ada_fwd", out_shape=jax.ShapeDtypeStruct((nb, nc), F32),
                          compiler_params=_cparams())(c_all, w_ada, b_slice)


def _adam_math(w, g, m, v):
    m = B1 * m + (1.0 - B1) * g
    v = B2 * v + (1.0 - B2) * jnp.square(g)
    m_hat = m / (1.0 - B1 ** STEP)
    v_hat = v / (1.0 - B2 ** STEP)
    delta = -LR * (m_hat / (jnp.sqrt(v_hat) + AEPS) + WD * w)
    return delta, m, v


def _ada_bwd_adam(c_all, dmod_slice, w, m, v):
    rows, cols = w.shape
    br = 256

    def body(c_ref, d_ref, w_ref, m_ref, v_ref, g_out, dl_out, m_out, v_out):
        cv = c_ref[...]
        act = cv * _sigmoid(cv)
        g = _mm_tn(act, d_ref[...])
        g_out[...] = g
        dl, mn, vn = _adam_math(w_ref[...], g, m_ref[...], v_ref[...])
        dl_out[...] = dl
        m_out[...] = mn
        v_out[...] = vn

    nb = c_all.shape[0]
    wspec = pl.BlockSpec((br, cols), lambda i: (i, 0))
    return pl.pallas_call(
        body, name="ada_bwd_adam", grid=(rows // br,),
        in_specs=[pl.BlockSpec((nb, br), lambda i: (0, i)), pl.BlockSpec((nb, cols), lambda i: (0, 0)),
                  wspec, wspec, wspec],
        out_specs=[wspec] * 4, out_shape=[jax.ShapeDtypeStruct((rows, cols), F32)] * 4,
        compiler_params=_cparams(),
    )(c_all, dmod_slice, w, m, v)


def _mix_in(x2, mod, g_mix, win_g, seq, shards):
    T = x2.shape[0]
    tm = min(512, seq)
    tps = seq // tm
    in_cols = win_g.shape[2]
    ns = len(shards)
    steps = T // tm

    def body(*refs):
        x_ref, mod_ref, g_ref, wb_ref = refs[:4]
        sh_refs = refs[4:4 + ns]
        u_ref, pm_ref, wc_ref = refs[4 + ns:7 + ns]
        ga_refs = refs[7 + ns:7 + 2 * ns]
        w_ref, send, recv, loc = refs[7 + 2 * ns:]
        step = pl.program_id(0)
        gather = _Gather(sh_refs, ga_refs, send, recv, loc)
        gather.begin_hosted(step, steps)

        @pl.when(step == 0)
        def _():
            w_ref[:, OFF_DT:] = jnp.zeros((D, PROJ_W - OFF_DT), BF16)
            for j in range(N_DEV):
                w_ref[:, in_cols * j:in_cols * (j + 1)] = wb_ref[j]
            wc_ref[...] = w_ref[...]

        x = x_ref[...]
        r = lax.rsqrt(jnp.mean(x * x, axis=-1, keepdims=True) + EPS)
        md = mod_ref[0]
        u = (x * r * g_ref[...]) * (1.0 + md[1:2]) + md[0:1]
        ub = u.astype(BF16)
        u_ref[...] = ub
        pm_ref[...] = jnp.dot(ub, w_ref[...], preferred_element_type=F32)
        gather.end_hosted(step, steps)

    whole = pl.BlockSpec(memory_space=pltpu.VMEM)
    return pl.pallas_call(
        body, name="mix_in", grid=(T // tm,),
        in_specs=[pl.BlockSpec((tm, D), lambda i: (i, 0)), pl.BlockSpec((1, 8, D), lambda i: (i // tps, 0, 0)),
                  pl.BlockSpec((1, D), lambda i: (0, 0)), whole] + [ANY_SPEC] * ns,
        out_specs=[pl.BlockSpec((tm, D), lambda i: (i, 0)), pl.BlockSpec((tm, PROJ_W), lambda i: (i, 0)),
                   pl.BlockSpec((D, PROJ_W), lambda i: (0, 0))] + [ANY_SPEC] * ns,
        out_shape=[jax.ShapeDtypeStruct((T, D), BF16), jax.ShapeDtypeStruct((T, PROJ_W), F32),
                   jax.ShapeDtypeStruct((D, PROJ_W), BF16)]
        + [jax.ShapeDtypeStruct((N_DEV,) + v.shape, v.dtype) for v in shards],
        scratch_shapes=[pltpu.VMEM((D, PROJ_W), BF16)] + _gather_scratch(shards),
        compiler_params=_cparams(),
    )(x2, mod, g_mix, win_g, *shards)


def _chunk_forward(up, z, ux, dtin, halo_p, halo_x, hprev, cw, cb, hp, gssd, wpool, pscale, t0, y_scr, cv=None):
    L = CHUNK
    out = {}
    row = lax.broadcasted_iota(jnp.int32, (L, 1), 0)
    t = (t0 + row + 1).astype(F32)
    e = jnp.concatenate([halo_p, up], axis=0)
    s2 = e + pltpu.roll(e, 1, 0)
    s4 = s2 + pltpu.roll(s2, 2, 0)
    s8 = s4 + pltpu.roll(s4, 4, 0)
    s16 = s8 + pltpu.roll(s8, 8, 0)
    sums = (s2, s4, s8, s16)
    p, inv, yp = [], [], []
    for gi, w in enumerate(WINDOWS):
        sl = slice(gi * LANES, (gi + 1) * LANES)
        ic = 1.0 / jnp.minimum(t, float(w))
        pg = sums[gi][POOL_HALO:, sl] * ic - up[:, sl]
        p.append(pg)
        inv.append(ic)
        yp.append(_mm(pg, wpool[gi]))
    out["p"], out["inv"], out["yp"] = p, inv, yp
    out["y_pool"] = jnp.concatenate(yp, axis=1) * pscale
    if cv is None:
        ex = jnp.concatenate([halo_x, ux], axis=0)
        taps = [pltpu.roll(ex, 3, 0)[CONV_HALO:], pltpu.roll(ex, 2, 0)[CONV_HALO:], pltpu.roll(ex, 1, 0)[CONV_HALO:], ux]
        cv = cb + taps[0] * cw[0:1] + taps[1] * cw[1:2] + taps[2] * cw[2:3] + taps[3] * cw[3:4]
    sg = _sigmoid(cv)
    xbc = cv * sg
    out["cv"], out["sg"] = cv, sg
    X = xbc[:, :D]
    Bm = xbc[:, D:D + N_GROUPS * N_STATE]
    Cm = xbc[:, D + N_GROUPS * N_STATE:]
    pre = dtin + hp[0:1]
    dt = jnp.maximum(pre, 0.0) + jnp.log(1.0 + jnp.exp(-jnp.abs(pre)))
    a_row = -jnp.exp(hp[1:2])
    da = dt * a_row
    ri = lax.broadcasted_iota(jnp.int32, (L, L), 0)
    ci = lax.broadcasted_iota(jnp.int32, (L, L), 1)
    causal = ri >= ci
    cum = _dot01(causal.astype(F32), da, 3, split_lhs=False)
    cum_t = cum.T
    cum_last = cum[L - 1:L]
    eo = jnp.exp(cum)
    dec = jnp.exp(cum_last - cum)
    cd = jnp.exp(cum_last)
    exm = _expand_mat()
    rows8 = jnp.concatenate([cd, hp[2:3], jnp.zeros((6, LANES), F32)], axis=0)
    rep = _dot01(jnp.concatenate([dt, eo, dec, rows8], axis=0), exm, 2)
    dt_rep, eo_rep, dec_rep = rep[0:L], rep[L:2 * L], rep[2 * L:3 * L]
    cd_rep, dskip_rep = rep[3 * L:3 * L + 1], rep[3 * L + 1:3 * L + 2]
    xdt = X * dt_rep
    out.update(X=X, Bm=Bm, Cm=Cm, pre=pre, dt=dt, a_row=a_row, cum=cum, cum_t=cum_t, eo=eo, dec=dec, cd=cd,
               dt_rep=dt_rep, eo_rep=eo_rep, dec_rep=dec_rep, cd_rep=cd_rep, dskip_rep=dskip_rep, xdt=xdt,
               causal=causal, anti=(ri <= ci).astype(F32), exm=exm)
    G, lms, yoff, hnew, xdec = [], [], [], [], []
    for g in range(N_GROUPS):
        gs = slice(g * GROUP_W, (g + 1) * GROUP_W)
        Bg = Bm[:, g * N_STATE:(g + 1) * N_STATE]
        Cg = Cm[:, g * N_STATE:(g + 1) * N_STATE]
        Gg = _mm_nt(Cg, Bg)
        G.append(Gg)
        for hh in range(N_HEADS // N_GROUPS):
            h = g * (N_HEADS // N_GROUPS) + hh
            seg = cum[:, h:h + 1] - cum_t[h:h + 1, :]
            lm = jnp.where(causal, jnp.exp(jnp.minimum(seg, 0.0)), 0.0)
            lms.append(lm)
            hs = slice(h * HEAD_DIM, (h + 1) * HEAD_DIM)
            y_scr[:, hs] = _mm(Gg * lm, xdt[:, hs])
        xd = xdt[:, gs] * dec_rep[:, gs]
        xdec.append(xd)
        sgm = _mm_tn(Bg, xd)
        yoff.append(_mm(Cg, hprev[g]) * eo_rep[:, gs])
        hnew.append(hprev[g] * cd_rep[:, gs] + sgm)
    out.update(G=G, lms=lms, yoff=yoff, hnew=hnew, xdec=xdec)
    y = y_scr[...] + jnp.concatenate(yoff, axis=1) + dskip_rep * X
    sz = _sigmoid(z)
    silz = z * sz
    yz = y * silz
    rg, yn = [], []
    for g in range(N_GROUPS):
        gs = slice(g * GROUP_W, (g + 1) * GROUP_W)
        r = lax.rsqrt(jnp.mean(yz[:, gs] * yz[:, gs], axis=-1, keepdims=True) + EPS)
        rg.append(r)
        yn.append(yz[:, gs] * r)
    yn = jnp.concatenate(yn, axis=1)
    out.update(y=y, sz=sz, silz=silz, rg=rg, yn=yn)
    out["y_ssd"] = yn * gssd
    return out


def _mixer_fwd(pm, cw, cb, hp, gssd, wpool, pscale, nb, seq, shards):
    nc = seq // CHUNK
    ns = len(shards)
    steps = nb * nc

    def body(*refs):
        pm_ref, cw_ref, cb_ref, hp_ref, gs_ref, wp_ref, ps_ref = refs[:7]
        sh_refs = refs[7:7 + ns]
        ym_ref, hs_ref, cv_ref = refs[7 + ns:10 + ns]
        ga_refs = refs[10 + ns:10 + 2 * ns]
        halo_p, halo_x, state, y_scr, send, recv, loc = refs[10 + 2 * ns:]
        c = pl.program_id(1)
        step = pl.program_id(0) * nc + c
        gather = _Gather(sh_refs, ga_refs, send, recv, loc)
        gather.begin_hosted(step, steps)

        @pl.when(c == 0)
        def _():
            halo_p[...] = jnp.zeros_like(halo_p)
            halo_x[...] = jnp.zeros_like(halo_x)
            state[...] = jnp.zeros_like(state)

        up = pm_ref[:, 0:POOL_W]
        z = pm_ref[:, OFF_Z:OFF_XBC]
        ux = pm_ref[:, OFF_XBC:OFF_DT]
        hprev = [state[0], state[1]]
        hs_ref[0, 0, 0] = hprev[0]
        hs_ref[0, 0, 1] = hprev[1]
        o = _chunk_forward(up, z, ux, pm_ref[:, OFF_DT:], halo_p[...], halo_x[...], hprev, cw_ref[...], cb_ref[...],
                           hp_ref[...], gs_ref[...], wp_ref[...], ps_ref[...], c * CHUNK, y_scr)
        ym_ref[:, 0:POOL_W] = o["y_pool"].astype(BF16)
        ym_ref[:, POOL_W:] = o["y_ssd"].astype(BF16)
        cv_ref[...] = o["cv"]
        state[0] = o["hnew"][0]
        state[1] = o["hnew"][1]
        halo_p[...] = up[CHUNK - POOL_HALO:]
        halo_x[...] = ux[CHUNK - CONV_HALO:]
        gather.end_hosted(step, steps)

    def full(shape):
        return pl.BlockSpec(shape, lambda b, c: (0,) * len(shape))

    T = nb * seq
    return pl.pallas_call(
        body, name="mixer_fwd", grid=(nb, nc),
        in_specs=[pl.BlockSpec((CHUNK, PROJ_W), lambda b, c: (b * nc + c, 0)),
                  full((4, CONV_CH)), full((1, CONV_CH)), full((8, LANES)), full((1, D)),
                  full((4, LANES, LANES)), full((1, POOL_W))] + [ANY_SPEC] * ns,
        out_specs=[pl.BlockSpec((CHUNK, MIX_W), lambda b, c: (b * nc + c, 0)),
                   pl.BlockSpec((1, 1, N_GROUPS, N_STATE, GROUP_W), lambda b, c: (b, c, 0, 0, 0)),
                   pl.BlockSpec((CHUNK, CONV_CH), lambda b, c: (b * nc + c, 0))] + [ANY_SPEC] * ns,
        out_shape=[jax.ShapeDtypeStruct((T, MIX_W), BF16),
                   jax.ShapeDtypeStruct((nb, nc, N_GROUPS, N_STATE, GROUP_W), F32),
                   jax.ShapeDtypeStruct((T, CONV_CH), F32)]
        + [jax.ShapeDtypeStruct((N_DEV,) + v.shape, v.dtype) for v in shards],
        scratch_shapes=[pltpu.VMEM((POOL_HALO, POOL_W), F32), pltpu.VMEM((CONV_HALO, CONV_CH), F32),
                        pltpu.VMEM((N_GROUPS, N_STATE, GROUP_W), F32), pltpu.VMEM((CHUNK, D), F32)] + _gather_scratch(shards),
        compiler_params=_cparams(),
    )(pm, cw, cb, hp, gssd, wpool, pscale, *shards)


def _mixer_bwd(pm, cvs, dym, hstates, cw, cb, hp, gssd, wpool, pscale, nb, seq, after=()):
    nc = seq // CHUNK
    hpg = N_HEADS // N_GROUPS
    na = len(after)

    def body(*refs):
        (pm_ref, hpool_ref, cv_ref, dy_ref, hs_ref, cw_ref, cb_ref, hp_ref, gs_ref, wp_ref, ps_ref) = refs[:11]
        dpm_ref, dconv_ref, dhp_ref, dvec_ref, dwp_ref = refs[11 + na:16 + na]
        nxt_q, nxt_cv, rstate, y_scr, dx_scr = refs[16 + na:]
        b = pl.program_id(0)
        ci = pl.program_id(1)
        c = nc - 1 - ci

        @pl.when((b == 0) & (ci == 0))
        def _():
            for r in (dconv_ref, dhp_ref, dvec_ref, dwp_ref):
                r[...] = jnp.zeros_like(r)

        @pl.when(ci == 0)
        def _():
            nxt_q[...] = jnp.zeros_like(nxt_q)
            nxt_cv[...] = jnp.zeros_like(nxt_cv)
            rstate[...] = jnp.zeros_like(rstate)

        first = (c > 0).astype(F32)
        up = pm_ref[:, 0:POOL_W]
        z = pm_ref[:, OFF_Z:OFF_XBC]
        ux = pm_ref[:, OFF_XBC:OFF_DT]
        halo_p = hpool_ref[...] * first
        hprev = [hs_ref[0, 0, 0], hs_ref[0, 0, 1]]
        cw, cb, hp, gssd, wpool, pscale = cw_ref[...], cb_ref[...], hp_ref[...], gs_ref[...], wp_ref[...], ps_ref[...]
        o = _chunk_forward(up, z, ux, pm_ref[:, OFF_DT:], halo_p, None, hprev, cw, cb, hp, gssd, wpool, pscale,
                           c * CHUNK, y_scr, cv=cv_ref[...])
        L = CHUNK
        dy_pool = dy_ref[:, 0:POOL_W].astype(F32)
        dy_ssd = dy_ref[:, POOL_W:].astype(F32)

        dvec_ref[1:2, 0:POOL_W] += jnp.sum(dy_pool * jnp.concatenate(o["yp"], axis=1), axis=0, keepdims=True)
        dyp = dy_pool * pscale
        qs = []
        dps = []
        for gi in range(len(WINDOWS)):
            sl = slice(gi * LANES, (gi + 1) * LANES)
            dwp_ref[gi] += _mm_tn(o["p"][gi], dyp[:, sl])
            dpg = _mm_nt(dyp[:, sl], wpool[gi])
            dps.append(dpg)
            qs.append(dpg * o["inv"][gi])
        q = jnp.concatenate(qs, axis=1)
        e = jnp.concatenate([q, nxt_q[...]], axis=0)
        n = L + POOL_HALO
        s2 = e + pltpu.roll(e, n - 1, 0)
        s4 = s2 + pltpu.roll(s2, n - 2, 0)
        s8 = s4 + pltpu.roll(s4, n - 4, 0)
        s16 = s8 + pltpu.roll(s8, n - 8, 0)
        sums = (s2, s4, s8, s16)
        for gi in range(len(WINDOWS)):
            sl = slice(gi * LANES, (gi + 1) * LANES)
            dpm_ref[:, sl] = (sums[gi][:L, sl] - dps[gi]).astype(BF16)
        nxt_q[...] = q[:POOL_HALO]

        yn, y, silz, sz = o["yn"], o["y"], o["silz"], o["sz"]
        dvec_ref[0:1] += jnp.sum(dy_ssd * yn, axis=0, keepdims=True)
        dyn = dy_ssd * gssd
        dyz = []
        for g in range(N_GROUPS):
            gs = slice(g * GROUP_W, (g + 1) * GROUP_W)
            mean = jnp.mean(dyn[:, gs] * yn[:, gs], axis=-1, keepdims=True)
            dyz.append(o["rg"][g] * (dyn[:, gs] - yn[:, gs] * mean))
        dyz = jnp.concatenate(dyz, axis=1)
        dyv = dyz * silz
        dpm_ref[:, OFF_Z:OFF_XBC] = (dyz * y * (sz * (1.0 + z * (1.0 - sz)))).astype(BF16)

        X, Bm, Cm, xdt = o["X"], o["Bm"], o["Cm"], o["xdt"]
        exm = o["exm"]
        rdm = _reduce_mat()
        lane = lax.broadcasted_iota(jnp.int32, (1, LANES), 1)
        sub = lax.broadcasted_iota(jnp.int32, (LANES, 1), 0)
        dX = o["dskip_rep"] * dyv
        yoff_full = jnp.concatenate(o["yoff"], axis=1)
        rs = jnp.zeros((L, LANES), F32)
        cs_t = jnp.zeros((LANES, L), F32)
        dBs, dCs = [], []
        rh_sums = []
        ddec = []
        for g in range(N_GROUPS):
            gs = slice(g * GROUP_W, (g + 1) * GROUP_W)
            Bg = Bm[:, g * N_STATE:(g + 1) * N_STATE]
            Cg = Cm[:, g * N_STATE:(g + 1) * N_STATE]
            Gg = o["G"][g]
            R = rstate[g]
            dwm = dyv[:, gs] * o["eo_rep"][:, gs]
            dC = _mm_nt(dwm, hprev[g])
            dH = _mm_tn(Cg, dwm)
            dG = jnp.zeros((L, L), F32)
            for hh in range(hpg):
                h = g * hpg + hh
                hs = slice(h * HEAD_DIM, (h + 1) * HEAD_DIM)
                lm = o["lms"][h]
                m_h = Gg * lm
                dM = _mm_nt(dyv[:, hs], xdt[:, hs])
                dx_scr[:, hs] = _mm_tn(m_h, dyv[:, hs])
                qm = dM * m_h
                rs = rs + jnp.sum(qm, axis=1, keepdims=True) * (lane == h).astype(F32)
                cs_t = cs_t + (sub == h).astype(F32) * jnp.sum(qm, axis=0, keepdims=True)
                dG = dG + dM * lm
            dC = dC + _mm(dG, Bg)
            dB = _mm_tn(dG, Cg)
            zx = _mm(Bg, R)
            dxdt_state = zx * o["dec_rep"][:, gs]
            ddec.append(zx * xdt[:, gs])
            dB = dB + _mm_nt(o["xdec"][g], R)
            rh_sums.append(jnp.sum(R * hprev[g], axis=0, keepdims=True))
            rstate[g] = dH + o["cd_rep"][:, gs] * R
            dx_scr[:, gs] = dx_scr[:, gs] + dxdt_state
            dBs.append(dB)
            dCs.append(dC)
        dxdt = dx_scr[...]
        tail = jnp.concatenate([jnp.sum(dyv * X, axis=0, keepdims=True), jnp.concatenate(rh_sums, axis=1),
                                jnp.zeros((6, D), F32)], axis=0)
        red = _dot01(jnp.concatenate([dyv * yoff_full, jnp.concatenate(ddec, axis=1), dxdt * X, tail], axis=0), rdm, 2)
        d_dskip, dcd_row = red[3 * L:3 * L + 1], red[3 * L + 1:3 * L + 2]
        ddec_h = red[L:2 * L] * o["dec"]
        dcum_last = jnp.sum(ddec_h, axis=0, keepdims=True) + dcd_row * o["cd"]
        dcum = red[0:L] + rs - cs_t.T - ddec_h + (sub == L - 1).astype(F32) * dcum_last
        dda = _dot01(o["anti"], dcum, 3, split_lhs=False)
        ddt_v = dda * o["a_row"] + red[2 * L:3 * L]
        dX = dX + dxdt * o["dt_rep"]
        head_mask = (lane < N_HEADS).astype(F32)
        d_alog = jnp.sum(dda * o["dt"], axis=0, keepdims=True) * o["a_row"] * head_mask
        dpre = ddt_v * _sigmoid(o["pre"]) * head_mask
        dpm_ref[:, OFF_DT:] = dpre.astype(BF16)
        d_dtb = jnp.sum(dpre, axis=0, keepdims=True)
        dhp_ref[...] += jnp.concatenate([d_dtb, d_alog, d_dskip * head_mask, jnp.zeros((5, LANES), F32)], axis=0)

        dxbc = jnp.concatenate([dX] + dBs + dCs, axis=1)
        sg, cv = o["sg"], o["cv"]
        dcv = dxbc * (sg * (1.0 + cv * (1.0 - sg)))
        e2 = jnp.concatenate([dcv, nxt_cv[...]], axis=0)
        n2 = L + CONV_HALO
        ahead = [dcv, pltpu.roll(e2, n2 - 1, 0)[:L], pltpu.roll(e2, n2 - 2, 0)[:L], pltpu.roll(e2, n2 - 3, 0)[:L]]
        dconv_ref[0:5] += jnp.concatenate(
            [jnp.sum(ux * ahead[3 - k], axis=0, keepdims=True) for k in range(4)]
            + [jnp.sum(dcv, axis=0, keepdims=True)], axis=0)
        dux = ahead[0] * cw[3:4] + ahead[1] * cw[2:3] + ahead[2] * cw[1:2] + ahead[3] * cw[0:1]
        dpm_ref[:, OFF_XBC:OFF_DT] = dux.astype(BF16)
        nxt_cv[...] = dcv[:CONV_HALO]

    def full(shape):
        return pl.BlockSpec(shape, lambda b, c: (0,) * len(shape))

    def rowblk(b, c):
        return b * nc + (nc - 1 - c)

    hp_blocks = CHUNK // POOL_HALO
    T = nb * seq
    return pl.pallas_call(
        body, name="mixer_bwd", grid=(nb, nc),
        in_specs=[pl.BlockSpec((CHUNK, PROJ_W), lambda b, c: (rowblk(b, c), 0)),
                  pl.BlockSpec((POOL_HALO, POOL_W), lambda b, c: (jnp.maximum(rowblk(b, c) * hp_blocks - 1, 0), 0)),
                  pl.BlockSpec((CHUNK, CONV_CH), lambda b, c: (rowblk(b, c), 0)),
                  pl.BlockSpec((CHUNK, MIX_W), lambda b, c: (rowblk(b, c), 0)),
                  pl.BlockSpec((1, 1, N_GROUPS, N_STATE, GROUP_W), lambda b, c: (b, nc - 1 - c, 0, 0, 0)),
                  full((4, CONV_CH)), full((1, CONV_CH)), full((8, LANES)), full((1, D)),
                  full((4, LANES, LANES)), full((1, POOL_W))] + [ANY_SPEC] * na,
        out_specs=[pl.BlockSpec((CHUNK, PROJ_W), lambda b, c: (rowblk(b, c), 0)),
                   full((8, CONV_CH)), full((8, LANES)), full((8, D)), full((4, LANES, LANES))],
        out_shape=[jax.ShapeDtypeStruct((T, PROJ_W), BF16),
                   jax.ShapeDtypeStruct((8, CONV_CH), F32), jax.ShapeDtypeStruct((8, LANES), F32),
                   jax.ShapeDtypeStruct((8, D), F32), jax.ShapeDtypeStruct((4, LANES, LANES), F32)],
        scratch_shapes=[pltpu.VMEM((POOL_HALO, POOL_W), F32), pltpu.VMEM((CONV_HALO, CONV_CH), F32),
                        pltpu.VMEM((N_GROUPS, N_STATE, GROUP_W), F32), pltpu.VMEM((CHUNK, D), F32),
                        pltpu.VMEM((CHUNK, D), F32)],
        compiler_params=_cparams(),
    )(pm, pm, cvs, dym, hstates, cw, cb, hp, gssd, wpool, pscale, *after)


def _mlp_fused(x2, ymix, target, mod, g_mlp, g_final, w_out, w_up, w_down, seq):
    T = x2.shape[0]
    tm = min(256, seq)
    tps = seq // tm
    nblk = D_FF // FF_BLK

    def body(x_ref, ym_ref, tg_ref, mod_ref, gm_ref, gf_ref, wo_ref, wu_ref, wd_ref,
             da_ref, dym_ref, dh1_ref, u2_ref, f_ref, dup_ref, ddn_ref, dmod_ref, acc_ref, relu_scr):
        i = pl.program_id(0)

        @pl.when(i == 0)
        def _():
            acc_ref[...] = jnp.zeros_like(acc_ref)

        @pl.when(i % tps == 0)
        def _():
            dmod_ref[...] = jnp.zeros_like(dmod_ref)

        md = mod_ref[0]
        gate_m, shift_f, scale_f, gate_f = md[2:3], md[3:4], md[4:5], md[5:6]
        g_mlp, g_fin = gm_ref[...], gf_ref[...]
        a = jnp.dot(ym_ref[...], wo_ref[...], preferred_element_type=F32)
        h1 = x_ref[...] + gate_m * a
        r2 = lax.rsqrt(jnp.mean(h1 * h1, axis=-1, keepdims=True) + EPS)
        n2 = h1 * r2
        u2 = (n2 * g_mlp) * (1.0 + scale_f) + shift_f
        u2b = u2.astype(BF16)
        u2_ref[...] = u2b
        dn = jnp.zeros((tm, D), F32)
        for j in range(nblk):
            js = slice(j * FF_BLK, (j + 1) * FF_BLK)
            upj = jnp.maximum(jnp.dot(u2b, wu_ref[j], preferred_element_type=F32), 0.0)
            relu_scr[:, js] = upj
            fj = (upj * upj).astype(BF16)
            f_ref[:, js] = fj
            dn = dn + jnp.dot(fj, wd_ref[j], preferred_element_type=F32)
        h2 = h1 + gate_f * dn
        r3 = lax.rsqrt(jnp.mean(h2 * h2, axis=-1, keepdims=True) + EPS)
        n3 = h2 * r3
        err = n3 * g_fin - tg_ref[...]
        loss = 0.5 * jnp.sum(jnp.mean(err * err, axis=-1, keepdims=True), axis=0, keepdims=True)
        dout = err * (1.0 / D)
        d_gfin = jnp.sum(dout * n3, axis=0, keepdims=True)
        dn3 = dout * g_fin
        dh2 = r3 * (dn3 - n3 * jnp.mean(dn3 * n3, axis=-1, keepdims=True))
        d_gate_f = jnp.sum(dh2 * dn, axis=0, keepdims=True)
        ddn = (gate_f * dh2).astype(BF16)
        ddn_ref[...] = ddn
        du2 = jnp.zeros((tm, D), F32)
        for j in range(nblk):
            js = slice(j * FF_BLK, (j + 1) * FF_BLK)
            dfj = lax.dot_general(ddn, wd_ref[j], (((1,), (1,)), ((), ())), preferred_element_type=F32)
            dupj = (dfj * (2.0 * relu_scr[:, js])).astype(BF16)
            dup_ref[:, js] = dupj
            du2 = du2 + lax.dot_general(dupj, wu_ref[j], (((1,), (1,)), ((), ())), preferred_element_type=F32)
        d_scale_f = jnp.sum(du2 * (n2 * g_mlp), axis=0, keepdims=True)
        d_shift_f = jnp.sum(du2, axis=0, keepdims=True)
        d_gmlp = jnp.sum(du2 * (1.0 + scale_f) * n2, axis=0, keepdims=True)
        dn2 = du2 * (g_mlp * (1.0 + scale_f))
        dh1 = dh2 + r2 * (dn2 - n2 * jnp.mean(dn2 * n2, axis=-1, keepdims=True))
        dh1_ref[...] = dh1
        d_gate_m = jnp.sum(dh1 * a, axis=0, keepdims=True)
        da = (gate_m * dh1).astype(BF16)
        da_ref[...] = da
        dym_ref[...] = lax.dot_general(da, wo_ref[...], (((1,), (1,)), ((), ())),
                                       preferred_element_type=F32).astype(BF16)
        dmod_ref[0] += jnp.concatenate([jnp.zeros((2, D), F32), d_gate_m, d_shift_f, d_scale_f, d_gate_f,
                                        jnp.zeros((2, D), F32)], axis=0)
        acc_ref[...] += jnp.concatenate([d_gmlp, d_gfin, loss * jnp.ones((1, D), F32), jnp.zeros((5, D), F32)], axis=0)

    whole = pl.BlockSpec(memory_space=pltpu.VMEM)

    def tok(w):
        return pl.BlockSpec((tm, w), lambda i: (i, 0))

    def vec():
        return pl.BlockSpec((1, D), lambda i: (0, 0))

    nb = T // seq
    return pl.pallas_call(
        body, name="mlp_fused", grid=(T // tm,),
        in_specs=[tok(D), tok(MIX_W), tok(D), pl.BlockSpec((1, 8, D), lambda i: (i // tps, 0, 0)), vec(), vec(),
                  whole, whole, whole],
        out_specs=[tok(D), tok(MIX_W), tok(D), tok(D), tok(D_FF), tok(D_FF), tok(D),
                   pl.BlockSpec((1, 8, D), lambda i: (i // tps, 0, 0)), pl.BlockSpec((8, D), lambda i: (0, 0))],
        out_shape=[jax.ShapeDtypeStruct((T, D), BF16), jax.ShapeDtypeStruct((T, MIX_W), BF16),
                   jax.ShapeDtypeStruct((T, D), F32), jax.ShapeDtypeStruct((T, D), BF16),
                   jax.ShapeDtypeStruct((T, D_FF), BF16), jax.ShapeDtypeStruct((T, D_FF), BF16),
                   jax.ShapeDtypeStruct((T, D), BF16), jax.ShapeDtypeStruct((nb, 8, D), F32),
                   jax.ShapeDtypeStruct((8, D), F32)],
        scratch_shapes=[pltpu.VMEM((tm, D_FF), F32)],
        compiler_params=_cparams(),
    )(x2, ymix, target, mod, g_mlp, g_final, w_out, w_up, w_down)


def _in_bwd(x2, dh1, dpb, mod, g_mix, w_cat, dmod_a, acc_a, seq):
    T = x2.shape[0]
    tm = min(1024, seq)
    tps = seq // tm
    steps = T // tm

    def body(x_ref, dh_ref, dpb_ref, mod_ref, g_ref, w_ref, dma_ref, acca_ref, dx_ref, dmod_ref, acc_ref):
        i = pl.program_id(0)

        @pl.when(i == 0)
        def _():
            acc_ref[...] = acca_ref[...]

        @pl.when(i % tps == 0)
        def _():
            dmod_ref[...] = dma_ref[...]

        du = lax.dot_general(dpb_ref[...], w_ref[...], (((1,), (1,)), ((), ())), preferred_element_type=F32)
        x = x_ref[...]
        md = mod_ref[0]
        g = g_ref[...]
        r = lax.rsqrt(jnp.mean(x * x, axis=-1, keepdims=True) + EPS)
        n1 = x * r
        d_scale = jnp.sum(du * (n1 * g), axis=0, keepdims=True)
        d_shift = jnp.sum(du, axis=0, keepdims=True)
        d_g = jnp.sum(du * (1.0 + md[1:2]) * n1, axis=0, keepdims=True)
        dn1 = du * (g * (1.0 + md[1:2]))
        dx_ref[...] = dh_ref[...] + r * (dn1 - n1 * jnp.mean(dn1 * n1, axis=-1, keepdims=True))
        dmod_ref[0] += jnp.concatenate([d_shift, d_scale, jnp.zeros((6, D), F32)], axis=0)
        acc_ref[...] += jnp.concatenate([jnp.zeros((3, D), F32), d_g, jnp.zeros((4, D), F32)], axis=0)

    whole = pl.BlockSpec(memory_space=pltpu.VMEM)
    nb = T // seq
    return pl.pallas_call(
        body, name="in_bwd", grid=(steps,),
        in_specs=[pl.BlockSpec((tm, D), lambda i: (i, 0)), pl.BlockSpec((tm, D), lambda i: (i, 0)),
                  pl.BlockSpec((tm, PROJ_W), lambda i: (i, 0)),
                  pl.BlockSpec((1, 8, D), lambda i: (i // tps, 0, 0)), pl.BlockSpec((1, D), lambda i: (0, 0)),
                  whole, pl.BlockSpec((1, 8, D), lambda i: (i // tps, 0, 0)), pl.BlockSpec((8, D), lambda i: (0, 0))],
        out_specs=[pl.BlockSpec((tm, D), lambda i: (i, 0)),
                   pl.BlockSpec((1, 8, D), lambda i: (i // tps, 0, 0)), pl.BlockSpec((8, D), lambda i: (0, 0))],
        out_shape=[jax.ShapeDtypeStruct((T, D), F32),
                   jax.ShapeDtypeStruct((nb, 8, D), F32), jax.ShapeDtypeStruct((8, D), F32)],
        compiler_params=_cparams(),
    )(x2, dh1, dpb, mod, g_mix, w_cat, dmod_a, acc_a)


def _dw_in(u_b, dpb, in_cols, shards):
    T = u_b.shape[0]
    bk = min(512, T)
    nk = T // bk
    ns = len(shards)
    starts = [(in_cols * j // LANES) * LANES for j in range(N_DEV)]
    assert all(s + DW_IN_WIN <= PROJ_W and in_cols * (j + 1) <= s + DW_IN_WIN for j, s in enumerate(starts))

    def body(*refs):
        u_ref, d_ref = refs[:2]
        sh_refs = refs[2:2 + ns]
        o_ref = refs[2 + ns]
        ga_refs = refs[3 + ns:3 + 2 * ns]
        acc, send, recv, loc = refs[3 + 2 * ns:]
        k = pl.program_id(0)
        gather = _Gather(sh_refs, ga_refs, send, recv, loc)
        gather.begin_hosted(k, nk)

        @pl.when(k == 0)
        def _():
            acc[...] = jnp.zeros_like(acc)

        ut = u_ref[...].T
        for j in range(N_DEV):
            acc[j] += jnp.dot(ut, d_ref[:, starts[j]:starts[j] + DW_IN_WIN], preferred_element_type=F32)

        @pl.when(k == nk - 1)
        def _():
            for j in range(N_DEV):
                off = in_cols * j - starts[j]
                o_ref[j] = acc[j][:, off:off + in_cols].astype(BF16)

        gather.end_hosted(k, nk)

    return pl.pallas_call(
        body, name="dw_in", grid=(nk,),
        in_specs=[pl.BlockSpec((bk, D), lambda k: (k, 0)), pl.BlockSpec((bk, PROJ_W), lambda k: (k, 0))]
        + [ANY_SPEC] * ns,
        out_specs=[pl.BlockSpec((N_DEV, D, in_cols), lambda k: (0, 0, 0))] + [ANY_SPEC] * ns,
        out_shape=[jax.ShapeDtypeStruct((N_DEV, D, in_cols), BF16)]
        + [jax.ShapeDtypeStruct((N_DEV,) + v.shape, v.dtype) for v in shards],
        scratch_shapes=[pltpu.VMEM((N_DEV, D, DW_IN_WIN), F32)] + _gather_scratch(shards),
        compiler_params=_cparams(),
    )(u_b, dpb, *shards)


def _dw_blocks(a, b, name, by_rows, per_step=1, after=()):
    T, M = a.shape
    N = b.shape[1]
    bk = min(2048, T)
    nk = T // bk
    whole = pl.BlockSpec(memory_space=pltpu.VMEM)
    if by_rows:
        rows = M // N_DEV
        am = rows * per_step
        nblk = N_DEV // per_step
        a_spec, b_spec = pl.BlockSpec((bk, am), lambda i, k: (k, i)), whole
        out_blk, acc_shape = (per_step, rows, N), (am, N)
    else:
        cols = N // N_DEV
        nblk = N_DEV
        a_spec, b_spec = whole, pl.BlockSpec((bk, cols), lambda i, k: (k, i))
        out_blk, acc_shape = (1, M, cols), (M, cols)

    def body(a_ref, b_ref, *rest):
        o_ref, acc = rest[len(after):]
        k = pl.program_id(1)

        @pl.when(k == 0)
        def _():
            acc[...] = jnp.zeros_like(acc)

        tok = pl.ds(pl.multiple_of(k * bk, bk), bk)
        a_blk = a_ref[...] if by_rows else a_ref[tok, :]
        b_blk = b_ref[tok, :] if by_rows else b_ref[...]
        acc[...] += lax.dot_general(a_blk, b_blk, (((0,), (0,)), ((), ())), preferred_element_type=F32)

        @pl.when(k == nk - 1)
        def _():
            o_ref[...] = acc[...].reshape(out_blk).astype(BF16)

    return pl.pallas_call(
        body, name=name, grid=(nblk, nk), in_specs=[a_spec, b_spec] + [ANY_SPEC] * len(after),
        out_specs=pl.BlockSpec(out_blk, lambda i, k: (i, 0, 0)),
        out_shape=jax.ShapeDtypeStruct((N_DEV,) + out_blk[1:], BF16),
        scratch_shapes=[pltpu.VMEM(acc_shape, F32)],
        compiler_params=_cparams(),
    )(a, b, *after)


def _adam_parts(parts, w, m, v, name):
    rows, cols = w.shape
    br = rows
    for cand in range(rows, 15, -16):
        if rows % cand == 0 and cand * cols * 4 <= ADAM_BLOCK_BYTES:
            br = cand
            break

    def body(p_ref, w_ref, m_ref, v_ref, g_out, dl_out, m_out, v_out):
        g = p_ref[0].astype(F32)
        for k in range(1, N_DEV):
            g = g + p_ref[k].astype(F32)
        g_out[...] = g
        dl, mn, vn = _adam_math(w_ref[...], g, m_ref[...], v_ref[...])
        dl_out[...] = dl
        m_out[...] = mn
        v_out[...] = vn

    wspec = pl.BlockSpec((br, cols), lambda i: (i, 0))
    return pl.pallas_call(
        body, name=name, grid=(rows // br,),
        in_specs=[pl.BlockSpec((N_DEV, br, cols), lambda i: (0, i, 0)), wspec, wspec, wspec],
        out_specs=[wspec] * 4, out_shape=[jax.ShapeDtypeStruct((rows, cols), F32)] * 4,
        compiler_params=_cparams(),
    )(parts, w, m, v)


def _adam_plain(g, w, m, v, name):
    def body(g_ref, w_ref, m_ref, v_ref, dl_out, m_out, v_out):
        dl, mn, vn = _adam_math(w_ref[...], g_ref[...], m_ref[...], v_ref[...])
        dl_out[...] = dl
        m_out[...] = mn
        v_out[...] = vn

    return pl.pallas_call(body, name=name, out_shape=[jax.ShapeDtypeStruct(w.shape, F32)] * 3,
                          compiler_params=_cparams())(g, w, m, v)


SMALL_PARAMS = ("b_ada", "g_mix", "conv_b", "dt_bias", "a_log", "d_skip", "g_ssd", "pool_scale", "g_mlp", "g_final")


def _small_adam(gathered, params):
    n_par = len(SMALL_PARAMS)
    nb = gathered[0].shape[1]

    def body(*refs):
        dmod_ref, acc_ref, conv_ref, vec_ref, hd_ref = refs[:5]
        par_refs = refs[5:5 + 3 * n_par]
        out_refs = refs[5 + 3 * n_par:5 + 7 * n_par]
        cw_out, acc_out = refs[5 + 7 * n_par:]

        def total(ref):
            t = ref[0]
            for k in range(1, N_DEV):
                t = t + ref[k]
            return t

        dm = total(dmod_ref)
        dmb = dm[0]
        for b in range(1, nb):
            dmb = dmb + dm[b]
        ac, cv, vc, hd = total(acc_ref), total(conv_ref), total(vec_ref), total(hd_ref)
        cw_out[...] = cv[0:4]
        acc_out[...] = ac
        grads = {
            "b_ada": jnp.concatenate([dmb[r:r + 1] for r in range(6)], axis=1), "g_mix": ac[3:4], "conv_b": cv[4:5],
            "dt_bias": hd[0:1, 0:N_HEADS], "a_log": hd[1:2, 0:N_HEADS], "d_skip": hd[2:3, 0:N_HEADS],
            "g_ssd": vc[0:1], "pool_scale": vc[1:2, 0:POOL_W], "g_mlp": ac[0:1], "g_final": ac[1:2],
        }
        for i, name in enumerate(SMALL_PARAMS):
            w_ref, m_ref, v_ref = par_refs[3 * i:3 * i + 3]
            g = grads[name]
            dl, mn, vn = _adam_math(w_ref[...], g, m_ref[...], v_ref[...])
            g_o, d_o, m_o, v_o = out_refs[4 * i:4 * i + 4]
            g_o[...] = g
            d_o[...] = dl
            m_o[...] = mn
            v_o[...] = vn

    flat = [a for name in SMALL_PARAMS for a in params[name]]
    out_shape = [jax.ShapeDtypeStruct(params[name][0].shape, F32) for name in SMALL_PARAMS for _ in range(4)]
    out_shape += [jax.ShapeDtypeStruct((4, CONV_CH), F32), jax.ShapeDtypeStruct((8, D), F32)]
    return pl.pallas_call(body, name="small_adam", out_shape=out_shape, compiler_params=_cparams())(*gathered, *flat)


def kernel(x, c, w_ada, b_ada, g_mix, w_in, conv_w, conv_b, dt_bias, a_log, d_skip, g_ssd, w_pool, pool_scale, w_out, g_mlp, w_up, w_down, g_final, loss_target, m_w_ada, m_b_ada, m_g_mix, m_w_in, m_conv_w, m_conv_b, m_dt_bias, m_a_log, m_d_skip, m_g_ssd, m_w_pool, m_pool_scale, m_w_out, m_g_mlp, m_w_up, m_w_down, m_g_final, v_w_ada, v_b_ada, v_g_mix, v_w_in, v_conv_w, v_conv_b, v_dt_bias, v_a_log, v_d_skip, v_g_ssd, v_w_pool, v_pool_scale, v_w_out, v_g_mlp, v_w_up, v_w_down, v_g_final):
    nb, seq, _ = x.shape
    T = nb * seq
    me = 4 * lax.axis_index("x") + 2 * lax.axis_index("y") + lax.axis_index("c")
    in_cols = w_in.shape[2]
    ada_cols = w_ada.shape[2]
    cw_cols = conv_w.shape[2]

    c_g, cw_g, win_g = _all_gather([c, conv_w[0], w_in[0].astype(BF16)], "ag_first")
    c_all = c_g.reshape(N_DEV * nb, D)
    cw_full = cw_g.transpose(1, 0, 2).reshape(4, CONV_CH)

    b_slice = lax.dynamic_slice(b_ada, (0, me * ada_cols), (1, ada_cols))
    mod_cols = _ada_fwd(c_all, w_ada[0], b_slice)
    (mod_g,) = _all_gather([mod_cols], "ag_mod")
    mod_all = mod_g.transpose(1, 0, 2).reshape(N_DEV * nb, 6, D)
    mod_mine = lax.dynamic_slice(mod_all, (me * nb, 0, 0), (nb, 6, D))
    mod = jnp.pad(mod_mine, ((0, 0), (0, 2), (0, 0)))

    x2 = x.reshape(T, D)
    tg2 = loss_target.reshape(T, D)
    heads = jnp.pad(jnp.concatenate([dt_bias, a_log, d_skip], axis=0), ((0, 5), (0, LANES - N_HEADS)))
    wpool_b = w_pool[0]
    u_b, pm, w_cat, wup_g = _mix_in(x2, mod, g_mix, win_g, seq, [w_up[0].astype(BF16)])
    ymix, hstates, cvs, wout_g, wdn_g = _mixer_fwd(
        pm, cw_full, conv_b, heads, g_ssd, wpool_b, pool_scale, nb, seq,
        [w_out[0].astype(BF16), w_down[0].astype(BF16)])
    da_b, dym, dh1, u2_b, f_b, dup_b, ddn_b, dmod_a, acc_a = _mlp_fused(
        x2, ymix, tg2, mod, g_mlp, g_final.reshape(1, D), wout_g.reshape(MIX_W, D), wup_g, wdn_g, seq)

    gout_p = _dw_blocks(ymix, da_b, "dw_out", True, per_step=4)
    gup_p = _dw_blocks(u2_b, dup_b, "dw_up", False)
    gdn_p = _dw_blocks(f_b, ddn_b, "dw_down", True)
    ex_mlp = _exchange_start([gout_p, gup_p, gdn_p], "gmlp_start")
    dpb, d_conv, d_heads, d_vec, d_wpool = _mixer_bwd(
        pm, cvs, dym, hstates, cw_full, conv_b, heads, g_ssd, wpool_b, pool_scale, nb, seq, after=[ex_mlp[4]])
    gin_p, conv_g, vec_g, heads_g, wpool_parts = _dw_in(
        u_b, dpb, in_cols, [d_conv, d_vec, d_heads, d_wpool.reshape(4 * LANES, LANES)])
    ex_in = _exchange_start([gin_p], "gin_start")
    grad_x2, dmod, acc = _in_bwd(x2, dh1, dpb, mod, g_mix + ex_in[4][0:1, 0:1], w_cat, dmod_a, acc_a, seq)

    gout_r, gup_r, gdn_r = _exchange_wait(ex_mlp, dmod, "gmlp_wait")
    g_out, d_out, nm_out, nv_out = _adam_parts(gout_r, w_out[0], m_w_out[0], v_w_out[0], "adam_w_out")
    g_up, d_up, nm_up, nv_up = _adam_parts(gup_r, w_up[0], m_w_up[0], v_w_up[0], "adam_w_up")
    g_dn, d_dn, nm_dn, nv_dn = _adam_parts(gdn_r, w_down[0], m_w_down[0], v_w_down[0], "adam_w_down")

    dmod_g, acc_g = _all_gather([dmod, acc], "ag_small_bwd", after=[nm_out, nm_up, nm_dn])
    pool2 = (4 * LANES, LANES)
    wpool_outs = _adam_parts(wpool_parts, w_pool.reshape(pool2), m_w_pool.reshape(pool2), v_w_pool.reshape(pool2),
                             "adam_w_pool")
    small_params = {
        "b_ada": (b_ada, m_b_ada, v_b_ada), "g_mix": (g_mix, m_g_mix, v_g_mix), "conv_b": (conv_b, m_conv_b, v_conv_b),
        "dt_bias": (dt_bias, m_dt_bias, v_dt_bias), "a_log": (a_log, m_a_log, v_a_log),
        "d_skip": (d_skip, m_d_skip, v_d_skip), "g_ssd": (g_ssd, m_g_ssd, v_g_ssd),
        "pool_scale": (pool_scale, m_pool_scale, v_pool_scale), "g_mlp": (g_mlp, m_g_mlp, v_g_mlp),
        "g_final": tuple(a.reshape(1, D) for a in (g_final, m_g_final, v_g_final)),
    }
    small_res = _small_adam([dmod_g, acc_g, conv_g, vec_g, heads_g], small_params)
    g_cw_full, acc_sum = small_res[-2:]
    loss = acc_sum[2, 0]

    g_cw = lax.dynamic_slice(g_cw_full, (0, me * cw_cols), (4, cw_cols))
    d_cwp, nm_cwp, nv_cwp = _adam_plain(g_cw, conv_w[0], m_conv_w[0], v_conv_w[0], "adam_conv_w")

    dmod_all = dmod_g[:, :, 0:6].reshape(N_DEV * nb, 6 * D)
    dmod_slice = lax.dynamic_slice(dmod_all, (0, me * ada_cols), (N_DEV * nb, ada_cols))
    g_ada, d_ada, nm_ada, nv_ada = _ada_bwd_adam(c_all, dmod_slice, w_ada[0], m_w_ada[0], v_w_ada[0])

    ex_after = nm_ada[0:8, 0:LANES] + acc_sum[:, 0:LANES]
    (gin_r,) = _exchange_wait(ex_in, ex_after, "gin_wait")
    g_in, d_in, nm_in, nv_in = _adam_parts(gin_r, w_in[0], m_w_in[0], v_w_in[0], "adam_w_in")

    def small_outs(kind, wpool):
        res = {name: small_res[4 * i + kind] for i, name in enumerate(SMALL_PARAMS)}
        res["g_final"] = res["g_final"].reshape(D)
        res["w_pool"] = wpool.reshape(1, 4, LANES, LANES)
        return res

    def big_outs(ada, win, cwp, wout, wup, wdn):
        return {"w_ada": ada[None], "w_in": win.reshape(1, D, in_cols), "conv_w": cwp[None], "w_out": wout[None],
                "w_up": wup[None], "w_down": wdn[None]}

    order = ["w_ada", "b_ada", "g_mix", "w_in", "conv_w", "conv_b", "dt_bias", "a_log", "d_skip", "g_ssd", "w_pool",
             "pool_scale", "w_out", "g_mlp", "w_up", "w_down", "g_final"]
    groups = [
        {**small_outs(0, wpool_outs[0]), **big_outs(g_ada, g_in, g_cw, g_out, g_up, g_dn)},
        {**small_outs(1, wpool_outs[1]), **big_outs(d_ada, d_in, d_cwp, d_out, d_up, d_dn)},
        {**small_outs(2, wpool_outs[2]), **big_outs(nm_ada, nm_in, nm_cwp, nm_out, nm_up, nm_dn)},
        {**small_outs(3, wpool_outs[3]), **big_outs(nv_ada, nv_in, nv_cwp, nv_out, nv_up, nv_dn)},
    ]
    outs = [loss, grad_x2.reshape(nb, seq, D)]
    for grp in groups:
        outs += [grp[n] for n in order]
    return tuple(outs)
```

```python
import functools

import jax
import jax.numpy as jnp
from jax import lax
from jax.experimental import pallas as pl
from jax.experimental.pallas import tpu as pltpu

F32, BF16 = jnp.float32, jnp.bfloat16
MESH = pl.DeviceIdType.MESH
N_DEV = 8
D = 1024
LANES = 128
CHUNK = 128
POOL_W = 512
WINDOWS = (2, 4, 8, 16)
N_HEADS = 16
HEAD_DIM = 64
N_GROUPS = 2
GROUP_W = 512
N_STATE = 128
CONV_CH = 1536
OFF_Z, OFF_XBC, OFF_DT, IN_W = 512, 1536, 3072, 3088
PROJ_W = OFF_DT + LANES
MIX_W = 1536
D_FF = 4096
FF_BLK = 512
EPS = 1e-5
LR, B1, B2, AEPS, WD, STEP = 0.001, 0.9, 0.999, 1e-08, 0.01, 10
POOL_HALO = 16
CONV_HALO = 8
VMEM_LIMIT = 56 << 20
ADAM_BLOCK_BYTES = 1 << 20
DW_IN_WIN = 512


def _cparams(**kw):
    return pltpu.CompilerParams(vmem_limit_bytes=VMEM_LIMIT, **kw)


def _mm(a, b):
    return jnp.dot(a.astype(BF16), b.astype(BF16), preferred_element_type=F32)


def _mm_nt(a, b):
    return lax.dot_general(a.astype(BF16), b.astype(BF16), (((1,), (1,)), ((), ())), preferred_element_type=F32)


def _mm_tn(a, b):
    return lax.dot_general(a.astype(BF16), b.astype(BF16), (((0,), (0,)), ((), ())), preferred_element_type=F32)


def _split_bf16(v, terms):
    parts, rest = [], v
    for t in range(terms):
        p = rest.astype(BF16)
        parts.append(p)
        if t + 1 < terms:
            rest = rest - p.astype(F32)
    return parts


def _dot01(a, b, terms, split_lhs=True):
    if split_lhs:
        bb = b.astype(BF16)
        prods = [jnp.dot(p, bb, preferred_element_type=F32) for p in _split_bf16(a, terms)]
    else:
        ab = a.astype(BF16)
        prods = [jnp.dot(ab, p, preferred_element_type=F32) for p in _split_bf16(b, terms)]
    out = prods[0]
    for q in prods[1:]:
        out = out + q
    return out


def _sigmoid(v):
    return 1.0 / (1.0 + jnp.exp(-v))


def _expand_mat():
    r = lax.broadcasted_iota(jnp.int32, (LANES, D), 0)
    c = lax.broadcasted_iota(jnp.int32, (LANES, D), 1)
    return (r == c // HEAD_DIM).astype(F32)


def _reduce_mat():
    r = lax.broadcasted_iota(jnp.int32, (D, LANES), 0)
    c = lax.broadcasted_iota(jnp.int32, (D, LANES), 1)
    return (c == r // HEAD_DIM).astype(F32)


def _pos():
    return lax.axis_index("x"), lax.axis_index("y"), lax.axis_index("c")


GATHER_PIECES = 4
GATHER_PIECE_BYTES = 96 << 10


def _pieces(shape, dtype):
    rows = shape[0]
    size = jnp.dtype(dtype).itemsize
    for d in shape:
        size *= d
    whole_tiles = rows % (GATHER_PIECES * 16) == 0
    return GATHER_PIECES if whole_tiles and size // GATHER_PIECES >= GATHER_PIECE_BYTES else 1


class _Gather:
    def __init__(self, x_refs, o_refs, send, recv, loc):
        self.x_refs, self.o_refs, self.send, self.recv, self.loc = x_refs, o_refs, send, recv, loc
        self.n = len(x_refs)
        self.pieces = [_pieces(r.shape, r.dtype) for r in x_refs]
        self.base = [7 * sum(self.pieces[:a]) for a in range(self.n)]
        x, y, c = _pos()
        self.c = c
        self.me, self.sib = (x, y, c), (x, y, 1 - c)
        self.chips = [(1 - x, y), (x, 1 - y), (1 - x, 1 - y)]

    def _rows(self, a, p):
        rows = self.x_refs[a].shape[0] // self.pieces[a]
        return pl.ds(p * rows, rows)

    def _cp(self, a, p, k, block, to, own=False):
        dst = self.o_refs[a].at[4 * block[0] + 2 * block[1] + block[2], self._rows(a, p)]
        sem = self.base[a] + 7 * p + k
        return pltpu.make_async_remote_copy(
            src_ref=self.x_refs[a].at[self._rows(a, p)] if own else dst, dst_ref=dst,
            send_sem=self.send.at[sem], recv_sem=self.recv.at[sem], device_id=to, device_id_type=MESH)

    def _mine(self, a):
        me = self.me
        return pltpu.make_async_copy(self.x_refs[a], self.o_refs[a].at[4 * me[0] + 2 * me[1] + me[2]], self.loc.at[a])

    def _first(self, a, p):
        cps = [self._cp(a, p, 0, self.me, self.sib, own=True)]
        return cps + [self._cp(a, p, 1 + j, self.me, (*chip, self.c), own=True) for j, chip in enumerate(self.chips)]

    def _passed(self, a, p, j):
        return self._cp(a, p, 4 + j, (*self.chips[j], self.c), self.sib)

    def start(self):
        for a in range(self.n):
            self._mine(a).start()
        for p in range(max(self.pieces)):
            for a in range(self.n):
                if p < self.pieces[a]:
                    for cp in self._first(a, p):
                        cp.start()

    def forward(self, p):
        for j, chip in enumerate(self.chips):
            for a in range(self.n):
                if p < self.pieces[a]:
                    self._cp(a, p, 1 + j, (*chip, self.c), self.me).wait_recv()
                    self._passed(a, p, j).start()

    def finish(self):
        for a in range(self.n):
            for p in range(self.pieces[a]):
                self._cp(a, p, 0, self.sib, self.me).wait_recv()
                for j, chip in enumerate(self.chips):
                    self._cp(a, p, 4 + j, (*chip, 1 - self.c), self.me).wait_recv()
        for a in range(self.n):
            for p in range(self.pieces[a]):
                for cp in self._first(a, p):
                    cp.wait_send()
                for j in range(3):
                    self._passed(a, p, j).wait_send()
            self._mine(a).wait()

    def begin_hosted(self, step, steps):
        @pl.when(step == 0)
        def _():
            self.start()

        n_p = max(self.pieces)
        for p in range(n_p):
            @pl.when(step == min(((p + 1) * 7 * steps) // (8 * n_p), steps - 1))
            def _():
                self.forward(p)

    def end_hosted(self, step, steps):
        @pl.when(step == steps - 1)
        def _():
            self.finish()


class _Exchange:
    def __init__(self, x_refs, o_refs, send, recv, loc):
        self.x_refs, self.o_refs, self.send, self.recv, self.loc = x_refs, o_refs, send, recv, loc
        self.n = len(x_refs)
        x, y, c = _pos()
        self.me_i = 4 * x + 2 * y + c
        self.peers = []
        for k in range(1, N_DEV):
            px = 1 - x if (k >> 2) & 1 else x
            py = 1 - y if (k >> 1) & 1 else y
            pc = 1 - c if k & 1 else c
            self.peers.append(((px, py, pc), 4 * px + 2 * py + pc))

    def _mine(self, a):
        return pltpu.make_async_copy(self.x_refs[a].at[self.me_i], self.o_refs[a].at[self.me_i], self.loc.at[a])

    def _cp(self, a, k, landing):
        peer, peer_i = self.peers[k]
        return pltpu.make_async_remote_copy(
            src_ref=self.x_refs[a].at[peer_i], dst_ref=self.o_refs[a].at[landing],
            send_sem=self.send.at[a * 7 + k], recv_sem=self.recv.at[a * 7 + k],
            device_id=peer, device_id_type=MESH)

    def start(self):
        for a in range(self.n):
            self._mine(a).start()
            for k in range(N_DEV - 1):
                self._cp(a, k, self.me_i).start()

    def finish(self):
        for a in range(self.n):
            for k in range(N_DEV - 1):
                self._cp(a, k, self.peers[k][1]).wait_recv()
        for a in range(self.n):
            for k in range(N_DEV - 1):
                self._cp(a, k, self.me_i).wait_send()
            self._mine(a).wait()


def _gather_scratch(xs):
    n_sem = 7 * sum(_pieces(v.shape, v.dtype) for v in xs)
    return [pltpu.SemaphoreType.DMA((n_sem,)), pltpu.SemaphoreType.DMA((n_sem,)), pltpu.SemaphoreType.DMA((len(xs),))]


ANY_SPEC = pl.BlockSpec(memory_space=pl.ANY)


def _all_gather(xs, name, after=()):
    n, na = len(xs), len(after)

    def body(*refs):
        g = _Gather(refs[:n], refs[n + na:2 * n + na], *refs[2 * n + na:])
        g.start()
        for p in range(max(g.pieces)):
            g.forward(p)
        g.finish()

    return pl.pallas_call(
        body, name=name,
        out_shape=[jax.ShapeDtypeStruct((N_DEV,) + v.shape, v.dtype) for v in xs],
        in_specs=[ANY_SPEC] * (n + na), out_specs=[ANY_SPEC] * n, scratch_shapes=_gather_scratch(xs),
    )(*xs, *after)


HBM_SPEC = pl.BlockSpec(memory_space=pltpu.HBM)
SEM_SPEC = pl.BlockSpec(memory_space=pltpu.SEMAPHORE)
VMEM_SPEC = pl.BlockSpec(memory_space=pltpu.VMEM)
SPLIT_EFFECT = pltpu.SideEffectType.DATAFLOW_SIDE_EFFECTING


def _in_hbm(v):
    return pltpu.with_memory_space_constraint(v, pltpu.HBM)


def _exchange_start(blocks, name):
    n = len(blocks)

    def body(*refs):
        x_refs, land_refs = refs[:n], refs[n:2 * n]
        send, recv = refs[2 * n:2 * n + 2]
        token = refs[-1]
        ex = _Exchange(x_refs, land_refs, send, recv, None)
        for a in range(n):
            for k in range(N_DEV - 1):
                ex._cp(a, k, ex.me_i).start()
        token[...] = jnp.zeros_like(token)

    lands = [lax.empty(v.shape, v.dtype) for v in blocks]
    hbm = tuple(pltpu.HBM(v.shape, v.dtype) for v in list(blocks) + lands)
    n_sem = (N_DEV - 1) * n
    out = pl.pallas_call(
        body, name=name,
        out_shape=(pltpu.SemaphoreType.DMA((n_sem,)), pltpu.SemaphoreType.DMA((n_sem,))) + hbm
        + (jax.ShapeDtypeStruct((8, LANES), F32),),
        in_specs=(HBM_SPEC,) * (2 * n), out_specs=(SEM_SPEC, SEM_SPEC) + (HBM_SPEC,) * (2 * n) + (VMEM_SPEC,),
        input_output_aliases={i: i + 2 for i in range(2 * n)},
        compiler_params=pltpu.CompilerParams(has_side_effects=SPLIT_EFFECT),
    )(*[_in_hbm(v) for v in list(blocks) + lands])
    return out[0], out[1], list(out[2:2 + n]), list(out[2 + n:2 + 2 * n]), out[-1]


def _exchange_wait(ex, after, name, me):
    send, recv, thru, lands, _ = ex
    n = len(thru)

    def body(*refs):
        x_refs, land_refs = refs[:n], refs[n:2 * n]
        send_ref, recv_ref = refs[2 * n:2 * n + 2]
        e = _Exchange(x_refs, land_refs, send_ref, recv_ref, None)
        for a in range(n):
            for k in range(N_DEV - 1):
                e._cp(a, k, e.me_i).wait_send()
                e._cp(a, k, e.peers[k][1]).wait_recv()

    hbm = tuple(pltpu.HBM(v.shape, v.dtype) for v in list(thru) + list(lands))
    out = pl.pallas_call(
        body, name=name, out_shape=hbm,
        in_specs=(HBM_SPEC,) * (2 * n) + (SEM_SPEC, SEM_SPEC, ANY_SPEC), out_specs=(HBM_SPEC,) * (2 * n),
        input_output_aliases={i: i for i in range(2 * n)},
        compiler_params=pltpu.CompilerParams(has_side_effects=SPLIT_EFFECT),
    )(*thru, *lands, send, recv, after)
    done = []
    for own, land in zip(out[:n], out[n:]):
        mine = lax.dynamic_slice(own, (me,) + (0,) * (own.ndim - 1), (1,) + own.shape[1:])
        done.append(lax.dynamic_update_slice(land, mine, (me,) + (0,) * (own.ndim - 1)))
    return done


def _ada_fwd(c_all, w_ada, b_slice):
    def body(c_ref, w_ref, b_ref, o_ref):
        cv = c_ref[...]
        act = cv * _sigmoid(cv)
        o_ref[...] = _mm(act, w_ref[...]) + b_ref[...]

    nb, nc = c_all.shape[0], w_ada.shape[1]
    return pl.pallas_call(body, name="ada_fwd", out_shape=jax.ShapeDtypeStruct((nb, nc), F32),
                          compiler_params=_cparams())(c_all, w_ada, b_slice)


def _adam_math(w, g, m, v):
    m = B1 * m + (1.0 - B1) * g
    v = B2 * v + (1.0 - B2) * jnp.square(g)
    m_hat = m / (1.0 - B1 ** STEP)
    v_hat = v / (1.0 - B2 ** STEP)
    delta = -LR * (m_hat / (jnp.sqrt(v_hat) + AEPS) + WD * w)
    return delta, m, v


def _ada_bwd_adam(c_all, dmod_slice, w, m, v):
    rows, cols = w.shape
    br = 256

    def body(c_ref, d_ref, w_ref, m_ref, v_ref, g_out, dl_out, m_out, v_out):
        cv = c_ref[...]
        act = cv * _sigmoid(cv)
        g = _mm_tn(act, d_ref[...])
        g_out[...] = g
        dl, mn, vn = _adam_math(w_ref[...], g, m_ref[...], v_ref[...])
        dl_out[...] = dl
        m_out[...] = mn
        v_out[...] = vn

    nb = c_all.shape[0]
    wspec = pl.BlockSpec((br, cols), lambda i: (i, 0))
    return pl.pallas_call(
        body, name="ada_bwd_adam", grid=(rows // br,),
        in_specs=[pl.BlockSpec((nb, br), lambda i: (0, i)), pl.BlockSpec((nb, cols), lambda i: (0, 0)),
                  wspec, wspec, wspec],
        out_specs=[wspec] * 4, out_shape=[jax.ShapeDtypeStruct((rows, cols), F32)] * 4,
        compiler_params=_cparams(),
    )(c_all, dmod_slice, w, m, v)


def _mix_in(x2, mod, g_mix, win_g, seq, shards):
    T = x2.shape[0]
    tm = min(512, seq)
    tps = seq // tm
    in_cols = win_g.shape[2]
    ns = len(shards)
    steps = T // tm

    def body(*refs):
        x_ref, mod_ref, g_ref, wb_ref = refs[:4]
        sh_refs = refs[4:4 + ns]
        u_ref, pm_ref, wc_ref = refs[4 + ns:7 + ns]
        ga_refs = refs[7 + ns:7 + 2 * ns]
        w_ref, send, recv, loc = refs[7 + 2 * ns:]
        step = pl.program_id(0)
        gather = _Gather(sh_refs, ga_refs, send, recv, loc)
        gather.begin_hosted(step, steps)

        @pl.when(step == 0)
        def _():
            w_ref[:, OFF_DT:] = jnp.zeros((D, PROJ_W - OFF_DT), BF16)
            for j in range(N_DEV):
                w_ref[:, in_cols * j:in_cols * (j + 1)] = wb_ref[j]
            wc_ref[...] = w_ref[...]

        x = x_ref[...]
        r = lax.rsqrt(jnp.mean(x * x, axis=-1, keepdims=True) + EPS)
        md = mod_ref[0]
        u = (x * r * g_ref[...]) * (1.0 + md[1:2]) + md[0:1]
        ub = u.astype(BF16)
        u_ref[...] = ub
        pm_ref[...] = jnp.dot(ub, w_ref[...], preferred_element_type=F32)
        gather.end_hosted(step, steps)

    whole = pl.BlockSpec(memory_space=pltpu.VMEM)
    return pl.pallas_call(
        body, name="mix_in", grid=(T // tm,),
        in_specs=[pl.BlockSpec((tm, D), lambda i: (i, 0)), pl.BlockSpec((1, 8, D), lambda i: (i // tps, 0, 0)),
                  pl.BlockSpec((1, D), lambda i: (0, 0)), whole] + [ANY_SPEC] * ns,
        out_specs=[pl.BlockSpec((tm, D), lambda i: (i, 0)), pl.BlockSpec((tm, PROJ_W), lambda i: (i, 0)),
                   pl.BlockSpec((D, PROJ_W), lambda i: (0, 0))] + [ANY_SPEC] * ns,
        out_shape=[jax.ShapeDtypeStruct((T, D), BF16), jax.ShapeDtypeStruct((T, PROJ_W), F32),
                   jax.ShapeDtypeStruct((D, PROJ_W), BF16)]
        + [jax.ShapeDtypeStruct((N_DEV,) + v.shape, v.dtype) for v in shards],
        scratch_shapes=[pltpu.VMEM((D, PROJ_W), BF16)] + _gather_scratch(shards),
        compiler_params=_cparams(),
    )(x2, mod, g_mix, win_g, *shards)


def _chunk_forward(up, z, ux, dtin, halo_p, halo_x, hprev, cw, cb, hp, gssd, wpool, pscale, t0, y_scr, cv=None):
    L = CHUNK
    out = {}
    row = lax.broadcasted_iota(jnp.int32, (L, 1), 0)
    t = (t0 + row + 1).astype(F32)
    e = jnp.concatenate([halo_p, up], axis=0)
    s2 = e + pltpu.roll(e, 1, 0)
    s4 = s2 + pltpu.roll(s2, 2, 0)
    s8 = s4 + pltpu.roll(s4, 4, 0)
    s16 = s8 + pltpu.roll(s8, 8, 0)
    sums = (s2, s4, s8, s16)
    p, inv, yp = [], [], []
    for gi, w in enumerate(WINDOWS):
        sl = slice(gi * LANES, (gi + 1) * LANES)
        ic = 1.0 / jnp.minimum(t, float(w))
        pg = sums[gi][POOL_HALO:, sl] * ic - up[:, sl]
        p.append(pg)
        inv.append(ic)
        yp.append(_mm(pg, wpool[gi]))
    out["p"], out["inv"], out["yp"] = p, inv, yp
    out["y_pool"] = jnp.concatenate(yp, axis=1) * pscale
    if cv is None:
        ex = jnp.concatenate([halo_x, ux], axis=0)
        taps = [pltpu.roll(ex, 3, 0)[CONV_HALO:], pltpu.roll(ex, 2, 0)[CONV_HALO:], pltpu.roll(ex, 1, 0)[CONV_HALO:], ux]
        cv = cb + taps[0] * cw[0:1] + taps[1] * cw[1:2] + taps[2] * cw[2:3] + taps[3] * cw[3:4]
    sg = _sigmoid(cv)
    xbc = cv * sg
    out["cv"], out["sg"] = cv, sg
    X = xbc[:, :D]
    Bm = xbc[:, D:D + N_GROUPS * N_STATE]
    Cm = xbc[:, D + N_GROUPS * N_STATE:]
    pre = dtin + hp[0:1]
    dt = jnp.maximum(pre, 0.0) + jnp.log(1.0 + jnp.exp(-jnp.abs(pre)))
    a_row = -jnp.exp(hp[1:2])
    da = dt * a_row
    ri = lax.broadcasted_iota(jnp.int32, (L, L), 0)
    ci = lax.broadcasted_iota(jnp.int32, (L, L), 1)
    causal = ri >= ci
    cum = _dot01(causal.astype(F32), da, 3, split_lhs=False)
    cum_t = cum.T
    cum_last = cum[L - 1:L]
    eo = jnp.exp(cum)
    dec = jnp.exp(cum_last - cum)
    cd = jnp.exp(cum_last)
    exm = _expand_mat()
    rows8 = jnp.concatenate([cd, hp[2:3], jnp.zeros((6, LANES), F32)], axis=0)
    rep = _dot01(jnp.concatenate([dt, eo, dec, rows8], axis=0), exm, 2)
    dt_rep, eo_rep, dec_rep = rep[0:L], rep[L:2 * L], rep[2 * L:3 * L]
    cd_rep, dskip_rep = rep[3 * L:3 * L + 1], rep[3 * L + 1:3 * L + 2]
    xdt = X * dt_rep
    out.update(X=X, Bm=Bm, Cm=Cm, pre=pre, dt=dt, a_row=a_row, cum=cum, cum_t=cum_t, eo=eo, dec=dec, cd=cd,
               dt_rep=dt_rep, eo_rep=eo_rep, dec_rep=dec_rep, cd_rep=cd_rep, dskip_rep=dskip_rep, xdt=xdt,
               causal=causal, anti=(ri <= ci).astype(F32), exm=exm)
    G, lms, yoff, hnew, xdec = [], [], [], [], []
    for g in range(N_GROUPS):
        gs = slice(g * GROUP_W, (g + 1) * GROUP_W)
        Bg = Bm[:, g * N_STATE:(g + 1) * N_STATE]
        Cg = Cm[:, g * N_STATE:(g + 1) * N_STATE]
        Gg = _mm_nt(Cg, Bg)
        G.append(Gg)
        for hh in range(N_HEADS // N_GROUPS):
            h = g * (N_HEADS // N_GROUPS) + hh
            seg = cum[:, h:h + 1] - cum_t[h:h + 1, :]
            lm = jnp.where(causal, jnp.exp(jnp.minimum(seg, 0.0)), 0.0)
            lms.append(lm)
            hs = slice(h * HEAD_DIM, (h + 1) * HEAD_DIM)
            y_scr[:, hs] = _mm(Gg * lm, xdt[:, hs])
        xd = xdt[:, gs] * dec_rep[:, gs]
        xdec.append(xd)
        sgm = _mm_tn(Bg, xd)
        yoff.append(_mm(Cg, hprev[g]) * eo_rep[:, gs])
        hnew.append(hprev[g] * cd_rep[:, gs] + sgm)
    out.update(G=G, lms=lms, yoff=yoff, hnew=hnew, xdec=xdec)
    y = y_scr[...] + jnp.concatenate(yoff, axis=1) + dskip_rep * X
    sz = _sigmoid(z)
    silz = z * sz
    yz = y * silz
    rg, yn = [], []
    for g in range(N_GROUPS):
        gs = slice(g * GROUP_W, (g + 1) * GROUP_W)
        r = lax.rsqrt(jnp.mean(yz[:, gs] * yz[:, gs], axis=-1, keepdims=True) + EPS)
        rg.append(r)
        yn.append(yz[:, gs] * r)
    yn = jnp.concatenate(yn, axis=1)
    out.update(y=y, sz=sz, silz=silz, rg=rg, yn=yn)
    out["y_ssd"] = yn * gssd
    return out


def _mixer_fwd(pm, cw, cb, hp, gssd, wpool, pscale, nb, seq, shards):
    nc = seq // CHUNK
    ns = len(shards)
    steps = nb * nc

    def body(*refs):
        pm_ref, cw_ref, cb_ref, hp_ref, gs_ref, wp_ref, ps_ref = refs[:7]
        sh_refs = refs[7:7 + ns]
        ym_ref, hs_ref, cv_ref = refs[7 + ns:10 + ns]
        ga_refs = refs[10 + ns:10 + 2 * ns]
        halo_p, halo_x, state, y_scr, send, recv, loc = refs[10 + 2 * ns:]
        c = pl.program_id(1)
        step = pl.program_id(0) * nc + c
        gather = _Gather(sh_refs, ga_refs, send, recv, loc)
        gather.begin_hosted(step, steps)

        @pl.when(c == 0)
        def _():
            halo_p[...] = jnp.zeros_like(halo_p)
            halo_x[...] = jnp.zeros_like(halo_x)
            state[...] = jnp.zeros_like(state)

        up = pm_ref[:, 0:POOL_W]
        z = pm_ref[:, OFF_Z:OFF_XBC]
        ux = pm_ref[:, OFF_XBC:OFF_DT]
        hprev = [state[0], state[1]]
        hs_ref[0, 0, 0] = hprev[0]
        hs_ref[0, 0, 1] = hprev[1]
        o = _chunk_forward(up, z, ux, pm_ref[:, OFF_DT:], halo_p[...], halo_x[...], hprev, cw_ref[...], cb_ref[...],
                           hp_ref[...], gs_ref[...], wp_ref[...], ps_ref[...], c * CHUNK, y_scr)
        ym_ref[:, 0:POOL_W] = o["y_pool"].astype(BF16)
        ym_ref[:, POOL_W:] = o["y_ssd"].astype(BF16)
        cv_ref[...] = o["cv"]
        state[0] = o["hnew"][0]
        state[1] = o["hnew"][1]
        halo_p[...] = up[CHUNK - POOL_HALO:]
        halo_x[...] = ux[CHUNK - CONV_HALO:]
        gather.end_hosted(step, steps)

    def full(shape):
        return pl.BlockSpec(shape, lambda b, c: (0,) * len(shape))

    T = nb * seq
    return pl.pallas_call(
        body, name="mixer_fwd", grid=(nb, nc),
        in_specs=[pl.BlockSpec((CHUNK, PROJ_W), lambda b, c: (b * nc + c, 0)),
                  full((4, CONV_CH)), full((1, CONV_CH)), full((8, LANES)), full((1, D)),
                  full((4, LANES, LANES)), full((1, POOL_W))] + [ANY_SPEC] * ns,
        out_specs=[pl.BlockSpec((CHUNK, MIX_W), lambda b, c: (b * nc + c, 0)),
                   pl.BlockSpec((1, 1, N_GROUPS, N_STATE, GROUP_W), lambda b, c: (b, c, 0, 0, 0)),
                   pl.BlockSpec((CHUNK, CONV_CH), lambda b, c: (b * nc + c, 0))] + [ANY_SPEC] * ns,
        out_shape=[jax.ShapeDtypeStruct((T, MIX_W), BF16),
                   jax.ShapeDtypeStruct((nb, nc, N_GROUPS, N_STATE, GROUP_W), F32),
                   jax.ShapeDtypeStruct((T, CONV_CH), F32)]
        + [jax.ShapeDtypeStruct((N_DEV,) + v.shape, v.dtype) for v in shards],
        scratch_shapes=[pltpu.VMEM((POOL_HALO, POOL_W), F32), pltpu.VMEM((CONV_HALO, CONV_CH), F32),
                        pltpu.VMEM((N_GROUPS, N_STATE, GROUP_W), F32), pltpu.VMEM((CHUNK, D), F32)] + _gather_scratch(shards),
        compiler_params=_cparams(),
    )(pm, cw, cb, hp, gssd, wpool, pscale, *shards)


def _mixer_bwd(pm, cvs, dym, hstates, cw, cb, hp, gssd, wpool, pscale, nb, seq, after=()):
    nc = seq // CHUNK
    hpg = N_HEADS // N_GROUPS
    na = len(after)

    def body(*refs):
        (pm_ref, hpool_ref, cv_ref, dy_ref, hs_ref, cw_ref, cb_ref, hp_ref, gs_ref, wp_ref, ps_ref) = refs[:11]
        dpm_ref, dconv_ref, dhp_ref, dvec_ref, dwp_ref = refs[11 + na:16 + na]
        nxt_q, nxt_cv, rstate, y_scr, dx_scr = refs[16 + na:]
        b = pl.program_id(0)
        ci = pl.program_id(1)
        c = nc - 1 - ci

        @pl.when((b == 0) & (ci == 0))
        def _():
            for r in (dconv_ref, dhp_ref, dvec_ref, dwp_ref):
                r[...] = jnp.zeros_like(r)

        @pl.when(ci == 0)
        def _():
            nxt_q[...] = jnp.zeros_like(nxt_q)
            nxt_cv[...] = jnp.zeros_like(nxt_cv)
            rstate[...] = jnp.zeros_like(rstate)

        first = (c > 0).astype(F32)
        up = pm_ref[:, 0:POOL_W]
        z = pm_ref[:, OFF_Z:OFF_XBC]
        ux = pm_ref[:, OFF_XBC:OFF_DT]
        halo_p = hpool_ref[...] * first
        hprev = [hs_ref[0, 0, 0], hs_ref[0, 0, 1]]
        cw, cb, hp, gssd, wpool, pscale = cw_ref[...], cb_ref[...], hp_ref[...], gs_ref[...], wp_ref[...], ps_ref[...]
        o = _chunk_forward(up, z, ux, pm_ref[:, OFF_DT:], halo_p, None, hprev, cw, cb, hp, gssd, wpool, pscale,
                           c * CHUNK, y_scr, cv=cv_ref[...])
        L = CHUNK
        dy_pool = dy_ref[:, 0:POOL_W].astype(F32)
        dy_ssd = dy_ref[:, POOL_W:].astype(F32)

        dvec_ref[1:2, 0:POOL_W] += jnp.sum(dy_pool * jnp.concatenate(o["yp"], axis=1), axis=0, keepdims=True)
        dyp = dy_pool * pscale
        qs = []
        dps = []
        for gi in range(len(WINDOWS)):
            sl = slice(gi * LANES, (gi + 1) * LANES)
            dwp_ref[gi] += _mm_tn(o["p"][gi], dyp[:, sl])
            dpg = _mm_nt(dyp[:, sl], wpool[gi])
            dps.append(dpg)
            qs.append(dpg * o["inv"][gi])
        q = jnp.concatenate(qs, axis=1)
        e = jnp.concatenate([q, nxt_q[...]], axis=0)
        n = L + POOL_HALO
        s2 = e + pltpu.roll(e, n - 1, 0)
        s4 = s2 + pltpu.roll(s2, n - 2, 0)
        s8 = s4 + pltpu.roll(s4, n - 4, 0)
        s16 = s8 + pltpu.roll(s8, n - 8, 0)
        sums = (s2, s4, s8, s16)
        for gi in range(len(WINDOWS)):
            sl = slice(gi * LANES, (gi + 1) * LANES)
            dpm_ref[:, sl] = (sums[gi][:L, sl] - dps[gi]).astype(BF16)
        nxt_q[...] = q[:POOL_HALO]

        yn, y, silz, sz = o["yn"], o["y"], o["silz"], o["sz"]
        dvec_ref[0:1] += jnp.sum(dy_ssd * yn, axis=0, keepdims=True)
        dyn = dy_ssd * gssd
        dyz = []
        for g in range(N_GROUPS):
            gs = slice(g * GROUP_W, (g + 1) * GROUP_W)
            mean = jnp.mean(dyn[:, gs] * yn[:, gs], axis=-1, keepdims=True)
            dyz.append(o["rg"][g] * (dyn[:, gs] - yn[:, gs] * mean))
        dyz = jnp.concatenate(dyz, axis=1)
        dyv = dyz * silz
        dpm_ref[:, OFF_Z:OFF_XBC] = (dyz * y * (sz * (1.0 + z * (1.0 - sz)))).astype(BF16)

        X, Bm, Cm, xdt = o["X"], o["Bm"], o["Cm"], o["xdt"]
        exm = o["exm"]
        rdm = _reduce_mat()
        lane = lax.broadcasted_iota(jnp.int32, (1, LANES), 1)
        sub = lax.broadcasted_iota(jnp.int32, (LANES, 1), 0)
        dX = o["dskip_rep"] * dyv
        yoff_full = jnp.concatenate(o["yoff"], axis=1)
        rs = jnp.zeros((L, LANES), F32)
        cs_t = jnp.zeros((LANES, L), F32)
        dBs, dCs = [], []
        rh_sums = []
        ddec = []
        for g in range(N_GROUPS):
            gs = slice(g * GROUP_W, (g + 1) * GROUP_W)
            Bg = Bm[:, g * N_STATE:(g + 1) * N_STATE]
            Cg = Cm[:, g * N_STATE:(g + 1) * N_STATE]
            Gg = o["G"][g]
            R = rstate[g]
            dwm = dyv[:, gs] * o["eo_rep"][:, gs]
            dC = _mm_nt(dwm, hprev[g])
            dH = _mm_tn(Cg, dwm)
            dG = jnp.zeros((L, L), F32)
            for hh in range(hpg):
                h = g * hpg + hh
                hs = slice(h * HEAD_DIM, (h + 1) * HEAD_DIM)
                lm = o["lms"][h]
                m_h = Gg * lm
                dM = _mm_nt(dyv[:, hs], xdt[:, hs])
                dx_scr[:, hs] = _mm_tn(m_h, dyv[:, hs])
                qm = dM * m_h
                rs = rs + jnp.sum(qm, axis=1, keepdims=True) * (lane == h).astype(F32)
                cs_t = cs_t + (sub == h).astype(F32) * jnp.sum(qm, axis=0, keepdims=True)
                dG = dG + dM * lm
            dC = dC + _mm(dG, Bg)
            dB = _mm_tn(dG, Cg)
            zx = _mm(Bg, R)
            dxdt_state = zx * o["dec_rep"][:, gs]
            ddec.append(zx * xdt[:, gs])
            dB = dB + _mm_nt(o["xdec"][g], R)
            rh_sums.append(jnp.sum(R * hprev[g], axis=0, keepdims=True))
            rstate[g] = dH + o["cd_rep"][:, gs] * R
            dx_scr[:, gs] = dx_scr[:, gs] + dxdt_state
            dBs.append(dB)
            dCs.append(dC)
        dxdt = dx_scr[...]
        tail = jnp.concatenate([jnp.sum(dyv * X, axis=0, keepdims=True), jnp.concatenate(rh_sums, axis=1),
                                jnp.zeros((6, D), F32)], axis=0)
        red = _dot01(jnp.concatenate([dyv * yoff_full, jnp.concatenate(ddec, axis=1), dxdt * X, tail], axis=0), rdm, 2)
        d_dskip, dcd_row = red[3 * L:3 * L + 1], red[3 * L + 1:3 * L + 2]
        ddec_h = red[L:2 * L] * o["dec"]
        dcum_last = jnp.sum(ddec_h, axis=0, keepdims=True) + dcd_row * o["cd"]
        dcum = red[0:L] + rs - cs_t.T - ddec_h + (sub == L - 1).astype(F32) * dcum_last
        dda = _dot01(o["anti"], dcum, 3, split_lhs=False)
        ddt_v = dda * o["a_row"] + red[2 * L:3 * L]
        dX = dX + dxdt * o["dt_rep"]
        head_mask = (lane < N_HEADS).astype(F32)
        d_alog = jnp.sum(dda * o["dt"], axis=0, keepdims=True) * o["a_row"] * head_mask
        dpre = ddt_v * _sigmoid(o["pre"]) * head_mask
        dpm_ref[:, OFF_DT:] = dpre.astype(BF16)
        d_dtb = jnp.sum(dpre, axis=0, keepdims=True)
        dhp_ref[...] += jnp.concatenate([d_dtb, d_alog, d_dskip * head_mask, jnp.zeros((5, LANES), F32)], axis=0)

        dxbc = jnp.concatenate([dX] + dBs + dCs, axis=1)
        sg, cv = o["sg"], o["cv"]
        dcv = dxbc * (sg * (1.0 + cv * (1.0 - sg)))
        e2 = jnp.concatenate([dcv, nxt_cv[...]], axis=0)
        n2 = L + CONV_HALO
        ahead = [dcv, pltpu.roll(e2, n2 - 1, 0)[:L], pltpu.roll(e2, n2 - 2, 0)[:L], pltpu.roll(e2, n2 - 3, 0)[:L]]
        dconv_ref[0:5] += jnp.concatenate(
            [jnp.sum(ux * ahead[3 - k], axis=0, keepdims=True) for k in range(4)]
            + [jnp.sum(dcv, axis=0, keepdims=True)], axis=0)
        dux = ahead[0] * cw[3:4] + ahead[1] * cw[2:3] + ahead[2] * cw[1:2] + ahead[3] * cw[0:1]
        dpm_ref[:, OFF_XBC:OFF_DT] = dux.astype(BF16)
        nxt_cv[...] = dcv[:CONV_HALO]

    def full(shape):
        return pl.BlockSpec(shape, lambda b, c: (0,) * len(shape))

    def rowblk(b, c):
        return b * nc + (nc - 1 - c)

    hp_blocks = CHUNK // POOL_HALO
    T = nb * seq
    return pl.pallas_call(
        body, name="mixer_bwd", grid=(nb, nc),
        in_specs=[pl.BlockSpec((CHUNK, PROJ_W), lambda b, c: (rowblk(b, c), 0)),
                  pl.BlockSpec((POOL_HALO, POOL_W), lambda b, c: (jnp.maximum(rowblk(b, c) * hp_blocks - 1, 0), 0)),
                  pl.BlockSpec((CHUNK, CONV_CH), lambda b, c: (rowblk(b, c), 0)),
                  pl.BlockSpec((CHUNK, MIX_W), lambda b, c: (rowblk(b, c), 0)),
                  pl.BlockSpec((1, 1, N_GROUPS, N_STATE, GROUP_W), lambda b, c: (b, nc - 1 - c, 0, 0, 0)),
                  full((4, CONV_CH)), full((1, CONV_CH)), full((8, LANES)), full((1, D)),
                  full((4, LANES, LANES)), full((1, POOL_W))] + [ANY_SPEC] * na,
        out_specs=[pl.BlockSpec((CHUNK, PROJ_W), lambda b, c: (rowblk(b, c), 0)),
                   full((8, CONV_CH)), full((8, LANES)), full((8, D)), full((4, LANES, LANES))],
        out_shape=[jax.ShapeDtypeStruct((T, PROJ_W), BF16),
                   jax.ShapeDtypeStruct((8, CONV_CH), F32), jax.ShapeDtypeStruct((8, LANES), F32),
                   jax.ShapeDtypeStruct((8, D), F32), jax.ShapeDtypeStruct((4, LANES, LANES), F32)],
        scratch_shapes=[pltpu.VMEM((POOL_HALO, POOL_W), F32), pltpu.VMEM((CONV_HALO, CONV_CH), F32),
                        pltpu.VMEM((N_GROUPS, N_STATE, GROUP_W), F32), pltpu.VMEM((CHUNK, D), F32),
                        pltpu.VMEM((CHUNK, D), F32)],
        compiler_params=_cparams(),
    )(pm, pm, cvs, dym, hstates, cw, cb, hp, gssd, wpool, pscale, *after)


def _mlp_fused(x2, ymix, target, mod, g_mlp, g_final, w_out, w_up, w_down, seq):
    T = x2.shape[0]
    tm = min(256, seq)
    tps = seq // tm
    nblk = D_FF // FF_BLK

    def body(x_ref, ym_ref, tg_ref, mod_ref, gm_ref, gf_ref, wo_ref, wu_ref, wd_ref,
             da_ref, dym_ref, dh1_ref, u2_ref, f_ref, dup_ref, ddn_ref, dmod_ref, acc_ref, relu_scr):
        i = pl.program_id(0)

        @pl.when(i == 0)
        def _():
            acc_ref[...] = jnp.zeros_like(acc_ref)

        @pl.when(i % tps == 0)
        def _():
            dmod_ref[...] = jnp.zeros_like(dmod_ref)

        md = mod_ref[0]
        gate_m, shift_f, scale_f, gate_f = md[2:3], md[3:4], md[4:5], md[5:6]
        g_mlp, g_fin = gm_ref[...], gf_ref[...]
        a = jnp.dot(ym_ref[...], wo_ref[...], preferred_element_type=F32)
        h1 = x_ref[...] + gate_m * a
        r2 = lax.rsqrt(jnp.mean(h1 * h1, axis=-1, keepdims=True) + EPS)
        n2 = h1 * r2
        u2 = (n2 * g_mlp) * (1.0 + scale_f) + shift_f
        u2b = u2.astype(BF16)
        u2_ref[...] = u2b
        dn = jnp.zeros((tm, D), F32)
        for j in range(nblk):
            js = slice(j * FF_BLK, (j + 1) * FF_BLK)
            upj = jnp.maximum(jnp.dot(u2b, wu_ref[j], preferred_element_type=F32), 0.0)
            relu_scr[:, js] = upj
            fj = (upj * upj).astype(BF16)
            f_ref[:, js] = fj
            dn = dn + jnp.dot(fj, wd_ref[j], preferred_element_type=F32)
        h2 = h1 + gate_f * dn
        r3 = lax.rsqrt(jnp.mean(h2 * h2, axis=-1, keepdims=True) + EPS)
        n3 = h2 * r3
        err = n3 * g_fin - tg_ref[...]
        loss = 0.5 * jnp.sum(jnp.mean(err * err, axis=-1, keepdims=True), axis=0, keepdims=True)
        dout = err * (1.0 / D)
        d_gfin = jnp.sum(dout * n3, axis=0, keepdims=True)
        dn3 = dout * g_fin
        dh2 = r3 * (dn3 - n3 * jnp.mean(dn3 * n3, axis=-1, keepdims=True))
        d_gate_f = jnp.sum(dh2 * dn, axis=0, keepdims=True)
        ddn = (gate_f * dh2).astype(BF16)
        ddn_ref[...] = ddn
        du2 = jnp.zeros((tm, D), F32)
        for j in range(nblk):
            js = slice(j * FF_BLK, (j + 1) * FF_BLK)
            dfj = lax.dot_general(ddn, wd_ref[j], (((1,), (1,)), ((), ())), preferred_element_type=F32)
            dupj = (dfj * (2.0 * relu_scr[:, js])).astype(BF16)
            dup_ref[:, js] = dupj
            du2 = du2 + lax.dot_general(dupj, wu_ref[j], (((1,), (1,)), ((), ())), preferred_element_type=F32)
        d_scale_f = jnp.sum(du2 * (n2 * g_mlp), axis=0, keepdims=True)
        d_shift_f = jnp.sum(du2, axis=0, keepdims=True)
        d_gmlp = jnp.sum(du2 * (1.0 + scale_f) * n2, axis=0, keepdims=True)
        dn2 = du2 * (g_mlp * (1.0 + scale_f))
        dh1 = dh2 + r2 * (dn2 - n2 * jnp.mean(dn2 * n2, axis=-1, keepdims=True))
        dh1_ref[...] = dh1
        d_gate_m = jnp.sum(dh1 * a, axis=0, keepdims=True)
        da = (gate_m * dh1).astype(BF16)
        da_ref[...] = da
        dym_ref[...] = lax.dot_general(da, wo_ref[...], (((1,), (1,)), ((), ())),
                                       preferred_element_type=F32).astype(BF16)
        dmod_ref[0] += jnp.concatenate([jnp.zeros((2, D), F32), d_gate_m, d_shift_f, d_scale_f, d_gate_f,
                                        jnp.zeros((2, D), F32)], axis=0)
        acc_ref[...] += jnp.concatenate([d_gmlp, d_gfin, loss * jnp.ones((1, D), F32), jnp.zeros((5, D), F32)], axis=0)

    whole = pl.BlockSpec(memory_space=pltpu.VMEM)

    def tok(w):
        return pl.BlockSpec((tm, w), lambda i: (i, 0))

    def vec():
        return pl.BlockSpec((1, D), lambda i: (0, 0))

    nb = T // seq
    return pl.pallas_call(
        body, name="mlp_fused", grid=(T // tm,),
        in_specs=[tok(D), tok(MIX_W), tok(D), pl.BlockSpec((1, 8, D), lambda i: (i // tps, 0, 0)), vec(), vec(),
                  whole, whole, whole],
        out_specs=[tok(D), tok(MIX_W), tok(D), tok(D), tok(D_FF), tok(D_FF), tok(D),
                   pl.BlockSpec((1, 8, D), lambda i: (i // tps, 0, 0)), pl.BlockSpec((8, D), lambda i: (0, 0))],
        out_shape=[jax.ShapeDtypeStruct((T, D), BF16), jax.ShapeDtypeStruct((T, MIX_W), BF16),
                   jax.ShapeDtypeStruct((T, D), F32), jax.ShapeDtypeStruct((T, D), BF16),
                   jax.ShapeDtypeStruct((T, D_FF), BF16), jax.ShapeDtypeStruct((T, D_FF), BF16),
                   jax.ShapeDtypeStruct((T, D), BF16), jax.ShapeDtypeStruct((nb, 8, D), F32),
                   jax.ShapeDtypeStruct((8, D), F32)],
        scratch_shapes=[pltpu.VMEM((tm, D_FF), F32)],
        compiler_params=_cparams(),
    )(x2, ymix, target, mod, g_mlp, g_final, w_out, w_up, w_down)


def _in_bwd(x2, dh1, dpb, mod, g_mix, w_cat, dmod_a, acc_a, seq):
    T = x2.shape[0]
    tm = min(1024, seq)
    tps = seq // tm
    steps = T // tm

    def body(x_ref, dh_ref, dpb_ref, mod_ref, g_ref, w_ref, dma_ref, acca_ref, dx_ref, dmod_ref, acc_ref):
        i = pl.program_id(0)

        @pl.when(i == 0)
        def _():
            acc_ref[...] = acca_ref[...]

        @pl.when(i % tps == 0)
        def _():
            dmod_ref[...] = dma_ref[...]

        du = lax.dot_general(dpb_ref[...], w_ref[...], (((1,), (1,)), ((), ())), preferred_element_type=F32)
        x = x_ref[...]
        md = mod_ref[0]
        g = g_ref[...]
        r = lax.rsqrt(jnp.mean(x * x, axis=-1, keepdims=True) + EPS)
        n1 = x * r
        d_scale = jnp.sum(du * (n1 * g), axis=0, keepdims=True)
        d_shift = jnp.sum(du, axis=0, keepdims=True)
        d_g = jnp.sum(du * (1.0 + md[1:2]) * n1, axis=0, keepdims=True)
        dn1 = du * (g * (1.0 + md[1:2]))
        dx_ref[...] = dh_ref[...] + r * (dn1 - n1 * jnp.mean(dn1 * n1, axis=-1, keepdims=True))
        dmod_ref[0] += jnp.concatenate([d_shift, d_scale, jnp.zeros((6, D), F32)], axis=0)
        acc_ref[...] += jnp.concatenate([jnp.zeros((3, D), F32), d_g, jnp.zeros((4, D), F32)], axis=0)

    whole = pl.BlockSpec(memory_space=pltpu.VMEM)
    nb = T // seq
    return pl.pallas_call(
        body, name="in_bwd", grid=(steps,),
        in_specs=[pl.BlockSpec((tm, D), lambda i: (i, 0)), pl.BlockSpec((tm, D), lambda i: (i, 0)),
                  pl.BlockSpec((tm, PROJ_W), lambda i: (i, 0)),
                  pl.BlockSpec((1, 8, D), lambda i: (i // tps, 0, 0)), pl.BlockSpec((1, D), lambda i: (0, 0)),
                  whole, pl.BlockSpec((1, 8, D), lambda i: (i // tps, 0, 0)), pl.BlockSpec((8, D), lambda i: (0, 0))],
        out_specs=[pl.BlockSpec((tm, D), lambda i: (i, 0)),
                   pl.BlockSpec((1, 8, D), lambda i: (i // tps, 0, 0)), pl.BlockSpec((8, D), lambda i: (0, 0))],
        out_shape=[jax.ShapeDtypeStruct((T, D), F32),
                   jax.ShapeDtypeStruct((nb, 8, D), F32), jax.ShapeDtypeStruct((8, D), F32)],
        compiler_params=_cparams(),
    )(x2, dh1, dpb, mod, g_mix, w_cat, dmod_a, acc_a)


def _dw_in(u_b, dpb, in_cols, shards):
    T = u_b.shape[0]
    bk = min(512, T)
    nk = T // bk
    ns = len(shards)
    starts = [(in_cols * j // LANES) * LANES for j in range(N_DEV)]
    assert all(s + DW_IN_WIN <= PROJ_W and in_cols * (j + 1) <= s + DW_IN_WIN for j, s in enumerate(starts))

    def body(*refs):
        u_ref, d_ref = refs[:2]
        sh_refs = refs[2:2 + ns]
        o_ref = refs[2 + ns]
        ga_refs = refs[3 + ns:3 + 2 * ns]
        acc, send, recv, loc = refs[3 + 2 * ns:]
        k = pl.program_id(0)
        gather = _Gather(sh_refs, ga_refs, send, recv, loc)
        gather.begin_hosted(k, nk)

        @pl.when(k == 0)
        def _():
            acc[...] = jnp.zeros_like(acc)

        ut = u_ref[...].T
        for j in range(N_DEV):
            acc[j] += jnp.dot(ut, d_ref[:, starts[j]:starts[j] + DW_IN_WIN], preferred_element_type=F32)

        @pl.when(k == nk - 1)
        def _():
            for j in range(N_DEV):
                off = in_cols * j - starts[j]
                o_ref[j] = acc[j][:, off:off + in_cols].astype(BF16)

        gather.end_hosted(k, nk)

    return pl.pallas_call(
        body, name="dw_in", grid=(nk,),
        in_specs=[pl.BlockSpec((bk, D), lambda k: (k, 0)), pl.BlockSpec((bk, PROJ_W), lambda k: (k, 0))]
        + [ANY_SPEC] * ns,
        out_specs=[pl.BlockSpec((N_DEV, D, in_cols), lambda k: (0, 0, 0))] + [ANY_SPEC] * ns,
        out_shape=[jax.ShapeDtypeStruct((N_DEV, D, in_cols), BF16)]
        + [jax.ShapeDtypeStruct((N_DEV,) + v.shape, v.dtype) for v in shards],
        scratch_shapes=[pltpu.VMEM((N_DEV, D, DW_IN_WIN), F32)] + _gather_scratch(shards),
        compiler_params=_cparams(),
    )(u_b, dpb, *shards)


def _dw_blocks(a, b, name, by_rows, per_step=1, after=()):
    T, M = a.shape
    N = b.shape[1]
    bk = min(2048, T)
    nk = T // bk
    whole = pl.BlockSpec(memory_space=pltpu.VMEM)
    if by_rows:
        rows = M // N_DEV
        am = rows * per_step
        nblk = N_DEV // per_step
        a_spec, b_spec = pl.BlockSpec((bk, am), lambda i, k: (k, i)), whole
        out_blk, acc_shape = (per_step, rows, N), (am, N)
    else:
        cols = N // N_DEV
        nblk = N_DEV
        a_spec, b_spec = whole, pl.BlockSpec((bk, cols), lambda i, k: (k, i))
        out_blk, acc_shape = (1, M, cols), (M, cols)

    def body(a_ref, b_ref, *rest):
        o_ref, acc = rest[len(after):]
        k = pl.program_id(1)

        @pl.when(k == 0)
        def _():
            acc[...] = jnp.zeros_like(acc)

        tok = pl.ds(pl.multiple_of(k * bk, bk), bk)
        a_blk = a_ref[...] if by_rows else a_ref[tok, :]
        b_blk = b_ref[tok, :] if by_rows else b_ref[...]
        acc[...] += lax.dot_general(a_blk, b_blk, (((0,), (0,)), ((), ())), preferred_element_type=F32)

        @pl.when(k == nk - 1)
        def _():
            o_ref[...] = acc[...].reshape(out_blk).astype(BF16)

    return pl.pallas_call(
        body, name=name, grid=(nblk, nk), in_specs=[a_spec, b_spec] + [ANY_SPEC] * len(after),
        out_specs=pl.BlockSpec(out_blk, lambda i, k: (i, 0, 0)),
        out_shape=jax.ShapeDtypeStruct((N_DEV,) + out_blk[1:], BF16),
        scratch_shapes=[pltpu.VMEM(acc_shape, F32)],
        compiler_params=_cparams(),
    )(a, b, *after)


def _adam_parts(parts, w, m, v, name):
    rows, cols = w.shape
    br = rows
    for cand in range(rows, 15, -16):
        if rows % cand == 0 and cand * cols * 4 <= ADAM_BLOCK_BYTES:
            br = cand
            break

    def body(p_ref, w_ref, m_ref, v_ref, g_out, dl_out, m_out, v_out):
        g = p_ref[0].astype(F32)
        for k in range(1, N_DEV):
            g = g + p_ref[k].astype(F32)
        g_out[...] = g
        dl, mn, vn = _adam_math(w_ref[...], g, m_ref[...], v_ref[...])
        dl_out[...] = dl
        m_out[...] = mn
        v_out[...] = vn

    wspec = pl.BlockSpec((br, cols), lambda i: (i, 0))
    return pl.pallas_call(
        body, name=name, grid=(rows // br,),
        in_specs=[pl.BlockSpec((N_DEV, br, cols), lambda i: (0, i, 0)), wspec, wspec, wspec],
        out_specs=[wspec] * 4, out_shape=[jax.ShapeDtypeStruct((rows, cols), F32)] * 4,
        compiler_params=_cparams(),
    )(parts, w, m, v)


def _adam_plain(g, w, m, v, name):
    def body(g_ref, w_ref, m_ref, v_ref, dl_out, m_out, v_out):
        dl, mn, vn = _adam_math(w_ref[...], g_ref[...], m_ref[...], v_ref[...])
        dl_out[...] = dl
        m_out[...] = mn
        v_out[...] = vn

    return pl.pallas_call(body, name=name, out_shape=[jax.ShapeDtypeStruct(w.shape, F32)] * 3,
                          compiler_params=_cparams())(g, w, m, v)


SMALL_PARAMS = ("b_ada", "g_mix", "conv_b", "dt_bias", "a_log", "d_skip", "g_ssd", "pool_scale", "g_mlp", "g_final")


def _small_adam(gathered, params):
    n_par = len(SMALL_PARAMS)
    nb = gathered[0].shape[1]

    def body(*refs):
        dmod_ref, acc_ref, conv_ref, vec_ref, hd_ref = refs[:5]
        par_refs = refs[5:5 + 3 * n_par]
        out_refs = refs[5 + 3 * n_par:5 + 7 * n_par]
        cw_out, acc_out = refs[5 + 7 * n_par:]

        def total(ref):
            t = ref[0]
            for k in range(1, N_DEV):
                t = t + ref[k]
            return t

        dm = total(dmod_ref)
        dmb = dm[0]
        for b in range(1, nb):
            dmb = dmb + dm[b]
        ac, cv, vc, hd = total(acc_ref), total(conv_ref), total(vec_ref), total(hd_ref)
        cw_out[...] = cv[0:4]
        acc_out[...] = ac
        grads = {
            "b_ada": jnp.concatenate([dmb[r:r + 1] for r in range(6)], axis=1), "g_mix": ac[3:4], "conv_b": cv[4:5],
            "dt_bias": hd[0:1, 0:N_HEADS], "a_log": hd[1:2, 0:N_HEADS], "d_skip": hd[2:3, 0:N_HEADS],
            "g_ssd": vc[0:1], "pool_scale": vc[1:2, 0:POOL_W], "g_mlp": ac[0:1], "g_final": ac[1:2],
        }
        for i, name in enumerate(SMALL_PARAMS):
            w_ref, m_ref, v_ref = par_refs[3 * i:3 * i + 3]
            g = grads[name]
            dl, mn, vn = _adam_math(w_ref[...], g, m_ref[...], v_ref[...])
            g_o, d_o, m_o, v_o = out_refs[4 * i:4 * i + 4]
            g_o[...] = g
            d_o[...] = dl
            m_o[...] = mn
            v_o[...] = vn

    flat = [a for name in SMALL_PARAMS for a in params[name]]
    out_shape = [jax.ShapeDtypeStruct(params[name][0].shape, F32) for name in SMALL_PARAMS for _ in range(4)]
    out_shape += [jax.ShapeDtypeStruct((4, CONV_CH), F32), jax.ShapeDtypeStruct((8, D), F32)]
    return pl.pallas_call(body, name="small_adam", out_shape=out_shape, compiler_params=_cparams())(*gathered, *flat)


def kernel(x, c, w_ada, b_ada, g_mix, w_in, conv_w, conv_b, dt_bias, a_log, d_skip, g_ssd, w_pool, pool_scale, w_out, g_mlp, w_up, w_down, g_final, loss_target, m_w_ada, m_b_ada, m_g_mix, m_w_in, m_conv_w, m_conv_b, m_dt_bias, m_a_log, m_d_skip, m_g_ssd, m_w_pool, m_pool_scale, m_w_out, m_g_mlp, m_w_up, m_w_down, m_g_final, v_w_ada, v_b_ada, v_g_mix, v_w_in, v_conv_w, v_conv_b, v_dt_bias, v_a_log, v_d_skip, v_g_ssd, v_w_pool, v_pool_scale, v_w_out, v_g_mlp, v_w_up, v_w_down, v_g_final):
    nb, seq, _ = x.shape
    T = nb * seq
    me = 4 * lax.axis_index("x") + 2 * lax.axis_index("y") + lax.axis_index("c")
    in_cols = w_in.shape[2]
    ada_cols = w_ada.shape[2]
    cw_cols = conv_w.shape[2]

    c_g, cw_g, win_g = _all_gather([c, conv_w[0], w_in[0].astype(BF16)], "ag_first")
    c_all = c_g.reshape(N_DEV * nb, D)
    cw_full = cw_g.transpose(1, 0, 2).reshape(4, CONV_CH)

    b_slice = lax.dynamic_slice(b_ada, (0, me * ada_cols), (1, ada_cols))
    mod_cols = _ada_fwd(c_all, w_ada[0], b_slice)
    (mod_g,) = _all_gather([mod_cols], "ag_mod")
    mod_all = mod_g.transpose(1, 0, 2).reshape(N_DEV * nb, 6, D)
    mod_mine = lax.dynamic_slice(mod_all, (me * nb, 0, 0), (nb, 6, D))
    mod = jnp.pad(mod_mine, ((0, 0), (0, 2), (0, 0)))

    x2 = x.reshape(T, D)
    tg2 = loss_target.reshape(T, D)
    heads = jnp.pad(jnp.concatenate([dt_bias, a_log, d_skip], axis=0), ((0, 5), (0, LANES - N_HEADS)))
    wpool_b = w_pool[0]
    u_b, pm, w_cat, wup_g = _mix_in(x2, mod, g_mix, win_g, seq, [w_up[0].astype(BF16)])
    ymix, hstates, cvs, wout_g, wdn_g = _mixer_fwd(
        pm, cw_full, conv_b, heads, g_ssd, wpool_b, pool_scale, nb, seq,
        [w_out[0].astype(BF16), w_down[0].astype(BF16)])
    da_b, dym, dh1, u2_b, f_b, dup_b, ddn_b, dmod_a, acc_a = _mlp_fused(
        x2, ymix, tg2, mod, g_mlp, g_final.reshape(1, D), wout_g.reshape(MIX_W, D), wup_g, wdn_g, seq)

    gout_p = _dw_blocks(ymix, da_b, "dw_out", True, per_step=4)
    gup_p = _dw_blocks(u2_b, dup_b, "dw_up", False)
    gdn_p = _dw_blocks(f_b, ddn_b, "dw_down", True)
    ex_mlp = _exchange_start([gout_p, gup_p, gdn_p], "gmlp_start")
    dpb, d_conv, d_heads, d_vec, d_wpool = _mixer_bwd(
        pm, cvs, dym, hstates, cw_full, conv_b, heads, g_ssd, wpool_b, pool_scale, nb, seq, after=[ex_mlp[4]])
    gin_p, conv_g, vec_g, heads_g, wpool_parts = _dw_in(
        u_b, dpb, in_cols, [d_conv, d_vec, d_heads, d_wpool.reshape(4 * LANES, LANES)])
    ex_in = _exchange_start([gin_p], "gin_start")
    grad_x2, dmod, acc = _in_bwd(x2, dh1, dpb, mod, g_mix + ex_in[4][0:1, 0:1], w_cat, dmod_a, acc_a, seq)

    gout_r, gup_r, gdn_r = _exchange_wait(ex_mlp, dmod, "gmlp_wait", me)
    g_out, d_out, nm_out, nv_out = _adam_parts(gout_r, w_out[0], m_w_out[0], v_w_out[0], "adam_w_out")
    g_up, d_up, nm_up, nv_up = _adam_parts(gup_r, w_up[0], m_w_up[0], v_w_up[0], "adam_w_up")
    g_dn, d_dn, nm_dn, nv_dn = _adam_parts(gdn_r, w_down[0], m_w_down[0], v_w_down[0], "adam_w_down")

    dmod_g, acc_g = _all_gather([dmod, acc], "ag_small_bwd", after=[nm_out, nm_up, nm_dn])
    pool2 = (4 * LANES, LANES)
    wpool_outs = _adam_parts(wpool_parts, w_pool.reshape(pool2), m_w_pool.reshape(pool2), v_w_pool.reshape(pool2),
                             "adam_w_pool")
    small_params = {
        "b_ada": (b_ada, m_b_ada, v_b_ada), "g_mix": (g_mix, m_g_mix, v_g_mix), "conv_b": (conv_b, m_conv_b, v_conv_b),
        "dt_bias": (dt_bias, m_dt_bias, v_dt_bias), "a_log": (a_log, m_a_log, v_a_log),
        "d_skip": (d_skip, m_d_skip, v_d_skip), "g_ssd": (g_ssd, m_g_ssd, v_g_ssd),
        "pool_scale": (pool_scale, m_pool_scale, v_pool_scale), "g_mlp": (g_mlp, m_g_mlp, v_g_mlp),
        "g_final": tuple(a.reshape(1, D) for a in (g_final, m_g_final, v_g_final)),
    }
    small_res = _small_adam([dmod_g, acc_g, conv_g, vec_g, heads_g], small_params)
    g_cw_full, acc_sum = small_res[-2:]
    loss = acc_sum[2, 0]

    g_cw = lax.dynamic_slice(g_cw_full, (0, me * cw_cols), (4, cw_cols))
    d_cwp, nm_cwp, nv_cwp = _adam_plain(g_cw, conv_w[0], m_conv_w[0], v_conv_w[0], "adam_conv_w")

    dmod_all = dmod_g[:, :, 0:6].reshape(N_DEV * nb, 6 * D)
    dmod_slice = lax.dynamic_slice(dmod_all, (0, me * ada_cols), (N_DEV * nb, ada_cols))
    g_ada, d_ada, nm_ada, nv_ada = _ada_bwd_adam(c_all, dmod_slice, w_ada[0], m_w_ada[0], v_w_ada[0])

    ex_after = nm_ada[0:8, 0:LANES] + acc_sum[:, 0:LANES]
    (gin_r,) = _exchange_wait(ex_in, ex_after, "gin_wait", me)
    g_in, d_in, nm_in, nv_in = _adam_parts(gin_r, w_in[0], m_w_in[0], v_w_in[0], "adam_w_in")

    def small_outs(kind, wpool):
        res = {name: small_res[4 * i + kind] for i, name in enumerate(SMALL_PARAMS)}
        res["g_final"] = res["g_final"].reshape(D)
        res["w_pool"] = wpool.reshape(1, 4, LANES, LANES)
        return res

    def big_outs(ada, win, cwp, wout, wup, wdn):
        return {"w_ada": ada[None], "w_in": win.reshape(1, D, in_cols), "conv_w": cwp[None], "w_out": wout[None],
                "w_up": wup[None], "w_down": wdn[None]}

    order = ["w_ada", "b_ada", "g_mix", "w_in", "conv_w", "conv_b", "dt_bias", "a_log", "d_skip", "g_ssd", "w_pool",
             "pool_scale", "w_out", "g_mlp", "w_up", "w_down", "g_final"]
    groups = [
        {**small_outs(0, wpool_outs[0]), **big_outs(g_ada, g_in, g_cw, g_out, g_up, g_dn)},
        {**small_outs(1, wpool_outs[1]), **big_outs(d_ada, d_in, d_cwp, d_out, d_up, d_dn)},
        {**small_outs(2, wpool_outs[2]), **big_outs(nm_ada, nm_in, nm_cwp, nm_out, nm_up, nm_dn)},
        {**small_outs(3, wpool_outs[3]), **big_outs(nv_ada, nv_in, nv_cwp, nv_out, nv_up, nv_dn)},
    ]
    outs = [loss, grad_x2.reshape(nb, seq, D)]
    for grp in groups:
        outs += [grp[n] for n in order]
    return tuple(outs)
```

```python
import functools

import jax
import jax.numpy as jnp
from jax import lax
from jax.experimental import pallas as pl
from jax.experimental.pallas import tpu as pltpu

F32, BF16 = jnp.float32, jnp.bfloat16
MESH = pl.DeviceIdType.MESH
N_DEV = 8
D = 1024
LANES = 128
CHUNK = 128
POOL_W = 512
WINDOWS = (2, 4, 8, 16)
N_HEADS = 16
HEAD_DIM = 64
N_GROUPS = 2
GROUP_W = 512
N_STATE = 128
CONV_CH = 1536
OFF_Z, OFF_XBC, OFF_DT, IN_W = 512, 1536, 3072, 3088
PROJ_W = OFF_DT + LANES
MIX_W = 1536
D_FF = 4096
FF_BLK = 512
EPS = 1e-5
LR, B1, B2, AEPS, WD, STEP = 0.001, 0.9, 0.999, 1e-08, 0.01, 10
POOL_HALO = 16
CONV_HALO = 8
VMEM_LIMIT = 56 << 20
ADAM_BLOCK_BYTES = 1 << 20
DW_IN_WIN = 512


def _cparams(**kw):
    return pltpu.CompilerParams(vmem_limit_bytes=VMEM_LIMIT, **kw)


def _mm(a, b):
    return jnp.dot(a.astype(BF16), b.astype(BF16), preferred_element_type=F32)


def _mm_nt(a, b):
    return lax.dot_general(a.astype(BF16), b.astype(BF16), (((1,), (1,)), ((), ())), preferred_element_type=F32)


def _mm_tn(a, b):
    return lax.dot_general(a.astype(BF16), b.astype(BF16), (((0,), (0,)), ((), ())), preferred_element_type=F32)


def _split_bf16(v, terms):
    parts, rest = [], v
    for t in range(terms):
        p = rest.astype(BF16)
        parts.append(p)
        if t + 1 < terms:
            rest = rest - p.astype(F32)
    return parts


def _dot01(a, b, terms, split_lhs=True):
    if split_lhs:
        bb = b.astype(BF16)
        prods = [jnp.dot(p, bb, preferred_element_type=F32) for p in _split_bf16(a, terms)]
    else:
        ab = a.astype(BF16)
        prods = [jnp.dot(ab, p, preferred_element_type=F32) for p in _split_bf16(b, terms)]
    out = prods[0]
    for q in prods[1:]:
        out = out + q
    return out


def _sigmoid(v):
    return 1.0 / (1.0 + jnp.exp(-v))


def _expand_mat():
    r = lax.broadcasted_iota(jnp.int32, (LANES, D), 0)
    c = lax.broadcasted_iota(jnp.int32, (LANES, D), 1)
    return (r == c // HEAD_DIM).astype(F32)


def _reduce_mat():
    r = lax.broadcasted_iota(jnp.int32, (D, LANES), 0)
    c = lax.broadcasted_iota(jnp.int32, (D, LANES), 1)
    return (c == r // HEAD_DIM).astype(F32)


def _pos():
    return lax.axis_index("x"), lax.axis_index("y"), lax.axis_index("c")


GATHER_PIECES = 4
GATHER_PIECE_BYTES = 96 << 10


def _pieces(shape, dtype):
    rows = shape[0]
    size = jnp.dtype(dtype).itemsize
    for d in shape:
        size *= d
    whole_tiles = rows % (GATHER_PIECES * 16) == 0
    return GATHER_PIECES if whole_tiles and size // GATHER_PIECES >= GATHER_PIECE_BYTES else 1


class _Gather:
    def __init__(self, x_refs, o_refs, send, recv, loc):
        self.x_refs, self.o_refs, self.send, self.recv, self.loc = x_refs, o_refs, send, recv, loc
        self.n = len(x_refs)
        self.pieces = [_pieces(r.shape, r.dtype) for r in x_refs]
        self.base = [7 * sum(self.pieces[:a]) for a in range(self.n)]
        x, y, c = _pos()
        self.c = c
        self.me, self.sib = (x, y, c), (x, y, 1 - c)
        self.chips = [(1 - x, y), (x, 1 - y), (1 - x, 1 - y)]

    def _rows(self, a, p):
        rows = self.x_refs[a].shape[0] // self.pieces[a]
        return pl.ds(p * rows, rows)

    def _cp(self, a, p, k, block, to, own=False):
        dst = self.o_refs[a].at[4 * block[0] + 2 * block[1] + block[2], self._rows(a, p)]
        sem = self.base[a] + 7 * p + k
        return pltpu.make_async_remote_copy(
            src_ref=self.x_refs[a].at[self._rows(a, p)] if own else dst, dst_ref=dst,
            send_sem=self.send.at[sem], recv_sem=self.recv.at[sem], device_id=to, device_id_type=MESH)

    def _mine(self, a):
        me = self.me
        return pltpu.make_async_copy(self.x_refs[a], self.o_refs[a].at[4 * me[0] + 2 * me[1] + me[2]], self.loc.at[a])

    def _first(self, a, p):
        cps = [self._cp(a, p, 0, self.me, self.sib, own=True)]
        return cps + [self._cp(a, p, 1 + j, self.me, (*chip, self.c), own=True) for j, chip in enumerate(self.chips)]

    def _passed(self, a, p, j):
        return self._cp(a, p, 4 + j, (*self.chips[j], self.c), self.sib)

    def start(self):
        for a in range(self.n):
            self._mine(a).start()
        for p in range(max(self.pieces)):
            for a in range(self.n):
                if p < self.pieces[a]:
                    for cp in self._first(a, p):
                        cp.start()

    def forward(self, p):
        for j, chip in enumerate(self.chips):
            for a in range(self.n):
                if p < self.pieces[a]:
                    self._cp(a, p, 1 + j, (*chip, self.c), self.me).wait_recv()
                    self._passed(a, p, j).start()

    def finish(self):
        for a in range(self.n):
            for p in range(self.pieces[a]):
                self._cp(a, p, 0, self.sib, self.me).wait_recv()
                for j, chip in enumerate(self.chips):
                    self._cp(a, p, 4 + j, (*chip, 1 - self.c), self.me).wait_recv()
        for a in range(self.n):
            for p in range(self.pieces[a]):
                for cp in self._first(a, p):
                    cp.wait_send()
                for j in range(3):
                    self._passed(a, p, j).wait_send()
            self._mine(a).wait()

    def begin_hosted(self, step, steps):
        @pl.when(step == 0)
        def _():
            self.start()

        n_p = max(self.pieces)
        for p in range(n_p):
            @pl.when(step == min(((p + 1) * 7 * steps) // (8 * n_p), steps - 1))
            def _():
                self.forward(p)

    def end_hosted(self, step, steps):
        @pl.when(step == steps - 1)
        def _():
            self.finish()


class _Exchange:
    def __init__(self, x_refs, o_refs, send, recv, loc):
        self.x_refs, self.o_refs, self.send, self.recv, self.loc = x_refs, o_refs, send, recv, loc
        self.n = len(x_refs)
        x, y, c = _pos()
        self.me_i = 4 * x + 2 * y + c
        self.peers = []
        for k in range(1, N_DEV):
            px = 1 - x if (k >> 2) & 1 else x
            py = 1 - y if (k >> 1) & 1 else y
            pc = 1 - c if k & 1 else c
            self.peers.append(((px, py, pc), 4 * px + 2 * py + pc))

    def _mine(self, a):
        return pltpu.make_async_copy(self.x_refs[a].at[self.me_i], self.o_refs[a].at[self.me_i], self.loc.at[a])

    def _cp(self, a, k, landing):
        peer, peer_i = self.peers[k]
        return pltpu.make_async_remote_copy(
            src_ref=self.x_refs[a].at[peer_i], dst_ref=self.o_refs[a].at[landing],
            send_sem=self.send.at[a * 7 + k], recv_sem=self.recv.at[a * 7 + k],
            device_id=peer, device_id_type=MESH)

    def start(self):
        for a in range(self.n):
            self._mine(a).start()
            for k in range(N_DEV - 1):
                self._cp(a, k, self.me_i).start()

    def finish(self):
        for a in range(self.n):
            for k in range(N_DEV - 1):
                self._cp(a, k, self.peers[k][1]).wait_recv()
        for a in range(self.n):
            for k in range(N_DEV - 1):
                self._cp(a, k, self.me_i).wait_send()
            self._mine(a).wait()


def _gather_scratch(xs):
    n_sem = 7 * sum(_pieces(v.shape, v.dtype) for v in xs)
    return [pltpu.SemaphoreType.DMA((n_sem,)), pltpu.SemaphoreType.DMA((n_sem,)), pltpu.SemaphoreType.DMA((len(xs),))]


ANY_SPEC = pl.BlockSpec(memory_space=pl.ANY)


def _all_gather(xs, name, after=()):
    n, na = len(xs), len(after)

    def body(*refs):
        g = _Gather(refs[:n], refs[n + na:2 * n + na], *refs[2 * n + na:])
        g.start()
        for p in range(max(g.pieces)):
            g.forward(p)
        g.finish()

    return pl.pallas_call(
        body, name=name,
        out_shape=[jax.ShapeDtypeStruct((N_DEV,) + v.shape, v.dtype) for v in xs],
        in_specs=[ANY_SPEC] * (n + na), out_specs=[ANY_SPEC] * n, scratch_shapes=_gather_scratch(xs),
    )(*xs, *after)


HBM_SPEC = pl.BlockSpec(memory_space=pltpu.HBM)
SEM_SPEC = pl.BlockSpec(memory_space=pltpu.SEMAPHORE)
VMEM_SPEC = pl.BlockSpec(memory_space=pltpu.VMEM)
SPLIT_EFFECT = pltpu.SideEffectType.DATAFLOW_SIDE_EFFECTING


def _in_hbm(v):
    return pltpu.with_memory_space_constraint(v, pltpu.HBM)


def _exchange_start(blocks, name):
    n = len(blocks)

    def body(*refs):
        x_refs, land_refs = refs[:n], refs[n:2 * n]
        send, recv = refs[2 * n:2 * n + 2]
        token = refs[-1]
        ex = _Exchange(x_refs, land_refs, send, recv, None)
        for a in range(n):
            for k in range(N_DEV - 1):
                ex._cp(a, k, ex.me_i).start()
        token[...] = jnp.zeros_like(token)

    lands = [lax.empty(v.shape, v.dtype) for v in blocks]
    hbm = tuple(pltpu.HBM(v.shape, v.dtype) for v in list(blocks) + lands)
    n_sem = (N_DEV - 1) * n
    out = pl.pallas_call(
        body, name=name,
        out_shape=(pltpu.SemaphoreType.DMA((n_sem,)), pltpu.SemaphoreType.DMA((n_sem,))) + hbm
        + (jax.ShapeDtypeStruct((8, LANES), F32),),
        in_specs=(HBM_SPEC,) * (2 * n), out_specs=(SEM_SPEC, SEM_SPEC) + (HBM_SPEC,) * (2 * n) + (VMEM_SPEC,),
        input_output_aliases={i: i + 2 for i in range(2 * n)},
        compiler_params=pltpu.CompilerParams(has_side_effects=SPLIT_EFFECT),
    )(*[_in_hbm(v) for v in list(blocks) + lands])
    return out[0], out[1], list(out[2:2 + n]), list(out[2 + n:2 + 2 * n]), out[-1]


def _exchange_wait(ex, after, name, me):
    send, recv, thru, lands, _ = ex
    n = len(thru)

    def body(*refs):
        x_refs, land_refs = refs[:n], refs[n:2 * n]
        send_ref, recv_ref = refs[2 * n:2 * n + 2]
        e = _Exchange(x_refs, land_refs, send_ref, recv_ref, None)
        for a in range(n):
            for k in range(N_DEV - 1):
                e._cp(a, k, e.me_i).wait_send()
                e._cp(a, k, e.peers[k][1]).wait_recv()

    hbm = tuple(pltpu.HBM(v.shape, v.dtype) for v in list(thru) + list(lands))
    out = pl.pallas_call(
        body, name=name, out_shape=hbm,
        in_specs=(HBM_SPEC,) * (2 * n) + (SEM_SPEC, SEM_SPEC, ANY_SPEC), out_specs=(HBM_SPEC,) * (2 * n),
        input_output_aliases={i: i for i in range(2 * n)},
        compiler_params=pltpu.CompilerParams(has_side_effects=SPLIT_EFFECT),
    )(*thru, *lands, send, recv, after)
    done = []
    for own, land in zip(out[:n], out[n:]):
        mine = lax.dynamic_slice(own, (me,) + (0,) * (own.ndim - 1), (1,) + own.shape[1:])
        done.append(lax.dynamic_update_slice(land, mine, (me,) + (0,) * (own.ndim - 1)))
    return done


def _ada_fwd(c_all, w_ada, b_slice):
    def body(c_ref, w_ref, b_ref, o_ref):
        cv = c_ref[...]
        act = cv * _sigmoid(cv)
        o_ref[...] = _mm(act, w_ref[...]) + b_ref[...]

    nb, nc = c_all.shape[0], w_ada.shape[1]
    return pl.pallas_call(body, name="ada_fwd", out_shape=jax.ShapeDtypeStruct((nb, nc), F32),
                          compiler_params=_cparams())(c_all, w_ada, b_slice)


def _adam_math(w, g, m, v):
    m = B1 * m + (1.0 - B1) * g
    v = B2 * v + (1.0 - B2) * jnp.square(g)
    m_hat = m / (1.0 - B1 ** STEP)
    v_hat = v / (1.0 - B2 ** STEP)
    delta = -LR * (m_hat / (jnp.sqrt(v_hat) + AEPS) + WD * w)
    return delta, m, v


def _ada_bwd_adam(c_all, dmod_slice, w, m, v):
    rows, cols = w.shape
    br = 256

    def body(c_ref, d_ref, w_ref, m_ref, v_ref, g_out, dl_out, m_out, v_out):
        cv = c_ref[...]
        act = cv * _sigmoid(cv)
        g = _mm_tn(act, d_ref[...])
        g_out[...] = g
        dl, mn, vn = _adam_math(w_ref[...], g, m_ref[...], v_ref[...])
        dl_out[...] = dl
        m_out[...] = mn
        v_out[...] = vn

    nb = c_all.shape[0]
    wspec = pl.BlockSpec((br, cols), lambda i: (i, 0))
    return pl.pallas_call(
        body, name="ada_bwd_adam", grid=(rows // br,),
        in_specs=[pl.BlockSpec((nb, br), lambda i: (0, i)), pl.BlockSpec((nb, cols), lambda i: (0, 0)),
                  wspec, wspec, wspec],
        out_specs=[wspec] * 4, out_shape=[jax.ShapeDtypeStruct((rows, cols), F32)] * 4,
        compiler_params=_cparams(),
    )(c_all, dmod_slice, w, m, v)


def _mix_in(x2, mod, g_mix, win_g, in_cols, seq, shards):
    T = x2.shape[0]
    tm = min(512, seq)
    tps = seq // tm
    blk_rows = win_g.shape[1]
    pad_rows = -blk_rows % (4 * LANES)
    ns = len(shards)
    steps = T // tm

    def body(*refs):
        x_ref, mod_ref, g_ref, wb_ref = refs[:4]
        sh_refs = refs[4:4 + ns]
        u_ref, pm_ref, wc_ref = refs[4 + ns:7 + ns]
        ga_refs = refs[7 + ns:7 + 2 * ns]
        w_ref, send, recv, loc = refs[7 + 2 * ns:]
        step = pl.program_id(0)
        gather = _Gather(sh_refs, ga_refs, send, recv, loc)
        gather.begin_hosted(step, steps)

        @pl.when(step == 0)
        def _():
            w_ref[:, OFF_DT:] = jnp.zeros((D, PROJ_W - OFF_DT), BF16)
            for j in range(N_DEV):
                blk = jnp.concatenate([wb_ref[j], jnp.zeros((pad_rows, D), BF16)], axis=0)
                w_ref[:, in_cols * j:in_cols * (j + 1)] = blk.T[:, :in_cols]
            wc_ref[...] = w_ref[...]

        x = x_ref[...]
        r = lax.rsqrt(jnp.mean(x * x, axis=-1, keepdims=True) + EPS)
        md = mod_ref[0]
        u = (x * r * g_ref[...]) * (1.0 + md[1:2]) + md[0:1]
        ub = u.astype(BF16)
        u_ref[...] = ub
        pm_ref[...] = jnp.dot(ub, w_ref[...], preferred_element_type=F32)
        gather.end_hosted(step, steps)

    whole = pl.BlockSpec(memory_space=pltpu.VMEM)
    return pl.pallas_call(
        body, name="mix_in", grid=(T // tm,),
        in_specs=[pl.BlockSpec((tm, D), lambda i: (i, 0)), pl.BlockSpec((1, 8, D), lambda i: (i // tps, 0, 0)),
                  pl.BlockSpec((1, D), lambda i: (0, 0)), whole] + [ANY_SPEC] * ns,
        out_specs=[pl.BlockSpec((tm, D), lambda i: (i, 0)), pl.BlockSpec((tm, PROJ_W), lambda i: (i, 0)),
                   pl.BlockSpec((D, PROJ_W), lambda i: (0, 0))] + [ANY_SPEC] * ns,
        out_shape=[jax.ShapeDtypeStruct((T, D), BF16), jax.ShapeDtypeStruct((T, PROJ_W), F32),
                   jax.ShapeDtypeStruct((D, PROJ_W), BF16)]
        + [jax.ShapeDtypeStruct((N_DEV,) + v.shape, v.dtype) for v in shards],
        scratch_shapes=[pltpu.VMEM((D, PROJ_W), BF16)] + _gather_scratch(shards),
        compiler_params=_cparams(),
    )(x2, mod, g_mix, win_g, *shards)


def _chunk_forward(up, z, ux, dtin, halo_p, halo_x, hprev, cw, cb, hp, gssd, wpool, pscale, t0, y_scr, cv=None):
    L = CHUNK
    out = {}
    row = lax.broadcasted_iota(jnp.int32, (L, 1), 0)
    t = (t0 + row + 1).astype(F32)
    e = jnp.concatenate([halo_p, up], axis=0)
    s2 = e + pltpu.roll(e, 1, 0)
    s4 = s2 + pltpu.roll(s2, 2, 0)
    s8 = s4 + pltpu.roll(s4, 4, 0)
    s16 = s8 + pltpu.roll(s8, 8, 0)
    sums = (s2, s4, s8, s16)
    p, inv, yp = [], [], []
    for gi, w in enumerate(WINDOWS):
        sl = slice(gi * LANES, (gi + 1) * LANES)
        ic = 1.0 / jnp.minimum(t, float(w))
        pg = sums[gi][POOL_HALO:, sl] * ic - up[:, sl]
        p.append(pg)
        inv.append(ic)
        yp.append(_mm(pg, wpool[gi]))
    out["p"], out["inv"], out["yp"] = p, inv, yp
    out["y_pool"] = jnp.concatenate(yp, axis=1) * pscale
    if cv is None:
        ex = jnp.concatenate([halo_x, ux], axis=0)
        taps = [pltpu.roll(ex, 3, 0)[CONV_HALO:], pltpu.roll(ex, 2, 0)[CONV_HALO:], pltpu.roll(ex, 1, 0)[CONV_HALO:], ux]
        cv = cb + taps[0] * cw[0:1] + taps[1] * cw[1:2] + taps[2] * cw[2:3] + taps[3] * cw[3:4]
    sg = _sigmoid(cv)
    xbc = cv * sg
    out["cv"], out["sg"] = cv, sg
    X = xbc[:, :D]
    Bm = xbc[:, D:D + N_GROUPS * N_STATE]
    Cm = xbc[:, D + N_GROUPS * N_STATE:]
    pre = dtin + hp[0:1]
    dt = jnp.maximum(pre, 0.0) + jnp.log(1.0 + jnp.exp(-jnp.abs(pre)))
    a_row = -jnp.exp(hp[1:2])
    da = dt * a_row
    ri = lax.broadcasted_iota(jnp.int32, (L, L), 0)
    ci = lax.broadcasted_iota(jnp.int32, (L, L), 1)
    causal = ri >= ci
    cum = _dot01(causal.astype(F32), da, 3, split_lhs=False)
    cum_t = cum.T
    cum_last = cum[L - 1:L]
    eo = jnp.exp(cum)
    dec = jnp.exp(cum_last - cum)
    cd = jnp.exp(cum_last)
    exm = _expand_mat()
    rows8 = jnp.concatenate([cd, hp[2:3], jnp.zeros((6, LANES), F32)], axis=0)
    rep = _dot01(jnp.concatenate([dt, eo, dec, rows8], axis=0), exm, 2)
    dt_rep, eo_rep, dec_rep = rep[0:L], rep[L:2 * L], rep[2 * L:3 * L]
    cd_rep, dskip_rep = rep[3 * L:3 * L + 1], rep[3 * L + 1:3 * L + 2]
    xdt = X * dt_rep
    out.update(X=X, Bm=Bm, Cm=Cm, pre=pre, dt=dt, a_row=a_row, cum=cum, cum_t=cum_t, eo=eo, dec=dec, cd=cd,
               dt_rep=dt_rep, eo_rep=eo_rep, dec_rep=dec_rep, cd_rep=cd_rep, dskip_rep=dskip_rep, xdt=xdt,
               causal=causal, anti=(ri <= ci).astype(F32), exm=exm)
    G, lms, yoff, hnew, xdec = [], [], [], [], []
    for g in range(N_GROUPS):
        gs = slice(g * GROUP_W, (g + 1) * GROUP_W)
        Bg = Bm[:, g * N_STATE:(g + 1) * N_STATE]
        Cg = Cm[:, g * N_STATE:(g + 1) * N_STATE]
        Gg = _mm_nt(Cg, Bg)
        G.append(Gg)
        for hh in range(N_HEADS // N_GROUPS):
            h = g * (N_HEADS // N_GROUPS) + hh
            seg = cum[:, h:h + 1] - cum_t[h:h + 1, :]
            lm = jnp.where(causal, jnp.exp(jnp.minimum(seg, 0.0)), 0.0)
            lms.append(lm)
            hs = slice(h * HEAD_DIM, (h + 1) * HEAD_DIM)
            y_scr[:, hs] = _mm(Gg * lm, xdt[:, hs])
        xd = xdt[:, gs] * dec_rep[:, gs]
        xdec.append(xd)
        sgm = _mm_tn(Bg, xd)
        yoff.append(_mm(Cg, hprev[g]) * eo_rep[:, gs])
        hnew.append(hprev[g] * cd_rep[:, gs] + sgm)
    out.update(G=G, lms=lms, yoff=yoff, hnew=hnew, xdec=xdec)
    y = y_scr[...] + jnp.concatenate(yoff, axis=1) + dskip_rep * X
    sz = _sigmoid(z)
    silz = z * sz
    yz = y * silz
    rg, yn = [], []
    for g in range(N_GROUPS):
        gs = slice(g * GROUP_W, (g + 1) * GROUP_W)
        r = lax.rsqrt(jnp.mean(yz[:, gs] * yz[:, gs], axis=-1, keepdims=True) + EPS)
        rg.append(r)
        yn.append(yz[:, gs] * r)
    yn = jnp.concatenate(yn, axis=1)
    out.update(y=y, sz=sz, silz=silz, rg=rg, yn=yn)
    out["y_ssd"] = yn * gssd
    return out


def _mixer_fwd(pm, cw, cb, hp, gssd, wpool, pscale, nb, seq, shards):
    nc = seq // CHUNK
    ns = len(shards)
    steps = nb * nc

    def body(*refs):
        pm_ref, cw_ref, cb_ref, hp_ref, gs_ref, wp_ref, ps_ref = refs[:7]
        sh_refs = refs[7:7 + ns]
        ym_ref, hs_ref, cv_ref = refs[7 + ns:10 + ns]
        ga_refs = refs[10 + ns:10 + 2 * ns]
        halo_p, halo_x, state, y_scr, send, recv, loc = refs[10 + 2 * ns:]
        c = pl.program_id(1)
        step = pl.program_id(0) * nc + c
        gather = _Gather(sh_refs, ga_refs, send, recv, loc)
        gather.begin_hosted(step, steps)

        @pl.when(c == 0)
        def _():
            halo_p[...] = jnp.zeros_like(halo_p)
            halo_x[...] = jnp.zeros_like(halo_x)
            state[...] = jnp.zeros_like(state)

        up = pm_ref[:, 0:POOL_W]
        z = pm_ref[:, OFF_Z:OFF_XBC]
        ux = pm_ref[:, OFF_XBC:OFF_DT]
        hprev = [state[0], state[1]]
        hs_ref[0, 0, 0] = hprev[0]
        hs_ref[0, 0, 1] = hprev[1]
        o = _chunk_forward(up, z, ux, pm_ref[:, OFF_DT:], halo_p[...], halo_x[...], hprev, cw_ref[...], cb_ref[...],
                           hp_ref[...], gs_ref[...], wp_ref[...], ps_ref[...], c * CHUNK, y_scr)
        ym_ref[:, 0:POOL_W] = o["y_pool"].astype(BF16)
        ym_ref[:, POOL_W:] = o["y_ssd"].astype(BF16)
        cv_ref[...] = o["cv"]
        state[0] = o["hnew"][0]
        state[1] = o["hnew"][1]
        halo_p[...] = up[CHUNK - POOL_HALO:]
        halo_x[...] = ux[CHUNK - CONV_HALO:]
        gather.end_hosted(step, steps)

    def full(shape):
        return pl.BlockSpec(shape, lambda b, c: (0,) * len(shape))

    T = nb * seq
    return pl.pallas_call(
        body, name="mixer_fwd", grid=(nb, nc),
        in_specs=[pl.BlockSpec((CHUNK, PROJ_W), lambda b, c: (b * nc + c, 0)),
                  full((4, CONV_CH)), full((1, CONV_CH)), full((8, LANES)), full((1, D)),
                  full((4, LANES, LANES)), full((1, POOL_W))] + [ANY_SPEC] * ns,
        out_specs=[pl.BlockSpec((CHUNK, MIX_W), lambda b, c: (b * nc + c, 0)),
                   pl.BlockSpec((1, 1, N_GROUPS, N_STATE, GROUP_W), lambda b, c: (b, c, 0, 0, 0)),
                   pl.BlockSpec((CHUNK, CONV_CH), lambda b, c: (b * nc + c, 0))] + [ANY_SPEC] * ns,
        out_shape=[jax.ShapeDtypeStruct((T, MIX_W), BF16),
                   jax.ShapeDtypeStruct((nb, nc, N_GROUPS, N_STATE, GROUP_W), F32),
                   jax.ShapeDtypeStruct((T, CONV_CH), F32)]
        + [jax.ShapeDtypeStruct((N_DEV,) + v.shape, v.dtype) for v in shards],
        scratch_shapes=[pltpu.VMEM((POOL_HALO, POOL_W), F32), pltpu.VMEM((CONV_HALO, CONV_CH), F32),
                        pltpu.VMEM((N_GROUPS, N_STATE, GROUP_W), F32), pltpu.VMEM((CHUNK, D), F32)] + _gather_scratch(shards),
        compiler_params=_cparams(),
    )(pm, cw, cb, hp, gssd, wpool, pscale, *shards)


def _mixer_bwd(pm, cvs, dym, hstates, cw, cb, hp, gssd, wpool, pscale, nb, seq, after=()):
    nc = seq // CHUNK
    hpg = N_HEADS // N_GROUPS
    na = len(after)

    def body(*refs):
        (pm_ref, hpool_ref, cv_ref, dy_ref, hs_ref, cw_ref, cb_ref, hp_ref, gs_ref, wp_ref, ps_ref) = refs[:11]
        dpm_ref, dconv_ref, dhp_ref, dvec_ref, dwp_ref = refs[11 + na:16 + na]
        nxt_q, nxt_cv, rstate, y_scr, dx_scr = refs[16 + na:]
        b = pl.program_id(0)
        ci = pl.program_id(1)
        c = nc - 1 - ci

        @pl.when((b == 0) & (ci == 0))
        def _():
            for r in (dconv_ref, dhp_ref, dvec_ref, dwp_ref):
                r[...] = jnp.zeros_like(r)

        @pl.when(ci == 0)
        def _():
            nxt_q[...] = jnp.zeros_like(nxt_q)
            nxt_cv[...] = jnp.zeros_like(nxt_cv)
            rstate[...] = jnp.zeros_like(rstate)

        first = (c > 0).astype(F32)
        up = pm_ref[:, 0:POOL_W]
        z = pm_ref[:, OFF_Z:OFF_XBC]
        ux = pm_ref[:, OFF_XBC:OFF_DT]
        halo_p = hpool_ref[...] * first
        hprev = [hs_ref[0, 0, 0], hs_ref[0, 0, 1]]
        cw, cb, hp, gssd, wpool, pscale = cw_ref[...], cb_ref[...], hp_ref[...], gs_ref[...], wp_ref[...], ps_ref[...]
        o = _chunk_forward(up, z, ux, pm_ref[:, OFF_DT:], halo_p, None, hprev, cw, cb, hp, gssd, wpool, pscale,
                           c * CHUNK, y_scr, cv=cv_ref[...])
        L = CHUNK
        dy_pool = dy_ref[:, 0:POOL_W].astype(F32)
        dy_ssd = dy_ref[:, POOL_W:].astype(F32)

        dvec_ref[1:2, 0:POOL_W] += jnp.sum(dy_pool * jnp.concatenate(o["yp"], axis=1), axis=0, keepdims=True)
        dyp = dy_pool * pscale
        qs = []
        dps = []
        for gi in range(len(WINDOWS)):
            sl = slice(gi * LANES, (gi + 1) * LANES)
            dwp_ref[gi] += _mm_tn(o["p"][gi], dyp[:, sl])
            dpg = _mm_nt(dyp[:, sl], wpool[gi])
            dps.append(dpg)
            qs.append(dpg * o["inv"][gi])
        q = jnp.concatenate(qs, axis=1)
        e = jnp.concatenate([q, nxt_q[...]], axis=0)
        n = L + POOL_HALO
        s2 = e + pltpu.roll(e, n - 1, 0)
        s4 = s2 + pltpu.roll(s2, n - 2, 0)
        s8 = s4 + pltpu.roll(s4, n - 4, 0)
        s16 = s8 + pltpu.roll(s8, n - 8, 0)
        sums = (s2, s4, s8, s16)
        for gi in range(len(WINDOWS)):
            sl = slice(gi * LANES, (gi + 1) * LANES)
            dpm_ref[:, sl] = (sums[gi][:L, sl] - dps[gi]).astype(BF16)
        nxt_q[...] = q[:POOL_HALO]

        yn, y, silz, sz = o["yn"], o["y"], o["silz"], o["sz"]
        dvec_ref[0:1] += jnp.sum(dy_ssd * yn, axis=0, keepdims=True)
        dyn = dy_ssd * gssd
        dyz = []
        for g in range(N_GROUPS):
            gs = slice(g * GROUP_W, (g + 1) * GROUP_W)
            mean = jnp.mean(dyn[:, gs] * yn[:, gs], axis=-1, keepdims=True)
            dyz.append(o["rg"][g] * (dyn[:, gs] - yn[:, gs] * mean))
        dyz = jnp.concatenate(dyz, axis=1)
        dyv = dyz * silz
        dpm_ref[:, OFF_Z:OFF_XBC] = (dyz * y * (sz * (1.0 + z * (1.0 - sz)))).astype(BF16)

        X, Bm, Cm, xdt = o["X"], o["Bm"], o["Cm"], o["xdt"]
        exm = o["exm"]
        rdm = _reduce_mat()
        lane = lax.broadcasted_iota(jnp.int32, (1, LANES), 1)
        sub = lax.broadcasted_iota(jnp.int32, (LANES, 1), 0)
        dX = o["dskip_rep"] * dyv
        yoff_full = jnp.concatenate(o["yoff"], axis=1)
        rs = jnp.zeros((L, LANES), F32)
        cs_t = jnp.zeros((LANES, L), F32)
        dBs, dCs = [], []
        rh_sums = []
        ddec = []
        for g in range(N_GROUPS):
            gs = slice(g * GROUP_W, (g + 1) * GROUP_W)
            Bg = Bm[:, g * N_STATE:(g + 1) * N_STATE]
            Cg = Cm[:, g * N_STATE:(g + 1) * N_STATE]
            Gg = o["G"][g]
            R = rstate[g]
            dwm = dyv[:, gs] * o["eo_rep"][:, gs]
            dC = _mm_nt(dwm, hprev[g])
            dH = _mm_tn(Cg, dwm)
            dG = jnp.zeros((L, L), F32)
            for hh in range(hpg):
                h = g * hpg + hh
                hs = slice(h * HEAD_DIM, (h + 1) * HEAD_DIM)
                lm = o["lms"][h]
                m_h = Gg * lm
                dM = _mm_nt(dyv[:, hs], xdt[:, hs])
                dx_scr[:, hs] = _mm_tn(m_h, dyv[:, hs])
                qm = dM * m_h
                rs = rs + jnp.sum(qm, axis=1, keepdims=True) * (lane == h).astype(F32)
                cs_t = cs_t + (sub == h).astype(F32) * jnp.sum(qm, axis=0, keepdims=True)
                dG = dG + dM * lm
            dC = dC + _mm(dG, Bg)
            dB = _mm_tn(dG, Cg)
            zx = _mm(Bg, R)
            dxdt_state = zx * o["dec_rep"][:, gs]
            ddec.append(zx * xdt[:, gs])
            dB = dB + _mm_nt(o["xdec"][g], R)
            rh_sums.append(jnp.sum(R * hprev[g], axis=0, keepdims=True))
            rstate[g] = dH + o["cd_rep"][:, gs] * R
            dx_scr[:, gs] = dx_scr[:, gs] + dxdt_state
            dBs.append(dB)
            dCs.append(dC)
        dxdt = dx_scr[...]
        tail = jnp.concatenate([jnp.sum(dyv * X, axis=0, keepdims=True), jnp.concatenate(rh_sums, axis=1),
                                jnp.zeros((6, D), F32)], axis=0)
        red = _dot01(jnp.concatenate([dyv * yoff_full, jnp.concatenate(ddec, axis=1), dxdt * X, tail], axis=0), rdm, 2)
        d_dskip, dcd_row = red[3 * L:3 * L + 1], red[3 * L + 1:3 * L + 2]
        ddec_h = red[L:2 * L] * o["dec"]
        dcum_last = jnp.sum(ddec_h, axis=0, keepdims=True) + dcd_row * o["cd"]
        dcum = red[0:L] + rs - cs_t.T - ddec_h + (sub == L - 1).astype(F32) * dcum_last
        dda = _dot01(o["anti"], dcum, 3, split_lhs=False)
        ddt_v = dda * o["a_row"] + red[2 * L:3 * L]
        dX = dX + dxdt * o["dt_rep"]
        head_mask = (lane < N_HEADS).astype(F32)
        d_alog = jnp.sum(dda * o["dt"], axis=0, keepdims=True) * o["a_row"] * head_mask
        dpre = ddt_v * _sigmoid(o["pre"]) * head_mask
        dpm_ref[:, OFF_DT:] = dpre.astype(BF16)
        d_dtb = jnp.sum(dpre, axis=0, keepdims=True)
        dhp_ref[...] += jnp.concatenate([d_dtb, d_alog, d_dskip * head_mask, jnp.zeros((5, LANES), F32)], axis=0)

        dxbc = jnp.concatenate([dX] + dBs + dCs, axis=1)
        sg, cv = o["sg"], o["cv"]
        dcv = dxbc * (sg * (1.0 + cv * (1.0 - sg)))
        e2 = jnp.concatenate([dcv, nxt_cv[...]], axis=0)
        n2 = L + CONV_HALO
        ahead = [dcv, pltpu.roll(e2, n2 - 1, 0)[:L], pltpu.roll(e2, n2 - 2, 0)[:L], pltpu.roll(e2, n2 - 3, 0)[:L]]
        dconv_ref[0:5] += jnp.concatenate(
            [jnp.sum(ux * ahead[3 - k], axis=0, keepdims=True) for k in range(4)]
            + [jnp.sum(dcv, axis=0, keepdims=True)], axis=0)
        dux = ahead[0] * cw[3:4] + ahead[1] * cw[2:3] + ahead[2] * cw[1:2] + ahead[3] * cw[0:1]
        dpm_ref[:, OFF_XBC:OFF_DT] = dux.astype(BF16)
        nxt_cv[...] = dcv[:CONV_HALO]

    def full(shape):
        return pl.BlockSpec(shape, lambda b, c: (0,) * len(shape))

    def rowblk(b, c):
        return b * nc + (nc - 1 - c)

    hp_blocks = CHUNK // POOL_HALO
    T = nb * seq
    return pl.pallas_call(
        body, name="mixer_bwd", grid=(nb, nc),
        in_specs=[pl.BlockSpec((CHUNK, PROJ_W), lambda b, c: (rowblk(b, c), 0)),
                  pl.BlockSpec((POOL_HALO, POOL_W), lambda b, c: (jnp.maximum(rowblk(b, c) * hp_blocks - 1, 0), 0)),
                  pl.BlockSpec((CHUNK, CONV_CH), lambda b, c: (rowblk(b, c), 0)),
                  pl.BlockSpec((CHUNK, MIX_W), lambda b, c: (rowblk(b, c), 0)),
                  pl.BlockSpec((1, 1, N_GROUPS, N_STATE, GROUP_W), lambda b, c: (b, nc - 1 - c, 0, 0, 0)),
                  full((4, CONV_CH)), full((1, CONV_CH)), full((8, LANES)), full((1, D)),
                  full((4, LANES, LANES)), full((1, POOL_W))] + [ANY_SPEC] * na,
        out_specs=[pl.BlockSpec((CHUNK, PROJ_W), lambda b, c: (rowblk(b, c), 0)),
                   full((8, CONV_CH)), full((8, LANES)), full((8, D)), full((4, LANES, LANES))],
        out_shape=[jax.ShapeDtypeStruct((T, PROJ_W), BF16),
                   jax.ShapeDtypeStruct((8, CONV_CH), F32), jax.ShapeDtypeStruct((8, LANES), F32),
                   jax.ShapeDtypeStruct((8, D), F32), jax.ShapeDtypeStruct((4, LANES, LANES), F32)],
        scratch_shapes=[pltpu.VMEM((POOL_HALO, POOL_W), F32), pltpu.VMEM((CONV_HALO, CONV_CH), F32),
                        pltpu.VMEM((N_GROUPS, N_STATE, GROUP_W), F32), pltpu.VMEM((CHUNK, D), F32),
                        pltpu.VMEM((CHUNK, D), F32)],
        compiler_params=_cparams(),
    )(pm, pm, cvs, dym, hstates, cw, cb, hp, gssd, wpool, pscale, *after)


def _mlp_fused(x2, ymix, target, mod, g_mlp, g_final, w_out, w_up, w_down, seq):
    T = x2.shape[0]
    tm = min(256, seq)
    tps = seq // tm
    nblk = D_FF // FF_BLK

    def body(x_ref, ym_ref, tg_ref, mod_ref, gm_ref, gf_ref, wo_ref, wu_ref, wd_ref,
             da_ref, dym_ref, dh1_ref, u2_ref, f_ref, dup_ref, ddn_ref, dmod_ref, acc_ref, relu_scr):
        i = pl.program_id(0)

        @pl.when(i == 0)
        def _():
            acc_ref[...] = jnp.zeros_like(acc_ref)

        @pl.when(i % tps == 0)
        def _():
            dmod_ref[...] = jnp.zeros_like(dmod_ref)

        md = mod_ref[0]
        gate_m, shift_f, scale_f, gate_f = md[2:3], md[3:4], md[4:5], md[5:6]
        g_mlp, g_fin = gm_ref[...], gf_ref[...]
        a = jnp.dot(ym_ref[...], wo_ref[...], preferred_element_type=F32)
        h1 = x_ref[...] + gate_m * a
        r2 = lax.rsqrt(jnp.mean(h1 * h1, axis=-1, keepdims=True) + EPS)
        n2 = h1 * r2
        u2 = (n2 * g_mlp) * (1.0 + scale_f) + shift_f
        u2b = u2.astype(BF16)
        u2_ref[...] = u2b
        dn = jnp.zeros((tm, D), F32)
        for j in range(nblk):
            js = slice(j * FF_BLK, (j + 1) * FF_BLK)
            upj = jnp.maximum(jnp.dot(u2b, wu_ref[j], preferred_element_type=F32), 0.0)
            relu_scr[:, js] = upj
            fj = (upj * upj).astype(BF16)
            f_ref[:, js] = fj
            dn = dn + jnp.dot(fj, wd_ref[j], preferred_element_type=F32)
        h2 = h1 + gate_f * dn
        r3 = lax.rsqrt(jnp.mean(h2 * h2, axis=-1, keepdims=True) + EPS)
        n3 = h2 * r3
        err = n3 * g_fin - tg_ref[...]
        loss = 0.5 * jnp.sum(jnp.mean(err * err, axis=-1, keepdims=True), axis=0, keepdims=True)
        dout = err * (1.0 / D)
        d_gfin = jnp.sum(dout * n3, axis=0, keepdims=True)
        dn3 = dout * g_fin
        dh2 = r3 * (dn3 - n3 * jnp.mean(dn3 * n3, axis=-1, keepdims=True))
        d_gate_f = jnp.sum(dh2 * dn, axis=0, keepdims=True)
        ddn = (gate_f * dh2).astype(BF16)
        ddn_ref[...] = ddn
        du2 = jnp.zeros((tm, D), F32)
        for j in range(nblk):
            js = slice(j * FF_BLK, (j + 1) * FF_BLK)
            dfj = lax.dot_general(ddn, wd_ref[j], (((1,), (1,)), ((), ())), preferred_element_type=F32)
            dupj = (dfj * (2.0 * relu_scr[:, js])).astype(BF16)
            dup_ref[:, js] = dupj
            du2 = du2 + lax.dot_general(dupj, wu_ref[j], (((1,), (1,)), ((), ())), preferred_element_type=F32)
        d_scale_f = jnp.sum(du2 * (n2 * g_mlp), axis=0, keepdims=True)
        d_shift_f = jnp.sum(du2, axis=0, keepdims=True)
        d_gmlp = jnp.sum(du2 * (1.0 + scale_f) * n2, axis=0, keepdims=True)
        dn2 = du2 * (g_mlp * (1.0 + scale_f))
        dh1 = dh2 + r2 * (dn2 - n2 * jnp.mean(dn2 * n2, axis=-1, keepdims=True))
        dh1_ref[...] = dh1
        d_gate_m = jnp.sum(dh1 * a, axis=0, keepdims=True)
        da = (gate_m * dh1).astype(BF16)
        da_ref[...] = da
        dym_ref[...] = lax.dot_general(da, wo_ref[...], (((1,), (1,)), ((), ())),
                                       preferred_element_type=F32).astype(BF16)
        dmod_ref[0] += jnp.concatenate([jnp.zeros((2, D), F32), d_gate_m, d_shift_f, d_scale_f, d_gate_f,
                                        jnp.zeros((2, D), F32)], axis=0)
        acc_ref[...] += jnp.concatenate([d_gmlp, d_gfin, loss * jnp.ones((1, D), F32), jnp.zeros((5, D), F32)], axis=0)

    whole = pl.BlockSpec(memory_space=pltpu.VMEM)

    def tok(w):
        return pl.BlockSpec((tm, w), lambda i: (i, 0))

    def vec():
        return pl.BlockSpec((1, D), lambda i: (0, 0))

    nb = T // seq
    return pl.pallas_call(
        body, name="mlp_fused", grid=(T // tm,),
        in_specs=[tok(D), tok(MIX_W), tok(D), pl.BlockSpec((1, 8, D), lambda i: (i // tps, 0, 0)), vec(), vec(),
                  whole, whole, whole],
        out_specs=[tok(D), tok(MIX_W), tok(D), tok(D), tok(D_FF), tok(D_FF), tok(D),
                   pl.BlockSpec((1, 8, D), lambda i: (i // tps, 0, 0)), pl.BlockSpec((8, D), lambda i: (0, 0))],
        out_shape=[jax.ShapeDtypeStruct((T, D), BF16), jax.ShapeDtypeStruct((T, MIX_W), BF16),
                   jax.ShapeDtypeStruct((T, D), F32), jax.ShapeDtypeStruct((T, D), BF16),
                   jax.ShapeDtypeStruct((T, D_FF), BF16), jax.ShapeDtypeStruct((T, D_FF), BF16),
                   jax.ShapeDtypeStruct((T, D), BF16), jax.ShapeDtypeStruct((nb, 8, D), F32),
                   jax.ShapeDtypeStruct((8, D), F32)],
        scratch_shapes=[pltpu.VMEM((tm, D_FF), F32)],
        compiler_params=_cparams(),
    )(x2, ymix, target, mod, g_mlp, g_final, w_out, w_up, w_down)


def _in_bwd(x2, dh1, dpb, mod, g_mix, w_cat, dmod_a, acc_a, seq):
    T = x2.shape[0]
    tm = min(1024, seq)
    tps = seq // tm
    steps = T // tm

    def body(x_ref, dh_ref, dpb_ref, mod_ref, g_ref, w_ref, dma_ref, acca_ref, dx_ref, dmod_ref, acc_ref):
        i = pl.program_id(0)

        @pl.when(i == 0)
        def _():
            acc_ref[...] = acca_ref[...]

        @pl.when(i % tps == 0)
        def _():
            dmod_ref[...] = dma_ref[...]

        du = lax.dot_general(dpb_ref[...], w_ref[...], (((1,), (1,)), ((), ())), preferred_element_type=F32)
        x = x_ref[...]
        md = mod_ref[0]
        g = g_ref[...]
        r = lax.rsqrt(jnp.mean(x * x, axis=-1, keepdims=True) + EPS)
        n1 = x * r
        d_scale = jnp.sum(du * (n1 * g), axis=0, keepdims=True)
        d_shift = jnp.sum(du, axis=0, keepdims=True)
        d_g = jnp.sum(du * (1.0 + md[1:2]) * n1, axis=0, keepdims=True)
        dn1 = du * (g * (1.0 + md[1:2]))
        dx_ref[...] = dh_ref[...] + r * (dn1 - n1 * jnp.mean(dn1 * n1, axis=-1, keepdims=True))
        dmod_ref[0] += jnp.concatenate([d_shift, d_scale, jnp.zeros((6, D), F32)], axis=0)
        acc_ref[...] += jnp.concatenate([jnp.zeros((3, D), F32), d_g, jnp.zeros((4, D), F32)], axis=0)

    whole = pl.BlockSpec(memory_space=pltpu.VMEM)
    nb = T // seq
    return pl.pallas_call(
        body, name="in_bwd", grid=(steps,),
        in_specs=[pl.BlockSpec((tm, D), lambda i: (i, 0)), pl.BlockSpec((tm, D), lambda i: (i, 0)),
                  pl.BlockSpec((tm, PROJ_W), lambda i: (i, 0)),
                  pl.BlockSpec((1, 8, D), lambda i: (i // tps, 0, 0)), pl.BlockSpec((1, D), lambda i: (0, 0)),
                  whole, pl.BlockSpec((1, 8, D), lambda i: (i // tps, 0, 0)), pl.BlockSpec((8, D), lambda i: (0, 0))],
        out_specs=[pl.BlockSpec((tm, D), lambda i: (i, 0)),
                   pl.BlockSpec((1, 8, D), lambda i: (i // tps, 0, 0)), pl.BlockSpec((8, D), lambda i: (0, 0))],
        out_shape=[jax.ShapeDtypeStruct((T, D), F32),
                   jax.ShapeDtypeStruct((nb, 8, D), F32), jax.ShapeDtypeStruct((8, D), F32)],
        compiler_params=_cparams(),
    )(x2, dh1, dpb, mod, g_mix, w_cat, dmod_a, acc_a)


def _dw_in(u_b, dpb, in_cols, shards):
    T = u_b.shape[0]
    bk = min(512, T)
    nk = T // bk
    ns = len(shards)
    starts = [(in_cols * j // LANES) * LANES for j in range(N_DEV)]
    assert all(s + DW_IN_WIN <= PROJ_W and in_cols * (j + 1) <= s + DW_IN_WIN for j, s in enumerate(starts))

    def body(*refs):
        u_ref, d_ref = refs[:2]
        sh_refs = refs[2:2 + ns]
        o_ref = refs[2 + ns]
        ga_refs = refs[3 + ns:3 + 2 * ns]
        acc, send, recv, loc = refs[3 + 2 * ns:]
        k = pl.program_id(0)
        gather = _Gather(sh_refs, ga_refs, send, recv, loc)
        gather.begin_hosted(k, nk)

        @pl.when(k == 0)
        def _():
            acc[...] = jnp.zeros_like(acc)

        ut = u_ref[...].T
        for j in range(N_DEV):
            acc[j] += jnp.dot(ut, d_ref[:, starts[j]:starts[j] + DW_IN_WIN], preferred_element_type=F32)

        @pl.when(k == nk - 1)
        def _():
            for j in range(N_DEV):
                off = in_cols * j - starts[j]
                o_ref[j] = acc[j][:, off:off + in_cols].astype(BF16)

        gather.end_hosted(k, nk)

    return pl.pallas_call(
        body, name="dw_in", grid=(nk,),
        in_specs=[pl.BlockSpec((bk, D), lambda k: (k, 0)), pl.BlockSpec((bk, PROJ_W), lambda k: (k, 0))]
        + [ANY_SPEC] * ns,
        out_specs=[pl.BlockSpec((N_DEV, D, in_cols), lambda k: (0, 0, 0))] + [ANY_SPEC] * ns,
        out_shape=[jax.ShapeDtypeStruct((N_DEV, D, in_cols), BF16)]
        + [jax.ShapeDtypeStruct((N_DEV,) + v.shape, v.dtype) for v in shards],
        scratch_shapes=[pltpu.VMEM((N_DEV, D, DW_IN_WIN), F32)] + _gather_scratch(shards),
        compiler_params=_cparams(),
    )(u_b, dpb, *shards)


def _dw_blocks(a, b, name, by_rows, per_step=1, after=()):
    T, M = a.shape
    N = b.shape[1]
    bk = min(2048, T)
    nk = T // bk
    whole = pl.BlockSpec(memory_space=pltpu.VMEM)
    if by_rows:
        rows = M // N_DEV
        am = rows * per_step
        nblk = N_DEV // per_step
        a_spec, b_spec = pl.BlockSpec((bk, am), lambda i, k: (k, i)), whole
        out_blk, acc_shape = (per_step, rows, N), (am, N)
    else:
        cols = N // N_DEV
        nblk = N_DEV
        a_spec, b_spec = whole, pl.BlockSpec((bk, cols), lambda i, k: (k, i))
        out_blk, acc_shape = (1, M, cols), (M, cols)

    def body(a_ref, b_ref, *rest):
        o_ref, acc = rest[len(after):]
        k = pl.program_id(1)

        @pl.when(k == 0)
        def _():
            acc[...] = jnp.zeros_like(acc)

        tok = pl.ds(pl.multiple_of(k * bk, bk), bk)
        a_blk = a_ref[...] if by_rows else a_ref[tok, :]
        b_blk = b_ref[tok, :] if by_rows else b_ref[...]
        acc[...] += lax.dot_general(a_blk, b_blk, (((0,), (0,)), ((), ())), preferred_element_type=F32)

        @pl.when(k == nk - 1)
        def _():
            o_ref[...] = acc[...].reshape(out_blk).astype(BF16)

    return pl.pallas_call(
        body, name=name, grid=(nblk, nk), in_specs=[a_spec, b_spec] + [ANY_SPEC] * len(after),
        out_specs=pl.BlockSpec(out_blk, lambda i, k: (i, 0, 0)),
        out_shape=jax.ShapeDtypeStruct((N_DEV,) + out_blk[1:], BF16),
        scratch_shapes=[pltpu.VMEM(acc_shape, F32)],
        compiler_params=_cparams(),
    )(a, b, *after)


def _adam_parts(parts, w, m, v, name):
    rows, cols = w.shape
    br = rows
    for cand in range(rows, 15, -16):
        if rows % cand == 0 and cand * cols * 4 <= ADAM_BLOCK_BYTES:
            br = cand
            break

    def body(p_ref, w_ref, m_ref, v_ref, g_out, dl_out, m_out, v_out):
        g = p_ref[0].astype(F32)
        for k in range(1, N_DEV):
            g = g + p_ref[k].astype(F32)
        g_out[...] = g
        dl, mn, vn = _adam_math(w_ref[...], g, m_ref[...], v_ref[...])
        dl_out[...] = dl
        m_out[...] = mn
        v_out[...] = vn

    wspec = pl.BlockSpec((br, cols), lambda i: (i, 0))
    return pl.pallas_call(
        body, name=name, grid=(rows // br,),
        in_specs=[pl.BlockSpec((N_DEV, br, cols), lambda i: (0, i, 0)), wspec, wspec, wspec],
        out_specs=[wspec] * 4, out_shape=[jax.ShapeDtypeStruct((rows, cols), F32)] * 4,
        compiler_params=_cparams(),
    )(parts, w, m, v)


def _adam_plain(g, w, m, v, name):
    def body(g_ref, w_ref, m_ref, v_ref, dl_out, m_out, v_out):
        dl, mn, vn = _adam_math(w_ref[...], g_ref[...], m_ref[...], v_ref[...])
        dl_out[...] = dl
        m_out[...] = mn
        v_out[...] = vn

    return pl.pallas_call(body, name=name, out_shape=[jax.ShapeDtypeStruct(w.shape, F32)] * 3,
                          compiler_params=_cparams())(g, w, m, v)


SMALL_PARAMS = ("b_ada", "g_mix", "conv_b", "dt_bias", "a_log", "d_skip", "g_ssd", "pool_scale", "g_mlp", "g_final")


def _small_adam(gathered, params):
    n_par = len(SMALL_PARAMS)
    nb = gathered[0].shape[1]

    def body(*refs):
        dmod_ref, acc_ref, conv_ref, vec_ref, hd_ref = refs[:5]
        par_refs = refs[5:5 + 3 * n_par]
        out_refs = refs[5 + 3 * n_par:5 + 7 * n_par]
        cw_out, acc_out = refs[5 + 7 * n_par:]

        def total(ref):
            t = ref[0]
            for k in range(1, N_DEV):
                t = t + ref[k]
            return t

        dm = total(dmod_ref)
        dmb = dm[0]
        for b in range(1, nb):
            dmb = dmb + dm[b]
        ac, cv, vc, hd = total(acc_ref), total(conv_ref), total(vec_ref), total(hd_ref)
        cw_out[...] = cv[0:4]
        acc_out[...] = ac
        grads = {
            "b_ada": jnp.concatenate([dmb[r:r + 1] for r in range(6)], axis=1), "g_mix": ac[3:4], "conv_b": cv[4:5],
            "dt_bias": hd[0:1, 0:N_HEADS], "a_log": hd[1:2, 0:N_HEADS], "d_skip": hd[2:3, 0:N_HEADS],
            "g_ssd": vc[0:1], "pool_scale": vc[1:2, 0:POOL_W], "g_mlp": ac[0:1], "g_final": ac[1:2],
        }
        for i, name in enumerate(SMALL_PARAMS):
            w_ref, m_ref, v_ref = par_refs[3 * i:3 * i + 3]
            g = grads[name]
            dl, mn, vn = _adam_math(w_ref[...], g, m_ref[...], v_ref[...])
            g_o, d_o, m_o, v_o = out_refs[4 * i:4 * i + 4]
            g_o[...] = g
            d_o[...] = dl
            m_o[...] = mn
            v_o[...] = vn

    flat = [a for name in SMALL_PARAMS for a in params[name]]
    out_shape = [jax.ShapeDtypeStruct(params[name][0].shape, F32) for name in SMALL_PARAMS for _ in range(4)]
    out_shape += [jax.ShapeDtypeStruct((4, CONV_CH), F32), jax.ShapeDtypeStruct((8, D), F32)]
    return pl.pallas_call(body, name="small_adam", out_shape=out_shape, compiler_params=_cparams())(*gathered, *flat)


def kernel(x, c, w_ada, b_ada, g_mix, w_in, conv_w, conv_b, dt_bias, a_log, d_skip, g_ssd, w_pool, pool_scale, w_out, g_mlp, w_up, w_down, g_final, loss_target, m_w_ada, m_b_ada, m_g_mix, m_w_in, m_conv_w, m_conv_b, m_dt_bias, m_a_log, m_d_skip, m_g_ssd, m_w_pool, m_pool_scale, m_w_out, m_g_mlp, m_w_up, m_w_down, m_g_final, v_w_ada, v_b_ada, v_g_mix, v_w_in, v_conv_w, v_conv_b, v_dt_bias, v_a_log, v_d_skip, v_g_ssd, v_w_pool, v_pool_scale, v_w_out, v_g_mlp, v_w_up, v_w_down, v_g_final):
    nb, seq, _ = x.shape
    T = nb * seq
    me = 4 * lax.axis_index("x") + 2 * lax.axis_index("y") + lax.axis_index("c")
    in_cols = w_in.shape[2]
    ada_cols = w_ada.shape[2]
    cw_cols = conv_w.shape[2]

    win_t = jnp.pad(w_in[0].astype(BF16).T, ((0, -in_cols % 16), (0, 0)))
    c_g, cw_g, win_g = _all_gather([c, conv_w[0], win_t], "ag_first")
    c_all = c_g.reshape(N_DEV * nb, D)
    cw_full = cw_g.transpose(1, 0, 2).reshape(4, CONV_CH)

    b_slice = lax.dynamic_slice(b_ada, (0, me * ada_cols), (1, ada_cols))
    mod_cols = _ada_fwd(c_all, w_ada[0], b_slice)
    (mod_g,) = _all_gather([mod_cols], "ag_mod")
    mod_all = mod_g.transpose(1, 0, 2).reshape(N_DEV * nb, 6, D)
    mod_mine = lax.dynamic_slice(mod_all, (me * nb, 0, 0), (nb, 6, D))
    mod = jnp.pad(mod_mine, ((0, 0), (0, 2), (0, 0)))

    x2 = x.reshape(T, D)
    tg2 = loss_target.reshape(T, D)
    heads = jnp.pad(jnp.concatenate([dt_bias, a_log, d_skip], axis=0), ((0, 5), (0, LANES - N_HEADS)))
    wpool_b = w_pool[0]
    u_b, pm, w_cat, wup_g = _mix_in(x2, mod, g_mix, win_g, in_cols, seq, [w_up[0].astype(BF16)])
    ymix, hstates, cvs, wout_g, wdn_g = _mixer_fwd(
        pm, cw_full, conv_b, heads, g_ssd, wpool_b, pool_scale, nb, seq,
        [w_out[0].astype(BF16), w_down[0].astype(BF16)])
    da_b, dym, dh1, u2_b, f_b, dup_b, ddn_b, dmod_a, acc_a = _mlp_fused(
        x2, ymix, tg2, mod, g_mlp, g_final.reshape(1, D), wout_g.reshape(MIX_W, D), wup_g, wdn_g, seq)

    gout_p = _dw_blocks(ymix, da_b, "dw_out", True, per_step=4)
    gup_p = _dw_blocks(u2_b, dup_b, "dw_up", False)
    gdn_p = _dw_blocks(f_b, ddn_b, "dw_down", True)
    ex_mlp = _exchange_start([gout_p, gup_p, gdn_p], "gmlp_start")
    dpb, d_conv, d_heads, d_vec, d_wpool = _mixer_bwd(
        pm, cvs, dym, hstates, cw_full, conv_b, heads, g_ssd, wpool_b, pool_scale, nb, seq, after=[ex_mlp[4]])
    gin_p, conv_g, vec_g, heads_g, wpool_parts = _dw_in(
        u_b, dpb, in_cols, [d_conv, d_vec, d_heads, d_wpool.reshape(4 * LANES, LANES)])
    ex_in = _exchange_start([gin_p], "gin_start")
    grad_x2, dmod, acc = _in_bwd(x2, dh1, dpb, mod, g_mix + ex_in[4][0:1, 0:1], w_cat, dmod_a, acc_a, seq)

    gout_r, gup_r, gdn_r = _exchange_wait(ex_mlp, dmod, "gmlp_wait", me)
    g_out, d_out, nm_out, nv_out = _adam_parts(gout_r, w_out[0], m_w_out[0], v_w_out[0], "adam_w_out")
    g_up, d_up, nm_up, nv_up = _adam_parts(gup_r, w_up[0], m_w_up[0], v_w_up[0], "adam_w_up")
    g_dn, d_dn, nm_dn, nv_dn = _adam_parts(gdn_r, w_down[0], m_w_down[0], v_w_down[0], "adam_w_down")

    dmod_g, acc_g = _all_gather([dmod, acc], "ag_small_bwd", after=[nm_out, nm_up, nm_dn])
    pool2 = (4 * LANES, LANES)
    wpool_outs = _adam_parts(wpool_parts, w_pool.reshape(pool2), m_w_pool.reshape(pool2), v_w_pool.reshape(pool2),
                             "adam_w_pool")
    small_params = {
        "b_ada": (b_ada, m_b_ada, v_b_ada), "g_mix": (g_mix, m_g_mix, v_g_mix), "conv_b": (conv_b, m_conv_b, v_conv_b),
        "dt_bias": (dt_bias, m_dt_bias, v_dt_bias), "a_log": (a_log, m_a_log, v_a_log),
        "d_skip": (d_skip, m_d_skip, v_d_skip), "g_ssd": (g_ssd, m_g_ssd, v_g_ssd),
        "pool_scale": (pool_scale, m_pool_scale, v_pool_scale), "g_mlp": (g_mlp, m_g_mlp, v_g_mlp),
        "g_final": tuple(a.reshape(1, D) for a in (g_final, m_g_final, v_g_final)),
    }
    small_res = _small_adam([dmod_g, acc_g, conv_g, vec_g, heads_g], small_params)
    g_cw_full, acc_sum = small_res[-2:]
    loss = acc_sum[2, 0]

    g_cw = lax.dynamic_slice(g_cw_full, (0, me * cw_cols), (4, cw_cols))
    d_cwp, nm_cwp, nv_cwp = _adam_plain(g_cw, conv_w[0], m_conv_w[0], v_conv_w[0], "adam_conv_w")

    dmod_all = dmod_g[:, :, 0:6].reshape(N_DEV * nb, 6 * D)
    dmod_slice = lax.dynamic_slice(dmod_all, (0, me * ada_cols), (N_DEV * nb, ada_cols))
    g_ada, d_ada, nm_ada, nv_ada = _ada_bwd_adam(c_all, dmod_slice, w_ada[0], m_w_ada[0], v_w_ada[0])

    ex_after = nm_ada[0:8, 0:LANES] + acc_sum[:, 0:LANES]
    (gin_r,) = _exchange_wait(ex_in, ex_after, "gin_wait", me)
    g_in, d_in, nm_in, nv_in = _adam_parts(gin_r, w_in[0], m_w_in[0], v_w_in[0], "adam_w_in")

    def small_outs(kind, wpool):
        res = {name: small_res[4 * i + kind] for i, name in enumerate(SMALL_PARAMS)}
        res["g_final"] = res["g_final"].reshape(D)
        res["w_pool"] = wpool.reshape(1, 4, LANES, LANES)
        return res

    def big_outs(ada, win, cwp, wout, wup, wdn):
        return {"w_ada": ada[None], "w_in": win.reshape(1, D, in_cols), "conv_w": cwp[None], "w_out": wout[None],
                "w_up": wup[None], "w_down": wdn[None]}

    order = ["w_ada", "b_ada", "g_mix", "w_in", "conv_w", "conv_b", "dt_bias", "a_log", "d_skip", "g_ssd", "w_pool",
             "pool_scale", "w_out", "g_mlp", "w_up", "w_down", "g_final"]
    groups = [
        {**small_outs(0, wpool_outs[0]), **big_outs(g_ada, g_in, g_cw, g_out, g_up, g_dn)},
        {**small_outs(1, wpool_outs[1]), **big_outs(d_ada, d_in, d_cwp, d_out, d_up, d_dn)},
        {**small_outs(2, wpool_outs[2]), **big_outs(nm_ada, nm_in, nm_cwp, nm_out, nm_up, nm_dn)},
        {**small_outs(3, wpool_outs[3]), **big_outs(nv_ada, nv_in, nv_cwp, nv_out, nv_up, nv_dn)},
    ]
    outs = [loss, grad_x2.reshape(nb, seq, D)]
    for grp in groups:
        outs += [grp[n] for n in order]
    return tuple(outs)
```

```python
import functools

import jax
import jax.numpy as jnp
from jax import lax
from jax.experimental import pallas as pl
from jax.experimental.pallas import tpu as pltpu

F32, BF16 = jnp.float32, jnp.bfloat16
MESH = pl.DeviceIdType.MESH
N_DEV = 8
D = 1024
LANES = 128
CHUNK = 128
POOL_W = 512
WINDOWS = (2, 4, 8, 16)
N_HEADS = 16
HEAD_DIM = 64
N_GROUPS = 2
GROUP_W = 512
N_STATE = 128
CONV_CH = 1536
OFF_Z, OFF_XBC, OFF_DT, IN_W = 512, 1536, 3072, 3088
PROJ_W = OFF_DT + LANES
MIX_W = 1536
D_FF = 4096
FF_BLK = 512
EPS = 1e-5
LR, B1, B2, AEPS, WD, STEP = 0.001, 0.9, 0.999, 1e-08, 0.01, 10
POOL_HALO = 16
CONV_HALO = 8
VMEM_LIMIT = 56 << 20
ADAM_BLOCK_BYTES = 1 << 20
DW_IN_WIN = 512


def _cparams(**kw):
    return pltpu.CompilerParams(vmem_limit_bytes=VMEM_LIMIT, **kw)


def _mm(a, b):
    return jnp.dot(a.astype(BF16), b.astype(BF16), preferred_element_type=F32)


def _mm_nt(a, b):
    return lax.dot_general(a.astype(BF16), b.astype(BF16), (((1,), (1,)), ((), ())), preferred_element_type=F32)


def _mm_tn(a, b):
    return lax.dot_general(a.astype(BF16), b.astype(BF16), (((0,), (0,)), ((), ())), preferred_element_type=F32)


def _split_bf16(v, terms):
    parts, rest = [], v
    for t in range(terms):
        p = rest.astype(BF16)
        parts.append(p)
        if t + 1 < terms:
            rest = rest - p.astype(F32)
    return parts


def _dot01(a, b, terms, split_lhs=True):
    if split_lhs:
        bb = b.astype(BF16)
        prods = [jnp.dot(p, bb, preferred_element_type=F32) for p in _split_bf16(a, terms)]
    else:
        ab = a.astype(BF16)
        prods = [jnp.dot(ab, p, preferred_element_type=F32) for p in _split_bf16(b, terms)]
    out = prods[0]
    for q in prods[1:]:
        out = out + q
    return out


def _sigmoid(v):
    return 1.0 / (1.0 + jnp.exp(-v))


def _expand_mat():
    r = lax.broadcasted_iota(jnp.int32, (LANES, D), 0)
    c = lax.broadcasted_iota(jnp.int32, (LANES, D), 1)
    return (r == c // HEAD_DIM).astype(F32)


def _reduce_mat():
    r = lax.broadcasted_iota(jnp.int32, (D, LANES), 0)
    c = lax.broadcasted_iota(jnp.int32, (D, LANES), 1)
    return (c == r // HEAD_DIM).astype(F32)


def _pos():
    return lax.axis_index("x"), lax.axis_index("y"), lax.axis_index("c")


GATHER_PIECES = 4
GATHER_PIECE_BYTES = 96 << 10


def _pieces(shape, dtype):
    rows = shape[0]
    size = jnp.dtype(dtype).itemsize
    for d in shape:
        size *= d
    whole_tiles = rows % (GATHER_PIECES * 16) == 0
    return GATHER_PIECES if whole_tiles and size // GATHER_PIECES >= GATHER_PIECE_BYTES else 1


class _Gather:
    def __init__(self, x_refs, o_refs, send, recv, loc):
        self.x_refs, self.o_refs, self.send, self.recv, self.loc = x_refs, o_refs, send, recv, loc
        self.n = len(x_refs)
        self.pieces = [_pieces(r.shape, r.dtype) for r in x_refs]
        self.base = [7 * sum(self.pieces[:a]) for a in range(self.n)]
        x, y, c = _pos()
        self.c = c
        self.me, self.sib = (x, y, c), (x, y, 1 - c)
        self.chips = [(1 - x, y), (x, 1 - y), (1 - x, 1 - y)]

    def _rows(self, a, p):
        rows = self.x_refs[a].shape[0] // self.pieces[a]
        return pl.ds(p * rows, rows)

    def _cp(self, a, p, k, block, to, own=False):
        dst = self.o_refs[a].at[4 * block[0] + 2 * block[1] + block[2], self._rows(a, p)]
        sem = self.base[a] + 7 * p + k
        return pltpu.make_async_remote_copy(
            src_ref=self.x_refs[a].at[self._rows(a, p)] if own else dst, dst_ref=dst,
            send_sem=self.send.at[sem], recv_sem=self.recv.at[sem], device_id=to, device_id_type=MESH)

    def _mine(self, a):
        me = self.me
        return pltpu.make_async_copy(self.x_refs[a], self.o_refs[a].at[4 * me[0] + 2 * me[1] + me[2]], self.loc.at[a])

    def _first(self, a, p):
        cps = [self._cp(a, p, 0, self.me, self.sib, own=True)]
        return cps + [self._cp(a, p, 1 + j, self.me, (*chip, self.c), own=True) for j, chip in enumerate(self.chips)]

    def _passed(self, a, p, j):
        return self._cp(a, p, 4 + j, (*self.chips[j], self.c), self.sib)

    def start(self):
        for a in range(self.n):
            self._mine(a).start()
        for p in range(max(self.pieces)):
            for a in range(self.n):
                if p < self.pieces[a]:
                    for cp in self._first(a, p):
                        cp.start()

    def forward(self, p):
        for j, chip in enumerate(self.chips):
            for a in range(self.n):
                if p < self.pieces[a]:
                    self._cp(a, p, 1 + j, (*chip, self.c), self.me).wait_recv()
                    self._passed(a, p, j).start()

    def finish(self):
        for a in range(self.n):
            for p in range(self.pieces[a]):
                self._cp(a, p, 0, self.sib, self.me).wait_recv()
                for j, chip in enumerate(self.chips):
                    self._cp(a, p, 4 + j, (*chip, 1 - self.c), self.me).wait_recv()
        for a in range(self.n):
            for p in range(self.pieces[a]):
                for cp in self._first(a, p):
                    cp.wait_send()
                for j in range(3):
                    self._passed(a, p, j).wait_send()
            self._mine(a).wait()

    def begin_hosted(self, step, steps):
        @pl.when(step == 0)
        def _():
            self.start()

        n_p = max(self.pieces)
        for p in range(n_p):
            @pl.when(step == min(((p + 1) * 7 * steps) // (8 * n_p), steps - 1))
            def _():
                self.forward(p)

    def end_hosted(self, step, steps):
        @pl.when(step == steps - 1)
        def _():
            self.finish()


class _Exchange:
    def __init__(self, x_refs, o_refs, send, recv, loc):
        self.x_refs, self.o_refs, self.send, self.recv, self.loc = x_refs, o_refs, send, recv, loc
        self.n = len(x_refs)
        x, y, c = _pos()
        self.me_i = 4 * x + 2 * y + c
        self.peers = []
        for k in range(1, N_DEV):
            px = 1 - x if (k >> 2) & 1 else x
            py = 1 - y if (k >> 1) & 1 else y
            pc = 1 - c if k & 1 else c
            self.peers.append(((px, py, pc), 4 * px + 2 * py + pc))

    def _mine(self, a):
        return pltpu.make_async_copy(self.x_refs[a].at[self.me_i], self.o_refs[a].at[self.me_i], self.loc.at[a])

    def _cp(self, a, k, landing):
        peer, peer_i = self.peers[k]
        return pltpu.make_async_remote_copy(
            src_ref=self.x_refs[a].at[peer_i], dst_ref=self.o_refs[a].at[landing],
            send_sem=self.send.at[a * 7 + k], recv_sem=self.recv.at[a * 7 + k],
            device_id=peer, device_id_type=MESH)

    def start(self):
        for a in range(self.n):
            self._mine(a).start()
            for k in range(N_DEV - 1):
                self._cp(a, k, self.me_i).start()

    def finish(self):
        for a in range(self.n):
            for k in range(N_DEV - 1):
                self._cp(a, k, self.peers[k][1]).wait_recv()
        for a in range(self.n):
            for k in range(N_DEV - 1):
                self._cp(a, k, self.me_i).wait_send()
            self._mine(a).wait()


def _gather_scratch(xs):
    n_sem = 7 * sum(_pieces(v.shape, v.dtype) for v in xs)
    return [pltpu.SemaphoreType.DMA((n_sem,)), pltpu.SemaphoreType.DMA((n_sem,)), pltpu.SemaphoreType.DMA((len(xs),))]


ANY_SPEC = pl.BlockSpec(memory_space=pl.ANY)


def _all_gather(xs, name, after=()):
    n, na = len(xs), len(after)

    def body(*refs):
        g = _Gather(refs[:n], refs[n + na:2 * n + na], *refs[2 * n + na:])
        g.start()
        for p in range(max(g.pieces)):
            g.forward(p)
        g.finish()

    return pl.pallas_call(
        body, name=name,
        out_shape=[jax.ShapeDtypeStruct((N_DEV,) + v.shape, v.dtype) for v in xs],
        in_specs=[ANY_SPEC] * (n + na), out_specs=[ANY_SPEC] * n, scratch_shapes=_gather_scratch(xs),
    )(*xs, *after)


HBM_SPEC = pl.BlockSpec(memory_space=pltpu.HBM)
SEM_SPEC = pl.BlockSpec(memory_space=pltpu.SEMAPHORE)
VMEM_SPEC = pl.BlockSpec(memory_space=pltpu.VMEM)
SPLIT_EFFECT = pltpu.SideEffectType.DATAFLOW_SIDE_EFFECTING


def _in_hbm(v):
    return pltpu.with_memory_space_constraint(v, pltpu.HBM)


def _exchange_start(blocks, name):
    n = len(blocks)

    def body(*refs):
        x_refs, land_refs = refs[:n], refs[n:2 * n]
        send, recv = refs[2 * n:2 * n + 2]
        token = refs[-1]
        ex = _Exchange(x_refs, land_refs, send, recv, None)
        for a in range(n):
            for k in range(N_DEV - 1):
                ex._cp(a, k, ex.me_i).start()
        token[...] = jnp.zeros_like(token)

    lands = [lax.empty(v.shape, v.dtype) for v in blocks]
    hbm = tuple(pltpu.HBM(v.shape, v.dtype) for v in list(blocks) + lands)
    n_sem = (N_DEV - 1) * n
    out = pl.pallas_call(
        body, name=name,
        out_shape=(pltpu.SemaphoreType.DMA((n_sem,)), pltpu.SemaphoreType.DMA((n_sem,))) + hbm
        + (jax.ShapeDtypeStruct((8, LANES), F32),),
        in_specs=(HBM_SPEC,) * (2 * n), out_specs=(SEM_SPEC, SEM_SPEC) + (HBM_SPEC,) * (2 * n) + (VMEM_SPEC,),
        input_output_aliases={i: i + 2 for i in range(2 * n)},
        compiler_params=pltpu.CompilerParams(has_side_effects=SPLIT_EFFECT),
    )(*[_in_hbm(v) for v in list(blocks) + lands])
    return out[0], out[1], list(out[2:2 + n]), list(out[2 + n:2 + 2 * n]), out[-1]


def _exchange_wait(ex, after, name, me):
    send, recv, thru, lands, _ = ex
    n = len(thru)

    def body(*refs):
        x_refs, land_refs = refs[:n], refs[n:2 * n]
        send_ref, recv_ref = refs[2 * n:2 * n + 2]
        e = _Exchange(x_refs, land_refs, send_ref, recv_ref, None)
        for a in range(n):
            for k in range(N_DEV - 1):
                e._cp(a, k, e.me_i).wait_send()
                e._cp(a, k, e.peers[k][1]).wait_recv()

    hbm = tuple(pltpu.HBM(v.shape, v.dtype) for v in list(thru) + list(lands))
    out = pl.pallas_call(
        body, name=name, out_shape=hbm,
        in_specs=(HBM_SPEC,) * (2 * n) + (SEM_SPEC, SEM_SPEC, ANY_SPEC), out_specs=(HBM_SPEC,) * (2 * n),
        input_output_aliases={i: i for i in range(2 * n)},
        compiler_params=pltpu.CompilerParams(has_side_effects=SPLIT_EFFECT),
    )(*thru, *lands, send, recv, after)
    done = []
    for own, land in zip(out[:n], out[n:]):
        mine = lax.dynamic_slice(own, (me,) + (0,) * (own.ndim - 1), (1,) + own.shape[1:])
        done.append(lax.dynamic_update_slice(land, mine, (me,) + (0,) * (own.ndim - 1)))
    return done


def _ada_fwd(c_all, w_ada, b_slice):
    def body(c_ref, w_ref, b_ref, o_ref):
        cv = c_ref[...]
        act = cv * _sigmoid(cv)
        o_ref[...] = _mm(act, w_ref[...]) + b_ref[...]

    nb, nc = c_all.shape[0], w_ada.shape[1]
    return pl.pallas_call(body, name="ada_fwd", out_shape=jax.ShapeDtypeStruct((nb, nc), F32),
                          compiler_params=_cparams())(c_all, w_ada, b_slice)


def _adam_math(w, g, m, v):
    m = B1 * m + (1.0 - B1) * g
    v = B2 * v + (1.0 - B2) * jnp.square(g)
    m_hat = m / (1.0 - B1 ** STEP)
    v_hat = v / (1.0 - B2 ** STEP)
    delta = -LR * (m_hat / (jnp.sqrt(v_hat) + AEPS) + WD * w)
    return delta, m, v


def _ada_bwd_adam(c_all, dmod_slice, w, m, v):
    rows, cols = w.shape
    br = 256

    def body(c_ref, d_ref, w_ref, m_ref, v_ref, g_out, dl_out, m_out, v_out):
        cv = c_ref[...]
        act = cv * _sigmoid(cv)
        g = _mm_tn(act, d_ref[...])
        g_out[...] = g
        dl, mn, vn = _adam_math(w_ref[...], g, m_ref[...], v_ref[...])
        dl_out[...] = dl
        m_out[...] = mn
        v_out[...] = vn

    nb = c_all.shape[0]
    wspec = pl.BlockSpec((br, cols), lambda i: (i, 0))
    return pl.pallas_call(
        body, name="ada_bwd_adam", grid=(rows // br,),
        in_specs=[pl.BlockSpec((nb, br), lambda i: (0, i)), pl.BlockSpec((nb, cols), lambda i: (0, 0)),
                  wspec, wspec, wspec],
        out_specs=[wspec] * 4, out_shape=[jax.ShapeDtypeStruct((rows, cols), F32)] * 4,
        compiler_params=_cparams(),
    )(c_all, dmod_slice, w, m, v)


def _mix_in(x2, mod, g_mix, win_g, in_cols, seq, shards):
    T = x2.shape[0]
    tm = min(512, seq)
    tps = seq // tm
    blk_rows = win_g.shape[1]
    pad_rows = -blk_rows % (4 * LANES)
    ns = len(shards)
    steps = T // tm

    def body(*refs):
        x_ref, mod_ref, g_ref, wb_ref = refs[:4]
        sh_refs = refs[4:4 + ns]
        u_ref, pm_ref, wc_ref = refs[4 + ns:7 + ns]
        ga_refs = refs[7 + ns:7 + 2 * ns]
        w_ref, send, recv, loc = refs[7 + 2 * ns:]
        step = pl.program_id(0)
        gather = _Gather(sh_refs, ga_refs, send, recv, loc)
        gather.begin_hosted(step, steps)

        @pl.when(step == 0)
        def _():
            w_ref[:, OFF_DT:] = jnp.zeros((D, PROJ_W - OFF_DT), BF16)
            for j in range(N_DEV):
                blk = jnp.concatenate([wb_ref[j], jnp.zeros((pad_rows, D), BF16)], axis=0)
                w_ref[:, in_cols * j:in_cols * (j + 1)] = blk.T[:, :in_cols]
            wc_ref[...] = w_ref[...]

        x = x_ref[...]
        r = lax.rsqrt(jnp.mean(x * x, axis=-1, keepdims=True) + EPS)
        md = mod_ref[0]
        u = (x * r * g_ref[...]) * (1.0 + md[1:2]) + md[0:1]
        ub = u.astype(BF16)
        u_ref[...] = ub
        pm_ref[...] = jnp.dot(ub, w_ref[...], preferred_element_type=F32)
        gather.end_hosted(step, steps)

    whole = pl.BlockSpec(memory_space=pltpu.VMEM)
    return pl.pallas_call(
        body, name="mix_in", grid=(T // tm,),
        in_specs=[pl.BlockSpec((tm, D), lambda i: (i, 0)), pl.BlockSpec((1, 8, D), lambda i: (i // tps, 0, 0)),
                  pl.BlockSpec((1, D), lambda i: (0, 0)), whole] + [ANY_SPEC] * ns,
        out_specs=[pl.BlockSpec((tm, D), lambda i: (i, 0)), pl.BlockSpec((tm, PROJ_W), lambda i: (i, 0)),
                   pl.BlockSpec((D, PROJ_W), lambda i: (0, 0))] + [ANY_SPEC] * ns,
        out_shape=[jax.ShapeDtypeStruct((T, D), BF16), jax.ShapeDtypeStruct((T, PROJ_W), F32),
                   jax.ShapeDtypeStruct((D, PROJ_W), BF16)]
        + [jax.ShapeDtypeStruct((N_DEV,) + v.shape, v.dtype) for v in shards],
        scratch_shapes=[pltpu.VMEM((D, PROJ_W), BF16)] + _gather_scratch(shards),
        compiler_params=_cparams(),
    )(x2, mod, g_mix, win_g, *shards)


def _chunk_forward(up, z, ux, dtin, halo_p, halo_x, hprev, cw, cb, hp, gssd, wpool, pscale, t0, y_scr, cv=None):
    L = CHUNK
    out = {}
    row = lax.broadcasted_iota(jnp.int32, (L, 1), 0)
    t = (t0 + row + 1).astype(F32)
    e = jnp.concatenate([halo_p, up], axis=0)
    s2 = e + pltpu.roll(e, 1, 0)
    s4 = s2 + pltpu.roll(s2, 2, 0)
    s8 = s4 + pltpu.roll(s4, 4, 0)
    s16 = s8 + pltpu.roll(s8, 8, 0)
    sums = (s2, s4, s8, s16)
    p, inv, yp = [], [], []
    for gi, w in enumerate(WINDOWS):
        sl = slice(gi * LANES, (gi + 1) * LANES)
        ic = 1.0 / jnp.minimum(t, float(w))
        pg = sums[gi][POOL_HALO:, sl] * ic - up[:, sl]
        p.append(pg)
        inv.append(ic)
        yp.append(_mm(pg, wpool[gi]))
    out["p"], out["inv"], out["yp"] = p, inv, yp
    out["y_pool"] = jnp.concatenate(yp, axis=1) * pscale
    if cv is None:
        ex = jnp.concatenate([halo_x, ux], axis=0)
        taps = [pltpu.roll(ex, 3, 0)[CONV_HALO:], pltpu.roll(ex, 2, 0)[CONV_HALO:], pltpu.roll(ex, 1, 0)[CONV_HALO:], ux]
        cv = cb + taps[0] * cw[0:1] + taps[1] * cw[1:2] + taps[2] * cw[2:3] + taps[3] * cw[3:4]
    sg = _sigmoid(cv)
    xbc = cv * sg
    out["cv"], out["sg"] = cv, sg
    X = xbc[:, :D]
    Bm = xbc[:, D:D + N_GROUPS * N_STATE]
    Cm = xbc[:, D + N_GROUPS * N_STATE:]
    pre = dtin + hp[0:1]
    dt = jnp.maximum(pre, 0.0) + jnp.log(1.0 + jnp.exp(-jnp.abs(pre)))
    a_row = -jnp.exp(hp[1:2])
    da = dt * a_row
    ri = lax.broadcasted_iota(jnp.int32, (L, L), 0)
    ci = lax.broadcasted_iota(jnp.int32, (L, L), 1)
    causal = ri >= ci
    cum = _dot01(causal.astype(F32), da, 3, split_lhs=False)
    cum_t = cum.T
    cum_last = cum[L - 1:L]
    eo = jnp.exp(cum)
    dec = jnp.exp(cum_last - cum)
    cd = jnp.exp(cum_last)
    exm = _expand_mat()
    rows8 = jnp.concatenate([cd, hp[2:3], jnp.zeros((6, LANES), F32)], axis=0)
    rep = _dot01(jnp.concatenate([dt, eo, dec, rows8], axis=0), exm, 2)
    dt_rep, eo_rep, dec_rep = rep[0:L], rep[L:2 * L], rep[2 * L:3 * L]
    cd_rep, dskip_rep = rep[3 * L:3 * L + 1], rep[3 * L + 1:3 * L + 2]
    xdt = X * dt_rep
    out.update(X=X, Bm=Bm, Cm=Cm, pre=pre, dt=dt, a_row=a_row, cum=cum, cum_t=cum_t, eo=eo, dec=dec, cd=cd,
               dt_rep=dt_rep, eo_rep=eo_rep, dec_rep=dec_rep, cd_rep=cd_rep, dskip_rep=dskip_rep, xdt=xdt,
               causal=causal, anti=(ri <= ci).astype(F32), exm=exm)
    G, lms, yoff, hnew, xdec = [], [], [], [], []
    for g in range(N_GROUPS):
        gs = slice(g * GROUP_W, (g + 1) * GROUP_W)
        Bg = Bm[:, g * N_STATE:(g + 1) * N_STATE]
        Cg = Cm[:, g * N_STATE:(g + 1) * N_STATE]
        Gg = _mm_nt(Cg, Bg)
        G.append(Gg)
        for hh in range(N_HEADS // N_GROUPS):
            h = g * (N_HEADS // N_GROUPS) + hh
            seg = cum[:, h:h + 1] - cum_t[h:h + 1, :]
            lm = jnp.where(causal, jnp.exp(jnp.minimum(seg, 0.0)), 0.0)
            lms.append(lm)
            hs = slice(h * HEAD_DIM, (h + 1) * HEAD_DIM)
            y_scr[:, hs] = _mm(Gg * lm, xdt[:, hs])
        xd = xdt[:, gs] * dec_rep[:, gs]
        xdec.append(xd)
        sgm = _mm_tn(Bg, xd)
        yoff.append(_mm(Cg, hprev[g]) * eo_rep[:, gs])
        hnew.append(hprev[g] * cd_rep[:, gs] + sgm)
    out.update(G=G, lms=lms, yoff=yoff, hnew=hnew, xdec=xdec)
    y = y_scr[...] + jnp.concatenate(yoff, axis=1) + dskip_rep * X
    sz = _sigmoid(z)
    silz = z * sz
    yz = y * silz
    rg, yn = [], []
    for g in range(N_GROUPS):
        gs = slice(g * GROUP_W, (g + 1) * GROUP_W)
        r = lax.rsqrt(jnp.mean(yz[:, gs] * yz[:, gs], axis=-1, keepdims=True) + EPS)
        rg.append(r)
        yn.append(yz[:, gs] * r)
    yn = jnp.concatenate(yn, axis=1)
    out.update(y=y, sz=sz, silz=silz, rg=rg, yn=yn)
    out["y_ssd"] = yn * gssd
    return out


def _mixer_fwd(pm, cw, cb, hp, gssd, wpool, pscale, nb, seq, shards):
    nc = seq // CHUNK
    ns = len(shards)
    steps = nb * nc

    def body(*refs):
        pm_ref, cw_ref, cb_ref, hp_ref, gs_ref, wp_ref, ps_ref = refs[:7]
        sh_refs = refs[7:7 + ns]
        ym_ref, hs_ref, cv_ref = refs[7 + ns:10 + ns]
        ga_refs = refs[10 + ns:10 + 2 * ns]
        halo_p, halo_x, state, y_scr, send, recv, loc = refs[10 + 2 * ns:]
        c = pl.program_id(1)
        step = pl.program_id(0) * nc + c
        gather = _Gather(sh_refs, ga_refs, send, recv, loc)
        gather.begin_hosted(step, steps)

        @pl.when(c == 0)
        def _():
            halo_p[...] = jnp.zeros_like(halo_p)
            halo_x[...] = jnp.zeros_like(halo_x)
            state[...] = jnp.zeros_like(state)

        up = pm_ref[:, 0:POOL_W]
        z = pm_ref[:, OFF_Z:OFF_XBC]
        ux = pm_ref[:, OFF_XBC:OFF_DT]
        hprev = [state[0], state[1]]
        hs_ref[0, 0, 0] = hprev[0]
        hs_ref[0, 0, 1] = hprev[1]
        o = _chunk_forward(up, z, ux, pm_ref[:, OFF_DT:], halo_p[...], halo_x[...], hprev, cw_ref[...], cb_ref[...],
                           hp_ref[...], gs_ref[...], wp_ref[...], ps_ref[...], c * CHUNK, y_scr)
        ym_ref[:, 0:POOL_W] = o["y_pool"].astype(BF16)
        ym_ref[:, POOL_W:] = o["y_ssd"].astype(BF16)
        cv_ref[...] = o["cv"]
        state[0] = o["hnew"][0]
        state[1] = o["hnew"][1]
        halo_p[...] = up[CHUNK - POOL_HALO:]
        halo_x[...] = ux[CHUNK - CONV_HALO:]
        gather.end_hosted(step, steps)

    def full(shape):
        return pl.BlockSpec(shape, lambda b, c: (0,) * len(shape))

    T = nb * seq
    return pl.pallas_call(
        body, name="mixer_fwd", grid=(nb, nc),
        in_specs=[pl.BlockSpec((CHUNK, PROJ_W), lambda b, c: (b * nc + c, 0)),
                  full((4, CONV_CH)), full((1, CONV_CH)), full((8, LANES)), full((1, D)),
                  full((4, LANES, LANES)), full((1, POOL_W))] + [ANY_SPEC] * ns,
        out_specs=[pl.BlockSpec((CHUNK, MIX_W), lambda b, c: (b * nc + c, 0)),
                   pl.BlockSpec((1, 1, N_GROUPS, N_STATE, GROUP_W), lambda b, c: (b, c, 0, 0, 0)),
                   pl.BlockSpec((CHUNK, CONV_CH), lambda b, c: (b * nc + c, 0))] + [ANY_SPEC] * ns,
        out_shape=[jax.ShapeDtypeStruct((T, MIX_W), BF16),
                   jax.ShapeDtypeStruct((nb, nc, N_GROUPS, N_STATE, GROUP_W), F32),
                   jax.ShapeDtypeStruct((T, CONV_CH), F32)]
        + [jax.ShapeDtypeStruct((N_DEV,) + v.shape, v.dtype) for v in shards],
        scratch_shapes=[pltpu.VMEM((POOL_HALO, POOL_W), F32), pltpu.VMEM((CONV_HALO, CONV_CH), F32),
                        pltpu.VMEM((N_GROUPS, N_STATE, GROUP_W), F32), pltpu.VMEM((CHUNK, D), F32)] + _gather_scratch(shards),
        compiler_params=_cparams(),
    )(pm, cw, cb, hp, gssd, wpool, pscale, *shards)


def _mixer_bwd(pm, cvs, dym, hstates, cw, cb, hp, gssd, wpool, pscale, nb, seq, after=()):
    nc = seq // CHUNK
    hpg = N_HEADS // N_GROUPS
    na = len(after)

    def body(*refs):
        (pm_ref, hpool_ref, cv_ref, dy_ref, hs_ref, cw_ref, cb_ref, hp_ref, gs_ref, wp_ref, ps_ref) = refs[:11]
        dpm_ref, dconv_ref, dhp_ref, dvec_ref, dwp_ref = refs[11 + na:16 + na]
        nxt_q, nxt_cv, rstate, y_scr, dx_scr = refs[16 + na:]
        b = pl.program_id(0)
        ci = pl.program_id(1)
        c = nc - 1 - ci

        @pl.when((b == 0) & (ci == 0))
        def _():
            for r in (dconv_ref, dhp_ref, dvec_ref, dwp_ref):
                r[...] = jnp.zeros_like(r)

        @pl.when(ci == 0)
        def _():
            nxt_q[...] = jnp.zeros_like(nxt_q)
            nxt_cv[...] = jnp.zeros_like(nxt_cv)
            rstate[...] = jnp.zeros_like(rstate)

        first = (c > 0).astype(F32)
        up = pm_ref[:, 0:POOL_W]
        z = pm_ref[:, OFF_Z:OFF_XBC]
        ux = pm_ref[:, OFF_XBC:OFF_DT]
        halo_p = hpool_ref[...] * first
        hprev = [hs_ref[0, 0, 0], hs_ref[0, 0, 1]]
        cw, cb, hp, gssd, wpool, pscale = cw_ref[...], cb_ref[...], hp_ref[...], gs_ref[...], wp_ref[...], ps_ref[...]
        o = _chunk_forward(up, z, ux, pm_ref[:, OFF_DT:], halo_p, None, hprev, cw, cb, hp, gssd, wpool, pscale,
                           c * CHUNK, y_scr, cv=cv_ref[...])
        L = CHUNK
        dy_pool = dy_ref[:, 0:POOL_W].astype(F32)
        dy_ssd = dy_ref[:, POOL_W:].astype(F32)

        dvec_ref[1:2, 0:POOL_W] += jnp.sum(dy_pool * jnp.concatenate(o["yp"], axis=1), axis=0, keepdims=True)
        dyp = dy_pool * pscale
        qs = []
        dps = []
        for gi in range(len(WINDOWS)):
            sl = slice(gi * LANES, (gi + 1) * LANES)
            dwp_ref[gi] += _mm_tn(o["p"][gi], dyp[:, sl])
            dpg = _mm_nt(dyp[:, sl], wpool[gi])
            dps.append(dpg)
            qs.append(dpg * o["inv"][gi])
        q = jnp.concatenate(qs, axis=1)
        e = jnp.concatenate([q, nxt_q[...]], axis=0)
        n = L + POOL_HALO
        s2 = e + pltpu.roll(e, n - 1, 0)
        s4 = s2 + pltpu.roll(s2, n - 2, 0)
        s8 = s4 + pltpu.roll(s4, n - 4, 0)
        s16 = s8 + pltpu.roll(s8, n - 8, 0)
        sums = (s2, s4, s8, s16)
        for gi in range(len(WINDOWS)):
            sl = slice(gi * LANES, (gi + 1) * LANES)
            dpm_ref[:, sl] = (sums[gi][:L, sl] - dps[gi]).astype(BF16)
        nxt_q[...] = q[:POOL_HALO]

        yn, y, silz, sz = o["yn"], o["y"], o["silz"], o["sz"]
        dvec_ref[0:1] += jnp.sum(dy_ssd * yn, axis=0, keepdims=True)
        dyn = dy_ssd * gssd
        dyz = []
        for g in range(N_GROUPS):
            gs = slice(g * GROUP_W, (g + 1) * GROUP_W)
            mean = jnp.mean(dyn[:, gs] * yn[:, gs], axis=-1, keepdims=True)
            dyz.append(o["rg"][g] * (dyn[:, gs] - yn[:, gs] * mean))
        dyz = jnp.concatenate(dyz, axis=1)
        dyv = dyz * silz
        dpm_ref[:, OFF_Z:OFF_XBC] = (dyz * y * (sz * (1.0 + z * (1.0 - sz)))).astype(BF16)

        X, Bm, Cm, xdt = o["X"], o["Bm"], o["Cm"], o["xdt"]
        exm = o["exm"]
        rdm = _reduce_mat()
        lane = lax.broadcasted_iota(jnp.int32, (1, LANES), 1)
        sub = lax.broadcasted_iota(jnp.int32, (LANES, 1), 0)
        dX = o["dskip_rep"] * dyv
        yoff_full = jnp.concatenate(o["yoff"], axis=1)
        rs = jnp.zeros((L, LANES), F32)
        cs_t = jnp.zeros((LANES, L), F32)
        dBs, dCs = [], []
        rh_sums = []
        ddec = []
        for g in range(N_GROUPS):
            gs = slice(g * GROUP_W, (g + 1) * GROUP_W)
            Bg = Bm[:, g * N_STATE:(g + 1) * N_STATE]
            Cg = Cm[:, g * N_STATE:(g + 1) * N_STATE]
            Gg = o["G"][g]
            R = rstate[g]
            dwm = dyv[:, gs] * o["eo_rep"][:, gs]
            dC = _mm_nt(dwm, hprev[g])
            dH = _mm_tn(Cg, dwm)
            dG = jnp.zeros((L, L), F32)
            for hh in range(hpg):
                h = g * hpg + hh
                hs = slice(h * HEAD_DIM, (h + 1) * HEAD_DIM)
                lm = o["lms"][h]
                m_h = Gg * lm
                dM = _mm_nt(dyv[:, hs], xdt[:, hs])
                dx_scr[:, hs] = _mm_tn(m_h, dyv[:, hs])
                qm = dM * m_h
                rs = rs + jnp.sum(qm, axis=1, keepdims=True) * (lane == h).astype(F32)
                cs_t = cs_t + (sub == h).astype(F32) * jnp.sum(qm, axis=0, keepdims=True)
                dG = dG + dM * lm
            dC = dC + _mm(dG, Bg)
            dB = _mm_tn(dG, Cg)
            zx = _mm(Bg, R)
            dxdt_state = zx * o["dec_rep"][:, gs]
            ddec.append(zx * xdt[:, gs])
            dB = dB + _mm_nt(o["xdec"][g], R)
            rh_sums.append(jnp.sum(R * hprev[g], axis=0, keepdims=True))
            rstate[g] = dH + o["cd_rep"][:, gs] * R
            dx_scr[:, gs] = dx_scr[:, gs] + dxdt_state
            dBs.append(dB)
            dCs.append(dC)
        dxdt = dx_scr[...]
        tail = jnp.concatenate([jnp.sum(dyv * X, axis=0, keepdims=True), jnp.concatenate(rh_sums, axis=1),
                                jnp.zeros((6, D), F32)], axis=0)
        red = _dot01(jnp.concatenate([dyv * yoff_full, jnp.concatenate(ddec, axis=1), dxdt * X, tail], axis=0), rdm, 2)
        d_dskip, dcd_row = red[3 * L:3 * L + 1], red[3 * L + 1:3 * L + 2]
        ddec_h = red[L:2 * L] * o["dec"]
        dcum_last = jnp.sum(ddec_h, axis=0, keepdims=True) + dcd_row * o["cd"]
        dcum = red[0:L] + rs - cs_t.T - ddec_h + (sub == L - 1).astype(F32) * dcum_last
        dda = _dot01(o["anti"], dcum, 3, split_lhs=False)
        ddt_v = dda * o["a_row"] + red[2 * L:3 * L]
        dX = dX + dxdt * o["dt_rep"]
        head_mask = (lane < N_HEADS).astype(F32)
        d_alog = jnp.sum(dda * o["dt"], axis=0, keepdims=True) * o["a_row"] * head_mask
        dpre = ddt_v * _sigmoid(o["pre"]) * head_mask
        dpm_ref[:, OFF_DT:] = dpre.astype(BF16)
        d_dtb = jnp.sum(dpre, axis=0, keepdims=True)
        dhp_ref[...] += jnp.concatenate([d_dtb, d_alog, d_dskip * head_mask, jnp.zeros((5, LANES), F32)], axis=0)

        dxbc = jnp.concatenate([dX] + dBs + dCs, axis=1)
        sg, cv = o["sg"], o["cv"]
        dcv = dxbc * (sg * (1.0 + cv * (1.0 - sg)))
        e2 = jnp.concatenate([dcv, nxt_cv[...]], axis=0)
        n2 = L + CONV_HALO
        ahead = [dcv, pltpu.roll(e2, n2 - 1, 0)[:L], pltpu.roll(e2, n2 - 2, 0)[:L], pltpu.roll(e2, n2 - 3, 0)[:L]]
        dconv_ref[0:5] += jnp.concatenate(
            [jnp.sum(ux * ahead[3 - k], axis=0, keepdims=True) for k in range(4)]
            + [jnp.sum(dcv, axis=0, keepdims=True)], axis=0)
        dux = ahead[0] * cw[3:4] + ahead[1] * cw[2:3] + ahead[2] * cw[1:2] + ahead[3] * cw[0:1]
        dpm_ref[:, OFF_XBC:OFF_DT] = dux.astype(BF16)
        nxt_cv[...] = dcv[:CONV_HALO]

    def full(shape):
        return pl.BlockSpec(shape, lambda b, c: (0,) * len(shape))

    def rowblk(b, c):
        return b * nc + (nc - 1 - c)

    hp_blocks = CHUNK // POOL_HALO
    T = nb * seq
    return pl.pallas_call(
        body, name="mixer_bwd", grid=(nb, nc),
        in_specs=[pl.BlockSpec((CHUNK, PROJ_W), lambda b, c: (rowblk(b, c), 0)),
                  pl.BlockSpec((POOL_HALO, POOL_W), lambda b, c: (jnp.maximum(rowblk(b, c) * hp_blocks - 1, 0), 0)),
                  pl.BlockSpec((CHUNK, CONV_CH), lambda b, c: (rowblk(b, c), 0)),
                  pl.BlockSpec((CHUNK, MIX_W), lambda b, c: (rowblk(b, c), 0)),
                  pl.BlockSpec((1, 1, N_GROUPS, N_STATE, GROUP_W), lambda b, c: (b, nc - 1 - c, 0, 0, 0)),
                  full((4, CONV_CH)), full((1, CONV_CH)), full((8, LANES)), full((1, D)),
                  full((4, LANES, LANES)), full((1, POOL_W))] + [ANY_SPEC] * na,
        out_specs=[pl.BlockSpec((CHUNK, PROJ_W), lambda b, c: (rowblk(b, c), 0)),
                   full((8, CONV_CH)), full((8, LANES)), full((8, D)), full((4, LANES, LANES))],
        out_shape=[jax.ShapeDtypeStruct((T, PROJ_W), BF16),
                   jax.ShapeDtypeStruct((8, CONV_CH), F32), jax.ShapeDtypeStruct((8, LANES), F32),
                   jax.ShapeDtypeStruct((8, D), F32), jax.ShapeDtypeStruct((4, LANES, LANES), F32)],
        scratch_shapes=[pltpu.VMEM((POOL_HALO, POOL_W), F32), pltpu.VMEM((CONV_HALO, CONV_CH), F32),
                        pltpu.VMEM((N_GROUPS, N_STATE, GROUP_W), F32), pltpu.VMEM((CHUNK, D), F32),
                        pltpu.VMEM((CHUNK, D), F32)],
        compiler_params=_cparams(),
    )(pm, pm, cvs, dym, hstates, cw, cb, hp, gssd, wpool, pscale, *after)


def _mlp_fused(x2, ymix, target, mod, g_mlp, g_final, w_out, w_up, w_down, seq):
    T = x2.shape[0]
    tm = min(256, seq)
    tps = seq // tm
    nblk = D_FF // FF_BLK

    def body(x_ref, ym_ref, tg_ref, mod_ref, gm_ref, gf_ref, wo_ref, wu_hbm, wd_hbm,
             da_ref, dym_ref, dh1_ref, u2_ref, f_ref, dup_ref, ddn_ref, dmod_ref, acc_ref,
             relu_scr, wu_ref, wd_ref, wsem):
        i = pl.program_id(0)
        w_copies = [pltpu.make_async_copy(wu_hbm, wu_ref, wsem.at[0]), pltpu.make_async_copy(wd_hbm, wd_ref, wsem.at[1])]

        @pl.when(i == 0)
        def _():
            for cp in w_copies:
                cp.start()
            acc_ref[...] = jnp.zeros_like(acc_ref)

        @pl.when(i % tps == 0)
        def _():
            dmod_ref[...] = jnp.zeros_like(dmod_ref)

        md = mod_ref[0]
        gate_m, shift_f, scale_f, gate_f = md[2:3], md[3:4], md[4:5], md[5:6]
        g_mlp, g_fin = gm_ref[...], gf_ref[...]
        a = jnp.dot(ym_ref[...], wo_ref[...], preferred_element_type=F32)
        h1 = x_ref[...] + gate_m * a
        r2 = lax.rsqrt(jnp.mean(h1 * h1, axis=-1, keepdims=True) + EPS)
        n2 = h1 * r2
        u2 = (n2 * g_mlp) * (1.0 + scale_f) + shift_f
        u2b = u2.astype(BF16)
        u2_ref[...] = u2b

        @pl.when(i == 0)
        def _():
            for cp in w_copies:
                cp.wait()

        dn = jnp.zeros((tm, D), F32)
        for j in range(nblk):
            js = slice(j * FF_BLK, (j + 1) * FF_BLK)
            upj = jnp.maximum(jnp.dot(u2b, wu_ref[j], preferred_element_type=F32), 0.0)
            relu_scr[:, js] = upj
            fj = (upj * upj).astype(BF16)
            f_ref[:, js] = fj
            dn = dn + jnp.dot(fj, wd_ref[j], preferred_element_type=F32)
        h2 = h1 + gate_f * dn
        r3 = lax.rsqrt(jnp.mean(h2 * h2, axis=-1, keepdims=True) + EPS)
        n3 = h2 * r3
        err = n3 * g_fin - tg_ref[...]
        loss = 0.5 * jnp.sum(jnp.mean(err * err, axis=-1, keepdims=True), axis=0, keepdims=True)
        dout = err * (1.0 / D)
        d_gfin = jnp.sum(dout * n3, axis=0, keepdims=True)
        dn3 = dout * g_fin
        dh2 = r3 * (dn3 - n3 * jnp.mean(dn3 * n3, axis=-1, keepdims=True))
        d_gate_f = jnp.sum(dh2 * dn, axis=0, keepdims=True)
        ddn = (gate_f * dh2).astype(BF16)
        ddn_ref[...] = ddn
        du2 = jnp.zeros((tm, D), F32)
        for j in range(nblk):
            js = slice(j * FF_BLK, (j + 1) * FF_BLK)
            dfj = lax.dot_general(ddn, wd_ref[j], (((1,), (1,)), ((), ())), preferred_element_type=F32)
            dupj = (dfj * (2.0 * relu_scr[:, js])).astype(BF16)
            dup_ref[:, js] = dupj
            du2 = du2 + lax.dot_general(dupj, wu_ref[j], (((1,), (1,)), ((), ())), preferred_element_type=F32)
        d_scale_f = jnp.sum(du2 * (n2 * g_mlp), axis=0, keepdims=True)
        d_shift_f = jnp.sum(du2, axis=0, keepdims=True)
        d_gmlp = jnp.sum(du2 * (1.0 + scale_f) * n2, axis=0, keepdims=True)
        dn2 = du2 * (g_mlp * (1.0 + scale_f))
        dh1 = dh2 + r2 * (dn2 - n2 * jnp.mean(dn2 * n2, axis=-1, keepdims=True))
        dh1_ref[...] = dh1
        d_gate_m = jnp.sum(dh1 * a, axis=0, keepdims=True)
        da = (gate_m * dh1).astype(BF16)
        da_ref[...] = da
        dym_ref[...] = lax.dot_general(da, wo_ref[...], (((1,), (1,)), ((), ())),
                                       preferred_element_type=F32).astype(BF16)
        dmod_ref[0] += jnp.concatenate([jnp.zeros((2, D), F32), d_gate_m, d_shift_f, d_scale_f, d_gate_f,
                                        jnp.zeros((2, D), F32)], axis=0)
        acc_ref[...] += jnp.concatenate([d_gmlp, d_gfin, loss * jnp.ones((1, D), F32), jnp.zeros((5, D), F32)], axis=0)

    whole = pl.BlockSpec(memory_space=pltpu.VMEM)

    def tok(w):
        return pl.BlockSpec((tm, w), lambda i: (i, 0))

    def vec():
        return pl.BlockSpec((1, D), lambda i: (0, 0))

    nb = T // seq
    return pl.pallas_call(
        body, name="mlp_fused", grid=(T // tm,),
        in_specs=[tok(D), tok(MIX_W), tok(D), pl.BlockSpec((1, 8, D), lambda i: (i // tps, 0, 0)), vec(), vec(),
                  whole, ANY_SPEC, ANY_SPEC],
        out_specs=[tok(D), tok(MIX_W), tok(D), tok(D), tok(D_FF), tok(D_FF), tok(D),
                   pl.BlockSpec((1, 8, D), lambda i: (i // tps, 0, 0)), pl.BlockSpec((8, D), lambda i: (0, 0))],
        out_shape=[jax.ShapeDtypeStruct((T, D), BF16), jax.ShapeDtypeStruct((T, MIX_W), BF16),
                   jax.ShapeDtypeStruct((T, D), F32), jax.ShapeDtypeStruct((T, D), BF16),
                   jax.ShapeDtypeStruct((T, D_FF), BF16), jax.ShapeDtypeStruct((T, D_FF), BF16),
                   jax.ShapeDtypeStruct((T, D), BF16), jax.ShapeDtypeStruct((nb, 8, D), F32),
                   jax.ShapeDtypeStruct((8, D), F32)],
        scratch_shapes=[pltpu.VMEM((tm, D_FF), F32), pltpu.VMEM(w_up.shape, BF16), pltpu.VMEM(w_down.shape, BF16),
                        pltpu.SemaphoreType.DMA((2,))],
        compiler_params=_cparams(),
    )(x2, ymix, target, mod, g_mlp, g_final, w_out, w_up, w_down)


def _in_bwd(x2, dh1, dpb, mod, g_mix, w_cat, dmod_a, acc_a, seq):
    T = x2.shape[0]
    tm = min(1024, seq)
    tps = seq // tm
    steps = T // tm

    def body(x_ref, dh_ref, dpb_ref, mod_ref, g_ref, w_ref, dma_ref, acca_ref, dx_ref, dmod_ref, acc_ref):
        i = pl.program_id(0)

        @pl.when(i == 0)
        def _():
            acc_ref[...] = acca_ref[...]

        @pl.when(i % tps == 0)
        def _():
            dmod_ref[...] = dma_ref[...]

        du = lax.dot_general(dpb_ref[...], w_ref[...], (((1,), (1,)), ((), ())), preferred_element_type=F32)
        x = x_ref[...]
        md = mod_ref[0]
        g = g_ref[...]
        r = lax.rsqrt(jnp.mean(x * x, axis=-1, keepdims=True) + EPS)
        n1 = x * r
        d_scale = jnp.sum(du * (n1 * g), axis=0, keepdims=True)
        d_shift = jnp.sum(du, axis=0, keepdims=True)
        d_g = jnp.sum(du * (1.0 + md[1:2]) * n1, axis=0, keepdims=True)
        dn1 = du * (g * (1.0 + md[1:2]))
        dx_ref[...] = dh_ref[...] + r * (dn1 - n1 * jnp.mean(dn1 * n1, axis=-1, keepdims=True))
        dmod_ref[0] += jnp.concatenate([d_shift, d_scale, jnp.zeros((6, D), F32)], axis=0)
        acc_ref[...] += jnp.concatenate([jnp.zeros((3, D), F32), d_g, jnp.zeros((4, D), F32)], axis=0)

    whole = pl.BlockSpec(memory_space=pltpu.VMEM)
    nb = T // seq
    return pl.pallas_call(
        body, name="in_bwd", grid=(steps,),
        in_specs=[pl.BlockSpec((tm, D), lambda i: (i, 0)), pl.BlockSpec((tm, D), lambda i: (i, 0)),
                  pl.BlockSpec((tm, PROJ_W), lambda i: (i, 0)),
                  pl.BlockSpec((1, 8, D), lambda i: (i // tps, 0, 0)), pl.BlockSpec((1, D), lambda i: (0, 0)),
                  whole, pl.BlockSpec((1, 8, D), lambda i: (i // tps, 0, 0)), pl.BlockSpec((8, D), lambda i: (0, 0))],
        out_specs=[pl.BlockSpec((tm, D), lambda i: (i, 0)),
                   pl.BlockSpec((1, 8, D), lambda i: (i // tps, 0, 0)), pl.BlockSpec((8, D), lambda i: (0, 0))],
        out_shape=[jax.ShapeDtypeStruct((T, D), F32),
                   jax.ShapeDtypeStruct((nb, 8, D), F32), jax.ShapeDtypeStruct((8, D), F32)],
        compiler_params=_cparams(),
    )(x2, dh1, dpb, mod, g_mix, w_cat, dmod_a, acc_a)


def _dw_in(u_b, dpb, in_cols, shards):
    T = u_b.shape[0]
    bk = min(512, T)
    nk = T // bk
    ns = len(shards)
    starts = [(in_cols * j // LANES) * LANES for j in range(N_DEV)]
    assert all(s + DW_IN_WIN <= PROJ_W and in_cols * (j + 1) <= s + DW_IN_WIN for j, s in enumerate(starts))

    def body(*refs):
        u_ref, d_ref = refs[:2]
        sh_refs = refs[2:2 + ns]
        o_ref = refs[2 + ns]
        ga_refs = refs[3 + ns:3 + 2 * ns]
        acc, send, recv, loc = refs[3 + 2 * ns:]
        k = pl.program_id(0)
        gather = _Gather(sh_refs, ga_refs, send, recv, loc)
        gather.begin_hosted(k, nk)

        @pl.when(k == 0)
        def _():
            acc[...] = jnp.zeros_like(acc)

        ut = u_ref[...].T
        for j in range(N_DEV):
            acc[j] += jnp.dot(ut, d_ref[:, starts[j]:starts[j] + DW_IN_WIN], preferred_element_type=F32)

        @pl.when(k == nk - 1)
        def _():
            for j in range(N_DEV):
                off = in_cols * j - starts[j]
                o_ref[j] = acc[j][:, off:off + in_cols].astype(BF16)

        gather.end_hosted(k, nk)

    return pl.pallas_call(
        body, name="dw_in", grid=(nk,),
        in_specs=[pl.BlockSpec((bk, D), lambda k: (k, 0)), pl.BlockSpec((bk, PROJ_W), lambda k: (k, 0))]
        + [ANY_SPEC] * ns,
        out_specs=[pl.BlockSpec((N_DEV, D, in_cols), lambda k: (0, 0, 0))] + [ANY_SPEC] * ns,
        out_shape=[jax.ShapeDtypeStruct((N_DEV, D, in_cols), BF16)]
        + [jax.ShapeDtypeStruct((N_DEV,) + v.shape, v.dtype) for v in shards],
        scratch_shapes=[pltpu.VMEM((N_DEV, D, DW_IN_WIN), F32)] + _gather_scratch(shards),
        compiler_params=_cparams(),
    )(u_b, dpb, *shards)


def _dw_blocks(a, b, name, by_rows, per_step=1, after=()):
    T, M = a.shape
    N = b.shape[1]
    bk = min(2048, T)
    nk = T // bk
    whole = pl.BlockSpec(memory_space=pltpu.VMEM)
    if by_rows:
        rows = M // N_DEV
        am = rows * per_step
        nblk = N_DEV // per_step
        a_spec, b_spec = pl.BlockSpec((bk, am), lambda i, k: (k, i)), whole
        out_blk, acc_shape = (per_step, rows, N), (am, N)
    else:
        cols = N // N_DEV
        nblk = N_DEV
        a_spec, b_spec = whole, pl.BlockSpec((bk, cols), lambda i, k: (k, i))
        out_blk, acc_shape = (1, M, cols), (M, cols)

    def body(a_ref, b_ref, *rest):
        o_ref, acc = rest[len(after):]
        k = pl.program_id(1)

        @pl.when(k == 0)
        def _():
            acc[...] = jnp.zeros_like(acc)

        tok = pl.ds(pl.multiple_of(k * bk, bk), bk)
        a_blk = a_ref[...] if by_rows else a_ref[tok, :]
        b_blk = b_ref[tok, :] if by_rows else b_ref[...]
        acc[...] += lax.dot_general(a_blk, b_blk, (((0,), (0,)), ((), ())), preferred_element_type=F32)

        @pl.when(k == nk - 1)
        def _():
            o_ref[...] = acc[...].reshape(out_blk).astype(BF16)

    return pl.pallas_call(
        body, name=name, grid=(nblk, nk), in_specs=[a_spec, b_spec] + [ANY_SPEC] * len(after),
        out_specs=pl.BlockSpec(out_blk, lambda i, k: (i, 0, 0)),
        out_shape=jax.ShapeDtypeStruct((N_DEV,) + out_blk[1:], BF16),
        scratch_shapes=[pltpu.VMEM(acc_shape, F32)],
        compiler_params=_cparams(),
    )(a, b, *after)


def _adam_parts(parts, w, m, v, name):
    rows, cols = w.shape
    br = rows
    for cand in range(rows, 15, -16):
        if rows % cand == 0 and cand * cols * 4 <= ADAM_BLOCK_BYTES:
            br = cand
            break

    def body(p_ref, w_ref, m_ref, v_ref, g_out, dl_out, m_out, v_out):
        g = p_ref[0].astype(F32)
        for k in range(1, N_DEV):
            g = g + p_ref[k].astype(F32)
        g_out[...] = g
        dl, mn, vn = _adam_math(w_ref[...], g, m_ref[...], v_ref[...])
        dl_out[...] = dl
        m_out[...] = mn
        v_out[...] = vn

    wspec = pl.BlockSpec((br, cols), lambda i: (i, 0))
    return pl.pallas_call(
        body, name=name, grid=(rows // br,),
        in_specs=[pl.BlockSpec((N_DEV, br, cols), lambda i: (0, i, 0)), wspec, wspec, wspec],
        out_specs=[wspec] * 4, out_shape=[jax.ShapeDtypeStruct((rows, cols), F32)] * 4,
        compiler_params=_cparams(),
    )(parts, w, m, v)


def _adam_plain(g, w, m, v, name):
    def body(g_ref, w_ref, m_ref, v_ref, dl_out, m_out, v_out):
        dl, mn, vn = _adam_math(w_ref[...], g_ref[...], m_ref[...], v_ref[...])
        dl_out[...] = dl
        m_out[...] = mn
        v_out[...] = vn

    return pl.pallas_call(body, name=name, out_shape=[jax.ShapeDtypeStruct(w.shape, F32)] * 3,
                          compiler_params=_cparams())(g, w, m, v)


SMALL_PARAMS = ("b_ada", "g_mix", "conv_b", "dt_bias", "a_log", "d_skip", "g_ssd", "pool_scale", "g_mlp", "g_final")


def _small_adam(gathered, params):
    n_par = len(SMALL_PARAMS)
    nb = gathered[0].shape[1]

    def body(*refs):
        dmod_ref, acc_ref, conv_ref, vec_ref, hd_ref = refs[:5]
        par_refs = refs[5:5 + 3 * n_par]
        out_refs = refs[5 + 3 * n_par:5 + 7 * n_par]
        cw_out, acc_out = refs[5 + 7 * n_par:]

        def total(ref):
            t = ref[0]
            for k in range(1, N_DEV):
                t = t + ref[k]
            return t

        dm = total(dmod_ref)
        dmb = dm[0]
        for b in range(1, nb):
            dmb = dmb + dm[b]
        ac, cv, vc, hd = total(acc_ref), total(conv_ref), total(vec_ref), total(hd_ref)
        cw_out[...] = cv[0:4]
        acc_out[...] = ac
        grads = {
            "b_ada": jnp.concatenate([dmb[r:r + 1] for r in range(6)], axis=1), "g_mix": ac[3:4], "conv_b": cv[4:5],
            "dt_bias": hd[0:1, 0:N_HEADS], "a_log": hd[1:2, 0:N_HEADS], "d_skip": hd[2:3, 0:N_HEADS],
            "g_ssd": vc[0:1], "pool_scale": vc[1:2, 0:POOL_W], "g_mlp": ac[0:1], "g_final": ac[1:2],
        }
        for i, name in enumerate(SMALL_PARAMS):
            w_ref, m_ref, v_ref = par_refs[3 * i:3 * i + 3]
            g = grads[name]
            dl, mn, vn = _adam_math(w_ref[...], g, m_ref[...], v_ref[...])
            g_o, d_o, m_o, v_o = out_refs[4 * i:4 * i + 4]
            g_o[...] = g
            d_o[...] = dl
            m_o[...] = mn
            v_o[...] = vn

    flat = [a for name in SMALL_PARAMS for a in params[name]]
    out_shape = [jax.ShapeDtypeStruct(params[name][0].shape, F32) for name in SMALL_PARAMS for _ in range(4)]
    out_shape += [jax.ShapeDtypeStruct((4, CONV_CH), F32), jax.ShapeDtypeStruct((8, D), F32)]
    return pl.pallas_call(body, name="small_adam", out_shape=out_shape, compiler_params=_cparams())(*gathered, *flat)


def kernel(x, c, w_ada, b_ada, g_mix, w_in, conv_w, conv_b, dt_bias, a_log, d_skip, g_ssd, w_pool, pool_scale, w_out, g_mlp, w_up, w_down, g_final, loss_target, m_w_ada, m_b_ada, m_g_mix, m_w_in, m_conv_w, m_conv_b, m_dt_bias, m_a_log, m_d_skip, m_g_ssd, m_w_pool, m_pool_scale, m_w_out, m_g_mlp, m_w_up, m_w_down, m_g_final, v_w_ada, v_b_ada, v_g_mix, v_w_in, v_conv_w, v_conv_b, v_dt_bias, v_a_log, v_d_skip, v_g_ssd, v_w_pool, v_pool_scale, v_w_out, v_g_mlp, v_w_up, v_w_down, v_g_final):
    nb, seq, _ = x.shape
    T = nb * seq
    me = 4 * lax.axis_index("x") + 2 * lax.axis_index("y") + lax.axis_index("c")
    in_cols = w_in.shape[2]
    ada_cols = w_ada.shape[2]
    cw_cols = conv_w.shape[2]

    win_t = jnp.pad(w_in[0].astype(BF16).T, ((0, -in_cols % 16), (0, 0)))
    c_g, cw_g, win_g = _all_gather([c, conv_w[0], win_t], "ag_first")
    c_all = c_g.reshape(N_DEV * nb, D)
    cw_full = cw_g.transpose(1, 0, 2).reshape(4, CONV_CH)

    b_slice = lax.dynamic_slice(b_ada, (0, me * ada_cols), (1, ada_cols))
    mod_cols = _ada_fwd(c_all, w_ada[0], b_slice)
    (mod_g,) = _all_gather([mod_cols], "ag_mod")
    mod_all = mod_g.transpose(1, 0, 2).reshape(N_DEV * nb, 6, D)
    mod_mine = lax.dynamic_slice(mod_all, (me * nb, 0, 0), (nb, 6, D))
    mod = jnp.pad(mod_mine, ((0, 0), (0, 2), (0, 0)))

    x2 = x.reshape(T, D)
    tg2 = loss_target.reshape(T, D)
    heads = jnp.pad(jnp.concatenate([dt_bias, a_log, d_skip], axis=0), ((0, 5), (0, LANES - N_HEADS)))
    wpool_b = w_pool[0]
    u_b, pm, w_cat, wup_g = _mix_in(x2, mod, g_mix, win_g, in_cols, seq, [w_up[0].astype(BF16)])
    ymix, hstates, cvs, wout_g, wdn_g = _mixer_fwd(
        pm, cw_full, conv_b, heads, g_ssd, wpool_b, pool_scale, nb, seq,
        [w_out[0].astype(BF16), w_down[0].astype(BF16)])
    da_b, dym, dh1, u2_b, f_b, dup_b, ddn_b, dmod_a, acc_a = _mlp_fused(
        x2, ymix, tg2, mod, g_mlp, g_final.reshape(1, D), wout_g.reshape(MIX_W, D), wup_g, wdn_g, seq)

    gout_p = _dw_blocks(ymix, da_b, "dw_out", True, per_step=4)
    gup_p = _dw_blocks(u2_b, dup_b, "dw_up", False)
    ex_a = _exchange_start([gout_p, gup_p], "ga_start")
    gdn_p = _dw_blocks(f_b, ddn_b, "dw_down", True, after=[ex_a[4]])
    ex_b = _exchange_start([gdn_p], "gb_start")
    dpb, d_conv, d_heads, d_vec, d_wpool = _mixer_bwd(
        pm, cvs, dym, hstates, cw_full, conv_b, heads, g_ssd, wpool_b, pool_scale, nb, seq, after=[ex_b[4]])
    gin_p, conv_g, vec_g, heads_g, wpool_parts = _dw_in(
        u_b, dpb, in_cols, [d_conv, d_vec, d_heads, d_wpool.reshape(4 * LANES, LANES)])
    ex_in = _exchange_start([gin_p], "gin_start")
    grad_x2, dmod, acc = _in_bwd(x2, dh1, dpb, mod, g_mix + ex_in[4][0:1, 0:1], w_cat, dmod_a, acc_a, seq)

    gout_r, gup_r = _exchange_wait(ex_a, dmod, "ga_wait", me)
    (gdn_r,) = _exchange_wait(ex_b, dmod, "gb_wait", me)
    g_out, d_out, nm_out, nv_out = _adam_parts(gout_r, w_out[0], m_w_out[0], v_w_out[0], "adam_w_out")
    g_up, d_up, nm_up, nv_up = _adam_parts(gup_r, w_up[0], m_w_up[0], v_w_up[0], "adam_w_up")
    g_dn, d_dn, nm_dn, nv_dn = _adam_parts(gdn_r, w_down[0], m_w_down[0], v_w_down[0], "adam_w_down")

    dmod_g, acc_g = _all_gather([dmod, acc], "ag_small_bwd", after=[nm_out, nm_up, nm_dn])
    pool2 = (4 * LANES, LANES)
    wpool_outs = _adam_parts(wpool_parts, w_pool.reshape(pool2), m_w_pool.reshape(pool2), v_w_pool.reshape(pool2),
                             "adam_w_pool")
    small_params = {
        "b_ada": (b_ada, m_b_ada, v_b_ada), "g_mix": (g_mix, m_g_mix, v_g_mix), "conv_b": (conv_b, m_conv_b, v_conv_b),
        "dt_bias": (dt_bias, m_dt_bias, v_dt_bias), "a_log": (a_log, m_a_log, v_a_log),
        "d_skip": (d_skip, m_d_skip, v_d_skip), "g_ssd": (g_ssd, m_g_ssd, v_g_ssd),
        "pool_scale": (pool_scale, m_pool_scale, v_pool_scale), "g_mlp": (g_mlp, m_g_mlp, v_g_mlp),
        "g_final": tuple(a.reshape(1, D) for a in (g_final, m_g_final, v_g_final)),
    }
    small_res = _small_adam([dmod_g, acc_g, conv_g, vec_g, heads_g], small_params)
    g_cw_full, acc_sum = small_res[-2:]
    loss = acc_sum[2, 0]

    g_cw = lax.dynamic_slice(g_cw_full, (0, me * cw_cols), (4, cw_cols))
    d_cwp, nm_cwp, nv_cwp = _adam_plain(g_cw, conv_w[0], m_conv_w[0], v_conv_w[0], "adam_conv_w")

    dmod_all = dmod_g[:, :, 0:6].reshape(N_DEV * nb, 6 * D)
    dmod_slice = lax.dynamic_slice(dmod_all, (0, me * ada_cols), (N_DEV * nb, ada_cols))
    g_ada, d_ada, nm_ada, nv_ada = _ada_bwd_adam(c_all, dmod_slice, w_ada[0], m_w_ada[0], v_w_ada[0])

    ex_after = nm_ada[0:8, 0:LANES] + acc_sum[:, 0:LANES]
    (gin_r,) = _exchange_wait(ex_in, ex_after, "gin_wait", me)
    g_in, d_in, nm_in, nv_in = _adam_parts(gin_r, w_in[0], m_w_in[0], v_w_in[0], "adam_w_in")

    def small_outs(kind, wpool):
        res = {name: small_res[4 * i + kind] for i, name in enumerate(SMALL_PARAMS)}
        res["g_final"] = res["g_final"].reshape(D)
        res["w_pool"] = wpool.reshape(1, 4, LANES, LANES)
        return res

    def big_outs(ada, win, cwp, wout, wup, wdn):
        return {"w_ada": ada[None], "w_in": win.reshape(1, D, in_cols), "conv_w": cwp[None], "w_out": wout[None],
                "w_up": wup[None], "w_down": wdn[None]}

    order = ["w_ada", "b_ada", "g_mix", "w_in", "conv_w", "conv_b", "dt_bias", "a_log", "d_skip", "g_ssd", "w_pool",
             "pool_scale", "w_out", "g_mlp", "w_up", "w_down", "g_final"]
    groups = [
        {**small_outs(0, wpool_outs[0]), **big_outs(g_ada, g_in, g_cw, g_out, g_up, g_dn)},
        {**small_outs(1, wpool_outs[1]), **big_outs(d_ada, d_in, d_cwp, d_out, d_up, d_dn)},
        {**small_outs(2, wpool_outs[2]), **big_outs(nm_ada, nm_in, nm_cwp, nm_out, nm_up, nm_dn)},
        {**small_outs(3, wpool_outs[3]), **big_outs(nv_ada, nv_in, nv_cwp, nv_out, nv_up, nv_dn)},
    ]
    outs = [loss, grad_x2.reshape(nb, seq, D)]
    for grp in groups:
        outs += [grp[n] for n in order]
    return tuple(outs)
```

```python
import functools

import jax
import jax.numpy as jnp
from jax import lax
from jax.experimental import pallas as pl
from jax.experimental.pallas import tpu as pltpu

F32, BF16 = jnp.float32, jnp.bfloat16
MESH = pl.DeviceIdType.MESH
N_DEV = 8
D = 1024
LANES = 128
CHUNK = 128
POOL_W = 512
WINDOWS = (2, 4, 8, 16)
N_HEADS = 16
HEAD_DIM = 64
N_GROUPS = 2
GROUP_W = 512
N_STATE = 128
CONV_CH = 1536
OFF_Z, OFF_XBC, OFF_DT, IN_W = 512, 1536, 3072, 3088
PROJ_W = OFF_DT + LANES
MIX_W = 1536
D_FF = 4096
FF_BLK = 512
EPS = 1e-5
LR, B1, B2, AEPS, WD, STEP = 0.001, 0.9, 0.999, 1e-08, 0.01, 10
POOL_HALO = 16
CONV_HALO = 8
VMEM_LIMIT = 56 << 20
ADAM_BLOCK_BYTES = 1 << 20
DW_IN_WIN = 512


def _cparams(**kw):
    return pltpu.CompilerParams(vmem_limit_bytes=VMEM_LIMIT, **kw)


def _mm(a, b):
    return jnp.dot(a.astype(BF16), b.astype(BF16), preferred_element_type=F32)


def _mm_nt(a, b):
    return lax.dot_general(a.astype(BF16), b.astype(BF16), (((1,), (1,)), ((), ())), preferred_element_type=F32)


def _mm_tn(a, b):
    return lax.dot_general(a.astype(BF16), b.astype(BF16), (((0,), (0,)), ((), ())), preferred_element_type=F32)


def _split_bf16(v, terms):
    parts, rest = [], v
    for t in range(terms):
        p = rest.astype(BF16)
        parts.append(p)
        if t + 1 < terms:
            rest = rest - p.astype(F32)
    return parts


def _dot01(a, b, terms, split_lhs=True):
    if split_lhs:
        bb = b.astype(BF16)
        prods = [jnp.dot(p, bb, preferred_element_type=F32) for p in _split_bf16(a, terms)]
    else:
        ab = a.astype(BF16)
        prods = [jnp.dot(ab, p, preferred_element_type=F32) for p in _split_bf16(b, terms)]
    out = prods[0]
    for q in prods[1:]:
        out = out + q
    return out


def _sigmoid(v):
    return 1.0 / (1.0 + jnp.exp(-v))


def _expand_mat():
    r = lax.broadcasted_iota(jnp.int32, (LANES, D), 0)
    c = lax.broadcasted_iota(jnp.int32, (LANES, D), 1)
    return (r == c // HEAD_DIM).astype(F32)


def _reduce_mat():
    r = lax.broadcasted_iota(jnp.int32, (D, LANES), 0)
    c = lax.broadcasted_iota(jnp.int32, (D, LANES), 1)
    return (c == r // HEAD_DIM).astype(F32)


def _pos():
    return lax.axis_index("x"), lax.axis_index("y"), lax.axis_index("c")


GATHER_PIECES = 4
GATHER_PIECE_BYTES = 96 << 10


def _pieces(shape, dtype):
    rows = shape[0]
    size = jnp.dtype(dtype).itemsize
    for d in shape:
        size *= d
    whole_tiles = rows % (GATHER_PIECES * 16) == 0
    return GATHER_PIECES if whole_tiles and size // GATHER_PIECES >= GATHER_PIECE_BYTES else 1


class _Gather:
    def __init__(self, x_refs, o_refs, send, recv, loc):
        self.x_refs, self.o_refs, self.send, self.recv, self.loc = x_refs, o_refs, send, recv, loc
        self.n = len(x_refs)
        self.pieces = [_pieces(r.shape, r.dtype) for r in x_refs]
        self.base = [7 * sum(self.pieces[:a]) for a in range(self.n)]
        x, y, c = _pos()
        self.c = c
        self.me, self.sib = (x, y, c), (x, y, 1 - c)
        self.chips = [(1 - x, y), (x, 1 - y), (1 - x, 1 - y)]

    def _rows(self, a, p):
        rows = self.x_refs[a].shape[0] // self.pieces[a]
        return pl.ds(p * rows, rows)

    def _cp(self, a, p, k, block, to, own=False):
        dst = self.o_refs[a].at[4 * block[0] + 2 * block[1] + block[2], self._rows(a, p)]
        sem = self.base[a] + 7 * p + k
        return pltpu.make_async_remote_copy(
            src_ref=self.x_refs[a].at[self._rows(a, p)] if own else dst, dst_ref=dst,
            send_sem=self.send.at[sem], recv_sem=self.recv.at[sem], device_id=to, device_id_type=MESH)

    def _mine(self, a):
        me = self.me
        return pltpu.make_async_copy(self.x_refs[a], self.o_refs[a].at[4 * me[0] + 2 * me[1] + me[2]], self.loc.at[a])

    def _first(self, a, p):
        cps = [self._cp(a, p, 0, self.me, self.sib, own=True)]
        return cps + [self._cp(a, p, 1 + j, self.me, (*chip, self.c), own=True) for j, chip in enumerate(self.chips)]

    def _passed(self, a, p, j):
        return self._cp(a, p, 4 + j, (*self.chips[j], self.c), self.sib)

    def start(self):
        for a in range(self.n):
            self._mine(a).start()
        for p in range(max(self.pieces)):
            for a in range(self.n):
                if p < self.pieces[a]:
                    for cp in self._first(a, p):
                        cp.start()

    def forward(self, p):
        for j, chip in enumerate(self.chips):
            for a in range(self.n):
                if p < self.pieces[a]:
                    self._cp(a, p, 1 + j, (*chip, self.c), self.me).wait_recv()
                    self._passed(a, p, j).start()

    def finish(self):
        for a in range(self.n):
            for p in range(self.pieces[a]):
                self._cp(a, p, 0, self.sib, self.me).wait_recv()
                for j, chip in enumerate(self.chips):
                    self._cp(a, p, 4 + j, (*chip, 1 - self.c), self.me).wait_recv()
        for a in range(self.n):
            for p in range(self.pieces[a]):
                for cp in self._first(a, p):
                    cp.wait_send()
                for j in range(3):
                    self._passed(a, p, j).wait_send()
            self._mine(a).wait()

    def begin_hosted(self, step, steps):
        @pl.when(step == 0)
        def _():
            self.start()

        n_p = max(self.pieces)
        for p in range(n_p):
            @pl.when(step == min(((p + 1) * 7 * steps) // (8 * n_p), steps - 1))
            def _():
                self.forward(p)

    def end_hosted(self, step, steps):
        @pl.when(step == steps - 1)
        def _():
            self.finish()


class _Exchange:
    def __init__(self, x_refs, o_refs, send, recv, loc):
        self.x_refs, self.o_refs, self.send, self.recv, self.loc = x_refs, o_refs, send, recv, loc
        self.n = len(x_refs)
        x, y, c = _pos()
        self.me_i = 4 * x + 2 * y + c
        self.peers = []
        for k in range(1, N_DEV):
            px = 1 - x if (k >> 2) & 1 else x
            py = 1 - y if (k >> 1) & 1 else y
            pc = 1 - c if k & 1 else c
            self.peers.append(((px, py, pc), 4 * px + 2 * py + pc))

    def _mine(self, a):
        return pltpu.make_async_copy(self.x_refs[a].at[self.me_i], self.o_refs[a].at[self.me_i], self.loc.at[a])

    def _cp(self, a, k, landing):
        peer, peer_i = self.peers[k]
        return pltpu.make_async_remote_copy(
            src_ref=self.x_refs[a].at[peer_i], dst_ref=self.o_refs[a].at[landing],
            send_sem=self.send.at[a * 7 + k], recv_sem=self.recv.at[a * 7 + k],
            device_id=peer, device_id_type=MESH)

    def start(self):
        for a in range(self.n):
            self._mine(a).start()
            for k in range(N_DEV - 1):
                self._cp(a, k, self.me_i).start()

    def finish(self):
        for a in range(self.n):
            for k in range(N_DEV - 1):
                self._cp(a, k, self.peers[k][1]).wait_recv()
        for a in range(self.n):
            for k in range(N_DEV - 1):
                self._cp(a, k, self.me_i).wait_send()
            self._mine(a).wait()


def _gather_scratch(xs):
    n_sem = 7 * sum(_pieces(v.shape, v.dtype) for v in xs)
    return [pltpu.SemaphoreType.DMA((n_sem,)), pltpu.SemaphoreType.DMA((n_sem,)), pltpu.SemaphoreType.DMA((len(xs),))]


ANY_SPEC = pl.BlockSpec(memory_space=pl.ANY)


def _all_gather(xs, name, after=()):
    n, na = len(xs), len(after)

    def body(*refs):
        g = _Gather(refs[:n], refs[n + na:2 * n + na], *refs[2 * n + na:])
        g.start()
        for p in range(max(g.pieces)):
            g.forward(p)
        g.finish()

    return pl.pallas_call(
        body, name=name,
        out_shape=[jax.ShapeDtypeStruct((N_DEV,) + v.shape, v.dtype) for v in xs],
        in_specs=[ANY_SPEC] * (n + na), out_specs=[ANY_SPEC] * n, scratch_shapes=_gather_scratch(xs),
    )(*xs, *after)


HBM_SPEC = pl.BlockSpec(memory_space=pltpu.HBM)
SEM_SPEC = pl.BlockSpec(memory_space=pltpu.SEMAPHORE)
VMEM_SPEC = pl.BlockSpec(memory_space=pltpu.VMEM)
SPLIT_EFFECT = pltpu.SideEffectType.DATAFLOW_SIDE_EFFECTING


def _in_hbm(v):
    return pltpu.with_memory_space_constraint(v, pltpu.HBM)


def _exchange_start(blocks, name):
    n = len(blocks)

    def body(*refs):
        x_refs, land_refs = refs[:n], refs[n:2 * n]
        send, recv = refs[2 * n:2 * n + 2]
        token = refs[-1]
        ex = _Exchange(x_refs, land_refs, send, recv, None)
        for a in range(n):
            for k in range(N_DEV - 1):
                ex._cp(a, k, ex.me_i).start()
        token[...] = jnp.zeros_like(token)

    lands = [lax.empty(v.shape, v.dtype) for v in blocks]
    hbm = tuple(pltpu.HBM(v.shape, v.dtype) for v in list(blocks) + lands)
    n_sem = (N_DEV - 1) * n
    out = pl.pallas_call(
        body, name=name,
        out_shape=(pltpu.SemaphoreType.DMA((n_sem,)), pltpu.SemaphoreType.DMA((n_sem,))) + hbm
        + (jax.ShapeDtypeStruct((8, LANES), F32),),
        in_specs=(HBM_SPEC,) * (2 * n), out_specs=(SEM_SPEC, SEM_SPEC) + (HBM_SPEC,) * (2 * n) + (VMEM_SPEC,),
        input_output_aliases={i: i + 2 for i in range(2 * n)},
        compiler_params=pltpu.CompilerParams(has_side_effects=SPLIT_EFFECT),
    )(*[_in_hbm(v) for v in list(blocks) + lands])
    return out[0], out[1], list(out[2:2 + n]), list(out[2 + n:2 + 2 * n]), out[-1]


def _exchange_wait(ex, after, name, me):
    send, recv, thru, lands, _ = ex
    n = len(thru)

    def body(*refs):
        x_refs, land_refs = refs[:n], refs[n:2 * n]
        send_ref, recv_ref = refs[2 * n:2 * n + 2]
        e = _Exchange(x_refs, land_refs, send_ref, recv_ref, None)
        for a in range(n):
            for k in range(N_DEV - 1):
                e._cp(a, k, e.me_i).wait_send()
                e._cp(a, k, e.peers[k][1]).wait_recv()

    hbm = tuple(pltpu.HBM(v.shape, v.dtype) for v in list(thru) + list(lands))
    out = pl.pallas_call(
        body, name=name, out_shape=hbm,
        in_specs=(HBM_SPEC,) * (2 * n) + (SEM_SPEC, SEM_SPEC, ANY_SPEC), out_specs=(HBM_SPEC,) * (2 * n),
        input_output_aliases={i: i for i in range(2 * n)},
        compiler_params=pltpu.CompilerParams(has_side_effects=SPLIT_EFFECT),
    )(*thru, *lands, send, recv, after)
    done = []
    for own, land in zip(out[:n], out[n:]):
        mine = lax.dynamic_slice(own, (me,) + (0,) * (own.ndim - 1), (1,) + own.shape[1:])
        done.append(lax.dynamic_update_slice(land, mine, (me,) + (0,) * (own.ndim - 1)))
    return done


def _ada_fwd(c_all, w_ada, b_slice):
    def body(c_ref, w_ref, b_ref, o_ref):
        cv = c_ref[...]
        act = cv * _sigmoid(cv)
        o_ref[...] = _mm(act, w_ref[...]) + b_ref[...]

    nb, nc = c_all.shape[0], w_ada.shape[1]
    return pl.pallas_call(body, name="ada_fwd", out_shape=jax.ShapeDtypeStruct((nb, nc), F32),
                          compiler_params=_cparams())(c_all, w_ada, b_slice)


def _adam_math(w, g, m, v):
    m = B1 * m + (1.0 - B1) * g
    v = B2 * v + (1.0 - B2) * jnp.square(g)
    m_hat = m / (1.0 - B1 ** STEP)
    v_hat = v / (1.0 - B2 ** STEP)
    delta = -LR * (m_hat / (jnp.sqrt(v_hat) + AEPS) + WD * w)
    return delta, m, v


def _ada_bwd_adam(c_all, dmod_slice, w, m, v):
    rows, cols = w.shape
    br = 256

    def body(c_ref, d_ref, w_ref, m_ref, v_ref, g_out, dl_out, m_out, v_out):
        cv = c_ref[...]
        act = cv * _sigmoid(cv)
        g = _mm_tn(act, d_ref[...])
        g_out[...] = g
        dl, mn, vn = _adam_math(w_ref[...], g, m_ref[...], v_ref[...])
        dl_out[...] = dl
        m_out[...] = mn
        v_out[...] = vn

    nb = c_all.shape[0]
    wspec = pl.BlockSpec((br, cols), lambda i: (i, 0))
    return pl.pallas_call(
        body, name="ada_bwd_adam", grid=(rows // br,),
        in_specs=[pl.BlockSpec((nb, br), lambda i: (0, i)), pl.BlockSpec((nb, cols), lambda i: (0, 0)),
                  wspec, wspec, wspec],
        out_specs=[wspec] * 4, out_shape=[jax.ShapeDtypeStruct((rows, cols), F32)] * 4,
        compiler_params=_cparams(),
    )(c_all, dmod_slice, w, m, v)


def _mix_in(x2, mod, g_mix, win_g, in_cols, seq, shards):
    T = x2.shape[0]
    tm = min(512, seq)
    tps = seq // tm
    blk_rows = win_g.shape[1]
    pad_rows = -blk_rows % (4 * LANES)
    ns = len(shards)
    steps = T // tm

    def body(*refs):
        x_ref, mod_ref, g_ref, wb_ref = refs[:4]
        sh_refs = refs[4:4 + ns]
        u_ref, pm_ref, wc_ref = refs[4 + ns:7 + ns]
        ga_refs = refs[7 + ns:7 + 2 * ns]
        w_ref, send, recv, loc = refs[7 + 2 * ns:]
        step = pl.program_id(0)
        gather = _Gather(sh_refs, ga_refs, send, recv, loc)
        gather.begin_hosted(step, steps)

        @pl.when(step == 0)
        def _():
            w_ref[:, OFF_DT:] = jnp.zeros((D, PROJ_W - OFF_DT), BF16)
            for j in range(N_DEV):
                blk = jnp.concatenate([wb_ref[j], jnp.zeros((pad_rows, D), BF16)], axis=0)
                w_ref[:, in_cols * j:in_cols * (j + 1)] = blk.T[:, :in_cols]
            wc_ref[...] = w_ref[...]

        x = x_ref[...]
        r = lax.rsqrt(jnp.mean(x * x, axis=-1, keepdims=True) + EPS)
        md = mod_ref[0]
        u = (x * r * g_ref[...]) * (1.0 + md[1:2]) + md[0:1]
        ub = u.astype(BF16)
        u_ref[...] = ub
        pm_ref[...] = jnp.dot(ub, w_ref[...], preferred_element_type=F32)
        gather.end_hosted(step, steps)

    whole = pl.BlockSpec(memory_space=pltpu.VMEM)
    return pl.pallas_call(
        body, name="mix_in", grid=(T // tm,),
        in_specs=[pl.BlockSpec((tm, D), lambda i: (i, 0)), pl.BlockSpec((1, 8, D), lambda i: (i // tps, 0, 0)),
                  pl.BlockSpec((1, D), lambda i: (0, 0)), whole] + [ANY_SPEC] * ns,
        out_specs=[pl.BlockSpec((tm, D), lambda i: (i, 0)), pl.BlockSpec((tm, PROJ_W), lambda i: (i, 0)),
                   pl.BlockSpec((D, PROJ_W), lambda i: (0, 0))] + [ANY_SPEC] * ns,
        out_shape=[jax.ShapeDtypeStruct((T, D), BF16), jax.ShapeDtypeStruct((T, PROJ_W), F32),
                   jax.ShapeDtypeStruct((D, PROJ_W), BF16)]
        + [jax.ShapeDtypeStruct((N_DEV,) + v.shape, v.dtype) for v in shards],
        scratch_shapes=[pltpu.VMEM((D, PROJ_W), BF16)] + _gather_scratch(shards),
        compiler_params=_cparams(),
    )(x2, mod, g_mix, win_g, *shards)


def _chunk_forward(up, z, ux, dtin, halo_p, halo_x, hprev, cw, cb, hp, gssd, wpool, pscale, t0, y_scr, cv=None):
    L = CHUNK
    out = {}
    row = lax.broadcasted_iota(jnp.int32, (L, 1), 0)
    t = (t0 + row + 1).astype(F32)
    e = jnp.concatenate([halo_p, up], axis=0)
    s2 = e + pltpu.roll(e, 1, 0)
    s4 = s2 + pltpu.roll(s2, 2, 0)
    s8 = s4 + pltpu.roll(s4, 4, 0)
    s16 = s8 + pltpu.roll(s8, 8, 0)
    sums = (s2, s4, s8, s16)
    p, inv, yp = [], [], []
    for gi, w in enumerate(WINDOWS):
        sl = slice(gi * LANES, (gi + 1) * LANES)
        ic = 1.0 / jnp.minimum(t, float(w))
        pg = sums[gi][POOL_HALO:, sl] * ic - up[:, sl]
        p.append(pg)
        inv.append(ic)
        yp.append(_mm(pg, wpool[gi]))
    out["p"], out["inv"], out["yp"] = p, inv, yp
    out["y_pool"] = jnp.concatenate(yp, axis=1) * pscale
    if cv is None:
        ex = jnp.concatenate([halo_x, ux], axis=0)
        taps = [pltpu.roll(ex, 3, 0)[CONV_HALO:], pltpu.roll(ex, 2, 0)[CONV_HALO:], pltpu.roll(ex, 1, 0)[CONV_HALO:], ux]
        cv = cb + taps[0] * cw[0:1] + taps[1] * cw[1:2] + taps[2] * cw[2:3] + taps[3] * cw[3:4]
    sg = _sigmoid(cv)
    xbc = cv * sg
    out["cv"], out["sg"] = cv, sg
    X = xbc[:, :D]
    Bm = xbc[:, D:D + N_GROUPS * N_STATE]
    Cm = xbc[:, D + N_GROUPS * N_STATE:]
    pre = dtin + hp[0:1]
    dt = jnp.maximum(pre, 0.0) + jnp.log(1.0 + jnp.exp(-jnp.abs(pre)))
    a_row = -jnp.exp(hp[1:2])
    da = dt * a_row
    ri = lax.broadcasted_iota(jnp.int32, (L, L), 0)
    ci = lax.broadcasted_iota(jnp.int32, (L, L), 1)
    causal = ri >= ci
    cum = _dot01(causal.astype(F32), da, 3, split_lhs=False)
    cum_t = cum.T
    cum_last = cum[L - 1:L]
    eo = jnp.exp(cum)
    dec = jnp.exp(cum_last - cum)
    cd = jnp.exp(cum_last)
    exm = _expand_mat()
    rows8 = jnp.concatenate([cd, hp[2:3], jnp.zeros((6, LANES), F32)], axis=0)
    rep = _dot01(jnp.concatenate([dt, eo, dec, rows8], axis=0), exm, 2)
    dt_rep, eo_rep, dec_rep = rep[0:L], rep[L:2 * L], rep[2 * L:3 * L]
    cd_rep, dskip_rep = rep[3 * L:3 * L + 1], rep[3 * L + 1:3 * L + 2]
    xdt = X * dt_rep
    out.update(X=X, Bm=Bm, Cm=Cm, pre=pre, dt=dt, a_row=a_row, cum=cum, cum_t=cum_t, eo=eo, dec=dec, cd=cd,
               dt_rep=dt_rep, eo_rep=eo_rep, dec_rep=dec_rep, cd_rep=cd_rep, dskip_rep=dskip_rep, xdt=xdt,
               causal=causal, anti=(ri <= ci).astype(F32), exm=exm)
    G, lms, yoff, hnew, xdec = [], [], [], [], []
    for g in range(N_GROUPS):
        gs = slice(g * GROUP_W, (g + 1) * GROUP_W)
        Bg = Bm[:, g * N_STATE:(g + 1) * N_STATE]
        Cg = Cm[:, g * N_STATE:(g + 1) * N_STATE]
        Gg = _mm_nt(Cg, Bg)
        G.append(Gg)
        for hh in range(N_HEADS // N_GROUPS):
            h = g * (N_HEADS // N_GROUPS) + hh
            seg = cum[:, h:h + 1] - cum_t[h:h + 1, :]
            lm = jnp.where(causal, jnp.exp(jnp.minimum(seg, 0.0)), 0.0)
            lms.append(lm)
            hs = slice(h * HEAD_DIM, (h + 1) * HEAD_DIM)
            y_scr[:, hs] = _mm(Gg * lm, xdt[:, hs])
        xd = xdt[:, gs] * dec_rep[:, gs]
        xdec.append(xd)
        sgm = _mm_tn(Bg, xd)
        yoff.append(_mm(Cg, hprev[g]) * eo_rep[:, gs])
        hnew.append(hprev[g] * cd_rep[:, gs] + sgm)
    out.update(G=G, lms=lms, yoff=yoff, hnew=hnew, xdec=xdec)
    y = y_scr[...] + jnp.concatenate(yoff, axis=1) + dskip_rep * X
    sz = _sigmoid(z)
    silz = z * sz
    yz = y * silz
    rg, yn = [], []
    for g in range(N_GROUPS):
        gs = slice(g * GROUP_W, (g + 1) * GROUP_W)
        r = lax.rsqrt(jnp.mean(yz[:, gs] * yz[:, gs], axis=-1, keepdims=True) + EPS)
        rg.append(r)
        yn.append(yz[:, gs] * r)
    yn = jnp.concatenate(yn, axis=1)
    out.update(y=y, sz=sz, silz=silz, rg=rg, yn=yn)
    out["y_ssd"] = yn * gssd
    return out


def _mixer_fwd(pm, cw, cb, hp, gssd, wpool, pscale, nb, seq, shards):
    nc = seq // CHUNK
    ns = len(shards)
    steps = nb * nc

    def body(*refs):
        pm_ref, cw_ref, cb_ref, hp_ref, gs_ref, wp_ref, ps_ref = refs[:7]
        sh_refs = refs[7:7 + ns]
        ym_ref, hs_ref, cv_ref = refs[7 + ns:10 + ns]
        ga_refs = refs[10 + ns:10 + 2 * ns]
        halo_p, halo_x, state, y_scr, send, recv, loc = refs[10 + 2 * ns:]
        c = pl.program_id(1)
        step = pl.program_id(0) * nc + c
        gather = _Gather(sh_refs, ga_refs, send, recv, loc)
        gather.begin_hosted(step, steps)

        @pl.when(c == 0)
        def _():
            halo_p[...] = jnp.zeros_like(halo_p)
            halo_x[...] = jnp.zeros_like(halo_x)
            state[...] = jnp.zeros_like(state)

        up = pm_ref[:, 0:POOL_W]
        z = pm_ref[:, OFF_Z:OFF_XBC]
        ux = pm_ref[:, OFF_XBC:OFF_DT]
        hprev = [state[0], state[1]]
        hs_ref[0, 0, 0] = hprev[0]
        hs_ref[0, 0, 1] = hprev[1]
        o = _chunk_forward(up, z, ux, pm_ref[:, OFF_DT:], halo_p[...], halo_x[...], hprev, cw_ref[...], cb_ref[...],
                           hp_ref[...], gs_ref[...], wp_ref[...], ps_ref[...], c * CHUNK, y_scr)
        ym_ref[:, 0:POOL_W] = o["y_pool"].astype(BF16)
        ym_ref[:, POOL_W:] = o["y_ssd"].astype(BF16)
        cv_ref[...] = o["cv"]
        state[0] = o["hnew"][0]
        state[1] = o["hnew"][1]
        halo_p[...] = up[CHUNK - POOL_HALO:]
        halo_x[...] = ux[CHUNK - CONV_HALO:]
        gather.end_hosted(step, steps)

    def full(shape):
        return pl.BlockSpec(shape, lambda b, c: (0,) * len(shape))

    T = nb * seq
    return pl.pallas_call(
        body, name="mixer_fwd", grid=(nb, nc),
        in_specs=[pl.BlockSpec((CHUNK, PROJ_W), lambda b, c: (b * nc + c, 0)),
                  full((4, CONV_CH)), full((1, CONV_CH)), full((8, LANES)), full((1, D)),
                  full((4, LANES, LANES)), full((1, POOL_W))] + [ANY_SPEC] * ns,
        out_specs=[pl.BlockSpec((CHUNK, MIX_W), lambda b, c: (b * nc + c, 0)),
                   pl.BlockSpec((1, 1, N_GROUPS, N_STATE, GROUP_W), lambda b, c: (b, c, 0, 0, 0)),
                   pl.BlockSpec((CHUNK, CONV_CH), lambda b, c: (b * nc + c, 0))] + [ANY_SPEC] * ns,
        out_shape=[jax.ShapeDtypeStruct((T, MIX_W), BF16),
                   jax.ShapeDtypeStruct((nb, nc, N_GROUPS, N_STATE, GROUP_W), F32),
                   jax.ShapeDtypeStruct((T, CONV_CH), F32)]
        + [jax.ShapeDtypeStruct((N_DEV,) + v.shape, v.dtype) for v in shards],
        scratch_shapes=[pltpu.VMEM((POOL_HALO, POOL_W), F32), pltpu.VMEM((CONV_HALO, CONV_CH), F32),
                        pltpu.VMEM((N_GROUPS, N_STATE, GROUP_W), F32), pltpu.VMEM((CHUNK, D), F32)] + _gather_scratch(shards),
        compiler_params=_cparams(),
    )(pm, cw, cb, hp, gssd, wpool, pscale, *shards)


def _mixer_bwd(pm, cvs, dym, hstates, cw, cb, hp, gssd, wpool, pscale, nb, seq, after=()):
    nc = seq // CHUNK
    hpg = N_HEADS // N_GROUPS
    na = len(after)

    def body(*refs):
        (pm_ref, hpool_ref, cv_ref, dy_ref, hs_ref, cw_ref, cb_ref, hp_ref, gs_ref, wp_ref, ps_ref) = refs[:11]
        dpm_ref, dconv_ref, dhp_ref, dvec_ref, dwp_ref = refs[11 + na:16 + na]
        nxt_q, nxt_cv, rstate, y_scr, dx_scr = refs[16 + na:]
        b = pl.program_id(0)
        ci = pl.program_id(1)
        c = nc - 1 - ci

        @pl.when((b == 0) & (ci == 0))
        def _():
            for r in (dconv_ref, dhp_ref, dvec_ref, dwp_ref):
                r[...] = jnp.zeros_like(r)

        @pl.when(ci == 0)
        def _():
            nxt_q[...] = jnp.zeros_like(nxt_q)
            nxt_cv[...] = jnp.zeros_like(nxt_cv)
            rstate[...] = jnp.zeros_like(rstate)

        first = (c > 0).astype(F32)
        up = pm_ref[:, 0:POOL_W]
        z = pm_ref[:, OFF_Z:OFF_XBC]
        ux = pm_ref[:, OFF_XBC:OFF_DT]
        halo_p = hpool_ref[...] * first
        hprev = [hs_ref[0, 0, 0], hs_ref[0, 0, 1]]
        cw, cb, hp, gssd, wpool, pscale = cw_ref[...], cb_ref[...], hp_ref[...], gs_ref[...], wp_ref[...], ps_ref[...]
        o = _chunk_forward(up, z, ux, pm_ref[:, OFF_DT:], halo_p, None, hprev, cw, cb, hp, gssd, wpool, pscale,
                           c * CHUNK, y_scr, cv=cv_ref[...])
        L = CHUNK
        dy_pool = dy_ref[:, 0:POOL_W].astype(F32)
        dy_ssd = dy_ref[:, POOL_W:].astype(F32)

        dvec_ref[1:2, 0:POOL_W] += jnp.sum(dy_pool * jnp.concatenate(o["yp"], axis=1), axis=0, keepdims=True)
        dyp = dy_pool * pscale
        qs = []
        dps = []
        for gi in range(len(WINDOWS)):
            sl = slice(gi * LANES, (gi + 1) * LANES)
            dwp_ref[gi] += _mm_tn(o["p"][gi], dyp[:, sl])
            dpg = _mm_nt(dyp[:, sl], wpool[gi])
            dps.append(dpg)
            qs.append(dpg * o["inv"][gi])
        q = jnp.concatenate(qs, axis=1)
        e = jnp.concatenate([q, nxt_q[...]], axis=0)
        n = L + POOL_HALO
        s2 = e + pltpu.roll(e, n - 1, 0)
        s4 = s2 + pltpu.roll(s2, n - 2, 0)
        s8 = s4 + pltpu.roll(s4, n - 4, 0)
        s16 = s8 + pltpu.roll(s8, n - 8, 0)
        sums = (s2, s4, s8, s16)
        for gi in range(len(WINDOWS)):
            sl = slice(gi * LANES, (gi + 1) * LANES)
            dpm_ref[:, sl] = (sums[gi][:L, sl] - dps[gi]).astype(BF16)
        nxt_q[...] = q[:POOL_HALO]

        yn, y, silz, sz = o["yn"], o["y"], o["silz"], o["sz"]
        dvec_ref[0:1] += jnp.sum(dy_ssd * yn, axis=0, keepdims=True)
        dyn = dy_ssd * gssd
        dyz = []
        for g in range(N_GROUPS):
            gs = slice(g * GROUP_W, (g + 1) * GROUP_W)
            mean = jnp.mean(dyn[:, gs] * yn[:, gs], axis=-1, keepdims=True)
            dyz.append(o["rg"][g] * (dyn[:, gs] - yn[:, gs] * mean))
        dyz = jnp.concatenate(dyz, axis=1)
        dyv = dyz * silz
        dpm_ref[:, OFF_Z:OFF_XBC] = (dyz * y * (sz * (1.0 + z * (1.0 - sz)))).astype(BF16)

        X, Bm, Cm, xdt = o["X"], o["Bm"], o["Cm"], o["xdt"]
        exm = o["exm"]
        rdm = _reduce_mat()
        lane = lax.broadcasted_iota(jnp.int32, (1, LANES), 1)
        sub = lax.broadcasted_iota(jnp.int32, (LANES, 1), 0)
        dX = o["dskip_rep"] * dyv
        yoff_full = jnp.concatenate(o["yoff"], axis=1)
        rs = jnp.zeros((L, LANES), F32)
        cs_t = jnp.zeros((LANES, L), F32)
        dBs, dCs = [], []
        rh_sums = []
        ddec = []
        for g in range(N_GROUPS):
            gs = slice(g * GROUP_W, (g + 1) * GROUP_W)
            Bg = Bm[:, g * N_STATE:(g + 1) * N_STATE]
            Cg = Cm[:, g * N_STATE:(g + 1) * N_STATE]
            Gg = o["G"][g]
            R = rstate[g]
            dwm = dyv[:, gs] * o["eo_rep"][:, gs]
            dC = _mm_nt(dwm, hprev[g])
            dH = _mm_tn(Cg, dwm)
            dG = jnp.zeros((L, L), F32)
            for hh in range(hpg):
                h = g * hpg + hh
                hs = slice(h * HEAD_DIM, (h + 1) * HEAD_DIM)
                lm = o["lms"][h]
                m_h = Gg * lm
                dM = _mm_nt(dyv[:, hs], xdt[:, hs])
                dx_scr[:, hs] = _mm_tn(m_h, dyv[:, hs])
                qm = dM * m_h
                rs = rs + jnp.sum(qm, axis=1, keepdims=True) * (lane == h).astype(F32)
                cs_t = cs_t + (sub == h).astype(F32) * jnp.sum(qm, axis=0, keepdims=True)
                dG = dG + dM * lm
            dC = dC + _mm(dG, Bg)
            dB = _mm_tn(dG, Cg)
            zx = _mm(Bg, R)
            dxdt_state = zx * o["dec_rep"][:, gs]
            ddec.append(zx * xdt[:, gs])
            dB = dB + _mm_nt(o["xdec"][g], R)
            rh_sums.append(jnp.sum(R * hprev[g], axis=0, keepdims=True))
            rstate[g] = dH + o["cd_rep"][:, gs] * R
            dx_scr[:, gs] = dx_scr[:, gs] + dxdt_state
            dBs.append(dB)
            dCs.append(dC)
        dxdt = dx_scr[...]
        tail = jnp.concatenate([jnp.sum(dyv * X, axis=0, keepdims=True), jnp.concatenate(rh_sums, axis=1),
                                jnp.zeros((6, D), F32)], axis=0)
        red = _dot01(jnp.concatenate([dyv * yoff_full, jnp.concatenate(ddec, axis=1), dxdt * X, tail], axis=0), rdm, 2)
        d_dskip, dcd_row = red[3 * L:3 * L + 1], red[3 * L + 1:3 * L + 2]
        ddec_h = red[L:2 * L] * o["dec"]
        dcum_last = jnp.sum(ddec_h, axis=0, keepdims=True) + dcd_row * o["cd"]
        dcum = red[0:L] + rs - cs_t.T - ddec_h + (sub == L - 1).astype(F32) * dcum_last
        dda = _dot01(o["anti"], dcum, 3, split_lhs=False)
        ddt_v = dda * o["a_row"] + red[2 * L:3 * L]
        dX = dX + dxdt * o["dt_rep"]
        head_mask = (lane < N_HEADS).astype(F32)
        d_alog = jnp.sum(dda * o["dt"], axis=0, keepdims=True) * o["a_row"] * head_mask
        dpre = ddt_v * _sigmoid(o["pre"]) * head_mask
        dpm_ref[:, OFF_DT:] = dpre.astype(BF16)
        d_dtb = jnp.sum(dpre, axis=0, keepdims=True)
        dhp_ref[...] += jnp.concatenate([d_dtb, d_alog, d_dskip * head_mask, jnp.zeros((5, LANES), F32)], axis=0)

        dxbc = jnp.concatenate([dX] + dBs + dCs, axis=1)
        sg, cv = o["sg"], o["cv"]
        dcv = dxbc * (sg * (1.0 + cv * (1.0 - sg)))
        e2 = jnp.concatenate([dcv, nxt_cv[...]], axis=0)
        n2 = L + CONV_HALO
        ahead = [dcv, pltpu.roll(e2, n2 - 1, 0)[:L], pltpu.roll(e2, n2 - 2, 0)[:L], pltpu.roll(e2, n2 - 3, 0)[:L]]
        dconv_ref[0:5] += jnp.concatenate(
            [jnp.sum(ux * ahead[3 - k], axis=0, keepdims=True) for k in range(4)]
            + [jnp.sum(dcv, axis=0, keepdims=True)], axis=0)
        dux = ahead[0] * cw[3:4] + ahead[1] * cw[2:3] + ahead[2] * cw[1:2] + ahead[3] * cw[0:1]
        dpm_ref[:, OFF_XBC:OFF_DT] = dux.astype(BF16)
        nxt_cv[...] = dcv[:CONV_HALO]

    def full(shape):
        return pl.BlockSpec(shape, lambda b, c: (0,) * len(shape))

    def rowblk(b, c):
        return b * nc + (nc - 1 - c)

    hp_blocks = CHUNK // POOL_HALO
    T = nb * seq
    return pl.pallas_call(
        body, name="mixer_bwd", grid=(nb, nc),
        in_specs=[pl.BlockSpec((CHUNK, PROJ_W), lambda b, c: (rowblk(b, c), 0)),
                  pl.BlockSpec((POOL_HALO, POOL_W), lambda b, c: (jnp.maximum(rowblk(b, c) * hp_blocks - 1, 0), 0)),
                  pl.BlockSpec((CHUNK, CONV_CH), lambda b, c: (rowblk(b, c), 0)),
                  pl.BlockSpec((CHUNK, MIX_W), lambda b, c: (rowblk(b, c), 0)),
                  pl.BlockSpec((1, 1, N_GROUPS, N_STATE, GROUP_W), lambda b, c: (b, nc - 1 - c, 0, 0, 0)),
                  full((4, CONV_CH)), full((1, CONV_CH)), full((8, LANES)), full((1, D)),
                  full((4, LANES, LANES)), full((1, POOL_W))] + [ANY_SPEC] * na,
        out_specs=[pl.BlockSpec((CHUNK, PROJ_W), lambda b, c: (rowblk(b, c), 0)),
                   full((8, CONV_CH)), full((8, LANES)), full((8, D)), full((4, LANES, LANES))],
        out_shape=[jax.ShapeDtypeStruct((T, PROJ_W), BF16),
                   jax.ShapeDtypeStruct((8, CONV_CH), F32), jax.ShapeDtypeStruct((8, LANES), F32),
                   jax.ShapeDtypeStruct((8, D), F32), jax.ShapeDtypeStruct((4, LANES, LANES), F32)],
        scratch_shapes=[pltpu.VMEM((POOL_HALO, POOL_W), F32), pltpu.VMEM((CONV_HALO, CONV_CH), F32),
                        pltpu.VMEM((N_GROUPS, N_STATE, GROUP_W), F32), pltpu.VMEM((CHUNK, D), F32),
                        pltpu.VMEM((CHUNK, D), F32)],
        compiler_params=_cparams(),
    )(pm, pm, cvs, dym, hstates, cw, cb, hp, gssd, wpool, pscale, *after)


def _mlp_fused(x2, ymix, target, mod, g_mlp, g_final, w_out, w_up, w_down, seq):
    T = x2.shape[0]
    tm = min(256, seq)
    tps = seq // tm
    nblk = D_FF // FF_BLK

    def body(x_ref, ym_ref, tg_ref, mod_ref, gm_ref, gf_ref, wo_ref, wu_ref, wd_ref,
             da_ref, dym_ref, dh1_ref, u2_ref, f_ref, dup_ref, ddn_ref, dmod_ref, acc_ref, relu_scr):
        i = pl.program_id(0)

        @pl.when(i == 0)
        def _():
            acc_ref[...] = jnp.zeros_like(acc_ref)

        @pl.when(i % tps == 0)
        def _():
            dmod_ref[...] = jnp.zeros_like(dmod_ref)

        md = mod_ref[0]
        gate_m, shift_f, scale_f, gate_f = md[2:3], md[3:4], md[4:5], md[5:6]
        g_mlp, g_fin = gm_ref[...], gf_ref[...]
        a = jnp.dot(ym_ref[...], wo_ref[...], preferred_element_type=F32)
        h1 = x_ref[...] + gate_m * a
        r2 = lax.rsqrt(jnp.mean(h1 * h1, axis=-1, keepdims=True) + EPS)
        n2 = h1 * r2
        u2 = (n2 * g_mlp) * (1.0 + scale_f) + shift_f
        u2b = u2.astype(BF16)
        u2_ref[...] = u2b
        dn = jnp.zeros((tm, D), F32)
        for j in range(nblk):
            js = slice(j * FF_BLK, (j + 1) * FF_BLK)
            upj = jnp.maximum(jnp.dot(u2b, wu_ref[j], preferred_element_type=F32), 0.0)
            relu_scr[:, js] = upj
            fj = (upj * upj).astype(BF16)
            f_ref[:, js] = fj
            dn = dn + jnp.dot(fj, wd_ref[j], preferred_element_type=F32)
        h2 = h1 + gate_f * dn
        r3 = lax.rsqrt(jnp.mean(h2 * h2, axis=-1, keepdims=True) + EPS)
        n3 = h2 * r3
        err = n3 * g_fin - tg_ref[...]
        loss = 0.5 * jnp.sum(jnp.mean(err * err, axis=-1, keepdims=True), axis=0, keepdims=True)
        dout = err * (1.0 / D)
        d_gfin = jnp.sum(dout * n3, axis=0, keepdims=True)
        dn3 = dout * g_fin
        dh2 = r3 * (dn3 - n3 * jnp.mean(dn3 * n3, axis=-1, keepdims=True))
        d_gate_f = jnp.sum(dh2 * dn, axis=0, keepdims=True)
        ddn = (gate_f * dh2).astype(BF16)
        ddn_ref[...] = ddn
        du2 = jnp.zeros((tm, D), F32)
        for j in range(nblk):
            js = slice(j * FF_BLK, (j + 1) * FF_BLK)
            dfj = lax.dot_general(ddn, wd_ref[j], (((1,), (1,)), ((), ())), preferred_element_type=F32)
            dupj = (dfj * (2.0 * relu_scr[:, js])).astype(BF16)
            dup_ref[:, js] = dupj
            du2 = du2 + lax.dot_general(dupj, wu_ref[j], (((1,), (1,)), ((), ())), preferred_element_type=F32)
        d_scale_f = jnp.sum(du2 * (n2 * g_mlp), axis=0, keepdims=True)
        d_shift_f = jnp.sum(du2, axis=0, keepdims=True)
        d_gmlp = jnp.sum(du2 * (1.0 + scale_f) * n2, axis=0, keepdims=True)
        dn2 = du2 * (g_mlp * (1.0 + scale_f))
        dh1 = dh2 + r2 * (dn2 - n2 * jnp.mean(dn2 * n2, axis=-1, keepdims=True))
        dh1_ref[...] = dh1
        d_gate_m = jnp.sum(dh1 * a, axis=0, keepdims=True)
        da = (gate_m * dh1).astype(BF16)
        da_ref[...] = da
        dym_ref[...] = lax.dot_general(da, wo_ref[...], (((1,), (1,)), ((), ())),
                                       preferred_element_type=F32).astype(BF16)
        dmod_ref[0] += jnp.concatenate([jnp.zeros((2, D), F32), d_gate_m, d_shift_f, d_scale_f, d_gate_f,
                                        jnp.zeros((2, D), F32)], axis=0)
        acc_ref[...] += jnp.concatenate([d_gmlp, d_gfin, loss * jnp.ones((1, D), F32), jnp.zeros((5, D), F32)], axis=0)

    whole = pl.BlockSpec(memory_space=pltpu.VMEM)

    def tok(w):
        return pl.BlockSpec((tm, w), lambda i: (i, 0))

    def vec():
        return pl.BlockSpec((1, D), lambda i: (0, 0))

    nb = T // seq
    return pl.pallas_call(
        body, name="mlp_fused", grid=(T // tm,),
        in_specs=[tok(D), tok(MIX_W), tok(D), pl.BlockSpec((1, 8, D), lambda i: (i // tps, 0, 0)), vec(), vec(),
                  whole, whole, whole],
        out_specs=[tok(D), tok(MIX_W), tok(D), tok(D), tok(D_FF), tok(D_FF), tok(D),
                   pl.BlockSpec((1, 8, D), lambda i: (i // tps, 0, 0)), pl.BlockSpec((8, D), lambda i: (0, 0))],
        out_shape=[jax.ShapeDtypeStruct((T, D), BF16), jax.ShapeDtypeStruct((T, MIX_W), BF16),
                   jax.ShapeDtypeStruct((T, D), F32), jax.ShapeDtypeStruct((T, D), BF16),
                   jax.ShapeDtypeStruct((T, D_FF), BF16), jax.ShapeDtypeStruct((T, D_FF), BF16),
                   jax.ShapeDtypeStruct((T, D), BF16), jax.ShapeDtypeStruct((nb, 8, D), F32),
                   jax.ShapeDtypeStruct((8, D), F32)],
        scratch_shapes=[pltpu.VMEM((tm, D_FF), F32)],
        compiler_params=_cparams(),
    )(x2, ymix, target, mod, g_mlp, g_final, w_out, w_up, w_down)


def _in_bwd(x2, dh1, dpb, mod, g_mix, w_cat, dmod_a, acc_a, seq):
    T = x2.shape[0]
    tm = min(1024, seq)
    tps = seq // tm
    steps = T // tm

    def body(x_ref, dh_ref, dpb_ref, mod_ref, g_ref, w_ref, dma_ref, acca_ref, dx_ref, dmod_ref, acc_ref):
        i = pl.program_id(0)

        @pl.when(i == 0)
        def _():
            acc_ref[...] = acca_ref[...]

        @pl.when(i % tps == 0)
        def _():
            dmod_ref[...] = dma_ref[...]

        du = lax.dot_general(dpb_ref[...], w_ref[...], (((1,), (1,)), ((), ())), preferred_element_type=F32)
        x = x_ref[...]
        md = mod_ref[0]
        g = g_ref[...]
        r = lax.rsqrt(jnp.mean(x * x, axis=-1, keepdims=True) + EPS)
        n1 = x * r
        d_scale = jnp.sum(du * (n1 * g), axis=0, keepdims=True)
        d_shift = jnp.sum(du, axis=0, keepdims=True)
        d_g = jnp.sum(du * (1.0 + md[1:2]) * n1, axis=0, keepdims=True)
        dn1 = du * (g * (1.0 + md[1:2]))
        dx_ref[...] = dh_ref[...] + r * (dn1 - n1 * jnp.mean(dn1 * n1, axis=-1, keepdims=True))
        dmod_ref[0] += jnp.concatenate([d_shift, d_scale, jnp.zeros((6, D), F32)], axis=0)
        acc_ref[...] += jnp.concatenate([jnp.zeros((3, D), F32), d_g, jnp.zeros((4, D), F32)], axis=0)

    whole = pl.BlockSpec(memory_space=pltpu.VMEM)
    nb = T // seq
    return pl.pallas_call(
        body, name="in_bwd", grid=(steps,),
        in_specs=[pl.BlockSpec((tm, D), lambda i: (i, 0)), pl.BlockSpec((tm, D), lambda i: (i, 0)),
                  pl.BlockSpec((tm, PROJ_W), lambda i: (i, 0)),
                  pl.BlockSpec((1, 8, D), lambda i: (i // tps, 0, 0)), pl.BlockSpec((1, D), lambda i: (0, 0)),
                  whole, pl.BlockSpec((1, 8, D), lambda i: (i // tps, 0, 0)), pl.BlockSpec((8, D), lambda i: (0, 0))],
        out_specs=[pl.BlockSpec((tm, D), lambda i: (i, 0)),
                   pl.BlockSpec((1, 8, D), lambda i: (i // tps, 0, 0)), pl.BlockSpec((8, D), lambda i: (0, 0))],
        out_shape=[jax.ShapeDtypeStruct((T, D), F32),
                   jax.ShapeDtypeStruct((nb, 8, D), F32), jax.ShapeDtypeStruct((8, D), F32)],
        compiler_params=_cparams(),
    )(x2, dh1, dpb, mod, g_mix, w_cat, dmod_a, acc_a)


def _dw_in(u_b, dpb, in_cols, shards):
    T = u_b.shape[0]
    bk = min(512, T)
    nk = T // bk
    ns = len(shards)
    starts = [(in_cols * j // LANES) * LANES for j in range(N_DEV)]
    assert all(s + DW_IN_WIN <= PROJ_W and in_cols * (j + 1) <= s + DW_IN_WIN for j, s in enumerate(starts))

    def body(*refs):
        u_ref, d_ref = refs[:2]
        sh_refs = refs[2:2 + ns]
        o_ref = refs[2 + ns]
        ga_refs = refs[3 + ns:3 + 2 * ns]
        acc, send, recv, loc = refs[3 + 2 * ns:]
        k = pl.program_id(0)
        gather = _Gather(sh_refs, ga_refs, send, recv, loc)
        gather.begin_hosted(k, nk)

        @pl.when(k == 0)
        def _():
            acc[...] = jnp.zeros_like(acc)

        ut = u_ref[...].T
        for j in range(N_DEV):
            acc[j] += jnp.dot(ut, d_ref[:, starts[j]:starts[j] + DW_IN_WIN], preferred_element_type=F32)

        @pl.when(k == nk - 1)
        def _():
            for j in range(N_DEV):
                off = in_cols * j - starts[j]
                o_ref[j] = acc[j][:, off:off + in_cols].astype(BF16)

        gather.end_hosted(k, nk)

    return pl.pallas_call(
        body, name="dw_in", grid=(nk,),
        in_specs=[pl.BlockSpec((bk, D), lambda k: (k, 0)), pl.BlockSpec((bk, PROJ_W), lambda k: (k, 0))]
        + [ANY_SPEC] * ns,
        out_specs=[pl.BlockSpec((N_DEV, D, in_cols), lambda k: (0, 0, 0))] + [ANY_SPEC] * ns,
        out_shape=[jax.ShapeDtypeStruct((N_DEV, D, in_cols), BF16)]
        + [jax.ShapeDtypeStruct((N_DEV,) + v.shape, v.dtype) for v in shards],
        scratch_shapes=[pltpu.VMEM((N_DEV, D, DW_IN_WIN), F32)] + _gather_scratch(shards),
        compiler_params=_cparams(),
    )(u_b, dpb, *shards)


def _dw_blocks(a, b, name, by_rows, per_step=1, after=()):
    T, M = a.shape
    N = b.shape[1]
    bk = min(4096, T)
    nk = T // bk
    whole = pl.BlockSpec(memory_space=pltpu.VMEM)
    if by_rows:
        rows = M // N_DEV
        am = rows * per_step
        nblk = N_DEV // per_step
        a_spec, b_spec = pl.BlockSpec((bk, am), lambda i, k: (k, i)), whole
        out_blk, acc_shape = (per_step, rows, N), (am, N)
    else:
        cols = N // N_DEV
        nblk = N_DEV
        a_spec, b_spec = whole, pl.BlockSpec((bk, cols), lambda i, k: (k, i))
        out_blk, acc_shape = (1, M, cols), (M, cols)

    def body(a_ref, b_ref, *rest):
        o_ref, acc = rest[len(after):]
        k = pl.program_id(1)

        @pl.when(k == 0)
        def _():
            acc[...] = jnp.zeros_like(acc)

        tok = pl.ds(pl.multiple_of(k * bk, bk), bk)
        a_blk = a_ref[...] if by_rows else a_ref[tok, :]
        b_blk = b_ref[tok, :] if by_rows else b_ref[...]
        acc[...] += lax.dot_general(a_blk, b_blk, (((0,), (0,)), ((), ())), preferred_element_type=F32)

        @pl.when(k == nk - 1)
        def _():
            o_ref[...] = acc[...].reshape(out_blk).astype(BF16)

    return pl.pallas_call(
        body, name=name, grid=(nblk, nk), in_specs=[a_spec, b_spec] + [ANY_SPEC] * len(after),
        out_specs=pl.BlockSpec(out_blk, lambda i, k: (i, 0, 0)),
        out_shape=jax.ShapeDtypeStruct((N_DEV,) + out_blk[1:], BF16),
        scratch_shapes=[pltpu.VMEM(acc_shape, F32)],
        compiler_params=_cparams(),
    )(a, b, *after)


def _adam_parts(parts, w, m, v, name):
    rows, cols = w.shape
    br = rows
    for cand in range(rows, 15, -16):
        if rows % cand == 0 and cand * cols * 4 <= ADAM_BLOCK_BYTES:
            br = cand
            break

    def body(p_ref, w_ref, m_ref, v_ref, g_out, dl_out, m_out, v_out):
        g = p_ref[0].astype(F32)
        for k in range(1, N_DEV):
            g = g + p_ref[k].astype(F32)
        g_out[...] = g
        dl, mn, vn = _adam_math(w_ref[...], g, m_ref[...], v_ref[...])
        dl_out[...] = dl
        m_out[...] = mn
        v_out[...] = vn

    wspec = pl.BlockSpec((br, cols), lambda i: (i, 0))
    return pl.pallas_call(
        body, name=name, grid=(rows // br,),
        in_specs=[pl.BlockSpec((N_DEV, br, cols), lambda i: (0, i, 0)), wspec, wspec, wspec],
        out_specs=[wspec] * 4, out_shape=[jax.ShapeDtypeStruct((rows, cols), F32)] * 4,
        compiler_params=_cparams(),
    )(parts, w, m, v)


def _adam_plain(g, w, m, v, name):
    def body(g_ref, w_ref, m_ref, v_ref, dl_out, m_out, v_out):
        dl, mn, vn = _adam_math(w_ref[...], g_ref[...], m_ref[...], v_ref[...])
        dl_out[...] = dl
        m_out[...] = mn
        v_out[...] = vn

    return pl.pallas_call(body, name=name, out_shape=[jax.ShapeDtypeStruct(w.shape, F32)] * 3,
                          compiler_params=_cparams())(g, w, m, v)


SMALL_PARAMS = ("b_ada", "g_mix", "conv_b", "dt_bias", "a_log", "d_skip", "g_ssd", "pool_scale", "g_mlp", "g_final")


def _small_adam(gathered, params):
    n_par = len(SMALL_PARAMS)
    nb = gathered[0].shape[1]

    def body(*refs):
        dmod_ref, acc_ref, conv_ref, vec_ref, hd_ref = refs[:5]
        par_refs = refs[5:5 + 3 * n_par]
        out_refs = refs[5 + 3 * n_par:5 + 7 * n_par]
        cw_out, acc_out = refs[5 + 7 * n_par:]

        def total(ref):
            t = ref[0]
            for k in range(1, N_DEV):
                t = t + ref[k]
            return t

        dm = total(dmod_ref)
        dmb = dm[0]
        for b in range(1, nb):
            dmb = dmb + dm[b]
        ac, cv, vc, hd = total(acc_ref), total(conv_ref), total(vec_ref), total(hd_ref)
        cw_out[...] = cv[0:4]
        acc_out[...] = ac
        grads = {
            "b_ada": jnp.concatenate([dmb[r:r + 1] for r in range(6)], axis=1), "g_mix": ac[3:4], "conv_b": cv[4:5],
            "dt_bias": hd[0:1, 0:N_HEADS], "a_log": hd[1:2, 0:N_HEADS], "d_skip": hd[2:3, 0:N_HEADS],
            "g_ssd": vc[0:1], "pool_scale": vc[1:2, 0:POOL_W], "g_mlp": ac[0:1], "g_final": ac[1:2],
        }
        for i, name in enumerate(SMALL_PARAMS):
            w_ref, m_ref, v_ref = par_refs[3 * i:3 * i + 3]
            g = grads[name]
            dl, mn, vn = _adam_math(w_ref[...], g, m_ref[...], v_ref[...])
            g_o, d_o, m_o, v_o = out_refs[4 * i:4 * i + 4]
            g_o[...] = g
            d_o[...] = dl
            m_o[...] = mn
            v_o[...] = vn

    flat = [a for name in SMALL_PARAMS for a in params[name]]
    out_shape = [jax.ShapeDtypeStruct(params[name][0].shape, F32) for name in SMALL_PARAMS for _ in range(4)]
    out_shape += [jax.ShapeDtypeStruct((4, CONV_CH), F32), jax.ShapeDtypeStruct((8, D), F32)]
    return pl.pallas_call(body, name="small_adam", out_shape=out_shape, compiler_params=_cparams())(*gathered, *flat)


def kernel(x, c, w_ada, b_ada, g_mix, w_in, conv_w, conv_b, dt_bias, a_log, d_skip, g_ssd, w_pool, pool_scale, w_out, g_mlp, w_up, w_down, g_final, loss_target, m_w_ada, m_b_ada, m_g_mix, m_w_in, m_conv_w, m_conv_b, m_dt_bias, m_a_log, m_d_skip, m_g_ssd, m_w_pool, m_pool_scale, m_w_out, m_g_mlp, m_w_up, m_w_down, m_g_final, v_w_ada, v_b_ada, v_g_mix, v_w_in, v_conv_w, v_conv_b, v_dt_bias, v_a_log, v_d_skip, v_g_ssd, v_w_pool, v_pool_scale, v_w_out, v_g_mlp, v_w_up, v_w_down, v_g_final):
    nb, seq, _ = x.shape
    T = nb * seq
    me = 4 * lax.axis_index("x") + 2 * lax.axis_index("y") + lax.axis_index("c")
    in_cols = w_in.shape[2]
    ada_cols = w_ada.shape[2]
    cw_cols = conv_w.shape[2]

    win_t = jnp.pad(w_in[0].astype(BF16).T, ((0, -in_cols % 16), (0, 0)))
    c_g, cw_g, win_g = _all_gather([c, conv_w[0], win_t], "ag_first")
    c_all = c_g.reshape(N_DEV * nb, D)
    cw_full = cw_g.transpose(1, 0, 2).reshape(4, CONV_CH)

    b_slice = lax.dynamic_slice(b_ada, (0, me * ada_cols), (1, ada_cols))
    mod_cols = _ada_fwd(c_all, w_ada[0], b_slice)
    (mod_g,) = _all_gather([mod_cols], "ag_mod")
    mod_all = mod_g.transpose(1, 0, 2).reshape(N_DEV * nb, 6, D)
    mod_mine = lax.dynamic_slice(mod_all, (me * nb, 0, 0), (nb, 6, D))
    mod = jnp.pad(mod_mine, ((0, 0), (0, 2), (0, 0)))

    x2 = x.reshape(T, D)
    tg2 = loss_target.reshape(T, D)
    heads = jnp.pad(jnp.concatenate([dt_bias, a_log, d_skip], axis=0), ((0, 5), (0, LANES - N_HEADS)))
    wpool_b = w_pool[0]
    u_b, pm, w_cat, wup_g = _mix_in(x2, mod, g_mix, win_g, in_cols, seq, [w_up[0].astype(BF16)])
    ymix, hstates, cvs, wout_g, wdn_g = _mixer_fwd(
        pm, cw_full, conv_b, heads, g_ssd, wpool_b, pool_scale, nb, seq,
        [w_out[0].astype(BF16), w_down[0].astype(BF16)])
    da_b, dym, dh1, u2_b, f_b, dup_b, ddn_b, dmod_a, acc_a = _mlp_fused(
        x2, ymix, tg2, mod, g_mlp, g_final.reshape(1, D), wout_g.reshape(MIX_W, D), wup_g, wdn_g, seq)

    gout_p = _dw_blocks(ymix, da_b, "dw_out", True, per_step=4)
    gup_p = _dw_blocks(u2_b, dup_b, "dw_up", False)
    ex_a = _exchange_start([gout_p, gup_p], "ga_start")
    gdn_p = _dw_blocks(f_b, ddn_b, "dw_down", True, after=[ex_a[4]])
    ex_b = _exchange_start([gdn_p], "gb_start")
    dpb, d_conv, d_heads, d_vec, d_wpool = _mixer_bwd(
        pm, cvs, dym, hstates, cw_full, conv_b, heads, g_ssd, wpool_b, pool_scale, nb, seq, after=[ex_b[4]])
    gin_p, conv_g, vec_g, heads_g, wpool_parts = _dw_in(
        u_b, dpb, in_cols, [d_conv, d_vec, d_heads, d_wpool.reshape(4 * LANES, LANES)])
    ex_in = _exchange_start([gin_p], "gin_start")
    grad_x2, dmod, acc = _in_bwd(x2, dh1, dpb, mod, g_mix + ex_in[4][0:1, 0:1], w_cat, dmod_a, acc_a, seq)

    gout_r, gup_r = _exchange_wait(ex_a, dmod, "ga_wait", me)
    (gdn_r,) = _exchange_wait(ex_b, dmod, "gb_wait", me)
    g_out, d_out, nm_out, nv_out = _adam_parts(gout_r, w_out[0], m_w_out[0], v_w_out[0], "adam_w_out")
    g_up, d_up, nm_up, nv_up = _adam_parts(gup_r, w_up[0], m_w_up[0], v_w_up[0], "adam_w_up")
    g_dn, d_dn, nm_dn, nv_dn = _adam_parts(gdn_r, w_down[0], m_w_down[0], v_w_down[0], "adam_w_down")

    dmod_g, acc_g = _all_gather([dmod, acc], "ag_small_bwd", after=[nm_out, nm_up, nm_dn])
    pool2 = (4 * LANES, LANES)
    wpool_outs = _adam_parts(wpool_parts, w_pool.reshape(pool2), m_w_pool.reshape(pool2), v_w_pool.reshape(pool2),
                             "adam_w_pool")
    small_params = {
        "b_ada": (b_ada, m_b_ada, v_b_ada), "g_mix": (g_mix, m_g_mix, v_g_mix), "conv_b": (conv_b, m_conv_b, v_conv_b),
        "dt_bias": (dt_bias, m_dt_bias, v_dt_bias), "a_log": (a_log, m_a_log, v_a_log),
        "d_skip": (d_skip, m_d_skip, v_d_skip), "g_ssd": (g_ssd, m_g_ssd, v_g_ssd),
        "pool_scale": (pool_scale, m_pool_scale, v_pool_scale), "g_mlp": (g_mlp, m_g_mlp, v_g_mlp),
        "g_final": tuple(a.reshape(1, D) for a in (g_final, m_g_final, v_g_final)),
    }
    small_res = _small_adam([dmod_g, acc_g, conv_g, vec_g, heads_g], small_params)
    g_cw_full, acc_sum = small_res[-2:]
    loss = acc_sum[2, 0]

    g_cw = lax.dynamic_slice(g_cw_full, (0, me * cw_cols), (4, cw_cols))
    d_cwp, nm_cwp, nv_cwp = _adam_plain(g_cw, conv_w[0], m_conv_w[0], v_conv_w[0], "adam_conv_w")

    dmod_all = dmod_g[:, :, 0:6].reshape(N_DEV * nb, 6 * D)
    dmod_slice = lax.dynamic_slice(dmod_all, (0, me * ada_cols), (N_DEV * nb, ada_cols))
    g_ada, d_ada, nm_ada, nv_ada = _ada_bwd_adam(c_all, dmod_slice, w_ada[0], m_w_ada[0], v_w_ada[0])

    ex_after = nm_ada[0:8, 0:LANES] + acc_sum[:, 0:LANES]
    (gin_r,) = _exchange_wait(ex_in, ex_after, "gin_wait", me)
    g_in, d_in, nm_in, nv_in = _adam_parts(gin_r, w_in[0], m_w_in[0], v_w_in[0], "adam_w_in")

    def small_outs(kind, wpool):
        res = {name: small_res[4 * i + kind] for i, name in enumerate(SMALL_PARAMS)}
        res["g_final"] = res["g_final"].reshape(D)
        res["w_pool"] = wpool.reshape(1, 4, LANES, LANES)
        return res

    def big_outs(ada, win, cwp, wout, wup, wdn):
        return {"w_ada": ada[None], "w_in": win.reshape(1, D, in_cols), "conv_w": cwp[None], "w_out": wout[None],
                "w_up": wup[None], "w_down": wdn[None]}

    order = ["w_ada", "b_ada", "g_mix", "w_in", "conv_w", "conv_b", "dt_bias", "a_log", "d_skip", "g_ssd", "w_pool",
             "pool_scale", "w_out", "g_mlp", "w_up", "w_down", "g_final"]
    groups = [
        {**small_outs(0, wpool_outs[0]), **big_outs(g_ada, g_in, g_cw, g_out, g_up, g_dn)},
        {**small_outs(1, wpool_outs[1]), **big_outs(d_ada, d_in, d_cwp, d_out, d_up, d_dn)},
        {**small_outs(2, wpool_outs[2]), **big_outs(nm_ada, nm_in, nm_cwp, nm_out, nm_up, nm_dn)},
        {**small_outs(3, wpool_outs[3]), **big_outs(nv_ada, nv_in, nv_cwp, nv_out, nv_up, nv_dn)},
    ]
    outs = [loss, grad_x2.reshape(nb, seq, D)]
    for grp in groups:
        outs += [grp[n] for n in order]
    return tuple(outs)
```

```python
import functools

import jax
import jax.numpy as jnp
from jax import lax
from jax.experimental import pallas as pl
from jax.experimental.pallas import tpu as pltpu

F32, BF16 = jnp.float32, jnp.bfloat16
MESH = pl.DeviceIdType.MESH
N_DEV = 8
D = 1024
LANES = 128
CHUNK = 128
POOL_W = 512
WINDOWS = (2, 4, 8, 16)
N_HEADS = 16
HEAD_DIM = 64
N_GROUPS = 2
GROUP_W = 512
N_STATE = 128
CONV_CH = 1536
OFF_Z, OFF_XBC, OFF_DT, IN_W = 512, 1536, 3072, 3088
PROJ_W = OFF_DT + LANES
MIX_W = 1536
D_FF = 4096
FF_BLK = 512
EPS = 1e-5
LR, B1, B2, AEPS, WD, STEP = 0.001, 0.9, 0.999, 1e-08, 0.01, 10
POOL_HALO = 16
CONV_HALO = 8
VMEM_LIMIT = 56 << 20
ADAM_BLOCK_BYTES = 1 << 20
DW_IN_WIN = 512


def _cparams(**kw):
    return pltpu.CompilerParams(vmem_limit_bytes=VMEM_LIMIT, **kw)


def _mm(a, b):
    return jnp.dot(a.astype(BF16), b.astype(BF16), preferred_element_type=F32)


def _mm_nt(a, b):
    return lax.dot_general(a.astype(BF16), b.astype(BF16), (((1,), (1,)), ((), ())), preferred_element_type=F32)


def _mm_tn(a, b):
    return lax.dot_general(a.astype(BF16), b.astype(BF16), (((0,), (0,)), ((), ())), preferred_element_type=F32)


def _split_bf16(v, terms):
    parts, rest = [], v
    for t in range(terms):
        p = rest.astype(BF16)
        parts.append(p)
        if t + 1 < terms:
            rest = rest - p.astype(F32)
    return parts


def _dot01(a, b, terms, split_lhs=True):
    if split_lhs:
        bb = b.astype(BF16)
        prods = [jnp.dot(p, bb, preferred_element_type=F32) for p in _split_bf16(a, terms)]
    else:
        ab = a.astype(BF16)
        prods = [jnp.dot(ab, p, preferred_element_type=F32) for p in _split_bf16(b, terms)]
    out = prods[0]
    for q in prods[1:]:
        out = out + q
    return out


def _sigmoid(v):
    return 1.0 / (1.0 + jnp.exp(-v))


def _expand_mat():
    r = lax.broadcasted_iota(jnp.int32, (LANES, D), 0)
    c = lax.broadcasted_iota(jnp.int32, (LANES, D), 1)
    return (r == c // HEAD_DIM).astype(F32)


def _reduce_mat():
    r = lax.broadcasted_iota(jnp.int32, (D, LANES), 0)
    c = lax.broadcasted_iota(jnp.int32, (D, LANES), 1)
    return (c == r // HEAD_DIM).astype(F32)


def _pos():
    return lax.axis_index("x"), lax.axis_index("y"), lax.axis_index("c")


GATHER_PIECES = 4
GATHER_PIECE_BYTES = 96 << 10


def _pieces(shape, dtype):
    rows = shape[0]
    size = jnp.dtype(dtype).itemsize
    for d in shape:
        size *= d
    whole_tiles = rows % (GATHER_PIECES * 16) == 0
    return GATHER_PIECES if whole_tiles and size // GATHER_PIECES >= GATHER_PIECE_BYTES else 1


class _Gather:
    def __init__(self, x_refs, o_refs, send, recv, loc):
        self.x_refs, self.o_refs, self.send, self.recv, self.loc = x_refs, o_refs, send, recv, loc
        self.n = len(x_refs)
        self.pieces = [_pieces(r.shape, r.dtype) for r in x_refs]
        self.base = [7 * sum(self.pieces[:a]) for a in range(self.n)]
        x, y, c = _pos()
        self.c = c
        self.me, self.sib = (x, y, c), (x, y, 1 - c)
        self.chips = [(1 - x, y), (x, 1 - y), (1 - x, 1 - y)]

    def _rows(self, a, p):
        rows = self.x_refs[a].shape[0] // self.pieces[a]
        return pl.ds(p * rows, rows)

    def _cp(self, a, p, k, block, to, own=False):
        dst = self.o_refs[a].at[4 * block[0] + 2 * block[1] + block[2], self._rows(a, p)]
        sem = self.base[a] + 7 * p + k
        return pltpu.make_async_remote_copy(
            src_ref=self.x_refs[a].at[self._rows(a, p)] if own else dst, dst_ref=dst,
            send_sem=self.send.at[sem], recv_sem=self.recv.at[sem], device_id=to, device_id_type=MESH)

    def _mine(self, a):
        me = self.me
        return pltpu.make_async_copy(self.x_refs[a], self.o_refs[a].at[4 * me[0] + 2 * me[1] + me[2]], self.loc.at[a])

    def _first(self, a, p):
        cps = [self._cp(a, p, 0, self.me, self.sib, own=True)]
        return cps + [self._cp(a, p, 1 + j, self.me, (*chip, self.c), own=True) for j, chip in enumerate(self.chips)]

    def _passed(self, a, p, j):
        return self._cp(a, p, 4 + j, (*self.chips[j], self.c), self.sib)

    def start(self):
        for a in range(self.n):
            self._mine(a).start()
        for p in range(max(self.pieces)):
            for a in range(self.n):
                if p < self.pieces[a]:
                    for cp in self._first(a, p):
                        cp.start()

    def forward(self, p):
        for j, chip in enumerate(self.chips):
            for a in range(self.n):
                if p < self.pieces[a]:
                    self._cp(a, p, 1 + j, (*chip, self.c), self.me).wait_recv()
                    self._passed(a, p, j).start()

    def finish(self):
        for a in range(self.n):
            for p in range(self.pieces[a]):
                self._cp(a, p, 0, self.sib, self.me).wait_recv()
                for j, chip in enumerate(self.chips):
                    self._cp(a, p, 4 + j, (*chip, 1 - self.c), self.me).wait_recv()
        for a in range(self.n):
            for p in range(self.pieces[a]):
                for cp in self._first(a, p):
                    cp.wait_send()
                for j in range(3):
                    self._passed(a, p, j).wait_send()
            self._mine(a).wait()

    def begin_hosted(self, step, steps):
        @pl.when(step == 0)
        def _():
            self.start()

        n_p = max(self.pieces)
        for p in range(n_p):
            @pl.when(step == min(((p + 1) * 7 * steps) // (8 * n_p), steps - 1))
            def _():
                self.forward(p)

    def end_hosted(self, step, steps):
        @pl.when(step == steps - 1)
        def _():
            self.finish()


class _Exchange:
    def __init__(self, x_refs, o_refs, send, recv, loc):
        self.x_refs, self.o_refs, self.send, self.recv, self.loc = x_refs, o_refs, send, recv, loc
        self.n = len(x_refs)
        x, y, c = _pos()
        self.me_i = 4 * x + 2 * y + c
        self.peers = []
        for k in range(1, N_DEV):
            px = 1 - x if (k >> 2) & 1 else x
            py = 1 - y if (k >> 1) & 1 else y
            pc = 1 - c if k & 1 else c
            self.peers.append(((px, py, pc), 4 * px + 2 * py + pc))

    def _mine(self, a):
        return pltpu.make_async_copy(self.x_refs[a].at[self.me_i], self.o_refs[a].at[self.me_i], self.loc.at[a])

    def _cp(self, a, k, landing):
        peer, peer_i = self.peers[k]
        return pltpu.make_async_remote_copy(
            src_ref=self.x_refs[a].at[peer_i], dst_ref=self.o_refs[a].at[landing],
            send_sem=self.send.at[a * 7 + k], recv_sem=self.recv.at[a * 7 + k],
            device_id=peer, device_id_type=MESH)

    def start(self):
        for a in range(self.n):
            self._mine(a).start()
            for k in range(N_DEV - 1):
                self._cp(a, k, self.me_i).start()

    def finish(self):
        for a in range(self.n):
            for k in range(N_DEV - 1):
                self._cp(a, k, self.peers[k][1]).wait_recv()
        for a in range(self.n):
            for k in range(N_DEV - 1):
                self._cp(a, k, self.me_i).wait_send()
            self._mine(a).wait()


def _gather_scratch(xs):
    n_sem = 7 * sum(_pieces(v.shape, v.dtype) for v in xs)
    return [pltpu.SemaphoreType.DMA((n_sem,)), pltpu.SemaphoreType.DMA((n_sem,)), pltpu.SemaphoreType.DMA((len(xs),))]


ANY_SPEC = pl.BlockSpec(memory_space=pl.ANY)


def _all_gather(xs, name, after=()):
    n, na = len(xs), len(after)

    def body(*refs):
        g = _Gather(refs[:n], refs[n + na:2 * n + na], *refs[2 * n + na:])
        g.start()
        for p in range(max(g.pieces)):
            g.forward(p)
        g.finish()

    return pl.pallas_call(
        body, name=name,
        out_shape=[jax.ShapeDtypeStruct((N_DEV,) + v.shape, v.dtype) for v in xs],
        in_specs=[ANY_SPEC] * (n + na), out_specs=[ANY_SPEC] * n, scratch_shapes=_gather_scratch(xs),
    )(*xs, *after)


HBM_SPEC = pl.BlockSpec(memory_space=pltpu.HBM)
SEM_SPEC = pl.BlockSpec(memory_space=pltpu.SEMAPHORE)
VMEM_SPEC = pl.BlockSpec(memory_space=pltpu.VMEM)
SPLIT_EFFECT = pltpu.SideEffectType.DATAFLOW_SIDE_EFFECTING


def _in_hbm(v):
    return pltpu.with_memory_space_constraint(v, pltpu.HBM)


def _exchange_start(blocks, name):
    n = len(blocks)

    def body(*refs):
        x_refs, land_refs = refs[:n], refs[n:2 * n]
        send, recv = refs[2 * n:2 * n + 2]
        token = refs[-1]
        ex = _Exchange(x_refs, land_refs, send, recv, None)
        for a in range(n):
            for k in range(N_DEV - 1):
                ex._cp(a, k, ex.me_i).start()
        token[...] = jnp.zeros_like(token)

    lands = [lax.empty(v.shape, v.dtype) for v in blocks]
    hbm = tuple(pltpu.HBM(v.shape, v.dtype) for v in list(blocks) + lands)
    n_sem = (N_DEV - 1) * n
    out = pl.pallas_call(
        body, name=name,
        out_shape=(pltpu.SemaphoreType.DMA((n_sem,)), pltpu.SemaphoreType.DMA((n_sem,))) + hbm
        + (jax.ShapeDtypeStruct((8, LANES), F32),),
        in_specs=(HBM_SPEC,) * (2 * n), out_specs=(SEM_SPEC, SEM_SPEC) + (HBM_SPEC,) * (2 * n) + (VMEM_SPEC,),
        input_output_aliases={i: i + 2 for i in range(2 * n)},
        compiler_params=pltpu.CompilerParams(has_side_effects=SPLIT_EFFECT),
    )(*[_in_hbm(v) for v in list(blocks) + lands])
    return out[0], out[1], list(out[2:2 + n]), list(out[2 + n:2 + 2 * n]), out[-1]


def _exchange_wait(ex, after, name, me):
    send, recv, thru, lands, _ = ex
    n = len(thru)

    def body(*refs):
        x_refs, land_refs = refs[:n], refs[n:2 * n]
        send_ref, recv_ref = refs[2 * n:2 * n + 2]
        e = _Exchange(x_refs, land_refs, send_ref, recv_ref, None)
        for a in range(n):
            for k in range(N_DEV - 1):
                e._cp(a, k, e.me_i).wait_send()
                e._cp(a, k, e.peers[k][1]).wait_recv()

    hbm = tuple(pltpu.HBM(v.shape, v.dtype) for v in list(thru) + list(lands))
    out = pl.pallas_call(
        body, name=name, out_shape=hbm,
        in_specs=(HBM_SPEC,) * (2 * n) + (SEM_SPEC, SEM_SPEC, ANY_SPEC), out_specs=(HBM_SPEC,) * (2 * n),
        input_output_aliases={i: i for i in range(2 * n)},
        compiler_params=pltpu.CompilerParams(has_side_effects=SPLIT_EFFECT),
    )(*thru, *lands, send, recv, after)
    done = []
    for own, land in zip(out[:n], out[n:]):
        mine = lax.dynamic_slice(own, (me,) + (0,) * (own.ndim - 1), (1,) + own.shape[1:])
        done.append(lax.dynamic_update_slice(land, mine, (me,) + (0,) * (own.ndim - 1)))
    return done


def _ada_fwd(c_all, w_ada, b_slice):
    def body(c_ref, w_ref, b_ref, o_ref):
        cv = c_ref[...]
        act = cv * _sigmoid(cv)
        o_ref[...] = _mm(act, w_ref[...]) + b_ref[...]

    nb, nc = c_all.shape[0], w_ada.shape[1]
    return pl.pallas_call(body, name="ada_fwd", out_shape=jax.ShapeDtypeStruct((nb, nc), F32),
                          compiler_params=_cparams())(c_all, w_ada, b_slice)


def _adam_math(w, g, m, v):
    m = B1 * m + (1.0 - B1) * g
    v = B2 * v + (1.0 - B2) * jnp.square(g)
    m_hat = m / (1.0 - B1 ** STEP)
    v_hat = v / (1.0 - B2 ** STEP)
    delta = -LR * (m_hat / (jnp.sqrt(v_hat) + AEPS) + WD * w)
    return delta, m, v


def _ada_bwd_adam(c_all, dmod_slice, w, m, v):
    rows, cols = w.shape
    br = 256

    def body(c_ref, d_ref, w_ref, m_ref, v_ref, g_out, dl_out, m_out, v_out):
        cv = c_ref[...]
        act = cv * _sigmoid(cv)
        g = _mm_tn(act, d_ref[...])
        g_out[...] = g
        dl, mn, vn = _adam_math(w_ref[...], g, m_ref[...], v_ref[...])
        dl_out[...] = dl
        m_out[...] = mn
        v_out[...] = vn

    nb = c_all.shape[0]
    wspec = pl.BlockSpec((br, cols), lambda i: (i, 0))
    return pl.pallas_call(
        body, name="ada_bwd_adam", grid=(rows // br,),
        in_specs=[pl.BlockSpec((nb, br), lambda i: (0, i)), pl.BlockSpec((nb, cols), lambda i: (0, 0)),
                  wspec, wspec, wspec],
        out_specs=[wspec] * 4, out_shape=[jax.ShapeDtypeStruct((rows, cols), F32)] * 4,
        compiler_params=_cparams(),
    )(c_all, dmod_slice, w, m, v)


def _mix_in(x2, mod, g_mix, win_g, in_cols, seq, shards):
    T = x2.shape[0]
    tm = min(512, seq)
    tps = seq // tm
    blk_rows = win_g.shape[1]
    pad_rows = -blk_rows % (4 * LANES)
    ns = len(shards)
    steps = T // tm

    def body(*refs):
        x_ref, mod_ref, g_ref, wb_ref = refs[:4]
        sh_refs = refs[4:4 + ns]
        u_ref, pm_ref, wc_ref = refs[4 + ns:7 + ns]
        ga_refs = refs[7 + ns:7 + 2 * ns]
        w_ref, send, recv, loc = refs[7 + 2 * ns:]
        step = pl.program_id(0)
        gather = _Gather(sh_refs, ga_refs, send, recv, loc)
        gather.begin_hosted(step, steps)

        @pl.when(step == 0)
        def _():
            w_ref[:, OFF_DT:] = jnp.zeros((D, PROJ_W - OFF_DT), BF16)
            for j in range(N_DEV):
                blk = jnp.concatenate([wb_ref[j], jnp.zeros((pad_rows, D), BF16)], axis=0)
                w_ref[:, in_cols * j:in_cols * (j + 1)] = blk.T[:, :in_cols]
            wc_ref[...] = w_ref[...]

        x = x_ref[...]
        r = lax.rsqrt(jnp.mean(x * x, axis=-1, keepdims=True) + EPS)
        md = mod_ref[0]
        u = (x * r * g_ref[...]) * (1.0 + md[1:2]) + md[0:1]
        ub = u.astype(BF16)
        u_ref[...] = ub
        pm_ref[...] = jnp.dot(ub, w_ref[...], preferred_element_type=F32)
        gather.end_hosted(step, steps)

    whole = pl.BlockSpec(memory_space=pltpu.VMEM)
    return pl.pallas_call(
        body, name="mix_in", grid=(T // tm,),
        in_specs=[pl.BlockSpec((tm, D), lambda i: (i, 0)), pl.BlockSpec((1, 8, D), lambda i: (i // tps, 0, 0)),
                  pl.BlockSpec((1, D), lambda i: (0, 0)), whole] + [ANY_SPEC] * ns,
        out_specs=[pl.BlockSpec((tm, D), lambda i: (i, 0)), pl.BlockSpec((tm, PROJ_W), lambda i: (i, 0)),
                   pl.BlockSpec((D, PROJ_W), lambda i: (0, 0))] + [ANY_SPEC] * ns,
        out_shape=[jax.ShapeDtypeStruct((T, D), BF16), jax.ShapeDtypeStruct((T, PROJ_W), F32),
                   jax.ShapeDtypeStruct((D, PROJ_W), BF16)]
        + [jax.ShapeDtypeStruct((N_DEV,) + v.shape, v.dtype) for v in shards],
        scratch_shapes=[pltpu.VMEM((D, PROJ_W), BF16)] + _gather_scratch(shards),
        compiler_params=_cparams(),
    )(x2, mod, g_mix, win_g, *shards)


def _chunk_forward(up, z, ux, dtin, halo_p, halo_x, hprev, cw, cb, hp, gssd, wpool, pscale, t0, y_scr, cv=None):
    L = CHUNK
    out = {}
    row = lax.broadcasted_iota(jnp.int32, (L, 1), 0)
    t = (t0 + row + 1).astype(F32)
    e = jnp.concatenate([halo_p, up], axis=0)
    s2 = e + pltpu.roll(e, 1, 0)
    s4 = s2 + pltpu.roll(s2, 2, 0)
    s8 = s4 + pltpu.roll(s4, 4, 0)
    s16 = s8 + pltpu.roll(s8, 8, 0)
    sums = (s2, s4, s8, s16)
    p, inv, yp = [], [], []
    for gi, w in enumerate(WINDOWS):
        sl = slice(gi * LANES, (gi + 1) * LANES)
        ic = 1.0 / jnp.minimum(t, float(w))
        pg = sums[gi][POOL_HALO:, sl] * ic - up[:, sl]
        p.append(pg)
        inv.append(ic)
        yp.append(_mm(pg, wpool[gi]))
    out["p"], out["inv"], out["yp"] = p, inv, yp
    out["y_pool"] = jnp.concatenate(yp, axis=1) * pscale
    if cv is None:
        ex = jnp.concatenate([halo_x, ux], axis=0)
        taps = [pltpu.roll(ex, 3, 0)[CONV_HALO:], pltpu.roll(ex, 2, 0)[CONV_HALO:], pltpu.roll(ex, 1, 0)[CONV_HALO:], ux]
        cv = cb + taps[0] * cw[0:1] + taps[1] * cw[1:2] + taps[2] * cw[2:3] + taps[3] * cw[3:4]
    sg = _sigmoid(cv)
    xbc = cv * sg
    out["cv"], out["sg"] = cv, sg
    X = xbc[:, :D]
    Bm = xbc[:, D:D + N_GROUPS * N_STATE]
    Cm = xbc[:, D + N_GROUPS * N_STATE:]
    pre = dtin + hp[0:1]
    dt = jnp.maximum(pre, 0.0) + jnp.log(1.0 + jnp.exp(-jnp.abs(pre)))
    a_row = -jnp.exp(hp[1:2])
    da = dt * a_row
    ri = lax.broadcasted_iota(jnp.int32, (L, L), 0)
    ci = lax.broadcasted_iota(jnp.int32, (L, L), 1)
    causal = ri >= ci
    cum = _dot01(causal.astype(F32), da, 3, split_lhs=False)
    cum_t = cum.T
    cum_last = cum[L - 1:L]
    eo = jnp.exp(cum)
    dec = jnp.exp(cum_last - cum)
    cd = jnp.exp(cum_last)
    exm = _expand_mat()
    rows8 = jnp.concatenate([cd, hp[2:3], jnp.zeros((6, LANES), F32)], axis=0)
    rep = _dot01(jnp.concatenate([dt, eo, dec, rows8], axis=0), exm, 2)
    dt_rep, eo_rep, dec_rep = rep[0:L], rep[L:2 * L], rep[2 * L:3 * L]
    cd_rep, dskip_rep = rep[3 * L:3 * L + 1], rep[3 * L + 1:3 * L + 2]
    xdt = X * dt_rep
    out.update(X=X, Bm=Bm, Cm=Cm, pre=pre, dt=dt, a_row=a_row, cum=cum, cum_t=cum_t, eo=eo, dec=dec, cd=cd,
               dt_rep=dt_rep, eo_rep=eo_rep, dec_rep=dec_rep, cd_rep=cd_rep, dskip_rep=dskip_rep, xdt=xdt,
               causal=causal, anti=(ri <= ci).astype(F32), exm=exm)
    G, lms, yoff, hnew, xdec = [], [], [], [], []
    for g in range(N_GROUPS):
        gs = slice(g * GROUP_W, (g + 1) * GROUP_W)
        Bg = Bm[:, g * N_STATE:(g + 1) * N_STATE]
        Cg = Cm[:, g * N_STATE:(g + 1) * N_STATE]
        Gg = _mm_nt(Cg, Bg)
        G.append(Gg)
        for hh in range(N_HEADS // N_GROUPS):
            h = g * (N_HEADS // N_GROUPS) + hh
            seg = cum[:, h:h + 1] - cum_t[h:h + 1, :]
            lm = jnp.where(causal, jnp.exp(jnp.minimum(seg, 0.0)), 0.0)
            lms.append(lm)
            hs = slice(h * HEAD_DIM, (h + 1) * HEAD_DIM)
            y_scr[:, hs] = _mm(Gg * lm, xdt[:, hs])
        xd = xdt[:, gs] * dec_rep[:, gs]
        xdec.append(xd)
        sgm = _mm_tn(Bg, xd)
        yoff.append(_mm(Cg, hprev[g]) * eo_rep[:, gs])
        hnew.append(hprev[g] * cd_rep[:, gs] + sgm)
    out.update(G=G, lms=lms, yoff=yoff, hnew=hnew, xdec=xdec)
    y = y_scr[...] + jnp.concatenate(yoff, axis=1) + dskip_rep * X
    sz = _sigmoid(z)
    silz = z * sz
    yz = y * silz
    rg, yn = [], []
    for g in range(N_GROUPS):
        gs = slice(g * GROUP_W, (g + 1) * GROUP_W)
        r = lax.rsqrt(jnp.mean(yz[:, gs] * yz[:, gs], axis=-1, keepdims=True) + EPS)
        rg.append(r)
        yn.append(yz[:, gs] * r)
    yn = jnp.concatenate(yn, axis=1)
    out.update(y=y, sz=sz, silz=silz, rg=rg, yn=yn)
    out["y_ssd"] = yn * gssd
    return out


def _mixer_fwd(pm, cw, cb, hp, gssd, wpool, pscale, nb, seq, shards):
    nc = seq // CHUNK
    ns = len(shards)
    steps = nb * nc

    def body(*refs):
        pm_ref, cw_ref, cb_ref, hp_ref, gs_ref, wp_ref, ps_ref = refs[:7]
        sh_refs = refs[7:7 + ns]
        ym_ref, hs_ref, cv_ref = refs[7 + ns:10 + ns]
        ga_refs = refs[10 + ns:10 + 2 * ns]
        halo_p, halo_x, state, y_scr, send, recv, loc = refs[10 + 2 * ns:]
        c = pl.program_id(1)
        step = pl.program_id(0) * nc + c
        gather = _Gather(sh_refs, ga_refs, send, recv, loc)
        gather.begin_hosted(step, steps)

        @pl.when(c == 0)
        def _():
            halo_p[...] = jnp.zeros_like(halo_p)
            halo_x[...] = jnp.zeros_like(halo_x)
            state[...] = jnp.zeros_like(state)

        up = pm_ref[:, 0:POOL_W]
        z = pm_ref[:, OFF_Z:OFF_XBC]
        ux = pm_ref[:, OFF_XBC:OFF_DT]
        hprev = [state[0], state[1]]
        hs_ref[0, 0, 0] = hprev[0]
        hs_ref[0, 0, 1] = hprev[1]
        o = _chunk_forward(up, z, ux, pm_ref[:, OFF_DT:], halo_p[...], halo_x[...], hprev, cw_ref[...], cb_ref[...],
                           hp_ref[...], gs_ref[...], wp_ref[...], ps_ref[...], c * CHUNK, y_scr)
        ym_ref[:, 0:POOL_W] = o["y_pool"].astype(BF16)
        ym_ref[:, POOL_W:] = o["y_ssd"].astype(BF16)
        cv_ref[...] = o["cv"]
        state[0] = o["hnew"][0]
        state[1] = o["hnew"][1]
        halo_p[...] = up[CHUNK - POOL_HALO:]
        halo_x[...] = ux[CHUNK - CONV_HALO:]
        gather.end_hosted(step, steps)

    def full(shape):
        return pl.BlockSpec(shape, lambda b, c: (0,) * len(shape))

    T = nb * seq
    return pl.pallas_call(
        body, name="mixer_fwd", grid=(nb, nc),
        in_specs=[pl.BlockSpec((CHUNK, PROJ_W), lambda b, c: (b * nc + c, 0)),
                  full((4, CONV_CH)), full((1, CONV_CH)), full((8, LANES)), full((1, D)),
                  full((4, LANES, LANES)), full((1, POOL_W))] + [ANY_SPEC] * ns,
        out_specs=[pl.BlockSpec((CHUNK, MIX_W), lambda b, c: (b * nc + c, 0)),
                   pl.BlockSpec((1, 1, N_GROUPS, N_STATE, GROUP_W), lambda b, c: (b, c, 0, 0, 0)),
                   pl.BlockSpec((CHUNK, CONV_CH), lambda b, c: (b * nc + c, 0))] + [ANY_SPEC] * ns,
        out_shape=[jax.ShapeDtypeStruct((T, MIX_W), BF16),
                   jax.ShapeDtypeStruct((nb, nc, N_GROUPS, N_STATE, GROUP_W), F32),
                   jax.ShapeDtypeStruct((T, CONV_CH), F32)]
        + [jax.ShapeDtypeStruct((N_DEV,) + v.shape, v.dtype) for v in shards],
        scratch_shapes=[pltpu.VMEM((POOL_HALO, POOL_W), F32), pltpu.VMEM((CONV_HALO, CONV_CH), F32),
                        pltpu.VMEM((N_GROUPS, N_STATE, GROUP_W), F32), pltpu.VMEM((CHUNK, D), F32)] + _gather_scratch(shards),
        compiler_params=_cparams(),
    )(pm, cw, cb, hp, gssd, wpool, pscale, *shards)


def _mixer_bwd(pm, cvs, dym, hstates, cw, cb, hp, gssd, wpool, pscale, nb, seq, after=()):
    nc = seq // CHUNK
    hpg = N_HEADS // N_GROUPS
    na = len(after)

    def body(*refs):
        (pm_ref, hpool_ref, cv_ref, dy_ref, hs_ref, cw_ref, cb_ref, hp_ref, gs_ref, wp_ref, ps_ref) = refs[:11]
        dpm_ref, dconv_ref, dhp_ref, dvec_ref, dwp_ref = refs[11 + na:16 + na]
        nxt_q, nxt_cv, rstate, y_scr, dx_scr = refs[16 + na:]
        b = pl.program_id(0)
        ci = pl.program_id(1)
        c = nc - 1 - ci

        @pl.when((b == 0) & (ci == 0))
        def _():
            for r in (dconv_ref, dhp_ref, dvec_ref, dwp_ref):
                r[...] = jnp.zeros_like(r)

        @pl.when(ci == 0)
        def _():
            nxt_q[...] = jnp.zeros_like(nxt_q)
            nxt_cv[...] = jnp.zeros_like(nxt_cv)
            rstate[...] = jnp.zeros_like(rstate)

        first = (c > 0).astype(F32)
        up = pm_ref[:, 0:POOL_W]
        z = pm_ref[:, OFF_Z:OFF_XBC]
        ux = pm_ref[:, OFF_XBC:OFF_DT]
        halo_p = hpool_ref[...] * first
        hprev = [hs_ref[0, 0, 0], hs_ref[0, 0, 1]]
        cw, cb, hp, gssd, wpool, pscale = cw_ref[...], cb_ref[...], hp_ref[...], gs_ref[...], wp_ref[...], ps_ref[...]
        o = _chunk_forward(up, z, ux, pm_ref[:, OFF_DT:], halo_p, None, hprev, cw, cb, hp, gssd, wpool, pscale,
                           c * CHUNK, y_scr, cv=cv_ref[...])
        L = CHUNK
        dy_pool = dy_ref[:, 0:POOL_W].astype(F32)
        dy_ssd = dy_ref[:, POOL_W:].astype(F32)

        dvec_ref[1:2, 0:POOL_W] += jnp.sum(dy_pool * jnp.concatenate(o["yp"], axis=1), axis=0, keepdims=True)
        dyp = dy_pool * pscale
        qs = []
        dps = []
        for gi in range(len(WINDOWS)):
            sl = slice(gi * LANES, (gi + 1) * LANES)
            dwp_ref[gi] += _mm_tn(o["p"][gi], dyp[:, sl])
            dpg = _mm_nt(dyp[:, sl], wpool[gi])
            dps.append(dpg)
            qs.append(dpg * o["inv"][gi])
        q = jnp.concatenate(qs, axis=1)
        e = jnp.concatenate([q, nxt_q[...]], axis=0)
        n = L + POOL_HALO
        s2 = e + pltpu.roll(e, n - 1, 0)
        s4 = s2 + pltpu.roll(s2, n - 2, 0)
        s8 = s4 + pltpu.roll(s4, n - 4, 0)
        s16 = s8 + pltpu.roll(s8, n - 8, 0)
        sums = (s2, s4, s8, s16)
        for gi in range(len(WINDOWS)):
            sl = slice(gi * LANES, (gi + 1) * LANES)
            dpm_ref[:, sl] = (sums[gi][:L, sl] - dps[gi]).astype(BF16)
        nxt_q[...] = q[:POOL_HALO]

        yn, y, silz, sz = o["yn"], o["y"], o["silz"], o["sz"]
        dvec_ref[0:1] += jnp.sum(dy_ssd * yn, axis=0, keepdims=True)
        dyn = dy_ssd * gssd
        dyz = []
        for g in range(N_GROUPS):
            gs = slice(g * GROUP_W, (g + 1) * GROUP_W)
            mean = jnp.mean(dyn[:, gs] * yn[:, gs], axis=-1, keepdims=True)
            dyz.append(o["rg"][g] * (dyn[:, gs] - yn[:, gs] * mean))
        dyz = jnp.concatenate(dyz, axis=1)
        dyv = dyz * silz
        dpm_ref[:, OFF_Z:OFF_XBC] = (dyz * y * (sz * (1.0 + z * (1.0 - sz)))).astype(BF16)

        X, Bm, Cm, xdt = o["X"], o["Bm"], o["Cm"], o["xdt"]
        exm = o["exm"]
        rdm = _reduce_mat()
        lane = lax.broadcasted_iota(jnp.int32, (1, LANES), 1)
        sub = lax.broadcasted_iota(jnp.int32, (LANES, 1), 0)
        dX = o["dskip_rep"] * dyv
        yoff_full = jnp.concatenate(o["yoff"], axis=1)
        rs = jnp.zeros((L, LANES), F32)
        cs_t = jnp.zeros((LANES, L), F32)
        dBs, dCs = [], []
        rh_sums = []
        ddec = []
        for g in range(N_GROUPS):
            gs = slice(g * GROUP_W, (g + 1) * GROUP_W)
            Bg = Bm[:, g * N_STATE:(g + 1) * N_STATE]
            Cg = Cm[:, g * N_STATE:(g + 1) * N_STATE]
            Gg = o["G"][g]
            R = rstate[g]
            dwm = dyv[:, gs] * o["eo_rep"][:, gs]
            dC = _mm_nt(dwm, hprev[g])
            dH = _mm_tn(Cg, dwm)
            dG = jnp.zeros((L, L), F32)
            for hh in range(hpg):
                h = g * hpg + hh
                hs = slice(h * HEAD_DIM, (h + 1) * HEAD_DIM)
                lm = o["lms"][h]
                m_h = Gg * lm
                dM = _mm_nt(dyv[:, hs], xdt[:, hs])
                dx_scr[:, hs] = _mm_tn(m_h, dyv[:, hs])
                qm = dM * m_h
                rs = rs + jnp.sum(qm, axis=1, keepdims=True) * (lane == h).astype(F32)
                cs_t = cs_t + (sub == h).astype(F32) * jnp.sum(qm, axis=0, keepdims=True)
                dG = dG + dM * lm
            dC = dC + _mm(dG, Bg)
            dB = _mm_tn(dG, Cg)
            zx = _mm(Bg, R)
            dxdt_state = zx * o["dec_rep"][:, gs]
            ddec.append(zx * xdt[:, gs])
            dB = dB + _mm_nt(o["xdec"][g], R)
            rh_sums.append(jnp.sum(R * hprev[g], axis=0, keepdims=True))
            rstate[g] = dH + o["cd_rep"][:, gs] * R
            dx_scr[:, gs] = dx_scr[:, gs] + dxdt_state
            dBs.append(dB)
            dCs.append(dC)
        dxdt = dx_scr[...]
        tail = jnp.concatenate([jnp.sum(dyv * X, axis=0, keepdims=True), jnp.concatenate(rh_sums, axis=1),
                                jnp.zeros((6, D), F32)], axis=0)
        red = _dot01(jnp.concatenate([dyv * yoff_full, jnp.concatenate(ddec, axis=1), dxdt * X, tail], axis=0), rdm, 2)
        d_dskip, dcd_row = red[3 * L:3 * L + 1], red[3 * L + 1:3 * L + 2]
        ddec_h = red[L:2 * L] * o["dec"]
        dcum_last = jnp.sum(ddec_h, axis=0, keepdims=True) + dcd_row * o["cd"]
        dcum = red[0:L] + rs - cs_t.T - ddec_h + (sub == L - 1).astype(F32) * dcum_last
        dda = _dot01(o["anti"], dcum, 3, split_lhs=False)
        ddt_v = dda * o["a_row"] + red[2 * L:3 * L]
        dX = dX + dxdt * o["dt_rep"]
        head_mask = (lane < N_HEADS).astype(F32)
        d_alog = jnp.sum(dda * o["dt"], axis=0, keepdims=True) * o["a_row"] * head_mask
        dpre = ddt_v * _sigmoid(o["pre"]) * head_mask
        dpm_ref[:, OFF_DT:] = dpre.astype(BF16)
        d_dtb = jnp.sum(dpre, axis=0, keepdims=True)
        dhp_ref[...] += jnp.concatenate([d_dtb, d_alog, d_dskip * head_mask, jnp.zeros((5, LANES), F32)], axis=0)

        dxbc = jnp.concatenate([dX] + dBs + dCs, axis=1)
        sg, cv = o["sg"], o["cv"]
        dcv = dxbc * (sg * (1.0 + cv * (1.0 - sg)))
        e2 = jnp.concatenate([dcv, nxt_cv[...]], axis=0)
        n2 = L + CONV_HALO
        ahead = [dcv, pltpu.roll(e2, n2 - 1, 0)[:L], pltpu.roll(e2, n2 - 2, 0)[:L], pltpu.roll(e2, n2 - 3, 0)[:L]]
        dconv_ref[0:5] += jnp.concatenate(
            [jnp.sum(ux * ahead[3 - k], axis=0, keepdims=True) for k in range(4)]
            + [jnp.sum(dcv, axis=0, keepdims=True)], axis=0)
        dux = ahead[0] * cw[3:4] + ahead[1] * cw[2:3] + ahead[2] * cw[1:2] + ahead[3] * cw[0:1]
        dpm_ref[:, OFF_XBC:OFF_DT] = dux.astype(BF16)
        nxt_cv[...] = dcv[:CONV_HALO]

    def full(shape):
        return pl.BlockSpec(shape, lambda b, c: (0,) * len(shape))

    def rowblk(b, c):
        return b * nc + (nc - 1 - c)

    hp_blocks = CHUNK // POOL_HALO
    T = nb * seq
    return pl.pallas_call(
        body, name="mixer_bwd", grid=(nb, nc),
        in_specs=[pl.BlockSpec((CHUNK, PROJ_W), lambda b, c: (rowblk(b, c), 0)),
                  pl.BlockSpec((POOL_HALO, POOL_W), lambda b, c: (jnp.maximum(rowblk(b, c) * hp_blocks - 1, 0), 0)),
                  pl.BlockSpec((CHUNK, CONV_CH), lambda b, c: (rowblk(b, c), 0)),
                  pl.BlockSpec((CHUNK, MIX_W), lambda b, c: (rowblk(b, c), 0)),
                  pl.BlockSpec((1, 1, N_GROUPS, N_STATE, GROUP_W), lambda b, c: (b, nc - 1 - c, 0, 0, 0)),
                  full((4, CONV_CH)), full((1, CONV_CH)), full((8, LANES)), full((1, D)),
                  full((4, LANES, LANES)), full((1, POOL_W))] + [ANY_SPEC] * na,
        out_specs=[pl.BlockSpec((CHUNK, PROJ_W), lambda b, c: (rowblk(b, c), 0)),
                   full((8, CONV_CH)), full((8, LANES)), full((8, D)), full((4, LANES, LANES))],
        out_shape=[jax.ShapeDtypeStruct((T, PROJ_W), BF16),
                   jax.ShapeDtypeStruct((8, CONV_CH), F32), jax.ShapeDtypeStruct((8, LANES), F32),
                   jax.ShapeDtypeStruct((8, D), F32), jax.ShapeDtypeStruct((4, LANES, LANES), F32)],
        scratch_shapes=[pltpu.VMEM((POOL_HALO, POOL_W), F32), pltpu.VMEM((CONV_HALO, CONV_CH), F32),
                        pltpu.VMEM((N_GROUPS, N_STATE, GROUP_W), F32), pltpu.VMEM((CHUNK, D), F32),
                        pltpu.VMEM((CHUNK, D), F32)],
        compiler_params=_cparams(),
    )(pm, pm, cvs, dym, hstates, cw, cb, hp, gssd, wpool, pscale, *after)


def _mlp_fused(x2, ymix, target, mod, g_mlp, g_final, w_out, w_up, w_down, seq):
    T = x2.shape[0]
    tm = min(256, seq)
    tps = seq // tm
    nblk = D_FF // FF_BLK

    def body(x_ref, ym_ref, tg_ref, mod_ref, gm_ref, gf_ref, wo_ref, wu_ref, wd_ref,
             da_ref, dym_ref, dh1_ref, u2_ref, f_ref, dup_ref, ddn_ref, dmod_ref, acc_ref, relu_scr):
        i = pl.program_id(0)

        @pl.when(i == 0)
        def _():
            acc_ref[...] = jnp.zeros_like(acc_ref)

        @pl.when(i % tps == 0)
        def _():
            dmod_ref[...] = jnp.zeros_like(dmod_ref)

        md = mod_ref[0]
        gate_m, shift_f, scale_f, gate_f = md[2:3], md[3:4], md[4:5], md[5:6]
        g_mlp, g_fin = gm_ref[...], gf_ref[...]
        a = jnp.dot(ym_ref[...], wo_ref[...], preferred_element_type=F32)
        h1 = x_ref[...] + gate_m * a
        r2 = lax.rsqrt(jnp.mean(h1 * h1, axis=-1, keepdims=True) + EPS)
        n2 = h1 * r2
        u2 = (n2 * g_mlp) * (1.0 + scale_f) + shift_f
        u2b = u2.astype(BF16)
        u2_ref[...] = u2b
        dn = jnp.zeros((tm, D), F32)
        for j in range(nblk):
            js = slice(j * FF_BLK, (j + 1) * FF_BLK)
            upj = jnp.maximum(jnp.dot(u2b, wu_ref[j], preferred_element_type=F32), 0.0)
            relu_scr[:, js] = upj
            fj = (upj * upj).astype(BF16)
            f_ref[:, js] = fj
            dn = dn + jnp.dot(fj, wd_ref[j], preferred_element_type=F32)
        h2 = h1 + gate_f * dn
        r3 = lax.rsqrt(jnp.mean(h2 * h2, axis=-1, keepdims=True) + EPS)
        n3 = h2 * r3
        err = n3 * g_fin - tg_ref[...]
        loss = 0.5 * jnp.sum(jnp.mean(err * err, axis=-1, keepdims=True), axis=0, keepdims=True)
        dout = err * (1.0 / D)
        d_gfin = jnp.sum(dout * n3, axis=0, keepdims=True)
        dn3 = dout * g_fin
        dh2 = r3 * (dn3 - n3 * jnp.mean(dn3 * n3, axis=-1, keepdims=True))
        d_gate_f = jnp.sum(dh2 * dn, axis=0, keepdims=True)
        ddn = (gate_f * dh2).astype(BF16)
        ddn_ref[...] = ddn
        du2 = jnp.zeros((tm, D), F32)
        for j in range(nblk):
            js = slice(j * FF_BLK, (j + 1) * FF_BLK)
            dfj = lax.dot_general(ddn, wd_ref[j], (((1,), (1,)), ((), ())), preferred_element_type=F32)
            dupj = (dfj * (2.0 * relu_scr[:, js])).astype(BF16)
            dup_ref[:, js] = dupj
            du2 = du2 + lax.dot_general(dupj, wu_ref[j], (((1,), (1,)), ((), ())), preferred_element_type=F32)
        d_scale_f = jnp.sum(du2 * (n2 * g_mlp), axis=0, keepdims=True)
        d_shift_f = jnp.sum(du2, axis=0, keepdims=True)
        d_gmlp = jnp.sum(du2 * (1.0 + scale_f) * n2, axis=0, keepdims=True)
        dn2 = du2 * (g_mlp * (1.0 + scale_f))
        dh1 = dh2 + r2 * (dn2 - n2 * jnp.mean(dn2 * n2, axis=-1, keepdims=True))
        dh1_ref[...] = dh1
        d_gate_m = jnp.sum(dh1 * a, axis=0, keepdims=True)
        da = (gate_m * dh1).astype(BF16)
        da_ref[...] = da
        dym_ref[...] = lax.dot_general(da, wo_ref[...], (((1,), (1,)), ((), ())),
                                       preferred_element_type=F32).astype(BF16)
        dmod_ref[0] += jnp.concatenate([jnp.zeros((2, D), F32), d_gate_m, d_shift_f, d_scale_f, d_gate_f,
                                        jnp.zeros((2, D), F32)], axis=0)
        acc_ref[...] += jnp.concatenate([d_gmlp, d_gfin, loss * jnp.ones((1, D), F32), jnp.zeros((5, D), F32)], axis=0)

    whole = pl.BlockSpec(memory_space=pltpu.VMEM)

    def tok(w):
        return pl.BlockSpec((tm, w), lambda i: (i, 0))

    def vec():
        return pl.BlockSpec((1, D), lambda i: (0, 0))

    nb = T // seq
    return pl.pallas_call(
        body, name="mlp_fused", grid=(T // tm,),
        in_specs=[tok(D), tok(MIX_W), tok(D), pl.BlockSpec((1, 8, D), lambda i: (i // tps, 0, 0)), vec(), vec(),
                  whole, whole, whole],
        out_specs=[tok(D), tok(MIX_W), tok(D), tok(D), tok(D_FF), tok(D_FF), tok(D),
                   pl.BlockSpec((1, 8, D), lambda i: (i // tps, 0, 0)), pl.BlockSpec((8, D), lambda i: (0, 0))],
        out_shape=[jax.ShapeDtypeStruct((T, D), BF16), jax.ShapeDtypeStruct((T, MIX_W), BF16),
                   jax.ShapeDtypeStruct((T, D), F32), jax.ShapeDtypeStruct((T, D), BF16),
                   jax.ShapeDtypeStruct((T, D_FF), BF16), jax.ShapeDtypeStruct((T, D_FF), BF16),
                   jax.ShapeDtypeStruct((T, D), BF16), jax.ShapeDtypeStruct((nb, 8, D), F32),
                   jax.ShapeDtypeStruct((8, D), F32)],
        scratch_shapes=[pltpu.VMEM((tm, D_FF), F32)],
        compiler_params=_cparams(),
    )(x2, ymix, target, mod, g_mlp, g_final, w_out, w_up, w_down)


def _in_bwd(x2, dh1, dpb, mod, g_mix, w_cat, dmod_a, acc_a, seq):
    T = x2.shape[0]
    tm = min(1024, seq)
    tps = seq // tm
    steps = T // tm

    def body(x_ref, dh_ref, dpb_ref, mod_ref, g_ref, w_ref, dma_ref, acca_ref, dx_ref, dmod_ref, acc_ref):
        i = pl.program_id(0)

        @pl.when(i == 0)
        def _():
            acc_ref[...] = acca_ref[...]

        @pl.when(i % tps == 0)
        def _():
            dmod_ref[...] = dma_ref[...]

        du = lax.dot_general(dpb_ref[...], w_ref[...], (((1,), (1,)), ((), ())), preferred_element_type=F32)
        x = x_ref[...]
        md = mod_ref[0]
        g = g_ref[...]
        r = lax.rsqrt(jnp.mean(x * x, axis=-1, keepdims=True) + EPS)
        n1 = x * r
        d_scale = jnp.sum(du * (n1 * g), axis=0, keepdims=True)
        d_shift = jnp.sum(du, axis=0, keepdims=True)
        d_g = jnp.sum(du * (1.0 + md[1:2]) * n1, axis=0, keepdims=True)
        dn1 = du * (g * (1.0 + md[1:2]))
        dx_ref[...] = dh_ref[...] + r * (dn1 - n1 * jnp.mean(dn1 * n1, axis=-1, keepdims=True))
        dmod_ref[0] += jnp.concatenate([d_shift, d_scale, jnp.zeros((6, D), F32)], axis=0)
        acc_ref[...] += jnp.concatenate([jnp.zeros((3, D), F32), d_g, jnp.zeros((4, D), F32)], axis=0)

    whole = pl.BlockSpec(memory_space=pltpu.VMEM)
    nb = T // seq
    return pl.pallas_call(
        body, name="in_bwd", grid=(steps,),
        in_specs=[pl.BlockSpec((tm, D), lambda i: (i, 0)), pl.BlockSpec((tm, D), lambda i: (i, 0)),
                  pl.BlockSpec((tm, PROJ_W), lambda i: (i, 0)),
                  pl.BlockSpec((1, 8, D), lambda i: (i // tps, 0, 0)), pl.BlockSpec((1, D), lambda i: (0, 0)),
                  whole, pl.BlockSpec((1, 8, D), lambda i: (i // tps, 0, 0)), pl.BlockSpec((8, D), lambda i: (0, 0))],
        out_specs=[pl.BlockSpec((tm, D), lambda i: (i, 0)),
                   pl.BlockSpec((1, 8, D), lambda i: (i // tps, 0, 0)), pl.BlockSpec((8, D), lambda i: (0, 0))],
        out_shape=[jax.ShapeDtypeStruct((T, D), F32),
                   jax.ShapeDtypeStruct((nb, 8, D), F32), jax.ShapeDtypeStruct((8, D), F32)],
        compiler_params=_cparams(),
    )(x2, dh1, dpb, mod, g_mix, w_cat, dmod_a, acc_a)


def _dw_in(u_b, dpb, in_cols, shards):
    T = u_b.shape[0]
    bk = min(1024, T)
    nk = T // bk
    ns = len(shards)
    starts = [(in_cols * j // LANES) * LANES for j in range(N_DEV)]
    assert all(s + DW_IN_WIN <= PROJ_W and in_cols * (j + 1) <= s + DW_IN_WIN for j, s in enumerate(starts))

    def body(*refs):
        u_ref, d_ref = refs[:2]
        sh_refs = refs[2:2 + ns]
        o_ref = refs[2 + ns]
        ga_refs = refs[3 + ns:3 + 2 * ns]
        acc, send, recv, loc = refs[3 + 2 * ns:]
        k = pl.program_id(0)
        gather = _Gather(sh_refs, ga_refs, send, recv, loc)
        gather.begin_hosted(k, nk)

        @pl.when(k == 0)
        def _():
            acc[...] = jnp.zeros_like(acc)

        ut = u_ref[...].T
        for j in range(N_DEV):
            acc[j] += jnp.dot(ut, d_ref[:, starts[j]:starts[j] + DW_IN_WIN], preferred_element_type=F32)

        @pl.when(k == nk - 1)
        def _():
            for j in range(N_DEV):
                off = in_cols * j - starts[j]
                o_ref[j] = acc[j][:, off:off + in_cols].astype(BF16)

        gather.end_hosted(k, nk)

    return pl.pallas_call(
        body, name="dw_in", grid=(nk,),
        in_specs=[pl.BlockSpec((bk, D), lambda k: (k, 0)), pl.BlockSpec((bk, PROJ_W), lambda k: (k, 0))]
        + [ANY_SPEC] * ns,
        out_specs=[pl.BlockSpec((N_DEV, D, in_cols), lambda k: (0, 0, 0))] + [ANY_SPEC] * ns,
        out_shape=[jax.ShapeDtypeStruct((N_DEV, D, in_cols), BF16)]
        + [jax.ShapeDtypeStruct((N_DEV,) + v.shape, v.dtype) for v in shards],
        scratch_shapes=[pltpu.VMEM((N_DEV, D, DW_IN_WIN), F32)] + _gather_scratch(shards),
        compiler_params=_cparams(),
    )(u_b, dpb, *shards)


def _dw_blocks(a, b, name, by_rows, per_step=1, after=()):
    T, M = a.shape
    N = b.shape[1]
    bk = min(4096, T)
    nk = T // bk
    whole = pl.BlockSpec(memory_space=pltpu.VMEM)
    if by_rows:
        rows = M // N_DEV
        am = rows * per_step
        nblk = N_DEV // per_step
        a_spec, b_spec = pl.BlockSpec((bk, am), lambda i, k: (k, i)), whole
        out_blk, acc_shape = (per_step, rows, N), (am, N)
    else:
        cols = N // N_DEV
        nblk = N_DEV
        a_spec, b_spec = whole, pl.BlockSpec((bk, cols), lambda i, k: (k, i))
        out_blk, acc_shape = (1, M, cols), (M, cols)

    def body(a_ref, b_ref, *rest):
        o_ref, acc = rest[len(after):]
        k = pl.program_id(1)

        @pl.when(k == 0)
        def _():
            acc[...] = jnp.zeros_like(acc)

        tok = pl.ds(pl.multiple_of(k * bk, bk), bk)
        a_blk = a_ref[...] if by_rows else a_ref[tok, :]
        b_blk = b_ref[tok, :] if by_rows else b_ref[...]
        acc[...] += lax.dot_general(a_blk, b_blk, (((0,), (0,)), ((), ())), preferred_element_type=F32)

        @pl.when(k == nk - 1)
        def _():
            o_ref[...] = acc[...].reshape(out_blk).astype(BF16)

    return pl.pallas_call(
        body, name=name, grid=(nblk, nk), in_specs=[a_spec, b_spec] + [ANY_SPEC] * len(after),
        out_specs=pl.BlockSpec(out_blk, lambda i, k: (i, 0, 0)),
        out_shape=jax.ShapeDtypeStruct((N_DEV,) + out_blk[1:], BF16),
        scratch_shapes=[pltpu.VMEM(acc_shape, F32)],
        compiler_params=_cparams(),
    )(a, b, *after)


def _adam_parts(parts, w, m, v, name):
    rows, cols = w.shape
    br = rows
    for cand in range(rows, 15, -16):
        if rows % cand == 0 and cand * cols * 4 <= ADAM_BLOCK_BYTES:
            br = cand
            break

    def body(p_ref, w_ref, m_ref, v_ref, g_out, dl_out, m_out, v_out):
        g = p_ref[0].astype(F32)
        for k in range(1, N_DEV):
            g = g + p_ref[k].astype(F32)
        g_out[...] = g
        dl, mn, vn = _adam_math(w_ref[...], g, m_ref[...], v_ref[...])
        dl_out[...] = dl
        m_out[...] = mn
        v_out[...] = vn

    wspec = pl.BlockSpec((br, cols), lambda i: (i, 0))
    return pl.pallas_call(
        body, name=name, grid=(rows // br,),
        in_specs=[pl.BlockSpec((N_DEV, br, cols), lambda i: (0, i, 0)), wspec, wspec, wspec],
        out_specs=[wspec] * 4, out_shape=[jax.ShapeDtypeStruct((rows, cols), F32)] * 4,
        compiler_params=_cparams(),
    )(parts, w, m, v)


def _adam_plain(g, w, m, v, name):
    def body(g_ref, w_ref, m_ref, v_ref, dl_out, m_out, v_out):
        dl, mn, vn = _adam_math(w_ref[...], g_ref[...], m_ref[...], v_ref[...])
        dl_out[...] = dl
        m_out[...] = mn
        v_out[...] = vn

    return pl.pallas_call(body, name=name, out_shape=[jax.ShapeDtypeStruct(w.shape, F32)] * 3,
                          compiler_params=_cparams())(g, w, m, v)


SMALL_PARAMS = ("b_ada", "g_mix", "conv_b", "dt_bias", "a_log", "d_skip", "g_ssd", "pool_scale", "g_mlp", "g_final")


def _small_adam(gathered, params):
    n_par = len(SMALL_PARAMS)
    nb = gathered[0].shape[1]

    def body(*refs):
        dmod_ref, acc_ref, conv_ref, vec_ref, hd_ref = refs[:5]
        par_refs = refs[5:5 + 3 * n_par]
        out_refs = refs[5 + 3 * n_par:5 + 7 * n_par]
        cw_out, acc_out = refs[5 + 7 * n_par:]

        def total(ref):
            t = ref[0]
            for k in range(1, N_DEV):
                t = t + ref[k]
            return t

        dm = total(dmod_ref)
        dmb = dm[0]
        for b in range(1, nb):
            dmb = dmb + dm[b]
        ac, cv, vc, hd = total(acc_ref), total(conv_ref), total(vec_ref), total(hd_ref)
        cw_out[...] = cv[0:4]
        acc_out[...] = ac
        grads = {
            "b_ada": jnp.concatenate([dmb[r:r + 1] for r in range(6)], axis=1), "g_mix": ac[3:4], "conv_b": cv[4:5],
            "dt_bias": hd[0:1, 0:N_HEADS], "a_log": hd[1:2, 0:N_HEADS], "d_skip": hd[2:3, 0:N_HEADS],
            "g_ssd": vc[0:1], "pool_scale": vc[1:2, 0:POOL_W], "g_mlp": ac[0:1], "g_final": ac[1:2],
        }
        for i, name in enumerate(SMALL_PARAMS):
            w_ref, m_ref, v_ref = par_refs[3 * i:3 * i + 3]
            g = grads[name]
            dl, mn, vn = _adam_math(w_ref[...], g, m_ref[...], v_ref[...])
            g_o, d_o, m_o, v_o = out_refs[4 * i:4 * i + 4]
            g_o[...] = g
            d_o[...] = dl
            m_o[...] = mn
            v_o[...] = vn

    flat = [a for name in SMALL_PARAMS for a in params[name]]
    out_shape = [jax.ShapeDtypeStruct(params[name][0].shape, F32) for name in SMALL_PARAMS for _ in range(4)]
    out_shape += [jax.ShapeDtypeStruct((4, CONV_CH), F32), jax.ShapeDtypeStruct((8, D), F32)]
    return pl.pallas_call(body, name="small_adam", out_shape=out_shape, compiler_params=_cparams())(*gathered, *flat)


def kernel(x, c, w_ada, b_ada, g_mix, w_in, conv_w, conv_b, dt_bias, a_log, d_skip, g_ssd, w_pool, pool_scale, w_out, g_mlp, w_up, w_down, g_final, loss_target, m_w_ada, m_b_ada, m_g_mix, m_w_in, m_conv_w, m_conv_b, m_dt_bias, m_a_log, m_d_skip, m_g_ssd, m_w_pool, m_pool_scale, m_w_out, m_g_mlp, m_w_up, m_w_down, m_g_final, v_w_ada, v_b_ada, v_g_mix, v_w_in, v_conv_w, v_conv_b, v_dt_bias, v_a_log, v_d_skip, v_g_ssd, v_w_pool, v_pool_scale, v_w_out, v_g_mlp, v_w_up, v_w_down, v_g_final):
    nb, seq, _ = x.shape
    T = nb * seq
    me = 4 * lax.axis_index("x") + 2 * lax.axis_index("y") + lax.axis_index("c")
    in_cols = w_in.shape[2]
    ada_cols = w_ada.shape[2]
    cw_cols = conv_w.shape[2]

    win_t = jnp.pad(w_in[0].astype(BF16).T, ((0, -in_cols % 16), (0, 0)))
    c_g, cw_g, win_g = _all_gather([c, conv_w[0], win_t], "ag_first")
    c_all = c_g.reshape(N_DEV * nb, D)
    cw_full = cw_g.transpose(1, 0, 2).reshape(4, CONV_CH)

    b_slice = lax.dynamic_slice(b_ada, (0, me * ada_cols), (1, ada_cols))
    mod_cols = _ada_fwd(c_all, w_ada[0], b_slice)
    (mod_g,) = _all_gather([mod_cols], "ag_mod")
    mod_all = mod_g.transpose(1, 0, 2).reshape(N_DEV * nb, 6, D)
    mod_mine = lax.dynamic_slice(mod_all, (me * nb, 0, 0), (nb, 6, D))
    mod = jnp.pad(mod_mine, ((0, 0), (0, 2), (0, 0)))

    x2 = x.reshape(T, D)
    tg2 = loss_target.reshape(T, D)
    heads = jnp.pad(jnp.concatenate([dt_bias, a_log, d_skip], axis=0), ((0, 5), (0, LANES - N_HEADS)))
    wpool_b = w_pool[0]
    u_b, pm, w_cat, wup_g = _mix_in(x2, mod, g_mix, win_g, in_cols, seq, [w_up[0].astype(BF16)])
    ymix, hstates, cvs, wout_g, wdn_g = _mixer_fwd(
        pm, cw_full, conv_b, heads, g_ssd, wpool_b, pool_scale, nb, seq,
        [w_out[0].astype(BF16), w_down[0].astype(BF16)])
    da_b, dym, dh1, u2_b, f_b, dup_b, ddn_b, dmod_a, acc_a = _mlp_fused(
        x2, ymix, tg2, mod, g_mlp, g_final.reshape(1, D), wout_g.reshape(MIX_W, D), wup_g, wdn_g, seq)

    gout_p = _dw_blocks(ymix, da_b, "dw_out", True, per_step=4)
    gup_p = _dw_blocks(u2_b, dup_b, "dw_up", False)
    ex_a = _exchange_start([gout_p, gup_p], "ga_start")
    gdn_p = _dw_blocks(f_b, ddn_b, "dw_down", True, after=[ex_a[4]])
    ex_b = _exchange_start([gdn_p], "gb_start")
    dpb, d_conv, d_heads, d_vec, d_wpool = _mixer_bwd(
        pm, cvs, dym, hstates, cw_full, conv_b, heads, g_ssd, wpool_b, pool_scale, nb, seq, after=[ex_b[4]])
    gin_p, conv_g, vec_g, heads_g, wpool_parts = _dw_in(
        u_b, dpb, in_cols, [d_conv, d_vec, d_heads, d_wpool.reshape(4 * LANES, LANES)])
    ex_in = _exchange_start([gin_p], "gin_start")
    grad_x2, dmod, acc = _in_bwd(x2, dh1, dpb, mod, g_mix + ex_in[4][0:1, 0:1], w_cat, dmod_a, acc_a, seq)

    gout_r, gup_r = _exchange_wait(ex_a, dmod, "ga_wait", me)
    (gdn_r,) = _exchange_wait(ex_b, dmod, "gb_wait", me)
    g_out, d_out, nm_out, nv_out = _adam_parts(gout_r, w_out[0], m_w_out[0], v_w_out[0], "adam_w_out")
    g_up, d_up, nm_up, nv_up = _adam_parts(gup_r, w_up[0], m_w_up[0], v_w_up[0], "adam_w_up")
    g_dn, d_dn, nm_dn, nv_dn = _adam_parts(gdn_r, w_down[0], m_w_down[0], v_w_down[0], "adam_w_down")

    dmod_g, acc_g = _all_gather([dmod, acc], "ag_small_bwd", after=[nm_out, nm_up, nm_dn])
    pool2 = (4 * LANES, LANES)
    wpool_outs = _adam_parts(wpool_parts, w_pool.reshape(pool2), m_w_pool.reshape(pool2), v_w_pool.reshape(pool2),
                             "adam_w_pool")
    small_params = {
        "b_ada": (b_ada, m_b_ada, v_b_ada), "g_mix": (g_mix, m_g_mix, v_g_mix), "conv_b": (conv_b, m_conv_b, v_conv_b),
        "dt_bias": (dt_bias, m_dt_bias, v_dt_bias), "a_log": (a_log, m_a_log, v_a_log),
        "d_skip": (d_skip, m_d_skip, v_d_skip), "g_ssd": (g_ssd, m_g_ssd, v_g_ssd),
        "pool_scale": (pool_scale, m_pool_scale, v_pool_scale), "g_mlp": (g_mlp, m_g_mlp, v_g_mlp),
        "g_final": tuple(a.reshape(1, D) for a in (g_final, m_g_final, v_g_final)),
    }
    small_res = _small_adam([dmod_g, acc_g, conv_g, vec_g, heads_g], small_params)
    g_cw_full, acc_sum = small_res[-2:]
    loss = acc_sum[2, 0]

    g_cw = lax.dynamic_slice(g_cw_full, (0, me * cw_cols), (4, cw_cols))
    d_cwp, nm_cwp, nv_cwp = _adam_plain(g_cw, conv_w[0], m_conv_w[0], v_conv_w[0], "adam_conv_w")

    dmod_all = dmod_g[:, :, 0:6].reshape(N_DEV * nb, 6 * D)
    dmod_slice = lax.dynamic_slice(dmod_all, (0, me * ada_cols), (N_DEV * nb, ada_cols))
    g_ada, d_ada, nm_ada, nv_ada = _ada_bwd_adam(c_all, dmod_slice, w_ada[0], m_w_ada[0], v_w_ada[0])

    ex_after = nm_ada[0:8, 0:LANES] + acc_sum[:, 0:LANES]
    (gin_r,) = _exchange_wait(ex_in, ex_after, "gin_wait", me)
    g_in, d_in, nm_in, nv_in = _adam_parts(gin_r, w_in[0], m_w_in[0], v_w_in[0], "adam_w_in")

    def small_outs(kind, wpool):
        res = {name: small_res[4 * i + kind] for i, name in enumerate(SMALL_PARAMS)}
        res["g_final"] = res["g_final"].reshape(D)
        res["w_pool"] = wpool.reshape(1, 4, LANES, LANES)
        return res

    def big_outs(ada, win, cwp, wout, wup, wdn):
        return {"w_ada": ada[None], "w_in": win.reshape(1, D, in_cols), "conv_w": cwp[None], "w_out": wout[None],
                "w_up": wup[None], "w_down": wdn[None]}

    order = ["w_ada", "b_ada", "g_mix", "w_in", "conv_w", "conv_b", "dt_bias", "a_log", "d_skip", "g_ssd", "w_pool",
             "pool_scale", "w_out", "g_mlp", "w_up", "w_down", "g_final"]
    groups = [
        {**small_outs(0, wpool_outs[0]), **big_outs(g_ada, g_in, g_cw, g_out, g_up, g_dn)},
        {**small_outs(1, wpool_outs[1]), **big_outs(d_ada, d_in, d_cwp, d_out, d_up, d_dn)},
        {**small_outs(2, wpool_outs[2]), **big_outs(nm_ada, nm_in, nm_cwp, nm_out, nm_up, nm_dn)},
        {**small_outs(3, wpool_outs[3]), **big_outs(nv_ada, nv_in, nv_cwp, nv_out, nv_up, nv_dn)},
    ]
    outs = [loss, grad_x2.reshape(nb, seq, D)]
    for grp in groups:
        outs += [grp[n] for n in order]
    return tuple(outs)
```

```python
import functools

import jax
import jax.numpy as jnp
from jax import lax
from jax.experimental import pallas as pl
from jax.experimental.pallas import tpu as pltpu

F32, BF16 = jnp.float32, jnp.bfloat16
MESH = pl.DeviceIdType.MESH
N_DEV = 8
D = 1024
LANES = 128
CHUNK = 128
POOL_W = 512
WINDOWS = (2, 4, 8, 16)
N_HEADS = 16
HEAD_DIM = 64
N_GROUPS = 2
GROUP_W = 512
N_STATE = 128
CONV_CH = 1536
OFF_Z, OFF_XBC, OFF_DT, IN_W = 512, 1536, 3072, 3088
PROJ_W = OFF_DT + LANES
MIX_W = 1536
D_FF = 4096
FF_BLK = 512
EPS = 1e-5
LR, B1, B2, AEPS, WD, STEP = 0.001, 0.9, 0.999, 1e-08, 0.01, 10
POOL_HALO = 16
CONV_HALO = 8
VMEM_LIMIT = 56 << 20
ADAM_BLOCK_BYTES = 512 << 10
DW_IN_WIN = 512


def _cparams(**kw):
    return pltpu.CompilerParams(vmem_limit_bytes=VMEM_LIMIT, **kw)


def _mm(a, b):
    return jnp.dot(a.astype(BF16), b.astype(BF16), preferred_element_type=F32)


def _mm_nt(a, b):
    return lax.dot_general(a.astype(BF16), b.astype(BF16), (((1,), (1,)), ((), ())), preferred_element_type=F32)


def _mm_tn(a, b):
    return lax.dot_general(a.astype(BF16), b.astype(BF16), (((0,), (0,)), ((), ())), preferred_element_type=F32)


def _split_bf16(v, terms):
    parts, rest = [], v
    for t in range(terms):
        p = rest.astype(BF16)
        parts.append(p)
        if t + 1 < terms:
            rest = rest - p.astype(F32)
    return parts


def _dot01(a, b, terms, split_lhs=True):
    if split_lhs:
        bb = b.astype(BF16)
        prods = [jnp.dot(p, bb, preferred_element_type=F32) for p in _split_bf16(a, terms)]
    else:
        ab = a.astype(BF16)
        prods = [jnp.dot(ab, p, preferred_element_type=F32) for p in _split_bf16(b, terms)]
    out = prods[0]
    for q in prods[1:]:
        out = out + q
    return out


def _sigmoid(v):
    return 1.0 / (1.0 + jnp.exp(-v))


def _expand_mat():
    r = lax.broadcasted_iota(jnp.int32, (LANES, D), 0)
    c = lax.broadcasted_iota(jnp.int32, (LANES, D), 1)
    return (r == c // HEAD_DIM).astype(F32)


def _reduce_mat():
    r = lax.broadcasted_iota(jnp.int32, (D, LANES), 0)
    c = lax.broadcasted_iota(jnp.int32, (D, LANES), 1)
    return (c == r // HEAD_DIM).astype(F32)


def _pos():
    return lax.axis_index("x"), lax.axis_index("y"), lax.axis_index("c")


GATHER_PIECES = 4
GATHER_PIECE_BYTES = 96 << 10


def _pieces(shape, dtype):
    rows = shape[0]
    size = jnp.dtype(dtype).itemsize
    for d in shape:
        size *= d
    whole_tiles = rows % (GATHER_PIECES * 16) == 0
    return GATHER_PIECES if whole_tiles and size // GATHER_PIECES >= GATHER_PIECE_BYTES else 1


class _Gather:
    def __init__(self, x_refs, o_refs, send, recv, loc):
        self.x_refs, self.o_refs, self.send, self.recv, self.loc = x_refs, o_refs, send, recv, loc
        self.n = len(x_refs)
        self.pieces = [_pieces(r.shape, r.dtype) for r in x_refs]
        self.base = [7 * sum(self.pieces[:a]) for a in range(self.n)]
        x, y, c = _pos()
        self.c = c
        self.me, self.sib = (x, y, c), (x, y, 1 - c)
        self.chips = [(1 - x, y), (x, 1 - y), (1 - x, 1 - y)]

    def _rows(self, a, p):
        rows = self.x_refs[a].shape[0] // self.pieces[a]
        return pl.ds(p * rows, rows)

    def _cp(self, a, p, k, block, to, own=False):
        dst = self.o_refs[a].at[4 * block[0] + 2 * block[1] + block[2], self._rows(a, p)]
        sem = self.base[a] + 7 * p + k
        return pltpu.make_async_remote_copy(
            src_ref=self.x_refs[a].at[self._rows(a, p)] if own else dst, dst_ref=dst,
            send_sem=self.send.at[sem], recv_sem=self.recv.at[sem], device_id=to, device_id_type=MESH)

    def _mine(self, a):
        me = self.me
        return pltpu.make_async_copy(self.x_refs[a], self.o_refs[a].at[4 * me[0] + 2 * me[1] + me[2]], self.loc.at[a])

    def _first(self, a, p):
        cps = [self._cp(a, p, 0, self.me, self.sib, own=True)]
        return cps + [self._cp(a, p, 1 + j, self.me, (*chip, self.c), own=True) for j, chip in enumerate(self.chips)]

    def _passed(self, a, p, j):
        return self._cp(a, p, 4 + j, (*self.chips[j], self.c), self.sib)

    def start(self):
        for a in range(self.n):
            self._mine(a).start()
        for p in range(max(self.pieces)):
            for a in range(self.n):
                if p < self.pieces[a]:
                    for cp in self._first(a, p):
                        cp.start()

    def forward(self, p):
        for j, chip in enumerate(self.chips):
            for a in range(self.n):
                if p < self.pieces[a]:
                    self._cp(a, p, 1 + j, (*chip, self.c), self.me).wait_recv()
                    self._passed(a, p, j).start()

    def finish(self):
        for a in range(self.n):
            for p in range(self.pieces[a]):
                self._cp(a, p, 0, self.sib, self.me).wait_recv()
                for j, chip in enumerate(self.chips):
                    self._cp(a, p, 4 + j, (*chip, 1 - self.c), self.me).wait_recv()
        for a in range(self.n):
            for p in range(self.pieces[a]):
                for cp in self._first(a, p):
                    cp.wait_send()
                for j in range(3):
                    self._passed(a, p, j).wait_send()
            self._mine(a).wait()

    def begin_hosted(self, step, steps):
        @pl.when(step == 0)
        def _():
            self.start()

        n_p = max(self.pieces)
        for p in range(n_p):
            @pl.when(step == min(((p + 1) * 7 * steps) // (8 * n_p), steps - 1))
            def _():
                self.forward(p)

    def end_hosted(self, step, steps):
        @pl.when(step == steps - 1)
        def _():
            self.finish()


class _Exchange:
    def __init__(self, x_refs, o_refs, send, recv, loc):
        self.x_refs, self.o_refs, self.send, self.recv, self.loc = x_refs, o_refs, send, recv, loc
        self.n = len(x_refs)
        x, y, c = _pos()
        self.me_i = 4 * x + 2 * y + c
        self.peers = []
        for k in range(1, N_DEV):
            px = 1 - x if (k >> 2) & 1 else x
            py = 1 - y if (k >> 1) & 1 else y
            pc = 1 - c if k & 1 else c
            self.peers.append(((px, py, pc), 4 * px + 2 * py + pc))

    def _mine(self, a):
        return pltpu.make_async_copy(self.x_refs[a].at[self.me_i], self.o_refs[a].at[self.me_i], self.loc.at[a])

    def _cp(self, a, k, landing):
        peer, peer_i = self.peers[k]
        return pltpu.make_async_remote_copy(
            src_ref=self.x_refs[a].at[peer_i], dst_ref=self.o_refs[a].at[landing],
            send_sem=self.send.at[a * 7 + k], recv_sem=self.recv.at[a * 7 + k],
            device_id=peer, device_id_type=MESH)

    def start(self):
        for a in range(self.n):
            self._mine(a).start()
            for k in range(N_DEV - 1):
                self._cp(a, k, self.me_i).start()

    def finish(self):
        for a in range(self.n):
            for k in range(N_DEV - 1):
                self._cp(a, k, self.peers[k][1]).wait_recv()
        for a in range(self.n):
            for k in range(N_DEV - 1):
                self._cp(a, k, self.me_i).wait_send()
            self._mine(a).wait()


def _gather_scratch(xs):
    n_sem = 7 * sum(_pieces(v.shape, v.dtype) for v in xs)
    return [pltpu.SemaphoreType.DMA((n_sem,)), pltpu.SemaphoreType.DMA((n_sem,)), pltpu.SemaphoreType.DMA((len(xs),))]


ANY_SPEC = pl.BlockSpec(memory_space=pl.ANY)


def _all_gather(xs, name, after=()):
    n, na = len(xs), len(after)

    def body(*refs):
        g = _Gather(refs[:n], refs[n + na:2 * n + na], *refs[2 * n + na:])
        g.start()
        for p in range(max(g.pieces)):
            g.forward(p)
        g.finish()

    return pl.pallas_call(
        body, name=name,
        out_shape=[jax.ShapeDtypeStruct((N_DEV,) + v.shape, v.dtype) for v in xs],
        in_specs=[ANY_SPEC] * (n + na), out_specs=[ANY_SPEC] * n, scratch_shapes=_gather_scratch(xs),
    )(*xs, *after)


HBM_SPEC = pl.BlockSpec(memory_space=pltpu.HBM)
SEM_SPEC = pl.BlockSpec(memory_space=pltpu.SEMAPHORE)
VMEM_SPEC = pl.BlockSpec(memory_space=pltpu.VMEM)
SPLIT_EFFECT = pltpu.SideEffectType.DATAFLOW_SIDE_EFFECTING


def _in_hbm(v):
    return pltpu.with_memory_space_constraint(v, pltpu.HBM)


def _exchange_start(blocks, name):
    n = len(blocks)

    def body(*refs):
        x_refs, land_refs = refs[:n], refs[n:2 * n]
        send, recv = refs[2 * n:2 * n + 2]
        token = refs[-1]
        ex = _Exchange(x_refs, land_refs, send, recv, None)
        for a in range(n):
            for k in range(N_DEV - 1):
                ex._cp(a, k, ex.me_i).start()
        token[...] = jnp.zeros_like(token)

    lands = [lax.empty(v.shape, v.dtype) for v in blocks]
    hbm = tuple(pltpu.HBM(v.shape, v.dtype) for v in list(blocks) + lands)
    n_sem = (N_DEV - 1) * n
    out = pl.pallas_call(
        body, name=name,
        out_shape=(pltpu.SemaphoreType.DMA((n_sem,)), pltpu.SemaphoreType.DMA((n_sem,))) + hbm
        + (jax.ShapeDtypeStruct((8, LANES), F32),),
        in_specs=(HBM_SPEC,) * (2 * n), out_specs=(SEM_SPEC, SEM_SPEC) + (HBM_SPEC,) * (2 * n) + (VMEM_SPEC,),
        input_output_aliases={i: i + 2 for i in range(2 * n)},
        compiler_params=pltpu.CompilerParams(has_side_effects=SPLIT_EFFECT),
    )(*[_in_hbm(v) for v in list(blocks) + lands])
    return out[0], out[1], list(out[2:2 + n]), list(out[2 + n:2 + 2 * n]), out[-1]


def _exchange_wait(ex, after, name, me):
    send, recv, thru, lands, _ = ex
    n = len(thru)

    def body(*refs):
        x_refs, land_refs = refs[:n], refs[n:2 * n]
        send_ref, recv_ref = refs[2 * n:2 * n + 2]
        e = _Exchange(x_refs, land_refs, send_ref, recv_ref, None)
        for a in range(n):
            for k in range(N_DEV - 1):
                e._cp(a, k, e.me_i).wait_send()
                e._cp(a, k, e.peers[k][1]).wait_recv()

    hbm = tuple(pltpu.HBM(v.shape, v.dtype) for v in list(thru) + list(lands))
    out = pl.pallas_call(
        body, name=name, out_shape=hbm,
        in_specs=(HBM_SPEC,) * (2 * n) + (SEM_SPEC, SEM_SPEC, ANY_SPEC), out_specs=(HBM_SPEC,) * (2 * n),
        input_output_aliases={i: i for i in range(2 * n)},
        compiler_params=pltpu.CompilerParams(has_side_effects=SPLIT_EFFECT),
    )(*thru, *lands, send, recv, after)
    done = []
    for own, land in zip(out[:n], out[n:]):
        mine = lax.dynamic_slice(own, (me,) + (0,) * (own.ndim - 1), (1,) + own.shape[1:])
        done.append(lax.dynamic_update_slice(land, mine, (me,) + (0,) * (own.ndim - 1)))
    return done


def _ada_fwd(c_all, w_ada, b_slice):
    def body(c_ref, w_ref, b_ref, o_ref):
        cv = c_ref[...]
        act = cv * _sigmoid(cv)
        o_ref[...] = _mm(act, w_ref[...]) + b_ref[...]

    nb, nc = c_all.shape[0], w_ada.shape[1]
    return pl.pallas_call(body, name="ada_fwd", out_shape=jax.ShapeDtypeStruct((nb, nc), F32),
                          compiler_params=_cparams())(c_all, w_ada, b_slice)


def _adam_math(w, g, m, v):
    m = B1 * m + (1.0 - B1) * g
    v = B2 * v + (1.0 - B2) * jnp.square(g)
    m_hat = m / (1.0 - B1 ** STEP)
    v_hat = v / (1.0 - B2 ** STEP)
    delta = -LR * (m_hat / (jnp.sqrt(v_hat) + AEPS) + WD * w)
    return delta, m, v


def _ada_bwd_adam(c_all, dmod_slice, w, m, v):
    rows, cols = w.shape
    br = 256

    def body(c_ref, d_ref, w_ref, m_ref, v_ref, g_out, dl_out, m_out, v_out):
        cv = c_ref[...]
        act = cv * _sigmoid(cv)
        g = _mm_tn(act, d_ref[...])
        g_out[...] = g
        dl, mn, vn = _adam_math(w_ref[...], g, m_ref[...], v_ref[...])
        dl_out[...] = dl
        m_out[...] = mn
        v_out[...] = vn

    nb = c_all.shape[0]
    wspec = pl.BlockSpec((br, cols), lambda i: (i, 0))
    return pl.pallas_call(
        body, name="ada_bwd_adam", grid=(rows // br,),
        in_specs=[pl.BlockSpec((nb, br), lambda i: (0, i)), pl.BlockSpec((nb, cols), lambda i: (0, 0)),
                  wspec, wspec, wspec],
        out_specs=[wspec] * 4, out_shape=[jax.ShapeDtypeStruct((rows, cols), F32)] * 4,
        compiler_params=_cparams(),
    )(c_all, dmod_slice, w, m, v)


def _mix_in(x2, mod, g_mix, win_g, in_cols, seq, shards):
    T = x2.shape[0]
    tm = min(512, seq)
    tps = seq // tm
    blk_rows = win_g.shape[1]
    pad_rows = -blk_rows % (4 * LANES)
    ns = len(shards)
    steps = T // tm

    def body(*refs):
        x_ref, mod_ref, g_ref, wb_ref = refs[:4]
        sh_refs = refs[4:4 + ns]
        u_ref, pm_ref, wc_ref = refs[4 + ns:7 + ns]
        ga_refs = refs[7 + ns:7 + 2 * ns]
        w_ref, send, recv, loc = refs[7 + 2 * ns:]
        step = pl.program_id(0)
        gather = _Gather(sh_refs, ga_refs, send, recv, loc)
        gather.begin_hosted(step, steps)

        @pl.when(step == 0)
        def _():
            w_ref[:, OFF_DT:] = jnp.zeros((D, PROJ_W - OFF_DT), BF16)
            for j in range(N_DEV):
                blk = jnp.concatenate([wb_ref[j], jnp.zeros((pad_rows, D), BF16)], axis=0)
                w_ref[:, in_cols * j:in_cols * (j + 1)] = blk.T[:, :in_cols]
            wc_ref[...] = w_ref[...]

        x = x_ref[...]
        r = lax.rsqrt(jnp.mean(x * x, axis=-1, keepdims=True) + EPS)
        md = mod_ref[0]
        u = (x * r * g_ref[...]) * (1.0 + md[1:2]) + md[0:1]
        ub = u.astype(BF16)
        u_ref[...] = ub
        pm_ref[...] = jnp.dot(ub, w_ref[...], preferred_element_type=F32)
        gather.end_hosted(step, steps)

    whole = pl.BlockSpec(memory_space=pltpu.VMEM)
    return pl.pallas_call(
        body, name="mix_in", grid=(T // tm,),
        in_specs=[pl.BlockSpec((tm, D), lambda i: (i, 0)), pl.BlockSpec((1, 8, D), lambda i: (i // tps, 0, 0)),
                  pl.BlockSpec((1, D), lambda i: (0, 0)), whole] + [ANY_SPEC] * ns,
        out_specs=[pl.BlockSpec((tm, D), lambda i: (i, 0)), pl.BlockSpec((tm, PROJ_W), lambda i: (i, 0)),
                   pl.BlockSpec((D, PROJ_W), lambda i: (0, 0))] + [ANY_SPEC] * ns,
        out_shape=[jax.ShapeDtypeStruct((T, D), BF16), jax.ShapeDtypeStruct((T, PROJ_W), F32),
                   jax.ShapeDtypeStruct((D, PROJ_W), BF16)]
        + [jax.ShapeDtypeStruct((N_DEV,) + v.shape, v.dtype) for v in shards],
        scratch_shapes=[pltpu.VMEM((D, PROJ_W), BF16)] + _gather_scratch(shards),
        compiler_params=_cparams(),
    )(x2, mod, g_mix, win_g, *shards)


def _chunk_forward(up, z, ux, dtin, halo_p, halo_x, hprev, cw, cb, hp, gssd, wpool, pscale, t0, y_scr, cv=None):
    L = CHUNK
    out = {}
    row = lax.broadcasted_iota(jnp.int32, (L, 1), 0)
    t = (t0 + row + 1).astype(F32)
    e = jnp.concatenate([halo_p, up], axis=0)
    s2 = e + pltpu.roll(e, 1, 0)
    s4 = s2 + pltpu.roll(s2, 2, 0)
    s8 = s4 + pltpu.roll(s4, 4, 0)
    s16 = s8 + pltpu.roll(s8, 8, 0)
    sums = (s2, s4, s8, s16)
    p, inv, yp = [], [], []
    for gi, w in enumerate(WINDOWS):
        sl = slice(gi * LANES, (gi + 1) * LANES)
        ic = 1.0 / jnp.minimum(t, float(w))
        pg = sums[gi][POOL_HALO:, sl] * ic - up[:, sl]
        p.append(pg)
        inv.append(ic)
        yp.append(_mm(pg, wpool[gi]))
    out["p"], out["inv"], out["yp"] = p, inv, yp
    out["y_pool"] = jnp.concatenate(yp, axis=1) * pscale
    if cv is None:
        ex = jnp.concatenate([halo_x, ux], axis=0)
        taps = [pltpu.roll(ex, 3, 0)[CONV_HALO:], pltpu.roll(ex, 2, 0)[CONV_HALO:], pltpu.roll(ex, 1, 0)[CONV_HALO:], ux]
        cv = cb + taps[0] * cw[0:1] + taps[1] * cw[1:2] + taps[2] * cw[2:3] + taps[3] * cw[3:4]
    sg = _sigmoid(cv)
    xbc = cv * sg
    out["cv"], out["sg"] = cv, sg
    X = xbc[:, :D]
    Bm = xbc[:, D:D + N_GROUPS * N_STATE]
    Cm = xbc[:, D + N_GROUPS * N_STATE:]
    pre = dtin + hp[0:1]
    dt = jnp.maximum(pre, 0.0) + jnp.log(1.0 + jnp.exp(-jnp.abs(pre)))
    a_row = -jnp.exp(hp[1:2])
    da = dt * a_row
    ri = lax.broadcasted_iota(jnp.int32, (L, L), 0)
    ci = lax.broadcasted_iota(jnp.int32, (L, L), 1)
    causal = ri >= ci
    cum = _dot01(causal.astype(F32), da, 3, split_lhs=False)
    cum_t = cum.T
    cum_last = cum[L - 1:L]
    eo = jnp.exp(cum)
    dec = jnp.exp(cum_last - cum)
    cd = jnp.exp(cum_last)
    exm = _expand_mat()
    rows8 = jnp.concatenate([cd, hp[2:3], jnp.zeros((6, LANES), F32)], axis=0)
    rep = _dot01(jnp.concatenate([dt, eo, dec, rows8], axis=0), exm, 2)
    dt_rep, eo_rep, dec_rep = rep[0:L], rep[L:2 * L], rep[2 * L:3 * L]
    cd_rep, dskip_rep = rep[3 * L:3 * L + 1], rep[3 * L + 1:3 * L + 2]
    xdt = X * dt_rep
    out.update(X=X, Bm=Bm, Cm=Cm, pre=pre, dt=dt, a_row=a_row, cum=cum, cum_t=cum_t, eo=eo, dec=dec, cd=cd,
               dt_rep=dt_rep, eo_rep=eo_rep, dec_rep=dec_rep, cd_rep=cd_rep, dskip_rep=dskip_rep, xdt=xdt,
               causal=causal, anti=(ri <= ci).astype(F32), exm=exm)
    G, lms, yoff, hnew, xdec = [], [], [], [], []
    for g in range(N_GROUPS):
        gs = slice(g * GROUP_W, (g + 1) * GROUP_W)
        Bg = Bm[:, g * N_STATE:(g + 1) * N_STATE]
        Cg = Cm[:, g * N_STATE:(g + 1) * N_STATE]
        Gg = _mm_nt(Cg, Bg)
        G.append(Gg)
        for hh in range(N_HEADS // N_GROUPS):
            h = g * (N_HEADS // N_GROUPS) + hh
            seg = cum[:, h:h + 1] - cum_t[h:h + 1, :]
            lm = jnp.where(causal, jnp.exp(jnp.minimum(seg, 0.0)), 0.0)
            lms.append(lm)
            hs = slice(h * HEAD_DIM, (h + 1) * HEAD_DIM)
            y_scr[:, hs] = _mm(Gg * lm, xdt[:, hs])
        xd = xdt[:, gs] * dec_rep[:, gs]
        xdec.append(xd)
        sgm = _mm_tn(Bg, xd)
        yoff.append(_mm(Cg, hprev[g]) * eo_rep[:, gs])
        hnew.append(hprev[g] * cd_rep[:, gs] + sgm)
    out.update(G=G, lms=lms, yoff=yoff, hnew=hnew, xdec=xdec)
    y = y_scr[...] + jnp.concatenate(yoff, axis=1) + dskip_rep * X
    sz = _sigmoid(z)
    silz = z * sz
    yz = y * silz
    rg, yn = [], []
    for g in range(N_GROUPS):
        gs = slice(g * GROUP_W, (g + 1) * GROUP_W)
        r = lax.rsqrt(jnp.mean(yz[:, gs] * yz[:, gs], axis=-1, keepdims=True) + EPS)
        rg.append(r)
        yn.append(yz[:, gs] * r)
    yn = jnp.concatenate(yn, axis=1)
    out.update(y=y, sz=sz, silz=silz, rg=rg, yn=yn)
    out["y_ssd"] = yn * gssd
    return out


def _mixer_fwd(pm, cw, cb, hp, gssd, wpool, pscale, nb, seq, shards):
    nc = seq // CHUNK
    ns = len(shards)
    steps = nb * nc

    def body(*refs):
        pm_ref, cw_ref, cb_ref, hp_ref, gs_ref, wp_ref, ps_ref = refs[:7]
        sh_refs = refs[7:7 + ns]
        ym_ref, hs_ref, cv_ref = refs[7 + ns:10 + ns]
        ga_refs = refs[10 + ns:10 + 2 * ns]
        halo_p, halo_x, state, y_scr, send, recv, loc = refs[10 + 2 * ns:]
        c = pl.program_id(1)
        step = pl.program_id(0) * nc + c
        gather = _Gather(sh_refs, ga_refs, send, recv, loc)
        gather.begin_hosted(step, steps)

        @pl.when(c == 0)
        def _():
            halo_p[...] = jnp.zeros_like(halo_p)
            halo_x[...] = jnp.zeros_like(halo_x)
            state[...] = jnp.zeros_like(state)

        up = pm_ref[:, 0:POOL_W]
        z = pm_ref[:, OFF_Z:OFF_XBC]
        ux = pm_ref[:, OFF_XBC:OFF_DT]
        hprev = [state[0], state[1]]
        hs_ref[0, 0, 0] = hprev[0]
        hs_ref[0, 0, 1] = hprev[1]
        o = _chunk_forward(up, z, ux, pm_ref[:, OFF_DT:], halo_p[...], halo_x[...], hprev, cw_ref[...], cb_ref[...],
                           hp_ref[...], gs_ref[...], wp_ref[...], ps_ref[...], c * CHUNK, y_scr)
        ym_ref[:, 0:POOL_W] = o["y_pool"].astype(BF16)
        ym_ref[:, POOL_W:] = o["y_ssd"].astype(BF16)
        cv_ref[...] = o["cv"]
        state[0] = o["hnew"][0]
        state[1] = o["hnew"][1]
        halo_p[...] = up[CHUNK - POOL_HALO:]
        halo_x[...] = ux[CHUNK - CONV_HALO:]
        gather.end_hosted(step, steps)

    def full(shape):
        return pl.BlockSpec(shape, lambda b, c: (0,) * len(shape))

    T = nb * seq
    return pl.pallas_call(
        body, name="mixer_fwd", grid=(nb, nc),
        in_specs=[pl.BlockSpec((CHUNK, PROJ_W), lambda b, c: (b * nc + c, 0)),
                  full((4, CONV_CH)), full((1, CONV_CH)), full((8, LANES)), full((1, D)),
                  full((4, LANES, LANES)), full((1, POOL_W))] + [ANY_SPEC] * ns,
        out_specs=[pl.BlockSpec((CHUNK, MIX_W), lambda b, c: (b * nc + c, 0)),
                   pl.BlockSpec((1, 1, N_GROUPS, N_STATE, GROUP_W), lambda b, c: (b, c, 0, 0, 0)),
                   pl.BlockSpec((CHUNK, CONV_CH), lambda b, c: (b * nc + c, 0))] + [ANY_SPEC] * ns,
        out_shape=[jax.ShapeDtypeStruct((T, MIX_W), BF16),
                   jax.ShapeDtypeStruct((nb, nc, N_GROUPS, N_STATE, GROUP_W), F32),
                   jax.ShapeDtypeStruct((T, CONV_CH), F32)]
        + [jax.ShapeDtypeStruct((N_DEV,) + v.shape, v.dtype) for v in shards],
        scratch_shapes=[pltpu.VMEM((POOL_HALO, POOL_W), F32), pltpu.VMEM((CONV_HALO, CONV_CH), F32),
                        pltpu.VMEM((N_GROUPS, N_STATE, GROUP_W), F32), pltpu.VMEM((CHUNK, D), F32)] + _gather_scratch(shards),
        compiler_params=_cparams(),
    )(pm, cw, cb, hp, gssd, wpool, pscale, *shards)


def _mixer_bwd(pm, cvs, dym, hstates, cw, cb, hp, gssd, wpool, pscale, nb, seq, after=()):
    nc = seq // CHUNK
    hpg = N_HEADS // N_GROUPS
    na = len(after)

    def body(*refs):
        (pm_ref, hpool_ref, cv_ref, dy_ref, hs_ref, cw_ref, cb_ref, hp_ref, gs_ref, wp_ref, ps_ref) = refs[:11]
        dpm_ref, dconv_ref, dhp_ref, dvec_ref, dwp_ref = refs[11 + na:16 + na]
        nxt_q, nxt_cv, rstate, y_scr, dx_scr = refs[16 + na:]
        b = pl.program_id(0)
        ci = pl.program_id(1)
        c = nc - 1 - ci

        @pl.when((b == 0) & (ci == 0))
        def _():
            for r in (dconv_ref, dhp_ref, dvec_ref, dwp_ref):
                r[...] = jnp.zeros_like(r)

        @pl.when(ci == 0)
        def _():
            nxt_q[...] = jnp.zeros_like(nxt_q)
            nxt_cv[...] = jnp.zeros_like(nxt_cv)
            rstate[...] = jnp.zeros_like(rstate)

        first = (c > 0).astype(F32)
        up = pm_ref[:, 0:POOL_W]
        z = pm_ref[:, OFF_Z:OFF_XBC]
        ux = pm_ref[:, OFF_XBC:OFF_DT]
        halo_p = hpool_ref[...] * first
        hprev = [hs_ref[0, 0, 0], hs_ref[0, 0, 1]]
        cw, cb, hp, gssd, wpool, pscale = cw_ref[...], cb_ref[...], hp_ref[...], gs_ref[...], wp_ref[...], ps_ref[...]
        o = _chunk_forward(up, z, ux, pm_ref[:, OFF_DT:], halo_p, None, hprev, cw, cb, hp, gssd, wpool, pscale,
                           c * CHUNK, y_scr, cv=cv_ref[...])
        L = CHUNK
        dy_pool = dy_ref[:, 0:POOL_W].astype(F32)
        dy_ssd = dy_ref[:, POOL_W:].astype(F32)

        dvec_ref[1:2, 0:POOL_W] += jnp.sum(dy_pool * jnp.concatenate(o["yp"], axis=1), axis=0, keepdims=True)
        dyp = dy_pool * pscale
        qs = []
        dps = []
        for gi in range(len(WINDOWS)):
            sl = slice(gi * LANES, (gi + 1) * LANES)
            dwp_ref[gi] += _mm_tn(o["p"][gi], dyp[:, sl])
            dpg = _mm_nt(dyp[:, sl], wpool[gi])
            dps.append(dpg)
            qs.append(dpg * o["inv"][gi])
        q = jnp.concatenate(qs, axis=1)
        e = jnp.concatenate([q, nxt_q[...]], axis=0)
        n = L + POOL_HALO
        s2 = e + pltpu.roll(e, n - 1, 0)
        s4 = s2 + pltpu.roll(s2, n - 2, 0)
        s8 = s4 + pltpu.roll(s4, n - 4, 0)
        s16 = s8 + pltpu.roll(s8, n - 8, 0)
        sums = (s2, s4, s8, s16)
        for gi in range(len(WINDOWS)):
            sl = slice(gi * LANES, (gi + 1) * LANES)
            dpm_ref[:, sl] = (sums[gi][:L, sl] - dps[gi]).astype(BF16)
        nxt_q[...] = q[:POOL_HALO]

        yn, y, silz, sz = o["yn"], o["y"], o["silz"], o["sz"]
        dvec_ref[0:1] += jnp.sum(dy_ssd * yn, axis=0, keepdims=True)
        dyn = dy_ssd * gssd
        dyz = []
        for g in range(N_GROUPS):
            gs = slice(g * GROUP_W, (g + 1) * GROUP_W)
            mean = jnp.mean(dyn[:, gs] * yn[:, gs], axis=-1, keepdims=True)
            dyz.append(o["rg"][g] * (dyn[:, gs] - yn[:, gs] * mean))
        dyz = jnp.concatenate(dyz, axis=1)
        dyv = dyz * silz
        dpm_ref[:, OFF_Z:OFF_XBC] = (dyz * y * (sz * (1.0 + z * (1.0 - sz)))).astype(BF16)

        X, Bm, Cm, xdt = o["X"], o["Bm"], o["Cm"], o["xdt"]
        exm = o["exm"]
        rdm = _reduce_mat()
        lane = lax.broadcasted_iota(jnp.int32, (1, LANES), 1)
        sub = lax.broadcasted_iota(jnp.int32, (LANES, 1), 0)
        dX = o["dskip_rep"] * dyv
        yoff_full = jnp.concatenate(o["yoff"], axis=1)
        rs = jnp.zeros((L, LANES), F32)
        cs_t = jnp.zeros((LANES, L), F32)
        dBs, dCs = [], []
        rh_sums = []
        ddec = []
        for g in range(N_GROUPS):
            gs = slice(g * GROUP_W, (g + 1) * GROUP_W)
            Bg = Bm[:, g * N_STATE:(g + 1) * N_STATE]
            Cg = Cm[:, g * N_STATE:(g + 1) * N_STATE]
            Gg = o["G"][g]
            R = rstate[g]
            dwm = dyv[:, gs] * o["eo_rep"][:, gs]
            dC = _mm_nt(dwm, hprev[g])
            dH = _mm_tn(Cg, dwm)
            dG = jnp.zeros((L, L), F32)
            for hh in range(hpg):
                h = g * hpg + hh
                hs = slice(h * HEAD_DIM, (h + 1) * HEAD_DIM)
                lm = o["lms"][h]
                m_h = Gg * lm
                dM = _mm_nt(dyv[:, hs], xdt[:, hs])
                dx_scr[:, hs] = _mm_tn(m_h, dyv[:, hs])
                qm = dM * m_h
                rs = rs + jnp.sum(qm, axis=1, keepdims=True) * (lane == h).astype(F32)
                cs_t = cs_t + (sub == h).astype(F32) * jnp.sum(qm, axis=0, keepdims=True)
                dG = dG + dM * lm
            dC = dC + _mm(dG, Bg)
            dB = _mm_tn(dG, Cg)
            zx = _mm(Bg, R)
            dxdt_state = zx * o["dec_rep"][:, gs]
            ddec.append(zx * xdt[:, gs])
            dB = dB + _mm_nt(o["xdec"][g], R)
            rh_sums.append(jnp.sum(R * hprev[g], axis=0, keepdims=True))
            rstate[g] = dH + o["cd_rep"][:, gs] * R
            dx_scr[:, gs] = dx_scr[:, gs] + dxdt_state
            dBs.append(dB)
            dCs.append(dC)
        dxdt = dx_scr[...]
        tail = jnp.concatenate([jnp.sum(dyv * X, axis=0, keepdims=True), jnp.concatenate(rh_sums, axis=1),
                                jnp.zeros((6, D), F32)], axis=0)
        red = _dot01(jnp.concatenate([dyv * yoff_full, jnp.concatenate(ddec, axis=1), dxdt * X, tail], axis=0), rdm, 2)
        d_dskip, dcd_row = red[3 * L:3 * L + 1], red[3 * L + 1:3 * L + 2]
        ddec_h = red[L:2 * L] * o["dec"]
        dcum_last = jnp.sum(ddec_h, axis=0, keepdims=True) + dcd_row * o["cd"]
        dcum = red[0:L] + rs - cs_t.T - ddec_h + (sub == L - 1).astype(F32) * dcum_last
        dda = _dot01(o["anti"], dcum, 3, split_lhs=False)
        ddt_v = dda * o["a_row"] + red[2 * L:3 * L]
        dX = dX + dxdt * o["dt_rep"]
        head_mask = (lane < N_HEADS).astype(F32)
        d_alog = jnp.sum(dda * o["dt"], axis=0, keepdims=True) * o["a_row"] * head_mask
        dpre = ddt_v * _sigmoid(o["pre"]) * head_mask
        dpm_ref[:, OFF_DT:] = dpre.astype(BF16)
        d_dtb = jnp.sum(dpre, axis=0, keepdims=True)
        dhp_ref[...] += jnp.concatenate([d_dtb, d_alog, d_dskip * head_mask, jnp.zeros((5, LANES), F32)], axis=0)

        dxbc = jnp.concatenate([dX] + dBs + dCs, axis=1)
        sg, cv = o["sg"], o["cv"]
        dcv = dxbc * (sg * (1.0 + cv * (1.0 - sg)))
        e2 = jnp.concatenate([dcv, nxt_cv[...]], axis=0)
        n2 = L + CONV_HALO
        ahead = [dcv, pltpu.roll(e2, n2 - 1, 0)[:L], pltpu.roll(e2, n2 - 2, 0)[:L], pltpu.roll(e2, n2 - 3, 0)[:L]]
        dconv_ref[0:5] += jnp.concatenate(
            [jnp.sum(ux * ahead[3 - k], axis=0, keepdims=True) for k in range(4)]
            + [jnp.sum(dcv, axis=0, keepdims=True)], axis=0)
        dux = ahead[0] * cw[3:4] + ahead[1] * cw[2:3] + ahead[2] * cw[1:2] + ahead[3] * cw[0:1]
        dpm_ref[:, OFF_XBC:OFF_DT] = dux.astype(BF16)
        nxt_cv[...] = dcv[:CONV_HALO]

    def full(shape):
        return pl.BlockSpec(shape, lambda b, c: (0,) * len(shape))

    def rowblk(b, c):
        return b * nc + (nc - 1 - c)

    hp_blocks = CHUNK // POOL_HALO
    T = nb * seq
    return pl.pallas_call(
        body, name="mixer_bwd", grid=(nb, nc),
        in_specs=[pl.BlockSpec((CHUNK, PROJ_W), lambda b, c: (rowblk(b, c), 0)),
                  pl.BlockSpec((POOL_HALO, POOL_W), lambda b, c: (jnp.maximum(rowblk(b, c) * hp_blocks - 1, 0), 0)),
                  pl.BlockSpec((CHUNK, CONV_CH), lambda b, c: (rowblk(b, c), 0)),
                  pl.BlockSpec((CHUNK, MIX_W), lambda b, c: (rowblk(b, c), 0)),
                  pl.BlockSpec((1, 1, N_GROUPS, N_STATE, GROUP_W), lambda b, c: (b, nc - 1 - c, 0, 0, 0)),
                  full((4, CONV_CH)), full((1, CONV_CH)), full((8, LANES)), full((1, D)),
                  full((4, LANES, LANES)), full((1, POOL_W))] + [ANY_SPEC] * na,
        out_specs=[pl.BlockSpec((CHUNK, PROJ_W), lambda b, c: (rowblk(b, c), 0)),
                   full((8, CONV_CH)), full((8, LANES)), full((8, D)), full((4, LANES, LANES))],
        out_shape=[jax.ShapeDtypeStruct((T, PROJ_W), BF16),
                   jax.ShapeDtypeStruct((8, CONV_CH), F32), jax.ShapeDtypeStruct((8, LANES), F32),
                   jax.ShapeDtypeStruct((8, D), F32), jax.ShapeDtypeStruct((4, LANES, LANES), F32)],
        scratch_shapes=[pltpu.VMEM((POOL_HALO, POOL_W), F32), pltpu.VMEM((CONV_HALO, CONV_CH), F32),
                        pltpu.VMEM((N_GROUPS, N_STATE, GROUP_W), F32), pltpu.VMEM((CHUNK, D), F32),
                        pltpu.VMEM((CHUNK, D), F32)],
        compiler_params=_cparams(),
    )(pm, pm, cvs, dym, hstates, cw, cb, hp, gssd, wpool, pscale, *after)


def _mlp_fused(x2, ymix, target, mod, g_mlp, g_final, w_out, w_up, w_down, seq):
    T = x2.shape[0]
    tm = min(256, seq)
    tps = seq // tm
    nblk = D_FF // FF_BLK

    def body(x_ref, ym_ref, tg_ref, mod_ref, gm_ref, gf_ref, wo_ref, wu_ref, wd_ref,
             da_ref, dym_ref, dh1_ref, u2_ref, f_ref, dup_ref, ddn_ref, dmod_ref, acc_ref, relu_scr):
        i = pl.program_id(0)

        @pl.when(i == 0)
        def _():
            acc_ref[...] = jnp.zeros_like(acc_ref)

        @pl.when(i % tps == 0)
        def _():
            dmod_ref[...] = jnp.zeros_like(dmod_ref)

        md = mod_ref[0]
        gate_m, shift_f, scale_f, gate_f = md[2:3], md[3:4], md[4:5], md[5:6]
        g_mlp, g_fin = gm_ref[...], gf_ref[...]
        a = jnp.dot(ym_ref[...], wo_ref[...], preferred_element_type=F32)
        h1 = x_ref[...] + gate_m * a
        r2 = lax.rsqrt(jnp.mean(h1 * h1, axis=-1, keepdims=True) + EPS)
        n2 = h1 * r2
        u2 = (n2 * g_mlp) * (1.0 + scale_f) + shift_f
        u2b = u2.astype(BF16)
        u2_ref[...] = u2b
        dn = jnp.zeros((tm, D), F32)
        for j in range(nblk):
            js = slice(j * FF_BLK, (j + 1) * FF_BLK)
            upj = jnp.maximum(jnp.dot(u2b, wu_ref[j], preferred_element_type=F32), 0.0)
            relu_scr[:, js] = upj
            fj = (upj * upj).astype(BF16)
            f_ref[:, js] = fj
            dn = dn + jnp.dot(fj, wd_ref[j], preferred_element_type=F32)
        h2 = h1 + gate_f * dn
        r3 = lax.rsqrt(jnp.mean(h2 * h2, axis=-1, keepdims=True) + EPS)
        n3 = h2 * r3
        err = n3 * g_fin - tg_ref[...]
        loss = 0.5 * jnp.sum(jnp.mean(err * err, axis=-1, keepdims=True), axis=0, keepdims=True)
        dout = err * (1.0 / D)
        d_gfin = jnp.sum(dout * n3, axis=0, keepdims=True)
        dn3 = dout * g_fin
        dh2 = r3 * (dn3 - n3 * jnp.mean(dn3 * n3, axis=-1, keepdims=True))
        d_gate_f = jnp.sum(dh2 * dn, axis=0, keepdims=True)
        ddn = (gate_f * dh2).astype(BF16)
        ddn_ref[...] = ddn
        du2 = jnp.zeros((tm, D), F32)
        for j in range(nblk):
            js = slice(j * FF_BLK, (j + 1) * FF_BLK)
            dfj = lax.dot_general(ddn, wd_ref[j], (((1,), (1,)), ((), ())), preferred_element_type=F32)
            dupj = (dfj * (2.0 * relu_scr[:, js])).astype(BF16)
            dup_ref[:, js] = dupj
            du2 = du2 + lax.dot_general(dupj, wu_ref[j], (((1,), (1,)), ((), ())), preferred_element_type=F32)
        d_scale_f = jnp.sum(du2 * (n2 * g_mlp), axis=0, keepdims=True)
        d_shift_f = jnp.sum(du2, axis=0, keepdims=True)
        d_gmlp = jnp.sum(du2 * (1.0 + scale_f) * n2, axis=0, keepdims=True)
        dn2 = du2 * (g_mlp * (1.0 + scale_f))
        dh1 = dh2 + r2 * (dn2 - n2 * jnp.mean(dn2 * n2, axis=-1, keepdims=True))
        dh1_ref[...] = dh1
        d_gate_m = jnp.sum(dh1 * a, axis=0, keepdims=True)
        da = (gate_m * dh1).astype(BF16)
        da_ref[...] = da
        dym_ref[...] = lax.dot_general(da, wo_ref[...], (((1,), (1,)), ((), ())),
                                       preferred_element_type=F32).astype(BF16)
        dmod_ref[0] += jnp.concatenate([jnp.zeros((2, D), F32), d_gate_m, d_shift_f, d_scale_f, d_gate_f,
                                        jnp.zeros((2, D), F32)], axis=0)
        acc_ref[...] += jnp.concatenate([d_gmlp, d_gfin, loss * jnp.ones((1, D), F32), jnp.zeros((5, D), F32)], axis=0)

    whole = pl.BlockSpec(memory_space=pltpu.VMEM)

    def tok(w):
        return pl.BlockSpec((tm, w), lambda i: (i, 0))

    def vec():
        return pl.BlockSpec((1, D), lambda i: (0, 0))

    nb = T // seq
    return pl.pallas_call(
        body, name="mlp_fused", grid=(T // tm,),
        in_specs=[tok(D), tok(MIX_W), tok(D), pl.BlockSpec((1, 8, D), lambda i: (i // tps, 0, 0)), vec(), vec(),
                  whole, whole, whole],
        out_specs=[tok(D), tok(MIX_W), tok(D), tok(D), tok(D_FF), tok(D_FF), tok(D),
                   pl.BlockSpec((1, 8, D), lambda i: (i // tps, 0, 0)), pl.BlockSpec((8, D), lambda i: (0, 0))],
        out_shape=[jax.ShapeDtypeStruct((T, D), BF16), jax.ShapeDtypeStruct((T, MIX_W), BF16),
                   jax.ShapeDtypeStruct((T, D), F32), jax.ShapeDtypeStruct((T, D), BF16),
                   jax.ShapeDtypeStruct((T, D_FF), BF16), jax.ShapeDtypeStruct((T, D_FF), BF16),
                   jax.ShapeDtypeStruct((T, D), BF16), jax.ShapeDtypeStruct((nb, 8, D), F32),
                   jax.ShapeDtypeStruct((8, D), F32)],
        scratch_shapes=[pltpu.VMEM((tm, D_FF), F32)],
        compiler_params=_cparams(),
    )(x2, ymix, target, mod, g_mlp, g_final, w_out, w_up, w_down)


def _in_bwd(x2, dh1, dpb, mod, g_mix, w_cat, dmod_a, acc_a, seq):
    T = x2.shape[0]
    tm = min(1024, seq)
    tps = seq // tm
    steps = T // tm

    def body(x_ref, dh_ref, dpb_ref, mod_ref, g_ref, w_ref, dma_ref, acca_ref, dx_ref, dmod_ref, acc_ref):
        i = pl.program_id(0)

        @pl.when(i == 0)
        def _():
            acc_ref[...] = acca_ref[...]

        @pl.when(i % tps == 0)
        def _():
            dmod_ref[...] = dma_ref[...]

        du = lax.dot_general(dpb_ref[...], w_ref[...], (((1,), (1,)), ((), ())), preferred_element_type=F32)
        x = x_ref[...]
        md = mod_ref[0]
        g = g_ref[...]
        r = lax.rsqrt(jnp.mean(x * x, axis=-1, keepdims=True) + EPS)
        n1 = x * r
        d_scale = jnp.sum(du * (n1 * g), axis=0, keepdims=True)
        d_shift = jnp.sum(du, axis=0, keepdims=True)
        d_g = jnp.sum(du * (1.0 + md[1:2]) * n1, axis=0, keepdims=True)
        dn1 = du * (g * (1.0 + md[1:2]))
        dx_ref[...] = dh_ref[...] + r * (dn1 - n1 * jnp.mean(dn1 * n1, axis=-1, keepdims=True))
        dmod_ref[0] += jnp.concatenate([d_shift, d_scale, jnp.zeros((6, D), F32)], axis=0)
        acc_ref[...] += jnp.concatenate([jnp.zeros((3, D), F32), d_g, jnp.zeros((4, D), F32)], axis=0)

    whole = pl.BlockSpec(memory_space=pltpu.VMEM)
    nb = T // seq
    return pl.pallas_call(
        body, name="in_bwd", grid=(steps,),
        in_specs=[pl.BlockSpec((tm, D), lambda i: (i, 0)), pl.BlockSpec((tm, D), lambda i: (i, 0)),
                  pl.BlockSpec((tm, PROJ_W), lambda i: (i, 0)),
                  pl.BlockSpec((1, 8, D), lambda i: (i // tps, 0, 0)), pl.BlockSpec((1, D), lambda i: (0, 0)),
                  whole, pl.BlockSpec((1, 8, D), lambda i: (i // tps, 0, 0)), pl.BlockSpec((8, D), lambda i: (0, 0))],
        out_specs=[pl.BlockSpec((tm, D), lambda i: (i, 0)),
                   pl.BlockSpec((1, 8, D), lambda i: (i // tps, 0, 0)), pl.BlockSpec((8, D), lambda i: (0, 0))],
        out_shape=[jax.ShapeDtypeStruct((T, D), F32),
                   jax.ShapeDtypeStruct((nb, 8, D), F32), jax.ShapeDtypeStruct((8, D), F32)],
        compiler_params=_cparams(),
    )(x2, dh1, dpb, mod, g_mix, w_cat, dmod_a, acc_a)


def _dw_in(u_b, dpb, in_cols, shards):
    T = u_b.shape[0]
    bk = min(1024, T)
    nk = T // bk
    ns = len(shards)
    starts = [(in_cols * j // LANES) * LANES for j in range(N_DEV)]
    assert all(s + DW_IN_WIN <= PROJ_W and in_cols * (j + 1) <= s + DW_IN_WIN for j, s in enumerate(starts))

    def body(*refs):
        u_ref, d_ref = refs[:2]
        sh_refs = refs[2:2 + ns]
        o_ref = refs[2 + ns]
        ga_refs = refs[3 + ns:3 + 2 * ns]
        acc, send, recv, loc = refs[3 + 2 * ns:]
        k = pl.program_id(0)
        gather = _Gather(sh_refs, ga_refs, send, recv, loc)
        gather.begin_hosted(k, nk)

        @pl.when(k == 0)
        def _():
            acc[...] = jnp.zeros_like(acc)

        ut = u_ref[...].T
        for j in range(N_DEV):
            acc[j] += jnp.dot(ut, d_ref[:, starts[j]:starts[j] + DW_IN_WIN], preferred_element_type=F32)

        @pl.when(k == nk - 1)
        def _():
            for j in range(N_DEV):
                off = in_cols * j - starts[j]
                o_ref[j] = acc[j][:, off:off + in_cols].astype(BF16)

        gather.end_hosted(k, nk)

    return pl.pallas_call(
        body, name="dw_in", grid=(nk,),
        in_specs=[pl.BlockSpec((bk, D), lambda k: (k, 0)), pl.BlockSpec((bk, PROJ_W), lambda k: (k, 0))]
        + [ANY_SPEC] * ns,
        out_specs=[pl.BlockSpec((N_DEV, D, in_cols), lambda k: (0, 0, 0))] + [ANY_SPEC] * ns,
        out_shape=[jax.ShapeDtypeStruct((N_DEV, D, in_cols), BF16)]
        + [jax.ShapeDtypeStruct((N_DEV,) + v.shape, v.dtype) for v in shards],
        scratch_shapes=[pltpu.VMEM((N_DEV, D, DW_IN_WIN), F32)] + _gather_scratch(shards),
        compiler_params=_cparams(),
    )(u_b, dpb, *shards)


def _dw_blocks(a, b, name, by_rows, per_step=1, after=()):
    T, M = a.shape
    N = b.shape[1]
    bk = min(4096, T)
    nk = T // bk
    whole = pl.BlockSpec(memory_space=pltpu.VMEM)
    if by_rows:
        rows = M // N_DEV
        am = rows * per_step
        nblk = N_DEV // per_step
        a_spec, b_spec = pl.BlockSpec((bk, am), lambda i, k: (k, i)), whole
        out_blk, acc_shape = (per_step, rows, N), (am, N)
    else:
        cols = N // N_DEV
        nblk = N_DEV
        a_spec, b_spec = whole, pl.BlockSpec((bk, cols), lambda i, k: (k, i))
        out_blk, acc_shape = (1, M, cols), (M, cols)

    def body(a_ref, b_ref, *rest):
        o_ref, acc = rest[len(after):]
        k = pl.program_id(1)

        tok = pl.ds(pl.multiple_of(k * bk, bk), bk)
        a_blk = a_ref[...] if by_rows else a_ref[tok, :]
        b_blk = b_ref[tok, :] if by_rows else b_ref[...]
        part = lax.dot_general(a_blk, b_blk, (((0,), (0,)), ((), ())), preferred_element_type=F32)
        if nk == 1:
            o_ref[...] = part.reshape(out_blk).astype(BF16)
            return

        @pl.when(k == 0)
        def _():
            acc[...] = jnp.zeros_like(acc)

        acc[...] += part

        @pl.when(k == nk - 1)
        def _():
            o_ref[...] = acc[...].reshape(out_blk).astype(BF16)

    return pl.pallas_call(
        body, name=name, grid=(nblk, nk), in_specs=[a_spec, b_spec] + [ANY_SPEC] * len(after),
        out_specs=pl.BlockSpec(out_blk, lambda i, k: (i, 0, 0)),
        out_shape=jax.ShapeDtypeStruct((N_DEV,) + out_blk[1:], BF16),
        scratch_shapes=[pltpu.VMEM(acc_shape, F32)],
        compiler_params=_cparams(),
    )(a, b, *after)


def _adam_parts(parts, w, m, v, name):
    rows, cols = w.shape
    br = rows
    for cand in range(rows, 15, -16):
        if rows % cand == 0 and cand * cols * 4 <= ADAM_BLOCK_BYTES:
            br = cand
            break

    def body(p_ref, w_ref, m_ref, v_ref, g_out, dl_out, m_out, v_out):
        g = p_ref[0].astype(F32)
        for k in range(1, N_DEV):
            g = g + p_ref[k].astype(F32)
        g_out[...] = g
        dl, mn, vn = _adam_math(w_ref[...], g, m_ref[...], v_ref[...])
        dl_out[...] = dl
        m_out[...] = mn
        v_out[...] = vn

    wspec = pl.BlockSpec((br, cols), lambda i: (i, 0))
    return pl.pallas_call(
        body, name=name, grid=(rows // br,),
        in_specs=[pl.BlockSpec((N_DEV, br, cols), lambda i: (0, i, 0)), wspec, wspec, wspec],
        out_specs=[wspec] * 4, out_shape=[jax.ShapeDtypeStruct((rows, cols), F32)] * 4,
        compiler_params=_cparams(),
    )(parts, w, m, v)


def _adam_plain(g, w, m, v, name):
    def body(g_ref, w_ref, m_ref, v_ref, dl_out, m_out, v_out):
        dl, mn, vn = _adam_math(w_ref[...], g_ref[...], m_ref[...], v_ref[...])
        dl_out[...] = dl
        m_out[...] = mn
        v_out[...] = vn

    return pl.pallas_call(body, name=name, out_shape=[jax.ShapeDtypeStruct(w.shape, F32)] * 3,
                          compiler_params=_cparams())(g, w, m, v)


SMALL_PARAMS = ("b_ada", "g_mix", "conv_b", "dt_bias", "a_log", "d_skip", "g_ssd", "pool_scale", "g_mlp", "g_final")


def _small_adam(gathered, params):
    n_par = len(SMALL_PARAMS)
    nb = gathered[0].shape[1]

    def body(*refs):
        dmod_ref, acc_ref, conv_ref, vec_ref, hd_ref = refs[:5]
        par_refs = refs[5:5 + 3 * n_par]
        out_refs = refs[5 + 3 * n_par:5 + 7 * n_par]
        cw_out, acc_out = refs[5 + 7 * n_par:]

        def total(ref):
            t = ref[0]
            for k in range(1, N_DEV):
                t = t + ref[k]
            return t

        dm = total(dmod_ref)
        dmb = dm[0]
        for b in range(1, nb):
            dmb = dmb + dm[b]
        ac, cv, vc, hd = total(acc_ref), total(conv_ref), total(vec_ref), total(hd_ref)
        cw_out[...] = cv[0:4]
        acc_out[...] = ac
        grads = {
            "b_ada": jnp.concatenate([dmb[r:r + 1] for r in range(6)], axis=1), "g_mix": ac[3:4], "conv_b": cv[4:5],
            "dt_bias": hd[0:1, 0:N_HEADS], "a_log": hd[1:2, 0:N_HEADS], "d_skip": hd[2:3, 0:N_HEADS],
            "g_ssd": vc[0:1], "pool_scale": vc[1:2, 0:POOL_W], "g_mlp": ac[0:1], "g_final": ac[1:2],
        }
        for i, name in enumerate(SMALL_PARAMS):
            w_ref, m_ref, v_ref = par_refs[3 * i:3 * i + 3]
            g = grads[name]
            dl, mn, vn = _adam_math(w_ref[...], g, m_ref[...], v_ref[...])
            g_o, d_o, m_o, v_o = out_refs[4 * i:4 * i + 4]
            g_o[...] = g
            d_o[...] = dl
            m_o[...] = mn
            v_o[...] = vn

    flat = [a for name in SMALL_PARAMS for a in params[name]]
    out_shape = [jax.ShapeDtypeStruct(params[name][0].shape, F32) for name in SMALL_PARAMS for _ in range(4)]
    out_shape += [jax.ShapeDtypeStruct((4, CONV_CH), F32), jax.ShapeDtypeStruct((8, D), F32)]
    return pl.pallas_call(body, name="small_adam", out_shape=out_shape, compiler_params=_cparams())(*gathered, *flat)


def kernel(x, c, w_ada, b_ada, g_mix, w_in, conv_w, conv_b, dt_bias, a_log, d_skip, g_ssd, w_pool, pool_scale, w_out, g_mlp, w_up, w_down, g_final, loss_target, m_w_ada, m_b_ada, m_g_mix, m_w_in, m_conv_w, m_conv_b, m_dt_bias, m_a_log, m_d_skip, m_g_ssd, m_w_pool, m_pool_scale, m_w_out, m_g_mlp, m_w_up, m_w_down, m_g_final, v_w_ada, v_b_ada, v_g_mix, v_w_in, v_conv_w, v_conv_b, v_dt_bias, v_a_log, v_d_skip, v_g_ssd, v_w_pool, v_pool_scale, v_w_out, v_g_mlp, v_w_up, v_w_down, v_g_final):
    nb, seq, _ = x.shape
    T = nb * seq
    me = 4 * lax.axis_index("x") + 2 * lax.axis_index("y") + lax.axis_index("c")
    in_cols = w_in.shape[2]
    ada_cols = w_ada.shape[2]
    cw_cols = conv_w.shape[2]

    win_t = jnp.pad(w_in[0].astype(BF16).T, ((0, -in_cols % 16), (0, 0)))
    c_g, cw_g, win_g = _all_gather([c, conv_w[0], win_t], "ag_first")
    c_all = c_g.reshape(N_DEV * nb, D)
    cw_full = cw_g.transpose(1, 0, 2).reshape(4, CONV_CH)

    b_slice = lax.dynamic_slice(b_ada, (0, me * ada_cols), (1, ada_cols))
    mod_cols = _ada_fwd(c_all, w_ada[0], b_slice)
    (mod_g,) = _all_gather([mod_cols], "ag_mod")
    mod_all = mod_g.transpose(1, 0, 2).reshape(N_DEV * nb, 6, D)
    mod_mine = lax.dynamic_slice(mod_all, (me * nb, 0, 0), (nb, 6, D))
    mod = jnp.pad(mod_mine, ((0, 0), (0, 2), (0, 0)))

    x2 = x.reshape(T, D)
    tg2 = loss_target.reshape(T, D)
    heads = jnp.pad(jnp.concatenate([dt_bias, a_log, d_skip], axis=0), ((0, 5), (0, LANES - N_HEADS)))
    wpool_b = w_pool[0]
    u_b, pm, w_cat, wup_g = _mix_in(x2, mod, g_mix, win_g, in_cols, seq, [w_up[0].astype(BF16)])
    ymix, hstates, cvs, wout_g, wdn_g = _mixer_fwd(
        pm, cw_full, conv_b, heads, g_ssd, wpool_b, pool_scale, nb, seq,
        [w_out[0].astype(BF16), w_down[0].astype(BF16)])
    da_b, dym, dh1, u2_b, f_b, dup_b, ddn_b, dmod_a, acc_a = _mlp_fused(
        x2, ymix, tg2, mod, g_mlp, g_final.reshape(1, D), wout_g.reshape(MIX_W, D), wup_g, wdn_g, seq)

    gout_p = _dw_blocks(ymix, da_b, "dw_out", True, per_step=4)
    gup_p = _dw_blocks(u2_b, dup_b, "dw_up", False)
    ex_a = _exchange_start([gout_p, gup_p], "ga_start")
    gdn_p = _dw_blocks(f_b, ddn_b, "dw_down", True, after=[ex_a[4]])
    ex_b = _exchange_start([gdn_p], "gb_start")
    dpb, d_conv, d_heads, d_vec, d_wpool = _mixer_bwd(
        pm, cvs, dym, hstates, cw_full, conv_b, heads, g_ssd, wpool_b, pool_scale, nb, seq, after=[ex_b[4]])
    gin_p, conv_g, vec_g, heads_g, wpool_parts = _dw_in(
        u_b, dpb, in_cols, [d_conv, d_vec, d_heads, d_wpool.reshape(4 * LANES, LANES)])
    ex_in = _exchange_start([gin_p], "gin_start")
    grad_x2, dmod, acc = _in_bwd(x2, dh1, dpb, mod, g_mix + ex_in[4][0:1, 0:1], w_cat, dmod_a, acc_a, seq)

    gout_r, gup_r = _exchange_wait(ex_a, dmod, "ga_wait", me)
    (gdn_r,) = _exchange_wait(ex_b, dmod, "gb_wait", me)
    g_out, d_out, nm_out, nv_out = _adam_parts(gout_r, w_out[0], m_w_out[0], v_w_out[0], "adam_w_out")
    g_up, d_up, nm_up, nv_up = _adam_parts(gup_r, w_up[0], m_w_up[0], v_w_up[0], "adam_w_up")
    g_dn, d_dn, nm_dn, nv_dn = _adam_parts(gdn_r, w_down[0], m_w_down[0], v_w_down[0], "adam_w_down")

    dmod_g, acc_g = _all_gather([dmod, acc], "ag_small_bwd", after=[nm_out, nm_up, nm_dn])
    pool2 = (4 * LANES, LANES)
    wpool_outs = _adam_parts(wpool_parts, w_pool.reshape(pool2), m_w_pool.reshape(pool2), v_w_pool.reshape(pool2),
                             "adam_w_pool")
    small_params = {
        "b_ada": (b_ada, m_b_ada, v_b_ada), "g_mix": (g_mix, m_g_mix, v_g_mix), "conv_b": (conv_b, m_conv_b, v_conv_b),
        "dt_bias": (dt_bias, m_dt_bias, v_dt_bias), "a_log": (a_log, m_a_log, v_a_log),
        "d_skip": (d_skip, m_d_skip, v_d_skip), "g_ssd": (g_ssd, m_g_ssd, v_g_ssd),
        "pool_scale": (pool_scale, m_pool_scale, v_pool_scale), "g_mlp": (g_mlp, m_g_mlp, v_g_mlp),
        "g_final": tuple(a.reshape(1, D) for a in (g_final, m_g_final, v_g_final)),
    }
    small_res = _small_adam([dmod_g, acc_g, conv_g, vec_g, heads_g], small_params)
    g_cw_full, acc_sum = small_res[-2:]
    loss = acc_sum[2, 0]

    g_cw = lax.dynamic_slice(g_cw_full, (0, me * cw_cols), (4, cw_cols))
    d_cwp, nm_cwp, nv_cwp = _adam_plain(g_cw, conv_w[0], m_conv_w[0], v_conv_w[0], "adam_conv_w")

    dmod_all = dmod_g[:, :, 0:6].reshape(N_DEV * nb, 6 * D)
    dmod_slice = lax.dynamic_slice(dmod_all, (0, me * ada_cols), (N_DEV * nb, ada_cols))
    g_ada, d_ada, nm_ada, nv_ada = _ada_bwd_adam(c_all, dmod_slice, w_ada[0], m_w_ada[0], v_w_ada[0])

    ex_after = nm_ada[0:8, 0:LANES] + acc_sum[:, 0:LANES]
    (gin_r,) = _exchange_wait(ex_in, ex_after, "gin_wait", me)
    g_in, d_in, nm_in, nv_in = _adam_parts(gin_r, w_in[0], m_w_in[0], v_w_in[0], "adam_w_in")

    def small_outs(kind, wpool):
        res = {name: small_res[4 * i + kind] for i, name in enumerate(SMALL_PARAMS)}
        res["g_final"] = res["g_final"].reshape(D)
        res["w_pool"] = wpool.reshape(1, 4, LANES, LANES)
        return res

    def big_outs(ada, win, cwp, wout, wup, wdn):
        return {"w_ada": ada[None], "w_in": win.reshape(1, D, in_cols), "conv_w": cwp[None], "w_out": wout[None],
                "w_up": wup[None], "w_down": wdn[None]}

    order = ["w_ada", "b_ada", "g_mix", "w_in", "conv_w", "conv_b", "dt_bias", "a_log", "d_skip", "g_ssd", "w_pool",
             "pool_scale", "w_out", "g_mlp", "w_up", "w_down", "g_final"]
    groups = [
        {**small_outs(0, wpool_outs[0]), **big_outs(g_ada, g_in, g_cw, g_out, g_up, g_dn)},
        {**small_outs(1, wpool_outs[1]), **big_outs(d_ada, d_in, d_cwp, d_out, d_up, d_dn)},
        {**small_outs(2, wpool_outs[2]), **big_outs(nm_ada, nm_in, nm_cwp, nm_out, nm_up, nm_dn)},
        {**small_outs(3, wpool_outs[3]), **big_outs(nv_ada, nv_in, nv_cwp, nv_out, nv_up, nv_dn)},
    ]
    outs = [loss, grad_x2.reshape(nb, seq, D)]
    for grp in groups:
        outs += [grp[n] for n in order]
    return tuple(outs)
```

```python
import functools

import jax
import jax.numpy as jnp
from jax import lax
from jax.experimental import pallas as pl
from jax.experimental.pallas import tpu as pltpu

F32, BF16 = jnp.float32, jnp.bfloat16
MESH = pl.DeviceIdType.MESH
N_DEV = 8
D = 1024
LANES = 128
CHUNK = 128
POOL_W = 512
WINDOWS = (2, 4, 8, 16)
N_HEADS = 16
HEAD_DIM = 64
N_GROUPS = 2
GROUP_W = 512
N_STATE = 128
CONV_CH = 1536
OFF_Z, OFF_XBC, OFF_DT, IN_W = 512, 1536, 3072, 3088
PROJ_W = OFF_DT + LANES
MIX_W = 1536
D_FF = 4096
FF_BLK = 512
EPS = 1e-5
LR, B1, B2, AEPS, WD, STEP = 0.001, 0.9, 0.999, 1e-08, 0.01, 10
POOL_HALO = 16
CONV_HALO = 8
VMEM_LIMIT = 56 << 20
ADAM_BLOCK_BYTES = 1 << 20


def _cparams(**kw):
    return pltpu.CompilerParams(vmem_limit_bytes=VMEM_LIMIT, **kw)


def _mm(a, b):
    return jnp.dot(a.astype(BF16), b.astype(BF16), preferred_element_type=F32)


def _mm_nt(a, b):
    return lax.dot_general(a.astype(BF16), b.astype(BF16), (((1,), (1,)), ((), ())), preferred_element_type=F32)


def _mm_tn(a, b):
    return lax.dot_general(a.astype(BF16), b.astype(BF16), (((0,), (0,)), ((), ())), preferred_element_type=F32)


def _split_bf16(v, terms):
    parts, rest = [], v
    for t in range(terms):
        p = rest.astype(BF16)
        parts.append(p)
        if t + 1 < terms:
            rest = rest - p.astype(F32)
    return parts


def _dot01(a, b, terms, split_lhs=True):
    if split_lhs:
        bb = b.astype(BF16)
        prods = [jnp.dot(p, bb, preferred_element_type=F32) for p in _split_bf16(a, terms)]
    else:
        ab = a.astype(BF16)
        prods = [jnp.dot(ab, p, preferred_element_type=F32) for p in _split_bf16(b, terms)]
    out = prods[0]
    for q in prods[1:]:
        out = out + q
    return out


def _sigmoid(v):
    return 1.0 / (1.0 + jnp.exp(-v))


def _expand_mat():
    r = lax.broadcasted_iota(jnp.int32, (LANES, D), 0)
    c = lax.broadcasted_iota(jnp.int32, (LANES, D), 1)
    return (r == c // HEAD_DIM).astype(F32)


def _reduce_mat():
    r = lax.broadcasted_iota(jnp.int32, (D, LANES), 0)
    c = lax.broadcasted_iota(jnp.int32, (D, LANES), 1)
    return (c == r // HEAD_DIM).astype(F32)


def _pos():
    return lax.axis_index("x"), lax.axis_index("y"), lax.axis_index("c")


GATHER_PIECES = 4
GATHER_PIECE_BYTES = 96 << 10


def _pieces(shape, dtype):
    rows = shape[0]
    size = jnp.dtype(dtype).itemsize
    for d in shape:
        size *= d
    whole_tiles = rows % (GATHER_PIECES * 16) == 0
    return GATHER_PIECES if whole_tiles and size // GATHER_PIECES >= GATHER_PIECE_BYTES else 1


class _Gather:
    def __init__(self, x_refs, o_refs, send, recv, loc):
        self.x_refs, self.o_refs, self.send, self.recv, self.loc = x_refs, o_refs, send, recv, loc
        self.n = len(x_refs)
        self.pieces = [_pieces(r.shape, r.dtype) for r in x_refs]
        self.base = [7 * sum(self.pieces[:a]) for a in range(self.n)]
        x, y, c = _pos()
        self.c = c
        self.me, self.sib = (x, y, c), (x, y, 1 - c)
        self.chips = [(1 - x, y), (x, 1 - y), (1 - x, 1 - y)]

    def _rows(self, a, p):
        rows = self.x_refs[a].shape[0] // self.pieces[a]
        return pl.ds(p * rows, rows)

    def _cp(self, a, p, k, block, to, own=False):
        dst = self.o_refs[a].at[4 * block[0] + 2 * block[1] + block[2], self._rows(a, p)]
        sem = self.base[a] + 7 * p + k
        return pltpu.make_async_remote_copy(
            src_ref=self.x_refs[a].at[self._rows(a, p)] if own else dst, dst_ref=dst,
            send_sem=self.send.at[sem], recv_sem=self.recv.at[sem], device_id=to, device_id_type=MESH)

    def _mine(self, a):
        me = self.me
        return pltpu.make_async_copy(self.x_refs[a], self.o_refs[a].at[4 * me[0] + 2 * me[1] + me[2]], self.loc.at[a])

    def _first(self, a, p):
        cps = [self._cp(a, p, 0, self.me, self.sib, own=True)]
        return cps + [self._cp(a, p, 1 + j, self.me, (*chip, self.c), own=True) for j, chip in enumerate(self.chips)]

    def _passed(self, a, p, j):
        return self._cp(a, p, 4 + j, (*self.chips[j], self.c), self.sib)

    def start(self):
        for a in range(self.n):
            self._mine(a).start()
        for p in range(max(self.pieces)):
            for a in range(self.n):
                if p < self.pieces[a]:
                    for cp in self._first(a, p):
                        cp.start()

    def forward(self, p):
        for j, chip in enumerate(self.chips):
            for a in range(self.n):
                if p < self.pieces[a]:
                    self._cp(a, p, 1 + j, (*chip, self.c), self.me).wait_recv()
                    self._passed(a, p, j).start()

    def finish(self):
        for a in range(self.n):
            for p in range(self.pieces[a]):
                self._cp(a, p, 0, self.sib, self.me).wait_recv()
                for j, chip in enumerate(self.chips):
                    self._cp(a, p, 4 + j, (*chip, 1 - self.c), self.me).wait_recv()
        for a in range(self.n):
            for p in range(self.pieces[a]):
                for cp in self._first(a, p):
                    cp.wait_send()
                for j in range(3):
                    self._passed(a, p, j).wait_send()
            self._mine(a).wait()

    def begin_hosted(self, step, steps):
        @pl.when(step == 0)
        def _():
            self.start()

        n_p = max(self.pieces)
        for p in range(n_p):
            @pl.when(step == min(((p + 1) * 7 * steps) // (8 * n_p), steps - 1))
            def _():
                self.forward(p)

    def end_hosted(self, step, steps):
        @pl.when(step == steps - 1)
        def _():
            self.finish()


class _Exchange:
    def __init__(self, x_refs, o_refs, send, recv, loc):
        self.x_refs, self.o_refs, self.send, self.recv, self.loc = x_refs, o_refs, send, recv, loc
        self.n = len(x_refs)
        x, y, c = _pos()
        self.me_i = 4 * x + 2 * y + c
        self.peers = []
        for k in range(1, N_DEV):
            px = 1 - x if (k >> 2) & 1 else x
            py = 1 - y if (k >> 1) & 1 else y
            pc = 1 - c if k & 1 else c
            self.peers.append(((px, py, pc), 4 * px + 2 * py + pc))

    def _mine(self, a):
        return pltpu.make_async_copy(self.x_refs[a].at[self.me_i], self.o_refs[a].at[self.me_i], self.loc.at[a])

    def _cp(self, a, k, landing):
        peer, peer_i = self.peers[k]
        return pltpu.make_async_remote_copy(
            src_ref=self.x_refs[a].at[peer_i], dst_ref=self.o_refs[a].at[landing],
            send_sem=self.send.at[a * 7 + k], recv_sem=self.recv.at[a * 7 + k],
            device_id=peer, device_id_type=MESH)

    def start(self):
        for a in range(self.n):
            self._mine(a).start()
            for k in range(N_DEV - 1):
                self._cp(a, k, self.me_i).start()

    def finish(self):
        for a in range(self.n):
            for k in range(N_DEV - 1):
                self._cp(a, k, self.peers[k][1]).wait_recv()
        for a in range(self.n):
            for k in range(N_DEV - 1):
                self._cp(a, k, self.me_i).wait_send()
            self._mine(a).wait()


def _gather_scratch(xs):
    n_sem = 7 * sum(_pieces(v.shape, v.dtype) for v in xs)
    return [pltpu.SemaphoreType.DMA((n_sem,)), pltpu.SemaphoreType.DMA((n_sem,)), pltpu.SemaphoreType.DMA((len(xs),))]


ANY_SPEC = pl.BlockSpec(memory_space=pl.ANY)


def _all_gather(xs, name, after=()):
    n, na = len(xs), len(after)

    def body(*refs):
        g = _Gather(refs[:n], refs[n + na:2 * n + na], *refs[2 * n + na:])
        g.start()
        for p in range(max(g.pieces)):
            g.forward(p)
        g.finish()

    return pl.pallas_call(
        body, name=name,
        out_shape=[jax.ShapeDtypeStruct((N_DEV,) + v.shape, v.dtype) for v in xs],
        in_specs=[ANY_SPEC] * (n + na), out_specs=[ANY_SPEC] * n, scratch_shapes=_gather_scratch(xs),
    )(*xs, *after)


HBM_SPEC = pl.BlockSpec(memory_space=pltpu.HBM)
SEM_SPEC = pl.BlockSpec(memory_space=pltpu.SEMAPHORE)
VMEM_SPEC = pl.BlockSpec(memory_space=pltpu.VMEM)
SPLIT_EFFECT = pltpu.SideEffectType.DATAFLOW_SIDE_EFFECTING


def _in_hbm(v):
    return pltpu.with_memory_space_constraint(v, pltpu.HBM)


def _exchange_start(blocks, name):
    n = len(blocks)

    def body(*refs):
        x_refs, land_refs = refs[:n], refs[n:2 * n]
        send, recv = refs[2 * n:2 * n + 2]
        token = refs[-1]
        ex = _Exchange(x_refs, land_refs, send, recv, None)
        for a in range(n):
            for k in range(N_DEV - 1):
                ex._cp(a, k, ex.me_i).start()
        token[...] = jnp.zeros_like(token)

    lands = [lax.empty(v.shape, v.dtype) for v in blocks]
    hbm = tuple(pltpu.HBM(v.shape, v.dtype) for v in list(blocks) + lands)
    n_sem = (N_DEV - 1) * n
    out = pl.pallas_call(
        body, name=name,
        out_shape=(pltpu.SemaphoreType.DMA((n_sem,)), pltpu.SemaphoreType.DMA((n_sem,))) + hbm
        + (jax.ShapeDtypeStruct((8, LANES), F32),),
        in_specs=(HBM_SPEC,) * (2 * n), out_specs=(SEM_SPEC, SEM_SPEC) + (HBM_SPEC,) * (2 * n) + (VMEM_SPEC,),
        input_output_aliases={i: i + 2 for i in range(2 * n)},
        compiler_params=pltpu.CompilerParams(has_side_effects=SPLIT_EFFECT),
    )(*[_in_hbm(v) for v in list(blocks) + lands])
    return out[0], out[1], list(out[2:2 + n]), list(out[2 + n:2 + 2 * n]), out[-1]


def _exchange_wait(ex, after, name, me):
    send, recv, thru, lands, _ = ex
    n = len(thru)

    def body(*refs):
        x_refs, land_refs = refs[:n], refs[n:2 * n]
        send_ref, recv_ref = refs[2 * n:2 * n + 2]
        e = _Exchange(x_refs, land_refs, send_ref, recv_ref, None)
        for a in range(n):
            for k in range(N_DEV - 1):
                e._cp(a, k, e.me_i).wait_send()
                e._cp(a, k, e.peers[k][1]).wait_recv()

    hbm = tuple(pltpu.HBM(v.shape, v.dtype) for v in list(thru) + list(lands))
    out = pl.pallas_call(
        body, name=name, out_shape=hbm,
        in_specs=(HBM_SPEC,) * (2 * n) + (SEM_SPEC, SEM_SPEC, ANY_SPEC), out_specs=(HBM_SPEC,) * (2 * n),
        input_output_aliases={i: i for i in range(2 * n)},
        compiler_params=pltpu.CompilerParams(has_side_effects=SPLIT_EFFECT),
    )(*thru, *lands, send, recv, after)
    done = []
    for own, land in zip(out[:n], out[n:]):
        mine = lax.dynamic_slice(own, (me,) + (0,) * (own.ndim - 1), (1,) + own.shape[1:])
        done.append(lax.dynamic_update_slice(land, mine, (me,) + (0,) * (own.ndim - 1)))
    return done


def _ada_fwd(c_all, w_ada, b_slice):
    def body(c_ref, w_ref, b_ref, o_ref):
        cv = c_ref[...]
        act = cv * _sigmoid(cv)
        o_ref[...] = _mm(act, w_ref[...]) + b_ref[...]

    nb, nc = c_all.shape[0], w_ada.shape[1]
    return pl.pallas_call(body, name="ada_fwd", out_shape=jax.ShapeDtypeStruct((nb, nc), F32),
                          compiler_params=_cparams())(c_all, w_ada, b_slice)


def _adam_math(w, g, m, v):
    m = B1 * m + (1.0 - B1) * g
    v = B2 * v + (1.0 - B2) * jnp.square(g)
    m_hat = m / (1.0 - B1 ** STEP)
    v_hat = v / (1.0 - B2 ** STEP)
    delta = -LR * (m_hat / (jnp.sqrt(v_hat) + AEPS) + WD * w)
    return delta, m, v


def _ada_bwd_adam(c_all, dmod_slice, w, m, v):
    rows, cols = w.shape
    br = 256

    def body(c_ref, d_ref, w_ref, m_ref, v_ref, g_out, dl_out, m_out, v_out):
        cv = c_ref[...]
        act = cv * _sigmoid(cv)
        g = _mm_tn(act, d_ref[...])
        g_out[...] = g
        dl, mn, vn = _adam_math(w_ref[...], g, m_ref[...], v_ref[...])
        dl_out[...] = dl
        m_out[...] = mn
        v_out[...] = vn

    nb = c_all.shape[0]
    wspec = pl.BlockSpec((br, cols), lambda i: (i, 0))
    return pl.pallas_call(
        body, name="ada_bwd_adam", grid=(rows // br,),
        in_specs=[pl.BlockSpec((nb, br), lambda i: (0, i)), pl.BlockSpec((nb, cols), lambda i: (0, 0)),
                  wspec, wspec, wspec],
        out_specs=[wspec] * 4, out_shape=[jax.ShapeDtypeStruct((rows, cols), F32)] * 4,
        compiler_params=_cparams(),
    )(c_all, dmod_slice, w, m, v)


def _mix_in(x2, mod, g_mix, win_g, in_cols, seq, shards):
    T = x2.shape[0]
    tm = min(512, seq)
    tps = seq // tm
    blk_rows = win_g.shape[1]
    pad_rows = -blk_rows % (4 * LANES)
    ns = len(shards)
    steps = T // tm

    def body(*refs):
        x_ref, mod_ref, g_ref, wb_ref = refs[:4]
        sh_refs = refs[4:4 + ns]
        u_ref, pm_ref, wc_ref = refs[4 + ns:7 + ns]
        ga_refs = refs[7 + ns:7 + 2 * ns]
        w_ref, send, recv, loc = refs[7 + 2 * ns:]
        step = pl.program_id(0)
        gather = _Gather(sh_refs, ga_refs, send, recv, loc)
        gather.begin_hosted(step, steps)

        @pl.when(step == 0)
        def _():
            w_ref[:, OFF_DT:] = jnp.zeros((D, PROJ_W - OFF_DT), BF16)
            for j in range(N_DEV):
                blk = jnp.concatenate([wb_ref[j], jnp.zeros((pad_rows, D), BF16)], axis=0)
                w_ref[:, in_cols * j:in_cols * (j + 1)] = blk.T[:, :in_cols]
            wc_ref[...] = w_ref[...]

        x = x_ref[...]
        r = lax.rsqrt(jnp.mean(x * x, axis=-1, keepdims=True) + EPS)
        md = mod_ref[0]
        u = (x * r * g_ref[...]) * (1.0 + md[1:2]) + md[0:1]
        ub = u.astype(BF16)
        u_ref[...] = ub
        pm_ref[...] = jnp.dot(ub, w_ref[...], preferred_element_type=F32)
        gather.end_hosted(step, steps)

    whole = pl.BlockSpec(memory_space=pltpu.VMEM)
    return pl.pallas_call(
        body, name="mix_in", grid=(T // tm,),
        in_specs=[pl.BlockSpec((tm, D), lambda i: (i, 0)), pl.BlockSpec((1, 8, D), lambda i: (i // tps, 0, 0)),
                  pl.BlockSpec((1, D), lambda i: (0, 0)), whole] + [ANY_SPEC] * ns,
        out_specs=[pl.BlockSpec((tm, D), lambda i: (i, 0)), pl.BlockSpec((tm, PROJ_W), lambda i: (i, 0)),
                   pl.BlockSpec((D, PROJ_W), lambda i: (0, 0))] + [ANY_SPEC] * ns,
        out_shape=[jax.ShapeDtypeStruct((T, D), BF16), jax.ShapeDtypeStruct((T, PROJ_W), F32),
                   jax.ShapeDtypeStruct((D, PROJ_W), BF16)]
        + [jax.ShapeDtypeStruct((N_DEV,) + v.shape, v.dtype) for v in shards],
        scratch_shapes=[pltpu.VMEM((D, PROJ_W), BF16)] + _gather_scratch(shards),
        compiler_params=_cparams(),
    )(x2, mod, g_mix, win_g, *shards)


def _chunk_forward(up, z, ux, dtin, halo_p, halo_x, hprev, cw, cb, hp, gssd, wpool, pscale, t0, y_scr, cv=None):
    L = CHUNK
    out = {}
    row = lax.broadcasted_iota(jnp.int32, (L, 1), 0)
    t = (t0 + row + 1).astype(F32)
    e = jnp.concatenate([halo_p, up], axis=0)
    s2 = e + pltpu.roll(e, 1, 0)
    s4 = s2 + pltpu.roll(s2, 2, 0)
    s8 = s4 + pltpu.roll(s4, 4, 0)
    s16 = s8 + pltpu.roll(s8, 8, 0)
    sums = (s2, s4, s8, s16)
    p, inv, yp = [], [], []
    for gi, w in enumerate(WINDOWS):
        sl = slice(gi * LANES, (gi + 1) * LANES)
        ic = 1.0 / jnp.minimum(t, float(w))
        pg = sums[gi][POOL_HALO:, sl] * ic - up[:, sl]
        p.append(pg)
        inv.append(ic)
        yp.append(_mm(pg, wpool[gi]))
    out["p"], out["inv"], out["yp"] = p, inv, yp
    out["y_pool"] = jnp.concatenate(yp, axis=1) * pscale
    if cv is None:
        ex = jnp.concatenate([halo_x, ux], axis=0)
        taps = [pltpu.roll(ex, 3, 0)[CONV_HALO:], pltpu.roll(ex, 2, 0)[CONV_HALO:], pltpu.roll(ex, 1, 0)[CONV_HALO:], ux]
        cv = cb + taps[0] * cw[0:1] + taps[1] * cw[1:2] + taps[2] * cw[2:3] + taps[3] * cw[3:4]
    sg = _sigmoid(cv)
    xbc = cv * sg
    out["cv"], out["sg"] = cv, sg
    X = xbc[:, :D]
    Bm = xbc[:, D:D + N_GROUPS * N_STATE]
    Cm = xbc[:, D + N_GROUPS * N_STATE:]
    pre = dtin + hp[0:1]
    dt = jnp.maximum(pre, 0.0) + jnp.log(1.0 + jnp.exp(-jnp.abs(pre)))
    a_row = -jnp.exp(hp[1:2])
    da = dt * a_row
    ri = lax.broadcasted_iota(jnp.int32, (L, L), 0)
    ci = lax.broadcasted_iota(jnp.int32, (L, L), 1)
    causal = ri >= ci
    cum = _dot01(causal.astype(F32), da, 3, split_lhs=False)
    cum_t = cum.T
    cum_last = cum[L - 1:L]
    eo = jnp.exp(cum)
    dec = jnp.exp(cum_last - cum)
    cd = jnp.exp(cum_last)
    exm = _expand_mat()
    rows8 = jnp.concatenate([cd, hp[2:3], jnp.zeros((6, LANES), F32)], axis=0)
    rep = _dot01(jnp.concatenate([dt, eo, dec, rows8], axis=0), exm, 2)
    dt_rep, eo_rep, dec_rep = rep[0:L], rep[L:2 * L], rep[2 * L:3 * L]
    cd_rep, dskip_rep = rep[3 * L:3 * L + 1], rep[3 * L + 1:3 * L + 2]
    xdt = X * dt_rep
    out.update(X=X, Bm=Bm, Cm=Cm, pre=pre, dt=dt, a_row=a_row, cum=cum, cum_t=cum_t, eo=eo, dec=dec, cd=cd,
               dt_rep=dt_rep, eo_rep=eo_rep, dec_rep=dec_rep, cd_rep=cd_rep, dskip_rep=dskip_rep, xdt=xdt,
               causal=causal, anti=(ri <= ci).astype(F32), exm=exm)
    G, lms, yoff, hnew, xdec = [], [], [], [], []
    for g in range(N_GROUPS):
        gs = slice(g * GROUP_W, (g + 1) * GROUP_W)
        Bg = Bm[:, g * N_STATE:(g + 1) * N_STATE]
        Cg = Cm[:, g * N_STATE:(g + 1) * N_STATE]
        Gg = _mm_nt(Cg, Bg)
        G.append(Gg)
        for hh in range(N_HEADS // N_GROUPS):
            h = g * (N_HEADS // N_GROUPS) + hh
            seg = cum[:, h:h + 1] - cum_t[h:h + 1, :]
            lm = jnp.where(causal, jnp.exp(jnp.minimum(seg, 0.0)), 0.0)
            lms.append(lm)
            hs = slice(h * HEAD_DIM, (h + 1) * HEAD_DIM)
            y_scr[:, hs] = _mm(Gg * lm, xdt[:, hs])
        xd = xdt[:, gs] * dec_rep[:, gs]
        xdec.append(xd)
        sgm = _mm_tn(Bg, xd)
        yoff.append(_mm(Cg, hprev[g]) * eo_rep[:, gs])
        hnew.append(hprev[g] * cd_rep[:, gs] + sgm)
    out.update(G=G, lms=lms, yoff=yoff, hnew=hnew, xdec=xdec)
    y = y_scr[...] + jnp.concatenate(yoff, axis=1) + dskip_rep * X
    sz = _sigmoid(z)
    silz = z * sz
    yz = y * silz
    rg, yn = [], []
    for g in range(N_GROUPS):
        gs = slice(g * GROUP_W, (g + 1) * GROUP_W)
        r = lax.rsqrt(jnp.mean(yz[:, gs] * yz[:, gs], axis=-1, keepdims=True) + EPS)
        rg.append(r)
        yn.append(yz[:, gs] * r)
    yn = jnp.concatenate(yn, axis=1)
    out.update(y=y, sz=sz, silz=silz, rg=rg, yn=yn)
    out["y_ssd"] = yn * gssd
    return out


def _mixer_fwd(pm, cw, cb, hp, gssd, wpool, pscale, nb, seq, shards):
    nc = seq // CHUNK
    ns = len(shards)
    steps = nb * nc

    def body(*refs):
        pm_ref, cw_ref, cb_ref, hp_ref, gs_ref, wp_ref, ps_ref = refs[:7]
        sh_refs = refs[7:7 + ns]
        ym_ref, hs_ref, cv_ref = refs[7 + ns:10 + ns]
        ga_refs = refs[10 + ns:10 + 2 * ns]
        halo_p, halo_x, state, y_scr, send, recv, loc = refs[10 + 2 * ns:]
        c = pl.program_id(1)
        step = pl.program_id(0) * nc + c
        gather = _Gather(sh_refs, ga_refs, send, recv, loc)
        gather.begin_hosted(step, steps)

        @pl.when(c == 0)
        def _():
            halo_p[...] = jnp.zeros_like(halo_p)
            halo_x[...] = jnp.zeros_like(halo_x)
            state[...] = jnp.zeros_like(state)

        up = pm_ref[:, 0:POOL_W]
        z = pm_ref[:, OFF_Z:OFF_XBC]
        ux = pm_ref[:, OFF_XBC:OFF_DT]
        hprev = [state[0], state[1]]
        hs_ref[0, 0, 0] = hprev[0]
        hs_ref[0, 0, 1] = hprev[1]
        o = _chunk_forward(up, z, ux, pm_ref[:, OFF_DT:], halo_p[...], halo_x[...], hprev, cw_ref[...], cb_ref[...],
                           hp_ref[...], gs_ref[...], wp_ref[...], ps_ref[...], c * CHUNK, y_scr)
        ym_ref[:, 0:POOL_W] = o["y_pool"].astype(BF16)
        ym_ref[:, POOL_W:] = o["y_ssd"].astype(BF16)
        cv_ref[...] = o["cv"]
        state[0] = o["hnew"][0]
        state[1] = o["hnew"][1]
        halo_p[...] = up[CHUNK - POOL_HALO:]
        halo_x[...] = ux[CHUNK - CONV_HALO:]
        gather.end_hosted(step, steps)

    def full(shape):
        return pl.BlockSpec(shape, lambda b, c: (0,) * len(shape))

    T = nb * seq
    return pl.pallas_call(
        body, name="mixer_fwd", grid=(nb, nc),
        in_specs=[pl.BlockSpec((CHUNK, PROJ_W), lambda b, c: (b * nc + c, 0)),
                  full((4, CONV_CH)), full((1, CONV_CH)), full((8, LANES)), full((1, D)),
                  full((4, LANES, LANES)), full((1, POOL_W))] + [ANY_SPEC] * ns,
        out_specs=[pl.BlockSpec((CHUNK, MIX_W), lambda b, c: (b * nc + c, 0)),
                   pl.BlockSpec((1, 1, N_GROUPS, N_STATE, GROUP_W), lambda b, c: (b, c, 0, 0, 0)),
                   pl.BlockSpec((CHUNK, CONV_CH), lambda b, c: (b * nc + c, 0))] + [ANY_SPEC] * ns,
        out_shape=[jax.ShapeDtypeStruct((T, MIX_W), BF16),
                   jax.ShapeDtypeStruct((nb, nc, N_GROUPS, N_STATE, GROUP_W), F32),
                   jax.ShapeDtypeStruct((T, CONV_CH), F32)]
        + [jax.ShapeDtypeStruct((N_DEV,) + v.shape, v.dtype) for v in shards],
        scratch_shapes=[pltpu.VMEM((POOL_HALO, POOL_W), F32), pltpu.VMEM((CONV_HALO, CONV_CH), F32),
                        pltpu.VMEM((N_GROUPS, N_STATE, GROUP_W), F32), pltpu.VMEM((CHUNK, D), F32)] + _gather_scratch(shards),
        compiler_params=_cparams(),
    )(pm, cw, cb, hp, gssd, wpool, pscale, *shards)


def _mixer_bwd(pm, cvs, dym, hstates, cw, cb, hp, gssd, wpool, pscale, nb, seq, after=()):
    nc = seq // CHUNK
    hpg = N_HEADS // N_GROUPS
    na = len(after)

    def body(*refs):
        (pm_ref, hpool_ref, cv_ref, dy_ref, hs_ref, cw_ref, cb_ref, hp_ref, gs_ref, wp_ref, ps_ref) = refs[:11]
        dpm_ref, dconv_ref, dhp_ref, dvec_ref, dwp_ref = refs[11 + na:16 + na]
        nxt_q, nxt_cv, rstate, y_scr, dx_scr = refs[16 + na:]
        b = pl.program_id(0)
        ci = pl.program_id(1)
        c = nc - 1 - ci

        @pl.when((b == 0) & (ci == 0))
        def _():
            for r in (dconv_ref, dhp_ref, dvec_ref, dwp_ref):
                r[...] = jnp.zeros_like(r)

        @pl.when(ci == 0)
        def _():
            nxt_q[...] = jnp.zeros_like(nxt_q)
            nxt_cv[...] = jnp.zeros_like(nxt_cv)
            rstate[...] = jnp.zeros_like(rstate)

        first = (c > 0).astype(F32)
        up = pm_ref[:, 0:POOL_W]
        z = pm_ref[:, OFF_Z:OFF_XBC]
        ux = pm_ref[:, OFF_XBC:OFF_DT]
        halo_p = hpool_ref[...] * first
        hprev = [hs_ref[0, 0, 0], hs_ref[0, 0, 1]]
        cw, cb, hp, gssd, wpool, pscale = cw_ref[...], cb_ref[...], hp_ref[...], gs_ref[...], wp_ref[...], ps_ref[...]
        o = _chunk_forward(up, z, ux, pm_ref[:, OFF_DT:], halo_p, None, hprev, cw, cb, hp, gssd, wpool, pscale,
                           c * CHUNK, y_scr, cv=cv_ref[...])
        L = CHUNK
        dy_pool = dy_ref[:, 0:POOL_W].astype(F32)
        dy_ssd = dy_ref[:, POOL_W:].astype(F32)

        dvec_ref[1:2, 0:POOL_W] += jnp.sum(dy_pool * jnp.concatenate(o["yp"], axis=1), axis=0, keepdims=True)
        dyp = dy_pool * pscale
        qs = []
        dps = []
        for gi in range(len(WINDOWS)):
            sl = slice(gi * LANES, (gi + 1) * LANES)
            dwp_ref[gi] += _mm_tn(o["p"][gi], dyp[:, sl])
            dpg = _mm_nt(dyp[:, sl], wpool[gi])
            dps.append(dpg)
            qs.append(dpg * o["inv"][gi])
        q = jnp.concatenate(qs, axis=1)
        e = jnp.concatenate([q, nxt_q[...]], axis=0)
        n = L + POOL_HALO
        s2 = e + pltpu.roll(e, n - 1, 0)
        s4 = s2 + pltpu.roll(s2, n - 2, 0)
        s8 = s4 + pltpu.roll(s4, n - 4, 0)
        s16 = s8 + pltpu.roll(s8, n - 8, 0)
        sums = (s2, s4, s8, s16)
        for gi in range(len(WINDOWS)):
            sl = slice(gi * LANES, (gi + 1) * LANES)
            dpm_ref[:, sl] = (sums[gi][:L, sl] - dps[gi]).astype(BF16)
        nxt_q[...] = q[:POOL_HALO]

        yn, y, silz, sz = o["yn"], o["y"], o["silz"], o["sz"]
        dvec_ref[0:1] += jnp.sum(dy_ssd * yn, axis=0, keepdims=True)
        dyn = dy_ssd * gssd
        dyz = []
        for g in range(N_GROUPS):
            gs = slice(g * GROUP_W, (g + 1) * GROUP_W)
            mean = jnp.mean(dyn[:, gs] * yn[:, gs], axis=-1, keepdims=True)
            dyz.append(o["rg"][g] * (dyn[:, gs] - yn[:, gs] * mean))
        dyz = jnp.concatenate(dyz, axis=1)
        dyv = dyz * silz
        dpm_ref[:, OFF_Z:OFF_XBC] = (dyz * y * (sz * (1.0 + z * (1.0 - sz)))).astype(BF16)

        X, Bm, Cm, xdt = o["X"], o["Bm"], o["Cm"], o["xdt"]
        exm = o["exm"]
        rdm = _reduce_mat()
        lane = lax.broadcasted_iota(jnp.int32, (1, LANES), 1)
        sub = lax.broadcasted_iota(jnp.int32, (LANES, 1), 0)
        dX = o["dskip_rep"] * dyv
        yoff_full = jnp.concatenate(o["yoff"], axis=1)
        rs = jnp.zeros((L, LANES), F32)
        cs_t = jnp.zeros((LANES, L), F32)
        dBs, dCs = [], []
        rh_sums = []
        ddec = []
        for g in range(N_GROUPS):
            gs = slice(g * GROUP_W, (g + 1) * GROUP_W)
            Bg = Bm[:, g * N_STATE:(g + 1) * N_STATE]
            Cg = Cm[:, g * N_STATE:(g + 1) * N_STATE]
            Gg = o["G"][g]
            R = rstate[g]
            dwm = dyv[:, gs] * o["eo_rep"][:, gs]
            dC = _mm_nt(dwm, hprev[g])
            dH = _mm_tn(Cg, dwm)
            dG = jnp.zeros((L, L), F32)
            for hh in range(hpg):
                h = g * hpg + hh
                hs = slice(h * HEAD_DIM, (h + 1) * HEAD_DIM)
                lm = o["lms"][h]
                m_h = Gg * lm
                dM = _mm_nt(dyv[:, hs], xdt[:, hs])
                dx_scr[:, hs] = _mm_tn(m_h, dyv[:, hs])
                qm = dM * m_h
                rs = rs + jnp.sum(qm, axis=1, keepdims=True) * (lane == h).astype(F32)
                cs_t = cs_t + (sub == h).astype(F32) * jnp.sum(qm, axis=0, keepdims=True)
                dG = dG + dM * lm
            dC = dC + _mm(dG, Bg)
            dB = _mm_tn(dG, Cg)
            zx = _mm(Bg, R)
            dxdt_state = zx * o["dec_rep"][:, gs]
            ddec.append(zx * xdt[:, gs])
            dB = dB + _mm_nt(o["xdec"][g], R)
            rh_sums.append(jnp.sum(R * hprev[g], axis=0, keepdims=True))
            rstate[g] = dH + o["cd_rep"][:, gs] * R
            dx_scr[:, gs] = dx_scr[:, gs] + dxdt_state
            dBs.append(dB)
            dCs.append(dC)
        dxdt = dx_scr[...]
        tail = jnp.concatenate([jnp.sum(dyv * X, axis=0, keepdims=True), jnp.concatenate(rh_sums, axis=1),
                                jnp.zeros((6, D), F32)], axis=0)
        red = _dot01(jnp.concatenate([dyv * yoff_full, jnp.concatenate(ddec, axis=1), dxdt * X, tail], axis=0), rdm, 2)
        d_dskip, dcd_row = red[3 * L:3 * L + 1], red[3 * L + 1:3 * L + 2]
        ddec_h = red[L:2 * L] * o["dec"]
        dcum_last = jnp.sum(ddec_h, axis=0, keepdims=True) + dcd_row * o["cd"]
        dcum = red[0:L] + rs - cs_t.T - ddec_h + (sub == L - 1).astype(F32) * dcum_last
        dda = _dot01(o["anti"], dcum, 3, split_lhs=False)
        ddt_v = dda * o["a_row"] + red[2 * L:3 * L]
        dX = dX + dxdt * o["dt_rep"]
        head_mask = (lane < N_HEADS).astype(F32)
        d_alog = jnp.sum(dda * o["dt"], axis=0, keepdims=True) * o["a_row"] * head_mask
        dpre = ddt_v * _sigmoid(o["pre"]) * head_mask
        dpm_ref[:, OFF_DT:] = dpre.astype(BF16)
        d_dtb = jnp.sum(dpre, axis=0, keepdims=True)
        dhp_ref[...] += jnp.concatenate([d_dtb, d_alog, d_dskip * head_mask, jnp.zeros((5, LANES), F32)], axis=0)

        dxbc = jnp.concatenate([dX] + dBs + dCs, axis=1)
        sg, cv = o["sg"], o["cv"]
        dcv = dxbc * (sg * (1.0 + cv * (1.0 - sg)))
        e2 = jnp.concatenate([dcv, nxt_cv[...]], axis=0)
        n2 = L + CONV_HALO
        ahead = [dcv, pltpu.roll(e2, n2 - 1, 0)[:L], pltpu.roll(e2, n2 - 2, 0)[:L], pltpu.roll(e2, n2 - 3, 0)[:L]]
        dconv_ref[0:5] += jnp.concatenate(
            [jnp.sum(ux * ahead[3 - k], axis=0, keepdims=True) for k in range(4)]
            + [jnp.sum(dcv, axis=0, keepdims=True)], axis=0)
        dux = ahead[0] * cw[3:4] + ahead[1] * cw[2:3] + ahead[2] * cw[1:2] + ahead[3] * cw[0:1]
        dpm_ref[:, OFF_XBC:OFF_DT] = dux.astype(BF16)
        nxt_cv[...] = dcv[:CONV_HALO]

    def full(shape):
        return pl.BlockSpec(shape, lambda b, c: (0,) * len(shape))

    def rowblk(b, c):
        return b * nc + (nc - 1 - c)

    hp_blocks = CHUNK // POOL_HALO
    T = nb * seq
    return pl.pallas_call(
        body, name="mixer_bwd", grid=(nb, nc),
        in_specs=[pl.BlockSpec((CHUNK, PROJ_W), lambda b, c: (rowblk(b, c), 0)),
                  pl.BlockSpec((POOL_HALO, POOL_W), lambda b, c: (jnp.maximum(rowblk(b, c) * hp_blocks - 1, 0), 0)),
                  pl.BlockSpec((CHUNK, CONV_CH), lambda b, c: (rowblk(b, c), 0)),
                  pl.BlockSpec((CHUNK, MIX_W), lambda b, c: (rowblk(b, c), 0)),
                  pl.BlockSpec((1, 1, N_GROUPS, N_STATE, GROUP_W), lambda b, c: (b, nc - 1 - c, 0, 0, 0)),
                  full((4, CONV_CH)), full((1, CONV_CH)), full((8, LANES)), full((1, D)),
                  full((4, LANES, LANES)), full((1, POOL_W))] + [ANY_SPEC] * na,
        out_specs=[pl.BlockSpec((CHUNK, PROJ_W), lambda b, c: (rowblk(b, c), 0)),
                   full((8, CONV_CH)), full((8, LANES)), full((8, D)), full((4, LANES, LANES))],
        out_shape=[jax.ShapeDtypeStruct((T, PROJ_W), BF16),
                   jax.ShapeDtypeStruct((8, CONV_CH), F32), jax.ShapeDtypeStruct((8, LANES), F32),
                   jax.ShapeDtypeStruct((8, D), F32), jax.ShapeDtypeStruct((4, LANES, LANES), F32)],
        scratch_shapes=[pltpu.VMEM((POOL_HALO, POOL_W), F32), pltpu.VMEM((CONV_HALO, CONV_CH), F32),
                        pltpu.VMEM((N_GROUPS, N_STATE, GROUP_W), F32), pltpu.VMEM((CHUNK, D), F32),
                        pltpu.VMEM((CHUNK, D), F32)],
        compiler_params=_cparams(),
    )(pm, pm, cvs, dym, hstates, cw, cb, hp, gssd, wpool, pscale, *after)


def _mlp_fused(x2, ymix, target, mod, g_mlp, g_final, w_out, w_up, w_down, seq):
    T = x2.shape[0]
    tm = min(256, seq)
    tps = seq // tm
    nblk = D_FF // FF_BLK

    def body(x_ref, ym_ref, tg_ref, mod_ref, gm_ref, gf_ref, wo_ref, wu_ref, wd_ref,
             da_ref, dym_ref, dh1_ref, u2_ref, f_ref, dup_ref, ddn_ref, dmod_ref, acc_ref, relu_scr):
        i = pl.program_id(0)

        @pl.when(i == 0)
        def _():
            acc_ref[...] = jnp.zeros_like(acc_ref)

        @pl.when(i % tps == 0)
        def _():
            dmod_ref[...] = jnp.zeros_like(dmod_ref)

        md = mod_ref[0]
        gate_m, shift_f, scale_f, gate_f = md[2:3], md[3:4], md[4:5], md[5:6]
        g_mlp, g_fin = gm_ref[...], gf_ref[...]
        a = jnp.dot(ym_ref[...], wo_ref[...], preferred_element_type=F32)
        h1 = x_ref[...] + gate_m * a
        r2 = lax.rsqrt(jnp.mean(h1 * h1, axis=-1, keepdims=True) + EPS)
        n2 = h1 * r2
        u2 = (n2 * g_mlp) * (1.0 + scale_f) + shift_f
        u2b = u2.astype(BF16)
        u2_ref[...] = u2b
        dn = jnp.zeros((tm, D), F32)
        for j in range(nblk):
            js = slice(j * FF_BLK, (j + 1) * FF_BLK)
            upj = jnp.maximum(jnp.dot(u2b, wu_ref[j], preferred_element_type=F32), 0.0)
            relu_scr[:, js] = upj
            fj = (upj * upj).astype(BF16)
            f_ref[:, js] = fj
            dn = dn + jnp.dot(fj, wd_ref[j], preferred_element_type=F32)
        h2 = h1 + gate_f * dn
        r3 = lax.rsqrt(jnp.mean(h2 * h2, axis=-1, keepdims=True) + EPS)
        n3 = h2 * r3
        err = n3 * g_fin - tg_ref[...]
        loss = 0.5 * jnp.sum(jnp.mean(err * err, axis=-1, keepdims=True), axis=0, keepdims=True)
        dout = err * (1.0 / D)
        d_gfin = jnp.sum(dout * n3, axis=0, keepdims=True)
        dn3 = dout * g_fin
        dh2 = r3 * (dn3 - n3 * jnp.mean(dn3 * n3, axis=-1, keepdims=True))
        d_gate_f = jnp.sum(dh2 * dn, axis=0, keepdims=True)
        ddn = (gate_f * dh2).astype(BF16)
        ddn_ref[...] = ddn
        du2 = jnp.zeros((tm, D), F32)
        for j in range(nblk):
            js = slice(j * FF_BLK, (j + 1) * FF_BLK)
            dfj = lax.dot_general(ddn, wd_ref[j], (((1,), (1,)), ((), ())), preferred_element_type=F32)
            dupj = (dfj * (2.0 * relu_scr[:, js])).astype(BF16)
            dup_ref[:, js] = dupj
            du2 = du2 + lax.dot_general(dupj, wu_ref[j], (((1,), (1,)), ((), ())), preferred_element_type=F32)
        d_scale_f = jnp.sum(du2 * (n2 * g_mlp), axis=0, keepdims=True)
        d_shift_f = jnp.sum(du2, axis=0, keepdims=True)
        d_gmlp = jnp.sum(du2 * (1.0 + scale_f) * n2, axis=0, keepdims=True)
        dn2 = du2 * (g_mlp * (1.0 + scale_f))
        dh1 = dh2 + r2 * (dn2 - n2 * jnp.mean(dn2 * n2, axis=-1, keepdims=True))
        dh1_ref[...] = dh1
        d_gate_m = jnp.sum(dh1 * a, axis=0, keepdims=True)
        da = (gate_m * dh1).astype(BF16)
        da_ref[...] = da
        dym_ref[...] = lax.dot_general(da, wo_ref[...], (((1,), (1,)), ((), ())),
                                       preferred_element_type=F32).astype(BF16)
        dmod_ref[0] += jnp.concatenate([jnp.zeros((2, D), F32), d_gate_m, d_shift_f, d_scale_f, d_gate_f,
                                        jnp.zeros((2, D), F32)], axis=0)
        acc_ref[...] += jnp.concatenate([d_gmlp, d_gfin, loss * jnp.ones((1, D), F32), jnp.zeros((5, D), F32)], axis=0)

    whole = pl.BlockSpec(memory_space=pltpu.VMEM)

    def tok(w):
        return pl.BlockSpec((tm, w), lambda i: (i, 0))

    def vec():
        return pl.BlockSpec((1, D), lambda i: (0, 0))

    nb = T // seq
    return pl.pallas_call(
        body, name="mlp_fused", grid=(T // tm,),
        in_specs=[tok(D), tok(MIX_W), tok(D), pl.BlockSpec((1, 8, D), lambda i: (i // tps, 0, 0)), vec(), vec(),
                  whole, whole, whole],
        out_specs=[tok(D), tok(MIX_W), tok(D), tok(D), tok(D_FF), tok(D_FF), tok(D),
                   pl.BlockSpec((1, 8, D), lambda i: (i // tps, 0, 0)), pl.BlockSpec((8, D), lambda i: (0, 0))],
        out_shape=[jax.ShapeDtypeStruct((T, D), BF16), jax.ShapeDtypeStruct((T, MIX_W), BF16),
                   jax.ShapeDtypeStruct((T, D), F32), jax.ShapeDtypeStruct((T, D), BF16),
                   jax.ShapeDtypeStruct((T, D_FF), BF16), jax.ShapeDtypeStruct((T, D_FF), BF16),
                   jax.ShapeDtypeStruct((T, D), BF16), jax.ShapeDtypeStruct((nb, 8, D), F32),
                   jax.ShapeDtypeStruct((8, D), F32)],
        scratch_shapes=[pltpu.VMEM((tm, D_FF), F32)],
        compiler_params=_cparams(),
    )(x2, ymix, target, mod, g_mlp, g_final, w_out, w_up, w_down)


def _in_bwd(x2, dh1, dpb, mod, g_mix, w_cat, dmod_a, acc_a, seq):
    T = x2.shape[0]
    tm = min(1024, seq)
    tps = seq // tm
    steps = T // tm

    def body(x_ref, dh_ref, dpb_ref, mod_ref, g_ref, w_ref, dma_ref, acca_ref, dx_ref, dmod_ref, acc_ref):
        i = pl.program_id(0)

        @pl.when(i == 0)
        def _():
            acc_ref[...] = acca_ref[...]

        @pl.when(i % tps == 0)
        def _():
            dmod_ref[...] = dma_ref[...]

        du = lax.dot_general(dpb_ref[...], w_ref[...], (((1,), (1,)), ((), ())), preferred_element_type=F32)
        x = x_ref[...]
        md = mod_ref[0]
        g = g_ref[...]
        r = lax.rsqrt(jnp.mean(x * x, axis=-1, keepdims=True) + EPS)
        n1 = x * r
        d_scale = jnp.sum(du * (n1 * g), axis=0, keepdims=True)
        d_shift = jnp.sum(du, axis=0, keepdims=True)
        d_g = jnp.sum(du * (1.0 + md[1:2]) * n1, axis=0, keepdims=True)
        dn1 = du * (g * (1.0 + md[1:2]))
        dx_ref[...] = dh_ref[...] + r * (dn1 - n1 * jnp.mean(dn1 * n1, axis=-1, keepdims=True))
        dmod_ref[0] += jnp.concatenate([d_shift, d_scale, jnp.zeros((6, D), F32)], axis=0)
        acc_ref[...] += jnp.concatenate([jnp.zeros((3, D), F32), d_g, jnp.zeros((4, D), F32)], axis=0)

    whole = pl.BlockSpec(memory_space=pltpu.VMEM)
    nb = T // seq
    return pl.pallas_call(
        body, name="in_bwd", grid=(steps,),
        in_specs=[pl.BlockSpec((tm, D), lambda i: (i, 0)), pl.BlockSpec((tm, D), lambda i: (i, 0)),
                  pl.BlockSpec((tm, PROJ_W), lambda i: (i, 0)),
                  pl.BlockSpec((1, 8, D), lambda i: (i // tps, 0, 0)), pl.BlockSpec((1, D), lambda i: (0, 0)),
                  whole, pl.BlockSpec((1, 8, D), lambda i: (i // tps, 0, 0)), pl.BlockSpec((8, D), lambda i: (0, 0))],
        out_specs=[pl.BlockSpec((tm, D), lambda i: (i, 0)),
                   pl.BlockSpec((1, 8, D), lambda i: (i // tps, 0, 0)), pl.BlockSpec((8, D), lambda i: (0, 0))],
        out_shape=[jax.ShapeDtypeStruct((T, D), F32),
                   jax.ShapeDtypeStruct((nb, 8, D), F32), jax.ShapeDtypeStruct((8, D), F32)],
        compiler_params=_cparams(),
    )(x2, dh1, dpb, mod, g_mix, w_cat, dmod_a, acc_a)


def _dw_in(u_b, dpb, in_cols, shards):
    T = u_b.shape[0]
    bk = min(1024, T)
    nk = T // bk
    ns = len(shards)

    def body(*refs):
        u_ref, d_ref = refs[:2]
        sh_refs = refs[2:2 + ns]
        o_ref = refs[2 + ns]
        ga_refs = refs[3 + ns:3 + 2 * ns]
        acc, send, recv, loc = refs[3 + 2 * ns:]
        k = pl.program_id(0)
        gather = _Gather(sh_refs, ga_refs, send, recv, loc)
        gather.begin_hosted(k, nk)

        @pl.when(k == 0)
        def _():
            acc[...] = jnp.zeros_like(acc)

        acc[...] += jnp.dot(u_ref[...].T, d_ref[...], preferred_element_type=F32)

        @pl.when(k == nk - 1)
        def _():
            for j in range(N_DEV):
                o_ref[j] = acc[:, in_cols * j:in_cols * (j + 1)].astype(BF16)

        gather.end_hosted(k, nk)

    return pl.pallas_call(
        body, name="dw_in", grid=(nk,),
        in_specs=[pl.BlockSpec((bk, D), lambda k: (k, 0)), pl.BlockSpec((bk, PROJ_W), lambda k: (k, 0))]
        + [ANY_SPEC] * ns,
        out_specs=[pl.BlockSpec((N_DEV, D, in_cols), lambda k: (0, 0, 0))] + [ANY_SPEC] * ns,
        out_shape=[jax.ShapeDtypeStruct((N_DEV, D, in_cols), BF16)]
        + [jax.ShapeDtypeStruct((N_DEV,) + v.shape, v.dtype) for v in shards],
        scratch_shapes=[pltpu.VMEM((D, PROJ_W), F32)] + _gather_scratch(shards),
        compiler_params=_cparams(),
    )(u_b, dpb, *shards)


def _dw_blocks(a, b, name, by_rows, per_step=1, after=()):
    T, M = a.shape
    N = b.shape[1]
    bk = min(4096, T)
    nk = T // bk
    whole = pl.BlockSpec(memory_space=pltpu.VMEM)
    if by_rows:
        rows = M // N_DEV
        am = rows * per_step
        nblk = N_DEV // per_step
        a_spec, b_spec = pl.BlockSpec((bk, am), lambda i, k: (k, i)), whole
        out_blk, acc_shape = (per_step, rows, N), (am, N)
    else:
        cols = N // N_DEV
        nblk = N_DEV
        a_spec, b_spec = whole, pl.BlockSpec((bk, cols), lambda i, k: (k, i))
        out_blk, acc_shape = (1, M, cols), (M, cols)

    def body(a_ref, b_ref, *rest):
        o_ref, acc = rest[len(after):]
        k = pl.program_id(1)

        @pl.when(k == 0)
        def _():
            acc[...] = jnp.zeros_like(acc)

        tok = pl.ds(pl.multiple_of(k * bk, bk), bk)
        a_blk = a_ref[...] if by_rows else a_ref[tok, :]
        b_blk = b_ref[tok, :] if by_rows else b_ref[...]
        acc[...] += lax.dot_general(a_blk, b_blk, (((0,), (0,)), ((), ())), preferred_element_type=F32)

        @pl.when(k == nk - 1)
        def _():
            o_ref[...] = acc[...].reshape(out_blk).astype(BF16)

    return pl.pallas_call(
        body, name=name, grid=(nblk, nk), in_specs=[a_spec, b_spec] + [ANY_SPEC] * len(after),
        out_specs=pl.BlockSpec(out_blk, lambda i, k: (i, 0, 0)),
        out_shape=jax.ShapeDtypeStruct((N_DEV,) + out_blk[1:], BF16),
        scratch_shapes=[pltpu.VMEM(acc_shape, F32)],
        compiler_params=_cparams(),
    )(a, b, *after)


def _adam_parts(parts, w, m, v, name):
    rows, cols = w.shape
    br = rows
    for cand in range(rows, 15, -16):
        if rows % cand == 0 and cand * cols * 4 <= ADAM_BLOCK_BYTES:
            br = cand
            break

    def body(p_ref, w_ref, m_ref, v_ref, g_out, dl_out, m_out, v_out):
        g = p_ref[0].astype(F32)
        for k in range(1, N_DEV):
            g = g + p_ref[k].astype(F32)
        g_out[...] = g
        dl, mn, vn = _adam_math(w_ref[...], g, m_ref[...], v_ref[...])
        dl_out[...] = dl
        m_out[...] = mn
        v_out[...] = vn

    wspec = pl.BlockSpec((br, cols), lambda i: (i, 0))
    return pl.pallas_call(
        body, name=name, grid=(rows // br,),
        in_specs=[pl.BlockSpec((N_DEV, br, cols), lambda i: (0, i, 0)), wspec, wspec, wspec],
        out_specs=[wspec] * 4, out_shape=[jax.ShapeDtypeStruct((rows, cols), F32)] * 4,
        compiler_params=_cparams(),
    )(parts, w, m, v)


def _adam_plain(g, w, m, v, name):
    def body(g_ref, w_ref, m_ref, v_ref, dl_out, m_out, v_out):
        dl, mn, vn = _adam_math(w_ref[...], g_ref[...], m_ref[...], v_ref[...])
        dl_out[...] = dl
        m_out[...] = mn
        v_out[...] = vn

    return pl.pallas_call(body, name=name, out_shape=[jax.ShapeDtypeStruct(w.shape, F32)] * 3,
                          compiler_params=_cparams())(g, w, m, v)


SMALL_PARAMS = ("b_ada", "g_mix", "conv_b", "dt_bias", "a_log", "d_skip", "g_ssd", "pool_scale", "g_mlp", "g_final")


def _small_adam(gathered, params):
    n_par = len(SMALL_PARAMS)
    nb = gathered[0].shape[1]

    def body(*refs):
        dmod_ref, acc_ref, conv_ref, vec_ref, hd_ref = refs[:5]
        par_refs = refs[5:5 + 3 * n_par]
        out_refs = refs[5 + 3 * n_par:5 + 7 * n_par]
        cw_out, acc_out = refs[5 + 7 * n_par:]

        def total(ref):
            t = ref[0]
            for k in range(1, N_DEV):
                t = t + ref[k]
            return t

        dm = total(dmod_ref)
        dmb = dm[0]
        for b in range(1, nb):
            dmb = dmb + dm[b]
        ac, cv, vc, hd = total(acc_ref), total(conv_ref), total(vec_ref), total(hd_ref)
        cw_out[...] = cv[0:4]
        acc_out[...] = ac
        grads = {
            "b_ada": jnp.concatenate([dmb[r:r + 1] for r in range(6)], axis=1), "g_mix": ac[3:4], "conv_b": cv[4:5],
            "dt_bias": hd[0:1, 0:N_HEADS], "a_log": hd[1:2, 0:N_HEADS], "d_skip": hd[2:3, 0:N_HEADS],
            "g_ssd": vc[0:1], "pool_scale": vc[1:2, 0:POOL_W], "g_mlp": ac[0:1], "g_final": ac[1:2],
        }
        for i, name in enumerate(SMALL_PARAMS):
            w_ref, m_ref, v_ref = par_refs[3 * i:3 * i + 3]
            g = grads[name]
            dl, mn, vn = _adam_math(w_ref[...], g, m_ref[...], v_ref[...])
            g_o, d_o, m_o, v_o = out_refs[4 * i:4 * i + 4]
            g_o[...] = g
            d_o[...] = dl
            m_o[...] = mn
            v_o[...] = vn

    flat = [a for name in SMALL_PARAMS for a in params[name]]
    out_shape = [jax.ShapeDtypeStruct(params[name][0].shape, F32) for name in SMALL_PARAMS for _ in range(4)]
    out_shape += [jax.ShapeDtypeStruct((4, CONV_CH), F32), jax.ShapeDtypeStruct((8, D), F32)]
    return pl.pallas_call(body, name="small_adam", out_shape=out_shape, compiler_params=_cparams())(*gathered, *flat)


def kernel(x, c, w_ada, b_ada, g_mix, w_in, conv_w, conv_b, dt_bias, a_log, d_skip, g_ssd, w_pool, pool_scale, w_out, g_mlp, w_up, w_down, g_final, loss_target, m_w_ada, m_b_ada, m_g_mix, m_w_in, m_conv_w, m_conv_b, m_dt_bias, m_a_log, m_d_skip, m_g_ssd, m_w_pool, m_pool_scale, m_w_out, m_g_mlp, m_w_up, m_w_down, m_g_final, v_w_ada, v_b_ada, v_g_mix, v_w_in, v_conv_w, v_conv_b, v_dt_bias, v_a_log, v_d_skip, v_g_ssd, v_w_pool, v_pool_scale, v_w_out, v_g_mlp, v_w_up, v_w_down, v_g_final):
    nb, seq, _ = x.shape
    T = nb * seq
    me = 4 * lax.axis_index("x") + 2 * lax.axis_index("y") + lax.axis_index("c")
    in_cols = w_in.shape[2]
    ada_cols = w_ada.shape[2]
    cw_cols = conv_w.shape[2]

    win_t = jnp.pad(w_in[0].astype(BF16).T, ((0, -in_cols % 16), (0, 0)))
    c_g, cw_g, win_g = _all_gather([c, conv_w[0], win_t], "ag_first")
    c_all = c_g.reshape(N_DEV * nb, D)
    cw_full = cw_g.transpose(1, 0, 2).reshape(4, CONV_CH)

    b_slice = lax.dynamic_slice(b_ada, (0, me * ada_cols), (1, ada_cols))
    mod_cols = _ada_fwd(c_all, w_ada[0], b_slice)
    (mod_g,) = _all_gather([mod_cols], "ag_mod")
    mod_all = mod_g.transpose(1, 0, 2).reshape(N_DEV * nb, 6, D)
    mod_mine = lax.dynamic_slice(mod_all, (me * nb, 0, 0), (nb, 6, D))
    mod = jnp.pad(mod_mine, ((0, 0), (0, 2), (0, 0)))

    x2 = x.reshape(T, D)
    tg2 = loss_target.reshape(T, D)
    heads = jnp.pad(jnp.concatenate([dt_bias, a_log, d_skip], axis=0), ((0, 5), (0, LANES - N_HEADS)))
    wpool_b = w_pool[0]
    u_b, pm, w_cat, wup_g = _mix_in(x2, mod, g_mix, win_g, in_cols, seq, [w_up[0].astype(BF16)])
    ymix, hstates, cvs, wout_g, wdn_g = _mixer_fwd(
        pm, cw_full, conv_b, heads, g_ssd, wpool_b, pool_scale, nb, seq,
        [w_out[0].astype(BF16), w_down[0].astype(BF16)])
    da_b, dym, dh1, u2_b, f_b, dup_b, ddn_b, dmod_a, acc_a = _mlp_fused(
        x2, ymix, tg2, mod, g_mlp, g_final.reshape(1, D), wout_g.reshape(MIX_W, D), wup_g, wdn_g, seq)

    gout_p = _dw_blocks(ymix, da_b, "dw_out", True, per_step=4)
    gup_p = _dw_blocks(u2_b, dup_b, "dw_up", False)
    ex_a = _exchange_start([gout_p, gup_p], "ga_start")
    gdn_p = _dw_blocks(f_b, ddn_b, "dw_down", True, after=[ex_a[4]])
    ex_b = _exchange_start([gdn_p], "gb_start")
    dpb, d_conv, d_heads, d_vec, d_wpool = _mixer_bwd(
        pm, cvs, dym, hstates, cw_full, conv_b, heads, g_ssd, wpool_b, pool_scale, nb, seq, after=[ex_b[4]])
    gin_p, conv_g, vec_g, heads_g, wpool_parts = _dw_in(
        u_b, dpb, in_cols, [d_conv, d_vec, d_heads, d_wpool.reshape(4 * LANES, LANES)])
    ex_in = _exchange_start([gin_p], "gin_start")
    grad_x2, dmod, acc = _in_bwd(x2, dh1, dpb, mod, g_mix + ex_in[4][0:1, 0:1], w_cat, dmod_a, acc_a, seq)

    gout_r, gup_r = _exchange_wait(ex_a, dmod, "ga_wait", me)
    (gdn_r,) = _exchange_wait(ex_b, dmod, "gb_wait", me)
    g_out, d_out, nm_out, nv_out = _adam_parts(gout_r, w_out[0], m_w_out[0], v_w_out[0], "adam_w_out")
    g_up, d_up, nm_up, nv_up = _adam_parts(gup_r, w_up[0], m_w_up[0], v_w_up[0], "adam_w_up")
    g_dn, d_dn, nm_dn, nv_dn = _adam_parts(gdn_r, w_down[0], m_w_down[0], v_w_down[0], "adam_w_down")

    dmod_g, acc_g = _all_gather([dmod, acc], "ag_small_bwd", after=[nm_out, nm_up, nm_dn])
    pool2 = (4 * LANES, LANES)
    wpool_outs = _adam_parts(wpool_parts, w_pool.reshape(pool2), m_w_pool.reshape(pool2), v_w_pool.reshape(pool2),
                             "adam_w_pool")
    small_params = {
        "b_ada": (b_ada, m_b_ada, v_b_ada), "g_mix": (g_mix, m_g_mix, v_g_mix), "conv_b": (conv_b, m_conv_b, v_conv_b),
        "dt_bias": (dt_bias, m_dt_bias, v_dt_bias), "a_log": (a_log, m_a_log, v_a_log),
        "d_skip": (d_skip, m_d_skip, v_d_skip), "g_ssd": (g_ssd, m_g_ssd, v_g_ssd),
        "pool_scale": (pool_scale, m_pool_scale, v_pool_scale), "g_mlp": (g_mlp, m_g_mlp, v_g_mlp),
        "g_final": tuple(a.reshape(1, D) for a in (g_final, m_g_final, v_g_final)),
    }
    small_res = _small_adam([dmod_g, acc_g, conv_g, vec_g, heads_g], small_params)
    g_cw_full, acc_sum = small_res[-2:]
    loss = acc_sum[2, 0]

    g_cw = lax.dynamic_slice(g_cw_full, (0, me * cw_cols), (4, cw_cols))
    d_cwp, nm_cwp, nv_cwp = _adam_plain(g_cw, conv_w[0], m_conv_w[0], v_conv_w[0], "adam_conv_w")

    dmod_all = dmod_g[:, :, 0:6].reshape(N_DEV * nb, 6 * D)
    dmod_slice = lax.dynamic_slice(dmod_all, (0, me * ada_cols), (N_DEV * nb, ada_cols))
    g_ada, d_ada, nm_ada, nv_ada = _ada_bwd_adam(c_all, dmod_slice, w_ada[0], m_w_ada[0], v_w_ada[0])

    ex_after = nm_ada[0:8, 0:LANES] + acc_sum[:, 0:LANES]
    (gin_r,) = _exchange_wait(ex_in, ex_after, "gin_wait", me)
    g_in, d_in, nm_in, nv_in = _adam_parts(gin_r, w_in[0], m_w_in[0], v_w_in[0], "adam_w_in")

    def small_outs(kind, wpool):
        res = {name: small_res[4 * i + kind] for i, name in enumerate(SMALL_PARAMS)}
        res["g_final"] = res["g_final"].reshape(D)
        res["w_pool"] = wpool.reshape(1, 4, LANES, LANES)
        return res

    def big_outs(ada, win, cwp, wout, wup, wdn):
        return {"w_ada": ada[None], "w_in": win.reshape(1, D, in_cols), "conv_w": cwp[None], "w_out": wout[None],
                "w_up": wup[None], "w_down": wdn[None]}

    order = ["w_ada", "b_ada", "g_mix", "w_in", "conv_w", "conv_b", "dt_bias", "a_log", "d_skip", "g_ssd", "w_pool",
             "pool_scale", "w_out", "g_mlp", "w_up", "w_down", "g_final"]
    groups = [
        {**small_outs(0, wpool_outs[0]), **big_outs(g_ada, g_in, g_cw, g_out, g_up, g_dn)},
        {**small_outs(1, wpool_outs[1]), **big_outs(d_ada, d_in, d_cwp, d_out, d_up, d_dn)},
        {**small_outs(2, wpool_outs[2]), **big_outs(nm_ada, nm_in, nm_cwp, nm_out, nm_up, nm_dn)},
        {**small_outs(3, wpool_outs[3]), **big_outs(nv_ada, nv_in, nv_cwp, nv_out, nv_up, nv_dn)},
    ]
    outs = [loss, grad_x2.reshape(nb, seq, D)]
    for grp in groups:
        outs += [grp[n] for n in order]
    return tuple(outs)
```

```python
import functools

import jax
import jax.numpy as jnp
from jax import lax
from jax.experimental import pallas as pl
from jax.experimental.pallas import tpu as pltpu

F32, BF16 = jnp.float32, jnp.bfloat16
MESH = pl.DeviceIdType.MESH
N_DEV = 8
D = 1024
LANES = 128
CHUNK = 128
POOL_W = 512
WINDOWS = (2, 4, 8, 16)
N_HEADS = 16
HEAD_DIM = 64
N_GROUPS = 2
GROUP_W = 512
N_STATE = 128
CONV_CH = 1536
OFF_Z, OFF_XBC, OFF_DT, IN_W = 512, 1536, 3072, 3088
PROJ_W = OFF_DT + LANES
MIX_W = 1536
D_FF = 4096
FF_BLK = 512
EPS = 1e-5
LR, B1, B2, AEPS, WD, STEP = 0.001, 0.9, 0.999, 1e-08, 0.01, 10
POOL_HALO = 16
CONV_HALO = 8
VMEM_LIMIT = 56 << 20
ADAM_BLOCK_BYTES = 1 << 20


def _cparams(**kw):
    return pltpu.CompilerParams(vmem_limit_bytes=VMEM_LIMIT, **kw)


def _mm(a, b):
    return jnp.dot(a.astype(BF16), b.astype(BF16), preferred_element_type=F32)


def _mm_nt(a, b):
    return lax.dot_general(a.astype(BF16), b.astype(BF16), (((1,), (1,)), ((), ())), preferred_element_type=F32)


def _mm_tn(a, b):
    return lax.dot_general(a.astype(BF16), b.astype(BF16), (((0,), (0,)), ((), ())), preferred_element_type=F32)


def _split_bf16(v, terms):
    parts, rest = [], v
    for t in range(terms):
        p = rest.astype(BF16)
        parts.append(p)
        if t + 1 < terms:
            rest = rest - p.astype(F32)
    return parts


def _dot01(a, b, terms, split_lhs=True):
    if split_lhs:
        bb = b.astype(BF16)
        prods = [jnp.dot(p, bb, preferred_element_type=F32) for p in _split_bf16(a, terms)]
    else:
        ab = a.astype(BF16)
        prods = [jnp.dot(ab, p, preferred_element_type=F32) for p in _split_bf16(b, terms)]
    out = prods[0]
    for q in prods[1:]:
        out = out + q
    return out


def _sigmoid(v):
    return 1.0 / (1.0 + jnp.exp(-v))


def _expand_mat():
    r = lax.broadcasted_iota(jnp.int32, (LANES, D), 0)
    c = lax.broadcasted_iota(jnp.int32, (LANES, D), 1)
    return (r == c // HEAD_DIM).astype(F32)


def _reduce_mat():
    r = lax.broadcasted_iota(jnp.int32, (D, LANES), 0)
    c = lax.broadcasted_iota(jnp.int32, (D, LANES), 1)
    return (c == r // HEAD_DIM).astype(F32)


def _pos():
    return lax.axis_index("x"), lax.axis_index("y"), lax.axis_index("c")


GATHER_PIECES = 4
GATHER_PIECE_BYTES = 96 << 10


def _pieces(shape, dtype):
    rows = shape[0]
    size = jnp.dtype(dtype).itemsize
    for d in shape:
        size *= d
    whole_tiles = rows % (GATHER_PIECES * 16) == 0
    return GATHER_PIECES if whole_tiles and size // GATHER_PIECES >= GATHER_PIECE_BYTES else 1


class _Gather:
    def __init__(self, x_refs, o_refs, send, recv, loc):
        self.x_refs, self.o_refs, self.send, self.recv, self.loc = x_refs, o_refs, send, recv, loc
        self.n = len(x_refs)
        self.pieces = [_pieces(r.shape, r.dtype) for r in x_refs]
        self.base = [7 * sum(self.pieces[:a]) for a in range(self.n)]
        x, y, c = _pos()
        self.c = c
        self.me, self.sib = (x, y, c), (x, y, 1 - c)
        self.chips = [(1 - x, y), (x, 1 - y), (1 - x, 1 - y)]

    def _rows(self, a, p):
        rows = self.x_refs[a].shape[0] // self.pieces[a]
        return pl.ds(p * rows, rows)

    def _cp(self, a, p, k, block, to, own=False):
        dst = self.o_refs[a].at[4 * block[0] + 2 * block[1] + block[2], self._rows(a, p)]
        sem = self.base[a] + 7 * p + k
        return pltpu.make_async_remote_copy(
            src_ref=self.x_refs[a].at[self._rows(a, p)] if own else dst, dst_ref=dst,
            send_sem=self.send.at[sem], recv_sem=self.recv.at[sem], device_id=to, device_id_type=MESH)

    def _mine(self, a):
        me = self.me
        return pltpu.make_async_copy(self.x_refs[a], self.o_refs[a].at[4 * me[0] + 2 * me[1] + me[2]], self.loc.at[a])

    def _first(self, a, p):
        cps = [self._cp(a, p, 0, self.me, self.sib, own=True)]
        return cps + [self._cp(a, p, 1 + j, self.me, (*chip, self.c), own=True) for j, chip in enumerate(self.chips)]

    def _passed(self, a, p, j):
        return self._cp(a, p, 4 + j, (*self.chips[j], self.c), self.sib)

    def start(self):
        for a in range(self.n):
            self._mine(a).start()
        for p in range(max(self.pieces)):
            for a in range(self.n):
                if p < self.pieces[a]:
                    for cp in self._first(a, p):
                        cp.start()

    def forward(self, p):
        for j, chip in enumerate(self.chips):
            for a in range(self.n):
                if p < self.pieces[a]:
                    self._cp(a, p, 1 + j, (*chip, self.c), self.me).wait_recv()
                    self._passed(a, p, j).start()

    def finish(self):
        for a in range(self.n):
            for p in range(self.pieces[a]):
                self._cp(a, p, 0, self.sib, self.me).wait_recv()
                for j, chip in enumerate(self.chips):
                    self._cp(a, p, 4 + j, (*chip, 1 - self.c), self.me).wait_recv()
        for a in range(self.n):
            for p in range(self.pieces[a]):
                for cp in self._first(a, p):
                    cp.wait_send()
                for j in range(3):
                    self._passed(a, p, j).wait_send()
            self._mine(a).wait()

    def begin_hosted(self, step, steps):
        @pl.when(step == 0)
        def _():
            self.start()

        n_p = max(self.pieces)
        for p in range(n_p):
            @pl.when(step == min(((p + 1) * 7 * steps) // (8 * n_p), steps - 1))
            def _():
                self.forward(p)

    def end_hosted(self, step, steps):
        @pl.when(step == steps - 1)
        def _():
            self.finish()


class _Exchange:
    def __init__(self, x_refs, o_refs, send, recv, loc):
        self.x_refs, self.o_refs, self.send, self.recv, self.loc = x_refs, o_refs, send, recv, loc
        self.n = len(x_refs)
        x, y, c = _pos()
        self.me_i = 4 * x + 2 * y + c
        self.peers = []
        for k in range(1, N_DEV):
            px = 1 - x if (k >> 2) & 1 else x
            py = 1 - y if (k >> 1) & 1 else y
            pc = 1 - c if k & 1 else c
            self.peers.append(((px, py, pc), 4 * px + 2 * py + pc))

    def _mine(self, a):
        return pltpu.make_async_copy(self.x_refs[a].at[self.me_i], self.o_refs[a].at[self.me_i], self.loc.at[a])

    def _cp(self, a, k, landing):
        peer, peer_i = self.peers[k]
        return pltpu.make_async_remote_copy(
            src_ref=self.x_refs[a].at[peer_i], dst_ref=self.o_refs[a].at[landing],
            send_sem=self.send.at[a * 7 + k], recv_sem=self.recv.at[a * 7 + k],
            device_id=peer, device_id_type=MESH)

    def start(self):
        for a in range(self.n):
            self._mine(a).start()
            for k in range(N_DEV - 1):
                self._cp(a, k, self.me_i).start()

    def finish(self):
        for a in range(self.n):
            for k in range(N_DEV - 1):
                self._cp(a, k, self.peers[k][1]).wait_recv()
        for a in range(self.n):
            for k in range(N_DEV - 1):
                self._cp(a, k, self.me_i).wait_send()
            self._mine(a).wait()


def _gather_scratch(xs):
    n_sem = 7 * sum(_pieces(v.shape, v.dtype) for v in xs)
    return [pltpu.SemaphoreType.DMA((n_sem,)), pltpu.SemaphoreType.DMA((n_sem,)), pltpu.SemaphoreType.DMA((len(xs),))]


ANY_SPEC = pl.BlockSpec(memory_space=pl.ANY)


def _all_gather(xs, name, after=()):
    n, na = len(xs), len(after)

    def body(*refs):
        g = _Gather(refs[:n], refs[n + na:2 * n + na], *refs[2 * n + na:])
        g.start()
        for p in range(max(g.pieces)):
            g.forward(p)
        g.finish()

    return pl.pallas_call(
        body, name=name,
        out_shape=[jax.ShapeDtypeStruct((N_DEV,) + v.shape, v.dtype) for v in xs],
        in_specs=[ANY_SPEC] * (n + na), out_specs=[ANY_SPEC] * n, scratch_shapes=_gather_scratch(xs),
    )(*xs, *after)


def _all_gather_small(xs, name, after=()):
    n, na = len(xs), len(after)

    def body(*refs):
        x_refs, o_refs = refs[:n], refs[n + na:2 * n + na]
        send, recv, loc = refs[2 * n + na:]
        ex = _Exchange(x_refs, o_refs, send, recv, loc)

        def cp(a, k, slot):
            return pltpu.make_async_remote_copy(
                src_ref=x_refs[a], dst_ref=o_refs[a].at[slot], send_sem=send.at[a * 7 + k],
                recv_sem=recv.at[a * 7 + k], device_id=ex.peers[k][0], device_id_type=MESH)

        mine = [pltpu.make_async_copy(x_refs[a], o_refs[a].at[ex.me_i], loc.at[a]) for a in range(n)]
        for a in range(n):
            mine[a].start()
            for k in range(N_DEV - 1):
                cp(a, k, ex.me_i).start()
        for a in range(n):
            for k in range(N_DEV - 1):
                cp(a, k, ex.peers[k][1]).wait_recv()
        for a in range(n):
            for k in range(N_DEV - 1):
                cp(a, k, ex.me_i).wait_send()
            mine[a].wait()

    n_sem = (N_DEV - 1) * n
    return pl.pallas_call(
        body, name=name,
        out_shape=[jax.ShapeDtypeStruct((N_DEV,) + v.shape, v.dtype) for v in xs],
        in_specs=[ANY_SPEC] * (n + na), out_specs=[ANY_SPEC] * n,
        scratch_shapes=[pltpu.SemaphoreType.DMA((n_sem,)), pltpu.SemaphoreType.DMA((n_sem,)),
                        pltpu.SemaphoreType.DMA((n,))],
    )(*xs, *after)


HBM_SPEC = pl.BlockSpec(memory_space=pltpu.HBM)
SEM_SPEC = pl.BlockSpec(memory_space=pltpu.SEMAPHORE)
VMEM_SPEC = pl.BlockSpec(memory_space=pltpu.VMEM)
SPLIT_EFFECT = pltpu.SideEffectType.DATAFLOW_SIDE_EFFECTING


def _in_hbm(v):
    return pltpu.with_memory_space_constraint(v, pltpu.HBM)


def _exchange_start(blocks, name):
    n = len(blocks)

    def body(*refs):
        x_refs, land_refs = refs[:n], refs[n:2 * n]
        send, recv = refs[2 * n:2 * n + 2]
        token = refs[-1]
        ex = _Exchange(x_refs, land_refs, send, recv, None)
        for a in range(n):
            for k in range(N_DEV - 1):
                ex._cp(a, k, ex.me_i).start()
        token[...] = jnp.zeros_like(token)

    lands = [lax.empty(v.shape, v.dtype) for v in blocks]
    hbm = tuple(pltpu.HBM(v.shape, v.dtype) for v in list(blocks) + lands)
    n_sem = (N_DEV - 1) * n
    out = pl.pallas_call(
        body, name=name,
        out_shape=(pltpu.SemaphoreType.DMA((n_sem,)), pltpu.SemaphoreType.DMA((n_sem,))) + hbm
        + (jax.ShapeDtypeStruct((8, LANES), F32),),
        in_specs=(HBM_SPEC,) * (2 * n), out_specs=(SEM_SPEC, SEM_SPEC) + (HBM_SPEC,) * (2 * n) + (VMEM_SPEC,),
        input_output_aliases={i: i + 2 for i in range(2 * n)},
        compiler_params=pltpu.CompilerParams(has_side_effects=SPLIT_EFFECT),
    )(*[_in_hbm(v) for v in list(blocks) + lands])
    return out[0], out[1], list(out[2:2 + n]), list(out[2 + n:2 + 2 * n]), out[-1]


def _exchange_wait(ex, after, name, me):
    send, recv, thru, lands, _ = ex
    n = len(thru)

    def body(*refs):
        x_refs, land_refs = refs[:n], refs[n:2 * n]
        send_ref, recv_ref = refs[2 * n:2 * n + 2]
        e = _Exchange(x_refs, land_refs, send_ref, recv_ref, None)
        for a in range(n):
            for k in range(N_DEV - 1):
                e._cp(a, k, e.me_i).wait_send()
                e._cp(a, k, e.peers[k][1]).wait_recv()

    hbm = tuple(pltpu.HBM(v.shape, v.dtype) for v in list(thru) + list(lands))
    out = pl.pallas_call(
        body, name=name, out_shape=hbm,
        in_specs=(HBM_SPEC,) * (2 * n) + (SEM_SPEC, SEM_SPEC, ANY_SPEC), out_specs=(HBM_SPEC,) * (2 * n),
        input_output_aliases={i: i for i in range(2 * n)},
        compiler_params=pltpu.CompilerParams(has_side_effects=SPLIT_EFFECT),
    )(*thru, *lands, send, recv, after)
    done = []
    for own, land in zip(out[:n], out[n:]):
        mine = lax.dynamic_slice(own, (me,) + (0,) * (own.ndim - 1), (1,) + own.shape[1:])
        done.append(lax.dynamic_update_slice(land, mine, (me,) + (0,) * (own.ndim - 1)))
    return done


def _ada_fwd(c_all, w_ada, b_slice):
    def body(c_ref, w_ref, b_ref, o_ref):
        cv = c_ref[...]
        act = cv * _sigmoid(cv)
        o_ref[...] = _mm(act, w_ref[...]) + b_ref[...]

    nb, nc = c_all.shape[0], w_ada.shape[1]
    return pl.pallas_call(body, name="ada_fwd", out_shape=jax.ShapeDtypeStruct((nb, nc), F32),
                          compiler_params=_cparams())(c_all, w_ada, b_slice)


def _adam_math(w, g, m, v):
    m = B1 * m + (1.0 - B1) * g
    v = B2 * v + (1.0 - B2) * jnp.square(g)
    m_hat = m / (1.0 - B1 ** STEP)
    v_hat = v / (1.0 - B2 ** STEP)
    delta = -LR * (m_hat / (jnp.sqrt(v_hat) + AEPS) + WD * w)
    return delta, m, v


def _ada_bwd_adam(c_all, dmod_slice, w, m, v):
    rows, cols = w.shape
    br = 256

    def body(c_ref, d_ref, w_ref, m_ref, v_ref, g_out, dl_out, m_out, v_out):
        cv = c_ref[...]
        act = cv * _sigmoid(cv)
        g = _mm_tn(act, d_ref[...])
        g_out[...] = g
        dl, mn, vn = _adam_math(w_ref[...], g, m_ref[...], v_ref[...])
        dl_out[...] = dl
        m_out[...] = mn
        v_out[...] = vn

    nb = c_all.shape[0]
    wspec = pl.BlockSpec((br, cols), lambda i: (i, 0))
    return pl.pallas_call(
        body, name="ada_bwd_adam", grid=(rows // br,),
        in_specs=[pl.BlockSpec((nb, br), lambda i: (0, i)), pl.BlockSpec((nb, cols), lambda i: (0, 0)),
                  wspec, wspec, wspec],
        out_specs=[wspec] * 4, out_shape=[jax.ShapeDtypeStruct((rows, cols), F32)] * 4,
        compiler_params=_cparams(),
    )(c_all, dmod_slice, w, m, v)


def _mix_in(x2, mod, g_mix, win_g, in_cols, seq, shards):
    T = x2.shape[0]
    tm = min(512, seq)
    tps = seq // tm
    blk_rows = win_g.shape[1]
    pad_rows = -blk_rows % (4 * LANES)
    ns = len(shards)
    steps = T // tm

    def body(*refs):
        x_ref, mod_ref, g_ref, wb_ref = refs[:4]
        sh_refs = refs[4:4 + ns]
        u_ref, pm_ref, wc_ref = refs[4 + ns:7 + ns]
        ga_refs = refs[7 + ns:7 + 2 * ns]
        w_ref, send, recv, loc = refs[7 + 2 * ns:]
        step = pl.program_id(0)
        gather = _Gather(sh_refs, ga_refs, send, recv, loc)
        gather.begin_hosted(step, steps)

        @pl.when(step == 0)
        def _():
            w_ref[:, OFF_DT:] = jnp.zeros((D, PROJ_W - OFF_DT), BF16)
            for j in range(N_DEV):
                blk = jnp.concatenate([wb_ref[j], jnp.zeros((pad_rows, D), BF16)], axis=0)
                w_ref[:, in_cols * j:in_cols * (j + 1)] = blk.T[:, :in_cols]
            wc_ref[...] = w_ref[...]

        x = x_ref[...]
        r = lax.rsqrt(jnp.mean(x * x, axis=-1, keepdims=True) + EPS)
        md = mod_ref[0]
        u = (x * r * g_ref[...]) * (1.0 + md[1:2]) + md[0:1]
        ub = u.astype(BF16)
        u_ref[...] = ub
        pm_ref[...] = jnp.dot(ub, w_ref[...], preferred_element_type=F32)
        gather.end_hosted(step, steps)

    whole = pl.BlockSpec(memory_space=pltpu.VMEM)
    return pl.pallas_call(
        body, name="mix_in", grid=(T // tm,),
        in_specs=[pl.BlockSpec((tm, D), lambda i: (i, 0)), pl.BlockSpec((1, 8, D), lambda i: (i // tps, 0, 0)),
                  pl.BlockSpec((1, D), lambda i: (0, 0)), whole] + [ANY_SPEC] * ns,
        out_specs=[pl.BlockSpec((tm, D), lambda i: (i, 0)), pl.BlockSpec((tm, PROJ_W), lambda i: (i, 0)),
                   pl.BlockSpec((D, PROJ_W), lambda i: (0, 0))] + [ANY_SPEC] * ns,
        out_shape=[jax.ShapeDtypeStruct((T, D), BF16), jax.ShapeDtypeStruct((T, PROJ_W), F32),
                   jax.ShapeDtypeStruct((D, PROJ_W), BF16)]
        + [jax.ShapeDtypeStruct((N_DEV,) + v.shape, v.dtype) for v in shards],
        scratch_shapes=[pltpu.VMEM((D, PROJ_W), BF16)] + _gather_scratch(shards),
        compiler_params=_cparams(),
    )(x2, mod, g_mix, win_g, *shards)


def _chunk_forward(up, z, ux, dtin, halo_p, halo_x, hprev, cw, cb, hp, gssd, wpool, pscale, t0, y_scr, cv=None):
    L = CHUNK
    out = {}
    row = lax.broadcasted_iota(jnp.int32, (L, 1), 0)
    t = (t0 + row + 1).astype(F32)
    e = jnp.concatenate([halo_p, up], axis=0)
    s2 = e + pltpu.roll(e, 1, 0)
    s4 = s2 + pltpu.roll(s2, 2, 0)
    s8 = s4 + pltpu.roll(s4, 4, 0)
    s16 = s8 + pltpu.roll(s8, 8, 0)
    sums = (s2, s4, s8, s16)
    p, inv, yp = [], [], []
    for gi, w in enumerate(WINDOWS):
        sl = slice(gi * LANES, (gi + 1) * LANES)
        ic = 1.0 / jnp.minimum(t, float(w))
        pg = sums[gi][POOL_HALO:, sl] * ic - up[:, sl]
        p.append(pg)
        inv.append(ic)
        yp.append(_mm(pg, wpool[gi]))
    out["p"], out["inv"], out["yp"] = p, inv, yp
    out["y_pool"] = jnp.concatenate(yp, axis=1) * pscale
    if cv is None:
        ex = jnp.concatenate([halo_x, ux], axis=0)
        taps = [pltpu.roll(ex, 3, 0)[CONV_HALO:], pltpu.roll(ex, 2, 0)[CONV_HALO:], pltpu.roll(ex, 1, 0)[CONV_HALO:], ux]
        cv = cb + taps[0] * cw[0:1] + taps[1] * cw[1:2] + taps[2] * cw[2:3] + taps[3] * cw[3:4]
    sg = _sigmoid(cv)
    xbc = cv * sg
    out["cv"], out["sg"] = cv, sg
    X = xbc[:, :D]
    Bm = xbc[:, D:D + N_GROUPS * N_STATE]
    Cm = xbc[:, D + N_GROUPS * N_STATE:]
    pre = dtin + hp[0:1]
    dt = jnp.maximum(pre, 0.0) + jnp.log(1.0 + jnp.exp(-jnp.abs(pre)))
    a_row = -jnp.exp(hp[1:2])
    da = dt * a_row
    ri = lax.broadcasted_iota(jnp.int32, (L, L), 0)
    ci = lax.broadcasted_iota(jnp.int32, (L, L), 1)
    causal = ri >= ci
    cum = _dot01(causal.astype(F32), da, 3, split_lhs=False)
    cum_t = cum.T
    cum_last = cum[L - 1:L]
    eo = jnp.exp(cum)
    dec = jnp.exp(cum_last - cum)
    cd = jnp.exp(cum_last)
    exm = _expand_mat()
    rows8 = jnp.concatenate([cd, hp[2:3], jnp.zeros((6, LANES), F32)], axis=0)
    rep = _dot01(jnp.concatenate([dt, eo, dec, rows8], axis=0), exm, 2)
    dt_rep, eo_rep, dec_rep = rep[0:L], rep[L:2 * L], rep[2 * L:3 * L]
    cd_rep, dskip_rep = rep[3 * L:3 * L + 1], rep[3 * L + 1:3 * L + 2]
    xdt = X * dt_rep
    out.update(X=X, Bm=Bm, Cm=Cm, pre=pre, dt=dt, a_row=a_row, cum=cum, cum_t=cum_t, eo=eo, dec=dec, cd=cd,
               dt_rep=dt_rep, eo_rep=eo_rep, dec_rep=dec_rep, cd_rep=cd_rep, dskip_rep=dskip_rep, xdt=xdt,
               causal=causal, anti=(ri <= ci).astype(F32), exm=exm)
    G, lms, yoff, hnew, xdec = [], [], [], [], []
    for g in range(N_GROUPS):
        gs = slice(g * GROUP_W, (g + 1) * GROUP_W)
        Bg = Bm[:, g * N_STATE:(g + 1) * N_STATE]
        Cg = Cm[:, g * N_STATE:(g + 1) * N_STATE]
        Gg = _mm_nt(Cg, Bg)
        G.append(Gg)
        for hh in range(N_HEADS // N_GROUPS):
            h = g * (N_HEADS // N_GROUPS) + hh
            seg = cum[:, h:h + 1] - cum_t[h:h + 1, :]
            lm = jnp.where(causal, jnp.exp(jnp.minimum(seg, 0.0)), 0.0)
            lms.append(lm)
            hs = slice(h * HEAD_DIM, (h + 1) * HEAD_DIM)
            y_scr[:, hs] = _mm(Gg * lm, xdt[:, hs])
        xd = xdt[:, gs] * dec_rep[:, gs]
        xdec.append(xd)
        sgm = _mm_tn(Bg, xd)
        yoff.append(_mm(Cg, hprev[g]) * eo_rep[:, gs])
        hnew.append(hprev[g] * cd_rep[:, gs] + sgm)
    out.update(G=G, lms=lms, yoff=yoff, hnew=hnew, xdec=xdec)
    y = y_scr[...] + jnp.concatenate(yoff, axis=1) + dskip_rep * X
    sz = _sigmoid(z)
    silz = z * sz
    yz = y * silz
    rg, yn = [], []
    for g in range(N_GROUPS):
        gs = slice(g * GROUP_W, (g + 1) * GROUP_W)
        r = lax.rsqrt(jnp.mean(yz[:, gs] * yz[:, gs], axis=-1, keepdims=True) + EPS)
        rg.append(r)
        yn.append(yz[:, gs] * r)
    yn = jnp.concatenate(yn, axis=1)
    out.update(y=y, sz=sz, silz=silz, rg=rg, yn=yn)
    out["y_ssd"] = yn * gssd
    return out


def _mixer_fwd(pm, cw, cb, hp, gssd, wpool, pscale, nb, seq, shards):
    nc = seq // CHUNK
    ns = len(shards)
    steps = nb * nc

    def body(*refs):
        pm_ref, cw_ref, cb_ref, hp_ref, gs_ref, wp_ref, ps_ref = refs[:7]
        sh_refs = refs[7:7 + ns]
        ym_ref, hs_ref, cv_ref = refs[7 + ns:10 + ns]
        ga_refs = refs[10 + ns:10 + 2 * ns]
        halo_p, halo_x, state, y_scr, send, recv, loc = refs[10 + 2 * ns:]
        c = pl.program_id(1)
        step = pl.program_id(0) * nc + c
        gather = _Gather(sh_refs, ga_refs, send, recv, loc)
        gather.begin_hosted(step, steps)

        @pl.when(c == 0)
        def _():
            halo_p[...] = jnp.zeros_like(halo_p)
            halo_x[...] = jnp.zeros_like(halo_x)
            state[...] = jnp.zeros_like(state)

        up = pm_ref[:, 0:POOL_W]
        z = pm_ref[:, OFF_Z:OFF_XBC]
        ux = pm_ref[:, OFF_XBC:OFF_DT]
        hprev = [state[0], state[1]]
        hs_ref[0, 0, 0] = hprev[0]
        hs_ref[0, 0, 1] = hprev[1]
        o = _chunk_forward(up, z, ux, pm_ref[:, OFF_DT:], halo_p[...], halo_x[...], hprev, cw_ref[...], cb_ref[...],
                           hp_ref[...], gs_ref[...], wp_ref[...], ps_ref[...], c * CHUNK, y_scr)
        ym_ref[:, 0:POOL_W] = o["y_pool"].astype(BF16)
        ym_ref[:, POOL_W:] = o["y_ssd"].astype(BF16)
        cv_ref[...] = o["cv"]
        state[0] = o["hnew"][0]
        state[1] = o["hnew"][1]
        halo_p[...] = up[CHUNK - POOL_HALO:]
        halo_x[...] = ux[CHUNK - CONV_HALO:]
        gather.end_hosted(step, steps)

    def full(shape):
        return pl.BlockSpec(shape, lambda b, c: (0,) * len(shape))

    T = nb * seq
    return pl.pallas_call(
        body, name="mixer_fwd", grid=(nb, nc),
        in_specs=[pl.BlockSpec((CHUNK, PROJ_W), lambda b, c: (b * nc + c, 0)),
                  full((4, CONV_CH)), full((1, CONV_CH)), full((8, LANES)), full((1, D)),
                  full((4, LANES, LANES)), full((1, POOL_W))] + [ANY_SPEC] * ns,
        out_specs=[pl.BlockSpec((CHUNK, MIX_W), lambda b, c: (b * nc + c, 0)),
                   pl.BlockSpec((1, 1, N_GROUPS, N_STATE, GROUP_W), lambda b, c: (b, c, 0, 0, 0)),
                   pl.BlockSpec((CHUNK, CONV_CH), lambda b, c: (b * nc + c, 0))] + [ANY_SPEC] * ns,
        out_shape=[jax.ShapeDtypeStruct((T, MIX_W), BF16),
                   jax.ShapeDtypeStruct((nb, nc, N_GROUPS, N_STATE, GROUP_W), F32),
                   jax.ShapeDtypeStruct((T, CONV_CH), F32)]
        + [jax.ShapeDtypeStruct((N_DEV,) + v.shape, v.dtype) for v in shards],
        scratch_shapes=[pltpu.VMEM((POOL_HALO, POOL_W), F32), pltpu.VMEM((CONV_HALO, CONV_CH), F32),
                        pltpu.VMEM((N_GROUPS, N_STATE, GROUP_W), F32), pltpu.VMEM((CHUNK, D), F32)] + _gather_scratch(shards),
        compiler_params=_cparams(),
    )(pm, cw, cb, hp, gssd, wpool, pscale, *shards)


def _mixer_bwd(pm, cvs, dym, hstates, cw, cb, hp, gssd, wpool, pscale, nb, seq, after=()):
    nc = seq // CHUNK
    hpg = N_HEADS // N_GROUPS
    na = len(after)

    def body(*refs):
        (pm_ref, hpool_ref, cv_ref, dy_ref, hs_ref, cw_ref, cb_ref, hp_ref, gs_ref, wp_ref, ps_ref) = refs[:11]
        dpm_ref, dconv_ref, dhp_ref, dvec_ref, dwp_ref = refs[11 + na:16 + na]
        nxt_q, nxt_cv, rstate, y_scr, dx_scr = refs[16 + na:]
        b = pl.program_id(0)
        ci = pl.program_id(1)
        c = nc - 1 - ci

        @pl.when((b == 0) & (ci == 0))
        def _():
            for r in (dconv_ref, dhp_ref, dvec_ref, dwp_ref):
                r[...] = jnp.zeros_like(r)

        @pl.when(ci == 0)
        def _():
            nxt_q[...] = jnp.zeros_like(nxt_q)
            nxt_cv[...] = jnp.zeros_like(nxt_cv)
            rstate[...] = jnp.zeros_like(rstate)

        first = (c > 0).astype(F32)
        up = pm_ref[:, 0:POOL_W]
        z = pm_ref[:, OFF_Z:OFF_XBC]
        ux = pm_ref[:, OFF_XBC:OFF_DT]
        halo_p = hpool_ref[...] * first
        hprev = [hs_ref[0, 0, 0], hs_ref[0, 0, 1]]
        cw, cb, hp, gssd, wpool, pscale = cw_ref[...], cb_ref[...], hp_ref[...], gs_ref[...], wp_ref[...], ps_ref[...]
        o = _chunk_forward(up, z, ux, pm_ref[:, OFF_DT:], halo_p, None, hprev, cw, cb, hp, gssd, wpool, pscale,
                           c * CHUNK, y_scr, cv=cv_ref[...])
        L = CHUNK
        dy_pool = dy_ref[:, 0:POOL_W].astype(F32)
        dy_ssd = dy_ref[:, POOL_W:].astype(F32)

        dvec_ref[1:2, 0:POOL_W] += jnp.sum(dy_pool * jnp.concatenate(o["yp"], axis=1), axis=0, keepdims=True)
        dyp = dy_pool * pscale
        qs = []
        dps = []
        for gi in range(len(WINDOWS)):
            sl = slice(gi * LANES, (gi + 1) * LANES)
            dwp_ref[gi] += _mm_tn(o["p"][gi], dyp[:, sl])
            dpg = _mm_nt(dyp[:, sl], wpool[gi])
            dps.append(dpg)
            qs.append(dpg * o["inv"][gi])
        q = jnp.concatenate(qs, axis=1)
        e = jnp.concatenate([q, nxt_q[...]], axis=0)
        n = L + POOL_HALO
        s2 = e + pltpu.roll(e, n - 1, 0)
        s4 = s2 + pltpu.roll(s2, n - 2, 0)
        s8 = s4 + pltpu.roll(s4, n - 4, 0)
        s16 = s8 + pltpu.roll(s8, n - 8, 0)
        sums = (s2, s4, s8, s16)
        for gi in range(len(WINDOWS)):
            sl = slice(gi * LANES, (gi + 1) * LANES)
            dpm_ref[:, sl] = (sums[gi][:L, sl] - dps[gi]).astype(BF16)
        nxt_q[...] = q[:POOL_HALO]

        yn, y, silz, sz = o["yn"], o["y"], o["silz"], o["sz"]
        dvec_ref[0:1] += jnp.sum(dy_ssd * yn, axis=0, keepdims=True)
        dyn = dy_ssd * gssd
        dyz = []
        for g in range(N_GROUPS):
            gs = slice(g * GROUP_W, (g + 1) * GROUP_W)
            mean = jnp.mean(dyn[:, gs] * yn[:, gs], axis=-1, keepdims=True)
            dyz.append(o["rg"][g] * (dyn[:, gs] - yn[:, gs] * mean))
        dyz = jnp.concatenate(dyz, axis=1)
        dyv = dyz * silz
        dpm_ref[:, OFF_Z:OFF_XBC] = (dyz * y * (sz * (1.0 + z * (1.0 - sz)))).astype(BF16)

        X, Bm, Cm, xdt = o["X"], o["Bm"], o["Cm"], o["xdt"]
        exm = o["exm"]
        rdm = _reduce_mat()
        lane = lax.broadcasted_iota(jnp.int32, (1, LANES), 1)
        sub = lax.broadcasted_iota(jnp.int32, (LANES, 1), 0)
        dX = o["dskip_rep"] * dyv
        yoff_full = jnp.concatenate(o["yoff"], axis=1)
        rs = jnp.zeros((L, LANES), F32)
        cs_t = jnp.zeros((LANES, L), F32)
        dBs, dCs = [], []
        rh_sums = []
        ddec = []
        for g in range(N_GROUPS):
            gs = slice(g * GROUP_W, (g + 1) * GROUP_W)
            Bg = Bm[:, g * N_STATE:(g + 1) * N_STATE]
            Cg = Cm[:, g * N_STATE:(g + 1) * N_STATE]
            Gg = o["G"][g]
            R = rstate[g]
            dwm = dyv[:, gs] * o["eo_rep"][:, gs]
            dC = _mm_nt(dwm, hprev[g])
            dH = _mm_tn(Cg, dwm)
            dG = jnp.zeros((L, L), F32)
            for hh in range(hpg):
                h = g * hpg + hh
                hs = slice(h * HEAD_DIM, (h + 1) * HEAD_DIM)
                lm = o["lms"][h]
                m_h = Gg * lm
                dM = _mm_nt(dyv[:, hs], xdt[:, hs])
                dx_scr[:, hs] = _mm_tn(m_h, dyv[:, hs])
                qm = dM * m_h
                rs = rs + jnp.sum(qm, axis=1, keepdims=True) * (lane == h).astype(F32)
                cs_t = cs_t + (sub == h).astype(F32) * jnp.sum(qm, axis=0, keepdims=True)
                dG = dG + dM * lm
            dC = dC + _mm(dG, Bg)
            dB = _mm_tn(dG, Cg)
            zx = _mm(Bg, R)
            dxdt_state = zx * o["dec_rep"][:, gs]
            ddec.append(zx * xdt[:, gs])
            dB = dB + _mm_nt(o["xdec"][g], R)
            rh_sums.append(jnp.sum(R * hprev[g], axis=0, keepdims=True))
            rstate[g] = dH + o["cd_rep"][:, gs] * R
            dx_scr[:, gs] = dx_scr[:, gs] + dxdt_state
            dBs.append(dB)
            dCs.append(dC)
        dxdt = dx_scr[...]
        tail = jnp.concatenate([jnp.sum(dyv * X, axis=0, keepdims=True), jnp.concatenate(rh_sums, axis=1),
                                jnp.zeros((6, D), F32)], axis=0)
        red = _dot01(jnp.concatenate([dyv * yoff_full, jnp.concatenate(ddec, axis=1), dxdt * X, tail], axis=0), rdm, 2)
        d_dskip, dcd_row = red[3 * L:3 * L + 1], red[3 * L + 1:3 * L + 2]
        ddec_h = red[L:2 * L] * o["dec"]
        dcum_last = jnp.sum(ddec_h, axis=0, keepdims=True) + dcd_row * o["cd"]
        dcum = red[0:L] + rs - cs_t.T - ddec_h + (sub == L - 1).astype(F32) * dcum_last
        dda = _dot01(o["anti"], dcum, 3, split_lhs=False)
        ddt_v = dda * o["a_row"] + red[2 * L:3 * L]
        dX = dX + dxdt * o["dt_rep"]
        head_mask = (lane < N_HEADS).astype(F32)
        d_alog = jnp.sum(dda * o["dt"], axis=0, keepdims=True) * o["a_row"] * head_mask
        dpre = ddt_v * _sigmoid(o["pre"]) * head_mask
        dpm_ref[:, OFF_DT:] = dpre.astype(BF16)
        d_dtb = jnp.sum(dpre, axis=0, keepdims=True)
        dhp_ref[...] += jnp.concatenate([d_dtb, d_alog, d_dskip * head_mask, jnp.zeros((5, LANES), F32)], axis=0)

        dxbc = jnp.concatenate([dX] + dBs + dCs, axis=1)
        sg, cv = o["sg"], o["cv"]
        dcv = dxbc * (sg * (1.0 + cv * (1.0 - sg)))
        e2 = jnp.concatenate([dcv, nxt_cv[...]], axis=0)
        n2 = L + CONV_HALO
        ahead = [dcv, pltpu.roll(e2, n2 - 1, 0)[:L], pltpu.roll(e2, n2 - 2, 0)[:L], pltpu.roll(e2, n2 - 3, 0)[:L]]
        dconv_ref[0:5] += jnp.concatenate(
            [jnp.sum(ux * ahead[3 - k], axis=0, keepdims=True) for k in range(4)]
            + [jnp.sum(dcv, axis=0, keepdims=True)], axis=0)
        dux = ahead[0] * cw[3:4] + ahead[1] * cw[2:3] + ahead[2] * cw[1:2] + ahead[3] * cw[0:1]
        dpm_ref[:, OFF_XBC:OFF_DT] = dux.astype(BF16)
        nxt_cv[...] = dcv[:CONV_HALO]

    def full(shape):
        return pl.BlockSpec(shape, lambda b, c: (0,) * len(shape))

    def rowblk(b, c):
        return b * nc + (nc - 1 - c)

    hp_blocks = CHUNK // POOL_HALO
    T = nb * seq
    return pl.pallas_call(
        body, name="mixer_bwd", grid=(nb, nc),
        in_specs=[pl.BlockSpec((CHUNK, PROJ_W), lambda b, c: (rowblk(b, c), 0)),
                  pl.BlockSpec((POOL_HALO, POOL_W), lambda b, c: (jnp.maximum(rowblk(b, c) * hp_blocks - 1, 0), 0)),
                  pl.BlockSpec((CHUNK, CONV_CH), lambda b, c: (rowblk(b, c), 0)),
                  pl.BlockSpec((CHUNK, MIX_W), lambda b, c: (rowblk(b, c), 0)),
                  pl.BlockSpec((1, 1, N_GROUPS, N_STATE, GROUP_W), lambda b, c: (b, nc - 1 - c, 0, 0, 0)),
                  full((4, CONV_CH)), full((1, CONV_CH)), full((8, LANES)), full((1, D)),
                  full((4, LANES, LANES)), full((1, POOL_W))] + [ANY_SPEC] * na,
        out_specs=[pl.BlockSpec((CHUNK, PROJ_W), lambda b, c: (rowblk(b, c), 0)),
                   full((8, CONV_CH)), full((8, LANES)), full((8, D)), full((4, LANES, LANES))],
        out_shape=[jax.ShapeDtypeStruct((T, PROJ_W), BF16),
                   jax.ShapeDtypeStruct((8, CONV_CH), F32), jax.ShapeDtypeStruct((8, LANES), F32),
                   jax.ShapeDtypeStruct((8, D), F32), jax.ShapeDtypeStruct((4, LANES, LANES), F32)],
        scratch_shapes=[pltpu.VMEM((POOL_HALO, POOL_W), F32), pltpu.VMEM((CONV_HALO, CONV_CH), F32),
                        pltpu.VMEM((N_GROUPS, N_STATE, GROUP_W), F32), pltpu.VMEM((CHUNK, D), F32),
                        pltpu.VMEM((CHUNK, D), F32)],
        compiler_params=_cparams(),
    )(pm, pm, cvs, dym, hstates, cw, cb, hp, gssd, wpool, pscale, *after)


def _mlp_fused(x2, ymix, target, mod, g_mlp, g_final, w_out, w_up, w_down, seq):
    T = x2.shape[0]
    tm = min(256, seq)
    tps = seq // tm
    nblk = D_FF // FF_BLK

    def body(x_ref, ym_ref, tg_ref, mod_ref, gm_ref, gf_ref, wo_ref, wu_ref, wd_ref,
             da_ref, dym_ref, dh1_ref, u2_ref, f_ref, dup_ref, ddn_ref, dmod_ref, acc_ref, relu_scr):
        i = pl.program_id(0)

        @pl.when(i == 0)
        def _():
            acc_ref[...] = jnp.zeros_like(acc_ref)

        @pl.when(i % tps == 0)
        def _():
            dmod_ref[...] = jnp.zeros_like(dmod_ref)

        md = mod_ref[0]
        gate_m, shift_f, scale_f, gate_f = md[2:3], md[3:4], md[4:5], md[5:6]
        g_mlp, g_fin = gm_ref[...], gf_ref[...]
        a = jnp.dot(ym_ref[...], wo_ref[...], preferred_element_type=F32)
        h1 = x_ref[...] + gate_m * a
        r2 = lax.rsqrt(jnp.mean(h1 * h1, axis=-1, keepdims=True) + EPS)
        n2 = h1 * r2
        u2 = (n2 * g_mlp) * (1.0 + scale_f) + shift_f
        u2b = u2.astype(BF16)
        u2_ref[...] = u2b
        dn = jnp.zeros((tm, D), F32)
        for j in range(nblk):
            js = slice(j * FF_BLK, (j + 1) * FF_BLK)
            upj = jnp.maximum(jnp.dot(u2b, wu_ref[j], preferred_element_type=F32), 0.0)
            relu_scr[:, js] = upj
            fj = (upj * upj).astype(BF16)
            f_ref[:, js] = fj
            dn = dn + jnp.dot(fj, wd_ref[j], preferred_element_type=F32)
        h2 = h1 + gate_f * dn
        r3 = lax.rsqrt(jnp.mean(h2 * h2, axis=-1, keepdims=True) + EPS)
        n3 = h2 * r3
        err = n3 * g_fin - tg_ref[...]
        loss = 0.5 * jnp.sum(jnp.mean(err * err, axis=-1, keepdims=True), axis=0, keepdims=True)
        dout = err * (1.0 / D)
        d_gfin = jnp.sum(dout * n3, axis=0, keepdims=True)
        dn3 = dout * g_fin
        dh2 = r3 * (dn3 - n3 * jnp.mean(dn3 * n3, axis=-1, keepdims=True))
        d_gate_f = jnp.sum(dh2 * dn, axis=0, keepdims=True)
        ddn = (gate_f * dh2).astype(BF16)
        ddn_ref[...] = ddn
        du2 = jnp.zeros((tm, D), F32)
        for j in range(nblk):
            js = slice(j * FF_BLK, (j + 1) * FF_BLK)
            dfj = lax.dot_general(ddn, wd_ref[j], (((1,), (1,)), ((), ())), preferred_element_type=F32)
            dupj = (dfj * (2.0 * relu_scr[:, js])).astype(BF16)
            dup_ref[:, js] = dupj
            du2 = du2 + lax.dot_general(dupj, wu_ref[j], (((1,), (1,)), ((), ())), preferred_element_type=F32)
        d_scale_f = jnp.sum(du2 * (n2 * g_mlp), axis=0, keepdims=True)
        d_shift_f = jnp.sum(du2, axis=0, keepdims=True)
        d_gmlp = jnp.sum(du2 * (1.0 + scale_f) * n2, axis=0, keepdims=True)
        dn2 = du2 * (g_mlp * (1.0 + scale_f))
        dh1 = dh2 + r2 * (dn2 - n2 * jnp.mean(dn2 * n2, axis=-1, keepdims=True))
        dh1_ref[...] = dh1
        d_gate_m = jnp.sum(dh1 * a, axis=0, keepdims=True)
        da = (gate_m * dh1).astype(BF16)
        da_ref[...] = da
        dym_ref[...] = lax.dot_general(da, wo_ref[...], (((1,), (1,)), ((), ())),
                                       preferred_element_type=F32).astype(BF16)
        dmod_ref[0] += jnp.concatenate([jnp.zeros((2, D), F32), d_gate_m, d_shift_f, d_scale_f, d_gate_f,
                                        jnp.zeros((2, D), F32)], axis=0)
        acc_ref[...] += jnp.concatenate([d_gmlp, d_gfin, loss * jnp.ones((1, D), F32), jnp.zeros((5, D), F32)], axis=0)

    whole = pl.BlockSpec(memory_space=pltpu.VMEM)

    def tok(w):
        return pl.BlockSpec((tm, w), lambda i: (i, 0))

    def vec():
        return pl.BlockSpec((1, D), lambda i: (0, 0))

    nb = T // seq
    return pl.pallas_call(
        body, name="mlp_fused", grid=(T // tm,),
        in_specs=[tok(D), tok(MIX_W), tok(D), pl.BlockSpec((1, 8, D), lambda i: (i // tps, 0, 0)), vec(), vec(),
                  whole, whole, whole],
        out_specs=[tok(D), tok(MIX_W), tok(D), tok(D), tok(D_FF), tok(D_FF), tok(D),
                   pl.BlockSpec((1, 8, D), lambda i: (i // tps, 0, 0)), pl.BlockSpec((8, D), lambda i: (0, 0))],
        out_shape=[jax.ShapeDtypeStruct((T, D), BF16), jax.ShapeDtypeStruct((T, MIX_W), BF16),
                   jax.ShapeDtypeStruct((T, D), F32), jax.ShapeDtypeStruct((T, D), BF16),
                   jax.ShapeDtypeStruct((T, D_FF), BF16), jax.ShapeDtypeStruct((T, D_FF), BF16),
                   jax.ShapeDtypeStruct((T, D), BF16), jax.ShapeDtypeStruct((nb, 8, D), F32),
                   jax.ShapeDtypeStruct((8, D), F32)],
        scratch_shapes=[pltpu.VMEM((tm, D_FF), F32)],
        compiler_params=_cparams(),
    )(x2, ymix, target, mod, g_mlp, g_final, w_out, w_up, w_down)


def _in_bwd(x2, dh1, dpb, mod, g_mix, w_cat, dmod_a, acc_a, seq):
    T = x2.shape[0]
    tm = min(1024, seq)
    tps = seq // tm
    steps = T // tm

    def body(x_ref, dh_ref, dpb_ref, mod_ref, g_ref, w_ref, dma_ref, acca_ref, dx_ref, dmod_ref, acc_ref):
        i = pl.program_id(0)

        @pl.when(i == 0)
        def _():
            acc_ref[...] = acca_ref[...]

        @pl.when(i % tps == 0)
        def _():
            dmod_ref[...] = dma_ref[...]

        du = lax.dot_general(dpb_ref[...], w_ref[...], (((1,), (1,)), ((), ())), preferred_element_type=F32)
        x = x_ref[...]
        md = mod_ref[0]
        g = g_ref[...]
        r = lax.rsqrt(jnp.mean(x * x, axis=-1, keepdims=True) + EPS)
        n1 = x * r
        d_scale = jnp.sum(du * (n1 * g), axis=0, keepdims=True)
        d_shift = jnp.sum(du, axis=0, keepdims=True)
        d_g = jnp.sum(du * (1.0 + md[1:2]) * n1, axis=0, keepdims=True)
        dn1 = du * (g * (1.0 + md[1:2]))
        dx_ref[...] = dh_ref[...] + r * (dn1 - n1 * jnp.mean(dn1 * n1, axis=-1, keepdims=True))
        dmod_ref[0] += jnp.concatenate([d_shift, d_scale, jnp.zeros((6, D), F32)], axis=0)
        acc_ref[...] += jnp.concatenate([jnp.zeros((3, D), F32), d_g, jnp.zeros((4, D), F32)], axis=0)

    whole = pl.BlockSpec(memory_space=pltpu.VMEM)
    nb = T // seq
    return pl.pallas_call(
        body, name="in_bwd", grid=(steps,),
        in_specs=[pl.BlockSpec((tm, D), lambda i: (i, 0)), pl.BlockSpec((tm, D), lambda i: (i, 0)),
                  pl.BlockSpec((tm, PROJ_W), lambda i: (i, 0)),
                  pl.BlockSpec((1, 8, D), lambda i: (i // tps, 0, 0)), pl.BlockSpec((1, D), lambda i: (0, 0)),
                  whole, pl.BlockSpec((1, 8, D), lambda i: (i // tps, 0, 0)), pl.BlockSpec((8, D), lambda i: (0, 0))],
        out_specs=[pl.BlockSpec((tm, D), lambda i: (i, 0)),
                   pl.BlockSpec((1, 8, D), lambda i: (i // tps, 0, 0)), pl.BlockSpec((8, D), lambda i: (0, 0))],
        out_shape=[jax.ShapeDtypeStruct((T, D), F32),
                   jax.ShapeDtypeStruct((nb, 8, D), F32), jax.ShapeDtypeStruct((8, D), F32)],
        compiler_params=_cparams(),
    )(x2, dh1, dpb, mod, g_mix, w_cat, dmod_a, acc_a)


def _dw_in(u_b, dpb, in_cols, shards):
    T = u_b.shape[0]
    bk = min(1024, T)
    nk = T // bk
    ns = len(shards)

    def body(*refs):
        u_ref, d_ref = refs[:2]
        sh_refs = refs[2:2 + ns]
        o_ref = refs[2 + ns]
        ga_refs = refs[3 + ns:3 + 2 * ns]
        acc, send, recv, loc = refs[3 + 2 * ns:]
        k = pl.program_id(0)
        gather = _Gather(sh_refs, ga_refs, send, recv, loc)
        gather.begin_hosted(k, nk)

        @pl.when(k == 0)
        def _():
            acc[...] = jnp.zeros_like(acc)

        acc[...] += jnp.dot(u_ref[...].T, d_ref[...], preferred_element_type=F32)

        @pl.when(k == nk - 1)
        def _():
            for j in range(N_DEV):
                o_ref[j] = acc[:, in_cols * j:in_cols * (j + 1)].astype(BF16)

        gather.end_hosted(k, nk)

    return pl.pallas_call(
        body, name="dw_in", grid=(nk,),
        in_specs=[pl.BlockSpec((bk, D), lambda k: (k, 0)), pl.BlockSpec((bk, PROJ_W), lambda k: (k, 0))]
        + [ANY_SPEC] * ns,
        out_specs=[pl.BlockSpec((N_DEV, D, in_cols), lambda k: (0, 0, 0))] + [ANY_SPEC] * ns,
        out_shape=[jax.ShapeDtypeStruct((N_DEV, D, in_cols), BF16)]
        + [jax.ShapeDtypeStruct((N_DEV,) + v.shape, v.dtype) for v in shards],
        scratch_shapes=[pltpu.VMEM((D, PROJ_W), F32)] + _gather_scratch(shards),
        compiler_params=_cparams(),
    )(u_b, dpb, *shards)


def _dw_blocks(a, b, name, by_rows, per_step=1, after=()):
    T, M = a.shape
    N = b.shape[1]
    bk = min(4096, T)
    nk = T // bk
    whole = pl.BlockSpec(memory_space=pltpu.VMEM)
    if by_rows:
        rows = M // N_DEV
        am = rows * per_step
        nblk = N_DEV // per_step
        a_spec, b_spec = pl.BlockSpec((bk, am), lambda i, k: (k, i)), whole
        out_blk, acc_shape = (per_step, rows, N), (am, N)
    else:
        cols = N // N_DEV
        nblk = N_DEV
        a_spec, b_spec = whole, pl.BlockSpec((bk, cols), lambda i, k: (k, i))
        out_blk, acc_shape = (1, M, cols), (M, cols)

    def body(a_ref, b_ref, *rest):
        o_ref, acc = rest[len(after):]
        k = pl.program_id(1)

        @pl.when(k == 0)
        def _():
            acc[...] = jnp.zeros_like(acc)

        tok = pl.ds(pl.multiple_of(k * bk, bk), bk)
        a_blk = a_ref[...] if by_rows else a_ref[tok, :]
        b_blk = b_ref[tok, :] if by_rows else b_ref[...]
        acc[...] += lax.dot_general(a_blk, b_blk, (((0,), (0,)), ((), ())), preferred_element_type=F32)

        @pl.when(k == nk - 1)
        def _():
            o_ref[...] = acc[...].reshape(out_blk).astype(BF16)

    return pl.pallas_call(
        body, name=name, grid=(nblk, nk), in_specs=[a_spec, b_spec] + [ANY_SPEC] * len(after),
        out_specs=pl.BlockSpec(out_blk, lambda i, k: (i, 0, 0)),
        out_shape=jax.ShapeDtypeStruct((N_DEV,) + out_blk[1:], BF16),
        scratch_shapes=[pltpu.VMEM(acc_shape, F32)],
        compiler_params=_cparams(),
    )(a, b, *after)


def _adam_parts(parts, w, m, v, name):
    rows, cols = w.shape
    br = rows
    for cand in range(rows, 15, -16):
        if rows % cand == 0 and cand * cols * 4 <= ADAM_BLOCK_BYTES:
            br = cand
            break

    def body(p_ref, w_ref, m_ref, v_ref, g_out, dl_out, m_out, v_out):
        g = p_ref[0].astype(F32)
        for k in range(1, N_DEV):
            g = g + p_ref[k].astype(F32)
        g_out[...] = g
        dl, mn, vn = _adam_math(w_ref[...], g, m_ref[...], v_ref[...])
        dl_out[...] = dl
        m_out[...] = mn
        v_out[...] = vn

    wspec = pl.BlockSpec((br, cols), lambda i: (i, 0))
    return pl.pallas_call(
        body, name=name, grid=(rows // br,),
        in_specs=[pl.BlockSpec((N_DEV, br, cols), lambda i: (0, i, 0)), wspec, wspec, wspec],
        out_specs=[wspec] * 4, out_shape=[jax.ShapeDtypeStruct((rows, cols), F32)] * 4,
        compiler_params=_cparams(),
    )(parts, w, m, v)


def _adam_plain(g, w, m, v, name):
    def body(g_ref, w_ref, m_ref, v_ref, dl_out, m_out, v_out):
        dl, mn, vn = _adam_math(w_ref[...], g_ref[...], m_ref[...], v_ref[...])
        dl_out[...] = dl
        m_out[...] = mn
        v_out[...] = vn

    return pl.pallas_call(body, name=name, out_shape=[jax.ShapeDtypeStruct(w.shape, F32)] * 3,
                          compiler_params=_cparams())(g, w, m, v)


SMALL_PARAMS = ("b_ada", "g_mix", "conv_b", "dt_bias", "a_log", "d_skip", "g_ssd", "pool_scale", "g_mlp", "g_final")


def _small_adam(gathered, params):
    n_par = len(SMALL_PARAMS)
    nb = gathered[0].shape[1]

    def body(*refs):
        dmod_ref, acc_ref, conv_ref, vec_ref, hd_ref = refs[:5]
        par_refs = refs[5:5 + 3 * n_par]
        out_refs = refs[5 + 3 * n_par:5 + 7 * n_par]
        cw_out, acc_out = refs[5 + 7 * n_par:]

        def total(ref):
            t = ref[0]
            for k in range(1, N_DEV):
                t = t + ref[k]
            return t

        dm = total(dmod_ref)
        dmb = dm[0]
        for b in range(1, nb):
            dmb = dmb + dm[b]
        ac, cv, vc, hd = total(acc_ref), total(conv_ref), total(vec_ref), total(hd_ref)
        cw_out[...] = cv[0:4]
        acc_out[...] = ac
        grads = {
            "b_ada": jnp.concatenate([dmb[r:r + 1] for r in range(6)], axis=1), "g_mix": ac[3:4], "conv_b": cv[4:5],
            "dt_bias": hd[0:1, 0:N_HEADS], "a_log": hd[1:2, 0:N_HEADS], "d_skip": hd[2:3, 0:N_HEADS],
            "g_ssd": vc[0:1], "pool_scale": vc[1:2, 0:POOL_W], "g_mlp": ac[0:1], "g_final": ac[1:2],
        }
        for i, name in enumerate(SMALL_PARAMS):
            w_ref, m_ref, v_ref = par_refs[3 * i:3 * i + 3]
            g = grads[name]
            dl, mn, vn = _adam_math(w_ref[...], g, m_ref[...], v_ref[...])
            g_o, d_o, m_o, v_o = out_refs[4 * i:4 * i + 4]
            g_o[...] = g
            d_o[...] = dl
            m_o[...] = mn
            v_o[...] = vn

    flat = [a for name in SMALL_PARAMS for a in params[name]]
    out_shape = [jax.ShapeDtypeStruct(params[name][0].shape, F32) for name in SMALL_PARAMS for _ in range(4)]
    out_shape += [jax.ShapeDtypeStruct((4, CONV_CH), F32), jax.ShapeDtypeStruct((8, D), F32)]
    return pl.pallas_call(body, name="small_adam", out_shape=out_shape, compiler_params=_cparams())(*gathered, *flat)


def kernel(x, c, w_ada, b_ada, g_mix, w_in, conv_w, conv_b, dt_bias, a_log, d_skip, g_ssd, w_pool, pool_scale, w_out, g_mlp, w_up, w_down, g_final, loss_target, m_w_ada, m_b_ada, m_g_mix, m_w_in, m_conv_w, m_conv_b, m_dt_bias, m_a_log, m_d_skip, m_g_ssd, m_w_pool, m_pool_scale, m_w_out, m_g_mlp, m_w_up, m_w_down, m_g_final, v_w_ada, v_b_ada, v_g_mix, v_w_in, v_conv_w, v_conv_b, v_dt_bias, v_a_log, v_d_skip, v_g_ssd, v_w_pool, v_pool_scale, v_w_out, v_g_mlp, v_w_up, v_w_down, v_g_final):
    nb, seq, _ = x.shape
    T = nb * seq
    me = 4 * lax.axis_index("x") + 2 * lax.axis_index("y") + lax.axis_index("c")
    in_cols = w_in.shape[2]
    ada_cols = w_ada.shape[2]
    cw_cols = conv_w.shape[2]

    win_t = jnp.pad(w_in[0].astype(BF16).T, ((0, -in_cols % 16), (0, 0)))
    c_g, cw_g, win_g = _all_gather([c, conv_w[0], win_t], "ag_first")
    c_all = c_g.reshape(N_DEV * nb, D)
    cw_full = cw_g.transpose(1, 0, 2).reshape(4, CONV_CH)

    b_slice = lax.dynamic_slice(b_ada, (0, me * ada_cols), (1, ada_cols))
    mod_cols = _ada_fwd(c_all, w_ada[0], b_slice)
    (mod_g,) = _all_gather_small([mod_cols], "ag_mod")
    mod_all = mod_g.transpose(1, 0, 2).reshape(N_DEV * nb, 6, D)
    mod_mine = lax.dynamic_slice(mod_all, (me * nb, 0, 0), (nb, 6, D))
    mod = jnp.pad(mod_mine, ((0, 0), (0, 2), (0, 0)))

    x2 = x.reshape(T, D)
    tg2 = loss_target.reshape(T, D)
    heads = jnp.pad(jnp.concatenate([dt_bias, a_log, d_skip], axis=0), ((0, 5), (0, LANES - N_HEADS)))
    wpool_b = w_pool[0]
    u_b, pm, w_cat, wup_g = _mix_in(x2, mod, g_mix, win_g, in_cols, seq, [w_up[0].astype(BF16)])
    ymix, hstates, cvs, wout_g, wdn_g = _mixer_fwd(
        pm, cw_full, conv_b, heads, g_ssd, wpool_b, pool_scale, nb, seq,
        [w_out[0].astype(BF16), w_down[0].astype(BF16)])
    da_b, dym, dh1, u2_b, f_b, dup_b, ddn_b, dmod_a, acc_a = _mlp_fused(
        x2, ymix, tg2, mod, g_mlp, g_final.reshape(1, D), wout_g.reshape(MIX_W, D), wup_g, wdn_g, seq)

    gout_p = _dw_blocks(ymix, da_b, "dw_out", True, per_step=4)
    gup_p = _dw_blocks(u2_b, dup_b, "dw_up", False)
    ex_a = _exchange_start([gout_p, gup_p], "ga_start")
    gdn_p = _dw_blocks(f_b, ddn_b, "dw_down", True, after=[ex_a[4]])
    ex_b = _exchange_start([gdn_p], "gb_start")
    dpb, d_conv, d_heads, d_vec, d_wpool = _mixer_bwd(
        pm, cvs, dym, hstates, cw_full, conv_b, heads, g_ssd, wpool_b, pool_scale, nb, seq, after=[ex_b[4]])
    gin_p, conv_g, vec_g, heads_g, wpool_parts = _dw_in(
        u_b, dpb, in_cols, [d_conv, d_vec, d_heads, d_wpool.reshape(4 * LANES, LANES)])
    ex_in = _exchange_start([gin_p], "gin_start")
    grad_x2, dmod, acc = _in_bwd(x2, dh1, dpb, mod, g_mix + ex_in[4][0:1, 0:1], w_cat, dmod_a, acc_a, seq)

    gout_r, gup_r = _exchange_wait(ex_a, dmod, "ga_wait", me)
    (gdn_r,) = _exchange_wait(ex_b, dmod, "gb_wait", me)
    g_out, d_out, nm_out, nv_out = _adam_parts(gout_r, w_out[0], m_w_out[0], v_w_out[0], "adam_w_out")
    g_up, d_up, nm_up, nv_up = _adam_parts(gup_r, w_up[0], m_w_up[0], v_w_up[0], "adam_w_up")
    g_dn, d_dn, nm_dn, nv_dn = _adam_parts(gdn_r, w_down[0], m_w_down[0], v_w_down[0], "adam_w_down")

    dmod_g, acc_g = _all_gather_small([dmod, acc], "ag_small_bwd", after=[nm_out, nm_up, nm_dn])
    pool2 = (4 * LANES, LANES)
    wpool_outs = _adam_parts(wpool_parts, w_pool.reshape(pool2), m_w_pool.reshape(pool2), v_w_pool.reshape(pool2),
                             "adam_w_pool")
    small_params = {
        "b_ada": (b_ada, m_b_ada, v_b_ada), "g_mix": (g_mix, m_g_mix, v_g_mix), "conv_b": (conv_b, m_conv_b, v_conv_b),
        "dt_bias": (dt_bias, m_dt_bias, v_dt_bias), "a_log": (a_log, m_a_log, v_a_log),
        "d_skip": (d_skip, m_d_skip, v_d_skip), "g_ssd": (g_ssd, m_g_ssd, v_g_ssd),
        "pool_scale": (pool_scale, m_pool_scale, v_pool_scale), "g_mlp": (g_mlp, m_g_mlp, v_g_mlp),
        "g_final": tuple(a.reshape(1, D) for a in (g_final, m_g_final, v_g_final)),
    }
    small_res = _small_adam([dmod_g, acc_g, conv_g, vec_g, heads_g], small_params)
    g_cw_full, acc_sum = small_res[-2:]
    loss = acc_sum[2, 0]

    g_cw = lax.dynamic_slice(g_cw_full, (0, me * cw_cols), (4, cw_cols))
    d_cwp, nm_cwp, nv_cwp = _adam_plain(g_cw, conv_w[0], m_conv_w[0], v_conv_w[0], "adam_conv_w")

    dmod_all = dmod_g[:, :, 0:6].reshape(N_DEV * nb, 6 * D)
    dmod_slice = lax.dynamic_slice(dmod_all, (0, me * ada_cols), (N_DEV * nb, ada_cols))
    g_ada, d_ada, nm_ada, nv_ada = _ada_bwd_adam(c_all, dmod_slice, w_ada[0], m_w_ada[0], v_w_ada[0])

    ex_after = nm_ada[0:8, 0:LANES] + acc_sum[:, 0:LANES]
    (gin_r,) = _exchange_wait(ex_in, ex_after, "gin_wait", me)
    g_in, d_in, nm_in, nv_in = _adam_parts(gin_r, w_in[0], m_w_in[0], v_w_in[0], "adam_w_in")

    def small_outs(kind, wpool):
        res = {name: small_res[4 * i + kind] for i, name in enumerate(SMALL_PARAMS)}
        res["g_final"] = res["g_final"].reshape(D)
        res["w_pool"] = wpool.reshape(1, 4, LANES, LANES)
        return res

    def big_outs(ada, win, cwp, wout, wup, wdn):
        return {"w_ada": ada[None], "w_in": win.reshape(1, D, in_cols), "conv_w": cwp[None], "w_out": wout[None],
                "w_up": wup[None], "w_down": wdn[None]}

    order = ["w_ada", "b_ada", "g_mix", "w_in", "conv_w", "conv_b", "dt_bias", "a_log", "d_skip", "g_ssd", "w_pool",
             "pool_scale", "w_out", "g_mlp", "w_up", "w_down", "g_final"]
    groups = [
        {**small_outs(0, wpool_outs[0]), **big_outs(g_ada, g_in, g_cw, g_out, g_up, g_dn)},
        {**small_outs(1, wpool_outs[1]), **big_outs(d_ada, d_in, d_cwp, d_out, d_up, d_dn)},
        {**small_outs(2, wpool_outs[2]), **big_outs(nm_ada, nm_in, nm_cwp, nm_out, nm_up, nm_dn)},
        {**small_outs(3, wpool_outs[3]), **big_outs(nv_ada, nv_in, nv_cwp, nv_out, nv_up, nv_dn)},
    ]
    outs = [loss, grad_x2.reshape(nb, seq, D)]
    for grp in groups:
        outs += [grp[n] for n in order]
    return tuple(outs)
```

```python
import functools

import jax
import jax.numpy as jnp
from jax import lax
from jax.experimental import pallas as pl
from jax.experimental.pallas import tpu as pltpu

F32, BF16 = jnp.float32, jnp.bfloat16
MESH = pl.DeviceIdType.MESH
N_DEV = 8
D = 1024
LANES = 128
CHUNK = 128
POOL_W = 512
WINDOWS = (2, 4, 8, 16)
N_HEADS = 16
HEAD_DIM = 64
N_GROUPS = 2
GROUP_W = 512
N_STATE = 128
CONV_CH = 1536
OFF_Z, OFF_XBC, OFF_DT, IN_W = 512, 1536, 3072, 3088
PROJ_W = OFF_DT + LANES
MIX_W = 1536
D_FF = 4096
FF_BLK = 512
EPS = 1e-5
LR, B1, B2, AEPS, WD, STEP = 0.001, 0.9, 0.999, 1e-08, 0.01, 10
POOL_HALO = 16
CONV_HALO = 8
VMEM_LIMIT = 56 << 20
ADAM_BLOCK_BYTES = 1 << 20


def _cparams(**kw):
    return pltpu.CompilerParams(vmem_limit_bytes=VMEM_LIMIT, **kw)


def _mm(a, b):
    return jnp.dot(a.astype(BF16), b.astype(BF16), preferred_element_type=F32)


def _mm_nt(a, b):
    return lax.dot_general(a.astype(BF16), b.astype(BF16), (((1,), (1,)), ((), ())), preferred_element_type=F32)


def _mm_tn(a, b):
    return lax.dot_general(a.astype(BF16), b.astype(BF16), (((0,), (0,)), ((), ())), preferred_element_type=F32)


def _split_bf16(v, terms):
    parts, rest = [], v
    for t in range(terms):
        p = rest.astype(BF16)
        parts.append(p)
        if t + 1 < terms:
            rest = rest - p.astype(F32)
    return parts


def _dot01(a, b, terms, split_lhs=True):
    if split_lhs:
        bb = b.astype(BF16)
        prods = [jnp.dot(p, bb, preferred_element_type=F32) for p in _split_bf16(a, terms)]
    else:
        ab = a.astype(BF16)
        prods = [jnp.dot(ab, p, preferred_element_type=F32) for p in _split_bf16(b, terms)]
    out = prods[0]
    for q in prods[1:]:
        out = out + q
    return out


def _sigmoid(v):
    return 1.0 / (1.0 + jnp.exp(-v))


def _expand_mat():
    r = lax.broadcasted_iota(jnp.int32, (LANES, D), 0)
    c = lax.broadcasted_iota(jnp.int32, (LANES, D), 1)
    return (r == c // HEAD_DIM).astype(F32)


def _reduce_mat():
    r = lax.broadcasted_iota(jnp.int32, (D, LANES), 0)
    c = lax.broadcasted_iota(jnp.int32, (D, LANES), 1)
    return (c == r // HEAD_DIM).astype(F32)


def _pos():
    return lax.axis_index("x"), lax.axis_index("y"), lax.axis_index("c")


GATHER_PIECES = 4
GATHER_PIECE_BYTES = 96 << 10


def _pieces(shape, dtype):
    rows = shape[0]
    size = jnp.dtype(dtype).itemsize
    for d in shape:
        size *= d
    whole_tiles = rows % (GATHER_PIECES * 16) == 0
    return GATHER_PIECES if whole_tiles and size // GATHER_PIECES >= GATHER_PIECE_BYTES else 1


class _Gather:
    def __init__(self, x_refs, o_refs, send, recv, loc):
        self.x_refs, self.o_refs, self.send, self.recv, self.loc = x_refs, o_refs, send, recv, loc
        self.n = len(x_refs)
        self.pieces = [_pieces(r.shape, r.dtype) for r in x_refs]
        self.base = [7 * sum(self.pieces[:a]) for a in range(self.n)]
        x, y, c = _pos()
        self.c = c
        self.me, self.sib = (x, y, c), (x, y, 1 - c)
        self.chips = [(1 - x, y), (x, 1 - y), (1 - x, 1 - y)]

    def _rows(self, a, p):
        rows = self.x_refs[a].shape[0] // self.pieces[a]
        return pl.ds(p * rows, rows)

    def _cp(self, a, p, k, block, to, own=False):
        dst = self.o_refs[a].at[4 * block[0] + 2 * block[1] + block[2], self._rows(a, p)]
        sem = self.base[a] + 7 * p + k
        return pltpu.make_async_remote_copy(
            src_ref=self.x_refs[a].at[self._rows(a, p)] if own else dst, dst_ref=dst,
            send_sem=self.send.at[sem], recv_sem=self.recv.at[sem], device_id=to, device_id_type=MESH)

    def _mine(self, a):
        me = self.me
        return pltpu.make_async_copy(self.x_refs[a], self.o_refs[a].at[4 * me[0] + 2 * me[1] + me[2]], self.loc.at[a])

    def _first(self, a, p):
        cps = [self._cp(a, p, 0, self.me, self.sib, own=True)]
        return cps + [self._cp(a, p, 1 + j, self.me, (*chip, self.c), own=True) for j, chip in enumerate(self.chips)]

    def _passed(self, a, p, j):
        return self._cp(a, p, 4 + j, (*self.chips[j], self.c), self.sib)

    def start(self):
        for a in range(self.n):
            self._mine(a).start()
        for p in range(max(self.pieces)):
            for a in range(self.n):
                if p < self.pieces[a]:
                    for cp in self._first(a, p):
                        cp.start()

    def forward(self, p):
        for j, chip in enumerate(self.chips):
            for a in range(self.n):
                if p < self.pieces[a]:
                    self._cp(a, p, 1 + j, (*chip, self.c), self.me).wait_recv()
                    self._passed(a, p, j).start()

    def finish(self):
        for a in range(self.n):
            for p in range(self.pieces[a]):
                self._cp(a, p, 0, self.sib, self.me).wait_recv()
                for j, chip in enumerate(self.chips):
                    self._cp(a, p, 4 + j, (*chip, 1 - self.c), self.me).wait_recv()
        for a in range(self.n):
            for p in range(self.pieces[a]):
                for cp in self._first(a, p):
                    cp.wait_send()
                for j in range(3):
                    self._passed(a, p, j).wait_send()
            self._mine(a).wait()

    def begin_hosted(self, step, steps):
        @pl.when(step == 0)
        def _():
            self.start()

        n_p = max(self.pieces)
        for p in range(n_p):
            @pl.when(step == min(((p + 1) * 7 * steps) // (8 * n_p), steps - 1))
            def _():
                self.forward(p)

    def end_hosted(self, step, steps):
        @pl.when(step == steps - 1)
        def _():
            self.finish()


class _Exchange:
    def __init__(self, x_refs, o_refs, send, recv, loc):
        self.x_refs, self.o_refs, self.send, self.recv, self.loc = x_refs, o_refs, send, recv, loc
        self.n = len(x_refs)
        x, y, c = _pos()
        self.me_i = 4 * x + 2 * y + c
        self.peers = []
        for k in range(1, N_DEV):
            px = 1 - x if (k >> 2) & 1 else x
            py = 1 - y if (k >> 1) & 1 else y
            pc = 1 - c if k & 1 else c
            self.peers.append(((px, py, pc), 4 * px + 2 * py + pc))

    def _mine(self, a):
        return pltpu.make_async_copy(self.x_refs[a].at[self.me_i], self.o_refs[a].at[self.me_i], self.loc.at[a])

    def _cp(self, a, k, landing):
        peer, peer_i = self.peers[k]
        return pltpu.make_async_remote_copy(
            src_ref=self.x_refs[a].at[peer_i], dst_ref=self.o_refs[a].at[landing],
            send_sem=self.send.at[a * 7 + k], recv_sem=self.recv.at[a * 7 + k],
            device_id=peer, device_id_type=MESH)

    def start(self):
        for a in range(self.n):
            self._mine(a).start()
            for k in range(N_DEV - 1):
                self._cp(a, k, self.me_i).start()

    def finish(self):
        for a in range(self.n):
            for k in range(N_DEV - 1):
                self._cp(a, k, self.peers[k][1]).wait_recv()
        for a in range(self.n):
            for k in range(N_DEV - 1):
                self._cp(a, k, self.me_i).wait_send()
            self._mine(a).wait()


def _gather_scratch(xs):
    n_sem = 7 * sum(_pieces(v.shape, v.dtype) for v in xs)
    return [pltpu.SemaphoreType.DMA((n_sem,)), pltpu.SemaphoreType.DMA((n_sem,)), pltpu.SemaphoreType.DMA((len(xs),))]


ANY_SPEC = pl.BlockSpec(memory_space=pl.ANY)


def _all_gather(xs, name, after=()):
    n, na = len(xs), len(after)

    def body(*refs):
        g = _Gather(refs[:n], refs[n + na:2 * n + na], *refs[2 * n + na:])
        g.start()
        for p in range(max(g.pieces)):
            g.forward(p)
        g.finish()

    return pl.pallas_call(
        body, name=name,
        out_shape=[jax.ShapeDtypeStruct((N_DEV,) + v.shape, v.dtype) for v in xs],
        in_specs=[ANY_SPEC] * (n + na), out_specs=[ANY_SPEC] * n, scratch_shapes=_gather_scratch(xs),
    )(*xs, *after)


HBM_SPEC = pl.BlockSpec(memory_space=pltpu.HBM)
SEM_SPEC = pl.BlockSpec(memory_space=pltpu.SEMAPHORE)
VMEM_SPEC = pl.BlockSpec(memory_space=pltpu.VMEM)
SPLIT_EFFECT = pltpu.SideEffectType.DATAFLOW_SIDE_EFFECTING


def _in_hbm(v):
    return pltpu.with_memory_space_constraint(v, pltpu.HBM)


def _exchange_start(blocks, name):
    n = len(blocks)

    def body(*refs):
        x_refs, land_refs = refs[:n], refs[n:2 * n]
        send, recv = refs[2 * n:2 * n + 2]
        token = refs[-1]
        ex = _Exchange(x_refs, land_refs, send, recv, None)
        for a in range(n):
            for k in range(N_DEV - 1):
                ex._cp(a, k, ex.me_i).start()
        token[...] = jnp.zeros_like(token)

    lands = [lax.empty(v.shape, v.dtype) for v in blocks]
    hbm = tuple(pltpu.HBM(v.shape, v.dtype) for v in list(blocks) + lands)
    n_sem = (N_DEV - 1) * n
    out = pl.pallas_call(
        body, name=name,
        out_shape=(pltpu.SemaphoreType.DMA((n_sem,)), pltpu.SemaphoreType.DMA((n_sem,))) + hbm
        + (jax.ShapeDtypeStruct((8, LANES), F32),),
        in_specs=(HBM_SPEC,) * (2 * n), out_specs=(SEM_SPEC, SEM_SPEC) + (HBM_SPEC,) * (2 * n) + (VMEM_SPEC,),
        input_output_aliases={i: i + 2 for i in range(2 * n)},
        compiler_params=pltpu.CompilerParams(has_side_effects=SPLIT_EFFECT),
    )(*[_in_hbm(v) for v in list(blocks) + lands])
    return out[0], out[1], list(out[2:2 + n]), list(out[2 + n:2 + 2 * n]), out[-1]


def _exchange_wait(ex, after, name, me):
    send, recv, thru, lands, _ = ex
    n = len(thru)

    def body(*refs):
        x_refs, land_refs = refs[:n], refs[n:2 * n]
        send_ref, recv_ref = refs[2 * n:2 * n + 2]
        e = _Exchange(x_refs, land_refs, send_ref, recv_ref, None)
        for a in range(n):
            for k in range(N_DEV - 1):
                e._cp(a, k, e.me_i).wait_send()
                e._cp(a, k, e.peers[k][1]).wait_recv()

    hbm = tuple(pltpu.HBM(v.shape, v.dtype) for v in list(thru) + list(lands))
    out = pl.pallas_call(
        body, name=name, out_shape=hbm,
        in_specs=(HBM_SPEC,) * (2 * n) + (SEM_SPEC, SEM_SPEC, ANY_SPEC), out_specs=(HBM_SPEC,) * (2 * n),
        input_output_aliases={i: i for i in range(2 * n)},
        compiler_params=pltpu.CompilerParams(has_side_effects=SPLIT_EFFECT),
    )(*thru, *lands, send, recv, after)
    done = []
    for own, land in zip(out[:n], out[n:]):
        mine = lax.dynamic_slice(own, (me,) + (0,) * (own.ndim - 1), (1,) + own.shape[1:])
        done.append(lax.dynamic_update_slice(land, mine, (me,) + (0,) * (own.ndim - 1)))
    return done


def _ada_fwd(c_all, w_ada, b_slice):
    def body(c_ref, w_ref, b_ref, o_ref):
        cv = c_ref[...]
        act = cv * _sigmoid(cv)
        o_ref[...] = _mm(act, w_ref[...]) + b_ref[...]

    nb, nc = c_all.shape[0], w_ada.shape[1]
    return pl.pallas_call(body, name="ada_fwd", out_shape=jax.ShapeDtypeStruct((nb, nc), F32),
                          compiler_params=_cparams())(c_all, w_ada, b_slice)


def _adam_math(w, g, m, v):
    m = B1 * m + (1.0 - B1) * g
    v = B2 * v + (1.0 - B2) * jnp.square(g)
    m_hat = m / (1.0 - B1 ** STEP)
    v_hat = v / (1.0 - B2 ** STEP)
    delta = -LR * (m_hat / (jnp.sqrt(v_hat) + AEPS) + WD * w)
    return delta, m, v


def _ada_bwd_adam(c_all, dmod_slice, w, m, v):
    rows, cols = w.shape
    br = 256

    def body(c_ref, d_ref, w_ref, m_ref, v_ref, g_out, dl_out, m_out, v_out):
        cv = c_ref[...]
        act = cv * _sigmoid(cv)
        g = _mm_tn(act, d_ref[...])
        g_out[...] = g
        dl, mn, vn = _adam_math(w_ref[...], g, m_ref[...], v_ref[...])
        dl_out[...] = dl
        m_out[...] = mn
        v_out[...] = vn

    nb = c_all.shape[0]
    wspec = pl.BlockSpec((br, cols), lambda i: (i, 0))
    return pl.pallas_call(
        body, name="ada_bwd_adam", grid=(rows // br,),
        in_specs=[pl.BlockSpec((nb, br), lambda i: (0, i)), pl.BlockSpec((nb, cols), lambda i: (0, 0)),
                  wspec, wspec, wspec],
        out_specs=[wspec] * 4, out_shape=[jax.ShapeDtypeStruct((rows, cols), F32)] * 4,
        compiler_params=_cparams(),
    )(c_all, dmod_slice, w, m, v)


def _mix_in(x2, mod, g_mix, win_g, in_cols, seq, shards):
    T = x2.shape[0]
    tm = min(512, seq)
    tps = seq // tm
    blk_rows = win_g.shape[1]
    pad_rows = -blk_rows % (4 * LANES)
    ns = len(shards)
    steps = T // tm

    def body(*refs):
        x_ref, mod_ref, g_ref, wb_ref = refs[:4]
        sh_refs = refs[4:4 + ns]
        u_ref, pm_ref, wc_ref = refs[4 + ns:7 + ns]
        ga_refs = refs[7 + ns:7 + 2 * ns]
        w_ref, send, recv, loc = refs[7 + 2 * ns:]
        step = pl.program_id(0)
        gather = _Gather(sh_refs, ga_refs, send, recv, loc)
        gather.begin_hosted(step, steps)

        @pl.when(step == 0)
        def _():
            w_ref[:, OFF_DT:] = jnp.zeros((D, PROJ_W - OFF_DT), BF16)
            for j in range(N_DEV):
                blk = jnp.concatenate([wb_ref[j], jnp.zeros((pad_rows, D), BF16)], axis=0)
                w_ref[:, in_cols * j:in_cols * (j + 1)] = blk.T[:, :in_cols]
            wc_ref[...] = w_ref[...]

        x = x_ref[...]
        r = lax.rsqrt(jnp.mean(x * x, axis=-1, keepdims=True) + EPS)
        md = mod_ref[0]
        u = (x * r * g_ref[...]) * (1.0 + md[1:2]) + md[0:1]
        ub = u.astype(BF16)
        u_ref[...] = ub
        pm_ref[...] = jnp.dot(ub, w_ref[...], preferred_element_type=F32)
        gather.end_hosted(step, steps)

    whole = pl.BlockSpec(memory_space=pltpu.VMEM)
    return pl.pallas_call(
        body, name="mix_in", grid=(T // tm,),
        in_specs=[pl.BlockSpec((tm, D), lambda i: (i, 0)), pl.BlockSpec((1, 8, D), lambda i: (i // tps, 0, 0)),
                  pl.BlockSpec((1, D), lambda i: (0, 0)), whole] + [ANY_SPEC] * ns,
        out_specs=[pl.BlockSpec((tm, D), lambda i: (i, 0)), pl.BlockSpec((tm, PROJ_W), lambda i: (i, 0)),
                   pl.BlockSpec((D, PROJ_W), lambda i: (0, 0))] + [ANY_SPEC] * ns,
        out_shape=[jax.ShapeDtypeStruct((T, D), BF16), jax.ShapeDtypeStruct((T, PROJ_W), F32),
                   jax.ShapeDtypeStruct((D, PROJ_W), BF16)]
        + [jax.ShapeDtypeStruct((N_DEV,) + v.shape, v.dtype) for v in shards],
        scratch_shapes=[pltpu.VMEM((D, PROJ_W), BF16)] + _gather_scratch(shards),
        compiler_params=_cparams(),
    )(x2, mod, g_mix, win_g, *shards)


def _chunk_forward(up, z, ux, dtin, halo_p, halo_x, hprev, cw, cb, hp, gssd, wpool, pscale, t0, y_scr, cv=None):
    L = CHUNK
    out = {}
    row = lax.broadcasted_iota(jnp.int32, (L, 1), 0)
    t = (t0 + row + 1).astype(F32)
    e = jnp.concatenate([halo_p, up], axis=0)
    s2 = e + pltpu.roll(e, 1, 0)
    s4 = s2 + pltpu.roll(s2, 2, 0)
    s8 = s4 + pltpu.roll(s4, 4, 0)
    s16 = s8 + pltpu.roll(s8, 8, 0)
    sums = (s2, s4, s8, s16)
    p, inv, yp = [], [], []
    for gi, w in enumerate(WINDOWS):
        sl = slice(gi * LANES, (gi + 1) * LANES)
        ic = 1.0 / jnp.minimum(t, float(w))
        pg = sums[gi][POOL_HALO:, sl] * ic - up[:, sl]
        p.append(pg)
        inv.append(ic)
        yp.append(_mm(pg, wpool[gi]))
    out["p"], out["inv"], out["yp"] = p, inv, yp
    out["y_pool"] = jnp.concatenate(yp, axis=1) * pscale
    if cv is None:
        ex = jnp.concatenate([halo_x, ux], axis=0)
        taps = [pltpu.roll(ex, 3, 0)[CONV_HALO:], pltpu.roll(ex, 2, 0)[CONV_HALO:], pltpu.roll(ex, 1, 0)[CONV_HALO:], ux]
        cv = cb + taps[0] * cw[0:1] + taps[1] * cw[1:2] + taps[2] * cw[2:3] + taps[3] * cw[3:4]
    sg = _sigmoid(cv)
    xbc = cv * sg
    out["cv"], out["sg"] = cv, sg
    X = xbc[:, :D]
    Bm = xbc[:, D:D + N_GROUPS * N_STATE]
    Cm = xbc[:, D + N_GROUPS * N_STATE:]
    pre = dtin + hp[0:1]
    dt = jnp.maximum(pre, 0.0) + jnp.log(1.0 + jnp.exp(-jnp.abs(pre)))
    a_row = -jnp.exp(hp[1:2])
    da = dt * a_row
    ri = lax.broadcasted_iota(jnp.int32, (L, L), 0)
    ci = lax.broadcasted_iota(jnp.int32, (L, L), 1)
    causal = ri >= ci
    cum = _dot01(causal.astype(F32), da, 3, split_lhs=False)
    cum_t = cum.T
    cum_last = cum[L - 1:L]
    eo = jnp.exp(cum)
    dec = jnp.exp(cum_last - cum)
    cd = jnp.exp(cum_last)
    exm = _expand_mat()
    rows8 = jnp.concatenate([cd, hp[2:3], jnp.zeros((6, LANES), F32)], axis=0)
    rep = _dot01(jnp.concatenate([dt, eo, dec, rows8], axis=0), exm, 2)
    dt_rep, eo_rep, dec_rep = rep[0:L], rep[L:2 * L], rep[2 * L:3 * L]
    cd_rep, dskip_rep = rep[3 * L:3 * L + 1], rep[3 * L + 1:3 * L + 2]
    xdt = X * dt_rep
    out.update(X=X, Bm=Bm, Cm=Cm, pre=pre, dt=dt, a_row=a_row, cum=cum, cum_t=cum_t, eo=eo, dec=dec, cd=cd,
               dt_rep=dt_rep, eo_rep=eo_rep, dec_rep=dec_rep, cd_rep=cd_rep, dskip_rep=dskip_rep, xdt=xdt,
               causal=causal, anti=(ri <= ci).astype(F32), exm=exm)
    G, lms, yoff, hnew, xdec = [], [], [], [], []
    for g in range(N_GROUPS):
        gs = slice(g * GROUP_W, (g + 1) * GROUP_W)
        Bg = Bm[:, g * N_STATE:(g + 1) * N_STATE]
        Cg = Cm[:, g * N_STATE:(g + 1) * N_STATE]
        Gg = _mm_nt(Cg, Bg)
        G.append(Gg)
        for hh in range(N_HEADS // N_GROUPS):
            h = g * (N_HEADS // N_GROUPS) + hh
            seg = cum[:, h:h + 1] - cum_t[h:h + 1, :]
            lm = jnp.where(causal, jnp.exp(jnp.minimum(seg, 0.0)), 0.0)
            lms.append(lm)
            hs = slice(h * HEAD_DIM, (h + 1) * HEAD_DIM)
            y_scr[:, hs] = _mm(Gg * lm, xdt[:, hs])
        xd = xdt[:, gs] * dec_rep[:, gs]
        xdec.append(xd)
        sgm = _mm_tn(Bg, xd)
        yoff.append(_mm(Cg, hprev[g]) * eo_rep[:, gs])
        hnew.append(hprev[g] * cd_rep[:, gs] + sgm)
    out.update(G=G, lms=lms, yoff=yoff, hnew=hnew, xdec=xdec)
    y = y_scr[...] + jnp.concatenate(yoff, axis=1) + dskip_rep * X
    sz = _sigmoid(z)
    silz = z * sz
    yz = y * silz
    rg, yn = [], []
    for g in range(N_GROUPS):
        gs = slice(g * GROUP_W, (g + 1) * GROUP_W)
        r = lax.rsqrt(jnp.mean(yz[:, gs] * yz[:, gs], axis=-1, keepdims=True) + EPS)
        rg.append(r)
        yn.append(yz[:, gs] * r)
    yn = jnp.concatenate(yn, axis=1)
    out.update(y=y, sz=sz, silz=silz, rg=rg, yn=yn)
    out["y_ssd"] = yn * gssd
    return out


def _mixer_fwd(pm, cw, cb, hp, gssd, wpool, pscale, nb, seq, shards):
    nc = seq // CHUNK
    ns = len(shards)
    steps = nb * nc

    def body(*refs):
        pm_ref, cw_ref, cb_ref, hp_ref, gs_ref, wp_ref, ps_ref = refs[:7]
        sh_refs = refs[7:7 + ns]
        ym_ref, hs_ref, cv_ref = refs[7 + ns:10 + ns]
        ga_refs = refs[10 + ns:10 + 2 * ns]
        halo_p, halo_x, state, y_scr, send, recv, loc = refs[10 + 2 * ns:]
        c = pl.program_id(1)
        step = pl.program_id(0) * nc + c
        gather = _Gather(sh_refs, ga_refs, send, recv, loc)
        gather.begin_hosted(step, steps)

        @pl.when(c == 0)
        def _():
            halo_p[...] = jnp.zeros_like(halo_p)
            halo_x[...] = jnp.zeros_like(halo_x)
            state[...] = jnp.zeros_like(state)

        up = pm_ref[:, 0:POOL_W]
        z = pm_ref[:, OFF_Z:OFF_XBC]
        ux = pm_ref[:, OFF_XBC:OFF_DT]
        hprev = [state[0], state[1]]
        hs_ref[0, 0, 0] = hprev[0]
        hs_ref[0, 0, 1] = hprev[1]
        o = _chunk_forward(up, z, ux, pm_ref[:, OFF_DT:], halo_p[...], halo_x[...], hprev, cw_ref[...], cb_ref[...],
                           hp_ref[...], gs_ref[...], wp_ref[...], ps_ref[...], c * CHUNK, y_scr)
        ym_ref[:, 0:POOL_W] = o["y_pool"].astype(BF16)
        ym_ref[:, POOL_W:] = o["y_ssd"].astype(BF16)
        cv_ref[...] = o["cv"]
        state[0] = o["hnew"][0]
        state[1] = o["hnew"][1]
        halo_p[...] = up[CHUNK - POOL_HALO:]
        halo_x[...] = ux[CHUNK - CONV_HALO:]
        gather.end_hosted(step, steps)

    def full(shape):
        return pl.BlockSpec(shape, lambda b, c: (0,) * len(shape))

    T = nb * seq
    return pl.pallas_call(
        body, name="mixer_fwd", grid=(nb, nc),
        in_specs=[pl.BlockSpec((CHUNK, PROJ_W), lambda b, c: (b * nc + c, 0)),
                  full((4, CONV_CH)), full((1, CONV_CH)), full((8, LANES)), full((1, D)),
                  full((4, LANES, LANES)), full((1, POOL_W))] + [ANY_SPEC] * ns,
        out_specs=[pl.BlockSpec((CHUNK, MIX_W), lambda b, c: (b * nc + c, 0)),
                   pl.BlockSpec((1, 1, N_GROUPS, N_STATE, GROUP_W), lambda b, c: (b, c, 0, 0, 0)),
                   pl.BlockSpec((CHUNK, CONV_CH), lambda b, c: (b * nc + c, 0))] + [ANY_SPEC] * ns,
        out_shape=[jax.ShapeDtypeStruct((T, MIX_W), BF16),
                   jax.ShapeDtypeStruct((nb, nc, N_GROUPS, N_STATE, GROUP_W), F32),
                   jax.ShapeDtypeStruct((T, CONV_CH), F32)]
        + [jax.ShapeDtypeStruct((N_DEV,) + v.shape, v.dtype) for v in shards],
        scratch_shapes=[pltpu.VMEM((POOL_HALO, POOL_W), F32), pltpu.VMEM((CONV_HALO, CONV_CH), F32),
                        pltpu.VMEM((N_GROUPS, N_STATE, GROUP_W), F32), pltpu.VMEM((CHUNK, D), F32)] + _gather_scratch(shards),
        compiler_params=_cparams(),
    )(pm, cw, cb, hp, gssd, wpool, pscale, *shards)


def _mixer_bwd(pm, cvs, dym, hstates, cw, cb, hp, gssd, wpool, pscale, nb, seq, after=()):
    nc = seq // CHUNK
    hpg = N_HEADS // N_GROUPS
    na = len(after)

    def body(*refs):
        (pm_ref, hpool_ref, cv_ref, dy_ref, hs_ref, cw_ref, cb_ref, hp_ref, gs_ref, wp_ref, ps_ref) = refs[:11]
        dpm_ref, dconv_ref, dhp_ref, dvec_ref, dwp_ref = refs[11 + na:16 + na]
        nxt_q, nxt_cv, rstate, y_scr, dx_scr = refs[16 + na:]
        b = pl.program_id(0)
        ci = pl.program_id(1)
        c = nc - 1 - ci

        @pl.when((b == 0) & (ci == 0))
        def _():
            for r in (dconv_ref, dhp_ref, dvec_ref, dwp_ref):
                r[...] = jnp.zeros_like(r)

        @pl.when(ci == 0)
        def _():
            nxt_q[...] = jnp.zeros_like(nxt_q)
            nxt_cv[...] = jnp.zeros_like(nxt_cv)
            rstate[...] = jnp.zeros_like(rstate)

        first = (c > 0).astype(F32)
        up = pm_ref[:, 0:POOL_W]
        z = pm_ref[:, OFF_Z:OFF_XBC]
        ux = pm_ref[:, OFF_XBC:OFF_DT]
        halo_p = hpool_ref[...] * first
        hprev = [hs_ref[0, 0, 0], hs_ref[0, 0, 1]]
        cw, cb, hp, gssd, wpool, pscale = cw_ref[...], cb_ref[...], hp_ref[...], gs_ref[...], wp_ref[...], ps_ref[...]
        o = _chunk_forward(up, z, ux, pm_ref[:, OFF_DT:], halo_p, None, hprev, cw, cb, hp, gssd, wpool, pscale,
                           c * CHUNK, y_scr, cv=cv_ref[...])
        L = CHUNK
        dy_pool = dy_ref[:, 0:POOL_W].astype(F32)
        dy_ssd = dy_ref[:, POOL_W:].astype(F32)

        dvec_ref[1:2, 0:POOL_W] += jnp.sum(dy_pool * jnp.concatenate(o["yp"], axis=1), axis=0, keepdims=True)
        dyp = dy_pool * pscale
        qs = []
        dps = []
        for gi in range(len(WINDOWS)):
            sl = slice(gi * LANES, (gi + 1) * LANES)
            dwp_ref[gi] += _mm_tn(o["p"][gi], dyp[:, sl])
            dpg = _mm_nt(dyp[:, sl], wpool[gi])
            dps.append(dpg)
            qs.append(dpg * o["inv"][gi])
        q = jnp.concatenate(qs, axis=1)
        e = jnp.concatenate([q, nxt_q[...]], axis=0)
        n = L + POOL_HALO
        s2 = e + pltpu.roll(e, n - 1, 0)
        s4 = s2 + pltpu.roll(s2, n - 2, 0)
        s8 = s4 + pltpu.roll(s4, n - 4, 0)
        s16 = s8 + pltpu.roll(s8, n - 8, 0)
        sums = (s2, s4, s8, s16)
        for gi in range(len(WINDOWS)):
            sl = slice(gi * LANES, (gi + 1) * LANES)
            dpm_ref[:, sl] = (sums[gi][:L, sl] - dps[gi]).astype(BF16)
        nxt_q[...] = q[:POOL_HALO]

        yn, y, silz, sz = o["yn"], o["y"], o["silz"], o["sz"]
        dvec_ref[0:1] += jnp.sum(dy_ssd * yn, axis=0, keepdims=True)
        dyn = dy_ssd * gssd
        dyz = []
        for g in range(N_GROUPS):
            gs = slice(g * GROUP_W, (g + 1) * GROUP_W)
            mean = jnp.mean(dyn[:, gs] * yn[:, gs], axis=-1, keepdims=True)
            dyz.append(o["rg"][g] * (dyn[:, gs] - yn[:, gs] * mean))
        dyz = jnp.concatenate(dyz, axis=1)
        dyv = dyz * silz
        dpm_ref[:, OFF_Z:OFF_XBC] = (dyz * y * (sz * (1.0 + z * (1.0 - sz)))).astype(BF16)

        X, Bm, Cm, xdt = o["X"], o["Bm"], o["Cm"], o["xdt"]
        exm = o["exm"]
        rdm = _reduce_mat()
        lane = lax.broadcasted_iota(jnp.int32, (1, LANES), 1)
        sub = lax.broadcasted_iota(jnp.int32, (LANES, 1), 0)
        dX = o["dskip_rep"] * dyv
        yoff_full = jnp.concatenate(o["yoff"], axis=1)
        rs = jnp.zeros((L, LANES), F32)
        cs_t = jnp.zeros((LANES, L), F32)
        dBs, dCs = [], []
        rh_sums = []
        ddec = []
        for g in range(N_GROUPS):
            gs = slice(g * GROUP_W, (g + 1) * GROUP_W)
            Bg = Bm[:, g * N_STATE:(g + 1) * N_STATE]
            Cg = Cm[:, g * N_STATE:(g + 1) * N_STATE]
            Gg = o["G"][g]
            R = rstate[g]
            dwm = dyv[:, gs] * o["eo_rep"][:, gs]
            dC = _mm_nt(dwm, hprev[g])
            dH = _mm_tn(Cg, dwm)
            dG = jnp.zeros((L, L), F32)
            for hh in range(hpg):
                h = g * hpg + hh
                hs = slice(h * HEAD_DIM, (h + 1) * HEAD_DIM)
                lm = o["lms"][h]
                m_h = Gg * lm
                dM = _mm_nt(dyv[:, hs], xdt[:, hs])
                dx_scr[:, hs] = _mm_tn(m_h, dyv[:, hs])
                qm = dM * m_h
                rs = rs + jnp.sum(qm, axis=1, keepdims=True) * (lane == h).astype(F32)
                cs_t = cs_t + (sub == h).astype(F32) * jnp.sum(qm, axis=0, keepdims=True)
                dG = dG + dM * lm
            dC = dC + _mm(dG, Bg)
            dB = _mm_tn(dG, Cg)
            zx = _mm(Bg, R)
            dxdt_state = zx * o["dec_rep"][:, gs]
            ddec.append(zx * xdt[:, gs])
            dB = dB + _mm_nt(o["xdec"][g], R)
            rh_sums.append(jnp.sum(R * hprev[g], axis=0, keepdims=True))
            rstate[g] = dH + o["cd_rep"][:, gs] * R
            dx_scr[:, gs] = dx_scr[:, gs] + dxdt_state
            dBs.append(dB)
            dCs.append(dC)
        dxdt = dx_scr[...]
        tail = jnp.concatenate([jnp.sum(dyv * X, axis=0, keepdims=True), jnp.concatenate(rh_sums, axis=1),
                                jnp.zeros((6, D), F32)], axis=0)
        red = _dot01(jnp.concatenate([dyv * yoff_full, jnp.concatenate(ddec, axis=1), dxdt * X, tail], axis=0), rdm, 2)
        d_dskip, dcd_row = red[3 * L:3 * L + 1], red[3 * L + 1:3 * L + 2]
        ddec_h = red[L:2 * L] * o["dec"]
        dcum_last = jnp.sum(ddec_h, axis=0, keepdims=True) + dcd_row * o["cd"]
        dcum = red[0:L] + rs - cs_t.T - ddec_h + (sub == L - 1).astype(F32) * dcum_last
        dda = _dot01(o["anti"], dcum, 3, split_lhs=False)
        ddt_v = dda * o["a_row"] + red[2 * L:3 * L]
        dX = dX + dxdt * o["dt_rep"]
        head_mask = (lane < N_HEADS).astype(F32)
        d_alog = jnp.sum(dda * o["dt"], axis=0, keepdims=True) * o["a_row"] * head_mask
        dpre = ddt_v * _sigmoid(o["pre"]) * head_mask
        dpm_ref[:, OFF_DT:] = dpre.astype(BF16)
        d_dtb = jnp.sum(dpre, axis=0, keepdims=True)
        dhp_ref[...] += jnp.concatenate([d_dtb, d_alog, d_dskip * head_mask, jnp.zeros((5, LANES), F32)], axis=0)

        dxbc = jnp.concatenate([dX] + dBs + dCs, axis=1)
        sg, cv = o["sg"], o["cv"]
        dcv = dxbc * (sg * (1.0 + cv * (1.0 - sg)))
        e2 = jnp.concatenate([dcv, nxt_cv[...]], axis=0)
        n2 = L + CONV_HALO
        ahead = [dcv, pltpu.roll(e2, n2 - 1, 0)[:L], pltpu.roll(e2, n2 - 2, 0)[:L], pltpu.roll(e2, n2 - 3, 0)[:L]]
        dconv_ref[0:5] += jnp.concatenate(
            [jnp.sum(ux * ahead[3 - k], axis=0, keepdims=True) for k in range(4)]
            + [jnp.sum(dcv, axis=0, keepdims=True)], axis=0)
        dux = ahead[0] * cw[3:4] + ahead[1] * cw[2:3] + ahead[2] * cw[1:2] + ahead[3] * cw[0:1]
        dpm_ref[:, OFF_XBC:OFF_DT] = dux.astype(BF16)
        nxt_cv[...] = dcv[:CONV_HALO]

    def full(shape):
        return pl.BlockSpec(shape, lambda b, c: (0,) * len(shape))

    def rowblk(b, c):
        return b * nc + (nc - 1 - c)

    hp_blocks = CHUNK // POOL_HALO
    T = nb * seq
    return pl.pallas_call(
        body, name="mixer_bwd", grid=(nb, nc),
        in_specs=[pl.BlockSpec((CHUNK, PROJ_W), lambda b, c: (rowblk(b, c), 0)),
                  pl.BlockSpec((POOL_HALO, POOL_W), lambda b, c: (jnp.maximum(rowblk(b, c) * hp_blocks - 1, 0), 0)),
                  pl.BlockSpec((CHUNK, CONV_CH), lambda b, c: (rowblk(b, c), 0)),
                  pl.BlockSpec((CHUNK, MIX_W), lambda b, c: (rowblk(b, c), 0)),
                  pl.BlockSpec((1, 1, N_GROUPS, N_STATE, GROUP_W), lambda b, c: (b, nc - 1 - c, 0, 0, 0)),
                  full((4, CONV_CH)), full((1, CONV_CH)), full((8, LANES)), full((1, D)),
                  full((4, LANES, LANES)), full((1, POOL_W))] + [ANY_SPEC] * na,
        out_specs=[pl.BlockSpec((CHUNK, PROJ_W), lambda b, c: (rowblk(b, c), 0)),
                   full((8, CONV_CH)), full((8, LANES)), full((8, D)), full((4, LANES, LANES))],
        out_shape=[jax.ShapeDtypeStruct((T, PROJ_W), BF16),
                   jax.ShapeDtypeStruct((8, CONV_CH), F32), jax.ShapeDtypeStruct((8, LANES), F32),
                   jax.ShapeDtypeStruct((8, D), F32), jax.ShapeDtypeStruct((4, LANES, LANES), F32)],
        scratch_shapes=[pltpu.VMEM((POOL_HALO, POOL_W), F32), pltpu.VMEM((CONV_HALO, CONV_CH), F32),
                        pltpu.VMEM((N_GROUPS, N_STATE, GROUP_W), F32), pltpu.VMEM((CHUNK, D), F32),
                        pltpu.VMEM((CHUNK, D), F32)],
        compiler_params=_cparams(),
    )(pm, pm, cvs, dym, hstates, cw, cb, hp, gssd, wpool, pscale, *after)


def _mlp_fused(x2, ymix, target, mod, g_mlp, g_final, w_out, w_up_a, w_up_b, w_down, seq):
    T = x2.shape[0]
    tm = min(256, seq)
    tps = seq // tm
    nblk = D_FF // FF_BLK

    ka = w_up_a.shape[1]

    def body(x_ref, ym_ref, tg_ref, mod_ref, gm_ref, gf_ref, wo_ref, wua_ref, wub_ref, wd_ref,
             da_ref, dym_ref, dh1_ref, u2_ref, f_ref, dup_ref, ddn_ref, dmod_ref, acc_ref, relu_scr):
        i = pl.program_id(0)

        @pl.when(i == 0)
        def _():
            acc_ref[...] = jnp.zeros_like(acc_ref)

        @pl.when(i % tps == 0)
        def _():
            dmod_ref[...] = jnp.zeros_like(dmod_ref)

        md = mod_ref[0]
        gate_m, shift_f, scale_f, gate_f = md[2:3], md[3:4], md[4:5], md[5:6]
        g_mlp, g_fin = gm_ref[...], gf_ref[...]
        a = jnp.dot(ym_ref[...], wo_ref[...], preferred_element_type=F32)
        h1 = x_ref[...] + gate_m * a
        r2 = lax.rsqrt(jnp.mean(h1 * h1, axis=-1, keepdims=True) + EPS)
        n2 = h1 * r2
        u2 = (n2 * g_mlp) * (1.0 + scale_f) + shift_f
        u2b = u2.astype(BF16)
        u2_ref[...] = u2b
        dn = jnp.zeros((tm, D), F32)
        for j in range(nblk):
            js = slice(j * FF_BLK, (j + 1) * FF_BLK)
            upj = jnp.maximum(jnp.dot(u2b[:, :ka], wua_ref[j], preferred_element_type=F32)
                              + jnp.dot(u2b[:, ka:], wub_ref[j], preferred_element_type=F32), 0.0)
            relu_scr[:, js] = upj
            fj = (upj * upj).astype(BF16)
            f_ref[:, js] = fj
            dn = dn + jnp.dot(fj, wd_ref[j], preferred_element_type=F32)
        h2 = h1 + gate_f * dn
        r3 = lax.rsqrt(jnp.mean(h2 * h2, axis=-1, keepdims=True) + EPS)
        n3 = h2 * r3
        err = n3 * g_fin - tg_ref[...]
        loss = 0.5 * jnp.sum(jnp.mean(err * err, axis=-1, keepdims=True), axis=0, keepdims=True)
        dout = err * (1.0 / D)
        d_gfin = jnp.sum(dout * n3, axis=0, keepdims=True)
        dn3 = dout * g_fin
        dh2 = r3 * (dn3 - n3 * jnp.mean(dn3 * n3, axis=-1, keepdims=True))
        d_gate_f = jnp.sum(dh2 * dn, axis=0, keepdims=True)
        ddn = (gate_f * dh2).astype(BF16)
        ddn_ref[...] = ddn
        du2 = jnp.zeros((tm, D), F32)
        for j in range(nblk):
            js = slice(j * FF_BLK, (j + 1) * FF_BLK)
            dfj = lax.dot_general(ddn, wd_ref[j], (((1,), (1,)), ((), ())), preferred_element_type=F32)
            dupj = (dfj * (2.0 * relu_scr[:, js])).astype(BF16)
            dup_ref[:, js] = dupj
            du2 = du2 + jnp.concatenate(
                [lax.dot_general(dupj, w[j], (((1,), (1,)), ((), ())), preferred_element_type=F32)
                 for w in (wua_ref, wub_ref)], axis=1)
        d_scale_f = jnp.sum(du2 * (n2 * g_mlp), axis=0, keepdims=True)
        d_shift_f = jnp.sum(du2, axis=0, keepdims=True)
        d_gmlp = jnp.sum(du2 * (1.0 + scale_f) * n2, axis=0, keepdims=True)
        dn2 = du2 * (g_mlp * (1.0 + scale_f))
        dh1 = dh2 + r2 * (dn2 - n2 * jnp.mean(dn2 * n2, axis=-1, keepdims=True))
        dh1_ref[...] = dh1
        d_gate_m = jnp.sum(dh1 * a, axis=0, keepdims=True)
        da = (gate_m * dh1).astype(BF16)
        da_ref[...] = da
        dym_ref[...] = lax.dot_general(da, wo_ref[...], (((1,), (1,)), ((), ())),
                                       preferred_element_type=F32).astype(BF16)
        dmod_ref[0] += jnp.concatenate([jnp.zeros((2, D), F32), d_gate_m, d_shift_f, d_scale_f, d_gate_f,
                                        jnp.zeros((2, D), F32)], axis=0)
        acc_ref[...] += jnp.concatenate([d_gmlp, d_gfin, loss * jnp.ones((1, D), F32), jnp.zeros((5, D), F32)], axis=0)

    whole = pl.BlockSpec(memory_space=pltpu.VMEM)

    def tok(w):
        return pl.BlockSpec((tm, w), lambda i: (i, 0))

    def vec():
        return pl.BlockSpec((1, D), lambda i: (0, 0))

    nb = T // seq
    return pl.pallas_call(
        body, name="mlp_fused", grid=(T // tm,),
        in_specs=[tok(D), tok(MIX_W), tok(D), pl.BlockSpec((1, 8, D), lambda i: (i // tps, 0, 0)), vec(), vec(),
                  whole, whole, whole, whole],
        out_specs=[tok(D), tok(MIX_W), tok(D), tok(D), tok(D_FF), tok(D_FF), tok(D),
                   pl.BlockSpec((1, 8, D), lambda i: (i // tps, 0, 0)), pl.BlockSpec((8, D), lambda i: (0, 0))],
        out_shape=[jax.ShapeDtypeStruct((T, D), BF16), jax.ShapeDtypeStruct((T, MIX_W), BF16),
                   jax.ShapeDtypeStruct((T, D), F32), jax.ShapeDtypeStruct((T, D), BF16),
                   jax.ShapeDtypeStruct((T, D_FF), BF16), jax.ShapeDtypeStruct((T, D_FF), BF16),
                   jax.ShapeDtypeStruct((T, D), BF16), jax.ShapeDtypeStruct((nb, 8, D), F32),
                   jax.ShapeDtypeStruct((8, D), F32)],
        scratch_shapes=[pltpu.VMEM((tm, D_FF), F32)],
        compiler_params=_cparams(),
    )(x2, ymix, target, mod, g_mlp, g_final, w_out, w_up_a, w_up_b, w_down)


def _in_bwd(x2, dh1, dpb, mod, g_mix, w_cat, dmod_a, acc_a, seq):
    T = x2.shape[0]
    tm = min(1024, seq)
    tps = seq // tm
    steps = T // tm

    def body(x_ref, dh_ref, dpb_ref, mod_ref, g_ref, w_ref, dma_ref, acca_ref, dx_ref, dmod_ref, acc_ref):
        i = pl.program_id(0)

        @pl.when(i == 0)
        def _():
            acc_ref[...] = acca_ref[...]

        @pl.when(i % tps == 0)
        def _():
            dmod_ref[...] = dma_ref[...]

        du = lax.dot_general(dpb_ref[...], w_ref[...], (((1,), (1,)), ((), ())), preferred_element_type=F32)
        x = x_ref[...]
        md = mod_ref[0]
        g = g_ref[...]
        r = lax.rsqrt(jnp.mean(x * x, axis=-1, keepdims=True) + EPS)
        n1 = x * r
        d_scale = jnp.sum(du * (n1 * g), axis=0, keepdims=True)
        d_shift = jnp.sum(du, axis=0, keepdims=True)
        d_g = jnp.sum(du * (1.0 + md[1:2]) * n1, axis=0, keepdims=True)
        dn1 = du * (g * (1.0 + md[1:2]))
        dx_ref[...] = dh_ref[...] + r * (dn1 - n1 * jnp.mean(dn1 * n1, axis=-1, keepdims=True))
        dmod_ref[0] += jnp.concatenate([d_shift, d_scale, jnp.zeros((6, D), F32)], axis=0)
        acc_ref[...] += jnp.concatenate([jnp.zeros((3, D), F32), d_g, jnp.zeros((4, D), F32)], axis=0)

    whole = pl.BlockSpec(memory_space=pltpu.VMEM)
    nb = T // seq
    return pl.pallas_call(
        body, name="in_bwd", grid=(steps,),
        in_specs=[pl.BlockSpec((tm, D), lambda i: (i, 0)), pl.BlockSpec((tm, D), lambda i: (i, 0)),
                  pl.BlockSpec((tm, PROJ_W), lambda i: (i, 0)),
                  pl.BlockSpec((1, 8, D), lambda i: (i // tps, 0, 0)), pl.BlockSpec((1, D), lambda i: (0, 0)),
                  whole, pl.BlockSpec((1, 8, D), lambda i: (i // tps, 0, 0)), pl.BlockSpec((8, D), lambda i: (0, 0))],
        out_specs=[pl.BlockSpec((tm, D), lambda i: (i, 0)),
                   pl.BlockSpec((1, 8, D), lambda i: (i // tps, 0, 0)), pl.BlockSpec((8, D), lambda i: (0, 0))],
        out_shape=[jax.ShapeDtypeStruct((T, D), F32),
                   jax.ShapeDtypeStruct((nb, 8, D), F32), jax.ShapeDtypeStruct((8, D), F32)],
        compiler_params=_cparams(),
    )(x2, dh1, dpb, mod, g_mix, w_cat, dmod_a, acc_a)


def _dw_in(u_b, dpb, in_cols, shards):
    T = u_b.shape[0]
    bk = min(1024, T)
    nk = T // bk
    ns = len(shards)

    def body(*refs):
        u_ref, d_ref = refs[:2]
        sh_refs = refs[2:2 + ns]
        o_ref = refs[2 + ns]
        ga_refs = refs[3 + ns:3 + 2 * ns]
        acc, send, recv, loc = refs[3 + 2 * ns:]
        k = pl.program_id(0)
        gather = _Gather(sh_refs, ga_refs, send, recv, loc)
        gather.begin_hosted(k, nk)

        @pl.when(k == 0)
        def _():
            acc[...] = jnp.zeros_like(acc)

        acc[...] += jnp.dot(u_ref[...].T, d_ref[...], preferred_element_type=F32)

        @pl.when(k == nk - 1)
        def _():
            for j in range(N_DEV):
                o_ref[j] = acc[:, in_cols * j:in_cols * (j + 1)].astype(BF16)

        gather.end_hosted(k, nk)

    return pl.pallas_call(
        body, name="dw_in", grid=(nk,),
        in_specs=[pl.BlockSpec((bk, D), lambda k: (k, 0)), pl.BlockSpec((bk, PROJ_W), lambda k: (k, 0))]
        + [ANY_SPEC] * ns,
        out_specs=[pl.BlockSpec((N_DEV, D, in_cols), lambda k: (0, 0, 0))] + [ANY_SPEC] * ns,
        out_shape=[jax.ShapeDtypeStruct((N_DEV, D, in_cols), BF16)]
        + [jax.ShapeDtypeStruct((N_DEV,) + v.shape, v.dtype) for v in shards],
        scratch_shapes=[pltpu.VMEM((D, PROJ_W), F32)] + _gather_scratch(shards),
        compiler_params=_cparams(),
    )(u_b, dpb, *shards)


def _dw_blocks(a, b, name, by_rows, per_step=1, after=()):
    T, M = a.shape
    N = b.shape[1]
    bk = min(4096, T)
    nk = T // bk
    whole = pl.BlockSpec(memory_space=pltpu.VMEM)
    if by_rows:
        rows = M // N_DEV
        am = rows * per_step
        nblk = N_DEV // per_step
        a_spec, b_spec = pl.BlockSpec((bk, am), lambda i, k: (k, i)), whole
        out_blk, acc_shape = (per_step, rows, N), (am, N)
    else:
        cols = N // N_DEV
        nblk = N_DEV
        a_spec, b_spec = whole, pl.BlockSpec((bk, cols), lambda i, k: (k, i))
        out_blk, acc_shape = (1, M, cols), (M, cols)

    def body(a_ref, b_ref, *rest):
        o_ref, acc = rest[len(after):]
        k = pl.program_id(1)

        @pl.when(k == 0)
        def _():
            acc[...] = jnp.zeros_like(acc)

        tok = pl.ds(pl.multiple_of(k * bk, bk), bk)
        a_blk = a_ref[...] if by_rows else a_ref[tok, :]
        b_blk = b_ref[tok, :] if by_rows else b_ref[...]
        acc[...] += lax.dot_general(a_blk, b_blk, (((0,), (0,)), ((), ())), preferred_element_type=F32)

        @pl.when(k == nk - 1)
        def _():
            o_ref[...] = acc[...].reshape(out_blk).astype(BF16)

    return pl.pallas_call(
        body, name=name, grid=(nblk, nk), in_specs=[a_spec, b_spec] + [ANY_SPEC] * len(after),
        out_specs=pl.BlockSpec(out_blk, lambda i, k: (i, 0, 0)),
        out_shape=jax.ShapeDtypeStruct((N_DEV,) + out_blk[1:], BF16),
        scratch_shapes=[pltpu.VMEM(acc_shape, F32)],
        compiler_params=_cparams(),
    )(a, b, *after)


def _adam_parts(parts, w, m, v, name):
    rows, cols = w.shape
    br = rows
    for cand in range(rows, 15, -16):
        if rows % cand == 0 and cand * cols * 4 <= ADAM_BLOCK_BYTES:
            br = cand
            break

    def body(p_ref, w_ref, m_ref, v_ref, g_out, dl_out, m_out, v_out):
        g = p_ref[0].astype(F32)
        for k in range(1, N_DEV):
            g = g + p_ref[k].astype(F32)
        g_out[...] = g
        dl, mn, vn = _adam_math(w_ref[...], g, m_ref[...], v_ref[...])
        dl_out[...] = dl
        m_out[...] = mn
        v_out[...] = vn

    wspec = pl.BlockSpec((br, cols), lambda i: (i, 0))
    return pl.pallas_call(
        body, name=name, grid=(rows // br,),
        in_specs=[pl.BlockSpec((N_DEV, br, cols), lambda i: (0, i, 0)), wspec, wspec, wspec],
        out_specs=[wspec] * 4, out_shape=[jax.ShapeDtypeStruct((rows, cols), F32)] * 4,
        compiler_params=_cparams(),
    )(parts, w, m, v)


def _adam_plain(g, w, m, v, name):
    def body(g_ref, w_ref, m_ref, v_ref, dl_out, m_out, v_out):
        dl, mn, vn = _adam_math(w_ref[...], g_ref[...], m_ref[...], v_ref[...])
        dl_out[...] = dl
        m_out[...] = mn
        v_out[...] = vn

    return pl.pallas_call(body, name=name, out_shape=[jax.ShapeDtypeStruct(w.shape, F32)] * 3,
                          compiler_params=_cparams())(g, w, m, v)


SMALL_PARAMS = ("b_ada", "g_mix", "conv_b", "dt_bias", "a_log", "d_skip", "g_ssd", "pool_scale", "g_mlp", "g_final")


def _small_adam(gathered, params):
    n_par = len(SMALL_PARAMS)
    nb = gathered[0].shape[1]

    def body(*refs):
        dmod_ref, acc_ref, conv_ref, vec_ref, hd_ref = refs[:5]
        par_refs = refs[5:5 + 3 * n_par]
        out_refs = refs[5 + 3 * n_par:5 + 7 * n_par]
        cw_out, acc_out = refs[5 + 7 * n_par:]

        def total(ref):
            t = ref[0]
            for k in range(1, N_DEV):
                t = t + ref[k]
            return t

        dm = total(dmod_ref)
        dmb = dm[0]
        for b in range(1, nb):
            dmb = dmb + dm[b]
        ac, cv, vc, hd = total(acc_ref), total(conv_ref), total(vec_ref), total(hd_ref)
        cw_out[...] = cv[0:4]
        acc_out[...] = ac
        grads = {
            "b_ada": jnp.concatenate([dmb[r:r + 1] for r in range(6)], axis=1), "g_mix": ac[3:4], "conv_b": cv[4:5],
            "dt_bias": hd[0:1, 0:N_HEADS], "a_log": hd[1:2, 0:N_HEADS], "d_skip": hd[2:3, 0:N_HEADS],
            "g_ssd": vc[0:1], "pool_scale": vc[1:2, 0:POOL_W], "g_mlp": ac[0:1], "g_final": ac[1:2],
        }
        for i, name in enumerate(SMALL_PARAMS):
            w_ref, m_ref, v_ref = par_refs[3 * i:3 * i + 3]
            g = grads[name]
            dl, mn, vn = _adam_math(w_ref[...], g, m_ref[...], v_ref[...])
            g_o, d_o, m_o, v_o = out_refs[4 * i:4 * i + 4]
            g_o[...] = g
            d_o[...] = dl
            m_o[...] = mn
            v_o[...] = vn

    flat = [a for name in SMALL_PARAMS for a in params[name]]
    out_shape = [jax.ShapeDtypeStruct(params[name][0].shape, F32) for name in SMALL_PARAMS for _ in range(4)]
    out_shape += [jax.ShapeDtypeStruct((4, CONV_CH), F32), jax.ShapeDtypeStruct((8, D), F32)]
    return pl.pallas_call(body, name="small_adam", out_shape=out_shape, compiler_params=_cparams())(*gathered, *flat)


def kernel(x, c, w_ada, b_ada, g_mix, w_in, conv_w, conv_b, dt_bias, a_log, d_skip, g_ssd, w_pool, pool_scale, w_out, g_mlp, w_up, w_down, g_final, loss_target, m_w_ada, m_b_ada, m_g_mix, m_w_in, m_conv_w, m_conv_b, m_dt_bias, m_a_log, m_d_skip, m_g_ssd, m_w_pool, m_pool_scale, m_w_out, m_g_mlp, m_w_up, m_w_down, m_g_final, v_w_ada, v_b_ada, v_g_mix, v_w_in, v_conv_w, v_conv_b, v_dt_bias, v_a_log, v_d_skip, v_g_ssd, v_w_pool, v_pool_scale, v_w_out, v_g_mlp, v_w_up, v_w_down, v_g_final):
    nb, seq, _ = x.shape
    T = nb * seq
    me = 4 * lax.axis_index("x") + 2 * lax.axis_index("y") + lax.axis_index("c")
    in_cols = w_in.shape[2]
    ada_cols = w_ada.shape[2]
    cw_cols = conv_w.shape[2]

    win_t = jnp.pad(w_in[0].astype(BF16).T, ((0, -in_cols % 16), (0, 0)))
    c_g, cw_g, win_g = _all_gather([c, conv_w[0], win_t], "ag_first")
    c_all = c_g.reshape(N_DEV * nb, D)
    cw_full = cw_g.transpose(1, 0, 2).reshape(4, CONV_CH)

    b_slice = lax.dynamic_slice(b_ada, (0, me * ada_cols), (1, ada_cols))
    mod_cols = _ada_fwd(c_all, w_ada[0], b_slice)
    (mod_g,) = _all_gather([mod_cols], "ag_mod")
    mod_all = mod_g.transpose(1, 0, 2).reshape(N_DEV * nb, 6, D)
    mod_mine = lax.dynamic_slice(mod_all, (me * nb, 0, 0), (nb, 6, D))
    mod = jnp.pad(mod_mine, ((0, 0), (0, 2), (0, 0)))

    x2 = x.reshape(T, D)
    tg2 = loss_target.reshape(T, D)
    heads = jnp.pad(jnp.concatenate([dt_bias, a_log, d_skip], axis=0), ((0, 5), (0, LANES - N_HEADS)))
    wpool_b = w_pool[0]
    wup_b = w_up[0].astype(BF16)
    u_b, pm, w_cat, wupa_g = _mix_in(x2, mod, g_mix, win_g, in_cols, seq, [wup_b[:D // 2]])
    ymix, hstates, cvs, wupb_g, wout_g, wdn_g = _mixer_fwd(
        pm, cw_full, conv_b, heads, g_ssd, wpool_b, pool_scale, nb, seq,
        [wup_b[D // 2:], w_out[0].astype(BF16), w_down[0].astype(BF16)])
    da_b, dym, dh1, u2_b, f_b, dup_b, ddn_b, dmod_a, acc_a = _mlp_fused(
        x2, ymix, tg2, mod, g_mlp, g_final.reshape(1, D), wout_g.reshape(MIX_W, D), wupa_g, wupb_g, wdn_g, seq)

    gout_p = _dw_blocks(ymix, da_b, "dw_out", True, per_step=4)
    gup_p = _dw_blocks(u2_b, dup_b, "dw_up", False)
    ex_a = _exchange_start([gout_p, gup_p], "ga_start")
    gdn_p = _dw_blocks(f_b, ddn_b, "dw_down", True, after=[ex_a[4]])
    ex_b = _exchange_start([gdn_p], "gb_start")
    dpb, d_conv, d_heads, d_vec, d_wpool = _mixer_bwd(
        pm, cvs, dym, hstates, cw_full, conv_b, heads, g_ssd, wpool_b, pool_scale, nb, seq, after=[ex_b[4]])
    gin_p, conv_g, vec_g, heads_g, wpool_parts = _dw_in(
        u_b, dpb, in_cols, [d_conv, d_vec, d_heads, d_wpool.reshape(4 * LANES, LANES)])
    ex_in = _exchange_start([gin_p], "gin_start")
    grad_x2, dmod, acc = _in_bwd(x2, dh1, dpb, mod, g_mix + ex_in[4][0:1, 0:1], w_cat, dmod_a, acc_a, seq)

    gout_r, gup_r = _exchange_wait(ex_a, dmod, "ga_wait", me)
    (gdn_r,) = _exchange_wait(ex_b, dmod, "gb_wait", me)
    g_out, d_out, nm_out, nv_out = _adam_parts(gout_r, w_out[0], m_w_out[0], v_w_out[0], "adam_w_out")
    g_up, d_up, nm_up, nv_up = _adam_parts(gup_r, w_up[0], m_w_up[0], v_w_up[0], "adam_w_up")
    g_dn, d_dn, nm_dn, nv_dn = _adam_parts(gdn_r, w_down[0], m_w_down[0], v_w_down[0], "adam_w_down")

    dmod_g, acc_g = _all_gather([dmod, acc], "ag_small_bwd", after=[nm_out, nm_up, nm_dn])
    pool2 = (4 * LANES, LANES)
    wpool_outs = _adam_parts(wpool_parts, w_pool.reshape(pool2), m_w_pool.reshape(pool2), v_w_pool.reshape(pool2),
                             "adam_w_pool")
    small_params = {
        "b_ada": (b_ada, m_b_ada, v_b_ada), "g_mix": (g_mix, m_g_mix, v_g_mix), "conv_b": (conv_b, m_conv_b, v_conv_b),
        "dt_bias": (dt_bias, m_dt_bias, v_dt_bias), "a_log": (a_log, m_a_log, v_a_log),
        "d_skip": (d_skip, m_d_skip, v_d_skip), "g_ssd": (g_ssd, m_g_ssd, v_g_ssd),
        "pool_scale": (pool_scale, m_pool_scale, v_pool_scale), "g_mlp": (g_mlp, m_g_mlp, v_g_mlp),
        "g_final": tuple(a.reshape(1, D) for a in (g_final, m_g_final, v_g_final)),
    }
    small_res = _small_adam([dmod_g, acc_g, conv_g, vec_g, heads_g], small_params)
    g_cw_full, acc_sum = small_res[-2:]
    loss = acc_sum[2, 0]

    g_cw = lax.dynamic_slice(g_cw_full, (0, me * cw_cols), (4, cw_cols))
    d_cwp, nm_cwp, nv_cwp = _adam_plain(g_cw, conv_w[0], m_conv_w[0], v_conv_w[0], "adam_conv_w")

    dmod_all = dmod_g[:, :, 0:6].reshape(N_DEV * nb, 6 * D)
    dmod_slice = lax.dynamic_slice(dmod_all, (0, me * ada_cols), (N_DEV * nb, ada_cols))
    g_ada, d_ada, nm_ada, nv_ada = _ada_bwd_adam(c_all, dmod_slice, w_ada[0], m_w_ada[0], v_w_ada[0])

    ex_after = nm_ada[0:8, 0:LANES] + acc_sum[:, 0:LANES]
    (gin_r,) = _exchange_wait(ex_in, ex_after, "gin_wait", me)
    g_in, d_in, nm_in, nv_in = _adam_parts(gin_r, w_in[0], m_w_in[0], v_w_in[0], "adam_w_in")

    def small_outs(kind, wpool):
        res = {name: small_res[4 * i + kind] for i, name in enumerate(SMALL_PARAMS)}
        res["g_final"] = res["g_final"].reshape(D)
        res["w_pool"] = wpool.reshape(1, 4, LANES, LANES)
        return res

    def big_outs(ada, win, cwp, wout, wup, wdn):
        return {"w_ada": ada[None], "w_in": win.reshape(1, D, in_cols), "conv_w": cwp[None], "w_out": wout[None],
                "w_up": wup[None], "w_down": wdn[None]}

    order = ["w_ada", "b_ada", "g_mix", "w_in", "conv_w", "conv_b", "dt_bias", "a_log", "d_skip", "g_ssd", "w_pool",
             "pool_scale", "w_out", "g_mlp", "w_up", "w_down", "g_final"]
    groups = [
        {**small_outs(0, wpool_outs[0]), **big_outs(g_ada, g_in, g_cw, g_out, g_up, g_dn)},
        {**small_outs(1, wpool_outs[1]), **big_outs(d_ada, d_in, d_cwp, d_out, d_up, d_dn)},
        {**small_outs(2, wpool_outs[2]), **big_outs(nm_ada, nm_in, nm_cwp, nm_out, nm_up, nm_dn)},
        {**small_outs(3, wpool_outs[3]), **big_outs(nv_ada, nv_in, nv_cwp, nv_out, nv_up, nv_dn)},
    ]
    outs = [loss, grad_x2.reshape(nb, seq, D)]
    for grp in groups:
        outs += [grp[n] for n in order]
    return tuple(outs)
```

```python
import functools

import jax
import jax.numpy as jnp
from jax import lax
from jax.experimental import pallas as pl
from jax.experimental.pallas import tpu as pltpu

F32, BF16 = jnp.float32, jnp.bfloat16
MESH = pl.DeviceIdType.MESH
N_DEV = 8
D = 1024
LANES = 128
CHUNK = 128
POOL_W = 512
WINDOWS = (2, 4, 8, 16)
N_HEADS = 16
HEAD_DIM = 64
N_GROUPS = 2
GROUP_W = 512
N_STATE = 128
CONV_CH = 1536
OFF_Z, OFF_XBC, OFF_DT, IN_W = 512, 1536, 3072, 3088
PROJ_W = OFF_DT + LANES
MIX_W = 1536
D_FF = 4096
FF_BLK = 512
EPS = 1e-5
LR, B1, B2, AEPS, WD, STEP = 0.001, 0.9, 0.999, 1e-08, 0.01, 10
POOL_HALO = 16
CONV_HALO = 8
VMEM_LIMIT = 56 << 20
ADAM_BLOCK_BYTES = 1 << 20


def _cparams(**kw):
    return pltpu.CompilerParams(vmem_limit_bytes=VMEM_LIMIT, **kw)


def _mm(a, b):
    return jnp.dot(a.astype(BF16), b.astype(BF16), preferred_element_type=F32)


def _mm_nt(a, b):
    return lax.dot_general(a.astype(BF16), b.astype(BF16), (((1,), (1,)), ((), ())), preferred_element_type=F32)


def _mm_tn(a, b):
    return lax.dot_general(a.astype(BF16), b.astype(BF16), (((0,), (0,)), ((), ())), preferred_element_type=F32)


def _split_bf16(v, terms):
    parts, rest = [], v
    for t in range(terms):
        p = rest.astype(BF16)
        parts.append(p)
        if t + 1 < terms:
            rest = rest - p.astype(F32)
    return parts


def _dot01(a, b, terms, split_lhs=True):
    if split_lhs:
        bb = b.astype(BF16)
        prods = [jnp.dot(p, bb, preferred_element_type=F32) for p in _split_bf16(a, terms)]
    else:
        ab = a.astype(BF16)
        prods = [jnp.dot(ab, p, preferred_element_type=F32) for p in _split_bf16(b, terms)]
    out = prods[0]
    for q in prods[1:]:
        out = out + q
    return out


def _sigmoid(v):
    return 1.0 / (1.0 + jnp.exp(-v))


def _expand_mat():
    r = lax.broadcasted_iota(jnp.int32, (LANES, D), 0)
    c = lax.broadcasted_iota(jnp.int32, (LANES, D), 1)
    return (r == c // HEAD_DIM).astype(F32)


def _reduce_mat():
    r = lax.broadcasted_iota(jnp.int32, (D, LANES), 0)
    c = lax.broadcasted_iota(jnp.int32, (D, LANES), 1)
    return (c == r // HEAD_DIM).astype(F32)


def _pos():
    return lax.axis_index("x"), lax.axis_index("y"), lax.axis_index("c")


GATHER_PIECES = 4
GATHER_PIECE_BYTES = 96 << 10


def _pieces(shape, dtype):
    rows = shape[0]
    size = jnp.dtype(dtype).itemsize
    for d in shape:
        size *= d
    whole_tiles = rows % (GATHER_PIECES * 16) == 0
    return GATHER_PIECES if whole_tiles and size // GATHER_PIECES >= GATHER_PIECE_BYTES else 1


class _Gather:
    def __init__(self, x_refs, o_refs, send, recv, loc):
        self.x_refs, self.o_refs, self.send, self.recv, self.loc = x_refs, o_refs, send, recv, loc
        self.n = len(x_refs)
        self.pieces = [_pieces(r.shape, r.dtype) for r in x_refs]
        self.base = [7 * sum(self.pieces[:a]) for a in range(self.n)]
        x, y, c = _pos()
        self.c = c
        self.me, self.sib = (x, y, c), (x, y, 1 - c)
        self.chips = [(1 - x, y), (x, 1 - y), (1 - x, 1 - y)]

    def _rows(self, a, p):
        rows = self.x_refs[a].shape[0] // self.pieces[a]
        return pl.ds(p * rows, rows)

    def _cp(self, a, p, k, block, to, own=False):
        dst = self.o_refs[a].at[4 * block[0] + 2 * block[1] + block[2], self._rows(a, p)]
        sem = self.base[a] + 7 * p + k
        return pltpu.make_async_remote_copy(
            src_ref=self.x_refs[a].at[self._rows(a, p)] if own else dst, dst_ref=dst,
            send_sem=self.send.at[sem], recv_sem=self.recv.at[sem], device_id=to, device_id_type=MESH)

    def _mine(self, a):
        me = self.me
        return pltpu.make_async_copy(self.x_refs[a], self.o_refs[a].at[4 * me[0] + 2 * me[1] + me[2]], self.loc.at[a])

    def _first(self, a, p):
        cps = [self._cp(a, p, 0, self.me, self.sib, own=True)]
        return cps + [self._cp(a, p, 1 + j, self.me, (*chip, self.c), own=True) for j, chip in enumerate(self.chips)]

    def _passed(self, a, p, j):
        return self._cp(a, p, 4 + j, (*self.chips[j], self.c), self.sib)

    def start(self):
        for a in range(self.n):
            self._mine(a).start()
        for p in range(max(self.pieces)):
            for a in range(self.n):
                if p < self.pieces[a]:
                    for cp in self._first(a, p):
                        cp.start()

    def forward(self, p):
        for j, chip in enumerate(self.chips):
            for a in range(self.n):
                if p < self.pieces[a]:
                    self._cp(a, p, 1 + j, (*chip, self.c), self.me).wait_recv()
                    self._passed(a, p, j).start()

    def finish(self):
        for a in range(self.n):
            for p in range(self.pieces[a]):
                self._cp(a, p, 0, self.sib, self.me).wait_recv()
                for j, chip in enumerate(self.chips):
                    self._cp(a, p, 4 + j, (*chip, 1 - self.c), self.me).wait_recv()
        for a in range(self.n):
            for p in range(self.pieces[a]):
                for cp in self._first(a, p):
                    cp.wait_send()
                for j in range(3):
                    self._passed(a, p, j).wait_send()
            self._mine(a).wait()

    def begin_hosted(self, step, steps):
        @pl.when(step == 0)
        def _():
            self.start()

        n_p = max(self.pieces)
        for p in range(n_p):
            @pl.when(step == min(((p + 1) * 7 * steps) // (8 * n_p), steps - 1))
            def _():
                self.forward(p)

    def end_hosted(self, step, steps):
        @pl.when(step == steps - 1)
        def _():
            self.finish()


class _Exchange:
    def __init__(self, x_refs, o_refs, send, recv, loc):
        self.x_refs, self.o_refs, self.send, self.recv, self.loc = x_refs, o_refs, send, recv, loc
        self.n = len(x_refs)
        x, y, c = _pos()
        self.me_i = 4 * x + 2 * y + c
        self.peers = []
        for k in range(1, N_DEV):
            px = 1 - x if (k >> 2) & 1 else x
            py = 1 - y if (k >> 1) & 1 else y
            pc = 1 - c if k & 1 else c
            self.peers.append(((px, py, pc), 4 * px + 2 * py + pc))

    def _mine(self, a):
        return pltpu.make_async_copy(self.x_refs[a].at[self.me_i], self.o_refs[a].at[self.me_i], self.loc.at[a])

    def _cp(self, a, k, landing):
        peer, peer_i = self.peers[k]
        return pltpu.make_async_remote_copy(
            src_ref=self.x_refs[a].at[peer_i], dst_ref=self.o_refs[a].at[landing],
            send_sem=self.send.at[a * 7 + k], recv_sem=self.recv.at[a * 7 + k],
            device_id=peer, device_id_type=MESH)

    def start(self):
        for a in range(self.n):
            self._mine(a).start()
            for k in range(N_DEV - 1):
                self._cp(a, k, self.me_i).start()

    def finish(self):
        for a in range(self.n):
            for k in range(N_DEV - 1):
                self._cp(a, k, self.peers[k][1]).wait_recv()
        for a in range(self.n):
            for k in range(N_DEV - 1):
                self._cp(a, k, self.me_i).wait_send()
            self._mine(a).wait()


def _gather_scratch(xs):
    n_sem = 7 * sum(_pieces(v.shape, v.dtype) for v in xs)
    return [pltpu.SemaphoreType.DMA((n_sem,)), pltpu.SemaphoreType.DMA((n_sem,)), pltpu.SemaphoreType.DMA((len(xs),))]


ANY_SPEC = pl.BlockSpec(memory_space=pl.ANY)


def _all_gather(xs, name, after=()):
    n, na = len(xs), len(after)

    def body(*refs):
        g = _Gather(refs[:n], refs[n + na:2 * n + na], *refs[2 * n + na:])
        g.start()
        for p in range(max(g.pieces)):
            g.forward(p)
        g.finish()

    return pl.pallas_call(
        body, name=name,
        out_shape=[jax.ShapeDtypeStruct((N_DEV,) + v.shape, v.dtype) for v in xs],
        in_specs=[ANY_SPEC] * (n + na), out_specs=[ANY_SPEC] * n, scratch_shapes=_gather_scratch(xs),
    )(*xs, *after)


HBM_SPEC = pl.BlockSpec(memory_space=pltpu.HBM)
SEM_SPEC = pl.BlockSpec(memory_space=pltpu.SEMAPHORE)
VMEM_SPEC = pl.BlockSpec(memory_space=pltpu.VMEM)
SPLIT_EFFECT = pltpu.SideEffectType.DATAFLOW_SIDE_EFFECTING


def _in_hbm(v):
    return pltpu.with_memory_space_constraint(v, pltpu.HBM)


def _exchange_start(blocks, name):
    n = len(blocks)

    def body(*refs):
        x_refs, land_refs = refs[:n], refs[n:2 * n]
        send, recv = refs[2 * n:2 * n + 2]
        token = refs[-1]
        ex = _Exchange(x_refs, land_refs, send, recv, None)
        for a in range(n):
            for k in range(N_DEV - 1):
                ex._cp(a, k, ex.me_i).start()
        token[...] = jnp.zeros_like(token)

    lands = [lax.empty(v.shape, v.dtype) for v in blocks]
    hbm = tuple(pltpu.HBM(v.shape, v.dtype) for v in list(blocks) + lands)
    n_sem = (N_DEV - 1) * n
    out = pl.pallas_call(
        body, name=name,
        out_shape=(pltpu.SemaphoreType.DMA((n_sem,)), pltpu.SemaphoreType.DMA((n_sem,))) + hbm
        + (jax.ShapeDtypeStruct((8, LANES), F32),),
        in_specs=(HBM_SPEC,) * (2 * n), out_specs=(SEM_SPEC, SEM_SPEC) + (HBM_SPEC,) * (2 * n) + (VMEM_SPEC,),
        input_output_aliases={i: i + 2 for i in range(2 * n)},
        compiler_params=pltpu.CompilerParams(has_side_effects=SPLIT_EFFECT),
    )(*[_in_hbm(v) for v in list(blocks) + lands])
    return out[0], out[1], list(out[2:2 + n]), list(out[2 + n:2 + 2 * n]), out[-1]


def _exchange_wait(ex, after, name):
    send, recv, thru, lands, _ = ex
    n = len(thru)

    def body(*refs):
        x_refs, land_refs = refs[:n], refs[n:2 * n]
        send_ref, recv_ref = refs[2 * n:2 * n + 2]
        e = _Exchange(x_refs, land_refs, send_ref, recv_ref, None)
        for a in range(n):
            for k in range(N_DEV - 1):
                e._cp(a, k, e.me_i).wait_send()
                e._cp(a, k, e.peers[k][1]).wait_recv()

    hbm = tuple(pltpu.HBM(v.shape, v.dtype) for v in list(thru) + list(lands))
    out = pl.pallas_call(
        body, name=name, out_shape=hbm,
        in_specs=(HBM_SPEC,) * (2 * n) + (SEM_SPEC, SEM_SPEC, ANY_SPEC), out_specs=(HBM_SPEC,) * (2 * n),
        input_output_aliases={i: i for i in range(2 * n)},
        compiler_params=pltpu.CompilerParams(has_side_effects=SPLIT_EFFECT),
    )(*thru, *lands, send, recv, after)
    return list(zip(out[:n], out[n:]))


def _ada_fwd(c_all, w_ada, b_slice):
    def body(c_ref, w_ref, b_ref, o_ref):
        cv = c_ref[...]
        act = cv * _sigmoid(cv)
        o_ref[...] = _mm(act, w_ref[...]) + b_ref[...]

    nb, nc = c_all.shape[0], w_ada.shape[1]
    return pl.pallas_call(body, name="ada_fwd", out_shape=jax.ShapeDtypeStruct((nb, nc), F32),
                          compiler_params=_cparams())(c_all, w_ada, b_slice)


def _adam_math(w, g, m, v):
    m = B1 * m + (1.0 - B1) * g
    v = B2 * v + (1.0 - B2) * jnp.square(g)
    m_hat = m / (1.0 - B1 ** STEP)
    v_hat = v / (1.0 - B2 ** STEP)
    delta = -LR * (m_hat / (jnp.sqrt(v_hat) + AEPS) + WD * w)
    return delta, m, v


def _ada_bwd_adam(c_all, dmod_slice, w, m, v):
    rows, cols = w.shape
    br = 256

    def body(c_ref, d_ref, w_ref, m_ref, v_ref, g_out, dl_out, m_out, v_out):
        cv = c_ref[...]
        act = cv * _sigmoid(cv)
        g = _mm_tn(act, d_ref[...])
        g_out[...] = g
        dl, mn, vn = _adam_math(w_ref[...], g, m_ref[...], v_ref[...])
        dl_out[...] = dl
        m_out[...] = mn
        v_out[...] = vn

    nb = c_all.shape[0]
    wspec = pl.BlockSpec((br, cols), lambda i: (i, 0))
    return pl.pallas_call(
        body, name="ada_bwd_adam", grid=(rows // br,),
        in_specs=[pl.BlockSpec((nb, br), lambda i: (0, i)), pl.BlockSpec((nb, cols), lambda i: (0, 0)),
                  wspec, wspec, wspec],
        out_specs=[wspec] * 4, out_shape=[jax.ShapeDtypeStruct((rows, cols), F32)] * 4,
        compiler_params=_cparams(),
    )(c_all, dmod_slice, w, m, v)


def _mix_in(x2, mod, g_mix, win_g, in_cols, seq, shards):
    T = x2.shape[0]
    tm = min(512, seq)
    tps = seq // tm
    blk_rows = win_g.shape[1]
    pad_rows = -blk_rows % (4 * LANES)
    ns = len(shards)
    steps = T // tm

    def body(*refs):
        x_ref, mod_ref, g_ref, wb_ref = refs[:4]
        sh_refs = refs[4:4 + ns]
        u_ref, pm_ref, wc_ref = refs[4 + ns:7 + ns]
        ga_refs = refs[7 + ns:7 + 2 * ns]
        w_ref, send, recv, loc = refs[7 + 2 * ns:]
        step = pl.program_id(0)
        gather = _Gather(sh_refs, ga_refs, send, recv, loc)
        gather.begin_hosted(step, steps)

        @pl.when(step == 0)
        def _():
            w_ref[:, OFF_DT:] = jnp.zeros((D, PROJ_W - OFF_DT), BF16)
            for j in range(N_DEV):
                blk = jnp.concatenate([wb_ref[j], jnp.zeros((pad_rows, D), BF16)], axis=0)
                w_ref[:, in_cols * j:in_cols * (j + 1)] = blk.T[:, :in_cols]
            wc_ref[...] = w_ref[...]

        x = x_ref[...]
        r = lax.rsqrt(jnp.mean(x * x, axis=-1, keepdims=True) + EPS)
        md = mod_ref[0]
        u = (x * r * g_ref[...]) * (1.0 + md[1:2]) + md[0:1]
        ub = u.astype(BF16)
        u_ref[...] = ub
        pm_ref[...] = jnp.dot(ub, w_ref[...], preferred_element_type=F32)
        gather.end_hosted(step, steps)

    whole = pl.BlockSpec(memory_space=pltpu.VMEM)
    return pl.pallas_call(
        body, name="mix_in", grid=(T // tm,),
        in_specs=[pl.BlockSpec((tm, D), lambda i: (i, 0)), pl.BlockSpec((1, 8, D), lambda i: (i // tps, 0, 0)),
                  pl.BlockSpec((1, D), lambda i: (0, 0)), whole] + [ANY_SPEC] * ns,
        out_specs=[pl.BlockSpec((tm, D), lambda i: (i, 0)), pl.BlockSpec((tm, PROJ_W), lambda i: (i, 0)),
                   pl.BlockSpec((D, PROJ_W), lambda i: (0, 0))] + [ANY_SPEC] * ns,
        out_shape=[jax.ShapeDtypeStruct((T, D), BF16), jax.ShapeDtypeStruct((T, PROJ_W), F32),
                   jax.ShapeDtypeStruct((D, PROJ_W), BF16)]
        + [jax.ShapeDtypeStruct((N_DEV,) + v.shape, v.dtype) for v in shards],
        scratch_shapes=[pltpu.VMEM((D, PROJ_W), BF16)] + _gather_scratch(shards),
        compiler_params=_cparams(),
    )(x2, mod, g_mix, win_g, *shards)


def _chunk_forward(up, z, ux, dtin, halo_p, halo_x, hprev, cw, cb, hp, gssd, wpool, pscale, t0, y_scr, cv=None):
    L = CHUNK
    out = {}
    row = lax.broadcasted_iota(jnp.int32, (L, 1), 0)
    t = (t0 + row + 1).astype(F32)
    e = jnp.concatenate([halo_p, up], axis=0)
    s2 = e + pltpu.roll(e, 1, 0)
    s4 = s2 + pltpu.roll(s2, 2, 0)
    s8 = s4 + pltpu.roll(s4, 4, 0)
    s16 = s8 + pltpu.roll(s8, 8, 0)
    sums = (s2, s4, s8, s16)
    p, inv, yp = [], [], []
    for gi, w in enumerate(WINDOWS):
        sl = slice(gi * LANES, (gi + 1) * LANES)
        ic = 1.0 / jnp.minimum(t, float(w))
        pg = sums[gi][POOL_HALO:, sl] * ic - up[:, sl]
        p.append(pg)
        inv.append(ic)
        yp.append(_mm(pg, wpool[gi]))
    out["p"], out["inv"], out["yp"] = p, inv, yp
    out["y_pool"] = jnp.concatenate(yp, axis=1) * pscale
    if cv is None:
        ex = jnp.concatenate([halo_x, ux], axis=0)
        taps = [pltpu.roll(ex, 3, 0)[CONV_HALO:], pltpu.roll(ex, 2, 0)[CONV_HALO:], pltpu.roll(ex, 1, 0)[CONV_HALO:], ux]
        cv = cb + taps[0] * cw[0:1] + taps[1] * cw[1:2] + taps[2] * cw[2:3] + taps[3] * cw[3:4]
    sg = _sigmoid(cv)
    xbc = cv * sg
    out["cv"], out["sg"] = cv, sg
    X = xbc[:, :D]
    Bm = xbc[:, D:D + N_GROUPS * N_STATE]
    Cm = xbc[:, D + N_GROUPS * N_STATE:]
    pre = dtin + hp[0:1]
    dt = jnp.maximum(pre, 0.0) + jnp.log(1.0 + jnp.exp(-jnp.abs(pre)))
    a_row = -jnp.exp(hp[1:2])
    da = dt * a_row
    ri = lax.broadcasted_iota(jnp.int32, (L, L), 0)
    ci = lax.broadcasted_iota(jnp.int32, (L, L), 1)
    causal = ri >= ci
    cum = _dot01(causal.astype(F32), da, 3, split_lhs=False)
    cum_t = cum.T
    cum_last = cum[L - 1:L]
    eo = jnp.exp(cum)
    dec = jnp.exp(cum_last - cum)
    cd = jnp.exp(cum_last)
    exm = _expand_mat()
    rows8 = jnp.concatenate([cd, hp[2:3], jnp.zeros((6, LANES), F32)], axis=0)
    rep = _dot01(jnp.concatenate([dt, eo, dec, rows8], axis=0), exm, 2)
    dt_rep, eo_rep, dec_rep = rep[0:L], rep[L:2 * L], rep[2 * L:3 * L]
    cd_rep, dskip_rep = rep[3 * L:3 * L + 1], rep[3 * L + 1:3 * L + 2]
    xdt = X * dt_rep
    out.update(X=X, Bm=Bm, Cm=Cm, pre=pre, dt=dt, a_row=a_row, cum=cum, cum_t=cum_t, eo=eo, dec=dec, cd=cd,
               dt_rep=dt_rep, eo_rep=eo_rep, dec_rep=dec_rep, cd_rep=cd_rep, dskip_rep=dskip_rep, xdt=xdt,
               causal=causal, anti=(ri <= ci).astype(F32), exm=exm)
    G, lms, yoff, hnew, xdec = [], [], [], [], []
    for g in range(N_GROUPS):
        gs = slice(g * GROUP_W, (g + 1) * GROUP_W)
        Bg = Bm[:, g * N_STATE:(g + 1) * N_STATE]
        Cg = Cm[:, g * N_STATE:(g + 1) * N_STATE]
        Gg = _mm_nt(Cg, Bg)
        G.append(Gg)
        for hh in range(N_HEADS // N_GROUPS):
            h = g * (N_HEADS // N_GROUPS) + hh
            seg = cum[:, h:h + 1] - cum_t[h:h + 1, :]
            lm = jnp.where(causal, jnp.exp(jnp.minimum(seg, 0.0)), 0.0)
            lms.append(lm)
            hs = slice(h * HEAD_DIM, (h + 1) * HEAD_DIM)
            y_scr[:, hs] = _mm(Gg * lm, xdt[:, hs])
        xd = xdt[:, gs] * dec_rep[:, gs]
        xdec.append(xd)
        sgm = _mm_tn(Bg, xd)
        yoff.append(_mm(Cg, hprev[g]) * eo_rep[:, gs])
        hnew.append(hprev[g] * cd_rep[:, gs] + sgm)
    out.update(G=G, lms=lms, yoff=yoff, hnew=hnew, xdec=xdec)
    y = y_scr[...] + jnp.concatenate(yoff, axis=1) + dskip_rep * X
    sz = _sigmoid(z)
    silz = z * sz
    yz = y * silz
    rg, yn = [], []
    for g in range(N_GROUPS):
        gs = slice(g * GROUP_W, (g + 1) * GROUP_W)
        r = lax.rsqrt(jnp.mean(yz[:, gs] * yz[:, gs], axis=-1, keepdims=True) + EPS)
        rg.append(r)
        yn.append(yz[:, gs] * r)
    yn = jnp.concatenate(yn, axis=1)
    out.update(y=y, sz=sz, silz=silz, rg=rg, yn=yn)
    out["y_ssd"] = yn * gssd
    return out


def _mixer_fwd(pm, cw, cb, hp, gssd, wpool, pscale, nb, seq, shards):
    nc = seq // CHUNK
    ns = len(shards)
    steps = nb * nc

    def body(*refs):
        pm_ref, cw_ref, cb_ref, hp_ref, gs_ref, wp_ref, ps_ref = refs[:7]
        sh_refs = refs[7:7 + ns]
        ym_ref, hs_ref, cv_ref = refs[7 + ns:10 + ns]
        ga_refs = refs[10 + ns:10 + 2 * ns]
        halo_p, halo_x, state, y_scr, send, recv, loc = refs[10 + 2 * ns:]
        c = pl.program_id(1)
        step = pl.program_id(0) * nc + c
        gather = _Gather(sh_refs, ga_refs, send, recv, loc)
        gather.begin_hosted(step, steps)

        @pl.when(c == 0)
        def _():
            halo_p[...] = jnp.zeros_like(halo_p)
            halo_x[...] = jnp.zeros_like(halo_x)
            state[...] = jnp.zeros_like(state)

        up = pm_ref[:, 0:POOL_W]
        z = pm_ref[:, OFF_Z:OFF_XBC]
        ux = pm_ref[:, OFF_XBC:OFF_DT]
        hprev = [state[0], state[1]]
        hs_ref[0, 0, 0] = hprev[0]
        hs_ref[0, 0, 1] = hprev[1]
        o = _chunk_forward(up, z, ux, pm_ref[:, OFF_DT:], halo_p[...], halo_x[...], hprev, cw_ref[...], cb_ref[...],
                           hp_ref[...], gs_ref[...], wp_ref[...], ps_ref[...], c * CHUNK, y_scr)
        ym_ref[:, 0:POOL_W] = o["y_pool"].astype(BF16)
        ym_ref[:, POOL_W:] = o["y_ssd"].astype(BF16)
        cv_ref[...] = o["cv"]
        state[0] = o["hnew"][0]
        state[1] = o["hnew"][1]
        halo_p[...] = up[CHUNK - POOL_HALO:]
        halo_x[...] = ux[CHUNK - CONV_HALO:]
        gather.end_hosted(step, steps)

    def full(shape):
        return pl.BlockSpec(shape, lambda b, c: (0,) * len(shape))

    T = nb * seq
    return pl.pallas_call(
        body, name="mixer_fwd", grid=(nb, nc),
        in_specs=[pl.BlockSpec((CHUNK, PROJ_W), lambda b, c: (b * nc + c, 0)),
                  full((4, CONV_CH)), full((1, CONV_CH)), full((8, LANES)), full((1, D)),
                  full((4, LANES, LANES)), full((1, POOL_W))] + [ANY_SPEC] * ns,
        out_specs=[pl.BlockSpec((CHUNK, MIX_W), lambda b, c: (b * nc + c, 0)),
                   pl.BlockSpec((1, 1, N_GROUPS, N_STATE, GROUP_W), lambda b, c: (b, c, 0, 0, 0)),
                   pl.BlockSpec((CHUNK, CONV_CH), lambda b, c: (b * nc + c, 0))] + [ANY_SPEC] * ns,
        out_shape=[jax.ShapeDtypeStruct((T, MIX_W), BF16),
                   jax.ShapeDtypeStruct((nb, nc, N_GROUPS, N_STATE, GROUP_W), F32),
                   jax.ShapeDtypeStruct((T, CONV_CH), F32)]
        + [jax.ShapeDtypeStruct((N_DEV,) + v.shape, v.dtype) for v in shards],
        scratch_shapes=[pltpu.VMEM((POOL_HALO, POOL_W), F32), pltpu.VMEM((CONV_HALO, CONV_CH), F32),
                        pltpu.VMEM((N_GROUPS, N_STATE, GROUP_W), F32), pltpu.VMEM((CHUNK, D), F32)] + _gather_scratch(shards),
        compiler_params=_cparams(),
    )(pm, cw, cb, hp, gssd, wpool, pscale, *shards)


def _mixer_bwd(pm, cvs, dym, hstates, cw, cb, hp, gssd, wpool, pscale, nb, seq, after=()):
    nc = seq // CHUNK
    hpg = N_HEADS // N_GROUPS
    na = len(after)

    def body(*refs):
        (pm_ref, hpool_ref, cv_ref, dy_ref, hs_ref, cw_ref, cb_ref, hp_ref, gs_ref, wp_ref, ps_ref) = refs[:11]
        dpm_ref, dconv_ref, dhp_ref, dvec_ref, dwp_ref = refs[11 + na:16 + na]
        nxt_q, nxt_cv, rstate, y_scr, dx_scr = refs[16 + na:]
        b = pl.program_id(0)
        ci = pl.program_id(1)
        c = nc - 1 - ci

        @pl.when((b == 0) & (ci == 0))
        def _():
            for r in (dconv_ref, dhp_ref, dvec_ref, dwp_ref):
                r[...] = jnp.zeros_like(r)

        @pl.when(ci == 0)
        def _():
            nxt_q[...] = jnp.zeros_like(nxt_q)
            nxt_cv[...] = jnp.zeros_like(nxt_cv)
            rstate[...] = jnp.zeros_like(rstate)

        first = (c > 0).astype(F32)
        up = pm_ref[:, 0:POOL_W]
        z = pm_ref[:, OFF_Z:OFF_XBC]
        ux = pm_ref[:, OFF_XBC:OFF_DT]
        halo_p = hpool_ref[...] * first
        hprev = [hs_ref[0, 0, 0], hs_ref[0, 0, 1]]
        cw, cb, hp, gssd, wpool, pscale = cw_ref[...], cb_ref[...], hp_ref[...], gs_ref[...], wp_ref[...], ps_ref[...]
        o = _chunk_forward(up, z, ux, pm_ref[:, OFF_DT:], halo_p, None, hprev, cw, cb, hp, gssd, wpool, pscale,
                           c * CHUNK, y_scr, cv=cv_ref[...])
        L = CHUNK
        dy_pool = dy_ref[:, 0:POOL_W].astype(F32)
        dy_ssd = dy_ref[:, POOL_W:].astype(F32)

        dvec_ref[1:2, 0:POOL_W] += jnp.sum(dy_pool * jnp.concatenate(o["yp"], axis=1), axis=0, keepdims=True)
        dyp = dy_pool * pscale
        qs = []
        dps = []
        for gi in range(len(WINDOWS)):
            sl = slice(gi * LANES, (gi + 1) * LANES)
            dwp_ref[gi] += _mm_tn(o["p"][gi], dyp[:, sl])
            dpg = _mm_nt(dyp[:, sl], wpool[gi])
            dps.append(dpg)
            qs.append(dpg * o["inv"][gi])
        q = jnp.concatenate(qs, axis=1)
        e = jnp.concatenate([q, nxt_q[...]], axis=0)
        n = L + POOL_HALO
        s2 = e + pltpu.roll(e, n - 1, 0)
        s4 = s2 + pltpu.roll(s2, n - 2, 0)
        s8 = s4 + pltpu.roll(s4, n - 4, 0)
        s16 = s8 + pltpu.roll(s8, n - 8, 0)
        sums = (s2, s4, s8, s16)
        for gi in range(len(WINDOWS)):
            sl = slice(gi * LANES, (gi + 1) * LANES)
            dpm_ref[:, sl] = (sums[gi][:L, sl] - dps[gi]).astype(BF16)
        nxt_q[...] = q[:POOL_HALO]

        yn, y, silz, sz = o["yn"], o["y"], o["silz"], o["sz"]
        dvec_ref[0:1] += jnp.sum(dy_ssd * yn, axis=0, keepdims=True)
        dyn = dy_ssd * gssd
        dyz = []
        for g in range(N_GROUPS):
            gs = slice(g * GROUP_W, (g + 1) * GROUP_W)
            mean = jnp.mean(dyn[:, gs] * yn[:, gs], axis=-1, keepdims=True)
            dyz.append(o["rg"][g] * (dyn[:, gs] - yn[:, gs] * mean))
        dyz = jnp.concatenate(dyz, axis=1)
        dyv = dyz * silz
        dpm_ref[:, OFF_Z:OFF_XBC] = (dyz * y * (sz * (1.0 + z * (1.0 - sz)))).astype(BF16)

        X, Bm, Cm, xdt = o["X"], o["Bm"], o["Cm"], o["xdt"]
        exm = o["exm"]
        rdm = _reduce_mat()
        lane = lax.broadcasted_iota(jnp.int32, (1, LANES), 1)
        sub = lax.broadcasted_iota(jnp.int32, (LANES, 1), 0)
        dX = o["dskip_rep"] * dyv
        yoff_full = jnp.concatenate(o["yoff"], axis=1)
        rs = jnp.zeros((L, LANES), F32)
        cs_t = jnp.zeros((LANES, L), F32)
        dBs, dCs = [], []
        rh_sums = []
        ddec = []
        for g in range(N_GROUPS):
            gs = slice(g * GROUP_W, (g + 1) * GROUP_W)
            Bg = Bm[:, g * N_STATE:(g + 1) * N_STATE]
            Cg = Cm[:, g * N_STATE:(g + 1) * N_STATE]
            Gg = o["G"][g]
            R = rstate[g]
            dwm = dyv[:, gs] * o["eo_rep"][:, gs]
            dC = _mm_nt(dwm, hprev[g])
            dH = _mm_tn(Cg, dwm)
            dG = jnp.zeros((L, L), F32)
            for hh in range(hpg):
                h = g * hpg + hh
                hs = slice(h * HEAD_DIM, (h + 1) * HEAD_DIM)
                lm = o["lms"][h]
                m_h = Gg * lm
                dM = _mm_nt(dyv[:, hs], xdt[:, hs])
                dx_scr[:, hs] = _mm_tn(m_h, dyv[:, hs])
                qm = dM * m_h
                rs = rs + jnp.sum(qm, axis=1, keepdims=True) * (lane == h).astype(F32)
                cs_t = cs_t + (sub == h).astype(F32) * jnp.sum(qm, axis=0, keepdims=True)
                dG = dG + dM * lm
            dC = dC + _mm(dG, Bg)
            dB = _mm_tn(dG, Cg)
            zx = _mm(Bg, R)
            dxdt_state = zx * o["dec_rep"][:, gs]
            ddec.append(zx * xdt[:, gs])
            dB = dB + _mm_nt(o["xdec"][g], R)
            rh_sums.append(jnp.sum(R * hprev[g], axis=0, keepdims=True))
            rstate[g] = dH + o["cd_rep"][:, gs] * R
            dx_scr[:, gs] = dx_scr[:, gs] + dxdt_state
            dBs.append(dB)
            dCs.append(dC)
        dxdt = dx_scr[...]
        tail = jnp.concatenate([jnp.sum(dyv * X, axis=0, keepdims=True), jnp.concatenate(rh_sums, axis=1),
                                jnp.zeros((6, D), F32)], axis=0)
        red = _dot01(jnp.concatenate([dyv * yoff_full, jnp.concatenate(ddec, axis=1), dxdt * X, tail], axis=0), rdm, 2)
        d_dskip, dcd_row = red[3 * L:3 * L + 1], red[3 * L + 1:3 * L + 2]
        ddec_h = red[L:2 * L] * o["dec"]
        dcum_last = jnp.sum(ddec_h, axis=0, keepdims=True) + dcd_row * o["cd"]
        dcum = red[0:L] + rs - cs_t.T - ddec_h + (sub == L - 1).astype(F32) * dcum_last
        dda = _dot01(o["anti"], dcum, 3, split_lhs=False)
        ddt_v = dda * o["a_row"] + red[2 * L:3 * L]
        dX = dX + dxdt * o["dt_rep"]
        head_mask = (lane < N_HEADS).astype(F32)
        d_alog = jnp.sum(dda * o["dt"], axis=0, keepdims=True) * o["a_row"] * head_mask
        dpre = ddt_v * _sigmoid(o["pre"]) * head_mask
        dpm_ref[:, OFF_DT:] = dpre.astype(BF16)
        d_dtb = jnp.sum(dpre, axis=0, keepdims=True)
        dhp_ref[...] += jnp.concatenate([d_dtb, d_alog, d_dskip * head_mask, jnp.zeros((5, LANES), F32)], axis=0)

        dxbc = jnp.concatenate([dX] + dBs + dCs, axis=1)
        sg, cv = o["sg"], o["cv"]
        dcv = dxbc * (sg * (1.0 + cv * (1.0 - sg)))
        e2 = jnp.concatenate([dcv, nxt_cv[...]], axis=0)
        n2 = L + CONV_HALO
        ahead = [dcv, pltpu.roll(e2, n2 - 1, 0)[:L], pltpu.roll(e2, n2 - 2, 0)[:L], pltpu.roll(e2, n2 - 3, 0)[:L]]
        dconv_ref[0:5] += jnp.concatenate(
            [jnp.sum(ux * ahead[3 - k], axis=0, keepdims=True) for k in range(4)]
            + [jnp.sum(dcv, axis=0, keepdims=True)], axis=0)
        dux = ahead[0] * cw[3:4] + ahead[1] * cw[2:3] + ahead[2] * cw[1:2] + ahead[3] * cw[0:1]
        dpm_ref[:, OFF_XBC:OFF_DT] = dux.astype(BF16)
        nxt_cv[...] = dcv[:CONV_HALO]

    def full(shape):
        return pl.BlockSpec(shape, lambda b, c: (0,) * len(shape))

    def rowblk(b, c):
        return b * nc + (nc - 1 - c)

    hp_blocks = CHUNK // POOL_HALO
    T = nb * seq
    return pl.pallas_call(
        body, name="mixer_bwd", grid=(nb, nc),
        in_specs=[pl.BlockSpec((CHUNK, PROJ_W), lambda b, c: (rowblk(b, c), 0)),
                  pl.BlockSpec((POOL_HALO, POOL_W), lambda b, c: (jnp.maximum(rowblk(b, c) * hp_blocks - 1, 0), 0)),
                  pl.BlockSpec((CHUNK, CONV_CH), lambda b, c: (rowblk(b, c), 0)),
                  pl.BlockSpec((CHUNK, MIX_W), lambda b, c: (rowblk(b, c), 0)),
                  pl.BlockSpec((1, 1, N_GROUPS, N_STATE, GROUP_W), lambda b, c: (b, nc - 1 - c, 0, 0, 0)),
                  full((4, CONV_CH)), full((1, CONV_CH)), full((8, LANES)), full((1, D)),
                  full((4, LANES, LANES)), full((1, POOL_W))] + [ANY_SPEC] * na,
        out_specs=[pl.BlockSpec((CHUNK, PROJ_W), lambda b, c: (rowblk(b, c), 0)),
                   full((8, CONV_CH)), full((8, LANES)), full((8, D)), full((4, LANES, LANES))],
        out_shape=[jax.ShapeDtypeStruct((T, PROJ_W), BF16),
                   jax.ShapeDtypeStruct((8, CONV_CH), F32), jax.ShapeDtypeStruct((8, LANES), F32),
                   jax.ShapeDtypeStruct((8, D), F32), jax.ShapeDtypeStruct((4, LANES, LANES), F32)],
        scratch_shapes=[pltpu.VMEM((POOL_HALO, POOL_W), F32), pltpu.VMEM((CONV_HALO, CONV_CH), F32),
                        pltpu.VMEM((N_GROUPS, N_STATE, GROUP_W), F32), pltpu.VMEM((CHUNK, D), F32),
                        pltpu.VMEM((CHUNK, D), F32)],
        compiler_params=_cparams(),
    )(pm, pm, cvs, dym, hstates, cw, cb, hp, gssd, wpool, pscale, *after)


def _mlp_fused(x2, ymix, target, mod, g_mlp, g_final, w_out, w_up, w_down, seq):
    T = x2.shape[0]
    tm = min(256, seq)
    tps = seq // tm
    nblk = D_FF // FF_BLK

    def body(x_ref, ym_ref, tg_ref, mod_ref, gm_ref, gf_ref, wo_ref, wu_ref, wd_ref,
             da_ref, dym_ref, dh1_ref, u2_ref, f_ref, dup_ref, ddn_ref, dmod_ref, acc_ref, relu_scr):
        i = pl.program_id(0)

        @pl.when(i == 0)
        def _():
            acc_ref[...] = jnp.zeros_like(acc_ref)

        @pl.when(i % tps == 0)
        def _():
            dmod_ref[...] = jnp.zeros_like(dmod_ref)

        md = mod_ref[0]
        gate_m, shift_f, scale_f, gate_f = md[2:3], md[3:4], md[4:5], md[5:6]
        g_mlp, g_fin = gm_ref[...], gf_ref[...]
        a = jnp.dot(ym_ref[...], wo_ref[...], preferred_element_type=F32)
        h1 = x_ref[...] + gate_m * a
        r2 = lax.rsqrt(jnp.mean(h1 * h1, axis=-1, keepdims=True) + EPS)
        n2 = h1 * r2
        u2 = (n2 * g_mlp) * (1.0 + scale_f) + shift_f
        u2b = u2.astype(BF16)
        u2_ref[...] = u2b
        dn = jnp.zeros((tm, D), F32)
        for j in range(nblk):
            js = slice(j * FF_BLK, (j + 1) * FF_BLK)
            upj = jnp.maximum(jnp.dot(u2b, wu_ref[j], preferred_element_type=F32), 0.0)
            relu_scr[:, js] = upj
            fj = (upj * upj).astype(BF16)
            f_ref[:, js] = fj
            dn = dn + jnp.dot(fj, wd_ref[j], preferred_element_type=F32)
        h2 = h1 + gate_f * dn
        r3 = lax.rsqrt(jnp.mean(h2 * h2, axis=-1, keepdims=True) + EPS)
        n3 = h2 * r3
        err = n3 * g_fin - tg_ref[...]
        loss = 0.5 * jnp.sum(jnp.mean(err * err, axis=-1, keepdims=True), axis=0, keepdims=True)
        dout = err * (1.0 / D)
        d_gfin = jnp.sum(dout * n3, axis=0, keepdims=True)
        dn3 = dout * g_fin
        dh2 = r3 * (dn3 - n3 * jnp.mean(dn3 * n3, axis=-1, keepdims=True))
        d_gate_f = jnp.sum(dh2 * dn, axis=0, keepdims=True)
        ddn = (gate_f * dh2).astype(BF16)
        ddn_ref[...] = ddn
        du2 = jnp.zeros((tm, D), F32)
        for j in range(nblk):
            js = slice(j * FF_BLK, (j + 1) * FF_BLK)
            dfj = lax.dot_general(ddn, wd_ref[j], (((1,), (1,)), ((), ())), preferred_element_type=F32)
            dupj = (dfj * (2.0 * relu_scr[:, js])).astype(BF16)
            dup_ref[:, js] = dupj
            du2 = du2 + lax.dot_general(dupj, wu_ref[j], (((1,), (1,)), ((), ())), preferred_element_type=F32)
        d_scale_f = jnp.sum(du2 * (n2 * g_mlp), axis=0, keepdims=True)
        d_shift_f = jnp.sum(du2, axis=0, keepdims=True)
        d_gmlp = jnp.sum(du2 * (1.0 + scale_f) * n2, axis=0, keepdims=True)
        dn2 = du2 * (g_mlp * (1.0 + scale_f))
        dh1 = dh2 + r2 * (dn2 - n2 * jnp.mean(dn2 * n2, axis=-1, keepdims=True))
        dh1_ref[...] = dh1
        d_gate_m = jnp.sum(dh1 * a, axis=0, keepdims=True)
        da = (gate_m * dh1).astype(BF16)
        da_ref[...] = da
        dym_ref[...] = lax.dot_general(da, wo_ref[...], (((1,), (1,)), ((), ())),
                                       preferred_element_type=F32).astype(BF16)
        dmod_ref[0] += jnp.concatenate([jnp.zeros((2, D), F32), d_gate_m, d_shift_f, d_scale_f, d_gate_f,
                                        jnp.zeros((2, D), F32)], axis=0)
        acc_ref[...] += jnp.concatenate([d_gmlp, d_gfin, loss * jnp.ones((1, D), F32), jnp.zeros((5, D), F32)], axis=0)

    whole = pl.BlockSpec(memory_space=pltpu.VMEM)

    def tok(w):
        return pl.BlockSpec((tm, w), lambda i: (i, 0))

    def vec():
        return pl.BlockSpec((1, D), lambda i: (0, 0))

    nb = T // seq
    return pl.pallas_call(
        body, name="mlp_fused", grid=(T // tm,),
        in_specs=[tok(D), tok(MIX_W), tok(D), pl.BlockSpec((1, 8, D), lambda i: (i // tps, 0, 0)), vec(), vec(),
                  whole, whole, whole],
        out_specs=[tok(D), tok(MIX_W), tok(D), tok(D), tok(D_FF), tok(D_FF), tok(D),
                   pl.BlockSpec((1, 8, D), lambda i: (i // tps, 0, 0)), pl.BlockSpec((8, D), lambda i: (0, 0))],
        out_shape=[jax.ShapeDtypeStruct((T, D), BF16), jax.ShapeDtypeStruct((T, MIX_W), BF16),
                   jax.ShapeDtypeStruct((T, D), F32), jax.ShapeDtypeStruct((T, D), BF16),
                   jax.ShapeDtypeStruct((T, D_FF), BF16), jax.ShapeDtypeStruct((T, D_FF), BF16),
                   jax.ShapeDtypeStruct((T, D), BF16), jax.ShapeDtypeStruct((nb, 8, D), F32),
                   jax.ShapeDtypeStruct((8, D), F32)],
        scratch_shapes=[pltpu.VMEM((tm, D_FF), F32)],
        compiler_params=_cparams(),
    )(x2, ymix, target, mod, g_mlp, g_final, w_out, w_up, w_down)


def _in_bwd(x2, dh1, dpb, mod, g_mix, w_cat, dmod_a, acc_a, seq):
    T = x2.shape[0]
    tm = min(1024, seq)
    tps = seq // tm
    steps = T // tm

    def body(x_ref, dh_ref, dpb_ref, mod_ref, g_ref, w_ref, dma_ref, acca_ref, dx_ref, dmod_ref, acc_ref):
        i = pl.program_id(0)

        @pl.when(i == 0)
        def _():
            acc_ref[...] = acca_ref[...]

        @pl.when(i % tps == 0)
        def _():
            dmod_ref[...] = dma_ref[...]

        du = lax.dot_general(dpb_ref[...], w_ref[...], (((1,), (1,)), ((), ())), preferred_element_type=F32)
        x = x_ref[...]
        md = mod_ref[0]
        g = g_ref[...]
        r = lax.rsqrt(jnp.mean(x * x, axis=-1, keepdims=True) + EPS)
        n1 = x * r
        d_scale = jnp.sum(du * (n1 * g), axis=0, keepdims=True)
        d_shift = jnp.sum(du, axis=0, keepdims=True)
        d_g = jnp.sum(du * (1.0 + md[1:2]) * n1, axis=0, keepdims=True)
        dn1 = du * (g * (1.0 + md[1:2]))
        dx_ref[...] = dh_ref[...] + r * (dn1 - n1 * jnp.mean(dn1 * n1, axis=-1, keepdims=True))
        dmod_ref[0] += jnp.concatenate([d_shift, d_scale, jnp.zeros((6, D), F32)], axis=0)
        acc_ref[...] += jnp.concatenate([jnp.zeros((3, D), F32), d_g, jnp.zeros((4, D), F32)], axis=0)

    whole = pl.BlockSpec(memory_space=pltpu.VMEM)
    nb = T // seq
    return pl.pallas_call(
        body, name="in_bwd", grid=(steps,),
        in_specs=[pl.BlockSpec((tm, D), lambda i: (i, 0)), pl.BlockSpec((tm, D), lambda i: (i, 0)),
                  pl.BlockSpec((tm, PROJ_W), lambda i: (i, 0)),
                  pl.BlockSpec((1, 8, D), lambda i: (i // tps, 0, 0)), pl.BlockSpec((1, D), lambda i: (0, 0)),
                  whole, pl.BlockSpec((1, 8, D), lambda i: (i // tps, 0, 0)), pl.BlockSpec((8, D), lambda i: (0, 0))],
        out_specs=[pl.BlockSpec((tm, D), lambda i: (i, 0)),
                   pl.BlockSpec((1, 8, D), lambda i: (i // tps, 0, 0)), pl.BlockSpec((8, D), lambda i: (0, 0))],
        out_shape=[jax.ShapeDtypeStruct((T, D), F32),
                   jax.ShapeDtypeStruct((nb, 8, D), F32), jax.ShapeDtypeStruct((8, D), F32)],
        compiler_params=_cparams(),
    )(x2, dh1, dpb, mod, g_mix, w_cat, dmod_a, acc_a)


def _dw_in(u_b, dpb, in_cols, shards):
    T = u_b.shape[0]
    bk = min(1024, T)
    nk = T // bk
    ns = len(shards)

    def body(*refs):
        u_ref, d_ref = refs[:2]
        sh_refs = refs[2:2 + ns]
        o_ref = refs[2 + ns]
        ga_refs = refs[3 + ns:3 + 2 * ns]
        acc, send, recv, loc = refs[3 + 2 * ns:]
        k = pl.program_id(0)
        gather = _Gather(sh_refs, ga_refs, send, recv, loc)
        gather.begin_hosted(k, nk)

        @pl.when(k == 0)
        def _():
            acc[...] = jnp.zeros_like(acc)

        acc[...] += jnp.dot(u_ref[...].T, d_ref[...], preferred_element_type=F32)

        @pl.when(k == nk - 1)
        def _():
            for j in range(N_DEV):
                o_ref[j] = acc[:, in_cols * j:in_cols * (j + 1)].astype(BF16)

        gather.end_hosted(k, nk)

    return pl.pallas_call(
        body, name="dw_in", grid=(nk,),
        in_specs=[pl.BlockSpec((bk, D), lambda k: (k, 0)), pl.BlockSpec((bk, PROJ_W), lambda k: (k, 0))]
        + [ANY_SPEC] * ns,
        out_specs=[pl.BlockSpec((N_DEV, D, in_cols), lambda k: (0, 0, 0))] + [ANY_SPEC] * ns,
        out_shape=[jax.ShapeDtypeStruct((N_DEV, D, in_cols), BF16)]
        + [jax.ShapeDtypeStruct((N_DEV,) + v.shape, v.dtype) for v in shards],
        scratch_shapes=[pltpu.VMEM((D, PROJ_W), F32)] + _gather_scratch(shards),
        compiler_params=_cparams(),
    )(u_b, dpb, *shards)


def _dw_blocks(a, b, name, by_rows, per_step=1, after=()):
    T, M = a.shape
    N = b.shape[1]
    bk = min(4096, T)
    nk = T // bk
    whole = pl.BlockSpec(memory_space=pltpu.VMEM)
    if by_rows:
        rows = M // N_DEV
        am = rows * per_step
        nblk = N_DEV // per_step
        a_spec, b_spec = pl.BlockSpec((bk, am), lambda i, k: (k, i)), whole
        out_blk, acc_shape = (per_step, rows, N), (am, N)
    else:
        cols = N // N_DEV
        nblk = N_DEV
        a_spec, b_spec = whole, pl.BlockSpec((bk, cols), lambda i, k: (k, i))
        out_blk, acc_shape = (1, M, cols), (M, cols)

    def body(a_ref, b_ref, *rest):
        o_ref, acc = rest[len(after):]
        k = pl.program_id(1)

        @pl.when(k == 0)
        def _():
            acc[...] = jnp.zeros_like(acc)

        tok = pl.ds(pl.multiple_of(k * bk, bk), bk)
        a_blk = a_ref[...] if by_rows else a_ref[tok, :]
        b_blk = b_ref[tok, :] if by_rows else b_ref[...]
        acc[...] += lax.dot_general(a_blk, b_blk, (((0,), (0,)), ((), ())), preferred_element_type=F32)

        @pl.when(k == nk - 1)
        def _():
            o_ref[...] = acc[...].reshape(out_blk).astype(BF16)

    return pl.pallas_call(
        body, name=name, grid=(nblk, nk), in_specs=[a_spec, b_spec] + [ANY_SPEC] * len(after),
        out_specs=pl.BlockSpec(out_blk, lambda i, k: (i, 0, 0)),
        out_shape=jax.ShapeDtypeStruct((N_DEV,) + out_blk[1:], BF16),
        scratch_shapes=[pltpu.VMEM(acc_shape, F32)],
        compiler_params=_cparams(),
    )(a, b, *after)


def _adam_parts(parts, w, m, v, name, me=None):
    rows, cols = w.shape
    br = rows
    for cand in range(rows, 15, -16):
        if rows % cand == 0 and cand * cols * 4 <= ADAM_BLOCK_BYTES:
            br = cand
            break

    if me is not None:
        sent, landed = parts

        def body_own(me_ref, own_ref, p_ref, w_ref, m_ref, v_ref, g_out, dl_out, m_out, v_out):
            g = None
            for k in range(N_DEV):
                blk = jnp.where(me_ref[0] == k, own_ref[0], p_ref[k]).astype(F32)
                g = blk if g is None else g + blk
            g_out[...] = g
            dl, mn, vn = _adam_math(w_ref[...], g, m_ref[...], v_ref[...])
            dl_out[...] = dl
            m_out[...] = mn
            v_out[...] = vn

        wspec = pl.BlockSpec((br, cols), lambda i, me_ref: (i, 0))
        grid_spec = pltpu.PrefetchScalarGridSpec(
            num_scalar_prefetch=1, grid=(rows // br,),
            in_specs=[pl.BlockSpec((1, br, cols), lambda i, me_ref: (me_ref[0], i, 0)),
                      pl.BlockSpec((N_DEV, br, cols), lambda i, me_ref: (0, i, 0)), wspec, wspec, wspec],
            out_specs=[wspec] * 4)
        return pl.pallas_call(
            body_own, name=name, grid_spec=grid_spec, out_shape=[jax.ShapeDtypeStruct((rows, cols), F32)] * 4,
            compiler_params=_cparams(),
        )(me.reshape(1).astype(jnp.int32), sent, landed, w, m, v)

    def body(p_ref, w_ref, m_ref, v_ref, g_out, dl_out, m_out, v_out):
        g = p_ref[0].astype(F32)
        for k in range(1, N_DEV):
            g = g + p_ref[k].astype(F32)
        g_out[...] = g
        dl, mn, vn = _adam_math(w_ref[...], g, m_ref[...], v_ref[...])
        dl_out[...] = dl
        m_out[...] = mn
        v_out[...] = vn

    wspec = pl.BlockSpec((br, cols), lambda i: (i, 0))
    return pl.pallas_call(
        body, name=name, grid=(rows // br,),
        in_specs=[pl.BlockSpec((N_DEV, br, cols), lambda i: (0, i, 0)), wspec, wspec, wspec],
        out_specs=[wspec] * 4, out_shape=[jax.ShapeDtypeStruct((rows, cols), F32)] * 4,
        compiler_params=_cparams(),
    )(parts, w, m, v)


def _adam_plain(g, w, m, v, name):
    def body(g_ref, w_ref, m_ref, v_ref, dl_out, m_out, v_out):
        dl, mn, vn = _adam_math(w_ref[...], g_ref[...], m_ref[...], v_ref[...])
        dl_out[...] = dl
        m_out[...] = mn
        v_out[...] = vn

    return pl.pallas_call(body, name=name, out_shape=[jax.ShapeDtypeStruct(w.shape, F32)] * 3,
                          compiler_params=_cparams())(g, w, m, v)


SMALL_PARAMS = ("b_ada", "g_mix", "conv_b", "dt_bias", "a_log", "d_skip", "g_ssd", "pool_scale", "g_mlp", "g_final")


def _small_adam(gathered, params):
    n_par = len(SMALL_PARAMS)
    nb = gathered[0].shape[1]

    def body(*refs):
        dmod_ref, acc_ref, conv_ref, vec_ref, hd_ref = refs[:5]
        par_refs = refs[5:5 + 3 * n_par]
        out_refs = refs[5 + 3 * n_par:5 + 7 * n_par]
        cw_out, acc_out = refs[5 + 7 * n_par:]

        def total(ref):
            t = ref[0]
            for k in range(1, N_DEV):
                t = t + ref[k]
            return t

        dm = total(dmod_ref)
        dmb = dm[0]
        for b in range(1, nb):
            dmb = dmb + dm[b]
        ac, cv, vc, hd = total(acc_ref), total(conv_ref), total(vec_ref), total(hd_ref)
        cw_out[...] = cv[0:4]
        acc_out[...] = ac
        grads = {
            "b_ada": jnp.concatenate([dmb[r:r + 1] for r in range(6)], axis=1), "g_mix": ac[3:4], "conv_b": cv[4:5],
            "dt_bias": hd[0:1, 0:N_HEADS], "a_log": hd[1:2, 0:N_HEADS], "d_skip": hd[2:3, 0:N_HEADS],
            "g_ssd": vc[0:1], "pool_scale": vc[1:2, 0:POOL_W], "g_mlp": ac[0:1], "g_final": ac[1:2],
        }
        for i, name in enumerate(SMALL_PARAMS):
            w_ref, m_ref, v_ref = par_refs[3 * i:3 * i + 3]
            g = grads[name]
            dl, mn, vn = _adam_math(w_ref[...], g, m_ref[...], v_ref[...])
            g_o, d_o, m_o, v_o = out_refs[4 * i:4 * i + 4]
            g_o[...] = g
            d_o[...] = dl
            m_o[...] = mn
            v_o[...] = vn

    flat = [a for name in SMALL_PARAMS for a in params[name]]
    out_shape = [jax.ShapeDtypeStruct(params[name][0].shape, F32) for name in SMALL_PARAMS for _ in range(4)]
    out_shape += [jax.ShapeDtypeStruct((4, CONV_CH), F32), jax.ShapeDtypeStruct((8, D), F32)]
    return pl.pallas_call(body, name="small_adam", out_shape=out_shape, compiler_params=_cparams())(*gathered, *flat)


def kernel(x, c, w_ada, b_ada, g_mix, w_in, conv_w, conv_b, dt_bias, a_log, d_skip, g_ssd, w_pool, pool_scale, w_out, g_mlp, w_up, w_down, g_final, loss_target, m_w_ada, m_b_ada, m_g_mix, m_w_in, m_conv_w, m_conv_b, m_dt_bias, m_a_log, m_d_skip, m_g_ssd, m_w_pool, m_pool_scale, m_w_out, m_g_mlp, m_w_up, m_w_down, m_g_final, v_w_ada, v_b_ada, v_g_mix, v_w_in, v_conv_w, v_conv_b, v_dt_bias, v_a_log, v_d_skip, v_g_ssd, v_w_pool, v_pool_scale, v_w_out, v_g_mlp, v_w_up, v_w_down, v_g_final):
    nb, seq, _ = x.shape
    T = nb * seq
    me = 4 * lax.axis_index("x") + 2 * lax.axis_index("y") + lax.axis_index("c")
    in_cols = w_in.shape[2]
    ada_cols = w_ada.shape[2]
    cw_cols = conv_w.shape[2]

    win_t = jnp.pad(w_in[0].astype(BF16).T, ((0, -in_cols % 16), (0, 0)))
    c_g, cw_g, win_g = _all_gather([c, conv_w[0], win_t], "ag_first")
    c_all = c_g.reshape(N_DEV * nb, D)
    cw_full = cw_g.transpose(1, 0, 2).reshape(4, CONV_CH)

    b_slice = lax.dynamic_slice(b_ada, (0, me * ada_cols), (1, ada_cols))
    mod_cols = _ada_fwd(c_all, w_ada[0], b_slice)
    (mod_g,) = _all_gather([mod_cols], "ag_mod")
    mod_all = mod_g.transpose(1, 0, 2).reshape(N_DEV * nb, 6, D)
    mod_mine = lax.dynamic_slice(mod_all, (me * nb, 0, 0), (nb, 6, D))
    mod = jnp.pad(mod_mine, ((0, 0), (0, 2), (0, 0)))

    x2 = x.reshape(T, D)
    tg2 = loss_target.reshape(T, D)
    heads = jnp.pad(jnp.concatenate([dt_bias, a_log, d_skip], axis=0), ((0, 5), (0, LANES - N_HEADS)))
    wpool_b = w_pool[0]
    u_b, pm, w_cat, wup_g = _mix_in(x2, mod, g_mix, win_g, in_cols, seq, [w_up[0].astype(BF16)])
    ymix, hstates, cvs, wout_g, wdn_g = _mixer_fwd(
        pm, cw_full, conv_b, heads, g_ssd, wpool_b, pool_scale, nb, seq,
        [w_out[0].astype(BF16), w_down[0].astype(BF16)])
    da_b, dym, dh1, u2_b, f_b, dup_b, ddn_b, dmod_a, acc_a = _mlp_fused(
        x2, ymix, tg2, mod, g_mlp, g_final.reshape(1, D), wout_g.reshape(MIX_W, D), wup_g, wdn_g, seq)

    gout_p = _dw_blocks(ymix, da_b, "dw_out", True, per_step=4)
    gup_p = _dw_blocks(u2_b, dup_b, "dw_up", False)
    ex_a = _exchange_start([gout_p, gup_p], "ga_start")
    gdn_p = _dw_blocks(f_b, ddn_b, "dw_down", True, after=[ex_a[4]])
    ex_b = _exchange_start([gdn_p], "gb_start")
    dpb, d_conv, d_heads, d_vec, d_wpool = _mixer_bwd(
        pm, cvs, dym, hstates, cw_full, conv_b, heads, g_ssd, wpool_b, pool_scale, nb, seq, after=[ex_b[4]])
    gin_p, conv_g, vec_g, heads_g, wpool_parts = _dw_in(
        u_b, dpb, in_cols, [d_conv, d_vec, d_heads, d_wpool.reshape(4 * LANES, LANES)])
    ex_in = _exchange_start([gin_p], "gin_start")
    grad_x2, dmod, acc = _in_bwd(x2, dh1, dpb, mod, g_mix + ex_in[4][0:1, 0:1], w_cat, dmod_a, acc_a, seq)

    gout_r, gup_r = _exchange_wait(ex_a, dmod, "ga_wait")
    (gdn_r,) = _exchange_wait(ex_b, dmod, "gb_wait")
    g_out, d_out, nm_out, nv_out = _adam_parts(gout_r, w_out[0], m_w_out[0], v_w_out[0], "adam_w_out", me)
    g_up, d_up, nm_up, nv_up = _adam_parts(gup_r, w_up[0], m_w_up[0], v_w_up[0], "adam_w_up", me)
    g_dn, d_dn, nm_dn, nv_dn = _adam_parts(gdn_r, w_down[0], m_w_down[0], v_w_down[0], "adam_w_down", me)

    dmod_g, acc_g = _all_gather([dmod, acc], "ag_small_bwd", after=[nm_out, nm_up, nm_dn])
    pool2 = (4 * LANES, LANES)
    wpool_outs = _adam_parts(wpool_parts, w_pool.reshape(pool2), m_w_pool.reshape(pool2), v_w_pool.reshape(pool2),
                             "adam_w_pool")
    small_params = {
        "b_ada": (b_ada, m_b_ada, v_b_ada), "g_mix": (g_mix, m_g_mix, v_g_mix), "conv_b": (conv_b, m_conv_b, v_conv_b),
        "dt_bias": (dt_bias, m_dt_bias, v_dt_bias), "a_log": (a_log, m_a_log, v_a_log),
        "d_skip": (d_skip, m_d_skip, v_d_skip), "g_ssd": (g_ssd, m_g_ssd, v_g_ssd),
        "pool_scale": (pool_scale, m_pool_scale, v_pool_scale), "g_mlp": (g_mlp, m_g_mlp, v_g_mlp),
        "g_final": tuple(a.reshape(1, D) for a in (g_final, m_g_final, v_g_final)),
    }
    small_res = _small_adam([dmod_g, acc_g, conv_g, vec_g, heads_g], small_params)
    g_cw_full, acc_sum = small_res[-2:]
    loss = acc_sum[2, 0]

    g_cw = lax.dynamic_slice(g_cw_full, (0, me * cw_cols), (4, cw_cols))
    d_cwp, nm_cwp, nv_cwp = _adam_plain(g_cw, conv_w[0], m_conv_w[0], v_conv_w[0], "adam_conv_w")

    dmod_all = dmod_g[:, :, 0:6].reshape(N_DEV * nb, 6 * D)
    dmod_slice = lax.dynamic_slice(dmod_all, (0, me * ada_cols), (N_DEV * nb, ada_cols))
    g_ada, d_ada, nm_ada, nv_ada = _ada_bwd_adam(c_all, dmod_slice, w_ada[0], m_w_ada[0], v_w_ada[0])

    ex_after = nm_ada[0:8, 0:LANES] + acc_sum[:, 0:LANES]
    (gin_r,) = _exchange_wait(ex_in, ex_after, "gin_wait")
    g_in, d_in, nm_in, nv_in = _adam_parts(gin_r, w_in[0], m_w_in[0], v_w_in[0], "adam_w_in", me)

    def small_outs(kind, wpool):
        res = {name: small_res[4 * i + kind] for i, name in enumerate(SMALL_PARAMS)}
        res["g_final"] = res["g_final"].reshape(D)
        res["w_pool"] = wpool.reshape(1, 4, LANES, LANES)
        return res

    def big_outs(ada, win, cwp, wout, wup, wdn):
        return {"w_ada": ada[None], "w_in": win.reshape(1, D, in_cols), "conv_w": cwp[None], "w_out": wout[None],
                "w_up": wup[None], "w_down": wdn[None]}

    order = ["w_ada", "b_ada", "g_mix", "w_in", "conv_w", "conv_b", "dt_bias", "a_log", "d_skip", "g_ssd", "w_pool",
             "pool_scale", "w_out", "g_mlp", "w_up", "w_down", "g_final"]
    groups = [
        {**small_outs(0, wpool_outs[0]), **big_outs(g_ada, g_in, g_cw, g_out, g_up, g_dn)},
        {**small_outs(1, wpool_outs[1]), **big_outs(d_ada, d_in, d_cwp, d_out, d_up, d_dn)},
        {**small_outs(2, wpool_outs[2]), **big_outs(nm_ada, nm_in, nm_cwp, nm_out, nm_up, nm_dn)},
        {**small_outs(3, wpool_outs[3]), **big_outs(nv_ada, nv_in, nv_cwp, nv_out, nv_up, nv_dn)},
    ]
    outs = [loss, grad_x2.reshape(nb, seq, D)]
    for grp in groups:
        outs += [grp[n] for n in order]
    return tuple(outs)
```

```python
import functools

import jax
import jax.numpy as jnp
from jax import lax
from jax.experimental import pallas as pl
from jax.experimental.pallas import tpu as pltpu

F32, BF16 = jnp.float32, jnp.bfloat16
MESH = pl.DeviceIdType.MESH
N_DEV = 8
D = 1024
LANES = 128
CHUNK = 128
POOL_W = 512
WINDOWS = (2, 4, 8, 16)
N_HEADS = 16
HEAD_DIM = 64
N_GROUPS = 2
GROUP_W = 512
N_STATE = 128
CONV_CH = 1536
OFF_Z, OFF_XBC, OFF_DT, IN_W = 512, 1536, 3072, 3088
PROJ_W = OFF_DT + LANES
MIX_W = 1536
D_FF = 4096
FF_BLK = 512
EPS = 1e-5
LR, B1, B2, AEPS, WD, STEP = 0.001, 0.9, 0.999, 1e-08, 0.01, 10
POOL_HALO = 16
CONV_HALO = 8
VMEM_LIMIT = 56 << 20
ADAM_BLOCK_BYTES = 1 << 20


def _cparams(**kw):
    return pltpu.CompilerParams(vmem_limit_bytes=VMEM_LIMIT, **kw)


def _mm(a, b):
    return jnp.dot(a.astype(BF16), b.astype(BF16), preferred_element_type=F32)


def _mm_nt(a, b):
    return lax.dot_general(a.astype(BF16), b.astype(BF16), (((1,), (1,)), ((), ())), preferred_element_type=F32)


def _mm_tn(a, b):
    return lax.dot_general(a.astype(BF16), b.astype(BF16), (((0,), (0,)), ((), ())), preferred_element_type=F32)


def _split_bf16(v, terms):
    parts, rest = [], v
    for t in range(terms):
        p = rest.astype(BF16)
        parts.append(p)
        if t + 1 < terms:
            rest = rest - p.astype(F32)
    return parts


def _dot01(a, b, terms, split_lhs=True):
    if split_lhs:
        bb = b.astype(BF16)
        prods = [jnp.dot(p, bb, preferred_element_type=F32) for p in _split_bf16(a, terms)]
    else:
        ab = a.astype(BF16)
        prods = [jnp.dot(ab, p, preferred_element_type=F32) for p in _split_bf16(b, terms)]
    out = prods[0]
    for q in prods[1:]:
        out = out + q
    return out


def _sigmoid(v):
    return 1.0 / (1.0 + jnp.exp(-v))


def _expand_mat():
    r = lax.broadcasted_iota(jnp.int32, (LANES, D), 0)
    c = lax.broadcasted_iota(jnp.int32, (LANES, D), 1)
    return (r == c // HEAD_DIM).astype(F32)


def _reduce_mat():
    r = lax.broadcasted_iota(jnp.int32, (D, LANES), 0)
    c = lax.broadcasted_iota(jnp.int32, (D, LANES), 1)
    return (c == r // HEAD_DIM).astype(F32)


def _pos():
    return lax.axis_index("x"), lax.axis_index("y"), lax.axis_index("c")


GATHER_PIECES = 4
GATHER_PIECE_BYTES = 96 << 10


def _pieces(shape, dtype):
    rows = shape[0]
    size = jnp.dtype(dtype).itemsize
    for d in shape:
        size *= d
    whole_tiles = rows % (GATHER_PIECES * 16) == 0
    return GATHER_PIECES if whole_tiles and size // GATHER_PIECES >= GATHER_PIECE_BYTES else 1


class _Gather:
    def __init__(self, x_refs, o_refs, send, recv, loc):
        self.x_refs, self.o_refs, self.send, self.recv, self.loc = x_refs, o_refs, send, recv, loc
        self.n = len(x_refs)
        self.pieces = [_pieces(r.shape, r.dtype) for r in x_refs]
        self.base = [7 * sum(self.pieces[:a]) for a in range(self.n)]
        x, y, c = _pos()
        self.c = c
        self.me, self.sib = (x, y, c), (x, y, 1 - c)
        self.chips = [(1 - x, y), (x, 1 - y), (1 - x, 1 - y)]

    def _rows(self, a, p):
        rows = self.x_refs[a].shape[0] // self.pieces[a]
        return pl.ds(p * rows, rows)

    def _cp(self, a, p, k, block, to, own=False):
        dst = self.o_refs[a].at[4 * block[0] + 2 * block[1] + block[2], self._rows(a, p)]
        sem = self.base[a] + 7 * p + k
        return pltpu.make_async_remote_copy(
            src_ref=self.x_refs[a].at[self._rows(a, p)] if own else dst, dst_ref=dst,
            send_sem=self.send.at[sem], recv_sem=self.recv.at[sem], device_id=to, device_id_type=MESH)

    def _mine(self, a):
        me = self.me
        return pltpu.make_async_copy(self.x_refs[a], self.o_refs[a].at[4 * me[0] + 2 * me[1] + me[2]], self.loc.at[a])

    def _first(self, a, p):
        cps = [self._cp(a, p, 0, self.me, self.sib, own=True)]
        return cps + [self._cp(a, p, 1 + j, self.me, (*chip, self.c), own=True) for j, chip in enumerate(self.chips)]

    def _passed(self, a, p, j):
        return self._cp(a, p, 4 + j, (*self.chips[j], self.c), self.sib)

    def start(self):
        for a in range(self.n):
            self._mine(a).start()
        for p in range(max(self.pieces)):
            for a in range(self.n):
                if p < self.pieces[a]:
                    for cp in self._first(a, p):
                        cp.start()

    def forward(self, p):
        for j, chip in enumerate(self.chips):
            for a in range(self.n):
                if p < self.pieces[a]:
                    self._cp(a, p, 1 + j, (*chip, self.c), self.me).wait_recv()
                    self._passed(a, p, j).start()

    def finish(self):
        for a in range(self.n):
            for p in range(self.pieces[a]):
                self._cp(a, p, 0, self.sib, self.me).wait_recv()
                for j, chip in enumerate(self.chips):
                    self._cp(a, p, 4 + j, (*chip, 1 - self.c), self.me).wait_recv()
        for a in range(self.n):
            for p in range(self.pieces[a]):
                for cp in self._first(a, p):
                    cp.wait_send()
                for j in range(3):
                    self._passed(a, p, j).wait_send()
            self._mine(a).wait()

    def begin_hosted(self, step, steps):
        @pl.when(step == 0)
        def _():
            self.start()

        n_p = max(self.pieces)
        for p in range(n_p):
            @pl.when(step == min(((p + 1) * 7 * steps) // (8 * n_p), steps - 1))
            def _():
                self.forward(p)

    def end_hosted(self, step, steps):
        @pl.when(step == steps - 1)
        def _():
            self.finish()


class _Exchange:
    def __init__(self, x_refs, o_refs, send, recv, loc):
        self.x_refs, self.o_refs, self.send, self.recv, self.loc = x_refs, o_refs, send, recv, loc
        self.n = len(x_refs)
        x, y, c = _pos()
        self.me_i = 4 * x + 2 * y + c
        self.peers = []
        for k in range(1, N_DEV):
            px = 1 - x if (k >> 2) & 1 else x
            py = 1 - y if (k >> 1) & 1 else y
            pc = 1 - c if k & 1 else c
            self.peers.append(((px, py, pc), 4 * px + 2 * py + pc))

    def _mine(self, a):
        return pltpu.make_async_copy(self.x_refs[a].at[self.me_i], self.o_refs[a].at[self.me_i], self.loc.at[a])

    def _cp(self, a, k, landing):
        peer, peer_i = self.peers[k]
        return pltpu.make_async_remote_copy(
            src_ref=self.x_refs[a].at[peer_i], dst_ref=self.o_refs[a].at[landing],
            send_sem=self.send.at[a * 7 + k], recv_sem=self.recv.at[a * 7 + k],
            device_id=peer, device_id_type=MESH)

    def start(self):
        for a in range(self.n):
            self._mine(a).start()
            for k in range(N_DEV - 1):
                self._cp(a, k, self.me_i).start()

    def finish(self):
        for a in range(self.n):
            for k in range(N_DEV - 1):
                self._cp(a, k, self.peers[k][1]).wait_recv()
        for a in range(self.n):
            for k in range(N_DEV - 1):
                self._cp(a, k, self.me_i).wait_send()
            self._mine(a).wait()


def _gather_scratch(xs):
    n_sem = 7 * sum(_pieces(v.shape, v.dtype) for v in xs)
    return [pltpu.SemaphoreType.DMA((n_sem,)), pltpu.SemaphoreType.DMA((n_sem,)), pltpu.SemaphoreType.DMA((len(xs),))]


ANY_SPEC = pl.BlockSpec(memory_space=pl.ANY)


def _all_gather(xs, name, after=()):
    n, na = len(xs), len(after)

    def body(*refs):
        g = _Gather(refs[:n], refs[n + na:2 * n + na], *refs[2 * n + na:])
        g.start()
        for p in range(max(g.pieces)):
            g.forward(p)
        g.finish()

    return pl.pallas_call(
        body, name=name,
        out_shape=[jax.ShapeDtypeStruct((N_DEV,) + v.shape, v.dtype) for v in xs],
        in_specs=[ANY_SPEC] * (n + na), out_specs=[ANY_SPEC] * n, scratch_shapes=_gather_scratch(xs),
    )(*xs, *after)


HBM_SPEC = pl.BlockSpec(memory_space=pltpu.HBM)
SEM_SPEC = pl.BlockSpec(memory_space=pltpu.SEMAPHORE)
VMEM_SPEC = pl.BlockSpec(memory_space=pltpu.VMEM)
SPLIT_EFFECT = pltpu.SideEffectType.DATAFLOW_SIDE_EFFECTING


def _in_hbm(v):
    return pltpu.with_memory_space_constraint(v, pltpu.HBM)


def _exchange_start(blocks, name):
    n = len(blocks)

    def body(*refs):
        x_refs, land_refs = refs[:n], refs[n:2 * n]
        send, recv = refs[2 * n:2 * n + 2]
        token = refs[-1]
        ex = _Exchange(x_refs, land_refs, send, recv, None)
        for a in range(n):
            for k in range(N_DEV - 1):
                ex._cp(a, k, ex.me_i).start()
        token[...] = jnp.zeros_like(token)

    lands = [lax.empty(v.shape, v.dtype) for v in blocks]
    hbm = tuple(pltpu.HBM(v.shape, v.dtype) for v in list(blocks) + lands)
    n_sem = (N_DEV - 1) * n
    out = pl.pallas_call(
        body, name=name,
        out_shape=(pltpu.SemaphoreType.DMA((n_sem,)), pltpu.SemaphoreType.DMA((n_sem,))) + hbm
        + (jax.ShapeDtypeStruct((8, LANES), F32),),
        in_specs=(HBM_SPEC,) * (2 * n), out_specs=(SEM_SPEC, SEM_SPEC) + (HBM_SPEC,) * (2 * n) + (VMEM_SPEC,),
        input_output_aliases={i: i + 2 for i in range(2 * n)},
        compiler_params=pltpu.CompilerParams(has_side_effects=SPLIT_EFFECT),
    )(*[_in_hbm(v) for v in list(blocks) + lands])
    return out[0], out[1], list(out[2:2 + n]), list(out[2 + n:2 + 2 * n]), out[-1]


def _exchange_wait(ex, after, name):
    send, recv, thru, lands, _ = ex
    n = len(thru)

    def body(*refs):
        x_refs, land_refs = refs[:n], refs[n:2 * n]
        send_ref, recv_ref = refs[2 * n:2 * n + 2]
        e = _Exchange(x_refs, land_refs, send_ref, recv_ref, None)
        for a in range(n):
            for k in range(N_DEV - 1):
                e._cp(a, k, e.me_i).wait_send()
                e._cp(a, k, e.peers[k][1]).wait_recv()

    hbm = tuple(pltpu.HBM(v.shape, v.dtype) for v in list(thru) + list(lands))
    out = pl.pallas_call(
        body, name=name, out_shape=hbm,
        in_specs=(HBM_SPEC,) * (2 * n) + (SEM_SPEC, SEM_SPEC, ANY_SPEC), out_specs=(HBM_SPEC,) * (2 * n),
        input_output_aliases={i: i for i in range(2 * n)},
        compiler_params=pltpu.CompilerParams(has_side_effects=SPLIT_EFFECT),
    )(*thru, *lands, send, recv, after)
    return list(zip(out[:n], out[n:]))


def _ada_fwd(c_all, w_ada, b_slice):
    def body(c_ref, w_ref, b_ref, o_ref):
        cv = c_ref[...]
        act = cv * _sigmoid(cv)
        o_ref[...] = _mm(act, w_ref[...]) + b_ref[...]

    nb, nc = c_all.shape[0], w_ada.shape[1]
    return pl.pallas_call(body, name="ada_fwd", out_shape=jax.ShapeDtypeStruct((nb, nc), F32),
                          compiler_params=_cparams())(c_all, w_ada, b_slice)


def _adam_math(w, g, m, v):
    m = B1 * m + (1.0 - B1) * g
    v = B2 * v + (1.0 - B2) * jnp.square(g)
    m_hat = m / (1.0 - B1 ** STEP)
    v_hat = v / (1.0 - B2 ** STEP)
    delta = -LR * (m_hat / (jnp.sqrt(v_hat) + AEPS) + WD * w)
    return delta, m, v


def _ada_bwd_adam(c_all, dmod_slice, w, m, v):
    rows, cols = w.shape
    br = 256

    def body(c_ref, d_ref, w_ref, m_ref, v_ref, g_out, dl_out, m_out, v_out):
        cv = c_ref[...]
        act = cv * _sigmoid(cv)
        g = _mm_tn(act, d_ref[...])
        g_out[...] = g
        dl, mn, vn = _adam_math(w_ref[...], g, m_ref[...], v_ref[...])
        dl_out[...] = dl
        m_out[...] = mn
        v_out[...] = vn

    nb = c_all.shape[0]
    wspec = pl.BlockSpec((br, cols), lambda i: (i, 0))
    return pl.pallas_call(
        body, name="ada_bwd_adam", grid=(rows // br,),
        in_specs=[pl.BlockSpec((nb, br), lambda i: (0, i)), pl.BlockSpec((nb, cols), lambda i: (0, 0)),
                  wspec, wspec, wspec],
        out_specs=[wspec] * 4, out_shape=[jax.ShapeDtypeStruct((rows, cols), F32)] * 4,
        compiler_params=_cparams(),
    )(c_all, dmod_slice, w, m, v)


def _mix_in(x2, mod, g_mix, win_g, in_cols, seq, shards):
    T = x2.shape[0]
    tm = min(512, seq)
    tps = seq // tm
    blk_rows = win_g.shape[1]
    pad_rows = -blk_rows % (4 * LANES)
    ns = len(shards)
    steps = T // tm

    def body(*refs):
        x_ref, mod_ref, g_ref, wb_ref = refs[:4]
        sh_refs = refs[4:4 + ns]
        u_ref, pm_ref, wc_ref = refs[4 + ns:7 + ns]
        ga_refs = refs[7 + ns:7 + 2 * ns]
        w_ref, send, recv, loc = refs[7 + 2 * ns:]
        step = pl.program_id(0)
        gather = _Gather(sh_refs, ga_refs, send, recv, loc)
        gather.begin_hosted(step, steps)

        @pl.when(step == 0)
        def _():
            w_ref[:, OFF_DT:] = jnp.zeros((D, PROJ_W - OFF_DT), BF16)
            for j in range(N_DEV):
                blk = jnp.concatenate([wb_ref[j], jnp.zeros((pad_rows, D), BF16)], axis=0)
                w_ref[:, in_cols * j:in_cols * (j + 1)] = blk.T[:, :in_cols]
            wc_ref[...] = w_ref[...]

        x = x_ref[...]
        r = lax.rsqrt(jnp.mean(x * x, axis=-1, keepdims=True) + EPS)
        md = mod_ref[0]
        u = (x * r * g_ref[...]) * (1.0 + md[1:2]) + md[0:1]
        ub = u.astype(BF16)
        u_ref[...] = ub
        pm_ref[...] = jnp.dot(ub, w_ref[...], preferred_element_type=F32)
        gather.end_hosted(step, steps)

    whole = pl.BlockSpec(memory_space=pltpu.VMEM)
    return pl.pallas_call(
        body, name="mix_in", grid=(T // tm,),
        in_specs=[pl.BlockSpec((tm, D), lambda i: (i, 0)), pl.BlockSpec((1, 8, D), lambda i: (i // tps, 0, 0)),
                  pl.BlockSpec((1, D), lambda i: (0, 0)), whole] + [ANY_SPEC] * ns,
        out_specs=[pl.BlockSpec((tm, D), lambda i: (i, 0)), pl.BlockSpec((tm, PROJ_W), lambda i: (i, 0)),
                   pl.BlockSpec((D, PROJ_W), lambda i: (0, 0))] + [ANY_SPEC] * ns,
        out_shape=[jax.ShapeDtypeStruct((T, D), BF16), jax.ShapeDtypeStruct((T, PROJ_W), F32),
                   jax.ShapeDtypeStruct((D, PROJ_W), BF16)]
        + [jax.ShapeDtypeStruct((N_DEV,) + v.shape, v.dtype) for v in shards],
        scratch_shapes=[pltpu.VMEM((D, PROJ_W), BF16)] + _gather_scratch(shards),
        compiler_params=_cparams(),
    )(x2, mod, g_mix, win_g, *shards)


def _chunk_forward(up, z, ux, dtin, halo_p, halo_x, hprev, cw, cb, hp, gssd, wpool, pscale, t0, y_scr, cv=None):
    L = CHUNK
    out = {}
    row = lax.broadcasted_iota(jnp.int32, (L, 1), 0)
    t = (t0 + row + 1).astype(F32)
    e = jnp.concatenate([halo_p, up], axis=0)
    s2 = e + pltpu.roll(e, 1, 0)
    s4 = s2 + pltpu.roll(s2, 2, 0)
    s8 = s4 + pltpu.roll(s4, 4, 0)
    s16 = s8 + pltpu.roll(s8, 8, 0)
    sums = (s2, s4, s8, s16)
    p, inv, yp = [], [], []
    for gi, w in enumerate(WINDOWS):
        sl = slice(gi * LANES, (gi + 1) * LANES)
        ic = 1.0 / jnp.minimum(t, float(w))
        pg = sums[gi][POOL_HALO:, sl] * ic - up[:, sl]
        p.append(pg)
        inv.append(ic)
        yp.append(_mm(pg, wpool[gi]))
    out["p"], out["inv"], out["yp"] = p, inv, yp
    out["y_pool"] = jnp.concatenate(yp, axis=1) * pscale
    if cv is None:
        ex = jnp.concatenate([halo_x, ux], axis=0)
        taps = [pltpu.roll(ex, 3, 0)[CONV_HALO:], pltpu.roll(ex, 2, 0)[CONV_HALO:], pltpu.roll(ex, 1, 0)[CONV_HALO:], ux]
        cv = cb + taps[0] * cw[0:1] + taps[1] * cw[1:2] + taps[2] * cw[2:3] + taps[3] * cw[3:4]
    sg = _sigmoid(cv)
    xbc = cv * sg
    out["cv"], out["sg"] = cv, sg
    X = xbc[:, :D]
    Bm = xbc[:, D:D + N_GROUPS * N_STATE]
    Cm = xbc[:, D + N_GROUPS * N_STATE:]
    pre = dtin + hp[0:1]
    dt = jnp.maximum(pre, 0.0) + jnp.log(1.0 + jnp.exp(-jnp.abs(pre)))
    a_row = -jnp.exp(hp[1:2])
    da = dt * a_row
    ri = lax.broadcasted_iota(jnp.int32, (L, L), 0)
    ci = lax.broadcasted_iota(jnp.int32, (L, L), 1)
    causal = ri >= ci
    cum = _dot01(causal.astype(F32), da, 3, split_lhs=False)
    cum_t = cum.T
    cum_last = cum[L - 1:L]
    eo = jnp.exp(cum)
    dec = jnp.exp(cum_last - cum)
    cd = jnp.exp(cum_last)
    exm = _expand_mat()
    rows8 = jnp.concatenate([cd, hp[2:3], jnp.zeros((6, LANES), F32)], axis=0)
    rep = _dot01(jnp.concatenate([dt, eo, dec, rows8], axis=0), exm, 2)
    dt_rep, eo_rep, dec_rep = rep[0:L], rep[L:2 * L], rep[2 * L:3 * L]
    cd_rep, dskip_rep = rep[3 * L:3 * L + 1], rep[3 * L + 1:3 * L + 2]
    xdt = X * dt_rep
    out.update(X=X, Bm=Bm, Cm=Cm, pre=pre, dt=dt, a_row=a_row, cum=cum, cum_t=cum_t, eo=eo, dec=dec, cd=cd,
               dt_rep=dt_rep, eo_rep=eo_rep, dec_rep=dec_rep, cd_rep=cd_rep, dskip_rep=dskip_rep, xdt=xdt,
               causal=causal, anti=(ri <= ci).astype(F32), exm=exm)
    G, lms, yoff, hnew, xdec = [], [], [], [], []
    for g in range(N_GROUPS):
        gs = slice(g * GROUP_W, (g + 1) * GROUP_W)
        Bg = Bm[:, g * N_STATE:(g + 1) * N_STATE]
        Cg = Cm[:, g * N_STATE:(g + 1) * N_STATE]
        Gg = _mm_nt(Cg, Bg)
        G.append(Gg)
        for hh in range(N_HEADS // N_GROUPS):
            h = g * (N_HEADS // N_GROUPS) + hh
            seg = cum[:, h:h + 1] - cum_t[h:h + 1, :]
            lm = jnp.where(causal, jnp.exp(jnp.minimum(seg, 0.0)), 0.0)
            lms.append(lm)
            hs = slice(h * HEAD_DIM, (h + 1) * HEAD_DIM)
            y_scr[:, hs] = _mm(Gg * lm, xdt[:, hs])
        xd = xdt[:, gs] * dec_rep[:, gs]
        xdec.append(xd)
        sgm = _mm_tn(Bg, xd)
        yoff.append(_mm(Cg, hprev[g]) * eo_rep[:, gs])
        hnew.append(hprev[g] * cd_rep[:, gs] + sgm)
    out.update(G=G, lms=lms, yoff=yoff, hnew=hnew, xdec=xdec)
    y = y_scr[...] + jnp.concatenate(yoff, axis=1) + dskip_rep * X
    rg, yn = [], []
    for g in range(N_GROUPS):
        gs = slice(g * GROUP_W, (g + 1) * GROUP_W)
        zg = z(g)
        yzg = y[:, gs] * (zg * _sigmoid(zg))
        r = lax.rsqrt(jnp.mean(yzg * yzg, axis=-1, keepdims=True) + EPS)
        rg.append(r)
        yn.append(yzg * r)
    yn = jnp.concatenate(yn, axis=1)
    out.update(y=y, rg=rg, yn=yn)
    out["y_ssd"] = yn * gssd
    return out


def _mixer_fwd(pm, cw, cb, hp, gssd, wpool, pscale, nb, seq, shards):
    nc = seq // CHUNK
    ns = len(shards)
    steps = nb * nc

    def body(*refs):
        pm_ref, cw_ref, cb_ref, hp_ref, gs_ref, wp_ref, ps_ref = refs[:7]
        sh_refs = refs[7:7 + ns]
        ym_ref, hs_ref, cv_ref = refs[7 + ns:10 + ns]
        ga_refs = refs[10 + ns:10 + 2 * ns]
        halo_p, halo_x, state, y_scr, send, recv, loc = refs[10 + 2 * ns:]
        c = pl.program_id(1)
        step = pl.program_id(0) * nc + c
        gather = _Gather(sh_refs, ga_refs, send, recv, loc)
        gather.begin_hosted(step, steps)

        @pl.when(c == 0)
        def _():
            halo_p[...] = jnp.zeros_like(halo_p)
            halo_x[...] = jnp.zeros_like(halo_x)
            state[...] = jnp.zeros_like(state)

        up = pm_ref[:, 0:POOL_W]
        def z(g):
            return pm_ref[:, OFF_Z + g * GROUP_W:OFF_Z + (g + 1) * GROUP_W]

        ux = pm_ref[:, OFF_XBC:OFF_DT]
        hprev = [state[0], state[1]]
        hs_ref[0, 0, 0] = hprev[0]
        hs_ref[0, 0, 1] = hprev[1]
        o = _chunk_forward(up, z, ux, pm_ref[:, OFF_DT:], halo_p[...], halo_x[...], hprev, cw_ref[...], cb_ref[...],
                           hp_ref[...], gs_ref[...], wp_ref[...], ps_ref[...], c * CHUNK, y_scr)
        ym_ref[:, 0:POOL_W] = o["y_pool"].astype(BF16)
        ym_ref[:, POOL_W:] = o["y_ssd"].astype(BF16)
        cv_ref[...] = o["cv"]
        state[0] = o["hnew"][0]
        state[1] = o["hnew"][1]
        halo_p[...] = up[CHUNK - POOL_HALO:]
        halo_x[...] = ux[CHUNK - CONV_HALO:]
        gather.end_hosted(step, steps)

    def full(shape):
        return pl.BlockSpec(shape, lambda b, c: (0,) * len(shape))

    T = nb * seq
    return pl.pallas_call(
        body, name="mixer_fwd", grid=(nb, nc),
        in_specs=[pl.BlockSpec((CHUNK, PROJ_W), lambda b, c: (b * nc + c, 0)),
                  full((4, CONV_CH)), full((1, CONV_CH)), full((8, LANES)), full((1, D)),
                  full((4, LANES, LANES)), full((1, POOL_W))] + [ANY_SPEC] * ns,
        out_specs=[pl.BlockSpec((CHUNK, MIX_W), lambda b, c: (b * nc + c, 0)),
                   pl.BlockSpec((1, 1, N_GROUPS, N_STATE, GROUP_W), lambda b, c: (b, c, 0, 0, 0)),
                   pl.BlockSpec((CHUNK, CONV_CH), lambda b, c: (b * nc + c, 0))] + [ANY_SPEC] * ns,
        out_shape=[jax.ShapeDtypeStruct((T, MIX_W), BF16),
                   jax.ShapeDtypeStruct((nb, nc, N_GROUPS, N_STATE, GROUP_W), F32),
                   jax.ShapeDtypeStruct((T, CONV_CH), F32)]
        + [jax.ShapeDtypeStruct((N_DEV,) + v.shape, v.dtype) for v in shards],
        scratch_shapes=[pltpu.VMEM((POOL_HALO, POOL_W), F32), pltpu.VMEM((CONV_HALO, CONV_CH), F32),
                        pltpu.VMEM((N_GROUPS, N_STATE, GROUP_W), F32), pltpu.VMEM((CHUNK, D), F32)] + _gather_scratch(shards),
        compiler_params=_cparams(),
    )(pm, cw, cb, hp, gssd, wpool, pscale, *shards)


def _mixer_bwd(pm, cvs, dym, hstates, cw, cb, hp, gssd, wpool, pscale, nb, seq, after=()):
    nc = seq // CHUNK
    hpg = N_HEADS // N_GROUPS
    na = len(after)

    def body(*refs):
        (pm_ref, hpool_ref, cv_ref, dy_ref, hs_ref, cw_ref, cb_ref, hp_ref, gs_ref, wp_ref, ps_ref) = refs[:11]
        dpm_ref, dconv_ref, dhp_ref, dvec_ref, dwp_ref = refs[11 + na:16 + na]
        nxt_q, nxt_cv, rstate, y_scr, dx_scr = refs[16 + na:]
        b = pl.program_id(0)
        ci = pl.program_id(1)
        c = nc - 1 - ci

        @pl.when((b == 0) & (ci == 0))
        def _():
            for r in (dconv_ref, dhp_ref, dvec_ref, dwp_ref):
                r[...] = jnp.zeros_like(r)

        @pl.when(ci == 0)
        def _():
            nxt_q[...] = jnp.zeros_like(nxt_q)
            nxt_cv[...] = jnp.zeros_like(nxt_cv)
            rstate[...] = jnp.zeros_like(rstate)

        first = (c > 0).astype(F32)
        up = pm_ref[:, 0:POOL_W]
        def z(g):
            return pm_ref[:, OFF_Z + g * GROUP_W:OFF_Z + (g + 1) * GROUP_W]

        ux = pm_ref[:, OFF_XBC:OFF_DT]
        halo_p = hpool_ref[...] * first
        hprev = [hs_ref[0, 0, 0], hs_ref[0, 0, 1]]
        cw, cb, hp, gssd, wpool, pscale = cw_ref[...], cb_ref[...], hp_ref[...], gs_ref[...], wp_ref[...], ps_ref[...]
        o = _chunk_forward(up, z, ux, pm_ref[:, OFF_DT:], halo_p, None, hprev, cw, cb, hp, gssd, wpool, pscale,
                           c * CHUNK, y_scr, cv=cv_ref[...])
        L = CHUNK
        dy_pool = dy_ref[:, 0:POOL_W].astype(F32)
        dy_ssd = dy_ref[:, POOL_W:].astype(F32)

        dvec_ref[1:2, 0:POOL_W] += jnp.sum(dy_pool * jnp.concatenate(o["yp"], axis=1), axis=0, keepdims=True)
        dyp = dy_pool * pscale
        qs = []
        dps = []
        for gi in range(len(WINDOWS)):
            sl = slice(gi * LANES, (gi + 1) * LANES)
            dwp_ref[gi] += _mm_tn(o["p"][gi], dyp[:, sl])
            dpg = _mm_nt(dyp[:, sl], wpool[gi])
            dps.append(dpg)
            qs.append(dpg * o["inv"][gi])
        q = jnp.concatenate(qs, axis=1)
        e = jnp.concatenate([q, nxt_q[...]], axis=0)
        n = L + POOL_HALO
        s2 = e + pltpu.roll(e, n - 1, 0)
        s4 = s2 + pltpu.roll(s2, n - 2, 0)
        s8 = s4 + pltpu.roll(s4, n - 4, 0)
        s16 = s8 + pltpu.roll(s8, n - 8, 0)
        sums = (s2, s4, s8, s16)
        for gi in range(len(WINDOWS)):
            sl = slice(gi * LANES, (gi + 1) * LANES)
            dpm_ref[:, sl] = (sums[gi][:L, sl] - dps[gi]).astype(BF16)
        nxt_q[...] = q[:POOL_HALO]

        yn, y = o["yn"], o["y"]
        dvec_ref[0:1] += jnp.sum(dy_ssd * yn, axis=0, keepdims=True)
        dyn = dy_ssd * gssd
        dyv = []
        for g in range(N_GROUPS):
            gs = slice(g * GROUP_W, (g + 1) * GROUP_W)
            mean = jnp.mean(dyn[:, gs] * yn[:, gs], axis=-1, keepdims=True)
            dyz = o["rg"][g] * (dyn[:, gs] - yn[:, gs] * mean)
            zg = z(g)
            szg = _sigmoid(zg)
            dyv.append(dyz * (zg * szg))
            dpm_ref[:, OFF_Z + g * GROUP_W:OFF_Z + (g + 1) * GROUP_W] = (
                dyz * y[:, gs] * (szg * (1.0 + zg * (1.0 - szg)))).astype(BF16)
        dyv = jnp.concatenate(dyv, axis=1)

        X, Bm, Cm, xdt = o["X"], o["Bm"], o["Cm"], o["xdt"]
        exm = o["exm"]
        rdm = _reduce_mat()
        lane = lax.broadcasted_iota(jnp.int32, (1, LANES), 1)
        sub = lax.broadcasted_iota(jnp.int32, (LANES, 1), 0)
        dX = o["dskip_rep"] * dyv
        yoff_full = jnp.concatenate(o["yoff"], axis=1)
        rs = jnp.zeros((L, LANES), F32)
        cs_t = jnp.zeros((LANES, L), F32)
        dBs, dCs = [], []
        rh_sums = []
        ddec = []
        for g in range(N_GROUPS):
            gs = slice(g * GROUP_W, (g + 1) * GROUP_W)
            Bg = Bm[:, g * N_STATE:(g + 1) * N_STATE]
            Cg = Cm[:, g * N_STATE:(g + 1) * N_STATE]
            Gg = o["G"][g]
            R = rstate[g]
            dwm = dyv[:, gs] * o["eo_rep"][:, gs]
            dC = _mm_nt(dwm, hprev[g])
            dH = _mm_tn(Cg, dwm)
            dG = jnp.zeros((L, L), F32)
            for hh in range(hpg):
                h = g * hpg + hh
                hs = slice(h * HEAD_DIM, (h + 1) * HEAD_DIM)
                lm = o["lms"][h]
                m_h = Gg * lm
                dM = _mm_nt(dyv[:, hs], xdt[:, hs])
                dx_scr[:, hs] = _mm_tn(m_h, dyv[:, hs])
                qm = dM * m_h
                rs = rs + jnp.sum(qm, axis=1, keepdims=True) * (lane == h).astype(F32)
                cs_t = cs_t + (sub == h).astype(F32) * jnp.sum(qm, axis=0, keepdims=True)
                dG = dG + dM * lm
            dC = dC + _mm(dG, Bg)
            dB = _mm_tn(dG, Cg)
            zx = _mm(Bg, R)
            dxdt_state = zx * o["dec_rep"][:, gs]
            ddec.append(zx * xdt[:, gs])
            dB = dB + _mm_nt(o["xdec"][g], R)
            rh_sums.append(jnp.sum(R * hprev[g], axis=0, keepdims=True))
            rstate[g] = dH + o["cd_rep"][:, gs] * R
            dx_scr[:, gs] = dx_scr[:, gs] + dxdt_state
            dBs.append(dB)
            dCs.append(dC)
        dxdt = dx_scr[...]
        tail = jnp.concatenate([jnp.sum(dyv * X, axis=0, keepdims=True), jnp.concatenate(rh_sums, axis=1),
                                jnp.zeros((6, D), F32)], axis=0)
        red = _dot01(jnp.concatenate([dyv * yoff_full, jnp.concatenate(ddec, axis=1), dxdt * X, tail], axis=0), rdm, 2)
        d_dskip, dcd_row = red[3 * L:3 * L + 1], red[3 * L + 1:3 * L + 2]
        ddec_h = red[L:2 * L] * o["dec"]
        dcum_last = jnp.sum(ddec_h, axis=0, keepdims=True) + dcd_row * o["cd"]
        dcum = red[0:L] + rs - cs_t.T - ddec_h + (sub == L - 1).astype(F32) * dcum_last
        dda = _dot01(o["anti"], dcum, 3, split_lhs=False)
        ddt_v = dda * o["a_row"] + red[2 * L:3 * L]
        dX = dX + dxdt * o["dt_rep"]
        head_mask = (lane < N_HEADS).astype(F32)
        d_alog = jnp.sum(dda * o["dt"], axis=0, keepdims=True) * o["a_row"] * head_mask
        dpre = ddt_v * _sigmoid(o["pre"]) * head_mask
        dpm_ref[:, OFF_DT:] = dpre.astype(BF16)
        d_dtb = jnp.sum(dpre, axis=0, keepdims=True)
        dhp_ref[...] += jnp.concatenate([d_dtb, d_alog, d_dskip * head_mask, jnp.zeros((5, LANES), F32)], axis=0)

        dxbc = jnp.concatenate([dX] + dBs + dCs, axis=1)
        sg, cv = o["sg"], o["cv"]
        dcv = dxbc * (sg * (1.0 + cv * (1.0 - sg)))
        e2 = jnp.concatenate([dcv, nxt_cv[...]], axis=0)
        n2 = L + CONV_HALO
        ahead = [dcv, pltpu.roll(e2, n2 - 1, 0)[:L], pltpu.roll(e2, n2 - 2, 0)[:L], pltpu.roll(e2, n2 - 3, 0)[:L]]
        dconv_ref[0:5] += jnp.concatenate(
            [jnp.sum(ux * ahead[3 - k], axis=0, keepdims=True) for k in range(4)]
            + [jnp.sum(dcv, axis=0, keepdims=True)], axis=0)
        dux = ahead[0] * cw[3:4] + ahead[1] * cw[2:3] + ahead[2] * cw[1:2] + ahead[3] * cw[0:1]
        dpm_ref[:, OFF_XBC:OFF_DT] = dux.astype(BF16)
        nxt_cv[...] = dcv[:CONV_HALO]

    def full(shape):
        return pl.BlockSpec(shape, lambda b, c: (0,) * len(shape))

    def rowblk(b, c):
        return b * nc + (nc - 1 - c)

    hp_blocks = CHUNK // POOL_HALO
    T = nb * seq
    return pl.pallas_call(
        body, name="mixer_bwd", grid=(nb, nc),
        in_specs=[pl.BlockSpec((CHUNK, PROJ_W), lambda b, c: (rowblk(b, c), 0)),
                  pl.BlockSpec((POOL_HALO, POOL_W), lambda b, c: (jnp.maximum(rowblk(b, c) * hp_blocks - 1, 0), 0)),
                  pl.BlockSpec((CHUNK, CONV_CH), lambda b, c: (rowblk(b, c), 0)),
                  pl.BlockSpec((CHUNK, MIX_W), lambda b, c: (rowblk(b, c), 0)),
                  pl.BlockSpec((1, 1, N_GROUPS, N_STATE, GROUP_W), lambda b, c: (b, nc - 1 - c, 0, 0, 0)),
                  full((4, CONV_CH)), full((1, CONV_CH)), full((8, LANES)), full((1, D)),
                  full((4, LANES, LANES)), full((1, POOL_W))] + [ANY_SPEC] * na,
        out_specs=[pl.BlockSpec((CHUNK, PROJ_W), lambda b, c: (rowblk(b, c), 0)),
                   full((8, CONV_CH)), full((8, LANES)), full((8, D)), full((4, LANES, LANES))],
        out_shape=[jax.ShapeDtypeStruct((T, PROJ_W), BF16),
                   jax.ShapeDtypeStruct((8, CONV_CH), F32), jax.ShapeDtypeStruct((8, LANES), F32),
                   jax.ShapeDtypeStruct((8, D), F32), jax.ShapeDtypeStruct((4, LANES, LANES), F32)],
        scratch_shapes=[pltpu.VMEM((POOL_HALO, POOL_W), F32), pltpu.VMEM((CONV_HALO, CONV_CH), F32),
                        pltpu.VMEM((N_GROUPS, N_STATE, GROUP_W), F32), pltpu.VMEM((CHUNK, D), F32),
                        pltpu.VMEM((CHUNK, D), F32)],
        compiler_params=_cparams(),
    )(pm, pm, cvs, dym, hstates, cw, cb, hp, gssd, wpool, pscale, *after)


def _mlp_fused(x2, ymix, target, mod, g_mlp, g_final, w_out, w_up, w_down, seq):
    T = x2.shape[0]
    tm = min(256, seq)
    tps = seq // tm
    nblk = D_FF // FF_BLK

    def body(x_ref, ym_ref, tg_ref, mod_ref, gm_ref, gf_ref, wo_ref, wu_ref, wd_ref,
             da_ref, dym_ref, dh1_ref, u2_ref, f_ref, dup_ref, ddn_ref, dmod_ref, acc_ref, relu_scr):
        i = pl.program_id(0)

        @pl.when(i == 0)
        def _():
            acc_ref[...] = jnp.zeros_like(acc_ref)

        @pl.when(i % tps == 0)
        def _():
            dmod_ref[...] = jnp.zeros_like(dmod_ref)

        md = mod_ref[0]
        gate_m, shift_f, scale_f, gate_f = md[2:3], md[3:4], md[4:5], md[5:6]
        g_mlp, g_fin = gm_ref[...], gf_ref[...]
        a = jnp.dot(ym_ref[...], wo_ref[...], preferred_element_type=F32)
        h1 = x_ref[...] + gate_m * a
        r2 = lax.rsqrt(jnp.mean(h1 * h1, axis=-1, keepdims=True) + EPS)
        n2 = h1 * r2
        u2 = (n2 * g_mlp) * (1.0 + scale_f) + shift_f
        u2b = u2.astype(BF16)
        u2_ref[...] = u2b
        dn = jnp.zeros((tm, D), F32)
        for j in range(nblk):
            js = slice(j * FF_BLK, (j + 1) * FF_BLK)
            upj = jnp.maximum(jnp.dot(u2b, wu_ref[j], preferred_element_type=F32), 0.0)
            relu_scr[:, js] = upj
            fj = (upj * upj).astype(BF16)
            f_ref[:, js] = fj
            dn = dn + jnp.dot(fj, wd_ref[j], preferred_element_type=F32)
        h2 = h1 + gate_f * dn
        r3 = lax.rsqrt(jnp.mean(h2 * h2, axis=-1, keepdims=True) + EPS)
        n3 = h2 * r3
        err = n3 * g_fin - tg_ref[...]
        loss = 0.5 * jnp.sum(jnp.mean(err * err, axis=-1, keepdims=True), axis=0, keepdims=True)
        dout = err * (1.0 / D)
        d_gfin = jnp.sum(dout * n3, axis=0, keepdims=True)
        dn3 = dout * g_fin
        dh2 = r3 * (dn3 - n3 * jnp.mean(dn3 * n3, axis=-1, keepdims=True))
        d_gate_f = jnp.sum(dh2 * dn, axis=0, keepdims=True)
        ddn = (gate_f * dh2).astype(BF16)
        ddn_ref[...] = ddn
        du2 = jnp.zeros((tm, D), F32)
        for j in range(nblk):
            js = slice(j * FF_BLK, (j + 1) * FF_BLK)
            dfj = lax.dot_general(ddn, wd_ref[j], (((1,), (1,)), ((), ())), preferred_element_type=F32)
            dupj = (dfj * (2.0 * relu_scr[:, js])).astype(BF16)
            dup_ref[:, js] = dupj
            du2 = du2 + lax.dot_general(dupj, wu_ref[j], (((1,), (1,)), ((), ())), preferred_element_type=F32)
        d_scale_f = jnp.sum(du2 * (n2 * g_mlp), axis=0, keepdims=True)
        d_shift_f = jnp.sum(du2, axis=0, keepdims=True)
        d_gmlp = jnp.sum(du2 * (1.0 + scale_f) * n2, axis=0, keepdims=True)
        dn2 = du2 * (g_mlp * (1.0 + scale_f))
        dh1 = dh2 + r2 * (dn2 - n2 * jnp.mean(dn2 * n2, axis=-1, keepdims=True))
        dh1_ref[...] = dh1
        d_gate_m = jnp.sum(dh1 * a, axis=0, keepdims=True)
        da = (gate_m * dh1).astype(BF16)
        da_ref[...] = da
        dym_ref[...] = lax.dot_general(da, wo_ref[...], (((1,), (1,)), ((), ())),
                                       preferred_element_type=F32).astype(BF16)
        dmod_ref[0] += jnp.concatenate([jnp.zeros((2, D), F32), d_gate_m, d_shift_f, d_scale_f, d_gate_f,
                                        jnp.zeros((2, D), F32)], axis=0)
        acc_ref[...] += jnp.concatenate([d_gmlp, d_gfin, loss * jnp.ones((1, D), F32), jnp.zeros((5, D), F32)], axis=0)

    whole = pl.BlockSpec(memory_space=pltpu.VMEM)

    def tok(w):
        return pl.BlockSpec((tm, w), lambda i: (i, 0))

    def vec():
        return pl.BlockSpec((1, D), lambda i: (0, 0))

    nb = T // seq
    return pl.pallas_call(
        body, name="mlp_fused", grid=(T // tm,),
        in_specs=[tok(D), tok(MIX_W), tok(D), pl.BlockSpec((1, 8, D), lambda i: (i // tps, 0, 0)), vec(), vec(),
                  whole, whole, whole],
        out_specs=[tok(D), tok(MIX_W), tok(D), tok(D), tok(D_FF), tok(D_FF), tok(D),
                   pl.BlockSpec((1, 8, D), lambda i: (i // tps, 0, 0)), pl.BlockSpec((8, D), lambda i: (0, 0))],
        out_shape=[jax.ShapeDtypeStruct((T, D), BF16), jax.ShapeDtypeStruct((T, MIX_W), BF16),
                   jax.ShapeDtypeStruct((T, D), F32), jax.ShapeDtypeStruct((T, D), BF16),
                   jax.ShapeDtypeStruct((T, D_FF), BF16), jax.ShapeDtypeStruct((T, D_FF), BF16),
                   jax.ShapeDtypeStruct((T, D), BF16), jax.ShapeDtypeStruct((nb, 8, D), F32),
                   jax.ShapeDtypeStruct((8, D), F32)],
        scratch_shapes=[pltpu.VMEM((tm, D_FF), F32)],
        compiler_params=_cparams(),
    )(x2, ymix, target, mod, g_mlp, g_final, w_out, w_up, w_down)


def _in_bwd(x2, dh1, dpb, mod, g_mix, w_cat, dmod_a, acc_a, seq):
    T = x2.shape[0]
    tm = min(1024, seq)
    tps = seq // tm
    steps = T // tm

    def body(x_ref, dh_ref, dpb_ref, mod_ref, g_ref, w_ref, dma_ref, acca_ref, dx_ref, dmod_ref, acc_ref):
        i = pl.program_id(0)

        @pl.when(i == 0)
        def _():
            acc_ref[...] = acca_ref[...]

        @pl.when(i % tps == 0)
        def _():
            dmod_ref[...] = dma_ref[...]

        du = lax.dot_general(dpb_ref[...], w_ref[...], (((1,), (1,)), ((), ())), preferred_element_type=F32)
        x = x_ref[...]
        md = mod_ref[0]
        g = g_ref[...]
        r = lax.rsqrt(jnp.mean(x * x, axis=-1, keepdims=True) + EPS)
        n1 = x * r
        d_scale = jnp.sum(du * (n1 * g), axis=0, keepdims=True)
        d_shift = jnp.sum(du, axis=0, keepdims=True)
        d_g = jnp.sum(du * (1.0 + md[1:2]) * n1, axis=0, keepdims=True)
        dn1 = du * (g * (1.0 + md[1:2]))
        dx_ref[...] = dh_ref[...] + r * (dn1 - n1 * jnp.mean(dn1 * n1, axis=-1, keepdims=True))
        dmod_ref[0] += jnp.concatenate([d_shift, d_scale, jnp.zeros((6, D), F32)], axis=0)
        acc_ref[...] += jnp.concatenate([jnp.zeros((3, D), F32), d_g, jnp.zeros((4, D), F32)], axis=0)

    whole = pl.BlockSpec(memory_space=pltpu.VMEM)
    nb = T // seq
    return pl.pallas_call(
        body, name="in_bwd", grid=(steps,),
        in_specs=[pl.BlockSpec((tm, D), lambda i: (i, 0)), pl.BlockSpec((tm, D), lambda i: (i, 0)),
                  pl.BlockSpec((tm, PROJ_W), lambda i: (i, 0)),
                  pl.BlockSpec((1, 8, D), lambda i: (i // tps, 0, 0)), pl.BlockSpec((1, D), lambda i: (0, 0)),
                  whole, pl.BlockSpec((1, 8, D), lambda i: (i // tps, 0, 0)), pl.BlockSpec((8, D), lambda i: (0, 0))],
        out_specs=[pl.BlockSpec((tm, D), lambda i: (i, 0)),
                   pl.BlockSpec((1, 8, D), lambda i: (i // tps, 0, 0)), pl.BlockSpec((8, D), lambda i: (0, 0))],
        out_shape=[jax.ShapeDtypeStruct((T, D), F32),
                   jax.ShapeDtypeStruct((nb, 8, D), F32), jax.ShapeDtypeStruct((8, D), F32)],
        compiler_params=_cparams(),
    )(x2, dh1, dpb, mod, g_mix, w_cat, dmod_a, acc_a)


def _dw_in(u_b, dpb, in_cols, shards):
    T = u_b.shape[0]
    bk = min(1024, T)
    nk = T // bk
    ns = len(shards)

    def body(*refs):
        u_ref, d_ref = refs[:2]
        sh_refs = refs[2:2 + ns]
        o_ref = refs[2 + ns]
        ga_refs = refs[3 + ns:3 + 2 * ns]
        acc, send, recv, loc = refs[3 + 2 * ns:]
        k = pl.program_id(0)
        gather = _Gather(sh_refs, ga_refs, send, recv, loc)
        gather.begin_hosted(k, nk)

        @pl.when(k == 0)
        def _():
            acc[...] = jnp.zeros_like(acc)

        acc[...] += jnp.dot(u_ref[...].T, d_ref[...], preferred_element_type=F32)

        @pl.when(k == nk - 1)
        def _():
            for j in range(N_DEV):
                o_ref[j] = acc[:, in_cols * j:in_cols * (j + 1)].astype(BF16)

        gather.end_hosted(k, nk)

    return pl.pallas_call(
        body, name="dw_in", grid=(nk,),
        in_specs=[pl.BlockSpec((bk, D), lambda k: (k, 0)), pl.BlockSpec((bk, PROJ_W), lambda k: (k, 0))]
        + [ANY_SPEC] * ns,
        out_specs=[pl.BlockSpec((N_DEV, D, in_cols), lambda k: (0, 0, 0))] + [ANY_SPEC] * ns,
        out_shape=[jax.ShapeDtypeStruct((N_DEV, D, in_cols), BF16)]
        + [jax.ShapeDtypeStruct((N_DEV,) + v.shape, v.dtype) for v in shards],
        scratch_shapes=[pltpu.VMEM((D, PROJ_W), F32)] + _gather_scratch(shards),
        compiler_params=_cparams(),
    )(u_b, dpb, *shards)


def _dw_blocks(a, b, name, by_rows, per_step=1, after=()):
    T, M = a.shape
    N = b.shape[1]
    bk = min(4096, T)
    nk = T // bk
    whole = pl.BlockSpec(memory_space=pltpu.VMEM)
    if by_rows:
        rows = M // N_DEV
        am = rows * per_step
        nblk = N_DEV // per_step
        a_spec, b_spec = pl.BlockSpec((bk, am), lambda i, k: (k, i)), whole
        out_blk, acc_shape = (per_step, rows, N), (am, N)
    else:
        cols = N // N_DEV
        nblk = N_DEV
        a_spec, b_spec = whole, pl.BlockSpec((bk, cols), lambda i, k: (k, i))
        out_blk, acc_shape = (1, M, cols), (M, cols)

    def body(a_ref, b_ref, *rest):
        o_ref, acc = rest[len(after):]
        k = pl.program_id(1)

        @pl.when(k == 0)
        def _():
            acc[...] = jnp.zeros_like(acc)

        tok = pl.ds(pl.multiple_of(k * bk, bk), bk)
        a_blk = a_ref[...] if by_rows else a_ref[tok, :]
        b_blk = b_ref[tok, :] if by_rows else b_ref[...]
        acc[...] += lax.dot_general(a_blk, b_blk, (((0,), (0,)), ((), ())), preferred_element_type=F32)

        @pl.when(k == nk - 1)
        def _():
            o_ref[...] = acc[...].reshape(out_blk).astype(BF16)

    return pl.pallas_call(
        body, name=name, grid=(nblk, nk), in_specs=[a_spec, b_spec] + [ANY_SPEC] * len(after),
        out_specs=pl.BlockSpec(out_blk, lambda i, k: (i, 0, 0)),
        out_shape=jax.ShapeDtypeStruct((N_DEV,) + out_blk[1:], BF16),
        scratch_shapes=[pltpu.VMEM(acc_shape, F32)],
        compiler_params=_cparams(),
    )(a, b, *after)


def _adam_parts(parts, w, m, v, name, me=None):
    rows, cols = w.shape
    br = rows
    for cand in range(rows, 15, -16):
        if rows % cand == 0 and cand * cols * 4 <= ADAM_BLOCK_BYTES:
            br = cand
            break

    if me is not None:
        sent, landed = parts

        def body_own(me_ref, own_ref, p_ref, w_ref, m_ref, v_ref, g_out, dl_out, m_out, v_out):
            g = None
            for k in range(N_DEV):
                blk = jnp.where(me_ref[0] == k, own_ref[0], p_ref[k]).astype(F32)
                g = blk if g is None else g + blk
            g_out[...] = g
            dl, mn, vn = _adam_math(w_ref[...], g, m_ref[...], v_ref[...])
            dl_out[...] = dl
            m_out[...] = mn
            v_out[...] = vn

        wspec = pl.BlockSpec((br, cols), lambda i, me_ref: (i, 0))
        grid_spec = pltpu.PrefetchScalarGridSpec(
            num_scalar_prefetch=1, grid=(rows // br,),
            in_specs=[pl.BlockSpec((1, br, cols), lambda i, me_ref: (me_ref[0], i, 0)),
                      pl.BlockSpec((N_DEV, br, cols), lambda i, me_ref: (0, i, 0)), wspec, wspec, wspec],
            out_specs=[wspec] * 4)
        return pl.pallas_call(
            body_own, name=name, grid_spec=grid_spec, out_shape=[jax.ShapeDtypeStruct((rows, cols), F32)] * 4,
            compiler_params=_cparams(),
        )(me.reshape(1).astype(jnp.int32), sent, landed, w, m, v)

    def body(p_ref, w_ref, m_ref, v_ref, g_out, dl_out, m_out, v_out):
        g = p_ref[0].astype(F32)
        for k in range(1, N_DEV):
            g = g + p_ref[k].astype(F32)
        g_out[...] = g
        dl, mn, vn = _adam_math(w_ref[...], g, m_ref[...], v_ref[...])
        dl_out[...] = dl
        m_out[...] = mn
        v_out[...] = vn

    wspec = pl.BlockSpec((br, cols), lambda i: (i, 0))
    return pl.pallas_call(
        body, name=name, grid=(rows // br,),
        in_specs=[pl.BlockSpec((N_DEV, br, cols), lambda i: (0, i, 0)), wspec, wspec, wspec],
        out_specs=[wspec] * 4, out_shape=[jax.ShapeDtypeStruct((rows, cols), F32)] * 4,
        compiler_params=_cparams(),
    )(parts, w, m, v)


def _adam_plain(g, w, m, v, name):
    def body(g_ref, w_ref, m_ref, v_ref, dl_out, m_out, v_out):
        dl, mn, vn = _adam_math(w_ref[...], g_ref[...], m_ref[...], v_ref[...])
        dl_out[...] = dl
        m_out[...] = mn
        v_out[...] = vn

    return pl.pallas_call(body, name=name, out_shape=[jax.ShapeDtypeStruct(w.shape, F32)] * 3,
                          compiler_params=_cparams())(g, w, m, v)


SMALL_PARAMS = ("b_ada", "g_mix", "conv_b", "dt_bias", "a_log", "d_skip", "g_ssd", "pool_scale", "g_mlp", "g_final")


def _small_adam(gathered, params):
    n_par = len(SMALL_PARAMS)
    nb = gathered[0].shape[1]

    def body(*refs):
        dmod_ref, acc_ref, conv_ref, vec_ref, hd_ref = refs[:5]
        par_refs = refs[5:5 + 3 * n_par]
        out_refs = refs[5 + 3 * n_par:5 + 7 * n_par]
        cw_out, acc_out = refs[5 + 7 * n_par:]

        def total(ref):
            t = ref[0]
            for k in range(1, N_DEV):
                t = t + ref[k]
            return t

        dm = total(dmod_ref)
        dmb = dm[0]
        for b in range(1, nb):
            dmb = dmb + dm[b]
        ac, cv, vc, hd = total(acc_ref), total(conv_ref), total(vec_ref), total(hd_ref)
        cw_out[...] = cv[0:4]
        acc_out[...] = ac
        grads = {
            "b_ada": jnp.concatenate([dmb[r:r + 1] for r in range(6)], axis=1), "g_mix": ac[3:4], "conv_b": cv[4:5],
            "dt_bias": hd[0:1, 0:N_HEADS], "a_log": hd[1:2, 0:N_HEADS], "d_skip": hd[2:3, 0:N_HEADS],
            "g_ssd": vc[0:1], "pool_scale": vc[1:2, 0:POOL_W], "g_mlp": ac[0:1], "g_final": ac[1:2],
        }
        for i, name in enumerate(SMALL_PARAMS):
            w_ref, m_ref, v_ref = par_refs[3 * i:3 * i + 3]
            g = grads[name]
            dl, mn, vn = _adam_math(w_ref[...], g, m_ref[...], v_ref[...])
            g_o, d_o, m_o, v_o = out_refs[4 * i:4 * i + 4]
            g_o[...] = g
            d_o[...] = dl
            m_o[...] = mn
            v_o[...] = vn

    flat = [a for name in SMALL_PARAMS for a in params[name]]
    out_shape = [jax.ShapeDtypeStruct(params[name][0].shape, F32) for name in SMALL_PARAMS for _ in range(4)]
    out_shape += [jax.ShapeDtypeStruct((4, CONV_CH), F32), jax.ShapeDtypeStruct((8, D), F32)]
    return pl.pallas_call(body, name="small_adam", out_shape=out_shape, compiler_params=_cparams())(*gathered, *flat)


def kernel(x, c, w_ada, b_ada, g_mix, w_in, conv_w, conv_b, dt_bias, a_log, d_skip, g_ssd, w_pool, pool_scale, w_out, g_mlp, w_up, w_down, g_final, loss_target, m_w_ada, m_b_ada, m_g_mix, m_w_in, m_conv_w, m_conv_b, m_dt_bias, m_a_log, m_d_skip, m_g_ssd, m_w_pool, m_pool_scale, m_w_out, m_g_mlp, m_w_up, m_w_down, m_g_final, v_w_ada, v_b_ada, v_g_mix, v_w_in, v_conv_w, v_conv_b, v_dt_bias, v_a_log, v_d_skip, v_g_ssd, v_w_pool, v_pool_scale, v_w_out, v_g_mlp, v_w_up, v_w_down, v_g_final):
    nb, seq, _ = x.shape
    T = nb * seq
    me = 4 * lax.axis_index("x") + 2 * lax.axis_index("y") + lax.axis_index("c")
    in_cols = w_in.shape[2]
    ada_cols = w_ada.shape[2]
    cw_cols = conv_w.shape[2]

    win_t = jnp.pad(w_in[0].astype(BF16).T, ((0, -in_cols % 16), (0, 0)))
    c_g, cw_g, win_g = _all_gather([c, conv_w[0], win_t], "ag_first")
    c_all = c_g.reshape(N_DEV * nb, D)
    cw_full = cw_g.transpose(1, 0, 2).reshape(4, CONV_CH)

    b_slice = lax.dynamic_slice(b_ada, (0, me * ada_cols), (1, ada_cols))
    mod_cols = _ada_fwd(c_all, w_ada[0], b_slice)
    (mod_g,) = _all_gather([mod_cols], "ag_mod")
    mod_all = mod_g.transpose(1, 0, 2).reshape(N_DEV * nb, 6, D)
    mod_mine = lax.dynamic_slice(mod_all, (me * nb, 0, 0), (nb, 6, D))
    mod = jnp.pad(mod_mine, ((0, 0), (0, 2), (0, 0)))

    x2 = x.reshape(T, D)
    tg2 = loss_target.reshape(T, D)
    heads = jnp.pad(jnp.concatenate([dt_bias, a_log, d_skip], axis=0), ((0, 5), (0, LANES - N_HEADS)))
    wpool_b = w_pool[0]
    u_b, pm, w_cat, wup_g = _mix_in(x2, mod, g_mix, win_g, in_cols, seq, [w_up[0].astype(BF16)])
    ymix, hstates, cvs, wout_g, wdn_g = _mixer_fwd(
        pm, cw_full, conv_b, heads, g_ssd, wpool_b, pool_scale, nb, seq,
        [w_out[0].astype(BF16), w_down[0].astype(BF16)])
    da_b, dym, dh1, u2_b, f_b, dup_b, ddn_b, dmod_a, acc_a = _mlp_fused(
        x2, ymix, tg2, mod, g_mlp, g_final.reshape(1, D), wout_g.reshape(MIX_W, D), wup_g, wdn_g, seq)

    gout_p = _dw_blocks(ymix, da_b, "dw_out", True, per_step=4)
    gup_p = _dw_blocks(u2_b, dup_b, "dw_up", False)
    ex_a = _exchange_start([gout_p, gup_p], "ga_start")
    gdn_p = _dw_blocks(f_b, ddn_b, "dw_down", True, after=[ex_a[4]])
    ex_b = _exchange_start([gdn_p], "gb_start")
    dpb, d_conv, d_heads, d_vec, d_wpool = _mixer_bwd(
        pm, cvs, dym, hstates, cw_full, conv_b, heads, g_ssd, wpool_b, pool_scale, nb, seq, after=[ex_b[4]])
    gin_p, conv_g, vec_g, heads_g, wpool_parts = _dw_in(
        u_b, dpb, in_cols, [d_conv, d_vec, d_heads, d_wpool.reshape(4 * LANES, LANES)])
    ex_in = _exchange_start([gin_p], "gin_start")
    grad_x2, dmod, acc = _in_bwd(x2, dh1, dpb, mod, g_mix + ex_in[4][0:1, 0:1], w_cat, dmod_a, acc_a, seq)

    gout_r, gup_r = _exchange_wait(ex_a, dmod, "ga_wait")
    (gdn_r,) = _exchange_wait(ex_b, dmod, "gb_wait")
    g_out, d_out, nm_out, nv_out = _adam_parts(gout_r, w_out[0], m_w_out[0], v_w_out[0], "adam_w_out", me)
    g_up, d_up, nm_up, nv_up = _adam_parts(gup_r, w_up[0], m_w_up[0], v_w_up[0], "adam_w_up", me)
    g_dn, d_dn, nm_dn, nv_dn = _adam_parts(gdn_r, w_down[0], m_w_down[0], v_w_down[0], "adam_w_down", me)

    dmod_g, acc_g = _all_gather([dmod, acc], "ag_small_bwd", after=[nm_out, nm_up, nm_dn])
    pool2 = (4 * LANES, LANES)
    wpool_outs = _adam_parts(wpool_parts, w_pool.reshape(pool2), m_w_pool.reshape(pool2), v_w_pool.reshape(pool2),
                             "adam_w_pool")
    small_params = {
        "b_ada": (b_ada, m_b_ada, v_b_ada), "g_mix": (g_mix, m_g_mix, v_g_mix), "conv_b": (conv_b, m_conv_b, v_conv_b),
        "dt_bias": (dt_bias, m_dt_bias, v_dt_bias), "a_log": (a_log, m_a_log, v_a_log),
        "d_skip": (d_skip, m_d_skip, v_d_skip), "g_ssd": (g_ssd, m_g_ssd, v_g_ssd),
        "pool_scale": (pool_scale, m_pool_scale, v_pool_scale), "g_mlp": (g_mlp, m_g_mlp, v_g_mlp),
        "g_final": tuple(a.reshape(1, D) for a in (g_final, m_g_final, v_g_final)),
    }
    small_res = _small_adam([dmod_g, acc_g, conv_g, vec_g, heads_g], small_params)
    g_cw_full, acc_sum = small_res[-2:]
    loss = acc_sum[2, 0]

    g_cw = lax.dynamic_slice(g_cw_full, (0, me * cw_cols), (4, cw_cols))
    d_cwp, nm_cwp, nv_cwp = _adam_plain(g_cw, conv_w[0], m_conv_w[0], v_conv_w[0], "adam_conv_w")

    dmod_all = dmod_g[:, :, 0:6].reshape(N_DEV * nb, 6 * D)
    dmod_slice = lax.dynamic_slice(dmod_all, (0, me * ada_cols), (N_DEV * nb, ada_cols))
    g_ada, d_ada, nm_ada, nv_ada = _ada_bwd_adam(c_all, dmod_slice, w_ada[0], m_w_ada[0], v_w_ada[0])

    ex_after = nm_ada[0:8, 0:LANES] + acc_sum[:, 0:LANES]
    (gin_r,) = _exchange_wait(ex_in, ex_after, "gin_wait")
    g_in, d_in, nm_in, nv_in = _adam_parts(gin_r, w_in[0], m_w_in[0], v_w_in[0], "adam_w_in", me)

    def small_outs(kind, wpool):
        res = {name: small_res[4 * i + kind] for i, name in enumerate(SMALL_PARAMS)}
        res["g_final"] = res["g_final"].reshape(D)
        res["w_pool"] = wpool.reshape(1, 4, LANES, LANES)
        return res

    def big_outs(ada, win, cwp, wout, wup, wdn):
        return {"w_ada": ada[None], "w_in": win.reshape(1, D, in_cols), "conv_w": cwp[None], "w_out": wout[None],
                "w_up": wup[None], "w_down": wdn[None]}

    order = ["w_ada", "b_ada", "g_mix", "w_in", "conv_w", "conv_b", "dt_bias", "a_log", "d_skip", "g_ssd", "w_pool",
             "pool_scale", "w_out", "g_mlp", "w_up", "w_down", "g_final"]
    groups = [
        {**small_outs(0, wpool_outs[0]), **big_outs(g_ada, g_in, g_cw, g_out, g_up, g_dn)},
        {**small_outs(1, wpool_outs[1]), **big_outs(d_ada, d_in, d_cwp, d_out, d_up, d_dn)},
        {**small_outs(2, wpool_outs[2]), **big_outs(nm_ada, nm_in, nm_cwp, nm_out, nm_up, nm_dn)},
        {**small_outs(3, wpool_outs[3]), **big_outs(nv_ada, nv_in, nv_cwp, nv_out, nv_up, nv_dn)},
    ]
    outs = [loss, grad_x2.reshape(nb, seq, D)]
    for grp in groups:
        outs += [grp[n] for n in order]
    return tuple(outs)
```

```python
import functools

import jax
import jax.numpy as jnp
from jax import lax
from jax.experimental import pallas as pl
from jax.experimental.pallas import tpu as pltpu

F32, BF16 = jnp.float32, jnp.bfloat16
MESH = pl.DeviceIdType.MESH
N_DEV = 8
D = 1024
LANES = 128
CHUNK = 128
POOL_W = 512
WINDOWS = (2, 4, 8, 16)
N_HEADS = 16
HEAD_DIM = 64
N_GROUPS = 2
GROUP_W = 512
N_STATE = 128
CONV_CH = 1536
OFF_Z, OFF_XBC, OFF_DT, IN_W = 512, 1536, 3072, 3088
PROJ_W = OFF_DT + LANES
MIX_W = 1536
D_FF = 4096
FF_BLK = 512
EPS = 1e-5
LR, B1, B2, AEPS, WD, STEP = 0.001, 0.9, 0.999, 1e-08, 0.01, 10
POOL_HALO = 16
CONV_HALO = 8
VMEM_LIMIT = 56 << 20
ADAM_BLOCK_BYTES = 1 << 20


def _cparams(**kw):
    return pltpu.CompilerParams(vmem_limit_bytes=VMEM_LIMIT, **kw)


def _mm(a, b):
    return jnp.dot(a.astype(BF16), b.astype(BF16), preferred_element_type=F32)


def _mm_nt(a, b):
    return lax.dot_general(a.astype(BF16), b.astype(BF16), (((1,), (1,)), ((), ())), preferred_element_type=F32)


def _mm_tn(a, b):
    return lax.dot_general(a.astype(BF16), b.astype(BF16), (((0,), (0,)), ((), ())), preferred_element_type=F32)


def _split_bf16(v, terms):
    parts, rest = [], v
    for t in range(terms):
        p = rest.astype(BF16)
        parts.append(p)
        if t + 1 < terms:
            rest = rest - p.astype(F32)
    return parts


def _dot01(a, b, terms, split_lhs=True):
    if split_lhs:
        bb = b.astype(BF16)
        prods = [jnp.dot(p, bb, preferred_element_type=F32) for p in _split_bf16(a, terms)]
    else:
        ab = a.astype(BF16)
        prods = [jnp.dot(ab, p, preferred_element_type=F32) for p in _split_bf16(b, terms)]
    out = prods[0]
    for q in prods[1:]:
        out = out + q
    return out


def _sigmoid(v):
    return 1.0 / (1.0 + jnp.exp(-v))


def _expand_mat():
    r = lax.broadcasted_iota(jnp.int32, (LANES, D), 0)
    c = lax.broadcasted_iota(jnp.int32, (LANES, D), 1)
    return (r == c // HEAD_DIM).astype(F32)


def _reduce_mat():
    r = lax.broadcasted_iota(jnp.int32, (D, LANES), 0)
    c = lax.broadcasted_iota(jnp.int32, (D, LANES), 1)
    return (c == r // HEAD_DIM).astype(F32)


def _pos():
    return lax.axis_index("x"), lax.axis_index("y"), lax.axis_index("c")


GATHER_PIECES = 4
GATHER_PIECE_BYTES = 96 << 10


def _pieces(shape, dtype):
    rows = shape[0]
    size = jnp.dtype(dtype).itemsize
    for d in shape:
        size *= d
    whole_tiles = rows % (GATHER_PIECES * 16) == 0
    return GATHER_PIECES if whole_tiles and size // GATHER_PIECES >= GATHER_PIECE_BYTES else 1


class _Gather:
    def __init__(self, x_refs, o_refs, send, recv, loc):
        self.x_refs, self.o_refs, self.send, self.recv, self.loc = x_refs, o_refs, send, recv, loc
        self.n = len(x_refs)
        self.pieces = [_pieces(r.shape, r.dtype) for r in x_refs]
        self.base = [7 * sum(self.pieces[:a]) for a in range(self.n)]
        x, y, c = _pos()
        self.c = c
        self.me, self.sib = (x, y, c), (x, y, 1 - c)
        self.chips = [(1 - x, y), (x, 1 - y), (1 - x, 1 - y)]

    def _rows(self, a, p):
        rows = self.x_refs[a].shape[0] // self.pieces[a]
        return pl.ds(p * rows, rows)

    def _cp(self, a, p, k, block, to, own=False):
        dst = self.o_refs[a].at[4 * block[0] + 2 * block[1] + block[2], self._rows(a, p)]
        sem = self.base[a] + 7 * p + k
        return pltpu.make_async_remote_copy(
            src_ref=self.x_refs[a].at[self._rows(a, p)] if own else dst, dst_ref=dst,
            send_sem=self.send.at[sem], recv_sem=self.recv.at[sem], device_id=to, device_id_type=MESH)

    def _mine(self, a):
        me = self.me
        return pltpu.make_async_copy(self.x_refs[a], self.o_refs[a].at[4 * me[0] + 2 * me[1] + me[2]], self.loc.at[a])

    def _first(self, a, p):
        cps = [self._cp(a, p, 0, self.me, self.sib, own=True)]
        return cps + [self._cp(a, p, 1 + j, self.me, (*chip, self.c), own=True) for j, chip in enumerate(self.chips)]

    def _passed(self, a, p, j):
        return self._cp(a, p, 4 + j, (*self.chips[j], self.c), self.sib)

    def start(self):
        for a in range(self.n):
            self._mine(a).start()
        for p in range(max(self.pieces)):
            for a in range(self.n):
                if p < self.pieces[a]:
                    for cp in self._first(a, p):
                        cp.start()

    def forward(self, p):
        for j, chip in enumerate(self.chips):
            for a in range(self.n):
                if p < self.pieces[a]:
                    self._cp(a, p, 1 + j, (*chip, self.c), self.me).wait_recv()
                    self._passed(a, p, j).start()

    def finish(self):
        for a in range(self.n):
            for p in range(self.pieces[a]):
                self._cp(a, p, 0, self.sib, self.me).wait_recv()
                for j, chip in enumerate(self.chips):
                    self._cp(a, p, 4 + j, (*chip, 1 - self.c), self.me).wait_recv()
        for a in range(self.n):
            for p in range(self.pieces[a]):
                for cp in self._first(a, p):
                    cp.wait_send()
                for j in range(3):
                    self._passed(a, p, j).wait_send()
            self._mine(a).wait()

    def begin_hosted(self, step, steps):
        @pl.when(step == 0)
        def _():
            self.start()

        n_p = max(self.pieces)
        for p in range(n_p):
            @pl.when(step == min(((p + 1) * 7 * steps) // (8 * n_p), steps - 1))
            def _():
                self.forward(p)

    def end_hosted(self, step, steps):
        @pl.when(step == steps - 1)
        def _():
            self.finish()


class _Exchange:
    def __init__(self, x_refs, o_refs, send, recv, loc):
        self.x_refs, self.o_refs, self.send, self.recv, self.loc = x_refs, o_refs, send, recv, loc
        self.n = len(x_refs)
        x, y, c = _pos()
        self.me_i = 4 * x + 2 * y + c
        self.peers = []
        for k in range(1, N_DEV):
            px = 1 - x if (k >> 2) & 1 else x
            py = 1 - y if (k >> 1) & 1 else y
            pc = 1 - c if k & 1 else c
            self.peers.append(((px, py, pc), 4 * px + 2 * py + pc))

    def _mine(self, a):
        return pltpu.make_async_copy(self.x_refs[a].at[self.me_i], self.o_refs[a].at[self.me_i], self.loc.at[a])

    def _cp(self, a, k, landing):
        peer, peer_i = self.peers[k]
        return pltpu.make_async_remote_copy(
            src_ref=self.x_refs[a].at[peer_i], dst_ref=self.o_refs[a].at[landing],
            send_sem=self.send.at[a * 7 + k], recv_sem=self.recv.at[a * 7 + k],
            device_id=peer, device_id_type=MESH)

    def start(self):
        for a in range(self.n):
            self._mine(a).start()
            for k in range(N_DEV - 1):
                self._cp(a, k, self.me_i).start()

    def finish(self):
        for a in range(self.n):
            for k in range(N_DEV - 1):
                self._cp(a, k, self.peers[k][1]).wait_recv()
        for a in range(self.n):
            for k in range(N_DEV - 1):
                self._cp(a, k, self.me_i).wait_send()
            self._mine(a).wait()


def _gather_scratch(xs):
    n_sem = 7 * sum(_pieces(v.shape, v.dtype) for v in xs)
    return [pltpu.SemaphoreType.DMA((n_sem,)), pltpu.SemaphoreType.DMA((n_sem,)), pltpu.SemaphoreType.DMA((len(xs),))]


ANY_SPEC = pl.BlockSpec(memory_space=pl.ANY)


def _all_gather(xs, name, after=()):
    n, na = len(xs), len(after)

    def body(*refs):
        g = _Gather(refs[:n], refs[n + na:2 * n + na], *refs[2 * n + na:])
        g.start()
        for p in range(max(g.pieces)):
            g.forward(p)
        g.finish()

    return pl.pallas_call(
        body, name=name,
        out_shape=[jax.ShapeDtypeStruct((N_DEV,) + v.shape, v.dtype) for v in xs],
        in_specs=[ANY_SPEC] * (n + na), out_specs=[ANY_SPEC] * n, scratch_shapes=_gather_scratch(xs),
    )(*xs, *after)


HBM_SPEC = pl.BlockSpec(memory_space=pltpu.HBM)
SEM_SPEC = pl.BlockSpec(memory_space=pltpu.SEMAPHORE)
VMEM_SPEC = pl.BlockSpec(memory_space=pltpu.VMEM)
SPLIT_EFFECT = pltpu.SideEffectType.DATAFLOW_SIDE_EFFECTING


def _in_hbm(v):
    return pltpu.with_memory_space_constraint(v, pltpu.HBM)


def _exchange_start(blocks, name):
    n = len(blocks)

    def body(*refs):
        x_refs, land_refs = refs[:n], refs[n:2 * n]
        send, recv = refs[2 * n:2 * n + 2]
        token = refs[-1]
        ex = _Exchange(x_refs, land_refs, send, recv, None)
        for a in range(n):
            for k in range(N_DEV - 1):
                ex._cp(a, k, ex.me_i).start()
        token[...] = jnp.zeros_like(token)

    lands = [lax.empty(v.shape, v.dtype) for v in blocks]
    hbm = tuple(pltpu.HBM(v.shape, v.dtype) for v in list(blocks) + lands)
    n_sem = (N_DEV - 1) * n
    out = pl.pallas_call(
        body, name=name,
        out_shape=(pltpu.SemaphoreType.DMA((n_sem,)), pltpu.SemaphoreType.DMA((n_sem,))) + hbm
        + (jax.ShapeDtypeStruct((8, LANES), F32),),
        in_specs=(HBM_SPEC,) * (2 * n), out_specs=(SEM_SPEC, SEM_SPEC) + (HBM_SPEC,) * (2 * n) + (VMEM_SPEC,),
        input_output_aliases={i: i + 2 for i in range(2 * n)},
        compiler_params=pltpu.CompilerParams(has_side_effects=SPLIT_EFFECT),
    )(*[_in_hbm(v) for v in list(blocks) + lands])
    return out[0], out[1], list(out[2:2 + n]), list(out[2 + n:2 + 2 * n]), out[-1]


def _exchange_wait(ex, after, name):
    send, recv, thru, lands, _ = ex
    n = len(thru)

    def body(*refs):
        x_refs, land_refs = refs[:n], refs[n:2 * n]
        send_ref, recv_ref = refs[2 * n:2 * n + 2]
        e = _Exchange(x_refs, land_refs, send_ref, recv_ref, None)
        for a in range(n):
            for k in range(N_DEV - 1):
                e._cp(a, k, e.me_i).wait_send()
                e._cp(a, k, e.peers[k][1]).wait_recv()

    hbm = tuple(pltpu.HBM(v.shape, v.dtype) for v in list(thru) + list(lands))
    out = pl.pallas_call(
        body, name=name, out_shape=hbm,
        in_specs=(HBM_SPEC,) * (2 * n) + (SEM_SPEC, SEM_SPEC, ANY_SPEC), out_specs=(HBM_SPEC,) * (2 * n),
        input_output_aliases={i: i for i in range(2 * n)},
        compiler_params=pltpu.CompilerParams(has_side_effects=SPLIT_EFFECT),
    )(*thru, *lands, send, recv, after)
    return list(zip(out[:n], out[n:]))


def _ada_fwd(c_all, w_ada, b_slice):
    def body(c_ref, w_ref, b_ref, o_ref):
        cv = c_ref[...]
        act = cv * _sigmoid(cv)
        o_ref[...] = _mm(act, w_ref[...]) + b_ref[...]

    nb, nc = c_all.shape[0], w_ada.shape[1]
    return pl.pallas_call(body, name="ada_fwd", out_shape=jax.ShapeDtypeStruct((nb, nc), F32),
                          compiler_params=_cparams())(c_all, w_ada, b_slice)


def _adam_math(w, g, m, v):
    m = B1 * m + (1.0 - B1) * g
    v = B2 * v + (1.0 - B2) * jnp.square(g)
    m_hat = m / (1.0 - B1 ** STEP)
    v_hat = v / (1.0 - B2 ** STEP)
    delta = -LR * (m_hat / (jnp.sqrt(v_hat) + AEPS) + WD * w)
    return delta, m, v


def _ada_bwd_adam(c_all, dmod_slice, w, m, v):
    rows, cols = w.shape
    br = 256

    def body(c_ref, d_ref, w_ref, m_ref, v_ref, g_out, dl_out, m_out, v_out):
        cv = c_ref[...]
        act = cv * _sigmoid(cv)
        g = _mm_tn(act, d_ref[...])
        g_out[...] = g
        dl, mn, vn = _adam_math(w_ref[...], g, m_ref[...], v_ref[...])
        dl_out[...] = dl
        m_out[...] = mn
        v_out[...] = vn

    nb = c_all.shape[0]
    wspec = pl.BlockSpec((br, cols), lambda i: (i, 0))
    return pl.pallas_call(
        body, name="ada_bwd_adam", grid=(rows // br,),
        in_specs=[pl.BlockSpec((nb, br), lambda i: (0, i)), pl.BlockSpec((nb, cols), lambda i: (0, 0)),
                  wspec, wspec, wspec],
        out_specs=[wspec] * 4, out_shape=[jax.ShapeDtypeStruct((rows, cols), F32)] * 4,
        compiler_params=_cparams(),
    )(c_all, dmod_slice, w, m, v)


def _mix_in(x2, mod, g_mix, win_g, in_cols, seq, shards):
    T = x2.shape[0]
    tm = min(512, seq)
    tps = seq // tm
    blk_rows = win_g.shape[1]
    pad_rows = -blk_rows % (4 * LANES)
    ns = len(shards)
    steps = T // tm

    def body(*refs):
        x_ref, mod_ref, g_ref, wb_ref = refs[:4]
        sh_refs = refs[4:4 + ns]
        u_ref, pm_ref, wc_ref = refs[4 + ns:7 + ns]
        ga_refs = refs[7 + ns:7 + 2 * ns]
        w_ref, send, recv, loc = refs[7 + 2 * ns:]
        step = pl.program_id(0)
        gather = _Gather(sh_refs, ga_refs, send, recv, loc)
        gather.begin_hosted(step, steps)

        @pl.when(step == 0)
        def _():
            w_ref[:, OFF_DT:] = jnp.zeros((D, PROJ_W - OFF_DT), BF16)
            for j in range(N_DEV):
                blk = jnp.concatenate([wb_ref[j], jnp.zeros((pad_rows, D), BF16)], axis=0)
                w_ref[:, in_cols * j:in_cols * (j + 1)] = blk.T[:, :in_cols]
            wc_ref[...] = w_ref[...]

        x = x_ref[...]
        r = lax.rsqrt(jnp.mean(x * x, axis=-1, keepdims=True) + EPS)
        md = mod_ref[0]
        u = (x * r * g_ref[...]) * (1.0 + md[1:2]) + md[0:1]
        ub = u.astype(BF16)
        u_ref[...] = ub
        pm_ref[...] = jnp.dot(ub, w_ref[...], preferred_element_type=F32)
        gather.end_hosted(step, steps)

    whole = pl.BlockSpec(memory_space=pltpu.VMEM)
    return pl.pallas_call(
        body, name="mix_in", grid=(T // tm,),
        in_specs=[pl.BlockSpec((tm, D), lambda i: (i, 0)), pl.BlockSpec((1, 8, D), lambda i: (i // tps, 0, 0)),
                  pl.BlockSpec((1, D), lambda i: (0, 0)), whole] + [ANY_SPEC] * ns,
        out_specs=[pl.BlockSpec((tm, D), lambda i: (i, 0)), pl.BlockSpec((tm, PROJ_W), lambda i: (i, 0)),
                   pl.BlockSpec((D, PROJ_W), lambda i: (0, 0))] + [ANY_SPEC] * ns,
        out_shape=[jax.ShapeDtypeStruct((T, D), BF16), jax.ShapeDtypeStruct((T, PROJ_W), F32),
                   jax.ShapeDtypeStruct((D, PROJ_W), BF16)]
        + [jax.ShapeDtypeStruct((N_DEV,) + v.shape, v.dtype) for v in shards],
        scratch_shapes=[pltpu.VMEM((D, PROJ_W), BF16)] + _gather_scratch(shards),
        compiler_params=_cparams(),
    )(x2, mod, g_mix, win_g, *shards)


def _chunk_forward(up, z, ux, dtin, halo_p, halo_x, hprev, cw, cb, hp, gssd, wpool, pscale, t0, y_scr, cv=None):
    L = CHUNK
    out = {}
    row = lax.broadcasted_iota(jnp.int32, (L, 1), 0)
    t = (t0 + row + 1).astype(F32)
    e = jnp.concatenate([halo_p, up], axis=0)
    s2 = e + pltpu.roll(e, 1, 0)
    s4 = s2 + pltpu.roll(s2, 2, 0)
    s8 = s4 + pltpu.roll(s4, 4, 0)
    s16 = s8 + pltpu.roll(s8, 8, 0)
    sums = (s2, s4, s8, s16)
    p, inv, yp = [], [], []
    for gi, w in enumerate(WINDOWS):
        sl = slice(gi * LANES, (gi + 1) * LANES)
        ic = 1.0 / jnp.minimum(t, float(w))
        pg = sums[gi][POOL_HALO:, sl] * ic - up[:, sl]
        p.append(pg)
        inv.append(ic)
        yp.append(_mm(pg, wpool[gi]))
    out["p"], out["inv"], out["yp"] = p, inv, yp
    out["y_pool"] = jnp.concatenate(yp, axis=1) * pscale
    if cv is None:
        ex = jnp.concatenate([halo_x, ux], axis=0)
        taps = [pltpu.roll(ex, 3, 0)[CONV_HALO:], pltpu.roll(ex, 2, 0)[CONV_HALO:], pltpu.roll(ex, 1, 0)[CONV_HALO:], ux]
        cv = cb + taps[0] * cw[0:1] + taps[1] * cw[1:2] + taps[2] * cw[2:3] + taps[3] * cw[3:4]
    sg = _sigmoid(cv)
    xbc = cv * sg
    out["cv"], out["sg"] = cv, sg
    X = xbc[:, :D]
    Bm = xbc[:, D:D + N_GROUPS * N_STATE]
    Cm = xbc[:, D + N_GROUPS * N_STATE:]
    pre = dtin + hp[0:1]
    dt = jnp.maximum(pre, 0.0) + jnp.log(1.0 + jnp.exp(-jnp.abs(pre)))
    a_row = -jnp.exp(hp[1:2])
    da = dt * a_row
    ri = lax.broadcasted_iota(jnp.int32, (L, L), 0)
    ci = lax.broadcasted_iota(jnp.int32, (L, L), 1)
    causal = ri >= ci
    cum = _dot01(causal.astype(F32), da, 3, split_lhs=False)
    cum_t = cum.T
    cum_last = cum[L - 1:L]
    eo = jnp.exp(cum)
    dec = jnp.exp(cum_last - cum)
    cd = jnp.exp(cum_last)
    exm = _expand_mat()
    rows8 = jnp.concatenate([cd, hp[2:3], jnp.zeros((6, LANES), F32)], axis=0)
    rep = _dot01(jnp.concatenate([dt, eo, dec, rows8], axis=0), exm, 2)
    dt_rep, eo_rep, dec_rep = rep[0:L], rep[L:2 * L], rep[2 * L:3 * L]
    cd_rep, dskip_rep = rep[3 * L:3 * L + 1], rep[3 * L + 1:3 * L + 2]
    xdt = X * dt_rep
    out.update(X=X, Bm=Bm, Cm=Cm, pre=pre, dt=dt, a_row=a_row, cum=cum, cum_t=cum_t, eo=eo, dec=dec, cd=cd,
               dt_rep=dt_rep, eo_rep=eo_rep, dec_rep=dec_rep, cd_rep=cd_rep, dskip_rep=dskip_rep, xdt=xdt,
               causal=causal, anti=(ri <= ci).astype(F32), exm=exm)
    G, lms, yoff, hnew, xdec = [], [], [], [], []
    for g in range(N_GROUPS):
        gs = slice(g * GROUP_W, (g + 1) * GROUP_W)
        Bg = Bm[:, g * N_STATE:(g + 1) * N_STATE]
        Cg = Cm[:, g * N_STATE:(g + 1) * N_STATE]
        Gg = _mm_nt(Cg, Bg)
        G.append(Gg)
        for hh in range(N_HEADS // N_GROUPS):
            h = g * (N_HEADS // N_GROUPS) + hh
            seg = cum[:, h:h + 1] - cum_t[h:h + 1, :]
            lm = jnp.where(causal, jnp.exp(jnp.minimum(seg, 0.0)), 0.0)
            lms.append(lm)
            hs = slice(h * HEAD_DIM, (h + 1) * HEAD_DIM)
            y_scr[:, hs] = _mm(Gg * lm, xdt[:, hs])
        xd = xdt[:, gs] * dec_rep[:, gs]
        xdec.append(xd)
        sgm = _mm_tn(Bg, xd)
        yoff.append(_mm(Cg, hprev[g]) * eo_rep[:, gs])
        hnew.append(hprev[g] * cd_rep[:, gs] + sgm)
    out.update(G=G, lms=lms, yoff=yoff, hnew=hnew, xdec=xdec)
    y = y_scr[...] + jnp.concatenate(yoff, axis=1) + dskip_rep * X
    sz = _sigmoid(z)
    silz = z * sz
    yz = y * silz
    rg, yn = [], []
    for g in range(N_GROUPS):
        gs = slice(g * GROUP_W, (g + 1) * GROUP_W)
        r = lax.rsqrt(jnp.mean(yz[:, gs] * yz[:, gs], axis=-1, keepdims=True) + EPS)
        rg.append(r)
        yn.append(yz[:, gs] * r)
    yn = jnp.concatenate(yn, axis=1)
    out.update(y=y, sz=sz, silz=silz, rg=rg, yn=yn)
    out["y_ssd"] = yn * gssd
    return out


def _mixer_fwd(pm, cw, cb, hp, gssd, wpool, pscale, nb, seq, shards):
    nc = seq // CHUNK
    ns = len(shards)
    steps = nb * nc

    def body(*refs):
        pm_ref, cw_ref, cb_ref, hp_ref, gs_ref, wp_ref, ps_ref = refs[:7]
        sh_refs = refs[7:7 + ns]
        ym_ref, hs_ref, cv_ref = refs[7 + ns:10 + ns]
        ga_refs = refs[10 + ns:10 + 2 * ns]
        halo_p, halo_x, state, y_scr, send, recv, loc = refs[10 + 2 * ns:]
        c = pl.program_id(1)
        step = pl.program_id(0) * nc + c
        gather = _Gather(sh_refs, ga_refs, send, recv, loc)
        gather.begin_hosted(step, steps)

        @pl.when(c == 0)
        def _():
            halo_p[...] = jnp.zeros_like(halo_p)
            halo_x[...] = jnp.zeros_like(halo_x)
            state[...] = jnp.zeros_like(state)

        up = pm_ref[:, 0:POOL_W]
        z = pm_ref[:, OFF_Z:OFF_XBC]
        ux = pm_ref[:, OFF_XBC:OFF_DT]
        hprev = [state[0], state[1]]
        hs_ref[0, 0, 0] = hprev[0]
        hs_ref[0, 0, 1] = hprev[1]
        o = _chunk_forward(up, z, ux, pm_ref[:, OFF_DT:], halo_p[...], halo_x[...], hprev, cw_ref[...], cb_ref[...],
                           hp_ref[...], gs_ref[...], wp_ref[...], ps_ref[...], c * CHUNK, y_scr)
        ym_ref[:, 0:POOL_W] = o["y_pool"].astype(BF16)
        ym_ref[:, POOL_W:] = o["y_ssd"].astype(BF16)
        cv_ref[...] = o["cv"]
        state[0] = o["hnew"][0]
        state[1] = o["hnew"][1]
        halo_p[...] = up[CHUNK - POOL_HALO:]
        halo_x[...] = ux[CHUNK - CONV_HALO:]
        gather.end_hosted(step, steps)

    def full(shape):
        return pl.BlockSpec(shape, lambda b, c: (0,) * len(shape))

    T = nb * seq
    return pl.pallas_call(
        body, name="mixer_fwd", grid=(nb, nc),
        in_specs=[pl.BlockSpec((CHUNK, PROJ_W), lambda b, c: (b * nc + c, 0)),
                  full((4, CONV_CH)), full((1, CONV_CH)), full((8, LANES)), full((1, D)),
                  full((4, LANES, LANES)), full((1, POOL_W))] + [ANY_SPEC] * ns,
        out_specs=[pl.BlockSpec((CHUNK, MIX_W), lambda b, c: (b * nc + c, 0)),
                   pl.BlockSpec((1, 1, N_GROUPS, N_STATE, GROUP_W), lambda b, c: (b, c, 0, 0, 0)),
                   pl.BlockSpec((CHUNK, CONV_CH), lambda b, c: (b * nc + c, 0))] + [ANY_SPEC] * ns,
        out_shape=[jax.ShapeDtypeStruct((T, MIX_W), BF16),
                   jax.ShapeDtypeStruct((nb, nc, N_GROUPS, N_STATE, GROUP_W), F32),
                   jax.ShapeDtypeStruct((T, CONV_CH), F32)]
        + [jax.ShapeDtypeStruct((N_DEV,) + v.shape, v.dtype) for v in shards],
        scratch_shapes=[pltpu.VMEM((POOL_HALO, POOL_W), F32), pltpu.VMEM((CONV_HALO, CONV_CH), F32),
                        pltpu.VMEM((N_GROUPS, N_STATE, GROUP_W), F32), pltpu.VMEM((CHUNK, D), F32)] + _gather_scratch(shards),
        compiler_params=_cparams(),
    )(pm, cw, cb, hp, gssd, wpool, pscale, *shards)


def _mixer_bwd(pm, cvs, dym, hstates, cw, cb, hp, gssd, wpool, pscale, nb, seq, after=()):
    nc = seq // CHUNK
    hpg = N_HEADS // N_GROUPS
    na = len(after)

    def body(*refs):
        (pm_ref, hpool_ref, cv_ref, dy_ref, hs_ref, cw_ref, cb_ref, hp_ref, gs_ref, wp_ref, ps_ref) = refs[:11]
        dpm_ref, dconv_ref, dhp_ref, dvec_ref, dwp_ref = refs[11 + na:16 + na]
        nxt_q, nxt_cv, rstate, y_scr, dx_scr = refs[16 + na:]
        b = pl.program_id(0)
        ci = pl.program_id(1)
        c = nc - 1 - ci

        @pl.when((b == 0) & (ci == 0))
        def _():
            for r in (dconv_ref, dhp_ref, dvec_ref, dwp_ref):
                r[...] = jnp.zeros_like(r)

        @pl.when(ci == 0)
        def _():
            nxt_q[...] = jnp.zeros_like(nxt_q)
            nxt_cv[...] = jnp.zeros_like(nxt_cv)
            rstate[...] = jnp.zeros_like(rstate)

        first = (c > 0).astype(F32)
        up = pm_ref[:, 0:POOL_W]
        z = pm_ref[:, OFF_Z:OFF_XBC]
        ux = pm_ref[:, OFF_XBC:OFF_DT]
        halo_p = hpool_ref[...] * first
        hprev = [hs_ref[0, 0, 0], hs_ref[0, 0, 1]]
        cw, cb, hp, gssd, wpool, pscale = cw_ref[...], cb_ref[...], hp_ref[...], gs_ref[...], wp_ref[...], ps_ref[...]
        o = _chunk_forward(up, z, ux, pm_ref[:, OFF_DT:], halo_p, None, hprev, cw, cb, hp, gssd, wpool, pscale,
                           c * CHUNK, y_scr, cv=cv_ref[...])
        L = CHUNK
        dy_pool = dy_ref[:, 0:POOL_W].astype(F32)
        dy_ssd = dy_ref[:, POOL_W:].astype(F32)

        dvec_ref[1:2, 0:POOL_W] += jnp.sum(dy_pool * jnp.concatenate(o["yp"], axis=1), axis=0, keepdims=True)
        dyp = dy_pool * pscale
        qs = []
        dps = []
        for gi in range(len(WINDOWS)):
            sl = slice(gi * LANES, (gi + 1) * LANES)
            dwp_ref[gi] += _mm_tn(o["p"][gi], dyp[:, sl])
            dpg = _mm_nt(dyp[:, sl], wpool[gi])
            dps.append(dpg)
            qs.append(dpg * o["inv"][gi])
        q = jnp.concatenate(qs, axis=1)
        e = jnp.concatenate([q, nxt_q[...]], axis=0)
        n = L + POOL_HALO
        s2 = e + pltpu.roll(e, n - 1, 0)
        s4 = s2 + pltpu.roll(s2, n - 2, 0)
        s8 = s4 + pltpu.roll(s4, n - 4, 0)
        s16 = s8 + pltpu.roll(s8, n - 8, 0)
        sums = (s2, s4, s8, s16)
        for gi in range(len(WINDOWS)):
            sl = slice(gi * LANES, (gi + 1) * LANES)
            dpm_ref[:, sl] = (sums[gi][:L, sl] - dps[gi]).astype(BF16)
        nxt_q[...] = q[:POOL_HALO]

        yn, y, silz, sz = o["yn"], o["y"], o["silz"], o["sz"]
        dvec_ref[0:1] += jnp.sum(dy_ssd * yn, axis=0, keepdims=True)
        dyn = dy_ssd * gssd
        dyz = []
        for g in range(N_GROUPS):
            gs = slice(g * GROUP_W, (g + 1) * GROUP_W)
            mean = jnp.mean(dyn[:, gs] * yn[:, gs], axis=-1, keepdims=True)
            dyz.append(o["rg"][g] * (dyn[:, gs] - yn[:, gs] * mean))
        dyz = jnp.concatenate(dyz, axis=1)
        dyv = dyz * silz
        dpm_ref[:, OFF_Z:OFF_XBC] = (dyz * y * (sz * (1.0 + z * (1.0 - sz)))).astype(BF16)

        X, Bm, Cm, xdt = o["X"], o["Bm"], o["Cm"], o["xdt"]
        exm = o["exm"]
        rdm = _reduce_mat()
        lane = lax.broadcasted_iota(jnp.int32, (1, LANES), 1)
        sub = lax.broadcasted_iota(jnp.int32, (LANES, 1), 0)
        dX = o["dskip_rep"] * dyv
        yoff_full = jnp.concatenate(o["yoff"], axis=1)
        rs = jnp.zeros((L, LANES), F32)
        cs_t = jnp.zeros((LANES, L), F32)
        dBs, dCs = [], []
        rh_sums = []
        ddec = []
        for g in range(N_GROUPS):
            gs = slice(g * GROUP_W, (g + 1) * GROUP_W)
            Bg = Bm[:, g * N_STATE:(g + 1) * N_STATE]
            Cg = Cm[:, g * N_STATE:(g + 1) * N_STATE]
            Gg = o["G"][g]
            R = rstate[g]
            dwm = dyv[:, gs] * o["eo_rep"][:, gs]
            dC = _mm_nt(dwm, hprev[g])
            dH = _mm_tn(Cg, dwm)
            dG = jnp.zeros((L, L), F32)
            for hh in range(hpg):
                h = g * hpg + hh
                hs = slice(h * HEAD_DIM, (h + 1) * HEAD_DIM)
                lm = o["lms"][h]
                m_h = Gg * lm
                dM = _mm_nt(dyv[:, hs], xdt[:, hs])
                dx_scr[:, hs] = _mm_tn(m_h, dyv[:, hs])
                qm = dM * m_h
                rs = rs + jnp.sum(qm, axis=1, keepdims=True) * (lane == h).astype(F32)
                cs_t = cs_t + (sub == h).astype(F32) * jnp.sum(qm, axis=0, keepdims=True)
                dG = dG + dM * lm
            dC = dC + _mm(dG, Bg)
            dB = _mm_tn(dG, Cg)
            zx = _mm(Bg, R)
            dxdt_state = zx * o["dec_rep"][:, gs]
            ddec.append(zx * xdt[:, gs])
            dB = dB + _mm_nt(o["xdec"][g], R)
            rh_sums.append(jnp.sum(R * hprev[g], axis=0, keepdims=True))
            rstate[g] = dH + o["cd_rep"][:, gs] * R
            dx_scr[:, gs] = dx_scr[:, gs] + dxdt_state
            dBs.append(dB)
            dCs.append(dC)
        dxdt = dx_scr[...]
        tail = jnp.concatenate([jnp.sum(dyv * X, axis=0, keepdims=True), jnp.concatenate(rh_sums, axis=1),
                                jnp.zeros((6, D), F32)], axis=0)
        red = _dot01(jnp.concatenate([dyv * yoff_full, jnp.concatenate(ddec, axis=1), dxdt * X, tail], axis=0), rdm, 2)
        d_dskip, dcd_row = red[3 * L:3 * L + 1], red[3 * L + 1:3 * L + 2]
        ddec_h = red[L:2 * L] * o["dec"]
        dcum_last = jnp.sum(ddec_h, axis=0, keepdims=True) + dcd_row * o["cd"]
        dcum = red[0:L] + rs - cs_t.T - ddec_h + (sub == L - 1).astype(F32) * dcum_last
        dda = _dot01(o["anti"], dcum, 3, split_lhs=False)
        ddt_v = dda * o["a_row"] + red[2 * L:3 * L]
        dX = dX + dxdt * o["dt_rep"]
        head_mask = (lane < N_HEADS).astype(F32)
        d_alog = jnp.sum(dda * o["dt"], axis=0, keepdims=True) * o["a_row"] * head_mask
        dpre = ddt_v * _sigmoid(o["pre"]) * head_mask
        dpm_ref[:, OFF_DT:] = dpre.astype(BF16)
        d_dtb = jnp.sum(dpre, axis=0, keepdims=True)
        dhp_ref[...] += jnp.concatenate([d_dtb, d_alog, d_dskip * head_mask, jnp.zeros((5, LANES), F32)], axis=0)

        dxbc = jnp.concatenate([dX] + dBs + dCs, axis=1)
        sg, cv = o["sg"], o["cv"]
        dcv = dxbc * (sg * (1.0 + cv * (1.0 - sg)))
        e2 = jnp.concatenate([dcv, nxt_cv[...]], axis=0)
        n2 = L + CONV_HALO
        ahead = [dcv, pltpu.roll(e2, n2 - 1, 0)[:L], pltpu.roll(e2, n2 - 2, 0)[:L], pltpu.roll(e2, n2 - 3, 0)[:L]]
        dconv_ref[0:5] += jnp.concatenate(
            [jnp.sum(ux * ahead[3 - k], axis=0, keepdims=True) for k in range(4)]
            + [jnp.sum(dcv, axis=0, keepdims=True)], axis=0)
        dux = ahead[0] * cw[3:4] + ahead[1] * cw[2:3] + ahead[2] * cw[1:2] + ahead[3] * cw[0:1]
        dpm_ref[:, OFF_XBC:OFF_DT] = dux.astype(BF16)
        nxt_cv[...] = dcv[:CONV_HALO]

    def full(shape):
        return pl.BlockSpec(shape, lambda b, c: (0,) * len(shape))

    def rowblk(b, c):
        return b * nc + (nc - 1 - c)

    hp_blocks = CHUNK // POOL_HALO
    T = nb * seq
    return pl.pallas_call(
        body, name="mixer_bwd", grid=(nb, nc),
        in_specs=[pl.BlockSpec((CHUNK, PROJ_W), lambda b, c: (rowblk(b, c), 0)),
                  pl.BlockSpec((POOL_HALO, POOL_W), lambda b, c: (jnp.maximum(rowblk(b, c) * hp_blocks - 1, 0), 0)),
                  pl.BlockSpec((CHUNK, CONV_CH), lambda b, c: (rowblk(b, c), 0)),
                  pl.BlockSpec((CHUNK, MIX_W), lambda b, c: (rowblk(b, c), 0)),
                  pl.BlockSpec((1, 1, N_GROUPS, N_STATE, GROUP_W), lambda b, c: (b, nc - 1 - c, 0, 0, 0)),
                  full((4, CONV_CH)), full((1, CONV_CH)), full((8, LANES)), full((1, D)),
                  full((4, LANES, LANES)), full((1, POOL_W))] + [ANY_SPEC] * na,
        out_specs=[pl.BlockSpec((CHUNK, PROJ_W), lambda b, c: (rowblk(b, c), 0)),
                   full((8, CONV_CH)), full((8, LANES)), full((8, D)), full((4, LANES, LANES))],
        out_shape=[jax.ShapeDtypeStruct((T, PROJ_W), BF16),
                   jax.ShapeDtypeStruct((8, CONV_CH), F32), jax.ShapeDtypeStruct((8, LANES), F32),
                   jax.ShapeDtypeStruct((8, D), F32), jax.ShapeDtypeStruct((4, LANES, LANES), F32)],
        scratch_shapes=[pltpu.VMEM((POOL_HALO, POOL_W), F32), pltpu.VMEM((CONV_HALO, CONV_CH), F32),
                        pltpu.VMEM((N_GROUPS, N_STATE, GROUP_W), F32), pltpu.VMEM((CHUNK, D), F32),
                        pltpu.VMEM((CHUNK, D), F32)],
        compiler_params=_cparams(),
    )(pm, pm, cvs, dym, hstates, cw, cb, hp, gssd, wpool, pscale, *after)


def _mlp_fused(x2, ymix, target, mod, g_mlp, g_final, w_out, w_up, w_down, seq):
    T = x2.shape[0]
    tm = min(256, seq)
    tps = seq // tm
    nblk = D_FF // FF_BLK

    def body(x_ref, ym_ref, tg_ref, mod_ref, gm_ref, gf_ref, wo_ref, wu_ref, wd_ref,
             da_ref, dym_ref, dh1_ref, u2_ref, f_ref, dup_ref, ddn_ref, dmod_ref, acc_ref, relu_scr):
        i = pl.program_id(0)

        @pl.when(i == 0)
        def _():
            acc_ref[...] = jnp.zeros_like(acc_ref)

        @pl.when(i % tps == 0)
        def _():
            dmod_ref[...] = jnp.zeros_like(dmod_ref)

        md = mod_ref[0]
        gate_m, shift_f, scale_f, gate_f = md[2:3], md[3:4], md[4:5], md[5:6]
        g_mlp, g_fin = gm_ref[...], gf_ref[...]
        a = jnp.dot(ym_ref[...], wo_ref[...], preferred_element_type=F32)
        h1 = x_ref[...] + gate_m * a
        r2 = lax.rsqrt(jnp.mean(h1 * h1, axis=-1, keepdims=True) + EPS)
        n2 = h1 * r2
        u2 = (n2 * g_mlp) * (1.0 + scale_f) + shift_f
        u2b = u2.astype(BF16)
        u2_ref[...] = u2b
        dn = jnp.zeros((tm, D), F32)
        for j in range(nblk):
            js = slice(j * FF_BLK, (j + 1) * FF_BLK)
            upj = jnp.maximum(jnp.dot(u2b, wu_ref[j], preferred_element_type=F32), 0.0)
            relu_scr[:, js] = upj
            fj = (upj * upj).astype(BF16)
            f_ref[:, js] = fj
            dn = dn + jnp.dot(fj, wd_ref[j], preferred_element_type=F32)
        h2 = h1 + gate_f * dn
        r3 = lax.rsqrt(jnp.mean(h2 * h2, axis=-1, keepdims=True) + EPS)
        n3 = h2 * r3
        err = n3 * g_fin - tg_ref[...]
        loss = 0.5 * jnp.sum(jnp.mean(err * err, axis=-1, keepdims=True), axis=0, keepdims=True)
        dout = err * (1.0 / D)
        d_gfin = jnp.sum(dout * n3, axis=0, keepdims=True)
        dn3 = dout * g_fin
        dh2 = r3 * (dn3 - n3 * jnp.mean(dn3 * n3, axis=-1, keepdims=True))
        d_gate_f = jnp.sum(dh2 * dn, axis=0, keepdims=True)
        ddn = (gate_f * dh2).astype(BF16)
        ddn_ref[...] = ddn
        du2 = jnp.zeros((tm, D), F32)
        for j in range(nblk):
            js = slice(j * FF_BLK, (j + 1) * FF_BLK)
            dfj = lax.dot_general(ddn, wd_ref[j], (((1,), (1,)), ((), ())), preferred_element_type=F32)
            dupj = (dfj * (2.0 * relu_scr[:, js])).astype(BF16)
            dup_ref[:, js] = dupj
            du2 = du2 + lax.dot_general(dupj, wu_ref[j], (((1,), (1,)), ((), ())), preferred_element_type=F32)
        d_scale_f = jnp.sum(du2 * (n2 * g_mlp), axis=0, keepdims=True)
        d_shift_f = jnp.sum(du2, axis=0, keepdims=True)
        d_gmlp = jnp.sum(du2 * (1.0 + scale_f) * n2, axis=0, keepdims=True)
        dn2 = du2 * (g_mlp * (1.0 + scale_f))
        dh1 = dh2 + r2 * (dn2 - n2 * jnp.mean(dn2 * n2, axis=-1, keepdims=True))
        dh1_ref[...] = dh1
        d_gate_m = jnp.sum(dh1 * a, axis=0, keepdims=True)
        da = (gate_m * dh1).astype(BF16)
        da_ref[...] = da
        dym_ref[...] = lax.dot_general(da, wo_ref[...], (((1,), (1,)), ((), ())),
                                       preferred_element_type=F32).astype(BF16)
        dmod_ref[0] += jnp.concatenate([jnp.zeros((2, D), F32), d_gate_m, d_shift_f, d_scale_f, d_gate_f,
                                        jnp.zeros((2, D), F32)], axis=0)
        acc_ref[...] += jnp.concatenate([d_gmlp, d_gfin, loss * jnp.ones((1, D), F32), jnp.zeros((5, D), F32)], axis=0)

    whole = pl.BlockSpec(memory_space=pltpu.VMEM)

    def tok(w):
        return pl.BlockSpec((tm, w), lambda i: (i, 0))

    def vec():
        return pl.BlockSpec((1, D), lambda i: (0, 0))

    nb = T // seq
    return pl.pallas_call(
        body, name="mlp_fused", grid=(T // tm,),
        in_specs=[tok(D), tok(MIX_W), tok(D), pl.BlockSpec((1, 8, D), lambda i: (i // tps, 0, 0)), vec(), vec(),
                  whole, whole, whole],
        out_specs=[tok(D), tok(MIX_W), tok(D), tok(D), tok(D_FF), tok(D_FF), tok(D),
                   pl.BlockSpec((1, 8, D), lambda i: (i // tps, 0, 0)), pl.BlockSpec((8, D), lambda i: (0, 0))],
        out_shape=[jax.ShapeDtypeStruct((T, D), BF16), jax.ShapeDtypeStruct((T, MIX_W), BF16),
                   jax.ShapeDtypeStruct((T, D), F32), jax.ShapeDtypeStruct((T, D), BF16),
                   jax.ShapeDtypeStruct((T, D_FF), BF16), jax.ShapeDtypeStruct((T, D_FF), BF16),
                   jax.ShapeDtypeStruct((T, D), BF16), jax.ShapeDtypeStruct((nb, 8, D), F32),
                   jax.ShapeDtypeStruct((8, D), F32)],
        scratch_shapes=[pltpu.VMEM((tm, D_FF), F32)],
        compiler_params=_cparams(),
    )(x2, ymix, target, mod, g_mlp, g_final, w_out, w_up, w_down)


def _in_bwd(x2, dh1, dpb, mod, g_mix, w_cat, dmod_a, acc_a, seq):
    T = x2.shape[0]
    tm = min(1024, seq)
    tps = seq // tm
    steps = T // tm

    def body(x_ref, dh_ref, dpb_ref, mod_ref, g_ref, w_ref, dma_ref, acca_ref, dx_ref, dmod_ref, acc_ref):
        i = pl.program_id(0)

        @pl.when(i == 0)
        def _():
            acc_ref[...] = acca_ref[...]

        @pl.when(i % tps == 0)
        def _():
            dmod_ref[...] = dma_ref[...]

        du = lax.dot_general(dpb_ref[...], w_ref[...], (((1,), (1,)), ((), ())), preferred_element_type=F32)
        x = x_ref[...]
        md = mod_ref[0]
        g = g_ref[...]
        r = lax.rsqrt(jnp.mean(x * x, axis=-1, keepdims=True) + EPS)
        n1 = x * r
        d_scale = jnp.sum(du * (n1 * g), axis=0, keepdims=True)
        d_shift = jnp.sum(du, axis=0, keepdims=True)
        d_g = jnp.sum(du * (1.0 + md[1:2]) * n1, axis=0, keepdims=True)
        dn1 = du * (g * (1.0 + md[1:2]))
        dx_ref[...] = dh_ref[...] + r * (dn1 - n1 * jnp.mean(dn1 * n1, axis=-1, keepdims=True))
        dmod_ref[0] += jnp.concatenate([d_shift, d_scale, jnp.zeros((6, D), F32)], axis=0)
        acc_ref[...] += jnp.concatenate([jnp.zeros((3, D), F32), d_g, jnp.zeros((4, D), F32)], axis=0)

    whole = pl.BlockSpec(memory_space=pltpu.VMEM)
    nb = T // seq
    return pl.pallas_call(
        body, name="in_bwd", grid=(steps,),
        in_specs=[pl.BlockSpec((tm, D), lambda i: (i, 0)), pl.BlockSpec((tm, D), lambda i: (i, 0)),
                  pl.BlockSpec((tm, PROJ_W), lambda i: (i, 0)),
                  pl.BlockSpec((1, 8, D), lambda i: (i // tps, 0, 0)), pl.BlockSpec((1, D), lambda i: (0, 0)),
                  whole, pl.BlockSpec((1, 8, D), lambda i: (i // tps, 0, 0)), pl.BlockSpec((8, D), lambda i: (0, 0))],
        out_specs=[pl.BlockSpec((tm, D), lambda i: (i, 0)),
                   pl.BlockSpec((1, 8, D), lambda i: (i // tps, 0, 0)), pl.BlockSpec((8, D), lambda i: (0, 0))],
        out_shape=[jax.ShapeDtypeStruct((T, D), F32),
                   jax.ShapeDtypeStruct((nb, 8, D), F32), jax.ShapeDtypeStruct((8, D), F32)],
        compiler_params=_cparams(),
    )(x2, dh1, dpb, mod, g_mix, w_cat, dmod_a, acc_a)


def _dw_in(u_b, dpb, in_cols, shards):
    T = u_b.shape[0]
    bk = min(1024, T)
    nk = T // bk
    ns = len(shards)

    def body(*refs):
        u_ref, d_ref = refs[:2]
        sh_refs = refs[2:2 + ns]
        o_ref = refs[2 + ns]
        ga_refs = refs[3 + ns:3 + 2 * ns]
        acc, send, recv, loc = refs[3 + 2 * ns:]
        k = pl.program_id(0)
        gather = _Gather(sh_refs, ga_refs, send, recv, loc)
        gather.begin_hosted(k, nk)

        @pl.when(k == 0)
        def _():
            acc[...] = jnp.zeros_like(acc)

        acc[...] += jnp.dot(u_ref[...].T, d_ref[...], preferred_element_type=F32)

        @pl.when(k == nk - 1)
        def _():
            for j in range(N_DEV):
                o_ref[j] = acc[:, in_cols * j:in_cols * (j + 1)].astype(BF16)

        gather.end_hosted(k, nk)

    return pl.pallas_call(
        body, name="dw_in", grid=(nk,),
        in_specs=[pl.BlockSpec((bk, D), lambda k: (k, 0)), pl.BlockSpec((bk, PROJ_W), lambda k: (k, 0))]
        + [ANY_SPEC] * ns,
        out_specs=[pl.BlockSpec((N_DEV, D, in_cols), lambda k: (0, 0, 0))] + [ANY_SPEC] * ns,
        out_shape=[jax.ShapeDtypeStruct((N_DEV, D, in_cols), BF16)]
        + [jax.ShapeDtypeStruct((N_DEV,) + v.shape, v.dtype) for v in shards],
        scratch_shapes=[pltpu.VMEM((D, PROJ_W), F32)] + _gather_scratch(shards),
        compiler_params=_cparams(),
    )(u_b, dpb, *shards)


def _dw_blocks(a, b, name, by_rows, per_step=1, after=()):
    T, M = a.shape
    N = b.shape[1]
    bk = min(4096, T)
    nk = T // bk
    whole = pl.BlockSpec(memory_space=pltpu.VMEM)
    if by_rows:
        rows = M // N_DEV
        am = rows * per_step
        nblk = N_DEV // per_step
        a_spec, b_spec = pl.BlockSpec((bk, am), lambda i, k: (k, i)), whole
        out_blk, acc_shape = (per_step, rows, N), (am, N)
    else:
        cols = N // N_DEV
        nblk = N_DEV
        a_spec, b_spec = whole, pl.BlockSpec((bk, cols), lambda i, k: (k, i))
        out_blk, acc_shape = (1, M, cols), (M, cols)

    def body(a_ref, b_ref, *rest):
        o_ref, acc = rest[len(after):]
        k = pl.program_id(1)

        @pl.when(k == 0)
        def _():
            acc[...] = jnp.zeros_like(acc)

        tok = pl.ds(pl.multiple_of(k * bk, bk), bk)
        a_blk = a_ref[...] if by_rows else a_ref[tok, :]
        b_blk = b_ref[tok, :] if by_rows else b_ref[...]
        acc[...] += lax.dot_general(a_blk, b_blk, (((0,), (0,)), ((), ())), preferred_element_type=F32)

        @pl.when(k == nk - 1)
        def _():
            o_ref[...] = acc[...].reshape(out_blk).astype(BF16)

    return pl.pallas_call(
        body, name=name, grid=(nblk, nk), in_specs=[a_spec, b_spec] + [ANY_SPEC] * len(after),
        out_specs=pl.BlockSpec(out_blk, lambda i, k: (i, 0, 0)),
        out_shape=jax.ShapeDtypeStruct((N_DEV,) + out_blk[1:], BF16),
        scratch_shapes=[pltpu.VMEM(acc_shape, F32)],
        compiler_params=_cparams(),
    )(a, b, *after)


def _adam_parts(parts, w, m, v, name, me=None):
    rows, cols = w.shape
    br = rows
    for cand in range(rows, 15, -16):
        if rows % cand == 0 and cand * cols * 4 <= ADAM_BLOCK_BYTES:
            br = cand
            break

    if me is not None:
        sent, landed = parts

        def body_own(me_ref, own_ref, p_ref, w_ref, m_ref, v_ref, g_out, dl_out, m_out, v_out):
            g = None
            for k in range(N_DEV):
                blk = jnp.where(me_ref[0] == k, own_ref[0], p_ref[k]).astype(F32)
                g = blk if g is None else g + blk
            g_out[...] = g
            dl, mn, vn = _adam_math(w_ref[...], g, m_ref[...], v_ref[...])
            dl_out[...] = dl
            m_out[...] = mn
            v_out[...] = vn

        wspec = pl.BlockSpec((br, cols), lambda i, me_ref: (i, 0))
        grid_spec = pltpu.PrefetchScalarGridSpec(
            num_scalar_prefetch=1, grid=(rows // br,),
            in_specs=[pl.BlockSpec((1, br, cols), lambda i, me_ref: (me_ref[0], i, 0)),
                      pl.BlockSpec((N_DEV, br, cols), lambda i, me_ref: (0, i, 0)), wspec, wspec, wspec],
            out_specs=[wspec] * 4)
        return pl.pallas_call(
            body_own, name=name, grid_spec=grid_spec, out_shape=[jax.ShapeDtypeStruct((rows, cols), F32)] * 4,
            compiler_params=_cparams(),
        )(me.reshape(1).astype(jnp.int32), sent, landed, w, m, v)

    def body(p_ref, w_ref, m_ref, v_ref, g_out, dl_out, m_out, v_out):
        g = p_ref[0].astype(F32)
        for k in range(1, N_DEV):
            g = g + p_ref[k].astype(F32)
        g_out[...] = g
        dl, mn, vn = _adam_math(w_ref[...], g, m_ref[...], v_ref[...])
        dl_out[...] = dl
        m_out[...] = mn
        v_out[...] = vn

    wspec = pl.BlockSpec((br, cols), lambda i: (i, 0))
    return pl.pallas_call(
        body, name=name, grid=(rows // br,),
        in_specs=[pl.BlockSpec((N_DEV, br, cols), lambda i: (0, i, 0)), wspec, wspec, wspec],
        out_specs=[wspec] * 4, out_shape=[jax.ShapeDtypeStruct((rows, cols), F32)] * 4,
        compiler_params=_cparams(),
    )(parts, w, m, v)


def _adam_plain(g, w, m, v, name):
    def body(g_ref, w_ref, m_ref, v_ref, dl_out, m_out, v_out):
        dl, mn, vn = _adam_math(w_ref[...], g_ref[...], m_ref[...], v_ref[...])
        dl_out[...] = dl
        m_out[...] = mn
        v_out[...] = vn

    return pl.pallas_call(body, name=name, out_shape=[jax.ShapeDtypeStruct(w.shape, F32)] * 3,
                          compiler_params=_cparams())(g, w, m, v)


SMALL_PARAMS = ("b_ada", "g_mix", "conv_b", "dt_bias", "a_log", "d_skip", "g_ssd", "pool_scale", "g_mlp", "g_final")


def _small_adam(gathered, params):
    n_par = len(SMALL_PARAMS)
    nb = gathered[0].shape[1]

    def body(*refs):
        dmod_ref, acc_ref, conv_ref, vec_ref, hd_ref = refs[:5]
        par_refs = refs[5:5 + 3 * n_par]
        out_refs = refs[5 + 3 * n_par:5 + 7 * n_par]
        cw_out, acc_out = refs[5 + 7 * n_par:]

        def total(ref):
            t = ref[0]
            for k in range(1, N_DEV):
                t = t + ref[k]
            return t

        dm = total(dmod_ref)
        dmb = dm[0]
        for b in range(1, nb):
            dmb = dmb + dm[b]
        ac, cv, vc, hd = total(acc_ref), total(conv_ref), total(vec_ref), total(hd_ref)
        cw_out[...] = cv[0:4]
        acc_out[...] = ac
        grads = {
            "b_ada": jnp.concatenate([dmb[r:r + 1] for r in range(6)], axis=1), "g_mix": ac[3:4], "conv_b": cv[4:5],
            "dt_bias": hd[0:1, 0:N_HEADS], "a_log": hd[1:2, 0:N_HEADS], "d_skip": hd[2:3, 0:N_HEADS],
            "g_ssd": vc[0:1], "pool_scale": vc[1:2, 0:POOL_W], "g_mlp": ac[0:1], "g_final": ac[1:2],
        }
        for i, name in enumerate(SMALL_PARAMS):
            w_ref, m_ref, v_ref = par_refs[3 * i:3 * i + 3]
            g = grads[name]
            dl, mn, vn = _adam_math(w_ref[...], g, m_ref[...], v_ref[...])
            g_o, d_o, m_o, v_o = out_refs[4 * i:4 * i + 4]
            g_o[...] = g
            d_o[...] = dl
            m_o[...] = mn
            v_o[...] = vn

    flat = [a for name in SMALL_PARAMS for a in params[name]]
    out_shape = [jax.ShapeDtypeStruct(params[name][0].shape, F32) for name in SMALL_PARAMS for _ in range(4)]
    out_shape += [jax.ShapeDtypeStruct((4, CONV_CH), F32), jax.ShapeDtypeStruct((8, D), F32)]
    return pl.pallas_call(body, name="small_adam", out_shape=out_shape, compiler_params=_cparams())(*gathered, *flat)


def kernel(x, c, w_ada, b_ada, g_mix, w_in, conv_w, conv_b, dt_bias, a_log, d_skip, g_ssd, w_pool, pool_scale, w_out, g_mlp, w_up, w_down, g_final, loss_target, m_w_ada, m_b_ada, m_g_mix, m_w_in, m_conv_w, m_conv_b, m_dt_bias, m_a_log, m_d_skip, m_g_ssd, m_w_pool, m_pool_scale, m_w_out, m_g_mlp, m_w_up, m_w_down, m_g_final, v_w_ada, v_b_ada, v_g_mix, v_w_in, v_conv_w, v_conv_b, v_dt_bias, v_a_log, v_d_skip, v_g_ssd, v_w_pool, v_pool_scale, v_w_out, v_g_mlp, v_w_up, v_w_down, v_g_final):
    nb, seq, _ = x.shape
    T = nb * seq
    me = 4 * lax.axis_index("x") + 2 * lax.axis_index("y") + lax.axis_index("c")
    in_cols = w_in.shape[2]
    ada_cols = w_ada.shape[2]
    cw_cols = conv_w.shape[2]

    win_t = jnp.pad(w_in[0].astype(BF16).T, ((0, -in_cols % 16), (0, 0)))
    c_g, cw_g, win_g = _all_gather([c, conv_w[0], win_t], "ag_first")
    c_all = c_g.reshape(N_DEV * nb, D)
    cw_full = cw_g.transpose(1, 0, 2).reshape(4, CONV_CH)

    b_slice = lax.dynamic_slice(b_ada, (0, me * ada_cols), (1, ada_cols))
    mod_cols = _ada_fwd(c_all, w_ada[0], b_slice)
    (mod_g,) = _all_gather([mod_cols], "ag_mod")
    mod_all = mod_g.transpose(1, 0, 2).reshape(N_DEV * nb, 6, D)
    mod_mine = lax.dynamic_slice(mod_all, (me * nb, 0, 0), (nb, 6, D))
    mod = jnp.pad(mod_mine, ((0, 0), (0, 2), (0, 0)))

    x2 = x.reshape(T, D)
    tg2 = loss_target.reshape(T, D)
    heads = jnp.pad(jnp.concatenate([dt_bias, a_log, d_skip], axis=0), ((0, 5), (0, LANES - N_HEADS)))
    wpool_b = w_pool[0]
    u_b, pm, w_cat, wup_g = _mix_in(x2, mod, g_mix, win_g, in_cols, seq, [w_up[0].astype(BF16)])
    ymix, hstates, cvs, wout_g, wdn_g = _mixer_fwd(
        pm, cw_full, conv_b, heads, g_ssd, wpool_b, pool_scale, nb, seq,
        [w_out[0].astype(BF16), w_down[0].astype(BF16)])
    da_b, dym, dh1, u2_b, f_b, dup_b, ddn_b, dmod_a, acc_a = _mlp_fused(
        x2, ymix, tg2, mod, g_mlp, g_final.reshape(1, D), wout_g.reshape(MIX_W, D), wup_g, wdn_g, seq)

    gout_p = _dw_blocks(ymix, da_b, "dw_out", True, per_step=8)
    gup_p = _dw_blocks(u2_b, dup_b, "dw_up", False)
    ex_a = _exchange_start([gout_p, gup_p], "ga_start")
    gdn_p = _dw_blocks(f_b, ddn_b, "dw_down", True, after=[ex_a[4]])
    ex_b = _exchange_start([gdn_p], "gb_start")
    dpb, d_conv, d_heads, d_vec, d_wpool = _mixer_bwd(
        pm, cvs, dym, hstates, cw_full, conv_b, heads, g_ssd, wpool_b, pool_scale, nb, seq, after=[ex_b[4]])
    gin_p, conv_g, vec_g, heads_g, wpool_parts = _dw_in(
        u_b, dpb, in_cols, [d_conv, d_vec, d_heads, d_wpool.reshape(4 * LANES, LANES)])
    ex_in = _exchange_start([gin_p], "gin_start")
    grad_x2, dmod, acc = _in_bwd(x2, dh1, dpb, mod, g_mix + ex_in[4][0:1, 0:1], w_cat, dmod_a, acc_a, seq)

    gout_r, gup_r = _exchange_wait(ex_a, dmod, "ga_wait")
    (gdn_r,) = _exchange_wait(ex_b, dmod, "gb_wait")
    g_out, d_out, nm_out, nv_out = _adam_parts(gout_r, w_out[0], m_w_out[0], v_w_out[0], "adam_w_out", me)
    g_up, d_up, nm_up, nv_up = _adam_parts(gup_r, w_up[0], m_w_up[0], v_w_up[0], "adam_w_up", me)
    g_dn, d_dn, nm_dn, nv_dn = _adam_parts(gdn_r, w_down[0], m_w_down[0], v_w_down[0], "adam_w_down", me)

    dmod_g, acc_g = _all_gather([dmod, acc], "ag_small_bwd", after=[nm_out, nm_up, nm_dn])
    pool2 = (4 * LANES, LANES)
    wpool_outs = _adam_parts(wpool_parts, w_pool.reshape(pool2), m_w_pool.reshape(pool2), v_w_pool.reshape(pool2),
                             "adam_w_pool")
    small_params = {
        "b_ada": (b_ada, m_b_ada, v_b_ada), "g_mix": (g_mix, m_g_mix, v_g_mix), "conv_b": (conv_b, m_conv_b, v_conv_b),
        "dt_bias": (dt_bias, m_dt_bias, v_dt_bias), "a_log": (a_log, m_a_log, v_a_log),
        "d_skip": (d_skip, m_d_skip, v_d_skip), "g_ssd": (g_ssd, m_g_ssd, v_g_ssd),
        "pool_scale": (pool_scale, m_pool_scale, v_pool_scale), "g_mlp": (g_mlp, m_g_mlp, v_g_mlp),
        "g_final": tuple(a.reshape(1, D) for a in (g_final, m_g_final, v_g_final)),
    }
    small_res = _small_adam([dmod_g, acc_g, conv_g, vec_g, heads_g], small_params)
    g_cw_full, acc_sum = small_res[-2:]
    loss = acc_sum[2, 0]

    g_cw = lax.dynamic_slice(g_cw_full, (0, me * cw_cols), (4, cw_cols))
    d_cwp, nm_cwp, nv_cwp = _adam_plain(g_cw, conv_w[0], m_conv_w[0], v_conv_w[0], "adam_conv_w")

    dmod_all = dmod_g[:, :, 0:6].reshape(N_DEV * nb, 6 * D)
    dmod_slice = lax.dynamic_slice(dmod_all, (0, me * ada_cols), (N_DEV * nb, ada_cols))
    g_ada, d_ada, nm_ada, nv_ada = _ada_bwd_adam(c_all, dmod_slice, w_ada[0], m_w_ada[0], v_w_ada[0])

    ex_after = nm_ada[0:8, 0:LANES] + acc_sum[:, 0:LANES]
    (gin_r,) = _exchange_wait(ex_in, ex_after, "gin_wait")
    g_in, d_in, nm_in, nv_in = _adam_parts(gin_r, w_in[0], m_w_in[0], v_w_in[0], "adam_w_in", me)

    def small_outs(kind, wpool):
        res = {name: small_res[4 * i + kind] for i, name in enumerate(SMALL_PARAMS)}
        res["g_final"] = res["g_final"].reshape(D)
        res["w_pool"] = wpool.reshape(1, 4, LANES, LANES)
        return res

    def big_outs(ada, win, cwp, wout, wup, wdn):
        return {"w_ada": ada[None], "w_in": win.reshape(1, D, in_cols), "conv_w": cwp[None], "w_out": wout[None],
                "w_up": wup[None], "w_down": wdn[None]}

    order = ["w_ada", "b_ada", "g_mix", "w_in", "conv_w", "conv_b", "dt_bias", "a_log", "d_skip", "g_ssd", "w_pool",
             "pool_scale", "w_out", "g_mlp", "w_up", "w_down", "g_final"]
    groups = [
        {**small_outs(0, wpool_outs[0]), **big_outs(g_ada, g_in, g_cw, g_out, g_up, g_dn)},
        {**small_outs(1, wpool_outs[1]), **big_outs(d_ada, d_in, d_cwp, d_out, d_up, d_dn)},
        {**small_outs(2, wpool_outs[2]), **big_outs(nm_ada, nm_in, nm_cwp, nm_out, nm_up, nm_dn)},
        {**small_outs(3, wpool_outs[3]), **big_outs(nv_ada, nv_in, nv_cwp, nv_out, nv_up, nv_dn)},
    ]
    outs = [loss, grad_x2.reshape(nb, seq, D)]
    for grp in groups:
        outs += [grp[n] for n in order]
    return tuple(outs)
```
